```python
import math
import jax, jax.numpy as jnp
from jax import lax
import numpy as np

D_MODEL = 2048
BATCH = 8
SEQ = 2048
DEPTH = 2

MEM_LEN = 256
CHUNK = 128
Q_BLOCK = 128
D_A = D_MODEL // 2
A_GROUPS = 8
A_GROUP_DIM = D_A // A_GROUPS
D_B = D_MODEL // 4
B_HEADS = 4
B_HEAD_DIM = D_B // B_HEADS
D_C = D_MODEL // 4
C_HEADS = 4
C_HEAD_DIM = D_C // C_HEADS
SPLIT_SIZES = (D_A, D_A, D_A, D_B, D_B, D_B, D_B, D_C, D_C)
IN_WIDTH = sum(SPLIT_SIZES)
EPS = 1e-6

kernel_name = "hybrid_sgu_stickbreak_memxattn"


def rms_norm(x, g):
    xf = x.astype(jnp.float32)
    y = xf * lax.rsqrt(jnp.mean(xf * xf, axis=-1, keepdims=True) + EPS)
    return (y * g.astype(jnp.float32)).astype(x.dtype)


def layer_norm(x, g, b):
    xf = x.astype(jnp.float32)
    mu = jnp.mean(xf, axis=-1, keepdims=True)
    xc = xf - mu
    y = xc * lax.rsqrt(jnp.mean(xc * xc, axis=-1, keepdims=True) + EPS)
    return (y * g.astype(jnp.float32) + b.astype(jnp.float32)).astype(x.dtype)


def sgu_mixer(u, v, ln_g, ln_b, w_s, b_s):
    bsz, s_len, _ = v.shape
    n_chunks = s_len // CHUNK
    v = layer_norm(v, ln_g, ln_b)
    vc = v.reshape(bsz, n_chunks, CHUNK, A_GROUPS, A_GROUP_DIM)
    mask = jnp.tril(jnp.ones((CHUNK, CHUNK), dtype=bool))
    w = jnp.where(mask[None], w_s, jnp.zeros_like(w_s))
    mixed = jnp.einsum('gts,bcsgd->bctgd', w, vc) + b_s.T[None, None, :, :, None]
    return u * mixed.reshape(bsz, s_len, D_A)


def stick_breaking_attention(q, k, v):
    s_len = q.shape[1]
    scale = 1.0 / math.sqrt(q.shape[-1])
    outs = []
    for i in range(s_len // Q_BLOCK):
        start = i * Q_BLOCK
        kv_len = start + Q_BLOCK
        q_blk = q[:, start:kv_len]
        k_blk = k[:, :kv_len]
        v_blk = v[:, :kv_len]
        z = jnp.einsum('bthd,bshd->bhts', q_blk, k_blk).astype(jnp.float32) * scale
        t_idx = start + jnp.arange(Q_BLOCK)[:, None]
        s_idx = jnp.arange(kv_len)[None, :]
        causal = s_idx < t_idx
        log_beta = jax.nn.log_sigmoid(z)
        log_1mb = jnp.where(causal, jax.nn.log_sigmoid(-z), 0.0)
        rc = lax.cumsum(log_1mb, axis=3, reverse=True)
        after = jnp.pad(rc[..., 1:], ((0, 0), (0, 0), (0, 0), (0, 1)))
        a = jnp.where(causal, jnp.exp(log_beta + after), 0.0)
        outs.append(jnp.einsum('bhts,bshd->bthd', a.astype(v.dtype), v_blk))
    return jnp.concatenate(outs, axis=1)


def memory_attention(q, mem_k, mem_v, q_g, k_g):
    scale = 1.0 / math.sqrt(q.shape[-1])
    qn = rms_norm(q, q_g)
    kn = rms_norm(mem_k, k_g)
    s = jnp.einsum('bthd,bmhd->bhtm', qn, kn).astype(jnp.float32) * scale
    p = jax.nn.softmax(s, axis=-1)
    return jnp.einsum('bhtm,bmhd->bthd', p.astype(mem_v.dtype), mem_v)


def _fwd_setup_inputs(seed: int = 0) -> dict:
    key = jax.random.key(seed)
    ks = jax.random.split(key, 16)
    f32 = jnp.float32
    x = jax.random.normal(ks[0], (BATCH, SEQ, D_MODEL), f32)
    mem = jax.random.normal(ks[1], (BATCH, MEM_LEN, D_MODEL), f32)
    norm_g = 1.0 + 0.01 * jax.random.normal(ks[2], (DEPTH, D_MODEL), f32)
    w_in = jax.random.normal(ks[3], (DEPTH, D_MODEL, IN_WIDTH), f32) * D_MODEL ** -0.5
    sgu_ln_g = 1.0 + 0.01 * jax.random.normal(ks[4], (DEPTH, D_A), f32)
    sgu_ln_b = 0.01 * jax.random.normal(ks[5], (DEPTH, D_A), f32)
    sgu_w = jax.random.normal(ks[6], (DEPTH, A_GROUPS, CHUNK, CHUNK), f32) * CHUNK ** -0.5
    sgu_b = 1.0 + 0.01 * jax.random.normal(ks[7], (DEPTH, A_GROUPS, CHUNK), f32)
    mem_norm_g = 1.0 + 0.01 * jax.random.normal(ks[8], (DEPTH, D_MODEL), f32)
    w_mem_kv = jax.random.normal(ks[9], (DEPTH, D_MODEL, 2 * D_C), f32) * D_MODEL ** -0.5
    q_norm_g = 1.0 + 0.01 * jax.random.normal(ks[10], (DEPTH, C_HEAD_DIM), f32)
    k_norm_g = 1.0 + 0.01 * jax.random.normal(ks[11], (DEPTH, C_HEAD_DIM), f32)
    w_out = jax.random.normal(ks[12], (DEPTH, D_MODEL, D_MODEL), f32) * D_MODEL ** -0.5
    return {"x": x, "mem": mem, "norm_g": norm_g, "w_in": w_in,
            "sgu_ln_g": sgu_ln_g, "sgu_ln_b": sgu_ln_b, "sgu_w": sgu_w, "sgu_b": sgu_b,
            "mem_norm_g": mem_norm_g, "w_mem_kv": w_mem_kv,
            "q_norm_g": q_norm_g, "k_norm_g": k_norm_g, "w_out": w_out}


def _fwd_reference(x, mem, norm_g, w_in, sgu_ln_g, sgu_ln_b, sgu_w, sgu_b,
              mem_norm_g, w_mem_kv, q_norm_g, k_norm_g, w_out):
    bsz, s_len, _ = x.shape
    split_idx = list(np.cumsum(SPLIT_SIZES)[:-1])
    for l in range(DEPTH):
        h = rms_norm(x, norm_g[l])
        proj = jnp.matmul(h, w_in[l])
        u_a, v_a, z_a, q_b, k_b, v_b, z_b, q_c, z_c = jnp.split(proj, split_idx, axis=-1)

        u_a = jax.nn.gelu(u_a, approximate=False)
        v_a = jax.nn.gelu(v_a, approximate=False)
        y_a = sgu_mixer(u_a, v_a, sgu_ln_g[l], sgu_ln_b[l], sgu_w[l], sgu_b[l]) * jax.nn.silu(z_a)

        hb = (bsz, s_len, B_HEADS, B_HEAD_DIM)
        y_b = stick_breaking_attention(q_b.reshape(hb), k_b.reshape(hb), v_b.reshape(hb))
        y_b = y_b.reshape(bsz, s_len, D_B) * jax.nn.silu(z_b)

        mem_h = rms_norm(mem, mem_norm_g[l])
        mem_kv = jnp.matmul(mem_h, w_mem_kv[l])
        mem_k, mem_v = jnp.split(mem_kv, 2, axis=-1)
        hm = (bsz, mem.shape[1], C_HEADS, C_HEAD_DIM)
        y_c = memory_attention(q_c.reshape(bsz, s_len, C_HEADS, C_HEAD_DIM),
                               mem_k.reshape(hm), mem_v.reshape(hm), q_norm_g[l], k_norm_g[l])
        y_c = y_c.reshape(bsz, s_len, D_C) * jax.nn.silu(z_c)

        y = jnp.concatenate([y_a, y_b, y_c], axis=-1)
        x = x + jnp.matmul(y, w_out[l])
    return x


import jax as _jax
import jax.numpy as _jnp

TWIN_FORMAT = 'train_step'
FWD_PARAMS = ['x', 'mem', 'norm_g', 'w_in', 'sgu_ln_g', 'sgu_ln_b', 'sgu_w', 'sgu_b', 'mem_norm_g', 'w_mem_kv', 'q_norm_g', 'k_norm_g', 'w_out']
TWIN_WEIGHTS = ['norm_g', 'w_in', 'sgu_ln_g', 'sgu_ln_b', 'sgu_w', 'sgu_b', 'mem_norm_g', 'w_mem_kv', 'q_norm_g', 'k_norm_g', 'w_out']
TWIN_DIFF_INPUT = 'x'
TWIN_INPUTS = ['x', 'mem', 'norm_g', 'w_in', 'sgu_ln_g', 'sgu_ln_b', 'sgu_w', 'sgu_b', 'mem_norm_g', 'w_mem_kv', 'q_norm_g', 'k_norm_g', 'w_out', 'loss_target', 'm_norm_g', 'm_w_in', 'm_sgu_ln_g', 'm_sgu_ln_b', 'm_sgu_w', 'm_sgu_b', 'm_mem_norm_g', 'm_w_mem_kv', 'm_q_norm_g', 'm_k_norm_g', 'm_w_out', 'v_norm_g', 'v_w_in', 'v_sgu_ln_g', 'v_sgu_ln_b', 'v_sgu_w', 'v_sgu_b', 'v_mem_norm_g', 'v_w_mem_kv', 'v_q_norm_g', 'v_k_norm_g', 'v_w_out']
TWIN_OUTPUTS = ['loss', 'grad_x', 'grad_norm_g', 'grad_w_in', 'grad_sgu_ln_g', 'grad_sgu_ln_b', 'grad_sgu_w', 'grad_sgu_b', 'grad_mem_norm_g', 'grad_w_mem_kv', 'grad_q_norm_g', 'grad_k_norm_g', 'grad_w_out', 'delta_norm_g', 'delta_w_in', 'delta_sgu_ln_g', 'delta_sgu_ln_b', 'delta_sgu_w', 'delta_sgu_b', 'delta_mem_norm_g', 'delta_w_mem_kv', 'delta_q_norm_g', 'delta_k_norm_g', 'delta_w_out', 'new_m_norm_g', 'new_m_w_in', 'new_m_sgu_ln_g', 'new_m_sgu_ln_b', 'new_m_sgu_w', 'new_m_sgu_b', 'new_m_mem_norm_g', 'new_m_w_mem_kv', 'new_m_q_norm_g', 'new_m_k_norm_g', 'new_m_w_out', 'new_v_norm_g', 'new_v_w_in', 'new_v_sgu_ln_g', 'new_v_sgu_ln_b', 'new_v_sgu_w', 'new_v_sgu_b', 'new_v_mem_norm_g', 'new_v_w_mem_kv', 'new_v_q_norm_g', 'new_v_k_norm_g', 'new_v_w_out']
TWIN_LEAF_KINDS = {'loss': 'loss', 'grad_x': 'grad_x', 'grad_norm_g': 'grad_w', 'grad_w_in': 'grad_w', 'grad_sgu_ln_g': 'grad_w', 'grad_sgu_ln_b': 'grad_w', 'grad_sgu_w': 'grad_w', 'grad_sgu_b': 'grad_w', 'grad_mem_norm_g': 'grad_w', 'grad_w_mem_kv': 'grad_w', 'grad_q_norm_g': 'grad_w', 'grad_k_norm_g': 'grad_w', 'grad_w_out': 'grad_w', 'delta_norm_g': 'delta_w', 'delta_w_in': 'delta_w', 'delta_sgu_ln_g': 'delta_w', 'delta_sgu_ln_b': 'delta_w', 'delta_sgu_w': 'delta_w', 'delta_sgu_b': 'delta_w', 'delta_mem_norm_g': 'delta_w', 'delta_w_mem_kv': 'delta_w', 'delta_q_norm_g': 'delta_w', 'delta_k_norm_g': 'delta_w', 'delta_w_out': 'delta_w', 'new_m_norm_g': 'new_m', 'new_m_w_in': 'new_m', 'new_m_sgu_ln_g': 'new_m', 'new_m_sgu_ln_b': 'new_m', 'new_m_sgu_w': 'new_m', 'new_m_sgu_b': 'new_m', 'new_m_mem_norm_g': 'new_m', 'new_m_w_mem_kv': 'new_m', 'new_m_q_norm_g': 'new_m', 'new_m_k_norm_g': 'new_m', 'new_m_w_out': 'new_m', 'new_v_norm_g': 'new_v', 'new_v_w_in': 'new_v', 'new_v_sgu_ln_g': 'new_v', 'new_v_sgu_ln_b': 'new_v', 'new_v_sgu_w': 'new_v', 'new_v_sgu_b': 'new_v', 'new_v_mem_norm_g': 'new_v', 'new_v_w_mem_kv': 'new_v', 'new_v_q_norm_g': 'new_v', 'new_v_k_norm_g': 'new_v', 'new_v_w_out': 'new_v'}


def _forward(args):
    return _fwd_reference(*[args[k] for k in FWD_PARAMS])


def _output_shape():
    out = _jax.eval_shape(lambda: _forward(_fwd_setup_inputs(0)))
    return out.shape, out.dtype

N_MICROBATCH = 1
ADAM_LR = 0.001
ADAM_B1 = 0.9
ADAM_B2 = 0.999
ADAM_EPS = 1e-08
ADAM_WD = 0.01
ADAM_STEP = 10
PER_EXAMPLE_BATCH_AXIS = {'x': 0, 'mem': 0, 'loss_target': 0}
SHARED_INPUTS = []
_WEIGHT_DTYPES = {'norm_g': _jnp.float32, 'w_in': _jnp.float32, 'sgu_ln_g': _jnp.float32, 'sgu_ln_b': _jnp.float32, 'sgu_w': _jnp.float32, 'sgu_b': _jnp.float32, 'mem_norm_g': _jnp.float32, 'w_mem_kv': _jnp.float32, 'q_norm_g': _jnp.float32, 'k_norm_g': _jnp.float32, 'w_out': _jnp.float32}
MOMENT_SCALE = {'norm_g': 2.825425e+00, 'w_in': 8.549429e-02, 'sgu_ln_g': 5.709195e-01, 'sgu_ln_b': 8.982743e-02, 'sgu_w': 8.869736e-02, 'sgu_b': 1.184374e+00, 'mem_norm_g': 8.898759e-03, 'w_mem_kv': 9.124801e-03, 'q_norm_g': 1.069980e-01, 'k_norm_g': 1.068689e-01, 'w_out': 6.592033e-02}


def _to_microbatches(a, axis):
    t = _jnp.moveaxis(a, axis, 0)
    t = t.reshape((N_MICROBATCH, t.shape[0] // N_MICROBATCH) + t.shape[1:])
    return _jnp.moveaxis(t, 1, axis + 1)


def setup_inputs(seed: int = 0) -> dict:
    inp = _fwd_setup_inputs(seed)
    key = _jax.random.fold_in(_jax.random.key(seed), 7919)
    shape, _ = _output_shape()
    out = dict(inp)
    out["loss_target"] = _jax.random.normal(_jax.random.fold_in(key, 0), shape, _jnp.float32)
    for i, name in enumerate(TWIN_WEIGHTS):
        w = inp[name].astype(_jnp.float32)
        if MOMENT_SCALE is None:
            s = _jnp.sqrt(_jnp.mean(_jnp.square(w)) + 1e-30)
        else:
            s = MOMENT_SCALE[name]
        km, kv = _jax.random.split(_jax.random.fold_in(key, i + 1))
        out[name] = w
        out["m_" + name] = s * _jax.random.normal(km, w.shape, _jnp.float32)
        out["v_" + name] = (s * s) * _jax.random.uniform(kv, w.shape, _jnp.float32, 0.5, 1.5)
    if N_MICROBATCH > 1:
        for name, axis in PER_EXAMPLE_BATCH_AXIS.items():
            out[name] = _to_microbatches(out[name], axis)
    return {'x': out['x'], 'mem': out['mem'], 'norm_g': out['norm_g'], 'w_in': out['w_in'], 'sgu_ln_g': out['sgu_ln_g'], 'sgu_ln_b': out['sgu_ln_b'], 'sgu_w': out['sgu_w'], 'sgu_b': out['sgu_b'], 'mem_norm_g': out['mem_norm_g'], 'w_mem_kv': out['w_mem_kv'], 'q_norm_g': out['q_norm_g'], 'k_norm_g': out['k_norm_g'], 'w_out': out['w_out'], 'loss_target': out['loss_target'], 'm_norm_g': out['m_norm_g'], 'm_w_in': out['m_w_in'], 'm_sgu_ln_g': out['m_sgu_ln_g'], 'm_sgu_ln_b': out['m_sgu_ln_b'], 'm_sgu_w': out['m_sgu_w'], 'm_sgu_b': out['m_sgu_b'], 'm_mem_norm_g': out['m_mem_norm_g'], 'm_w_mem_kv': out['m_w_mem_kv'], 'm_q_norm_g': out['m_q_norm_g'], 'm_k_norm_g': out['m_k_norm_g'], 'm_w_out': out['m_w_out'], 'v_norm_g': out['v_norm_g'], 'v_w_in': out['v_w_in'], 'v_sgu_ln_g': out['v_sgu_ln_g'], 'v_sgu_ln_b': out['v_sgu_ln_b'], 'v_sgu_w': out['v_sgu_w'], 'v_sgu_b': out['v_sgu_b'], 'v_mem_norm_g': out['v_mem_norm_g'], 'v_w_mem_kv': out['v_w_mem_kv'], 'v_q_norm_g': out['v_q_norm_g'], 'v_k_norm_g': out['v_k_norm_g'], 'v_w_out': out['v_w_out']}


def _loss(weights, diff, rest, loss_target):
    with _jax.named_scope("forward"):
        args = {**rest, TWIN_DIFF_INPUT: diff, **{k: w.astype(_WEIGHT_DTYPES[k]) for k, w in weights.items()}}
        y = _forward(args)
    with _jax.named_scope("loss_head"):
        err = _jnp.square(y.astype(_jnp.float32) - loss_target)
        return 0.5 * _jnp.sum(_jnp.mean(err, axis=-1)) if err.ndim else 0.5 * err


def _adamw(w, g, m, v):
    m = ADAM_B1 * m + (1.0 - ADAM_B1) * g
    v = ADAM_B2 * v + (1.0 - ADAM_B2) * _jnp.square(g)
    m_hat = m / (1.0 - ADAM_B1 ** ADAM_STEP)
    v_hat = v / (1.0 - ADAM_B2 ** ADAM_STEP)
    delta = -ADAM_LR * (m_hat / (_jnp.sqrt(v_hat) + ADAM_EPS) + ADAM_WD * w)
    return delta, m, v


def reference(x, mem, norm_g, w_in, sgu_ln_g, sgu_ln_b, sgu_w, sgu_b, mem_norm_g, w_mem_kv, q_norm_g, k_norm_g, w_out, loss_target, m_norm_g, m_w_in, m_sgu_ln_g, m_sgu_ln_b, m_sgu_w, m_sgu_b, m_mem_norm_g, m_w_mem_kv, m_q_norm_g, m_k_norm_g, m_w_out, v_norm_g, v_w_in, v_sgu_ln_g, v_sgu_ln_b, v_sgu_w, v_sgu_b, v_mem_norm_g, v_w_mem_kv, v_q_norm_g, v_k_norm_g, v_w_out):
    given = dict(x=x, mem=mem, norm_g=norm_g, w_in=w_in, sgu_ln_g=sgu_ln_g, sgu_ln_b=sgu_ln_b, sgu_w=sgu_w, sgu_b=sgu_b, mem_norm_g=mem_norm_g, w_mem_kv=w_mem_kv, q_norm_g=q_norm_g, k_norm_g=k_norm_g, w_out=w_out, loss_target=loss_target, m_norm_g=m_norm_g, m_w_in=m_w_in, m_sgu_ln_g=m_sgu_ln_g, m_sgu_ln_b=m_sgu_ln_b, m_sgu_w=m_sgu_w, m_sgu_b=m_sgu_b, m_mem_norm_g=m_mem_norm_g, m_w_mem_kv=m_w_mem_kv, m_q_norm_g=m_q_norm_g, m_k_norm_g=m_k_norm_g, m_w_out=m_w_out, v_norm_g=v_norm_g, v_w_in=v_w_in, v_sgu_ln_g=v_sgu_ln_g, v_sgu_ln_b=v_sgu_ln_b, v_sgu_w=v_sgu_w, v_sgu_b=v_sgu_b, v_mem_norm_g=v_mem_norm_g, v_w_mem_kv=v_w_mem_kv, v_q_norm_g=v_q_norm_g, v_k_norm_g=v_k_norm_g, v_w_out=v_w_out)
    weights = {n: given[n] for n in TWIN_WEIGHTS}
    shared = {n: given[n] for n in SHARED_INPUTS}
    per_example = {n: given[n] for n in ['x', 'mem']}
    grad_fn = _jax.value_and_grad(_loss, argnums=(0, 1))

    def one_microbatch(ex, loss_target):
        ex = dict(ex)
        diff = ex.pop(TWIN_DIFF_INPUT)
        return grad_fn(weights, diff, {**shared, **ex}, loss_target)

    if N_MICROBATCH == 1:
        loss, (grad_w, grad_x) = one_microbatch(per_example, given["loss_target"])
    else:
        def body(carry, xs):
            loss_sum, grad_sum = carry
            l_k, (gw_k, gx_k) = one_microbatch(xs[0], xs[1])
            with _jax.named_scope("update"):
                return (loss_sum + l_k, _jax.tree.map(_jnp.add, grad_sum, gw_k)), gx_k

        init = (_jnp.zeros((), _jnp.float32), _jax.tree.map(_jnp.zeros_like, weights))
        (loss, grad_w), grad_x = _jax.lax.scan(body, init, (per_example, given["loss_target"]))
    with _jax.named_scope("update"):
        delta_w, new_m, new_v = {}, {}, {}
        for n in TWIN_WEIGHTS:
            delta_w[n], new_m[n], new_v[n] = _adamw(weights[n], grad_w[n], given["m_" + n], given["v_" + n])
    return (loss, grad_x, *[grad_w[n] for n in TWIN_WEIGHTS], *[delta_w[n] for n in TWIN_WEIGHTS],
            *[new_m[n] for n in TWIN_WEIGHTS], *[new_v[n] for n in TWIN_WEIGHTS])
```

```python
import functools
import math

import jax
import jax.numpy as jnp
from jax import lax
from jax.experimental import pallas as pl
from jax.experimental.pallas import tpu as pltpu

F32 = jnp.float32
BF16 = jnp.bfloat16
MESH = pl.DeviceIdType.MESH

D_MODEL = 2048
DEPTH = 2
CHUNK = 128
D_A = 1024
A_GROUPS = 8
D_B = 512
D_C = 512
HEADS = 4
HEAD_DIM = 128
IN_WIDTH = 6144
N_CHIPS = 4
EPS = 1e-6
ATT_SCALE = 1.0 / math.sqrt(HEAD_DIM)

OFF_U, OFF_V, OFF_ZA = 0, 1024, 2048
OFF_QB, OFF_KB, OFF_VB, OFF_ZB = 3072, 3584, 4096, 4608
OFF_QC, OFF_ZC = 5120, 5632
OFF_YB, OFF_YC = 1024, 1536

ADAM_LR = 0.001
ADAM_B1 = 0.9
ADAM_B2 = 0.999
ADAM_EPS = 1e-08
ADAM_WD = 0.01
ADAM_STEP = 10

MIB = 1024 * 1024
ANY = pl.BlockSpec(memory_space=pl.ANY)


def _params(semantics=None, vmem_mb=48):
    return pltpu.CompilerParams(dimension_semantics=semantics, vmem_limit_bytes=vmem_mb * MIB)


def _gelu(x):
    return 0.5 * x * (1.0 + lax.erf(x * (1.0 / math.sqrt(2.0))))


def _gelu_grad(x):
    cdf = 0.5 * (1.0 + lax.erf(x * (1.0 / math.sqrt(2.0))))
    pdf = jnp.exp(-0.5 * x * x) * (1.0 / math.sqrt(2.0 * math.pi))
    return cdf + x * pdf


def _sigmoid(x):
    return 1.0 / (1.0 + jnp.exp(-x))


def _silu_and_grad(z):
    s = _sigmoid(z)
    return z * s, s * (1.0 + z * (1.0 - s))


def _split_bf16(x):
    hi = x.astype(BF16)
    lo = (x - hi.astype(F32)).astype(BF16)
    return hi, lo


def _dot(a, b, dims):
    return lax.dot_general(a, b, (dims, ((), ())), preferred_element_type=F32)


NN = ((1,), (0,))
NT = ((1,), (1,))
TN = ((0,), (0,))


def _matmul(name, a, b, *, grid, a_spec, b_spec, o_spec, out_shape, dims, res=None, res_spec=None, alias=None,
            vmem_mb=48):
    nk = grid[2]
    n_in = 2 + (res is not None) + (alias is not None)

    def body(*refs):
        a_ref, b_ref = refs[0], refs[1]
        r_ref = refs[2] if res is not None else None
        o_ref = refs[n_in]
        part = _dot(a_ref[...], b_ref[...], dims)
        if nk == 1:
            if r_ref is not None:
                part = part + r_ref[...]
            o_ref[...] = part.astype(o_ref.dtype)
            return
        acc_ref = refs[n_in + 1]
        k = pl.program_id(2)

        @pl.when(k == 0)
        def _():
            acc_ref[...] = part

        @pl.when(k > 0)
        def _():
            acc_ref[...] += part

        @pl.when(k == nk - 1)
        def _():
            tot = acc_ref[...]
            if r_ref is not None:
                tot = tot + r_ref[...]
            o_ref[...] = tot.astype(o_ref.dtype)

    in_specs = [a_spec, b_spec]
    args = [a, b]
    if res is not None:
        in_specs.append(res_spec)
        args.append(res)
    aliases = {}
    if alias is not None:
        in_specs.append(ANY)
        args.append(alias)
        aliases = {len(args) - 1: 0}
    acc_shape = tuple(d for d in o_spec.block_shape if d is not None)
    scratch = [pltpu.VMEM(acc_shape, F32)] if nk > 1 else []
    return pl.pallas_call(
        body, name=name, grid=grid, in_specs=in_specs, out_specs=o_spec, out_shape=out_shape,
        scratch_shapes=scratch, input_output_aliases=aliases,
        compiler_params=_params(("parallel", "parallel", "arbitrary"), vmem_mb),
    )(*args)


def _rms_fwd(name, x, g, tr):
    rows, d = x.shape

    def body(x_ref, g_ref, h_ref):
        xv = x_ref[...]
        r = lax.rsqrt(jnp.mean(xv * xv, axis=-1, keepdims=True) + EPS)
        h_ref[...] = (xv * r * g_ref[...]).astype(BF16)

    return pl.pallas_call(
        body, name=name, grid=(rows // tr,),
        in_specs=[pl.BlockSpec((tr, d), lambda i: (i, 0)), pl.BlockSpec((1, d), lambda i: (0, 0))],
        out_specs=pl.BlockSpec((tr, d), lambda i: (i, 0)),
        out_shape=jax.ShapeDtypeStruct((rows, d), BF16),
        compiler_params=_params(("parallel",)),
    )(x, g)


def _rms_bwd(name, x, dh, dres, g, tr):
    rows, d = x.shape

    def body(x_ref, dh_ref, dres_ref, g_ref, dx_ref, dxb_ref, dg_ref):
        xv = x_ref[...]
        r = lax.rsqrt(jnp.mean(xv * xv, axis=-1, keepdims=True) + EPS)
        xhat = xv * r
        dhv = dh_ref[...]
        dxh = dhv * g_ref[...]
        dx = r * (dxh - xhat * jnp.mean(dxh * xhat, axis=-1, keepdims=True)) + dres_ref[...]
        dx_ref[...] = dx
        dxb_ref[...] = dx.astype(BF16)
        part = jnp.sum(dhv * xhat, axis=0, keepdims=True)

        @pl.when(pl.program_id(0) == 0)
        def _():
            dg_ref[...] = part

        @pl.when(pl.program_id(0) > 0)
        def _():
            dg_ref[...] += part

    blk = pl.BlockSpec((tr, d), lambda i: (i, 0))
    vec = pl.BlockSpec((1, d), lambda i: (0, 0))
    return pl.pallas_call(
        body, name=name, grid=(rows // tr,), in_specs=[blk, blk, blk, vec], out_specs=[blk, blk, vec],
        out_shape=[jax.ShapeDtypeStruct((rows, d), F32), jax.ShapeDtypeStruct((rows, d), BF16),
                   jax.ShapeDtypeStruct((1, d), F32)],
        compiler_params=_params(("arbitrary",)),
    )(x, dh, dres, g)


def _rms_gain_grad(name, x, dh):
    rows, d = x.shape

    def body(x_ref, dh_ref, dg_ref):
        xv = x_ref[...]
        r = lax.rsqrt(jnp.mean(xv * xv, axis=-1, keepdims=True) + EPS)
        dg_ref[...] = jnp.sum(dh_ref[...] * xv * r, axis=0, keepdims=True)

    return pl.pallas_call(
        body, name=name, out_shape=jax.ShapeDtypeStruct((1, d), F32), compiler_params=_params(None),
    )(x, dh)


def _loss_and_grad(name, y, target, tr):
    rows, d = y.shape
    n = rows // tr

    def body(y_ref, t_ref, dx_ref, dxb_ref, loss_ref, acc_ref):
        e = y_ref[...] - t_ref[...]
        dx = e * (1.0 / d)
        dx_ref[...] = dx
        dxb_ref[...] = dx.astype(BF16)
        part = jnp.sum(e * e, axis=0, keepdims=True)
        i = pl.program_id(0)

        @pl.when(i == 0)
        def _():
            acc_ref[...] = part

        @pl.when(i > 0)
        def _():
            acc_ref[...] += part

        @pl.when(i == n - 1)
        def _():
            loss_ref[...] = jnp.sum(acc_ref[...], axis=-1, keepdims=True) * (0.5 / d)

    blk = pl.BlockSpec((tr, d), lambda i: (i, 0))
    return pl.pallas_call(
        body, name=name, grid=(n,), in_specs=[blk, blk],
        out_specs=[blk, blk, pl.BlockSpec((1, 1), lambda i: (0, 0))],
        out_shape=[jax.ShapeDtypeStruct((rows, d), F32), jax.ShapeDtypeStruct((rows, d), BF16),
                   jax.ShapeDtypeStruct((1, 1), F32)],
        scratch_shapes=[pltpu.VMEM((1, d), F32)],
        compiler_params=_params(("arbitrary",)),
    )(y, target)


SB_T = 256


def _sb_scores(q, kblk):
    z = _dot(q, kblk, NT) * ATT_SCALE
    e = jnp.exp(-jnp.abs(z))
    sp = jnp.log1p(e)
    lb = jnp.minimum(z, 0.0) - sp
    l1 = lb - z
    return z, e, lb, l1


def _sb_fwd(name, proj):
    s_len = proj.shape[0]
    t = SB_T
    nq = s_len // t

    def body(q_ref, k_ref, v_ref, o_ref):
        i = pl.program_id(1)
        q = q_ref[...].astype(BF16)
        row = lax.broadcasted_iota(jnp.int32, (t, t), 0)
        col = lax.broadcasted_iota(jnp.int32, (t, t), 1)
        causal = col < row
        after_mat = (row > col).astype(BF16)

        def tile(kb, carry, acc, masked):
            start = pl.multiple_of(kb * t, t)
            kblk = k_ref[pl.ds(start, t), :].astype(BF16)
            vblk = v_ref[pl.ds(start, t), :].astype(BF16)
            _, _, lb, l1 = _sb_scores(q, kblk)
            if masked:
                l1 = jnp.where(causal, l1, 0.0)
            hi, lo = _split_bf16(l1)
            after = _dot(hi, after_mat, NN) + _dot(lo, after_mat, NN) + carry
            a = jnp.exp(lb + after)
            if masked:
                a = jnp.where(causal, a, 0.0)
            acc = acc + _dot(a.astype(BF16), vblk, NN)
            carry = carry + jnp.sum(l1, axis=-1, keepdims=True)
            return carry, acc

        carry, acc = tile(i, jnp.zeros((t, 1), F32), jnp.zeros((t, HEAD_DIM), F32), True)

        def step(n, state):
            return tile(i - 1 - n, state[0], state[1], False)

        carry, acc = lax.fori_loop(0, i, step, (carry, acc))
        o_ref[...] = acc

    cb = HEAD_DIM
    return pl.pallas_call(
        body, name=name, grid=(HEADS, nq),
        in_specs=[pl.BlockSpec((t, cb), lambda h, i: (i, OFF_QB // cb + h)),
                  pl.BlockSpec((s_len, cb), lambda h, i: (0, OFF_KB // cb + h)),
                  pl.BlockSpec((s_len, cb), lambda h, i: (0, OFF_VB // cb + h))],
        out_specs=pl.BlockSpec((t, cb), lambda h, i: (i, h)),
        out_shape=jax.ShapeDtypeStruct((s_len, D_B), F32),
        compiler_params=_params(("parallel", "arbitrary")),
    )(proj, proj, proj)


def _sb_bwd(name, proj, dy):
    s_len = proj.shape[0]
    t = SB_T
    nq = s_len // t

    def body(q_ref, k_ref, v_ref, z_ref, dy_ref, dq_ref, dk_ref, dv_ref, a_ref, s_ref):
        i = pl.program_id(1)

        @pl.when(i == 0)
        def _():
            dk_ref[...] = jnp.zeros_like(dk_ref)
            dv_ref[...] = jnp.zeros_like(dv_ref)

        q = q_ref[...].astype(BF16)
        silu_z, _ = _silu_and_grad(z_ref[...])
        do_b = (dy_ref[...] * silu_z).astype(BF16)
        row = lax.broadcasted_iota(jnp.int32, (t, t), 0)
        col = lax.broadcasted_iota(jnp.int32, (t, t), 1)
        causal = col < row
        after_mat = (row > col).astype(BF16)
        before_mat = (row < col).astype(BF16)

        def weights(kb, carry, masked):
            start = pl.multiple_of(kb * t, t)
            kblk = k_ref[pl.ds(start, t), :].astype(BF16)
            z, _, lb, l1 = _sb_scores(q, kblk)
            if masked:
                l1 = jnp.where(causal, l1, 0.0)
            hi, lo = _split_bf16(l1)
            after = _dot(hi, after_mat, NN) + _dot(lo, after_mat, NN) + carry
            a = jnp.exp(lb + after)
            if masked:
                a = jnp.where(causal, a, 0.0)
            a_ref[kb] = a
            s_ref[kb] = z
            return carry + jnp.sum(l1, axis=-1, keepdims=True)

        carry = weights(i, jnp.zeros((t, 1), F32), True)
        lax.fori_loop(0, i, lambda n, c: weights(i - 1 - n, c, False), carry)

        def grads(kb, carry, dq, masked):
            start = pl.multiple_of(kb * t, t)
            kblk = k_ref[pl.ds(start, t), :].astype(BF16)
            vblk = v_ref[pl.ds(start, t), :].astype(BF16)
            a = a_ref[kb]
            z = s_ref[kb]
            g = _dot(do_b, vblk, NT) * a
            ghi, glo = _split_bf16(g)
            prefix = _dot(ghi, before_mat, NN) + _dot(glo, before_mat, NN) + carry
            e = jnp.exp(-jnp.abs(z))
            inv = 1.0 / (1.0 + e)
            pos = z >= 0.0
            beta = jnp.where(pos, inv, e * inv)
            one_m_beta = jnp.where(pos, e * inv, inv)
            dz = (g * one_m_beta - prefix * beta) * ATT_SCALE
            if masked:
                dz = jnp.where(causal, dz, 0.0)
            dz_b = dz.astype(BF16)
            dq = dq + _dot(dz_b, kblk, NN)
            dk_ref[pl.ds(start, t), :] += _dot(dz_b, q, TN)
            dv_ref[pl.ds(start, t), :] += _dot(a.astype(BF16), do_b, TN)
            return carry + jnp.sum(g, axis=-1, keepdims=True), dq

        state = lax.fori_loop(0, i, lambda kb, st: grads(kb, st[0], st[1], False),
                              (jnp.zeros((t, 1), F32), jnp.zeros((t, HEAD_DIM), F32)))
        _, dq = grads(i, state[0], state[1], True)
        dq_ref[...] = dq

    cb = HEAD_DIM
    qblk = lambda off: pl.BlockSpec((t, cb), lambda h, i: (i, off // cb + h))
    full = lambda off: pl.BlockSpec((s_len, cb), lambda h, i: (0, off // cb + h))
    out = jax.ShapeDtypeStruct((s_len, D_B), F32)
    return pl.pallas_call(
        body, name=name, grid=(HEADS, nq),
        in_specs=[qblk(OFF_QB), full(OFF_KB), full(OFF_VB), qblk(OFF_ZB), qblk(OFF_YB)],
        out_specs=[qblk(0), full(0), full(0)],
        out_shape=[out, out, out],
        scratch_shapes=[pltpu.VMEM((nq, t, t), F32), pltpu.VMEM((nq, t, t), F32)],
        compiler_params=_params(("parallel", "arbitrary")),
    )(proj, proj, proj, proj, dy)


MEM_TQ = 512


def _qk_norm(x, g):
    r = lax.rsqrt(jnp.mean(x * x, axis=-1, keepdims=True) + EPS)
    xhat = x * r
    return xhat * g, xhat, r


def _qk_norm_bwd(dn, g, xhat, r):
    dxh = dn * g
    return r * (dxh - xhat * jnp.mean(dxh * xhat, axis=-1, keepdims=True))


def _mem_probs(q, mk, qg, kg):
    qn, qhat, rq = _qk_norm(q, qg)
    kn, khat, rk = _qk_norm(mk, kg)
    qn_b, kn_b = qn.astype(BF16), kn.astype(BF16)
    s = _dot(qn_b, kn_b, NT) * ATT_SCALE
    p = jnp.exp(s - jnp.max(s, axis=-1, keepdims=True))
    p = p / jnp.sum(p, axis=-1, keepdims=True)
    return p, qn_b, kn_b, qhat, rq, khat, rk


def _mem_fwd(name, proj, mem_kv, qg, kg):
    s_len = proj.shape[0]
    m_len = mem_kv.shape[0]
    tq = min(MEM_TQ, s_len)

    def body(q_ref, mk_ref, mv_ref, qg_ref, kg_ref, o_ref):
        p = _mem_probs(q_ref[...], mk_ref[...], qg_ref[...], kg_ref[...])[0]
        o_ref[...] = _dot(p.astype(BF16), mv_ref[...].astype(BF16), NN)

    cb = HEAD_DIM
    vec = pl.BlockSpec((1, cb), lambda h, i: (0, 0))
    return pl.pallas_call(
        body, name=name, grid=(HEADS, s_len // tq),
        in_specs=[pl.BlockSpec((tq, cb), lambda h, i: (i, OFF_QC // cb + h)),
                  pl.BlockSpec((m_len, cb), lambda h, i: (0, h)),
                  pl.BlockSpec((m_len, cb), lambda h, i: (0, HEADS + h)), vec, vec],
        out_specs=pl.BlockSpec((tq, cb), lambda h, i: (i, h)),
        out_shape=jax.ShapeDtypeStruct((s_len, D_C), F32),
        compiler_params=_params(("parallel", "parallel")),
    )(proj, mem_kv, mem_kv, qg, kg)


def _mem_bwd(name, proj, mem_kv, qg, kg, dy):
    s_len = proj.shape[0]
    m_len = mem_kv.shape[0]
    tq = min(MEM_TQ, s_len)

    def body(q_ref, mk_ref, mv_ref, qg_ref, kg_ref, z_ref, dy_ref, dq_ref, dmk_ref, dmv_ref, dqg_ref, dkg_ref):
        h, i = pl.program_id(0), pl.program_id(1)

        @pl.when(i == 0)
        def _():
            dmk_ref[...] = jnp.zeros_like(dmk_ref)
            dmv_ref[...] = jnp.zeros_like(dmv_ref)

        @pl.when((i == 0) & (h == 0))
        def _():
            dqg_ref[...] = jnp.zeros_like(dqg_ref)
            dkg_ref[...] = jnp.zeros_like(dkg_ref)

        qg, kg = qg_ref[...], kg_ref[...]
        p, qn_b, kn_b, qhat, rq, khat, rk = _mem_probs(q_ref[...], mk_ref[...], qg, kg)
        silu_z, _ = _silu_and_grad(z_ref[...])
        do_b = (dy_ref[...] * silu_z).astype(BF16)
        dmv_ref[...] += _dot(p.astype(BF16), do_b, TN)
        dp = _dot(do_b, mv_ref[...].astype(BF16), NT)
        ds = (p * (dp - jnp.sum(dp * p, axis=-1, keepdims=True)) * ATT_SCALE).astype(BF16)
        dqn = _dot(ds, kn_b, NN)
        dkn = _dot(ds, qn_b, TN)
        dq_ref[...] = _qk_norm_bwd(dqn, qg, qhat, rq)
        dmk_ref[...] += _qk_norm_bwd(dkn, kg, khat, rk)
        dqg_ref[...] += jnp.sum(dqn * qhat, axis=0, keepdims=True)
        dkg_ref[...] += jnp.sum(dkn * khat, axis=0, keepdims=True)

    cb = HEAD_DIM
    vec = pl.BlockSpec((1, cb), lambda h, i: (0, 0))
    qblk = lambda off: pl.BlockSpec((tq, cb), lambda h, i: (i, off // cb + h))
    memblk = lambda off: pl.BlockSpec((m_len, cb), lambda h, i: (0, off + h))
    return pl.pallas_call(
        body, name=name, grid=(HEADS, s_len // tq),
        in_specs=[qblk(OFF_QC), memblk(0), memblk(HEADS), vec, vec, qblk(OFF_ZC), qblk(OFF_YC)],
        out_specs=[qblk(0), memblk(0), memblk(0), vec, vec],
        out_shape=[jax.ShapeDtypeStruct((s_len, D_C), F32), jax.ShapeDtypeStruct((m_len, D_C), F32),
                   jax.ShapeDtypeStruct((m_len, D_C), F32), jax.ShapeDtypeStruct((1, cb), F32),
                   jax.ShapeDtypeStruct((1, cb), F32)],
        compiler_params=_params(("arbitrary", "arbitrary")),
    )(proj, mem_kv, mem_kv, qg, kg, proj, dy)


def _sgu_common(u_ref, v_ref, lng_ref, lnb_ref, w_ref, bias_ref):
    ug = _gelu(u_ref[...])
    vg = _gelu(v_ref[...])
    mu = jnp.mean(vg, axis=-1, keepdims=True)
    xc = vg - mu
    rstd = lax.rsqrt(jnp.mean(xc * xc, axis=-1, keepdims=True) + EPS)
    xhat = xc * rstd
    vn = xhat * lng_ref[...] + lnb_ref[...]
    vn_b = vn.astype(BF16)
    row = lax.broadcasted_iota(jnp.int32, (CHUNK, CHUNK), 0)
    col = lax.broadcasted_iota(jnp.int32, (CHUNK, CHUNK), 1)
    tril = row >= col
    mixed = []
    for g in range(A_GROUPS):
        w = jnp.where(tril, w_ref[g], 0.0).astype(BF16)
        sl = slice(g * CHUNK, (g + 1) * CHUNK)
        mixed.append(_dot(w, vn_b[:, sl], NN) + bias_ref[:, sl])
    return ug, xhat, rstd, vn_b, mixed, tril


def _gate_fwd(name, proj, o_b, o_c, lng, lnb, w_s, bias):
    s_len = proj.shape[0]

    def body(u_ref, v_ref, za_ref, zb_ref, zc_ref, ob_ref, oc_ref, lng_ref, lnb_ref, w_ref, bias_ref, y_ref):
        ug, _, _, _, mixed, _ = _sgu_common(u_ref, v_ref, lng_ref, lnb_ref, w_ref, bias_ref)
        sza, _ = _silu_and_grad(za_ref[...])
        gate = ug * sza
        for g in range(A_GROUPS):
            sl = slice(g * CHUNK, (g + 1) * CHUNK)
            y_ref[:, sl] = (gate[:, sl] * mixed[g]).astype(BF16)
        szb, _ = _silu_and_grad(zb_ref[...])
        y_ref[:, OFF_YB:OFF_YB + D_B] = (ob_ref[...] * szb).astype(BF16)
        szc, _ = _silu_and_grad(zc_ref[...])
        y_ref[:, OFF_YC:OFF_YC + D_C] = (oc_ref[...] * szc).astype(BF16)

    wide = lambda off: pl.BlockSpec((CHUNK, D_A), lambda i: (i, off // D_A))
    narrow = lambda off: pl.BlockSpec((CHUNK, D_B), lambda i: (i, off // D_B))
    vec = pl.BlockSpec((1, D_A), lambda i: (0, 0))
    return pl.pallas_call(
        body, name=name, grid=(s_len // CHUNK,),
        in_specs=[wide(OFF_U), wide(OFF_V), wide(OFF_ZA), narrow(OFF_ZB), narrow(OFF_ZC), narrow(0), narrow(0), vec, vec,
                  pl.BlockSpec((A_GROUPS, CHUNK, CHUNK), lambda i: (0, 0, 0)),
                  pl.BlockSpec((CHUNK, D_A), lambda i: (0, 0))],
        out_specs=pl.BlockSpec((CHUNK, D_MODEL), lambda i: (i, 0)),
        out_shape=jax.ShapeDtypeStruct((s_len, D_MODEL), BF16),
        compiler_params=_params(("parallel",)),
    )(proj, proj, proj, proj, proj, o_b, o_c, lng, lnb, w_s, bias)


def _gate_bwd(name, proj, dy, o_b, o_c, dqkv, dq_c, lng, lnb, w_s, w_s_t, bias):
    s_len = proj.shape[0]
    n = s_len // CHUNK
    dq_b, dk_b, dv_b = dqkv

    def body(u_ref, v_ref, za_ref, zb_ref, zc_ref, dya_ref, dyb_ref, dyc_ref, ob_ref, oc_ref, dq_ref, dk_ref, dv_ref,
             dqc_ref, lng_ref, lnb_ref, w_ref, wt_ref, bias_ref, dp_ref, dw_ref, dsb_ref, dlng_ref, dlnb_ref, dbias_ref):
        i = pl.program_id(0)

        @pl.when(i == 0)
        def _():
            dw_ref[...] = jnp.zeros_like(dw_ref)
            dbias_ref[...] = jnp.zeros_like(dbias_ref)
            dlng_ref[...] = jnp.zeros_like(dlng_ref)
            dlnb_ref[...] = jnp.zeros_like(dlnb_ref)

        ug, xhat, rstd, vn_b, mixed, tril = _sgu_common(u_ref, v_ref, lng_ref, lnb_ref, w_ref, bias_ref)
        za = za_ref[...]
        sza, dsza = _silu_and_grad(za)
        dya = dya_ref[...]
        mixed_all = jnp.concatenate(mixed, axis=-1)
        d_mixed = dya * ug * sza
        dp_ref[:, OFF_U:OFF_U + D_A] = (dya * mixed_all * sza * _gelu_grad(u_ref[...])).astype(BF16)
        dp_ref[:, OFF_ZA:OFF_ZA + D_A] = (dya * ug * mixed_all * dsza).astype(BF16)
        dbias_ref[...] += d_mixed
        dm_b = d_mixed.astype(BF16)
        triu = lax.broadcasted_iota(jnp.int32, (CHUNK, CHUNK), 0) <= lax.broadcasted_iota(jnp.int32, (CHUNK, CHUNK), 1)
        d_vn = []
        for g in range(A_GROUPS):
            sl = slice(g * CHUNK, (g + 1) * CHUNK)
            wt = jnp.where(triu, wt_ref[g], 0.0).astype(BF16)
            d_vn.append(_dot(wt, dm_b[:, sl], NN))
            dw_ref[g] += jnp.where(tril, _dot(dm_b[:, sl], vn_b[:, sl], NT), 0.0)
        d_vn = jnp.concatenate(d_vn, axis=-1)
        dlng_ref[...] += jnp.sum(d_vn * xhat, axis=0, keepdims=True)
        dlnb_ref[...] += jnp.sum(d_vn, axis=0, keepdims=True)
        dxh = d_vn * lng_ref[...]
        d_vg = rstd * (dxh - jnp.mean(dxh, axis=-1, keepdims=True)
                       - xhat * jnp.mean(dxh * xhat, axis=-1, keepdims=True))
        dp_ref[:, OFF_V:OFF_V + D_A] = (d_vg * _gelu_grad(v_ref[...])).astype(BF16)
        dp_ref[:, OFF_QB:OFF_QB + D_B] = dq_ref[...].astype(BF16)
        dp_ref[:, OFF_KB:OFF_KB + D_B] = dk_ref[...].astype(BF16)
        dp_ref[:, OFF_VB:OFF_VB + D_B] = dv_ref[...].astype(BF16)
        _, dszb = _silu_and_grad(zb_ref[...])
        dp_ref[:, OFF_ZB:OFF_ZB + D_B] = (dyb_ref[...] * ob_ref[...] * dszb).astype(BF16)
        dp_ref[:, OFF_QC:OFF_QC + D_C] = dqc_ref[...].astype(BF16)
        _, dszc = _silu_and_grad(zc_ref[...])
        dp_ref[:, OFF_ZC:OFF_ZC + D_C] = (dyc_ref[...] * oc_ref[...] * dszc).astype(BF16)

        @pl.when(i == n - 1)
        def _():
            ch = lax.broadcasted_iota(jnp.int32, (D_A, CHUNK), 0)
            gcol = lax.broadcasted_iota(jnp.int32, (D_A, CHUNK), 1)
            pick = (ch // (D_A // A_GROUPS) == gcol).astype(BF16)
            rest = dbias_ref[...]
            tot = jnp.zeros((CHUNK, CHUNK), F32)
            for _ in range(3):
                term = rest.astype(BF16)
                tot = tot + _dot(term, pick, NN)
                rest = rest - term.astype(F32)
            dsb_ref[...] = tot

    wide = lambda off: pl.BlockSpec((CHUNK, D_A), lambda i: (i, off // D_A))
    narrow = lambda off: pl.BlockSpec((CHUNK, D_B), lambda i: (i, off // D_B))
    vec = pl.BlockSpec((1, D_A), lambda i: (0, 0))
    wspec = pl.BlockSpec((A_GROUPS, CHUNK, CHUNK), lambda i: (0, 0, 0))
    bspec = pl.BlockSpec((CHUNK, D_A), lambda i: (0, 0))
    return pl.pallas_call(
        body, name=name, grid=(n,),
        in_specs=[wide(OFF_U), wide(OFF_V), wide(OFF_ZA), narrow(OFF_ZB), narrow(OFF_ZC),
                  wide(0), narrow(OFF_YB), narrow(OFF_YC), narrow(0), narrow(0), narrow(0), narrow(0), narrow(0),
                  narrow(0), vec, vec, wspec, wspec, bspec],
        out_specs=[pl.BlockSpec((CHUNK, IN_WIDTH), lambda i: (i, 0)), wspec,
                   pl.BlockSpec((CHUNK, CHUNK), lambda i: (0, 0)), vec, vec],
        out_shape=[jax.ShapeDtypeStruct((s_len, IN_WIDTH), BF16), jax.ShapeDtypeStruct((A_GROUPS, CHUNK, CHUNK), F32),
                   jax.ShapeDtypeStruct((CHUNK, CHUNK), F32), jax.ShapeDtypeStruct((1, D_A), F32),
                   jax.ShapeDtypeStruct((1, D_A), F32)],
        scratch_shapes=[pltpu.VMEM((CHUNK, D_A), F32)],
        compiler_params=_params(("arbitrary",)),
    )(proj, proj, proj, proj, proj, dy, dy, dy, o_b, o_c, dq_b, dk_b, dv_b, dq_c, lng, lnb, w_s, w_s_t, bias)


IN_SHARD = IN_WIDTH // N_CHIPS
ROW_SHARD = D_MODEL // N_CHIPS


def _bias_rows(sgu_b_l):
    return jnp.repeat(sgu_b_l.T, D_A // A_GROUPS, axis=1)


def _layer_fwd(l, x, mem, sm, w_in_all, w_kv_all, w_out_all):
    s_len = x.shape[0]
    m_len = mem.shape[0]
    tm = min(1024, s_len)
    tn = 768
    per = IN_SHARD // tn
    h = _rms_fwd(f"rms_fwd_{l}", x, sm["norm_g"][l][None], min(256, s_len))
    proj = _matmul(
        f"in_proj_{l}", h, w_in_all, grid=(s_len // tm, IN_WIDTH // tn, 1),
        a_spec=pl.BlockSpec((tm, D_MODEL), lambda i, j, k: (i, 0)),
        b_spec=pl.BlockSpec((None, None, D_MODEL, tn), lambda i, j, k: (l, j // per, 0, j % per)),
        o_spec=pl.BlockSpec((tm, tn), lambda i, j, k: (i, j)),
        out_shape=jax.ShapeDtypeStruct((s_len, IN_WIDTH), F32), dims=NN)
    mem_h = _rms_fwd(f"mem_rms_fwd_{l}", mem, sm["mem_norm_g"][l][None], m_len)
    mem_kv = _matmul(
        f"mem_kv_{l}", mem_h, w_kv_all, grid=(1, 2, N_CHIPS),
        a_spec=pl.BlockSpec((m_len, ROW_SHARD), lambda i, j, k: (0, k)),
        b_spec=pl.BlockSpec((None, None, ROW_SHARD, D_C), lambda i, j, k: (l, k, 0, j)),
        o_spec=pl.BlockSpec((m_len, D_C), lambda i, j, k: (0, j)),
        out_shape=jax.ShapeDtypeStruct((m_len, 2 * D_C), F32), dims=NN)
    o_b = _sb_fwd(f"sb_fwd_{l}", proj)
    qg, kg = sm["q_norm_g"][l][None], sm["k_norm_g"][l][None]
    o_c = _mem_fwd(f"mem_fwd_{l}", proj, mem_kv, qg, kg)
    bias = _bias_rows(sm["sgu_b"][l])
    y = _gate_fwd(f"gate_fwd_{l}", proj, o_b, o_c, sm["sgu_ln_g"][l][None], sm["sgu_ln_b"][l][None], sm["sgu_w"][l], bias)
    tn_o = 512
    x_next = _matmul(
        f"out_proj_{l}", y, w_out_all, grid=(s_len // tm, D_MODEL // tn_o, N_CHIPS),
        a_spec=pl.BlockSpec((tm, ROW_SHARD), lambda i, j, k: (i, k)),
        b_spec=pl.BlockSpec((None, None, ROW_SHARD, tn_o), lambda i, j, k: (l, k, 0, j)),
        o_spec=pl.BlockSpec((tm, tn_o), lambda i, j, k: (i, j)),
        out_shape=jax.ShapeDtypeStruct((s_len, D_MODEL), F32), dims=NN,
        res=x, res_spec=pl.BlockSpec((tm, tn_o), lambda i, j, k: (i, j)))
    saved = dict(x=x, h=h, proj=proj, mem_h=mem_h, mem_kv=mem_kv, o_b=o_b, o_c=o_c, y=y, bias=bias)
    return x_next, saved


def _layer_bwd(l, dxo, dxo_b, mem, sm, saved, w_in_all, w_kv_all, w_out_all, g_in, g_kv, g_out):
    s_len = dxo.shape[0]
    m_len = mem.shape[0]
    proj, y, h, mem_h, mem_kv = saved["proj"], saved["y"], saved["h"], saved["mem_h"], saved["mem_kv"]
    tm = min(1024, s_len)
    tk = min(1024, s_len)
    g_out = _matmul(
        f"d_w_out_{l}", y, dxo_b, grid=(N_CHIPS, D_MODEL // 1024, s_len // tk),
        a_spec=pl.BlockSpec((tk, ROW_SHARD), lambda i, j, k: (k, i)),
        b_spec=pl.BlockSpec((tk, 1024), lambda i, j, k: (k, j)),
        o_spec=pl.BlockSpec((None, None, ROW_SHARD, 1024), lambda i, j, k: (l, i, 0, j)),
        out_shape=jax.ShapeDtypeStruct((DEPTH, N_CHIPS, ROW_SHARD, D_MODEL), F32), dims=TN, alias=g_out)
    dy = _matmul(
        f"d_y_{l}", dxo_b, w_out_all, grid=(s_len // tm, N_CHIPS, 1),
        a_spec=pl.BlockSpec((tm, D_MODEL), lambda i, j, k: (i, 0)),
        b_spec=pl.BlockSpec((None, None, ROW_SHARD, D_MODEL), lambda i, j, k: (l, j, 0, 0)),
        o_spec=pl.BlockSpec((tm, ROW_SHARD), lambda i, j, k: (i, j)),
        out_shape=jax.ShapeDtypeStruct((s_len, D_MODEL), F32), dims=NT)
    qg, kg = sm["q_norm_g"][l][None], sm["k_norm_g"][l][None]
    dq_c, dmk, dmv, dqg, dkg = _mem_bwd(f"mem_bwd_{l}", proj, mem_kv, qg, kg, dy)
    dqkv = _sb_bwd(f"sb_bwd_{l}", proj, dy)
    w_s = sm["sgu_w"][l]
    dproj, dws, dbias, dlng, dlnb = _gate_bwd(
        f"gate_bwd_{l}", proj, dy, saved["o_b"], saved["o_c"], dqkv, dq_c, sm["sgu_ln_g"][l][None],
        sm["sgu_ln_b"][l][None], w_s, jnp.swapaxes(w_s, 1, 2), saved["bias"])
    tn = 768
    per = IN_SHARD // tn
    g_in = _matmul(
        f"d_w_in_{l}", h, dproj, grid=(D_MODEL // 1024, IN_WIDTH // tn, s_len // tk),
        a_spec=pl.BlockSpec((tk, 1024), lambda i, j, k: (k, i)),
        b_spec=pl.BlockSpec((tk, tn), lambda i, j, k: (k, j)),
        o_spec=pl.BlockSpec((None, None, 1024, tn), lambda i, j, k: (l, j // per, i, j % per)),
        out_shape=jax.ShapeDtypeStruct((DEPTH, N_CHIPS, D_MODEL, IN_SHARD), F32), dims=TN, alias=g_in)
    dh = _matmul(
        f"d_h_{l}", dproj, w_in_all, grid=(s_len // tm, D_MODEL // 1024, N_CHIPS),
        a_spec=pl.BlockSpec((tm, IN_SHARD), lambda i, j, k: (i, k)),
        b_spec=pl.BlockSpec((None, None, 1024, IN_SHARD), lambda i, j, k: (l, k, j, 0)),
        o_spec=pl.BlockSpec((tm, 1024), lambda i, j, k: (i, j)),
        out_shape=jax.ShapeDtypeStruct((s_len, D_MODEL), F32), dims=NT)
    dx, dx_b, dng = _rms_bwd(f"rms_bwd_{l}", saved["x"], dh, dxo, sm["norm_g"][l][None], min(256, s_len))
    dkv_b = jnp.concatenate([dmk, dmv], axis=1).astype(BF16)
    g_kv = _matmul(
        f"d_w_kv_{l}", mem_h, dkv_b, grid=(N_CHIPS, 1, 1),
        a_spec=pl.BlockSpec((m_len, ROW_SHARD), lambda i, j, k: (0, i)),
        b_spec=pl.BlockSpec((m_len, 2 * D_C), lambda i, j, k: (0, 0)),
        o_spec=pl.BlockSpec((None, None, ROW_SHARD, 2 * D_C), lambda i, j, k: (l, i, 0, 0)),
        out_shape=jax.ShapeDtypeStruct((DEPTH, N_CHIPS, ROW_SHARD, 2 * D_C), F32), dims=TN, alias=g_kv)
    d_mem_h = _matmul(
        f"d_mem_h_{l}", dkv_b, w_kv_all, grid=(1, N_CHIPS, 1),
        a_spec=pl.BlockSpec((m_len, 2 * D_C), lambda i, j, k: (0, 0)),
        b_spec=pl.BlockSpec((None, None, ROW_SHARD, 2 * D_C), lambda i, j, k: (l, j, 0, 0)),
        o_spec=pl.BlockSpec((m_len, ROW_SHARD), lambda i, j, k: (0, j)),
        out_shape=jax.ShapeDtypeStruct((m_len, D_MODEL), F32), dims=NT)
    dmng = _rms_gain_grad(f"mem_rms_bwd_{l}", mem, d_mem_h)
    dsgu_b = dbias[:, :A_GROUPS].T
    small = dict(norm_g=dng[0], sgu_ln_g=dlng[0], sgu_ln_b=dlnb[0], sgu_w=dws, sgu_b=dsgu_b, mem_norm_g=dmng[0],
                 q_norm_g=dqg[0], k_norm_g=dkg[0])
    return dx, dx_b, small, g_in, g_kv, g_out


SMALL_NAMES = ("norm_g", "sgu_ln_g", "sgu_ln_b", "sgu_w", "sgu_b", "mem_norm_g", "q_norm_g", "k_norm_g")


def _local_step(x, mem, target, sm, w_in_all, w_kv_all, w_out_all):
    saved = []
    cur = x
    for l in range(DEPTH):
        cur, sv = _layer_fwd(l, cur, mem, sm, w_in_all, w_kv_all, w_out_all)
        saved.append(sv)
    dxo, dxo_b, loss = _loss_and_grad("loss", cur, target, min(256, x.shape[0]))
    g_in = g_kv = g_out = None
    small = [None] * DEPTH
    for l in reversed(range(DEPTH)):
        dxo, dxo_b, small[l], g_in, g_kv, g_out = _layer_bwd(
            l, dxo, dxo_b, mem, sm, saved[l], w_in_all, w_kv_all, w_out_all, g_in, g_kv, g_out)
    small = {k: jnp.stack([small[l][k] for l in range(DEPTH)]) for k in SMALL_NAMES}
    return loss, dxo, small, g_in, g_kv, g_out


def _place():
    x, y, c = lax.axis_index("x"), lax.axis_index("y"), lax.axis_index("c")
    return x, y, c


def _other_chips(x, y):
    return [(1 - x, y, 2 * (1 - x) + y), (x, 1 - y, 2 * x + 1 - y), (1 - x, 1 - y, 2 * (1 - x) + 1 - y)]


def _all_gather_weights(shards):
    n_t = len(shards)

    def body(*refs):
        src, dst = refs[:n_t], refs[n_t:2 * n_t]
        send_sems, recv_sems, fwd_send, fwd_recv, local_sems = refs[2 * n_t:]
        x, y, c = _place()
        me = 2 * x + y
        chips = _other_chips(x, y)

        def half(ref_rows):
            return pl.ds(c * (ref_rows // 2), ref_rows // 2)

        local = []
        for t in range(n_t):
            cp = pltpu.make_async_copy(src[t], dst[t].at[:, me], local_sems.at[t])
            cp.start()
            local.append(cp)
        sends, fwds = [], []
        for t in range(n_t):
            rows = src[t].shape[1]
            for j, (px, py, pk) in enumerate(chips):
                s = t * 3 + j
                cp = pltpu.make_async_remote_copy(
                    src_ref=src[t].at[:, half(rows)], dst_ref=dst[t].at[:, me, half(rows)],
                    send_sem=send_sems.at[s], recv_sem=recv_sems.at[s], device_id=(px, py, c), device_id_type=MESH)
                cp.start()
                sends.append(cp)
        for t in range(n_t):
            rows = src[t].shape[1]
            for j, (px, py, pk) in enumerate(chips):
                s = t * 3 + j
                landed = dst[t].at[:, pk, half(rows)]
                pltpu.make_async_remote_copy(
                    src_ref=landed, dst_ref=landed, send_sem=send_sems.at[s], recv_sem=recv_sems.at[s],
                    device_id=(px, py, c), device_id_type=MESH).wait_recv()
                cp = pltpu.make_async_remote_copy(
                    src_ref=landed, dst_ref=landed, send_sem=fwd_send.at[s], recv_sem=fwd_recv.at[s],
                    device_id=(x, y, 1 - c), device_id_type=MESH)
                cp.start()
                fwds.append(cp)
        for t in range(n_t):
            rows = src[t].shape[1]
            for j, (px, py, pk) in enumerate(chips):
                s = t * 3 + j
                other = dst[t].at[:, pk, pl.ds((1 - c) * (rows // 2), rows // 2)]
                pltpu.make_async_remote_copy(
                    src_ref=other, dst_ref=other, send_sem=fwd_send.at[s], recv_sem=fwd_recv.at[s],
                    device_id=(x, y, 1 - c), device_id_type=MESH).wait_recv()
        for cp in sends + fwds:
            cp.wait_send()
        for cp in local:
            cp.wait()

    n = 3 * n_t
    return pl.pallas_call(
        body, name="all_gather_weights",
        in_specs=[ANY] * n_t, out_specs=[ANY] * n_t,
        out_shape=[jax.ShapeDtypeStruct((s.shape[0], N_CHIPS) + s.shape[1:], s.dtype) for s in shards],
        scratch_shapes=[pltpu.SemaphoreType.DMA((n,)), pltpu.SemaphoreType.DMA((n,)), pltpu.SemaphoreType.DMA((n,)),
                        pltpu.SemaphoreType.DMA((n,)), pltpu.SemaphoreType.DMA((n_t,))],
        compiler_params=pltpu.CompilerParams(has_side_effects=True),
    )(*shards)


def _core_exchange(grads):
    n_t = len(grads)

    def body(*refs):
        src = refs[:n_t]
        mine, theirs = refs[n_t:2 * n_t], refs[2 * n_t:3 * n_t]
        send_sems, recv_sems, local_sems = refs[3 * n_t:]
        x, y, c = _place()
        copies, local = [], []
        for t in range(n_t):
            hr = src[t].shape[2] // 2
            cp = pltpu.make_async_remote_copy(
                src_ref=src[t].at[:, :, pl.ds((1 - c) * hr, hr)], dst_ref=theirs[t],
                send_sem=send_sems.at[t], recv_sem=recv_sems.at[t], device_id=(x, y, 1 - c), device_id_type=MESH)
            cp.start()
            copies.append(cp)
            lc = pltpu.make_async_copy(src[t].at[:, :, pl.ds(c * hr, hr)], mine[t], local_sems.at[t])
            lc.start()
            local.append(lc)
        for cp in copies:
            cp.wait()
        for lc in local:
            lc.wait()

    half = [jax.ShapeDtypeStruct(g.shape[:2] + (g.shape[2] // 2, g.shape[3]), g.dtype) for g in grads]
    outs = pl.pallas_call(
        body, name="grad_core_exchange",
        in_specs=[ANY] * n_t, out_specs=[ANY] * (2 * n_t), out_shape=half + half,
        scratch_shapes=[pltpu.SemaphoreType.DMA((n_t,)), pltpu.SemaphoreType.DMA((n_t,)),
                        pltpu.SemaphoreType.DMA((n_t,))],
        compiler_params=pltpu.CompilerParams(has_side_effects=True),
    )(*grads)
    return list(zip(outs[:n_t], outs[n_t:]))


def _add_to_bf16(name, a, b):
    depth, chips, rows, cols = a.shape
    tr = min(256, rows)

    def body(a_ref, b_ref, o_ref):
        o_ref[...] = (a_ref[...] + b_ref[...]).astype(BF16)

    blk = pl.BlockSpec((None, None, tr, cols), lambda l, k, i: (l, k, i, 0))
    return pl.pallas_call(
        body, name=name, grid=(depth, chips, rows // tr), in_specs=[blk, blk], out_specs=blk,
        out_shape=jax.ShapeDtypeStruct(a.shape, BF16), compiler_params=_params(("parallel",) * 3),
    )(a, b)


def _chip_exchange(parts):
    n_t = len(parts)

    def body(*refs):
        src, dst = refs[:n_t], refs[n_t:2 * n_t]
        send_sems, recv_sems, local_sems = refs[2 * n_t:]
        x, y, c = _place()
        me = 2 * x + y
        chips = _other_chips(x, y)
        copies, local = [], []
        for t in range(n_t):
            lc = pltpu.make_async_copy(src[t].at[:, me], dst[t].at[me], local_sems.at[t])
            lc.start()
            local.append(lc)
            for j, (px, py, pk) in enumerate(chips):
                s = t * 3 + j
                cp = pltpu.make_async_remote_copy(
                    src_ref=src[t].at[:, pk], dst_ref=dst[t].at[me], send_sem=send_sems.at[s], recv_sem=recv_sems.at[s],
                    device_id=(px, py, c), device_id_type=MESH)
                cp.start()
                copies.append(cp)
        for t in range(n_t):
            for j, (px, py, pk) in enumerate(chips):
                s = t * 3 + j
                landed = dst[t].at[pk]
                pltpu.make_async_remote_copy(
                    src_ref=landed, dst_ref=landed, send_sem=send_sems.at[s], recv_sem=recv_sems.at[s],
                    device_id=(px, py, c), device_id_type=MESH).wait_recv()
        for cp in copies:
            cp.wait_send()
        for lc in local:
            lc.wait()

    n = 3 * n_t
    return pl.pallas_call(
        body, name="grad_chip_exchange",
        in_specs=[ANY] * n_t, out_specs=[ANY] * n_t,
        out_shape=[jax.ShapeDtypeStruct((N_CHIPS, p.shape[0]) + p.shape[2:], p.dtype) for p in parts],
        scratch_shapes=[pltpu.SemaphoreType.DMA((n,)), pltpu.SemaphoreType.DMA((n,)), pltpu.SemaphoreType.DMA((n_t,))],
        compiler_params=pltpu.CompilerParams(has_side_effects=True),
    )(*parts)


def _sum_chips(name, q):
    chips, depth, rows, cols = q.shape
    tr = min(256, rows)

    def body(q_ref, o_ref):
        tot = q_ref[0].astype(F32)
        for k in range(1, chips):
            tot = tot + q_ref[k].astype(F32)
        o_ref[...] = tot

    return pl.pallas_call(
        body, name=name, grid=(depth, rows // tr),
        in_specs=[pl.BlockSpec((chips, None, tr, cols), lambda l, i: (0, l, i, 0))],
        out_specs=pl.BlockSpec((None, tr, cols), lambda l, i: (l, i, 0)),
        out_shape=jax.ShapeDtypeStruct((depth, rows, cols), F32), compiler_params=_params(("parallel",) * 2),
    )(q)


def _core_share(halves):
    n_t = len(halves)

    def body(*refs):
        src, dst = refs[:n_t], refs[n_t:2 * n_t]
        send_sems, recv_sems, local_sems = refs[2 * n_t:]
        x, y, c = _place()
        copies, local = [], []
        for t in range(n_t):
            hr = src[t].shape[1]
            rows = dst[t].at[:, pl.ds(c * hr, hr)]
            cp = pltpu.make_async_remote_copy(
                src_ref=src[t], dst_ref=rows, send_sem=send_sems.at[t], recv_sem=recv_sems.at[t],
                device_id=(x, y, 1 - c), device_id_type=MESH)
            cp.start()
            copies.append(cp)
            lc = pltpu.make_async_copy(src[t], rows, local_sems.at[t])
            lc.start()
            local.append(lc)
        for t in range(n_t):
            hr = src[t].shape[1]
            theirs = dst[t].at[:, pl.ds((1 - c) * hr, hr)]
            pltpu.make_async_remote_copy(
                src_ref=src[t], dst_ref=theirs, send_sem=send_sems.at[t], recv_sem=recv_sems.at[t],
                device_id=(x, y, 1 - c), device_id_type=MESH).wait_recv()
        for cp in copies:
            cp.wait_send()
        for lc in local:
            lc.wait()

    return pl.pallas_call(
        body, name="grad_core_share",
        in_specs=[ANY] * n_t, out_specs=[ANY] * n_t,
        out_shape=[jax.ShapeDtypeStruct((h.shape[0], 2 * h.shape[1], h.shape[2]), h.dtype) for h in halves],
        scratch_shapes=[pltpu.SemaphoreType.DMA((n_t,)), pltpu.SemaphoreType.DMA((n_t,)),
                        pltpu.SemaphoreType.DMA((n_t,))],
        compiler_params=pltpu.CompilerParams(has_side_effects=True),
    )(*halves)


N_DEV = 8


def _all_reduce_small(vec):
    rows, lanes = vec.shape

    def body(v_ref, o_ref, gath_ref, send_sems, recv_sems):
        x, y, c = _place()
        me = 4 * x + 2 * y + c
        gath_ref[me] = v_ref[...]
        peers = [(x, y, 1 - c)]
        for px, py, _ in _other_chips(x, y):
            peers += [(px, py, c), (px, py, 1 - c)]
        copies = []
        for j, peer in enumerate(peers):
            cp = pltpu.make_async_remote_copy(
                src_ref=v_ref, dst_ref=gath_ref.at[me], send_sem=send_sems.at[j], recv_sem=recv_sems.at[j],
                device_id=peer, device_id_type=MESH)
            cp.start()
            copies.append(cp)
        for j, (px, py, pc) in enumerate(peers):
            slot = gath_ref.at[4 * px + 2 * py + pc]
            pltpu.make_async_remote_copy(
                src_ref=v_ref, dst_ref=slot, send_sem=send_sems.at[j], recv_sem=recv_sems.at[j],
                device_id=(px, py, pc), device_id_type=MESH).wait_recv()
        tot = gath_ref[0]
        for k in range(1, N_DEV):
            tot = tot + gath_ref[k]
        o_ref[...] = tot
        for cp in copies:
            cp.wait_send()

    vm = pl.BlockSpec(memory_space=pltpu.VMEM)
    return pl.pallas_call(
        body, name="small_all_reduce", in_specs=[vm], out_specs=vm,
        out_shape=jax.ShapeDtypeStruct((rows, lanes), F32),
        scratch_shapes=[pltpu.VMEM((N_DEV, rows, lanes), F32), pltpu.SemaphoreType.DMA((N_DEV - 1,)),
                        pltpu.SemaphoreType.DMA((N_DEV - 1,))],
        compiler_params=pltpu.CompilerParams(has_side_effects=True, vmem_limit_bytes=48 * MIB),
    )(vec)


def _adamw(name, w, g, m, v):
    rows, cols = w.shape
    tr = rows
    for cand in (256, 128, 64, 32, 16, 8):
        if rows % cand == 0:
            tr = cand
            break
    c1 = 1.0 - ADAM_B1 ** ADAM_STEP
    c2 = 1.0 - ADAM_B2 ** ADAM_STEP

    def body(w_ref, g_ref, m_ref, v_ref, d_ref, nm_ref, nv_ref):
        gv = g_ref[...]
        nm = ADAM_B1 * m_ref[...] + (1.0 - ADAM_B1) * gv
        nv = ADAM_B2 * v_ref[...] + (1.0 - ADAM_B2) * (gv * gv)
        nm_ref[...] = nm
        nv_ref[...] = nv
        d_ref[...] = -ADAM_LR * ((nm / c1) / (jnp.sqrt(nv / c2) + ADAM_EPS) + ADAM_WD * w_ref[...])

    blk = pl.BlockSpec((tr, cols), lambda i: (i, 0))
    out = jax.ShapeDtypeStruct((rows, cols), F32)
    return pl.pallas_call(
        body, name=name, grid=(rows // tr,), in_specs=[blk] * 4, out_specs=[blk] * 3, out_shape=[out] * 3,
        compiler_params=_params(("parallel",)),
    )(w, g, m, v)


def _pack_small(parts):
    flat = jnp.concatenate([parts[k].reshape(-1) for k in SMALL_NAMES])
    n = flat.shape[0]
    rows = -(-n // (256 * 128)) * 256
    return jnp.pad(flat, (0, rows * 128 - n)).reshape(rows, 128)


def _unpack_small(packed, like):
    flat = packed.reshape(-1)
    out, off = {}, 0
    for k in SMALL_NAMES:
        n = like[k].size
        out[k] = flat[off:off + n].reshape(like[k].shape)
        off += n
    return out


WEIGHT_ORDER = ("norm_g", "w_in", "sgu_ln_g", "sgu_ln_b", "sgu_w", "sgu_b", "mem_norm_g", "w_mem_kv", "q_norm_g",
                "k_norm_g", "w_out")


def kernel(x, mem, norm_g, w_in, sgu_ln_g, sgu_ln_b, sgu_w, sgu_b, mem_norm_g, w_mem_kv, q_norm_g, k_norm_g, w_out, loss_target, m_norm_g, m_w_in, m_sgu_ln_g, m_sgu_ln_b, m_sgu_w, m_sgu_b, m_mem_norm_g, m_w_mem_kv, m_q_norm_g, m_k_norm_g, m_w_out, v_norm_g, v_w_in, v_sgu_ln_g, v_sgu_ln_b, v_sgu_w, v_sgu_b, v_mem_norm_g, v_w_mem_kv, v_q_norm_g, v_k_norm_g, v_w_out):
    weights = dict(norm_g=norm_g, w_in=w_in, sgu_ln_g=sgu_ln_g, sgu_ln_b=sgu_ln_b, sgu_w=sgu_w, sgu_b=sgu_b,
                   mem_norm_g=mem_norm_g, w_mem_kv=w_mem_kv, q_norm_g=q_norm_g, k_norm_g=k_norm_g, w_out=w_out)
    mom_m = dict(norm_g=m_norm_g, w_in=m_w_in, sgu_ln_g=m_sgu_ln_g, sgu_ln_b=m_sgu_ln_b, sgu_w=m_sgu_w, sgu_b=m_sgu_b,
                 mem_norm_g=m_mem_norm_g, w_mem_kv=m_w_mem_kv, q_norm_g=m_q_norm_g, k_norm_g=m_k_norm_g, w_out=m_w_out)
    mom_v = dict(norm_g=v_norm_g, w_in=v_w_in, sgu_ln_g=v_sgu_ln_g, sgu_ln_b=v_sgu_ln_b, sgu_w=v_sgu_w, sgu_b=v_sgu_b,
                 mem_norm_g=v_mem_norm_g, w_mem_kv=v_w_mem_kv, q_norm_g=v_q_norm_g, k_norm_g=v_k_norm_g, w_out=v_w_out)
    big = ("w_in", "w_mem_kv", "w_out")
    sm = {k: weights[k] for k in SMALL_NAMES}

    w_in_all, w_kv_all, w_out_all = _all_gather_weights([weights[k].astype(BF16) for k in big])
    loss_part, grad_x, small_g, g_in, g_kv, g_out = _local_step(
        x[0], mem[0], loss_target[0], sm, w_in_all, w_kv_all, w_out_all)
    loss = lax.psum(loss_part[0, 0], ("x", "y", "c"))

    pairs = _core_exchange([g_in, g_kv, g_out])
    parts = [_add_to_bf16(f"grad_core_sum_{t}", mine, theirs) for t, (mine, theirs) in enumerate(pairs)]
    landed = _chip_exchange(parts)
    halves = [_sum_chips(f"grad_chip_sum_{t}", q) for t, q in enumerate(landed)]
    big_g = dict(zip(big, _core_share(halves)))

    small_sum = _unpack_small(_all_reduce_small(_pack_small(small_g)), sm)

    grads, delta, new_m, new_v = {}, {}, {}, {}
    for k in big:
        shape = weights[k].shape
        two_d = (shape[0] * shape[1], shape[2])
        grads[k] = big_g[k]
        d, nm, nv = _adamw(f"adamw_{k}", weights[k].reshape(two_d), big_g[k].reshape(two_d),
                           mom_m[k].reshape(two_d), mom_v[k].reshape(two_d))
        delta[k], new_m[k], new_v[k] = d.reshape(shape), nm.reshape(shape), nv.reshape(shape)
    d, nm, nv = _adamw("adamw_small", _pack_small(sm), _pack_small(small_sum),
                       _pack_small({k: mom_m[k] for k in SMALL_NAMES}), _pack_small({k: mom_v[k] for k in SMALL_NAMES}))
    grads.update(small_sum)
    delta.update(_unpack_small(d, sm))
    new_m.update(_unpack_small(nm, sm))
    new_v.update(_unpack_small(nv, sm))
    return (loss, grad_x[None], *[grads[k] for k in WEIGHT_ORDER], *[delta[k] for k in WEIGHT_ORDER],
            *[new_m[k] for k in WEIGHT_ORDER], *[new_v[k] for k in WEIGHT_ORDER])
```

```python
import functools
import math

import jax
import jax.numpy as jnp
from jax import lax
from jax.experimental import pallas as pl
from jax.experimental.pallas import tpu as pltpu

F32 = jnp.float32
BF16 = jnp.bfloat16
MESH = pl.DeviceIdType.MESH

D_MODEL = 2048
DEPTH = 2
CHUNK = 128
D_A = 1024
A_GROUPS = 8
D_B = 512
D_C = 512
HEADS = 4
HEAD_DIM = 128
IN_WIDTH = 6144
N_CHIPS = 4
EPS = 1e-6
ATT_SCALE = 1.0 / math.sqrt(HEAD_DIM)

OFF_U, OFF_V, OFF_ZA = 0, 1024, 2048
OFF_QB, OFF_KB, OFF_VB, OFF_ZB = 3072, 3584, 4096, 4608
OFF_QC, OFF_ZC = 5120, 5632
OFF_YB, OFF_YC = 1024, 1536

ADAM_LR = 0.001
ADAM_B1 = 0.9
ADAM_B2 = 0.999
ADAM_EPS = 1e-08
ADAM_WD = 0.01
ADAM_STEP = 10

MIB = 1024 * 1024
ANY = pl.BlockSpec(memory_space=pl.ANY)


def _params(semantics=None, vmem_mb=48):
    return pltpu.CompilerParams(dimension_semantics=semantics, vmem_limit_bytes=vmem_mb * MIB)


def _gelu(x):
    return 0.5 * x * (1.0 + lax.erf(x * (1.0 / math.sqrt(2.0))))


def _gelu_grad(x):
    cdf = 0.5 * (1.0 + lax.erf(x * (1.0 / math.sqrt(2.0))))
    pdf = jnp.exp(-0.5 * x * x) * (1.0 / math.sqrt(2.0 * math.pi))
    return cdf + x * pdf


def _sigmoid(x):
    return 1.0 / (1.0 + jnp.exp(-x))


def _silu_and_grad(z):
    s = _sigmoid(z)
    return z * s, s * (1.0 + z * (1.0 - s))


def _split_bf16(x):
    hi = x.astype(BF16)
    lo = (x - hi.astype(F32)).astype(BF16)
    return hi, lo


def _dot(a, b, dims):
    return lax.dot_general(a, b, (dims, ((), ())), preferred_element_type=F32)


NN = ((1,), (0,))
NT = ((1,), (1,))
TN = ((0,), (0,))


def _matmul(name, a, b, *, grid, a_spec, b_spec, o_spec, out_shape, dims, res=None, res_spec=None, alias=None,
            vmem_mb=48):
    nk = grid[2]
    n_in = 2 + (res is not None) + (alias is not None)

    def body(*refs):
        a_ref, b_ref = refs[0], refs[1]
        r_ref = refs[2] if res is not None else None
        o_ref = refs[n_in]
        part = _dot(a_ref[...], b_ref[...], dims)
        if nk == 1:
            if r_ref is not None:
                part = part + r_ref[...]
            o_ref[...] = part.astype(o_ref.dtype)
            return
        acc_ref = refs[n_in + 1]
        k = pl.program_id(2)

        @pl.when(k == 0)
        def _():
            acc_ref[...] = part

        @pl.when(k > 0)
        def _():
            acc_ref[...] += part

        @pl.when(k == nk - 1)
        def _():
            tot = acc_ref[...]
            if r_ref is not None:
                tot = tot + r_ref[...]
            o_ref[...] = tot.astype(o_ref.dtype)

    in_specs = [a_spec, b_spec]
    args = [a, b]
    if res is not None:
        in_specs.append(res_spec)
        args.append(res)
    aliases = {}
    if alias is not None:
        in_specs.append(ANY)
        args.append(alias)
        aliases = {len(args) - 1: 0}
    acc_shape = tuple(d for d in o_spec.block_shape if d is not None)
    scratch = [pltpu.VMEM(acc_shape, F32)] if nk > 1 else []
    return pl.pallas_call(
        body, name=name, grid=grid, in_specs=in_specs, out_specs=o_spec, out_shape=out_shape,
        scratch_shapes=scratch, input_output_aliases=aliases,
        compiler_params=_params(("parallel", "parallel", "arbitrary"), vmem_mb),
    )(*args)


def _rms_fwd(name, x, g, tr):
    rows, d = x.shape

    def body(x_ref, g_ref, h_ref):
        xv = x_ref[...]
        r = lax.rsqrt(jnp.mean(xv * xv, axis=-1, keepdims=True) + EPS)
        h_ref[...] = (xv * r * g_ref[...]).astype(BF16)

    return pl.pallas_call(
        body, name=name, grid=(rows // tr,),
        in_specs=[pl.BlockSpec((tr, d), lambda i: (i, 0)), pl.BlockSpec((1, d), lambda i: (0, 0))],
        out_specs=pl.BlockSpec((tr, d), lambda i: (i, 0)),
        out_shape=jax.ShapeDtypeStruct((rows, d), BF16),
        compiler_params=_params(("parallel",)),
    )(x, g)


def _rms_bwd(name, x, dh, dres, g, tr):
    rows, d = x.shape

    def body(x_ref, dh_ref, dres_ref, g_ref, dx_ref, dxb_ref, dg_ref):
        xv = x_ref[...]
        r = lax.rsqrt(jnp.mean(xv * xv, axis=-1, keepdims=True) + EPS)
        xhat = xv * r
        dhv = dh_ref[...]
        dxh = dhv * g_ref[...]
        dx = r * (dxh - xhat * jnp.mean(dxh * xhat, axis=-1, keepdims=True)) + dres_ref[...]
        dx_ref[...] = dx
        dxb_ref[...] = dx.astype(BF16)
        part = jnp.sum(dhv * xhat, axis=0, keepdims=True)

        @pl.when(pl.program_id(0) == 0)
        def _():
            dg_ref[...] = part

        @pl.when(pl.program_id(0) > 0)
        def _():
            dg_ref[...] += part

    blk = pl.BlockSpec((tr, d), lambda i: (i, 0))
    vec = pl.BlockSpec((1, d), lambda i: (0, 0))
    return pl.pallas_call(
        body, name=name, grid=(rows // tr,), in_specs=[blk, blk, blk, vec], out_specs=[blk, blk, vec],
        out_shape=[jax.ShapeDtypeStruct((rows, d), F32), jax.ShapeDtypeStruct((rows, d), BF16),
                   jax.ShapeDtypeStruct((1, d), F32)],
        compiler_params=_params(("arbitrary",)),
    )(x, dh, dres, g)


def _rms_gain_grad(name, x, dh):
    rows, d = x.shape

    def body(x_ref, dh_ref, dg_ref):
        xv = x_ref[...]
        r = lax.rsqrt(jnp.mean(xv * xv, axis=-1, keepdims=True) + EPS)
        dg_ref[...] = jnp.sum(dh_ref[...] * xv * r, axis=0, keepdims=True)

    return pl.pallas_call(
        body, name=name, out_shape=jax.ShapeDtypeStruct((1, d), F32), compiler_params=_params(None),
    )(x, dh)


def _loss_and_grad(name, y, target, tr):
    rows, d = y.shape
    n = rows // tr

    def body(y_ref, t_ref, dx_ref, dxb_ref, loss_ref, acc_ref):
        e = y_ref[...] - t_ref[...]
        dx = e * (1.0 / d)
        dx_ref[...] = dx
        dxb_ref[...] = dx.astype(BF16)
        part = jnp.sum(e * e, axis=0, keepdims=True)
        i = pl.program_id(0)

        @pl.when(i == 0)
        def _():
            acc_ref[...] = part

        @pl.when(i > 0)
        def _():
            acc_ref[...] += part

        @pl.when(i == n - 1)
        def _():
            loss_ref[...] = jnp.sum(acc_ref[...], axis=-1, keepdims=True) * (0.5 / d)

    blk = pl.BlockSpec((tr, d), lambda i: (i, 0))
    return pl.pallas_call(
        body, name=name, grid=(n,), in_specs=[blk, blk],
        out_specs=[blk, blk, pl.BlockSpec((1, 1), lambda i: (0, 0))],
        out_shape=[jax.ShapeDtypeStruct((rows, d), F32), jax.ShapeDtypeStruct((rows, d), BF16),
                   jax.ShapeDtypeStruct((1, 1), F32)],
        scratch_shapes=[pltpu.VMEM((1, d), F32)],
        compiler_params=_params(("arbitrary",)),
    )(y, target)


SB_T = 256


def _sb_scores(q, kblk):
    z = _dot(q, kblk, NT) * ATT_SCALE
    e = jnp.exp(-jnp.abs(z))
    sp = jnp.log1p(e)
    lb = jnp.minimum(z, 0.0) - sp
    l1 = lb - z
    return z, e, lb, l1


def _sb_fwd(name, proj):
    s_len = proj.shape[0]
    t = SB_T
    nq = s_len // t

    def body(q_ref, k_ref, v_ref, o_ref):
        i = pl.program_id(1)
        q = q_ref[...].astype(BF16)
        row = lax.broadcasted_iota(jnp.int32, (t, t), 0)
        col = lax.broadcasted_iota(jnp.int32, (t, t), 1)
        causal = col < row
        after_mat = (row > col).astype(BF16)

        def tile(kb, carry, acc, masked):
            start = pl.multiple_of(kb * t, t)
            kblk = k_ref[pl.ds(start, t), :].astype(BF16)
            vblk = v_ref[pl.ds(start, t), :].astype(BF16)
            _, _, lb, l1 = _sb_scores(q, kblk)
            if masked:
                l1 = jnp.where(causal, l1, 0.0)
            hi, lo = _split_bf16(l1)
            after = _dot(hi, after_mat, NN) + _dot(lo, after_mat, NN) + carry
            a = jnp.exp(lb + after)
            if masked:
                a = jnp.where(causal, a, 0.0)
            acc = acc + _dot(a.astype(BF16), vblk, NN)
            carry = carry + jnp.sum(l1, axis=-1, keepdims=True)
            return carry, acc

        carry, acc = tile(i, jnp.zeros((t, 1), F32), jnp.zeros((t, HEAD_DIM), F32), True)

        def step(n, state):
            return tile(i - 1 - n, state[0], state[1], False)

        carry, acc = lax.fori_loop(0, i, step, (carry, acc))
        o_ref[...] = acc

    cb = HEAD_DIM
    return pl.pallas_call(
        body, name=name, grid=(HEADS, nq),
        in_specs=[pl.BlockSpec((t, cb), lambda h, i: (i, OFF_QB // cb + h)),
                  pl.BlockSpec((s_len, cb), lambda h, i: (0, OFF_KB // cb + h)),
                  pl.BlockSpec((s_len, cb), lambda h, i: (0, OFF_VB // cb + h))],
        out_specs=pl.BlockSpec((t, cb), lambda h, i: (i, h)),
        out_shape=jax.ShapeDtypeStruct((s_len, D_B), F32),
        compiler_params=_params(("parallel", "arbitrary")),
    )(proj, proj, proj)


def _sb_bwd(name, proj, dy):
    s_len = proj.shape[0]
    t = SB_T
    nq = s_len // t

    def body(q_ref, k_ref, v_ref, z_ref, dy_ref, dq_ref, dk_ref, dv_ref, a_ref, s_ref):
        i = pl.program_id(1)

        @pl.when(i == 0)
        def _():
            dk_ref[...] = jnp.zeros_like(dk_ref)
            dv_ref[...] = jnp.zeros_like(dv_ref)

        q = q_ref[...].astype(BF16)
        silu_z, _ = _silu_and_grad(z_ref[...])
        do_b = (dy_ref[...] * silu_z).astype(BF16)
        row = lax.broadcasted_iota(jnp.int32, (t, t), 0)
        col = lax.broadcasted_iota(jnp.int32, (t, t), 1)
        causal = col < row
        after_mat = (row > col).astype(BF16)
        before_mat = (row < col).astype(BF16)

        def weights(kb, carry, masked):
            start = pl.multiple_of(kb * t, t)
            kblk = k_ref[pl.ds(start, t), :].astype(BF16)
            z, _, lb, l1 = _sb_scores(q, kblk)
            if masked:
                l1 = jnp.where(causal, l1, 0.0)
            hi, lo = _split_bf16(l1)
            after = _dot(hi, after_mat, NN) + _dot(lo, after_mat, NN) + carry
            a = jnp.exp(lb + after)
            if masked:
                a = jnp.where(causal, a, 0.0)
            a_ref[kb] = a
            s_ref[kb] = z
            return carry + jnp.sum(l1, axis=-1, keepdims=True)

        carry = weights(i, jnp.zeros((t, 1), F32), True)
        lax.fori_loop(0, i, lambda n, c: weights(i - 1 - n, c, False), carry)

        def grads(kb, carry, dq, masked):
            start = pl.multiple_of(kb * t, t)
            kblk = k_ref[pl.ds(start, t), :].astype(BF16)
            vblk = v_ref[pl.ds(start, t), :].astype(BF16)
            a = a_ref[kb]
            z = s_ref[kb]
            g = _dot(do_b, vblk, NT) * a
            ghi, glo = _split_bf16(g)
            prefix = _dot(ghi, before_mat, NN) + _dot(glo, before_mat, NN) + carry
            e = jnp.exp(-jnp.abs(z))
            inv = 1.0 / (1.0 + e)
            pos = z >= 0.0
            beta = jnp.where(pos, inv, e * inv)
            one_m_beta = jnp.where(pos, e * inv, inv)
            dz = (g * one_m_beta - prefix * beta) * ATT_SCALE
            if masked:
                dz = jnp.where(causal, dz, 0.0)
            dz_b = dz.astype(BF16)
            dq = dq + _dot(dz_b, kblk, NN)
            dk_ref[pl.ds(start, t), :] += _dot(dz_b, q, TN)
            dv_ref[pl.ds(start, t), :] += _dot(a.astype(BF16), do_b, TN)
            return carry + jnp.sum(g, axis=-1, keepdims=True), dq

        state = lax.fori_loop(0, i, lambda kb, st: grads(kb, st[0], st[1], False),
                              (jnp.zeros((t, 1), F32), jnp.zeros((t, HEAD_DIM), F32)))
        _, dq = grads(i, state[0], state[1], True)
        dq_ref[...] = dq

    cb = HEAD_DIM
    qblk = lambda off: pl.BlockSpec((t, cb), lambda h, i: (i, off // cb + h))
    full = lambda off: pl.BlockSpec((s_len, cb), lambda h, i: (0, off // cb + h))
    out = jax.ShapeDtypeStruct((s_len, D_B), F32)
    return pl.pallas_call(
        body, name=name, grid=(HEADS, nq),
        in_specs=[qblk(OFF_QB), full(OFF_KB), full(OFF_VB), qblk(OFF_ZB), qblk(OFF_YB)],
        out_specs=[qblk(0), full(0), full(0)],
        out_shape=[out, out, out],
        scratch_shapes=[pltpu.VMEM((nq, t, t), F32), pltpu.VMEM((nq, t, t), F32)],
        compiler_params=_params(("parallel", "arbitrary")),
    )(proj, proj, proj, proj, dy)


MEM_TQ = 512


def _qk_norm(x, g):
    r = lax.rsqrt(jnp.mean(x * x, axis=-1, keepdims=True) + EPS)
    xhat = x * r
    return xhat * g, xhat, r


def _qk_norm_bwd(dn, g, xhat, r):
    dxh = dn * g
    return r * (dxh - xhat * jnp.mean(dxh * xhat, axis=-1, keepdims=True))


def _mem_probs(q, mk, qg, kg):
    qn, qhat, rq = _qk_norm(q, qg)
    kn, khat, rk = _qk_norm(mk, kg)
    qn_b, kn_b = qn.astype(BF16), kn.astype(BF16)
    s = _dot(qn_b, kn_b, NT) * ATT_SCALE
    p = jnp.exp(s - jnp.max(s, axis=-1, keepdims=True))
    p = p / jnp.sum(p, axis=-1, keepdims=True)
    return p, qn_b, kn_b, qhat, rq, khat, rk


def _mem_fwd(name, proj, mem_kv, qg, kg):
    s_len = proj.shape[0]
    m_len = mem_kv.shape[0]
    tq = min(MEM_TQ, s_len)

    def body(q_ref, mk_ref, mv_ref, qg_ref, kg_ref, o_ref):
        p = _mem_probs(q_ref[...], mk_ref[...], qg_ref[...], kg_ref[...])[0]
        o_ref[...] = _dot(p.astype(BF16), mv_ref[...].astype(BF16), NN)

    cb = HEAD_DIM
    vec = pl.BlockSpec((1, cb), lambda h, i: (0, 0))
    return pl.pallas_call(
        body, name=name, grid=(HEADS, s_len // tq),
        in_specs=[pl.BlockSpec((tq, cb), lambda h, i: (i, OFF_QC // cb + h)),
                  pl.BlockSpec((m_len, cb), lambda h, i: (0, h)),
                  pl.BlockSpec((m_len, cb), lambda h, i: (0, HEADS + h)), vec, vec],
        out_specs=pl.BlockSpec((tq, cb), lambda h, i: (i, h)),
        out_shape=jax.ShapeDtypeStruct((s_len, D_C), F32),
        compiler_params=_params(("parallel", "parallel")),
    )(proj, mem_kv, mem_kv, qg, kg)


def _mem_bwd(name, proj, mem_kv, qg, kg, dy):
    s_len = proj.shape[0]
    m_len = mem_kv.shape[0]
    tq = min(MEM_TQ, s_len)

    def body(q_ref, mk_ref, mv_ref, qg_ref, kg_ref, z_ref, dy_ref, dq_ref, dmk_ref, dmv_ref, dqg_ref, dkg_ref):
        h, i = pl.program_id(0), pl.program_id(1)

        @pl.when(i == 0)
        def _():
            dmk_ref[...] = jnp.zeros_like(dmk_ref)
            dmv_ref[...] = jnp.zeros_like(dmv_ref)

        @pl.when((i == 0) & (h == 0))
        def _():
            dqg_ref[...] = jnp.zeros_like(dqg_ref)
            dkg_ref[...] = jnp.zeros_like(dkg_ref)

        qg, kg = qg_ref[...], kg_ref[...]
        p, qn_b, kn_b, qhat, rq, khat, rk = _mem_probs(q_ref[...], mk_ref[...], qg, kg)
        silu_z, _ = _silu_and_grad(z_ref[...])
        do_b = (dy_ref[...] * silu_z).astype(BF16)
        dmv_ref[...] += _dot(p.astype(BF16), do_b, TN)
        dp = _dot(do_b, mv_ref[...].astype(BF16), NT)
        ds = (p * (dp - jnp.sum(dp * p, axis=-1, keepdims=True)) * ATT_SCALE).astype(BF16)
        dqn = _dot(ds, kn_b, NN)
        dkn = _dot(ds, qn_b, TN)
        dq_ref[...] = _qk_norm_bwd(dqn, qg, qhat, rq)
        dmk_ref[...] += _qk_norm_bwd(dkn, kg, khat, rk)
        dqg_ref[...] += jnp.sum(dqn * qhat, axis=0, keepdims=True)
        dkg_ref[...] += jnp.sum(dkn * khat, axis=0, keepdims=True)

    cb = HEAD_DIM
    vec = pl.BlockSpec((1, cb), lambda h, i: (0, 0))
    qblk = lambda off: pl.BlockSpec((tq, cb), lambda h, i: (i, off // cb + h))
    memblk = lambda off: pl.BlockSpec((m_len, cb), lambda h, i: (0, off + h))
    return pl.pallas_call(
        body, name=name, grid=(HEADS, s_len // tq),
        in_specs=[qblk(OFF_QC), memblk(0), memblk(HEADS), vec, vec, qblk(OFF_ZC), qblk(OFF_YC)],
        out_specs=[qblk(0), memblk(0), memblk(0), vec, vec],
        out_shape=[jax.ShapeDtypeStruct((s_len, D_C), F32), jax.ShapeDtypeStruct((m_len, D_C), F32),
                   jax.ShapeDtypeStruct((m_len, D_C), F32), jax.ShapeDtypeStruct((1, cb), F32),
                   jax.ShapeDtypeStruct((1, cb), F32)],
        compiler_params=_params(("arbitrary", "arbitrary")),
    )(proj, mem_kv, mem_kv, qg, kg, proj, dy)


def _sgu_common(u_ref, v_ref, lng_ref, lnb_ref, w_ref, bias_ref):
    ug = _gelu(u_ref[...])
    vg = _gelu(v_ref[...])
    mu = jnp.mean(vg, axis=-1, keepdims=True)
    xc = vg - mu
    rstd = lax.rsqrt(jnp.mean(xc * xc, axis=-1, keepdims=True) + EPS)
    xhat = xc * rstd
    vn = xhat * lng_ref[...] + lnb_ref[...]
    vn_b = vn.astype(BF16)
    row = lax.broadcasted_iota(jnp.int32, (CHUNK, CHUNK), 0)
    col = lax.broadcasted_iota(jnp.int32, (CHUNK, CHUNK), 1)
    tril = row >= col
    mixed = []
    for g in range(A_GROUPS):
        w = jnp.where(tril, w_ref[g], 0.0).astype(BF16)
        sl = slice(g * CHUNK, (g + 1) * CHUNK)
        mixed.append(_dot(w, vn_b[:, sl], NN) + bias_ref[:, sl])
    return ug, xhat, rstd, vn_b, mixed, tril


def _gate_fwd(name, proj, o_b, o_c, lng, lnb, w_s, bias):
    s_len = proj.shape[0]

    def body(u_ref, v_ref, za_ref, zb_ref, zc_ref, ob_ref, oc_ref, lng_ref, lnb_ref, w_ref, bias_ref, y_ref):
        ug, _, _, _, mixed, _ = _sgu_common(u_ref, v_ref, lng_ref, lnb_ref, w_ref, bias_ref)
        sza, _ = _silu_and_grad(za_ref[...])
        gate = ug * sza
        for g in range(A_GROUPS):
            sl = slice(g * CHUNK, (g + 1) * CHUNK)
            y_ref[:, sl] = (gate[:, sl] * mixed[g]).astype(BF16)
        szb, _ = _silu_and_grad(zb_ref[...])
        y_ref[:, OFF_YB:OFF_YB + D_B] = (ob_ref[...] * szb).astype(BF16)
        szc, _ = _silu_and_grad(zc_ref[...])
        y_ref[:, OFF_YC:OFF_YC + D_C] = (oc_ref[...] * szc).astype(BF16)

    wide = lambda off: pl.BlockSpec((CHUNK, D_A), lambda i: (i, off // D_A))
    narrow = lambda off: pl.BlockSpec((CHUNK, D_B), lambda i: (i, off // D_B))
    vec = pl.BlockSpec((1, D_A), lambda i: (0, 0))
    return pl.pallas_call(
        body, name=name, grid=(s_len // CHUNK,),
        in_specs=[wide(OFF_U), wide(OFF_V), wide(OFF_ZA), narrow(OFF_ZB), narrow(OFF_ZC), narrow(0), narrow(0), vec, vec,
                  pl.BlockSpec((A_GROUPS, CHUNK, CHUNK), lambda i: (0, 0, 0)),
                  pl.BlockSpec((CHUNK, D_A), lambda i: (0, 0))],
        out_specs=pl.BlockSpec((CHUNK, D_MODEL), lambda i: (i, 0)),
        out_shape=jax.ShapeDtypeStruct((s_len, D_MODEL), BF16),
        compiler_params=_params(("parallel",)),
    )(proj, proj, proj, proj, proj, o_b, o_c, lng, lnb, w_s, bias)


def _gate_bwd(name, proj, dy, o_b, o_c, dqkv, dq_c, lng, lnb, w_s, w_s_t, bias):
    s_len = proj.shape[0]
    n = s_len // CHUNK
    dq_b, dk_b, dv_b = dqkv

    def body(u_ref, v_ref, za_ref, zb_ref, zc_ref, dya_ref, dyb_ref, dyc_ref, ob_ref, oc_ref, dq_ref, dk_ref, dv_ref,
             dqc_ref, lng_ref, lnb_ref, w_ref, wt_ref, bias_ref, dp_ref, dw_ref, dsb_ref, dlng_ref, dlnb_ref, dbias_ref):
        i = pl.program_id(0)

        @pl.when(i == 0)
        def _():
            dw_ref[...] = jnp.zeros_like(dw_ref)
            dbias_ref[...] = jnp.zeros_like(dbias_ref)
            dlng_ref[...] = jnp.zeros_like(dlng_ref)
            dlnb_ref[...] = jnp.zeros_like(dlnb_ref)

        ug, xhat, rstd, vn_b, mixed, tril = _sgu_common(u_ref, v_ref, lng_ref, lnb_ref, w_ref, bias_ref)
        za = za_ref[...]
        sza, dsza = _silu_and_grad(za)
        dya = dya_ref[...]
        mixed_all = jnp.concatenate(mixed, axis=-1)
        d_mixed = dya * ug * sza
        dp_ref[:, OFF_U:OFF_U + D_A] = (dya * mixed_all * sza * _gelu_grad(u_ref[...])).astype(BF16)
        dp_ref[:, OFF_ZA:OFF_ZA + D_A] = (dya * ug * mixed_all * dsza).astype(BF16)
        dbias_ref[...] += d_mixed
        dm_b = d_mixed.astype(BF16)
        triu = lax.broadcasted_iota(jnp.int32, (CHUNK, CHUNK), 0) <= lax.broadcasted_iota(jnp.int32, (CHUNK, CHUNK), 1)
        d_vn = []
        for g in range(A_GROUPS):
            sl = slice(g * CHUNK, (g + 1) * CHUNK)
            wt = jnp.where(triu, wt_ref[g], 0.0).astype(BF16)
            d_vn.append(_dot(wt, dm_b[:, sl], NN))
            dw_ref[g] += jnp.where(tril, _dot(dm_b[:, sl], vn_b[:, sl], NT), 0.0)
        d_vn = jnp.concatenate(d_vn, axis=-1)
        dlng_ref[...] += jnp.sum(d_vn * xhat, axis=0, keepdims=True)
        dlnb_ref[...] += jnp.sum(d_vn, axis=0, keepdims=True)
        dxh = d_vn * lng_ref[...]
        d_vg = rstd * (dxh - jnp.mean(dxh, axis=-1, keepdims=True)
                       - xhat * jnp.mean(dxh * xhat, axis=-1, keepdims=True))
        dp_ref[:, OFF_V:OFF_V + D_A] = (d_vg * _gelu_grad(v_ref[...])).astype(BF16)
        dp_ref[:, OFF_QB:OFF_QB + D_B] = dq_ref[...].astype(BF16)
        dp_ref[:, OFF_KB:OFF_KB + D_B] = dk_ref[...].astype(BF16)
        dp_ref[:, OFF_VB:OFF_VB + D_B] = dv_ref[...].astype(BF16)
        _, dszb = _silu_and_grad(zb_ref[...])
        dp_ref[:, OFF_ZB:OFF_ZB + D_B] = (dyb_ref[...] * ob_ref[...] * dszb).astype(BF16)
        dp_ref[:, OFF_QC:OFF_QC + D_C] = dqc_ref[...].astype(BF16)
        _, dszc = _silu_and_grad(zc_ref[...])
        dp_ref[:, OFF_ZC:OFF_ZC + D_C] = (dyc_ref[...] * oc_ref[...] * dszc).astype(BF16)

        @pl.when(i == n - 1)
        def _():
            ch = lax.broadcasted_iota(jnp.int32, (D_A, CHUNK), 0)
            gcol = lax.broadcasted_iota(jnp.int32, (D_A, CHUNK), 1)
            pick = (ch // (D_A // A_GROUPS) == gcol).astype(BF16)
            rest = dbias_ref[...]
            tot = jnp.zeros((CHUNK, CHUNK), F32)
            for _ in range(3):
                term = rest.astype(BF16)
                tot = tot + _dot(term, pick, NN)
                rest = rest - term.astype(F32)
            dsb_ref[...] = tot

    wide = lambda off: pl.BlockSpec((CHUNK, D_A), lambda i: (i, off // D_A))
    narrow = lambda off: pl.BlockSpec((CHUNK, D_B), lambda i: (i, off // D_B))
    vec = pl.BlockSpec((1, D_A), lambda i: (0, 0))
    wspec = pl.BlockSpec((A_GROUPS, CHUNK, CHUNK), lambda i: (0, 0, 0))
    bspec = pl.BlockSpec((CHUNK, D_A), lambda i: (0, 0))
    return pl.pallas_call(
        body, name=name, grid=(n,),
        in_specs=[wide(OFF_U), wide(OFF_V), wide(OFF_ZA), narrow(OFF_ZB), narrow(OFF_ZC),
                  wide(0), narrow(OFF_YB), narrow(OFF_YC), narrow(0), narrow(0), narrow(0), narrow(0), narrow(0),
                  narrow(0), vec, vec, wspec, wspec, bspec],
        out_specs=[pl.BlockSpec((CHUNK, IN_WIDTH), lambda i: (i, 0)), wspec,
                   pl.BlockSpec((CHUNK, CHUNK), lambda i: (0, 0)), vec, vec],
        out_shape=[jax.ShapeDtypeStruct((s_len, IN_WIDTH), BF16), jax.ShapeDtypeStruct((A_GROUPS, CHUNK, CHUNK), F32),
                   jax.ShapeDtypeStruct((CHUNK, CHUNK), F32), jax.ShapeDtypeStruct((1, D_A), F32),
                   jax.ShapeDtypeStruct((1, D_A), F32)],
        scratch_shapes=[pltpu.VMEM((CHUNK, D_A), F32)],
        compiler_params=_params(("arbitrary",)),
    )(proj, proj, proj, proj, proj, dy, dy, dy, o_b, o_c, dq_b, dk_b, dv_b, dq_c, lng, lnb, w_s, w_s_t, bias)


IN_SHARD = IN_WIDTH // N_CHIPS
ROW_SHARD = D_MODEL // N_CHIPS


def _bias_rows(sgu_b_l):
    return jnp.repeat(sgu_b_l.T, D_A // A_GROUPS, axis=1)


def _layer_fwd(l, x, mem, sm, w_in_all, w_kv_all, w_out_all):
    s_len = x.shape[0]
    m_len = mem.shape[0]
    tm = min(1024, s_len)
    tn = 768
    per = IN_SHARD // tn
    h = _rms_fwd(f"rms_fwd_{l}", x, sm["norm_g"][l][None], min(256, s_len))
    proj = _matmul(
        f"in_proj_{l}", h, w_in_all, grid=(s_len // tm, IN_WIDTH // tn, 1),
        a_spec=pl.BlockSpec((tm, D_MODEL), lambda i, j, k: (i, 0)),
        b_spec=pl.BlockSpec((None, None, D_MODEL, tn), lambda i, j, k: (l, j // per, 0, j % per)),
        o_spec=pl.BlockSpec((tm, tn), lambda i, j, k: (i, j)),
        out_shape=jax.ShapeDtypeStruct((s_len, IN_WIDTH), F32), dims=NN)
    mem_h = _rms_fwd(f"mem_rms_fwd_{l}", mem, sm["mem_norm_g"][l][None], m_len)
    mem_kv = _matmul(
        f"mem_kv_{l}", mem_h, w_kv_all, grid=(1, 2, N_CHIPS),
        a_spec=pl.BlockSpec((m_len, ROW_SHARD), lambda i, j, k: (0, k)),
        b_spec=pl.BlockSpec((None, None, ROW_SHARD, D_C), lambda i, j, k: (l, k, 0, j)),
        o_spec=pl.BlockSpec((m_len, D_C), lambda i, j, k: (0, j)),
        out_shape=jax.ShapeDtypeStruct((m_len, 2 * D_C), F32), dims=NN)
    o_b = _sb_fwd(f"sb_fwd_{l}", proj)
    qg, kg = sm["q_norm_g"][l][None], sm["k_norm_g"][l][None]
    o_c = _mem_fwd(f"mem_fwd_{l}", proj, mem_kv, qg, kg)
    bias = _bias_rows(sm["sgu_b"][l])
    y = _gate_fwd(f"gate_fwd_{l}", proj, o_b, o_c, sm["sgu_ln_g"][l][None], sm["sgu_ln_b"][l][None], sm["sgu_w"][l], bias)
    tn_o = 512
    x_next = _matmul(
        f"out_proj_{l}", y, w_out_all, grid=(s_len // tm, D_MODEL // tn_o, N_CHIPS),
        a_spec=pl.BlockSpec((tm, ROW_SHARD), lambda i, j, k: (i, k)),
        b_spec=pl.BlockSpec((None, None, ROW_SHARD, tn_o), lambda i, j, k: (l, k, 0, j)),
        o_spec=pl.BlockSpec((tm, tn_o), lambda i, j, k: (i, j)),
        out_shape=jax.ShapeDtypeStruct((s_len, D_MODEL), F32), dims=NN,
        res=x, res_spec=pl.BlockSpec((tm, tn_o), lambda i, j, k: (i, j)))
    saved = dict(x=x, h=h, proj=proj, mem_h=mem_h, mem_kv=mem_kv, o_b=o_b, o_c=o_c, y=y, bias=bias)
    return x_next, saved


def _layer_bwd(l, dxo, dxo_b, mem, sm, saved, w_in_all, w_kv_all, w_out_all, g_in, g_kv, g_out):
    s_len = dxo.shape[0]
    m_len = mem.shape[0]
    proj, y, h, mem_h, mem_kv = saved["proj"], saved["y"], saved["h"], saved["mem_h"], saved["mem_kv"]
    tm = min(1024, s_len)
    tk = min(1024, s_len)
    g_out = _matmul(
        f"d_w_out_{l}", y, dxo_b, grid=(N_CHIPS, D_MODEL // 1024, s_len // tk),
        a_spec=pl.BlockSpec((tk, ROW_SHARD), lambda i, j, k: (k, i)),
        b_spec=pl.BlockSpec((tk, 1024), lambda i, j, k: (k, j)),
        o_spec=pl.BlockSpec((None, None, ROW_SHARD, 1024), lambda i, j, k: (l, i, 0, j)),
        out_shape=jax.ShapeDtypeStruct((DEPTH, N_CHIPS, ROW_SHARD, D_MODEL), F32), dims=TN, alias=g_out)
    dy = _matmul(
        f"d_y_{l}", dxo_b, w_out_all, grid=(s_len // tm, N_CHIPS, 1),
        a_spec=pl.BlockSpec((tm, D_MODEL), lambda i, j, k: (i, 0)),
        b_spec=pl.BlockSpec((None, None, ROW_SHARD, D_MODEL), lambda i, j, k: (l, j, 0, 0)),
        o_spec=pl.BlockSpec((tm, ROW_SHARD), lambda i, j, k: (i, j)),
        out_shape=jax.ShapeDtypeStruct((s_len, D_MODEL), F32), dims=NT)
    qg, kg = sm["q_norm_g"][l][None], sm["k_norm_g"][l][None]
    dq_c, dmk, dmv, dqg, dkg = _mem_bwd(f"mem_bwd_{l}", proj, mem_kv, qg, kg, dy)
    dqkv = _sb_bwd(f"sb_bwd_{l}", proj, dy)
    w_s = sm["sgu_w"][l]
    dproj, dws, dbias, dlng, dlnb = _gate_bwd(
        f"gate_bwd_{l}", proj, dy, saved["o_b"], saved["o_c"], dqkv, dq_c, sm["sgu_ln_g"][l][None],
        sm["sgu_ln_b"][l][None], w_s, jnp.swapaxes(w_s, 1, 2), saved["bias"])
    tn = 768
    per = IN_SHARD // tn
    g_in = _matmul(
        f"d_w_in_{l}", h, dproj, grid=(D_MODEL // 1024, IN_WIDTH // tn, s_len // tk),
        a_spec=pl.BlockSpec((tk, 1024), lambda i, j, k: (k, i)),
        b_spec=pl.BlockSpec((tk, tn), lambda i, j, k: (k, j)),
        o_spec=pl.BlockSpec((None, None, 1024, tn), lambda i, j, k: (l, j // per, i, j % per)),
        out_shape=jax.ShapeDtypeStruct((DEPTH, N_CHIPS, D_MODEL, IN_SHARD), F32), dims=TN, alias=g_in)
    dh = _matmul(
        f"d_h_{l}", dproj, w_in_all, grid=(s_len // tm, D_MODEL // 1024, N_CHIPS),
        a_spec=pl.BlockSpec((tm, IN_SHARD), lambda i, j, k: (i, k)),
        b_spec=pl.BlockSpec((None, None, 1024, IN_SHARD), lambda i, j, k: (l, k, j, 0)),
        o_spec=pl.BlockSpec((tm, 1024), lambda i, j, k: (i, j)),
        out_shape=jax.ShapeDtypeStruct((s_len, D_MODEL), F32), dims=NT)
    dx, dx_b, dng = _rms_bwd(f"rms_bwd_{l}", saved["x"], dh, dxo, sm["norm_g"][l][None], min(256, s_len))
    dkv_b = jnp.concatenate([dmk, dmv], axis=1).astype(BF16)
    g_kv = _matmul(
        f"d_w_kv_{l}", mem_h, dkv_b, grid=(N_CHIPS, 1, 1),
        a_spec=pl.BlockSpec((m_len, ROW_SHARD), lambda i, j, k: (0, i)),
        b_spec=pl.BlockSpec((m_len, 2 * D_C), lambda i, j, k: (0, 0)),
        o_spec=pl.BlockSpec((None, None, ROW_SHARD, 2 * D_C), lambda i, j, k: (l, i, 0, 0)),
        out_shape=jax.ShapeDtypeStruct((DEPTH, N_CHIPS, ROW_SHARD, 2 * D_C), F32), dims=TN, alias=g_kv)
    d_mem_h = _matmul(
        f"d_mem_h_{l}", dkv_b, w_kv_all, grid=(1, N_CHIPS, 1),
        a_spec=pl.BlockSpec((m_len, 2 * D_C), lambda i, j, k: (0, 0)),
        b_spec=pl.BlockSpec((None, None, ROW_SHARD, 2 * D_C), lambda i, j, k: (l, j, 0, 0)),
        o_spec=pl.BlockSpec((m_len, ROW_SHARD), lambda i, j, k: (0, j)),
        out_shape=jax.ShapeDtypeStruct((m_len, D_MODEL), F32), dims=NT)
    dmng = _rms_gain_grad(f"mem_rms_bwd_{l}", mem, d_mem_h)
    dsgu_b = dbias[:, :A_GROUPS].T
    small = dict(norm_g=dng[0], sgu_ln_g=dlng[0], sgu_ln_b=dlnb[0], sgu_w=dws, sgu_b=dsgu_b, mem_norm_g=dmng[0],
                 q_norm_g=dqg[0], k_norm_g=dkg[0])
    return dx, dx_b, small, g_in, g_kv, g_out


SMALL_NAMES = ("norm_g", "sgu_ln_g", "sgu_ln_b", "sgu_w", "sgu_b", "mem_norm_g", "q_norm_g", "k_norm_g")


def _local_step(x, mem, target, sm, w_in_all, w_kv_all, w_out_all):
    saved = []
    cur = x
    for l in range(DEPTH):
        cur, sv = _layer_fwd(l, cur, mem, sm, w_in_all, w_kv_all, w_out_all)
        saved.append(sv)
    dxo, dxo_b, loss = _loss_and_grad("loss", cur, target, min(256, x.shape[0]))
    g_in = g_kv = g_out = None
    small = [None] * DEPTH
    for l in reversed(range(DEPTH)):
        dxo, dxo_b, small[l], g_in, g_kv, g_out = _layer_bwd(
            l, dxo, dxo_b, mem, sm, saved[l], w_in_all, w_kv_all, w_out_all, g_in, g_kv, g_out)
    small = {k: jnp.stack([small[l][k] for l in range(DEPTH)]) for k in SMALL_NAMES}
    return loss, dxo, small, g_in, g_kv, g_out


def _place():
    x, y, c = lax.axis_index("x"), lax.axis_index("y"), lax.axis_index("c")
    return x, y, c


def _other_chips(x, y):
    return [(1 - x, y, 2 * (1 - x) + y), (x, 1 - y, 2 * x + 1 - y), (1 - x, 1 - y, 2 * (1 - x) + 1 - y)]


AG_CHUNKS = 4
D2D_CHUNKS = 8


def _place_index():
    return jnp.stack([2 * lax.axis_index("x") + lax.axis_index("y"), lax.axis_index("c")]).astype(jnp.int32)


def _cast_into_slot(name, w, place):
    depth, rows, cols = w.shape
    tr = min(256, rows)

    def body(k_ref, w_ref, o_ref):
        o_ref[...] = w_ref[...].astype(BF16)

    return pl.pallas_call(
        body, name=name,
        grid_spec=pltpu.PrefetchScalarGridSpec(
            num_scalar_prefetch=1, grid=(depth, rows // tr),
            in_specs=[pl.BlockSpec((None, tr, cols), lambda l, i, k: (l, i, 0))],
            out_specs=pl.BlockSpec((None, None, tr, cols), lambda l, i, k: (l, k[0], i, 0))),
        out_shape=jax.ShapeDtypeStruct((depth, N_CHIPS, rows, cols), BF16),
        compiler_params=_params(("parallel", "parallel")),
    )(place, w)


def _all_gather_weights(bufs):
    n_t = len(bufs)
    n = 3 * n_t * AG_CHUNKS

    def body(*refs):
        own, buf = refs[:n_t], refs[n_t:2 * n_t]
        send_sems, recv_sems, fwd_send, fwd_recv = refs[2 * n_t:]
        x, y, c = _place()
        me = 2 * x + y
        chips = _other_chips(x, y)

        def piece(t, slot, core, q, ref=buf):
            hr = ref[t].shape[2] // 2
            cr = hr // AG_CHUNKS
            return ref[t].at[:, slot, pl.ds(core * hr + q * cr, cr)]

        sends, fwds = [], []
        for q in range(AG_CHUNKS):
            for t in range(n_t):
                for j, (px, py, pk) in enumerate(chips):
                    s = (t * 3 + j) * AG_CHUNKS + q
                    cp = pltpu.make_async_remote_copy(
                        src_ref=piece(t, me, c, q, own), dst_ref=piece(t, me, c, q), send_sem=send_sems.at[s],
                        recv_sem=recv_sems.at[s], device_id=(px, py, c), device_id_type=MESH)
                    cp.start()
                    sends.append(cp)
        for q in range(AG_CHUNKS):
            for t in range(n_t):
                for j, (px, py, pk) in enumerate(chips):
                    s = (t * 3 + j) * AG_CHUNKS + q
                    landed = piece(t, pk, c, q)
                    pltpu.make_async_remote_copy(
                        src_ref=landed, dst_ref=landed, send_sem=send_sems.at[s], recv_sem=recv_sems.at[s],
                        device_id=(px, py, c), device_id_type=MESH).wait_recv()
                    cp = pltpu.make_async_remote_copy(
                        src_ref=landed, dst_ref=landed, send_sem=fwd_send.at[s], recv_sem=fwd_recv.at[s],
                        device_id=(x, y, 1 - c), device_id_type=MESH)
                    cp.start()
                    fwds.append(cp)
        for q in range(AG_CHUNKS):
            for t in range(n_t):
                for j, (px, py, pk) in enumerate(chips):
                    s = (t * 3 + j) * AG_CHUNKS + q
                    other = piece(t, pk, 1 - c, q)
                    pltpu.make_async_remote_copy(
                        src_ref=other, dst_ref=other, send_sem=fwd_send.at[s], recv_sem=fwd_recv.at[s],
                        device_id=(x, y, 1 - c), device_id_type=MESH).wait_recv()
        for cp in sends + fwds:
            cp.wait_send()

    return pl.pallas_call(
        body, name="all_gather_weights",
        in_specs=[ANY] * n_t, out_specs=[ANY] * n_t,
        out_shape=[jax.ShapeDtypeStruct(b.shape, b.dtype) for b in bufs],
        input_output_aliases={t: t for t in range(n_t)},
        scratch_shapes=[pltpu.SemaphoreType.DMA((n,))] * 4,
        compiler_params=pltpu.CompilerParams(has_side_effects=True),
    )(*bufs)


def _core_exchange(grads):
    n_t = len(grads)
    n = n_t * D2D_CHUNKS

    def body(*refs):
        src, theirs = refs[:n_t], refs[n_t:2 * n_t]
        send_sems, recv_sems = refs[2 * n_t:]
        x, y, c = _place()
        copies = []
        for q in range(D2D_CHUNKS):
            for t in range(n_t):
                hr = src[t].shape[2] // 2
                cr = hr // D2D_CHUNKS
                s = t * D2D_CHUNKS + q
                cp = pltpu.make_async_remote_copy(
                    src_ref=src[t].at[:, :, pl.ds((1 - c) * hr + q * cr, cr)],
                    dst_ref=theirs[t].at[:, :, pl.ds(q * cr, cr)],
                    send_sem=send_sems.at[s], recv_sem=recv_sems.at[s], device_id=(x, y, 1 - c), device_id_type=MESH)
                cp.start()
                copies.append(cp)
        for cp in copies:
            cp.wait()

    half = [jax.ShapeDtypeStruct(g.shape[:2] + (g.shape[2] // 2, g.shape[3]), g.dtype) for g in grads]
    return pl.pallas_call(
        body, name="grad_core_exchange",
        in_specs=[ANY] * n_t, out_specs=[ANY] * n_t, out_shape=half,
        scratch_shapes=[pltpu.SemaphoreType.DMA((n,)), pltpu.SemaphoreType.DMA((n,))],
        compiler_params=pltpu.CompilerParams(has_side_effects=True),
    )(*grads)


def _add_to_bf16(name, full, theirs, place):
    depth, chips, rows, cols = theirs.shape
    tr = min(256, rows)
    per = rows // tr

    def body(p_ref, a_ref, b_ref, o_ref):
        o_ref[...] = (a_ref[...] + b_ref[...]).astype(BF16)

    blk = pl.BlockSpec((None, None, tr, cols), lambda l, k, i, p: (l, k, i, 0))
    return pl.pallas_call(
        body, name=name,
        grid_spec=pltpu.PrefetchScalarGridSpec(
            num_scalar_prefetch=1, grid=(depth, chips, per),
            in_specs=[pl.BlockSpec((None, None, tr, cols), lambda l, k, i, p: (l, k, p[1] * per + i, 0)), blk],
            out_specs=blk),
        out_shape=jax.ShapeDtypeStruct(theirs.shape, BF16), compiler_params=_params(("parallel",) * 3),
    )(place, full, theirs)


def _chip_exchange(parts):
    n_t = len(parts)

    def body(*refs):
        src, dst = refs[:n_t], refs[n_t:2 * n_t]
        send_sems, recv_sems = refs[2 * n_t:]
        x, y, c = _place()
        me = 2 * x + y
        chips = _other_chips(x, y)
        copies = []
        for t in range(n_t):
            for j, (px, py, pk) in enumerate(chips):
                s = t * 3 + j
                cp = pltpu.make_async_remote_copy(
                    src_ref=src[t].at[:, pk], dst_ref=dst[t].at[me], send_sem=send_sems.at[s], recv_sem=recv_sems.at[s],
                    device_id=(px, py, c), device_id_type=MESH)
                cp.start()
                copies.append(cp)
        for t in range(n_t):
            for j, (px, py, pk) in enumerate(chips):
                s = t * 3 + j
                landed = dst[t].at[pk]
                pltpu.make_async_remote_copy(
                    src_ref=landed, dst_ref=landed, send_sem=send_sems.at[s], recv_sem=recv_sems.at[s],
                    device_id=(px, py, c), device_id_type=MESH).wait_recv()
        for cp in copies:
            cp.wait_send()

    n = 3 * n_t
    return pl.pallas_call(
        body, name="grad_chip_exchange",
        in_specs=[ANY] * n_t, out_specs=[ANY] * n_t,
        out_shape=[jax.ShapeDtypeStruct((N_CHIPS, p.shape[0]) + p.shape[2:], p.dtype) for p in parts],
        scratch_shapes=[pltpu.SemaphoreType.DMA((n,)), pltpu.SemaphoreType.DMA((n,))],
        compiler_params=pltpu.CompilerParams(has_side_effects=True),
    )(*parts)


def _sum_chips(name, parts, landed, place):
    chips, depth, rows, cols = landed.shape
    tr = min(256, rows)
    per = rows // tr

    def body(p_ref, own_ref, *refs):
        land, o_ref = refs[:chips], refs[chips]
        tot = None
        for k in range(chips):
            term = jnp.where(p_ref[0] == k, own_ref[...], land[k][...]).astype(F32)
            tot = term if tot is None else tot + term
        o_ref[...] = tot

    def from_chip(k):
        return pl.BlockSpec((None, None, tr, cols),
                            lambda l, i, p: (jnp.where(p[0] == k, (k + 1) % chips, k), l, i, 0))

    return pl.pallas_call(
        body, name=name,
        grid_spec=pltpu.PrefetchScalarGridSpec(
            num_scalar_prefetch=1, grid=(depth, per),
            in_specs=[pl.BlockSpec((None, None, tr, cols), lambda l, i, p: (l, p[0], i, 0))]
            + [from_chip(k) for k in range(chips)],
            out_specs=pl.BlockSpec((None, tr, cols), lambda l, i, p: (l, p[1] * per + i, 0))),
        out_shape=jax.ShapeDtypeStruct((depth, 2 * rows, cols), F32), compiler_params=_params(("parallel",) * 2),
    )(place, parts, *([landed] * chips))


def _core_share(bufs):
    n_t = len(bufs)
    n = n_t * D2D_CHUNKS

    def body(*refs):
        mine, buf = refs[:n_t], refs[n_t:2 * n_t]
        send_sems, recv_sems = refs[2 * n_t:]
        x, y, c = _place()

        def piece(ref, t, core, q):
            hr = ref[t].shape[1] // 2
            cr = hr // D2D_CHUNKS
            return ref[t].at[:, pl.ds(core * hr + q * cr, cr)]

        copies = []
        for q in range(D2D_CHUNKS):
            for t in range(n_t):
                s = t * D2D_CHUNKS + q
                cp = pltpu.make_async_remote_copy(
                    src_ref=piece(mine, t, c, q), dst_ref=piece(buf, t, c, q), send_sem=send_sems.at[s],
                    recv_sem=recv_sems.at[s], device_id=(x, y, 1 - c), device_id_type=MESH)
                cp.start()
                copies.append(cp)
        for q in range(D2D_CHUNKS):
            for t in range(n_t):
                s = t * D2D_CHUNKS + q
                theirs = piece(buf, t, 1 - c, q)
                pltpu.make_async_remote_copy(
                    src_ref=theirs, dst_ref=theirs, send_sem=send_sems.at[s], recv_sem=recv_sems.at[s],
                    device_id=(x, y, 1 - c), device_id_type=MESH).wait_recv()
        for cp in copies:
            cp.wait_send()

    return pl.pallas_call(
        body, name="grad_core_share",
        in_specs=[ANY] * n_t, out_specs=[ANY] * n_t,
        out_shape=[jax.ShapeDtypeStruct(b.shape, b.dtype) for b in bufs],
        input_output_aliases={t: t for t in range(n_t)},
        scratch_shapes=[pltpu.SemaphoreType.DMA((n,)), pltpu.SemaphoreType.DMA((n,))],
        compiler_params=pltpu.CompilerParams(has_side_effects=True),
    )(*bufs)


N_DEV = 8


def _all_reduce_small(vec):
    rows, lanes = vec.shape

    def body(v_ref, o_ref, gath_ref, send_sems, recv_sems):
        x, y, c = _place()
        me = 4 * x + 2 * y + c
        gath_ref[me] = v_ref[...]
        peers = [(x, y, 1 - c)]
        for px, py, _ in _other_chips(x, y):
            peers += [(px, py, c), (px, py, 1 - c)]
        copies = []
        for j, peer in enumerate(peers):
            cp = pltpu.make_async_remote_copy(
                src_ref=v_ref, dst_ref=gath_ref.at[me], send_sem=send_sems.at[j], recv_sem=recv_sems.at[j],
                device_id=peer, device_id_type=MESH)
            cp.start()
            copies.append(cp)
        for j, (px, py, pc) in enumerate(peers):
            slot = gath_ref.at[4 * px + 2 * py + pc]
            pltpu.make_async_remote_copy(
                src_ref=v_ref, dst_ref=slot, send_sem=send_sems.at[j], recv_sem=recv_sems.at[j],
                device_id=(px, py, pc), device_id_type=MESH).wait_recv()
        tot = gath_ref[0]
        for k in range(1, N_DEV):
            tot = tot + gath_ref[k]
        o_ref[...] = tot
        for cp in copies:
            cp.wait_send()

    vm = pl.BlockSpec(memory_space=pltpu.VMEM)
    return pl.pallas_call(
        body, name="small_all_reduce", in_specs=[vm], out_specs=vm,
        out_shape=jax.ShapeDtypeStruct((rows, lanes), F32),
        scratch_shapes=[pltpu.VMEM((N_DEV, rows, lanes), F32), pltpu.SemaphoreType.DMA((N_DEV - 1,)),
                        pltpu.SemaphoreType.DMA((N_DEV - 1,))],
        compiler_params=pltpu.CompilerParams(has_side_effects=True, vmem_limit_bytes=48 * MIB),
    )(vec)


def _adamw(name, w, g, m, v):
    rows, cols = w.shape
    tr = rows
    for cand in (256, 128, 64, 32, 16, 8):
        if rows % cand == 0:
            tr = cand
            break
    c1 = 1.0 - ADAM_B1 ** ADAM_STEP
    c2 = 1.0 - ADAM_B2 ** ADAM_STEP

    def body(w_ref, g_ref, m_ref, v_ref, d_ref, nm_ref, nv_ref):
        gv = g_ref[...]
        nm = ADAM_B1 * m_ref[...] + (1.0 - ADAM_B1) * gv
        nv = ADAM_B2 * v_ref[...] + (1.0 - ADAM_B2) * (gv * gv)
        nm_ref[...] = nm
        nv_ref[...] = nv
        d_ref[...] = -ADAM_LR * ((nm / c1) / (jnp.sqrt(nv / c2) + ADAM_EPS) + ADAM_WD * w_ref[...])

    blk = pl.BlockSpec((tr, cols), lambda i: (i, 0))
    out = jax.ShapeDtypeStruct((rows, cols), F32)
    return pl.pallas_call(
        body, name=name, grid=(rows // tr,), in_specs=[blk] * 4, out_specs=[blk] * 3, out_shape=[out] * 3,
        compiler_params=_params(("parallel",)),
    )(w, g, m, v)


def _pack_small(parts):
    flat = jnp.concatenate([parts[k].reshape(-1) for k in SMALL_NAMES])
    n = flat.shape[0]
    rows = -(-n // (256 * 128)) * 256
    return jnp.pad(flat, (0, rows * 128 - n)).reshape(rows, 128)


def _unpack_small(packed, like):
    flat = packed.reshape(-1)
    out, off = {}, 0
    for k in SMALL_NAMES:
        n = like[k].size
        out[k] = flat[off:off + n].reshape(like[k].shape)
        off += n
    return out


WEIGHT_ORDER = ("norm_g", "w_in", "sgu_ln_g", "sgu_ln_b", "sgu_w", "sgu_b", "mem_norm_g", "w_mem_kv", "q_norm_g",
                "k_norm_g", "w_out")


def kernel(x, mem, norm_g, w_in, sgu_ln_g, sgu_ln_b, sgu_w, sgu_b, mem_norm_g, w_mem_kv, q_norm_g, k_norm_g, w_out, loss_target, m_norm_g, m_w_in, m_sgu_ln_g, m_sgu_ln_b, m_sgu_w, m_sgu_b, m_mem_norm_g, m_w_mem_kv, m_q_norm_g, m_k_norm_g, m_w_out, v_norm_g, v_w_in, v_sgu_ln_g, v_sgu_ln_b, v_sgu_w, v_sgu_b, v_mem_norm_g, v_w_mem_kv, v_q_norm_g, v_k_norm_g, v_w_out):
    weights = dict(norm_g=norm_g, w_in=w_in, sgu_ln_g=sgu_ln_g, sgu_ln_b=sgu_ln_b, sgu_w=sgu_w, sgu_b=sgu_b,
                   mem_norm_g=mem_norm_g, w_mem_kv=w_mem_kv, q_norm_g=q_norm_g, k_norm_g=k_norm_g, w_out=w_out)
    mom_m = dict(norm_g=m_norm_g, w_in=m_w_in, sgu_ln_g=m_sgu_ln_g, sgu_ln_b=m_sgu_ln_b, sgu_w=m_sgu_w, sgu_b=m_sgu_b,
                 mem_norm_g=m_mem_norm_g, w_mem_kv=m_w_mem_kv, q_norm_g=m_q_norm_g, k_norm_g=m_k_norm_g, w_out=m_w_out)
    mom_v = dict(norm_g=v_norm_g, w_in=v_w_in, sgu_ln_g=v_sgu_ln_g, sgu_ln_b=v_sgu_ln_b, sgu_w=v_sgu_w, sgu_b=v_sgu_b,
                 mem_norm_g=v_mem_norm_g, w_mem_kv=v_w_mem_kv, q_norm_g=v_q_norm_g, k_norm_g=v_k_norm_g, w_out=v_w_out)
    big = ("w_in", "w_mem_kv", "w_out")
    sm = {k: weights[k] for k in SMALL_NAMES}

    place = _place_index()
    w_in_all, w_kv_all, w_out_all = _all_gather_weights(
        [_cast_into_slot(f"cast_{k}", weights[k], place) for k in big])
    loss_part, grad_x, small_g, g_in, g_kv, g_out = _local_step(
        x[0], mem[0], loss_target[0], sm, w_in_all, w_kv_all, w_out_all)
    loss = lax.psum(loss_part[0, 0], ("x", "y", "c"))

    full = [g_in, g_kv, g_out]
    theirs = _core_exchange(full)
    parts = [_add_to_bf16(f"grad_core_sum_{t}", full[t], theirs[t], place) for t in range(3)]
    landed = _chip_exchange(parts)
    halves = [_sum_chips(f"grad_chip_sum_{t}", parts[t], landed[t], place) for t in range(3)]
    big_g = dict(zip(big, _core_share(halves)))

    small_sum = _unpack_small(_all_reduce_small(_pack_small(small_g)), sm)

    grads, delta, new_m, new_v = {}, {}, {}, {}
    for k in big:
        shape = weights[k].shape
        two_d = (shape[0] * shape[1], shape[2])
        grads[k] = big_g[k]
        d, nm, nv = _adamw(f"adamw_{k}", weights[k].reshape(two_d), big_g[k].reshape(two_d),
                           mom_m[k].reshape(two_d), mom_v[k].reshape(two_d))
        delta[k], new_m[k], new_v[k] = d.reshape(shape), nm.reshape(shape), nv.reshape(shape)
    d, nm, nv = _adamw("adamw_small", _pack_small(sm), _pack_small(small_sum),
                       _pack_small({k: mom_m[k] for k in SMALL_NAMES}), _pack_small({k: mom_v[k] for k in SMALL_NAMES}))
    grads.update(small_sum)
    delta.update(_unpack_small(d, sm))
    new_m.update(_unpack_small(nm, sm))
    new_v.update(_unpack_small(nv, sm))
    return (loss, grad_x[None], *[grads[k] for k in WEIGHT_ORDER], *[delta[k] for k in WEIGHT_ORDER],
            *[new_m[k] for k in WEIGHT_ORDER], *[new_v[k] for k in WEIGHT_ORDER])
```

```python
import functools
import math

import jax
import jax.numpy as jnp
from jax import lax
from jax.experimental import pallas as pl
from jax.experimental.pallas import tpu as pltpu

F32 = jnp.float32
BF16 = jnp.bfloat16
MESH = pl.DeviceIdType.MESH

D_MODEL = 2048
DEPTH = 2
CHUNK = 128
D_A = 1024
A_GROUPS = 8
D_B = 512
D_C = 512
HEADS = 4
HEAD_DIM = 128
IN_WIDTH = 6144
N_CHIPS = 4
EPS = 1e-6
ATT_SCALE = 1.0 / math.sqrt(HEAD_DIM)

OFF_U, OFF_V, OFF_ZA = 0, 1024, 2048
OFF_QB, OFF_KB, OFF_VB, OFF_ZB = 3072, 3584, 4096, 4608
OFF_QC, OFF_ZC = 5120, 5632
OFF_YB, OFF_YC = 1024, 1536

ADAM_LR = 0.001
ADAM_B1 = 0.9
ADAM_B2 = 0.999
ADAM_EPS = 1e-08
ADAM_WD = 0.01
ADAM_STEP = 10

MIB = 1024 * 1024
ANY = pl.BlockSpec(memory_space=pl.ANY)


def _params(semantics=None, vmem_mb=48):
    return pltpu.CompilerParams(dimension_semantics=semantics, vmem_limit_bytes=vmem_mb * MIB)


def _gelu(x):
    return 0.5 * x * (1.0 + lax.erf(x * (1.0 / math.sqrt(2.0))))


def _gelu_grad(x):
    cdf = 0.5 * (1.0 + lax.erf(x * (1.0 / math.sqrt(2.0))))
    pdf = jnp.exp(-0.5 * x * x) * (1.0 / math.sqrt(2.0 * math.pi))
    return cdf + x * pdf


def _sigmoid(x):
    return 1.0 / (1.0 + jnp.exp(-x))


def _silu_and_grad(z):
    s = _sigmoid(z)
    return z * s, s * (1.0 + z * (1.0 - s))


def _split_bf16(x):
    hi = x.astype(BF16)
    lo = (x - hi.astype(F32)).astype(BF16)
    return hi, lo


def _dot(a, b, dims):
    return lax.dot_general(a, b, (dims, ((), ())), preferred_element_type=F32)


NN = ((1,), (0,))
NT = ((1,), (1,))
TN = ((0,), (0,))


def _matmul(name, a, b, *, grid, a_spec, b_spec, o_spec, out_shape, dims, res=None, res_spec=None, after=None,
            vmem_mb=48):
    nk = grid[2]
    n_in = 2 + (res is not None) + (after is not None)

    def body(*refs):
        a_ref, b_ref = refs[0], refs[1]
        r_ref = refs[2] if res is not None else None
        o_ref = refs[n_in]
        part = _dot(a_ref[...], b_ref[...], dims)
        if nk == 1:
            if r_ref is not None:
                part = part + r_ref[...]
            o_ref[...] = part.astype(o_ref.dtype)
            return
        acc_ref = refs[n_in + 1]
        k = pl.program_id(2)

        @pl.when(k == 0)
        def _():
            acc_ref[...] = part

        @pl.when(k > 0)
        def _():
            acc_ref[...] += part

        @pl.when(k == nk - 1)
        def _():
            tot = acc_ref[...]
            if r_ref is not None:
                tot = tot + r_ref[...]
            o_ref[...] = tot.astype(o_ref.dtype)

    in_specs = [a_spec, b_spec]
    args = [a, b]
    if res is not None:
        in_specs.append(res_spec)
        args.append(res)
    if after is not None:
        in_specs.append(ANY)
        args.append(after)
    acc_shape = tuple(d for d in o_spec.block_shape if d is not None)
    scratch = [pltpu.VMEM(acc_shape, F32)] if nk > 1 else []
    return pl.pallas_call(
        body, name=name, grid=grid, in_specs=in_specs, out_specs=o_spec, out_shape=out_shape,
        scratch_shapes=scratch,
        compiler_params=_params(("parallel", "parallel", "arbitrary"), vmem_mb),
    )(*args)


def _rms_fwd(name, x, g, tr, after=None):
    rows, d = x.shape

    def body(x_ref, g_ref, *refs):
        h_ref = refs[-1]
        xv = x_ref[...]
        r = lax.rsqrt(jnp.mean(xv * xv, axis=-1, keepdims=True) + EPS)
        h_ref[...] = (xv * r * g_ref[...]).astype(BF16)

    return pl.pallas_call(
        body, name=name, grid=(rows // tr,),
        in_specs=[pl.BlockSpec((tr, d), lambda i: (i, 0)), pl.BlockSpec((1, d), lambda i: (0, 0))]
        + ([] if after is None else [ANY]),
        out_specs=pl.BlockSpec((tr, d), lambda i: (i, 0)),
        out_shape=jax.ShapeDtypeStruct((rows, d), BF16),
        compiler_params=_params(("parallel",)),
    )(x, g, *([] if after is None else [after]))


def _rms_bwd(name, x, dh, dres, g, tr):
    rows, d = x.shape

    def body(x_ref, dh_ref, dres_ref, g_ref, dx_ref, dxb_ref, dg_ref):
        xv = x_ref[...]
        r = lax.rsqrt(jnp.mean(xv * xv, axis=-1, keepdims=True) + EPS)
        xhat = xv * r
        dhv = dh_ref[...]
        dxh = dhv * g_ref[...]
        dx = r * (dxh - xhat * jnp.mean(dxh * xhat, axis=-1, keepdims=True)) + dres_ref[...]
        dx_ref[...] = dx
        dxb_ref[...] = dx.astype(BF16)
        part = jnp.sum(dhv * xhat, axis=0, keepdims=True)

        @pl.when(pl.program_id(0) == 0)
        def _():
            dg_ref[...] = part

        @pl.when(pl.program_id(0) > 0)
        def _():
            dg_ref[...] += part

    blk = pl.BlockSpec((tr, d), lambda i: (i, 0))
    vec = pl.BlockSpec((1, d), lambda i: (0, 0))
    return pl.pallas_call(
        body, name=name, grid=(rows // tr,), in_specs=[blk, blk, blk, vec], out_specs=[blk, blk, vec],
        out_shape=[jax.ShapeDtypeStruct((rows, d), F32), jax.ShapeDtypeStruct((rows, d), BF16),
                   jax.ShapeDtypeStruct((1, d), F32)],
        compiler_params=_params(("arbitrary",)),
    )(x, dh, dres, g)


def _rms_gain_grad(name, x, dh):
    rows, d = x.shape

    def body(x_ref, dh_ref, dg_ref):
        xv = x_ref[...]
        r = lax.rsqrt(jnp.mean(xv * xv, axis=-1, keepdims=True) + EPS)
        dg_ref[...] = jnp.sum(dh_ref[...] * xv * r, axis=0, keepdims=True)

    return pl.pallas_call(
        body, name=name, out_shape=jax.ShapeDtypeStruct((1, d), F32), compiler_params=_params(None),
    )(x, dh)


def _loss_and_grad(name, y, target, tr):
    rows, d = y.shape
    n = rows // tr

    def body(y_ref, t_ref, dx_ref, dxb_ref, loss_ref, acc_ref):
        e = y_ref[...] - t_ref[...]
        dx = e * (1.0 / d)
        dx_ref[...] = dx
        dxb_ref[...] = dx.astype(BF16)
        part = jnp.sum(e * e, axis=0, keepdims=True)
        i = pl.program_id(0)

        @pl.when(i == 0)
        def _():
            acc_ref[...] = part

        @pl.when(i > 0)
        def _():
            acc_ref[...] += part

        @pl.when(i == n - 1)
        def _():
            loss_ref[...] = jnp.sum(acc_ref[...], axis=-1, keepdims=True) * (0.5 / d)

    blk = pl.BlockSpec((tr, d), lambda i: (i, 0))
    return pl.pallas_call(
        body, name=name, grid=(n,), in_specs=[blk, blk],
        out_specs=[blk, blk, pl.BlockSpec((1, 1), lambda i: (0, 0))],
        out_shape=[jax.ShapeDtypeStruct((rows, d), F32), jax.ShapeDtypeStruct((rows, d), BF16),
                   jax.ShapeDtypeStruct((1, 1), F32)],
        scratch_shapes=[pltpu.VMEM((1, d), F32)],
        compiler_params=_params(("arbitrary",)),
    )(y, target)


SB_T = 256


def _sb_scores(q, kblk):
    z = _dot(q, kblk, NT) * ATT_SCALE
    e = jnp.exp(-jnp.abs(z))
    sp = jnp.log1p(e)
    lb = jnp.minimum(z, 0.0) - sp
    l1 = lb - z
    return z, e, lb, l1


def _sb_fwd(name, proj):
    s_len = proj.shape[0]
    t = SB_T
    nq = s_len // t

    def body(q_ref, k_ref, v_ref, o_ref):
        i = pl.program_id(1)
        q = q_ref[...].astype(BF16)
        row = lax.broadcasted_iota(jnp.int32, (t, t), 0)
        col = lax.broadcasted_iota(jnp.int32, (t, t), 1)
        causal = col < row
        after_mat = (row > col).astype(BF16)

        def tile(kb, carry, acc, masked):
            start = pl.multiple_of(kb * t, t)
            kblk = k_ref[pl.ds(start, t), :].astype(BF16)
            vblk = v_ref[pl.ds(start, t), :].astype(BF16)
            _, _, lb, l1 = _sb_scores(q, kblk)
            if masked:
                l1 = jnp.where(causal, l1, 0.0)
            hi, lo = _split_bf16(l1)
            after = _dot(hi, after_mat, NN) + _dot(lo, after_mat, NN) + carry
            a = jnp.exp(lb + after)
            if masked:
                a = jnp.where(causal, a, 0.0)
            acc = acc + _dot(a.astype(BF16), vblk, NN)
            carry = carry + jnp.sum(l1, axis=-1, keepdims=True)
            return carry, acc

        carry, acc = tile(i, jnp.zeros((t, 1), F32), jnp.zeros((t, HEAD_DIM), F32), True)

        def step(n, state):
            return tile(i - 1 - n, state[0], state[1], False)

        carry, acc = lax.fori_loop(0, i, step, (carry, acc))
        o_ref[...] = acc

    cb = HEAD_DIM
    return pl.pallas_call(
        body, name=name, grid=(HEADS, nq),
        in_specs=[pl.BlockSpec((t, cb), lambda h, i: (i, OFF_QB // cb + h)),
                  pl.BlockSpec((s_len, cb), lambda h, i: (0, OFF_KB // cb + h)),
                  pl.BlockSpec((s_len, cb), lambda h, i: (0, OFF_VB // cb + h))],
        out_specs=pl.BlockSpec((t, cb), lambda h, i: (i, h)),
        out_shape=jax.ShapeDtypeStruct((s_len, D_B), F32),
        compiler_params=_params(("parallel", "arbitrary")),
    )(proj, proj, proj)


def _sb_bwd(name, proj, dy):
    s_len = proj.shape[0]
    t = SB_T
    nq = s_len // t

    def body(q_ref, k_ref, v_ref, z_ref, dy_ref, dq_ref, dk_ref, dv_ref, a_ref, s_ref):
        i = pl.program_id(1)

        @pl.when(i == 0)
        def _():
            dk_ref[...] = jnp.zeros_like(dk_ref)
            dv_ref[...] = jnp.zeros_like(dv_ref)

        q = q_ref[...].astype(BF16)
        silu_z, _ = _silu_and_grad(z_ref[...])
        do_b = (dy_ref[...] * silu_z).astype(BF16)
        row = lax.broadcasted_iota(jnp.int32, (t, t), 0)
        col = lax.broadcasted_iota(jnp.int32, (t, t), 1)
        causal = col < row
        after_mat = (row > col).astype(BF16)
        before_mat = (row < col).astype(BF16)

        def weights(kb, carry, masked):
            start = pl.multiple_of(kb * t, t)
            kblk = k_ref[pl.ds(start, t), :].astype(BF16)
            z, _, lb, l1 = _sb_scores(q, kblk)
            if masked:
                l1 = jnp.where(causal, l1, 0.0)
            hi, lo = _split_bf16(l1)
            after = _dot(hi, after_mat, NN) + _dot(lo, after_mat, NN) + carry
            a = jnp.exp(lb + after)
            if masked:
                a = jnp.where(causal, a, 0.0)
            a_ref[kb] = a
            s_ref[kb] = z
            return carry + jnp.sum(l1, axis=-1, keepdims=True)

        carry = weights(i, jnp.zeros((t, 1), F32), True)
        lax.fori_loop(0, i, lambda n, c: weights(i - 1 - n, c, False), carry)

        def grads(kb, carry, dq, masked):
            start = pl.multiple_of(kb * t, t)
            kblk = k_ref[pl.ds(start, t), :].astype(BF16)
            vblk = v_ref[pl.ds(start, t), :].astype(BF16)
            a = a_ref[kb]
            z = s_ref[kb]
            g = _dot(do_b, vblk, NT) * a
            ghi, glo = _split_bf16(g)
            prefix = _dot(ghi, before_mat, NN) + _dot(glo, before_mat, NN) + carry
            e = jnp.exp(-jnp.abs(z))
            inv = 1.0 / (1.0 + e)
            pos = z >= 0.0
            beta = jnp.where(pos, inv, e * inv)
            one_m_beta = jnp.where(pos, e * inv, inv)
            dz = (g * one_m_beta - prefix * beta) * ATT_SCALE
            if masked:
                dz = jnp.where(causal, dz, 0.0)
            dz_b = dz.astype(BF16)
            dq = dq + _dot(dz_b, kblk, NN)
            dk_ref[pl.ds(start, t), :] += _dot(dz_b, q, TN)
            dv_ref[pl.ds(start, t), :] += _dot(a.astype(BF16), do_b, TN)
            return carry + jnp.sum(g, axis=-1, keepdims=True), dq

        state = lax.fori_loop(0, i, lambda kb, st: grads(kb, st[0], st[1], False),
                              (jnp.zeros((t, 1), F32), jnp.zeros((t, HEAD_DIM), F32)))
        _, dq = grads(i, state[0], state[1], True)
        dq_ref[...] = dq

    cb = HEAD_DIM
    qblk = lambda off: pl.BlockSpec((t, cb), lambda h, i: (i, off // cb + h))
    full = lambda off: pl.BlockSpec((s_len, cb), lambda h, i: (0, off // cb + h))
    out = jax.ShapeDtypeStruct((s_len, D_B), F32)
    return pl.pallas_call(
        body, name=name, grid=(HEADS, nq),
        in_specs=[qblk(OFF_QB), full(OFF_KB), full(OFF_VB), qblk(OFF_ZB), qblk(OFF_YB)],
        out_specs=[qblk(0), full(0), full(0)],
        out_shape=[out, out, out],
        scratch_shapes=[pltpu.VMEM((nq, t, t), F32), pltpu.VMEM((nq, t, t), F32)],
        compiler_params=_params(("parallel", "arbitrary")),
    )(proj, proj, proj, proj, dy)


MEM_TQ = 512


def _qk_norm(x, g):
    r = lax.rsqrt(jnp.mean(x * x, axis=-1, keepdims=True) + EPS)
    xhat = x * r
    return xhat * g, xhat, r


def _qk_norm_bwd(dn, g, xhat, r):
    dxh = dn * g
    return r * (dxh - xhat * jnp.mean(dxh * xhat, axis=-1, keepdims=True))


def _mem_probs(q, mk, qg, kg):
    qn, qhat, rq = _qk_norm(q, qg)
    kn, khat, rk = _qk_norm(mk, kg)
    qn_b, kn_b = qn.astype(BF16), kn.astype(BF16)
    s = _dot(qn_b, kn_b, NT) * ATT_SCALE
    p = jnp.exp(s - jnp.max(s, axis=-1, keepdims=True))
    p = p / jnp.sum(p, axis=-1, keepdims=True)
    return p, qn_b, kn_b, qhat, rq, khat, rk


def _mem_fwd(name, proj, mem_kv, qg, kg):
    s_len = proj.shape[0]
    m_len = mem_kv.shape[0]
    tq = min(MEM_TQ, s_len)

    def body(q_ref, mk_ref, mv_ref, qg_ref, kg_ref, o_ref):
        p = _mem_probs(q_ref[...], mk_ref[...], qg_ref[...], kg_ref[...])[0]
        o_ref[...] = _dot(p.astype(BF16), mv_ref[...].astype(BF16), NN)

    cb = HEAD_DIM
    vec = pl.BlockSpec((1, cb), lambda h, i: (0, 0))
    return pl.pallas_call(
        body, name=name, grid=(HEADS, s_len // tq),
        in_specs=[pl.BlockSpec((tq, cb), lambda h, i: (i, OFF_QC // cb + h)),
                  pl.BlockSpec((m_len, cb), lambda h, i: (0, h)),
                  pl.BlockSpec((m_len, cb), lambda h, i: (0, HEADS + h)), vec, vec],
        out_specs=pl.BlockSpec((tq, cb), lambda h, i: (i, h)),
        out_shape=jax.ShapeDtypeStruct((s_len, D_C), F32),
        compiler_params=_params(("parallel", "parallel")),
    )(proj, mem_kv, mem_kv, qg, kg)


def _mem_bwd(name, proj, mem_kv, qg, kg, dy):
    s_len = proj.shape[0]
    m_len = mem_kv.shape[0]
    tq = min(MEM_TQ, s_len)

    def body(q_ref, mk_ref, mv_ref, qg_ref, kg_ref, z_ref, dy_ref, dq_ref, dmk_ref, dmv_ref, dqg_ref, dkg_ref):
        h, i = pl.program_id(0), pl.program_id(1)

        @pl.when(i == 0)
        def _():
            dmk_ref[...] = jnp.zeros_like(dmk_ref)
            dmv_ref[...] = jnp.zeros_like(dmv_ref)

        @pl.when((i == 0) & (h == 0))
        def _():
            dqg_ref[...] = jnp.zeros_like(dqg_ref)
            dkg_ref[...] = jnp.zeros_like(dkg_ref)

        qg, kg = qg_ref[...], kg_ref[...]
        p, qn_b, kn_b, qhat, rq, khat, rk = _mem_probs(q_ref[...], mk_ref[...], qg, kg)
        silu_z, _ = _silu_and_grad(z_ref[...])
        do_b = (dy_ref[...] * silu_z).astype(BF16)
        dmv_ref[...] += _dot(p.astype(BF16), do_b, TN)
        dp = _dot(do_b, mv_ref[...].astype(BF16), NT)
        ds = (p * (dp - jnp.sum(dp * p, axis=-1, keepdims=True)) * ATT_SCALE).astype(BF16)
        dqn = _dot(ds, kn_b, NN)
        dkn = _dot(ds, qn_b, TN)
        dq_ref[...] = _qk_norm_bwd(dqn, qg, qhat, rq)
        dmk_ref[...] += _qk_norm_bwd(dkn, kg, khat, rk)
        dqg_ref[...] += jnp.sum(dqn * qhat, axis=0, keepdims=True)
        dkg_ref[...] += jnp.sum(dkn * khat, axis=0, keepdims=True)

    cb = HEAD_DIM
    vec = pl.BlockSpec((1, cb), lambda h, i: (0, 0))
    qblk = lambda off: pl.BlockSpec((tq, cb), lambda h, i: (i, off // cb + h))
    memblk = lambda off: pl.BlockSpec((m_len, cb), lambda h, i: (0, off + h))
    return pl.pallas_call(
        body, name=name, grid=(HEADS, s_len // tq),
        in_specs=[qblk(OFF_QC), memblk(0), memblk(HEADS), vec, vec, qblk(OFF_ZC), qblk(OFF_YC)],
        out_specs=[qblk(0), memblk(0), memblk(0), vec, vec],
        out_shape=[jax.ShapeDtypeStruct((s_len, D_C), F32), jax.ShapeDtypeStruct((m_len, D_C), F32),
                   jax.ShapeDtypeStruct((m_len, D_C), F32), jax.ShapeDtypeStruct((1, cb), F32),
                   jax.ShapeDtypeStruct((1, cb), F32)],
        compiler_params=_params(("arbitrary", "arbitrary")),
    )(proj, mem_kv, mem_kv, qg, kg, proj, dy)


def _sgu_common(u_ref, v_ref, lng_ref, lnb_ref, w_ref, bias_ref):
    ug = _gelu(u_ref[...])
    vg = _gelu(v_ref[...])
    mu = jnp.mean(vg, axis=-1, keepdims=True)
    xc = vg - mu
    rstd = lax.rsqrt(jnp.mean(xc * xc, axis=-1, keepdims=True) + EPS)
    xhat = xc * rstd
    vn = xhat * lng_ref[...] + lnb_ref[...]
    vn_b = vn.astype(BF16)
    row = lax.broadcasted_iota(jnp.int32, (CHUNK, CHUNK), 0)
    col = lax.broadcasted_iota(jnp.int32, (CHUNK, CHUNK), 1)
    tril = row >= col
    mixed = []
    for g in range(A_GROUPS):
        w = jnp.where(tril, w_ref[g], 0.0).astype(BF16)
        sl = slice(g * CHUNK, (g + 1) * CHUNK)
        mixed.append(_dot(w, vn_b[:, sl], NN) + bias_ref[:, sl])
    return ug, xhat, rstd, vn_b, mixed, tril


def _gate_fwd(name, proj, o_b, o_c, lng, lnb, w_s, bias):
    s_len = proj.shape[0]

    def body(u_ref, v_ref, za_ref, zb_ref, zc_ref, ob_ref, oc_ref, lng_ref, lnb_ref, w_ref, bias_ref, y_ref):
        ug, _, _, _, mixed, _ = _sgu_common(u_ref, v_ref, lng_ref, lnb_ref, w_ref, bias_ref)
        sza, _ = _silu_and_grad(za_ref[...])
        gate = ug * sza
        for g in range(A_GROUPS):
            sl = slice(g * CHUNK, (g + 1) * CHUNK)
            y_ref[:, sl] = (gate[:, sl] * mixed[g]).astype(BF16)
        szb, _ = _silu_and_grad(zb_ref[...])
        y_ref[:, OFF_YB:OFF_YB + D_B] = (ob_ref[...] * szb).astype(BF16)
        szc, _ = _silu_and_grad(zc_ref[...])
        y_ref[:, OFF_YC:OFF_YC + D_C] = (oc_ref[...] * szc).astype(BF16)

    wide = lambda off: pl.BlockSpec((CHUNK, D_A), lambda i: (i, off // D_A))
    narrow = lambda off: pl.BlockSpec((CHUNK, D_B), lambda i: (i, off // D_B))
    vec = pl.BlockSpec((1, D_A), lambda i: (0, 0))
    return pl.pallas_call(
        body, name=name, grid=(s_len // CHUNK,),
        in_specs=[wide(OFF_U), wide(OFF_V), wide(OFF_ZA), narrow(OFF_ZB), narrow(OFF_ZC), narrow(0), narrow(0), vec, vec,
                  pl.BlockSpec((A_GROUPS, CHUNK, CHUNK), lambda i: (0, 0, 0)),
                  pl.BlockSpec((CHUNK, D_A), lambda i: (0, 0))],
        out_specs=pl.BlockSpec((CHUNK, D_MODEL), lambda i: (i, 0)),
        out_shape=jax.ShapeDtypeStruct((s_len, D_MODEL), BF16),
        compiler_params=_params(("parallel",)),
    )(proj, proj, proj, proj, proj, o_b, o_c, lng, lnb, w_s, bias)


def _gate_bwd(name, proj, dy, o_b, o_c, dqkv, dq_c, lng, lnb, w_s, w_s_t, bias):
    s_len = proj.shape[0]
    n = s_len // CHUNK
    dq_b, dk_b, dv_b = dqkv

    def body(u_ref, v_ref, za_ref, zb_ref, zc_ref, dya_ref, dyb_ref, dyc_ref, ob_ref, oc_ref, dq_ref, dk_ref, dv_ref,
             dqc_ref, lng_ref, lnb_ref, w_ref, wt_ref, bias_ref, dp_ref, dw_ref, dsb_ref, dlng_ref, dlnb_ref, dbias_ref):
        i = pl.program_id(0)

        @pl.when(i == 0)
        def _():
            dw_ref[...] = jnp.zeros_like(dw_ref)
            dbias_ref[...] = jnp.zeros_like(dbias_ref)
            dlng_ref[...] = jnp.zeros_like(dlng_ref)
            dlnb_ref[...] = jnp.zeros_like(dlnb_ref)

        ug, xhat, rstd, vn_b, mixed, tril = _sgu_common(u_ref, v_ref, lng_ref, lnb_ref, w_ref, bias_ref)
        za = za_ref[...]
        sza, dsza = _silu_and_grad(za)
        dya = dya_ref[...]
        mixed_all = jnp.concatenate(mixed, axis=-1)
        d_mixed = dya * ug * sza
        dp_ref[:, OFF_U:OFF_U + D_A] = (dya * mixed_all * sza * _gelu_grad(u_ref[...])).astype(BF16)
        dp_ref[:, OFF_ZA:OFF_ZA + D_A] = (dya * ug * mixed_all * dsza).astype(BF16)
        dbias_ref[...] += d_mixed
        dm_b = d_mixed.astype(BF16)
        triu = lax.broadcasted_iota(jnp.int32, (CHUNK, CHUNK), 0) <= lax.broadcasted_iota(jnp.int32, (CHUNK, CHUNK), 1)
        d_vn = []
        for g in range(A_GROUPS):
            sl = slice(g * CHUNK, (g + 1) * CHUNK)
            wt = jnp.where(triu, wt_ref[g], 0.0).astype(BF16)
            d_vn.append(_dot(wt, dm_b[:, sl], NN))
            dw_ref[g] += jnp.where(tril, _dot(dm_b[:, sl], vn_b[:, sl], NT), 0.0)
        d_vn = jnp.concatenate(d_vn, axis=-1)
        dlng_ref[...] += jnp.sum(d_vn * xhat, axis=0, keepdims=True)
        dlnb_ref[...] += jnp.sum(d_vn, axis=0, keepdims=True)
        dxh = d_vn * lng_ref[...]
        d_vg = rstd * (dxh - jnp.mean(dxh, axis=-1, keepdims=True)
                       - xhat * jnp.mean(dxh * xhat, axis=-1, keepdims=True))
        dp_ref[:, OFF_V:OFF_V + D_A] = (d_vg * _gelu_grad(v_ref[...])).astype(BF16)
        dp_ref[:, OFF_QB:OFF_QB + D_B] = dq_ref[...].astype(BF16)
        dp_ref[:, OFF_KB:OFF_KB + D_B] = dk_ref[...].astype(BF16)
        dp_ref[:, OFF_VB:OFF_VB + D_B] = dv_ref[...].astype(BF16)
        _, dszb = _silu_and_grad(zb_ref[...])
        dp_ref[:, OFF_ZB:OFF_ZB + D_B] = (dyb_ref[...] * ob_ref[...] * dszb).astype(BF16)
        dp_ref[:, OFF_QC:OFF_QC + D_C] = dqc_ref[...].astype(BF16)
        _, dszc = _silu_and_grad(zc_ref[...])
        dp_ref[:, OFF_ZC:OFF_ZC + D_C] = (dyc_ref[...] * oc_ref[...] * dszc).astype(BF16)

        @pl.when(i == n - 1)
        def _():
            ch = lax.broadcasted_iota(jnp.int32, (D_A, CHUNK), 0)
            gcol = lax.broadcasted_iota(jnp.int32, (D_A, CHUNK), 1)
            pick = (ch // (D_A // A_GROUPS) == gcol).astype(BF16)
            rest = dbias_ref[...]
            tot = jnp.zeros((CHUNK, CHUNK), F32)
            for _ in range(3):
                term = rest.astype(BF16)
                tot = tot + _dot(term, pick, NN)
                rest = rest - term.astype(F32)
            dsb_ref[...] = tot

    wide = lambda off: pl.BlockSpec((CHUNK, D_A), lambda i: (i, off // D_A))
    narrow = lambda off: pl.BlockSpec((CHUNK, D_B), lambda i: (i, off // D_B))
    vec = pl.BlockSpec((1, D_A), lambda i: (0, 0))
    wspec = pl.BlockSpec((A_GROUPS, CHUNK, CHUNK), lambda i: (0, 0, 0))
    bspec = pl.BlockSpec((CHUNK, D_A), lambda i: (0, 0))
    return pl.pallas_call(
        body, name=name, grid=(n,),
        in_specs=[wide(OFF_U), wide(OFF_V), wide(OFF_ZA), narrow(OFF_ZB), narrow(OFF_ZC),
                  wide(0), narrow(OFF_YB), narrow(OFF_YC), narrow(0), narrow(0), narrow(0), narrow(0), narrow(0),
                  narrow(0), vec, vec, wspec, wspec, bspec],
        out_specs=[pl.BlockSpec((CHUNK, IN_WIDTH), lambda i: (i, 0)), wspec,
                   pl.BlockSpec((CHUNK, CHUNK), lambda i: (0, 0)), vec, vec],
        out_shape=[jax.ShapeDtypeStruct((s_len, IN_WIDTH), BF16), jax.ShapeDtypeStruct((A_GROUPS, CHUNK, CHUNK), F32),
                   jax.ShapeDtypeStruct((CHUNK, CHUNK), F32), jax.ShapeDtypeStruct((1, D_A), F32),
                   jax.ShapeDtypeStruct((1, D_A), F32)],
        scratch_shapes=[pltpu.VMEM((CHUNK, D_A), F32)],
        compiler_params=_params(("arbitrary",)),
    )(proj, proj, proj, proj, proj, dy, dy, dy, o_b, o_c, dq_b, dk_b, dv_b, dq_c, lng, lnb, w_s, w_s_t, bias)


IN_SHARD = IN_WIDTH // N_CHIPS
ROW_SHARD = D_MODEL // N_CHIPS


def _bias_rows(sgu_b_l):
    return jnp.repeat(sgu_b_l.T, D_A // A_GROUPS, axis=1)


def _layer_fwd(l, x, mem, sm, w_in_all, w_kv_all, w_out_all, after=None):
    s_len = x.shape[0]
    m_len = mem.shape[0]
    tm = min(1024, s_len)
    tn = 768
    per = IN_SHARD // tn
    h = _rms_fwd(f"rms_fwd_{l}", x, sm["norm_g"][l][None], min(256, s_len), after)
    proj = _matmul(
        f"in_proj_{l}", h, w_in_all, grid=(s_len // tm, IN_WIDTH // tn, 1),
        a_spec=pl.BlockSpec((tm, D_MODEL), lambda i, j, k: (i, 0)),
        b_spec=pl.BlockSpec((None, D_MODEL, tn), lambda i, j, k: (j // per, 0, j % per)),
        o_spec=pl.BlockSpec((tm, tn), lambda i, j, k: (i, j)),
        out_shape=jax.ShapeDtypeStruct((s_len, IN_WIDTH), F32), dims=NN)
    mem_h = _rms_fwd(f"mem_rms_fwd_{l}", mem, sm["mem_norm_g"][l][None], m_len)
    mem_kv = _matmul(
        f"mem_kv_{l}", mem_h, w_kv_all, grid=(1, 2, N_CHIPS),
        a_spec=pl.BlockSpec((m_len, ROW_SHARD), lambda i, j, k: (0, k)),
        b_spec=pl.BlockSpec((None, ROW_SHARD, D_C), lambda i, j, k: (k, 0, j)),
        o_spec=pl.BlockSpec((m_len, D_C), lambda i, j, k: (0, j)),
        out_shape=jax.ShapeDtypeStruct((m_len, 2 * D_C), F32), dims=NN)
    o_b = _sb_fwd(f"sb_fwd_{l}", proj)
    qg, kg = sm["q_norm_g"][l][None], sm["k_norm_g"][l][None]
    o_c = _mem_fwd(f"mem_fwd_{l}", proj, mem_kv, qg, kg)
    bias = _bias_rows(sm["sgu_b"][l])
    y = _gate_fwd(f"gate_fwd_{l}", proj, o_b, o_c, sm["sgu_ln_g"][l][None], sm["sgu_ln_b"][l][None], sm["sgu_w"][l], bias)
    tn_o = 512
    x_next = _matmul(
        f"out_proj_{l}", y, w_out_all, grid=(s_len // tm, D_MODEL // tn_o, N_CHIPS),
        a_spec=pl.BlockSpec((tm, ROW_SHARD), lambda i, j, k: (i, k)),
        b_spec=pl.BlockSpec((None, ROW_SHARD, tn_o), lambda i, j, k: (k, 0, j)),
        o_spec=pl.BlockSpec((tm, tn_o), lambda i, j, k: (i, j)),
        out_shape=jax.ShapeDtypeStruct((s_len, D_MODEL), F32), dims=NN,
        res=x, res_spec=pl.BlockSpec((tm, tn_o), lambda i, j, k: (i, j)))
    saved = dict(x=x, h=h, proj=proj, mem_h=mem_h, mem_kv=mem_kv, o_b=o_b, o_c=o_c, y=y, bias=bias)
    return x_next, saved


def _layer_bwd(l, dxo, dxo_b, mem, sm, saved, w_in_all, w_kv_all, w_out_all, after=None):
    s_len = dxo.shape[0]
    m_len = mem.shape[0]
    proj, y, h, mem_h, mem_kv = saved["proj"], saved["y"], saved["h"], saved["mem_h"], saved["mem_kv"]
    tm = min(1024, s_len)
    tk = min(1024, s_len)
    g_out = _matmul(
        f"d_w_out_{l}", y, dxo_b, grid=(N_CHIPS, D_MODEL // 1024, s_len // tk),
        a_spec=pl.BlockSpec((tk, ROW_SHARD), lambda i, j, k: (k, i)),
        b_spec=pl.BlockSpec((tk, 1024), lambda i, j, k: (k, j)),
        o_spec=pl.BlockSpec((None, ROW_SHARD, 1024), lambda i, j, k: (i, 0, j)),
        out_shape=jax.ShapeDtypeStruct((N_CHIPS, ROW_SHARD, D_MODEL), F32), dims=TN, after=after)
    dy = _matmul(
        f"d_y_{l}", dxo_b, w_out_all, grid=(s_len // tm, N_CHIPS, 1),
        a_spec=pl.BlockSpec((tm, D_MODEL), lambda i, j, k: (i, 0)),
        b_spec=pl.BlockSpec((None, ROW_SHARD, D_MODEL), lambda i, j, k: (j, 0, 0)),
        o_spec=pl.BlockSpec((tm, ROW_SHARD), lambda i, j, k: (i, j)),
        out_shape=jax.ShapeDtypeStruct((s_len, D_MODEL), F32), dims=NT)
    qg, kg = sm["q_norm_g"][l][None], sm["k_norm_g"][l][None]
    dq_c, dmk, dmv, dqg, dkg = _mem_bwd(f"mem_bwd_{l}", proj, mem_kv, qg, kg, dy)
    dqkv = _sb_bwd(f"sb_bwd_{l}", proj, dy)
    w_s = sm["sgu_w"][l]
    dproj, dws, dbias, dlng, dlnb = _gate_bwd(
        f"gate_bwd_{l}", proj, dy, saved["o_b"], saved["o_c"], dqkv, dq_c, sm["sgu_ln_g"][l][None],
        sm["sgu_ln_b"][l][None], w_s, jnp.swapaxes(w_s, 1, 2), saved["bias"])
    tn = 768
    per = IN_SHARD // tn
    g_in = _matmul(
        f"d_w_in_{l}", h, dproj, grid=(D_MODEL // 1024, IN_WIDTH // tn, s_len // tk),
        a_spec=pl.BlockSpec((tk, 1024), lambda i, j, k: (k, i)),
        b_spec=pl.BlockSpec((tk, tn), lambda i, j, k: (k, j)),
        o_spec=pl.BlockSpec((None, 1024, tn), lambda i, j, k: (j // per, i, j % per)),
        out_shape=jax.ShapeDtypeStruct((N_CHIPS, D_MODEL, IN_SHARD), F32), dims=TN)
    dh = _matmul(
        f"d_h_{l}", dproj, w_in_all, grid=(s_len // tm, D_MODEL // 1024, N_CHIPS),
        a_spec=pl.BlockSpec((tm, IN_SHARD), lambda i, j, k: (i, k)),
        b_spec=pl.BlockSpec((None, 1024, IN_SHARD), lambda i, j, k: (k, j, 0)),
        o_spec=pl.BlockSpec((tm, 1024), lambda i, j, k: (i, j)),
        out_shape=jax.ShapeDtypeStruct((s_len, D_MODEL), F32), dims=NT)
    dx, dx_b, dng = _rms_bwd(f"rms_bwd_{l}", saved["x"], dh, dxo, sm["norm_g"][l][None], min(256, s_len))
    dkv_b = jnp.concatenate([dmk, dmv], axis=1).astype(BF16)
    g_kv = _matmul(
        f"d_w_kv_{l}", mem_h, dkv_b, grid=(N_CHIPS, 1, 1),
        a_spec=pl.BlockSpec((m_len, ROW_SHARD), lambda i, j, k: (0, i)),
        b_spec=pl.BlockSpec((m_len, 2 * D_C), lambda i, j, k: (0, 0)),
        o_spec=pl.BlockSpec((None, ROW_SHARD, 2 * D_C), lambda i, j, k: (i, 0, 0)),
        out_shape=jax.ShapeDtypeStruct((N_CHIPS, ROW_SHARD, 2 * D_C), F32), dims=TN)
    d_mem_h = _matmul(
        f"d_mem_h_{l}", dkv_b, w_kv_all, grid=(1, N_CHIPS, 1),
        a_spec=pl.BlockSpec((m_len, 2 * D_C), lambda i, j, k: (0, 0)),
        b_spec=pl.BlockSpec((None, ROW_SHARD, 2 * D_C), lambda i, j, k: (j, 0, 0)),
        o_spec=pl.BlockSpec((m_len, ROW_SHARD), lambda i, j, k: (0, j)),
        out_shape=jax.ShapeDtypeStruct((m_len, D_MODEL), F32), dims=NT)
    dmng = _rms_gain_grad(f"mem_rms_bwd_{l}", mem, d_mem_h)
    dsgu_b = dbias[:, :A_GROUPS].T
    small = dict(norm_g=dng[0], sgu_ln_g=dlng[0], sgu_ln_b=dlnb[0], sgu_w=dws, sgu_b=dsgu_b, mem_norm_g=dmng[0],
                 q_norm_g=dqg[0], k_norm_g=dkg[0])
    return dx, dx_b, small, g_in, g_kv, g_out


SMALL_NAMES = ("norm_g", "sgu_ln_g", "sgu_ln_b", "sgu_w", "sgu_b", "mem_norm_g", "q_norm_g", "k_norm_g")


def _local_step(x, mem, target, sm, w_all):
    saved = []
    cur = x
    for l in range(DEPTH):
        cur, sv = _layer_fwd(l, cur, mem, sm, *w_all[l])
        saved.append(sv)
    dxo, dxo_b, loss = _loss_and_grad("loss", cur, target, min(256, x.shape[0]))
    small, big = [None] * DEPTH, [None] * DEPTH
    for l in reversed(range(DEPTH)):
        dxo, dxo_b, small[l], *big[l] = _layer_bwd(l, dxo, dxo_b, mem, sm, saved[l], *w_all[l])
    small = {k: jnp.stack([small[l][k] for l in range(DEPTH)]) for k in SMALL_NAMES}
    return loss, dxo, small, big


def _place():
    x, y, c = lax.axis_index("x"), lax.axis_index("y"), lax.axis_index("c")
    return x, y, c


def _other_chips(x, y):
    return [(1 - x, y, 2 * (1 - x) + y), (x, 1 - y, 2 * x + 1 - y), (1 - x, 1 - y, 2 * (1 - x) + 1 - y)]


AG_CHUNKS = 4
D2D_CHUNKS = 8


def _place_index():
    return jnp.stack([2 * lax.axis_index("x") + lax.axis_index("y"), lax.axis_index("c")]).astype(jnp.int32)


def _cast_into_slot(name, w, l, place):
    _, rows, cols = w.shape
    tr = min(256, rows)

    def body(p_ref, w_ref, o_ref):
        o_ref[...] = w_ref[...].astype(BF16)

    return pl.pallas_call(
        body, name=name,
        grid_spec=pltpu.PrefetchScalarGridSpec(
            num_scalar_prefetch=1, grid=(rows // tr,),
            in_specs=[pl.BlockSpec((None, tr, cols), lambda i, p: (l, i, 0))],
            out_specs=pl.BlockSpec((None, tr, cols), lambda i, p: (p[0], i, 0))),
        out_shape=jax.ShapeDtypeStruct((N_CHIPS, rows, cols), BF16),
        compiler_params=_params(("parallel",)),
    )(place, w)


HBM = pl.BlockSpec(memory_space=pltpu.HBM)
SEM = pl.BlockSpec(memory_space=pltpu.SEMAPHORE)
DATAFLOW = pltpu.SideEffectType.DATAFLOW_SIDE_EFFECTING


def _in_hbm(a):
    return pltpu.with_memory_space_constraint(a, pltpu.HBM)


def _chip_copies_start(name, srcs, lands, make_copy):
    n_t = len(srcs)
    in_place = lands is None

    def body(*refs):
        src = refs[:n_t]
        k = n_t if in_place else 2 * n_t
        send_sems, recv_sems = refs[k], refs[k + 1]
        land = refs[k + 2:k + 2 + n_t] if in_place else refs[k + 2 + n_t:k + 2 + 2 * n_t]
        token = refs[-1]
        x, y, c = _place()
        me = 2 * x + y
        for t in range(n_t):
            for px, py, pk in _other_chips(x, y):
                s, d = make_copy(src[t], land[t], me, pk, c)
                pltpu.make_async_remote_copy(
                    src_ref=s, dst_ref=d, send_sem=send_sems.at[t], recv_sem=recv_sems.at[t],
                    device_id=(px, py, c), device_id_type=MESH).start()
        token[...] = jnp.zeros_like(token)

    bufs = list(srcs) if in_place else list(srcs) + list(lands)
    outs = pl.pallas_call(
        body, name=name,
        in_specs=[HBM] * len(bufs),
        out_specs=[SEM, SEM] + [HBM] * len(bufs) + [pl.BlockSpec(memory_space=pltpu.VMEM)],
        out_shape=[pltpu.SemaphoreType.DMA((n_t,)), pltpu.SemaphoreType.DMA((n_t,))]
        + [pltpu.HBM(b.shape, b.dtype) for b in bufs] + [jax.ShapeDtypeStruct((8, 128), F32)],
        input_output_aliases={i: 2 + i for i in range(len(bufs))},
        compiler_params=pltpu.CompilerParams(has_side_effects=DATAFLOW),
    )(*[_in_hbm(b) for b in bufs])
    return outs[0], outs[1], list(outs[2:2 + len(bufs)]), outs[-1]


def _chip_copies_wait(name, send_sems, recv_sems, bufs, sent, landed, after):
    n_b = len(bufs)

    def body(*refs):
        buf = refs[:n_b]
        send_ref, recv_ref = refs[n_b], refs[n_b + 1]
        x, y, c = _place()
        for t, (s, d) in enumerate(zip(sent(buf), landed(buf))):
            out = pltpu.make_async_remote_copy(src_ref=s, dst_ref=s, send_sem=send_ref.at[t], recv_sem=recv_ref.at[t],
                                               device_id=(x, y, c), device_id_type=MESH)
            out.wait_send()
            arrived = pltpu.make_async_remote_copy(src_ref=d, dst_ref=d, send_sem=send_ref.at[t],
                                                   recv_sem=recv_ref.at[t], device_id=(x, y, c), device_id_type=MESH)
            arrived.wait_recv()

    return pl.pallas_call(
        body, name=name,
        in_specs=[HBM] * n_b + [SEM, SEM, ANY], out_specs=[HBM] * n_b,
        out_shape=[pltpu.HBM(b.shape, b.dtype) for b in bufs],
        input_output_aliases={i: i for i in range(n_b)},
        compiler_params=pltpu.CompilerParams(has_side_effects=DATAFLOW),
    )(*bufs, send_sems, recv_sems, after)


def _gather_start(name, bufs):
    def make_copy(src, land, me, pk, c):
        hr = src.shape[1] // 2
        return src.at[me, pl.ds(c * hr, hr)], land.at[me, pl.ds(c * hr, hr)]

    return _chip_copies_start(name, bufs, None, make_copy)


def _gather_wait(name, send_sems, recv_sems, bufs, after):
    def three_halves(buf):
        return [b.at[pl.ds(0, 3), pl.ds(0, b.shape[1] // 2)] for b in buf]

    return _chip_copies_wait(name, send_sems, recv_sems, bufs, three_halves, three_halves, after)


def _gather_forward(name, bufs):
    n_t = len(bufs)
    n = 3 * n_t * D2D_CHUNKS

    def body(*refs):
        mine, buf = refs[:n_t], refs[n_t:2 * n_t]
        send_sems, recv_sems = refs[2 * n_t:]
        x, y, c = _place()
        chips = _other_chips(x, y)

        def piece(ref, t, slot, core, q):
            hr = ref[t].shape[1] // 2
            cr = hr // D2D_CHUNKS
            return ref[t].at[slot, pl.ds(core * hr + q * cr, cr)]

        copies = []
        for q in range(D2D_CHUNKS):
            for t in range(n_t):
                for j, (_, _, pk) in enumerate(chips):
                    s = (t * 3 + j) * D2D_CHUNKS + q
                    cp = pltpu.make_async_remote_copy(
                        src_ref=piece(mine, t, pk, c, q), dst_ref=piece(buf, t, pk, c, q), send_sem=send_sems.at[s],
                        recv_sem=recv_sems.at[s], device_id=(x, y, 1 - c), device_id_type=MESH)
                    cp.start()
                    copies.append(cp)
        for q in range(D2D_CHUNKS):
            for t in range(n_t):
                for j, (_, _, pk) in enumerate(chips):
                    s = (t * 3 + j) * D2D_CHUNKS + q
                    theirs = piece(buf, t, pk, 1 - c, q)
                    pltpu.make_async_remote_copy(
                        src_ref=theirs, dst_ref=theirs, send_sem=send_sems.at[s], recv_sem=recv_sems.at[s],
                        device_id=(x, y, 1 - c), device_id_type=MESH).wait_recv()
        for cp in copies:
            cp.wait_send()

    return pl.pallas_call(
        body, name=name,
        in_specs=[ANY] * n_t, out_specs=[ANY] * n_t,
        out_shape=[jax.ShapeDtypeStruct(b.shape, b.dtype) for b in bufs],
        input_output_aliases={t: t for t in range(n_t)},
        scratch_shapes=[pltpu.SemaphoreType.DMA((n,)), pltpu.SemaphoreType.DMA((n,))],
        compiler_params=pltpu.CompilerParams(has_side_effects=True),
    )(*bufs)


def _core_exchange(name, grads):
    n_t = len(grads)
    n = n_t * D2D_CHUNKS

    def body(*refs):
        src, theirs = refs[:n_t], refs[n_t:2 * n_t]
        send_sems, recv_sems = refs[2 * n_t:]
        x, y, c = _place()
        copies = []
        for q in range(D2D_CHUNKS):
            for t in range(n_t):
                hr = src[t].shape[1] // 2
                cr = hr // D2D_CHUNKS
                s = t * D2D_CHUNKS + q
                cp = pltpu.make_async_remote_copy(
                    src_ref=src[t].at[:, pl.ds((1 - c) * hr + q * cr, cr)],
                    dst_ref=theirs[t].at[:, pl.ds(q * cr, cr)],
                    send_sem=send_sems.at[s], recv_sem=recv_sems.at[s], device_id=(x, y, 1 - c), device_id_type=MESH)
                cp.start()
                copies.append(cp)
        for cp in copies:
            cp.wait()

    half = [jax.ShapeDtypeStruct((g.shape[0], g.shape[1] // 2, g.shape[2]), g.dtype) for g in grads]
    return pl.pallas_call(
        body, name=name,
        in_specs=[ANY] * n_t, out_specs=[ANY] * n_t, out_shape=half,
        scratch_shapes=[pltpu.SemaphoreType.DMA((n,)), pltpu.SemaphoreType.DMA((n,))],
        compiler_params=pltpu.CompilerParams(has_side_effects=True),
    )(*grads)


def _add_to_bf16(name, full, theirs, place):
    chips, rows, cols = theirs.shape
    tr = min(256, rows)
    per = rows // tr

    def body(p_ref, a_ref, b_ref, o_ref):
        o_ref[...] = (a_ref[...] + b_ref[...]).astype(BF16)

    blk = pl.BlockSpec((None, tr, cols), lambda k, i, p: (k, i, 0))
    return pl.pallas_call(
        body, name=name,
        grid_spec=pltpu.PrefetchScalarGridSpec(
            num_scalar_prefetch=1, grid=(chips, per),
            in_specs=[pl.BlockSpec((None, tr, cols), lambda k, i, p: (k, p[1] * per + i, 0)), blk],
            out_specs=blk),
        out_shape=jax.ShapeDtypeStruct(theirs.shape, BF16), compiler_params=_params(("parallel",) * 2),
    )(place, full, theirs)


def _chip_exchange_start(name, parts):
    lands = [lax.empty(p.shape, p.dtype) for p in parts]
    return _chip_copies_start(name, parts, lands, lambda src, land, me, pk, c: (src.at[pk], land.at[me]))


def _chip_exchange_wait(name, send_sems, recv_sems, bufs, after):
    n_t = len(bufs) // 2
    return _chip_copies_wait(name, send_sems, recv_sems, bufs,
                             lambda buf: [b.at[pl.ds(0, 3)] for b in buf[:n_t]],
                             lambda buf: [b.at[pl.ds(0, 3)] for b in buf[n_t:]], after)


def _sum_chips(name, parts, landed, place, l, stacked):
    chips, rows, cols = landed.shape
    tr = min(256, rows)
    per = rows // tr

    def body(p_ref, own_ref, *refs):
        land, o_ref = refs[:chips], refs[-1]
        tot = None
        for k in range(chips):
            term = jnp.where(p_ref[0] == k, own_ref[...], land[k][...]).astype(F32)
            tot = term if tot is None else tot + term
        o_ref[...] = tot

    def from_chip(k):
        return pl.BlockSpec((None, tr, cols), lambda i, p: (jnp.where(p[0] == k, (k + 1) % chips, k), i, 0))

    in_specs = [pl.BlockSpec((None, tr, cols), lambda i, p: (p[0], i, 0))] + [from_chip(k) for k in range(chips)]
    args = [parts] + [landed] * chips
    aliases = {}
    if stacked is not None:
        in_specs.append(ANY)
        args.append(stacked)
        aliases = {len(args): 0}
    return pl.pallas_call(
        body, name=name,
        grid_spec=pltpu.PrefetchScalarGridSpec(
            num_scalar_prefetch=1, grid=(per,), in_specs=in_specs,
            out_specs=pl.BlockSpec((None, tr, cols), lambda i, p: (l, p[1] * per + i, 0))),
        out_shape=jax.ShapeDtypeStruct((DEPTH, 2 * rows, cols), F32), input_output_aliases=aliases,
        compiler_params=_params(("parallel",)),
    )(place, *args)


def _core_share(bufs):
    n_t = len(bufs)
    n = n_t * D2D_CHUNKS

    def body(*refs):
        mine, buf = refs[:n_t], refs[n_t:2 * n_t]
        send_sems, recv_sems = refs[2 * n_t:]
        x, y, c = _place()

        def piece(ref, t, core, q):
            hr = ref[t].shape[1] // 2
            cr = hr // D2D_CHUNKS
            return ref[t].at[:, pl.ds(core * hr + q * cr, cr)]

        copies = []
        for q in range(D2D_CHUNKS):
            for t in range(n_t):
                s = t * D2D_CHUNKS + q
                cp = pltpu.make_async_remote_copy(
                    src_ref=piece(mine, t, c, q), dst_ref=piece(buf, t, c, q), send_sem=send_sems.at[s],
                    recv_sem=recv_sems.at[s], device_id=(x, y, 1 - c), device_id_type=MESH)
                cp.start()
                copies.append(cp)
        for q in range(D2D_CHUNKS):
            for t in range(n_t):
                s = t * D2D_CHUNKS + q
                theirs = piece(buf, t, 1 - c, q)
                pltpu.make_async_remote_copy(
                    src_ref=theirs, dst_ref=theirs, send_sem=send_sems.at[s], recv_sem=recv_sems.at[s],
                    device_id=(x, y, 1 - c), device_id_type=MESH).wait_recv()
        for cp in copies:
            cp.wait_send()

    return pl.pallas_call(
        body, name="grad_core_share",
        in_specs=[ANY] * n_t, out_specs=[ANY] * n_t,
        out_shape=[jax.ShapeDtypeStruct(b.shape, b.dtype) for b in bufs],
        input_output_aliases={t: t for t in range(n_t)},
        scratch_shapes=[pltpu.SemaphoreType.DMA((n,)), pltpu.SemaphoreType.DMA((n,))],
        compiler_params=pltpu.CompilerParams(has_side_effects=True),
    )(*bufs)


N_DEV = 8


def _all_reduce_small(vec):
    rows, lanes = vec.shape

    def body(v_ref, o_ref, gath_ref, send_sems, recv_sems):
        x, y, c = _place()
        me = 4 * x + 2 * y + c
        gath_ref[me] = v_ref[...]
        peers = [(x, y, 1 - c)]
        for px, py, _ in _other_chips(x, y):
            peers += [(px, py, c), (px, py, 1 - c)]
        copies = []
        for j, peer in enumerate(peers):
            cp = pltpu.make_async_remote_copy(
                src_ref=v_ref, dst_ref=gath_ref.at[me], send_sem=send_sems.at[j], recv_sem=recv_sems.at[j],
                device_id=peer, device_id_type=MESH)
            cp.start()
            copies.append(cp)
        for j, (px, py, pc) in enumerate(peers):
            slot = gath_ref.at[4 * px + 2 * py + pc]
            pltpu.make_async_remote_copy(
                src_ref=v_ref, dst_ref=slot, send_sem=send_sems.at[j], recv_sem=recv_sems.at[j],
                device_id=(px, py, pc), device_id_type=MESH).wait_recv()
        tot = gath_ref[0]
        for k in range(1, N_DEV):
            tot = tot + gath_ref[k]
        o_ref[...] = tot
        for cp in copies:
            cp.wait_send()

    vm = pl.BlockSpec(memory_space=pltpu.VMEM)
    return pl.pallas_call(
        body, name="small_all_reduce", in_specs=[vm], out_specs=vm,
        out_shape=jax.ShapeDtypeStruct((rows, lanes), F32),
        scratch_shapes=[pltpu.VMEM((N_DEV, rows, lanes), F32), pltpu.SemaphoreType.DMA((N_DEV - 1,)),
                        pltpu.SemaphoreType.DMA((N_DEV - 1,))],
        compiler_params=pltpu.CompilerParams(has_side_effects=True, vmem_limit_bytes=48 * MIB),
    )(vec)


def _adamw(name, w, g, m, v):
    rows, cols = w.shape
    tr = rows
    for cand in (256, 128, 64, 32, 16, 8):
        if rows % cand == 0:
            tr = cand
            break
    c1 = 1.0 - ADAM_B1 ** ADAM_STEP
    c2 = 1.0 - ADAM_B2 ** ADAM_STEP

    def body(w_ref, g_ref, m_ref, v_ref, d_ref, nm_ref, nv_ref):
        gv = g_ref[...]
        nm = ADAM_B1 * m_ref[...] + (1.0 - ADAM_B1) * gv
        nv = ADAM_B2 * v_ref[...] + (1.0 - ADAM_B2) * (gv * gv)
        nm_ref[...] = nm
        nv_ref[...] = nv
        d_ref[...] = -ADAM_LR * ((nm / c1) / (jnp.sqrt(nv / c2) + ADAM_EPS) + ADAM_WD * w_ref[...])

    blk = pl.BlockSpec((tr, cols), lambda i: (i, 0))
    out = jax.ShapeDtypeStruct((rows, cols), F32)
    return pl.pallas_call(
        body, name=name, grid=(rows // tr,), in_specs=[blk] * 4, out_specs=[blk] * 3, out_shape=[out] * 3,
        compiler_params=_params(("parallel",)),
    )(w, g, m, v)


def _pack_small(parts):
    flat = jnp.concatenate([parts[k].reshape(-1) for k in SMALL_NAMES])
    n = flat.shape[0]
    rows = -(-n // (256 * 128)) * 256
    return jnp.pad(flat, (0, rows * 128 - n)).reshape(rows, 128)


def _unpack_small(packed, like):
    flat = packed.reshape(-1)
    out, off = {}, 0
    for k in SMALL_NAMES:
        n = like[k].size
        out[k] = flat[off:off + n].reshape(like[k].shape)
        off += n
    return out


WEIGHT_ORDER = ("norm_g", "w_in", "sgu_ln_g", "sgu_ln_b", "sgu_w", "sgu_b", "mem_norm_g", "w_mem_kv", "q_norm_g",
                "k_norm_g", "w_out")


def kernel(x, mem, norm_g, w_in, sgu_ln_g, sgu_ln_b, sgu_w, sgu_b, mem_norm_g, w_mem_kv, q_norm_g, k_norm_g, w_out, loss_target, m_norm_g, m_w_in, m_sgu_ln_g, m_sgu_ln_b, m_sgu_w, m_sgu_b, m_mem_norm_g, m_w_mem_kv, m_q_norm_g, m_k_norm_g, m_w_out, v_norm_g, v_w_in, v_sgu_ln_g, v_sgu_ln_b, v_sgu_w, v_sgu_b, v_mem_norm_g, v_w_mem_kv, v_q_norm_g, v_k_norm_g, v_w_out):
    weights = dict(norm_g=norm_g, w_in=w_in, sgu_ln_g=sgu_ln_g, sgu_ln_b=sgu_ln_b, sgu_w=sgu_w, sgu_b=sgu_b,
                   mem_norm_g=mem_norm_g, w_mem_kv=w_mem_kv, q_norm_g=q_norm_g, k_norm_g=k_norm_g, w_out=w_out)
    mom_m = dict(norm_g=m_norm_g, w_in=m_w_in, sgu_ln_g=m_sgu_ln_g, sgu_ln_b=m_sgu_ln_b, sgu_w=m_sgu_w, sgu_b=m_sgu_b,
                 mem_norm_g=m_mem_norm_g, w_mem_kv=m_w_mem_kv, q_norm_g=m_q_norm_g, k_norm_g=m_k_norm_g, w_out=m_w_out)
    mom_v = dict(norm_g=v_norm_g, w_in=v_w_in, sgu_ln_g=v_sgu_ln_g, sgu_ln_b=v_sgu_ln_b, sgu_w=v_sgu_w, sgu_b=v_sgu_b,
                 mem_norm_g=v_mem_norm_g, w_mem_kv=v_w_mem_kv, q_norm_g=v_q_norm_g, k_norm_g=v_k_norm_g, w_out=v_w_out)
    big = ("w_in", "w_mem_kv", "w_out")
    sm = {k: weights[k] for k in SMALL_NAMES}

    place = _place_index()
    xs, mems, target = x[0], mem[0], loss_target[0]

    slots = [[_cast_into_slot(f"cast_{k}_{l}", weights[k], l, place) for k in big] for l in range(DEPTH)]
    w_all, saved = [None] * DEPTH, [None] * DEPTH
    send_sems, recv_sems, bufs, token = _gather_start("gather_start_0", slots[0])
    cur = xs
    for l in range(DEPTH):
        bufs = _gather_wait(f"gather_wait_{l}", send_sems, recv_sems, bufs, cur)
        w_all[l] = _gather_forward(f"gather_forward_{l}", bufs)
        if l + 1 < DEPTH:
            send_sems, recv_sems, bufs, token = _gather_start(f"gather_start_{l + 1}", slots[l + 1])
        cur, saved[l] = _layer_fwd(l, cur, mems, sm, *w_all[l], after=token if l + 1 < DEPTH else None)
    dxo, dxo_b, loss_part = _loss_and_grad("loss", cur, target, min(256, xs.shape[0]))
    loss = lax.psum(loss_part[0, 0], ("x", "y", "c"))

    small_g, flight = [None] * DEPTH, [None] * DEPTH
    token = None
    for l in reversed(range(DEPTH)):
        dxo, dxo_b, small_g[l], *full = _layer_bwd(l, dxo, dxo_b, mems, sm, saved[l], *w_all[l], after=token)
        theirs = _core_exchange(f"grad_core_exchange_{l}", full)
        parts = [_add_to_bf16(f"grad_core_sum_{l}_{t}", full[t], theirs[t], place) for t in range(3)]
        *flight[l], token = _chip_exchange_start(f"grad_chip_start_{l}", parts)
    grad_x = dxo
    halves = [None] * 3
    for l in reversed(range(DEPTH)):
        send_sems, recv_sems, bufs = flight[l]
        bufs = _chip_exchange_wait(f"grad_chip_wait_{l}", send_sems, recv_sems, bufs, grad_x)
        for t in range(3):
            halves[t] = _sum_chips(f"grad_chip_sum_{l}_{t}", bufs[t], bufs[3 + t], place, l, halves[t])
    big_g = dict(zip(big, _core_share(halves)))

    small_g = {k: jnp.stack([small_g[l][k] for l in range(DEPTH)]) for k in SMALL_NAMES}
    small_sum = _unpack_small(_all_reduce_small(_pack_small(small_g)), sm)

    grads, delta, new_m, new_v = {}, {}, {}, {}
    for k in big:
        shape = weights[k].shape
        two_d = (shape[0] * shape[1], shape[2])
        grads[k] = big_g[k]
        d, nm, nv = _adamw(f"adamw_{k}", weights[k].reshape(two_d), big_g[k].reshape(two_d),
                           mom_m[k].reshape(two_d), mom_v[k].reshape(two_d))
        delta[k], new_m[k], new_v[k] = d.reshape(shape), nm.reshape(shape), nv.reshape(shape)
    d, nm, nv = _adamw("adamw_small", _pack_small(sm), _pack_small(small_sum),
                       _pack_small({k: mom_m[k] for k in SMALL_NAMES}), _pack_small({k: mom_v[k] for k in SMALL_NAMES}))
    grads.update(small_sum)
    delta.update(_unpack_small(d, sm))
    new_m.update(_unpack_small(nm, sm))
    new_v.update(_unpack_small(nv, sm))
    return (loss, grad_x[None], *[grads[k] for k in WEIGHT_ORDER], *[delta[k] for k in WEIGHT_ORDER],
            *[new_m[k] for k in WEIGHT_ORDER], *[new_v[k] for k in WEIGHT_ORDER])
```

```python
import functools
import math

import jax
import jax.numpy as jnp
from jax import lax
from jax.experimental import pallas as pl
from jax.experimental.pallas import tpu as pltpu

F32 = jnp.float32
BF16 = jnp.bfloat16
MESH = pl.DeviceIdType.MESH

D_MODEL = 2048
DEPTH = 2
CHUNK = 128
D_A = 1024
A_GROUPS = 8
D_B = 512
D_C = 512
HEADS = 4
HEAD_DIM = 128
IN_WIDTH = 6144
N_CHIPS = 4
EPS = 1e-6
ATT_SCALE = 1.0 / math.sqrt(HEAD_DIM)

OFF_U, OFF_V, OFF_ZA = 0, 1024, 2048
OFF_QB, OFF_KB, OFF_VB, OFF_ZB = 3072, 3584, 4096, 4608
OFF_QC, OFF_ZC = 5120, 5632
OFF_YB, OFF_YC = 1024, 1536

ADAM_LR = 0.001
ADAM_B1 = 0.9
ADAM_B2 = 0.999
ADAM_EPS = 1e-08
ADAM_WD = 0.01
ADAM_STEP = 10

MIB = 1024 * 1024
ANY = pl.BlockSpec(memory_space=pl.ANY)


def _params(semantics=None, vmem_mb=48):
    return pltpu.CompilerParams(dimension_semantics=semantics, vmem_limit_bytes=vmem_mb * MIB)


def _gelu(x):
    return 0.5 * x * (1.0 + lax.erf(x * (1.0 / math.sqrt(2.0))))


def _gelu_grad(x):
    cdf = 0.5 * (1.0 + lax.erf(x * (1.0 / math.sqrt(2.0))))
    pdf = jnp.exp(-0.5 * x * x) * (1.0 / math.sqrt(2.0 * math.pi))
    return cdf + x * pdf


def _sigmoid(x):
    return 1.0 / (1.0 + jnp.exp(-x))


def _silu_and_grad(z):
    s = _sigmoid(z)
    return z * s, s * (1.0 + z * (1.0 - s))


def _split_bf16(x):
    hi = x.astype(BF16)
    lo = (x - hi.astype(F32)).astype(BF16)
    return hi, lo


def _dot(a, b, dims):
    return lax.dot_general(a, b, (dims, ((), ())), preferred_element_type=F32)


NN = ((1,), (0,))
NT = ((1,), (1,))
TN = ((0,), (0,))


def _matmul(name, a, b, *, grid, a_spec, b_spec, o_spec, out_shape, dims, res=None, res_spec=None, after=None,
            vmem_mb=48):
    nk = grid[2]
    n_in = 2 + (res is not None) + (after is not None)

    def body(*refs):
        a_ref, b_ref = refs[0], refs[1]
        r_ref = refs[2] if res is not None else None
        o_ref = refs[n_in]
        part = _dot(a_ref[...], b_ref[...], dims)
        if nk == 1:
            if r_ref is not None:
                part = part + r_ref[...]
            o_ref[...] = part.astype(o_ref.dtype)
            return
        acc_ref = refs[n_in + 1]
        k = pl.program_id(2)

        @pl.when(k == 0)
        def _():
            acc_ref[...] = part

        @pl.when(k > 0)
        def _():
            acc_ref[...] += part

        @pl.when(k == nk - 1)
        def _():
            tot = acc_ref[...]
            if r_ref is not None:
                tot = tot + r_ref[...]
            o_ref[...] = tot.astype(o_ref.dtype)

    in_specs = [a_spec, b_spec]
    args = [a, b]
    if res is not None:
        in_specs.append(res_spec)
        args.append(res)
    if after is not None:
        in_specs.append(ANY)
        args.append(after)
    acc_shape = tuple(d for d in o_spec.block_shape if d is not None)
    scratch = [pltpu.VMEM(acc_shape, F32)] if nk > 1 else []
    return pl.pallas_call(
        body, name=name, grid=grid, in_specs=in_specs, out_specs=o_spec, out_shape=out_shape,
        scratch_shapes=scratch,
        compiler_params=_params(("parallel", "parallel", "arbitrary"), vmem_mb),
    )(*args)


def _rms_fwd(name, x, g, tr, after=None):
    rows, d = x.shape

    def body(x_ref, g_ref, *refs):
        h_ref = refs[-1]
        xv = x_ref[...]
        r = lax.rsqrt(jnp.mean(xv * xv, axis=-1, keepdims=True) + EPS)
        h_ref[...] = (xv * r * g_ref[...]).astype(BF16)

    return pl.pallas_call(
        body, name=name, grid=(rows // tr,),
        in_specs=[pl.BlockSpec((tr, d), lambda i: (i, 0)), pl.BlockSpec((1, d), lambda i: (0, 0))]
        + ([] if after is None else [ANY]),
        out_specs=pl.BlockSpec((tr, d), lambda i: (i, 0)),
        out_shape=jax.ShapeDtypeStruct((rows, d), BF16),
        compiler_params=_params(("parallel",)),
    )(x, g, *([] if after is None else [after]))


def _rms_bwd(name, x, dh, dres, g, tr):
    rows, d = x.shape

    def body(x_ref, dh_ref, dres_ref, g_ref, dx_ref, dxb_ref, dg_ref):
        xv = x_ref[...]
        r = lax.rsqrt(jnp.mean(xv * xv, axis=-1, keepdims=True) + EPS)
        xhat = xv * r
        dhv = dh_ref[...]
        dxh = dhv * g_ref[...]
        dx = r * (dxh - xhat * jnp.mean(dxh * xhat, axis=-1, keepdims=True)) + dres_ref[...]
        dx_ref[...] = dx
        dxb_ref[...] = dx.astype(BF16)
        part = jnp.sum(dhv * xhat, axis=0, keepdims=True)

        @pl.when(pl.program_id(0) == 0)
        def _():
            dg_ref[...] = part

        @pl.when(pl.program_id(0) > 0)
        def _():
            dg_ref[...] += part

    blk = pl.BlockSpec((tr, d), lambda i: (i, 0))
    vec = pl.BlockSpec((1, d), lambda i: (0, 0))
    return pl.pallas_call(
        body, name=name, grid=(rows // tr,), in_specs=[blk, blk, blk, vec], out_specs=[blk, blk, vec],
        out_shape=[jax.ShapeDtypeStruct((rows, d), F32), jax.ShapeDtypeStruct((rows, d), BF16),
                   jax.ShapeDtypeStruct((1, d), F32)],
        compiler_params=_params(("arbitrary",)),
    )(x, dh, dres, g)


def _rms_gain_grad(name, x, dh):
    rows, d = x.shape

    def body(x_ref, dh_ref, dg_ref):
        xv = x_ref[...]
        r = lax.rsqrt(jnp.mean(xv * xv, axis=-1, keepdims=True) + EPS)
        dg_ref[...] = jnp.sum(dh_ref[...] * xv * r, axis=0, keepdims=True)

    return pl.pallas_call(
        body, name=name, out_shape=jax.ShapeDtypeStruct((1, d), F32), compiler_params=_params(None),
    )(x, dh)


def _loss_and_grad(name, y, target, tr):
    rows, d = y.shape
    n = rows // tr

    def body(y_ref, t_ref, dx_ref, dxb_ref, loss_ref, acc_ref):
        e = y_ref[...] - t_ref[...]
        dx = e * (1.0 / d)
        dx_ref[...] = dx
        dxb_ref[...] = dx.astype(BF16)
        part = jnp.sum(e * e, axis=0, keepdims=True)
        i = pl.program_id(0)

        @pl.when(i == 0)
        def _():
            acc_ref[...] = part

        @pl.when(i > 0)
        def _():
            acc_ref[...] += part

        @pl.when(i == n - 1)
        def _():
            loss_ref[...] = jnp.sum(acc_ref[...], axis=-1, keepdims=True) * (0.5 / d)

    blk = pl.BlockSpec((tr, d), lambda i: (i, 0))
    return pl.pallas_call(
        body, name=name, grid=(n,), in_specs=[blk, blk],
        out_specs=[blk, blk, pl.BlockSpec((1, 1), lambda i: (0, 0))],
        out_shape=[jax.ShapeDtypeStruct((rows, d), F32), jax.ShapeDtypeStruct((rows, d), BF16),
                   jax.ShapeDtypeStruct((1, 1), F32)],
        scratch_shapes=[pltpu.VMEM((1, d), F32)],
        compiler_params=_params(("arbitrary",)),
    )(y, target)


SB_T = 256


def _sb_scores(q, kblk):
    z = _dot(q, kblk, NT) * ATT_SCALE
    e = jnp.exp(-jnp.abs(z))
    sp = jnp.log1p(e)
    lb = jnp.minimum(z, 0.0) - sp
    l1 = lb - z
    return z, e, lb, l1


def _sb_fwd(name, proj):
    s_len = proj.shape[0]
    t = SB_T
    nq = s_len // t

    def body(q_ref, k_ref, v_ref, o_ref):
        i = pl.program_id(1)
        q = q_ref[...].astype(BF16)
        row = lax.broadcasted_iota(jnp.int32, (t, t), 0)
        col = lax.broadcasted_iota(jnp.int32, (t, t), 1)
        causal = col < row
        after_mat = (row > col).astype(BF16)

        def tile(kb, carry, acc, masked):
            start = pl.multiple_of(kb * t, t)
            kblk = k_ref[pl.ds(start, t), :].astype(BF16)
            vblk = v_ref[pl.ds(start, t), :].astype(BF16)
            _, _, lb, l1 = _sb_scores(q, kblk)
            if masked:
                l1 = jnp.where(causal, l1, 0.0)
            hi, lo = _split_bf16(l1)
            after = _dot(hi, after_mat, NN) + _dot(lo, after_mat, NN) + carry
            a = jnp.exp(lb + after)
            if masked:
                a = jnp.where(causal, a, 0.0)
            acc = acc + _dot(a.astype(BF16), vblk, NN)
            carry = carry + jnp.sum(l1, axis=-1, keepdims=True)
            return carry, acc

        carry, acc = tile(i, jnp.zeros((t, 1), F32), jnp.zeros((t, HEAD_DIM), F32), True)

        def step(n, state):
            return tile(i - 1 - n, state[0], state[1], False)

        carry, acc = lax.fori_loop(0, i, step, (carry, acc))
        o_ref[...] = acc

    cb = HEAD_DIM
    return pl.pallas_call(
        body, name=name, grid=(HEADS, nq),
        in_specs=[pl.BlockSpec((t, cb), lambda h, i: (i, OFF_QB // cb + h)),
                  pl.BlockSpec((s_len, cb), lambda h, i: (0, OFF_KB // cb + h)),
                  pl.BlockSpec((s_len, cb), lambda h, i: (0, OFF_VB // cb + h))],
        out_specs=pl.BlockSpec((t, cb), lambda h, i: (i, h)),
        out_shape=jax.ShapeDtypeStruct((s_len, D_B), F32),
        compiler_params=_params(("parallel", "arbitrary")),
    )(proj, proj, proj)


def _sb_bwd(name, proj, dy):
    s_len = proj.shape[0]
    t = SB_T
    nq = s_len // t

    def body(q_ref, k_ref, v_ref, z_ref, dy_ref, dq_ref, dk_ref, dv_ref, a_ref, s_ref):
        i = pl.program_id(1)

        @pl.when(i == 0)
        def _():
            dk_ref[...] = jnp.zeros_like(dk_ref)
            dv_ref[...] = jnp.zeros_like(dv_ref)

        q = q_ref[...].astype(BF16)
        silu_z, _ = _silu_and_grad(z_ref[...])
        do_b = (dy_ref[...] * silu_z).astype(BF16)
        row = lax.broadcasted_iota(jnp.int32, (t, t), 0)
        col = lax.broadcasted_iota(jnp.int32, (t, t), 1)
        causal = col < row
        after_mat = (row > col).astype(BF16)
        before_mat = (row < col).astype(BF16)

        def weights(kb, carry, masked):
            start = pl.multiple_of(kb * t, t)
            kblk = k_ref[pl.ds(start, t), :].astype(BF16)
            z, _, lb, l1 = _sb_scores(q, kblk)
            if masked:
                l1 = jnp.where(causal, l1, 0.0)
            hi, lo = _split_bf16(l1)
            after = _dot(hi, after_mat, NN) + _dot(lo, after_mat, NN) + carry
            a = jnp.exp(lb + after)
            if masked:
                a = jnp.where(causal, a, 0.0)
            a_ref[kb] = a
            s_ref[kb] = z
            return carry + jnp.sum(l1, axis=-1, keepdims=True)

        carry = weights(i, jnp.zeros((t, 1), F32), True)
        lax.fori_loop(0, i, lambda n, c: weights(i - 1 - n, c, False), carry)

        def grads(kb, carry, dq, masked):
            start = pl.multiple_of(kb * t, t)
            kblk = k_ref[pl.ds(start, t), :].astype(BF16)
            vblk = v_ref[pl.ds(start, t), :].astype(BF16)
            a = a_ref[kb]
            z = s_ref[kb]
            g = _dot(do_b, vblk, NT) * a
            ghi, glo = _split_bf16(g)
            prefix = _dot(ghi, before_mat, NN) + _dot(glo, before_mat, NN) + carry
            e = jnp.exp(-jnp.abs(z))
            inv = 1.0 / (1.0 + e)
            pos = z >= 0.0
            beta = jnp.where(pos, inv, e * inv)
            one_m_beta = jnp.where(pos, e * inv, inv)
            dz = (g * one_m_beta - prefix * beta) * ATT_SCALE
            if masked:
                dz = jnp.where(causal, dz, 0.0)
            dz_b = dz.astype(BF16)
            dq = dq + _dot(dz_b, kblk, NN)
            dk_ref[pl.ds(start, t), :] += _dot(dz_b, q, TN)
            dv_ref[pl.ds(start, t), :] += _dot(a.astype(BF16), do_b, TN)
            return carry + jnp.sum(g, axis=-1, keepdims=True), dq

        state = lax.fori_loop(0, i, lambda kb, st: grads(kb, st[0], st[1], False),
                              (jnp.zeros((t, 1), F32), jnp.zeros((t, HEAD_DIM), F32)))
        _, dq = grads(i, state[0], state[1], True)
        dq_ref[...] = dq

    cb = HEAD_DIM
    qblk = lambda off: pl.BlockSpec((t, cb), lambda h, i: (i, off // cb + h))
    full = lambda off: pl.BlockSpec((s_len, cb), lambda h, i: (0, off // cb + h))
    out = jax.ShapeDtypeStruct((s_len, D_B), F32)
    return pl.pallas_call(
        body, name=name, grid=(HEADS, nq),
        in_specs=[qblk(OFF_QB), full(OFF_KB), full(OFF_VB), qblk(OFF_ZB), qblk(OFF_YB)],
        out_specs=[qblk(0), full(0), full(0)],
        out_shape=[out, out, out],
        scratch_shapes=[pltpu.VMEM((nq, t, t), F32), pltpu.VMEM((nq, t, t), F32)],
        compiler_params=_params(("parallel", "arbitrary")),
    )(proj, proj, proj, proj, dy)


MEM_TQ = 512


def _qk_norm(x, g):
    r = lax.rsqrt(jnp.mean(x * x, axis=-1, keepdims=True) + EPS)
    xhat = x * r
    return xhat * g, xhat, r


def _qk_norm_bwd(dn, g, xhat, r):
    dxh = dn * g
    return r * (dxh - xhat * jnp.mean(dxh * xhat, axis=-1, keepdims=True))


def _mem_probs(q, mk, qg, kg):
    qn, qhat, rq = _qk_norm(q, qg)
    kn, khat, rk = _qk_norm(mk, kg)
    qn_b, kn_b = qn.astype(BF16), kn.astype(BF16)
    s = _dot(qn_b, kn_b, NT) * ATT_SCALE
    p = jnp.exp(s - jnp.max(s, axis=-1, keepdims=True))
    p = p / jnp.sum(p, axis=-1, keepdims=True)
    return p, qn_b, kn_b, qhat, rq, khat, rk


def _mem_fwd(name, proj, mem_kv, qg, kg):
    s_len = proj.shape[0]
    m_len = mem_kv.shape[0]
    tq = min(MEM_TQ, s_len)

    def body(q_ref, mk_ref, mv_ref, qg_ref, kg_ref, o_ref):
        p = _mem_probs(q_ref[...], mk_ref[...], qg_ref[...], kg_ref[...])[0]
        o_ref[...] = _dot(p.astype(BF16), mv_ref[...].astype(BF16), NN)

    cb = HEAD_DIM
    vec = pl.BlockSpec((1, cb), lambda h, i: (0, 0))
    return pl.pallas_call(
        body, name=name, grid=(HEADS, s_len // tq),
        in_specs=[pl.BlockSpec((tq, cb), lambda h, i: (i, OFF_QC // cb + h)),
                  pl.BlockSpec((m_len, cb), lambda h, i: (0, h)),
                  pl.BlockSpec((m_len, cb), lambda h, i: (0, HEADS + h)), vec, vec],
        out_specs=pl.BlockSpec((tq, cb), lambda h, i: (i, h)),
        out_shape=jax.ShapeDtypeStruct((s_len, D_C), F32),
        compiler_params=_params(("parallel", "parallel")),
    )(proj, mem_kv, mem_kv, qg, kg)


def _mem_bwd(name, proj, mem_kv, qg, kg, dy):
    s_len = proj.shape[0]
    m_len = mem_kv.shape[0]
    tq = min(MEM_TQ, s_len)

    def body(q_ref, mk_ref, mv_ref, qg_ref, kg_ref, z_ref, dy_ref, dq_ref, dmk_ref, dmv_ref, dqg_ref, dkg_ref):
        h, i = pl.program_id(0), pl.program_id(1)

        @pl.when(i == 0)
        def _():
            dmk_ref[...] = jnp.zeros_like(dmk_ref)
            dmv_ref[...] = jnp.zeros_like(dmv_ref)

        @pl.when((i == 0) & (h == 0))
        def _():
            dqg_ref[...] = jnp.zeros_like(dqg_ref)
            dkg_ref[...] = jnp.zeros_like(dkg_ref)

        qg, kg = qg_ref[...], kg_ref[...]
        p, qn_b, kn_b, qhat, rq, khat, rk = _mem_probs(q_ref[...], mk_ref[...], qg, kg)
        silu_z, _ = _silu_and_grad(z_ref[...])
        do_b = (dy_ref[...] * silu_z).astype(BF16)
        dmv_ref[...] += _dot(p.astype(BF16), do_b, TN)
        dp = _dot(do_b, mv_ref[...].astype(BF16), NT)
        ds = (p * (dp - jnp.sum(dp * p, axis=-1, keepdims=True)) * ATT_SCALE).astype(BF16)
        dqn = _dot(ds, kn_b, NN)
        dkn = _dot(ds, qn_b, TN)
        dq_ref[...] = _qk_norm_bwd(dqn, qg, qhat, rq)
        dmk_ref[...] += _qk_norm_bwd(dkn, kg, khat, rk)
        dqg_ref[...] += jnp.sum(dqn * qhat, axis=0, keepdims=True)
        dkg_ref[...] += jnp.sum(dkn * khat, axis=0, keepdims=True)

    cb = HEAD_DIM
    vec = pl.BlockSpec((1, cb), lambda h, i: (0, 0))
    qblk = lambda off: pl.BlockSpec((tq, cb), lambda h, i: (i, off // cb + h))
    memblk = lambda off: pl.BlockSpec((m_len, cb), lambda h, i: (0, off + h))
    return pl.pallas_call(
        body, name=name, grid=(HEADS, s_len // tq),
        in_specs=[qblk(OFF_QC), memblk(0), memblk(HEADS), vec, vec, qblk(OFF_ZC), qblk(OFF_YC)],
        out_specs=[qblk(0), memblk(0), memblk(0), vec, vec],
        out_shape=[jax.ShapeDtypeStruct((s_len, D_C), F32), jax.ShapeDtypeStruct((m_len, D_C), F32),
                   jax.ShapeDtypeStruct((m_len, D_C), F32), jax.ShapeDtypeStruct((1, cb), F32),
                   jax.ShapeDtypeStruct((1, cb), F32)],
        compiler_params=_params(("arbitrary", "arbitrary")),
    )(proj, mem_kv, mem_kv, qg, kg, proj, dy)


def _sgu_common(u_ref, v_ref, lng_ref, lnb_ref, w_ref, bias_ref):
    ug = _gelu(u_ref[...])
    vg = _gelu(v_ref[...])
    mu = jnp.mean(vg, axis=-1, keepdims=True)
    xc = vg - mu
    rstd = lax.rsqrt(jnp.mean(xc * xc, axis=-1, keepdims=True) + EPS)
    xhat = xc * rstd
    vn = xhat * lng_ref[...] + lnb_ref[...]
    vn_b = vn.astype(BF16)
    row = lax.broadcasted_iota(jnp.int32, (CHUNK, CHUNK), 0)
    col = lax.broadcasted_iota(jnp.int32, (CHUNK, CHUNK), 1)
    tril = row >= col
    mixed = []
    for g in range(A_GROUPS):
        w = jnp.where(tril, w_ref[g], 0.0).astype(BF16)
        sl = slice(g * CHUNK, (g + 1) * CHUNK)
        mixed.append(_dot(w, vn_b[:, sl], NN) + bias_ref[:, sl])
    return ug, xhat, rstd, vn_b, mixed, tril


def _gate_fwd(name, proj, o_b, o_c, lng, lnb, w_s, bias):
    s_len = proj.shape[0]

    def body(u_ref, v_ref, za_ref, zb_ref, zc_ref, ob_ref, oc_ref, lng_ref, lnb_ref, w_ref, bias_ref, y_ref):
        ug, _, _, _, mixed, _ = _sgu_common(u_ref, v_ref, lng_ref, lnb_ref, w_ref, bias_ref)
        sza, _ = _silu_and_grad(za_ref[...])
        gate = ug * sza
        for g in range(A_GROUPS):
            sl = slice(g * CHUNK, (g + 1) * CHUNK)
            y_ref[:, sl] = (gate[:, sl] * mixed[g]).astype(BF16)
        szb, _ = _silu_and_grad(zb_ref[...])
        y_ref[:, OFF_YB:OFF_YB + D_B] = (ob_ref[...] * szb).astype(BF16)
        szc, _ = _silu_and_grad(zc_ref[...])
        y_ref[:, OFF_YC:OFF_YC + D_C] = (oc_ref[...] * szc).astype(BF16)

    wide = lambda off: pl.BlockSpec((CHUNK, D_A), lambda i: (i, off // D_A))
    narrow = lambda off: pl.BlockSpec((CHUNK, D_B), lambda i: (i, off // D_B))
    vec = pl.BlockSpec((1, D_A), lambda i: (0, 0))
    return pl.pallas_call(
        body, name=name, grid=(s_len // CHUNK,),
        in_specs=[wide(OFF_U), wide(OFF_V), wide(OFF_ZA), narrow(OFF_ZB), narrow(OFF_ZC), narrow(0), narrow(0), vec, vec,
                  pl.BlockSpec((A_GROUPS, CHUNK, CHUNK), lambda i: (0, 0, 0)),
                  pl.BlockSpec((CHUNK, D_A), lambda i: (0, 0))],
        out_specs=pl.BlockSpec((CHUNK, D_MODEL), lambda i: (i, 0)),
        out_shape=jax.ShapeDtypeStruct((s_len, D_MODEL), BF16),
        compiler_params=_params(("parallel",)),
    )(proj, proj, proj, proj, proj, o_b, o_c, lng, lnb, w_s, bias)


def _gate_bwd(name, proj, dy, o_b, o_c, dqkv, dq_c, lng, lnb, w_s, w_s_t, bias):
    s_len = proj.shape[0]
    n = s_len // CHUNK
    dq_b, dk_b, dv_b = dqkv

    def body(u_ref, v_ref, za_ref, zb_ref, zc_ref, dya_ref, dyb_ref, dyc_ref, ob_ref, oc_ref, dq_ref, dk_ref, dv_ref,
             dqc_ref, lng_ref, lnb_ref, w_ref, wt_ref, bias_ref, dp_ref, dw_ref, dsb_ref, dlng_ref, dlnb_ref, dbias_ref):
        i = pl.program_id(0)

        @pl.when(i == 0)
        def _():
            dw_ref[...] = jnp.zeros_like(dw_ref)
            dbias_ref[...] = jnp.zeros_like(dbias_ref)
            dlng_ref[...] = jnp.zeros_like(dlng_ref)
            dlnb_ref[...] = jnp.zeros_like(dlnb_ref)

        ug, xhat, rstd, vn_b, mixed, tril = _sgu_common(u_ref, v_ref, lng_ref, lnb_ref, w_ref, bias_ref)
        za = za_ref[...]
        sza, dsza = _silu_and_grad(za)
        dya = dya_ref[...]
        mixed_all = jnp.concatenate(mixed, axis=-1)
        d_mixed = dya * ug * sza
        dp_ref[:, OFF_U:OFF_U + D_A] = (dya * mixed_all * sza * _gelu_grad(u_ref[...])).astype(BF16)
        dp_ref[:, OFF_ZA:OFF_ZA + D_A] = (dya * ug * mixed_all * dsza).astype(BF16)
        dbias_ref[...] += d_mixed
        dm_b = d_mixed.astype(BF16)
        triu = lax.broadcasted_iota(jnp.int32, (CHUNK, CHUNK), 0) <= lax.broadcasted_iota(jnp.int32, (CHUNK, CHUNK), 1)
        d_vn = []
        for g in range(A_GROUPS):
            sl = slice(g * CHUNK, (g + 1) * CHUNK)
            wt = jnp.where(triu, wt_ref[g], 0.0).astype(BF16)
            d_vn.append(_dot(wt, dm_b[:, sl], NN))
            dw_ref[g] += jnp.where(tril, _dot(dm_b[:, sl], vn_b[:, sl], NT), 0.0)
        d_vn = jnp.concatenate(d_vn, axis=-1)
        dlng_ref[...] += jnp.sum(d_vn * xhat, axis=0, keepdims=True)
        dlnb_ref[...] += jnp.sum(d_vn, axis=0, keepdims=True)
        dxh = d_vn * lng_ref[...]
        d_vg = rstd * (dxh - jnp.mean(dxh, axis=-1, keepdims=True)
                       - xhat * jnp.mean(dxh * xhat, axis=-1, keepdims=True))
        dp_ref[:, OFF_V:OFF_V + D_A] = (d_vg * _gelu_grad(v_ref[...])).astype(BF16)
        dp_ref[:, OFF_QB:OFF_QB + D_B] = dq_ref[...].astype(BF16)
        dp_ref[:, OFF_KB:OFF_KB + D_B] = dk_ref[...].astype(BF16)
        dp_ref[:, OFF_VB:OFF_VB + D_B] = dv_ref[...].astype(BF16)
        _, dszb = _silu_and_grad(zb_ref[...])
        dp_ref[:, OFF_ZB:OFF_ZB + D_B] = (dyb_ref[...] * ob_ref[...] * dszb).astype(BF16)
        dp_ref[:, OFF_QC:OFF_QC + D_C] = dqc_ref[...].astype(BF16)
        _, dszc = _silu_and_grad(zc_ref[...])
        dp_ref[:, OFF_ZC:OFF_ZC + D_C] = (dyc_ref[...] * oc_ref[...] * dszc).astype(BF16)

        @pl.when(i == n - 1)
        def _():
            ch = lax.broadcasted_iota(jnp.int32, (D_A, CHUNK), 0)
            gcol = lax.broadcasted_iota(jnp.int32, (D_A, CHUNK), 1)
            pick = (ch // (D_A // A_GROUPS) == gcol).astype(BF16)
            rest = dbias_ref[...]
            tot = jnp.zeros((CHUNK, CHUNK), F32)
            for _ in range(3):
                term = rest.astype(BF16)
                tot = tot + _dot(term, pick, NN)
                rest = rest - term.astype(F32)
            dsb_ref[...] = tot

    wide = lambda off: pl.BlockSpec((CHUNK, D_A), lambda i: (i, off // D_A))
    narrow = lambda off: pl.BlockSpec((CHUNK, D_B), lambda i: (i, off // D_B))
    vec = pl.BlockSpec((1, D_A), lambda i: (0, 0))
    wspec = pl.BlockSpec((A_GROUPS, CHUNK, CHUNK), lambda i: (0, 0, 0))
    bspec = pl.BlockSpec((CHUNK, D_A), lambda i: (0, 0))
    return pl.pallas_call(
        body, name=name, grid=(n,),
        in_specs=[wide(OFF_U), wide(OFF_V), wide(OFF_ZA), narrow(OFF_ZB), narrow(OFF_ZC),
                  wide(0), narrow(OFF_YB), narrow(OFF_YC), narrow(0), narrow(0), narrow(0), narrow(0), narrow(0),
                  narrow(0), vec, vec, wspec, wspec, bspec],
        out_specs=[pl.BlockSpec((CHUNK, IN_WIDTH), lambda i: (i, 0)), wspec,
                   pl.BlockSpec((CHUNK, CHUNK), lambda i: (0, 0)), vec, vec],
        out_shape=[jax.ShapeDtypeStruct((s_len, IN_WIDTH), BF16), jax.ShapeDtypeStruct((A_GROUPS, CHUNK, CHUNK), F32),
                   jax.ShapeDtypeStruct((CHUNK, CHUNK), F32), jax.ShapeDtypeStruct((1, D_A), F32),
                   jax.ShapeDtypeStruct((1, D_A), F32)],
        scratch_shapes=[pltpu.VMEM((CHUNK, D_A), F32)],
        compiler_params=_params(("arbitrary",)),
    )(proj, proj, proj, proj, proj, dy, dy, dy, o_b, o_c, dq_b, dk_b, dv_b, dq_c, lng, lnb, w_s, w_s_t, bias)


IN_SHARD = IN_WIDTH // N_CHIPS
ROW_SHARD = D_MODEL // N_CHIPS


def _bias_rows(sgu_b_l):
    return jnp.repeat(sgu_b_l.T, D_A // A_GROUPS, axis=1)


def _layer_fwd(l, x, mem, sm, w_in_all, rest):
    s_len = x.shape[0]
    m_len = mem.shape[0]
    tm = min(1024, s_len)
    tn = 768
    per = IN_SHARD // tn
    h = _rms_fwd(f"rms_fwd_{l}", x, sm["norm_g"][l][None], min(256, s_len))
    proj = _matmul(
        f"in_proj_{l}", h, w_in_all, grid=(s_len // tm, IN_WIDTH // tn, 1),
        a_spec=pl.BlockSpec((tm, D_MODEL), lambda i, j, k: (i, 0)),
        b_spec=pl.BlockSpec((None, D_MODEL, tn), lambda i, j, k: (j // per, 0, j % per)),
        o_spec=pl.BlockSpec((tm, tn), lambda i, j, k: (i, j)),
        out_shape=jax.ShapeDtypeStruct((s_len, IN_WIDTH), F32), dims=NN)
    w_kv_all, w_out_all, after = rest(proj)
    mem_h = _rms_fwd(f"mem_rms_fwd_{l}", mem, sm["mem_norm_g"][l][None], m_len, after)
    mem_kv = _matmul(
        f"mem_kv_{l}", mem_h, w_kv_all, grid=(1, 2, N_CHIPS),
        a_spec=pl.BlockSpec((m_len, ROW_SHARD), lambda i, j, k: (0, k)),
        b_spec=pl.BlockSpec((None, ROW_SHARD, D_C), lambda i, j, k: (k, 0, j)),
        o_spec=pl.BlockSpec((m_len, D_C), lambda i, j, k: (0, j)),
        out_shape=jax.ShapeDtypeStruct((m_len, 2 * D_C), F32), dims=NN)
    o_b = _sb_fwd(f"sb_fwd_{l}", proj)
    qg, kg = sm["q_norm_g"][l][None], sm["k_norm_g"][l][None]
    o_c = _mem_fwd(f"mem_fwd_{l}", proj, mem_kv, qg, kg)
    bias = _bias_rows(sm["sgu_b"][l])
    y = _gate_fwd(f"gate_fwd_{l}", proj, o_b, o_c, sm["sgu_ln_g"][l][None], sm["sgu_ln_b"][l][None], sm["sgu_w"][l], bias)
    tn_o = 512
    x_next = _matmul(
        f"out_proj_{l}", y, w_out_all, grid=(s_len // tm, D_MODEL // tn_o, N_CHIPS),
        a_spec=pl.BlockSpec((tm, ROW_SHARD), lambda i, j, k: (i, k)),
        b_spec=pl.BlockSpec((None, ROW_SHARD, tn_o), lambda i, j, k: (k, 0, j)),
        o_spec=pl.BlockSpec((tm, tn_o), lambda i, j, k: (i, j)),
        out_shape=jax.ShapeDtypeStruct((s_len, D_MODEL), F32), dims=NN,
        res=x, res_spec=pl.BlockSpec((tm, tn_o), lambda i, j, k: (i, j)))
    saved = dict(x=x, h=h, proj=proj, mem_h=mem_h, mem_kv=mem_kv, o_b=o_b, o_c=o_c, y=y, bias=bias,
                 weights=(w_in_all, w_kv_all, w_out_all))
    return x_next, saved


def _layer_bwd(l, dxo, dxo_b, mem, sm, saved, on_weight_grads=None):
    s_len = dxo.shape[0]
    m_len = mem.shape[0]
    proj, y, h, mem_h, mem_kv = saved["proj"], saved["y"], saved["h"], saved["mem_h"], saved["mem_kv"]
    w_in_all, w_kv_all, w_out_all = saved["weights"]
    tm = min(1024, s_len)
    tk = min(1024, s_len)
    g_out = _matmul(
        f"d_w_out_{l}", y, dxo_b, grid=(N_CHIPS, D_MODEL // 1024, s_len // tk),
        a_spec=pl.BlockSpec((tk, ROW_SHARD), lambda i, j, k: (k, i)),
        b_spec=pl.BlockSpec((tk, 1024), lambda i, j, k: (k, j)),
        o_spec=pl.BlockSpec((None, ROW_SHARD, 1024), lambda i, j, k: (i, 0, j)),
        out_shape=jax.ShapeDtypeStruct((N_CHIPS, ROW_SHARD, D_MODEL), F32), dims=TN)
    dy = _matmul(
        f"d_y_{l}", dxo_b, w_out_all, grid=(s_len // tm, N_CHIPS, 1),
        a_spec=pl.BlockSpec((tm, D_MODEL), lambda i, j, k: (i, 0)),
        b_spec=pl.BlockSpec((None, ROW_SHARD, D_MODEL), lambda i, j, k: (j, 0, 0)),
        o_spec=pl.BlockSpec((tm, ROW_SHARD), lambda i, j, k: (i, j)),
        out_shape=jax.ShapeDtypeStruct((s_len, D_MODEL), F32), dims=NT)
    qg, kg = sm["q_norm_g"][l][None], sm["k_norm_g"][l][None]
    dq_c, dmk, dmv, dqg, dkg = _mem_bwd(f"mem_bwd_{l}", proj, mem_kv, qg, kg, dy)
    dqkv = _sb_bwd(f"sb_bwd_{l}", proj, dy)
    w_s = sm["sgu_w"][l]
    dproj, dws, dbias, dlng, dlnb = _gate_bwd(
        f"gate_bwd_{l}", proj, dy, saved["o_b"], saved["o_c"], dqkv, dq_c, sm["sgu_ln_g"][l][None],
        sm["sgu_ln_b"][l][None], w_s, jnp.swapaxes(w_s, 1, 2), saved["bias"])
    tn = 768
    per = IN_SHARD // tn
    g_in = _matmul(
        f"d_w_in_{l}", h, dproj, grid=(D_MODEL // 1024, IN_WIDTH // tn, s_len // tk),
        a_spec=pl.BlockSpec((tk, 1024), lambda i, j, k: (k, i)),
        b_spec=pl.BlockSpec((tk, tn), lambda i, j, k: (k, j)),
        o_spec=pl.BlockSpec((None, 1024, tn), lambda i, j, k: (j // per, i, j % per)),
        out_shape=jax.ShapeDtypeStruct((N_CHIPS, D_MODEL, IN_SHARD), F32), dims=TN)
    dkv_b = jnp.concatenate([dmk, dmv], axis=1).astype(BF16)
    g_kv = _matmul(
        f"d_w_kv_{l}", mem_h, dkv_b, grid=(N_CHIPS, 1, 1),
        a_spec=pl.BlockSpec((m_len, ROW_SHARD), lambda i, j, k: (0, i)),
        b_spec=pl.BlockSpec((m_len, 2 * D_C), lambda i, j, k: (0, 0)),
        o_spec=pl.BlockSpec((None, ROW_SHARD, 2 * D_C), lambda i, j, k: (i, 0, 0)),
        out_shape=jax.ShapeDtypeStruct((N_CHIPS, ROW_SHARD, 2 * D_C), F32), dims=TN)
    token = None if on_weight_grads is None else on_weight_grads([g_in, g_kv, g_out])
    dh = _matmul(
        f"d_h_{l}", dproj, w_in_all, grid=(s_len // tm, D_MODEL // 1024, N_CHIPS),
        a_spec=pl.BlockSpec((tm, IN_SHARD), lambda i, j, k: (i, k)),
        b_spec=pl.BlockSpec((None, 1024, IN_SHARD), lambda i, j, k: (k, j, 0)),
        o_spec=pl.BlockSpec((tm, 1024), lambda i, j, k: (i, j)),
        out_shape=jax.ShapeDtypeStruct((s_len, D_MODEL), F32), dims=NT, after=token)
    dx, dx_b, dng = _rms_bwd(f"rms_bwd_{l}", saved["x"], dh, dxo, sm["norm_g"][l][None], min(256, s_len))
    d_mem_h = _matmul(
        f"d_mem_h_{l}", dkv_b, w_kv_all, grid=(1, N_CHIPS, 1),
        a_spec=pl.BlockSpec((m_len, 2 * D_C), lambda i, j, k: (0, 0)),
        b_spec=pl.BlockSpec((None, ROW_SHARD, 2 * D_C), lambda i, j, k: (j, 0, 0)),
        o_spec=pl.BlockSpec((m_len, ROW_SHARD), lambda i, j, k: (0, j)),
        out_shape=jax.ShapeDtypeStruct((m_len, D_MODEL), F32), dims=NT)
    dmng = _rms_gain_grad(f"mem_rms_bwd_{l}", mem, d_mem_h)
    dsgu_b = dbias[:, :A_GROUPS].T
    small = dict(norm_g=dng[0], sgu_ln_g=dlng[0], sgu_ln_b=dlnb[0], sgu_w=dws, sgu_b=dsgu_b, mem_norm_g=dmng[0],
                 q_norm_g=dqg[0], k_norm_g=dkg[0])
    return dx, dx_b, small, g_in, g_kv, g_out


SMALL_NAMES = ("norm_g", "sgu_ln_g", "sgu_ln_b", "sgu_w", "sgu_b", "mem_norm_g", "q_norm_g", "k_norm_g")


def _local_step(x, mem, target, sm, w_all):
    saved = []
    cur = x
    for l in range(DEPTH):
        cur, sv = _layer_fwd(l, cur, mem, sm, w_all[l][0], lambda proj, l=l: (w_all[l][1], w_all[l][2], None))
        saved.append(sv)
    dxo, dxo_b, loss = _loss_and_grad("loss", cur, target, min(256, x.shape[0]))
    small, big = [None] * DEPTH, [None] * DEPTH
    for l in reversed(range(DEPTH)):
        dxo, dxo_b, small[l], *big[l] = _layer_bwd(l, dxo, dxo_b, mem, sm, saved[l])
    small = {k: jnp.stack([small[l][k] for l in range(DEPTH)]) for k in SMALL_NAMES}
    return loss, dxo, small, big


def _place():
    x, y, c = lax.axis_index("x"), lax.axis_index("y"), lax.axis_index("c")
    return x, y, c


def _other_chips(x, y):
    return [(1 - x, y, 2 * (1 - x) + y), (x, 1 - y, 2 * x + 1 - y), (1 - x, 1 - y, 2 * (1 - x) + 1 - y)]


AG_CHUNKS = 4
D2D_CHUNKS = 8


def _place_index():
    return jnp.stack([2 * lax.axis_index("x") + lax.axis_index("y"), lax.axis_index("c")]).astype(jnp.int32)


def _cast_into_slot(name, w, l, place):
    _, rows, cols = w.shape
    tr = min(256, rows)

    def body(p_ref, w_ref, o_ref):
        o_ref[...] = w_ref[...].astype(BF16)

    return pl.pallas_call(
        body, name=name,
        grid_spec=pltpu.PrefetchScalarGridSpec(
            num_scalar_prefetch=1, grid=(rows // tr,),
            in_specs=[pl.BlockSpec((None, tr, cols), lambda i, p: (l, i, 0))],
            out_specs=pl.BlockSpec((None, tr, cols), lambda i, p: (p[0], i, 0))),
        out_shape=jax.ShapeDtypeStruct((N_CHIPS, rows, cols), BF16),
        compiler_params=_params(("parallel",)),
    )(place, w)


HBM = pl.BlockSpec(memory_space=pltpu.HBM)
SEM = pl.BlockSpec(memory_space=pltpu.SEMAPHORE)
DATAFLOW = pltpu.SideEffectType.DATAFLOW_SIDE_EFFECTING


def _in_hbm(a):
    return pltpu.with_memory_space_constraint(a, pltpu.HBM)


def _chip_copies_start(name, srcs, lands, make_copy, after=None):
    n_t = len(srcs)
    in_place = lands is None
    n_after = 0 if after is None else 1

    def body(*refs):
        src = refs[:n_t]
        k = (n_t if in_place else 2 * n_t) + n_after
        send_sems, recv_sems = refs[k], refs[k + 1]
        land = refs[k + 2:k + 2 + n_t] if in_place else refs[k + 2 + n_t:k + 2 + 2 * n_t]
        token = refs[-1]
        x, y, c = _place()
        me = 2 * x + y
        for t in range(n_t):
            for px, py, pk in _other_chips(x, y):
                s, d = make_copy(src[t], land[t], me, pk, c)
                pltpu.make_async_remote_copy(
                    src_ref=s, dst_ref=d, send_sem=send_sems.at[t], recv_sem=recv_sems.at[t],
                    device_id=(px, py, c), device_id_type=MESH).start()
        token[...] = jnp.zeros_like(token)

    bufs = list(srcs) if in_place else list(srcs) + list(lands)
    outs = pl.pallas_call(
        body, name=name,
        in_specs=[HBM] * len(bufs) + [ANY] * n_after,
        out_specs=[SEM, SEM] + [HBM] * len(bufs) + [pl.BlockSpec(memory_space=pltpu.VMEM)],
        out_shape=[pltpu.SemaphoreType.DMA((n_t,)), pltpu.SemaphoreType.DMA((n_t,))]
        + [pltpu.HBM(b.shape, b.dtype) for b in bufs] + [jax.ShapeDtypeStruct((8, 128), F32)],
        input_output_aliases={i: 2 + i for i in range(len(bufs))},
        compiler_params=pltpu.CompilerParams(has_side_effects=DATAFLOW),
    )(*[_in_hbm(b) for b in bufs], *([] if after is None else [after]))
    return outs[0], outs[1], list(outs[2:2 + len(bufs)]), outs[-1]


def _chip_copies_wait(name, send_sems, recv_sems, bufs, sent, landed, after):
    n_b = len(bufs)

    def body(*refs):
        buf = refs[:n_b]
        send_ref, recv_ref = refs[n_b], refs[n_b + 1]
        x, y, c = _place()
        for t, (s, d) in enumerate(zip(sent(buf), landed(buf))):
            out = pltpu.make_async_remote_copy(src_ref=s, dst_ref=s, send_sem=send_ref.at[t], recv_sem=recv_ref.at[t],
                                               device_id=(x, y, c), device_id_type=MESH)
            out.wait_send()
            arrived = pltpu.make_async_remote_copy(src_ref=d, dst_ref=d, send_sem=send_ref.at[t],
                                                   recv_sem=recv_ref.at[t], device_id=(x, y, c), device_id_type=MESH)
            arrived.wait_recv()

    return pl.pallas_call(
        body, name=name,
        in_specs=[HBM] * n_b + [SEM, SEM, ANY], out_specs=[HBM] * n_b,
        out_shape=[pltpu.HBM(b.shape, b.dtype) for b in bufs],
        input_output_aliases={i: i for i in range(n_b)},
        compiler_params=pltpu.CompilerParams(has_side_effects=DATAFLOW),
    )(*bufs, send_sems, recv_sems, after)


def _gather_start(name, bufs, after=None):
    def make_copy(src, land, me, pk, c):
        hr = src.shape[1] // 2
        return src.at[me, pl.ds(c * hr, hr)], land.at[me, pl.ds(c * hr, hr)]

    return _chip_copies_start(name, bufs, None, make_copy, after)


def _gather_wait(name, send_sems, recv_sems, bufs, after):
    def three_halves(buf):
        return [b.at[pl.ds(0, 3), pl.ds(0, b.shape[1] // 2)] for b in buf]

    return _chip_copies_wait(name, send_sems, recv_sems, bufs, three_halves, three_halves, after)


def _gather_forward(name, bufs):
    n_t = len(bufs)
    n = 3 * n_t * D2D_CHUNKS

    def body(*refs):
        mine, buf = refs[:n_t], refs[n_t:2 * n_t]
        send_sems, recv_sems = refs[2 * n_t:]
        x, y, c = _place()
        chips = _other_chips(x, y)

        def piece(ref, t, slot, core, q):
            hr = ref[t].shape[1] // 2
            cr = hr // D2D_CHUNKS
            return ref[t].at[slot, pl.ds(core * hr + q * cr, cr)]

        copies = []
        for q in range(D2D_CHUNKS):
            for t in range(n_t):
                for j, (_, _, pk) in enumerate(chips):
                    s = (t * 3 + j) * D2D_CHUNKS + q
                    cp = pltpu.make_async_remote_copy(
                        src_ref=piece(mine, t, pk, c, q), dst_ref=piece(buf, t, pk, c, q), send_sem=send_sems.at[s],
                        recv_sem=recv_sems.at[s], device_id=(x, y, 1 - c), device_id_type=MESH)
                    cp.start()
                    copies.append(cp)
        for q in range(D2D_CHUNKS):
            for t in range(n_t):
                for j, (_, _, pk) in enumerate(chips):
                    s = (t * 3 + j) * D2D_CHUNKS + q
                    theirs = piece(buf, t, pk, 1 - c, q)
                    pltpu.make_async_remote_copy(
                        src_ref=theirs, dst_ref=theirs, send_sem=send_sems.at[s], recv_sem=recv_sems.at[s],
                        device_id=(x, y, 1 - c), device_id_type=MESH).wait_recv()
        for cp in copies:
            cp.wait_send()

    return pl.pallas_call(
        body, name=name,
        in_specs=[ANY] * n_t, out_specs=[ANY] * n_t,
        out_shape=[jax.ShapeDtypeStruct(b.shape, b.dtype) for b in bufs],
        input_output_aliases={t: t for t in range(n_t)},
        scratch_shapes=[pltpu.SemaphoreType.DMA((n,)), pltpu.SemaphoreType.DMA((n,))],
        compiler_params=pltpu.CompilerParams(has_side_effects=True),
    )(*bufs)


def _core_exchange(name, grads):
    n_t = len(grads)
    n = n_t * D2D_CHUNKS

    def body(*refs):
        src, theirs = refs[:n_t], refs[n_t:2 * n_t]
        send_sems, recv_sems = refs[2 * n_t:]
        x, y, c = _place()
        copies = []
        for q in range(D2D_CHUNKS):
            for t in range(n_t):
                hr = src[t].shape[1] // 2
                cr = hr // D2D_CHUNKS
                s = t * D2D_CHUNKS + q
                cp = pltpu.make_async_remote_copy(
                    src_ref=src[t].at[:, pl.ds((1 - c) * hr + q * cr, cr)],
                    dst_ref=theirs[t].at[:, pl.ds(q * cr, cr)],
                    send_sem=send_sems.at[s], recv_sem=recv_sems.at[s], device_id=(x, y, 1 - c), device_id_type=MESH)
                cp.start()
                copies.append(cp)
        for cp in copies:
            cp.wait()

    half = [jax.ShapeDtypeStruct((g.shape[0], g.shape[1] // 2, g.shape[2]), g.dtype) for g in grads]
    return pl.pallas_call(
        body, name=name,
        in_specs=[ANY] * n_t, out_specs=[ANY] * n_t, out_shape=half,
        scratch_shapes=[pltpu.SemaphoreType.DMA((n,)), pltpu.SemaphoreType.DMA((n,))],
        compiler_params=pltpu.CompilerParams(has_side_effects=True),
    )(*grads)


def _add_to_bf16(name, full, theirs, place):
    chips, rows, cols = theirs.shape
    tr = min(256, rows)
    per = rows // tr

    def body(p_ref, a_ref, b_ref, o_ref):
        o_ref[...] = (a_ref[...] + b_ref[...]).astype(BF16)

    blk = pl.BlockSpec((None, tr, cols), lambda k, i, p: (k, i, 0))
    return pl.pallas_call(
        body, name=name,
        grid_spec=pltpu.PrefetchScalarGridSpec(
            num_scalar_prefetch=1, grid=(chips, per),
            in_specs=[pl.BlockSpec((None, tr, cols), lambda k, i, p: (k, p[1] * per + i, 0)), blk],
            out_specs=blk),
        out_shape=jax.ShapeDtypeStruct(theirs.shape, BF16), compiler_params=_params(("parallel",) * 2),
    )(place, full, theirs)


def _chip_exchange_start(name, parts):
    lands = [lax.empty(p.shape, p.dtype) for p in parts]
    return _chip_copies_start(name, parts, lands, lambda src, land, me, pk, c: (src.at[pk], land.at[me]))


def _chip_exchange_wait(name, send_sems, recv_sems, bufs, after):
    n_t = len(bufs) // 2
    return _chip_copies_wait(name, send_sems, recv_sems, bufs,
                             lambda buf: [b.at[pl.ds(0, 3)] for b in buf[:n_t]],
                             lambda buf: [b.at[pl.ds(0, 3)] for b in buf[n_t:]], after)


def _sum_chips(name, parts, landed, place, l, stacked):
    chips, rows, cols = landed.shape
    tr = min(256, rows)
    per = rows // tr

    def body(p_ref, own_ref, *refs):
        land, o_ref = refs[:chips], refs[-1]
        tot = None
        for k in range(chips):
            term = jnp.where(p_ref[0] == k, own_ref[...], land[k][...]).astype(F32)
            tot = term if tot is None else tot + term
        o_ref[...] = tot

    def from_chip(k):
        return pl.BlockSpec((None, tr, cols), lambda i, p: (jnp.where(p[0] == k, (k + 1) % chips, k), i, 0))

    in_specs = [pl.BlockSpec((None, tr, cols), lambda i, p: (p[0], i, 0))] + [from_chip(k) for k in range(chips)]
    args = [parts] + [landed] * chips
    aliases = {}
    if stacked is not None:
        in_specs.append(ANY)
        args.append(stacked)
        aliases = {len(args): 0}
    return pl.pallas_call(
        body, name=name,
        grid_spec=pltpu.PrefetchScalarGridSpec(
            num_scalar_prefetch=1, grid=(per,), in_specs=in_specs,
            out_specs=pl.BlockSpec((None, tr, cols), lambda i, p: (l, p[1] * per + i, 0))),
        out_shape=jax.ShapeDtypeStruct((DEPTH, 2 * rows, cols), F32), input_output_aliases=aliases,
        compiler_params=_params(("parallel",)),
    )(place, *args)


def _core_share(bufs):
    n_t = len(bufs)
    n = n_t * D2D_CHUNKS

    def body(*refs):
        mine, buf = refs[:n_t], refs[n_t:2 * n_t]
        send_sems, recv_sems = refs[2 * n_t:]
        x, y, c = _place()

        def piece(ref, t, core, q):
            hr = ref[t].shape[1] // 2
            cr = hr // D2D_CHUNKS
            return ref[t].at[:, pl.ds(core * hr + q * cr, cr)]

        copies = []
        for q in range(D2D_CHUNKS):
            for t in range(n_t):
                s = t * D2D_CHUNKS + q
                cp = pltpu.make_async_remote_copy(
                    src_ref=piece(mine, t, c, q), dst_ref=piece(buf, t, c, q), send_sem=send_sems.at[s],
                    recv_sem=recv_sems.at[s], device_id=(x, y, 1 - c), device_id_type=MESH)
                cp.start()
                copies.append(cp)
        for q in range(D2D_CHUNKS):
            for t in range(n_t):
                s = t * D2D_CHUNKS + q
                theirs = piece(buf, t, 1 - c, q)
                pltpu.make_async_remote_copy(
                    src_ref=theirs, dst_ref=theirs, send_sem=send_sems.at[s], recv_sem=recv_sems.at[s],
                    device_id=(x, y, 1 - c), device_id_type=MESH).wait_recv()
        for cp in copies:
            cp.wait_send()

    return pl.pallas_call(
        body, name="grad_core_share",
        in_specs=[ANY] * n_t, out_specs=[ANY] * n_t,
        out_shape=[jax.ShapeDtypeStruct(b.shape, b.dtype) for b in bufs],
        input_output_aliases={t: t for t in range(n_t)},
        scratch_shapes=[pltpu.SemaphoreType.DMA((n,)), pltpu.SemaphoreType.DMA((n,))],
        compiler_params=pltpu.CompilerParams(has_side_effects=True),
    )(*bufs)


N_DEV = 8


def _all_reduce_small(vec):
    rows, lanes = vec.shape

    def body(v_ref, o_ref, gath_ref, send_sems, recv_sems):
        x, y, c = _place()
        me = 4 * x + 2 * y + c
        gath_ref[me] = v_ref[...]
        peers = [(x, y, 1 - c)]
        for px, py, _ in _other_chips(x, y):
            peers += [(px, py, c), (px, py, 1 - c)]
        copies = []
        for j, peer in enumerate(peers):
            cp = pltpu.make_async_remote_copy(
                src_ref=v_ref, dst_ref=gath_ref.at[me], send_sem=send_sems.at[j], recv_sem=recv_sems.at[j],
                device_id=peer, device_id_type=MESH)
            cp.start()
            copies.append(cp)
        for j, (px, py, pc) in enumerate(peers):
            slot = gath_ref.at[4 * px + 2 * py + pc]
            pltpu.make_async_remote_copy(
                src_ref=v_ref, dst_ref=slot, send_sem=send_sems.at[j], recv_sem=recv_sems.at[j],
                device_id=(px, py, pc), device_id_type=MESH).wait_recv()
        tot = gath_ref[0]
        for k in range(1, N_DEV):
            tot = tot + gath_ref[k]
        o_ref[...] = tot
        for cp in copies:
            cp.wait_send()

    vm = pl.BlockSpec(memory_space=pltpu.VMEM)
    return pl.pallas_call(
        body, name="small_all_reduce", in_specs=[vm], out_specs=vm,
        out_shape=jax.ShapeDtypeStruct((rows, lanes), F32),
        scratch_shapes=[pltpu.VMEM((N_DEV, rows, lanes), F32), pltpu.SemaphoreType.DMA((N_DEV - 1,)),
                        pltpu.SemaphoreType.DMA((N_DEV - 1,))],
        compiler_params=pltpu.CompilerParams(has_side_effects=True, vmem_limit_bytes=48 * MIB),
    )(vec)


def _adamw(name, w, g, m, v):
    rows, cols = w.shape
    tr = rows
    for cand in (256, 128, 64, 32, 16, 8):
        if rows % cand == 0:
            tr = cand
            break
    c1 = 1.0 - ADAM_B1 ** ADAM_STEP
    c2 = 1.0 - ADAM_B2 ** ADAM_STEP

    def body(w_ref, g_ref, m_ref, v_ref, d_ref, nm_ref, nv_ref):
        gv = g_ref[...]
        nm = ADAM_B1 * m_ref[...] + (1.0 - ADAM_B1) * gv
        nv = ADAM_B2 * v_ref[...] + (1.0 - ADAM_B2) * (gv * gv)
        nm_ref[...] = nm
        nv_ref[...] = nv
        d_ref[...] = -ADAM_LR * ((nm / c1) / (jnp.sqrt(nv / c2) + ADAM_EPS) + ADAM_WD * w_ref[...])

    blk = pl.BlockSpec((tr, cols), lambda i: (i, 0))
    out = jax.ShapeDtypeStruct((rows, cols), F32)
    return pl.pallas_call(
        body, name=name, grid=(rows // tr,), in_specs=[blk] * 4, out_specs=[blk] * 3, out_shape=[out] * 3,
        compiler_params=_params(("parallel",)),
    )(w, g, m, v)


def _pack_small(parts):
    flat = jnp.concatenate([parts[k].reshape(-1) for k in SMALL_NAMES])
    n = flat.shape[0]
    rows = -(-n // (256 * 128)) * 256
    return jnp.pad(flat, (0, rows * 128 - n)).reshape(rows, 128)


def _unpack_small(packed, like):
    flat = packed.reshape(-1)
    out, off = {}, 0
    for k in SMALL_NAMES:
        n = like[k].size
        out[k] = flat[off:off + n].reshape(like[k].shape)
        off += n
    return out


WEIGHT_ORDER = ("norm_g", "w_in", "sgu_ln_g", "sgu_ln_b", "sgu_w", "sgu_b", "mem_norm_g", "w_mem_kv", "q_norm_g",
                "k_norm_g", "w_out")


def kernel(x, mem, norm_g, w_in, sgu_ln_g, sgu_ln_b, sgu_w, sgu_b, mem_norm_g, w_mem_kv, q_norm_g, k_norm_g, w_out, loss_target, m_norm_g, m_w_in, m_sgu_ln_g, m_sgu_ln_b, m_sgu_w, m_sgu_b, m_mem_norm_g, m_w_mem_kv, m_q_norm_g, m_k_norm_g, m_w_out, v_norm_g, v_w_in, v_sgu_ln_g, v_sgu_ln_b, v_sgu_w, v_sgu_b, v_mem_norm_g, v_w_mem_kv, v_q_norm_g, v_k_norm_g, v_w_out):
    weights = dict(norm_g=norm_g, w_in=w_in, sgu_ln_g=sgu_ln_g, sgu_ln_b=sgu_ln_b, sgu_w=sgu_w, sgu_b=sgu_b,
                   mem_norm_g=mem_norm_g, w_mem_kv=w_mem_kv, q_norm_g=q_norm_g, k_norm_g=k_norm_g, w_out=w_out)
    mom_m = dict(norm_g=m_norm_g, w_in=m_w_in, sgu_ln_g=m_sgu_ln_g, sgu_ln_b=m_sgu_ln_b, sgu_w=m_sgu_w, sgu_b=m_sgu_b,
                 mem_norm_g=m_mem_norm_g, w_mem_kv=m_w_mem_kv, q_norm_g=m_q_norm_g, k_norm_g=m_k_norm_g, w_out=m_w_out)
    mom_v = dict(norm_g=v_norm_g, w_in=v_w_in, sgu_ln_g=v_sgu_ln_g, sgu_ln_b=v_sgu_ln_b, sgu_w=v_sgu_w, sgu_b=v_sgu_b,
                 mem_norm_g=v_mem_norm_g, w_mem_kv=v_w_mem_kv, q_norm_g=v_q_norm_g, k_norm_g=v_k_norm_g, w_out=v_w_out)
    big = ("w_in", "w_mem_kv", "w_out")
    sm = {k: weights[k] for k in SMALL_NAMES}

    place = _place_index()
    xs, mems, target = x[0], mem[0], loss_target[0]

    slots = [[_cast_into_slot(f"cast_{k}_{l}", weights[k], l, place) for k in big] for l in range(DEPTH)]
    saved = [None] * DEPTH

    def gathered(tag, flight, after):
        send_sems, recv_sems, bufs, _ = flight
        return _gather_forward(f"gather_forward_{tag}", _gather_wait(f"gather_wait_{tag}", send_sems, recv_sems,
                                                                      bufs, after))

    flights = {}

    def start_gather(l, after=None):
        flights[l, "in"] = _gather_start(f"gather_start_{l}_in", slots[l][:1], after)
        flights[l, "rest"] = _gather_start(f"gather_start_{l}_rest", slots[l][1:], flights[l, "in"][3])
        return flights[l, "rest"][3]

    start_gather(0)
    cur = xs
    for l in range(DEPTH):
        (w_in_all,) = gathered(f"{l}_in", flights[l, "in"], cur)

        def rest(proj, l=l):
            w_kv_all, w_out_all = gathered(f"{l}_rest", flights[l, "rest"], proj)
            token = start_gather(l + 1, w_out_all) if l + 1 < DEPTH else None
            return w_kv_all, w_out_all, token

        cur, saved[l] = _layer_fwd(l, cur, mems, sm, w_in_all, rest)
    dxo, dxo_b, loss_part = _loss_and_grad("loss", cur, target, min(256, xs.shape[0]))
    loss = lax.psum(loss_part[0, 0], ("x", "y", "c"))

    small_g, flight = [None] * DEPTH, [None] * DEPTH
    for l in reversed(range(DEPTH)):
        def start_exchange(full, l=l):
            theirs = _core_exchange(f"grad_core_exchange_{l}", full)
            parts = [_add_to_bf16(f"grad_core_sum_{l}_{t}", full[t], theirs[t], place) for t in range(3)]
            *flight[l], token = _chip_exchange_start(f"grad_chip_start_{l}", parts)
            return token

        dxo, dxo_b, small_g[l], *_ = _layer_bwd(l, dxo, dxo_b, mems, sm, saved[l], on_weight_grads=start_exchange)
    grad_x = dxo
    halves = [None] * 3
    for l in reversed(range(DEPTH)):
        send_sems, recv_sems, bufs = flight[l]
        bufs = _chip_exchange_wait(f"grad_chip_wait_{l}", send_sems, recv_sems, bufs, grad_x)
        for t in range(3):
            halves[t] = _sum_chips(f"grad_chip_sum_{l}_{t}", bufs[t], bufs[3 + t], place, l, halves[t])
    big_g = dict(zip(big, _core_share(halves)))

    small_g = {k: jnp.stack([small_g[l][k] for l in range(DEPTH)]) for k in SMALL_NAMES}
    small_sum = _unpack_small(_all_reduce_small(_pack_small(small_g)), sm)

    grads, delta, new_m, new_v = {}, {}, {}, {}
    for k in big:
        shape = weights[k].shape
        two_d = (shape[0] * shape[1], shape[2])
        grads[k] = big_g[k]
        d, nm, nv = _adamw(f"adamw_{k}", weights[k].reshape(two_d), big_g[k].reshape(two_d),
                           mom_m[k].reshape(two_d), mom_v[k].reshape(two_d))
        delta[k], new_m[k], new_v[k] = d.reshape(shape), nm.reshape(shape), nv.reshape(shape)
    d, nm, nv = _adamw("adamw_small", _pack_small(sm), _pack_small(small_sum),
                       _pack_small({k: mom_m[k] for k in SMALL_NAMES}), _pack_small({k: mom_v[k] for k in SMALL_NAMES}))
    grads.update(small_sum)
    delta.update(_unpack_small(d, sm))
    new_m.update(_unpack_small(nm, sm))
    new_v.update(_unpack_small(nv, sm))
    return (loss, grad_x[None], *[grads[k] for k in WEIGHT_ORDER], *[delta[k] for k in WEIGHT_ORDER],
            *[new_m[k] for k in WEIGHT_ORDER], *[new_v[k] for k in WEIGHT_ORDER])
```

```python
import functools
import math

import jax
import jax.numpy as jnp
from jax import lax
from jax.experimental import pallas as pl
from jax.experimental.pallas import tpu as pltpu

F32 = jnp.float32
BF16 = jnp.bfloat16
MESH = pl.DeviceIdType.MESH

D_MODEL = 2048
DEPTH = 2
CHUNK = 128
D_A = 1024
A_GROUPS = 8
D_B = 512
D_C = 512
HEADS = 4
HEAD_DIM = 128
IN_WIDTH = 6144
N_CHIPS = 4
EPS = 1e-6
ATT_SCALE = 1.0 / math.sqrt(HEAD_DIM)

OFF_U, OFF_V, OFF_ZA = 0, 1024, 2048
OFF_QB, OFF_KB, OFF_VB, OFF_ZB = 3072, 3584, 4096, 4608
OFF_QC, OFF_ZC = 5120, 5632
OFF_YB, OFF_YC = 1024, 1536

ADAM_LR = 0.001
ADAM_B1 = 0.9
ADAM_B2 = 0.999
ADAM_EPS = 1e-08
ADAM_WD = 0.01
ADAM_STEP = 10

MIB = 1024 * 1024
ANY = pl.BlockSpec(memory_space=pl.ANY)


def _params(semantics=None, vmem_mb=48):
    return pltpu.CompilerParams(dimension_semantics=semantics, vmem_limit_bytes=vmem_mb * MIB)


def _gelu(x):
    return 0.5 * x * (1.0 + lax.erf(x * (1.0 / math.sqrt(2.0))))


def _gelu_grad(x):
    cdf = 0.5 * (1.0 + lax.erf(x * (1.0 / math.sqrt(2.0))))
    pdf = jnp.exp(-0.5 * x * x) * (1.0 / math.sqrt(2.0 * math.pi))
    return cdf + x * pdf


def _sigmoid(x):
    return 1.0 / (1.0 + jnp.exp(-x))


def _silu_and_grad(z):
    s = _sigmoid(z)
    return z * s, s * (1.0 + z * (1.0 - s))


def _split_bf16(x):
    hi = x.astype(BF16)
    lo = (x - hi.astype(F32)).astype(BF16)
    return hi, lo


def _dot(a, b, dims):
    return lax.dot_general(a, b, (dims, ((), ())), preferred_element_type=F32)


NN = ((1,), (0,))
NT = ((1,), (1,))
TN = ((0,), (0,))


def _matmul(name, a, b, *, grid, a_spec, b_spec, o_spec, out_shape, dims, res=None, res_spec=None, after=None,
            vmem_mb=48):
    nk = grid[2]
    n_in = 2 + (res is not None) + (after is not None)

    def body(*refs):
        a_ref, b_ref = refs[0], refs[1]
        r_ref = refs[2] if res is not None else None
        o_ref = refs[n_in]
        part = _dot(a_ref[...], b_ref[...], dims)
        if nk == 1:
            if r_ref is not None:
                part = part + r_ref[...]
            o_ref[...] = part.astype(o_ref.dtype)
            return
        acc_ref = refs[n_in + 1]
        k = pl.program_id(2)

        @pl.when(k == 0)
        def _():
            acc_ref[...] = part

        @pl.when(k > 0)
        def _():
            acc_ref[...] += part

        @pl.when(k == nk - 1)
        def _():
            tot = acc_ref[...]
            if r_ref is not None:
                tot = tot + r_ref[...]
            o_ref[...] = tot.astype(o_ref.dtype)

    in_specs = [a_spec, b_spec]
    args = [a, b]
    if res is not None:
        in_specs.append(res_spec)
        args.append(res)
    if after is not None:
        in_specs.append(ANY)
        args.append(after)
    acc_shape = tuple(d for d in o_spec.block_shape if d is not None)
    scratch = [pltpu.VMEM(acc_shape, F32)] if nk > 1 else []
    return pl.pallas_call(
        body, name=name, grid=grid, in_specs=in_specs, out_specs=o_spec, out_shape=out_shape,
        scratch_shapes=scratch,
        compiler_params=_params(("parallel", "parallel", "arbitrary"), vmem_mb),
    )(*args)


def _rms_fwd(name, x, g, tr, after=None):
    rows, d = x.shape

    def body(x_ref, g_ref, *refs):
        h_ref = refs[-1]
        xv = x_ref[...]
        r = lax.rsqrt(jnp.mean(xv * xv, axis=-1, keepdims=True) + EPS)
        h_ref[...] = (xv * r * g_ref[...]).astype(BF16)

    return pl.pallas_call(
        body, name=name, grid=(rows // tr,),
        in_specs=[pl.BlockSpec((tr, d), lambda i: (i, 0)), pl.BlockSpec((1, d), lambda i: (0, 0))]
        + ([] if after is None else [ANY]),
        out_specs=pl.BlockSpec((tr, d), lambda i: (i, 0)),
        out_shape=jax.ShapeDtypeStruct((rows, d), BF16),
        compiler_params=_params(("parallel",)),
    )(x, g, *([] if after is None else [after]))


def _rms_bwd(name, x, dh, dres, g, tr):
    rows, d = x.shape

    def body(x_ref, dh_ref, dres_ref, g_ref, dx_ref, dxb_ref, dg_ref):
        xv = x_ref[...]
        r = lax.rsqrt(jnp.mean(xv * xv, axis=-1, keepdims=True) + EPS)
        xhat = xv * r
        dhv = dh_ref[...]
        dxh = dhv * g_ref[...]
        dx = r * (dxh - xhat * jnp.mean(dxh * xhat, axis=-1, keepdims=True)) + dres_ref[...]
        dx_ref[...] = dx
        dxb_ref[...] = dx.astype(BF16)
        part = jnp.sum(dhv * xhat, axis=0, keepdims=True)

        @pl.when(pl.program_id(0) == 0)
        def _():
            dg_ref[...] = part

        @pl.when(pl.program_id(0) > 0)
        def _():
            dg_ref[...] += part

    blk = pl.BlockSpec((tr, d), lambda i: (i, 0))
    vec = pl.BlockSpec((1, d), lambda i: (0, 0))
    return pl.pallas_call(
        body, name=name, grid=(rows // tr,), in_specs=[blk, blk, blk, vec], out_specs=[blk, blk, vec],
        out_shape=[jax.ShapeDtypeStruct((rows, d), F32), jax.ShapeDtypeStruct((rows, d), BF16),
                   jax.ShapeDtypeStruct((1, d), F32)],
        compiler_params=_params(("arbitrary",)),
    )(x, dh, dres, g)


def _rms_gain_grad(name, x, dh):
    rows, d = x.shape

    def body(x_ref, dh_ref, dg_ref):
        xv = x_ref[...]
        r = lax.rsqrt(jnp.mean(xv * xv, axis=-1, keepdims=True) + EPS)
        dg_ref[...] = jnp.sum(dh_ref[...] * xv * r, axis=0, keepdims=True)

    return pl.pallas_call(
        body, name=name, out_shape=jax.ShapeDtypeStruct((1, d), F32), compiler_params=_params(None),
    )(x, dh)


def _loss_and_grad(name, y, target, tr):
    rows, d = y.shape
    n = rows // tr

    def body(y_ref, t_ref, dx_ref, dxb_ref, loss_ref, acc_ref):
        e = y_ref[...] - t_ref[...]
        dx = e * (1.0 / d)
        dx_ref[...] = dx
        dxb_ref[...] = dx.astype(BF16)
        part = jnp.sum(e * e, axis=0, keepdims=True)
        i = pl.program_id(0)

        @pl.when(i == 0)
        def _():
            acc_ref[...] = part

        @pl.when(i > 0)
        def _():
            acc_ref[...] += part

        @pl.when(i == n - 1)
        def _():
            loss_ref[...] = jnp.sum(acc_ref[...], axis=-1, keepdims=True) * (0.5 / d)

    blk = pl.BlockSpec((tr, d), lambda i: (i, 0))
    return pl.pallas_call(
        body, name=name, grid=(n,), in_specs=[blk, blk],
        out_specs=[blk, blk, pl.BlockSpec((1, 1), lambda i: (0, 0))],
        out_shape=[jax.ShapeDtypeStruct((rows, d), F32), jax.ShapeDtypeStruct((rows, d), BF16),
                   jax.ShapeDtypeStruct((1, 1), F32)],
        scratch_shapes=[pltpu.VMEM((1, d), F32)],
        compiler_params=_params(("arbitrary",)),
    )(y, target)


SB_T = 256


def _sb_scores(q, kblk):
    z = _dot(q, kblk, NT) * ATT_SCALE
    e = jnp.exp(-jnp.abs(z))
    sp = jnp.log1p(e)
    lb = jnp.minimum(z, 0.0) - sp
    l1 = lb - z
    return z, e, lb, l1


def _sb_fwd(name, proj):
    s_len = proj.shape[0]
    t = SB_T
    nq = s_len // t

    def body(q_ref, k_ref, v_ref, o_ref):
        i = pl.program_id(1)
        q = q_ref[...].astype(BF16)
        row = lax.broadcasted_iota(jnp.int32, (t, t), 0)
        col = lax.broadcasted_iota(jnp.int32, (t, t), 1)
        causal = col < row
        after_mat = (row > col).astype(BF16)

        def tile(kb, carry, acc, masked):
            start = pl.multiple_of(kb * t, t)
            kblk = k_ref[pl.ds(start, t), :].astype(BF16)
            vblk = v_ref[pl.ds(start, t), :].astype(BF16)
            _, _, lb, l1 = _sb_scores(q, kblk)
            if masked:
                l1 = jnp.where(causal, l1, 0.0)
            hi, lo = _split_bf16(l1)
            after = _dot(hi, after_mat, NN) + _dot(lo, after_mat, NN) + carry
            a = jnp.exp(lb + after)
            if masked:
                a = jnp.where(causal, a, 0.0)
            acc = acc + _dot(a.astype(BF16), vblk, NN)
            carry = carry + jnp.sum(l1, axis=-1, keepdims=True)
            return carry, acc

        carry, acc = tile(i, jnp.zeros((t, 1), F32), jnp.zeros((t, HEAD_DIM), F32), True)

        def step(n, state):
            return tile(i - 1 - n, state[0], state[1], False)

        carry, acc = lax.fori_loop(0, i, step, (carry, acc))
        o_ref[...] = acc

    cb = HEAD_DIM
    return pl.pallas_call(
        body, name=name, grid=(HEADS, nq),
        in_specs=[pl.BlockSpec((t, cb), lambda h, i: (i, OFF_QB // cb + h)),
                  pl.BlockSpec((s_len, cb), lambda h, i: (0, OFF_KB // cb + h)),
                  pl.BlockSpec((s_len, cb), lambda h, i: (0, OFF_VB // cb + h))],
        out_specs=pl.BlockSpec((t, cb), lambda h, i: (i, h)),
        out_shape=jax.ShapeDtypeStruct((s_len, D_B), F32),
        compiler_params=_params(("parallel", "arbitrary")),
    )(proj, proj, proj)


def _sb_bwd(name, proj, dy):
    s_len = proj.shape[0]
    t = SB_T
    nq = s_len // t

    def body(q_ref, k_ref, v_ref, z_ref, dy_ref, dq_ref, dk_ref, dv_ref, a_ref, s_ref):
        i = pl.program_id(1)

        @pl.when(i == 0)
        def _():
            dk_ref[...] = jnp.zeros_like(dk_ref)
            dv_ref[...] = jnp.zeros_like(dv_ref)

        q = q_ref[...].astype(BF16)
        silu_z, _ = _silu_and_grad(z_ref[...])
        do_b = (dy_ref[...] * silu_z).astype(BF16)
        row = lax.broadcasted_iota(jnp.int32, (t, t), 0)
        col = lax.broadcasted_iota(jnp.int32, (t, t), 1)
        causal = col < row
        after_mat = (row > col).astype(BF16)
        before_mat = (row < col).astype(BF16)

        def weights(kb, carry, masked):
            start = pl.multiple_of(kb * t, t)
            kblk = k_ref[pl.ds(start, t), :].astype(BF16)
            z, _, lb, l1 = _sb_scores(q, kblk)
            if masked:
                l1 = jnp.where(causal, l1, 0.0)
            hi, lo = _split_bf16(l1)
            after = _dot(hi, after_mat, NN) + _dot(lo, after_mat, NN) + carry
            a = jnp.exp(lb + after)
            if masked:
                a = jnp.where(causal, a, 0.0)
            a_ref[kb] = a
            s_ref[kb] = z
            return carry + jnp.sum(l1, axis=-1, keepdims=True)

        carry = weights(i, jnp.zeros((t, 1), F32), True)
        lax.fori_loop(0, i, lambda n, c: weights(i - 1 - n, c, False), carry)

        def grads(kb, carry, dq, masked):
            start = pl.multiple_of(kb * t, t)
            kblk = k_ref[pl.ds(start, t), :].astype(BF16)
            vblk = v_ref[pl.ds(start, t), :].astype(BF16)
            a = a_ref[kb]
            z = s_ref[kb]
            g = _dot(do_b, vblk, NT) * a
            ghi, glo = _split_bf16(g)
            prefix = _dot(ghi, before_mat, NN) + _dot(glo, before_mat, NN) + carry
            e = jnp.exp(-jnp.abs(z))
            inv = 1.0 / (1.0 + e)
            pos = z >= 0.0
            beta = jnp.where(pos, inv, e * inv)
            one_m_beta = jnp.where(pos, e * inv, inv)
            dz = (g * one_m_beta - prefix * beta) * ATT_SCALE
            if masked:
                dz = jnp.where(causal, dz, 0.0)
            dz_b = dz.astype(BF16)
            dq = dq + _dot(dz_b, kblk, NN)
            dk_ref[pl.ds(start, t), :] += _dot(dz_b, q, TN)
            dv_ref[pl.ds(start, t), :] += _dot(a.astype(BF16), do_b, TN)
            return carry + jnp.sum(g, axis=-1, keepdims=True), dq

        state = lax.fori_loop(0, i, lambda kb, st: grads(kb, st[0], st[1], False),
                              (jnp.zeros((t, 1), F32), jnp.zeros((t, HEAD_DIM), F32)))
        _, dq = grads(i, state[0], state[1], True)
        dq_ref[...] = dq

    cb = HEAD_DIM
    qblk = lambda off: pl.BlockSpec((t, cb), lambda h, i: (i, off // cb + h))
    full = lambda off: pl.BlockSpec((s_len, cb), lambda h, i: (0, off // cb + h))
    out = jax.ShapeDtypeStruct((s_len, D_B), F32)
    return pl.pallas_call(
        body, name=name, grid=(HEADS, nq),
        in_specs=[qblk(OFF_QB), full(OFF_KB), full(OFF_VB), qblk(OFF_ZB), qblk(OFF_YB)],
        out_specs=[qblk(0), full(0), full(0)],
        out_shape=[out, out, out],
        scratch_shapes=[pltpu.VMEM((nq, t, t), F32), pltpu.VMEM((nq, t, t), F32)],
        compiler_params=_params(("parallel", "arbitrary")),
    )(proj, proj, proj, proj, dy)


MEM_TQ = 512


def _qk_norm(x, g):
    r = lax.rsqrt(jnp.mean(x * x, axis=-1, keepdims=True) + EPS)
    xhat = x * r
    return xhat * g, xhat, r


def _qk_norm_bwd(dn, g, xhat, r):
    dxh = dn * g
    return r * (dxh - xhat * jnp.mean(dxh * xhat, axis=-1, keepdims=True))


def _mem_probs(q, mk, qg, kg):
    qn, qhat, rq = _qk_norm(q, qg)
    kn, khat, rk = _qk_norm(mk, kg)
    qn_b, kn_b = qn.astype(BF16), kn.astype(BF16)
    s = _dot(qn_b, kn_b, NT) * ATT_SCALE
    p = jnp.exp(s - jnp.max(s, axis=-1, keepdims=True))
    p = p / jnp.sum(p, axis=-1, keepdims=True)
    return p, qn_b, kn_b, qhat, rq, khat, rk


def _mem_fwd(name, proj, mem_kv, qg, kg):
    s_len = proj.shape[0]
    m_len = mem_kv.shape[0]
    tq = min(MEM_TQ, s_len)

    def body(q_ref, mk_ref, mv_ref, qg_ref, kg_ref, o_ref):
        p = _mem_probs(q_ref[...], mk_ref[...], qg_ref[...], kg_ref[...])[0]
        o_ref[...] = _dot(p.astype(BF16), mv_ref[...].astype(BF16), NN)

    cb = HEAD_DIM
    vec = pl.BlockSpec((1, cb), lambda h, i: (0, 0))
    return pl.pallas_call(
        body, name=name, grid=(HEADS, s_len // tq),
        in_specs=[pl.BlockSpec((tq, cb), lambda h, i: (i, OFF_QC // cb + h)),
                  pl.BlockSpec((m_len, cb), lambda h, i: (0, h)),
                  pl.BlockSpec((m_len, cb), lambda h, i: (0, HEADS + h)), vec, vec],
        out_specs=pl.BlockSpec((tq, cb), lambda h, i: (i, h)),
        out_shape=jax.ShapeDtypeStruct((s_len, D_C), F32),
        compiler_params=_params(("parallel", "parallel")),
    )(proj, mem_kv, mem_kv, qg, kg)


def _mem_bwd(name, proj, mem_kv, qg, kg, dy):
    s_len = proj.shape[0]
    m_len = mem_kv.shape[0]
    tq = min(MEM_TQ, s_len)

    def body(q_ref, mk_ref, mv_ref, qg_ref, kg_ref, z_ref, dy_ref, dq_ref, dmk_ref, dmv_ref, dqg_ref, dkg_ref):
        h, i = pl.program_id(0), pl.program_id(1)

        @pl.when(i == 0)
        def _():
            dmk_ref[...] = jnp.zeros_like(dmk_ref)
            dmv_ref[...] = jnp.zeros_like(dmv_ref)

        @pl.when((i == 0) & (h == 0))
        def _():
            dqg_ref[...] = jnp.zeros_like(dqg_ref)
            dkg_ref[...] = jnp.zeros_like(dkg_ref)

        qg, kg = qg_ref[...], kg_ref[...]
        p, qn_b, kn_b, qhat, rq, khat, rk = _mem_probs(q_ref[...], mk_ref[...], qg, kg)
        silu_z, _ = _silu_and_grad(z_ref[...])
        do_b = (dy_ref[...] * silu_z).astype(BF16)
        dmv_ref[...] += _dot(p.astype(BF16), do_b, TN)
        dp = _dot(do_b, mv_ref[...].astype(BF16), NT)
        ds = (p * (dp - jnp.sum(dp * p, axis=-1, keepdims=True)) * ATT_SCALE).astype(BF16)
        dqn = _dot(ds, kn_b, NN)
        dkn = _dot(ds, qn_b, TN)
        dq_ref[...] = _qk_norm_bwd(dqn, qg, qhat, rq)
        dmk_ref[...] += _qk_norm_bwd(dkn, kg, khat, rk)
        dqg_ref[...] += jnp.sum(dqn * qhat, axis=0, keepdims=True)
        dkg_ref[...] += jnp.sum(dkn * khat, axis=0, keepdims=True)

    cb = HEAD_DIM
    vec = pl.BlockSpec((1, cb), lambda h, i: (0, 0))
    qblk = lambda off: pl.BlockSpec((tq, cb), lambda h, i: (i, off // cb + h))
    memblk = lambda off: pl.BlockSpec((m_len, cb), lambda h, i: (0, off + h))
    return pl.pallas_call(
        body, name=name, grid=(HEADS, s_len // tq),
        in_specs=[qblk(OFF_QC), memblk(0), memblk(HEADS), vec, vec, qblk(OFF_ZC), qblk(OFF_YC)],
        out_specs=[qblk(0), memblk(0), memblk(0), vec, vec],
        out_shape=[jax.ShapeDtypeStruct((s_len, D_C), F32), jax.ShapeDtypeStruct((m_len, D_C), F32),
                   jax.ShapeDtypeStruct((m_len, D_C), F32), jax.ShapeDtypeStruct((1, cb), F32),
                   jax.ShapeDtypeStruct((1, cb), F32)],
        compiler_params=_params(("arbitrary", "arbitrary")),
    )(proj, mem_kv, mem_kv, qg, kg, proj, dy)


def _sgu_common(u_ref, v_ref, lng_ref, lnb_ref, w_ref, bias_ref):
    ug = _gelu(u_ref[...])
    vg = _gelu(v_ref[...])
    mu = jnp.mean(vg, axis=-1, keepdims=True)
    xc = vg - mu
    rstd = lax.rsqrt(jnp.mean(xc * xc, axis=-1, keepdims=True) + EPS)
    xhat = xc * rstd
    vn = xhat * lng_ref[...] + lnb_ref[...]
    vn_b = vn.astype(BF16)
    row = lax.broadcasted_iota(jnp.int32, (CHUNK, CHUNK), 0)
    col = lax.broadcasted_iota(jnp.int32, (CHUNK, CHUNK), 1)
    tril = row >= col
    mixed = []
    for g in range(A_GROUPS):
        w = jnp.where(tril, w_ref[g], 0.0).astype(BF16)
        sl = slice(g * CHUNK, (g + 1) * CHUNK)
        mixed.append(_dot(w, vn_b[:, sl], NN) + bias_ref[:, sl])
    return ug, xhat, rstd, vn_b, mixed, tril


def _gate_fwd(name, proj, o_b, o_c, lng, lnb, w_s, bias):
    s_len = proj.shape[0]

    def body(u_ref, v_ref, za_ref, zb_ref, zc_ref, ob_ref, oc_ref, lng_ref, lnb_ref, w_ref, bias_ref, y_ref):
        ug, _, _, _, mixed, _ = _sgu_common(u_ref, v_ref, lng_ref, lnb_ref, w_ref, bias_ref)
        sza, _ = _silu_and_grad(za_ref[...])
        gate = ug * sza
        for g in range(A_GROUPS):
            sl = slice(g * CHUNK, (g + 1) * CHUNK)
            y_ref[:, sl] = (gate[:, sl] * mixed[g]).astype(BF16)
        szb, _ = _silu_and_grad(zb_ref[...])
        y_ref[:, OFF_YB:OFF_YB + D_B] = (ob_ref[...] * szb).astype(BF16)
        szc, _ = _silu_and_grad(zc_ref[...])
        y_ref[:, OFF_YC:OFF_YC + D_C] = (oc_ref[...] * szc).astype(BF16)

    wide = lambda off: pl.BlockSpec((CHUNK, D_A), lambda i: (i, off // D_A))
    narrow = lambda off: pl.BlockSpec((CHUNK, D_B), lambda i: (i, off // D_B))
    vec = pl.BlockSpec((1, D_A), lambda i: (0, 0))
    return pl.pallas_call(
        body, name=name, grid=(s_len // CHUNK,),
        in_specs=[wide(OFF_U), wide(OFF_V), wide(OFF_ZA), narrow(OFF_ZB), narrow(OFF_ZC), narrow(0), narrow(0), vec, vec,
                  pl.BlockSpec((A_GROUPS, CHUNK, CHUNK), lambda i: (0, 0, 0)),
                  pl.BlockSpec((CHUNK, D_A), lambda i: (0, 0))],
        out_specs=pl.BlockSpec((CHUNK, D_MODEL), lambda i: (i, 0)),
        out_shape=jax.ShapeDtypeStruct((s_len, D_MODEL), BF16),
        compiler_params=_params(("parallel",)),
    )(proj, proj, proj, proj, proj, o_b, o_c, lng, lnb, w_s, bias)


def _gate_bwd(name, proj, dy, o_b, o_c, dqkv, dq_c, lng, lnb, w_s, w_s_t, bias):
    s_len = proj.shape[0]
    n = s_len // CHUNK
    dq_b, dk_b, dv_b = dqkv

    def body(u_ref, v_ref, za_ref, zb_ref, zc_ref, dya_ref, dyb_ref, dyc_ref, ob_ref, oc_ref, dq_ref, dk_ref, dv_ref,
             dqc_ref, lng_ref, lnb_ref, w_ref, wt_ref, bias_ref, dp_ref, dw_ref, dsb_ref, dlng_ref, dlnb_ref, dbias_ref):
        i = pl.program_id(0)

        @pl.when(i == 0)
        def _():
            dw_ref[...] = jnp.zeros_like(dw_ref)
            dbias_ref[...] = jnp.zeros_like(dbias_ref)
            dlng_ref[...] = jnp.zeros_like(dlng_ref)
            dlnb_ref[...] = jnp.zeros_like(dlnb_ref)

        ug, xhat, rstd, vn_b, mixed, tril = _sgu_common(u_ref, v_ref, lng_ref, lnb_ref, w_ref, bias_ref)
        za = za_ref[...]
        sza, dsza = _silu_and_grad(za)
        dya = dya_ref[...]
        mixed_all = jnp.concatenate(mixed, axis=-1)
        d_mixed = dya * ug * sza
        dp_ref[:, OFF_U:OFF_U + D_A] = (dya * mixed_all * sza * _gelu_grad(u_ref[...])).astype(BF16)
        dp_ref[:, OFF_ZA:OFF_ZA + D_A] = (dya * ug * mixed_all * dsza).astype(BF16)
        dbias_ref[...] += d_mixed
        dm_b = d_mixed.astype(BF16)
        triu = lax.broadcasted_iota(jnp.int32, (CHUNK, CHUNK), 0) <= lax.broadcasted_iota(jnp.int32, (CHUNK, CHUNK), 1)
        d_vn = []
        for g in range(A_GROUPS):
            sl = slice(g * CHUNK, (g + 1) * CHUNK)
            wt = jnp.where(triu, wt_ref[g], 0.0).astype(BF16)
            d_vn.append(_dot(wt, dm_b[:, sl], NN))
            dw_ref[g] += jnp.where(tril, _dot(dm_b[:, sl], vn_b[:, sl], NT), 0.0)
        d_vn = jnp.concatenate(d_vn, axis=-1)
        dlng_ref[...] += jnp.sum(d_vn * xhat, axis=0, keepdims=True)
        dlnb_ref[...] += jnp.sum(d_vn, axis=0, keepdims=True)
        dxh = d_vn * lng_ref[...]
        d_vg = rstd * (dxh - jnp.mean(dxh, axis=-1, keepdims=True)
                       - xhat * jnp.mean(dxh * xhat, axis=-1, keepdims=True))
        dp_ref[:, OFF_V:OFF_V + D_A] = (d_vg * _gelu_grad(v_ref[...])).astype(BF16)
        dp_ref[:, OFF_QB:OFF_QB + D_B] = dq_ref[...].astype(BF16)
        dp_ref[:, OFF_KB:OFF_KB + D_B] = dk_ref[...].astype(BF16)
        dp_ref[:, OFF_VB:OFF_VB + D_B] = dv_ref[...].astype(BF16)
        _, dszb = _silu_and_grad(zb_ref[...])
        dp_ref[:, OFF_ZB:OFF_ZB + D_B] = (dyb_ref[...] * ob_ref[...] * dszb).astype(BF16)
        dp_ref[:, OFF_QC:OFF_QC + D_C] = dqc_ref[...].astype(BF16)
        _, dszc = _silu_and_grad(zc_ref[...])
        dp_ref[:, OFF_ZC:OFF_ZC + D_C] = (dyc_ref[...] * oc_ref[...] * dszc).astype(BF16)

        @pl.when(i == n - 1)
        def _():
            ch = lax.broadcasted_iota(jnp.int32, (D_A, CHUNK), 0)
            gcol = lax.broadcasted_iota(jnp.int32, (D_A, CHUNK), 1)
            pick = (ch // (D_A // A_GROUPS) == gcol).astype(BF16)
            rest = dbias_ref[...]
            tot = jnp.zeros((CHUNK, CHUNK), F32)
            for _ in range(3):
                term = rest.astype(BF16)
                tot = tot + _dot(term, pick, NN)
                rest = rest - term.astype(F32)
            dsb_ref[...] = tot

    wide = lambda off: pl.BlockSpec((CHUNK, D_A), lambda i: (i, off // D_A))
    narrow = lambda off: pl.BlockSpec((CHUNK, D_B), lambda i: (i, off // D_B))
    vec = pl.BlockSpec((1, D_A), lambda i: (0, 0))
    wspec = pl.BlockSpec((A_GROUPS, CHUNK, CHUNK), lambda i: (0, 0, 0))
    bspec = pl.BlockSpec((CHUNK, D_A), lambda i: (0, 0))
    return pl.pallas_call(
        body, name=name, grid=(n,),
        in_specs=[wide(OFF_U), wide(OFF_V), wide(OFF_ZA), narrow(OFF_ZB), narrow(OFF_ZC),
                  wide(0), narrow(OFF_YB), narrow(OFF_YC), narrow(0), narrow(0), narrow(0), narrow(0), narrow(0),
                  narrow(0), vec, vec, wspec, wspec, bspec],
        out_specs=[pl.BlockSpec((CHUNK, IN_WIDTH), lambda i: (i, 0)), wspec,
                   pl.BlockSpec((CHUNK, CHUNK), lambda i: (0, 0)), vec, vec],
        out_shape=[jax.ShapeDtypeStruct((s_len, IN_WIDTH), BF16), jax.ShapeDtypeStruct((A_GROUPS, CHUNK, CHUNK), F32),
                   jax.ShapeDtypeStruct((CHUNK, CHUNK), F32), jax.ShapeDtypeStruct((1, D_A), F32),
                   jax.ShapeDtypeStruct((1, D_A), F32)],
        scratch_shapes=[pltpu.VMEM((CHUNK, D_A), F32)],
        compiler_params=_params(("arbitrary",)),
    )(proj, proj, proj, proj, proj, dy, dy, dy, o_b, o_c, dq_b, dk_b, dv_b, dq_c, lng, lnb, w_s, w_s_t, bias)


IN_SHARD = IN_WIDTH // N_CHIPS
ROW_SHARD = D_MODEL // N_CHIPS


def _bias_rows(sgu_b_l):
    return jnp.repeat(sgu_b_l.T, D_A // A_GROUPS, axis=1)


def _layer_fwd(l, x, mem, sm, w_in_all, rest):
    s_len = x.shape[0]
    m_len = mem.shape[0]
    tm = min(1024, s_len)
    tn = 768
    per = IN_SHARD // tn
    h = _rms_fwd(f"rms_fwd_{l}", x, sm["norm_g"][l][None], min(256, s_len))
    proj = _matmul(
        f"in_proj_{l}", h, w_in_all, grid=(s_len // tm, IN_WIDTH // tn, 1),
        a_spec=pl.BlockSpec((tm, D_MODEL), lambda i, j, k: (i, 0)),
        b_spec=pl.BlockSpec((None, D_MODEL, tn), lambda i, j, k: (j // per, 0, j % per)),
        o_spec=pl.BlockSpec((tm, tn), lambda i, j, k: (i, j)),
        out_shape=jax.ShapeDtypeStruct((s_len, IN_WIDTH), F32), dims=NN)
    w_kv_all, w_out_all, after = rest(proj)
    mem_h = _rms_fwd(f"mem_rms_fwd_{l}", mem, sm["mem_norm_g"][l][None], m_len, after)
    mem_kv = _matmul(
        f"mem_kv_{l}", mem_h, w_kv_all, grid=(1, 2, N_CHIPS),
        a_spec=pl.BlockSpec((m_len, ROW_SHARD), lambda i, j, k: (0, k)),
        b_spec=pl.BlockSpec((None, ROW_SHARD, D_C), lambda i, j, k: (k, 0, j)),
        o_spec=pl.BlockSpec((m_len, D_C), lambda i, j, k: (0, j)),
        out_shape=jax.ShapeDtypeStruct((m_len, 2 * D_C), F32), dims=NN)
    o_b = _sb_fwd(f"sb_fwd_{l}", proj)
    qg, kg = sm["q_norm_g"][l][None], sm["k_norm_g"][l][None]
    o_c = _mem_fwd(f"mem_fwd_{l}", proj, mem_kv, qg, kg)
    bias = _bias_rows(sm["sgu_b"][l])
    y = _gate_fwd(f"gate_fwd_{l}", proj, o_b, o_c, sm["sgu_ln_g"][l][None], sm["sgu_ln_b"][l][None], sm["sgu_w"][l], bias)
    tn_o = 512
    x_next = _matmul(
        f"out_proj_{l}", y, w_out_all, grid=(s_len // tm, D_MODEL // tn_o, N_CHIPS),
        a_spec=pl.BlockSpec((tm, ROW_SHARD), lambda i, j, k: (i, k)),
        b_spec=pl.BlockSpec((None, ROW_SHARD, tn_o), lambda i, j, k: (k, 0, j)),
        o_spec=pl.BlockSpec((tm, tn_o), lambda i, j, k: (i, j)),
        out_shape=jax.ShapeDtypeStruct((s_len, D_MODEL), F32), dims=NN,
        res=x, res_spec=pl.BlockSpec((tm, tn_o), lambda i, j, k: (i, j)))
    saved = dict(x=x, h=h, proj=proj, mem_h=mem_h, mem_kv=mem_kv, o_b=o_b, o_c=o_c, y=y, bias=bias,
                 weights=(w_in_all, w_kv_all, w_out_all))
    return x_next, saved


def _layer_bwd(l, dxo, dxo_b, mem, sm, saved, on_weight_grads=None):
    s_len = dxo.shape[0]
    m_len = mem.shape[0]
    proj, y, h, mem_h, mem_kv = saved["proj"], saved["y"], saved["h"], saved["mem_h"], saved["mem_kv"]
    w_in_all, w_kv_all, w_out_all = saved["weights"]
    tm = min(1024, s_len)
    tk = min(1024, s_len)
    g_out = _matmul(
        f"d_w_out_{l}", y, dxo_b, grid=(N_CHIPS, D_MODEL // 1024, s_len // tk),
        a_spec=pl.BlockSpec((tk, ROW_SHARD), lambda i, j, k: (k, i)),
        b_spec=pl.BlockSpec((tk, 1024), lambda i, j, k: (k, j)),
        o_spec=pl.BlockSpec((None, ROW_SHARD, 1024), lambda i, j, k: (i, 0, j)),
        out_shape=jax.ShapeDtypeStruct((N_CHIPS, ROW_SHARD, D_MODEL), F32), dims=TN)
    token = None if on_weight_grads is None else on_weight_grads("out", [g_out])
    dy = _matmul(
        f"d_y_{l}", dxo_b, w_out_all, grid=(s_len // tm, N_CHIPS, 1),
        a_spec=pl.BlockSpec((tm, D_MODEL), lambda i, j, k: (i, 0)),
        b_spec=pl.BlockSpec((None, ROW_SHARD, D_MODEL), lambda i, j, k: (j, 0, 0)),
        o_spec=pl.BlockSpec((tm, ROW_SHARD), lambda i, j, k: (i, j)),
        out_shape=jax.ShapeDtypeStruct((s_len, D_MODEL), F32), dims=NT, after=token)
    qg, kg = sm["q_norm_g"][l][None], sm["k_norm_g"][l][None]
    dq_c, dmk, dmv, dqg, dkg = _mem_bwd(f"mem_bwd_{l}", proj, mem_kv, qg, kg, dy)
    dqkv = _sb_bwd(f"sb_bwd_{l}", proj, dy)
    w_s = sm["sgu_w"][l]
    dproj, dws, dbias, dlng, dlnb = _gate_bwd(
        f"gate_bwd_{l}", proj, dy, saved["o_b"], saved["o_c"], dqkv, dq_c, sm["sgu_ln_g"][l][None],
        sm["sgu_ln_b"][l][None], w_s, jnp.swapaxes(w_s, 1, 2), saved["bias"])
    tn = 768
    per = IN_SHARD // tn
    g_in = _matmul(
        f"d_w_in_{l}", h, dproj, grid=(D_MODEL // 1024, IN_WIDTH // tn, s_len // tk),
        a_spec=pl.BlockSpec((tk, 1024), lambda i, j, k: (k, i)),
        b_spec=pl.BlockSpec((tk, tn), lambda i, j, k: (k, j)),
        o_spec=pl.BlockSpec((None, 1024, tn), lambda i, j, k: (j // per, i, j % per)),
        out_shape=jax.ShapeDtypeStruct((N_CHIPS, D_MODEL, IN_SHARD), F32), dims=TN)
    dkv_b = jnp.concatenate([dmk, dmv], axis=1).astype(BF16)
    g_kv = _matmul(
        f"d_w_kv_{l}", mem_h, dkv_b, grid=(N_CHIPS, 1, 1),
        a_spec=pl.BlockSpec((m_len, ROW_SHARD), lambda i, j, k: (0, i)),
        b_spec=pl.BlockSpec((m_len, 2 * D_C), lambda i, j, k: (0, 0)),
        o_spec=pl.BlockSpec((None, ROW_SHARD, 2 * D_C), lambda i, j, k: (i, 0, 0)),
        out_shape=jax.ShapeDtypeStruct((N_CHIPS, ROW_SHARD, 2 * D_C), F32), dims=TN)
    token = None if on_weight_grads is None else on_weight_grads("in", [g_in, g_kv])
    dh = _matmul(
        f"d_h_{l}", dproj, w_in_all, grid=(s_len // tm, D_MODEL // 1024, N_CHIPS),
        a_spec=pl.BlockSpec((tm, IN_SHARD), lambda i, j, k: (i, k)),
        b_spec=pl.BlockSpec((None, 1024, IN_SHARD), lambda i, j, k: (k, j, 0)),
        o_spec=pl.BlockSpec((tm, 1024), lambda i, j, k: (i, j)),
        out_shape=jax.ShapeDtypeStruct((s_len, D_MODEL), F32), dims=NT, after=token)
    dx, dx_b, dng = _rms_bwd(f"rms_bwd_{l}", saved["x"], dh, dxo, sm["norm_g"][l][None], min(256, s_len))
    d_mem_h = _matmul(
        f"d_mem_h_{l}", dkv_b, w_kv_all, grid=(1, N_CHIPS, 1),
        a_spec=pl.BlockSpec((m_len, 2 * D_C), lambda i, j, k: (0, 0)),
        b_spec=pl.BlockSpec((None, ROW_SHARD, 2 * D_C), lambda i, j, k: (j, 0, 0)),
        o_spec=pl.BlockSpec((m_len, ROW_SHARD), lambda i, j, k: (0, j)),
        out_shape=jax.ShapeDtypeStruct((m_len, D_MODEL), F32), dims=NT)
    dmng = _rms_gain_grad(f"mem_rms_bwd_{l}", mem, d_mem_h)
    dsgu_b = dbias[:, :A_GROUPS].T
    small = dict(norm_g=dng[0], sgu_ln_g=dlng[0], sgu_ln_b=dlnb[0], sgu_w=dws, sgu_b=dsgu_b, mem_norm_g=dmng[0],
                 q_norm_g=dqg[0], k_norm_g=dkg[0])
    return dx, dx_b, small, g_in, g_kv, g_out


SMALL_NAMES = ("norm_g", "sgu_ln_g", "sgu_ln_b", "sgu_w", "sgu_b", "mem_norm_g", "q_norm_g", "k_norm_g")


def _local_step(x, mem, target, sm, w_all):
    saved = []
    cur = x
    for l in range(DEPTH):
        cur, sv = _layer_fwd(l, cur, mem, sm, w_all[l][0], lambda proj, l=l: (w_all[l][1], w_all[l][2], None))
        saved.append(sv)
    dxo, dxo_b, loss = _loss_and_grad("loss", cur, target, min(256, x.shape[0]))
    small, big = [None] * DEPTH, [None] * DEPTH
    for l in reversed(range(DEPTH)):
        dxo, dxo_b, small[l], *big[l] = _layer_bwd(l, dxo, dxo_b, mem, sm, saved[l])
    small = {k: jnp.stack([small[l][k] for l in range(DEPTH)]) for k in SMALL_NAMES}
    return loss, dxo, small, big


def _place():
    x, y, c = lax.axis_index("x"), lax.axis_index("y"), lax.axis_index("c")
    return x, y, c


def _other_chips(x, y):
    return [(1 - x, y, 2 * (1 - x) + y), (x, 1 - y, 2 * x + 1 - y), (1 - x, 1 - y, 2 * (1 - x) + 1 - y)]


AG_CHUNKS = 4
D2D_CHUNKS = 8


def _place_index():
    return jnp.stack([2 * lax.axis_index("x") + lax.axis_index("y"), lax.axis_index("c")]).astype(jnp.int32)


def _cast_into_slot(name, w, l, place):
    _, rows, cols = w.shape
    tr = min(256, rows)

    def body(p_ref, w_ref, o_ref):
        o_ref[...] = w_ref[...].astype(BF16)

    return pl.pallas_call(
        body, name=name,
        grid_spec=pltpu.PrefetchScalarGridSpec(
            num_scalar_prefetch=1, grid=(rows // tr,),
            in_specs=[pl.BlockSpec((None, tr, cols), lambda i, p: (l, i, 0))],
            out_specs=pl.BlockSpec((None, tr, cols), lambda i, p: (p[0], i, 0))),
        out_shape=jax.ShapeDtypeStruct((N_CHIPS, rows, cols), BF16),
        compiler_params=_params(("parallel",)),
    )(place, w)


HBM = pl.BlockSpec(memory_space=pltpu.HBM)
SEM = pl.BlockSpec(memory_space=pltpu.SEMAPHORE)
DATAFLOW = pltpu.SideEffectType.DATAFLOW_SIDE_EFFECTING


def _in_hbm(a):
    return pltpu.with_memory_space_constraint(a, pltpu.HBM)


def _chip_copies_start(name, srcs, lands, make_copy, after=None):
    n_t = len(srcs)
    in_place = lands is None
    n_after = 0 if after is None else 1

    def body(*refs):
        src = refs[:n_t]
        k = (n_t if in_place else 2 * n_t) + n_after
        send_sems, recv_sems = refs[k], refs[k + 1]
        land = refs[k + 2:k + 2 + n_t] if in_place else refs[k + 2 + n_t:k + 2 + 2 * n_t]
        token = refs[-1]
        x, y, c = _place()
        me = 2 * x + y
        for t in range(n_t):
            for px, py, pk in _other_chips(x, y):
                s, d = make_copy(src[t], land[t], me, pk, c)
                pltpu.make_async_remote_copy(
                    src_ref=s, dst_ref=d, send_sem=send_sems.at[t], recv_sem=recv_sems.at[t],
                    device_id=(px, py, c), device_id_type=MESH).start()
        token[...] = jnp.zeros_like(token)

    bufs = list(srcs) if in_place else list(srcs) + list(lands)
    outs = pl.pallas_call(
        body, name=name,
        in_specs=[HBM] * len(bufs) + [ANY] * n_after,
        out_specs=[SEM, SEM] + [HBM] * len(bufs) + [pl.BlockSpec(memory_space=pltpu.VMEM)],
        out_shape=[pltpu.SemaphoreType.DMA((n_t,)), pltpu.SemaphoreType.DMA((n_t,))]
        + [pltpu.HBM(b.shape, b.dtype) for b in bufs] + [jax.ShapeDtypeStruct((8, 128), F32)],
        input_output_aliases={i: 2 + i for i in range(len(bufs))},
        compiler_params=pltpu.CompilerParams(has_side_effects=DATAFLOW),
    )(*[_in_hbm(b) for b in bufs], *([] if after is None else [after]))
    return outs[0], outs[1], list(outs[2:2 + len(bufs)]), outs[-1]


def _chip_copies_wait(name, send_sems, recv_sems, bufs, sent, landed, after):
    n_b = len(bufs)

    def body(*refs):
        buf = refs[:n_b]
        send_ref, recv_ref = refs[n_b], refs[n_b + 1]
        x, y, c = _place()
        for t, (s, d) in enumerate(zip(sent(buf), landed(buf))):
            out = pltpu.make_async_remote_copy(src_ref=s, dst_ref=s, send_sem=send_ref.at[t], recv_sem=recv_ref.at[t],
                                               device_id=(x, y, c), device_id_type=MESH)
            out.wait_send()
            arrived = pltpu.make_async_remote_copy(src_ref=d, dst_ref=d, send_sem=send_ref.at[t],
                                                   recv_sem=recv_ref.at[t], device_id=(x, y, c), device_id_type=MESH)
            arrived.wait_recv()

    return pl.pallas_call(
        body, name=name,
        in_specs=[HBM] * n_b + [SEM, SEM, ANY], out_specs=[HBM] * n_b,
        out_shape=[pltpu.HBM(b.shape, b.dtype) for b in bufs],
        input_output_aliases={i: i for i in range(n_b)},
        compiler_params=pltpu.CompilerParams(has_side_effects=DATAFLOW),
    )(*bufs, send_sems, recv_sems, after)


def _gather_start(name, bufs, after=None):
    def make_copy(src, land, me, pk, c):
        hr = src.shape[1] // 2
        return src.at[me, pl.ds(c * hr, hr)], land.at[me, pl.ds(c * hr, hr)]

    return _chip_copies_start(name, bufs, None, make_copy, after)


def _gather_wait(name, send_sems, recv_sems, bufs, after):
    def three_halves(buf):
        return [b.at[pl.ds(0, 3), pl.ds(0, b.shape[1] // 2)] for b in buf]

    return _chip_copies_wait(name, send_sems, recv_sems, bufs, three_halves, three_halves, after)


def _gather_forward(name, bufs):
    n_t = len(bufs)
    n = 3 * n_t * D2D_CHUNKS

    def body(*refs):
        mine, buf = refs[:n_t], refs[n_t:2 * n_t]
        send_sems, recv_sems = refs[2 * n_t:]
        x, y, c = _place()
        chips = _other_chips(x, y)

        def piece(ref, t, slot, core, q):
            hr = ref[t].shape[1] // 2
            cr = hr // D2D_CHUNKS
            return ref[t].at[slot, pl.ds(core * hr + q * cr, cr)]

        copies = []
        for q in range(D2D_CHUNKS):
            for t in range(n_t):
                for j, (_, _, pk) in enumerate(chips):
                    s = (t * 3 + j) * D2D_CHUNKS + q
                    cp = pltpu.make_async_remote_copy(
                        src_ref=piece(mine, t, pk, c, q), dst_ref=piece(buf, t, pk, c, q), send_sem=send_sems.at[s],
                        recv_sem=recv_sems.at[s], device_id=(x, y, 1 - c), device_id_type=MESH)
                    cp.start()
                    copies.append(cp)
        for q in range(D2D_CHUNKS):
            for t in range(n_t):
                for j, (_, _, pk) in enumerate(chips):
                    s = (t * 3 + j) * D2D_CHUNKS + q
                    theirs = piece(buf, t, pk, 1 - c, q)
                    pltpu.make_async_remote_copy(
                        src_ref=theirs, dst_ref=theirs, send_sem=send_sems.at[s], recv_sem=recv_sems.at[s],
                        device_id=(x, y, 1 - c), device_id_type=MESH).wait_recv()
        for cp in copies:
            cp.wait_send()

    return pl.pallas_call(
        body, name=name,
        in_specs=[ANY] * n_t, out_specs=[ANY] * n_t,
        out_shape=[jax.ShapeDtypeStruct(b.shape, b.dtype) for b in bufs],
        input_output_aliases={t: t for t in range(n_t)},
        scratch_shapes=[pltpu.SemaphoreType.DMA((n,)), pltpu.SemaphoreType.DMA((n,))],
        compiler_params=pltpu.CompilerParams(has_side_effects=True),
    )(*bufs)


def _core_exchange(name, grads):
    n_t = len(grads)
    n = n_t * D2D_CHUNKS

    def body(*refs):
        src, theirs = refs[:n_t], refs[n_t:2 * n_t]
        send_sems, recv_sems = refs[2 * n_t:]
        x, y, c = _place()
        copies = []
        for q in range(D2D_CHUNKS):
            for t in range(n_t):
                hr = src[t].shape[1] // 2
                cr = hr // D2D_CHUNKS
                s = t * D2D_CHUNKS + q
                cp = pltpu.make_async_remote_copy(
                    src_ref=src[t].at[:, pl.ds((1 - c) * hr + q * cr, cr)],
                    dst_ref=theirs[t].at[:, pl.ds(q * cr, cr)],
                    send_sem=send_sems.at[s], recv_sem=recv_sems.at[s], device_id=(x, y, 1 - c), device_id_type=MESH)
                cp.start()
                copies.append(cp)
        for cp in copies:
            cp.wait()

    half = [jax.ShapeDtypeStruct((g.shape[0], g.shape[1] // 2, g.shape[2]), g.dtype) for g in grads]
    return pl.pallas_call(
        body, name=name,
        in_specs=[ANY] * n_t, out_specs=[ANY] * n_t, out_shape=half,
        scratch_shapes=[pltpu.SemaphoreType.DMA((n,)), pltpu.SemaphoreType.DMA((n,))],
        compiler_params=pltpu.CompilerParams(has_side_effects=True),
    )(*grads)


def _add_to_bf16(name, full, theirs, place):
    chips, rows, cols = theirs.shape
    tr = min(256, rows)
    per = rows // tr

    def body(p_ref, a_ref, b_ref, o_ref):
        o_ref[...] = (a_ref[...] + b_ref[...]).astype(BF16)

    blk = pl.BlockSpec((None, tr, cols), lambda k, i, p: (k, i, 0))
    return pl.pallas_call(
        body, name=name,
        grid_spec=pltpu.PrefetchScalarGridSpec(
            num_scalar_prefetch=1, grid=(chips, per),
            in_specs=[pl.BlockSpec((None, tr, cols), lambda k, i, p: (k, p[1] * per + i, 0)), blk],
            out_specs=blk),
        out_shape=jax.ShapeDtypeStruct(theirs.shape, BF16), compiler_params=_params(("parallel",) * 2),
    )(place, full, theirs)


def _chip_exchange_start(name, parts):
    lands = [lax.empty(p.shape, p.dtype) for p in parts]
    return _chip_copies_start(name, parts, lands, lambda src, land, me, pk, c: (src.at[pk], land.at[me]))


def _chip_exchange_wait(name, send_sems, recv_sems, bufs, after):
    n_t = len(bufs) // 2
    return _chip_copies_wait(name, send_sems, recv_sems, bufs,
                             lambda buf: [b.at[pl.ds(0, 3)] for b in buf[:n_t]],
                             lambda buf: [b.at[pl.ds(0, 3)] for b in buf[n_t:]], after)


def _sum_chips(name, parts, landed, place, l, stacked):
    chips, rows, cols = landed.shape
    tr = min(256, rows)
    per = rows // tr

    def body(p_ref, own_ref, *refs):
        land, o_ref = refs[:chips], refs[-1]
        tot = None
        for k in range(chips):
            term = jnp.where(p_ref[0] == k, own_ref[...], land[k][...]).astype(F32)
            tot = term if tot is None else tot + term
        o_ref[...] = tot

    def from_chip(k):
        return pl.BlockSpec((None, tr, cols), lambda i, p: (jnp.where(p[0] == k, (k + 1) % chips, k), i, 0))

    in_specs = [pl.BlockSpec((None, tr, cols), lambda i, p: (p[0], i, 0))] + [from_chip(k) for k in range(chips)]
    args = [parts] + [landed] * chips
    aliases = {}
    if stacked is not None:
        in_specs.append(ANY)
        args.append(stacked)
        aliases = {len(args): 0}
    return pl.pallas_call(
        body, name=name,
        grid_spec=pltpu.PrefetchScalarGridSpec(
            num_scalar_prefetch=1, grid=(per,), in_specs=in_specs,
            out_specs=pl.BlockSpec((None, tr, cols), lambda i, p: (l, p[1] * per + i, 0))),
        out_shape=jax.ShapeDtypeStruct((DEPTH, 2 * rows, cols), F32), input_output_aliases=aliases,
        compiler_params=_params(("parallel",)),
    )(place, *args)


def _core_share(bufs):
    n_t = len(bufs)
    n = n_t * D2D_CHUNKS

    def body(*refs):
        mine, buf = refs[:n_t], refs[n_t:2 * n_t]
        send_sems, recv_sems = refs[2 * n_t:]
        x, y, c = _place()

        def piece(ref, t, core, q):
            hr = ref[t].shape[1] // 2
            cr = hr // D2D_CHUNKS
            return ref[t].at[:, pl.ds(core * hr + q * cr, cr)]

        copies = []
        for q in range(D2D_CHUNKS):
            for t in range(n_t):
                s = t * D2D_CHUNKS + q
                cp = pltpu.make_async_remote_copy(
                    src_ref=piece(mine, t, c, q), dst_ref=piece(buf, t, c, q), send_sem=send_sems.at[s],
                    recv_sem=recv_sems.at[s], device_id=(x, y, 1 - c), device_id_type=MESH)
                cp.start()
                copies.append(cp)
        for q in range(D2D_CHUNKS):
            for t in range(n_t):
                s = t * D2D_CHUNKS + q
                theirs = piece(buf, t, 1 - c, q)
                pltpu.make_async_remote_copy(
                    src_ref=theirs, dst_ref=theirs, send_sem=send_sems.at[s], recv_sem=recv_sems.at[s],
                    device_id=(x, y, 1 - c), device_id_type=MESH).wait_recv()
        for cp in copies:
            cp.wait_send()

    return pl.pallas_call(
        body, name="grad_core_share",
        in_specs=[ANY] * n_t, out_specs=[ANY] * n_t,
        out_shape=[jax.ShapeDtypeStruct(b.shape, b.dtype) for b in bufs],
        input_output_aliases={t: t for t in range(n_t)},
        scratch_shapes=[pltpu.SemaphoreType.DMA((n,)), pltpu.SemaphoreType.DMA((n,))],
        compiler_params=pltpu.CompilerParams(has_side_effects=True),
    )(*bufs)


def _all_reduce_small(vec):
    rows, lanes = vec.shape
    hr = rows // 2

    def body(v_ref, o_ref, sib_ref, chips_ref, send_sems, recv_sems):
        x, y, c = _place()
        me = 2 * x + y
        sibling = (x, y, 1 - c)
        mine = pl.ds(pl.multiple_of(c * hr, 8), hr)
        theirs = pl.ds(pl.multiple_of((1 - c) * hr, 8), hr)
        swap = pltpu.make_async_remote_copy(
            src_ref=v_ref.at[theirs], dst_ref=sib_ref, send_sem=send_sems.at[0], recv_sem=recv_sems.at[0],
            device_id=sibling, device_id_type=MESH)
        swap.start()
        swap.wait_recv()
        chips_ref[me] = v_ref[mine] + sib_ref[...]
        copies = []
        for j, (px, py, pk) in enumerate(_other_chips(x, y)):
            cp = pltpu.make_async_remote_copy(
                src_ref=chips_ref.at[me], dst_ref=chips_ref.at[me], send_sem=send_sems.at[1 + j],
                recv_sem=recv_sems.at[1 + j], device_id=(px, py, c), device_id_type=MESH)
            cp.start()
            copies.append(cp)
        for j, (px, py, pk) in enumerate(_other_chips(x, y)):
            pltpu.make_async_remote_copy(
                src_ref=chips_ref.at[pk], dst_ref=chips_ref.at[pk], send_sem=send_sems.at[1 + j],
                recv_sem=recv_sems.at[1 + j], device_id=(px, py, c), device_id_type=MESH).wait_recv()
        tot = chips_ref[0]
        for k in range(1, N_CHIPS):
            tot = tot + chips_ref[k]
        o_ref[mine] = tot
        share = pltpu.make_async_remote_copy(
            src_ref=o_ref.at[mine], dst_ref=o_ref.at[mine], send_sem=send_sems.at[4], recv_sem=recv_sems.at[4],
            device_id=sibling, device_id_type=MESH)
        share.start()
        pltpu.make_async_remote_copy(
            src_ref=o_ref.at[theirs], dst_ref=o_ref.at[theirs], send_sem=send_sems.at[4], recv_sem=recv_sems.at[4],
            device_id=sibling, device_id_type=MESH).wait_recv()
        swap.wait_send()
        for cp in copies:
            cp.wait_send()
        share.wait_send()

    vm = pl.BlockSpec(memory_space=pltpu.VMEM)
    return pl.pallas_call(
        body, name="small_all_reduce", in_specs=[vm], out_specs=vm,
        out_shape=jax.ShapeDtypeStruct((rows, lanes), F32),
        scratch_shapes=[pltpu.VMEM((hr, lanes), F32), pltpu.VMEM((N_CHIPS, hr, lanes), F32),
                        pltpu.SemaphoreType.DMA((5,)), pltpu.SemaphoreType.DMA((5,))],
        compiler_params=pltpu.CompilerParams(has_side_effects=True, vmem_limit_bytes=48 * MIB),
    )(vec)


def _adamw(name, w, g, m, v):
    rows, cols = w.shape
    tr = rows
    for cand in (256, 128, 64, 32, 16, 8):
        if rows % cand == 0:
            tr = cand
            break
    c1 = 1.0 - ADAM_B1 ** ADAM_STEP
    c2 = 1.0 - ADAM_B2 ** ADAM_STEP

    def body(w_ref, g_ref, m_ref, v_ref, d_ref, nm_ref, nv_ref):
        gv = g_ref[...]
        nm = ADAM_B1 * m_ref[...] + (1.0 - ADAM_B1) * gv
        nv = ADAM_B2 * v_ref[...] + (1.0 - ADAM_B2) * (gv * gv)
        nm_ref[...] = nm
        nv_ref[...] = nv
        d_ref[...] = -ADAM_LR * ((nm / c1) / (jnp.sqrt(nv / c2) + ADAM_EPS) + ADAM_WD * w_ref[...])

    blk = pl.BlockSpec((tr, cols), lambda i: (i, 0))
    out = jax.ShapeDtypeStruct((rows, cols), F32)
    return pl.pallas_call(
        body, name=name, grid=(rows // tr,), in_specs=[blk] * 4, out_specs=[blk] * 3, out_shape=[out] * 3,
        compiler_params=_params(("parallel",)),
    )(w, g, m, v)


def _pack_small(parts):
    flat = jnp.concatenate([parts[k].reshape(-1) for k in SMALL_NAMES])
    n = flat.shape[0]
    rows = -(-n // (256 * 128)) * 256
    return jnp.pad(flat, (0, rows * 128 - n)).reshape(rows, 128)


def _unpack_small(packed, like):
    flat = packed.reshape(-1)
    out, off = {}, 0
    for k in SMALL_NAMES:
        n = like[k].size
        out[k] = flat[off:off + n].reshape(like[k].shape)
        off += n
    return out


WEIGHT_ORDER = ("norm_g", "w_in", "sgu_ln_g", "sgu_ln_b", "sgu_w", "sgu_b", "mem_norm_g", "w_mem_kv", "q_norm_g",
                "k_norm_g", "w_out")


def kernel(x, mem, norm_g, w_in, sgu_ln_g, sgu_ln_b, sgu_w, sgu_b, mem_norm_g, w_mem_kv, q_norm_g, k_norm_g, w_out, loss_target, m_norm_g, m_w_in, m_sgu_ln_g, m_sgu_ln_b, m_sgu_w, m_sgu_b, m_mem_norm_g, m_w_mem_kv, m_q_norm_g, m_k_norm_g, m_w_out, v_norm_g, v_w_in, v_sgu_ln_g, v_sgu_ln_b, v_sgu_w, v_sgu_b, v_mem_norm_g, v_w_mem_kv, v_q_norm_g, v_k_norm_g, v_w_out):
    weights = dict(norm_g=norm_g, w_in=w_in, sgu_ln_g=sgu_ln_g, sgu_ln_b=sgu_ln_b, sgu_w=sgu_w, sgu_b=sgu_b,
                   mem_norm_g=mem_norm_g, w_mem_kv=w_mem_kv, q_norm_g=q_norm_g, k_norm_g=k_norm_g, w_out=w_out)
    mom_m = dict(norm_g=m_norm_g, w_in=m_w_in, sgu_ln_g=m_sgu_ln_g, sgu_ln_b=m_sgu_ln_b, sgu_w=m_sgu_w, sgu_b=m_sgu_b,
                 mem_norm_g=m_mem_norm_g, w_mem_kv=m_w_mem_kv, q_norm_g=m_q_norm_g, k_norm_g=m_k_norm_g, w_out=m_w_out)
    mom_v = dict(norm_g=v_norm_g, w_in=v_w_in, sgu_ln_g=v_sgu_ln_g, sgu_ln_b=v_sgu_ln_b, sgu_w=v_sgu_w, sgu_b=v_sgu_b,
                 mem_norm_g=v_mem_norm_g, w_mem_kv=v_w_mem_kv, q_norm_g=v_q_norm_g, k_norm_g=v_k_norm_g, w_out=v_w_out)
    big = ("w_in", "w_mem_kv", "w_out")
    sm = {k: weights[k] for k in SMALL_NAMES}

    place = _place_index()
    xs, mems, target = x[0], mem[0], loss_target[0]

    slots = [[_cast_into_slot(f"cast_{k}_{l}", weights[k], l, place) for k in big] for l in range(DEPTH)]
    saved = [None] * DEPTH

    def gathered(tag, flight, after):
        send_sems, recv_sems, bufs, _ = flight
        return _gather_forward(f"gather_forward_{tag}", _gather_wait(f"gather_wait_{tag}", send_sems, recv_sems,
                                                                      bufs, after))

    flights = {}

    def start_gather(l, after=None):
        flights[l, "in"] = _gather_start(f"gather_start_{l}_in", slots[l][:1], after)
        flights[l, "rest"] = _gather_start(f"gather_start_{l}_rest", slots[l][1:], flights[l, "in"][3])
        return flights[l, "rest"][3]

    start_gather(0)
    cur = xs
    for l in range(DEPTH):
        (w_in_all,) = gathered(f"{l}_in", flights[l, "in"], flights[l, "rest"][3] if l == 0 else cur)

        def rest(proj, l=l):
            w_kv_all, w_out_all = gathered(f"{l}_rest", flights[l, "rest"], proj)
            token = start_gather(l + 1, w_out_all) if l + 1 < DEPTH else None
            return w_kv_all, w_out_all, token

        cur, saved[l] = _layer_fwd(l, cur, mems, sm, w_in_all, rest)
    dxo, dxo_b, loss_part = _loss_and_grad("loss", cur, target, min(256, xs.shape[0]))
    loss = lax.psum(loss_part[0, 0], ("x", "y", "c"))

    small_g = [None] * DEPTH
    flight = {}
    for l in reversed(range(DEPTH)):
        def start_exchange(group, full, l=l):
            theirs = _core_exchange(f"grad_core_exchange_{l}_{group}", full)
            parts = [_add_to_bf16(f"grad_core_sum_{l}_{group}_{t}", g, th, place)
                     for t, (g, th) in enumerate(zip(full, theirs))]
            *flight[l, group], token = _chip_exchange_start(f"grad_chip_start_{l}_{group}", parts)
            return token

        dxo, dxo_b, small_g[l], *_ = _layer_bwd(l, dxo, dxo_b, mems, sm, saved[l], on_weight_grads=start_exchange)
    grad_x = dxo
    halves = dict.fromkeys(big)
    for l in reversed(range(DEPTH)):
        for group, names in (("out", ("w_out",)), ("in", ("w_in", "w_mem_kv"))):
            send_sems, recv_sems, bufs = flight[l, group]
            bufs = _chip_exchange_wait(f"grad_chip_wait_{l}_{group}", send_sems, recv_sems, bufs, grad_x)
            for t, k in enumerate(names):
                halves[k] = _sum_chips(f"grad_chip_sum_{l}_{k}", bufs[t], bufs[len(names) + t], place, l, halves[k])
    big_g = dict(zip(big, _core_share([halves[k] for k in big])))

    small_g = {k: jnp.stack([small_g[l][k] for l in range(DEPTH)]) for k in SMALL_NAMES}
    small_sum = _unpack_small(_all_reduce_small(_pack_small(small_g)), sm)

    grads, delta, new_m, new_v = {}, {}, {}, {}
    for k in big:
        shape = weights[k].shape
        two_d = (shape[0] * shape[1], shape[2])
        grads[k] = big_g[k]
        d, nm, nv = _adamw(f"adamw_{k}", weights[k].reshape(two_d), big_g[k].reshape(two_d),
                           mom_m[k].reshape(two_d), mom_v[k].reshape(two_d))
        delta[k], new_m[k], new_v[k] = d.reshape(shape), nm.reshape(shape), nv.reshape(shape)
    d, nm, nv = _adamw("adamw_small", _pack_small(sm), _pack_small(small_sum),
                       _pack_small({k: mom_m[k] for k in SMALL_NAMES}), _pack_small({k: mom_v[k] for k in SMALL_NAMES}))
    grads.update(small_sum)
    delta.update(_unpack_small(d, sm))
    new_m.update(_unpack_small(nm, sm))
    new_v.update(_unpack_small(nv, sm))
    return (loss, grad_x[None], *[grads[k] for k in WEIGHT_ORDER], *[delta[k] for k in WEIGHT_ORDER],
            *[new_m[k] for k in WEIGHT_ORDER], *[new_v[k] for k in WEIGHT_ORDER])
```

```python
import functools
import math

import jax
import jax.numpy as jnp
from jax import lax
from jax.experimental import pallas as pl
from jax.experimental.pallas import tpu as pltpu

F32 = jnp.float32
BF16 = jnp.bfloat16
MESH = pl.DeviceIdType.MESH

D_MODEL = 2048
DEPTH = 2
CHUNK = 128
D_A = 1024
A_GROUPS = 8
D_B = 512
D_C = 512
HEADS = 4
HEAD_DIM = 128
IN_WIDTH = 6144
N_CHIPS = 4
EPS = 1e-6
ATT_SCALE = 1.0 / math.sqrt(HEAD_DIM)

OFF_U, OFF_V, OFF_ZA = 0, 1024, 2048
OFF_QB, OFF_KB, OFF_VB, OFF_ZB = 3072, 3584, 4096, 4608
OFF_QC, OFF_ZC = 5120, 5632
OFF_YB, OFF_YC = 1024, 1536

ADAM_LR = 0.001
ADAM_B1 = 0.9
ADAM_B2 = 0.999
ADAM_EPS = 1e-08
ADAM_WD = 0.01
ADAM_STEP = 10

MIB = 1024 * 1024
ANY = pl.BlockSpec(memory_space=pl.ANY)


def _params(semantics=None, vmem_mb=48):
    return pltpu.CompilerParams(dimension_semantics=semantics, vmem_limit_bytes=vmem_mb * MIB)


def _gelu(x):
    return 0.5 * x * (1.0 + lax.erf(x * (1.0 / math.sqrt(2.0))))


def _gelu_grad(x):
    cdf = 0.5 * (1.0 + lax.erf(x * (1.0 / math.sqrt(2.0))))
    pdf = jnp.exp(-0.5 * x * x) * (1.0 / math.sqrt(2.0 * math.pi))
    return cdf + x * pdf


def _sigmoid(x):
    return 1.0 / (1.0 + jnp.exp(-x))


def _silu_and_grad(z):
    s = _sigmoid(z)
    return z * s, s * (1.0 + z * (1.0 - s))


def _split_bf16(x):
    hi = x.astype(BF16)
    lo = (x - hi.astype(F32)).astype(BF16)
    return hi, lo


def _dot(a, b, dims):
    return lax.dot_general(a, b, (dims, ((), ())), preferred_element_type=F32)


NN = ((1,), (0,))
NT = ((1,), (1,))
TN = ((0,), (0,))


def _matmul(name, a, b, *, grid, a_spec, b_spec, o_spec, out_shape, dims, res=None, res_spec=None, after=None,
            vmem_mb=48):
    nk = grid[2]
    n_in = 2 + (res is not None) + (after is not None)

    def body(*refs):
        a_ref, b_ref = refs[0], refs[1]
        r_ref = refs[2] if res is not None else None
        o_ref = refs[n_in]
        bv = b_ref[...]
        if bv.ndim == 3:
            bv = bv.reshape(-1, bv.shape[-1])
        part = _dot(a_ref[...], bv, dims)
        if nk == 1:
            if r_ref is not None:
                part = part + r_ref[...]
            o_ref[...] = part.astype(o_ref.dtype)
            return
        acc_ref = refs[n_in + 1]
        k = pl.program_id(2)

        @pl.when(k == 0)
        def _():
            acc_ref[...] = part

        @pl.when(k > 0)
        def _():
            acc_ref[...] += part

        @pl.when(k == nk - 1)
        def _():
            tot = acc_ref[...]
            if r_ref is not None:
                tot = tot + r_ref[...]
            o_ref[...] = tot.astype(o_ref.dtype)

    in_specs = [a_spec, b_spec]
    args = [a, b]
    if res is not None:
        in_specs.append(res_spec)
        args.append(res)
    if after is not None:
        in_specs.append(ANY)
        args.append(after)
    acc_shape = tuple(d for d in o_spec.block_shape if d is not None)
    scratch = [pltpu.VMEM(acc_shape, F32)] if nk > 1 else []
    return pl.pallas_call(
        body, name=name, grid=grid, in_specs=in_specs, out_specs=o_spec, out_shape=out_shape,
        scratch_shapes=scratch,
        compiler_params=_params(("parallel", "parallel", "arbitrary"), vmem_mb),
    )(*args)


def _rms_fwd(name, x, g, tr, after=None, transposed=False):
    rows, d = x.shape

    def body(x_ref, g_ref, *refs):
        outs = refs[1:] if after is not None else refs
        xv = x_ref[...]
        r = lax.rsqrt(jnp.mean(xv * xv, axis=-1, keepdims=True) + EPS)
        h = xv * r * g_ref[...]
        outs[0][...] = h.astype(BF16)
        if transposed:
            outs[1][...] = h.T.astype(BF16)

    out_specs = [pl.BlockSpec((tr, d), lambda i: (i, 0))]
    out_shape = [jax.ShapeDtypeStruct((rows, d), BF16)]
    if transposed:
        out_specs.append(pl.BlockSpec((d, tr), lambda i: (0, i)))
        out_shape.append(jax.ShapeDtypeStruct((d, rows), BF16))
    outs = pl.pallas_call(
        body, name=name, grid=(rows // tr,),
        in_specs=[pl.BlockSpec((tr, d), lambda i: (i, 0)), pl.BlockSpec((1, d), lambda i: (0, 0))]
        + ([] if after is None else [ANY]),
        out_specs=out_specs, out_shape=out_shape,
        compiler_params=_params(("parallel",)),
    )(x, g, *([] if after is None else [after]))
    return outs if transposed else outs[0]


def _rms_bwd(name, x, dh, dres, g, tr):
    rows, d = x.shape

    def body(x_ref, dh_ref, dres_ref, g_ref, dx_ref, dxb_ref, dg_ref):
        xv = x_ref[...]
        r = lax.rsqrt(jnp.mean(xv * xv, axis=-1, keepdims=True) + EPS)
        xhat = xv * r
        dhv = dh_ref[...]
        dxh = dhv * g_ref[...]
        dx = r * (dxh - xhat * jnp.mean(dxh * xhat, axis=-1, keepdims=True)) + dres_ref[...]
        dx_ref[...] = dx
        dxb_ref[...] = dx.astype(BF16)
        part = jnp.sum(dhv * xhat, axis=0, keepdims=True)

        @pl.when(pl.program_id(0) == 0)
        def _():
            dg_ref[...] = part

        @pl.when(pl.program_id(0) > 0)
        def _():
            dg_ref[...] += part

    blk = pl.BlockSpec((tr, d), lambda i: (i, 0))
    vec = pl.BlockSpec((1, d), lambda i: (0, 0))
    return pl.pallas_call(
        body, name=name, grid=(rows // tr,), in_specs=[blk, blk, blk, vec], out_specs=[blk, blk, vec],
        out_shape=[jax.ShapeDtypeStruct((rows, d), F32), jax.ShapeDtypeStruct((rows, d), BF16),
                   jax.ShapeDtypeStruct((1, d), F32)],
        compiler_params=_params(("arbitrary",)),
    )(x, dh, dres, g)


def _rms_gain_grad(name, x, dh):
    rows, d = x.shape

    def body(x_ref, dh_ref, dg_ref):
        xv = x_ref[...]
        r = lax.rsqrt(jnp.mean(xv * xv, axis=-1, keepdims=True) + EPS)
        dg_ref[...] = jnp.sum(dh_ref[...] * xv * r, axis=0, keepdims=True)

    return pl.pallas_call(
        body, name=name, out_shape=jax.ShapeDtypeStruct((1, d), F32), compiler_params=_params(None),
    )(x, dh)


def _loss_and_grad(name, y, target, tr):
    rows, d = y.shape
    n = rows // tr

    def body(y_ref, t_ref, dx_ref, dxb_ref, loss_ref, acc_ref):
        e = y_ref[...] - t_ref[...]
        dx = e * (1.0 / d)
        dx_ref[...] = dx
        dxb_ref[...] = dx.astype(BF16)
        part = jnp.sum(e * e, axis=0, keepdims=True)
        i = pl.program_id(0)

        @pl.when(i == 0)
        def _():
            acc_ref[...] = part

        @pl.when(i > 0)
        def _():
            acc_ref[...] += part

        @pl.when(i == n - 1)
        def _():
            loss_ref[...] = jnp.sum(acc_ref[...], axis=-1, keepdims=True) * (0.5 / d)

    blk = pl.BlockSpec((tr, d), lambda i: (i, 0))
    return pl.pallas_call(
        body, name=name, grid=(n,), in_specs=[blk, blk],
        out_specs=[blk, blk, pl.BlockSpec((1, 1), lambda i: (0, 0))],
        out_shape=[jax.ShapeDtypeStruct((rows, d), F32), jax.ShapeDtypeStruct((rows, d), BF16),
                   jax.ShapeDtypeStruct((1, 1), F32)],
        scratch_shapes=[pltpu.VMEM((1, d), F32)],
        compiler_params=_params(("arbitrary",)),
    )(y, target)


SB_T = 256
SB_HEADS = 2


def _sb_scores(q, kblk):
    z = _dot(q, kblk, NT) * ATT_SCALE
    e = jnp.exp(-jnp.abs(z))
    sp = jnp.log1p(e)
    lb = jnp.minimum(z, 0.0) - sp
    l1 = lb - z
    return z, e, lb, l1


def _sb_fwd(name, proj):
    s_len = proj.shape[0]
    t = SB_T
    nq = s_len // t

    def body(q_ref, k_ref, v_ref, o_ref):
        i = pl.program_id(1)
        row = lax.broadcasted_iota(jnp.int32, (t, t), 0)
        col = lax.broadcasted_iota(jnp.int32, (t, t), 1)
        causal = col < row
        after_mat = (row > col).astype(BF16)
        heads = [slice(hh * HEAD_DIM, (hh + 1) * HEAD_DIM) for hh in range(SB_HEADS)]
        q = [q_ref[:, sl].astype(BF16) for sl in heads]

        def tile(kb, state, masked):
            start = pl.multiple_of(kb * t, t)
            out = []
            for hh, sl in enumerate(heads):
                carry, acc = state[hh]
                kblk = k_ref[pl.ds(start, t), sl].astype(BF16)
                vblk = v_ref[pl.ds(start, t), sl].astype(BF16)
                _, _, lb, l1 = _sb_scores(q[hh], kblk)
                if masked:
                    l1 = jnp.where(causal, l1, 0.0)
                hi, lo = _split_bf16(l1)
                after = _dot(hi, after_mat, NN) + _dot(lo, after_mat, NN) + carry
                a = jnp.exp(lb + after)
                if masked:
                    a = jnp.where(causal, a, 0.0)
                acc = acc + _dot(a.astype(BF16), vblk, NN)
                carry = carry + jnp.sum(l1, axis=-1, keepdims=True)
                out.append((carry, acc))
            return tuple(out)

        zero = (jnp.zeros((t, 1), F32), jnp.zeros((t, HEAD_DIM), F32))
        state = tile(i, (zero,) * SB_HEADS, True)
        state = lax.fori_loop(0, i, lambda n, st: tile(i - 1 - n, st, False), state)
        for hh, sl in enumerate(heads):
            o_ref[:, sl] = state[hh][1]

    cb = SB_HEADS * HEAD_DIM
    return pl.pallas_call(
        body, name=name, grid=(HEADS // SB_HEADS, nq),
        in_specs=[pl.BlockSpec((t, cb), lambda h, i: (i, OFF_QB // cb + h)),
                  pl.BlockSpec((s_len, cb), lambda h, i: (0, OFF_KB // cb + h)),
                  pl.BlockSpec((s_len, cb), lambda h, i: (0, OFF_VB // cb + h))],
        out_specs=pl.BlockSpec((t, cb), lambda h, i: (i, h)),
        out_shape=jax.ShapeDtypeStruct((s_len, D_B), F32),
        compiler_params=_params(("parallel", "arbitrary")),
    )(proj, proj, proj)


def _sb_bwd(name, proj, dy):
    s_len = proj.shape[0]
    t = SB_T
    nq = s_len // t

    def body(q_ref, k_ref, v_ref, z_ref, dy_ref, dq_ref, dk_ref, dv_ref, a_ref, s_ref):
        i = pl.program_id(1)

        @pl.when(i == 0)
        def _():
            dk_ref[...] = jnp.zeros_like(dk_ref)
            dv_ref[...] = jnp.zeros_like(dv_ref)

        heads = [slice(hh * HEAD_DIM, (hh + 1) * HEAD_DIM) for hh in range(SB_HEADS)]
        q = [q_ref[:, sl].astype(BF16) for sl in heads]
        silu_z, _ = _silu_and_grad(z_ref[...])
        do_all = dy_ref[...] * silu_z
        do_b = [do_all[:, sl].astype(BF16) for sl in heads]
        row = lax.broadcasted_iota(jnp.int32, (t, t), 0)
        col = lax.broadcasted_iota(jnp.int32, (t, t), 1)
        causal = col < row
        after_mat = (row > col).astype(BF16)
        before_mat = (row < col).astype(BF16)

        def weights(kb, carries, masked):
            start = pl.multiple_of(kb * t, t)
            out = []
            for hh, sl in enumerate(heads):
                kblk = k_ref[pl.ds(start, t), sl].astype(BF16)
                z, _, lb, l1 = _sb_scores(q[hh], kblk)
                if masked:
                    l1 = jnp.where(causal, l1, 0.0)
                hi, lo = _split_bf16(l1)
                after = _dot(hi, after_mat, NN) + _dot(lo, after_mat, NN) + carries[hh]
                a = jnp.exp(lb + after)
                if masked:
                    a = jnp.where(causal, a, 0.0)
                a_ref[hh, kb] = a
                s_ref[hh, kb] = z
                out.append(carries[hh] + jnp.sum(l1, axis=-1, keepdims=True))
            return tuple(out)

        carries = weights(i, (jnp.zeros((t, 1), F32),) * SB_HEADS, True)
        lax.fori_loop(0, i, lambda n, c: weights(i - 1 - n, c, False), carries)

        def grads(kb, state, masked):
            start = pl.multiple_of(kb * t, t)
            out = []
            for hh, sl in enumerate(heads):
                carry, dq = state[hh]
                kblk = k_ref[pl.ds(start, t), sl].astype(BF16)
                vblk = v_ref[pl.ds(start, t), sl].astype(BF16)
                a = a_ref[hh, kb]
                z = s_ref[hh, kb]
                g = _dot(do_b[hh], vblk, NT) * a
                ghi, glo = _split_bf16(g)
                prefix = _dot(ghi, before_mat, NN) + _dot(glo, before_mat, NN) + carry
                e = jnp.exp(-jnp.abs(z))
                inv = 1.0 / (1.0 + e)
                pos = z >= 0.0
                beta = jnp.where(pos, inv, e * inv)
                one_m_beta = jnp.where(pos, e * inv, inv)
                dz = (g * one_m_beta - prefix * beta) * ATT_SCALE
                if masked:
                    dz = jnp.where(causal, dz, 0.0)
                dz_b = dz.astype(BF16)
                dq = dq + _dot(dz_b, kblk, NN)
                dk_ref[pl.ds(start, t), sl] += _dot(dz_b, q[hh], TN)
                dv_ref[pl.ds(start, t), sl] += _dot(a.astype(BF16), do_b[hh], TN)
                out.append((carry + jnp.sum(g, axis=-1, keepdims=True), dq))
            return tuple(out)

        zero = (jnp.zeros((t, 1), F32), jnp.zeros((t, HEAD_DIM), F32))
        state = lax.fori_loop(0, i, lambda kb, st: grads(kb, st, False), (zero,) * SB_HEADS)
        state = grads(i, state, True)
        for hh, sl in enumerate(heads):
            dq_ref[:, sl] = state[hh][1]

    cb = SB_HEADS * HEAD_DIM
    qblk = lambda off: pl.BlockSpec((t, cb), lambda h, i: (i, off // cb + h))
    full = lambda off: pl.BlockSpec((s_len, cb), lambda h, i: (0, off // cb + h))
    out = jax.ShapeDtypeStruct((s_len, D_B), F32)
    return pl.pallas_call(
        body, name=name, grid=(HEADS // SB_HEADS, nq),
        in_specs=[qblk(OFF_QB), full(OFF_KB), full(OFF_VB), qblk(OFF_ZB), qblk(OFF_YB)],
        out_specs=[qblk(0), full(0), full(0)],
        out_shape=[out, out, out],
        scratch_shapes=[pltpu.VMEM((SB_HEADS, nq, t, t), F32), pltpu.VMEM((SB_HEADS, nq, t, t), F32)],
        compiler_params=_params(("parallel", "arbitrary")),
    )(proj, proj, proj, proj, dy)


MEM_TQ = 512


def _qk_norm(x, g):
    r = lax.rsqrt(jnp.mean(x * x, axis=-1, keepdims=True) + EPS)
    xhat = x * r
    return xhat * g, xhat, r


def _qk_norm_bwd(dn, g, xhat, r):
    dxh = dn * g
    return r * (dxh - xhat * jnp.mean(dxh * xhat, axis=-1, keepdims=True))


def _mem_probs(q, mk, qg, kg):
    qn, qhat, rq = _qk_norm(q, qg)
    kn, khat, rk = _qk_norm(mk, kg)
    qn_b, kn_b = qn.astype(BF16), kn.astype(BF16)
    s = _dot(qn_b, kn_b, NT) * ATT_SCALE
    p = jnp.exp(s - jnp.max(s, axis=-1, keepdims=True))
    p = p / jnp.sum(p, axis=-1, keepdims=True)
    return p, qn_b, kn_b, qhat, rq, khat, rk


def _mem_fwd(name, proj, mem_kv, qg, kg):
    s_len = proj.shape[0]
    m_len = mem_kv.shape[0]
    tq = min(MEM_TQ, s_len)

    def body(q_ref, mk_ref, mv_ref, qg_ref, kg_ref, o_ref):
        p = _mem_probs(q_ref[...], mk_ref[...], qg_ref[...], kg_ref[...])[0]
        o_ref[...] = _dot(p.astype(BF16), mv_ref[...].astype(BF16), NN)

    cb = HEAD_DIM
    vec = pl.BlockSpec((1, cb), lambda h, i: (0, 0))
    return pl.pallas_call(
        body, name=name, grid=(HEADS, s_len // tq),
        in_specs=[pl.BlockSpec((tq, cb), lambda h, i: (i, OFF_QC // cb + h)),
                  pl.BlockSpec((m_len, cb), lambda h, i: (0, h)),
                  pl.BlockSpec((m_len, cb), lambda h, i: (0, HEADS + h)), vec, vec],
        out_specs=pl.BlockSpec((tq, cb), lambda h, i: (i, h)),
        out_shape=jax.ShapeDtypeStruct((s_len, D_C), F32),
        compiler_params=_params(("parallel", "parallel")),
    )(proj, mem_kv, mem_kv, qg, kg)


def _mem_bwd(name, proj, mem_kv, qg, kg, dy):
    s_len = proj.shape[0]
    m_len = mem_kv.shape[0]
    tq = min(MEM_TQ, s_len)

    def body(q_ref, mk_ref, mv_ref, qg_ref, kg_ref, z_ref, dy_ref, dq_ref, dmk_ref, dmv_ref, dqg_ref, dkg_ref):
        h, i = pl.program_id(0), pl.program_id(1)

        @pl.when(i == 0)
        def _():
            dmk_ref[...] = jnp.zeros_like(dmk_ref)
            dmv_ref[...] = jnp.zeros_like(dmv_ref)

        @pl.when((i == 0) & (h == 0))
        def _():
            dqg_ref[...] = jnp.zeros_like(dqg_ref)
            dkg_ref[...] = jnp.zeros_like(dkg_ref)

        qg, kg = qg_ref[...], kg_ref[...]
        p, qn_b, kn_b, qhat, rq, khat, rk = _mem_probs(q_ref[...], mk_ref[...], qg, kg)
        silu_z, _ = _silu_and_grad(z_ref[...])
        do_b = (dy_ref[...] * silu_z).astype(BF16)
        dmv_ref[...] += _dot(p.astype(BF16), do_b, TN)
        dp = _dot(do_b, mv_ref[...].astype(BF16), NT)
        ds = (p * (dp - jnp.sum(dp * p, axis=-1, keepdims=True)) * ATT_SCALE).astype(BF16)
        dqn = _dot(ds, kn_b, NN)
        dkn = _dot(ds, qn_b, TN)
        dq_ref[...] = _qk_norm_bwd(dqn, qg, qhat, rq)
        dmk_ref[...] += _qk_norm_bwd(dkn, kg, khat, rk)
        dqg_ref[...] += jnp.sum(dqn * qhat, axis=0, keepdims=True)
        dkg_ref[...] += jnp.sum(dkn * khat, axis=0, keepdims=True)

    cb = HEAD_DIM
    vec = pl.BlockSpec((1, cb), lambda h, i: (0, 0))
    qblk = lambda off: pl.BlockSpec((tq, cb), lambda h, i: (i, off // cb + h))
    memblk = lambda off: pl.BlockSpec((m_len, cb), lambda h, i: (0, off + h))
    return pl.pallas_call(
        body, name=name, grid=(HEADS, s_len // tq),
        in_specs=[qblk(OFF_QC), memblk(0), memblk(HEADS), vec, vec, qblk(OFF_ZC), qblk(OFF_YC)],
        out_specs=[qblk(0), memblk(0), memblk(0), vec, vec],
        out_shape=[jax.ShapeDtypeStruct((s_len, D_C), F32), jax.ShapeDtypeStruct((m_len, D_C), F32),
                   jax.ShapeDtypeStruct((m_len, D_C), F32), jax.ShapeDtypeStruct((1, cb), F32),
                   jax.ShapeDtypeStruct((1, cb), F32)],
        compiler_params=_params(("arbitrary", "arbitrary")),
    )(proj, mem_kv, mem_kv, qg, kg, proj, dy)


def _sgu_common(u_ref, v_ref, lng_ref, lnb_ref, w_ref, bias_ref):
    ug = _gelu(u_ref[...])
    vg = _gelu(v_ref[...])
    mu = jnp.mean(vg, axis=-1, keepdims=True)
    xc = vg - mu
    rstd = lax.rsqrt(jnp.mean(xc * xc, axis=-1, keepdims=True) + EPS)
    xhat = xc * rstd
    vn = xhat * lng_ref[...] + lnb_ref[...]
    vn_b = vn.astype(BF16)
    row = lax.broadcasted_iota(jnp.int32, (CHUNK, CHUNK), 0)
    col = lax.broadcasted_iota(jnp.int32, (CHUNK, CHUNK), 1)
    tril = row >= col
    mixed = []
    for g in range(A_GROUPS):
        w = jnp.where(tril, w_ref[g], 0.0).astype(BF16)
        sl = slice(g * CHUNK, (g + 1) * CHUNK)
        mixed.append(_dot(w, vn_b[:, sl], NN) + bias_ref[:, sl])
    return ug, xhat, rstd, vn_b, mixed, tril


def _gate_fwd(name, proj, o_b, o_c, lng, lnb, w_s, bias):
    s_len = proj.shape[0]

    def body(u_ref, v_ref, za_ref, zb_ref, zc_ref, ob_ref, oc_ref, lng_ref, lnb_ref, w_ref, bias_ref, y_ref, yt_ref):
        ug, _, _, _, mixed, _ = _sgu_common(u_ref, v_ref, lng_ref, lnb_ref, w_ref, bias_ref)
        sza, _ = _silu_and_grad(za_ref[...])
        gate = ug * sza

        def put(off, width, val):
            y_ref[:, off:off + width] = val.astype(BF16)
            yt_ref[off:off + width, :] = val.T.astype(BF16)

        for g in range(A_GROUPS):
            sl = slice(g * CHUNK, (g + 1) * CHUNK)
            put(g * CHUNK, CHUNK, gate[:, sl] * mixed[g])
        szb, _ = _silu_and_grad(zb_ref[...])
        put(OFF_YB, D_B, ob_ref[...] * szb)
        szc, _ = _silu_and_grad(zc_ref[...])
        put(OFF_YC, D_C, oc_ref[...] * szc)

    wide = lambda off: pl.BlockSpec((CHUNK, D_A), lambda i: (i, off // D_A))
    narrow = lambda off: pl.BlockSpec((CHUNK, D_B), lambda i: (i, off // D_B))
    vec = pl.BlockSpec((1, D_A), lambda i: (0, 0))
    return pl.pallas_call(
        body, name=name, grid=(s_len // CHUNK,),
        in_specs=[wide(OFF_U), wide(OFF_V), wide(OFF_ZA), narrow(OFF_ZB), narrow(OFF_ZC), narrow(0), narrow(0), vec, vec,
                  pl.BlockSpec((A_GROUPS, CHUNK, CHUNK), lambda i: (0, 0, 0)),
                  pl.BlockSpec((CHUNK, D_A), lambda i: (0, 0))],
        out_specs=[pl.BlockSpec((CHUNK, D_MODEL), lambda i: (i, 0)), pl.BlockSpec((D_MODEL, CHUNK), lambda i: (0, i))],
        out_shape=[jax.ShapeDtypeStruct((s_len, D_MODEL), BF16), jax.ShapeDtypeStruct((D_MODEL, s_len), BF16)],
        compiler_params=_params(("parallel",)),
    )(proj, proj, proj, proj, proj, o_b, o_c, lng, lnb, w_s, bias)


def _gate_bwd(name, proj, dy, o_b, o_c, dqkv, dq_c, lng, lnb, w_s, w_s_t, bias):
    s_len = proj.shape[0]
    n = s_len // CHUNK
    dq_b, dk_b, dv_b = dqkv

    def body(u_ref, v_ref, za_ref, zb_ref, zc_ref, dya_ref, dyb_ref, dyc_ref, ob_ref, oc_ref, dq_ref, dk_ref, dv_ref,
             dqc_ref, lng_ref, lnb_ref, w_ref, wt_ref, bias_ref, dp_ref, dw_ref, dsb_ref, dlng_ref, dlnb_ref, dbias_ref):
        i = pl.program_id(0)

        @pl.when(i == 0)
        def _():
            dw_ref[...] = jnp.zeros_like(dw_ref)
            dbias_ref[...] = jnp.zeros_like(dbias_ref)
            dlng_ref[...] = jnp.zeros_like(dlng_ref)
            dlnb_ref[...] = jnp.zeros_like(dlnb_ref)

        ug, xhat, rstd, vn_b, mixed, tril = _sgu_common(u_ref, v_ref, lng_ref, lnb_ref, w_ref, bias_ref)
        za = za_ref[...]
        sza, dsza = _silu_and_grad(za)
        dya = dya_ref[...]
        mixed_all = jnp.concatenate(mixed, axis=-1)
        d_mixed = dya * ug * sza
        dp_ref[:, OFF_U:OFF_U + D_A] = (dya * mixed_all * sza * _gelu_grad(u_ref[...])).astype(BF16)
        dp_ref[:, OFF_ZA:OFF_ZA + D_A] = (dya * ug * mixed_all * dsza).astype(BF16)
        dbias_ref[...] += d_mixed
        dm_b = d_mixed.astype(BF16)
        triu = lax.broadcasted_iota(jnp.int32, (CHUNK, CHUNK), 0) <= lax.broadcasted_iota(jnp.int32, (CHUNK, CHUNK), 1)
        d_vn = []
        for g in range(A_GROUPS):
            sl = slice(g * CHUNK, (g + 1) * CHUNK)
            wt = jnp.where(triu, wt_ref[g], 0.0).astype(BF16)
            d_vn.append(_dot(wt, dm_b[:, sl], NN))
            dw_ref[g] += jnp.where(tril, _dot(dm_b[:, sl], vn_b[:, sl], NT), 0.0)
        d_vn = jnp.concatenate(d_vn, axis=-1)
        dlng_ref[...] += jnp.sum(d_vn * xhat, axis=0, keepdims=True)
        dlnb_ref[...] += jnp.sum(d_vn, axis=0, keepdims=True)
        dxh = d_vn * lng_ref[...]
        d_vg = rstd * (dxh - jnp.mean(dxh, axis=-1, keepdims=True)
                       - xhat * jnp.mean(dxh * xhat, axis=-1, keepdims=True))
        dp_ref[:, OFF_V:OFF_V + D_A] = (d_vg * _gelu_grad(v_ref[...])).astype(BF16)
        dp_ref[:, OFF_QB:OFF_QB + D_B] = dq_ref[...].astype(BF16)
        dp_ref[:, OFF_KB:OFF_KB + D_B] = dk_ref[...].astype(BF16)
        dp_ref[:, OFF_VB:OFF_VB + D_B] = dv_ref[...].astype(BF16)
        _, dszb = _silu_and_grad(zb_ref[...])
        dp_ref[:, OFF_ZB:OFF_ZB + D_B] = (dyb_ref[...] * ob_ref[...] * dszb).astype(BF16)
        dp_ref[:, OFF_QC:OFF_QC + D_C] = dqc_ref[...].astype(BF16)
        _, dszc = _silu_and_grad(zc_ref[...])
        dp_ref[:, OFF_ZC:OFF_ZC + D_C] = (dyc_ref[...] * oc_ref[...] * dszc).astype(BF16)

        @pl.when(i == n - 1)
        def _():
            ch = lax.broadcasted_iota(jnp.int32, (D_A, CHUNK), 0)
            gcol = lax.broadcasted_iota(jnp.int32, (D_A, CHUNK), 1)
            pick = (ch // (D_A // A_GROUPS) == gcol).astype(BF16)
            rest = dbias_ref[...]
            tot = jnp.zeros((CHUNK, CHUNK), F32)
            for _ in range(3):
                term = rest.astype(BF16)
                tot = tot + _dot(term, pick, NN)
                rest = rest - term.astype(F32)
            dsb_ref[...] = tot

    wide = lambda off: pl.BlockSpec((CHUNK, D_A), lambda i: (i, off // D_A))
    narrow = lambda off: pl.BlockSpec((CHUNK, D_B), lambda i: (i, off // D_B))
    vec = pl.BlockSpec((1, D_A), lambda i: (0, 0))
    wspec = pl.BlockSpec((A_GROUPS, CHUNK, CHUNK), lambda i: (0, 0, 0))
    bspec = pl.BlockSpec((CHUNK, D_A), lambda i: (0, 0))
    return pl.pallas_call(
        body, name=name, grid=(n,),
        in_specs=[wide(OFF_U), wide(OFF_V), wide(OFF_ZA), narrow(OFF_ZB), narrow(OFF_ZC),
                  wide(0), narrow(OFF_YB), narrow(OFF_YC), narrow(0), narrow(0), narrow(0), narrow(0), narrow(0),
                  narrow(0), vec, vec, wspec, wspec, bspec],
        out_specs=[pl.BlockSpec((CHUNK, IN_WIDTH), lambda i: (i, 0)), wspec,
                   pl.BlockSpec((CHUNK, CHUNK), lambda i: (0, 0)), vec, vec],
        out_shape=[jax.ShapeDtypeStruct((s_len, IN_WIDTH), BF16), jax.ShapeDtypeStruct((A_GROUPS, CHUNK, CHUNK), F32),
                   jax.ShapeDtypeStruct((CHUNK, CHUNK), F32), jax.ShapeDtypeStruct((1, D_A), F32),
                   jax.ShapeDtypeStruct((1, D_A), F32)],
        scratch_shapes=[pltpu.VMEM((CHUNK, D_A), F32)],
        compiler_params=_params(("arbitrary",)),
    )(proj, proj, proj, proj, proj, dy, dy, dy, o_b, o_c, dq_b, dk_b, dv_b, dq_c, lng, lnb, w_s, w_s_t, bias)


IN_SHARD = IN_WIDTH // N_CHIPS
ROW_SHARD = D_MODEL // N_CHIPS


def _bias_rows(sgu_b_l):
    return jnp.repeat(sgu_b_l.T, D_A // A_GROUPS, axis=1)


def _layer_fwd(l, x, mem, sm, w_in_all, rest):
    s_len = x.shape[0]
    m_len = mem.shape[0]
    tm = min(1024, s_len)
    tn = 768
    per = IN_SHARD // tn
    h, h_t = _rms_fwd(f"rms_fwd_{l}", x, sm["norm_g"][l][None], min(256, s_len), transposed=True)
    proj = _matmul(
        f"in_proj_{l}", h, w_in_all, grid=(s_len // tm, IN_WIDTH // tn, 1),
        a_spec=pl.BlockSpec((tm, D_MODEL), lambda i, j, k: (i, 0)),
        b_spec=pl.BlockSpec((None, D_MODEL, tn), lambda i, j, k: (j // per, 0, j % per)),
        o_spec=pl.BlockSpec((tm, tn), lambda i, j, k: (i, j)),
        out_shape=jax.ShapeDtypeStruct((s_len, IN_WIDTH), F32), dims=NN)
    w_kv_all, w_out_all, after = rest(proj)
    mem_h = _rms_fwd(f"mem_rms_fwd_{l}", mem, sm["mem_norm_g"][l][None], m_len, after)
    mem_kv = _matmul(
        f"mem_kv_{l}", mem_h, w_kv_all, grid=(1, 2, N_CHIPS),
        a_spec=pl.BlockSpec((m_len, ROW_SHARD), lambda i, j, k: (0, k)),
        b_spec=pl.BlockSpec((None, ROW_SHARD, D_C), lambda i, j, k: (k, 0, j)),
        o_spec=pl.BlockSpec((m_len, D_C), lambda i, j, k: (0, j)),
        out_shape=jax.ShapeDtypeStruct((m_len, 2 * D_C), F32), dims=NN)
    o_b = _sb_fwd(f"sb_fwd_{l}", proj)
    qg, kg = sm["q_norm_g"][l][None], sm["k_norm_g"][l][None]
    o_c = _mem_fwd(f"mem_fwd_{l}", proj, mem_kv, qg, kg)
    bias = _bias_rows(sm["sgu_b"][l])
    y, y_t = _gate_fwd(f"gate_fwd_{l}", proj, o_b, o_c, sm["sgu_ln_g"][l][None], sm["sgu_ln_b"][l][None],
                       sm["sgu_w"][l], bias)
    tn_o = 512
    x_next = _matmul(
        f"out_proj_{l}", y, w_out_all, grid=(s_len // tm, D_MODEL // tn_o, 1),
        a_spec=pl.BlockSpec((tm, D_MODEL), lambda i, j, k: (i, 0)),
        b_spec=pl.BlockSpec((N_CHIPS, ROW_SHARD, tn_o), lambda i, j, k: (0, 0, j)),
        o_spec=pl.BlockSpec((tm, tn_o), lambda i, j, k: (i, j)),
        out_shape=jax.ShapeDtypeStruct((s_len, D_MODEL), F32), dims=NN,
        res=x, res_spec=pl.BlockSpec((tm, tn_o), lambda i, j, k: (i, j)))
    saved = dict(x=x, h_t=h_t, proj=proj, mem_h=mem_h, mem_kv=mem_kv, o_b=o_b, o_c=o_c, y_t=y_t, bias=bias,
                 weights=(w_in_all, w_kv_all, w_out_all))
    return x_next, saved


def _layer_bwd(l, dxo, dxo_b, mem, sm, saved, on_weight_grads=None):
    s_len = dxo.shape[0]
    m_len = mem.shape[0]
    proj, y_t, h_t, mem_h, mem_kv = saved["proj"], saved["y_t"], saved["h_t"], saved["mem_h"], saved["mem_kv"]
    w_in_all, w_kv_all, w_out_all = saved["weights"]
    tm = min(1024, s_len)
    tk = min(1024, s_len)
    g_out = _matmul(
        f"d_w_out_{l}", y_t, dxo_b, grid=(N_CHIPS, D_MODEL // 1024, s_len // tk),
        a_spec=pl.BlockSpec((ROW_SHARD, tk), lambda i, j, k: (i, k)),
        b_spec=pl.BlockSpec((tk, 1024), lambda i, j, k: (k, j)),
        o_spec=pl.BlockSpec((None, ROW_SHARD, 1024), lambda i, j, k: (i, 0, j)),
        out_shape=jax.ShapeDtypeStruct((N_CHIPS, ROW_SHARD, D_MODEL), F32), dims=NN)
    token = None if on_weight_grads is None else on_weight_grads("out", [g_out])
    dy = _matmul(
        f"d_y_{l}", dxo_b, w_out_all, grid=(s_len // tm, N_CHIPS, 1),
        a_spec=pl.BlockSpec((tm, D_MODEL), lambda i, j, k: (i, 0)),
        b_spec=pl.BlockSpec((None, ROW_SHARD, D_MODEL), lambda i, j, k: (j, 0, 0)),
        o_spec=pl.BlockSpec((tm, ROW_SHARD), lambda i, j, k: (i, j)),
        out_shape=jax.ShapeDtypeStruct((s_len, D_MODEL), F32), dims=NT, after=token)
    qg, kg = sm["q_norm_g"][l][None], sm["k_norm_g"][l][None]
    dq_c, dmk, dmv, dqg, dkg = _mem_bwd(f"mem_bwd_{l}", proj, mem_kv, qg, kg, dy)
    dqkv = _sb_bwd(f"sb_bwd_{l}", proj, dy)
    w_s = sm["sgu_w"][l]
    dproj, dws, dbias, dlng, dlnb = _gate_bwd(
        f"gate_bwd_{l}", proj, dy, saved["o_b"], saved["o_c"], dqkv, dq_c, sm["sgu_ln_g"][l][None],
        sm["sgu_ln_b"][l][None], w_s, jnp.swapaxes(w_s, 1, 2), saved["bias"])
    tn = 768
    per = IN_SHARD // tn
    g_in = _matmul(
        f"d_w_in_{l}", h_t, dproj, grid=(D_MODEL // 1024, IN_WIDTH // tn, s_len // tk),
        a_spec=pl.BlockSpec((1024, tk), lambda i, j, k: (i, k)),
        b_spec=pl.BlockSpec((tk, tn), lambda i, j, k: (k, j)),
        o_spec=pl.BlockSpec((None, 1024, tn), lambda i, j, k: (j // per, i, j % per)),
        out_shape=jax.ShapeDtypeStruct((N_CHIPS, D_MODEL, IN_SHARD), F32), dims=NN)
    dkv_b = jnp.concatenate([dmk, dmv], axis=1).astype(BF16)
    g_kv = _matmul(
        f"d_w_kv_{l}", mem_h, dkv_b, grid=(N_CHIPS, 1, 1),
        a_spec=pl.BlockSpec((m_len, ROW_SHARD), lambda i, j, k: (0, i)),
        b_spec=pl.BlockSpec((m_len, 2 * D_C), lambda i, j, k: (0, 0)),
        o_spec=pl.BlockSpec((None, ROW_SHARD, 2 * D_C), lambda i, j, k: (i, 0, 0)),
        out_shape=jax.ShapeDtypeStruct((N_CHIPS, ROW_SHARD, 2 * D_C), F32), dims=TN)
    token = None if on_weight_grads is None else on_weight_grads("in", [g_in, g_kv])
    dh = _matmul(
        f"d_h_{l}", dproj, w_in_all, grid=(s_len // tm, D_MODEL // 1024, N_CHIPS),
        a_spec=pl.BlockSpec((tm, IN_SHARD), lambda i, j, k: (i, k)),
        b_spec=pl.BlockSpec((None, 1024, IN_SHARD), lambda i, j, k: (k, j, 0)),
        o_spec=pl.BlockSpec((tm, 1024), lambda i, j, k: (i, j)),
        out_shape=jax.ShapeDtypeStruct((s_len, D_MODEL), F32), dims=NT, after=token)
    dx, dx_b, dng = _rms_bwd(f"rms_bwd_{l}", saved["x"], dh, dxo, sm["norm_g"][l][None], min(256, s_len))
    d_mem_h = _matmul(
        f"d_mem_h_{l}", dkv_b, w_kv_all, grid=(1, N_CHIPS, 1),
        a_spec=pl.BlockSpec((m_len, 2 * D_C), lambda i, j, k: (0, 0)),
        b_spec=pl.BlockSpec((None, ROW_SHARD, 2 * D_C), lambda i, j, k: (j, 0, 0)),
        o_spec=pl.BlockSpec((m_len, ROW_SHARD), lambda i, j, k: (0, j)),
        out_shape=jax.ShapeDtypeStruct((m_len, D_MODEL), F32), dims=NT)
    dmng = _rms_gain_grad(f"mem_rms_bwd_{l}", mem, d_mem_h)
    dsgu_b = dbias[:, :A_GROUPS].T
    small = dict(norm_g=dng[0], sgu_ln_g=dlng[0], sgu_ln_b=dlnb[0], sgu_w=dws, sgu_b=dsgu_b, mem_norm_g=dmng[0],
                 q_norm_g=dqg[0], k_norm_g=dkg[0])
    return dx, dx_b, small, g_in, g_kv, g_out


SMALL_NAMES = ("norm_g", "sgu_ln_g", "sgu_ln_b", "sgu_w", "sgu_b", "mem_norm_g", "q_norm_g", "k_norm_g")


def _local_step(x, mem, target, sm, w_all):
    saved = []
    cur = x
    for l in range(DEPTH):
        cur, sv = _layer_fwd(l, cur, mem, sm, w_all[l][0], lambda proj, l=l: (w_all[l][1], w_all[l][2], None))
        saved.append(sv)
    dxo, dxo_b, loss = _loss_and_grad("loss", cur, target, min(256, x.shape[0]))
    small, big = [None] * DEPTH, [None] * DEPTH
    for l in reversed(range(DEPTH)):
        dxo, dxo_b, small[l], *big[l] = _layer_bwd(l, dxo, dxo_b, mem, sm, saved[l])
    small = {k: jnp.stack([small[l][k] for l in range(DEPTH)]) for k in SMALL_NAMES}
    return loss, dxo, small, big


def _place():
    x, y, c = lax.axis_index("x"), lax.axis_index("y"), lax.axis_index("c")
    return x, y, c


def _other_chips(x, y):
    return [(1 - x, y, 2 * (1 - x) + y), (x, 1 - y, 2 * x + 1 - y), (1 - x, 1 - y, 2 * (1 - x) + 1 - y)]


AG_CHUNKS = 4
D2D_CHUNKS = 8


def _place_index():
    return jnp.stack([2 * lax.axis_index("x") + lax.axis_index("y"), lax.axis_index("c")]).astype(jnp.int32)


def _cast_into_slot(name, w, l, place):
    _, rows, cols = w.shape
    tr = min(256, rows)

    def body(p_ref, w_ref, o_ref):
        o_ref[...] = w_ref[...].astype(BF16)

    return pl.pallas_call(
        body, name=name,
        grid_spec=pltpu.PrefetchScalarGridSpec(
            num_scalar_prefetch=1, grid=(rows // tr,),
            in_specs=[pl.BlockSpec((None, tr, cols), lambda i, p: (l, i, 0))],
            out_specs=pl.BlockSpec((None, tr, cols), lambda i, p: (p[0], i, 0))),
        out_shape=jax.ShapeDtypeStruct((N_CHIPS, rows, cols), BF16),
        compiler_params=_params(("parallel",)),
    )(place, w)


HBM = pl.BlockSpec(memory_space=pltpu.HBM)
SEM = pl.BlockSpec(memory_space=pltpu.SEMAPHORE)
DATAFLOW = pltpu.SideEffectType.DATAFLOW_SIDE_EFFECTING


def _in_hbm(a):
    return pltpu.with_memory_space_constraint(a, pltpu.HBM)


def _chip_copies_start(name, srcs, lands, make_copy, after=None):
    n_t = len(srcs)
    in_place = lands is None
    n_after = 0 if after is None else 1

    def body(*refs):
        src = refs[:n_t]
        k = (n_t if in_place else 2 * n_t) + n_after
        send_sems, recv_sems = refs[k], refs[k + 1]
        land = refs[k + 2:k + 2 + n_t] if in_place else refs[k + 2 + n_t:k + 2 + 2 * n_t]
        token = refs[-1]
        x, y, c = _place()
        me = 2 * x + y
        for t in range(n_t):
            for px, py, pk in _other_chips(x, y):
                s, d = make_copy(src[t], land[t], me, pk, c)
                pltpu.make_async_remote_copy(
                    src_ref=s, dst_ref=d, send_sem=send_sems.at[t], recv_sem=recv_sems.at[t],
                    device_id=(px, py, c), device_id_type=MESH).start()
        token[...] = jnp.zeros_like(token)

    bufs = list(srcs) if in_place else list(srcs) + list(lands)
    outs = pl.pallas_call(
        body, name=name,
        in_specs=[HBM] * len(bufs) + [ANY] * n_after,
        out_specs=[SEM, SEM] + [HBM] * len(bufs) + [pl.BlockSpec(memory_space=pltpu.VMEM)],
        out_shape=[pltpu.SemaphoreType.DMA((n_t,)), pltpu.SemaphoreType.DMA((n_t,))]
        + [pltpu.HBM(b.shape, b.dtype) for b in bufs] + [jax.ShapeDtypeStruct((8, 128), F32)],
        input_output_aliases={i: 2 + i for i in range(len(bufs))},
        compiler_params=pltpu.CompilerParams(has_side_effects=DATAFLOW),
    )(*[_in_hbm(b) for b in bufs], *([] if after is None else [after]))
    return outs[0], outs[1], list(outs[2:2 + len(bufs)]), outs[-1]


def _chip_copies_wait(name, send_sems, recv_sems, bufs, sent, landed, after):
    n_b = len(bufs)

    def body(*refs):
        buf = refs[:n_b]
        send_ref, recv_ref = refs[n_b], refs[n_b + 1]
        x, y, c = _place()
        for t, (s, d) in enumerate(zip(sent(buf), landed(buf))):
            out = pltpu.make_async_remote_copy(src_ref=s, dst_ref=s, send_sem=send_ref.at[t], recv_sem=recv_ref.at[t],
                                               device_id=(x, y, c), device_id_type=MESH)
            out.wait_send()
            arrived = pltpu.make_async_remote_copy(src_ref=d, dst_ref=d, send_sem=send_ref.at[t],
                                                   recv_sem=recv_ref.at[t], device_id=(x, y, c), device_id_type=MESH)
            arrived.wait_recv()

    return pl.pallas_call(
        body, name=name,
        in_specs=[HBM] * n_b + [SEM, SEM, ANY], out_specs=[HBM] * n_b,
        out_shape=[pltpu.HBM(b.shape, b.dtype) for b in bufs],
        input_output_aliases={i: i for i in range(n_b)},
        compiler_params=pltpu.CompilerParams(has_side_effects=DATAFLOW),
    )(*bufs, send_sems, recv_sems, after)


def _gather_start(name, bufs, after=None):
    def make_copy(src, land, me, pk, c):
        hr = src.shape[1] // 2
        return src.at[me, pl.ds(c * hr, hr)], land.at[me, pl.ds(c * hr, hr)]

    return _chip_copies_start(name, bufs, None, make_copy, after)


def _gather_wait(name, send_sems, recv_sems, bufs, after):
    def three_halves(buf):
        return [b.at[pl.ds(0, 3), pl.ds(0, b.shape[1] // 2)] for b in buf]

    return _chip_copies_wait(name, send_sems, recv_sems, bufs, three_halves, three_halves, after)


def _gather_forward(name, bufs):
    n_t = len(bufs)
    n = 3 * n_t * D2D_CHUNKS

    def body(*refs):
        mine, buf = refs[:n_t], refs[n_t:2 * n_t]
        send_sems, recv_sems = refs[2 * n_t:]
        x, y, c = _place()
        chips = _other_chips(x, y)

        def piece(ref, t, slot, core, q):
            hr = ref[t].shape[1] // 2
            cr = hr // D2D_CHUNKS
            return ref[t].at[slot, pl.ds(core * hr + q * cr, cr)]

        copies = []
        for q in range(D2D_CHUNKS):
            for t in range(n_t):
                for j, (_, _, pk) in enumerate(chips):
                    s = (t * 3 + j) * D2D_CHUNKS + q
                    cp = pltpu.make_async_remote_copy(
                        src_ref=piece(mine, t, pk, c, q), dst_ref=piece(buf, t, pk, c, q), send_sem=send_sems.at[s],
                        recv_sem=recv_sems.at[s], device_id=(x, y, 1 - c), device_id_type=MESH)
                    cp.start()
                    copies.append(cp)
        for q in range(D2D_CHUNKS):
            for t in range(n_t):
                for j, (_, _, pk) in enumerate(chips):
                    s = (t * 3 + j) * D2D_CHUNKS + q
                    theirs = piece(buf, t, pk, 1 - c, q)
                    pltpu.make_async_remote_copy(
                        src_ref=theirs, dst_ref=theirs, send_sem=send_sems.at[s], recv_sem=recv_sems.at[s],
                        device_id=(x, y, 1 - c), device_id_type=MESH).wait_recv()
        for cp in copies:
            cp.wait_send()

    return pl.pallas_call(
        body, name=name,
        in_specs=[ANY] * n_t, out_specs=[ANY] * n_t,
        out_shape=[jax.ShapeDtypeStruct(b.shape, b.dtype) for b in bufs],
        input_output_aliases={t: t for t in range(n_t)},
        scratch_shapes=[pltpu.SemaphoreType.DMA((n,)), pltpu.SemaphoreType.DMA((n,))],
        compiler_params=pltpu.CompilerParams(has_side_effects=True),
    )(*bufs)


def _core_exchange(name, grads):
    n_t = len(grads)
    n = n_t * D2D_CHUNKS

    def body(*refs):
        src, theirs = refs[:n_t], refs[n_t:2 * n_t]
        send_sems, recv_sems = refs[2 * n_t:]
        x, y, c = _place()
        copies = []
        for q in range(D2D_CHUNKS):
            for t in range(n_t):
                hr = src[t].shape[1] // 2
                cr = hr // D2D_CHUNKS
                s = t * D2D_CHUNKS + q
                cp = pltpu.make_async_remote_copy(
                    src_ref=src[t].at[:, pl.ds((1 - c) * hr + q * cr, cr)],
                    dst_ref=theirs[t].at[:, pl.ds(q * cr, cr)],
                    send_sem=send_sems.at[s], recv_sem=recv_sems.at[s], device_id=(x, y, 1 - c), device_id_type=MESH)
                cp.start()
                copies.append(cp)
        for cp in copies:
            cp.wait()

    half = [jax.ShapeDtypeStruct((g.shape[0], g.shape[1] // 2, g.shape[2]), g.dtype) for g in grads]
    return pl.pallas_call(
        body, name=name,
        in_specs=[ANY] * n_t, out_specs=[ANY] * n_t, out_shape=half,
        scratch_shapes=[pltpu.SemaphoreType.DMA((n,)), pltpu.SemaphoreType.DMA((n,))],
        compiler_params=pltpu.CompilerParams(has_side_effects=True),
    )(*grads)


def _add_to_bf16(name, full, theirs, place):
    chips, rows, cols = theirs.shape
    tr = min(256, rows)
    per = rows // tr

    def body(p_ref, a_ref, b_ref, o_ref):
        o_ref[...] = (a_ref[...] + b_ref[...]).astype(BF16)

    blk = pl.BlockSpec((None, tr, cols), lambda k, i, p: (k, i, 0))
    return pl.pallas_call(
        body, name=name,
        grid_spec=pltpu.PrefetchScalarGridSpec(
            num_scalar_prefetch=1, grid=(chips, per),
            in_specs=[pl.BlockSpec((None, tr, cols), lambda k, i, p: (k, p[1] * per + i, 0)), blk],
            out_specs=blk),
        out_shape=jax.ShapeDtypeStruct(theirs.shape, BF16), compiler_params=_params(("parallel",) * 2),
    )(place, full, theirs)


def _chip_exchange_start(name, parts):
    lands = [lax.empty(p.shape, p.dtype) for p in parts]
    return _chip_copies_start(name, parts, lands, lambda src, land, me, pk, c: (src.at[pk], land.at[me]))


def _chip_exchange_wait(name, send_sems, recv_sems, bufs, after):
    n_t = len(bufs) // 2
    return _chip_copies_wait(name, send_sems, recv_sems, bufs,
                             lambda buf: [b.at[pl.ds(0, 3)] for b in buf[:n_t]],
                             lambda buf: [b.at[pl.ds(0, 3)] for b in buf[n_t:]], after)


def _sum_chips(name, parts, landed, place, l, stacked):
    chips, rows, cols = landed.shape
    tr = min(256, rows)
    per = rows // tr

    def body(p_ref, own_ref, *refs):
        land, o_ref = refs[:chips], refs[-1]
        tot = None
        for k in range(chips):
            term = jnp.where(p_ref[0] == k, own_ref[...], land[k][...]).astype(F32)
            tot = term if tot is None else tot + term
        o_ref[...] = tot

    def from_chip(k):
        return pl.BlockSpec((None, tr, cols), lambda i, p: (jnp.where(p[0] == k, (k + 1) % chips, k), i, 0))

    in_specs = [pl.BlockSpec((None, tr, cols), lambda i, p: (p[0], i, 0))] + [from_chip(k) for k in range(chips)]
    args = [parts] + [landed] * chips
    aliases = {}
    if stacked is not None:
        in_specs.append(ANY)
        args.append(stacked)
        aliases = {len(args): 0}
    return pl.pallas_call(
        body, name=name,
        grid_spec=pltpu.PrefetchScalarGridSpec(
            num_scalar_prefetch=1, grid=(per,), in_specs=in_specs,
            out_specs=pl.BlockSpec((None, tr, cols), lambda i, p: (l, p[1] * per + i, 0))),
        out_shape=jax.ShapeDtypeStruct((DEPTH, 2 * rows, cols), F32), input_output_aliases=aliases,
        compiler_params=_params(("parallel",)),
    )(place, *args)


def _core_share(bufs):
    n_t = len(bufs)
    n = n_t * D2D_CHUNKS

    def body(*refs):
        mine, buf = refs[:n_t], refs[n_t:2 * n_t]
        send_sems, recv_sems = refs[2 * n_t:]
        x, y, c = _place()

        def piece(ref, t, core, q):
            hr = ref[t].shape[1] // 2
            cr = hr // D2D_CHUNKS
            return ref[t].at[:, pl.ds(core * hr + q * cr, cr)]

        copies = []
        for q in range(D2D_CHUNKS):
            for t in range(n_t):
                s = t * D2D_CHUNKS + q
                cp = pltpu.make_async_remote_copy(
                    src_ref=piece(mine, t, c, q), dst_ref=piece(buf, t, c, q), send_sem=send_sems.at[s],
                    recv_sem=recv_sems.at[s], device_id=(x, y, 1 - c), device_id_type=MESH)
                cp.start()
                copies.append(cp)
        for q in range(D2D_CHUNKS):
            for t in range(n_t):
                s = t * D2D_CHUNKS + q
                theirs = piece(buf, t, 1 - c, q)
                pltpu.make_async_remote_copy(
                    src_ref=theirs, dst_ref=theirs, send_sem=send_sems.at[s], recv_sem=recv_sems.at[s],
                    device_id=(x, y, 1 - c), device_id_type=MESH).wait_recv()
        for cp in copies:
            cp.wait_send()

    return pl.pallas_call(
        body, name="grad_core_share",
        in_specs=[ANY] * n_t, out_specs=[ANY] * n_t,
        out_shape=[jax.ShapeDtypeStruct(b.shape, b.dtype) for b in bufs],
        input_output_aliases={t: t for t in range(n_t)},
        scratch_shapes=[pltpu.SemaphoreType.DMA((n,)), pltpu.SemaphoreType.DMA((n,))],
        compiler_params=pltpu.CompilerParams(has_side_effects=True),
    )(*bufs)


def _all_reduce_small(vec):
    rows, lanes = vec.shape
    hr = rows // 2

    def body(v_ref, o_ref, sib_ref, chips_ref, send_sems, recv_sems):
        x, y, c = _place()
        me = 2 * x + y
        sibling = (x, y, 1 - c)
        mine = pl.ds(pl.multiple_of(c * hr, 8), hr)
        theirs = pl.ds(pl.multiple_of((1 - c) * hr, 8), hr)
        swap = pltpu.make_async_remote_copy(
            src_ref=v_ref.at[theirs], dst_ref=sib_ref, send_sem=send_sems.at[0], recv_sem=recv_sems.at[0],
            device_id=sibling, device_id_type=MESH)
        swap.start()
        swap.wait_recv()
        chips_ref[me] = v_ref[mine] + sib_ref[...]
        copies = []
        for j, (px, py, pk) in enumerate(_other_chips(x, y)):
            cp = pltpu.make_async_remote_copy(
                src_ref=chips_ref.at[me], dst_ref=chips_ref.at[me], send_sem=send_sems.at[1 + j],
                recv_sem=recv_sems.at[1 + j], device_id=(px, py, c), device_id_type=MESH)
            cp.start()
            copies.append(cp)
        for j, (px, py, pk) in enumerate(_other_chips(x, y)):
            pltpu.make_async_remote_copy(
                src_ref=chips_ref.at[pk], dst_ref=chips_ref.at[pk], send_sem=send_sems.at[1 + j],
                recv_sem=recv_sems.at[1 + j], device_id=(px, py, c), device_id_type=MESH).wait_recv()
        tot = chips_ref[0]
        for k in range(1, N_CHIPS):
            tot = tot + chips_ref[k]
        o_ref[mine] = tot
        share = pltpu.make_async_remote_copy(
            src_ref=o_ref.at[mine], dst_ref=o_ref.at[mine], send_sem=send_sems.at[4], recv_sem=recv_sems.at[4],
            device_id=sibling, device_id_type=MESH)
        share.start()
        pltpu.make_async_remote_copy(
            src_ref=o_ref.at[theirs], dst_ref=o_ref.at[theirs], send_sem=send_sems.at[4], recv_sem=recv_sems.at[4],
            device_id=sibling, device_id_type=MESH).wait_recv()
        swap.wait_send()
        for cp in copies:
            cp.wait_send()
        share.wait_send()

    vm = pl.BlockSpec(memory_space=pltpu.VMEM)
    return pl.pallas_call(
        body, name="small_all_reduce", in_specs=[vm], out_specs=vm,
        out_shape=jax.ShapeDtypeStruct((rows, lanes), F32),
        scratch_shapes=[pltpu.VMEM((hr, lanes), F32), pltpu.VMEM((N_CHIPS, hr, lanes), F32),
                        pltpu.SemaphoreType.DMA((5,)), pltpu.SemaphoreType.DMA((5,))],
        compiler_params=pltpu.CompilerParams(has_side_effects=True, vmem_limit_bytes=48 * MIB),
    )(vec)


def _adamw(name, w, g, m, v):
    rows, cols = w.shape
    tr = rows
    for cand in (256, 128, 64, 32, 16, 8):
        if rows % cand == 0:
            tr = cand
            break
    c1 = 1.0 - ADAM_B1 ** ADAM_STEP
    c2 = 1.0 - ADAM_B2 ** ADAM_STEP

    def body(w_ref, g_ref, m_ref, v_ref, d_ref, nm_ref, nv_ref):
        gv = g_ref[...]
        nm = ADAM_B1 * m_ref[...] + (1.0 - ADAM_B1) * gv
        nv = ADAM_B2 * v_ref[...] + (1.0 - ADAM_B2) * (gv * gv)
        nm_ref[...] = nm
        nv_ref[...] = nv
        d_ref[...] = -ADAM_LR * ((nm / c1) / (jnp.sqrt(nv / c2) + ADAM_EPS) + ADAM_WD * w_ref[...])

    blk = pl.BlockSpec((tr, cols), lambda i: (i, 0))
    out = jax.ShapeDtypeStruct((rows, cols), F32)
    return pl.pallas_call(
        body, name=name, grid=(rows // tr,), in_specs=[blk] * 4, out_specs=[blk] * 3, out_shape=[out] * 3,
        compiler_params=_params(("parallel",)),
    )(w, g, m, v)


def _pack_small(parts):
    flat = jnp.concatenate([parts[k].reshape(-1) for k in SMALL_NAMES])
    n = flat.shape[0]
    rows = -(-n // (256 * 128)) * 256
    return jnp.pad(flat, (0, rows * 128 - n)).reshape(rows, 128)


def _unpack_small(packed, like):
    flat = packed.reshape(-1)
    out, off = {}, 0
    for k in SMALL_NAMES:
        n = like[k].size
        out[k] = flat[off:off + n].reshape(like[k].shape)
        off += n
    return out


WEIGHT_ORDER = ("norm_g", "w_in", "sgu_ln_g", "sgu_ln_b", "sgu_w", "sgu_b", "mem_norm_g", "w_mem_kv", "q_norm_g",
                "k_norm_g", "w_out")


def kernel(x, mem, norm_g, w_in, sgu_ln_g, sgu_ln_b, sgu_w, sgu_b, mem_norm_g, w_mem_kv, q_norm_g, k_norm_g, w_out, loss_target, m_norm_g, m_w_in, m_sgu_ln_g, m_sgu_ln_b, m_sgu_w, m_sgu_b, m_mem_norm_g, m_w_mem_kv, m_q_norm_g, m_k_norm_g, m_w_out, v_norm_g, v_w_in, v_sgu_ln_g, v_sgu_ln_b, v_sgu_w, v_sgu_b, v_mem_norm_g, v_w_mem_kv, v_q_norm_g, v_k_norm_g, v_w_out):
    weights = dict(norm_g=norm_g, w_in=w_in, sgu_ln_g=sgu_ln_g, sgu_ln_b=sgu_ln_b, sgu_w=sgu_w, sgu_b=sgu_b,
                   mem_norm_g=mem_norm_g, w_mem_kv=w_mem_kv, q_norm_g=q_norm_g, k_norm_g=k_norm_g, w_out=w_out)
    mom_m = dict(norm_g=m_norm_g, w_in=m_w_in, sgu_ln_g=m_sgu_ln_g, sgu_ln_b=m_sgu_ln_b, sgu_w=m_sgu_w, sgu_b=m_sgu_b,
                 mem_norm_g=m_mem_norm_g, w_mem_kv=m_w_mem_kv, q_norm_g=m_q_norm_g, k_norm_g=m_k_norm_g, w_out=m_w_out)
    mom_v = dict(norm_g=v_norm_g, w_in=v_w_in, sgu_ln_g=v_sgu_ln_g, sgu_ln_b=v_sgu_ln_b, sgu_w=v_sgu_w, sgu_b=v_sgu_b,
                 mem_norm_g=v_mem_norm_g, w_mem_kv=v_w_mem_kv, q_norm_g=v_q_norm_g, k_norm_g=v_k_norm_g, w_out=v_w_out)
    big = ("w_in", "w_mem_kv", "w_out")
    sm = {k: weights[k] for k in SMALL_NAMES}

    place = _place_index()
    xs, mems, target = x[0], mem[0], loss_target[0]

    slots = [[_cast_into_slot(f"cast_{k}_{l}", weights[k], l, place) for k in big] for l in range(DEPTH)]
    saved = [None] * DEPTH

    def gathered(tag, flight, after):
        send_sems, recv_sems, bufs, _ = flight
        return _gather_forward(f"gather_forward_{tag}", _gather_wait(f"gather_wait_{tag}", send_sems, recv_sems,
                                                                      bufs, after))

    flights = {}

    def start_gather(l, after=None):
        flights[l, "in"] = _gather_start(f"gather_start_{l}_in", slots[l][:1], after)
        flights[l, "rest"] = _gather_start(f"gather_start_{l}_rest", slots[l][1:], flights[l, "in"][3])
        return flights[l, "rest"][3]

    start_gather(0)
    cur = xs
    for l in range(DEPTH):
        (w_in_all,) = gathered(f"{l}_in", flights[l, "in"], flights[l, "rest"][3] if l == 0 else cur)

        def rest(proj, l=l):
            w_kv_all, w_out_all = gathered(f"{l}_rest", flights[l, "rest"], proj)
            token = start_gather(l + 1, w_out_all) if l + 1 < DEPTH else None
            return w_kv_all, w_out_all, token

        cur, saved[l] = _layer_fwd(l, cur, mems, sm, w_in_all, rest)
    dxo, dxo_b, loss_part = _loss_and_grad("loss", cur, target, min(256, xs.shape[0]))
    loss = lax.psum(loss_part[0, 0], ("x", "y", "c"))

    small_g = [None] * DEPTH
    flight = {}
    for l in reversed(range(DEPTH)):
        def start_exchange(group, full, l=l):
            theirs = _core_exchange(f"grad_core_exchange_{l}_{group}", full)
            parts = [_add_to_bf16(f"grad_core_sum_{l}_{group}_{t}", g, th, place)
                     for t, (g, th) in enumerate(zip(full, theirs))]
            *flight[l, group], token = _chip_exchange_start(f"grad_chip_start_{l}_{group}", parts)
            return token

        dxo, dxo_b, small_g[l], *_ = _layer_bwd(l, dxo, dxo_b, mems, sm, saved[l], on_weight_grads=start_exchange)
    grad_x = dxo
    halves = dict.fromkeys(big)
    for l in reversed(range(DEPTH)):
        for group, names in (("out", ("w_out",)), ("in", ("w_in", "w_mem_kv"))):
            send_sems, recv_sems, bufs = flight[l, group]
            bufs = _chip_exchange_wait(f"grad_chip_wait_{l}_{group}", send_sems, recv_sems, bufs, grad_x)
            for t, k in enumerate(names):
                halves[k] = _sum_chips(f"grad_chip_sum_{l}_{k}", bufs[t], bufs[len(names) + t], place, l, halves[k])
    big_g = dict(zip(big, _core_share([halves[k] for k in big])))

    small_g = {k: jnp.stack([small_g[l][k] for l in range(DEPTH)]) for k in SMALL_NAMES}
    small_sum = _unpack_small(_all_reduce_small(_pack_small(small_g)), sm)

    grads, delta, new_m, new_v = {}, {}, {}, {}
    for k in big:
        shape = weights[k].shape
        two_d = (shape[0] * shape[1], shape[2])
        grads[k] = big_g[k]
        d, nm, nv = _adamw(f"adamw_{k}", weights[k].reshape(two_d), big_g[k].reshape(two_d),
                           mom_m[k].reshape(two_d), mom_v[k].reshape(two_d))
        delta[k], new_m[k], new_v[k] = d.reshape(shape), nm.reshape(shape), nv.reshape(shape)
    d, nm, nv = _adamw("adamw_small", _pack_small(sm), _pack_small(small_sum),
                       _pack_small({k: mom_m[k] for k in SMALL_NAMES}), _pack_small({k: mom_v[k] for k in SMALL_NAMES}))
    grads.update(small_sum)
    delta.update(_unpack_small(d, sm))
    new_m.update(_unpack_small(nm, sm))
    new_v.update(_unpack_small(nv, sm))
    return (loss, grad_x[None], *[grads[k] for k in WEIGHT_ORDER], *[delta[k] for k in WEIGHT_ORDER],
            *[new_m[k] for k in WEIGHT_ORDER], *[new_v[k] for k in WEIGHT_ORDER])
```

```python
import functools
import math

import jax
import jax.numpy as jnp
from jax import lax
from jax.experimental import pallas as pl
from jax.experimental.pallas import tpu as pltpu

F32 = jnp.float32
BF16 = jnp.bfloat16
MESH = pl.DeviceIdType.MESH

D_MODEL = 2048
DEPTH = 2
CHUNK = 128
D_A = 1024
A_GROUPS = 8
D_B = 512
D_C = 512
HEADS = 4
HEAD_DIM = 128
IN_WIDTH = 6144
N_CHIPS = 4
EPS = 1e-6
ATT_SCALE = 1.0 / math.sqrt(HEAD_DIM)

OFF_U, OFF_V, OFF_ZA = 0, 1024, 2048
OFF_QB, OFF_KB, OFF_VB, OFF_ZB = 3072, 3584, 4096, 4608
OFF_QC, OFF_ZC = 5120, 5632
OFF_YB, OFF_YC = 1024, 1536

ADAM_LR = 0.001
ADAM_B1 = 0.9
ADAM_B2 = 0.999
ADAM_EPS = 1e-08
ADAM_WD = 0.01
ADAM_STEP = 10

MIB = 1024 * 1024
ANY = pl.BlockSpec(memory_space=pl.ANY)


def _params(semantics=None, vmem_mb=48):
    return pltpu.CompilerParams(dimension_semantics=semantics, vmem_limit_bytes=vmem_mb * MIB)


def _gelu(x):
    return 0.5 * x * (1.0 + lax.erf(x * (1.0 / math.sqrt(2.0))))


def _gelu_grad(x):
    cdf = 0.5 * (1.0 + lax.erf(x * (1.0 / math.sqrt(2.0))))
    pdf = jnp.exp(-0.5 * x * x) * (1.0 / math.sqrt(2.0 * math.pi))
    return cdf + x * pdf


def _sigmoid(x):
    return 1.0 / (1.0 + jnp.exp(-x))


def _silu_and_grad(z):
    s = _sigmoid(z)
    return z * s, s * (1.0 + z * (1.0 - s))


def _split_bf16(x):
    hi = x.astype(BF16)
    lo = (x - hi.astype(F32)).astype(BF16)
    return hi, lo


def _dot(a, b, dims):
    return lax.dot_general(a, b, (dims, ((), ())), preferred_element_type=F32)


NN = ((1,), (0,))
NT = ((1,), (1,))
TN = ((0,), (0,))


def _matmul(name, a, b, *, grid, a_spec, b_spec, o_spec, out_shape, dims, res=None, res_spec=None, after=None,
            vmem_mb=48):
    nk = grid[2]
    n_in = 2 + (res is not None) + (after is not None)

    def body(*refs):
        a_ref, b_ref = refs[0], refs[1]
        r_ref = refs[2] if res is not None else None
        o_ref = refs[n_in]
        bv = b_ref[...]
        if bv.ndim == 3:
            bv = bv.reshape(-1, bv.shape[-1])
        part = _dot(a_ref[...], bv, dims)
        if nk == 1:
            if r_ref is not None:
                part = part + r_ref[...]
            o_ref[...] = part.astype(o_ref.dtype)
            return
        acc_ref = refs[n_in + 1]
        k = pl.program_id(2)

        @pl.when(k == 0)
        def _():
            acc_ref[...] = part

        @pl.when(k > 0)
        def _():
            acc_ref[...] += part

        @pl.when(k == nk - 1)
        def _():
            tot = acc_ref[...]
            if r_ref is not None:
                tot = tot + r_ref[...]
            o_ref[...] = tot.astype(o_ref.dtype)

    in_specs = [a_spec, b_spec]
    args = [a, b]
    if res is not None:
        in_specs.append(res_spec)
        args.append(res)
    if after is not None:
        in_specs.append(ANY)
        args.append(after)
    acc_shape = tuple(d for d in o_spec.block_shape if d is not None)
    scratch = [pltpu.VMEM(acc_shape, F32)] if nk > 1 else []
    return pl.pallas_call(
        body, name=name, grid=grid, in_specs=in_specs, out_specs=o_spec, out_shape=out_shape,
        scratch_shapes=scratch,
        compiler_params=_params(("parallel", "parallel", "arbitrary"), vmem_mb),
    )(*args)


def _rms_fwd(name, x, g, tr, after=None, transposed=False):
    rows, d = x.shape

    def body(x_ref, g_ref, *refs):
        outs = refs[1:] if after is not None else refs
        xv = x_ref[...]
        r = lax.rsqrt(jnp.mean(xv * xv, axis=-1, keepdims=True) + EPS)
        h = xv * r * g_ref[...]
        outs[0][...] = h.astype(BF16)
        if transposed:
            outs[1][...] = h.T.astype(BF16)

    out_specs = [pl.BlockSpec((tr, d), lambda i: (i, 0))]
    out_shape = [jax.ShapeDtypeStruct((rows, d), BF16)]
    if transposed:
        out_specs.append(pl.BlockSpec((d, tr), lambda i: (0, i)))
        out_shape.append(jax.ShapeDtypeStruct((d, rows), BF16))
    outs = pl.pallas_call(
        body, name=name, grid=(rows // tr,),
        in_specs=[pl.BlockSpec((tr, d), lambda i: (i, 0)), pl.BlockSpec((1, d), lambda i: (0, 0))]
        + ([] if after is None else [ANY]),
        out_specs=out_specs, out_shape=out_shape,
        compiler_params=_params(("parallel",)),
    )(x, g, *([] if after is None else [after]))
    return outs if transposed else outs[0]


def _rms_bwd(name, x, dh, dres, g, tr, after=None):
    rows, d = x.shape

    def body(x_ref, dh_ref, dres_ref, g_ref, *refs):
        dx_ref, dxb_ref, dg_ref = refs[-3:]
        xv = x_ref[...]
        r = lax.rsqrt(jnp.mean(xv * xv, axis=-1, keepdims=True) + EPS)
        xhat = xv * r
        dhv = dh_ref[...]
        dxh = dhv * g_ref[...]
        dx = r * (dxh - xhat * jnp.mean(dxh * xhat, axis=-1, keepdims=True)) + dres_ref[...]
        dx_ref[...] = dx
        dxb_ref[...] = dx.astype(BF16)
        part = jnp.sum(dhv * xhat, axis=0, keepdims=True)

        @pl.when(pl.program_id(0) == 0)
        def _():
            dg_ref[...] = part

        @pl.when(pl.program_id(0) > 0)
        def _():
            dg_ref[...] += part

    blk = pl.BlockSpec((tr, d), lambda i: (i, 0))
    vec = pl.BlockSpec((1, d), lambda i: (0, 0))
    return pl.pallas_call(
        body, name=name, grid=(rows // tr,), in_specs=[blk, blk, blk, vec] + ([] if after is None else [ANY]),
        out_specs=[blk, blk, vec],
        out_shape=[jax.ShapeDtypeStruct((rows, d), F32), jax.ShapeDtypeStruct((rows, d), BF16),
                   jax.ShapeDtypeStruct((1, d), F32)],
        compiler_params=_params(("arbitrary",)),
    )(x, dh, dres, g, *([] if after is None else [after]))


def _rms_gain_grad(name, x, dh):
    rows, d = x.shape

    def body(x_ref, dh_ref, dg_ref):
        xv = x_ref[...]
        r = lax.rsqrt(jnp.mean(xv * xv, axis=-1, keepdims=True) + EPS)
        dg_ref[...] = jnp.sum(dh_ref[...] * xv * r, axis=0, keepdims=True)

    return pl.pallas_call(
        body, name=name, out_shape=jax.ShapeDtypeStruct((1, d), F32), compiler_params=_params(None),
    )(x, dh)


def _loss_and_grad(name, y, target, tr):
    rows, d = y.shape
    n = rows // tr

    def body(y_ref, t_ref, dx_ref, dxb_ref, loss_ref, acc_ref):
        e = y_ref[...] - t_ref[...]
        dx = e * (1.0 / d)
        dx_ref[...] = dx
        dxb_ref[...] = dx.astype(BF16)
        part = jnp.sum(e * e, axis=0, keepdims=True)
        i = pl.program_id(0)

        @pl.when(i == 0)
        def _():
            acc_ref[...] = part

        @pl.when(i > 0)
        def _():
            acc_ref[...] += part

        @pl.when(i == n - 1)
        def _():
            loss_ref[...] = jnp.sum(acc_ref[...], axis=-1, keepdims=True) * (0.5 / d)

    blk = pl.BlockSpec((tr, d), lambda i: (i, 0))
    return pl.pallas_call(
        body, name=name, grid=(n,), in_specs=[blk, blk],
        out_specs=[blk, blk, pl.BlockSpec((1, 1), lambda i: (0, 0))],
        out_shape=[jax.ShapeDtypeStruct((rows, d), F32), jax.ShapeDtypeStruct((rows, d), BF16),
                   jax.ShapeDtypeStruct((1, 1), F32)],
        scratch_shapes=[pltpu.VMEM((1, d), F32)],
        compiler_params=_params(("arbitrary",)),
    )(y, target)


SB_T = 256
SB_HEADS = 2


def _sb_scores(q, kblk):
    z = _dot(q, kblk, NT) * ATT_SCALE
    e = jnp.exp(-jnp.abs(z))
    sp = jnp.log1p(e)
    lb = jnp.minimum(z, 0.0) - sp
    l1 = lb - z
    return z, e, lb, l1


def _sb_fwd(name, proj):
    s_len = proj.shape[0]
    t = SB_T
    nq = s_len // t

    def body(q_ref, k_ref, v_ref, o_ref):
        i = pl.program_id(1)
        row = lax.broadcasted_iota(jnp.int32, (t, t), 0)
        col = lax.broadcasted_iota(jnp.int32, (t, t), 1)
        causal = col < row
        after_mat = (row > col).astype(BF16)
        heads = [slice(hh * HEAD_DIM, (hh + 1) * HEAD_DIM) for hh in range(SB_HEADS)]
        q = [q_ref[:, sl].astype(BF16) for sl in heads]

        def tile(kb, state, masked):
            start = pl.multiple_of(kb * t, t)
            out = []
            for hh, sl in enumerate(heads):
                carry, acc = state[hh]
                kblk = k_ref[pl.ds(start, t), sl].astype(BF16)
                vblk = v_ref[pl.ds(start, t), sl].astype(BF16)
                _, _, lb, l1 = _sb_scores(q[hh], kblk)
                if masked:
                    l1 = jnp.where(causal, l1, 0.0)
                hi, lo = _split_bf16(l1)
                after = _dot(hi, after_mat, NN) + _dot(lo, after_mat, NN) + carry
                a = jnp.exp(lb + after)
                if masked:
                    a = jnp.where(causal, a, 0.0)
                acc = acc + _dot(a.astype(BF16), vblk, NN)
                carry = carry + jnp.sum(l1, axis=-1, keepdims=True)
                out.append((carry, acc))
            return tuple(out)

        zero = (jnp.zeros((t, 1), F32), jnp.zeros((t, HEAD_DIM), F32))
        state = tile(i, (zero,) * SB_HEADS, True)
        state = lax.fori_loop(0, i, lambda n, st: tile(i - 1 - n, st, False), state)
        for hh, sl in enumerate(heads):
            o_ref[:, sl] = state[hh][1]

    cb = SB_HEADS * HEAD_DIM
    return pl.pallas_call(
        body, name=name, grid=(HEADS // SB_HEADS, nq),
        in_specs=[pl.BlockSpec((t, cb), lambda h, i: (i, OFF_QB // cb + h)),
                  pl.BlockSpec((s_len, cb), lambda h, i: (0, OFF_KB // cb + h)),
                  pl.BlockSpec((s_len, cb), lambda h, i: (0, OFF_VB // cb + h))],
        out_specs=pl.BlockSpec((t, cb), lambda h, i: (i, h)),
        out_shape=jax.ShapeDtypeStruct((s_len, D_B), F32),
        compiler_params=_params(("parallel", "arbitrary")),
    )(proj, proj, proj)


def _sb_bwd(name, proj, dy, after=None):
    s_len = proj.shape[0]
    t = SB_T
    nq = s_len // t

    def body(q_ref, k_ref, v_ref, z_ref, dy_ref, *refs):
        dq_ref, dk_ref, dv_ref, a_ref, s_ref = refs[-5:]
        i = pl.program_id(1)

        @pl.when(i == 0)
        def _():
            dk_ref[...] = jnp.zeros_like(dk_ref)
            dv_ref[...] = jnp.zeros_like(dv_ref)

        heads = [slice(hh * HEAD_DIM, (hh + 1) * HEAD_DIM) for hh in range(SB_HEADS)]
        q = [q_ref[:, sl].astype(BF16) for sl in heads]
        silu_z, _ = _silu_and_grad(z_ref[...])
        do_all = dy_ref[...] * silu_z
        do_b = [do_all[:, sl].astype(BF16) for sl in heads]
        row = lax.broadcasted_iota(jnp.int32, (t, t), 0)
        col = lax.broadcasted_iota(jnp.int32, (t, t), 1)
        causal = col < row
        after_mat = (row > col).astype(BF16)
        before_mat = (row < col).astype(BF16)

        def weights(kb, carries, masked):
            start = pl.multiple_of(kb * t, t)
            out = []
            for hh, sl in enumerate(heads):
                kblk = k_ref[pl.ds(start, t), sl].astype(BF16)
                z, _, lb, l1 = _sb_scores(q[hh], kblk)
                if masked:
                    l1 = jnp.where(causal, l1, 0.0)
                hi, lo = _split_bf16(l1)
                after = _dot(hi, after_mat, NN) + _dot(lo, after_mat, NN) + carries[hh]
                a = jnp.exp(lb + after)
                if masked:
                    a = jnp.where(causal, a, 0.0)
                a_ref[hh, kb] = a
                s_ref[hh, kb] = z
                out.append(carries[hh] + jnp.sum(l1, axis=-1, keepdims=True))
            return tuple(out)

        carries = weights(i, (jnp.zeros((t, 1), F32),) * SB_HEADS, True)
        lax.fori_loop(0, i, lambda n, c: weights(i - 1 - n, c, False), carries)

        def grads(kb, state, masked):
            start = pl.multiple_of(kb * t, t)
            out = []
            for hh, sl in enumerate(heads):
                carry, dq = state[hh]
                kblk = k_ref[pl.ds(start, t), sl].astype(BF16)
                vblk = v_ref[pl.ds(start, t), sl].astype(BF16)
                a = a_ref[hh, kb]
                z = s_ref[hh, kb]
                g = _dot(do_b[hh], vblk, NT) * a
                ghi, glo = _split_bf16(g)
                prefix = _dot(ghi, before_mat, NN) + _dot(glo, before_mat, NN) + carry
                e = jnp.exp(-jnp.abs(z))
                inv = 1.0 / (1.0 + e)
                pos = z >= 0.0
                beta = jnp.where(pos, inv, e * inv)
                one_m_beta = jnp.where(pos, e * inv, inv)
                dz = (g * one_m_beta - prefix * beta) * ATT_SCALE
                if masked:
                    dz = jnp.where(causal, dz, 0.0)
                dz_b = dz.astype(BF16)
                dq = dq + _dot(dz_b, kblk, NN)
                dk_ref[pl.ds(start, t), sl] += _dot(dz_b, q[hh], TN)
                dv_ref[pl.ds(start, t), sl] += _dot(a.astype(BF16), do_b[hh], TN)
                out.append((carry + jnp.sum(g, axis=-1, keepdims=True), dq))
            return tuple(out)

        zero = (jnp.zeros((t, 1), F32), jnp.zeros((t, HEAD_DIM), F32))
        state = lax.fori_loop(0, i, lambda kb, st: grads(kb, st, False), (zero,) * SB_HEADS)
        state = grads(i, state, True)
        for hh, sl in enumerate(heads):
            dq_ref[:, sl] = state[hh][1]

    cb = SB_HEADS * HEAD_DIM
    qblk = lambda off: pl.BlockSpec((t, cb), lambda h, i: (i, off // cb + h))
    full = lambda off: pl.BlockSpec((s_len, cb), lambda h, i: (0, off // cb + h))
    out = jax.ShapeDtypeStruct((s_len, D_B), F32)
    return pl.pallas_call(
        body, name=name, grid=(HEADS // SB_HEADS, nq),
        in_specs=[qblk(OFF_QB), full(OFF_KB), full(OFF_VB), qblk(OFF_ZB), qblk(OFF_YB)]
        + ([] if after is None else [ANY]),
        out_specs=[qblk(0), full(0), full(0)],
        out_shape=[out, out, out],
        scratch_shapes=[pltpu.VMEM((SB_HEADS, nq, t, t), F32), pltpu.VMEM((SB_HEADS, nq, t, t), F32)],
        compiler_params=_params(("parallel", "arbitrary")),
    )(proj, proj, proj, proj, dy, *([] if after is None else [after]))


MEM_TQ = 512


def _qk_norm(x, g):
    r = lax.rsqrt(jnp.mean(x * x, axis=-1, keepdims=True) + EPS)
    xhat = x * r
    return xhat * g, xhat, r


def _qk_norm_bwd(dn, g, xhat, r):
    dxh = dn * g
    return r * (dxh - xhat * jnp.mean(dxh * xhat, axis=-1, keepdims=True))


def _mem_probs(q, mk, qg, kg):
    qn, qhat, rq = _qk_norm(q, qg)
    kn, khat, rk = _qk_norm(mk, kg)
    qn_b, kn_b = qn.astype(BF16), kn.astype(BF16)
    s = _dot(qn_b, kn_b, NT) * ATT_SCALE
    p = jnp.exp(s - jnp.max(s, axis=-1, keepdims=True))
    p = p / jnp.sum(p, axis=-1, keepdims=True)
    return p, qn_b, kn_b, qhat, rq, khat, rk


def _mem_fwd(name, proj, mem_kv, qg, kg):
    s_len = proj.shape[0]
    m_len = mem_kv.shape[0]
    tq = min(MEM_TQ, s_len)

    def body(q_ref, mk_ref, mv_ref, qg_ref, kg_ref, o_ref):
        p = _mem_probs(q_ref[...], mk_ref[...], qg_ref[...], kg_ref[...])[0]
        o_ref[...] = _dot(p.astype(BF16), mv_ref[...].astype(BF16), NN)

    cb = HEAD_DIM
    vec = pl.BlockSpec((1, cb), lambda h, i: (0, 0))
    return pl.pallas_call(
        body, name=name, grid=(HEADS, s_len // tq),
        in_specs=[pl.BlockSpec((tq, cb), lambda h, i: (i, OFF_QC // cb + h)),
                  pl.BlockSpec((m_len, cb), lambda h, i: (0, h)),
                  pl.BlockSpec((m_len, cb), lambda h, i: (0, HEADS + h)), vec, vec],
        out_specs=pl.BlockSpec((tq, cb), lambda h, i: (i, h)),
        out_shape=jax.ShapeDtypeStruct((s_len, D_C), F32),
        compiler_params=_params(("parallel", "parallel")),
    )(proj, mem_kv, mem_kv, qg, kg)


def _mem_bwd(name, proj, mem_kv, qg, kg, dy):
    s_len = proj.shape[0]
    m_len = mem_kv.shape[0]
    tq = min(MEM_TQ, s_len)

    def body(q_ref, mk_ref, mv_ref, qg_ref, kg_ref, z_ref, dy_ref, dq_ref, dmk_ref, dmv_ref, dqg_ref, dkg_ref):
        h, i = pl.program_id(0), pl.program_id(1)

        @pl.when(i == 0)
        def _():
            dmk_ref[...] = jnp.zeros_like(dmk_ref)
            dmv_ref[...] = jnp.zeros_like(dmv_ref)

        @pl.when((i == 0) & (h == 0))
        def _():
            dqg_ref[...] = jnp.zeros_like(dqg_ref)
            dkg_ref[...] = jnp.zeros_like(dkg_ref)

        qg, kg = qg_ref[...], kg_ref[...]
        p, qn_b, kn_b, qhat, rq, khat, rk = _mem_probs(q_ref[...], mk_ref[...], qg, kg)
        silu_z, _ = _silu_and_grad(z_ref[...])
        do_b = (dy_ref[...] * silu_z).astype(BF16)
        dmv_ref[...] += _dot(p.astype(BF16), do_b, TN)
        dp = _dot(do_b, mv_ref[...].astype(BF16), NT)
        ds = (p * (dp - jnp.sum(dp * p, axis=-1, keepdims=True)) * ATT_SCALE).astype(BF16)
        dqn = _dot(ds, kn_b, NN)
        dkn = _dot(ds, qn_b, TN)
        dq_ref[...] = _qk_norm_bwd(dqn, qg, qhat, rq)
        dmk_ref[...] += _qk_norm_bwd(dkn, kg, khat, rk)
        dqg_ref[...] += jnp.sum(dqn * qhat, axis=0, keepdims=True)
        dkg_ref[...] += jnp.sum(dkn * khat, axis=0, keepdims=True)

    cb = HEAD_DIM
    vec = pl.BlockSpec((1, cb), lambda h, i: (0, 0))
    qblk = lambda off: pl.BlockSpec((tq, cb), lambda h, i: (i, off // cb + h))
    memblk = lambda off: pl.BlockSpec((m_len, cb), lambda h, i: (0, off + h))
    return pl.pallas_call(
        body, name=name, grid=(HEADS, s_len // tq),
        in_specs=[qblk(OFF_QC), memblk(0), memblk(HEADS), vec, vec, qblk(OFF_ZC), qblk(OFF_YC)],
        out_specs=[qblk(0), memblk(0), memblk(0), vec, vec],
        out_shape=[jax.ShapeDtypeStruct((s_len, D_C), F32), jax.ShapeDtypeStruct((m_len, D_C), F32),
                   jax.ShapeDtypeStruct((m_len, D_C), F32), jax.ShapeDtypeStruct((1, cb), F32),
                   jax.ShapeDtypeStruct((1, cb), F32)],
        compiler_params=_params(("arbitrary", "arbitrary")),
    )(proj, mem_kv, mem_kv, qg, kg, proj, dy)


def _sgu_common(u_ref, v_ref, lng_ref, lnb_ref, w_ref, bias_ref):
    ug = _gelu(u_ref[...])
    vg = _gelu(v_ref[...])
    mu = jnp.mean(vg, axis=-1, keepdims=True)
    xc = vg - mu
    rstd = lax.rsqrt(jnp.mean(xc * xc, axis=-1, keepdims=True) + EPS)
    xhat = xc * rstd
    vn = xhat * lng_ref[...] + lnb_ref[...]
    vn_b = vn.astype(BF16)
    row = lax.broadcasted_iota(jnp.int32, (CHUNK, CHUNK), 0)
    col = lax.broadcasted_iota(jnp.int32, (CHUNK, CHUNK), 1)
    tril = row >= col
    mixed = []
    for g in range(A_GROUPS):
        w = jnp.where(tril, w_ref[g], 0.0).astype(BF16)
        sl = slice(g * CHUNK, (g + 1) * CHUNK)
        mixed.append(_dot(w, vn_b[:, sl], NN) + bias_ref[:, sl])
    return ug, xhat, rstd, vn_b, mixed, tril


def _gate_fwd(name, proj, o_b, o_c, lng, lnb, w_s, bias):
    s_len = proj.shape[0]

    def body(u_ref, v_ref, za_ref, zb_ref, zc_ref, ob_ref, oc_ref, lng_ref, lnb_ref, w_ref, bias_ref, y_ref, yt_ref):
        ug, _, _, _, mixed, _ = _sgu_common(u_ref, v_ref, lng_ref, lnb_ref, w_ref, bias_ref)
        sza, _ = _silu_and_grad(za_ref[...])
        gate = ug * sza

        def put(off, width, val):
            y_ref[:, off:off + width] = val.astype(BF16)
            yt_ref[off:off + width, :] = val.T.astype(BF16)

        for g in range(A_GROUPS):
            sl = slice(g * CHUNK, (g + 1) * CHUNK)
            put(g * CHUNK, CHUNK, gate[:, sl] * mixed[g])
        szb, _ = _silu_and_grad(zb_ref[...])
        put(OFF_YB, D_B, ob_ref[...] * szb)
        szc, _ = _silu_and_grad(zc_ref[...])
        put(OFF_YC, D_C, oc_ref[...] * szc)

    wide = lambda off: pl.BlockSpec((CHUNK, D_A), lambda i: (i, off // D_A))
    narrow = lambda off: pl.BlockSpec((CHUNK, D_B), lambda i: (i, off // D_B))
    vec = pl.BlockSpec((1, D_A), lambda i: (0, 0))
    return pl.pallas_call(
        body, name=name, grid=(s_len // CHUNK,),
        in_specs=[wide(OFF_U), wide(OFF_V), wide(OFF_ZA), narrow(OFF_ZB), narrow(OFF_ZC), narrow(0), narrow(0), vec, vec,
                  pl.BlockSpec((A_GROUPS, CHUNK, CHUNK), lambda i: (0, 0, 0)),
                  pl.BlockSpec((CHUNK, D_A), lambda i: (0, 0))],
        out_specs=[pl.BlockSpec((CHUNK, D_MODEL), lambda i: (i, 0)), pl.BlockSpec((D_MODEL, CHUNK), lambda i: (0, i))],
        out_shape=[jax.ShapeDtypeStruct((s_len, D_MODEL), BF16), jax.ShapeDtypeStruct((D_MODEL, s_len), BF16)],
        compiler_params=_params(("parallel",)),
    )(proj, proj, proj, proj, proj, o_b, o_c, lng, lnb, w_s, bias)


def _gate_bwd(name, proj, dy, o_b, o_c, dqkv, dq_c, lng, lnb, w_s, w_s_t, bias):
    s_len = proj.shape[0]
    n = s_len // CHUNK
    dq_b, dk_b, dv_b = dqkv

    def body(u_ref, v_ref, za_ref, zb_ref, zc_ref, dya_ref, dyb_ref, dyc_ref, ob_ref, oc_ref, dq_ref, dk_ref, dv_ref,
             dqc_ref, lng_ref, lnb_ref, w_ref, wt_ref, bias_ref, dp_ref, dw_ref, dsb_ref, dlng_ref, dlnb_ref, dbias_ref):
        i = pl.program_id(0)

        @pl.when(i == 0)
        def _():
            dw_ref[...] = jnp.zeros_like(dw_ref)
            dbias_ref[...] = jnp.zeros_like(dbias_ref)
            dlng_ref[...] = jnp.zeros_like(dlng_ref)
            dlnb_ref[...] = jnp.zeros_like(dlnb_ref)

        ug, xhat, rstd, vn_b, mixed, tril = _sgu_common(u_ref, v_ref, lng_ref, lnb_ref, w_ref, bias_ref)
        za = za_ref[...]
        sza, dsza = _silu_and_grad(za)
        dya = dya_ref[...]
        mixed_all = jnp.concatenate(mixed, axis=-1)
        d_mixed = dya * ug * sza
        dp_ref[:, OFF_U:OFF_U + D_A] = (dya * mixed_all * sza * _gelu_grad(u_ref[...])).astype(BF16)
        dp_ref[:, OFF_ZA:OFF_ZA + D_A] = (dya * ug * mixed_all * dsza).astype(BF16)
        dbias_ref[...] += d_mixed
        dm_b = d_mixed.astype(BF16)
        triu = lax.broadcasted_iota(jnp.int32, (CHUNK, CHUNK), 0) <= lax.broadcasted_iota(jnp.int32, (CHUNK, CHUNK), 1)
        d_vn = []
        for g in range(A_GROUPS):
            sl = slice(g * CHUNK, (g + 1) * CHUNK)
            wt = jnp.where(triu, wt_ref[g], 0.0).astype(BF16)
            d_vn.append(_dot(wt, dm_b[:, sl], NN))
            dw_ref[g] += jnp.where(tril, _dot(dm_b[:, sl], vn_b[:, sl], NT), 0.0)
        d_vn = jnp.concatenate(d_vn, axis=-1)
        dlng_ref[...] += jnp.sum(d_vn * xhat, axis=0, keepdims=True)
        dlnb_ref[...] += jnp.sum(d_vn, axis=0, keepdims=True)
        dxh = d_vn * lng_ref[...]
        d_vg = rstd * (dxh - jnp.mean(dxh, axis=-1, keepdims=True)
                       - xhat * jnp.mean(dxh * xhat, axis=-1, keepdims=True))
        dp_ref[:, OFF_V:OFF_V + D_A] = (d_vg * _gelu_grad(v_ref[...])).astype(BF16)
        dp_ref[:, OFF_QB:OFF_QB + D_B] = dq_ref[...].astype(BF16)
        dp_ref[:, OFF_KB:OFF_KB + D_B] = dk_ref[...].astype(BF16)
        dp_ref[:, OFF_VB:OFF_VB + D_B] = dv_ref[...].astype(BF16)
        _, dszb = _silu_and_grad(zb_ref[...])
        dp_ref[:, OFF_ZB:OFF_ZB + D_B] = (dyb_ref[...] * ob_ref[...] * dszb).astype(BF16)
        dp_ref[:, OFF_QC:OFF_QC + D_C] = dqc_ref[...].astype(BF16)
        _, dszc = _silu_and_grad(zc_ref[...])
        dp_ref[:, OFF_ZC:OFF_ZC + D_C] = (dyc_ref[...] * oc_ref[...] * dszc).astype(BF16)

        @pl.when(i == n - 1)
        def _():
            ch = lax.broadcasted_iota(jnp.int32, (D_A, CHUNK), 0)
            gcol = lax.broadcasted_iota(jnp.int32, (D_A, CHUNK), 1)
            pick = (ch // (D_A // A_GROUPS) == gcol).astype(BF16)
            rest = dbias_ref[...]
            tot = jnp.zeros((CHUNK, CHUNK), F32)
            for _ in range(3):
                term = rest.astype(BF16)
                tot = tot + _dot(term, pick, NN)
                rest = rest - term.astype(F32)
            dsb_ref[...] = tot

    wide = lambda off: pl.BlockSpec((CHUNK, D_A), lambda i: (i, off // D_A))
    narrow = lambda off: pl.BlockSpec((CHUNK, D_B), lambda i: (i, off // D_B))
    vec = pl.BlockSpec((1, D_A), lambda i: (0, 0))
    wspec = pl.BlockSpec((A_GROUPS, CHUNK, CHUNK), lambda i: (0, 0, 0))
    bspec = pl.BlockSpec((CHUNK, D_A), lambda i: (0, 0))
    return pl.pallas_call(
        body, name=name, grid=(n,),
        in_specs=[wide(OFF_U), wide(OFF_V), wide(OFF_ZA), narrow(OFF_ZB), narrow(OFF_ZC),
                  wide(0), narrow(OFF_YB), narrow(OFF_YC), narrow(0), narrow(0), narrow(0), narrow(0), narrow(0),
                  narrow(0), vec, vec, wspec, wspec, bspec],
        out_specs=[pl.BlockSpec((CHUNK, IN_WIDTH), lambda i: (i, 0)), wspec,
                   pl.BlockSpec((CHUNK, CHUNK), lambda i: (0, 0)), vec, vec],
        out_shape=[jax.ShapeDtypeStruct((s_len, IN_WIDTH), BF16), jax.ShapeDtypeStruct((A_GROUPS, CHUNK, CHUNK), F32),
                   jax.ShapeDtypeStruct((CHUNK, CHUNK), F32), jax.ShapeDtypeStruct((1, D_A), F32),
                   jax.ShapeDtypeStruct((1, D_A), F32)],
        scratch_shapes=[pltpu.VMEM((CHUNK, D_A), F32)],
        compiler_params=_params(("arbitrary",)),
    )(proj, proj, proj, proj, proj, dy, dy, dy, o_b, o_c, dq_b, dk_b, dv_b, dq_c, lng, lnb, w_s, w_s_t, bias)


IN_SHARD = IN_WIDTH // N_CHIPS
ROW_SHARD = D_MODEL // N_CHIPS


def _bias_rows(sgu_b_l):
    return jnp.repeat(sgu_b_l.T, D_A // A_GROUPS, axis=1)


def _layer_fwd(l, x, mem, sm, w_in_all, rest):
    s_len = x.shape[0]
    m_len = mem.shape[0]
    tm = min(1024, s_len)
    tn = 768
    per = IN_SHARD // tn
    h, h_t = _rms_fwd(f"rms_fwd_{l}", x, sm["norm_g"][l][None], min(256, s_len), transposed=True)
    proj = _matmul(
        f"in_proj_{l}", h, w_in_all, grid=(s_len // tm, IN_WIDTH // tn, 1),
        a_spec=pl.BlockSpec((tm, D_MODEL), lambda i, j, k: (i, 0)),
        b_spec=pl.BlockSpec((None, D_MODEL, tn), lambda i, j, k: (j // per, 0, j % per)),
        o_spec=pl.BlockSpec((tm, tn), lambda i, j, k: (i, j)),
        out_shape=jax.ShapeDtypeStruct((s_len, IN_WIDTH), F32), dims=NN)
    w_kv_all, w_out_all, after = rest(proj)
    mem_h = _rms_fwd(f"mem_rms_fwd_{l}", mem, sm["mem_norm_g"][l][None], m_len, after)
    mem_kv = _matmul(
        f"mem_kv_{l}", mem_h, w_kv_all, grid=(1, 2, N_CHIPS),
        a_spec=pl.BlockSpec((m_len, ROW_SHARD), lambda i, j, k: (0, k)),
        b_spec=pl.BlockSpec((None, ROW_SHARD, D_C), lambda i, j, k: (k, 0, j)),
        o_spec=pl.BlockSpec((m_len, D_C), lambda i, j, k: (0, j)),
        out_shape=jax.ShapeDtypeStruct((m_len, 2 * D_C), F32), dims=NN)
    o_b = _sb_fwd(f"sb_fwd_{l}", proj)
    qg, kg = sm["q_norm_g"][l][None], sm["k_norm_g"][l][None]
    o_c = _mem_fwd(f"mem_fwd_{l}", proj, mem_kv, qg, kg)
    bias = _bias_rows(sm["sgu_b"][l])
    y, y_t = _gate_fwd(f"gate_fwd_{l}", proj, o_b, o_c, sm["sgu_ln_g"][l][None], sm["sgu_ln_b"][l][None],
                       sm["sgu_w"][l], bias)
    tn_o = 512
    x_next = _matmul(
        f"out_proj_{l}", y, w_out_all, grid=(s_len // tm, D_MODEL // tn_o, 1),
        a_spec=pl.BlockSpec((tm, D_MODEL), lambda i, j, k: (i, 0)),
        b_spec=pl.BlockSpec((N_CHIPS, ROW_SHARD, tn_o), lambda i, j, k: (0, 0, j)),
        o_spec=pl.BlockSpec((tm, tn_o), lambda i, j, k: (i, j)),
        out_shape=jax.ShapeDtypeStruct((s_len, D_MODEL), F32), dims=NN,
        res=x, res_spec=pl.BlockSpec((tm, tn_o), lambda i, j, k: (i, j)))
    saved = dict(x=x, h_t=h_t, proj=proj, mem_h=mem_h, mem_kv=mem_kv, o_b=o_b, o_c=o_c, y_t=y_t, bias=bias,
                 weights=(w_in_all, w_kv_all, w_out_all))
    return x_next, saved


def _layer_bwd(l, dxo, dxo_b, mem, sm, saved, on_weight_grads=None):
    s_len = dxo.shape[0]
    m_len = mem.shape[0]
    proj, y_t, h_t, mem_h, mem_kv = saved["proj"], saved["y_t"], saved["h_t"], saved["mem_h"], saved["mem_kv"]
    w_in_all, w_kv_all, w_out_all = saved["weights"]
    tm = min(1024, s_len)
    tk = s_len
    g_out = _matmul(
        f"d_w_out_{l}", y_t, dxo_b, grid=(N_CHIPS, D_MODEL // 1024, s_len // tk),
        a_spec=pl.BlockSpec((ROW_SHARD, tk), lambda i, j, k: (i, k)),
        b_spec=pl.BlockSpec((tk, 1024), lambda i, j, k: (k, j)),
        o_spec=pl.BlockSpec((None, ROW_SHARD, 1024), lambda i, j, k: (i, 0, j)),
        out_shape=jax.ShapeDtypeStruct((N_CHIPS, ROW_SHARD, D_MODEL), F32), dims=NN)
    token, finish = (None, None) if on_weight_grads is None else on_weight_grads("out", [g_out])
    dy = _matmul(
        f"d_y_{l}", dxo_b, w_out_all, grid=(s_len // tm, N_CHIPS, 1),
        a_spec=pl.BlockSpec((tm, D_MODEL), lambda i, j, k: (i, 0)),
        b_spec=pl.BlockSpec((None, ROW_SHARD, D_MODEL), lambda i, j, k: (j, 0, 0)),
        o_spec=pl.BlockSpec((tm, ROW_SHARD), lambda i, j, k: (i, j)),
        out_shape=jax.ShapeDtypeStruct((s_len, D_MODEL), F32), dims=NT, after=token)
    token = None if finish is None else finish(dy)
    qg, kg = sm["q_norm_g"][l][None], sm["k_norm_g"][l][None]
    dqkv = _sb_bwd(f"sb_bwd_{l}", proj, dy, token)
    dq_c, dmk, dmv, dqg, dkg = _mem_bwd(f"mem_bwd_{l}", proj, mem_kv, qg, kg, dy)
    w_s = sm["sgu_w"][l]
    dproj, dws, dbias, dlng, dlnb = _gate_bwd(
        f"gate_bwd_{l}", proj, dy, saved["o_b"], saved["o_c"], dqkv, dq_c, sm["sgu_ln_g"][l][None],
        sm["sgu_ln_b"][l][None], w_s, jnp.swapaxes(w_s, 1, 2), saved["bias"])
    tn = 768
    per = IN_SHARD // tn
    g_in = _matmul(
        f"d_w_in_{l}", h_t, dproj, grid=(D_MODEL // 1024, IN_WIDTH // tn, s_len // tk),
        a_spec=pl.BlockSpec((1024, tk), lambda i, j, k: (i, k)),
        b_spec=pl.BlockSpec((tk, tn), lambda i, j, k: (k, j)),
        o_spec=pl.BlockSpec((None, 1024, tn), lambda i, j, k: (j // per, i, j % per)),
        out_shape=jax.ShapeDtypeStruct((N_CHIPS, D_MODEL, IN_SHARD), F32), dims=NN)
    dkv_b = jnp.concatenate([dmk, dmv], axis=1).astype(BF16)
    g_kv = _matmul(
        f"d_w_kv_{l}", mem_h, dkv_b, grid=(N_CHIPS, 1, 1),
        a_spec=pl.BlockSpec((m_len, ROW_SHARD), lambda i, j, k: (0, i)),
        b_spec=pl.BlockSpec((m_len, 2 * D_C), lambda i, j, k: (0, 0)),
        o_spec=pl.BlockSpec((None, ROW_SHARD, 2 * D_C), lambda i, j, k: (i, 0, 0)),
        out_shape=jax.ShapeDtypeStruct((N_CHIPS, ROW_SHARD, 2 * D_C), F32), dims=TN)
    token, finish = (None, None) if on_weight_grads is None else on_weight_grads("in", [g_in, g_kv])
    dh = _matmul(
        f"d_h_{l}", dproj, w_in_all, grid=(s_len // tm, D_MODEL // 1024, N_CHIPS),
        a_spec=pl.BlockSpec((tm, IN_SHARD), lambda i, j, k: (i, k)),
        b_spec=pl.BlockSpec((None, 1024, IN_SHARD), lambda i, j, k: (k, j, 0)),
        o_spec=pl.BlockSpec((tm, 1024), lambda i, j, k: (i, j)),
        out_shape=jax.ShapeDtypeStruct((s_len, D_MODEL), F32), dims=NT, after=token)
    token = None if finish is None else finish(dh)
    dx, dx_b, dng = _rms_bwd(f"rms_bwd_{l}", saved["x"], dh, dxo, sm["norm_g"][l][None], min(256, s_len), token)
    d_mem_h = _matmul(
        f"d_mem_h_{l}", dkv_b, w_kv_all, grid=(1, N_CHIPS, 1),
        a_spec=pl.BlockSpec((m_len, 2 * D_C), lambda i, j, k: (0, 0)),
        b_spec=pl.BlockSpec((None, ROW_SHARD, 2 * D_C), lambda i, j, k: (j, 0, 0)),
        o_spec=pl.BlockSpec((m_len, ROW_SHARD), lambda i, j, k: (0, j)),
        out_shape=jax.ShapeDtypeStruct((m_len, D_MODEL), F32), dims=NT)
    dmng = _rms_gain_grad(f"mem_rms_bwd_{l}", mem, d_mem_h)
    dsgu_b = dbias[:, :A_GROUPS].T
    small = dict(norm_g=dng[0], sgu_ln_g=dlng[0], sgu_ln_b=dlnb[0], sgu_w=dws, sgu_b=dsgu_b, mem_norm_g=dmng[0],
                 q_norm_g=dqg[0], k_norm_g=dkg[0])
    return dx, dx_b, small, g_in, g_kv, g_out


SMALL_NAMES = ("norm_g", "sgu_ln_g", "sgu_ln_b", "sgu_w", "sgu_b", "mem_norm_g", "q_norm_g", "k_norm_g")


def _local_step(x, mem, target, sm, w_all):
    saved = []
    cur = x
    for l in range(DEPTH):
        cur, sv = _layer_fwd(l, cur, mem, sm, w_all[l][0], lambda proj, l=l: (w_all[l][1], w_all[l][2], None))
        saved.append(sv)
    dxo, dxo_b, loss = _loss_and_grad("loss", cur, target, min(256, x.shape[0]))
    small, big = [None] * DEPTH, [None] * DEPTH
    for l in reversed(range(DEPTH)):
        dxo, dxo_b, small[l], *big[l] = _layer_bwd(l, dxo, dxo_b, mem, sm, saved[l])
    small = {k: jnp.stack([small[l][k] for l in range(DEPTH)]) for k in SMALL_NAMES}
    return loss, dxo, small, big


def _place():
    x, y, c = lax.axis_index("x"), lax.axis_index("y"), lax.axis_index("c")
    return x, y, c


def _other_chips(x, y):
    return [(1 - x, y, 2 * (1 - x) + y), (x, 1 - y, 2 * x + 1 - y), (1 - x, 1 - y, 2 * (1 - x) + 1 - y)]


AG_CHUNKS = 4
D2D_CHUNKS = 8


def _place_index():
    return jnp.stack([2 * lax.axis_index("x") + lax.axis_index("y"), lax.axis_index("c")]).astype(jnp.int32)


def _cast_into_slot(name, w, l, place):
    _, rows, cols = w.shape
    tr = min(256, rows)

    def body(p_ref, w_ref, o_ref):
        o_ref[...] = w_ref[...].astype(BF16)

    return pl.pallas_call(
        body, name=name,
        grid_spec=pltpu.PrefetchScalarGridSpec(
            num_scalar_prefetch=1, grid=(rows // tr,),
            in_specs=[pl.BlockSpec((None, tr, cols), lambda i, p: (l, i, 0))],
            out_specs=pl.BlockSpec((None, tr, cols), lambda i, p: (p[0], i, 0))),
        out_shape=jax.ShapeDtypeStruct((N_CHIPS, rows, cols), BF16),
        compiler_params=_params(("parallel",)),
    )(place, w)


HBM = pl.BlockSpec(memory_space=pltpu.HBM)
SEM = pl.BlockSpec(memory_space=pltpu.SEMAPHORE)
DATAFLOW = pltpu.SideEffectType.DATAFLOW_SIDE_EFFECTING


def _in_hbm(a):
    return pltpu.with_memory_space_constraint(a, pltpu.HBM)


def _chip_copies_start(name, srcs, lands, make_copy, after=None):
    n_t = len(srcs)
    in_place = lands is None
    n_after = 0 if after is None else 1

    def body(*refs):
        src = refs[:n_t]
        k = (n_t if in_place else 2 * n_t) + n_after
        send_sems, recv_sems = refs[k], refs[k + 1]
        land = refs[k + 2:k + 2 + n_t] if in_place else refs[k + 2 + n_t:k + 2 + 2 * n_t]
        token = refs[-1]
        x, y, c = _place()
        me = 2 * x + y
        for t in range(n_t):
            for px, py, pk in _other_chips(x, y):
                s, d = make_copy(src[t], land[t], me, pk, c)
                pltpu.make_async_remote_copy(
                    src_ref=s, dst_ref=d, send_sem=send_sems.at[t], recv_sem=recv_sems.at[t],
                    device_id=(px, py, c), device_id_type=MESH).start()
        token[...] = jnp.zeros_like(token)

    bufs = list(srcs) if in_place else list(srcs) + list(lands)
    outs = pl.pallas_call(
        body, name=name,
        in_specs=[HBM] * len(bufs) + [ANY] * n_after,
        out_specs=[SEM, SEM] + [HBM] * len(bufs) + [pl.BlockSpec(memory_space=pltpu.VMEM)],
        out_shape=[pltpu.SemaphoreType.DMA((n_t,)), pltpu.SemaphoreType.DMA((n_t,))]
        + [pltpu.HBM(b.shape, b.dtype) for b in bufs] + [jax.ShapeDtypeStruct((8, 128), F32)],
        input_output_aliases={i: 2 + i for i in range(len(bufs))},
        compiler_params=pltpu.CompilerParams(has_side_effects=DATAFLOW),
    )(*[_in_hbm(b) for b in bufs], *([] if after is None else [after]))
    return outs[0], outs[1], list(outs[2:2 + len(bufs)]), outs[-1]


def _chip_copies_wait(name, send_sems, recv_sems, bufs, sent, landed, after):
    n_b = len(bufs)

    def body(*refs):
        buf = refs[:n_b]
        send_ref, recv_ref = refs[n_b], refs[n_b + 1]
        x, y, c = _place()
        for t, (s, d) in enumerate(zip(sent(buf), landed(buf))):
            out = pltpu.make_async_remote_copy(src_ref=s, dst_ref=s, send_sem=send_ref.at[t], recv_sem=recv_ref.at[t],
                                               device_id=(x, y, c), device_id_type=MESH)
            out.wait_send()
            arrived = pltpu.make_async_remote_copy(src_ref=d, dst_ref=d, send_sem=send_ref.at[t],
                                                   recv_sem=recv_ref.at[t], device_id=(x, y, c), device_id_type=MESH)
            arrived.wait_recv()

    return pl.pallas_call(
        body, name=name,
        in_specs=[HBM] * n_b + [SEM, SEM, ANY], out_specs=[HBM] * n_b,
        out_shape=[pltpu.HBM(b.shape, b.dtype) for b in bufs],
        input_output_aliases={i: i for i in range(n_b)},
        compiler_params=pltpu.CompilerParams(has_side_effects=DATAFLOW),
    )(*bufs, send_sems, recv_sems, after)


def _gather_start(name, bufs, after=None):
    def make_copy(src, land, me, pk, c):
        hr = src.shape[1] // 2
        return src.at[me, pl.ds(c * hr, hr)], land.at[me, pl.ds(c * hr, hr)]

    return _chip_copies_start(name, bufs, None, make_copy, after)


def _gather_wait(name, send_sems, recv_sems, bufs, after):
    def three_halves(buf):
        return [b.at[pl.ds(0, 3), pl.ds(0, b.shape[1] // 2)] for b in buf]

    return _chip_copies_wait(name, send_sems, recv_sems, bufs, three_halves, three_halves, after)


def _gather_forward(name, bufs):
    n_t = len(bufs)
    n = 3 * n_t * D2D_CHUNKS

    def body(*refs):
        mine, buf = refs[:n_t], refs[n_t:2 * n_t]
        send_sems, recv_sems = refs[2 * n_t:]
        x, y, c = _place()
        chips = _other_chips(x, y)

        def piece(ref, t, slot, core, q):
            hr = ref[t].shape[1] // 2
            cr = hr // D2D_CHUNKS
            return ref[t].at[slot, pl.ds(core * hr + q * cr, cr)]

        copies = []
        for q in range(D2D_CHUNKS):
            for t in range(n_t):
                for j, (_, _, pk) in enumerate(chips):
                    s = (t * 3 + j) * D2D_CHUNKS + q
                    cp = pltpu.make_async_remote_copy(
                        src_ref=piece(mine, t, pk, c, q), dst_ref=piece(buf, t, pk, c, q), send_sem=send_sems.at[s],
                        recv_sem=recv_sems.at[s], device_id=(x, y, 1 - c), device_id_type=MESH)
                    cp.start()
                    copies.append(cp)
        for q in range(D2D_CHUNKS):
            for t in range(n_t):
                for j, (_, _, pk) in enumerate(chips):
                    s = (t * 3 + j) * D2D_CHUNKS + q
                    theirs = piece(buf, t, pk, 1 - c, q)
                    pltpu.make_async_remote_copy(
                        src_ref=theirs, dst_ref=theirs, send_sem=send_sems.at[s], recv_sem=recv_sems.at[s],
                        device_id=(x, y, 1 - c), device_id_type=MESH).wait_recv()
        for cp in copies:
            cp.wait_send()

    return pl.pallas_call(
        body, name=name,
        in_specs=[ANY] * n_t, out_specs=[ANY] * n_t,
        out_shape=[jax.ShapeDtypeStruct(b.shape, b.dtype) for b in bufs],
        input_output_aliases={t: t for t in range(n_t)},
        scratch_shapes=[pltpu.SemaphoreType.DMA((n,)), pltpu.SemaphoreType.DMA((n,))],
        compiler_params=pltpu.CompilerParams(has_side_effects=True),
    )(*bufs)


def _core_exchange_start(name, grads):
    n_t = len(grads)
    lands = [lax.empty((g.shape[0], g.shape[1] // 2, g.shape[2]), g.dtype) for g in grads]

    def body(*refs):
        src = refs[:n_t]
        send_sems, recv_sems = refs[2 * n_t], refs[2 * n_t + 1]
        land = refs[2 * n_t + 2 + n_t:2 * n_t + 2 + 2 * n_t]
        token = refs[-1]
        x, y, c = _place()
        for q in range(D2D_CHUNKS):
            for t in range(n_t):
                hr = src[t].shape[1] // 2
                cr = hr // D2D_CHUNKS
                pltpu.make_async_remote_copy(
                    src_ref=src[t].at[:, pl.ds((1 - c) * hr + q * cr, cr)], dst_ref=land[t].at[:, pl.ds(q * cr, cr)],
                    send_sem=send_sems.at[t], recv_sem=recv_sems.at[t], device_id=(x, y, 1 - c),
                    device_id_type=MESH).start()
        token[...] = jnp.zeros_like(token)

    bufs = list(grads) + lands
    outs = pl.pallas_call(
        body, name=name,
        in_specs=[HBM] * len(bufs),
        out_specs=[SEM, SEM] + [HBM] * len(bufs) + [pl.BlockSpec(memory_space=pltpu.VMEM)],
        out_shape=[pltpu.SemaphoreType.DMA((n_t,)), pltpu.SemaphoreType.DMA((n_t,))]
        + [pltpu.HBM(b.shape, b.dtype) for b in bufs] + [jax.ShapeDtypeStruct((8, 128), F32)],
        input_output_aliases={i: 2 + i for i in range(len(bufs))},
        compiler_params=pltpu.CompilerParams(has_side_effects=DATAFLOW),
    )(*[_in_hbm(b) for b in bufs])
    return outs[0], outs[1], list(outs[2:2 + len(bufs)]), outs[-1]


def _core_exchange_wait(name, send_sems, recv_sems, bufs, after):
    n_t = len(bufs) // 2

    def body(*refs):
        land = refs[n_t:2 * n_t]
        send_ref, recv_ref = refs[2 * n_t], refs[2 * n_t + 1]
        x, y, c = _place()
        for t in range(n_t):
            whole = pltpu.make_async_remote_copy(src_ref=land[t], dst_ref=land[t], send_sem=send_ref.at[t],
                                                 recv_sem=recv_ref.at[t], device_id=(x, y, c), device_id_type=MESH)
            whole.wait_send()
            whole.wait_recv()

    outs = pl.pallas_call(
        body, name=name,
        in_specs=[HBM] * (2 * n_t) + [SEM, SEM, ANY], out_specs=[HBM] * (2 * n_t),
        out_shape=[pltpu.HBM(b.shape, b.dtype) for b in bufs],
        input_output_aliases={i: i for i in range(2 * n_t)},
        compiler_params=pltpu.CompilerParams(has_side_effects=DATAFLOW),
    )(*bufs, send_sems, recv_sems, after)
    return list(outs[:n_t]), list(outs[n_t:])


def _add_to_bf16(name, full, theirs, place):
    chips, rows, cols = theirs.shape
    tr = min(256, rows)
    per = rows // tr

    def body(p_ref, a_ref, b_ref, o_ref):
        o_ref[...] = (a_ref[...] + b_ref[...]).astype(BF16)

    blk = pl.BlockSpec((None, tr, cols), lambda k, i, p: (k, i, 0))
    return pl.pallas_call(
        body, name=name,
        grid_spec=pltpu.PrefetchScalarGridSpec(
            num_scalar_prefetch=1, grid=(chips, per),
            in_specs=[pl.BlockSpec((None, tr, cols), lambda k, i, p: (k, p[1] * per + i, 0)), blk],
            out_specs=blk),
        out_shape=jax.ShapeDtypeStruct(theirs.shape, BF16), compiler_params=_params(("parallel",) * 2),
    )(place, full, theirs)


def _chip_exchange_start(name, parts):
    lands = [lax.empty(p.shape, p.dtype) for p in parts]
    return _chip_copies_start(name, parts, lands, lambda src, land, me, pk, c: (src.at[pk], land.at[me]))


def _chip_exchange_wait(name, send_sems, recv_sems, bufs, after):
    n_t = len(bufs) // 2
    return _chip_copies_wait(name, send_sems, recv_sems, bufs,
                             lambda buf: [b.at[pl.ds(0, 3)] for b in buf[:n_t]],
                             lambda buf: [b.at[pl.ds(0, 3)] for b in buf[n_t:]], after)


def _sum_chips(name, parts, landed, place, l, stacked):
    chips, rows, cols = landed.shape
    tr = min(256, rows)
    per = rows // tr

    def body(p_ref, own_ref, *refs):
        land, o_ref = refs[:chips], refs[-1]
        tot = None
        for k in range(chips):
            term = jnp.where(p_ref[0] == k, own_ref[...], land[k][...]).astype(F32)
            tot = term if tot is None else tot + term
        o_ref[...] = tot

    def from_chip(k):
        return pl.BlockSpec((None, tr, cols), lambda i, p: (jnp.where(p[0] == k, (k + 1) % chips, k), i, 0))

    in_specs = [pl.BlockSpec((None, tr, cols), lambda i, p: (p[0], i, 0))] + [from_chip(k) for k in range(chips)]
    args = [parts] + [landed] * chips
    aliases = {}
    if stacked is not None:
        in_specs.append(ANY)
        args.append(stacked)
        aliases = {len(args): 0}
    return pl.pallas_call(
        body, name=name,
        grid_spec=pltpu.PrefetchScalarGridSpec(
            num_scalar_prefetch=1, grid=(per,), in_specs=in_specs,
            out_specs=pl.BlockSpec((None, tr, cols), lambda i, p: (l, p[1] * per + i, 0))),
        out_shape=jax.ShapeDtypeStruct((DEPTH, 2 * rows, cols), F32), input_output_aliases=aliases,
        compiler_params=_params(("parallel",)),
    )(place, *args)


def _core_share(name, bufs, l):
    n_t = len(bufs)
    n = n_t * D2D_CHUNKS

    def body(*refs):
        mine, buf = refs[:n_t], refs[n_t:2 * n_t]
        send_sems, recv_sems = refs[2 * n_t:]
        x, y, c = _place()

        def piece(ref, t, core, q):
            hr = ref[t].shape[1] // 2
            cr = hr // D2D_CHUNKS
            return ref[t].at[l, pl.ds(core * hr + q * cr, cr)]

        copies = []
        for q in range(D2D_CHUNKS):
            for t in range(n_t):
                s = t * D2D_CHUNKS + q
                cp = pltpu.make_async_remote_copy(
                    src_ref=piece(mine, t, c, q), dst_ref=piece(buf, t, c, q), send_sem=send_sems.at[s],
                    recv_sem=recv_sems.at[s], device_id=(x, y, 1 - c), device_id_type=MESH)
                cp.start()
                copies.append(cp)
        for q in range(D2D_CHUNKS):
            for t in range(n_t):
                s = t * D2D_CHUNKS + q
                theirs = piece(buf, t, 1 - c, q)
                pltpu.make_async_remote_copy(
                    src_ref=theirs, dst_ref=theirs, send_sem=send_sems.at[s], recv_sem=recv_sems.at[s],
                    device_id=(x, y, 1 - c), device_id_type=MESH).wait_recv()
        for cp in copies:
            cp.wait_send()

    return pl.pallas_call(
        body, name=name,
        in_specs=[ANY] * n_t, out_specs=[ANY] * n_t,
        out_shape=[jax.ShapeDtypeStruct(b.shape, b.dtype) for b in bufs],
        input_output_aliases={t: t for t in range(n_t)},
        scratch_shapes=[pltpu.SemaphoreType.DMA((n,)), pltpu.SemaphoreType.DMA((n,))],
        compiler_params=pltpu.CompilerParams(has_side_effects=True),
    )(*bufs)


def _all_reduce_small(vec, after=None):
    rows, lanes = vec.shape
    hr = rows // 2

    def body(v_ref, *refs):
        o_ref, sib_ref, chips_ref, send_sems, recv_sems = refs[-5:]
        x, y, c = _place()
        me = 2 * x + y
        sibling = (x, y, 1 - c)
        mine = pl.ds(pl.multiple_of(c * hr, 8), hr)
        theirs = pl.ds(pl.multiple_of((1 - c) * hr, 8), hr)
        swap = pltpu.make_async_remote_copy(
            src_ref=v_ref.at[theirs], dst_ref=sib_ref, send_sem=send_sems.at[0], recv_sem=recv_sems.at[0],
            device_id=sibling, device_id_type=MESH)
        swap.start()
        swap.wait_recv()
        chips_ref[me] = v_ref[mine] + sib_ref[...]
        copies = []
        for j, (px, py, pk) in enumerate(_other_chips(x, y)):
            cp = pltpu.make_async_remote_copy(
                src_ref=chips_ref.at[me], dst_ref=chips_ref.at[me], send_sem=send_sems.at[1 + j],
                recv_sem=recv_sems.at[1 + j], device_id=(px, py, c), device_id_type=MESH)
            cp.start()
            copies.append(cp)
        for j, (px, py, pk) in enumerate(_other_chips(x, y)):
            pltpu.make_async_remote_copy(
                src_ref=chips_ref.at[pk], dst_ref=chips_ref.at[pk], send_sem=send_sems.at[1 + j],
                recv_sem=recv_sems.at[1 + j], device_id=(px, py, c), device_id_type=MESH).wait_recv()
        tot = chips_ref[0]
        for k in range(1, N_CHIPS):
            tot = tot + chips_ref[k]
        o_ref[mine] = tot
        share = pltpu.make_async_remote_copy(
            src_ref=o_ref.at[mine], dst_ref=o_ref.at[mine], send_sem=send_sems.at[4], recv_sem=recv_sems.at[4],
            device_id=sibling, device_id_type=MESH)
        share.start()
        pltpu.make_async_remote_copy(
            src_ref=o_ref.at[theirs], dst_ref=o_ref.at[theirs], send_sem=send_sems.at[4], recv_sem=recv_sems.at[4],
            device_id=sibling, device_id_type=MESH).wait_recv()
        swap.wait_send()
        for cp in copies:
            cp.wait_send()
        share.wait_send()

    vm = pl.BlockSpec(memory_space=pltpu.VMEM)
    return pl.pallas_call(
        body, name="small_all_reduce", in_specs=[vm] + ([] if after is None else [ANY]), out_specs=vm,
        out_shape=jax.ShapeDtypeStruct((rows, lanes), F32),
        scratch_shapes=[pltpu.VMEM((hr, lanes), F32), pltpu.VMEM((N_CHIPS, hr, lanes), F32),
                        pltpu.SemaphoreType.DMA((5,)), pltpu.SemaphoreType.DMA((5,))],
        compiler_params=pltpu.CompilerParams(has_side_effects=True, vmem_limit_bytes=48 * MIB),
    )(vec, *([] if after is None else [after]))


def _adamw(name, w, g, m, v, l=0, done=None, after=None):
    layers, rows, cols = w.shape
    tr = rows
    for cand in (256, 128, 64, 32, 16, 8):
        if rows % cand == 0:
            tr = cand
            break
    c1 = 1.0 - ADAM_B1 ** ADAM_STEP
    c2 = 1.0 - ADAM_B2 ** ADAM_STEP

    def body(w_ref, g_ref, m_ref, v_ref, *refs):
        go_ref, d_ref, nm_ref, nv_ref = refs[-4:]
        gv = g_ref[...]
        nm = ADAM_B1 * m_ref[...] + (1.0 - ADAM_B1) * gv
        nv = ADAM_B2 * v_ref[...] + (1.0 - ADAM_B2) * (gv * gv)
        go_ref[...] = gv
        nm_ref[...] = nm
        nv_ref[...] = nv
        d_ref[...] = -ADAM_LR * ((nm / c1) / (jnp.sqrt(nv / c2) + ADAM_EPS) + ADAM_WD * w_ref[...])

    blk = pl.BlockSpec((None, tr, cols), lambda i: (l, i, 0))
    out = jax.ShapeDtypeStruct((layers, rows, cols), F32)
    extra = ([] if done is None else list(done)) + ([] if after is None else [after])
    aliases = {} if done is None else {4 + i: i for i in range(4)}
    return pl.pallas_call(
        body, name=name, grid=(rows // tr,), in_specs=[blk] * 4 + [ANY] * len(extra), out_specs=[blk] * 4,
        out_shape=[out] * 4, input_output_aliases=aliases,
        compiler_params=_params(("parallel",)),
    )(w, g, m, v, *extra)


def _pack_small(parts):
    flat = jnp.concatenate([parts[k].reshape(-1) for k in SMALL_NAMES])
    n = flat.shape[0]
    rows = -(-n // (256 * 128)) * 256
    return jnp.pad(flat, (0, rows * 128 - n)).reshape(rows, 128)


def _unpack_small(packed, like):
    flat = packed.reshape(-1)
    out, off = {}, 0
    for k in SMALL_NAMES:
        n = like[k].size
        out[k] = flat[off:off + n].reshape(like[k].shape)
        off += n
    return out


WEIGHT_ORDER = ("norm_g", "w_in", "sgu_ln_g", "sgu_ln_b", "sgu_w", "sgu_b", "mem_norm_g", "w_mem_kv", "q_norm_g",
                "k_norm_g", "w_out")


def kernel(x, mem, norm_g, w_in, sgu_ln_g, sgu_ln_b, sgu_w, sgu_b, mem_norm_g, w_mem_kv, q_norm_g, k_norm_g, w_out, loss_target, m_norm_g, m_w_in, m_sgu_ln_g, m_sgu_ln_b, m_sgu_w, m_sgu_b, m_mem_norm_g, m_w_mem_kv, m_q_norm_g, m_k_norm_g, m_w_out, v_norm_g, v_w_in, v_sgu_ln_g, v_sgu_ln_b, v_sgu_w, v_sgu_b, v_mem_norm_g, v_w_mem_kv, v_q_norm_g, v_k_norm_g, v_w_out):
    weights = dict(norm_g=norm_g, w_in=w_in, sgu_ln_g=sgu_ln_g, sgu_ln_b=sgu_ln_b, sgu_w=sgu_w, sgu_b=sgu_b,
                   mem_norm_g=mem_norm_g, w_mem_kv=w_mem_kv, q_norm_g=q_norm_g, k_norm_g=k_norm_g, w_out=w_out)
    mom_m = dict(norm_g=m_norm_g, w_in=m_w_in, sgu_ln_g=m_sgu_ln_g, sgu_ln_b=m_sgu_ln_b, sgu_w=m_sgu_w, sgu_b=m_sgu_b,
                 mem_norm_g=m_mem_norm_g, w_mem_kv=m_w_mem_kv, q_norm_g=m_q_norm_g, k_norm_g=m_k_norm_g, w_out=m_w_out)
    mom_v = dict(norm_g=v_norm_g, w_in=v_w_in, sgu_ln_g=v_sgu_ln_g, sgu_ln_b=v_sgu_ln_b, sgu_w=v_sgu_w, sgu_b=v_sgu_b,
                 mem_norm_g=v_mem_norm_g, w_mem_kv=v_w_mem_kv, q_norm_g=v_q_norm_g, k_norm_g=v_k_norm_g, w_out=v_w_out)
    big = ("w_in", "w_mem_kv", "w_out")
    sm = {k: weights[k] for k in SMALL_NAMES}

    place = _place_index()
    xs, mems, target = x[0], mem[0], loss_target[0]

    slots = [[_cast_into_slot(f"cast_{k}_{l}", weights[k], l, place) for k in big] for l in range(DEPTH)]
    saved = [None] * DEPTH

    def gathered(tag, flight, after):
        send_sems, recv_sems, bufs, _ = flight
        return _gather_forward(f"gather_forward_{tag}", _gather_wait(f"gather_wait_{tag}", send_sems, recv_sems,
                                                                      bufs, after))

    flights = {}

    def start_gather(l, after=None):
        flights[l, "in"] = _gather_start(f"gather_start_{l}_in", slots[l][:1], after)
        flights[l, "rest"] = _gather_start(f"gather_start_{l}_rest", slots[l][1:], flights[l, "in"][3])
        return flights[l, "rest"][3]

    start_gather(0)
    cur = xs
    for l in range(DEPTH):
        (w_in_all,) = gathered(f"{l}_in", flights[l, "in"], flights[l, "rest"][3] if l == 0 else cur)

        def rest(proj, l=l):
            w_kv_all, w_out_all = gathered(f"{l}_rest", flights[l, "rest"], proj)
            token = start_gather(l + 1, w_out_all) if l + 1 < DEPTH else None
            return w_kv_all, w_out_all, token

        cur, saved[l] = _layer_fwd(l, cur, mems, sm, w_in_all, rest)
    dxo, dxo_b, loss_part = _loss_and_grad("loss", cur, target, min(256, xs.shape[0]))
    loss = lax.psum(loss_part[0, 0], ("x", "y", "c"))

    small_g = [None] * DEPTH
    flight = {}
    for l in reversed(range(DEPTH)):
        def start_exchange(group, full, l=l):
            send_sems, recv_sems, bufs, token = _core_exchange_start(f"grad_core_start_{l}_{group}", full)

            def finish(after):
                grads_l, theirs = _core_exchange_wait(f"grad_core_wait_{l}_{group}", send_sems, recv_sems, bufs, after)
                parts = [_add_to_bf16(f"grad_core_sum_{l}_{group}_{t}", g, th, place)
                         for t, (g, th) in enumerate(zip(grads_l, theirs))]
                *flight[l, group], token = _chip_exchange_start(f"grad_chip_start_{l}_{group}", parts)
                return token

            return token, finish

        dxo, dxo_b, small_g[l], *_ = _layer_bwd(l, dxo, dxo_b, mems, sm, saved[l], on_weight_grads=start_exchange)
    grad_x = dxo

    groups = (("out", ("w_out",)), ("in", ("w_in", "w_mem_kv")))
    halves, stepped = dict.fromkeys(big), dict.fromkeys(big)
    small_g = {k: jnp.stack([small_g[l][k] for l in range(DEPTH)]) for k in SMALL_NAMES}
    after = grad_x
    for l in reversed(range(DEPTH)):
        for group, names in groups:
            send_sems, recv_sems, bufs = flight[l, group]
            bufs = _chip_exchange_wait(f"grad_chip_wait_{l}_{group}", send_sems, recv_sems, bufs, after)
            for t, k in enumerate(names):
                halves[k] = _sum_chips(f"grad_chip_sum_{l}_{k}", bufs[t], bufs[len(names) + t], place, l, halves[k])
        shared = dict(zip(big, _core_share(f"grad_core_share_{l}", [halves[k] for k in big], l)))
        for k in big:
            halves[k] = shared[k]
            stepped[k] = _adamw(f"adamw_{k}_{l}", weights[k], shared[k], mom_m[k], mom_v[k], l, stepped[k], after)
            after = stepped[k][1]
        if l == DEPTH - 1:
            small_sum = _all_reduce_small(_pack_small(small_g), after)
            packed = [a[None] for a in (_pack_small(sm), small_sum, _pack_small({k: mom_m[k] for k in SMALL_NAMES}),
                                        _pack_small({k: mom_v[k] for k in SMALL_NAMES}))]
            small_step = _adamw("adamw_small", *packed)

    grads, delta, new_m, new_v = ({k: stepped[k][i] for k in big} for i in range(4))
    for out, packed in zip((grads, delta, new_m, new_v), small_step):
        out.update(_unpack_small(packed[0], sm))
    return (loss, grad_x[None], *[grads[k] for k in WEIGHT_ORDER], *[delta[k] for k in WEIGHT_ORDER],
            *[new_m[k] for k in WEIGHT_ORDER], *[new_v[k] for k in WEIGHT_ORDER])
```

```python
import functools
import math

import jax
import jax.numpy as jnp
from jax import lax
from jax.experimental import pallas as pl
from jax.experimental.pallas import tpu as pltpu

F32 = jnp.float32
BF16 = jnp.bfloat16
MESH = pl.DeviceIdType.MESH

D_MODEL = 2048
DEPTH = 2
CHUNK = 128
D_A = 1024
A_GROUPS = 8
D_B = 512
D_C = 512
HEADS = 4
HEAD_DIM = 128
IN_WIDTH = 6144
N_CHIPS = 4
EPS = 1e-6
ATT_SCALE = 1.0 / math.sqrt(HEAD_DIM)

OFF_U, OFF_V, OFF_ZA = 0, 1024, 2048
OFF_QB, OFF_KB, OFF_VB, OFF_ZB = 3072, 3584, 4096, 4608
OFF_QC, OFF_ZC = 5120, 5632
OFF_YB, OFF_YC = 1024, 1536

ADAM_LR = 0.001
ADAM_B1 = 0.9
ADAM_B2 = 0.999
ADAM_EPS = 1e-08
ADAM_WD = 0.01
ADAM_STEP = 10

MIB = 1024 * 1024
ANY = pl.BlockSpec(memory_space=pl.ANY)


def _params(semantics=None, vmem_mb=48):
    return pltpu.CompilerParams(dimension_semantics=semantics, vmem_limit_bytes=vmem_mb * MIB)


def _gelu(x):
    return 0.5 * x * (1.0 + lax.erf(x * (1.0 / math.sqrt(2.0))))


def _gelu_grad(x):
    cdf = 0.5 * (1.0 + lax.erf(x * (1.0 / math.sqrt(2.0))))
    pdf = jnp.exp(-0.5 * x * x) * (1.0 / math.sqrt(2.0 * math.pi))
    return cdf + x * pdf


def _sigmoid(x):
    return 1.0 / (1.0 + jnp.exp(-x))


def _silu_and_grad(z):
    s = _sigmoid(z)
    return z * s, s * (1.0 + z * (1.0 - s))


def _split_bf16(x):
    hi = x.astype(BF16)
    lo = (x - hi.astype(F32)).astype(BF16)
    return hi, lo


def _dot(a, b, dims):
    return lax.dot_general(a, b, (dims, ((), ())), preferred_element_type=F32)


NN = ((1,), (0,))
NT = ((1,), (1,))
TN = ((0,), (0,))


def _matmul(name, a, b, *, grid, a_spec, b_spec, o_spec, out_shape, dims, res=None, res_spec=None, after=None,
            vmem_mb=48):
    nk = grid[2]
    n_in = 2 + (res is not None) + (after is not None)

    def body(*refs):
        a_ref, b_ref = refs[0], refs[1]
        r_ref = refs[2] if res is not None else None
        o_ref = refs[n_in]
        bv = b_ref[...]
        if bv.ndim == 3:
            bv = bv.reshape(-1, bv.shape[-1])
        part = _dot(a_ref[...], bv, dims)
        if nk == 1:
            if r_ref is not None:
                part = part + r_ref[...]
            o_ref[...] = part.astype(o_ref.dtype)
            return
        acc_ref = refs[n_in + 1]
        k = pl.program_id(2)

        @pl.when(k == 0)
        def _():
            acc_ref[...] = part

        @pl.when(k > 0)
        def _():
            acc_ref[...] += part

        @pl.when(k == nk - 1)
        def _():
            tot = acc_ref[...]
            if r_ref is not None:
                tot = tot + r_ref[...]
            o_ref[...] = tot.astype(o_ref.dtype)

    in_specs = [a_spec, b_spec]
    args = [a, b]
    if res is not None:
        in_specs.append(res_spec)
        args.append(res)
    if after is not None:
        in_specs.append(ANY)
        args.append(after)
    acc_shape = tuple(d for d in o_spec.block_shape if d is not None)
    scratch = [pltpu.VMEM(acc_shape, F32)] if nk > 1 else []
    return pl.pallas_call(
        body, name=name, grid=grid, in_specs=in_specs, out_specs=o_spec, out_shape=out_shape,
        scratch_shapes=scratch,
        compiler_params=_params(("parallel", "parallel", "arbitrary"), vmem_mb),
    )(*args)


def _rms_fwd(name, x, g, tr, after=None, transposed=False):
    rows, d = x.shape

    def body(x_ref, g_ref, *refs):
        outs = refs[1:] if after is not None else refs
        xv = x_ref[...]
        r = lax.rsqrt(jnp.mean(xv * xv, axis=-1, keepdims=True) + EPS)
        h = xv * r * g_ref[...]
        outs[0][...] = h.astype(BF16)
        if transposed:
            outs[1][...] = h.T.astype(BF16)

    out_specs = [pl.BlockSpec((tr, d), lambda i: (i, 0))]
    out_shape = [jax.ShapeDtypeStruct((rows, d), BF16)]
    if transposed:
        out_specs.append(pl.BlockSpec((d, tr), lambda i: (0, i)))
        out_shape.append(jax.ShapeDtypeStruct((d, rows), BF16))
    outs = pl.pallas_call(
        body, name=name, grid=(rows // tr,),
        in_specs=[pl.BlockSpec((tr, d), lambda i: (i, 0)), pl.BlockSpec((1, d), lambda i: (0, 0))]
        + ([] if after is None else [ANY]),
        out_specs=out_specs, out_shape=out_shape,
        compiler_params=_params(("parallel",)),
    )(x, g, *([] if after is None else [after]))
    return outs if transposed else outs[0]


def _rms_bwd(name, x, dh, dres, g, tr, after=None):
    rows, d = x.shape

    def body(x_ref, dh_ref, dres_ref, g_ref, *refs):
        dx_ref, dxb_ref, dg_ref = refs[-3:]
        xv = x_ref[...]
        r = lax.rsqrt(jnp.mean(xv * xv, axis=-1, keepdims=True) + EPS)
        xhat = xv * r
        dhv = dh_ref[...]
        dxh = dhv * g_ref[...]
        dx = r * (dxh - xhat * jnp.mean(dxh * xhat, axis=-1, keepdims=True)) + dres_ref[...]
        dx_ref[...] = dx
        dxb_ref[...] = dx.astype(BF16)
        part = jnp.sum(dhv * xhat, axis=0, keepdims=True)

        @pl.when(pl.program_id(0) == 0)
        def _():
            dg_ref[...] = part

        @pl.when(pl.program_id(0) > 0)
        def _():
            dg_ref[...] += part

    blk = pl.BlockSpec((tr, d), lambda i: (i, 0))
    vec = pl.BlockSpec((1, d), lambda i: (0, 0))
    return pl.pallas_call(
        body, name=name, grid=(rows // tr,), in_specs=[blk, blk, blk, vec] + ([] if after is None else [ANY]),
        out_specs=[blk, blk, vec],
        out_shape=[jax.ShapeDtypeStruct((rows, d), F32), jax.ShapeDtypeStruct((rows, d), BF16),
                   jax.ShapeDtypeStruct((1, d), F32)],
        compiler_params=_params(("arbitrary",)),
    )(x, dh, dres, g, *([] if after is None else [after]))


def _rms_gain_grad(name, x, dh):
    rows, d = x.shape

    def body(x_ref, dh_ref, dg_ref):
        xv = x_ref[...]
        r = lax.rsqrt(jnp.mean(xv * xv, axis=-1, keepdims=True) + EPS)
        dg_ref[...] = jnp.sum(dh_ref[...] * xv * r, axis=0, keepdims=True)

    return pl.pallas_call(
        body, name=name, out_shape=jax.ShapeDtypeStruct((1, d), F32), compiler_params=_params(None),
    )(x, dh)


def _loss_and_grad(name, y, target, tr):
    rows, d = y.shape
    n = rows // tr

    def body(y_ref, t_ref, dx_ref, dxb_ref, loss_ref, acc_ref):
        e = y_ref[...] - t_ref[...]
        dx = e * (1.0 / d)
        dx_ref[...] = dx
        dxb_ref[...] = dx.astype(BF16)
        part = jnp.sum(e * e, axis=0, keepdims=True)
        i = pl.program_id(0)

        @pl.when(i == 0)
        def _():
            acc_ref[...] = part

        @pl.when(i > 0)
        def _():
            acc_ref[...] += part

        @pl.when(i == n - 1)
        def _():
            loss_ref[...] = jnp.sum(acc_ref[...], axis=-1, keepdims=True) * (0.5 / d)

    blk = pl.BlockSpec((tr, d), lambda i: (i, 0))
    return pl.pallas_call(
        body, name=name, grid=(n,), in_specs=[blk, blk],
        out_specs=[blk, blk, pl.BlockSpec((1, 1), lambda i: (0, 0))],
        out_shape=[jax.ShapeDtypeStruct((rows, d), F32), jax.ShapeDtypeStruct((rows, d), BF16),
                   jax.ShapeDtypeStruct((1, 1), F32)],
        scratch_shapes=[pltpu.VMEM((1, d), F32)],
        compiler_params=_params(("arbitrary",)),
    )(y, target)


SB_T = 256
SB_HEADS = 2


def _sb_scores(q, kblk):
    z = _dot(q, kblk, NT) * ATT_SCALE
    e = jnp.exp(-jnp.abs(z))
    sp = jnp.log1p(e)
    lb = jnp.minimum(z, 0.0) - sp
    l1 = lb - z
    return z, e, lb, l1


def _sb_fwd(name, proj, after=None):
    s_len = proj.shape[0]
    t = SB_T
    nq = s_len // t

    def body(q_ref, k_ref, v_ref, *refs):
        o_ref = refs[-1]
        i = pl.program_id(1)
        row = lax.broadcasted_iota(jnp.int32, (t, t), 0)
        col = lax.broadcasted_iota(jnp.int32, (t, t), 1)
        causal = col < row
        after_mat = (row > col).astype(BF16)
        heads = [slice(hh * HEAD_DIM, (hh + 1) * HEAD_DIM) for hh in range(SB_HEADS)]
        q = [q_ref[:, sl].astype(BF16) for sl in heads]

        def tile(kb, state, masked):
            start = pl.multiple_of(kb * t, t)
            out = []
            for hh, sl in enumerate(heads):
                carry, acc = state[hh]
                kblk = k_ref[pl.ds(start, t), sl].astype(BF16)
                vblk = v_ref[pl.ds(start, t), sl].astype(BF16)
                _, _, lb, l1 = _sb_scores(q[hh], kblk)
                if masked:
                    l1 = jnp.where(causal, l1, 0.0)
                hi, lo = _split_bf16(l1)
                after = _dot(hi, after_mat, NN) + _dot(lo, after_mat, NN) + carry
                a = jnp.exp(lb + after)
                if masked:
                    a = jnp.where(causal, a, 0.0)
                acc = acc + _dot(a.astype(BF16), vblk, NN)
                carry = carry + jnp.sum(l1, axis=-1, keepdims=True)
                out.append((carry, acc))
            return tuple(out)

        zero = (jnp.zeros((t, 1), F32), jnp.zeros((t, HEAD_DIM), F32))
        state = tile(i, (zero,) * SB_HEADS, True)
        state = lax.fori_loop(0, i, lambda n, st: tile(i - 1 - n, st, False), state)
        for hh, sl in enumerate(heads):
            o_ref[:, sl] = state[hh][1]

    cb = SB_HEADS * HEAD_DIM
    return pl.pallas_call(
        body, name=name, grid=(HEADS // SB_HEADS, nq),
        in_specs=[pl.BlockSpec((t, cb), lambda h, i: (i, OFF_QB // cb + h)),
                  pl.BlockSpec((s_len, cb), lambda h, i: (0, OFF_KB // cb + h)),
                  pl.BlockSpec((s_len, cb), lambda h, i: (0, OFF_VB // cb + h))] + ([] if after is None else [ANY]),
        out_specs=pl.BlockSpec((t, cb), lambda h, i: (i, h)),
        out_shape=jax.ShapeDtypeStruct((s_len, D_B), F32),
        compiler_params=_params(("parallel", "arbitrary")),
    )(proj, proj, proj, *([] if after is None else [after]))


def _sb_bwd(name, proj, dy, after=None):
    s_len = proj.shape[0]
    t = SB_T
    nq = s_len // t

    def body(q_ref, k_ref, v_ref, z_ref, dy_ref, *refs):
        dq_ref, dk_ref, dv_ref, a_ref, s_ref = refs[-5:]
        i = pl.program_id(1)

        @pl.when(i == 0)
        def _():
            dk_ref[...] = jnp.zeros_like(dk_ref)
            dv_ref[...] = jnp.zeros_like(dv_ref)

        heads = [slice(hh * HEAD_DIM, (hh + 1) * HEAD_DIM) for hh in range(SB_HEADS)]
        q = [q_ref[:, sl].astype(BF16) for sl in heads]
        silu_z, _ = _silu_and_grad(z_ref[...])
        do_all = dy_ref[...] * silu_z
        do_b = [do_all[:, sl].astype(BF16) for sl in heads]
        row = lax.broadcasted_iota(jnp.int32, (t, t), 0)
        col = lax.broadcasted_iota(jnp.int32, (t, t), 1)
        causal = col < row
        after_mat = (row > col).astype(BF16)
        before_mat = (row < col).astype(BF16)

        def weights(kb, carries, masked):
            start = pl.multiple_of(kb * t, t)
            out = []
            for hh, sl in enumerate(heads):
                kblk = k_ref[pl.ds(start, t), sl].astype(BF16)
                z, _, lb, l1 = _sb_scores(q[hh], kblk)
                if masked:
                    l1 = jnp.where(causal, l1, 0.0)
                hi, lo = _split_bf16(l1)
                after = _dot(hi, after_mat, NN) + _dot(lo, after_mat, NN) + carries[hh]
                a = jnp.exp(lb + after)
                if masked:
                    a = jnp.where(causal, a, 0.0)
                a_ref[hh, kb] = a
                s_ref[hh, kb] = z
                out.append(carries[hh] + jnp.sum(l1, axis=-1, keepdims=True))
            return tuple(out)

        carries = weights(i, (jnp.zeros((t, 1), F32),) * SB_HEADS, True)
        lax.fori_loop(0, i, lambda n, c: weights(i - 1 - n, c, False), carries)

        def grads(kb, state, masked):
            start = pl.multiple_of(kb * t, t)
            out = []
            for hh, sl in enumerate(heads):
                carry, dq = state[hh]
                kblk = k_ref[pl.ds(start, t), sl].astype(BF16)
                vblk = v_ref[pl.ds(start, t), sl].astype(BF16)
                a = a_ref[hh, kb]
                z = s_ref[hh, kb]
                g = _dot(do_b[hh], vblk, NT) * a
                ghi, glo = _split_bf16(g)
                prefix = _dot(ghi, before_mat, NN) + _dot(glo, before_mat, NN) + carry
                e = jnp.exp(-jnp.abs(z))
                inv = 1.0 / (1.0 + e)
                pos = z >= 0.0
                beta = jnp.where(pos, inv, e * inv)
                one_m_beta = jnp.where(pos, e * inv, inv)
                dz = (g * one_m_beta - prefix * beta) * ATT_SCALE
                if masked:
                    dz = jnp.where(causal, dz, 0.0)
                dz_b = dz.astype(BF16)
                dq = dq + _dot(dz_b, kblk, NN)
                dk_ref[pl.ds(start, t), sl] += _dot(dz_b, q[hh], TN)
                dv_ref[pl.ds(start, t), sl] += _dot(a.astype(BF16), do_b[hh], TN)
                out.append((carry + jnp.sum(g, axis=-1, keepdims=True), dq))
            return tuple(out)

        zero = (jnp.zeros((t, 1), F32), jnp.zeros((t, HEAD_DIM), F32))
        state = lax.fori_loop(0, i, lambda kb, st: grads(kb, st, False), (zero,) * SB_HEADS)
        state = grads(i, state, True)
        for hh, sl in enumerate(heads):
            dq_ref[:, sl] = state[hh][1]

    cb = SB_HEADS * HEAD_DIM
    qblk = lambda off: pl.BlockSpec((t, cb), lambda h, i: (i, off // cb + h))
    full = lambda off: pl.BlockSpec((s_len, cb), lambda h, i: (0, off // cb + h))
    out = jax.ShapeDtypeStruct((s_len, D_B), F32)
    return pl.pallas_call(
        body, name=name, grid=(HEADS // SB_HEADS, nq),
        in_specs=[qblk(OFF_QB), full(OFF_KB), full(OFF_VB), qblk(OFF_ZB), qblk(OFF_YB)]
        + ([] if after is None else [ANY]),
        out_specs=[qblk(0), full(0), full(0)],
        out_shape=[out, out, out],
        scratch_shapes=[pltpu.VMEM((SB_HEADS, nq, t, t), F32), pltpu.VMEM((SB_HEADS, nq, t, t), F32)],
        compiler_params=_params(("parallel", "arbitrary")),
    )(proj, proj, proj, proj, dy, *([] if after is None else [after]))


MEM_TQ = 512


def _qk_norm(x, g):
    r = lax.rsqrt(jnp.mean(x * x, axis=-1, keepdims=True) + EPS)
    xhat = x * r
    return xhat * g, xhat, r


def _qk_norm_bwd(dn, g, xhat, r):
    dxh = dn * g
    return r * (dxh - xhat * jnp.mean(dxh * xhat, axis=-1, keepdims=True))


def _mem_probs(q, mk, qg, kg):
    qn, qhat, rq = _qk_norm(q, qg)
    kn, khat, rk = _qk_norm(mk, kg)
    qn_b, kn_b = qn.astype(BF16), kn.astype(BF16)
    s = _dot(qn_b, kn_b, NT) * ATT_SCALE
    p = jnp.exp(s - jnp.max(s, axis=-1, keepdims=True))
    p = p / jnp.sum(p, axis=-1, keepdims=True)
    return p, qn_b, kn_b, qhat, rq, khat, rk


def _mem_fwd(name, proj, mem_kv, qg, kg):
    s_len = proj.shape[0]
    m_len = mem_kv.shape[0]
    tq = min(MEM_TQ, s_len)

    def body(q_ref, mk_ref, mv_ref, qg_ref, kg_ref, o_ref):
        p = _mem_probs(q_ref[...], mk_ref[...], qg_ref[...], kg_ref[...])[0]
        o_ref[...] = _dot(p.astype(BF16), mv_ref[...].astype(BF16), NN)

    cb = HEAD_DIM
    vec = pl.BlockSpec((1, cb), lambda h, i: (0, 0))
    return pl.pallas_call(
        body, name=name, grid=(HEADS, s_len // tq),
        in_specs=[pl.BlockSpec((tq, cb), lambda h, i: (i, OFF_QC // cb + h)),
                  pl.BlockSpec((m_len, cb), lambda h, i: (0, h)),
                  pl.BlockSpec((m_len, cb), lambda h, i: (0, HEADS + h)), vec, vec],
        out_specs=pl.BlockSpec((tq, cb), lambda h, i: (i, h)),
        out_shape=jax.ShapeDtypeStruct((s_len, D_C), F32),
        compiler_params=_params(("parallel", "parallel")),
    )(proj, mem_kv, mem_kv, qg, kg)


def _mem_bwd(name, proj, mem_kv, qg, kg, dy):
    s_len = proj.shape[0]
    m_len = mem_kv.shape[0]
    tq = min(MEM_TQ, s_len)

    def body(q_ref, mk_ref, mv_ref, qg_ref, kg_ref, z_ref, dy_ref, dq_ref, dmk_ref, dmv_ref, dqg_ref, dkg_ref):
        h, i = pl.program_id(0), pl.program_id(1)

        @pl.when(i == 0)
        def _():
            dmk_ref[...] = jnp.zeros_like(dmk_ref)
            dmv_ref[...] = jnp.zeros_like(dmv_ref)

        @pl.when((i == 0) & (h == 0))
        def _():
            dqg_ref[...] = jnp.zeros_like(dqg_ref)
            dkg_ref[...] = jnp.zeros_like(dkg_ref)

        qg, kg = qg_ref[...], kg_ref[...]
        p, qn_b, kn_b, qhat, rq, khat, rk = _mem_probs(q_ref[...], mk_ref[...], qg, kg)
        silu_z, _ = _silu_and_grad(z_ref[...])
        do_b = (dy_ref[...] * silu_z).astype(BF16)
        dmv_ref[...] += _dot(p.astype(BF16), do_b, TN)
        dp = _dot(do_b, mv_ref[...].astype(BF16), NT)
        ds = (p * (dp - jnp.sum(dp * p, axis=-1, keepdims=True)) * ATT_SCALE).astype(BF16)
        dqn = _dot(ds, kn_b, NN)
        dkn = _dot(ds, qn_b, TN)
        dq_ref[...] = _qk_norm_bwd(dqn, qg, qhat, rq)
        dmk_ref[...] += _qk_norm_bwd(dkn, kg, khat, rk)
        dqg_ref[...] += jnp.sum(dqn * qhat, axis=0, keepdims=True)
        dkg_ref[...] += jnp.sum(dkn * khat, axis=0, keepdims=True)

    cb = HEAD_DIM
    vec = pl.BlockSpec((1, cb), lambda h, i: (0, 0))
    qblk = lambda off: pl.BlockSpec((tq, cb), lambda h, i: (i, off // cb + h))
    memblk = lambda off: pl.BlockSpec((m_len, cb), lambda h, i: (0, off + h))
    return pl.pallas_call(
        body, name=name, grid=(HEADS, s_len // tq),
        in_specs=[qblk(OFF_QC), memblk(0), memblk(HEADS), vec, vec, qblk(OFF_ZC), qblk(OFF_YC)],
        out_specs=[qblk(0), memblk(0), memblk(0), vec, vec],
        out_shape=[jax.ShapeDtypeStruct((s_len, D_C), F32), jax.ShapeDtypeStruct((m_len, D_C), F32),
                   jax.ShapeDtypeStruct((m_len, D_C), F32), jax.ShapeDtypeStruct((1, cb), F32),
                   jax.ShapeDtypeStruct((1, cb), F32)],
        compiler_params=_params(("arbitrary", "arbitrary")),
    )(proj, mem_kv, mem_kv, qg, kg, proj, dy)


def _sgu_common(u_ref, v_ref, lng_ref, lnb_ref, w_ref, bias_ref):
    ug = _gelu(u_ref[...])
    vg = _gelu(v_ref[...])
    mu = jnp.mean(vg, axis=-1, keepdims=True)
    xc = vg - mu
    rstd = lax.rsqrt(jnp.mean(xc * xc, axis=-1, keepdims=True) + EPS)
    xhat = xc * rstd
    vn = xhat * lng_ref[...] + lnb_ref[...]
    vn_b = vn.astype(BF16)
    row = lax.broadcasted_iota(jnp.int32, (CHUNK, CHUNK), 0)
    col = lax.broadcasted_iota(jnp.int32, (CHUNK, CHUNK), 1)
    tril = row >= col
    mixed = []
    for g in range(A_GROUPS):
        w = jnp.where(tril, w_ref[g], 0.0).astype(BF16)
        sl = slice(g * CHUNK, (g + 1) * CHUNK)
        mixed.append(_dot(w, vn_b[:, sl], NN) + bias_ref[:, sl])
    return ug, xhat, rstd, vn_b, mixed, tril


def _gate_fwd(name, proj, o_b, o_c, lng, lnb, w_s, bias):
    s_len = proj.shape[0]

    def body(u_ref, v_ref, za_ref, zb_ref, zc_ref, ob_ref, oc_ref, lng_ref, lnb_ref, w_ref, bias_ref, y_ref, yt_ref):
        ug, _, _, _, mixed, _ = _sgu_common(u_ref, v_ref, lng_ref, lnb_ref, w_ref, bias_ref)
        sza, _ = _silu_and_grad(za_ref[...])
        gate = ug * sza

        def put(off, width, val):
            y_ref[:, off:off + width] = val.astype(BF16)
            yt_ref[off:off + width, :] = val.T.astype(BF16)

        for g in range(A_GROUPS):
            sl = slice(g * CHUNK, (g + 1) * CHUNK)
            put(g * CHUNK, CHUNK, gate[:, sl] * mixed[g])
        szb, _ = _silu_and_grad(zb_ref[...])
        put(OFF_YB, D_B, ob_ref[...] * szb)
        szc, _ = _silu_and_grad(zc_ref[...])
        put(OFF_YC, D_C, oc_ref[...] * szc)

    wide = lambda off: pl.BlockSpec((CHUNK, D_A), lambda i: (i, off // D_A))
    narrow = lambda off: pl.BlockSpec((CHUNK, D_B), lambda i: (i, off // D_B))
    vec = pl.BlockSpec((1, D_A), lambda i: (0, 0))
    return pl.pallas_call(
        body, name=name, grid=(s_len // CHUNK,),
        in_specs=[wide(OFF_U), wide(OFF_V), wide(OFF_ZA), narrow(OFF_ZB), narrow(OFF_ZC), narrow(0), narrow(0), vec, vec,
                  pl.BlockSpec((A_GROUPS, CHUNK, CHUNK), lambda i: (0, 0, 0)),
                  pl.BlockSpec((CHUNK, D_A), lambda i: (0, 0))],
        out_specs=[pl.BlockSpec((CHUNK, D_MODEL), lambda i: (i, 0)), pl.BlockSpec((D_MODEL, CHUNK), lambda i: (0, i))],
        out_shape=[jax.ShapeDtypeStruct((s_len, D_MODEL), BF16), jax.ShapeDtypeStruct((D_MODEL, s_len), BF16)],
        compiler_params=_params(("parallel",)),
    )(proj, proj, proj, proj, proj, o_b, o_c, lng, lnb, w_s, bias)


def _gate_bwd(name, proj, dy, o_b, o_c, dqkv, dq_c, lng, lnb, w_s, w_s_t, bias):
    s_len = proj.shape[0]
    n = s_len // CHUNK
    dq_b, dk_b, dv_b = dqkv

    def body(u_ref, v_ref, za_ref, zb_ref, zc_ref, dya_ref, dyb_ref, dyc_ref, ob_ref, oc_ref, dq_ref, dk_ref, dv_ref,
             dqc_ref, lng_ref, lnb_ref, w_ref, wt_ref, bias_ref, dp_ref, dw_ref, dsb_ref, dlng_ref, dlnb_ref, dbias_ref):
        i = pl.program_id(0)

        @pl.when(i == 0)
        def _():
            dw_ref[...] = jnp.zeros_like(dw_ref)
            dbias_ref[...] = jnp.zeros_like(dbias_ref)
            dlng_ref[...] = jnp.zeros_like(dlng_ref)
            dlnb_ref[...] = jnp.zeros_like(dlnb_ref)

        ug, xhat, rstd, vn_b, mixed, tril = _sgu_common(u_ref, v_ref, lng_ref, lnb_ref, w_ref, bias_ref)
        za = za_ref[...]
        sza, dsza = _silu_and_grad(za)
        dya = dya_ref[...]
        mixed_all = jnp.concatenate(mixed, axis=-1)
        d_mixed = dya * ug * sza
        dp_ref[:, OFF_U:OFF_U + D_A] = (dya * mixed_all * sza * _gelu_grad(u_ref[...])).astype(BF16)
        dp_ref[:, OFF_ZA:OFF_ZA + D_A] = (dya * ug * mixed_all * dsza).astype(BF16)
        dbias_ref[...] += d_mixed
        dm_b = d_mixed.astype(BF16)
        triu = lax.broadcasted_iota(jnp.int32, (CHUNK, CHUNK), 0) <= lax.broadcasted_iota(jnp.int32, (CHUNK, CHUNK), 1)
        d_vn = []
        for g in range(A_GROUPS):
            sl = slice(g * CHUNK, (g + 1) * CHUNK)
            wt = jnp.where(triu, wt_ref[g], 0.0).astype(BF16)
            d_vn.append(_dot(wt, dm_b[:, sl], NN))
            dw_ref[g] += jnp.where(tril, _dot(dm_b[:, sl], vn_b[:, sl], NT), 0.0)
        d_vn = jnp.concatenate(d_vn, axis=-1)
        dlng_ref[...] += jnp.sum(d_vn * xhat, axis=0, keepdims=True)
        dlnb_ref[...] += jnp.sum(d_vn, axis=0, keepdims=True)
        dxh = d_vn * lng_ref[...]
        d_vg = rstd * (dxh - jnp.mean(dxh, axis=-1, keepdims=True)
                       - xhat * jnp.mean(dxh * xhat, axis=-1, keepdims=True))
        dp_ref[:, OFF_V:OFF_V + D_A] = (d_vg * _gelu_grad(v_ref[...])).astype(BF16)
        dp_ref[:, OFF_QB:OFF_QB + D_B] = dq_ref[...].astype(BF16)
        dp_ref[:, OFF_KB:OFF_KB + D_B] = dk_ref[...].astype(BF16)
        dp_ref[:, OFF_VB:OFF_VB + D_B] = dv_ref[...].astype(BF16)
        _, dszb = _silu_and_grad(zb_ref[...])
        dp_ref[:, OFF_ZB:OFF_ZB + D_B] = (dyb_ref[...] * ob_ref[...] * dszb).astype(BF16)
        dp_ref[:, OFF_QC:OFF_QC + D_C] = dqc_ref[...].astype(BF16)
        _, dszc = _silu_and_grad(zc_ref[...])
        dp_ref[:, OFF_ZC:OFF_ZC + D_C] = (dyc_ref[...] * oc_ref[...] * dszc).astype(BF16)

        @pl.when(i == n - 1)
        def _():
            ch = lax.broadcasted_iota(jnp.int32, (D_A, CHUNK), 0)
            gcol = lax.broadcasted_iota(jnp.int32, (D_A, CHUNK), 1)
            pick = (ch // (D_A // A_GROUPS) == gcol).astype(BF16)
            rest = dbias_ref[...]
            tot = jnp.zeros((CHUNK, CHUNK), F32)
            for _ in range(3):
                term = rest.astype(BF16)
                tot = tot + _dot(term, pick, NN)
                rest = rest - term.astype(F32)
            dsb_ref[...] = tot

    wide = lambda off: pl.BlockSpec((CHUNK, D_A), lambda i: (i, off // D_A))
    narrow = lambda off: pl.BlockSpec((CHUNK, D_B), lambda i: (i, off // D_B))
    vec = pl.BlockSpec((1, D_A), lambda i: (0, 0))
    wspec = pl.BlockSpec((A_GROUPS, CHUNK, CHUNK), lambda i: (0, 0, 0))
    bspec = pl.BlockSpec((CHUNK, D_A), lambda i: (0, 0))
    return pl.pallas_call(
        body, name=name, grid=(n,),
        in_specs=[wide(OFF_U), wide(OFF_V), wide(OFF_ZA), narrow(OFF_ZB), narrow(OFF_ZC),
                  wide(0), narrow(OFF_YB), narrow(OFF_YC), narrow(0), narrow(0), narrow(0), narrow(0), narrow(0),
                  narrow(0), vec, vec, wspec, wspec, bspec],
        out_specs=[pl.BlockSpec((CHUNK, IN_WIDTH), lambda i: (i, 0)), wspec,
                   pl.BlockSpec((CHUNK, CHUNK), lambda i: (0, 0)), vec, vec],
        out_shape=[jax.ShapeDtypeStruct((s_len, IN_WIDTH), BF16), jax.ShapeDtypeStruct((A_GROUPS, CHUNK, CHUNK), F32),
                   jax.ShapeDtypeStruct((CHUNK, CHUNK), F32), jax.ShapeDtypeStruct((1, D_A), F32),
                   jax.ShapeDtypeStruct((1, D_A), F32)],
        scratch_shapes=[pltpu.VMEM((CHUNK, D_A), F32)],
        compiler_params=_params(("arbitrary",)),
    )(proj, proj, proj, proj, proj, dy, dy, dy, o_b, o_c, dq_b, dk_b, dv_b, dq_c, lng, lnb, w_s, w_s_t, bias)


IN_SHARD = IN_WIDTH // N_CHIPS
ROW_SHARD = D_MODEL // N_CHIPS


def _bias_rows(sgu_b_l):
    return jnp.repeat(sgu_b_l.T, D_A // A_GROUPS, axis=1)


class _WholeWeights:
    def __init__(self, w_in_all, w_kv_all, w_out_all):
        self.weights = (w_in_all, w_kv_all, w_out_all)

    def w_in(self, h):
        return self.weights[0]

    def rest_start(self, proj):
        return None

    def rest_finish(self, o_b):
        return self.weights[1], self.weights[2], None

    def before_out(self, y):
        return None


def _layer_fwd(l, x, mem, sm, hooks):
    s_len = x.shape[0]
    m_len = mem.shape[0]
    tm = min(1024, s_len)
    tn = 768
    per = IN_SHARD // tn
    h, h_t = _rms_fwd(f"rms_fwd_{l}", x, sm["norm_g"][l][None], min(256, s_len), transposed=True)
    w_in_all = hooks.w_in(h)
    proj = _matmul(
        f"in_proj_{l}", h, w_in_all, grid=(s_len // tm, IN_WIDTH // tn, 1),
        a_spec=pl.BlockSpec((tm, D_MODEL), lambda i, j, k: (i, 0)),
        b_spec=pl.BlockSpec((None, D_MODEL, tn), lambda i, j, k: (j // per, 0, j % per)),
        o_spec=pl.BlockSpec((tm, tn), lambda i, j, k: (i, j)),
        out_shape=jax.ShapeDtypeStruct((s_len, IN_WIDTH), F32), dims=NN)
    o_b = _sb_fwd(f"sb_fwd_{l}", proj, hooks.rest_start(proj))
    w_kv_all, w_out_all, after = hooks.rest_finish(o_b)
    mem_h = _rms_fwd(f"mem_rms_fwd_{l}", mem, sm["mem_norm_g"][l][None], m_len, after)
    mem_kv = _matmul(
        f"mem_kv_{l}", mem_h, w_kv_all, grid=(1, 2, N_CHIPS),
        a_spec=pl.BlockSpec((m_len, ROW_SHARD), lambda i, j, k: (0, k)),
        b_spec=pl.BlockSpec((None, ROW_SHARD, D_C), lambda i, j, k: (k, 0, j)),
        o_spec=pl.BlockSpec((m_len, D_C), lambda i, j, k: (0, j)),
        out_shape=jax.ShapeDtypeStruct((m_len, 2 * D_C), F32), dims=NN)
    qg, kg = sm["q_norm_g"][l][None], sm["k_norm_g"][l][None]
    o_c = _mem_fwd(f"mem_fwd_{l}", proj, mem_kv, qg, kg)
    bias = _bias_rows(sm["sgu_b"][l])
    y, y_t = _gate_fwd(f"gate_fwd_{l}", proj, o_b, o_c, sm["sgu_ln_g"][l][None], sm["sgu_ln_b"][l][None],
                       sm["sgu_w"][l], bias)
    tn_o = 512
    x_next = _matmul(
        f"out_proj_{l}", y, w_out_all, grid=(s_len // tm, D_MODEL // tn_o, 1),
        a_spec=pl.BlockSpec((tm, D_MODEL), lambda i, j, k: (i, 0)),
        b_spec=pl.BlockSpec((N_CHIPS, ROW_SHARD, tn_o), lambda i, j, k: (0, 0, j)),
        o_spec=pl.BlockSpec((tm, tn_o), lambda i, j, k: (i, j)),
        out_shape=jax.ShapeDtypeStruct((s_len, D_MODEL), F32), dims=NN,
        res=x, res_spec=pl.BlockSpec((tm, tn_o), lambda i, j, k: (i, j)), after=hooks.before_out(y))
    saved = dict(x=x, h_t=h_t, proj=proj, mem_h=mem_h, mem_kv=mem_kv, o_b=o_b, o_c=o_c, y_t=y_t, bias=bias,
                 weights=(w_in_all, w_kv_all, w_out_all))
    return x_next, saved


def _layer_bwd(l, dxo, dxo_b, mem, sm, saved, on_weight_grads=None):
    s_len = dxo.shape[0]
    m_len = mem.shape[0]
    proj, y_t, h_t, mem_h, mem_kv = saved["proj"], saved["y_t"], saved["h_t"], saved["mem_h"], saved["mem_kv"]
    w_in_all, w_kv_all, w_out_all = saved["weights"]
    tm = min(1024, s_len)
    tk = s_len
    g_out = _matmul(
        f"d_w_out_{l}", y_t, dxo_b, grid=(N_CHIPS, D_MODEL // 1024, s_len // tk),
        a_spec=pl.BlockSpec((ROW_SHARD, tk), lambda i, j, k: (i, k)),
        b_spec=pl.BlockSpec((tk, 1024), lambda i, j, k: (k, j)),
        o_spec=pl.BlockSpec((None, ROW_SHARD, 1024), lambda i, j, k: (i, 0, j)),
        out_shape=jax.ShapeDtypeStruct((N_CHIPS, ROW_SHARD, D_MODEL), F32), dims=NN)
    token, finish = (None, None) if on_weight_grads is None else on_weight_grads("out", [g_out])
    dy = _matmul(
        f"d_y_{l}", dxo_b, w_out_all, grid=(s_len // tm, N_CHIPS, 1),
        a_spec=pl.BlockSpec((tm, D_MODEL), lambda i, j, k: (i, 0)),
        b_spec=pl.BlockSpec((None, ROW_SHARD, D_MODEL), lambda i, j, k: (j, 0, 0)),
        o_spec=pl.BlockSpec((tm, ROW_SHARD), lambda i, j, k: (i, j)),
        out_shape=jax.ShapeDtypeStruct((s_len, D_MODEL), F32), dims=NT, after=token)
    token = None if finish is None else finish(dy)
    qg, kg = sm["q_norm_g"][l][None], sm["k_norm_g"][l][None]
    dqkv = _sb_bwd(f"sb_bwd_{l}", proj, dy, token)
    dq_c, dmk, dmv, dqg, dkg = _mem_bwd(f"mem_bwd_{l}", proj, mem_kv, qg, kg, dy)
    w_s = sm["sgu_w"][l]
    dproj, dws, dbias, dlng, dlnb = _gate_bwd(
        f"gate_bwd_{l}", proj, dy, saved["o_b"], saved["o_c"], dqkv, dq_c, sm["sgu_ln_g"][l][None],
        sm["sgu_ln_b"][l][None], w_s, jnp.swapaxes(w_s, 1, 2), saved["bias"])
    tn = 768
    per = IN_SHARD // tn
    g_in = _matmul(
        f"d_w_in_{l}", h_t, dproj, grid=(D_MODEL // 1024, IN_WIDTH // tn, s_len // tk),
        a_spec=pl.BlockSpec((1024, tk), lambda i, j, k: (i, k)),
        b_spec=pl.BlockSpec((tk, tn), lambda i, j, k: (k, j)),
        o_spec=pl.BlockSpec((None, 1024, tn), lambda i, j, k: (j // per, i, j % per)),
        out_shape=jax.ShapeDtypeStruct((N_CHIPS, D_MODEL, IN_SHARD), F32), dims=NN)
    dkv_b = jnp.concatenate([dmk, dmv], axis=1).astype(BF16)
    g_kv = _matmul(
        f"d_w_kv_{l}", mem_h, dkv_b, grid=(N_CHIPS, 1, 1),
        a_spec=pl.BlockSpec((m_len, ROW_SHARD), lambda i, j, k: (0, i)),
        b_spec=pl.BlockSpec((m_len, 2 * D_C), lambda i, j, k: (0, 0)),
        o_spec=pl.BlockSpec((None, ROW_SHARD, 2 * D_C), lambda i, j, k: (i, 0, 0)),
        out_shape=jax.ShapeDtypeStruct((N_CHIPS, ROW_SHARD, 2 * D_C), F32), dims=TN)
    token, finish = (None, None) if on_weight_grads is None else on_weight_grads("in", [g_in, g_kv])
    dh = _matmul(
        f"d_h_{l}", dproj, w_in_all, grid=(s_len // tm, D_MODEL // 1024, N_CHIPS),
        a_spec=pl.BlockSpec((tm, IN_SHARD), lambda i, j, k: (i, k)),
        b_spec=pl.BlockSpec((None, 1024, IN_SHARD), lambda i, j, k: (k, j, 0)),
        o_spec=pl.BlockSpec((tm, 1024), lambda i, j, k: (i, j)),
        out_shape=jax.ShapeDtypeStruct((s_len, D_MODEL), F32), dims=NT, after=token)
    token = None if finish is None else finish(dh)
    dx, dx_b, dng = _rms_bwd(f"rms_bwd_{l}", saved["x"], dh, dxo, sm["norm_g"][l][None], min(256, s_len), token)
    d_mem_h = _matmul(
        f"d_mem_h_{l}", dkv_b, w_kv_all, grid=(1, N_CHIPS, 1),
        a_spec=pl.BlockSpec((m_len, 2 * D_C), lambda i, j, k: (0, 0)),
        b_spec=pl.BlockSpec((None, ROW_SHARD, 2 * D_C), lambda i, j, k: (j, 0, 0)),
        o_spec=pl.BlockSpec((m_len, ROW_SHARD), lambda i, j, k: (0, j)),
        out_shape=jax.ShapeDtypeStruct((m_len, D_MODEL), F32), dims=NT)
    dmng = _rms_gain_grad(f"mem_rms_bwd_{l}", mem, d_mem_h)
    dsgu_b = dbias[:, :A_GROUPS].T
    small = dict(norm_g=dng[0], sgu_ln_g=dlng[0], sgu_ln_b=dlnb[0], sgu_w=dws, sgu_b=dsgu_b, mem_norm_g=dmng[0],
                 q_norm_g=dqg[0], k_norm_g=dkg[0])
    return dx, dx_b, small, g_in, g_kv, g_out


SMALL_NAMES = ("norm_g", "sgu_ln_g", "sgu_ln_b", "sgu_w", "sgu_b", "mem_norm_g", "q_norm_g", "k_norm_g")


def _local_step(x, mem, target, sm, w_all):
    saved = []
    cur = x
    for l in range(DEPTH):
        cur, sv = _layer_fwd(l, cur, mem, sm, _WholeWeights(*w_all[l]))
        saved.append(sv)
    dxo, dxo_b, loss = _loss_and_grad("loss", cur, target, min(256, x.shape[0]))
    small, big = [None] * DEPTH, [None] * DEPTH
    for l in reversed(range(DEPTH)):
        dxo, dxo_b, small[l], *big[l] = _layer_bwd(l, dxo, dxo_b, mem, sm, saved[l])
    small = {k: jnp.stack([small[l][k] for l in range(DEPTH)]) for k in SMALL_NAMES}
    return loss, dxo, small, big


def _place():
    x, y, c = lax.axis_index("x"), lax.axis_index("y"), lax.axis_index("c")
    return x, y, c


def _other_chips(x, y):
    return [(1 - x, y, 2 * (1 - x) + y), (x, 1 - y, 2 * x + 1 - y), (1 - x, 1 - y, 2 * (1 - x) + 1 - y)]


AG_CHUNKS = 4
D2D_CHUNKS = 8


def _place_index():
    return jnp.stack([2 * lax.axis_index("x") + lax.axis_index("y"), lax.axis_index("c")]).astype(jnp.int32)


def _cast_into_slot(name, w, l, place):
    _, rows, cols = w.shape
    tr = min(256, rows)

    def body(p_ref, w_ref, o_ref):
        o_ref[...] = w_ref[...].astype(BF16)

    return pl.pallas_call(
        body, name=name,
        grid_spec=pltpu.PrefetchScalarGridSpec(
            num_scalar_prefetch=1, grid=(rows // tr,),
            in_specs=[pl.BlockSpec((None, tr, cols), lambda i, p: (l, i, 0))],
            out_specs=pl.BlockSpec((None, tr, cols), lambda i, p: (p[0], i, 0))),
        out_shape=jax.ShapeDtypeStruct((N_CHIPS, rows, cols), BF16),
        compiler_params=_params(("parallel",)),
    )(place, w)


HBM = pl.BlockSpec(memory_space=pltpu.HBM)
SEM = pl.BlockSpec(memory_space=pltpu.SEMAPHORE)
DATAFLOW = pltpu.SideEffectType.DATAFLOW_SIDE_EFFECTING


def _in_hbm(a):
    return pltpu.with_memory_space_constraint(a, pltpu.HBM)


def _chip_copies_start(name, srcs, lands, make_copy, after=None):
    n_t = len(srcs)
    in_place = lands is None
    n_after = 0 if after is None else 1

    def body(*refs):
        src = refs[:n_t]
        k = (n_t if in_place else 2 * n_t) + n_after
        send_sems, recv_sems = refs[k], refs[k + 1]
        land = refs[k + 2:k + 2 + n_t] if in_place else refs[k + 2 + n_t:k + 2 + 2 * n_t]
        token = refs[-1]
        x, y, c = _place()
        me = 2 * x + y
        for t in range(n_t):
            for px, py, pk in _other_chips(x, y):
                s, d = make_copy(src[t], land[t], me, pk, c)
                pltpu.make_async_remote_copy(
                    src_ref=s, dst_ref=d, send_sem=send_sems.at[t], recv_sem=recv_sems.at[t],
                    device_id=(px, py, c), device_id_type=MESH).start()
        token[...] = jnp.zeros_like(token)

    bufs = list(srcs) if in_place else list(srcs) + list(lands)
    outs = pl.pallas_call(
        body, name=name,
        in_specs=[HBM] * len(bufs) + [ANY] * n_after,
        out_specs=[SEM, SEM] + [HBM] * len(bufs) + [pl.BlockSpec(memory_space=pltpu.VMEM)],
        out_shape=[pltpu.SemaphoreType.DMA((n_t,)), pltpu.SemaphoreType.DMA((n_t,))]
        + [pltpu.HBM(b.shape, b.dtype) for b in bufs] + [jax.ShapeDtypeStruct((8, 128), F32)],
        input_output_aliases={i: 2 + i for i in range(len(bufs))},
        compiler_params=pltpu.CompilerParams(has_side_effects=DATAFLOW),
    )(*[_in_hbm(b) for b in bufs], *([] if after is None else [after]))
    return outs[0], outs[1], list(outs[2:2 + len(bufs)]), outs[-1]


def _chip_copies_wait(name, send_sems, recv_sems, bufs, sent, landed, after):
    n_b = len(bufs)

    def body(*refs):
        buf = refs[:n_b]
        send_ref, recv_ref = refs[n_b], refs[n_b + 1]
        x, y, c = _place()
        for t, (s, d) in enumerate(zip(sent(buf), landed(buf))):
            out = pltpu.make_async_remote_copy(src_ref=s, dst_ref=s, send_sem=send_ref.at[t], recv_sem=recv_ref.at[t],
                                               device_id=(x, y, c), device_id_type=MESH)
            out.wait_send()
            arrived = pltpu.make_async_remote_copy(src_ref=d, dst_ref=d, send_sem=send_ref.at[t],
                                                   recv_sem=recv_ref.at[t], device_id=(x, y, c), device_id_type=MESH)
            arrived.wait_recv()

    after = list(after) if isinstance(after, (list, tuple)) else [after]
    return pl.pallas_call(
        body, name=name,
        in_specs=[HBM] * n_b + [SEM, SEM] + [ANY] * len(after), out_specs=[HBM] * n_b,
        out_shape=[pltpu.HBM(b.shape, b.dtype) for b in bufs],
        input_output_aliases={i: i for i in range(n_b)},
        compiler_params=pltpu.CompilerParams(has_side_effects=DATAFLOW),
    )(*bufs, send_sems, recv_sems, *after)


def _gather_start(name, bufs, after=None):
    def make_copy(src, land, me, pk, c):
        hr = src.shape[1] // 2
        return src.at[me, pl.ds(c * hr, hr)], land.at[me, pl.ds(c * hr, hr)]

    return _chip_copies_start(name, bufs, None, make_copy, after)


def _gather_wait(name, send_sems, recv_sems, bufs, after):
    def three_halves(buf):
        return [b.at[pl.ds(0, 3), pl.ds(0, b.shape[1] // 2)] for b in buf]

    return _chip_copies_wait(name, send_sems, recv_sems, bufs, three_halves, three_halves, after)


def _gather_forward_start(name, bufs):
    n_t = len(bufs)

    def body(*refs):
        mine = refs[:n_t]
        send_sems, recv_sems = refs[n_t], refs[n_t + 1]
        buf = refs[n_t + 2:2 * n_t + 2]
        token = refs[-1]
        x, y, c = _place()
        for q in range(D2D_CHUNKS):
            for t in range(n_t):
                hr = mine[t].shape[1] // 2
                cr = hr // D2D_CHUNKS
                rows = pl.ds(c * hr + q * cr, cr)
                for _, _, pk in _other_chips(x, y):
                    pltpu.make_async_remote_copy(
                        src_ref=mine[t].at[pk, rows], dst_ref=buf[t].at[pk, rows], send_sem=send_sems.at[t],
                        recv_sem=recv_sems.at[t], device_id=(x, y, 1 - c), device_id_type=MESH).start()
        token[...] = jnp.zeros_like(token)

    outs = pl.pallas_call(
        body, name=name,
        in_specs=[HBM] * n_t,
        out_specs=[SEM, SEM] + [HBM] * n_t + [pl.BlockSpec(memory_space=pltpu.VMEM)],
        out_shape=[pltpu.SemaphoreType.DMA((n_t,)), pltpu.SemaphoreType.DMA((n_t,))]
        + [pltpu.HBM(b.shape, b.dtype) for b in bufs] + [jax.ShapeDtypeStruct((8, 128), F32)],
        input_output_aliases={i: 2 + i for i in range(n_t)},
        compiler_params=pltpu.CompilerParams(has_side_effects=DATAFLOW),
    )(*[_in_hbm(b) for b in bufs])
    return outs[0], outs[1], list(outs[2:2 + n_t]), outs[-1]


def _core_exchange_start(name, grads):
    n_t = len(grads)
    lands = [lax.empty((g.shape[0], g.shape[1] // 2, g.shape[2]), g.dtype) for g in grads]

    def body(*refs):
        src = refs[:n_t]
        send_sems, recv_sems = refs[2 * n_t], refs[2 * n_t + 1]
        land = refs[2 * n_t + 2 + n_t:2 * n_t + 2 + 2 * n_t]
        token = refs[-1]
        x, y, c = _place()
        for q in range(D2D_CHUNKS):
            for t in range(n_t):
                hr = src[t].shape[1] // 2
                cr = hr // D2D_CHUNKS
                pltpu.make_async_remote_copy(
                    src_ref=src[t].at[:, pl.ds((1 - c) * hr + q * cr, cr)], dst_ref=land[t].at[:, pl.ds(q * cr, cr)],
                    send_sem=send_sems.at[t], recv_sem=recv_sems.at[t], device_id=(x, y, 1 - c),
                    device_id_type=MESH).start()
        token[...] = jnp.zeros_like(token)

    bufs = list(grads) + lands
    outs = pl.pallas_call(
        body, name=name,
        in_specs=[HBM] * len(bufs),
        out_specs=[SEM, SEM] + [HBM] * len(bufs) + [pl.BlockSpec(memory_space=pltpu.VMEM)],
        out_shape=[pltpu.SemaphoreType.DMA((n_t,)), pltpu.SemaphoreType.DMA((n_t,))]
        + [pltpu.HBM(b.shape, b.dtype) for b in bufs] + [jax.ShapeDtypeStruct((8, 128), F32)],
        input_output_aliases={i: 2 + i for i in range(len(bufs))},
        compiler_params=pltpu.CompilerParams(has_side_effects=DATAFLOW),
    )(*[_in_hbm(b) for b in bufs])
    return outs[0], outs[1], list(outs[2:2 + len(bufs)]), outs[-1]


def _core_exchange_wait(name, send_sems, recv_sems, bufs, after):
    n_t = len(bufs) // 2

    def body(*refs):
        land = refs[n_t:2 * n_t]
        send_ref, recv_ref = refs[2 * n_t], refs[2 * n_t + 1]
        x, y, c = _place()
        for t in range(n_t):
            whole = pltpu.make_async_remote_copy(src_ref=land[t], dst_ref=land[t], send_sem=send_ref.at[t],
                                                 recv_sem=recv_ref.at[t], device_id=(x, y, c), device_id_type=MESH)
            whole.wait_send()
            whole.wait_recv()

    outs = pl.pallas_call(
        body, name=name,
        in_specs=[HBM] * (2 * n_t) + [SEM, SEM, ANY], out_specs=[HBM] * (2 * n_t),
        out_shape=[pltpu.HBM(b.shape, b.dtype) for b in bufs],
        input_output_aliases={i: i for i in range(2 * n_t)},
        compiler_params=pltpu.CompilerParams(has_side_effects=DATAFLOW),
    )(*bufs, send_sems, recv_sems, after)
    return list(outs[:n_t]), list(outs[n_t:])


def _add_to_bf16(name, full, theirs, place):
    chips, rows, cols = theirs.shape
    tr = min(256, rows)
    per = rows // tr

    def body(p_ref, a_ref, b_ref, o_ref):
        o_ref[...] = (a_ref[...] + b_ref[...]).astype(BF16)

    blk = pl.BlockSpec((None, tr, cols), lambda k, i, p: (k, i, 0))
    return pl.pallas_call(
        body, name=name,
        grid_spec=pltpu.PrefetchScalarGridSpec(
            num_scalar_prefetch=1, grid=(chips, per),
            in_specs=[pl.BlockSpec((None, tr, cols), lambda k, i, p: (k, p[1] * per + i, 0)), blk],
            out_specs=blk),
        out_shape=jax.ShapeDtypeStruct(theirs.shape, BF16), compiler_params=_params(("parallel",) * 2),
    )(place, full, theirs)


def _chip_exchange_start(name, parts):
    lands = [lax.empty(p.shape, p.dtype) for p in parts]
    return _chip_copies_start(name, parts, lands, lambda src, land, me, pk, c: (src.at[pk], land.at[me]))


def _chip_exchange_wait(name, send_sems, recv_sems, bufs, after):
    n_t = len(bufs) // 2
    return _chip_copies_wait(name, send_sems, recv_sems, bufs,
                             lambda buf: [b.at[pl.ds(0, 3)] for b in buf[:n_t]],
                             lambda buf: [b.at[pl.ds(0, 3)] for b in buf[n_t:]], after)


def _sum_chips(name, parts, landed, place, l, stacked):
    chips, rows, cols = landed.shape
    tr = min(256, rows)
    per = rows // tr

    def body(p_ref, own_ref, *refs):
        land, o_ref = refs[:chips], refs[-1]
        tot = None
        for k in range(chips):
            term = jnp.where(p_ref[0] == k, own_ref[...], land[k][...]).astype(F32)
            tot = term if tot is None else tot + term
        o_ref[...] = tot

    def from_chip(k):
        return pl.BlockSpec((None, tr, cols), lambda i, p: (jnp.where(p[0] == k, (k + 1) % chips, k), i, 0))

    in_specs = [pl.BlockSpec((None, tr, cols), lambda i, p: (p[0], i, 0))] + [from_chip(k) for k in range(chips)]
    args = [parts] + [landed] * chips
    aliases = {}
    if stacked is not None:
        in_specs.append(ANY)
        args.append(stacked)
        aliases = {len(args): 0}
    return pl.pallas_call(
        body, name=name,
        grid_spec=pltpu.PrefetchScalarGridSpec(
            num_scalar_prefetch=1, grid=(per,), in_specs=in_specs,
            out_specs=pl.BlockSpec((None, tr, cols), lambda i, p: (l, p[1] * per + i, 0))),
        out_shape=jax.ShapeDtypeStruct((DEPTH, 2 * rows, cols), F32), input_output_aliases=aliases,
        compiler_params=_params(("parallel",)),
    )(place, *args)


def _core_share_start(name, bufs, l):
    n_t = len(bufs)

    def body(*refs):
        mine = refs[:n_t]
        send_sems, recv_sems = refs[n_t], refs[n_t + 1]
        buf = refs[n_t + 2:2 * n_t + 2]
        token = refs[-1]
        x, y, c = _place()
        for q in range(D2D_CHUNKS):
            for t in range(n_t):
                hr = mine[t].shape[1] // 2
                cr = hr // D2D_CHUNKS
                rows = pl.ds(c * hr + q * cr, cr)
                pltpu.make_async_remote_copy(
                    src_ref=mine[t].at[l, rows], dst_ref=buf[t].at[l, rows], send_sem=send_sems.at[t],
                    recv_sem=recv_sems.at[t], device_id=(x, y, 1 - c), device_id_type=MESH).start()
        token[...] = jnp.zeros_like(token)

    outs = pl.pallas_call(
        body, name=name,
        in_specs=[HBM] * n_t,
        out_specs=[SEM, SEM] + [HBM] * n_t + [pl.BlockSpec(memory_space=pltpu.VMEM)],
        out_shape=[pltpu.SemaphoreType.DMA((n_t,)), pltpu.SemaphoreType.DMA((n_t,))]
        + [pltpu.HBM(b.shape, b.dtype) for b in bufs] + [jax.ShapeDtypeStruct((8, 128), F32)],
        input_output_aliases={i: 2 + i for i in range(n_t)},
        compiler_params=pltpu.CompilerParams(has_side_effects=DATAFLOW),
    )(*[_in_hbm(b) for b in bufs])
    return outs[0], outs[1], list(outs[2:2 + n_t]), outs[-1]


def _core_share_wait(name, send_sems, recv_sems, bufs, l, after):
    def half_layer(buf):
        return [b.at[l, pl.ds(0, b.shape[1] // 2)] for b in buf]

    return _chip_copies_wait(name, send_sems, recv_sems, bufs, half_layer, half_layer, after)


def _all_reduce_small(vec, after=None):
    rows, lanes = vec.shape
    hr = rows // 2

    def body(v_ref, *refs):
        o_ref, sib_ref, chips_ref, send_sems, recv_sems = refs[-5:]
        x, y, c = _place()
        me = 2 * x + y
        sibling = (x, y, 1 - c)
        mine = pl.ds(pl.multiple_of(c * hr, 8), hr)
        theirs = pl.ds(pl.multiple_of((1 - c) * hr, 8), hr)
        swap = pltpu.make_async_remote_copy(
            src_ref=v_ref.at[theirs], dst_ref=sib_ref, send_sem=send_sems.at[0], recv_sem=recv_sems.at[0],
            device_id=sibling, device_id_type=MESH)
        swap.start()
        swap.wait_recv()
        chips_ref[me] = v_ref[mine] + sib_ref[...]
        copies = []
        for j, (px, py, pk) in enumerate(_other_chips(x, y)):
            cp = pltpu.make_async_remote_copy(
                src_ref=chips_ref.at[me], dst_ref=chips_ref.at[me], send_sem=send_sems.at[1 + j],
                recv_sem=recv_sems.at[1 + j], device_id=(px, py, c), device_id_type=MESH)
            cp.start()
            copies.append(cp)
        for j, (px, py, pk) in enumerate(_other_chips(x, y)):
            pltpu.make_async_remote_copy(
                src_ref=chips_ref.at[pk], dst_ref=chips_ref.at[pk], send_sem=send_sems.at[1 + j],
                recv_sem=recv_sems.at[1 + j], device_id=(px, py, c), device_id_type=MESH).wait_recv()
        tot = chips_ref[0]
        for k in range(1, N_CHIPS):
            tot = tot + chips_ref[k]
        o_ref[mine] = tot
        share = pltpu.make_async_remote_copy(
            src_ref=o_ref.at[mine], dst_ref=o_ref.at[mine], send_sem=send_sems.at[4], recv_sem=recv_sems.at[4],
            device_id=sibling, device_id_type=MESH)
        share.start()
        pltpu.make_async_remote_copy(
            src_ref=o_ref.at[theirs], dst_ref=o_ref.at[theirs], send_sem=send_sems.at[4], recv_sem=recv_sems.at[4],
            device_id=sibling, device_id_type=MESH).wait_recv()
        swap.wait_send()
        for cp in copies:
            cp.wait_send()
        share.wait_send()

    vm = pl.BlockSpec(memory_space=pltpu.VMEM)
    return pl.pallas_call(
        body, name="small_all_reduce", in_specs=[vm] + ([] if after is None else [ANY]), out_specs=vm,
        out_shape=jax.ShapeDtypeStruct((rows, lanes), F32),
        scratch_shapes=[pltpu.VMEM((hr, lanes), F32), pltpu.VMEM((N_CHIPS, hr, lanes), F32),
                        pltpu.SemaphoreType.DMA((5,)), pltpu.SemaphoreType.DMA((5,))],
        compiler_params=pltpu.CompilerParams(has_side_effects=True, vmem_limit_bytes=48 * MIB),
    )(vec, *([] if after is None else [after]))


def _adamw(name, w, g, m, v, place, l=0, half=None, done=None, after=None):
    layers, rows, cols = w.shape
    span = rows if half is None else rows // 2
    tr = span
    for cand in (256, 128, 64, 32, 16, 8):
        if span % cand == 0:
            tr = cand
            break
    per = span // tr
    c1 = 1.0 - ADAM_B1 ** ADAM_STEP
    c2 = 1.0 - ADAM_B2 ** ADAM_STEP

    def first_block(p):
        return 0 if half is None else (p[1] if half == "own" else 1 - p[1]) * per

    def body(p_ref, w_ref, g_ref, m_ref, v_ref, *refs):
        go_ref, d_ref, nm_ref, nv_ref = refs[-4:]
        gv = g_ref[...]
        nm = ADAM_B1 * m_ref[...] + (1.0 - ADAM_B1) * gv
        nv = ADAM_B2 * v_ref[...] + (1.0 - ADAM_B2) * (gv * gv)
        go_ref[...] = gv
        nm_ref[...] = nm
        nv_ref[...] = nv
        d_ref[...] = -ADAM_LR * ((nm / c1) / (jnp.sqrt(nv / c2) + ADAM_EPS) + ADAM_WD * w_ref[...])

    blk = pl.BlockSpec((None, tr, cols), lambda i, p: (l, first_block(p) + i, 0))
    out = jax.ShapeDtypeStruct((layers, rows, cols), F32)
    extra = ([] if done is None else list(done)) + ([] if after is None else [after])
    aliases = {} if done is None else {5 + i: i for i in range(4)}
    return pl.pallas_call(
        body, name=name,
        grid_spec=pltpu.PrefetchScalarGridSpec(
            num_scalar_prefetch=1, grid=(per,), in_specs=[blk] * 4 + [ANY] * len(extra), out_specs=[blk] * 4),
        out_shape=[out] * 4, input_output_aliases=aliases,
        compiler_params=_params(("parallel",)),
    )(place, w, g, m, v, *extra)


def _pack_small(parts):
    flat = jnp.concatenate([parts[k].reshape(-1) for k in SMALL_NAMES])
    n = flat.shape[0]
    rows = -(-n // (256 * 128)) * 256
    return jnp.pad(flat, (0, rows * 128 - n)).reshape(rows, 128)


def _unpack_small(packed, like):
    flat = packed.reshape(-1)
    out, off = {}, 0
    for k in SMALL_NAMES:
        n = like[k].size
        out[k] = flat[off:off + n].reshape(like[k].shape)
        off += n
    return out


WEIGHT_ORDER = ("norm_g", "w_in", "sgu_ln_g", "sgu_ln_b", "sgu_w", "sgu_b", "mem_norm_g", "w_mem_kv", "q_norm_g",
                "k_norm_g", "w_out")


def kernel(x, mem, norm_g, w_in, sgu_ln_g, sgu_ln_b, sgu_w, sgu_b, mem_norm_g, w_mem_kv, q_norm_g, k_norm_g, w_out, loss_target, m_norm_g, m_w_in, m_sgu_ln_g, m_sgu_ln_b, m_sgu_w, m_sgu_b, m_mem_norm_g, m_w_mem_kv, m_q_norm_g, m_k_norm_g, m_w_out, v_norm_g, v_w_in, v_sgu_ln_g, v_sgu_ln_b, v_sgu_w, v_sgu_b, v_mem_norm_g, v_w_mem_kv, v_q_norm_g, v_k_norm_g, v_w_out):
    weights = dict(norm_g=norm_g, w_in=w_in, sgu_ln_g=sgu_ln_g, sgu_ln_b=sgu_ln_b, sgu_w=sgu_w, sgu_b=sgu_b,
                   mem_norm_g=mem_norm_g, w_mem_kv=w_mem_kv, q_norm_g=q_norm_g, k_norm_g=k_norm_g, w_out=w_out)
    mom_m = dict(norm_g=m_norm_g, w_in=m_w_in, sgu_ln_g=m_sgu_ln_g, sgu_ln_b=m_sgu_ln_b, sgu_w=m_sgu_w, sgu_b=m_sgu_b,
                 mem_norm_g=m_mem_norm_g, w_mem_kv=m_w_mem_kv, q_norm_g=m_q_norm_g, k_norm_g=m_k_norm_g, w_out=m_w_out)
    mom_v = dict(norm_g=v_norm_g, w_in=v_w_in, sgu_ln_g=v_sgu_ln_g, sgu_ln_b=v_sgu_ln_b, sgu_w=v_sgu_w, sgu_b=v_sgu_b,
                 mem_norm_g=v_mem_norm_g, w_mem_kv=v_w_mem_kv, q_norm_g=v_q_norm_g, k_norm_g=v_k_norm_g, w_out=v_w_out)
    big = ("w_in", "w_mem_kv", "w_out")
    sm = {k: weights[k] for k in SMALL_NAMES}

    place = _place_index()
    xs, mems, target = x[0], mem[0], loss_target[0]

    slots = [[_cast_into_slot(f"cast_{k}_{l}", weights[k], l, place) for k in big] for l in range(DEPTH)]
    saved = [None] * DEPTH

    chips, cores = {}, {}

    def start_gather(l, after=None):
        chips[l, "in"] = _gather_start(f"gather_start_{l}_in", slots[l][:1], after)
        chips[l, "rest"] = _gather_start(f"gather_start_{l}_rest", slots[l][1:], chips[l, "in"][3])
        return chips[l, "rest"][3]

    def hand_to_sibling(l, group, after):
        send_sems, recv_sems, bufs, _ = chips[l, group]
        bufs = _gather_wait(f"gather_wait_{l}_{group}", send_sems, recv_sems, bufs, after)
        cores[l, group] = _gather_forward_start(f"gather_forward_{l}_{group}", bufs)
        return cores[l, group][3]

    def whole(l, group, after):
        send_sems, recv_sems, bufs, _ = cores[l, group]
        return _gather_wait(f"gather_whole_{l}_{group}", send_sems, recv_sems, bufs, after)

    class Gathered:
        def __init__(self, l):
            self.l = l

        def w_in(self, h):
            return whole(self.l, "in", h)[0]

        def rest_start(self, proj):
            return hand_to_sibling(self.l, "rest", proj)

        def rest_finish(self, o_b):
            w_kv_all, w_out_all = whole(self.l, "rest", o_b)
            token = start_gather(self.l + 1, w_out_all) if self.l + 1 < DEPTH else None
            return w_kv_all, w_out_all, token

        def before_out(self, y):
            return hand_to_sibling(self.l + 1, "in", y) if self.l + 1 < DEPTH else None

    hand_to_sibling(0, "in", [start_gather(0)] + [s for layer in slots[1:] for s in layer])
    cur = xs
    for l in range(DEPTH):
        cur, saved[l] = _layer_fwd(l, cur, mems, sm, Gathered(l))
    dxo, dxo_b, loss_part = _loss_and_grad("loss", cur, target, min(256, xs.shape[0]))
    loss = lax.psum(loss_part[0, 0], ("x", "y", "c"))

    small_g = [None] * DEPTH
    flight = {}
    for l in reversed(range(DEPTH)):
        def start_exchange(group, full, l=l):
            send_sems, recv_sems, bufs, token = _core_exchange_start(f"grad_core_start_{l}_{group}", full)

            def finish(after):
                grads_l, theirs = _core_exchange_wait(f"grad_core_wait_{l}_{group}", send_sems, recv_sems, bufs, after)
                parts = [_add_to_bf16(f"grad_core_sum_{l}_{group}_{t}", g, th, place)
                         for t, (g, th) in enumerate(zip(grads_l, theirs))]
                *flight[l, group], token = _chip_exchange_start(f"grad_chip_start_{l}_{group}", parts)
                return token

            return token, finish

        dxo, dxo_b, small_g[l], *_ = _layer_bwd(l, dxo, dxo_b, mems, sm, saved[l], on_weight_grads=start_exchange)
    grad_x = dxo

    groups = (("out", ("w_out",)), ("in", ("w_in", "w_mem_kv")))
    halves, stepped = dict.fromkeys(big), dict.fromkeys(big)
    small_g = {k: jnp.stack([small_g[l][k] for l in range(DEPTH)]) for k in SMALL_NAMES}
    after = grad_x
    for l in reversed(range(DEPTH)):
        for group, names in groups:
            send_sems, recv_sems, bufs = flight[l, group]
            bufs = _chip_exchange_wait(f"grad_chip_wait_{l}_{group}", send_sems, recv_sems, bufs, after)
            for t, k in enumerate(names):
                halves[k] = _sum_chips(f"grad_chip_sum_{l}_{k}", bufs[t], bufs[len(names) + t], place, l, halves[k])
        send_sems, recv_sems, bufs, after = _core_share_start(f"grad_core_share_{l}", [halves[k] for k in big], l)
        for k, buf in zip(big, bufs):
            stepped[k] = _adamw(f"adamw_{k}_{l}_own", weights[k], buf, mom_m[k], mom_v[k], place, l, "own",
                                stepped[k], after)
            after = stepped[k][1]
        bufs = _core_share_wait(f"grad_core_shared_{l}", send_sems, recv_sems, bufs, l, after)
        for k, buf in zip(big, bufs):
            halves[k] = buf
            stepped[k] = _adamw(f"adamw_{k}_{l}_other", weights[k], buf, mom_m[k], mom_v[k], place, l, "other",
                                stepped[k], after)
            after = stepped[k][1]
        if l == DEPTH - 1:
            small_sum = _all_reduce_small(_pack_small(small_g), after)
            packed = [a[None] for a in (_pack_small(sm), small_sum, _pack_small({k: mom_m[k] for k in SMALL_NAMES}),
                                        _pack_small({k: mom_v[k] for k in SMALL_NAMES}))]
            small_step = _adamw("adamw_small", *packed, place)

    grads, delta, new_m, new_v = ({k: stepped[k][i] for k in big} for i in range(4))
    for out, packed in zip((grads, delta, new_m, new_v), small_step):
        out.update(_unpack_small(packed[0], sm))
    return (loss, grad_x[None], *[grads[k] for k in WEIGHT_ORDER], *[delta[k] for k in WEIGHT_ORDER],
            *[new_m[k] for k in WEIGHT_ORDER], *[new_v[k] for k in WEIGHT_ORDER])
```

```python
import functools
import math

import jax
import jax.numpy as jnp
from jax import lax
from jax.experimental import pallas as pl
from jax.experimental.pallas import tpu as pltpu

F32 = jnp.float32
BF16 = jnp.bfloat16
MESH = pl.DeviceIdType.MESH

D_MODEL = 2048
DEPTH = 2
CHUNK = 128
D_A = 1024
A_GROUPS = 8
D_B = 512
D_C = 512
HEADS = 4
HEAD_DIM = 128
IN_WIDTH = 6144
N_CHIPS = 4
EPS = 1e-6
ATT_SCALE = 1.0 / math.sqrt(HEAD_DIM)

OFF_U, OFF_V, OFF_ZA = 0, 1024, 2048
OFF_QB, OFF_KB, OFF_VB, OFF_ZB = 3072, 3584, 4096, 4608
OFF_QC, OFF_ZC = 5120, 5632
OFF_YB, OFF_YC = 1024, 1536

ADAM_LR = 0.001
ADAM_B1 = 0.9
ADAM_B2 = 0.999
ADAM_EPS = 1e-08
ADAM_WD = 0.01
ADAM_STEP = 10

MIB = 1024 * 1024
ANY = pl.BlockSpec(memory_space=pl.ANY)


def _params(semantics=None, vmem_mb=48):
    return pltpu.CompilerParams(dimension_semantics=semantics, vmem_limit_bytes=vmem_mb * MIB)


def _gelu(x):
    return 0.5 * x * (1.0 + lax.erf(x * (1.0 / math.sqrt(2.0))))


def _gelu_grad(x):
    cdf = 0.5 * (1.0 + lax.erf(x * (1.0 / math.sqrt(2.0))))
    pdf = jnp.exp(-0.5 * x * x) * (1.0 / math.sqrt(2.0 * math.pi))
    return cdf + x * pdf


def _sigmoid(x):
    return 1.0 / (1.0 + jnp.exp(-x))


def _silu_and_grad(z):
    s = _sigmoid(z)
    return z * s, s * (1.0 + z * (1.0 - s))


def _split_bf16(x):
    hi = x.astype(BF16)
    lo = (x - hi.astype(F32)).astype(BF16)
    return hi, lo


def _dot(a, b, dims):
    return lax.dot_general(a, b, (dims, ((), ())), preferred_element_type=F32)


NN = ((1,), (0,))
NT = ((1,), (1,))
TN = ((0,), (0,))


def _matmul(name, a, b, *, grid, a_spec, b_spec, o_spec, out_shape, dims, res=None, res_spec=None, after=None,
            vmem_mb=48):
    nk = grid[2]
    n_in = 2 + (res is not None) + (after is not None)

    def body(*refs):
        a_ref, b_ref = refs[0], refs[1]
        r_ref = refs[2] if res is not None else None
        o_ref = refs[n_in]
        if len(b_ref.shape) == 3 and dims == NN:
            part = _dot(a_ref[...], b_ref[...].reshape(-1, b_ref.shape[-1]), dims)
        elif len(b_ref.shape) == 3:
            width = b_ref.shape[-1]
            part = None
            for s in range(b_ref.shape[0]):
                term = _dot(a_ref[:, s * width:(s + 1) * width], b_ref[s], dims)
                part = term if part is None else part + term
        else:
            part = _dot(a_ref[...], b_ref[...], dims)
        if nk == 1:
            if r_ref is not None:
                part = part + r_ref[...]
            o_ref[...] = part.astype(o_ref.dtype)
            return
        acc_ref = refs[n_in + 1]
        k = pl.program_id(2)

        @pl.when(k == 0)
        def _():
            acc_ref[...] = part

        @pl.when(k > 0)
        def _():
            acc_ref[...] += part

        @pl.when(k == nk - 1)
        def _():
            tot = acc_ref[...]
            if r_ref is not None:
                tot = tot + r_ref[...]
            o_ref[...] = tot.astype(o_ref.dtype)

    in_specs = [a_spec, b_spec]
    args = [a, b]
    if res is not None:
        in_specs.append(res_spec)
        args.append(res)
    if after is not None:
        in_specs.append(ANY)
        args.append(after)
    acc_shape = tuple(d for d in o_spec.block_shape if d is not None)
    scratch = [pltpu.VMEM(acc_shape, F32)] if nk > 1 else []
    return pl.pallas_call(
        body, name=name, grid=grid, in_specs=in_specs, out_specs=o_spec, out_shape=out_shape,
        scratch_shapes=scratch,
        compiler_params=_params(("parallel", "parallel", "arbitrary"), vmem_mb),
    )(*args)


def _rms_fwd(name, x, g, tr, after=None, transposed=False):
    rows, d = x.shape

    def body(x_ref, g_ref, *refs):
        outs = refs[1:] if after is not None else refs
        xv = x_ref[...]
        r = lax.rsqrt(jnp.mean(xv * xv, axis=-1, keepdims=True) + EPS)
        h = xv * r * g_ref[...]
        outs[0][...] = h.astype(BF16)
        if transposed:
            outs[1][...] = h.T.astype(BF16)

    out_specs = [pl.BlockSpec((tr, d), lambda i: (i, 0))]
    out_shape = [jax.ShapeDtypeStruct((rows, d), BF16)]
    if transposed:
        out_specs.append(pl.BlockSpec((d, tr), lambda i: (0, i)))
        out_shape.append(jax.ShapeDtypeStruct((d, rows), BF16))
    outs = pl.pallas_call(
        body, name=name, grid=(rows // tr,),
        in_specs=[pl.BlockSpec((tr, d), lambda i: (i, 0)), pl.BlockSpec((1, d), lambda i: (0, 0))]
        + ([] if after is None else [ANY]),
        out_specs=out_specs, out_shape=out_shape,
        compiler_params=_params(("parallel",)),
    )(x, g, *([] if after is None else [after]))
    return outs if transposed else outs[0]


def _rms_bwd(name, x, dh, dres, g, tr, after=None):
    rows, d = x.shape

    def body(x_ref, dh_ref, dres_ref, g_ref, *refs):
        dx_ref, dxb_ref, dg_ref = refs[-3:]
        xv = x_ref[...]
        r = lax.rsqrt(jnp.mean(xv * xv, axis=-1, keepdims=True) + EPS)
        xhat = xv * r
        dhv = dh_ref[...]
        dxh = dhv * g_ref[...]
        dx = r * (dxh - xhat * jnp.mean(dxh * xhat, axis=-1, keepdims=True)) + dres_ref[...]
        dx_ref[...] = dx
        dxb_ref[...] = dx.astype(BF16)
        part = jnp.sum(dhv * xhat, axis=0, keepdims=True)

        @pl.when(pl.program_id(0) == 0)
        def _():
            dg_ref[...] = part

        @pl.when(pl.program_id(0) > 0)
        def _():
            dg_ref[...] += part

    blk = pl.BlockSpec((tr, d), lambda i: (i, 0))
    vec = pl.BlockSpec((1, d), lambda i: (0, 0))
    return pl.pallas_call(
        body, name=name, grid=(rows // tr,), in_specs=[blk, blk, blk, vec] + ([] if after is None else [ANY]),
        out_specs=[blk, blk, vec],
        out_shape=[jax.ShapeDtypeStruct((rows, d), F32), jax.ShapeDtypeStruct((rows, d), BF16),
                   jax.ShapeDtypeStruct((1, d), F32)],
        compiler_params=_params(("arbitrary",)),
    )(x, dh, dres, g, *([] if after is None else [after]))


def _rms_gain_grad(name, x, dh):
    rows, d = x.shape

    def body(x_ref, dh_ref, dg_ref):
        xv = x_ref[...]
        r = lax.rsqrt(jnp.mean(xv * xv, axis=-1, keepdims=True) + EPS)
        dg_ref[...] = jnp.sum(dh_ref[...] * xv * r, axis=0, keepdims=True)

    return pl.pallas_call(
        body, name=name, out_shape=jax.ShapeDtypeStruct((1, d), F32), compiler_params=_params(None),
    )(x, dh)


def _loss_and_grad(name, y, target, tr):
    rows, d = y.shape
    n = rows // tr

    def body(y_ref, t_ref, dx_ref, dxb_ref, loss_ref, acc_ref):
        e = y_ref[...] - t_ref[...]
        dx = e * (1.0 / d)
        dx_ref[...] = dx
        dxb_ref[...] = dx.astype(BF16)
        part = jnp.sum(e * e, axis=0, keepdims=True)
        i = pl.program_id(0)

        @pl.when(i == 0)
        def _():
            acc_ref[...] = part

        @pl.when(i > 0)
        def _():
            acc_ref[...] += part

        @pl.when(i == n - 1)
        def _():
            loss_ref[...] = jnp.sum(acc_ref[...], axis=-1, keepdims=True) * (0.5 / d)

    blk = pl.BlockSpec((tr, d), lambda i: (i, 0))
    return pl.pallas_call(
        body, name=name, grid=(n,), in_specs=[blk, blk],
        out_specs=[blk, blk, pl.BlockSpec((1, 1), lambda i: (0, 0))],
        out_shape=[jax.ShapeDtypeStruct((rows, d), F32), jax.ShapeDtypeStruct((rows, d), BF16),
                   jax.ShapeDtypeStruct((1, 1), F32)],
        scratch_shapes=[pltpu.VMEM((1, d), F32)],
        compiler_params=_params(("arbitrary",)),
    )(y, target)


SB_T = 256
SB_HEADS = 2


def _sb_scores(q, kblk):
    z = _dot(q, kblk, NT) * ATT_SCALE
    e = jnp.exp(-jnp.abs(z))
    sp = jnp.log1p(e)
    lb = jnp.minimum(z, 0.0) - sp
    l1 = lb - z
    return z, e, lb, l1


def _sb_fwd(name, proj, after=None):
    s_len = proj.shape[0]
    t = SB_T
    nq = s_len // t

    def body(q_ref, k_ref, v_ref, *refs):
        o_ref = refs[-1]
        i = pl.program_id(1)
        row = lax.broadcasted_iota(jnp.int32, (t, t), 0)
        col = lax.broadcasted_iota(jnp.int32, (t, t), 1)
        causal = col < row
        after_mat = (row > col).astype(BF16)
        heads = [slice(hh * HEAD_DIM, (hh + 1) * HEAD_DIM) for hh in range(SB_HEADS)]
        q = [q_ref[:, sl].astype(BF16) for sl in heads]

        def tile(kb, state, masked):
            start = pl.multiple_of(kb * t, t)
            out = []
            for hh, sl in enumerate(heads):
                carry, acc = state[hh]
                kblk = k_ref[pl.ds(start, t), sl].astype(BF16)
                vblk = v_ref[pl.ds(start, t), sl].astype(BF16)
                _, _, lb, l1 = _sb_scores(q[hh], kblk)
                if masked:
                    l1 = jnp.where(causal, l1, 0.0)
                hi, lo = _split_bf16(l1)
                after = _dot(hi, after_mat, NN) + _dot(lo, after_mat, NN) + carry
                a = jnp.exp(lb + after)
                if masked:
                    a = jnp.where(causal, a, 0.0)
                acc = acc + _dot(a.astype(BF16), vblk, NN)
                carry = carry + jnp.sum(l1, axis=-1, keepdims=True)
                out.append((carry, acc))
            return tuple(out)

        zero = (jnp.zeros((t, 1), F32), jnp.zeros((t, HEAD_DIM), F32))
        state = tile(i, (zero,) * SB_HEADS, True)
        state = lax.fori_loop(0, i, lambda n, st: tile(i - 1 - n, st, False), state)
        for hh, sl in enumerate(heads):
            o_ref[:, sl] = state[hh][1]

    cb = SB_HEADS * HEAD_DIM
    return pl.pallas_call(
        body, name=name, grid=(HEADS // SB_HEADS, nq),
        in_specs=[pl.BlockSpec((t, cb), lambda h, i: (i, OFF_QB // cb + h)),
                  pl.BlockSpec((s_len, cb), lambda h, i: (0, OFF_KB // cb + h)),
                  pl.BlockSpec((s_len, cb), lambda h, i: (0, OFF_VB // cb + h))] + ([] if after is None else [ANY]),
        out_specs=pl.BlockSpec((t, cb), lambda h, i: (i, h)),
        out_shape=jax.ShapeDtypeStruct((s_len, D_B), F32),
        compiler_params=_params(("parallel", "arbitrary")),
    )(proj, proj, proj, *([] if after is None else [after]))


def _sb_bwd(name, proj, dy, after=None):
    s_len = proj.shape[0]
    t = SB_T
    nq = s_len // t

    def body(q_ref, k_ref, v_ref, z_ref, dy_ref, *refs):
        dq_ref, dk_ref, dv_ref, a_ref, s_ref = refs[-5:]
        i = pl.program_id(1)

        @pl.when(i == 0)
        def _():
            dk_ref[...] = jnp.zeros_like(dk_ref)
            dv_ref[...] = jnp.zeros_like(dv_ref)

        heads = [slice(hh * HEAD_DIM, (hh + 1) * HEAD_DIM) for hh in range(SB_HEADS)]
        q = [q_ref[:, sl].astype(BF16) for sl in heads]
        silu_z, _ = _silu_and_grad(z_ref[...])
        do_all = dy_ref[...] * silu_z
        do_b = [do_all[:, sl].astype(BF16) for sl in heads]
        row = lax.broadcasted_iota(jnp.int32, (t, t), 0)
        col = lax.broadcasted_iota(jnp.int32, (t, t), 1)
        causal = col < row
        after_mat = (row > col).astype(BF16)
        before_mat = (row < col).astype(BF16)

        def weights(kb, carries, masked):
            start = pl.multiple_of(kb * t, t)
            out = []
            for hh, sl in enumerate(heads):
                kblk = k_ref[pl.ds(start, t), sl].astype(BF16)
                z, _, lb, l1 = _sb_scores(q[hh], kblk)
                if masked:
                    l1 = jnp.where(causal, l1, 0.0)
                hi, lo = _split_bf16(l1)
                after = _dot(hi, after_mat, NN) + _dot(lo, after_mat, NN) + carries[hh]
                a = jnp.exp(lb + after)
                if masked:
                    a = jnp.where(causal, a, 0.0)
                a_ref[hh, kb] = a
                s_ref[hh, kb] = z
                out.append(carries[hh] + jnp.sum(l1, axis=-1, keepdims=True))
            return tuple(out)

        carries = weights(i, (jnp.zeros((t, 1), F32),) * SB_HEADS, True)
        lax.fori_loop(0, i, lambda n, c: weights(i - 1 - n, c, False), carries)

        def grads(kb, state, masked):
            start = pl.multiple_of(kb * t, t)
            out = []
            for hh, sl in enumerate(heads):
                carry, dq = state[hh]
                kblk = k_ref[pl.ds(start, t), sl].astype(BF16)
                vblk = v_ref[pl.ds(start, t), sl].astype(BF16)
                a = a_ref[hh, kb]
                z = s_ref[hh, kb]
                g = _dot(do_b[hh], vblk, NT) * a
                ghi, glo = _split_bf16(g)
                prefix = _dot(ghi, before_mat, NN) + _dot(glo, before_mat, NN) + carry
                e = jnp.exp(-jnp.abs(z))
                inv = 1.0 / (1.0 + e)
                pos = z >= 0.0
                beta = jnp.where(pos, inv, e * inv)
                one_m_beta = jnp.where(pos, e * inv, inv)
                dz = (g * one_m_beta - prefix * beta) * ATT_SCALE
                if masked:
                    dz = jnp.where(causal, dz, 0.0)
                dz_b = dz.astype(BF16)
                dq = dq + _dot(dz_b, kblk, NN)
                dk_ref[pl.ds(start, t), sl] += _dot(dz_b, q[hh], TN)
                dv_ref[pl.ds(start, t), sl] += _dot(a.astype(BF16), do_b[hh], TN)
                out.append((carry + jnp.sum(g, axis=-1, keepdims=True), dq))
            return tuple(out)

        zero = (jnp.zeros((t, 1), F32), jnp.zeros((t, HEAD_DIM), F32))
        state = lax.fori_loop(0, i, lambda kb, st: grads(kb, st, False), (zero,) * SB_HEADS)
        state = grads(i, state, True)
        for hh, sl in enumerate(heads):
            dq_ref[:, sl] = state[hh][1]

    cb = SB_HEADS * HEAD_DIM
    qblk = lambda off: pl.BlockSpec((t, cb), lambda h, i: (i, off // cb + h))
    full = lambda off: pl.BlockSpec((s_len, cb), lambda h, i: (0, off // cb + h))
    out = jax.ShapeDtypeStruct((s_len, D_B), F32)
    return pl.pallas_call(
        body, name=name, grid=(HEADS // SB_HEADS, nq),
        in_specs=[qblk(OFF_QB), full(OFF_KB), full(OFF_VB), qblk(OFF_ZB), qblk(OFF_YB)]
        + ([] if after is None else [ANY]),
        out_specs=[qblk(0), full(0), full(0)],
        out_shape=[out, out, out],
        scratch_shapes=[pltpu.VMEM((SB_HEADS, nq, t, t), F32), pltpu.VMEM((SB_HEADS, nq, t, t), F32)],
        compiler_params=_params(("parallel", "arbitrary")),
    )(proj, proj, proj, proj, dy, *([] if after is None else [after]))


MEM_TQ = 512


def _qk_norm(x, g):
    r = lax.rsqrt(jnp.mean(x * x, axis=-1, keepdims=True) + EPS)
    xhat = x * r
    return xhat * g, xhat, r


def _qk_norm_bwd(dn, g, xhat, r):
    dxh = dn * g
    return r * (dxh - xhat * jnp.mean(dxh * xhat, axis=-1, keepdims=True))


def _mem_probs(q, mk, qg, kg):
    qn, qhat, rq = _qk_norm(q, qg)
    kn, khat, rk = _qk_norm(mk, kg)
    qn_b, kn_b = qn.astype(BF16), kn.astype(BF16)
    s = _dot(qn_b, kn_b, NT) * ATT_SCALE
    p = jnp.exp(s - jnp.max(s, axis=-1, keepdims=True))
    p = p / jnp.sum(p, axis=-1, keepdims=True)
    return p, qn_b, kn_b, qhat, rq, khat, rk


def _mem_fwd(name, proj, mem_kv, qg, kg):
    s_len = proj.shape[0]
    m_len = mem_kv.shape[0]
    tq = min(MEM_TQ, s_len)

    def body(q_ref, mk_ref, mv_ref, qg_ref, kg_ref, o_ref):
        p = _mem_probs(q_ref[...], mk_ref[...], qg_ref[...], kg_ref[...])[0]
        o_ref[...] = _dot(p.astype(BF16), mv_ref[...].astype(BF16), NN)

    cb = HEAD_DIM
    vec = pl.BlockSpec((1, cb), lambda h, i: (0, 0))
    return pl.pallas_call(
        body, name=name, grid=(HEADS, s_len // tq),
        in_specs=[pl.BlockSpec((tq, cb), lambda h, i: (i, OFF_QC // cb + h)),
                  pl.BlockSpec((m_len, cb), lambda h, i: (0, h)),
                  pl.BlockSpec((m_len, cb), lambda h, i: (0, HEADS + h)), vec, vec],
        out_specs=pl.BlockSpec((tq, cb), lambda h, i: (i, h)),
        out_shape=jax.ShapeDtypeStruct((s_len, D_C), F32),
        compiler_params=_params(("parallel", "parallel")),
    )(proj, mem_kv, mem_kv, qg, kg)


def _mem_bwd(name, proj, mem_kv, qg, kg, dy):
    s_len = proj.shape[0]
    m_len = mem_kv.shape[0]
    tq = min(MEM_TQ, s_len)

    def body(q_ref, mk_ref, mv_ref, qg_ref, kg_ref, z_ref, dy_ref, dq_ref, dmk_ref, dmv_ref, dqg_ref, dkg_ref):
        h, i = pl.program_id(0), pl.program_id(1)

        @pl.when(i == 0)
        def _():
            dmk_ref[...] = jnp.zeros_like(dmk_ref)
            dmv_ref[...] = jnp.zeros_like(dmv_ref)

        @pl.when((i == 0) & (h == 0))
        def _():
            dqg_ref[...] = jnp.zeros_like(dqg_ref)
            dkg_ref[...] = jnp.zeros_like(dkg_ref)

        qg, kg = qg_ref[...], kg_ref[...]
        p, qn_b, kn_b, qhat, rq, khat, rk = _mem_probs(q_ref[...], mk_ref[...], qg, kg)
        silu_z, _ = _silu_and_grad(z_ref[...])
        do_b = (dy_ref[...] * silu_z).astype(BF16)
        dmv_ref[...] += _dot(p.astype(BF16), do_b, TN)
        dp = _dot(do_b, mv_ref[...].astype(BF16), NT)
        ds = (p * (dp - jnp.sum(dp * p, axis=-1, keepdims=True)) * ATT_SCALE).astype(BF16)
        dqn = _dot(ds, kn_b, NN)
        dkn = _dot(ds, qn_b, TN)
        dq_ref[...] = _qk_norm_bwd(dqn, qg, qhat, rq)
        dmk_ref[...] += _qk_norm_bwd(dkn, kg, khat, rk)
        dqg_ref[...] += jnp.sum(dqn * qhat, axis=0, keepdims=True)
        dkg_ref[...] += jnp.sum(dkn * khat, axis=0, keepdims=True)

    cb = HEAD_DIM
    vec = pl.BlockSpec((1, cb), lambda h, i: (0, 0))
    qblk = lambda off: pl.BlockSpec((tq, cb), lambda h, i: (i, off // cb + h))
    memblk = lambda off: pl.BlockSpec((m_len, cb), lambda h, i: (0, off + h))
    return pl.pallas_call(
        body, name=name, grid=(HEADS, s_len // tq),
        in_specs=[qblk(OFF_QC), memblk(0), memblk(HEADS), vec, vec, qblk(OFF_ZC), qblk(OFF_YC)],
        out_specs=[qblk(0), memblk(0), memblk(0), vec, vec],
        out_shape=[jax.ShapeDtypeStruct((s_len, D_C), F32), jax.ShapeDtypeStruct((m_len, D_C), F32),
                   jax.ShapeDtypeStruct((m_len, D_C), F32), jax.ShapeDtypeStruct((1, cb), F32),
                   jax.ShapeDtypeStruct((1, cb), F32)],
        compiler_params=_params(("arbitrary", "arbitrary")),
    )(proj, mem_kv, mem_kv, qg, kg, proj, dy)


def _sgu_common(u_ref, v_ref, lng_ref, lnb_ref, w_ref, bias_ref):
    ug = _gelu(u_ref[...])
    vg = _gelu(v_ref[...])
    mu = jnp.mean(vg, axis=-1, keepdims=True)
    xc = vg - mu
    rstd = lax.rsqrt(jnp.mean(xc * xc, axis=-1, keepdims=True) + EPS)
    xhat = xc * rstd
    vn = xhat * lng_ref[...] + lnb_ref[...]
    vn_b = vn.astype(BF16)
    row = lax.broadcasted_iota(jnp.int32, (CHUNK, CHUNK), 0)
    col = lax.broadcasted_iota(jnp.int32, (CHUNK, CHUNK), 1)
    tril = row >= col
    mixed = []
    for g in range(A_GROUPS):
        w = jnp.where(tril, w_ref[g], 0.0).astype(BF16)
        sl = slice(g * CHUNK, (g + 1) * CHUNK)
        mixed.append(_dot(w, vn_b[:, sl], NN) + bias_ref[:, sl])
    return ug, xhat, rstd, vn_b, mixed, tril


def _gate_fwd(name, proj, o_b, o_c, lng, lnb, w_s, bias):
    s_len = proj.shape[0]

    def body(u_ref, v_ref, za_ref, zb_ref, zc_ref, ob_ref, oc_ref, lng_ref, lnb_ref, w_ref, bias_ref, y_ref, yt_ref):
        ug, _, _, _, mixed, _ = _sgu_common(u_ref, v_ref, lng_ref, lnb_ref, w_ref, bias_ref)
        sza, _ = _silu_and_grad(za_ref[...])
        gate = ug * sza

        def put(off, width, val):
            y_ref[:, off:off + width] = val.astype(BF16)
            yt_ref[off:off + width, :] = val.T.astype(BF16)

        for g in range(A_GROUPS):
            sl = slice(g * CHUNK, (g + 1) * CHUNK)
            put(g * CHUNK, CHUNK, gate[:, sl] * mixed[g])
        szb, _ = _silu_and_grad(zb_ref[...])
        put(OFF_YB, D_B, ob_ref[...] * szb)
        szc, _ = _silu_and_grad(zc_ref[...])
        put(OFF_YC, D_C, oc_ref[...] * szc)

    wide = lambda off: pl.BlockSpec((CHUNK, D_A), lambda i: (i, off // D_A))
    narrow = lambda off: pl.BlockSpec((CHUNK, D_B), lambda i: (i, off // D_B))
    vec = pl.BlockSpec((1, D_A), lambda i: (0, 0))
    return pl.pallas_call(
        body, name=name, grid=(s_len // CHUNK,),
        in_specs=[wide(OFF_U), wide(OFF_V), wide(OFF_ZA), narrow(OFF_ZB), narrow(OFF_ZC), narrow(0), narrow(0), vec, vec,
                  pl.BlockSpec((A_GROUPS, CHUNK, CHUNK), lambda i: (0, 0, 0)),
                  pl.BlockSpec((CHUNK, D_A), lambda i: (0, 0))],
        out_specs=[pl.BlockSpec((CHUNK, D_MODEL), lambda i: (i, 0)), pl.BlockSpec((D_MODEL, CHUNK), lambda i: (0, i))],
        out_shape=[jax.ShapeDtypeStruct((s_len, D_MODEL), BF16), jax.ShapeDtypeStruct((D_MODEL, s_len), BF16)],
        compiler_params=_params(("parallel",)),
    )(proj, proj, proj, proj, proj, o_b, o_c, lng, lnb, w_s, bias)


def _gate_bwd(name, proj, dy, o_b, o_c, dqkv, dq_c, lng, lnb, w_s, w_s_t, bias):
    s_len = proj.shape[0]
    n = s_len // CHUNK
    dq_b, dk_b, dv_b = dqkv

    def body(u_ref, v_ref, za_ref, zb_ref, zc_ref, dya_ref, dyb_ref, dyc_ref, ob_ref, oc_ref, dq_ref, dk_ref, dv_ref,
             dqc_ref, lng_ref, lnb_ref, w_ref, wt_ref, bias_ref, dp_ref, dw_ref, dsb_ref, dlng_ref, dlnb_ref, dbias_ref):
        i = pl.program_id(0)

        @pl.when(i == 0)
        def _():
            dw_ref[...] = jnp.zeros_like(dw_ref)
            dbias_ref[...] = jnp.zeros_like(dbias_ref)
            dlng_ref[...] = jnp.zeros_like(dlng_ref)
            dlnb_ref[...] = jnp.zeros_like(dlnb_ref)

        ug, xhat, rstd, vn_b, mixed, tril = _sgu_common(u_ref, v_ref, lng_ref, lnb_ref, w_ref, bias_ref)
        za = za_ref[...]
        sza, dsza = _silu_and_grad(za)
        dya = dya_ref[...]
        mixed_all = jnp.concatenate(mixed, axis=-1)
        d_mixed = dya * ug * sza
        dp_ref[:, OFF_U:OFF_U + D_A] = (dya * mixed_all * sza * _gelu_grad(u_ref[...])).astype(BF16)
        dp_ref[:, OFF_ZA:OFF_ZA + D_A] = (dya * ug * mixed_all * dsza).astype(BF16)
        dbias_ref[...] += d_mixed
        dm_b = d_mixed.astype(BF16)
        triu = lax.broadcasted_iota(jnp.int32, (CHUNK, CHUNK), 0) <= lax.broadcasted_iota(jnp.int32, (CHUNK, CHUNK), 1)
        d_vn = []
        for g in range(A_GROUPS):
            sl = slice(g * CHUNK, (g + 1) * CHUNK)
            wt = jnp.where(triu, wt_ref[g], 0.0).astype(BF16)
            d_vn.append(_dot(wt, dm_b[:, sl], NN))
            dw_ref[g] += jnp.where(tril, _dot(dm_b[:, sl], vn_b[:, sl], NT), 0.0)
        d_vn = jnp.concatenate(d_vn, axis=-1)
        dlng_ref[...] += jnp.sum(d_vn * xhat, axis=0, keepdims=True)
        dlnb_ref[...] += jnp.sum(d_vn, axis=0, keepdims=True)
        dxh = d_vn * lng_ref[...]
        d_vg = rstd * (dxh - jnp.mean(dxh, axis=-1, keepdims=True)
                       - xhat * jnp.mean(dxh * xhat, axis=-1, keepdims=True))
        dp_ref[:, OFF_V:OFF_V + D_A] = (d_vg * _gelu_grad(v_ref[...])).astype(BF16)
        dp_ref[:, OFF_QB:OFF_QB + D_B] = dq_ref[...].astype(BF16)
        dp_ref[:, OFF_KB:OFF_KB + D_B] = dk_ref[...].astype(BF16)
        dp_ref[:, OFF_VB:OFF_VB + D_B] = dv_ref[...].astype(BF16)
        _, dszb = _silu_and_grad(zb_ref[...])
        dp_ref[:, OFF_ZB:OFF_ZB + D_B] = (dyb_ref[...] * ob_ref[...] * dszb).astype(BF16)
        dp_ref[:, OFF_QC:OFF_QC + D_C] = dqc_ref[...].astype(BF16)
        _, dszc = _silu_and_grad(zc_ref[...])
        dp_ref[:, OFF_ZC:OFF_ZC + D_C] = (dyc_ref[...] * oc_ref[...] * dszc).astype(BF16)

        @pl.when(i == n - 1)
        def _():
            ch = lax.broadcasted_iota(jnp.int32, (D_A, CHUNK), 0)
            gcol = lax.broadcasted_iota(jnp.int32, (D_A, CHUNK), 1)
            pick = (ch // (D_A // A_GROUPS) == gcol).astype(BF16)
            rest = dbias_ref[...]
            tot = jnp.zeros((CHUNK, CHUNK), F32)
            for _ in range(3):
                term = rest.astype(BF16)
                tot = tot + _dot(term, pick, NN)
                rest = rest - term.astype(F32)
            dsb_ref[...] = tot

    wide = lambda off: pl.BlockSpec((CHUNK, D_A), lambda i: (i, off // D_A))
    narrow = lambda off: pl.BlockSpec((CHUNK, D_B), lambda i: (i, off // D_B))
    vec = pl.BlockSpec((1, D_A), lambda i: (0, 0))
    wspec = pl.BlockSpec((A_GROUPS, CHUNK, CHUNK), lambda i: (0, 0, 0))
    bspec = pl.BlockSpec((CHUNK, D_A), lambda i: (0, 0))
    return pl.pallas_call(
        body, name=name, grid=(n,),
        in_specs=[wide(OFF_U), wide(OFF_V), wide(OFF_ZA), narrow(OFF_ZB), narrow(OFF_ZC),
                  wide(0), narrow(OFF_YB), narrow(OFF_YC), narrow(0), narrow(0), narrow(0), narrow(0), narrow(0),
                  narrow(0), vec, vec, wspec, wspec, bspec],
        out_specs=[pl.BlockSpec((CHUNK, IN_WIDTH), lambda i: (i, 0)), wspec,
                   pl.BlockSpec((CHUNK, CHUNK), lambda i: (0, 0)), vec, vec],
        out_shape=[jax.ShapeDtypeStruct((s_len, IN_WIDTH), BF16), jax.ShapeDtypeStruct((A_GROUPS, CHUNK, CHUNK), F32),
                   jax.ShapeDtypeStruct((CHUNK, CHUNK), F32), jax.ShapeDtypeStruct((1, D_A), F32),
                   jax.ShapeDtypeStruct((1, D_A), F32)],
        scratch_shapes=[pltpu.VMEM((CHUNK, D_A), F32)],
        compiler_params=_params(("arbitrary",)),
    )(proj, proj, proj, proj, proj, dy, dy, dy, o_b, o_c, dq_b, dk_b, dv_b, dq_c, lng, lnb, w_s, w_s_t, bias)


IN_SHARD = IN_WIDTH // N_CHIPS
ROW_SHARD = D_MODEL // N_CHIPS


def _bias_rows(sgu_b_l):
    return jnp.repeat(sgu_b_l.T, D_A // A_GROUPS, axis=1)


class _WholeWeights:
    def __init__(self, w_in_all, w_kv_all, w_out_all):
        self.weights = (w_in_all, w_kv_all, w_out_all)

    def w_in(self, h):
        return self.weights[0]

    def rest_start(self, proj):
        return None

    def rest_finish(self, o_b):
        return self.weights[1], self.weights[2], None

    def before_out(self, y):
        return None


def _layer_fwd(l, x, mem, sm, hooks):
    s_len = x.shape[0]
    m_len = mem.shape[0]
    tm = min(1024, s_len)
    tn = 768
    per = IN_SHARD // tn
    h, h_t = _rms_fwd(f"rms_fwd_{l}", x, sm["norm_g"][l][None], min(256, s_len), transposed=True)
    w_in_all = hooks.w_in(h)
    proj = _matmul(
        f"in_proj_{l}", h, w_in_all, grid=(s_len // tm, IN_WIDTH // tn, 1),
        a_spec=pl.BlockSpec((tm, D_MODEL), lambda i, j, k: (i, 0)),
        b_spec=pl.BlockSpec((None, D_MODEL, tn), lambda i, j, k: (j // per, 0, j % per)),
        o_spec=pl.BlockSpec((tm, tn), lambda i, j, k: (i, j)),
        out_shape=jax.ShapeDtypeStruct((s_len, IN_WIDTH), F32), dims=NN)
    o_b = _sb_fwd(f"sb_fwd_{l}", proj, hooks.rest_start(proj))
    w_kv_all, w_out_all, after = hooks.rest_finish(o_b)
    mem_h = _rms_fwd(f"mem_rms_fwd_{l}", mem, sm["mem_norm_g"][l][None], m_len, after)
    mem_kv = _matmul(
        f"mem_kv_{l}", mem_h, w_kv_all, grid=(1, 2, N_CHIPS),
        a_spec=pl.BlockSpec((m_len, ROW_SHARD), lambda i, j, k: (0, k)),
        b_spec=pl.BlockSpec((None, ROW_SHARD, D_C), lambda i, j, k: (k, 0, j)),
        o_spec=pl.BlockSpec((m_len, D_C), lambda i, j, k: (0, j)),
        out_shape=jax.ShapeDtypeStruct((m_len, 2 * D_C), F32), dims=NN)
    qg, kg = sm["q_norm_g"][l][None], sm["k_norm_g"][l][None]
    o_c = _mem_fwd(f"mem_fwd_{l}", proj, mem_kv, qg, kg)
    bias = _bias_rows(sm["sgu_b"][l])
    y, y_t = _gate_fwd(f"gate_fwd_{l}", proj, o_b, o_c, sm["sgu_ln_g"][l][None], sm["sgu_ln_b"][l][None],
                       sm["sgu_w"][l], bias)
    tn_o = 512
    x_next = _matmul(
        f"out_proj_{l}", y, w_out_all, grid=(s_len // tm, D_MODEL // tn_o, 1),
        a_spec=pl.BlockSpec((tm, D_MODEL), lambda i, j, k: (i, 0)),
        b_spec=pl.BlockSpec((N_CHIPS, ROW_SHARD, tn_o), lambda i, j, k: (0, 0, j)),
        o_spec=pl.BlockSpec((tm, tn_o), lambda i, j, k: (i, j)),
        out_shape=jax.ShapeDtypeStruct((s_len, D_MODEL), F32), dims=NN,
        res=x, res_spec=pl.BlockSpec((tm, tn_o), lambda i, j, k: (i, j)), after=hooks.before_out(y))
    saved = dict(x=x, h_t=h_t, proj=proj, mem_h=mem_h, mem_kv=mem_kv, o_b=o_b, o_c=o_c, y_t=y_t, bias=bias,
                 weights=(w_in_all, w_kv_all, w_out_all))
    return x_next, saved


def _layer_bwd(l, dxo, dxo_b, mem, sm, saved, on_weight_grads=None):
    s_len = dxo.shape[0]
    m_len = mem.shape[0]
    proj, y_t, h_t, mem_h, mem_kv = saved["proj"], saved["y_t"], saved["h_t"], saved["mem_h"], saved["mem_kv"]
    w_in_all, w_kv_all, w_out_all = saved["weights"]
    tm = min(1024, s_len)
    tk = s_len
    g_out = _matmul(
        f"d_w_out_{l}", y_t, dxo_b, grid=(N_CHIPS, D_MODEL // 1024, s_len // tk),
        a_spec=pl.BlockSpec((ROW_SHARD, tk), lambda i, j, k: (i, k)),
        b_spec=pl.BlockSpec((tk, 1024), lambda i, j, k: (k, j)),
        o_spec=pl.BlockSpec((None, ROW_SHARD, 1024), lambda i, j, k: (i, 0, j)),
        out_shape=jax.ShapeDtypeStruct((N_CHIPS, ROW_SHARD, D_MODEL), F32), dims=NN)
    token, finish = (None, None) if on_weight_grads is None else on_weight_grads("out", [g_out])
    dy = _matmul(
        f"d_y_{l}", dxo_b, w_out_all, grid=(s_len // tm, N_CHIPS, 1),
        a_spec=pl.BlockSpec((tm, D_MODEL), lambda i, j, k: (i, 0)),
        b_spec=pl.BlockSpec((None, ROW_SHARD, D_MODEL), lambda i, j, k: (j, 0, 0)),
        o_spec=pl.BlockSpec((tm, ROW_SHARD), lambda i, j, k: (i, j)),
        out_shape=jax.ShapeDtypeStruct((s_len, D_MODEL), F32), dims=NT, after=token)
    token = None if finish is None else finish(dy)
    qg, kg = sm["q_norm_g"][l][None], sm["k_norm_g"][l][None]
    dqkv = _sb_bwd(f"sb_bwd_{l}", proj, dy, token)
    dq_c, dmk, dmv, dqg, dkg = _mem_bwd(f"mem_bwd_{l}", proj, mem_kv, qg, kg, dy)
    w_s = sm["sgu_w"][l]
    dproj, dws, dbias, dlng, dlnb = _gate_bwd(
        f"gate_bwd_{l}", proj, dy, saved["o_b"], saved["o_c"], dqkv, dq_c, sm["sgu_ln_g"][l][None],
        sm["sgu_ln_b"][l][None], w_s, jnp.swapaxes(w_s, 1, 2), saved["bias"])
    tn = 768
    per = IN_SHARD // tn
    g_in = _matmul(
        f"d_w_in_{l}", h_t, dproj, grid=(D_MODEL // 1024, IN_WIDTH // tn, s_len // tk),
        a_spec=pl.BlockSpec((1024, tk), lambda i, j, k: (i, k)),
        b_spec=pl.BlockSpec((tk, tn), lambda i, j, k: (k, j)),
        o_spec=pl.BlockSpec((None, 1024, tn), lambda i, j, k: (j // per, i, j % per)),
        out_shape=jax.ShapeDtypeStruct((N_CHIPS, D_MODEL, IN_SHARD), F32), dims=NN)
    dkv_b = jnp.concatenate([dmk, dmv], axis=1).astype(BF16)
    g_kv = _matmul(
        f"d_w_kv_{l}", mem_h, dkv_b, grid=(N_CHIPS, 1, 1),
        a_spec=pl.BlockSpec((m_len, ROW_SHARD), lambda i, j, k: (0, i)),
        b_spec=pl.BlockSpec((m_len, 2 * D_C), lambda i, j, k: (0, 0)),
        o_spec=pl.BlockSpec((None, ROW_SHARD, 2 * D_C), lambda i, j, k: (i, 0, 0)),
        out_shape=jax.ShapeDtypeStruct((N_CHIPS, ROW_SHARD, 2 * D_C), F32), dims=TN)
    token, finish = (None, None) if on_weight_grads is None else on_weight_grads("in", [g_in, g_kv])
    th = min(512, s_len)
    dh = _matmul(
        f"d_h_{l}", dproj, w_in_all, grid=(s_len // th, D_MODEL // 512, 1),
        a_spec=pl.BlockSpec((th, IN_WIDTH), lambda i, j, k: (i, 0)),
        b_spec=pl.BlockSpec((N_CHIPS, 512, IN_SHARD), lambda i, j, k: (0, j, 0)),
        o_spec=pl.BlockSpec((th, 512), lambda i, j, k: (i, j)),
        out_shape=jax.ShapeDtypeStruct((s_len, D_MODEL), F32), dims=NT, after=token)
    token = None if finish is None else finish(dh)
    dx, dx_b, dng = _rms_bwd(f"rms_bwd_{l}", saved["x"], dh, dxo, sm["norm_g"][l][None], min(256, s_len), token)
    d_mem_h = _matmul(
        f"d_mem_h_{l}", dkv_b, w_kv_all, grid=(1, N_CHIPS, 1),
        a_spec=pl.BlockSpec((m_len, 2 * D_C), lambda i, j, k: (0, 0)),
        b_spec=pl.BlockSpec((None, ROW_SHARD, 2 * D_C), lambda i, j, k: (j, 0, 0)),
        o_spec=pl.BlockSpec((m_len, ROW_SHARD), lambda i, j, k: (0, j)),
        out_shape=jax.ShapeDtypeStruct((m_len, D_MODEL), F32), dims=NT)
    dmng = _rms_gain_grad(f"mem_rms_bwd_{l}", mem, d_mem_h)
    dsgu_b = dbias[:, :A_GROUPS].T
    small = dict(norm_g=dng[0], sgu_ln_g=dlng[0], sgu_ln_b=dlnb[0], sgu_w=dws, sgu_b=dsgu_b, mem_norm_g=dmng[0],
                 q_norm_g=dqg[0], k_norm_g=dkg[0])
    return dx, dx_b, small, g_in, g_kv, g_out


SMALL_NAMES = ("norm_g", "sgu_ln_g", "sgu_ln_b", "sgu_w", "sgu_b", "mem_norm_g", "q_norm_g", "k_norm_g")


def _local_step(x, mem, target, sm, w_all):
    saved = []
    cur = x
    for l in range(DEPTH):
        cur, sv = _layer_fwd(l, cur, mem, sm, _WholeWeights(*w_all[l]))
        saved.append(sv)
    dxo, dxo_b, loss = _loss_and_grad("loss", cur, target, min(256, x.shape[0]))
    small, big = [None] * DEPTH, [None] * DEPTH
    for l in reversed(range(DEPTH)):
        dxo, dxo_b, small[l], *big[l] = _layer_bwd(l, dxo, dxo_b, mem, sm, saved[l])
    small = {k: jnp.stack([small[l][k] for l in range(DEPTH)]) for k in SMALL_NAMES}
    return loss, dxo, small, big


def _place():
    x, y, c = lax.axis_index("x"), lax.axis_index("y"), lax.axis_index("c")
    return x, y, c


def _other_chips(x, y):
    return [(1 - x, y, 2 * (1 - x) + y), (x, 1 - y, 2 * x + 1 - y), (1 - x, 1 - y, 2 * (1 - x) + 1 - y)]


AG_CHUNKS = 4
D2D_CHUNKS = 8


def _place_index():
    return jnp.stack([2 * lax.axis_index("x") + lax.axis_index("y"), lax.axis_index("c")]).astype(jnp.int32)


def _cast_into_slot(name, w, l, place):
    _, rows, cols = w.shape
    tr = min(256, rows)

    def body(p_ref, w_ref, o_ref):
        o_ref[...] = w_ref[...].astype(BF16)

    return pl.pallas_call(
        body, name=name,
        grid_spec=pltpu.PrefetchScalarGridSpec(
            num_scalar_prefetch=1, grid=(rows // tr,),
            in_specs=[pl.BlockSpec((None, tr, cols), lambda i, p: (l, i, 0))],
            out_specs=pl.BlockSpec((None, tr, cols), lambda i, p: (p[0], i, 0))),
        out_shape=jax.ShapeDtypeStruct((N_CHIPS, rows, cols), BF16),
        compiler_params=_params(("parallel",)),
    )(place, w)


HBM = pl.BlockSpec(memory_space=pltpu.HBM)
SEM = pl.BlockSpec(memory_space=pltpu.SEMAPHORE)
DATAFLOW = pltpu.SideEffectType.DATAFLOW_SIDE_EFFECTING


def _in_hbm(a):
    return pltpu.with_memory_space_constraint(a, pltpu.HBM)


def _chip_copies_start(name, srcs, lands, make_copy, after=None):
    n_t = len(srcs)
    in_place = lands is None
    n_after = 0 if after is None else 1

    def body(*refs):
        src = refs[:n_t]
        k = (n_t if in_place else 2 * n_t) + n_after
        send_sems, recv_sems = refs[k], refs[k + 1]
        land = refs[k + 2:k + 2 + n_t] if in_place else refs[k + 2 + n_t:k + 2 + 2 * n_t]
        token = refs[-1]
        x, y, c = _place()
        me = 2 * x + y
        for t in range(n_t):
            for px, py, pk in _other_chips(x, y):
                s, d = make_copy(src[t], land[t], me, pk, c)
                pltpu.make_async_remote_copy(
                    src_ref=s, dst_ref=d, send_sem=send_sems.at[t], recv_sem=recv_sems.at[t],
                    device_id=(px, py, c), device_id_type=MESH).start()
        token[...] = jnp.zeros_like(token)

    bufs = list(srcs) if in_place else list(srcs) + list(lands)
    outs = pl.pallas_call(
        body, name=name,
        in_specs=[HBM] * len(bufs) + [ANY] * n_after,
        out_specs=[SEM, SEM] + [HBM] * len(bufs) + [pl.BlockSpec(memory_space=pltpu.VMEM)],
        out_shape=[pltpu.SemaphoreType.DMA((n_t,)), pltpu.SemaphoreType.DMA((n_t,))]
        + [pltpu.HBM(b.shape, b.dtype) for b in bufs] + [jax.ShapeDtypeStruct((8, 128), F32)],
        input_output_aliases={i: 2 + i for i in range(len(bufs))},
        compiler_params=pltpu.CompilerParams(has_side_effects=DATAFLOW),
    )(*[_in_hbm(b) for b in bufs], *([] if after is None else [after]))
    return outs[0], outs[1], list(outs[2:2 + len(bufs)]), outs[-1]


def _chip_copies_wait(name, send_sems, recv_sems, bufs, sent, landed, after):
    n_b = len(bufs)

    def body(*refs):
        buf = refs[:n_b]
        send_ref, recv_ref = refs[n_b], refs[n_b + 1]
        x, y, c = _place()
        for t, (s, d) in enumerate(zip(sent(buf), landed(buf))):
            out = pltpu.make_async_remote_copy(src_ref=s, dst_ref=s, send_sem=send_ref.at[t], recv_sem=recv_ref.at[t],
                                               device_id=(x, y, c), device_id_type=MESH)
            out.wait_send()
            arrived = pltpu.make_async_remote_copy(src_ref=d, dst_ref=d, send_sem=send_ref.at[t],
                                                   recv_sem=recv_ref.at[t], device_id=(x, y, c), device_id_type=MESH)
            arrived.wait_recv()

    after = list(after) if isinstance(after, (list, tuple)) else [after]
    return pl.pallas_call(
        body, name=name,
        in_specs=[HBM] * n_b + [SEM, SEM] + [ANY] * len(after), out_specs=[HBM] * n_b,
        out_shape=[pltpu.HBM(b.shape, b.dtype) for b in bufs],
        input_output_aliases={i: i for i in range(n_b)},
        compiler_params=pltpu.CompilerParams(has_side_effects=DATAFLOW),
    )(*bufs, send_sems, recv_sems, *after)


def _gather_start(name, bufs, after=None):
    def make_copy(src, land, me, pk, c):
        hr = src.shape[1] // 2
        return src.at[me, pl.ds(c * hr, hr)], land.at[me, pl.ds(c * hr, hr)]

    return _chip_copies_start(name, bufs, None, make_copy, after)


def _gather_wait(name, send_sems, recv_sems, bufs, after):
    def three_halves(buf):
        return [b.at[pl.ds(0, 3), pl.ds(0, b.shape[1] // 2)] for b in buf]

    return _chip_copies_wait(name, send_sems, recv_sems, bufs, three_halves, three_halves, after)


def _gather_forward_start(name, bufs):
    n_t = len(bufs)

    def body(*refs):
        mine = refs[:n_t]
        send_sems, recv_sems = refs[n_t], refs[n_t + 1]
        buf = refs[n_t + 2:2 * n_t + 2]
        token = refs[-1]
        x, y, c = _place()
        for q in range(D2D_CHUNKS):
            for t in range(n_t):
                hr = mine[t].shape[1] // 2
                cr = hr // D2D_CHUNKS
                rows = pl.ds(c * hr + q * cr, cr)
                for _, _, pk in _other_chips(x, y):
                    pltpu.make_async_remote_copy(
                        src_ref=mine[t].at[pk, rows], dst_ref=buf[t].at[pk, rows], send_sem=send_sems.at[t],
                        recv_sem=recv_sems.at[t], device_id=(x, y, 1 - c), device_id_type=MESH).start()
        token[...] = jnp.zeros_like(token)

    outs = pl.pallas_call(
        body, name=name,
        in_specs=[HBM] * n_t,
        out_specs=[SEM, SEM] + [HBM] * n_t + [pl.BlockSpec(memory_space=pltpu.VMEM)],
        out_shape=[pltpu.SemaphoreType.DMA((n_t,)), pltpu.SemaphoreType.DMA((n_t,))]
        + [pltpu.HBM(b.shape, b.dtype) for b in bufs] + [jax.ShapeDtypeStruct((8, 128), F32)],
        input_output_aliases={i: 2 + i for i in range(n_t)},
        compiler_params=pltpu.CompilerParams(has_side_effects=DATAFLOW),
    )(*[_in_hbm(b) for b in bufs])
    return outs[0], outs[1], list(outs[2:2 + n_t]), outs[-1]


def _core_exchange_start(name, grads):
    n_t = len(grads)
    lands = [lax.empty((g.shape[0], g.shape[1] // 2, g.shape[2]), g.dtype) for g in grads]

    def body(*refs):
        src = refs[:n_t]
        send_sems, recv_sems = refs[2 * n_t], refs[2 * n_t + 1]
        land = refs[2 * n_t + 2 + n_t:2 * n_t + 2 + 2 * n_t]
        token = refs[-1]
        x, y, c = _place()
        for q in range(D2D_CHUNKS):
            for t in range(n_t):
                hr = src[t].shape[1] // 2
                cr = hr // D2D_CHUNKS
                pltpu.make_async_remote_copy(
                    src_ref=src[t].at[:, pl.ds((1 - c) * hr + q * cr, cr)], dst_ref=land[t].at[:, pl.ds(q * cr, cr)],
                    send_sem=send_sems.at[t], recv_sem=recv_sems.at[t], device_id=(x, y, 1 - c),
                    device_id_type=MESH).start()
        token[...] = jnp.zeros_like(token)

    bufs = list(grads) + lands
    outs = pl.pallas_call(
        body, name=name,
        in_specs=[HBM] * len(bufs),
        out_specs=[SEM, SEM] + [HBM] * len(bufs) + [pl.BlockSpec(memory_space=pltpu.VMEM)],
        out_shape=[pltpu.SemaphoreType.DMA((n_t,)), pltpu.SemaphoreType.DMA((n_t,))]
        + [pltpu.HBM(b.shape, b.dtype) for b in bufs] + [jax.ShapeDtypeStruct((8, 128), F32)],
        input_output_aliases={i: 2 + i for i in range(len(bufs))},
        compiler_params=pltpu.CompilerParams(has_side_effects=DATAFLOW),
    )(*[_in_hbm(b) for b in bufs])
    return outs[0], outs[1], list(outs[2:2 + len(bufs)]), outs[-1]


def _core_exchange_wait(name, send_sems, recv_sems, bufs, after):
    n_t = len(bufs) // 2

    def body(*refs):
        land = refs[n_t:2 * n_t]
        send_ref, recv_ref = refs[2 * n_t], refs[2 * n_t + 1]
        x, y, c = _place()
        for t in range(n_t):
            whole = pltpu.make_async_remote_copy(src_ref=land[t], dst_ref=land[t], send_sem=send_ref.at[t],
                                                 recv_sem=recv_ref.at[t], device_id=(x, y, c), device_id_type=MESH)
            whole.wait_send()
            whole.wait_recv()

    outs = pl.pallas_call(
        body, name=name,
        in_specs=[HBM] * (2 * n_t) + [SEM, SEM, ANY], out_specs=[HBM] * (2 * n_t),
        out_shape=[pltpu.HBM(b.shape, b.dtype) for b in bufs],
        input_output_aliases={i: i for i in range(2 * n_t)},
        compiler_params=pltpu.CompilerParams(has_side_effects=DATAFLOW),
    )(*bufs, send_sems, recv_sems, after)
    return list(outs[:n_t]), list(outs[n_t:])


def _add_to_bf16(name, full, theirs, place):
    chips, rows, cols = theirs.shape
    tr = min(256, rows)
    per = rows // tr

    def body(p_ref, a_ref, b_ref, o_ref):
        o_ref[...] = (a_ref[...] + b_ref[...]).astype(BF16)

    blk = pl.BlockSpec((None, tr, cols), lambda k, i, p: (k, i, 0))
    return pl.pallas_call(
        body, name=name,
        grid_spec=pltpu.PrefetchScalarGridSpec(
            num_scalar_prefetch=1, grid=(chips, per),
            in_specs=[pl.BlockSpec((None, tr, cols), lambda k, i, p: (k, p[1] * per + i, 0)), blk],
            out_specs=blk),
        out_shape=jax.ShapeDtypeStruct(theirs.shape, BF16), compiler_params=_params(("parallel",) * 2),
    )(place, full, theirs)


def _chip_exchange_start(name, parts):
    lands = [lax.empty(p.shape, p.dtype) for p in parts]
    return _chip_copies_start(name, parts, lands, lambda src, land, me, pk, c: (src.at[pk], land.at[me]))


def _chip_exchange_wait(name, send_sems, recv_sems, bufs, after):
    n_t = len(bufs) // 2
    return _chip_copies_wait(name, send_sems, recv_sems, bufs,
                             lambda buf: [b.at[pl.ds(0, 3)] for b in buf[:n_t]],
                             lambda buf: [b.at[pl.ds(0, 3)] for b in buf[n_t:]], after)


def _sum_chips(name, parts, landed, place, l, stacked):
    chips, rows, cols = landed.shape
    tr = min(256, rows)
    per = rows // tr

    def body(p_ref, own_ref, *refs):
        land, o_ref = refs[:chips], refs[-1]
        tot = None
        for k in range(chips):
            term = jnp.where(p_ref[0] == k, own_ref[...], land[k][...]).astype(F32)
            tot = term if tot is None else tot + term
        o_ref[...] = tot

    def from_chip(k):
        return pl.BlockSpec((None, tr, cols), lambda i, p: (jnp.where(p[0] == k, (k + 1) % chips, k), i, 0))

    in_specs = [pl.BlockSpec((None, tr, cols), lambda i, p: (p[0], i, 0))] + [from_chip(k) for k in range(chips)]
    args = [parts] + [landed] * chips
    aliases = {}
    if stacked is not None:
        in_specs.append(ANY)
        args.append(stacked)
        aliases = {len(args): 0}
    return pl.pallas_call(
        body, name=name,
        grid_spec=pltpu.PrefetchScalarGridSpec(
            num_scalar_prefetch=1, grid=(per,), in_specs=in_specs,
            out_specs=pl.BlockSpec((None, tr, cols), lambda i, p: (l, p[1] * per + i, 0))),
        out_shape=jax.ShapeDtypeStruct((DEPTH, 2 * rows, cols), F32), input_output_aliases=aliases,
        compiler_params=_params(("parallel",)),
    )(place, *args)


def _core_share_start(name, bufs, l):
    n_t = len(bufs)

    def body(*refs):
        mine = refs[:n_t]
        send_sems, recv_sems = refs[n_t], refs[n_t + 1]
        buf = refs[n_t + 2:2 * n_t + 2]
        token = refs[-1]
        x, y, c = _place()
        for q in range(D2D_CHUNKS):
            for t in range(n_t):
                hr = mine[t].shape[1] // 2
                cr = hr // D2D_CHUNKS
                rows = pl.ds(c * hr + q * cr, cr)
                pltpu.make_async_remote_copy(
                    src_ref=mine[t].at[l, rows], dst_ref=buf[t].at[l, rows], send_sem=send_sems.at[t],
                    recv_sem=recv_sems.at[t], device_id=(x, y, 1 - c), device_id_type=MESH).start()
        token[...] = jnp.zeros_like(token)

    outs = pl.pallas_call(
        body, name=name,
        in_specs=[HBM] * n_t,
        out_specs=[SEM, SEM] + [HBM] * n_t + [pl.BlockSpec(memory_space=pltpu.VMEM)],
        out_shape=[pltpu.SemaphoreType.DMA((n_t,)), pltpu.SemaphoreType.DMA((n_t,))]
        + [pltpu.HBM(b.shape, b.dtype) for b in bufs] + [jax.ShapeDtypeStruct((8, 128), F32)],
        input_output_aliases={i: 2 + i for i in range(n_t)},
        compiler_params=pltpu.CompilerParams(has_side_effects=DATAFLOW),
    )(*[_in_hbm(b) for b in bufs])
    return outs[0], outs[1], list(outs[2:2 + n_t]), outs[-1]


def _core_share_wait(name, send_sems, recv_sems, bufs, l, after):
    def half_layer(buf):
        return [b.at[l, pl.ds(0, b.shape[1] // 2)] for b in buf]

    return _chip_copies_wait(name, send_sems, recv_sems, bufs, half_layer, half_layer, after)


def _all_reduce_small(vec, after=None):
    rows, lanes = vec.shape
    hr = rows // 2

    def body(v_ref, *refs):
        o_ref, sib_ref, chips_ref, send_sems, recv_sems = refs[-5:]
        x, y, c = _place()
        me = 2 * x + y
        sibling = (x, y, 1 - c)
        mine = pl.ds(pl.multiple_of(c * hr, 8), hr)
        theirs = pl.ds(pl.multiple_of((1 - c) * hr, 8), hr)
        swap = pltpu.make_async_remote_copy(
            src_ref=v_ref.at[theirs], dst_ref=sib_ref, send_sem=send_sems.at[0], recv_sem=recv_sems.at[0],
            device_id=sibling, device_id_type=MESH)
        swap.start()
        swap.wait_recv()
        chips_ref[me] = v_ref[mine] + sib_ref[...]
        copies = []
        for j, (px, py, pk) in enumerate(_other_chips(x, y)):
            cp = pltpu.make_async_remote_copy(
                src_ref=chips_ref.at[me], dst_ref=chips_ref.at[me], send_sem=send_sems.at[1 + j],
                recv_sem=recv_sems.at[1 + j], device_id=(px, py, c), device_id_type=MESH)
            cp.start()
            copies.append(cp)
        for j, (px, py, pk) in enumerate(_other_chips(x, y)):
            pltpu.make_async_remote_copy(
                src_ref=chips_ref.at[pk], dst_ref=chips_ref.at[pk], send_sem=send_sems.at[1 + j],
                recv_sem=recv_sems.at[1 + j], device_id=(px, py, c), device_id_type=MESH).wait_recv()
        tot = chips_ref[0]
        for k in range(1, N_CHIPS):
            tot = tot + chips_ref[k]
        o_ref[mine] = tot
        share = pltpu.make_async_remote_copy(
            src_ref=o_ref.at[mine], dst_ref=o_ref.at[mine], send_sem=send_sems.at[4], recv_sem=recv_sems.at[4],
            device_id=sibling, device_id_type=MESH)
        share.start()
        pltpu.make_async_remote_copy(
            src_ref=o_ref.at[theirs], dst_ref=o_ref.at[theirs], send_sem=send_sems.at[4], recv_sem=recv_sems.at[4],
            device_id=sibling, device_id_type=MESH).wait_recv()
        swap.wait_send()
        for cp in copies:
            cp.wait_send()
        share.wait_send()

    vm = pl.BlockSpec(memory_space=pltpu.VMEM)
    return pl.pallas_call(
        body, name="small_all_reduce", in_specs=[vm] + ([] if after is None else [ANY]), out_specs=vm,
        out_shape=jax.ShapeDtypeStruct((rows, lanes), F32),
        scratch_shapes=[pltpu.VMEM((hr, lanes), F32), pltpu.VMEM((N_CHIPS, hr, lanes), F32),
                        pltpu.SemaphoreType.DMA((5,)), pltpu.SemaphoreType.DMA((5,))],
        compiler_params=pltpu.CompilerParams(has_side_effects=True, vmem_limit_bytes=48 * MIB),
    )(vec, *([] if after is None else [after]))


def _adamw(name, w, g, m, v, place, l=0, half=None, done=None, after=None):
    layers, rows, cols = w.shape
    span = rows if half is None else rows // 2
    tr = span
    for cand in (256, 128, 64, 32, 16, 8):
        if span % cand == 0:
            tr = cand
            break
    per = span // tr
    c1 = 1.0 - ADAM_B1 ** ADAM_STEP
    c2 = 1.0 - ADAM_B2 ** ADAM_STEP

    def first_block(p):
        return 0 if half is None else (p[1] if half == "own" else 1 - p[1]) * per

    def body(p_ref, w_ref, g_ref, m_ref, v_ref, *refs):
        go_ref, d_ref, nm_ref, nv_ref = refs[-4:]
        gv = g_ref[...]
        nm = ADAM_B1 * m_ref[...] + (1.0 - ADAM_B1) * gv
        nv = ADAM_B2 * v_ref[...] + (1.0 - ADAM_B2) * (gv * gv)
        go_ref[...] = gv
        nm_ref[...] = nm
        nv_ref[...] = nv
        d_ref[...] = -ADAM_LR * ((nm / c1) / (jnp.sqrt(nv / c2) + ADAM_EPS) + ADAM_WD * w_ref[...])

    blk = pl.BlockSpec((None, tr, cols), lambda i, p: (l, first_block(p) + i, 0))
    out = jax.ShapeDtypeStruct((layers, rows, cols), F32)
    extra = ([] if done is None else list(done)) + ([] if after is None else [after])
    aliases = {} if done is None else {5 + i: i for i in range(4)}
    return pl.pallas_call(
        body, name=name,
        grid_spec=pltpu.PrefetchScalarGridSpec(
            num_scalar_prefetch=1, grid=(per,), in_specs=[blk] * 4 + [ANY] * len(extra), out_specs=[blk] * 4),
        out_shape=[out] * 4, input_output_aliases=aliases,
        compiler_params=_params(("parallel",)),
    )(place, w, g, m, v, *extra)


def _pack_small(parts):
    flat = jnp.concatenate([parts[k].reshape(-1) for k in SMALL_NAMES])
    n = flat.shape[0]
    rows = -(-n // (256 * 128)) * 256
    return jnp.pad(flat, (0, rows * 128 - n)).reshape(rows, 128)


def _unpack_small(packed, like):
    flat = packed.reshape(-1)
    out, off = {}, 0
    for k in SMALL_NAMES:
        n = like[k].size
        out[k] = flat[off:off + n].reshape(like[k].shape)
        off += n
    return out


WEIGHT_ORDER = ("norm_g", "w_in", "sgu_ln_g", "sgu_ln_b", "sgu_w", "sgu_b", "mem_norm_g", "w_mem_kv", "q_norm_g",
                "k_norm_g", "w_out")


def kernel(x, mem, norm_g, w_in, sgu_ln_g, sgu_ln_b, sgu_w, sgu_b, mem_norm_g, w_mem_kv, q_norm_g, k_norm_g, w_out, loss_target, m_norm_g, m_w_in, m_sgu_ln_g, m_sgu_ln_b, m_sgu_w, m_sgu_b, m_mem_norm_g, m_w_mem_kv, m_q_norm_g, m_k_norm_g, m_w_out, v_norm_g, v_w_in, v_sgu_ln_g, v_sgu_ln_b, v_sgu_w, v_sgu_b, v_mem_norm_g, v_w_mem_kv, v_q_norm_g, v_k_norm_g, v_w_out):
    weights = dict(norm_g=norm_g, w_in=w_in, sgu_ln_g=sgu_ln_g, sgu_ln_b=sgu_ln_b, sgu_w=sgu_w, sgu_b=sgu_b,
                   mem_norm_g=mem_norm_g, w_mem_kv=w_mem_kv, q_norm_g=q_norm_g, k_norm_g=k_norm_g, w_out=w_out)
    mom_m = dict(norm_g=m_norm_g, w_in=m_w_in, sgu_ln_g=m_sgu_ln_g, sgu_ln_b=m_sgu_ln_b, sgu_w=m_sgu_w, sgu_b=m_sgu_b,
                 mem_norm_g=m_mem_norm_g, w_mem_kv=m_w_mem_kv, q_norm_g=m_q_norm_g, k_norm_g=m_k_norm_g, w_out=m_w_out)
    mom_v = dict(norm_g=v_norm_g, w_in=v_w_in, sgu_ln_g=v_sgu_ln_g, sgu_ln_b=v_sgu_ln_b, sgu_w=v_sgu_w, sgu_b=v_sgu_b,
                 mem_norm_g=v_mem_norm_g, w_mem_kv=v_w_mem_kv, q_norm_g=v_q_norm_g, k_norm_g=v_k_norm_g, w_out=v_w_out)
    big = ("w_in", "w_mem_kv", "w_out")
    sm = {k: weights[k] for k in SMALL_NAMES}

    place = _place_index()
    xs, mems, target = x[0], mem[0], loss_target[0]

    slots = [[_cast_into_slot(f"cast_{k}_{l}", weights[k], l, place) for k in big] for l in range(DEPTH)]
    saved = [None] * DEPTH

    chips, cores = {}, {}

    def start_gather(l, after=None):
        chips[l, "in"] = _gather_start(f"gather_start_{l}_in", slots[l][:1], after)
        chips[l, "rest"] = _gather_start(f"gather_start_{l}_rest", slots[l][1:], chips[l, "in"][3])
        return chips[l, "rest"][3]

    def hand_to_sibling(l, group, after):
        send_sems, recv_sems, bufs, _ = chips[l, group]
        bufs = _gather_wait(f"gather_wait_{l}_{group}", send_sems, recv_sems, bufs, after)
        cores[l, group] = _gather_forward_start(f"gather_forward_{l}_{group}", bufs)
        return cores[l, group][3]

    def whole(l, group, after):
        send_sems, recv_sems, bufs, _ = cores[l, group]
        return _gather_wait(f"gather_whole_{l}_{group}", send_sems, recv_sems, bufs, after)

    class Gathered:
        def __init__(self, l):
            self.l = l

        def w_in(self, h):
            return whole(self.l, "in", h)[0]

        def rest_start(self, proj):
            token = hand_to_sibling(self.l, "rest", proj)
            return start_gather(self.l + 1, token) if self.l + 1 < DEPTH else token

        def rest_finish(self, o_b):
            w_kv_all, w_out_all = whole(self.l, "rest", o_b)
            return w_kv_all, w_out_all, None

        def before_out(self, y):
            return hand_to_sibling(self.l + 1, "in", y) if self.l + 1 < DEPTH else None

    hand_to_sibling(0, "in", [start_gather(0)] + [s for layer in slots[1:] for s in layer])
    cur = xs
    for l in range(DEPTH):
        cur, saved[l] = _layer_fwd(l, cur, mems, sm, Gathered(l))
    dxo, dxo_b, loss_part = _loss_and_grad("loss", cur, target, min(256, xs.shape[0]))
    loss = lax.psum(loss_part[0, 0], ("x", "y", "c"))

    small_g = [None] * DEPTH
    flight = {}
    for l in reversed(range(DEPTH)):
        def start_exchange(group, full, l=l):
            send_sems, recv_sems, bufs, token = _core_exchange_start(f"grad_core_start_{l}_{group}", full)

            def finish(after):
                grads_l, theirs = _core_exchange_wait(f"grad_core_wait_{l}_{group}", send_sems, recv_sems, bufs, after)
                parts = [_add_to_bf16(f"grad_core_sum_{l}_{group}_{t}", g, th, place)
                         for t, (g, th) in enumerate(zip(grads_l, theirs))]
                *flight[l, group], token = _chip_exchange_start(f"grad_chip_start_{l}_{group}", parts)
                return token

            return token, finish

        dxo, dxo_b, small_g[l], *_ = _layer_bwd(l, dxo, dxo_b, mems, sm, saved[l], on_weight_grads=start_exchange)
    grad_x = dxo

    groups = (("out", ("w_out",)), ("in", ("w_in", "w_mem_kv")))
    halves, stepped = dict.fromkeys(big), dict.fromkeys(big)
    small_g = {k: jnp.stack([small_g[l][k] for l in range(DEPTH)]) for k in SMALL_NAMES}
    after = grad_x
    for l in reversed(range(DEPTH)):
        for group, names in groups:
            send_sems, recv_sems, bufs = flight[l, group]
            bufs = _chip_exchange_wait(f"grad_chip_wait_{l}_{group}", send_sems, recv_sems, bufs, after)
            for t, k in enumerate(names):
                halves[k] = _sum_chips(f"grad_chip_sum_{l}_{k}", bufs[t], bufs[len(names) + t], place, l, halves[k])
        send_sems, recv_sems, bufs, after = _core_share_start(f"grad_core_share_{l}", [halves[k] for k in big], l)
        for k, buf in zip(big, bufs):
            stepped[k] = _adamw(f"adamw_{k}_{l}_own", weights[k], buf, mom_m[k], mom_v[k], place, l, "own",
                                stepped[k], after)
            after = stepped[k][1]
        bufs = _core_share_wait(f"grad_core_shared_{l}", send_sems, recv_sems, bufs, l, after)
        for k, buf in zip(big, bufs):
            halves[k] = buf
            stepped[k] = _adamw(f"adamw_{k}_{l}_other", weights[k], buf, mom_m[k], mom_v[k], place, l, "other",
                                stepped[k], after)
            after = stepped[k][1]
        if l == DEPTH - 1:
            small_sum = _all_reduce_small(_pack_small(small_g), after)
            packed = [a[None] for a in (_pack_small(sm), small_sum, _pack_small({k: mom_m[k] for k in SMALL_NAMES}),
                                        _pack_small({k: mom_v[k] for k in SMALL_NAMES}))]
            small_step = _adamw("adamw_small", *packed, place)

    grads, delta, new_m, new_v = ({k: stepped[k][i] for k in big} for i in range(4))
    for out, packed in zip((grads, delta, new_m, new_v), small_step):
        out.update(_unpack_small(packed[0], sm))
    return (loss, grad_x[None], *[grads[k] for k in WEIGHT_ORDER], *[delta[k] for k in WEIGHT_ORDER],
            *[new_m[k] for k in WEIGHT_ORDER], *[new_v[k] for k in WEIGHT_ORDER])
```

```python
import functools
import math

import jax
import jax.numpy as jnp
from jax import lax
from jax.experimental import pallas as pl
from jax.experimental.pallas import tpu as pltpu

F32 = jnp.float32
BF16 = jnp.bfloat16
MESH = pl.DeviceIdType.MESH

D_MODEL = 2048
DEPTH = 2
CHUNK = 128
D_A = 1024
A_GROUPS = 8
D_B = 512
D_C = 512
HEADS = 4
HEAD_DIM = 128
IN_WIDTH = 6144
N_CHIPS = 4
EPS = 1e-6
ATT_SCALE = 1.0 / math.sqrt(HEAD_DIM)

OFF_U, OFF_V, OFF_ZA = 0, 1024, 2048
OFF_QB, OFF_KB, OFF_VB, OFF_ZB = 3072, 3584, 4096, 4608
OFF_QC, OFF_ZC = 5120, 5632
OFF_YB, OFF_YC = 1024, 1536

ADAM_LR = 0.001
ADAM_B1 = 0.9
ADAM_B2 = 0.999
ADAM_EPS = 1e-08
ADAM_WD = 0.01
ADAM_STEP = 10

MIB = 1024 * 1024
ANY = pl.BlockSpec(memory_space=pl.ANY)


def _params(semantics=None, vmem_mb=48):
    return pltpu.CompilerParams(dimension_semantics=semantics, vmem_limit_bytes=vmem_mb * MIB)


def _gelu(x):
    return 0.5 * x * (1.0 + lax.erf(x * (1.0 / math.sqrt(2.0))))


def _gelu_grad(x):
    cdf = 0.5 * (1.0 + lax.erf(x * (1.0 / math.sqrt(2.0))))
    pdf = jnp.exp(-0.5 * x * x) * (1.0 / math.sqrt(2.0 * math.pi))
    return cdf + x * pdf


def _sigmoid(x):
    return 1.0 / (1.0 + jnp.exp(-x))


def _silu_and_grad(z):
    s = _sigmoid(z)
    return z * s, s * (1.0 + z * (1.0 - s))


def _split_bf16(x):
    hi = x.astype(BF16)
    lo = (x - hi.astype(F32)).astype(BF16)
    return hi, lo


def _dot(a, b, dims):
    return lax.dot_general(a, b, (dims, ((), ())), preferred_element_type=F32)


NN = ((1,), (0,))
NT = ((1,), (1,))
TN = ((0,), (0,))


def _matmul(name, a, b, *, grid, a_spec, b_spec, o_spec, out_shape, dims, res=None, res_spec=None, after=None,
            place=None, vmem_mb=48):
    nk = grid[2]
    n_in = 2 + (res is not None) + (after is not None)

    def body(*refs):
        if place is not None:
            refs = refs[1:]
        a_ref, b_ref = refs[0], refs[1]
        r_ref = refs[2] if res is not None else None
        o_ref = refs[n_in]
        if len(b_ref.shape) == 3 and dims == NN:
            part = _dot(a_ref[...], b_ref[...].reshape(-1, b_ref.shape[-1]), dims)
        elif len(b_ref.shape) == 3:
            width = b_ref.shape[-1]
            part = None
            for s in range(b_ref.shape[0]):
                term = _dot(a_ref[:, s * width:(s + 1) * width], b_ref[s], dims)
                part = term if part is None else part + term
        else:
            part = _dot(a_ref[...], b_ref[...], dims)
        if nk == 1:
            if r_ref is not None:
                part = part + r_ref[...]
            o_ref[...] = part.astype(o_ref.dtype)
            return
        acc_ref = refs[n_in + 1]
        k = pl.program_id(2)

        @pl.when(k == 0)
        def _():
            acc_ref[...] = part

        @pl.when(k > 0)
        def _():
            acc_ref[...] += part

        @pl.when(k == nk - 1)
        def _():
            tot = acc_ref[...]
            if r_ref is not None:
                tot = tot + r_ref[...]
            o_ref[...] = tot.astype(o_ref.dtype)

    in_specs = [a_spec, b_spec]
    args = [a, b]
    if res is not None:
        in_specs.append(res_spec)
        args.append(res)
    if after is not None:
        in_specs.append(ANY)
        args.append(after)
    acc_shape = tuple(d for d in o_spec.block_shape if d is not None)
    scratch = [pltpu.VMEM(acc_shape, F32)] if nk > 1 else []
    params = _params(("parallel", "parallel", "arbitrary"), vmem_mb)
    if place is not None:
        return pl.pallas_call(
            body, name=name, out_shape=out_shape, compiler_params=params,
            grid_spec=pltpu.PrefetchScalarGridSpec(num_scalar_prefetch=1, grid=grid, in_specs=in_specs,
                                                   out_specs=o_spec, scratch_shapes=scratch),
        )(place, *args)
    return pl.pallas_call(
        body, name=name, grid=grid, in_specs=in_specs, out_specs=o_spec, out_shape=out_shape,
        scratch_shapes=scratch, compiler_params=params,
    )(*args)


def _rms_fwd(name, x, g, tr, after=None, transposed=False):
    rows, d = x.shape

    def body(x_ref, g_ref, *refs):
        outs = refs[1:] if after is not None else refs
        xv = x_ref[...]
        r = lax.rsqrt(jnp.mean(xv * xv, axis=-1, keepdims=True) + EPS)
        h = xv * r * g_ref[...]
        outs[0][...] = h.astype(BF16)
        if transposed:
            outs[1][...] = h.T.astype(BF16)

    out_specs = [pl.BlockSpec((tr, d), lambda i: (i, 0))]
    out_shape = [jax.ShapeDtypeStruct((rows, d), BF16)]
    if transposed:
        out_specs.append(pl.BlockSpec((d, tr), lambda i: (0, i)))
        out_shape.append(jax.ShapeDtypeStruct((d, rows), BF16))
    outs = pl.pallas_call(
        body, name=name, grid=(rows // tr,),
        in_specs=[pl.BlockSpec((tr, d), lambda i: (i, 0)), pl.BlockSpec((1, d), lambda i: (0, 0))]
        + ([] if after is None else [ANY]),
        out_specs=out_specs, out_shape=out_shape,
        compiler_params=_params(("parallel",)),
    )(x, g, *([] if after is None else [after]))
    return outs if transposed else outs[0]


def _rms_bwd(name, x, dh, dres, g, tr, after=None):
    rows, d = x.shape

    def body(x_ref, dh_ref, dres_ref, g_ref, *refs):
        dx_ref, dxb_ref, dg_ref = refs[-3:]
        xv = x_ref[...]
        r = lax.rsqrt(jnp.mean(xv * xv, axis=-1, keepdims=True) + EPS)
        xhat = xv * r
        dhv = dh_ref[...]
        dxh = dhv * g_ref[...]
        dx = r * (dxh - xhat * jnp.mean(dxh * xhat, axis=-1, keepdims=True)) + dres_ref[...]
        dx_ref[...] = dx
        dxb_ref[...] = dx.astype(BF16)
        part = jnp.sum(dhv * xhat, axis=0, keepdims=True)

        @pl.when(pl.program_id(0) == 0)
        def _():
            dg_ref[...] = part

        @pl.when(pl.program_id(0) > 0)
        def _():
            dg_ref[...] += part

    blk = pl.BlockSpec((tr, d), lambda i: (i, 0))
    vec = pl.BlockSpec((1, d), lambda i: (0, 0))
    return pl.pallas_call(
        body, name=name, grid=(rows // tr,), in_specs=[blk, blk, blk, vec] + ([] if after is None else [ANY]),
        out_specs=[blk, blk, vec],
        out_shape=[jax.ShapeDtypeStruct((rows, d), F32), jax.ShapeDtypeStruct((rows, d), BF16),
                   jax.ShapeDtypeStruct((1, d), F32)],
        compiler_params=_params(("arbitrary",)),
    )(x, dh, dres, g, *([] if after is None else [after]))


def _rms_gain_grad(name, x, dh):
    rows, d = x.shape

    def body(x_ref, dh_ref, dg_ref):
        xv = x_ref[...]
        r = lax.rsqrt(jnp.mean(xv * xv, axis=-1, keepdims=True) + EPS)
        dg_ref[...] = jnp.sum(dh_ref[...] * xv * r, axis=0, keepdims=True)

    return pl.pallas_call(
        body, name=name, out_shape=jax.ShapeDtypeStruct((1, d), F32), compiler_params=_params(None),
    )(x, dh)


def _loss_and_grad(name, y, target, tr):
    rows, d = y.shape
    n = rows // tr

    def body(y_ref, t_ref, dx_ref, dxb_ref, loss_ref, acc_ref):
        e = y_ref[...] - t_ref[...]
        dx = e * (1.0 / d)
        dx_ref[...] = dx
        dxb_ref[...] = dx.astype(BF16)
        part = jnp.sum(e * e, axis=0, keepdims=True)
        i = pl.program_id(0)

        @pl.when(i == 0)
        def _():
            acc_ref[...] = part

        @pl.when(i > 0)
        def _():
            acc_ref[...] += part

        @pl.when(i == n - 1)
        def _():
            loss_ref[...] = jnp.sum(acc_ref[...], axis=-1, keepdims=True) * (0.5 / d)

    blk = pl.BlockSpec((tr, d), lambda i: (i, 0))
    return pl.pallas_call(
        body, name=name, grid=(n,), in_specs=[blk, blk],
        out_specs=[blk, blk, pl.BlockSpec((1, 1), lambda i: (0, 0))],
        out_shape=[jax.ShapeDtypeStruct((rows, d), F32), jax.ShapeDtypeStruct((rows, d), BF16),
                   jax.ShapeDtypeStruct((1, 1), F32)],
        scratch_shapes=[pltpu.VMEM((1, d), F32)],
        compiler_params=_params(("arbitrary",)),
    )(y, target)


SB_T = 256
SB_HEADS = 4


def _sb_scores(q, kblk):
    z = _dot(q, kblk, NT) * ATT_SCALE
    e = jnp.exp(-jnp.abs(z))
    sp = jnp.log1p(e)
    lb = jnp.minimum(z, 0.0) - sp
    l1 = lb - z
    return z, e, lb, l1


def _sb_fwd(name, proj, after=None):
    s_len = proj.shape[0]
    t = SB_T
    nq = s_len // t

    def body(q_ref, k_ref, v_ref, *refs):
        o_ref = refs[-1]
        i = pl.program_id(1)
        row = lax.broadcasted_iota(jnp.int32, (t, t), 0)
        col = lax.broadcasted_iota(jnp.int32, (t, t), 1)
        causal = col < row
        after_mat = (row > col).astype(BF16)
        heads = [slice(hh * HEAD_DIM, (hh + 1) * HEAD_DIM) for hh in range(SB_HEADS)]
        q = [q_ref[:, sl].astype(BF16) for sl in heads]

        def tile(kb, state, masked):
            start = pl.multiple_of(kb * t, t)
            out = []
            for hh, sl in enumerate(heads):
                carry, acc = state[hh]
                kblk = k_ref[pl.ds(start, t), sl].astype(BF16)
                vblk = v_ref[pl.ds(start, t), sl].astype(BF16)
                _, _, lb, l1 = _sb_scores(q[hh], kblk)
                if masked:
                    l1 = jnp.where(causal, l1, 0.0)
                hi, lo = _split_bf16(l1)
                after = _dot(hi, after_mat, NN) + _dot(lo, after_mat, NN) + carry
                a = jnp.exp(lb + after)
                if masked:
                    a = jnp.where(causal, a, 0.0)
                acc = acc + _dot(a.astype(BF16), vblk, NN)
                carry = carry + jnp.sum(l1, axis=-1, keepdims=True)
                out.append((carry, acc))
            return tuple(out)

        zero = (jnp.zeros((t, 1), F32), jnp.zeros((t, HEAD_DIM), F32))
        state = tile(i, (zero,) * SB_HEADS, True)
        state = lax.fori_loop(0, i, lambda n, st: tile(i - 1 - n, st, False), state)
        for hh, sl in enumerate(heads):
            o_ref[:, sl] = state[hh][1]

    cb = SB_HEADS * HEAD_DIM
    return pl.pallas_call(
        body, name=name, grid=(HEADS // SB_HEADS, nq),
        in_specs=[pl.BlockSpec((t, cb), lambda h, i: (i, OFF_QB // cb + h)),
                  pl.BlockSpec((s_len, cb), lambda h, i: (0, OFF_KB // cb + h)),
                  pl.BlockSpec((s_len, cb), lambda h, i: (0, OFF_VB // cb + h))] + ([] if after is None else [ANY]),
        out_specs=pl.BlockSpec((t, cb), lambda h, i: (i, h)),
        out_shape=jax.ShapeDtypeStruct((s_len, D_B), F32),
        compiler_params=_params(("parallel", "arbitrary")),
    )(proj, proj, proj, *([] if after is None else [after]))


def _sb_bwd(name, proj, dy, after=None):
    s_len = proj.shape[0]
    t = SB_T
    nq = s_len // t

    def body(q_ref, k_ref, v_ref, z_ref, dy_ref, *refs):
        dq_ref, dk_ref, dv_ref, a_ref, s_ref = refs[-5:]
        i = pl.program_id(1)

        @pl.when(i == 0)
        def _():
            dk_ref[...] = jnp.zeros_like(dk_ref)
            dv_ref[...] = jnp.zeros_like(dv_ref)

        heads = [slice(hh * HEAD_DIM, (hh + 1) * HEAD_DIM) for hh in range(SB_HEADS)]
        q = [q_ref[:, sl].astype(BF16) for sl in heads]
        silu_z, _ = _silu_and_grad(z_ref[...])
        do_all = dy_ref[...] * silu_z
        do_b = [do_all[:, sl].astype(BF16) for sl in heads]
        row = lax.broadcasted_iota(jnp.int32, (t, t), 0)
        col = lax.broadcasted_iota(jnp.int32, (t, t), 1)
        causal = col < row
        after_mat = (row > col).astype(BF16)
        before_mat = (row < col).astype(BF16)

        def weights(kb, carries, masked):
            start = pl.multiple_of(kb * t, t)
            out = []
            for hh, sl in enumerate(heads):
                kblk = k_ref[pl.ds(start, t), sl].astype(BF16)
                z, _, lb, l1 = _sb_scores(q[hh], kblk)
                if masked:
                    l1 = jnp.where(causal, l1, 0.0)
                hi, lo = _split_bf16(l1)
                after = _dot(hi, after_mat, NN) + _dot(lo, after_mat, NN) + carries[hh]
                a = jnp.exp(lb + after)
                if masked:
                    a = jnp.where(causal, a, 0.0)
                a_ref[hh, kb] = a
                s_ref[hh, kb] = z
                out.append(carries[hh] + jnp.sum(l1, axis=-1, keepdims=True))
            return tuple(out)

        carries = weights(i, (jnp.zeros((t, 1), F32),) * SB_HEADS, True)
        lax.fori_loop(0, i, lambda n, c: weights(i - 1 - n, c, False), carries)

        def grads(kb, state, masked):
            start = pl.multiple_of(kb * t, t)
            out = []
            for hh, sl in enumerate(heads):
                carry, dq = state[hh]
                kblk = k_ref[pl.ds(start, t), sl].astype(BF16)
                vblk = v_ref[pl.ds(start, t), sl].astype(BF16)
                a = a_ref[hh, kb]
                z = s_ref[hh, kb]
                g = _dot(do_b[hh], vblk, NT) * a
                ghi, glo = _split_bf16(g)
                prefix = _dot(ghi, before_mat, NN) + _dot(glo, before_mat, NN) + carry
                e = jnp.exp(-jnp.abs(z))
                inv = 1.0 / (1.0 + e)
                pos = z >= 0.0
                beta = jnp.where(pos, inv, e * inv)
                one_m_beta = jnp.where(pos, e * inv, inv)
                dz = (g * one_m_beta - prefix * beta) * ATT_SCALE
                if masked:
                    dz = jnp.where(causal, dz, 0.0)
                dz_b = dz.astype(BF16)
                dq = dq + _dot(dz_b, kblk, NN)
                dk_ref[pl.ds(start, t), sl] += _dot(dz_b, q[hh], TN)
                dv_ref[pl.ds(start, t), sl] += _dot(a.astype(BF16), do_b[hh], TN)
                out.append((carry + jnp.sum(g, axis=-1, keepdims=True), dq))
            return tuple(out)

        zero = (jnp.zeros((t, 1), F32), jnp.zeros((t, HEAD_DIM), F32))
        state = lax.fori_loop(0, i, lambda kb, st: grads(kb, st, False), (zero,) * SB_HEADS)
        state = grads(i, state, True)
        for hh, sl in enumerate(heads):
            dq_ref[:, sl] = state[hh][1]

    cb = SB_HEADS * HEAD_DIM
    qblk = lambda off: pl.BlockSpec((t, cb), lambda h, i: (i, off // cb + h))
    full = lambda off: pl.BlockSpec((s_len, cb), lambda h, i: (0, off // cb + h))
    out = jax.ShapeDtypeStruct((s_len, D_B), F32)
    return pl.pallas_call(
        body, name=name, grid=(HEADS // SB_HEADS, nq),
        in_specs=[qblk(OFF_QB), full(OFF_KB), full(OFF_VB), qblk(OFF_ZB), qblk(OFF_YB)]
        + ([] if after is None else [ANY]),
        out_specs=[qblk(0), full(0), full(0)],
        out_shape=[out, out, out],
        scratch_shapes=[pltpu.VMEM((SB_HEADS, nq, t, t), F32), pltpu.VMEM((SB_HEADS, nq, t, t), F32)],
        compiler_params=_params(("parallel", "arbitrary")),
    )(proj, proj, proj, proj, dy, *([] if after is None else [after]))


MEM_TQ = 512


def _qk_norm(x, g):
    r = lax.rsqrt(jnp.mean(x * x, axis=-1, keepdims=True) + EPS)
    xhat = x * r
    return xhat * g, xhat, r


def _qk_norm_bwd(dn, g, xhat, r):
    dxh = dn * g
    return r * (dxh - xhat * jnp.mean(dxh * xhat, axis=-1, keepdims=True))


def _mem_probs(q, mk, qg, kg):
    qn, qhat, rq = _qk_norm(q, qg)
    kn, khat, rk = _qk_norm(mk, kg)
    qn_b, kn_b = qn.astype(BF16), kn.astype(BF16)
    s = _dot(qn_b, kn_b, NT) * ATT_SCALE
    p = jnp.exp(s - jnp.max(s, axis=-1, keepdims=True))
    p = p / jnp.sum(p, axis=-1, keepdims=True)
    return p, qn_b, kn_b, qhat, rq, khat, rk


def _mem_fwd(name, proj, mem_kv, qg, kg):
    s_len = proj.shape[0]
    m_len = mem_kv.shape[0]
    tq = min(MEM_TQ, s_len)

    def body(q_ref, mk_ref, mv_ref, qg_ref, kg_ref, o_ref):
        p = _mem_probs(q_ref[...], mk_ref[...], qg_ref[...], kg_ref[...])[0]
        o_ref[...] = _dot(p.astype(BF16), mv_ref[...].astype(BF16), NN)

    cb = HEAD_DIM
    vec = pl.BlockSpec((1, cb), lambda h, i: (0, 0))
    return pl.pallas_call(
        body, name=name, grid=(HEADS, s_len // tq),
        in_specs=[pl.BlockSpec((tq, cb), lambda h, i: (i, OFF_QC // cb + h)),
                  pl.BlockSpec((m_len, cb), lambda h, i: (0, h)),
                  pl.BlockSpec((m_len, cb), lambda h, i: (0, HEADS + h)), vec, vec],
        out_specs=pl.BlockSpec((tq, cb), lambda h, i: (i, h)),
        out_shape=jax.ShapeDtypeStruct((s_len, D_C), F32),
        compiler_params=_params(("parallel", "parallel")),
    )(proj, mem_kv, mem_kv, qg, kg)


def _mem_bwd(name, proj, mem_kv, qg, kg, dy):
    s_len = proj.shape[0]
    m_len = mem_kv.shape[0]
    tq = min(MEM_TQ, s_len)

    def body(q_ref, mk_ref, mv_ref, qg_ref, kg_ref, z_ref, dy_ref, dq_ref, dmk_ref, dmv_ref, dqg_ref, dkg_ref):
        h, i = pl.program_id(0), pl.program_id(1)

        @pl.when(i == 0)
        def _():
            dmk_ref[...] = jnp.zeros_like(dmk_ref)
            dmv_ref[...] = jnp.zeros_like(dmv_ref)

        @pl.when((i == 0) & (h == 0))
        def _():
            dqg_ref[...] = jnp.zeros_like(dqg_ref)
            dkg_ref[...] = jnp.zeros_like(dkg_ref)

        qg, kg = qg_ref[...], kg_ref[...]
        p, qn_b, kn_b, qhat, rq, khat, rk = _mem_probs(q_ref[...], mk_ref[...], qg, kg)
        silu_z, _ = _silu_and_grad(z_ref[...])
        do_b = (dy_ref[...] * silu_z).astype(BF16)
        dmv_ref[...] += _dot(p.astype(BF16), do_b, TN)
        dp = _dot(do_b, mv_ref[...].astype(BF16), NT)
        ds = (p * (dp - jnp.sum(dp * p, axis=-1, keepdims=True)) * ATT_SCALE).astype(BF16)
        dqn = _dot(ds, kn_b, NN)
        dkn = _dot(ds, qn_b, TN)
        dq_ref[...] = _qk_norm_bwd(dqn, qg, qhat, rq)
        dmk_ref[...] += _qk_norm_bwd(dkn, kg, khat, rk)
        dqg_ref[...] += jnp.sum(dqn * qhat, axis=0, keepdims=True)
        dkg_ref[...] += jnp.sum(dkn * khat, axis=0, keepdims=True)

    cb = HEAD_DIM
    vec = pl.BlockSpec((1, cb), lambda h, i: (0, 0))
    qblk = lambda off: pl.BlockSpec((tq, cb), lambda h, i: (i, off // cb + h))
    memblk = lambda off: pl.BlockSpec((m_len, cb), lambda h, i: (0, off + h))
    return pl.pallas_call(
        body, name=name, grid=(HEADS, s_len // tq),
        in_specs=[qblk(OFF_QC), memblk(0), memblk(HEADS), vec, vec, qblk(OFF_ZC), qblk(OFF_YC)],
        out_specs=[qblk(0), memblk(0), memblk(0), vec, vec],
        out_shape=[jax.ShapeDtypeStruct((s_len, D_C), F32), jax.ShapeDtypeStruct((m_len, D_C), F32),
                   jax.ShapeDtypeStruct((m_len, D_C), F32), jax.ShapeDtypeStruct((1, cb), F32),
                   jax.ShapeDtypeStruct((1, cb), F32)],
        compiler_params=_params(("arbitrary", "arbitrary")),
    )(proj, mem_kv, mem_kv, qg, kg, proj, dy)


def _sgu_common(u_ref, v_ref, lng_ref, lnb_ref, w_ref, bias_ref):
    ug = _gelu(u_ref[...])
    vg = _gelu(v_ref[...])
    mu = jnp.mean(vg, axis=-1, keepdims=True)
    xc = vg - mu
    rstd = lax.rsqrt(jnp.mean(xc * xc, axis=-1, keepdims=True) + EPS)
    xhat = xc * rstd
    vn = xhat * lng_ref[...] + lnb_ref[...]
    vn_b = vn.astype(BF16)
    row = lax.broadcasted_iota(jnp.int32, (CHUNK, CHUNK), 0)
    col = lax.broadcasted_iota(jnp.int32, (CHUNK, CHUNK), 1)
    tril = row >= col
    mixed = []
    for g in range(A_GROUPS):
        w = jnp.where(tril, w_ref[g], 0.0).astype(BF16)
        sl = slice(g * CHUNK, (g + 1) * CHUNK)
        mixed.append(_dot(w, vn_b[:, sl], NN) + bias_ref[:, sl])
    return ug, xhat, rstd, vn_b, mixed, tril


def _gate_fwd(name, proj, o_b, o_c, lng, lnb, w_s, bias):
    s_len = proj.shape[0]

    def body(u_ref, v_ref, za_ref, zb_ref, zc_ref, ob_ref, oc_ref, lng_ref, lnb_ref, w_ref, bias_ref, y_ref, yt_ref):
        ug, _, _, _, mixed, _ = _sgu_common(u_ref, v_ref, lng_ref, lnb_ref, w_ref, bias_ref)
        sza, _ = _silu_and_grad(za_ref[...])
        gate = ug * sza

        def put(off, width, val):
            y_ref[:, off:off + width] = val.astype(BF16)
            yt_ref[off:off + width, :] = val.T.astype(BF16)

        for g in range(A_GROUPS):
            sl = slice(g * CHUNK, (g + 1) * CHUNK)
            put(g * CHUNK, CHUNK, gate[:, sl] * mixed[g])
        szb, _ = _silu_and_grad(zb_ref[...])
        put(OFF_YB, D_B, ob_ref[...] * szb)
        szc, _ = _silu_and_grad(zc_ref[...])
        put(OFF_YC, D_C, oc_ref[...] * szc)

    wide = lambda off: pl.BlockSpec((CHUNK, D_A), lambda i: (i, off // D_A))
    narrow = lambda off: pl.BlockSpec((CHUNK, D_B), lambda i: (i, off // D_B))
    vec = pl.BlockSpec((1, D_A), lambda i: (0, 0))
    return pl.pallas_call(
        body, name=name, grid=(s_len // CHUNK,),
        in_specs=[wide(OFF_U), wide(OFF_V), wide(OFF_ZA), narrow(OFF_ZB), narrow(OFF_ZC), narrow(0), narrow(0), vec, vec,
                  pl.BlockSpec((A_GROUPS, CHUNK, CHUNK), lambda i: (0, 0, 0)),
                  pl.BlockSpec((CHUNK, D_A), lambda i: (0, 0))],
        out_specs=[pl.BlockSpec((CHUNK, D_MODEL), lambda i: (i, 0)), pl.BlockSpec((D_MODEL, CHUNK), lambda i: (0, i))],
        out_shape=[jax.ShapeDtypeStruct((s_len, D_MODEL), BF16), jax.ShapeDtypeStruct((D_MODEL, s_len), BF16)],
        compiler_params=_params(("parallel",)),
    )(proj, proj, proj, proj, proj, o_b, o_c, lng, lnb, w_s, bias)


def _gate_bwd(name, proj, dy, o_b, o_c, dqkv, dq_c, lng, lnb, w_s, w_s_t, bias):
    s_len = proj.shape[0]
    n = s_len // CHUNK
    dq_b, dk_b, dv_b = dqkv

    def body(u_ref, v_ref, za_ref, zb_ref, zc_ref, dya_ref, dyb_ref, dyc_ref, ob_ref, oc_ref, dq_ref, dk_ref, dv_ref,
             dqc_ref, lng_ref, lnb_ref, w_ref, wt_ref, bias_ref, dp_ref, dw_ref, dsb_ref, dlng_ref, dlnb_ref, dbias_ref):
        i = pl.program_id(0)

        @pl.when(i == 0)
        def _():
            dw_ref[...] = jnp.zeros_like(dw_ref)
            dbias_ref[...] = jnp.zeros_like(dbias_ref)
            dlng_ref[...] = jnp.zeros_like(dlng_ref)
            dlnb_ref[...] = jnp.zeros_like(dlnb_ref)

        ug, xhat, rstd, vn_b, mixed, tril = _sgu_common(u_ref, v_ref, lng_ref, lnb_ref, w_ref, bias_ref)
        za = za_ref[...]
        sza, dsza = _silu_and_grad(za)
        dya = dya_ref[...]
        mixed_all = jnp.concatenate(mixed, axis=-1)
        d_mixed = dya * ug * sza
        dp_ref[:, OFF_U:OFF_U + D_A] = (dya * mixed_all * sza * _gelu_grad(u_ref[...])).astype(BF16)
        dp_ref[:, OFF_ZA:OFF_ZA + D_A] = (dya * ug * mixed_all * dsza).astype(BF16)
        dbias_ref[...] += d_mixed
        dm_b = d_mixed.astype(BF16)
        triu = lax.broadcasted_iota(jnp.int32, (CHUNK, CHUNK), 0) <= lax.broadcasted_iota(jnp.int32, (CHUNK, CHUNK), 1)
        d_vn = []
        for g in range(A_GROUPS):
            sl = slice(g * CHUNK, (g + 1) * CHUNK)
            wt = jnp.where(triu, wt_ref[g], 0.0).astype(BF16)
            d_vn.append(_dot(wt, dm_b[:, sl], NN))
            dw_ref[g] += jnp.where(tril, _dot(dm_b[:, sl], vn_b[:, sl], NT), 0.0)
        d_vn = jnp.concatenate(d_vn, axis=-1)
        dlng_ref[...] += jnp.sum(d_vn * xhat, axis=0, keepdims=True)
        dlnb_ref[...] += jnp.sum(d_vn, axis=0, keepdims=True)
        dxh = d_vn * lng_ref[...]
        d_vg = rstd * (dxh - jnp.mean(dxh, axis=-1, keepdims=True)
                       - xhat * jnp.mean(dxh * xhat, axis=-1, keepdims=True))
        dp_ref[:, OFF_V:OFF_V + D_A] = (d_vg * _gelu_grad(v_ref[...])).astype(BF16)
        dp_ref[:, OFF_QB:OFF_QB + D_B] = dq_ref[...].astype(BF16)
        dp_ref[:, OFF_KB:OFF_KB + D_B] = dk_ref[...].astype(BF16)
        dp_ref[:, OFF_VB:OFF_VB + D_B] = dv_ref[...].astype(BF16)
        _, dszb = _silu_and_grad(zb_ref[...])
        dp_ref[:, OFF_ZB:OFF_ZB + D_B] = (dyb_ref[...] * ob_ref[...] * dszb).astype(BF16)
        dp_ref[:, OFF_QC:OFF_QC + D_C] = dqc_ref[...].astype(BF16)
        _, dszc = _silu_and_grad(zc_ref[...])
        dp_ref[:, OFF_ZC:OFF_ZC + D_C] = (dyc_ref[...] * oc_ref[...] * dszc).astype(BF16)

        @pl.when(i == n - 1)
        def _():
            ch = lax.broadcasted_iota(jnp.int32, (D_A, CHUNK), 0)
            gcol = lax.broadcasted_iota(jnp.int32, (D_A, CHUNK), 1)
            pick = (ch // (D_A // A_GROUPS) == gcol).astype(BF16)
            rest = dbias_ref[...]
            tot = jnp.zeros((CHUNK, CHUNK), F32)
            for _ in range(3):
                term = rest.astype(BF16)
                tot = tot + _dot(term, pick, NN)
                rest = rest - term.astype(F32)
            dsb_ref[...] = tot

    wide = lambda off: pl.BlockSpec((CHUNK, D_A), lambda i: (i, off // D_A))
    narrow = lambda off: pl.BlockSpec((CHUNK, D_B), lambda i: (i, off // D_B))
    vec = pl.BlockSpec((1, D_A), lambda i: (0, 0))
    wspec = pl.BlockSpec((A_GROUPS, CHUNK, CHUNK), lambda i: (0, 0, 0))
    bspec = pl.BlockSpec((CHUNK, D_A), lambda i: (0, 0))
    return pl.pallas_call(
        body, name=name, grid=(n,),
        in_specs=[wide(OFF_U), wide(OFF_V), wide(OFF_ZA), narrow(OFF_ZB), narrow(OFF_ZC),
                  wide(0), narrow(OFF_YB), narrow(OFF_YC), narrow(0), narrow(0), narrow(0), narrow(0), narrow(0),
                  narrow(0), vec, vec, wspec, wspec, bspec],
        out_specs=[pl.BlockSpec((CHUNK, IN_WIDTH), lambda i: (i, 0)), wspec,
                   pl.BlockSpec((CHUNK, CHUNK), lambda i: (0, 0)), vec, vec],
        out_shape=[jax.ShapeDtypeStruct((s_len, IN_WIDTH), BF16), jax.ShapeDtypeStruct((A_GROUPS, CHUNK, CHUNK), F32),
                   jax.ShapeDtypeStruct((CHUNK, CHUNK), F32), jax.ShapeDtypeStruct((1, D_A), F32),
                   jax.ShapeDtypeStruct((1, D_A), F32)],
        scratch_shapes=[pltpu.VMEM((CHUNK, D_A), F32)],
        compiler_params=_params(("arbitrary",)),
    )(proj, proj, proj, proj, proj, dy, dy, dy, o_b, o_c, dq_b, dk_b, dv_b, dq_c, lng, lnb, w_s, w_s_t, bias)


IN_SHARD = IN_WIDTH // N_CHIPS
ROW_SHARD = D_MODEL // N_CHIPS


def _bias_rows(sgu_b_l):
    return jnp.repeat(sgu_b_l.T, D_A // A_GROUPS, axis=1)


class _WholeWeights:
    def __init__(self, w_in_all, w_kv_all, w_out_all):
        self.weights = (w_in_all, w_kv_all, w_out_all)

    def w_in(self, h):
        return self.weights[0]

    def rest_start(self, proj):
        return None

    def rest_finish(self, o_b):
        return self.weights[1], self.weights[2], None

    def before_out(self, y):
        return None


def _layer_fwd(l, x, mem, sm, hooks):
    s_len = x.shape[0]
    m_len = mem.shape[0]
    tm = min(1024, s_len)
    tn = 768
    per = IN_SHARD // tn
    h, h_t = _rms_fwd(f"rms_fwd_{l}", x, sm["norm_g"][l][None], min(256, s_len), transposed=True)
    w_in_all = hooks.w_in(h)
    proj = _matmul(
        f"in_proj_{l}", h, w_in_all, grid=(s_len // tm, IN_WIDTH // tn, 1),
        a_spec=pl.BlockSpec((tm, D_MODEL), lambda i, j, k: (i, 0)),
        b_spec=pl.BlockSpec((None, D_MODEL, tn), lambda i, j, k: (j // per, 0, j % per)),
        o_spec=pl.BlockSpec((tm, tn), lambda i, j, k: (i, j)),
        out_shape=jax.ShapeDtypeStruct((s_len, IN_WIDTH), F32), dims=NN)
    o_b = _sb_fwd(f"sb_fwd_{l}", proj, hooks.rest_start(proj))
    w_kv_all, w_out_all, after = hooks.rest_finish(o_b)
    mem_h = _rms_fwd(f"mem_rms_fwd_{l}", mem, sm["mem_norm_g"][l][None], m_len, after)
    mem_kv = _matmul(
        f"mem_kv_{l}", mem_h, w_kv_all, grid=(1, 2, N_CHIPS),
        a_spec=pl.BlockSpec((m_len, ROW_SHARD), lambda i, j, k: (0, k)),
        b_spec=pl.BlockSpec((None, ROW_SHARD, D_C), lambda i, j, k: (k, 0, j)),
        o_spec=pl.BlockSpec((m_len, D_C), lambda i, j, k: (0, j)),
        out_shape=jax.ShapeDtypeStruct((m_len, 2 * D_C), F32), dims=NN)
    qg, kg = sm["q_norm_g"][l][None], sm["k_norm_g"][l][None]
    o_c = _mem_fwd(f"mem_fwd_{l}", proj, mem_kv, qg, kg)
    bias = _bias_rows(sm["sgu_b"][l])
    y, y_t = _gate_fwd(f"gate_fwd_{l}", proj, o_b, o_c, sm["sgu_ln_g"][l][None], sm["sgu_ln_b"][l][None],
                       sm["sgu_w"][l], bias)
    tn_o = 512
    x_next = _matmul(
        f"out_proj_{l}", y, w_out_all, grid=(s_len // tm, D_MODEL // tn_o, 1),
        a_spec=pl.BlockSpec((tm, D_MODEL), lambda i, j, k: (i, 0)),
        b_spec=pl.BlockSpec((N_CHIPS, ROW_SHARD, tn_o), lambda i, j, k: (0, 0, j)),
        o_spec=pl.BlockSpec((tm, tn_o), lambda i, j, k: (i, j)),
        out_shape=jax.ShapeDtypeStruct((s_len, D_MODEL), F32), dims=NN,
        res=x, res_spec=pl.BlockSpec((tm, tn_o), lambda i, j, k: (i, j)), after=hooks.before_out(y))
    saved = dict(x=x, h_t=h_t, proj=proj, mem_h=mem_h, mem_kv=mem_kv, o_b=o_b, o_c=o_c, y_t=y_t, bias=bias,
                 weights=(w_in_all, w_kv_all, w_out_all))
    return x_next, saved


class _NoExchange:
    def __init__(self):
        self.gave, self.kept = {}, {}

    def start(self, l, group, gives):
        self.gave[l, group] = gives
        return None

    def landed(self, l, group, after):
        return [jnp.zeros_like(g) for g in self.gave[l, group]]

    def send(self, l, group, parts):
        self.kept[l, group] = parts
        return None


def _layer_bwd(l, dxo, dxo_b, mem, sm, saved, place, exchange):
    s_len = dxo.shape[0]
    m_len = mem.shape[0]
    proj, y_t, h_t, mem_h, mem_kv = saved["proj"], saved["y_t"], saved["h_t"], saved["mem_h"], saved["mem_kv"]
    w_in_all, w_kv_all, w_out_all = saved["weights"]
    tm = min(1024, s_len)
    tn = 768
    per = IN_SHARD // tn
    half_rows = ROW_SHARD // 2

    def halves(make):
        give = lambda: make("give", lambda p: 1 - p[1], None, F32)
        keep = lambda theirs: make("keep", lambda p: p[1], theirs, BF16)
        return give, keep

    def grad_out(tag, half, theirs, dtype):
        o_spec = pl.BlockSpec((None, half_rows, 1024), lambda i, j, k, p: (i, 0, j))
        return _matmul(
            f"d_w_out_{l}_{tag}", y_t, dxo_b, grid=(N_CHIPS, D_MODEL // 1024, 1), place=place,
            a_spec=pl.BlockSpec((half_rows, s_len), lambda i, j, k, p: (2 * i + half(p), 0)),
            b_spec=pl.BlockSpec((s_len, 1024), lambda i, j, k, p: (0, j)), o_spec=o_spec,
            out_shape=jax.ShapeDtypeStruct((N_CHIPS, half_rows, D_MODEL), dtype), dims=NN,
            res=theirs, res_spec=o_spec)

    def grad_in(tag, half, theirs, dtype):
        o_spec = pl.BlockSpec((None, D_MODEL // 2, tn), lambda i, j, k, p: (j // per, 0, j % per))
        return _matmul(
            f"d_w_in_{l}_{tag}", h_t, dproj, grid=(1, IN_WIDTH // tn, 1), place=place,
            a_spec=pl.BlockSpec((D_MODEL // 2, s_len), lambda i, j, k, p: (half(p), 0)),
            b_spec=pl.BlockSpec((s_len, tn), lambda i, j, k, p: (0, j)), o_spec=o_spec,
            out_shape=jax.ShapeDtypeStruct((N_CHIPS, D_MODEL // 2, IN_SHARD), dtype), dims=NN,
            res=theirs, res_spec=o_spec)

    def grad_kv(tag, half, theirs, dtype):
        o_spec = pl.BlockSpec((None, half_rows, 2 * D_C), lambda i, j, k, p: (i, 0, 0))
        return _matmul(
            f"d_w_kv_{l}_{tag}", mem_h, dkv_b, grid=(N_CHIPS, 1, 1), place=place,
            a_spec=pl.BlockSpec((m_len, half_rows), lambda i, j, k, p: (0, 2 * i + half(p))),
            b_spec=pl.BlockSpec((m_len, 2 * D_C), lambda i, j, k, p: (0, 0)), o_spec=o_spec,
            out_shape=jax.ShapeDtypeStruct((N_CHIPS, half_rows, 2 * D_C), dtype), dims=TN,
            res=theirs, res_spec=o_spec)

    give_out, keep_out = halves(grad_out)
    token = exchange.start(l, "out", [give_out()])
    dy = _matmul(
        f"d_y_{l}", dxo_b, w_out_all, grid=(s_len // tm, N_CHIPS, 1),
        a_spec=pl.BlockSpec((tm, D_MODEL), lambda i, j, k: (i, 0)),
        b_spec=pl.BlockSpec((None, ROW_SHARD, D_MODEL), lambda i, j, k: (j, 0, 0)),
        o_spec=pl.BlockSpec((tm, ROW_SHARD), lambda i, j, k: (i, j)),
        out_shape=jax.ShapeDtypeStruct((s_len, D_MODEL), F32), dims=NT, after=token)
    (theirs_out,) = exchange.landed(l, "out", dy)
    token = exchange.send(l, "out", [keep_out(theirs_out)])
    qg, kg = sm["q_norm_g"][l][None], sm["k_norm_g"][l][None]
    dqkv = _sb_bwd(f"sb_bwd_{l}", proj, dy, token)
    dq_c, dmk, dmv, dqg, dkg = _mem_bwd(f"mem_bwd_{l}", proj, mem_kv, qg, kg, dy)
    w_s = sm["sgu_w"][l]
    dproj, dws, dbias, dlng, dlnb = _gate_bwd(
        f"gate_bwd_{l}", proj, dy, saved["o_b"], saved["o_c"], dqkv, dq_c, sm["sgu_ln_g"][l][None],
        sm["sgu_ln_b"][l][None], w_s, jnp.swapaxes(w_s, 1, 2), saved["bias"])
    dkv_b = jnp.concatenate([dmk, dmv], axis=1).astype(BF16)
    give_in, keep_in = halves(grad_in)
    give_kv, keep_kv = halves(grad_kv)
    token = exchange.start(l, "in", [give_in(), give_kv()])
    dh = _matmul(
        f"d_h_{l}", dproj, w_in_all, grid=(s_len // tm, D_MODEL // 512, 1),
        a_spec=pl.BlockSpec((tm, IN_WIDTH), lambda i, j, k: (i, 0)),
        b_spec=pl.BlockSpec((N_CHIPS, 512, IN_SHARD), lambda i, j, k: (0, j, 0)),
        o_spec=pl.BlockSpec((tm, 512), lambda i, j, k: (i, j)),
        out_shape=jax.ShapeDtypeStruct((s_len, D_MODEL), F32), dims=NT, after=token, vmem_mb=56)
    theirs_in, theirs_kv = exchange.landed(l, "in", dh)
    token = exchange.send(l, "in", [keep_in(theirs_in), keep_kv(theirs_kv)])
    dx, dx_b, dng = _rms_bwd(f"rms_bwd_{l}", saved["x"], dh, dxo, sm["norm_g"][l][None], min(256, s_len), token)
    d_mem_h = _matmul(
        f"d_mem_h_{l}", dkv_b, w_kv_all, grid=(1, N_CHIPS, 1),
        a_spec=pl.BlockSpec((m_len, 2 * D_C), lambda i, j, k: (0, 0)),
        b_spec=pl.BlockSpec((None, ROW_SHARD, 2 * D_C), lambda i, j, k: (j, 0, 0)),
        o_spec=pl.BlockSpec((m_len, ROW_SHARD), lambda i, j, k: (0, j)),
        out_shape=jax.ShapeDtypeStruct((m_len, D_MODEL), F32), dims=NT)
    dmng = _rms_gain_grad(f"mem_rms_bwd_{l}", mem, d_mem_h)
    dsgu_b = dbias[:, :A_GROUPS].T
    small = dict(norm_g=dng[0], sgu_ln_g=dlng[0], sgu_ln_b=dlnb[0], sgu_w=dws, sgu_b=dsgu_b, mem_norm_g=dmng[0],
                 q_norm_g=dqg[0], k_norm_g=dkg[0])
    return dx, dx_b, small


SMALL_NAMES = ("norm_g", "sgu_ln_g", "sgu_ln_b", "sgu_w", "sgu_b", "mem_norm_g", "q_norm_g", "k_norm_g")


def _local_step(x, mem, target, sm, w_all):
    saved = []
    cur = x
    for l in range(DEPTH):
        cur, sv = _layer_fwd(l, cur, mem, sm, _WholeWeights(*w_all[l]))
        saved.append(sv)
    dxo, dxo_b, loss = _loss_and_grad("loss", cur, target, min(256, x.shape[0]))
    small = [None] * DEPTH
    exchange = _NoExchange()
    place = jnp.zeros((2,), jnp.int32)
    for l in reversed(range(DEPTH)):
        dxo, dxo_b, small[l] = _layer_bwd(l, dxo, dxo_b, mem, sm, saved[l], place, exchange)
    small = {k: jnp.stack([small[l][k] for l in range(DEPTH)]) for k in SMALL_NAMES}
    return loss, dxo, small, exchange.gave, exchange.kept


def _place():
    x, y, c = lax.axis_index("x"), lax.axis_index("y"), lax.axis_index("c")
    return x, y, c


def _other_chips(x, y):
    return [(1 - x, y, 2 * (1 - x) + y), (x, 1 - y, 2 * x + 1 - y), (1 - x, 1 - y, 2 * (1 - x) + 1 - y)]


D2D_CHUNKS = 8


def _place_index():
    return jnp.stack([2 * lax.axis_index("x") + lax.axis_index("y"), lax.axis_index("c")]).astype(jnp.int32)


def _cast_into_slot(name, w, l, place):
    _, rows, cols = w.shape
    tr = min(256, rows)

    def body(p_ref, w_ref, o_ref):
        o_ref[...] = w_ref[...].astype(BF16)

    return pl.pallas_call(
        body, name=name,
        grid_spec=pltpu.PrefetchScalarGridSpec(
            num_scalar_prefetch=1, grid=(rows // tr,),
            in_specs=[pl.BlockSpec((None, tr, cols), lambda i, p: (l, i, 0))],
            out_specs=pl.BlockSpec((None, tr, cols), lambda i, p: (p[0], i, 0))),
        out_shape=jax.ShapeDtypeStruct((N_CHIPS, rows, cols), BF16),
        compiler_params=_params(("parallel",)),
    )(place, w)


HBM = pl.BlockSpec(memory_space=pltpu.HBM)
SEM = pl.BlockSpec(memory_space=pltpu.SEMAPHORE)
DATAFLOW = pltpu.SideEffectType.DATAFLOW_SIDE_EFFECTING


def _in_hbm(a):
    return pltpu.with_memory_space_constraint(a, pltpu.HBM)


def _chip_copies_start(name, srcs, lands, make_copy, after=None):
    n_t = len(srcs)
    in_place = lands is None
    n_after = 0 if after is None else 1

    def body(*refs):
        src = refs[:n_t]
        k = (n_t if in_place else 2 * n_t) + n_after
        send_sems, recv_sems = refs[k], refs[k + 1]
        land = refs[k + 2:k + 2 + n_t] if in_place else refs[k + 2 + n_t:k + 2 + 2 * n_t]
        token = refs[-1]
        x, y, c = _place()
        me = 2 * x + y
        for t in range(n_t):
            for px, py, pk in _other_chips(x, y):
                s, d = make_copy(src[t], land[t], me, pk, c)
                pltpu.make_async_remote_copy(
                    src_ref=s, dst_ref=d, send_sem=send_sems.at[t], recv_sem=recv_sems.at[t],
                    device_id=(px, py, c), device_id_type=MESH).start()
        token[...] = jnp.zeros_like(token)

    bufs = list(srcs) if in_place else list(srcs) + list(lands)
    outs = pl.pallas_call(
        body, name=name,
        in_specs=[HBM] * len(bufs) + [ANY] * n_after,
        out_specs=[SEM, SEM] + [HBM] * len(bufs) + [pl.BlockSpec(memory_space=pltpu.VMEM)],
        out_shape=[pltpu.SemaphoreType.DMA((n_t,)), pltpu.SemaphoreType.DMA((n_t,))]
        + [pltpu.HBM(b.shape, b.dtype) for b in bufs] + [jax.ShapeDtypeStruct((8, 128), F32)],
        input_output_aliases={i: 2 + i for i in range(len(bufs))},
        compiler_params=pltpu.CompilerParams(has_side_effects=DATAFLOW),
    )(*[_in_hbm(b) for b in bufs], *([] if after is None else [after]))
    return outs[0], outs[1], list(outs[2:2 + len(bufs)]), outs[-1]


def _chip_copies_wait(name, send_sems, recv_sems, bufs, sent, landed, after):
    n_b = len(bufs)

    def body(*refs):
        buf = refs[:n_b]
        send_ref, recv_ref = refs[n_b], refs[n_b + 1]
        x, y, c = _place()
        for t, (s, d) in enumerate(zip(sent(buf), landed(buf))):
            out = pltpu.make_async_remote_copy(src_ref=s, dst_ref=s, send_sem=send_ref.at[t], recv_sem=recv_ref.at[t],
                                               device_id=(x, y, c), device_id_type=MESH)
            out.wait_send()
            arrived = pltpu.make_async_remote_copy(src_ref=d, dst_ref=d, send_sem=send_ref.at[t],
                                                   recv_sem=recv_ref.at[t], device_id=(x, y, c), device_id_type=MESH)
            arrived.wait_recv()

    after = list(after) if isinstance(after, (list, tuple)) else [after]
    return pl.pallas_call(
        body, name=name,
        in_specs=[HBM] * n_b + [SEM, SEM] + [ANY] * len(after), out_specs=[HBM] * n_b,
        out_shape=[pltpu.HBM(b.shape, b.dtype) for b in bufs],
        input_output_aliases={i: i for i in range(n_b)},
        compiler_params=pltpu.CompilerParams(has_side_effects=DATAFLOW),
    )(*bufs, send_sems, recv_sems, *after)


def _gather_start(name, bufs, after=None):
    def make_copy(src, land, me, pk, c):
        hr = src.shape[1] // 2
        return src.at[me, pl.ds(c * hr, hr)], land.at[me, pl.ds(c * hr, hr)]

    return _chip_copies_start(name, bufs, None, make_copy, after)


def _gather_wait(name, send_sems, recv_sems, bufs, after):
    def three_halves(buf):
        return [b.at[pl.ds(0, 3), pl.ds(0, b.shape[1] // 2)] for b in buf]

    return _chip_copies_wait(name, send_sems, recv_sems, bufs, three_halves, three_halves, after)


def _gather_forward_start(name, bufs):
    n_t = len(bufs)

    def body(*refs):
        mine = refs[:n_t]
        send_sems, recv_sems = refs[n_t], refs[n_t + 1]
        buf = refs[n_t + 2:2 * n_t + 2]
        token = refs[-1]
        x, y, c = _place()
        for q in range(D2D_CHUNKS):
            for t in range(n_t):
                hr = mine[t].shape[1] // 2
                cr = hr // D2D_CHUNKS
                rows = pl.ds(c * hr + q * cr, cr)
                for _, _, pk in _other_chips(x, y):
                    pltpu.make_async_remote_copy(
                        src_ref=mine[t].at[pk, rows], dst_ref=buf[t].at[pk, rows], send_sem=send_sems.at[t],
                        recv_sem=recv_sems.at[t], device_id=(x, y, 1 - c), device_id_type=MESH).start()
        token[...] = jnp.zeros_like(token)

    outs = pl.pallas_call(
        body, name=name,
        in_specs=[HBM] * n_t,
        out_specs=[SEM, SEM] + [HBM] * n_t + [pl.BlockSpec(memory_space=pltpu.VMEM)],
        out_shape=[pltpu.SemaphoreType.DMA((n_t,)), pltpu.SemaphoreType.DMA((n_t,))]
        + [pltpu.HBM(b.shape, b.dtype) for b in bufs] + [jax.ShapeDtypeStruct((8, 128), F32)],
        input_output_aliases={i: 2 + i for i in range(n_t)},
        compiler_params=pltpu.CompilerParams(has_side_effects=DATAFLOW),
    )(*[_in_hbm(b) for b in bufs])
    return outs[0], outs[1], list(outs[2:2 + n_t]), outs[-1]


def _core_exchange_start(name, grads):
    n_t = len(grads)
    lands = [lax.empty(g.shape, g.dtype) for g in grads]

    def body(*refs):
        src = refs[:n_t]
        send_sems, recv_sems = refs[2 * n_t], refs[2 * n_t + 1]
        land = refs[2 * n_t + 2 + n_t:2 * n_t + 2 + 2 * n_t]
        token = refs[-1]
        x, y, c = _place()
        for q in range(D2D_CHUNKS):
            for t in range(n_t):
                cr = src[t].shape[1] // D2D_CHUNKS
                rows = pl.ds(q * cr, cr)
                pltpu.make_async_remote_copy(
                    src_ref=src[t].at[:, rows], dst_ref=land[t].at[:, rows], send_sem=send_sems.at[t],
                    recv_sem=recv_sems.at[t], device_id=(x, y, 1 - c), device_id_type=MESH).start()
        token[...] = jnp.zeros_like(token)

    bufs = list(grads) + lands
    outs = pl.pallas_call(
        body, name=name,
        in_specs=[HBM] * len(bufs),
        out_specs=[SEM, SEM] + [HBM] * len(bufs) + [pl.BlockSpec(memory_space=pltpu.VMEM)],
        out_shape=[pltpu.SemaphoreType.DMA((n_t,)), pltpu.SemaphoreType.DMA((n_t,))]
        + [pltpu.HBM(b.shape, b.dtype) for b in bufs] + [jax.ShapeDtypeStruct((8, 128), F32)],
        input_output_aliases={i: 2 + i for i in range(len(bufs))},
        compiler_params=pltpu.CompilerParams(has_side_effects=DATAFLOW),
    )(*[_in_hbm(b) for b in bufs])
    return outs[0], outs[1], list(outs[2:2 + len(bufs)]), outs[-1]


def _core_exchange_wait(name, send_sems, recv_sems, bufs, after):
    n_t = len(bufs) // 2

    def body(*refs):
        land = refs[n_t:2 * n_t]
        send_ref, recv_ref = refs[2 * n_t], refs[2 * n_t + 1]
        x, y, c = _place()
        for t in range(n_t):
            whole = pltpu.make_async_remote_copy(src_ref=land[t], dst_ref=land[t], send_sem=send_ref.at[t],
                                                 recv_sem=recv_ref.at[t], device_id=(x, y, c), device_id_type=MESH)
            whole.wait_send()
            whole.wait_recv()

    outs = pl.pallas_call(
        body, name=name,
        in_specs=[HBM] * (2 * n_t) + [SEM, SEM, ANY], out_specs=[HBM] * (2 * n_t),
        out_shape=[pltpu.HBM(b.shape, b.dtype) for b in bufs],
        input_output_aliases={i: i for i in range(2 * n_t)},
        compiler_params=pltpu.CompilerParams(has_side_effects=DATAFLOW),
    )(*bufs, send_sems, recv_sems, after)
    return list(outs[:n_t]), list(outs[n_t:])


def _chip_exchange_start(name, parts):
    lands = [lax.empty(p.shape, p.dtype) for p in parts]
    return _chip_copies_start(name, parts, lands, lambda src, land, me, pk, c: (src.at[pk], land.at[me]))


def _chip_exchange_wait(name, send_sems, recv_sems, bufs, after):
    n_t = len(bufs) // 2
    return _chip_copies_wait(name, send_sems, recv_sems, bufs,
                             lambda buf: [b.at[pl.ds(0, 3)] for b in buf[:n_t]],
                             lambda buf: [b.at[pl.ds(0, 3)] for b in buf[n_t:]], after)


def _sum_chips(name, parts, landed, place, l, stacked):
    chips, rows, cols = landed.shape
    tr = min(256, rows)
    per = rows // tr

    def body(p_ref, own_ref, *refs):
        land, o_ref = refs[:chips], refs[-1]
        tot = None
        for k in range(chips):
            term = jnp.where(p_ref[0] == k, own_ref[...], land[k][...]).astype(F32)
            tot = term if tot is None else tot + term
        o_ref[...] = tot

    def from_chip(k):
        return pl.BlockSpec((None, tr, cols), lambda i, p: (jnp.where(p[0] == k, (k + 1) % chips, k), i, 0))

    in_specs = [pl.BlockSpec((None, tr, cols), lambda i, p: (p[0], i, 0))] + [from_chip(k) for k in range(chips)]
    args = [parts] + [landed] * chips
    aliases = {}
    if stacked is not None:
        in_specs.append(ANY)
        args.append(stacked)
        aliases = {len(args): 0}
    return pl.pallas_call(
        body, name=name,
        grid_spec=pltpu.PrefetchScalarGridSpec(
            num_scalar_prefetch=1, grid=(per,), in_specs=in_specs,
            out_specs=pl.BlockSpec((None, tr, cols), lambda i, p: (l, p[1] * per + i, 0))),
        out_shape=jax.ShapeDtypeStruct((DEPTH, 2 * rows, cols), F32), input_output_aliases=aliases,
        compiler_params=_params(("parallel",)),
    )(place, *args)


def _core_share_start(name, bufs, l):
    n_t = len(bufs)

    def body(*refs):
        mine = refs[:n_t]
        send_sems, recv_sems = refs[n_t], refs[n_t + 1]
        buf = refs[n_t + 2:2 * n_t + 2]
        token = refs[-1]
        x, y, c = _place()
        for q in range(D2D_CHUNKS):
            for t in range(n_t):
                hr = mine[t].shape[1] // 2
                cr = hr // D2D_CHUNKS
                rows = pl.ds(c * hr + q * cr, cr)
                pltpu.make_async_remote_copy(
                    src_ref=mine[t].at[l, rows], dst_ref=buf[t].at[l, rows], send_sem=send_sems.at[t],
                    recv_sem=recv_sems.at[t], device_id=(x, y, 1 - c), device_id_type=MESH).start()
        token[...] = jnp.zeros_like(token)

    outs = pl.pallas_call(
        body, name=name,
        in_specs=[HBM] * n_t,
        out_specs=[SEM, SEM] + [HBM] * n_t + [pl.BlockSpec(memory_space=pltpu.VMEM)],
        out_shape=[pltpu.SemaphoreType.DMA((n_t,)), pltpu.SemaphoreType.DMA((n_t,))]
        + [pltpu.HBM(b.shape, b.dtype) for b in bufs] + [jax.ShapeDtypeStruct((8, 128), F32)],
        input_output_aliases={i: 2 + i for i in range(n_t)},
        compiler_params=pltpu.CompilerParams(has_side_effects=DATAFLOW),
    )(*[_in_hbm(b) for b in bufs])
    return outs[0], outs[1], list(outs[2:2 + n_t]), outs[-1]


def _core_share_wait(name, send_sems, recv_sems, bufs, l, after):
    def half_layer(buf):
        return [b.at[l, pl.ds(0, b.shape[1] // 2)] for b in buf]

    return _chip_copies_wait(name, send_sems, recv_sems, bufs, half_layer, half_layer, after)


def _all_reduce_small(vec, after=None):
    rows, lanes = vec.shape
    hr = rows // 2

    def body(v_ref, *refs):
        o_ref, sib_ref, chips_ref, send_sems, recv_sems = refs[-5:]
        x, y, c = _place()
        me = 2 * x + y
        sibling = (x, y, 1 - c)
        mine = pl.ds(pl.multiple_of(c * hr, 8), hr)
        theirs = pl.ds(pl.multiple_of((1 - c) * hr, 8), hr)
        swap = pltpu.make_async_remote_copy(
            src_ref=v_ref.at[theirs], dst_ref=sib_ref, send_sem=send_sems.at[0], recv_sem=recv_sems.at[0],
            device_id=sibling, device_id_type=MESH)
        swap.start()
        swap.wait_recv()
        chips_ref[me] = v_ref[mine] + sib_ref[...]
        copies = []
        for j, (px, py, pk) in enumerate(_other_chips(x, y)):
            cp = pltpu.make_async_remote_copy(
                src_ref=chips_ref.at[me], dst_ref=chips_ref.at[me], send_sem=send_sems.at[1 + j],
                recv_sem=recv_sems.at[1 + j], device_id=(px, py, c), device_id_type=MESH)
            cp.start()
            copies.append(cp)
        for j, (px, py, pk) in enumerate(_other_chips(x, y)):
            pltpu.make_async_remote_copy(
                src_ref=chips_ref.at[pk], dst_ref=chips_ref.at[pk], send_sem=send_sems.at[1 + j],
                recv_sem=recv_sems.at[1 + j], device_id=(px, py, c), device_id_type=MESH).wait_recv()
        tot = chips_ref[0]
        for k in range(1, N_CHIPS):
            tot = tot + chips_ref[k]
        o_ref[mine] = tot
        share = pltpu.make_async_remote_copy(
            src_ref=o_ref.at[mine], dst_ref=o_ref.at[mine], send_sem=send_sems.at[4], recv_sem=recv_sems.at[4],
            device_id=sibling, device_id_type=MESH)
        share.start()
        pltpu.make_async_remote_copy(
            src_ref=o_ref.at[theirs], dst_ref=o_ref.at[theirs], send_sem=send_sems.at[4], recv_sem=recv_sems.at[4],
            device_id=sibling, device_id_type=MESH).wait_recv()
        swap.wait_send()
        for cp in copies:
            cp.wait_send()
        share.wait_send()

    vm = pl.BlockSpec(memory_space=pltpu.VMEM)
    return pl.pallas_call(
        body, name="small_all_reduce", in_specs=[vm] + ([] if after is None else [ANY]), out_specs=vm,
        out_shape=jax.ShapeDtypeStruct((rows, lanes), F32),
        scratch_shapes=[pltpu.VMEM((hr, lanes), F32), pltpu.VMEM((N_CHIPS, hr, lanes), F32),
                        pltpu.SemaphoreType.DMA((5,)), pltpu.SemaphoreType.DMA((5,))],
        compiler_params=pltpu.CompilerParams(has_side_effects=True, vmem_limit_bytes=48 * MIB),
    )(vec, *([] if after is None else [after]))


def _adamw(name, w, g, m, v, place, l=0, half=None, done=None, after=None):
    layers, rows, cols = w.shape
    span = rows if half is None else rows // 2
    tr = span
    for cand in (256, 128, 64, 32, 16, 8):
        if span % cand == 0:
            tr = cand
            break
    per = span // tr
    c1 = 1.0 - ADAM_B1 ** ADAM_STEP
    c2 = 1.0 - ADAM_B2 ** ADAM_STEP

    def first_block(p):
        return 0 if half is None else (p[1] if half == "own" else 1 - p[1]) * per

    def body(p_ref, w_ref, g_ref, m_ref, v_ref, *refs):
        go_ref, d_ref, nm_ref, nv_ref = refs[-4:]
        gv = g_ref[...]
        nm = ADAM_B1 * m_ref[...] + (1.0 - ADAM_B1) * gv
        nv = ADAM_B2 * v_ref[...] + (1.0 - ADAM_B2) * (gv * gv)
        go_ref[...] = gv
        nm_ref[...] = nm
        nv_ref[...] = nv
        d_ref[...] = -ADAM_LR * ((nm / c1) / (jnp.sqrt(nv / c2) + ADAM_EPS) + ADAM_WD * w_ref[...])

    blk = pl.BlockSpec((None, tr, cols), lambda i, p: (l, first_block(p) + i, 0))
    out = jax.ShapeDtypeStruct((layers, rows, cols), F32)
    extra = ([] if done is None else list(done)) + ([] if after is None else [after])
    aliases = {} if done is None else {5 + i: i for i in range(4)}
    return pl.pallas_call(
        body, name=name,
        grid_spec=pltpu.PrefetchScalarGridSpec(
            num_scalar_prefetch=1, grid=(per,), in_specs=[blk] * 4 + [ANY] * len(extra), out_specs=[blk] * 4),
        out_shape=[out] * 4, input_output_aliases=aliases,
        compiler_params=_params(("parallel",)),
    )(place, w, g, m, v, *extra)


def _pack_small(parts):
    flat = jnp.concatenate([parts[k].reshape(-1) for k in SMALL_NAMES])
    n = flat.shape[0]
    rows = -(-n // (256 * 128)) * 256
    return jnp.pad(flat, (0, rows * 128 - n)).reshape(rows, 128)


def _unpack_small(packed, like):
    flat = packed.reshape(-1)
    out, off = {}, 0
    for k in SMALL_NAMES:
        n = like[k].size
        out[k] = flat[off:off + n].reshape(like[k].shape)
        off += n
    return out


WEIGHT_ORDER = ("norm_g", "w_in", "sgu_ln_g", "sgu_ln_b", "sgu_w", "sgu_b", "mem_norm_g", "w_mem_kv", "q_norm_g",
                "k_norm_g", "w_out")


def kernel(x, mem, norm_g, w_in, sgu_ln_g, sgu_ln_b, sgu_w, sgu_b, mem_norm_g, w_mem_kv, q_norm_g, k_norm_g, w_out, loss_target, m_norm_g, m_w_in, m_sgu_ln_g, m_sgu_ln_b, m_sgu_w, m_sgu_b, m_mem_norm_g, m_w_mem_kv, m_q_norm_g, m_k_norm_g, m_w_out, v_norm_g, v_w_in, v_sgu_ln_g, v_sgu_ln_b, v_sgu_w, v_sgu_b, v_mem_norm_g, v_w_mem_kv, v_q_norm_g, v_k_norm_g, v_w_out):
    weights = dict(norm_g=norm_g, w_in=w_in, sgu_ln_g=sgu_ln_g, sgu_ln_b=sgu_ln_b, sgu_w=sgu_w, sgu_b=sgu_b,
                   mem_norm_g=mem_norm_g, w_mem_kv=w_mem_kv, q_norm_g=q_norm_g, k_norm_g=k_norm_g, w_out=w_out)
    mom_m = dict(norm_g=m_norm_g, w_in=m_w_in, sgu_ln_g=m_sgu_ln_g, sgu_ln_b=m_sgu_ln_b, sgu_w=m_sgu_w, sgu_b=m_sgu_b,
                 mem_norm_g=m_mem_norm_g, w_mem_kv=m_w_mem_kv, q_norm_g=m_q_norm_g, k_norm_g=m_k_norm_g, w_out=m_w_out)
    mom_v = dict(norm_g=v_norm_g, w_in=v_w_in, sgu_ln_g=v_sgu_ln_g, sgu_ln_b=v_sgu_ln_b, sgu_w=v_sgu_w, sgu_b=v_sgu_b,
                 mem_norm_g=v_mem_norm_g, w_mem_kv=v_w_mem_kv, q_norm_g=v_q_norm_g, k_norm_g=v_k_norm_g, w_out=v_w_out)
    big = ("w_in", "w_mem_kv", "w_out")
    sm = {k: weights[k] for k in SMALL_NAMES}

    place = _place_index()
    xs, mems, target = x[0], mem[0], loss_target[0]

    slots = [[_cast_into_slot(f"cast_{k}_{l}", weights[k], l, place) for k in big] for l in range(DEPTH)]
    saved = [None] * DEPTH

    chips, cores = {}, {}

    def start_gather(l, after=None):
        chips[l, "in"] = _gather_start(f"gather_start_{l}_in", slots[l][:1], after)
        chips[l, "rest"] = _gather_start(f"gather_start_{l}_rest", slots[l][1:], chips[l, "in"][3])
        return chips[l, "rest"][3]

    def hand_to_sibling(l, group, after):
        send_sems, recv_sems, bufs, _ = chips[l, group]
        bufs = _gather_wait(f"gather_wait_{l}_{group}", send_sems, recv_sems, bufs, after)
        cores[l, group] = _gather_forward_start(f"gather_forward_{l}_{group}", bufs)
        return cores[l, group][3]

    def whole(l, group, after):
        send_sems, recv_sems, bufs, _ = cores[l, group]
        return _gather_wait(f"gather_whole_{l}_{group}", send_sems, recv_sems, bufs, after)

    class Gathered:
        def __init__(self, l):
            self.l = l

        def w_in(self, h):
            return whole(self.l, "in", h)[0]

        def rest_start(self, proj):
            token = hand_to_sibling(self.l, "rest", proj)
            return start_gather(self.l + 1, token) if self.l + 1 < DEPTH else token

        def rest_finish(self, o_b):
            w_kv_all, w_out_all = whole(self.l, "rest", o_b)
            return w_kv_all, w_out_all, None

        def before_out(self, y):
            return hand_to_sibling(self.l + 1, "in", y) if self.l + 1 < DEPTH else None

    hand_to_sibling(0, "in", [start_gather(0)] + [s for layer in slots[1:] for s in layer])
    cur = xs
    for l in range(DEPTH):
        cur, saved[l] = _layer_fwd(l, cur, mems, sm, Gathered(l))
    dxo, dxo_b, loss_part = _loss_and_grad("loss", cur, target, min(256, xs.shape[0]))
    loss = lax.psum(loss_part[0, 0], ("x", "y", "c"))

    small_g = [None] * DEPTH
    flight = {}

    class Exchange:
        def __init__(self):
            self.cores = {}

        def start(self, l, group, gives):
            *self.cores[l, group], token = _core_exchange_start(f"grad_core_start_{l}_{group}", gives)
            return token

        def landed(self, l, group, after):
            send_sems, recv_sems, bufs = self.cores[l, group]
            return _core_exchange_wait(f"grad_core_wait_{l}_{group}", send_sems, recv_sems, bufs, after)[1]

        def send(self, l, group, parts):
            *flight[l, group], token = _chip_exchange_start(f"grad_chip_start_{l}_{group}", parts)
            return token

    exchange = Exchange()
    for l in reversed(range(DEPTH)):
        dxo, dxo_b, small_g[l] = _layer_bwd(l, dxo, dxo_b, mems, sm, saved[l], place, exchange)
    grad_x = dxo

    groups = (("out", ("w_out",)), ("in", ("w_in", "w_mem_kv")))
    halves, stepped = dict.fromkeys(big), dict.fromkeys(big)
    small_g = {k: jnp.stack([small_g[l][k] for l in range(DEPTH)]) for k in SMALL_NAMES}
    after = grad_x
    for l in reversed(range(DEPTH)):
        for group, names in groups:
            send_sems, recv_sems, bufs = flight[l, group]
            bufs = _chip_exchange_wait(f"grad_chip_wait_{l}_{group}", send_sems, recv_sems, bufs, after)
            for t, k in enumerate(names):
                halves[k] = _sum_chips(f"grad_chip_sum_{l}_{k}", bufs[t], bufs[len(names) + t], place, l, halves[k])
        send_sems, recv_sems, bufs, after = _core_share_start(f"grad_core_share_{l}", [halves[k] for k in big], l)
        for k, buf in zip(big, bufs):
            stepped[k] = _adamw(f"adamw_{k}_{l}_own", weights[k], buf, mom_m[k], mom_v[k], place, l, "own",
                                stepped[k], after)
            after = stepped[k][1]
        bufs = _core_share_wait(f"grad_core_shared_{l}", send_sems, recv_sems, bufs, l, after)
        for k, buf in zip(big, bufs):
            halves[k] = buf
            stepped[k] = _adamw(f"adamw_{k}_{l}_other", weights[k], buf, mom_m[k], mom_v[k], place, l, "other",
                                stepped[k], after)
            after = stepped[k][1]
        if l == DEPTH - 1:
            small_sum = _all_reduce_small(_pack_small(small_g), after)
            packed = [a[None] for a in (_pack_small(sm), small_sum, _pack_small({k: mom_m[k] for k in SMALL_NAMES}),
                                        _pack_small({k: mom_v[k] for k in SMALL_NAMES}))]
            small_step = _adamw("adamw_small", *packed, place)

    grads, delta, new_m, new_v = ({k: stepped[k][i] for k in big} for i in range(4))
    for out, packed in zip((grads, delta, new_m, new_v), small_step):
        out.update(_unpack_small(packed[0], sm))
    return (loss, grad_x[None], *[grads[k] for k in WEIGHT_ORDER], *[delta[k] for k in WEIGHT_ORDER],
            *[new_m[k] for k in WEIGHT_ORDER], *[new_v[k] for k in WEIGHT_ORDER])
```

```python
import functools
import math

import jax
import jax.numpy as jnp
from jax import lax
from jax.experimental import pallas as pl
from jax.experimental.pallas import tpu as pltpu

F32 = jnp.float32
BF16 = jnp.bfloat16
MESH = pl.DeviceIdType.MESH

D_MODEL = 2048
DEPTH = 2
CHUNK = 128
D_A = 1024
A_GROUPS = 8
D_B = 512
D_C = 512
HEADS = 4
HEAD_DIM = 128
IN_WIDTH = 6144
N_CHIPS = 4
EPS = 1e-6
ATT_SCALE = 1.0 / math.sqrt(HEAD_DIM)

OFF_U, OFF_V, OFF_ZA = 0, 1024, 2048
OFF_QB, OFF_KB, OFF_VB, OFF_ZB = 3072, 3584, 4096, 4608
OFF_QC, OFF_ZC = 5120, 5632
OFF_YB, OFF_YC = 1024, 1536

ADAM_LR = 0.001
ADAM_B1 = 0.9
ADAM_B2 = 0.999
ADAM_EPS = 1e-08
ADAM_WD = 0.01
ADAM_STEP = 10

MIB = 1024 * 1024
ANY = pl.BlockSpec(memory_space=pl.ANY)


def _params(semantics=None, vmem_mb=48):
    return pltpu.CompilerParams(dimension_semantics=semantics, vmem_limit_bytes=vmem_mb * MIB)


def _gelu(x):
    return 0.5 * x * (1.0 + lax.erf(x * (1.0 / math.sqrt(2.0))))


def _gelu_grad(x):
    cdf = 0.5 * (1.0 + lax.erf(x * (1.0 / math.sqrt(2.0))))
    pdf = jnp.exp(-0.5 * x * x) * (1.0 / math.sqrt(2.0 * math.pi))
    return cdf + x * pdf


def _sigmoid(x):
    return 1.0 / (1.0 + jnp.exp(-x))


def _silu_and_grad(z):
    s = _sigmoid(z)
    return z * s, s * (1.0 + z * (1.0 - s))


def _split_bf16(x):
    hi = x.astype(BF16)
    lo = (x - hi.astype(F32)).astype(BF16)
    return hi, lo


def _dot(a, b, dims):
    return lax.dot_general(a, b, (dims, ((), ())), preferred_element_type=F32)


NN = ((1,), (0,))
NT = ((1,), (1,))
TN = ((0,), (0,))


def _matmul(name, a, b, *, grid, a_spec, b_spec, o_spec, out_shape, dims, res=None, res_spec=None, after=None,
            place=None, vmem_mb=48):
    nk = grid[2]
    n_in = 2 + (res is not None) + (after is not None)

    def body(*refs):
        if place is not None:
            refs = refs[1:]
        a_ref, b_ref = refs[0], refs[1]
        r_ref = refs[2] if res is not None else None
        o_ref = refs[n_in]
        if len(b_ref.shape) == 3 and dims == NN:
            part = _dot(a_ref[...], b_ref[...].reshape(-1, b_ref.shape[-1]), dims)
        elif len(b_ref.shape) == 3:
            width = b_ref.shape[-1]
            part = None
            for s in range(b_ref.shape[0]):
                term = _dot(a_ref[:, s * width:(s + 1) * width], b_ref[s], dims)
                part = term if part is None else part + term
        else:
            part = _dot(a_ref[...], b_ref[...], dims)
        if nk == 1:
            if r_ref is not None:
                part = part + r_ref[...]
            o_ref[...] = part.astype(o_ref.dtype)
            return
        acc_ref = refs[n_in + 1]
        k = pl.program_id(2)

        @pl.when(k == 0)
        def _():
            acc_ref[...] = part

        @pl.when(k > 0)
        def _():
            acc_ref[...] += part

        @pl.when(k == nk - 1)
        def _():
            tot = acc_ref[...]
            if r_ref is not None:
                tot = tot + r_ref[...]
            o_ref[...] = tot.astype(o_ref.dtype)

    in_specs = [a_spec, b_spec]
    args = [a, b]
    if res is not None:
        in_specs.append(res_spec)
        args.append(res)
    if after is not None:
        in_specs.append(ANY)
        args.append(after)
    acc_shape = tuple(d for d in o_spec.block_shape if d is not None)
    scratch = [pltpu.VMEM(acc_shape, F32)] if nk > 1 else []
    params = _params(("parallel", "parallel", "arbitrary"), vmem_mb)
    if place is not None:
        return pl.pallas_call(
            body, name=name, out_shape=out_shape, compiler_params=params,
            grid_spec=pltpu.PrefetchScalarGridSpec(num_scalar_prefetch=1, grid=grid, in_specs=in_specs,
                                                   out_specs=o_spec, scratch_shapes=scratch),
        )(place, *args)
    return pl.pallas_call(
        body, name=name, grid=grid, in_specs=in_specs, out_specs=o_spec, out_shape=out_shape,
        scratch_shapes=scratch, compiler_params=params,
    )(*args)


def _rms_fwd(name, x, g, tr, after=None, transposed=False):
    rows, d = x.shape

    def body(x_ref, g_ref, *refs):
        outs = refs[1:] if after is not None else refs
        xv = x_ref[...]
        r = lax.rsqrt(jnp.mean(xv * xv, axis=-1, keepdims=True) + EPS)
        h = xv * r * g_ref[...]
        outs[0][...] = h.astype(BF16)
        if transposed:
            outs[1][...] = h.T.astype(BF16)

    out_specs = [pl.BlockSpec((tr, d), lambda i: (i, 0))]
    out_shape = [jax.ShapeDtypeStruct((rows, d), BF16)]
    if transposed:
        out_specs.append(pl.BlockSpec((d, tr), lambda i: (0, i)))
        out_shape.append(jax.ShapeDtypeStruct((d, rows), BF16))
    outs = pl.pallas_call(
        body, name=name, grid=(rows // tr,),
        in_specs=[pl.BlockSpec((tr, d), lambda i: (i, 0)), pl.BlockSpec((1, d), lambda i: (0, 0))]
        + ([] if after is None else [ANY]),
        out_specs=out_specs, out_shape=out_shape,
        compiler_params=_params(("parallel",)),
    )(x, g, *([] if after is None else [after]))
    return outs if transposed else outs[0]


def _rms_bwd(name, x, dh, dres, g, tr, after=None):
    rows, d = x.shape

    def body(x_ref, dh_ref, dres_ref, g_ref, *refs):
        dx_ref, dxb_ref, dg_ref = refs[-3:]
        xv = x_ref[...]
        r = lax.rsqrt(jnp.mean(xv * xv, axis=-1, keepdims=True) + EPS)
        xhat = xv * r
        dhv = dh_ref[...]
        dxh = dhv * g_ref[...]
        dx = r * (dxh - xhat * jnp.mean(dxh * xhat, axis=-1, keepdims=True)) + dres_ref[...]
        dx_ref[...] = dx
        dxb_ref[...] = dx.astype(BF16)
        part = jnp.sum(dhv * xhat, axis=0, keepdims=True)

        @pl.when(pl.program_id(0) == 0)
        def _():
            dg_ref[...] = part

        @pl.when(pl.program_id(0) > 0)
        def _():
            dg_ref[...] += part

    blk = pl.BlockSpec((tr, d), lambda i: (i, 0))
    vec = pl.BlockSpec((1, d), lambda i: (0, 0))
    return pl.pallas_call(
        body, name=name, grid=(rows // tr,), in_specs=[blk, blk, blk, vec] + ([] if after is None else [ANY]),
        out_specs=[blk, blk, vec],
        out_shape=[jax.ShapeDtypeStruct((rows, d), F32), jax.ShapeDtypeStruct((rows, d), BF16),
                   jax.ShapeDtypeStruct((1, d), F32)],
        compiler_params=_params(("arbitrary",)),
    )(x, dh, dres, g, *([] if after is None else [after]))


def _rms_gain_grad(name, x, dh):
    rows, d = x.shape

    def body(x_ref, dh_ref, dg_ref):
        xv = x_ref[...]
        r = lax.rsqrt(jnp.mean(xv * xv, axis=-1, keepdims=True) + EPS)
        dg_ref[...] = jnp.sum(dh_ref[...] * xv * r, axis=0, keepdims=True)

    return pl.pallas_call(
        body, name=name, out_shape=jax.ShapeDtypeStruct((1, d), F32), compiler_params=_params(None),
    )(x, dh)


def _loss_and_grad(name, y, target, tr):
    rows, d = y.shape
    n = rows // tr

    def body(y_ref, t_ref, dx_ref, dxb_ref, loss_ref, acc_ref):
        e = y_ref[...] - t_ref[...]
        dx = e * (1.0 / d)
        dx_ref[...] = dx
        dxb_ref[...] = dx.astype(BF16)
        part = jnp.sum(e * e, axis=0, keepdims=True)
        i = pl.program_id(0)

        @pl.when(i == 0)
        def _():
            acc_ref[...] = part

        @pl.when(i > 0)
        def _():
            acc_ref[...] += part

        @pl.when(i == n - 1)
        def _():
            loss_ref[...] = jnp.sum(acc_ref[...], axis=-1, keepdims=True) * (0.5 / d)

    blk = pl.BlockSpec((tr, d), lambda i: (i, 0))
    return pl.pallas_call(
        body, name=name, grid=(n,), in_specs=[blk, blk],
        out_specs=[blk, blk, pl.BlockSpec((1, 1), lambda i: (0, 0))],
        out_shape=[jax.ShapeDtypeStruct((rows, d), F32), jax.ShapeDtypeStruct((rows, d), BF16),
                   jax.ShapeDtypeStruct((1, 1), F32)],
        scratch_shapes=[pltpu.VMEM((1, d), F32)],
        compiler_params=_params(("arbitrary",)),
    )(y, target)


SB_T = 256
SB_HEADS = 4


def _sb_scores(q, kblk):
    z = _dot(q, kblk, NT) * ATT_SCALE
    e = jnp.exp(-jnp.abs(z))
    sp = jnp.log1p(e)
    lb = jnp.minimum(z, 0.0) - sp
    l1 = lb - z
    return z, e, lb, l1


def _sb_fwd(name, proj, after=None):
    s_len = proj.shape[0]
    t = SB_T
    nq = s_len // t

    def body(q_ref, k_ref, v_ref, *refs):
        o_ref = refs[-1]
        i = pl.program_id(1)
        row = lax.broadcasted_iota(jnp.int32, (t, t), 0)
        col = lax.broadcasted_iota(jnp.int32, (t, t), 1)
        causal = col < row
        after_mat = (row > col).astype(BF16)
        heads = [slice(hh * HEAD_DIM, (hh + 1) * HEAD_DIM) for hh in range(SB_HEADS)]
        q = [q_ref[:, sl].astype(BF16) for sl in heads]

        def tile(kb, state, masked):
            start = pl.multiple_of(kb * t, t)
            out = []
            for hh, sl in enumerate(heads):
                carry, acc = state[hh]
                kblk = k_ref[pl.ds(start, t), sl].astype(BF16)
                vblk = v_ref[pl.ds(start, t), sl].astype(BF16)
                _, _, lb, l1 = _sb_scores(q[hh], kblk)
                if masked:
                    l1 = jnp.where(causal, l1, 0.0)
                hi, lo = _split_bf16(l1)
                after = _dot(hi, after_mat, NN) + _dot(lo, after_mat, NN) + carry
                a = jnp.exp(lb + after)
                if masked:
                    a = jnp.where(causal, a, 0.0)
                acc = acc + _dot(a.astype(BF16), vblk, NN)
                carry = carry + jnp.sum(l1, axis=-1, keepdims=True)
                out.append((carry, acc))
            return tuple(out)

        zero = (jnp.zeros((t, 1), F32), jnp.zeros((t, HEAD_DIM), F32))
        state = tile(i, (zero,) * SB_HEADS, True)
        state = lax.fori_loop(0, i, lambda n, st: tile(i - 1 - n, st, False), state)
        for hh, sl in enumerate(heads):
            o_ref[:, sl] = state[hh][1]

    cb = SB_HEADS * HEAD_DIM
    return pl.pallas_call(
        body, name=name, grid=(HEADS // SB_HEADS, nq),
        in_specs=[pl.BlockSpec((t, cb), lambda h, i: (i, OFF_QB // cb + h)),
                  pl.BlockSpec((s_len, cb), lambda h, i: (0, OFF_KB // cb + h)),
                  pl.BlockSpec((s_len, cb), lambda h, i: (0, OFF_VB // cb + h))] + ([] if after is None else [ANY]),
        out_specs=pl.BlockSpec((t, cb), lambda h, i: (i, h)),
        out_shape=jax.ShapeDtypeStruct((s_len, D_B), F32),
        compiler_params=_params(("parallel", "arbitrary")),
    )(proj, proj, proj, *([] if after is None else [after]))


def _sb_bwd(name, proj, dy, after=None):
    s_len = proj.shape[0]
    t = SB_T
    nq = s_len // t

    def body(q_ref, k_ref, v_ref, z_ref, dy_ref, *refs):
        dq_ref, dk_ref, dv_ref, a_ref, s_ref = refs[-5:]
        i = pl.program_id(1)

        @pl.when(i == 0)
        def _():
            dk_ref[...] = jnp.zeros_like(dk_ref)
            dv_ref[...] = jnp.zeros_like(dv_ref)

        heads = [slice(hh * HEAD_DIM, (hh + 1) * HEAD_DIM) for hh in range(SB_HEADS)]
        q = [q_ref[:, sl].astype(BF16) for sl in heads]
        silu_z, _ = _silu_and_grad(z_ref[...])
        do_all = dy_ref[...] * silu_z
        do_b = [do_all[:, sl].astype(BF16) for sl in heads]
        row = lax.broadcasted_iota(jnp.int32, (t, t), 0)
        col = lax.broadcasted_iota(jnp.int32, (t, t), 1)
        causal = col < row
        after_mat = (row > col).astype(BF16)
        before_mat = (row < col).astype(BF16)

        def weights(kb, carries, masked):
            start = pl.multiple_of(kb * t, t)
            out = []
            for hh, sl in enumerate(heads):
                kblk = k_ref[pl.ds(start, t), sl].astype(BF16)
                z, _, lb, l1 = _sb_scores(q[hh], kblk)
                if masked:
                    l1 = jnp.where(causal, l1, 0.0)
                hi, lo = _split_bf16(l1)
                after = _dot(hi, after_mat, NN) + _dot(lo, after_mat, NN) + carries[hh]
                a = jnp.exp(lb + after)
                if masked:
                    a = jnp.where(causal, a, 0.0)
                a_ref[hh, kb] = a
                s_ref[hh, kb] = z
                out.append(carries[hh] + jnp.sum(l1, axis=-1, keepdims=True))
            return tuple(out)

        carries = weights(i, (jnp.zeros((t, 1), F32),) * SB_HEADS, True)
        lax.fori_loop(0, i, lambda n, c: weights(i - 1 - n, c, False), carries)

        def grads(kb, state, masked):
            start = pl.multiple_of(kb * t, t)
            out = []
            for hh, sl in enumerate(heads):
                carry, dq = state[hh]
                kblk = k_ref[pl.ds(start, t), sl].astype(BF16)
                vblk = v_ref[pl.ds(start, t), sl].astype(BF16)
                a = a_ref[hh, kb]
                z = s_ref[hh, kb]
                g = _dot(do_b[hh], vblk, NT) * a
                ghi, glo = _split_bf16(g)
                prefix = _dot(ghi, before_mat, NN) + _dot(glo, before_mat, NN) + carry
                e = jnp.exp(-jnp.abs(z))
                inv = 1.0 / (1.0 + e)
                pos = z >= 0.0
                beta = jnp.where(pos, inv, e * inv)
                one_m_beta = jnp.where(pos, e * inv, inv)
                dz = (g * one_m_beta - prefix * beta) * ATT_SCALE
                if masked:
                    dz = jnp.where(causal, dz, 0.0)
                dz_b = dz.astype(BF16)
                dq = dq + _dot(dz_b, kblk, NN)
                dk_ref[pl.ds(start, t), sl] += _dot(dz_b, q[hh], TN)
                dv_ref[pl.ds(start, t), sl] += _dot(a.astype(BF16), do_b[hh], TN)
                out.append((carry + jnp.sum(g, axis=-1, keepdims=True), dq))
            return tuple(out)

        zero = (jnp.zeros((t, 1), F32), jnp.zeros((t, HEAD_DIM), F32))
        state = lax.fori_loop(0, i, lambda kb, st: grads(kb, st, False), (zero,) * SB_HEADS)
        state = grads(i, state, True)
        for hh, sl in enumerate(heads):
            dq_ref[:, sl] = state[hh][1]

    cb = SB_HEADS * HEAD_DIM
    qblk = lambda off: pl.BlockSpec((t, cb), lambda h, i: (i, off // cb + h))
    full = lambda off: pl.BlockSpec((s_len, cb), lambda h, i: (0, off // cb + h))
    out = jax.ShapeDtypeStruct((s_len, D_B), F32)
    return pl.pallas_call(
        body, name=name, grid=(HEADS // SB_HEADS, nq),
        in_specs=[qblk(OFF_QB), full(OFF_KB), full(OFF_VB), qblk(OFF_ZB), qblk(OFF_YB)]
        + ([] if after is None else [ANY]),
        out_specs=[qblk(0), full(0), full(0)],
        out_shape=[out, out, out],
        scratch_shapes=[pltpu.VMEM((SB_HEADS, nq, t, t), F32), pltpu.VMEM((SB_HEADS, nq, t, t), F32)],
        compiler_params=_params(("parallel", "arbitrary")),
    )(proj, proj, proj, proj, dy, *([] if after is None else [after]))


MEM_TQ = 512


def _qk_norm(x, g):
    r = lax.rsqrt(jnp.mean(x * x, axis=-1, keepdims=True) + EPS)
    xhat = x * r
    return xhat * g, xhat, r


def _qk_norm_bwd(dn, g, xhat, r):
    dxh = dn * g
    return r * (dxh - xhat * jnp.mean(dxh * xhat, axis=-1, keepdims=True))


def _mem_probs(q, mk, qg, kg):
    qn, qhat, rq = _qk_norm(q, qg)
    kn, khat, rk = _qk_norm(mk, kg)
    qn_b, kn_b = qn.astype(BF16), kn.astype(BF16)
    s = _dot(qn_b, kn_b, NT) * ATT_SCALE
    p = jnp.exp(s - jnp.max(s, axis=-1, keepdims=True))
    p = p / jnp.sum(p, axis=-1, keepdims=True)
    return p, qn_b, kn_b, qhat, rq, khat, rk


def _mem_fwd(name, proj, mem_kv, qg, kg):
    s_len = proj.shape[0]
    m_len = mem_kv.shape[0]
    tq = min(MEM_TQ, s_len)

    def body(q_ref, mk_ref, mv_ref, qg_ref, kg_ref, o_ref):
        p = _mem_probs(q_ref[...], mk_ref[...], qg_ref[...], kg_ref[...])[0]
        o_ref[...] = _dot(p.astype(BF16), mv_ref[...].astype(BF16), NN)

    cb = HEAD_DIM
    vec = pl.BlockSpec((1, cb), lambda h, i: (0, 0))
    return pl.pallas_call(
        body, name=name, grid=(HEADS, s_len // tq),
        in_specs=[pl.BlockSpec((tq, cb), lambda h, i: (i, OFF_QC // cb + h)),
                  pl.BlockSpec((m_len, cb), lambda h, i: (0, h)),
                  pl.BlockSpec((m_len, cb), lambda h, i: (0, HEADS + h)), vec, vec],
        out_specs=pl.BlockSpec((tq, cb), lambda h, i: (i, h)),
        out_shape=jax.ShapeDtypeStruct((s_len, D_C), F32),
        compiler_params=_params(("parallel", "parallel")),
    )(proj, mem_kv, mem_kv, qg, kg)


def _mem_bwd(name, proj, mem_kv, qg, kg, dy):
    s_len = proj.shape[0]
    m_len = mem_kv.shape[0]
    tq = min(MEM_TQ, s_len)

    def body(q_ref, mk_ref, mv_ref, qg_ref, kg_ref, z_ref, dy_ref, dq_ref, dmk_ref, dmv_ref, dqg_ref, dkg_ref):
        h, i = pl.program_id(0), pl.program_id(1)

        @pl.when(i == 0)
        def _():
            dmk_ref[...] = jnp.zeros_like(dmk_ref)
            dmv_ref[...] = jnp.zeros_like(dmv_ref)

        @pl.when((i == 0) & (h == 0))
        def _():
            dqg_ref[...] = jnp.zeros_like(dqg_ref)
            dkg_ref[...] = jnp.zeros_like(dkg_ref)

        qg, kg = qg_ref[...], kg_ref[...]
        p, qn_b, kn_b, qhat, rq, khat, rk = _mem_probs(q_ref[...], mk_ref[...], qg, kg)
        silu_z, _ = _silu_and_grad(z_ref[...])
        do_b = (dy_ref[...] * silu_z).astype(BF16)
        dmv_ref[...] += _dot(p.astype(BF16), do_b, TN)
        dp = _dot(do_b, mv_ref[...].astype(BF16), NT)
        ds = (p * (dp - jnp.sum(dp * p, axis=-1, keepdims=True)) * ATT_SCALE).astype(BF16)
        dqn = _dot(ds, kn_b, NN)
        dkn = _dot(ds, qn_b, TN)
        dq_ref[...] = _qk_norm_bwd(dqn, qg, qhat, rq)
        dmk_ref[...] += _qk_norm_bwd(dkn, kg, khat, rk)
        dqg_ref[...] += jnp.sum(dqn * qhat, axis=0, keepdims=True)
        dkg_ref[...] += jnp.sum(dkn * khat, axis=0, keepdims=True)

    cb = HEAD_DIM
    vec = pl.BlockSpec((1, cb), lambda h, i: (0, 0))
    qblk = lambda off: pl.BlockSpec((tq, cb), lambda h, i: (i, off // cb + h))
    memblk = lambda off: pl.BlockSpec((m_len, cb), lambda h, i: (0, off + h))
    return pl.pallas_call(
        body, name=name, grid=(HEADS, s_len // tq),
        in_specs=[qblk(OFF_QC), memblk(0), memblk(HEADS), vec, vec, qblk(OFF_ZC), qblk(OFF_YC)],
        out_specs=[qblk(0), memblk(0), memblk(0), vec, vec],
        out_shape=[jax.ShapeDtypeStruct((s_len, D_C), F32), jax.ShapeDtypeStruct((m_len, D_C), F32),
                   jax.ShapeDtypeStruct((m_len, D_C), F32), jax.ShapeDtypeStruct((1, cb), F32),
                   jax.ShapeDtypeStruct((1, cb), F32)],
        compiler_params=_params(("arbitrary", "arbitrary")),
    )(proj, mem_kv, mem_kv, qg, kg, proj, dy)


def _sgu_common(u_ref, v_ref, lng_ref, lnb_ref, w_ref, bias_ref):
    ug = _gelu(u_ref[...])
    vg = _gelu(v_ref[...])
    mu = jnp.mean(vg, axis=-1, keepdims=True)
    xc = vg - mu
    rstd = lax.rsqrt(jnp.mean(xc * xc, axis=-1, keepdims=True) + EPS)
    xhat = xc * rstd
    vn = xhat * lng_ref[...] + lnb_ref[...]
    vn_b = vn.astype(BF16)
    row = lax.broadcasted_iota(jnp.int32, (CHUNK, CHUNK), 0)
    col = lax.broadcasted_iota(jnp.int32, (CHUNK, CHUNK), 1)
    tril = row >= col
    mixed = []
    for g in range(A_GROUPS):
        w = jnp.where(tril, w_ref[g], 0.0).astype(BF16)
        sl = slice(g * CHUNK, (g + 1) * CHUNK)
        mixed.append(_dot(w, vn_b[:, sl], NN) + bias_ref[:, sl])
    return ug, xhat, rstd, vn_b, mixed, tril


def _gate_fwd(name, proj, o_b, o_c, lng, lnb, w_s, bias):
    s_len = proj.shape[0]

    def body(u_ref, v_ref, za_ref, zb_ref, zc_ref, ob_ref, oc_ref, lng_ref, lnb_ref, w_ref, bias_ref, y_ref, yt_ref):
        ug, _, _, _, mixed, _ = _sgu_common(u_ref, v_ref, lng_ref, lnb_ref, w_ref, bias_ref)
        sza, _ = _silu_and_grad(za_ref[...])
        gate = ug * sza

        def put(off, width, val):
            y_ref[:, off:off + width] = val.astype(BF16)
            yt_ref[off:off + width, :] = val.T.astype(BF16)

        for g in range(A_GROUPS):
            sl = slice(g * CHUNK, (g + 1) * CHUNK)
            put(g * CHUNK, CHUNK, gate[:, sl] * mixed[g])
        szb, _ = _silu_and_grad(zb_ref[...])
        put(OFF_YB, D_B, ob_ref[...] * szb)
        szc, _ = _silu_and_grad(zc_ref[...])
        put(OFF_YC, D_C, oc_ref[...] * szc)

    wide = lambda off: pl.BlockSpec((CHUNK, D_A), lambda i: (i, off // D_A))
    narrow = lambda off: pl.BlockSpec((CHUNK, D_B), lambda i: (i, off // D_B))
    vec = pl.BlockSpec((1, D_A), lambda i: (0, 0))
    return pl.pallas_call(
        body, name=name, grid=(s_len // CHUNK,),
        in_specs=[wide(OFF_U), wide(OFF_V), wide(OFF_ZA), narrow(OFF_ZB), narrow(OFF_ZC), narrow(0), narrow(0), vec, vec,
                  pl.BlockSpec((A_GROUPS, CHUNK, CHUNK), lambda i: (0, 0, 0)),
                  pl.BlockSpec((CHUNK, D_A), lambda i: (0, 0))],
        out_specs=[pl.BlockSpec((CHUNK, D_MODEL), lambda i: (i, 0)), pl.BlockSpec((D_MODEL, CHUNK), lambda i: (0, i))],
        out_shape=[jax.ShapeDtypeStruct((s_len, D_MODEL), BF16), jax.ShapeDtypeStruct((D_MODEL, s_len), BF16)],
        compiler_params=_params(("parallel",)),
    )(proj, proj, proj, proj, proj, o_b, o_c, lng, lnb, w_s, bias)


def _gate_bwd(name, proj, dy, o_b, o_c, dqkv, dq_c, lng, lnb, w_s, w_s_t, bias):
    s_len = proj.shape[0]
    n = s_len // CHUNK
    dq_b, dk_b, dv_b = dqkv

    def body(u_ref, v_ref, za_ref, zb_ref, zc_ref, dya_ref, dyb_ref, dyc_ref, ob_ref, oc_ref, dq_ref, dk_ref, dv_ref,
             dqc_ref, lng_ref, lnb_ref, w_ref, wt_ref, bias_ref, dp_ref, dw_ref, dsb_ref, dlng_ref, dlnb_ref, dbias_ref):
        i = pl.program_id(0)

        @pl.when(i == 0)
        def _():
            dw_ref[...] = jnp.zeros_like(dw_ref)
            dbias_ref[...] = jnp.zeros_like(dbias_ref)
            dlng_ref[...] = jnp.zeros_like(dlng_ref)
            dlnb_ref[...] = jnp.zeros_like(dlnb_ref)

        ug, xhat, rstd, vn_b, mixed, tril = _sgu_common(u_ref, v_ref, lng_ref, lnb_ref, w_ref, bias_ref)
        za = za_ref[...]
        sza, dsza = _silu_and_grad(za)
        dya = dya_ref[...]
        mixed_all = jnp.concatenate(mixed, axis=-1)
        d_mixed = dya * ug * sza
        dp_ref[:, OFF_U:OFF_U + D_A] = (dya * mixed_all * sza * _gelu_grad(u_ref[...])).astype(BF16)
        dp_ref[:, OFF_ZA:OFF_ZA + D_A] = (dya * ug * mixed_all * dsza).astype(BF16)
        dbias_ref[...] += d_mixed
        dm_b = d_mixed.astype(BF16)
        triu = lax.broadcasted_iota(jnp.int32, (CHUNK, CHUNK), 0) <= lax.broadcasted_iota(jnp.int32, (CHUNK, CHUNK), 1)
        d_vn = []
        for g in range(A_GROUPS):
            sl = slice(g * CHUNK, (g + 1) * CHUNK)
            wt = jnp.where(triu, wt_ref[g], 0.0).astype(BF16)
            d_vn.append(_dot(wt, dm_b[:, sl], NN))
            dw_ref[g] += jnp.where(tril, _dot(dm_b[:, sl], vn_b[:, sl], NT), 0.0)
        d_vn = jnp.concatenate(d_vn, axis=-1)
        dlng_ref[...] += jnp.sum(d_vn * xhat, axis=0, keepdims=True)
        dlnb_ref[...] += jnp.sum(d_vn, axis=0, keepdims=True)
        dxh = d_vn * lng_ref[...]
        d_vg = rstd * (dxh - jnp.mean(dxh, axis=-1, keepdims=True)
                       - xhat * jnp.mean(dxh * xhat, axis=-1, keepdims=True))
        dp_ref[:, OFF_V:OFF_V + D_A] = (d_vg * _gelu_grad(v_ref[...])).astype(BF16)
        dp_ref[:, OFF_QB:OFF_QB + D_B] = dq_ref[...].astype(BF16)
        dp_ref[:, OFF_KB:OFF_KB + D_B] = dk_ref[...].astype(BF16)
        dp_ref[:, OFF_VB:OFF_VB + D_B] = dv_ref[...].astype(BF16)
        _, dszb = _silu_and_grad(zb_ref[...])
        dp_ref[:, OFF_ZB:OFF_ZB + D_B] = (dyb_ref[...] * ob_ref[...] * dszb).astype(BF16)
        dp_ref[:, OFF_QC:OFF_QC + D_C] = dqc_ref[...].astype(BF16)
        _, dszc = _silu_and_grad(zc_ref[...])
        dp_ref[:, OFF_ZC:OFF_ZC + D_C] = (dyc_ref[...] * oc_ref[...] * dszc).astype(BF16)

        @pl.when(i == n - 1)
        def _():
            ch = lax.broadcasted_iota(jnp.int32, (D_A, CHUNK), 0)
            gcol = lax.broadcasted_iota(jnp.int32, (D_A, CHUNK), 1)
            pick = (ch // (D_A // A_GROUPS) == gcol).astype(BF16)
            rest = dbias_ref[...]
            tot = jnp.zeros((CHUNK, CHUNK), F32)
            for _ in range(3):
                term = rest.astype(BF16)
                tot = tot + _dot(term, pick, NN)
                rest = rest - term.astype(F32)
            dsb_ref[...] = tot

    wide = lambda off: pl.BlockSpec((CHUNK, D_A), lambda i: (i, off // D_A))
    narrow = lambda off: pl.BlockSpec((CHUNK, D_B), lambda i: (i, off // D_B))
    vec = pl.BlockSpec((1, D_A), lambda i: (0, 0))
    wspec = pl.BlockSpec((A_GROUPS, CHUNK, CHUNK), lambda i: (0, 0, 0))
    bspec = pl.BlockSpec((CHUNK, D_A), lambda i: (0, 0))
    return pl.pallas_call(
        body, name=name, grid=(n,),
        in_specs=[wide(OFF_U), wide(OFF_V), wide(OFF_ZA), narrow(OFF_ZB), narrow(OFF_ZC),
                  wide(0), narrow(OFF_YB), narrow(OFF_YC), narrow(0), narrow(0), narrow(0), narrow(0), narrow(0),
                  narrow(0), vec, vec, wspec, wspec, bspec],
        out_specs=[pl.BlockSpec((CHUNK, IN_WIDTH), lambda i: (i, 0)), wspec,
                   pl.BlockSpec((CHUNK, CHUNK), lambda i: (0, 0)), vec, vec],
        out_shape=[jax.ShapeDtypeStruct((s_len, IN_WIDTH), BF16), jax.ShapeDtypeStruct((A_GROUPS, CHUNK, CHUNK), F32),
                   jax.ShapeDtypeStruct((CHUNK, CHUNK), F32), jax.ShapeDtypeStruct((1, D_A), F32),
                   jax.ShapeDtypeStruct((1, D_A), F32)],
        scratch_shapes=[pltpu.VMEM((CHUNK, D_A), F32)],
        compiler_params=_params(("arbitrary",)),
    )(proj, proj, proj, proj, proj, dy, dy, dy, o_b, o_c, dq_b, dk_b, dv_b, dq_c, lng, lnb, w_s, w_s_t, bias)


IN_SHARD = IN_WIDTH // N_CHIPS
ROW_SHARD = D_MODEL // N_CHIPS


def _bias_rows(sgu_b_l):
    return jnp.repeat(sgu_b_l.T, D_A // A_GROUPS, axis=1)


class _WholeWeights:
    def __init__(self, w_in_all, w_kv_all, w_out_all):
        self.weights = (w_in_all, w_kv_all, w_out_all)

    def w_in(self, h):
        return self.weights[0]

    def rest_start(self, proj):
        return None

    def rest_finish(self, o_b):
        return self.weights[1], self.weights[2], None

    def before_out(self, y):
        return None


def _layer_fwd(l, x, mem, sm, hooks):
    s_len = x.shape[0]
    m_len = mem.shape[0]
    tm = min(1024, s_len)
    tn = 768
    per = IN_SHARD // tn
    h, h_t = _rms_fwd(f"rms_fwd_{l}", x, sm["norm_g"][l][None], min(256, s_len), transposed=True)
    w_in_all = hooks.w_in(h)
    proj = _matmul(
        f"in_proj_{l}", h, w_in_all, grid=(s_len // tm, IN_WIDTH // tn, 1),
        a_spec=pl.BlockSpec((tm, D_MODEL), lambda i, j, k: (i, 0)),
        b_spec=pl.BlockSpec((None, D_MODEL, tn), lambda i, j, k: (j // per, 0, j % per)),
        o_spec=pl.BlockSpec((tm, tn), lambda i, j, k: (i, j)),
        out_shape=jax.ShapeDtypeStruct((s_len, IN_WIDTH), F32), dims=NN)
    o_b = _sb_fwd(f"sb_fwd_{l}", proj, hooks.rest_start(proj))
    w_kv_all, w_out_all, after = hooks.rest_finish(o_b)
    mem_h = _rms_fwd(f"mem_rms_fwd_{l}", mem, sm["mem_norm_g"][l][None], m_len, after)
    mem_kv = _matmul(
        f"mem_kv_{l}", mem_h, w_kv_all, grid=(1, 2, N_CHIPS),
        a_spec=pl.BlockSpec((m_len, ROW_SHARD), lambda i, j, k: (0, k)),
        b_spec=pl.BlockSpec((None, ROW_SHARD, D_C), lambda i, j, k: (k, 0, j)),
        o_spec=pl.BlockSpec((m_len, D_C), lambda i, j, k: (0, j)),
        out_shape=jax.ShapeDtypeStruct((m_len, 2 * D_C), F32), dims=NN)
    qg, kg = sm["q_norm_g"][l][None], sm["k_norm_g"][l][None]
    o_c = _mem_fwd(f"mem_fwd_{l}", proj, mem_kv, qg, kg)
    bias = _bias_rows(sm["sgu_b"][l])
    y, y_t = _gate_fwd(f"gate_fwd_{l}", proj, o_b, o_c, sm["sgu_ln_g"][l][None], sm["sgu_ln_b"][l][None],
                       sm["sgu_w"][l], bias)
    tn_o = 512
    x_next = _matmul(
        f"out_proj_{l}", y, w_out_all, grid=(s_len // tm, D_MODEL // tn_o, 1),
        a_spec=pl.BlockSpec((tm, D_MODEL), lambda i, j, k: (i, 0)),
        b_spec=pl.BlockSpec((N_CHIPS, ROW_SHARD, tn_o), lambda i, j, k: (0, 0, j)),
        o_spec=pl.BlockSpec((tm, tn_o), lambda i, j, k: (i, j)),
        out_shape=jax.ShapeDtypeStruct((s_len, D_MODEL), F32), dims=NN,
        res=x, res_spec=pl.BlockSpec((tm, tn_o), lambda i, j, k: (i, j)), after=hooks.before_out(y))
    saved = dict(x=x, h_t=h_t, proj=proj, mem_h=mem_h, mem_kv=mem_kv, o_b=o_b, o_c=o_c, y_t=y_t, bias=bias,
                 weights=(w_in_all, w_kv_all, w_out_all))
    return x_next, saved


class _NoExchange:
    def __init__(self):
        self.gave, self.kept = {}, {}

    def start(self, l, group, gives):
        self.gave[l, group] = gives
        return None

    def landed(self, l, group, after):
        return [jnp.zeros_like(g) for g in self.gave[l, group]]

    def send(self, l, group, parts):
        self.kept[l, group] = parts
        return None


def _layer_bwd(l, dxo, dxo_b, mem, sm, saved, place, exchange):
    s_len = dxo.shape[0]
    m_len = mem.shape[0]
    proj, y_t, h_t, mem_h, mem_kv = saved["proj"], saved["y_t"], saved["h_t"], saved["mem_h"], saved["mem_kv"]
    w_in_all, w_kv_all, w_out_all = saved["weights"]
    tm = min(1024, s_len)
    tn = 768
    per = IN_SHARD // tn
    half_rows = ROW_SHARD // 2

    def halves(make):
        give = lambda: make("give", lambda p: 1 - p[1], None, F32)
        keep = lambda theirs: make("keep", lambda p: p[1], theirs, BF16)
        return give, keep

    def grad_out(tag, half, theirs, dtype):
        o_spec = pl.BlockSpec((None, half_rows, 1024), lambda i, j, k, p: (i, 0, j))
        return _matmul(
            f"d_w_out_{l}_{tag}", y_t, dxo_b, grid=(N_CHIPS, D_MODEL // 1024, 1), place=place,
            a_spec=pl.BlockSpec((half_rows, s_len), lambda i, j, k, p: (2 * i + half(p), 0)),
            b_spec=pl.BlockSpec((s_len, 1024), lambda i, j, k, p: (0, j)), o_spec=o_spec,
            out_shape=jax.ShapeDtypeStruct((N_CHIPS, half_rows, D_MODEL), dtype), dims=NN,
            res=theirs, res_spec=o_spec)

    def grad_in(tag, half, theirs, dtype):
        o_spec = pl.BlockSpec((None, D_MODEL // 2, tn), lambda i, j, k, p: (j // per, 0, j % per))
        return _matmul(
            f"d_w_in_{l}_{tag}", h_t, dproj, grid=(1, IN_WIDTH // tn, 1), place=place,
            a_spec=pl.BlockSpec((D_MODEL // 2, s_len), lambda i, j, k, p: (half(p), 0)),
            b_spec=pl.BlockSpec((s_len, tn), lambda i, j, k, p: (0, j)), o_spec=o_spec,
            out_shape=jax.ShapeDtypeStruct((N_CHIPS, D_MODEL // 2, IN_SHARD), dtype), dims=NN,
            res=theirs, res_spec=o_spec)

    def grad_kv(tag, half, theirs, dtype):
        o_spec = pl.BlockSpec((None, half_rows, 2 * D_C), lambda i, j, k, p: (i, 0, 0))
        return _matmul(
            f"d_w_kv_{l}_{tag}", mem_h, dkv_b, grid=(N_CHIPS, 1, 1), place=place,
            a_spec=pl.BlockSpec((m_len, half_rows), lambda i, j, k, p: (0, 2 * i + half(p))),
            b_spec=pl.BlockSpec((m_len, 2 * D_C), lambda i, j, k, p: (0, 0)), o_spec=o_spec,
            out_shape=jax.ShapeDtypeStruct((N_CHIPS, half_rows, 2 * D_C), dtype), dims=TN,
            res=theirs, res_spec=o_spec)

    give_out, keep_out = halves(grad_out)
    token = exchange.start(l, "out", [give_out()])
    dy = _matmul(
        f"d_y_{l}", dxo_b, w_out_all, grid=(s_len // tm, N_CHIPS, 1),
        a_spec=pl.BlockSpec((tm, D_MODEL), lambda i, j, k: (i, 0)),
        b_spec=pl.BlockSpec((None, ROW_SHARD, D_MODEL), lambda i, j, k: (j, 0, 0)),
        o_spec=pl.BlockSpec((tm, ROW_SHARD), lambda i, j, k: (i, j)),
        out_shape=jax.ShapeDtypeStruct((s_len, D_MODEL), F32), dims=NT, after=token)
    (theirs_out,) = exchange.landed(l, "out", dy)
    token = exchange.send(l, "out", [keep_out(theirs_out)])
    qg, kg = sm["q_norm_g"][l][None], sm["k_norm_g"][l][None]
    dqkv = _sb_bwd(f"sb_bwd_{l}", proj, dy, token)
    dq_c, dmk, dmv, dqg, dkg = _mem_bwd(f"mem_bwd_{l}", proj, mem_kv, qg, kg, dy)
    w_s = sm["sgu_w"][l]
    dproj, dws, dbias, dlng, dlnb = _gate_bwd(
        f"gate_bwd_{l}", proj, dy, saved["o_b"], saved["o_c"], dqkv, dq_c, sm["sgu_ln_g"][l][None],
        sm["sgu_ln_b"][l][None], w_s, jnp.swapaxes(w_s, 1, 2), saved["bias"])
    dkv_b = jnp.concatenate([dmk, dmv], axis=1).astype(BF16)
    give_in, keep_in = halves(grad_in)
    give_kv, keep_kv = halves(grad_kv)
    token = exchange.start(l, "in", [give_in(), give_kv()])
    dh = _matmul(
        f"d_h_{l}", dproj, w_in_all, grid=(s_len // tm, D_MODEL // 512, 1),
        a_spec=pl.BlockSpec((tm, IN_WIDTH), lambda i, j, k: (i, 0)),
        b_spec=pl.BlockSpec((N_CHIPS, 512, IN_SHARD), lambda i, j, k: (0, j, 0)),
        o_spec=pl.BlockSpec((tm, 512), lambda i, j, k: (i, j)),
        out_shape=jax.ShapeDtypeStruct((s_len, D_MODEL), F32), dims=NT, after=token, vmem_mb=56)
    theirs_in, theirs_kv = exchange.landed(l, "in", dh)
    token = exchange.send(l, "in", [keep_in(theirs_in), keep_kv(theirs_kv)])
    dx, dx_b, dng = _rms_bwd(f"rms_bwd_{l}", saved["x"], dh, dxo, sm["norm_g"][l][None], min(256, s_len), token)
    d_mem_h = _matmul(
        f"d_mem_h_{l}", dkv_b, w_kv_all, grid=(1, N_CHIPS, 1),
        a_spec=pl.BlockSpec((m_len, 2 * D_C), lambda i, j, k: (0, 0)),
        b_spec=pl.BlockSpec((None, ROW_SHARD, 2 * D_C), lambda i, j, k: (j, 0, 0)),
        o_spec=pl.BlockSpec((m_len, ROW_SHARD), lambda i, j, k: (0, j)),
        out_shape=jax.ShapeDtypeStruct((m_len, D_MODEL), F32), dims=NT)
    dmng = _rms_gain_grad(f"mem_rms_bwd_{l}", mem, d_mem_h)
    dsgu_b = dbias[:, :A_GROUPS].T
    small = dict(norm_g=dng[0], sgu_ln_g=dlng[0], sgu_ln_b=dlnb[0], sgu_w=dws, sgu_b=dsgu_b, mem_norm_g=dmng[0],
                 q_norm_g=dqg[0], k_norm_g=dkg[0])
    return dx, dx_b, small


SMALL_NAMES = ("norm_g", "sgu_ln_g", "sgu_ln_b", "sgu_w", "sgu_b", "mem_norm_g", "q_norm_g", "k_norm_g")


def _local_step(x, mem, target, sm, w_all):
    saved = []
    cur = x
    for l in range(DEPTH):
        cur, sv = _layer_fwd(l, cur, mem, sm, _WholeWeights(*w_all[l]))
        saved.append(sv)
    dxo, dxo_b, loss = _loss_and_grad("loss", cur, target, min(256, x.shape[0]))
    small = [None] * DEPTH
    exchange = _NoExchange()
    place = jnp.zeros((2,), jnp.int32)
    for l in reversed(range(DEPTH)):
        dxo, dxo_b, small[l] = _layer_bwd(l, dxo, dxo_b, mem, sm, saved[l], place, exchange)
    small = {k: jnp.stack([small[l][k] for l in range(DEPTH)]) for k in SMALL_NAMES}
    return loss, dxo, small, exchange.gave, exchange.kept


def _place():
    x, y, c = lax.axis_index("x"), lax.axis_index("y"), lax.axis_index("c")
    return x, y, c


def _other_chips(x, y):
    return [(1 - x, y, 2 * (1 - x) + y), (x, 1 - y, 2 * x + 1 - y), (1 - x, 1 - y, 2 * (1 - x) + 1 - y)]


D2D_CHUNKS = 8


def _place_index():
    return jnp.stack([2 * lax.axis_index("x") + lax.axis_index("y"), lax.axis_index("c")]).astype(jnp.int32)


def _cast_into_slot(name, w, l, place):
    _, rows, cols = w.shape
    tr = min(256, rows)

    def body(p_ref, w_ref, o_ref):
        o_ref[...] = w_ref[...].astype(BF16)

    return pl.pallas_call(
        body, name=name,
        grid_spec=pltpu.PrefetchScalarGridSpec(
            num_scalar_prefetch=1, grid=(rows // tr,),
            in_specs=[pl.BlockSpec((None, tr, cols), lambda i, p: (l, i, 0))],
            out_specs=pl.BlockSpec((None, tr, cols), lambda i, p: (p[0], i, 0))),
        out_shape=jax.ShapeDtypeStruct((N_CHIPS, rows, cols), BF16),
        compiler_params=_params(("parallel",)),
    )(place, w)


HBM = pl.BlockSpec(memory_space=pltpu.HBM)
SEM = pl.BlockSpec(memory_space=pltpu.SEMAPHORE)
DATAFLOW = pltpu.SideEffectType.DATAFLOW_SIDE_EFFECTING


def _in_hbm(a):
    return pltpu.with_memory_space_constraint(a, pltpu.HBM)


def _chip_copies_start(name, srcs, lands, make_copy, after=None):
    n_t = len(srcs)
    in_place = lands is None
    n_after = 0 if after is None else 1

    def body(*refs):
        src = refs[:n_t]
        k = (n_t if in_place else 2 * n_t) + n_after
        send_sems, recv_sems = refs[k], refs[k + 1]
        land = refs[k + 2:k + 2 + n_t] if in_place else refs[k + 2 + n_t:k + 2 + 2 * n_t]
        token = refs[-1]
        x, y, c = _place()
        me = 2 * x + y
        for t in range(n_t):
            for px, py, pk in _other_chips(x, y):
                s, d = make_copy(src[t], land[t], me, pk, c)
                pltpu.make_async_remote_copy(
                    src_ref=s, dst_ref=d, send_sem=send_sems.at[t], recv_sem=recv_sems.at[t],
                    device_id=(px, py, c), device_id_type=MESH).start()
        token[...] = jnp.zeros_like(token)

    bufs = list(srcs) if in_place else list(srcs) + list(lands)
    outs = pl.pallas_call(
        body, name=name,
        in_specs=[HBM] * len(bufs) + [ANY] * n_after,
        out_specs=[SEM, SEM] + [HBM] * len(bufs) + [pl.BlockSpec(memory_space=pltpu.VMEM)],
        out_shape=[pltpu.SemaphoreType.DMA((n_t,)), pltpu.SemaphoreType.DMA((n_t,))]
        + [pltpu.HBM(b.shape, b.dtype) for b in bufs] + [jax.ShapeDtypeStruct((8, 128), F32)],
        input_output_aliases={i: 2 + i for i in range(len(bufs))},
        compiler_params=pltpu.CompilerParams(has_side_effects=DATAFLOW),
    )(*[_in_hbm(b) for b in bufs], *([] if after is None else [after]))
    return outs[0], outs[1], list(outs[2:2 + len(bufs)]), outs[-1]


def _chip_copies_wait(name, send_sems, recv_sems, bufs, sent, landed, after):
    n_b = len(bufs)

    def body(*refs):
        buf = refs[:n_b]
        send_ref, recv_ref = refs[n_b], refs[n_b + 1]
        x, y, c = _place()
        for t, (s, d) in enumerate(zip(sent(buf), landed(buf))):
            out = pltpu.make_async_remote_copy(src_ref=s, dst_ref=s, send_sem=send_ref.at[t], recv_sem=recv_ref.at[t],
                                               device_id=(x, y, c), device_id_type=MESH)
            out.wait_send()
            arrived = pltpu.make_async_remote_copy(src_ref=d, dst_ref=d, send_sem=send_ref.at[t],
                                                   recv_sem=recv_ref.at[t], device_id=(x, y, c), device_id_type=MESH)
            arrived.wait_recv()

    after = list(after) if isinstance(after, (list, tuple)) else [after]
    return pl.pallas_call(
        body, name=name,
        in_specs=[HBM] * n_b + [SEM, SEM] + [ANY] * len(after), out_specs=[HBM] * n_b,
        out_shape=[pltpu.HBM(b.shape, b.dtype) for b in bufs],
        input_output_aliases={i: i for i in range(n_b)},
        compiler_params=pltpu.CompilerParams(has_side_effects=DATAFLOW),
    )(*bufs, send_sems, recv_sems, *after)


def _gather_start(name, bufs, after=None):
    def make_copy(src, land, me, pk, c):
        hr = src.shape[1] // 2
        return src.at[me, pl.ds(c * hr, hr)], land.at[me, pl.ds(c * hr, hr)]

    return _chip_copies_start(name, bufs, None, make_copy, after)


def _gather_wait(name, send_sems, recv_sems, bufs, after):
    def three_halves(buf):
        return [b.at[pl.ds(0, 3), pl.ds(0, b.shape[1] // 2)] for b in buf]

    return _chip_copies_wait(name, send_sems, recv_sems, bufs, three_halves, three_halves, after)


def _gather_forward_start(name, bufs):
    n_t = len(bufs)

    def body(*refs):
        mine = refs[:n_t]
        send_sems, recv_sems = refs[n_t], refs[n_t + 1]
        buf = refs[n_t + 2:2 * n_t + 2]
        token = refs[-1]
        x, y, c = _place()
        for q in range(D2D_CHUNKS):
            for t in range(n_t):
                hr = mine[t].shape[1] // 2
                cr = hr // D2D_CHUNKS
                rows = pl.ds(c * hr + q * cr, cr)
                for _, _, pk in _other_chips(x, y):
                    pltpu.make_async_remote_copy(
                        src_ref=mine[t].at[pk, rows], dst_ref=buf[t].at[pk, rows], send_sem=send_sems.at[t],
                        recv_sem=recv_sems.at[t], device_id=(x, y, 1 - c), device_id_type=MESH).start()
        token[...] = jnp.zeros_like(token)

    outs = pl.pallas_call(
        body, name=name,
        in_specs=[HBM] * n_t,
        out_specs=[SEM, SEM] + [HBM] * n_t + [pl.BlockSpec(memory_space=pltpu.VMEM)],
        out_shape=[pltpu.SemaphoreType.DMA((n_t,)), pltpu.SemaphoreType.DMA((n_t,))]
        + [pltpu.HBM(b.shape, b.dtype) for b in bufs] + [jax.ShapeDtypeStruct((8, 128), F32)],
        input_output_aliases={i: 2 + i for i in range(n_t)},
        compiler_params=pltpu.CompilerParams(has_side_effects=DATAFLOW),
    )(*[_in_hbm(b) for b in bufs])
    return outs[0], outs[1], list(outs[2:2 + n_t]), outs[-1]


def _core_exchange_start(name, grads):
    n_t = len(grads)
    lands = [lax.empty(g.shape, g.dtype) for g in grads]

    def body(*refs):
        src = refs[:n_t]
        send_sems, recv_sems = refs[2 * n_t], refs[2 * n_t + 1]
        land = refs[2 * n_t + 2 + n_t:2 * n_t + 2 + 2 * n_t]
        token = refs[-1]
        x, y, c = _place()
        for q in range(D2D_CHUNKS):
            for t in range(n_t):
                cr = src[t].shape[1] // D2D_CHUNKS
                rows = pl.ds(q * cr, cr)
                pltpu.make_async_remote_copy(
                    src_ref=src[t].at[:, rows], dst_ref=land[t].at[:, rows], send_sem=send_sems.at[t],
                    recv_sem=recv_sems.at[t], device_id=(x, y, 1 - c), device_id_type=MESH).start()
        token[...] = jnp.zeros_like(token)

    bufs = list(grads) + lands
    outs = pl.pallas_call(
        body, name=name,
        in_specs=[HBM] * len(bufs),
        out_specs=[SEM, SEM] + [HBM] * len(bufs) + [pl.BlockSpec(memory_space=pltpu.VMEM)],
        out_shape=[pltpu.SemaphoreType.DMA((n_t,)), pltpu.SemaphoreType.DMA((n_t,))]
        + [pltpu.HBM(b.shape, b.dtype) for b in bufs] + [jax.ShapeDtypeStruct((8, 128), F32)],
        input_output_aliases={i: 2 + i for i in range(len(bufs))},
        compiler_params=pltpu.CompilerParams(has_side_effects=DATAFLOW),
    )(*[_in_hbm(b) for b in bufs])
    return outs[0], outs[1], list(outs[2:2 + len(bufs)]), outs[-1]


def _core_exchange_wait(name, send_sems, recv_sems, bufs, after):
    n_t = len(bufs) // 2

    def body(*refs):
        land = refs[n_t:2 * n_t]
        send_ref, recv_ref = refs[2 * n_t], refs[2 * n_t + 1]
        x, y, c = _place()
        for t in range(n_t):
            whole = pltpu.make_async_remote_copy(src_ref=land[t], dst_ref=land[t], send_sem=send_ref.at[t],
                                                 recv_sem=recv_ref.at[t], device_id=(x, y, c), device_id_type=MESH)
            whole.wait_send()
            whole.wait_recv()

    outs = pl.pallas_call(
        body, name=name,
        in_specs=[HBM] * (2 * n_t) + [SEM, SEM, ANY], out_specs=[HBM] * (2 * n_t),
        out_shape=[pltpu.HBM(b.shape, b.dtype) for b in bufs],
        input_output_aliases={i: i for i in range(2 * n_t)},
        compiler_params=pltpu.CompilerParams(has_side_effects=DATAFLOW),
    )(*bufs, send_sems, recv_sems, after)
    return list(outs[:n_t]), list(outs[n_t:])


def _chip_exchange_start(name, parts):
    lands = [lax.empty(p.shape, p.dtype) for p in parts]
    return _chip_copies_start(name, parts, lands, lambda src, land, me, pk, c: (src.at[pk], land.at[me]))


def _chip_exchange_wait(name, send_sems, recv_sems, bufs, after):
    n_t = len(bufs) // 2
    return _chip_copies_wait(name, send_sems, recv_sems, bufs,
                             lambda buf: [b.at[pl.ds(0, 3)] for b in buf[:n_t]],
                             lambda buf: [b.at[pl.ds(0, 3)] for b in buf[n_t:]], after)


def _sum_chips(name, parts, landed, place, l, stacked):
    chips, rows, cols = landed.shape
    tr = min(256, rows)
    per = rows // tr

    def body(p_ref, own_ref, *refs):
        land, o_ref = refs[:chips], refs[-1]
        tot = None
        for k in range(chips):
            term = jnp.where(p_ref[0] == k, own_ref[...], land[k][...]).astype(F32)
            tot = term if tot is None else tot + term
        o_ref[...] = tot

    def from_chip(k):
        return pl.BlockSpec((None, tr, cols), lambda i, p: (jnp.where(p[0] == k, (k + 1) % chips, k), i, 0))

    in_specs = [pl.BlockSpec((None, tr, cols), lambda i, p: (p[0], i, 0))] + [from_chip(k) for k in range(chips)]
    args = [parts] + [landed] * chips
    aliases = {}
    if stacked is not None:
        in_specs.append(ANY)
        args.append(stacked)
        aliases = {len(args): 0}
    return pl.pallas_call(
        body, name=name,
        grid_spec=pltpu.PrefetchScalarGridSpec(
            num_scalar_prefetch=1, grid=(per,), in_specs=in_specs,
            out_specs=pl.BlockSpec((None, tr, cols), lambda i, p: (l, p[1] * per + i, 0))),
        out_shape=jax.ShapeDtypeStruct((DEPTH, 2 * rows, cols), F32), input_output_aliases=aliases,
        compiler_params=_params(("parallel",)),
    )(place, *args)


def _core_share_start(name, bufs, l):
    n_t = len(bufs)

    def body(*refs):
        mine = refs[:n_t]
        send_sems, recv_sems = refs[n_t], refs[n_t + 1]
        buf = refs[n_t + 2:2 * n_t + 2]
        token = refs[-1]
        x, y, c = _place()
        for q in range(D2D_CHUNKS):
            for t in range(n_t):
                hr = mine[t].shape[1] // 2
                cr = hr // D2D_CHUNKS
                rows = pl.ds(c * hr + q * cr, cr)
                pltpu.make_async_remote_copy(
                    src_ref=mine[t].at[l, rows], dst_ref=buf[t].at[l, rows], send_sem=send_sems.at[t],
                    recv_sem=recv_sems.at[t], device_id=(x, y, 1 - c), device_id_type=MESH).start()
        token[...] = jnp.zeros_like(token)

    outs = pl.pallas_call(
        body, name=name,
        in_specs=[HBM] * n_t,
        out_specs=[SEM, SEM] + [HBM] * n_t + [pl.BlockSpec(memory_space=pltpu.VMEM)],
        out_shape=[pltpu.SemaphoreType.DMA((n_t,)), pltpu.SemaphoreType.DMA((n_t,))]
        + [pltpu.HBM(b.shape, b.dtype) for b in bufs] + [jax.ShapeDtypeStruct((8, 128), F32)],
        input_output_aliases={i: 2 + i for i in range(n_t)},
        compiler_params=pltpu.CompilerParams(has_side_effects=DATAFLOW),
    )(*[_in_hbm(b) for b in bufs])
    return outs[0], outs[1], list(outs[2:2 + n_t]), outs[-1]


def _core_share_wait(name, send_sems, recv_sems, bufs, l, after):
    def half_layer(buf):
        return [b.at[l, pl.ds(0, b.shape[1] // 2)] for b in buf]

    return _chip_copies_wait(name, send_sems, recv_sems, bufs, half_layer, half_layer, after)


def _all_reduce_small(vec, after=None):
    rows, lanes = vec.shape
    hr = rows // 2

    def body(v_ref, *refs):
        o_ref, sib_ref, chips_ref, send_sems, recv_sems = refs[-5:]
        x, y, c = _place()
        me = 2 * x + y
        sibling = (x, y, 1 - c)
        mine = pl.ds(pl.multiple_of(c * hr, 8), hr)
        theirs = pl.ds(pl.multiple_of((1 - c) * hr, 8), hr)
        swap = pltpu.make_async_remote_copy(
            src_ref=v_ref.at[theirs], dst_ref=sib_ref, send_sem=send_sems.at[0], recv_sem=recv_sems.at[0],
            device_id=sibling, device_id_type=MESH)
        swap.start()
        swap.wait_recv()
        chips_ref[me] = v_ref[mine] + sib_ref[...]
        copies = []
        for j, (px, py, pk) in enumerate(_other_chips(x, y)):
            cp = pltpu.make_async_remote_copy(
                src_ref=chips_ref.at[me], dst_ref=chips_ref.at[me], send_sem=send_sems.at[1 + j],
                recv_sem=recv_sems.at[1 + j], device_id=(px, py, c), device_id_type=MESH)
            cp.start()
            copies.append(cp)
        for j, (px, py, pk) in enumerate(_other_chips(x, y)):
            pltpu.make_async_remote_copy(
                src_ref=chips_ref.at[pk], dst_ref=chips_ref.at[pk], send_sem=send_sems.at[1 + j],
                recv_sem=recv_sems.at[1 + j], device_id=(px, py, c), device_id_type=MESH).wait_recv()
        tot = chips_ref[0]
        for k in range(1, N_CHIPS):
            tot = tot + chips_ref[k]
        o_ref[mine] = tot
        share = pltpu.make_async_remote_copy(
            src_ref=o_ref.at[mine], dst_ref=o_ref.at[mine], send_sem=send_sems.at[4], recv_sem=recv_sems.at[4],
            device_id=sibling, device_id_type=MESH)
        share.start()
        pltpu.make_async_remote_copy(
            src_ref=o_ref.at[theirs], dst_ref=o_ref.at[theirs], send_sem=send_sems.at[4], recv_sem=recv_sems.at[4],
            device_id=sibling, device_id_type=MESH).wait_recv()
        swap.wait_send()
        for cp in copies:
            cp.wait_send()
        share.wait_send()

    vm = pl.BlockSpec(memory_space=pltpu.VMEM)
    return pl.pallas_call(
        body, name="small_all_reduce", in_specs=[vm] + ([] if after is None else [ANY]), out_specs=vm,
        out_shape=jax.ShapeDtypeStruct((rows, lanes), F32),
        scratch_shapes=[pltpu.VMEM((hr, lanes), F32), pltpu.VMEM((N_CHIPS, hr, lanes), F32),
                        pltpu.SemaphoreType.DMA((5,)), pltpu.SemaphoreType.DMA((5,))],
        compiler_params=pltpu.CompilerParams(has_side_effects=True, vmem_limit_bytes=48 * MIB),
    )(vec, *([] if after is None else [after]))


def _adamw(name, w, g, m, v, place, l=0, half=None, done=None, after=None):
    layers, rows, cols = w.shape
    span = rows if half is None else rows // 2
    tr = span
    for cand in (256, 128, 64, 32, 16, 8):
        if span % cand == 0:
            tr = cand
            break
    per = span // tr
    c1 = 1.0 - ADAM_B1 ** ADAM_STEP
    c2 = 1.0 - ADAM_B2 ** ADAM_STEP

    def first_block(p):
        return 0 if half is None else (p[1] if half == "own" else 1 - p[1]) * per

    def body(p_ref, w_ref, g_ref, m_ref, v_ref, *refs):
        go_ref, d_ref, nm_ref, nv_ref = refs[-4:]
        gv = g_ref[...]
        nm = ADAM_B1 * m_ref[...] + (1.0 - ADAM_B1) * gv
        nv = ADAM_B2 * v_ref[...] + (1.0 - ADAM_B2) * (gv * gv)
        go_ref[...] = gv
        nm_ref[...] = nm
        nv_ref[...] = nv
        d_ref[...] = -ADAM_LR * ((nm / c1) / (jnp.sqrt(nv / c2) + ADAM_EPS) + ADAM_WD * w_ref[...])

    blk = pl.BlockSpec((None, tr, cols), lambda i, p: (l, first_block(p) + i, 0))
    out = jax.ShapeDtypeStruct((layers, rows, cols), F32)
    extra = ([] if done is None else list(done)) + ([] if after is None else [after])
    aliases = {} if done is None else {5 + i: i for i in range(4)}
    return pl.pallas_call(
        body, name=name,
        grid_spec=pltpu.PrefetchScalarGridSpec(
            num_scalar_prefetch=1, grid=(per,), in_specs=[blk] * 4 + [ANY] * len(extra), out_specs=[blk] * 4),
        out_shape=[out] * 4, input_output_aliases=aliases,
        compiler_params=_params(("parallel",)),
    )(place, w, g, m, v, *extra)


def _pack_small(parts):
    flat = jnp.concatenate([parts[k].reshape(-1) for k in SMALL_NAMES])
    n = flat.shape[0]
    rows = -(-n // (256 * 128)) * 256
    return jnp.pad(flat, (0, rows * 128 - n)).reshape(rows, 128)


def _unpack_small(packed, like):
    flat = packed.reshape(-1)
    out, off = {}, 0
    for k in SMALL_NAMES:
        n = like[k].size
        out[k] = flat[off:off + n].reshape(like[k].shape)
        off += n
    return out


WEIGHT_ORDER = ("norm_g", "w_in", "sgu_ln_g", "sgu_ln_b", "sgu_w", "sgu_b", "mem_norm_g", "w_mem_kv", "q_norm_g",
                "k_norm_g", "w_out")


def kernel(x, mem, norm_g, w_in, sgu_ln_g, sgu_ln_b, sgu_w, sgu_b, mem_norm_g, w_mem_kv, q_norm_g, k_norm_g, w_out, loss_target, m_norm_g, m_w_in, m_sgu_ln_g, m_sgu_ln_b, m_sgu_w, m_sgu_b, m_mem_norm_g, m_w_mem_kv, m_q_norm_g, m_k_norm_g, m_w_out, v_norm_g, v_w_in, v_sgu_ln_g, v_sgu_ln_b, v_sgu_w, v_sgu_b, v_mem_norm_g, v_w_mem_kv, v_q_norm_g, v_k_norm_g, v_w_out):
    weights = dict(norm_g=norm_g, w_in=w_in, sgu_ln_g=sgu_ln_g, sgu_ln_b=sgu_ln_b, sgu_w=sgu_w, sgu_b=sgu_b,
                   mem_norm_g=mem_norm_g, w_mem_kv=w_mem_kv, q_norm_g=q_norm_g, k_norm_g=k_norm_g, w_out=w_out)
    mom_m = dict(norm_g=m_norm_g, w_in=m_w_in, sgu_ln_g=m_sgu_ln_g, sgu_ln_b=m_sgu_ln_b, sgu_w=m_sgu_w, sgu_b=m_sgu_b,
                 mem_norm_g=m_mem_norm_g, w_mem_kv=m_w_mem_kv, q_norm_g=m_q_norm_g, k_norm_g=m_k_norm_g, w_out=m_w_out)
    mom_v = dict(norm_g=v_norm_g, w_in=v_w_in, sgu_ln_g=v_sgu_ln_g, sgu_ln_b=v_sgu_ln_b, sgu_w=v_sgu_w, sgu_b=v_sgu_b,
                 mem_norm_g=v_mem_norm_g, w_mem_kv=v_w_mem_kv, q_norm_g=v_q_norm_g, k_norm_g=v_k_norm_g, w_out=v_w_out)
    big = ("w_in", "w_mem_kv", "w_out")
    sm = {k: weights[k] for k in SMALL_NAMES}

    place = _place_index()
    xs, mems, target = x[0], mem[0], loss_target[0]

    slots = [[_cast_into_slot(f"cast_{k}_{l}", weights[k], l, place) for k in big] for l in range(DEPTH)]
    saved = [None] * DEPTH

    chips, cores = {}, {}

    def start_gather(l, after=None):
        chips[l, "in"] = _gather_start(f"gather_start_{l}_in", slots[l][:1], after)
        chips[l, "rest"] = _gather_start(f"gather_start_{l}_rest", slots[l][1:], chips[l, "in"][3])
        return chips[l, "rest"][3]

    def hand_to_sibling(l, group, after):
        send_sems, recv_sems, bufs, _ = chips[l, group]
        bufs = _gather_wait(f"gather_wait_{l}_{group}", send_sems, recv_sems, bufs, after)
        cores[l, group] = _gather_forward_start(f"gather_forward_{l}_{group}", bufs)
        return cores[l, group][3]

    def whole(l, group, after):
        send_sems, recv_sems, bufs, _ = cores[l, group]
        return _gather_wait(f"gather_whole_{l}_{group}", send_sems, recv_sems, bufs, after)

    class Gathered:
        def __init__(self, l):
            self.l = l

        def w_in(self, h):
            return whole(self.l, "in", h)[0]

        def rest_start(self, proj):
            token = hand_to_sibling(self.l, "rest", proj)
            return start_gather(self.l + 1, token) if self.l + 1 < DEPTH else token

        def rest_finish(self, o_b):
            w_kv_all, w_out_all = whole(self.l, "rest", o_b)
            return w_kv_all, w_out_all, None

        def before_out(self, y):
            return hand_to_sibling(self.l + 1, "in", y) if self.l + 1 < DEPTH else None

    hand_to_sibling(0, "in", [start_gather(0)] + [s for layer in slots[1:] for s in layer])
    cur = xs
    for l in range(DEPTH):
        cur, saved[l] = _layer_fwd(l, cur, mems, sm, Gathered(l))
    dxo, dxo_b, loss_part = _loss_and_grad("loss", cur, target, min(256, xs.shape[0]))
    loss = lax.psum(loss_part[0, 0], ("x", "y", "c"))

    small_g = [None] * DEPTH
    flight = {}

    class Exchange:
        def __init__(self):
            self.cores = {}

        def start(self, l, group, gives):
            *self.cores[l, group], token = _core_exchange_start(f"grad_core_start_{l}_{group}", gives)
            return token

        def landed(self, l, group, after):
            send_sems, recv_sems, bufs = self.cores[l, group]
            return _core_exchange_wait(f"grad_core_wait_{l}_{group}", send_sems, recv_sems, bufs, after)[1]

        def send(self, l, group, parts):
            *flight[l, group], token = _chip_exchange_start(f"grad_chip_start_{l}_{group}", parts)
            return token

    exchange = Exchange()
    for l in reversed(range(DEPTH)):
        dxo, dxo_b, small_g[l] = _layer_bwd(l, dxo, dxo_b, mems, sm, saved[l], place, exchange)
    grad_x = dxo

    groups = (("out", ("w_out",)), ("in", ("w_in", "w_mem_kv")))
    halves, stepped = dict.fromkeys(big), dict.fromkeys(big)
    small_g = {k: jnp.stack([small_g[l][k] for l in range(DEPTH)]) for k in SMALL_NAMES}
    after = grad_x
    sharing = {}

    def reduce_group(l, group, names):
        nonlocal after
        send_sems, recv_sems, bufs = flight[l, group]
        bufs = _chip_exchange_wait(f"grad_chip_wait_{l}_{group}", send_sems, recv_sems, bufs, after)
        for t, k in enumerate(names):
            halves[k] = _sum_chips(f"grad_chip_sum_{l}_{k}", bufs[t], bufs[len(names) + t], place, l, halves[k])
        *sharing[l, group], after = _core_share_start(f"grad_core_share_{l}_{group}", [halves[k] for k in names], l)

    def step(l, k, buf, half):
        nonlocal after
        tag = "" if half is None else "_" + half
        stepped[k] = _adamw(f"adamw_{k}_{l}{tag}", weights[k], buf, mom_m[k], mom_v[k], place, l, half, stepped[k],
                            after)
        after = stepped[k][1]

    def step_group(l, group, names, overlap):
        nonlocal after
        send_sems, recv_sems, bufs = sharing[l, group]
        if overlap:
            for k, buf in zip(names, bufs):
                step(l, k, buf, "own")
        bufs = _core_share_wait(f"grad_core_shared_{l}_{group}", send_sems, recv_sems, bufs, l, after)
        for k, buf in zip(names, bufs):
            halves[k] = buf
            step(l, k, buf, "other" if overlap else None)

    for l in reversed(range(DEPTH)):
        last = l == 0
        (g_out, n_out), (g_in, n_in) = groups
        reduce_group(l, g_out, n_out)
        if last:
            step_group(l, g_out, n_out, False)
            small_sum = _all_reduce_small(_pack_small(small_g), after)
            packed = [a[None] for a in (_pack_small(sm), small_sum, _pack_small({k: mom_m[k] for k in SMALL_NAMES}),
                                        _pack_small({k: mom_v[k] for k in SMALL_NAMES}))]
            small_step = _adamw("adamw_small", *packed, place)
            after = small_step[1]
        reduce_group(l, g_in, n_in)
        if not last:
            step_group(l, g_out, n_out, False)
        step_group(l, g_in, n_in, last)

    grads, delta, new_m, new_v = ({k: stepped[k][i] for k in big} for i in range(4))
    for out, packed in zip((grads, delta, new_m, new_v), small_step):
        out.update(_unpack_small(packed[0], sm))
    return (loss, grad_x[None], *[grads[k] for k in WEIGHT_ORDER], *[delta[k] for k in WEIGHT_ORDER],
            *[new_m[k] for k in WEIGHT_ORDER], *[new_v[k] for k in WEIGHT_ORDER])
```

```python
import functools
import math

import jax
import jax.numpy as jnp
from jax import lax
from jax.experimental import pallas as pl
from jax.experimental.pallas import tpu as pltpu

F32 = jnp.float32
BF16 = jnp.bfloat16
MESH = pl.DeviceIdType.MESH

D_MODEL = 2048
DEPTH = 2
CHUNK = 128
D_A = 1024
A_GROUPS = 8
D_B = 512
D_C = 512
HEADS = 4
HEAD_DIM = 128
IN_WIDTH = 6144
N_CHIPS = 4
EPS = 1e-6
ATT_SCALE = 1.0 / math.sqrt(HEAD_DIM)

OFF_U, OFF_V, OFF_ZA = 0, 1024, 2048
OFF_QB, OFF_KB, OFF_VB, OFF_ZB = 3072, 3584, 4096, 4608
OFF_QC, OFF_ZC = 5120, 5632
OFF_YB, OFF_YC = 1024, 1536

ADAM_LR = 0.001
ADAM_B1 = 0.9
ADAM_B2 = 0.999
ADAM_EPS = 1e-08
ADAM_WD = 0.01
ADAM_STEP = 10

MIB = 1024 * 1024
ANY = pl.BlockSpec(memory_space=pl.ANY)


def _params(semantics=None, vmem_mb=48):
    return pltpu.CompilerParams(dimension_semantics=semantics, vmem_limit_bytes=vmem_mb * MIB)


def _gelu(x):
    return 0.5 * x * (1.0 + lax.erf(x * (1.0 / math.sqrt(2.0))))


def _gelu_grad(x):
    cdf = 0.5 * (1.0 + lax.erf(x * (1.0 / math.sqrt(2.0))))
    pdf = jnp.exp(-0.5 * x * x) * (1.0 / math.sqrt(2.0 * math.pi))
    return cdf + x * pdf


def _sigmoid(x):
    return 1.0 / (1.0 + jnp.exp(-x))


def _silu_and_grad(z):
    s = _sigmoid(z)
    return z * s, s * (1.0 + z * (1.0 - s))


def _split_bf16(x):
    hi = x.astype(BF16)
    lo = (x - hi.astype(F32)).astype(BF16)
    return hi, lo


def _dot(a, b, dims):
    return lax.dot_general(a, b, (dims, ((), ())), preferred_element_type=F32)


NN = ((1,), (0,))
NT = ((1,), (1,))
TN = ((0,), (0,))


def _matmul(name, a, b, *, grid, a_spec, b_spec, o_spec, out_shape, dims, res=None, res_spec=None, after=None,
            place=None, vmem_mb=48):
    nk = grid[2]
    n_in = 2 + (res is not None) + (after is not None)

    def body(*refs):
        if place is not None:
            refs = refs[1:]
        a_ref, b_ref = refs[0], refs[1]
        r_ref = refs[2] if res is not None else None
        o_ref = refs[n_in]
        if len(b_ref.shape) == 3 and dims == NN:
            part = _dot(a_ref[...], b_ref[...].reshape(-1, b_ref.shape[-1]), dims)
        elif len(b_ref.shape) == 3:
            width = b_ref.shape[-1]
            part = None
            for s in range(b_ref.shape[0]):
                term = _dot(a_ref[:, s * width:(s + 1) * width], b_ref[s], dims)
                part = term if part is None else part + term
        else:
            part = _dot(a_ref[...], b_ref[...], dims)
        if nk == 1:
            if r_ref is not None:
                part = part + r_ref[...]
            o_ref[...] = part.astype(o_ref.dtype)
            return
        acc_ref = refs[n_in + 1]
        k = pl.program_id(2)

        @pl.when(k == 0)
        def _():
            acc_ref[...] = part

        @pl.when(k > 0)
        def _():
            acc_ref[...] += part

        @pl.when(k == nk - 1)
        def _():
            tot = acc_ref[...]
            if r_ref is not None:
                tot = tot + r_ref[...]
            o_ref[...] = tot.astype(o_ref.dtype)

    in_specs = [a_spec, b_spec]
    args = [a, b]
    if res is not None:
        in_specs.append(res_spec)
        args.append(res)
    if after is not None:
        in_specs.append(ANY)
        args.append(after)
    acc_shape = tuple(d for d in o_spec.block_shape if d is not None)
    scratch = [pltpu.VMEM(acc_shape, F32)] if nk > 1 else []
    params = _params(("parallel", "parallel", "arbitrary"), vmem_mb)
    if place is not None:
        return pl.pallas_call(
            body, name=name, out_shape=out_shape, compiler_params=params,
            grid_spec=pltpu.PrefetchScalarGridSpec(num_scalar_prefetch=1, grid=grid, in_specs=in_specs,
                                                   out_specs=o_spec, scratch_shapes=scratch),
        )(place, *args)
    return pl.pallas_call(
        body, name=name, grid=grid, in_specs=in_specs, out_specs=o_spec, out_shape=out_shape,
        scratch_shapes=scratch, compiler_params=params,
    )(*args)


def _rms_fwd(name, x, g, tr, after=None, transposed=False):
    rows, d = x.shape

    def body(x_ref, g_ref, *refs):
        outs = refs[1:] if after is not None else refs
        xv = x_ref[...]
        r = lax.rsqrt(jnp.mean(xv * xv, axis=-1, keepdims=True) + EPS)
        h = xv * r * g_ref[...]
        outs[0][...] = h.astype(BF16)
        if transposed:
            outs[1][...] = h.T.astype(BF16)

    out_specs = [pl.BlockSpec((tr, d), lambda i: (i, 0))]
    out_shape = [jax.ShapeDtypeStruct((rows, d), BF16)]
    if transposed:
        out_specs.append(pl.BlockSpec((d, tr), lambda i: (0, i)))
        out_shape.append(jax.ShapeDtypeStruct((d, rows), BF16))
    outs = pl.pallas_call(
        body, name=name, grid=(rows // tr,),
        in_specs=[pl.BlockSpec((tr, d), lambda i: (i, 0)), pl.BlockSpec((1, d), lambda i: (0, 0))]
        + ([] if after is None else [ANY]),
        out_specs=out_specs, out_shape=out_shape,
        compiler_params=_params(("parallel",)),
    )(x, g, *([] if after is None else [after]))
    return outs if transposed else outs[0]


def _rms_bwd(name, x, dh, dres, g, tr, after=None):
    rows, d = x.shape

    def body(x_ref, dh_ref, dres_ref, g_ref, *refs):
        dx_ref, dxb_ref, dg_ref = refs[-3:]
        xv = x_ref[...]
        r = lax.rsqrt(jnp.mean(xv * xv, axis=-1, keepdims=True) + EPS)
        xhat = xv * r
        dhv = dh_ref[...]
        dxh = dhv * g_ref[...]
        dx = r * (dxh - xhat * jnp.mean(dxh * xhat, axis=-1, keepdims=True)) + dres_ref[...]
        dx_ref[...] = dx
        dxb_ref[...] = dx.astype(BF16)
        part = jnp.sum(dhv * xhat, axis=0, keepdims=True)

        @pl.when(pl.program_id(0) == 0)
        def _():
            dg_ref[...] = part

        @pl.when(pl.program_id(0) > 0)
        def _():
            dg_ref[...] += part

    blk = pl.BlockSpec((tr, d), lambda i: (i, 0))
    vec = pl.BlockSpec((1, d), lambda i: (0, 0))
    return pl.pallas_call(
        body, name=name, grid=(rows // tr,), in_specs=[blk, blk, blk, vec] + ([] if after is None else [ANY]),
        out_specs=[blk, blk, vec],
        out_shape=[jax.ShapeDtypeStruct((rows, d), F32), jax.ShapeDtypeStruct((rows, d), BF16),
                   jax.ShapeDtypeStruct((1, d), F32)],
        compiler_params=_params(("arbitrary",)),
    )(x, dh, dres, g, *([] if after is None else [after]))


def _rms_gain_grad(name, x, dh):
    rows, d = x.shape

    def body(x_ref, dh_ref, dg_ref):
        xv = x_ref[...]
        r = lax.rsqrt(jnp.mean(xv * xv, axis=-1, keepdims=True) + EPS)
        dg_ref[...] = jnp.sum(dh_ref[...] * xv * r, axis=0, keepdims=True)

    return pl.pallas_call(
        body, name=name, out_shape=jax.ShapeDtypeStruct((1, d), F32), compiler_params=_params(None),
    )(x, dh)


def _loss_and_grad(name, y, target, tr):
    rows, d = y.shape
    n = rows // tr

    def body(y_ref, t_ref, dx_ref, dxb_ref, loss_ref, acc_ref):
        e = y_ref[...] - t_ref[...]
        dx = e * (1.0 / d)
        dx_ref[...] = dx
        dxb_ref[...] = dx.astype(BF16)
        part = jnp.sum(e * e, axis=0, keepdims=True)
        i = pl.program_id(0)

        @pl.when(i == 0)
        def _():
            acc_ref[...] = part

        @pl.when(i > 0)
        def _():
            acc_ref[...] += part

        @pl.when(i == n - 1)
        def _():
            loss_ref[...] = jnp.sum(acc_ref[...], axis=-1, keepdims=True) * (0.5 / d)

    blk = pl.BlockSpec((tr, d), lambda i: (i, 0))
    return pl.pallas_call(
        body, name=name, grid=(n,), in_specs=[blk, blk],
        out_specs=[blk, blk, pl.BlockSpec((1, 1), lambda i: (0, 0))],
        out_shape=[jax.ShapeDtypeStruct((rows, d), F32), jax.ShapeDtypeStruct((rows, d), BF16),
                   jax.ShapeDtypeStruct((1, 1), F32)],
        scratch_shapes=[pltpu.VMEM((1, d), F32)],
        compiler_params=_params(("arbitrary",)),
    )(y, target)


SB_T = 256
SB_HEADS = 4


def _sb_scores(q, kblk):
    z = _dot(q, kblk, NT) * ATT_SCALE
    e = jnp.exp(-jnp.abs(z))
    sp = jnp.log1p(e)
    lb = jnp.minimum(z, 0.0) - sp
    l1 = lb - z
    return z, e, lb, l1


def _sb_fwd(name, proj, after=None):
    s_len = proj.shape[0]
    t = SB_T
    nq = s_len // t

    def body(q_ref, k_ref, v_ref, *refs):
        o_ref = refs[-1]
        i = pl.program_id(1)
        row = lax.broadcasted_iota(jnp.int32, (t, t), 0)
        col = lax.broadcasted_iota(jnp.int32, (t, t), 1)
        causal = col < row
        after_mat = (row > col).astype(BF16)
        heads = [slice(hh * HEAD_DIM, (hh + 1) * HEAD_DIM) for hh in range(SB_HEADS)]
        q = [q_ref[:, sl].astype(BF16) for sl in heads]

        def tile(kb, state, masked):
            start = pl.multiple_of(kb * t, t)
            out = []
            for hh, sl in enumerate(heads):
                carry, acc = state[hh]
                kblk = k_ref[pl.ds(start, t), sl].astype(BF16)
                vblk = v_ref[pl.ds(start, t), sl].astype(BF16)
                _, _, lb, l1 = _sb_scores(q[hh], kblk)
                if masked:
                    l1 = jnp.where(causal, l1, 0.0)
                hi, lo = _split_bf16(l1)
                after = _dot(hi, after_mat, NN) + _dot(lo, after_mat, NN) + carry
                a = jnp.exp(lb + after)
                if masked:
                    a = jnp.where(causal, a, 0.0)
                acc = acc + _dot(a.astype(BF16), vblk, NN)
                carry = carry + jnp.sum(l1, axis=-1, keepdims=True)
                out.append((carry, acc))
            return tuple(out)

        zero = (jnp.zeros((t, 1), F32), jnp.zeros((t, HEAD_DIM), F32))
        state = tile(i, (zero,) * SB_HEADS, True)
        state = lax.fori_loop(0, i, lambda n, st: tile(i - 1 - n, st, False), state)
        for hh, sl in enumerate(heads):
            o_ref[:, sl] = state[hh][1]

    cb = SB_HEADS * HEAD_DIM
    return pl.pallas_call(
        body, name=name, grid=(HEADS // SB_HEADS, nq),
        in_specs=[pl.BlockSpec((t, cb), lambda h, i: (i, OFF_QB // cb + h)),
                  pl.BlockSpec((s_len, cb), lambda h, i: (0, OFF_KB // cb + h)),
                  pl.BlockSpec((s_len, cb), lambda h, i: (0, OFF_VB // cb + h))] + ([] if after is None else [ANY]),
        out_specs=pl.BlockSpec((t, cb), lambda h, i: (i, h)),
        out_shape=jax.ShapeDtypeStruct((s_len, D_B), F32),
        compiler_params=_params(("parallel", "arbitrary")),
    )(proj, proj, proj, *([] if after is None else [after]))


def _sb_bwd(name, proj, dy, after=None):
    s_len = proj.shape[0]
    t = SB_T
    nq = s_len // t

    def body(q_ref, k_ref, v_ref, z_ref, dy_ref, *refs):
        dq_ref, dk_ref, dv_ref, a_ref, s_ref = refs[-5:]
        i = pl.program_id(1)

        @pl.when(i == 0)
        def _():
            dk_ref[...] = jnp.zeros_like(dk_ref)
            dv_ref[...] = jnp.zeros_like(dv_ref)

        heads = [slice(hh * HEAD_DIM, (hh + 1) * HEAD_DIM) for hh in range(SB_HEADS)]
        q = [q_ref[:, sl].astype(BF16) for sl in heads]
        silu_z, _ = _silu_and_grad(z_ref[...])
        do_all = dy_ref[...] * silu_z
        do_b = [do_all[:, sl].astype(BF16) for sl in heads]
        row = lax.broadcasted_iota(jnp.int32, (t, t), 0)
        col = lax.broadcasted_iota(jnp.int32, (t, t), 1)
        causal = col < row
        after_mat = (row > col).astype(BF16)
        before_mat = (row < col).astype(BF16)

        def weights(kb, carries, masked):
            start = pl.multiple_of(kb * t, t)
            out = []
            for hh, sl in enumerate(heads):
                kblk = k_ref[pl.ds(start, t), sl].astype(BF16)
                z, _, lb, l1 = _sb_scores(q[hh], kblk)
                if masked:
                    l1 = jnp.where(causal, l1, 0.0)
                hi, lo = _split_bf16(l1)
                after = _dot(hi, after_mat, NN) + _dot(lo, after_mat, NN) + carries[hh]
                a = jnp.exp(lb + after)
                if masked:
                    a = jnp.where(causal, a, 0.0)
                a_ref[hh, kb] = a
                s_ref[hh, kb] = z
                out.append(carries[hh] + jnp.sum(l1, axis=-1, keepdims=True))
            return tuple(out)

        carries = weights(i, (jnp.zeros((t, 1), F32),) * SB_HEADS, True)
        lax.fori_loop(0, i, lambda n, c: weights(i - 1 - n, c, False), carries)

        def grads(kb, state, masked):
            start = pl.multiple_of(kb * t, t)
            out = []
            for hh, sl in enumerate(heads):
                carry, dq = state[hh]
                kblk = k_ref[pl.ds(start, t), sl].astype(BF16)
                vblk = v_ref[pl.ds(start, t), sl].astype(BF16)
                a = a_ref[hh, kb]
                z = s_ref[hh, kb]
                g = _dot(do_b[hh], vblk, NT) * a
                ghi, glo = _split_bf16(g)
                prefix = _dot(ghi, before_mat, NN) + _dot(glo, before_mat, NN) + carry
                e = jnp.exp(-jnp.abs(z))
                inv = 1.0 / (1.0 + e)
                pos = z >= 0.0
                beta = jnp.where(pos, inv, e * inv)
                one_m_beta = jnp.where(pos, e * inv, inv)
                dz = (g * one_m_beta - prefix * beta) * ATT_SCALE
                if masked:
                    dz = jnp.where(causal, dz, 0.0)
                dz_b = dz.astype(BF16)
                dq = dq + _dot(dz_b, kblk, NN)
                dk_ref[pl.ds(start, t), sl] += _dot(dz_b, q[hh], TN)
                dv_ref[pl.ds(start, t), sl] += _dot(a.astype(BF16), do_b[hh], TN)
                out.append((carry + jnp.sum(g, axis=-1, keepdims=True), dq))
            return tuple(out)

        zero = (jnp.zeros((t, 1), F32), jnp.zeros((t, HEAD_DIM), F32))
        state = lax.fori_loop(0, i, lambda kb, st: grads(kb, st, False), (zero,) * SB_HEADS)
        state = grads(i, state, True)
        for hh, sl in enumerate(heads):
            dq_ref[:, sl] = state[hh][1]

    cb = SB_HEADS * HEAD_DIM
    qblk = lambda off: pl.BlockSpec((t, cb), lambda h, i: (i, off // cb + h))
    full = lambda off: pl.BlockSpec((s_len, cb), lambda h, i: (0, off // cb + h))
    out = jax.ShapeDtypeStruct((s_len, D_B), F32)
    return pl.pallas_call(
        body, name=name, grid=(HEADS // SB_HEADS, nq),
        in_specs=[qblk(OFF_QB), full(OFF_KB), full(OFF_VB), qblk(OFF_ZB), qblk(OFF_YB)]
        + ([] if after is None else [ANY]),
        out_specs=[qblk(0), full(0), full(0)],
        out_shape=[out, out, out],
        scratch_shapes=[pltpu.VMEM((SB_HEADS, nq, t, t), F32), pltpu.VMEM((SB_HEADS, nq, t, t), F32)],
        compiler_params=_params(("parallel", "arbitrary")),
    )(proj, proj, proj, proj, dy, *([] if after is None else [after]))


MEM_TQ = 512


def _qk_norm(x, g):
    r = lax.rsqrt(jnp.mean(x * x, axis=-1, keepdims=True) + EPS)
    xhat = x * r
    return xhat * g, xhat, r


def _qk_norm_bwd(dn, g, xhat, r):
    dxh = dn * g
    return r * (dxh - xhat * jnp.mean(dxh * xhat, axis=-1, keepdims=True))


def _mem_probs(q, mk, qg, kg):
    qn, qhat, rq = _qk_norm(q, qg)
    kn, khat, rk = _qk_norm(mk, kg)
    qn_b, kn_b = qn.astype(BF16), kn.astype(BF16)
    s = _dot(qn_b, kn_b, NT) * ATT_SCALE
    p = jnp.exp(s - jnp.max(s, axis=-1, keepdims=True))
    p = p / jnp.sum(p, axis=-1, keepdims=True)
    return p, qn_b, kn_b, qhat, rq, khat, rk


def _mem_fwd(name, proj, mem_kv, qg, kg):
    s_len = proj.shape[0]
    m_len = mem_kv.shape[0]
    tq = min(MEM_TQ, s_len)

    def body(q_ref, mk_ref, mv_ref, qg_ref, kg_ref, o_ref):
        p = _mem_probs(q_ref[...], mk_ref[...], qg_ref[...], kg_ref[...])[0]
        o_ref[...] = _dot(p.astype(BF16), mv_ref[...].astype(BF16), NN)

    cb = HEAD_DIM
    vec = pl.BlockSpec((1, cb), lambda h, i: (0, 0))
    return pl.pallas_call(
        body, name=name, grid=(HEADS, s_len // tq),
        in_specs=[pl.BlockSpec((tq, cb), lambda h, i: (i, OFF_QC // cb + h)),
                  pl.BlockSpec((m_len, cb), lambda h, i: (0, h)),
                  pl.BlockSpec((m_len, cb), lambda h, i: (0, HEADS + h)), vec, vec],
        out_specs=pl.BlockSpec((tq, cb), lambda h, i: (i, h)),
        out_shape=jax.ShapeDtypeStruct((s_len, D_C), F32),
        compiler_params=_params(("parallel", "parallel")),
    )(proj, mem_kv, mem_kv, qg, kg)


def _mem_bwd(name, proj, mem_kv, qg, kg, dy):
    s_len = proj.shape[0]
    m_len = mem_kv.shape[0]
    tq = min(MEM_TQ, s_len)

    def body(q_ref, mk_ref, mv_ref, qg_ref, kg_ref, z_ref, dy_ref, dq_ref, dmk_ref, dmv_ref, dqg_ref, dkg_ref):
        h, i = pl.program_id(0), pl.program_id(1)

        @pl.when(i == 0)
        def _():
            dmk_ref[...] = jnp.zeros_like(dmk_ref)
            dmv_ref[...] = jnp.zeros_like(dmv_ref)

        @pl.when((i == 0) & (h == 0))
        def _():
            dqg_ref[...] = jnp.zeros_like(dqg_ref)
            dkg_ref[...] = jnp.zeros_like(dkg_ref)

        qg, kg = qg_ref[...], kg_ref[...]
        p, qn_b, kn_b, qhat, rq, khat, rk = _mem_probs(q_ref[...], mk_ref[...], qg, kg)
        silu_z, _ = _silu_and_grad(z_ref[...])
        do_b = (dy_ref[...] * silu_z).astype(BF16)
        dmv_ref[...] += _dot(p.astype(BF16), do_b, TN)
        dp = _dot(do_b, mv_ref[...].astype(BF16), NT)
        ds = (p * (dp - jnp.sum(dp * p, axis=-1, keepdims=True)) * ATT_SCALE).astype(BF16)
        dqn = _dot(ds, kn_b, NN)
        dkn = _dot(ds, qn_b, TN)
        dq_ref[...] = _qk_norm_bwd(dqn, qg, qhat, rq)
        dmk_ref[...] += _qk_norm_bwd(dkn, kg, khat, rk)
        dqg_ref[...] += jnp.sum(dqn * qhat, axis=0, keepdims=True)
        dkg_ref[...] += jnp.sum(dkn * khat, axis=0, keepdims=True)

    cb = HEAD_DIM
    vec = pl.BlockSpec((1, cb), lambda h, i: (0, 0))
    qblk = lambda off: pl.BlockSpec((tq, cb), lambda h, i: (i, off // cb + h))
    memblk = lambda off: pl.BlockSpec((m_len, cb), lambda h, i: (0, off + h))
    return pl.pallas_call(
        body, name=name, grid=(HEADS, s_len // tq),
        in_specs=[qblk(OFF_QC), memblk(0), memblk(HEADS), vec, vec, qblk(OFF_ZC), qblk(OFF_YC)],
        out_specs=[qblk(0), memblk(0), memblk(0), vec, vec],
        out_shape=[jax.ShapeDtypeStruct((s_len, D_C), F32), jax.ShapeDtypeStruct((m_len, D_C), F32),
                   jax.ShapeDtypeStruct((m_len, D_C), F32), jax.ShapeDtypeStruct((1, cb), F32),
                   jax.ShapeDtypeStruct((1, cb), F32)],
        compiler_params=_params(("arbitrary", "arbitrary")),
    )(proj, mem_kv, mem_kv, qg, kg, proj, dy)


def _sgu_common(u_ref, v_ref, lng_ref, lnb_ref, w_ref, bias_ref):
    ug = _gelu(u_ref[...])
    vg = _gelu(v_ref[...])
    mu = jnp.mean(vg, axis=-1, keepdims=True)
    xc = vg - mu
    rstd = lax.rsqrt(jnp.mean(xc * xc, axis=-1, keepdims=True) + EPS)
    xhat = xc * rstd
    vn = xhat * lng_ref[...] + lnb_ref[...]
    vn_b = vn.astype(BF16)
    row = lax.broadcasted_iota(jnp.int32, (CHUNK, CHUNK), 0)
    col = lax.broadcasted_iota(jnp.int32, (CHUNK, CHUNK), 1)
    tril = row >= col
    mixed = []
    for g in range(A_GROUPS):
        w = jnp.where(tril, w_ref[g], 0.0).astype(BF16)
        sl = slice(g * CHUNK, (g + 1) * CHUNK)
        mixed.append(_dot(w, vn_b[:, sl], NN) + bias_ref[:, sl])
    return ug, xhat, rstd, vn_b, mixed, tril


def _gate_fwd(name, proj, o_b, o_c, lng, lnb, w_s, bias):
    s_len = proj.shape[0]

    def body(u_ref, v_ref, za_ref, zb_ref, zc_ref, ob_ref, oc_ref, lng_ref, lnb_ref, w_ref, bias_ref, y_ref, yt_ref):
        ug, _, _, _, mixed, _ = _sgu_common(u_ref, v_ref, lng_ref, lnb_ref, w_ref, bias_ref)
        sza, _ = _silu_and_grad(za_ref[...])
        gate = ug * sza

        def put(off, width, val):
            y_ref[:, off:off + width] = val.astype(BF16)
            yt_ref[off:off + width, :] = val.T.astype(BF16)

        for g in range(A_GROUPS):
            sl = slice(g * CHUNK, (g + 1) * CHUNK)
            put(g * CHUNK, CHUNK, gate[:, sl] * mixed[g])
        szb, _ = _silu_and_grad(zb_ref[...])
        put(OFF_YB, D_B, ob_ref[...] * szb)
        szc, _ = _silu_and_grad(zc_ref[...])
        put(OFF_YC, D_C, oc_ref[...] * szc)

    wide = lambda off: pl.BlockSpec((CHUNK, D_A), lambda i: (i, off // D_A))
    narrow = lambda off: pl.BlockSpec((CHUNK, D_B), lambda i: (i, off // D_B))
    vec = pl.BlockSpec((1, D_A), lambda i: (0, 0))
    return pl.pallas_call(
        body, name=name, grid=(s_len // CHUNK,),
        in_specs=[wide(OFF_U), wide(OFF_V), wide(OFF_ZA), narrow(OFF_ZB), narrow(OFF_ZC), narrow(0), narrow(0), vec, vec,
                  pl.BlockSpec((A_GROUPS, CHUNK, CHUNK), lambda i: (0, 0, 0)),
                  pl.BlockSpec((CHUNK, D_A), lambda i: (0, 0))],
        out_specs=[pl.BlockSpec((CHUNK, D_MODEL), lambda i: (i, 0)), pl.BlockSpec((D_MODEL, CHUNK), lambda i: (0, i))],
        out_shape=[jax.ShapeDtypeStruct((s_len, D_MODEL), BF16), jax.ShapeDtypeStruct((D_MODEL, s_len), BF16)],
        compiler_params=_params(("parallel",)),
    )(proj, proj, proj, proj, proj, o_b, o_c, lng, lnb, w_s, bias)


def _gate_bwd(name, proj, dy, o_b, o_c, dqkv, dq_c, lng, lnb, w_s, w_s_t, bias):
    s_len = proj.shape[0]
    n = s_len // CHUNK
    dq_b, dk_b, dv_b = dqkv

    def body(u_ref, v_ref, za_ref, zb_ref, zc_ref, dya_ref, dyb_ref, dyc_ref, ob_ref, oc_ref, dq_ref, dk_ref, dv_ref,
             dqc_ref, lng_ref, lnb_ref, w_ref, wt_ref, bias_ref, dp_ref, dw_ref, dsb_ref, dlng_ref, dlnb_ref, dbias_ref):
        i = pl.program_id(0)

        @pl.when(i == 0)
        def _():
            dw_ref[...] = jnp.zeros_like(dw_ref)
            dbias_ref[...] = jnp.zeros_like(dbias_ref)
            dlng_ref[...] = jnp.zeros_like(dlng_ref)
            dlnb_ref[...] = jnp.zeros_like(dlnb_ref)

        ug, xhat, rstd, vn_b, mixed, tril = _sgu_common(u_ref, v_ref, lng_ref, lnb_ref, w_ref, bias_ref)
        za = za_ref[...]
        sza, dsza = _silu_and_grad(za)
        dya = dya_ref[...]
        mixed_all = jnp.concatenate(mixed, axis=-1)
        d_mixed = dya * ug * sza
        dp_ref[:, OFF_U:OFF_U + D_A] = (dya * mixed_all * sza * _gelu_grad(u_ref[...])).astype(BF16)
        dp_ref[:, OFF_ZA:OFF_ZA + D_A] = (dya * ug * mixed_all * dsza).astype(BF16)
        dbias_ref[...] += d_mixed
        dm_b = d_mixed.astype(BF16)
        triu = lax.broadcasted_iota(jnp.int32, (CHUNK, CHUNK), 0) <= lax.broadcasted_iota(jnp.int32, (CHUNK, CHUNK), 1)
        d_vn = []
        for g in range(A_GROUPS):
            sl = slice(g * CHUNK, (g + 1) * CHUNK)
            wt = jnp.where(triu, wt_ref[g], 0.0).astype(BF16)
            d_vn.append(_dot(wt, dm_b[:, sl], NN))
            dw_ref[g] += jnp.where(tril, _dot(dm_b[:, sl], vn_b[:, sl], NT), 0.0)
        d_vn = jnp.concatenate(d_vn, axis=-1)
        dlng_ref[...] += jnp.sum(d_vn * xhat, axis=0, keepdims=True)
        dlnb_ref[...] += jnp.sum(d_vn, axis=0, keepdims=True)
        dxh = d_vn * lng_ref[...]
        d_vg = rstd * (dxh - jnp.mean(dxh, axis=-1, keepdims=True)
                       - xhat * jnp.mean(dxh * xhat, axis=-1, keepdims=True))
        dp_ref[:, OFF_V:OFF_V + D_A] = (d_vg * _gelu_grad(v_ref[...])).astype(BF16)
        dp_ref[:, OFF_QB:OFF_QB + D_B] = dq_ref[...].astype(BF16)
        dp_ref[:, OFF_KB:OFF_KB + D_B] = dk_ref[...].astype(BF16)
        dp_ref[:, OFF_VB:OFF_VB + D_B] = dv_ref[...].astype(BF16)
        _, dszb = _silu_and_grad(zb_ref[...])
        dp_ref[:, OFF_ZB:OFF_ZB + D_B] = (dyb_ref[...] * ob_ref[...] * dszb).astype(BF16)
        dp_ref[:, OFF_QC:OFF_QC + D_C] = dqc_ref[...].astype(BF16)
        _, dszc = _silu_and_grad(zc_ref[...])
        dp_ref[:, OFF_ZC:OFF_ZC + D_C] = (dyc_ref[...] * oc_ref[...] * dszc).astype(BF16)

        @pl.when(i == n - 1)
        def _():
            ch = lax.broadcasted_iota(jnp.int32, (D_A, CHUNK), 0)
            gcol = lax.broadcasted_iota(jnp.int32, (D_A, CHUNK), 1)
            pick = (ch // (D_A // A_GROUPS) == gcol).astype(BF16)
            rest = dbias_ref[...]
            tot = jnp.zeros((CHUNK, CHUNK), F32)
            for _ in range(3):
                term = rest.astype(BF16)
                tot = tot + _dot(term, pick, NN)
                rest = rest - term.astype(F32)
            dsb_ref[...] = tot

    wide = lambda off: pl.BlockSpec((CHUNK, D_A), lambda i: (i, off // D_A))
    narrow = lambda off: pl.BlockSpec((CHUNK, D_B), lambda i: (i, off // D_B))
    vec = pl.BlockSpec((1, D_A), lambda i: (0, 0))
    wspec = pl.BlockSpec((A_GROUPS, CHUNK, CHUNK), lambda i: (0, 0, 0))
    bspec = pl.BlockSpec((CHUNK, D_A), lambda i: (0, 0))
    return pl.pallas_call(
        body, name=name, grid=(n,),
        in_specs=[wide(OFF_U), wide(OFF_V), wide(OFF_ZA), narrow(OFF_ZB), narrow(OFF_ZC),
                  wide(0), narrow(OFF_YB), narrow(OFF_YC), narrow(0), narrow(0), narrow(0), narrow(0), narrow(0),
                  narrow(0), vec, vec, wspec, wspec, bspec],
        out_specs=[pl.BlockSpec((CHUNK, IN_WIDTH), lambda i: (i, 0)), wspec,
                   pl.BlockSpec((CHUNK, CHUNK), lambda i: (0, 0)), vec, vec],
        out_shape=[jax.ShapeDtypeStruct((s_len, IN_WIDTH), BF16), jax.ShapeDtypeStruct((A_GROUPS, CHUNK, CHUNK), F32),
                   jax.ShapeDtypeStruct((CHUNK, CHUNK), F32), jax.ShapeDtypeStruct((1, D_A), F32),
                   jax.ShapeDtypeStruct((1, D_A), F32)],
        scratch_shapes=[pltpu.VMEM((CHUNK, D_A), F32)],
        compiler_params=_params(("arbitrary",)),
    )(proj, proj, proj, proj, proj, dy, dy, dy, o_b, o_c, dq_b, dk_b, dv_b, dq_c, lng, lnb, w_s, w_s_t, bias)


IN_SHARD = IN_WIDTH // N_CHIPS
ROW_SHARD = D_MODEL // N_CHIPS


def _bias_rows(sgu_b_l):
    return jnp.repeat(sgu_b_l.T, D_A // A_GROUPS, axis=1)


class _WholeWeights:
    def __init__(self, w_in_all, w_kv_all, w_out_all):
        self.weights = (w_in_all, w_kv_all, w_out_all)

    def w_in(self, h):
        return self.weights[0]

    def rest_start(self, proj):
        return None

    def rest_finish(self, o_b):
        return self.weights[1], self.weights[2], None

    def before_out(self, y):
        return None


def _layer_fwd(l, x, mem, sm, hooks):
    s_len = x.shape[0]
    m_len = mem.shape[0]
    tm = min(1024, s_len)
    tn = 768
    per = IN_SHARD // tn
    h, h_t = _rms_fwd(f"rms_fwd_{l}", x, sm["norm_g"][l][None], min(256, s_len), transposed=True)
    w_in_all = hooks.w_in(h)
    proj = _matmul(
        f"in_proj_{l}", h, w_in_all, grid=(s_len // tm, IN_WIDTH // tn, 1),
        a_spec=pl.BlockSpec((tm, D_MODEL), lambda i, j, k: (i, 0)),
        b_spec=pl.BlockSpec((None, D_MODEL, tn), lambda i, j, k: (j // per, 0, j % per)),
        o_spec=pl.BlockSpec((tm, tn), lambda i, j, k: (i, j)),
        out_shape=jax.ShapeDtypeStruct((s_len, IN_WIDTH), F32), dims=NN)
    o_b = _sb_fwd(f"sb_fwd_{l}", proj, hooks.rest_start(proj))
    w_kv_all, w_out_all, after = hooks.rest_finish(o_b)
    mem_h = _rms_fwd(f"mem_rms_fwd_{l}", mem, sm["mem_norm_g"][l][None], m_len, after)
    mem_kv = _matmul(
        f"mem_kv_{l}", mem_h, w_kv_all, grid=(1, 2, N_CHIPS),
        a_spec=pl.BlockSpec((m_len, ROW_SHARD), lambda i, j, k: (0, k)),
        b_spec=pl.BlockSpec((None, ROW_SHARD, D_C), lambda i, j, k: (k, 0, j)),
        o_spec=pl.BlockSpec((m_len, D_C), lambda i, j, k: (0, j)),
        out_shape=jax.ShapeDtypeStruct((m_len, 2 * D_C), F32), dims=NN)
    qg, kg = sm["q_norm_g"][l][None], sm["k_norm_g"][l][None]
    o_c = _mem_fwd(f"mem_fwd_{l}", proj, mem_kv, qg, kg)
    bias = _bias_rows(sm["sgu_b"][l])
    y, y_t = _gate_fwd(f"gate_fwd_{l}", proj, o_b, o_c, sm["sgu_ln_g"][l][None], sm["sgu_ln_b"][l][None],
                       sm["sgu_w"][l], bias)
    tn_o = 512
    x_next = _matmul(
        f"out_proj_{l}", y, w_out_all, grid=(s_len // tm, D_MODEL // tn_o, 1),
        a_spec=pl.BlockSpec((tm, D_MODEL), lambda i, j, k: (i, 0)),
        b_spec=pl.BlockSpec((N_CHIPS, ROW_SHARD, tn_o), lambda i, j, k: (0, 0, j)),
        o_spec=pl.BlockSpec((tm, tn_o), lambda i, j, k: (i, j)),
        out_shape=jax.ShapeDtypeStruct((s_len, D_MODEL), F32), dims=NN,
        res=x, res_spec=pl.BlockSpec((tm, tn_o), lambda i, j, k: (i, j)), after=hooks.before_out(y))
    saved = dict(x=x, h_t=h_t, proj=proj, mem_h=mem_h, mem_kv=mem_kv, o_b=o_b, o_c=o_c, y_t=y_t, bias=bias,
                 weights=(w_in_all, w_kv_all, w_out_all))
    return x_next, saved


class _NoExchange:
    def __init__(self):
        self.gave, self.kept = {}, {}

    def start(self, l, group, gives):
        self.gave[l, group] = gives
        return None

    def landed(self, l, group, after):
        return [jnp.zeros_like(g) for g in self.gave[l, group]]

    def send(self, l, group, parts):
        self.kept[l, group] = parts
        return None


def _layer_bwd(l, dxo, dxo_b, mem, sm, saved, place, exchange):
    s_len = dxo.shape[0]
    m_len = mem.shape[0]
    proj, y_t, h_t, mem_h, mem_kv = saved["proj"], saved["y_t"], saved["h_t"], saved["mem_h"], saved["mem_kv"]
    w_in_all, w_kv_all, w_out_all = saved["weights"]
    tm = min(1024, s_len)
    tn = 768
    per = IN_SHARD // tn
    half_rows = ROW_SHARD // 2

    def halves(make):
        give = lambda: make("give", lambda p: 1 - p[1], None, F32)
        keep = lambda theirs: make("keep", lambda p: p[1], theirs, BF16)
        return give, keep

    def grad_out(tag, half, theirs, dtype):
        o_spec = pl.BlockSpec((None, half_rows, 1024), lambda i, j, k, p: (i, 0, j))
        return _matmul(
            f"d_w_out_{l}_{tag}", y_t, dxo_b, grid=(N_CHIPS, D_MODEL // 1024, 1), place=place,
            a_spec=pl.BlockSpec((half_rows, s_len), lambda i, j, k, p: (2 * i + half(p), 0)),
            b_spec=pl.BlockSpec((s_len, 1024), lambda i, j, k, p: (0, j)), o_spec=o_spec,
            out_shape=jax.ShapeDtypeStruct((N_CHIPS, half_rows, D_MODEL), dtype), dims=NN,
            res=theirs, res_spec=o_spec)

    def grad_in(tag, half, theirs, dtype):
        o_spec = pl.BlockSpec((None, D_MODEL // 2, tn), lambda i, j, k, p: (j // per, 0, j % per))
        return _matmul(
            f"d_w_in_{l}_{tag}", h_t, dproj, grid=(1, IN_WIDTH // tn, 1), place=place,
            a_spec=pl.BlockSpec((D_MODEL // 2, s_len), lambda i, j, k, p: (half(p), 0)),
            b_spec=pl.BlockSpec((s_len, tn), lambda i, j, k, p: (0, j)), o_spec=o_spec,
            out_shape=jax.ShapeDtypeStruct((N_CHIPS, D_MODEL // 2, IN_SHARD), dtype), dims=NN,
            res=theirs, res_spec=o_spec)

    def grad_kv(tag, half, theirs, dtype):
        o_spec = pl.BlockSpec((None, half_rows, 2 * D_C), lambda i, j, k, p: (i, 0, 0))
        return _matmul(
            f"d_w_kv_{l}_{tag}", mem_h, dkv_b, grid=(N_CHIPS, 1, 1), place=place,
            a_spec=pl.BlockSpec((m_len, half_rows), lambda i, j, k, p: (0, 2 * i + half(p))),
            b_spec=pl.BlockSpec((m_len, 2 * D_C), lambda i, j, k, p: (0, 0)), o_spec=o_spec,
            out_shape=jax.ShapeDtypeStruct((N_CHIPS, half_rows, 2 * D_C), dtype), dims=TN,
            res=theirs, res_spec=o_spec)

    give_out, keep_out = halves(grad_out)
    token = exchange.start(l, "out", [give_out()])
    dy = _matmul(
        f"d_y_{l}", dxo_b, w_out_all, grid=(s_len // tm, N_CHIPS, 1),
        a_spec=pl.BlockSpec((tm, D_MODEL), lambda i, j, k: (i, 0)),
        b_spec=pl.BlockSpec((None, ROW_SHARD, D_MODEL), lambda i, j, k: (j, 0, 0)),
        o_spec=pl.BlockSpec((tm, ROW_SHARD), lambda i, j, k: (i, j)),
        out_shape=jax.ShapeDtypeStruct((s_len, D_MODEL), F32), dims=NT, after=token)
    (theirs_out,) = exchange.landed(l, "out", dy)
    token = exchange.send(l, "out", [keep_out(theirs_out)])
    qg, kg = sm["q_norm_g"][l][None], sm["k_norm_g"][l][None]
    dqkv = _sb_bwd(f"sb_bwd_{l}", proj, dy, token)
    dq_c, dmk, dmv, dqg, dkg = _mem_bwd(f"mem_bwd_{l}", proj, mem_kv, qg, kg, dy)
    w_s = sm["sgu_w"][l]
    dproj, dws, dbias, dlng, dlnb = _gate_bwd(
        f"gate_bwd_{l}", proj, dy, saved["o_b"], saved["o_c"], dqkv, dq_c, sm["sgu_ln_g"][l][None],
        sm["sgu_ln_b"][l][None], w_s, jnp.swapaxes(w_s, 1, 2), saved["bias"])
    dkv_b = jnp.concatenate([dmk, dmv], axis=1).astype(BF16)
    give_in, keep_in = halves(grad_in)
    give_kv, keep_kv = halves(grad_kv)
    token = exchange.start(l, "in", [give_in(), give_kv()])
    dh = _matmul(
        f"d_h_{l}", dproj, w_in_all, grid=(s_len // tm, D_MODEL // 512, 1),
        a_spec=pl.BlockSpec((tm, IN_WIDTH), lambda i, j, k: (i, 0)),
        b_spec=pl.BlockSpec((N_CHIPS, 512, IN_SHARD), lambda i, j, k: (0, j, 0)),
        o_spec=pl.BlockSpec((tm, 512), lambda i, j, k: (i, j)),
        out_shape=jax.ShapeDtypeStruct((s_len, D_MODEL), F32), dims=NT, after=token, vmem_mb=56)
    theirs_in, theirs_kv = exchange.landed(l, "in", dh)
    token = exchange.send(l, "in", [keep_in(theirs_in), keep_kv(theirs_kv)])
    dx, dx_b, dng = _rms_bwd(f"rms_bwd_{l}", saved["x"], dh, dxo, sm["norm_g"][l][None], min(256, s_len), token)
    d_mem_h = _matmul(
        f"d_mem_h_{l}", dkv_b, w_kv_all, grid=(1, N_CHIPS, 1),
        a_spec=pl.BlockSpec((m_len, 2 * D_C), lambda i, j, k: (0, 0)),
        b_spec=pl.BlockSpec((None, ROW_SHARD, 2 * D_C), lambda i, j, k: (j, 0, 0)),
        o_spec=pl.BlockSpec((m_len, ROW_SHARD), lambda i, j, k: (0, j)),
        out_shape=jax.ShapeDtypeStruct((m_len, D_MODEL), F32), dims=NT)
    dmng = _rms_gain_grad(f"mem_rms_bwd_{l}", mem, d_mem_h)
    dsgu_b = dbias[:, :A_GROUPS].T
    small = dict(norm_g=dng[0], sgu_ln_g=dlng[0], sgu_ln_b=dlnb[0], sgu_w=dws, sgu_b=dsgu_b, mem_norm_g=dmng[0],
                 q_norm_g=dqg[0], k_norm_g=dkg[0])
    return dx, dx_b, small


SMALL_NAMES = ("norm_g", "sgu_ln_g", "sgu_ln_b", "sgu_w", "sgu_b", "mem_norm_g", "q_norm_g", "k_norm_g")


def _local_step(x, mem, target, sm, w_all):
    saved = []
    cur = x
    for l in range(DEPTH):
        cur, sv = _layer_fwd(l, cur, mem, sm, _WholeWeights(*w_all[l]))
        saved.append(sv)
    dxo, dxo_b, loss = _loss_and_grad("loss", cur, target, min(256, x.shape[0]))
    small = [None] * DEPTH
    exchange = _NoExchange()
    place = jnp.zeros((2,), jnp.int32)
    for l in reversed(range(DEPTH)):
        dxo, dxo_b, small[l] = _layer_bwd(l, dxo, dxo_b, mem, sm, saved[l], place, exchange)
    small = {k: jnp.stack([small[l][k] for l in range(DEPTH)]) for k in SMALL_NAMES}
    return loss, dxo, small, exchange.gave, exchange.kept


def _place():
    x, y, c = lax.axis_index("x"), lax.axis_index("y"), lax.axis_index("c")
    return x, y, c


def _other_chips(x, y):
    return [(1 - x, y, 2 * (1 - x) + y), (x, 1 - y, 2 * x + 1 - y), (1 - x, 1 - y, 2 * (1 - x) + 1 - y)]


D2D_CHUNKS = 8


def _place_index():
    return jnp.stack([2 * lax.axis_index("x") + lax.axis_index("y"), lax.axis_index("c")]).astype(jnp.int32)


def _cast_into_slot(name, w, l, place):
    _, rows, cols = w.shape
    tr = min(256, rows)

    def body(p_ref, w_ref, o_ref):
        o_ref[...] = w_ref[...].astype(BF16)

    return pl.pallas_call(
        body, name=name,
        grid_spec=pltpu.PrefetchScalarGridSpec(
            num_scalar_prefetch=1, grid=(rows // tr,),
            in_specs=[pl.BlockSpec((None, tr, cols), lambda i, p: (l, i, 0))],
            out_specs=pl.BlockSpec((None, tr, cols), lambda i, p: (p[0], i, 0))),
        out_shape=jax.ShapeDtypeStruct((N_CHIPS, rows, cols), BF16),
        compiler_params=_params(("parallel",)),
    )(place, w)


HBM = pl.BlockSpec(memory_space=pltpu.HBM)
SEM = pl.BlockSpec(memory_space=pltpu.SEMAPHORE)
DATAFLOW = pltpu.SideEffectType.DATAFLOW_SIDE_EFFECTING


def _in_hbm(a):
    return pltpu.with_memory_space_constraint(a, pltpu.HBM)


def _chip_copies_start(name, srcs, lands, make_copy, after=None):
    n_t = len(srcs)
    in_place = lands is None
    n_after = 0 if after is None else 1

    def body(*refs):
        src = refs[:n_t]
        k = (n_t if in_place else 2 * n_t) + n_after
        send_sems, recv_sems = refs[k], refs[k + 1]
        land = refs[k + 2:k + 2 + n_t] if in_place else refs[k + 2 + n_t:k + 2 + 2 * n_t]
        token = refs[-1]
        x, y, c = _place()
        me = 2 * x + y
        for t in range(n_t):
            for px, py, pk in _other_chips(x, y):
                s, d = make_copy(src[t], land[t], me, pk, c)
                pltpu.make_async_remote_copy(
                    src_ref=s, dst_ref=d, send_sem=send_sems.at[t], recv_sem=recv_sems.at[t],
                    device_id=(px, py, c), device_id_type=MESH).start()
        token[...] = jnp.zeros_like(token)

    bufs = list(srcs) if in_place else list(srcs) + list(lands)
    outs = pl.pallas_call(
        body, name=name,
        in_specs=[HBM] * len(bufs) + [ANY] * n_after,
        out_specs=[SEM, SEM] + [HBM] * len(bufs) + [pl.BlockSpec(memory_space=pltpu.VMEM)],
        out_shape=[pltpu.SemaphoreType.DMA((n_t,)), pltpu.SemaphoreType.DMA((n_t,))]
        + [pltpu.HBM(b.shape, b.dtype) for b in bufs] + [jax.ShapeDtypeStruct((8, 128), F32)],
        input_output_aliases={i: 2 + i for i in range(len(bufs))},
        compiler_params=pltpu.CompilerParams(has_side_effects=DATAFLOW),
    )(*[_in_hbm(b) for b in bufs], *([] if after is None else [after]))
    return outs[0], outs[1], list(outs[2:2 + len(bufs)]), outs[-1]


def _chip_copies_wait(name, send_sems, recv_sems, bufs, sent, landed, after):
    n_b = len(bufs)

    def body(*refs):
        buf = refs[:n_b]
        send_ref, recv_ref = refs[n_b], refs[n_b + 1]
        x, y, c = _place()
        for t, (s, d) in enumerate(zip(sent(buf), landed(buf))):
            out = pltpu.make_async_remote_copy(src_ref=s, dst_ref=s, send_sem=send_ref.at[t], recv_sem=recv_ref.at[t],
                                               device_id=(x, y, c), device_id_type=MESH)
            out.wait_send()
            arrived = pltpu.make_async_remote_copy(src_ref=d, dst_ref=d, send_sem=send_ref.at[t],
                                                   recv_sem=recv_ref.at[t], device_id=(x, y, c), device_id_type=MESH)
            arrived.wait_recv()

    after = list(after) if isinstance(after, (list, tuple)) else [after]
    return pl.pallas_call(
        body, name=name,
        in_specs=[HBM] * n_b + [SEM, SEM] + [ANY] * len(after), out_specs=[HBM] * n_b,
        out_shape=[pltpu.HBM(b.shape, b.dtype) for b in bufs],
        input_output_aliases={i: i for i in range(n_b)},
        compiler_params=pltpu.CompilerParams(has_side_effects=DATAFLOW),
    )(*bufs, send_sems, recv_sems, *after)


def _gather_start(name, bufs, after=None):
    def make_copy(src, land, me, pk, c):
        hr = src.shape[1] // 2
        return src.at[me, pl.ds(c * hr, hr)], land.at[me, pl.ds(c * hr, hr)]

    return _chip_copies_start(name, bufs, None, make_copy, after)


def _gather_wait(name, send_sems, recv_sems, bufs, after):
    def three_halves(buf):
        return [b.at[pl.ds(0, 3), pl.ds(0, b.shape[1] // 2)] for b in buf]

    return _chip_copies_wait(name, send_sems, recv_sems, bufs, three_halves, three_halves, after)


def _gather_forward_start(name, bufs):
    n_t = len(bufs)

    def body(*refs):
        mine = refs[:n_t]
        send_sems, recv_sems = refs[n_t], refs[n_t + 1]
        buf = refs[n_t + 2:2 * n_t + 2]
        token = refs[-1]
        x, y, c = _place()
        for q in range(D2D_CHUNKS):
            for t in range(n_t):
                hr = mine[t].shape[1] // 2
                cr = hr // D2D_CHUNKS
                rows = pl.ds(c * hr + q * cr, cr)
                for _, _, pk in _other_chips(x, y):
                    pltpu.make_async_remote_copy(
                        src_ref=mine[t].at[pk, rows], dst_ref=buf[t].at[pk, rows], send_sem=send_sems.at[t],
                        recv_sem=recv_sems.at[t], device_id=(x, y, 1 - c), device_id_type=MESH).start()
        token[...] = jnp.zeros_like(token)

    outs = pl.pallas_call(
        body, name=name,
        in_specs=[HBM] * n_t,
        out_specs=[SEM, SEM] + [HBM] * n_t + [pl.BlockSpec(memory_space=pltpu.VMEM)],
        out_shape=[pltpu.SemaphoreType.DMA((n_t,)), pltpu.SemaphoreType.DMA((n_t,))]
        + [pltpu.HBM(b.shape, b.dtype) for b in bufs] + [jax.ShapeDtypeStruct((8, 128), F32)],
        input_output_aliases={i: 2 + i for i in range(n_t)},
        compiler_params=pltpu.CompilerParams(has_side_effects=DATAFLOW),
    )(*[_in_hbm(b) for b in bufs])
    return outs[0], outs[1], list(outs[2:2 + n_t]), outs[-1]


def _core_exchange_start(name, grads):
    n_t = len(grads)
    lands = [lax.empty(g.shape, g.dtype) for g in grads]

    def body(*refs):
        src = refs[:n_t]
        send_sems, recv_sems = refs[2 * n_t], refs[2 * n_t + 1]
        land = refs[2 * n_t + 2 + n_t:2 * n_t + 2 + 2 * n_t]
        token = refs[-1]
        x, y, c = _place()
        for q in range(D2D_CHUNKS):
            for t in range(n_t):
                cr = src[t].shape[1] // D2D_CHUNKS
                rows = pl.ds(q * cr, cr)
                pltpu.make_async_remote_copy(
                    src_ref=src[t].at[:, rows], dst_ref=land[t].at[:, rows], send_sem=send_sems.at[t],
                    recv_sem=recv_sems.at[t], device_id=(x, y, 1 - c), device_id_type=MESH).start()
        token[...] = jnp.zeros_like(token)

    bufs = list(grads) + lands
    outs = pl.pallas_call(
        body, name=name,
        in_specs=[HBM] * len(bufs),
        out_specs=[SEM, SEM] + [HBM] * len(bufs) + [pl.BlockSpec(memory_space=pltpu.VMEM)],
        out_shape=[pltpu.SemaphoreType.DMA((n_t,)), pltpu.SemaphoreType.DMA((n_t,))]
        + [pltpu.HBM(b.shape, b.dtype) for b in bufs] + [jax.ShapeDtypeStruct((8, 128), F32)],
        input_output_aliases={i: 2 + i for i in range(len(bufs))},
        compiler_params=pltpu.CompilerParams(has_side_effects=DATAFLOW),
    )(*[_in_hbm(b) for b in bufs])
    return outs[0], outs[1], list(outs[2:2 + len(bufs)]), outs[-1]


def _core_exchange_wait(name, send_sems, recv_sems, bufs, after):
    n_t = len(bufs) // 2

    def body(*refs):
        land = refs[n_t:2 * n_t]
        send_ref, recv_ref = refs[2 * n_t], refs[2 * n_t + 1]
        x, y, c = _place()
        for t in range(n_t):
            whole = pltpu.make_async_remote_copy(src_ref=land[t], dst_ref=land[t], send_sem=send_ref.at[t],
                                                 recv_sem=recv_ref.at[t], device_id=(x, y, c), device_id_type=MESH)
            whole.wait_send()
            whole.wait_recv()

    outs = pl.pallas_call(
        body, name=name,
        in_specs=[HBM] * (2 * n_t) + [SEM, SEM, ANY], out_specs=[HBM] * (2 * n_t),
        out_shape=[pltpu.HBM(b.shape, b.dtype) for b in bufs],
        input_output_aliases={i: i for i in range(2 * n_t)},
        compiler_params=pltpu.CompilerParams(has_side_effects=DATAFLOW),
    )(*bufs, send_sems, recv_sems, after)
    return list(outs[:n_t]), list(outs[n_t:])


def _chip_exchange_start(name, parts):
    lands = [lax.empty(p.shape, p.dtype) for p in parts]
    return _chip_copies_start(name, parts, lands, lambda src, land, me, pk, c: (src.at[pk], land.at[me]))


def _chip_exchange_wait(name, send_sems, recv_sems, bufs, after):
    n_t = len(bufs) // 2
    return _chip_copies_wait(name, send_sems, recv_sems, bufs,
                             lambda buf: [b.at[pl.ds(0, 3)] for b in buf[:n_t]],
                             lambda buf: [b.at[pl.ds(0, 3)] for b in buf[n_t:]], after)


def _sum_chips(name, parts, landed, place, l, stacked):
    chips, rows, cols = landed.shape
    tr = min(256, rows)
    per = rows // tr

    def body(p_ref, own_ref, *refs):
        land, o_ref = refs[:chips], refs[-1]
        tot = None
        for k in range(chips):
            term = jnp.where(p_ref[0] == k, own_ref[...], land[k][...]).astype(F32)
            tot = term if tot is None else tot + term
        o_ref[...] = tot

    def from_chip(k):
        return pl.BlockSpec((None, tr, cols), lambda i, p: (jnp.where(p[0] == k, (k + 1) % chips, k), i, 0))

    in_specs = [pl.BlockSpec((None, tr, cols), lambda i, p: (p[0], i, 0))] + [from_chip(k) for k in range(chips)]
    args = [parts] + [landed] * chips
    aliases = {}
    if stacked is not None:
        in_specs.append(ANY)
        args.append(stacked)
        aliases = {len(args): 0}
    return pl.pallas_call(
        body, name=name,
        grid_spec=pltpu.PrefetchScalarGridSpec(
            num_scalar_prefetch=1, grid=(per,), in_specs=in_specs,
            out_specs=pl.BlockSpec((None, tr, cols), lambda i, p: (l, p[1] * per + i, 0))),
        out_shape=jax.ShapeDtypeStruct((DEPTH, 2 * rows, cols), F32), input_output_aliases=aliases,
        compiler_params=_params(("parallel",)),
    )(place, *args)


def _core_share_start(name, bufs, l):
    n_t = len(bufs)

    def body(*refs):
        mine = refs[:n_t]
        send_sems, recv_sems = refs[n_t], refs[n_t + 1]
        buf = refs[n_t + 2:2 * n_t + 2]
        token = refs[-1]
        x, y, c = _place()
        for q in range(D2D_CHUNKS):
            for t in range(n_t):
                hr = mine[t].shape[1] // 2
                cr = hr // D2D_CHUNKS
                rows = pl.ds(c * hr + q * cr, cr)
                pltpu.make_async_remote_copy(
                    src_ref=mine[t].at[l, rows], dst_ref=buf[t].at[l, rows], send_sem=send_sems.at[t],
                    recv_sem=recv_sems.at[t], device_id=(x, y, 1 - c), device_id_type=MESH).start()
        token[...] = jnp.zeros_like(token)

    outs = pl.pallas_call(
        body, name=name,
        in_specs=[HBM] * n_t,
        out_specs=[SEM, SEM] + [HBM] * n_t + [pl.BlockSpec(memory_space=pltpu.VMEM)],
        out_shape=[pltpu.SemaphoreType.DMA((n_t,)), pltpu.SemaphoreType.DMA((n_t,))]
        + [pltpu.HBM(b.shape, b.dtype) for b in bufs] + [jax.ShapeDtypeStruct((8, 128), F32)],
        input_output_aliases={i: 2 + i for i in range(n_t)},
        compiler_params=pltpu.CompilerParams(has_side_effects=DATAFLOW),
    )(*[_in_hbm(b) for b in bufs])
    return outs[0], outs[1], list(outs[2:2 + n_t]), outs[-1]


def _core_share_wait(name, send_sems, recv_sems, bufs, l, after):
    def half_layer(buf):
        return [b.at[l, pl.ds(0, b.shape[1] // 2)] for b in buf]

    return _chip_copies_wait(name, send_sems, recv_sems, bufs, half_layer, half_layer, after)


def _all_reduce_small(vec, after=None):
    rows, lanes = vec.shape
    hr = rows // 2

    def body(v_ref, *refs):
        o_ref, sib_ref, chips_ref, send_sems, recv_sems = refs[-5:]
        x, y, c = _place()
        me = 2 * x + y
        sibling = (x, y, 1 - c)
        mine = pl.ds(pl.multiple_of(c * hr, 8), hr)
        theirs = pl.ds(pl.multiple_of((1 - c) * hr, 8), hr)
        swap = pltpu.make_async_remote_copy(
            src_ref=v_ref.at[theirs], dst_ref=sib_ref, send_sem=send_sems.at[0], recv_sem=recv_sems.at[0],
            device_id=sibling, device_id_type=MESH)
        swap.start()
        swap.wait_recv()
        chips_ref[me] = v_ref[mine] + sib_ref[...]
        copies = []
        for j, (px, py, pk) in enumerate(_other_chips(x, y)):
            cp = pltpu.make_async_remote_copy(
                src_ref=chips_ref.at[me], dst_ref=chips_ref.at[me], send_sem=send_sems.at[1 + j],
                recv_sem=recv_sems.at[1 + j], device_id=(px, py, c), device_id_type=MESH)
            cp.start()
            copies.append(cp)
        for j, (px, py, pk) in enumerate(_other_chips(x, y)):
            pltpu.make_async_remote_copy(
                src_ref=chips_ref.at[pk], dst_ref=chips_ref.at[pk], send_sem=send_sems.at[1 + j],
                recv_sem=recv_sems.at[1 + j], device_id=(px, py, c), device_id_type=MESH).wait_recv()
        tot = chips_ref[0]
        for k in range(1, N_CHIPS):
            tot = tot + chips_ref[k]
        o_ref[mine] = tot
        share = pltpu.make_async_remote_copy(
            src_ref=o_ref.at[mine], dst_ref=o_ref.at[mine], send_sem=send_sems.at[4], recv_sem=recv_sems.at[4],
            device_id=sibling, device_id_type=MESH)
        share.start()
        pltpu.make_async_remote_copy(
            src_ref=o_ref.at[theirs], dst_ref=o_ref.at[theirs], send_sem=send_sems.at[4], recv_sem=recv_sems.at[4],
            device_id=sibling, device_id_type=MESH).wait_recv()
        swap.wait_send()
        for cp in copies:
            cp.wait_send()
        share.wait_send()

    vm = pl.BlockSpec(memory_space=pltpu.VMEM)
    return pl.pallas_call(
        body, name="small_all_reduce", in_specs=[vm] + ([] if after is None else [ANY]), out_specs=vm,
        out_shape=jax.ShapeDtypeStruct((rows, lanes), F32),
        scratch_shapes=[pltpu.VMEM((hr, lanes), F32), pltpu.VMEM((N_CHIPS, hr, lanes), F32),
                        pltpu.SemaphoreType.DMA((5,)), pltpu.SemaphoreType.DMA((5,))],
        compiler_params=pltpu.CompilerParams(has_side_effects=True, vmem_limit_bytes=48 * MIB),
    )(vec, *([] if after is None else [after]))


def _adamw(name, w, g, m, v, place, l=0, half=None, done=None, after=None):
    layers, rows, cols = w.shape
    span = rows if half is None else rows // 2
    tr = span
    for cand in (256, 128, 64, 32, 16, 8):
        if span % cand == 0:
            tr = cand
            break
    per = span // tr
    c1 = 1.0 - ADAM_B1 ** ADAM_STEP
    c2 = 1.0 - ADAM_B2 ** ADAM_STEP

    def first_block(p):
        return 0 if half is None else (p[1] if half == "own" else 1 - p[1]) * per

    def body(p_ref, w_ref, g_ref, m_ref, v_ref, *refs):
        go_ref, d_ref, nm_ref, nv_ref = refs[-4:]
        gv = g_ref[...]
        nm = ADAM_B1 * m_ref[...] + (1.0 - ADAM_B1) * gv
        nv = ADAM_B2 * v_ref[...] + (1.0 - ADAM_B2) * (gv * gv)
        go_ref[...] = gv
        nm_ref[...] = nm
        nv_ref[...] = nv
        d_ref[...] = -ADAM_LR * ((nm / c1) / (jnp.sqrt(nv / c2) + ADAM_EPS) + ADAM_WD * w_ref[...])

    blk = pl.BlockSpec((None, tr, cols), lambda i, p: (l, first_block(p) + i, 0))
    out = jax.ShapeDtypeStruct((layers, rows, cols), F32)
    extra = ([] if done is None else list(done)) + ([] if after is None else [after])
    aliases = {} if done is None else {5 + i: i for i in range(4)}
    return pl.pallas_call(
        body, name=name,
        grid_spec=pltpu.PrefetchScalarGridSpec(
            num_scalar_prefetch=1, grid=(per,), in_specs=[blk] * 4 + [ANY] * len(extra), out_specs=[blk] * 4),
        out_shape=[out] * 4, input_output_aliases=aliases,
        compiler_params=_params(("parallel",)),
    )(place, w, g, m, v, *extra)


LANES = 128
SUBLANES = 8
SMALL_SHAPES = {
    "norm_g": (DEPTH, D_MODEL), "sgu_ln_g": (DEPTH, D_A), "sgu_ln_b": (DEPTH, D_A),
    "sgu_w": (DEPTH, A_GROUPS, CHUNK, CHUNK), "sgu_b": (DEPTH, A_GROUPS, CHUNK), "mem_norm_g": (DEPTH, D_MODEL),
    "q_norm_g": (DEPTH, HEAD_DIM), "k_norm_g": (DEPTH, HEAD_DIM)}


def _small_layout():
    at, off = {}, 0
    for k in SMALL_NAMES:
        n = math.prod(SMALL_SHAPES[k]) // LANES
        at[k] = (off, n)
        off += -(-n // SUBLANES) * SUBLANES
    return at, off, -(-(off + SUBLANES) // (2 * SUBLANES)) * 2 * SUBLANES


def _pack_small(parts, loss=None):
    at, loss_row, rows = _small_layout()
    pieces = []
    for k in SMALL_NAMES:
        n = at[k][1]
        pieces.append(jnp.pad(parts[k].reshape(n, LANES), ((0, -(-n // SUBLANES) * SUBLANES - n), (0, 0))))
    tile = jnp.zeros((SUBLANES, LANES), F32) if loss is None else jnp.broadcast_to(loss.reshape(1, 1), (SUBLANES, LANES))
    pieces += [tile, jnp.zeros((rows - loss_row - SUBLANES, LANES), F32)]
    return jnp.concatenate(pieces)


def _adamw_small(w, g, m, v):
    at, _, rows = _small_layout()
    c1 = 1.0 - ADAM_B1 ** ADAM_STEP
    c2 = 1.0 - ADAM_B2 ** ADAM_STEP
    n_names = len(SMALL_NAMES)

    def body(w_ref, g_ref, m_ref, v_ref, *refs):
        outs, (d_ref, nm_ref, nv_ref) = refs[:4 * n_names], refs[4 * n_names:]
        gv = g_ref[...]
        nm = ADAM_B1 * m_ref[...] + (1.0 - ADAM_B1) * gv
        nv = ADAM_B2 * v_ref[...] + (1.0 - ADAM_B2) * (gv * gv)
        nm_ref[...] = nm
        nv_ref[...] = nv
        d_ref[...] = -ADAM_LR * ((nm / c1) / (jnp.sqrt(nv / c2) + ADAM_EPS) + ADAM_WD * w_ref[...])
        for kind, src in enumerate((g_ref, d_ref, nm_ref, nv_ref)):
            for i, k in enumerate(SMALL_NAMES):
                o_ref = outs[kind * n_names + i]
                first, n = at[k]
                shape = SMALL_SHAPES[k]
                if shape[-1] == LANES:
                    o_ref[...] = src[pl.ds(first, n), :].reshape(shape)
                else:
                    per = shape[-1] // LANES
                    for r in range(n):
                        o_ref[pl.ds(r // per, 1), pl.ds((r % per) * LANES, LANES)] = src[pl.ds(first + r, 1), :]

    out_shape = [jax.ShapeDtypeStruct(SMALL_SHAPES[k], F32) for _ in range(4) for k in SMALL_NAMES]
    outs = pl.pallas_call(
        body, name="adamw_small", out_shape=out_shape,
        scratch_shapes=[pltpu.VMEM((rows, LANES), F32)] * 3, compiler_params=_params(None),
    )(w, g, m, v)
    return [dict(zip(SMALL_NAMES, outs[kind * n_names:(kind + 1) * n_names])) for kind in range(4)]


WEIGHT_ORDER = ("norm_g", "w_in", "sgu_ln_g", "sgu_ln_b", "sgu_w", "sgu_b", "mem_norm_g", "w_mem_kv", "q_norm_g",
                "k_norm_g", "w_out")


def kernel(x, mem, norm_g, w_in, sgu_ln_g, sgu_ln_b, sgu_w, sgu_b, mem_norm_g, w_mem_kv, q_norm_g, k_norm_g, w_out, loss_target, m_norm_g, m_w_in, m_sgu_ln_g, m_sgu_ln_b, m_sgu_w, m_sgu_b, m_mem_norm_g, m_w_mem_kv, m_q_norm_g, m_k_norm_g, m_w_out, v_norm_g, v_w_in, v_sgu_ln_g, v_sgu_ln_b, v_sgu_w, v_sgu_b, v_mem_norm_g, v_w_mem_kv, v_q_norm_g, v_k_norm_g, v_w_out):
    weights = dict(norm_g=norm_g, w_in=w_in, sgu_ln_g=sgu_ln_g, sgu_ln_b=sgu_ln_b, sgu_w=sgu_w, sgu_b=sgu_b,
                   mem_norm_g=mem_norm_g, w_mem_kv=w_mem_kv, q_norm_g=q_norm_g, k_norm_g=k_norm_g, w_out=w_out)
    mom_m = dict(norm_g=m_norm_g, w_in=m_w_in, sgu_ln_g=m_sgu_ln_g, sgu_ln_b=m_sgu_ln_b, sgu_w=m_sgu_w, sgu_b=m_sgu_b,
                 mem_norm_g=m_mem_norm_g, w_mem_kv=m_w_mem_kv, q_norm_g=m_q_norm_g, k_norm_g=m_k_norm_g, w_out=m_w_out)
    mom_v = dict(norm_g=v_norm_g, w_in=v_w_in, sgu_ln_g=v_sgu_ln_g, sgu_ln_b=v_sgu_ln_b, sgu_w=v_sgu_w, sgu_b=v_sgu_b,
                 mem_norm_g=v_mem_norm_g, w_mem_kv=v_w_mem_kv, q_norm_g=v_q_norm_g, k_norm_g=v_k_norm_g, w_out=v_w_out)
    big = ("w_in", "w_mem_kv", "w_out")
    sm = {k: weights[k] for k in SMALL_NAMES}

    place = _place_index()
    xs, mems, target = x[0], mem[0], loss_target[0]

    slots = [[_cast_into_slot(f"cast_{k}_{l}", weights[k], l, place) for k in big] for l in range(DEPTH)]
    saved = [None] * DEPTH

    chips, cores = {}, {}

    def start_gather(l, after=None):
        chips[l, "in"] = _gather_start(f"gather_start_{l}_in", slots[l][:1], after)
        chips[l, "rest"] = _gather_start(f"gather_start_{l}_rest", slots[l][1:], chips[l, "in"][3])
        return chips[l, "rest"][3]

    def hand_to_sibling(l, group, after):
        send_sems, recv_sems, bufs, _ = chips[l, group]
        bufs = _gather_wait(f"gather_wait_{l}_{group}", send_sems, recv_sems, bufs, after)
        cores[l, group] = _gather_forward_start(f"gather_forward_{l}_{group}", bufs)
        return cores[l, group][3]

    def whole(l, group, after):
        send_sems, recv_sems, bufs, _ = cores[l, group]
        return _gather_wait(f"gather_whole_{l}_{group}", send_sems, recv_sems, bufs, after)

    class Gathered:
        def __init__(self, l):
            self.l = l

        def w_in(self, h):
            return whole(self.l, "in", h)[0]

        def rest_start(self, proj):
            token = hand_to_sibling(self.l, "rest", proj)
            return start_gather(self.l + 1, token) if self.l + 1 < DEPTH else token

        def rest_finish(self, o_b):
            w_kv_all, w_out_all = whole(self.l, "rest", o_b)
            return w_kv_all, w_out_all, None

        def before_out(self, y):
            return hand_to_sibling(self.l + 1, "in", y) if self.l + 1 < DEPTH else None

    hand_to_sibling(0, "in", [start_gather(0)] + [s for layer in slots[1:] for s in layer])
    cur = xs
    for l in range(DEPTH):
        cur, saved[l] = _layer_fwd(l, cur, mems, sm, Gathered(l))
    dxo, dxo_b, loss_part = _loss_and_grad("loss", cur, target, min(256, xs.shape[0]))

    small_g = [None] * DEPTH
    flight = {}

    class Exchange:
        def __init__(self):
            self.cores = {}

        def start(self, l, group, gives):
            *self.cores[l, group], token = _core_exchange_start(f"grad_core_start_{l}_{group}", gives)
            return token

        def landed(self, l, group, after):
            send_sems, recv_sems, bufs = self.cores[l, group]
            return _core_exchange_wait(f"grad_core_wait_{l}_{group}", send_sems, recv_sems, bufs, after)[1]

        def send(self, l, group, parts):
            *flight[l, group], token = _chip_exchange_start(f"grad_chip_start_{l}_{group}", parts)
            return token

    exchange = Exchange()
    for l in reversed(range(DEPTH)):
        dxo, dxo_b, small_g[l] = _layer_bwd(l, dxo, dxo_b, mems, sm, saved[l], place, exchange)
    grad_x = dxo

    groups = (("out", ("w_out",)), ("in", ("w_in", "w_mem_kv")))
    halves, stepped = dict.fromkeys(big), dict.fromkeys(big)
    small_g = {k: jnp.stack([small_g[l][k] for l in range(DEPTH)]) for k in SMALL_NAMES}
    after = grad_x
    sharing = {}

    def reduce_group(l, group, names):
        nonlocal after
        send_sems, recv_sems, bufs = flight[l, group]
        bufs = _chip_exchange_wait(f"grad_chip_wait_{l}_{group}", send_sems, recv_sems, bufs, after)
        for t, k in enumerate(names):
            halves[k] = _sum_chips(f"grad_chip_sum_{l}_{k}", bufs[t], bufs[len(names) + t], place, l, halves[k])
        *sharing[l, group], after = _core_share_start(f"grad_core_share_{l}_{group}", [halves[k] for k in names], l)

    def step(l, k, buf, half):
        nonlocal after
        tag = "" if half is None else "_" + half
        stepped[k] = _adamw(f"adamw_{k}_{l}{tag}", weights[k], buf, mom_m[k], mom_v[k], place, l, half, stepped[k],
                            after)
        after = stepped[k][1]

    def step_group(l, group, names, overlap):
        nonlocal after
        send_sems, recv_sems, bufs = sharing[l, group]
        if overlap:
            for k, buf in zip(names, bufs):
                step(l, k, buf, "own")
        bufs = _core_share_wait(f"grad_core_shared_{l}_{group}", send_sems, recv_sems, bufs, l, after)
        for k, buf in zip(names, bufs):
            halves[k] = buf
            step(l, k, buf, "other" if overlap else None)

    for l in reversed(range(DEPTH)):
        last = l == 0
        (g_out, n_out), (g_in, n_in) = groups
        reduce_group(l, g_out, n_out)
        if last:
            step_group(l, g_out, n_out, False)
            small_sum = _all_reduce_small(_pack_small(small_g, loss_part), after)
            small_step = _adamw_small(_pack_small(sm), small_sum, _pack_small({k: mom_m[k] for k in SMALL_NAMES}),
                                      _pack_small({k: mom_v[k] for k in SMALL_NAMES}))
            after = small_step[1]["sgu_w"]
        reduce_group(l, g_in, n_in)
        if not last:
            step_group(l, g_out, n_out, False)
        step_group(l, g_in, n_in, last)

    grads, delta, new_m, new_v = ({k: stepped[k][i] for k in big} for i in range(4))
    for out, small in zip((grads, delta, new_m, new_v), small_step):
        out.update(small)
    loss = small_sum[_small_layout()[1], 0]
    return (loss, grad_x[None], *[grads[k] for k in WEIGHT_ORDER], *[delta[k] for k in WEIGHT_ORDER],
            *[new_m[k] for k in WEIGHT_ORDER], *[new_v[k] for k in WEIGHT_ORDER])
```

```python
import functools
import math

import jax
import jax.numpy as jnp
from jax import lax
from jax.experimental import pallas as pl
from jax.experimental.pallas import tpu as pltpu

F32 = jnp.float32
BF16 = jnp.bfloat16
MESH = pl.DeviceIdType.MESH

D_MODEL = 2048
DEPTH = 2
CHUNK = 128
D_A = 1024
A_GROUPS = 8
D_B = 512
D_C = 512
HEADS = 4
HEAD_DIM = 128
IN_WIDTH = 6144
N_CHIPS = 4
EPS = 1e-6
ATT_SCALE = 1.0 / math.sqrt(HEAD_DIM)

OFF_U, OFF_V, OFF_ZA = 0, 1024, 2048
OFF_QB, OFF_KB, OFF_VB, OFF_ZB = 3072, 3584, 4096, 4608
OFF_QC, OFF_ZC = 5120, 5632
OFF_YB, OFF_YC = 1024, 1536

ADAM_LR = 0.001
ADAM_B1 = 0.9
ADAM_B2 = 0.999
ADAM_EPS = 1e-08
ADAM_WD = 0.01
ADAM_STEP = 10

MIB = 1024 * 1024
ANY = pl.BlockSpec(memory_space=pl.ANY)


def _params(semantics=None, vmem_mb=48):
    return pltpu.CompilerParams(dimension_semantics=semantics, vmem_limit_bytes=vmem_mb * MIB)


def _gelu(x):
    return 0.5 * x * (1.0 + lax.erf(x * (1.0 / math.sqrt(2.0))))


def _gelu_grad(x):
    cdf = 0.5 * (1.0 + lax.erf(x * (1.0 / math.sqrt(2.0))))
    pdf = jnp.exp(-0.5 * x * x) * (1.0 / math.sqrt(2.0 * math.pi))
    return cdf + x * pdf


def _sigmoid(x):
    return 1.0 / (1.0 + jnp.exp(-x))


def _silu_and_grad(z):
    s = _sigmoid(z)
    return z * s, s * (1.0 + z * (1.0 - s))


def _split_bf16(x):
    hi = x.astype(BF16)
    lo = (x - hi.astype(F32)).astype(BF16)
    return hi, lo


def _dot(a, b, dims):
    return lax.dot_general(a, b, (dims, ((), ())), preferred_element_type=F32)


NN = ((1,), (0,))
NT = ((1,), (1,))
TN = ((0,), (0,))


def _matmul(name, a, b, *, grid, a_spec, b_spec, o_spec, out_shape, dims, res=None, res_spec=None, after=None,
            place=None, vmem_mb=48):
    nk = grid[2]
    n_in = 2 + (res is not None) + (after is not None)

    def body(*refs):
        if place is not None:
            refs = refs[1:]
        a_ref, b_ref = refs[0], refs[1]
        r_ref = refs[2] if res is not None else None
        o_ref = refs[n_in]
        if len(b_ref.shape) == 3 and dims == NN:
            part = _dot(a_ref[...], b_ref[...].reshape(-1, b_ref.shape[-1]), dims)
        elif len(b_ref.shape) == 3:
            width = b_ref.shape[-1]
            part = None
            for s in range(b_ref.shape[0]):
                term = _dot(a_ref[:, s * width:(s + 1) * width], b_ref[s], dims)
                part = term if part is None else part + term
        else:
            part = _dot(a_ref[...], b_ref[...], dims)
        if nk == 1:
            if r_ref is not None:
                part = part + r_ref[...]
            o_ref[...] = part.astype(o_ref.dtype)
            return
        acc_ref = refs[n_in + 1]
        k = pl.program_id(2)

        @pl.when(k == 0)
        def _():
            acc_ref[...] = part

        @pl.when(k > 0)
        def _():
            acc_ref[...] += part

        @pl.when(k == nk - 1)
        def _():
            tot = acc_ref[...]
            if r_ref is not None:
                tot = tot + r_ref[...]
            o_ref[...] = tot.astype(o_ref.dtype)

    in_specs = [a_spec, b_spec]
    args = [a, b]
    if res is not None:
        in_specs.append(res_spec)
        args.append(res)
    if after is not None:
        in_specs.append(ANY)
        args.append(after)
    acc_shape = tuple(d for d in o_spec.block_shape if d is not None)
    scratch = [pltpu.VMEM(acc_shape, F32)] if nk > 1 else []
    params = _params(("parallel", "parallel", "arbitrary"), vmem_mb)
    if place is not None:
        return pl.pallas_call(
            body, name=name, out_shape=out_shape, compiler_params=params,
            grid_spec=pltpu.PrefetchScalarGridSpec(num_scalar_prefetch=1, grid=grid, in_specs=in_specs,
                                                   out_specs=o_spec, scratch_shapes=scratch),
        )(place, *args)
    return pl.pallas_call(
        body, name=name, grid=grid, in_specs=in_specs, out_specs=o_spec, out_shape=out_shape,
        scratch_shapes=scratch, compiler_params=params,
    )(*args)


def _rms_fwd(name, x, g, tr, after=None, transposed=False):
    rows, d = x.shape

    def body(x_ref, g_ref, *refs):
        outs = refs[1:] if after is not None else refs
        xv = x_ref[...]
        r = lax.rsqrt(jnp.mean(xv * xv, axis=-1, keepdims=True) + EPS)
        h = xv * r * g_ref[...]
        outs[0][...] = h.astype(BF16)
        if transposed:
            outs[1][...] = h.T.astype(BF16)

    out_specs = [pl.BlockSpec((tr, d), lambda i: (i, 0))]
    out_shape = [jax.ShapeDtypeStruct((rows, d), BF16)]
    if transposed:
        out_specs.append(pl.BlockSpec((d, tr), lambda i: (0, i)))
        out_shape.append(jax.ShapeDtypeStruct((d, rows), BF16))
    outs = pl.pallas_call(
        body, name=name, grid=(rows // tr,),
        in_specs=[pl.BlockSpec((tr, d), lambda i: (i, 0)), pl.BlockSpec((1, d), lambda i: (0, 0))]
        + ([] if after is None else [ANY]),
        out_specs=out_specs, out_shape=out_shape,
        compiler_params=_params(("parallel",)),
    )(x, g, *([] if after is None else [after]))
    return outs if transposed else outs[0]


def _rms_bwd(name, x, dh, dres, g, tr, after=None):
    rows, d = x.shape

    def body(x_ref, dh_ref, dres_ref, g_ref, *refs):
        dx_ref, dxb_ref, dg_ref = refs[-3:]
        xv = x_ref[...]
        r = lax.rsqrt(jnp.mean(xv * xv, axis=-1, keepdims=True) + EPS)
        xhat = xv * r
        dhv = dh_ref[...]
        dxh = dhv * g_ref[...]
        dx = r * (dxh - xhat * jnp.mean(dxh * xhat, axis=-1, keepdims=True)) + dres_ref[...]
        dx_ref[...] = dx
        dxb_ref[...] = dx.astype(BF16)
        part = jnp.sum(dhv * xhat, axis=0, keepdims=True)

        @pl.when(pl.program_id(0) == 0)
        def _():
            dg_ref[...] = part

        @pl.when(pl.program_id(0) > 0)
        def _():
            dg_ref[...] += part

    blk = pl.BlockSpec((tr, d), lambda i: (i, 0))
    vec = pl.BlockSpec((1, d), lambda i: (0, 0))
    return pl.pallas_call(
        body, name=name, grid=(rows // tr,), in_specs=[blk, blk, blk, vec] + ([] if after is None else [ANY]),
        out_specs=[blk, blk, vec],
        out_shape=[jax.ShapeDtypeStruct((rows, d), F32), jax.ShapeDtypeStruct((rows, d), BF16),
                   jax.ShapeDtypeStruct((1, d), F32)],
        compiler_params=_params(("arbitrary",)),
    )(x, dh, dres, g, *([] if after is None else [after]))


def _rms_gain_grad(name, x, dh):
    rows, d = x.shape

    def body(x_ref, dh_ref, dg_ref):
        xv = x_ref[...]
        r = lax.rsqrt(jnp.mean(xv * xv, axis=-1, keepdims=True) + EPS)
        dg_ref[...] = jnp.sum(dh_ref[...] * xv * r, axis=0, keepdims=True)

    return pl.pallas_call(
        body, name=name, out_shape=jax.ShapeDtypeStruct((1, d), F32), compiler_params=_params(None),
    )(x, dh)


def _loss_and_grad(name, y, target, tr):
    rows, d = y.shape
    n = rows // tr

    def body(y_ref, t_ref, dx_ref, dxb_ref, loss_ref, acc_ref):
        e = y_ref[...] - t_ref[...]
        dx = e * (1.0 / d)
        dx_ref[...] = dx
        dxb_ref[...] = dx.astype(BF16)
        part = jnp.sum(e * e, axis=0, keepdims=True)
        i = pl.program_id(0)

        @pl.when(i == 0)
        def _():
            acc_ref[...] = part

        @pl.when(i > 0)
        def _():
            acc_ref[...] += part

        @pl.when(i == n - 1)
        def _():
            loss_ref[...] = jnp.sum(acc_ref[...], axis=-1, keepdims=True) * (0.5 / d)

    blk = pl.BlockSpec((tr, d), lambda i: (i, 0))
    return pl.pallas_call(
        body, name=name, grid=(n,), in_specs=[blk, blk],
        out_specs=[blk, blk, pl.BlockSpec((1, 1), lambda i: (0, 0))],
        out_shape=[jax.ShapeDtypeStruct((rows, d), F32), jax.ShapeDtypeStruct((rows, d), BF16),
                   jax.ShapeDtypeStruct((1, 1), F32)],
        scratch_shapes=[pltpu.VMEM((1, d), F32)],
        compiler_params=_params(("arbitrary",)),
    )(y, target)


SB_T = 256
SB_HEADS = 4


LOG2E = 1.4426950408889634


def _sb_scores(q, kblk):
    z2 = _dot(q, kblk, NT) * (ATT_SCALE * LOG2E)
    e = jnp.exp2(-jnp.abs(z2))
    l1 = jnp.minimum(-z2, 0.0) - jnp.log2(1.0 + e)
    lb = l1 + z2
    return z2, e, lb, l1


def _sb_fwd(name, proj, after=None):
    s_len = proj.shape[0]
    t = SB_T
    nq = s_len // t

    def body(q_ref, k_ref, v_ref, *refs):
        o_ref = refs[-1]
        i = pl.program_id(1)
        row = lax.broadcasted_iota(jnp.int32, (t, t), 0)
        col = lax.broadcasted_iota(jnp.int32, (t, t), 1)
        causal = col < row
        after_mat = (row > col).astype(BF16)
        heads = [slice(hh * HEAD_DIM, (hh + 1) * HEAD_DIM) for hh in range(SB_HEADS)]
        q = [q_ref[:, sl].astype(BF16) for sl in heads]

        def tile(kb, state, masked):
            start = pl.multiple_of(kb * t, t)
            out = []
            for hh, sl in enumerate(heads):
                carry, acc = state[hh]
                kblk = k_ref[pl.ds(start, t), sl].astype(BF16)
                vblk = v_ref[pl.ds(start, t), sl].astype(BF16)
                _, _, lb, l1 = _sb_scores(q[hh], kblk)
                if masked:
                    l1 = jnp.where(causal, l1, 0.0)
                hi, lo = _split_bf16(l1)
                after = _dot(hi, after_mat, NN) + _dot(lo, after_mat, NN) + carry
                a = jnp.exp2(lb + after)
                if masked:
                    a = jnp.where(causal, a, 0.0)
                acc = acc + _dot(a.astype(BF16), vblk, NN)
                carry = carry + jnp.sum(l1, axis=-1, keepdims=True)
                out.append((carry, acc))
            return tuple(out)

        zero = (jnp.zeros((t, 1), F32), jnp.zeros((t, HEAD_DIM), F32))
        state = tile(i, (zero,) * SB_HEADS, True)
        state = lax.fori_loop(0, i, lambda n, st: tile(i - 1 - n, st, False), state)
        for hh, sl in enumerate(heads):
            o_ref[:, sl] = state[hh][1]

    cb = SB_HEADS * HEAD_DIM
    return pl.pallas_call(
        body, name=name, grid=(HEADS // SB_HEADS, nq),
        in_specs=[pl.BlockSpec((t, cb), lambda h, i: (i, OFF_QB // cb + h)),
                  pl.BlockSpec((s_len, cb), lambda h, i: (0, OFF_KB // cb + h)),
                  pl.BlockSpec((s_len, cb), lambda h, i: (0, OFF_VB // cb + h))] + ([] if after is None else [ANY]),
        out_specs=pl.BlockSpec((t, cb), lambda h, i: (i, h)),
        out_shape=jax.ShapeDtypeStruct((s_len, D_B), F32),
        compiler_params=_params(("parallel", "arbitrary")),
    )(proj, proj, proj, *([] if after is None else [after]))


def _sb_bwd(name, proj, dy, after=None):
    s_len = proj.shape[0]
    t = SB_T
    nq = s_len // t

    def body(q_ref, k_ref, v_ref, z_ref, dy_ref, *refs):
        dq_ref, dk_ref, dv_ref, a_ref, s_ref = refs[-5:]
        i = pl.program_id(1)

        @pl.when(i == 0)
        def _():
            dk_ref[...] = jnp.zeros_like(dk_ref)
            dv_ref[...] = jnp.zeros_like(dv_ref)

        heads = [slice(hh * HEAD_DIM, (hh + 1) * HEAD_DIM) for hh in range(SB_HEADS)]
        q = [q_ref[:, sl].astype(BF16) for sl in heads]
        silu_z, _ = _silu_and_grad(z_ref[...])
        do_all = dy_ref[...] * silu_z
        do_b = [do_all[:, sl].astype(BF16) for sl in heads]
        row = lax.broadcasted_iota(jnp.int32, (t, t), 0)
        col = lax.broadcasted_iota(jnp.int32, (t, t), 1)
        causal = col < row
        after_mat = (row > col).astype(BF16)
        before_mat = (row < col).astype(BF16)

        def weights(kb, carries, masked):
            start = pl.multiple_of(kb * t, t)
            out = []
            for hh, sl in enumerate(heads):
                kblk = k_ref[pl.ds(start, t), sl].astype(BF16)
                z, _, lb, l1 = _sb_scores(q[hh], kblk)
                if masked:
                    l1 = jnp.where(causal, l1, 0.0)
                hi, lo = _split_bf16(l1)
                after = _dot(hi, after_mat, NN) + _dot(lo, after_mat, NN) + carries[hh]
                a = jnp.exp2(lb + after)
                if masked:
                    a = jnp.where(causal, a, 0.0)
                a_ref[hh, kb] = a
                s_ref[hh, kb] = z
                out.append(carries[hh] + jnp.sum(l1, axis=-1, keepdims=True))
            return tuple(out)

        carries = weights(i, (jnp.zeros((t, 1), F32),) * SB_HEADS, True)
        lax.fori_loop(0, i, lambda n, c: weights(i - 1 - n, c, False), carries)

        def grads(kb, state, masked):
            start = pl.multiple_of(kb * t, t)
            out = []
            for hh, sl in enumerate(heads):
                carry, dq = state[hh]
                kblk = k_ref[pl.ds(start, t), sl].astype(BF16)
                vblk = v_ref[pl.ds(start, t), sl].astype(BF16)
                a = a_ref[hh, kb]
                z = s_ref[hh, kb]
                g = _dot(do_b[hh], vblk, NT) * a
                ghi, glo = _split_bf16(g)
                prefix = _dot(ghi, before_mat, NN) + _dot(glo, before_mat, NN) + carry
                e = jnp.exp2(-jnp.abs(z))
                inv = 1.0 / (1.0 + e)
                pos = z >= 0.0
                beta = jnp.where(pos, inv, e * inv)
                one_m_beta = jnp.where(pos, e * inv, inv)
                dz = (g * one_m_beta - prefix * beta) * ATT_SCALE
                if masked:
                    dz = jnp.where(causal, dz, 0.0)
                dz_b = dz.astype(BF16)
                dq = dq + _dot(dz_b, kblk, NN)
                dk_ref[pl.ds(start, t), sl] += _dot(dz_b, q[hh], TN)
                dv_ref[pl.ds(start, t), sl] += _dot(a.astype(BF16), do_b[hh], TN)
                out.append((carry + jnp.sum(g, axis=-1, keepdims=True), dq))
            return tuple(out)

        zero = (jnp.zeros((t, 1), F32), jnp.zeros((t, HEAD_DIM), F32))
        state = lax.fori_loop(0, i, lambda kb, st: grads(kb, st, False), (zero,) * SB_HEADS)
        state = grads(i, state, True)
        for hh, sl in enumerate(heads):
            dq_ref[:, sl] = state[hh][1]

    cb = SB_HEADS * HEAD_DIM
    qblk = lambda off: pl.BlockSpec((t, cb), lambda h, i: (i, off // cb + h))
    full = lambda off: pl.BlockSpec((s_len, cb), lambda h, i: (0, off // cb + h))
    out = jax.ShapeDtypeStruct((s_len, D_B), F32)
    return pl.pallas_call(
        body, name=name, grid=(HEADS // SB_HEADS, nq),
        in_specs=[qblk(OFF_QB), full(OFF_KB), full(OFF_VB), qblk(OFF_ZB), qblk(OFF_YB)]
        + ([] if after is None else [ANY]),
        out_specs=[qblk(0), full(0), full(0)],
        out_shape=[out, out, out],
        scratch_shapes=[pltpu.VMEM((SB_HEADS, nq, t, t), F32), pltpu.VMEM((SB_HEADS, nq, t, t), F32)],
        compiler_params=_params(("parallel", "arbitrary")),
    )(proj, proj, proj, proj, dy, *([] if after is None else [after]))


MEM_TQ = 512


def _qk_norm(x, g):
    r = lax.rsqrt(jnp.mean(x * x, axis=-1, keepdims=True) + EPS)
    xhat = x * r
    return xhat * g, xhat, r


def _qk_norm_bwd(dn, g, xhat, r):
    dxh = dn * g
    return r * (dxh - xhat * jnp.mean(dxh * xhat, axis=-1, keepdims=True))


def _mem_probs(q, mk, qg, kg):
    qn, qhat, rq = _qk_norm(q, qg)
    kn, khat, rk = _qk_norm(mk, kg)
    qn_b, kn_b = qn.astype(BF16), kn.astype(BF16)
    s = _dot(qn_b, kn_b, NT) * ATT_SCALE
    p = jnp.exp(s - jnp.max(s, axis=-1, keepdims=True))
    p = p / jnp.sum(p, axis=-1, keepdims=True)
    return p, qn_b, kn_b, qhat, rq, khat, rk


def _mem_fwd(name, proj, mem_kv, qg, kg):
    s_len = proj.shape[0]
    m_len = mem_kv.shape[0]
    tq = min(MEM_TQ, s_len)

    def body(q_ref, mk_ref, mv_ref, qg_ref, kg_ref, o_ref):
        p = _mem_probs(q_ref[...], mk_ref[...], qg_ref[...], kg_ref[...])[0]
        o_ref[...] = _dot(p.astype(BF16), mv_ref[...].astype(BF16), NN)

    cb = HEAD_DIM
    vec = pl.BlockSpec((1, cb), lambda h, i: (0, 0))
    return pl.pallas_call(
        body, name=name, grid=(HEADS, s_len // tq),
        in_specs=[pl.BlockSpec((tq, cb), lambda h, i: (i, OFF_QC // cb + h)),
                  pl.BlockSpec((m_len, cb), lambda h, i: (0, h)),
                  pl.BlockSpec((m_len, cb), lambda h, i: (0, HEADS + h)), vec, vec],
        out_specs=pl.BlockSpec((tq, cb), lambda h, i: (i, h)),
        out_shape=jax.ShapeDtypeStruct((s_len, D_C), F32),
        compiler_params=_params(("parallel", "parallel")),
    )(proj, mem_kv, mem_kv, qg, kg)


def _mem_bwd(name, proj, mem_kv, qg, kg, dy):
    s_len = proj.shape[0]
    m_len = mem_kv.shape[0]
    tq = min(MEM_TQ, s_len)

    def body(q_ref, mk_ref, mv_ref, qg_ref, kg_ref, z_ref, dy_ref, dq_ref, dmk_ref, dmv_ref, dqg_ref, dkg_ref):
        h, i = pl.program_id(0), pl.program_id(1)

        @pl.when(i == 0)
        def _():
            dmk_ref[...] = jnp.zeros_like(dmk_ref)
            dmv_ref[...] = jnp.zeros_like(dmv_ref)

        @pl.when((i == 0) & (h == 0))
        def _():
            dqg_ref[...] = jnp.zeros_like(dqg_ref)
            dkg_ref[...] = jnp.zeros_like(dkg_ref)

        qg, kg = qg_ref[...], kg_ref[...]
        p, qn_b, kn_b, qhat, rq, khat, rk = _mem_probs(q_ref[...], mk_ref[...], qg, kg)
        silu_z, _ = _silu_and_grad(z_ref[...])
        do_b = (dy_ref[...] * silu_z).astype(BF16)
        dmv_ref[...] += _dot(p.astype(BF16), do_b, TN)
        dp = _dot(do_b, mv_ref[...].astype(BF16), NT)
        ds = (p * (dp - jnp.sum(dp * p, axis=-1, keepdims=True)) * ATT_SCALE).astype(BF16)
        dqn = _dot(ds, kn_b, NN)
        dkn = _dot(ds, qn_b, TN)
        dq_ref[...] = _qk_norm_bwd(dqn, qg, qhat, rq)
        dmk_ref[...] += _qk_norm_bwd(dkn, kg, khat, rk)
        dqg_ref[...] += jnp.sum(dqn * qhat, axis=0, keepdims=True)
        dkg_ref[...] += jnp.sum(dkn * khat, axis=0, keepdims=True)

    cb = HEAD_DIM
    vec = pl.BlockSpec((1, cb), lambda h, i: (0, 0))
    qblk = lambda off: pl.BlockSpec((tq, cb), lambda h, i: (i, off // cb + h))
    memblk = lambda off: pl.BlockSpec((m_len, cb), lambda h, i: (0, off + h))
    return pl.pallas_call(
        body, name=name, grid=(HEADS, s_len // tq),
        in_specs=[qblk(OFF_QC), memblk(0), memblk(HEADS), vec, vec, qblk(OFF_ZC), qblk(OFF_YC)],
        out_specs=[qblk(0), memblk(0), memblk(0), vec, vec],
        out_shape=[jax.ShapeDtypeStruct((s_len, D_C), F32), jax.ShapeDtypeStruct((m_len, D_C), F32),
                   jax.ShapeDtypeStruct((m_len, D_C), F32), jax.ShapeDtypeStruct((1, cb), F32),
                   jax.ShapeDtypeStruct((1, cb), F32)],
        compiler_params=_params(("arbitrary", "arbitrary")),
    )(proj, mem_kv, mem_kv, qg, kg, proj, dy)


def _sgu_common(u_ref, v_ref, lng_ref, lnb_ref, w_ref, bias_ref):
    ug = _gelu(u_ref[...])
    vg = _gelu(v_ref[...])
    mu = jnp.mean(vg, axis=-1, keepdims=True)
    xc = vg - mu
    rstd = lax.rsqrt(jnp.mean(xc * xc, axis=-1, keepdims=True) + EPS)
    xhat = xc * rstd
    vn = xhat * lng_ref[...] + lnb_ref[...]
    vn_b = vn.astype(BF16)
    row = lax.broadcasted_iota(jnp.int32, (CHUNK, CHUNK), 0)
    col = lax.broadcasted_iota(jnp.int32, (CHUNK, CHUNK), 1)
    tril = row >= col
    mixed = []
    for g in range(A_GROUPS):
        w = jnp.where(tril, w_ref[g], 0.0).astype(BF16)
        sl = slice(g * CHUNK, (g + 1) * CHUNK)
        mixed.append(_dot(w, vn_b[:, sl], NN) + bias_ref[:, sl])
    return ug, xhat, rstd, vn_b, mixed, tril


def _gate_fwd(name, proj, o_b, o_c, lng, lnb, w_s, bias):
    s_len = proj.shape[0]

    def body(u_ref, v_ref, za_ref, zb_ref, zc_ref, ob_ref, oc_ref, lng_ref, lnb_ref, w_ref, bias_ref, y_ref, yt_ref):
        ug, _, _, _, mixed, _ = _sgu_common(u_ref, v_ref, lng_ref, lnb_ref, w_ref, bias_ref)
        sza, _ = _silu_and_grad(za_ref[...])
        gate = ug * sza

        def put(off, width, val):
            y_ref[:, off:off + width] = val.astype(BF16)
            yt_ref[off:off + width, :] = val.T.astype(BF16)

        for g in range(A_GROUPS):
            sl = slice(g * CHUNK, (g + 1) * CHUNK)
            put(g * CHUNK, CHUNK, gate[:, sl] * mixed[g])
        szb, _ = _silu_and_grad(zb_ref[...])
        put(OFF_YB, D_B, ob_ref[...] * szb)
        szc, _ = _silu_and_grad(zc_ref[...])
        put(OFF_YC, D_C, oc_ref[...] * szc)

    wide = lambda off: pl.BlockSpec((CHUNK, D_A), lambda i: (i, off // D_A))
    narrow = lambda off: pl.BlockSpec((CHUNK, D_B), lambda i: (i, off // D_B))
    vec = pl.BlockSpec((1, D_A), lambda i: (0, 0))
    return pl.pallas_call(
        body, name=name, grid=(s_len // CHUNK,),
        in_specs=[wide(OFF_U), wide(OFF_V), wide(OFF_ZA), narrow(OFF_ZB), narrow(OFF_ZC), narrow(0), narrow(0), vec, vec,
                  pl.BlockSpec((A_GROUPS, CHUNK, CHUNK), lambda i: (0, 0, 0)),
                  pl.BlockSpec((CHUNK, D_A), lambda i: (0, 0))],
        out_specs=[pl.BlockSpec((CHUNK, D_MODEL), lambda i: (i, 0)), pl.BlockSpec((D_MODEL, CHUNK), lambda i: (0, i))],
        out_shape=[jax.ShapeDtypeStruct((s_len, D_MODEL), BF16), jax.ShapeDtypeStruct((D_MODEL, s_len), BF16)],
        compiler_params=_params(("parallel",)),
    )(proj, proj, proj, proj, proj, o_b, o_c, lng, lnb, w_s, bias)


def _gate_bwd(name, proj, dy, o_b, o_c, dqkv, dq_c, lng, lnb, w_s, w_s_t, bias):
    s_len = proj.shape[0]
    n = s_len // CHUNK
    dq_b, dk_b, dv_b = dqkv

    def body(u_ref, v_ref, za_ref, zb_ref, zc_ref, dya_ref, dyb_ref, dyc_ref, ob_ref, oc_ref, dq_ref, dk_ref, dv_ref,
             dqc_ref, lng_ref, lnb_ref, w_ref, wt_ref, bias_ref, dp_ref, dw_ref, dsb_ref, dlng_ref, dlnb_ref, dbias_ref):
        i = pl.program_id(0)

        @pl.when(i == 0)
        def _():
            dw_ref[...] = jnp.zeros_like(dw_ref)
            dbias_ref[...] = jnp.zeros_like(dbias_ref)
            dlng_ref[...] = jnp.zeros_like(dlng_ref)
            dlnb_ref[...] = jnp.zeros_like(dlnb_ref)

        ug, xhat, rstd, vn_b, mixed, tril = _sgu_common(u_ref, v_ref, lng_ref, lnb_ref, w_ref, bias_ref)
        za = za_ref[...]
        sza, dsza = _silu_and_grad(za)
        dya = dya_ref[...]
        mixed_all = jnp.concatenate(mixed, axis=-1)
        d_mixed = dya * ug * sza
        dp_ref[:, OFF_U:OFF_U + D_A] = (dya * mixed_all * sza * _gelu_grad(u_ref[...])).astype(BF16)
        dp_ref[:, OFF_ZA:OFF_ZA + D_A] = (dya * ug * mixed_all * dsza).astype(BF16)
        dbias_ref[...] += d_mixed
        dm_b = d_mixed.astype(BF16)
        triu = lax.broadcasted_iota(jnp.int32, (CHUNK, CHUNK), 0) <= lax.broadcasted_iota(jnp.int32, (CHUNK, CHUNK), 1)
        d_vn = []
        for g in range(A_GROUPS):
            sl = slice(g * CHUNK, (g + 1) * CHUNK)
            wt = jnp.where(triu, wt_ref[g], 0.0).astype(BF16)
            d_vn.append(_dot(wt, dm_b[:, sl], NN))
            dw_ref[g] += jnp.where(tril, _dot(dm_b[:, sl], vn_b[:, sl], NT), 0.0)
        d_vn = jnp.concatenate(d_vn, axis=-1)
        dlng_ref[...] += jnp.sum(d_vn * xhat, axis=0, keepdims=True)
        dlnb_ref[...] += jnp.sum(d_vn, axis=0, keepdims=True)
        dxh = d_vn * lng_ref[...]
        d_vg = rstd * (dxh - jnp.mean(dxh, axis=-1, keepdims=True)
                       - xhat * jnp.mean(dxh * xhat, axis=-1, keepdims=True))
        dp_ref[:, OFF_V:OFF_V + D_A] = (d_vg * _gelu_grad(v_ref[...])).astype(BF16)
        dp_ref[:, OFF_QB:OFF_QB + D_B] = dq_ref[...].astype(BF16)
        dp_ref[:, OFF_KB:OFF_KB + D_B] = dk_ref[...].astype(BF16)
        dp_ref[:, OFF_VB:OFF_VB + D_B] = dv_ref[...].astype(BF16)
        _, dszb = _silu_and_grad(zb_ref[...])
        dp_ref[:, OFF_ZB:OFF_ZB + D_B] = (dyb_ref[...] * ob_ref[...] * dszb).astype(BF16)
        dp_ref[:, OFF_QC:OFF_QC + D_C] = dqc_ref[...].astype(BF16)
        _, dszc = _silu_and_grad(zc_ref[...])
        dp_ref[:, OFF_ZC:OFF_ZC + D_C] = (dyc_ref[...] * oc_ref[...] * dszc).astype(BF16)

        @pl.when(i == n - 1)
        def _():
            ch = lax.broadcasted_iota(jnp.int32, (D_A, CHUNK), 0)
            gcol = lax.broadcasted_iota(jnp.int32, (D_A, CHUNK), 1)
            pick = (ch // (D_A // A_GROUPS) == gcol).astype(BF16)
            rest = dbias_ref[...]
            tot = jnp.zeros((CHUNK, CHUNK), F32)
            for _ in range(3):
                term = rest.astype(BF16)
                tot = tot + _dot(term, pick, NN)
                rest = rest - term.astype(F32)
            dsb_ref[...] = tot

    wide = lambda off: pl.BlockSpec((CHUNK, D_A), lambda i: (i, off // D_A))
    narrow = lambda off: pl.BlockSpec((CHUNK, D_B), lambda i: (i, off // D_B))
    vec = pl.BlockSpec((1, D_A), lambda i: (0, 0))
    wspec = pl.BlockSpec((A_GROUPS, CHUNK, CHUNK), lambda i: (0, 0, 0))
    bspec = pl.BlockSpec((CHUNK, D_A), lambda i: (0, 0))
    return pl.pallas_call(
        body, name=name, grid=(n,),
        in_specs=[wide(OFF_U), wide(OFF_V), wide(OFF_ZA), narrow(OFF_ZB), narrow(OFF_ZC),
                  wide(0), narrow(OFF_YB), narrow(OFF_YC), narrow(0), narrow(0), narrow(0), narrow(0), narrow(0),
                  narrow(0), vec, vec, wspec, wspec, bspec],
        out_specs=[pl.BlockSpec((CHUNK, IN_WIDTH), lambda i: (i, 0)), wspec,
                   pl.BlockSpec((CHUNK, CHUNK), lambda i: (0, 0)), vec, vec],
        out_shape=[jax.ShapeDtypeStruct((s_len, IN_WIDTH), BF16), jax.ShapeDtypeStruct((A_GROUPS, CHUNK, CHUNK), F32),
                   jax.ShapeDtypeStruct((CHUNK, CHUNK), F32), jax.ShapeDtypeStruct((1, D_A), F32),
                   jax.ShapeDtypeStruct((1, D_A), F32)],
        scratch_shapes=[pltpu.VMEM((CHUNK, D_A), F32)],
        compiler_params=_params(("arbitrary",)),
    )(proj, proj, proj, proj, proj, dy, dy, dy, o_b, o_c, dq_b, dk_b, dv_b, dq_c, lng, lnb, w_s, w_s_t, bias)


IN_SHARD = IN_WIDTH // N_CHIPS
ROW_SHARD = D_MODEL // N_CHIPS


def _bias_rows(sgu_b_l):
    return jnp.repeat(sgu_b_l.T, D_A // A_GROUPS, axis=1)


class _WholeWeights:
    def __init__(self, w_in_all, w_kv_all, w_out_all):
        self.weights = (w_in_all, w_kv_all, w_out_all)

    def w_in(self, h):
        return self.weights[0]

    def rest_start(self, proj):
        return None

    def rest_finish(self, o_b):
        return self.weights[1], self.weights[2], None

    def before_out(self, y):
        return None


def _layer_fwd(l, x, mem, sm, hooks):
    s_len = x.shape[0]
    m_len = mem.shape[0]
    tm = min(1024, s_len)
    tn = IN_SHARD
    per = IN_SHARD // tn
    h, h_t = _rms_fwd(f"rms_fwd_{l}", x, sm["norm_g"][l][None], min(256, s_len), transposed=True)
    w_in_all = hooks.w_in(h)
    proj = _matmul(
        f"in_proj_{l}", h, w_in_all, grid=(s_len // tm, IN_WIDTH // tn, 1),
        a_spec=pl.BlockSpec((tm, D_MODEL), lambda i, j, k: (i, 0)),
        b_spec=pl.BlockSpec((None, D_MODEL, tn), lambda i, j, k: (j // per, 0, j % per)),
        o_spec=pl.BlockSpec((tm, tn), lambda i, j, k: (i, j)),
        out_shape=jax.ShapeDtypeStruct((s_len, IN_WIDTH), F32), dims=NN)
    o_b = _sb_fwd(f"sb_fwd_{l}", proj, hooks.rest_start(proj))
    w_kv_all, w_out_all, after = hooks.rest_finish(o_b)
    mem_h = _rms_fwd(f"mem_rms_fwd_{l}", mem, sm["mem_norm_g"][l][None], m_len, after)
    mem_kv = _matmul(
        f"mem_kv_{l}", mem_h, w_kv_all, grid=(1, 2, N_CHIPS),
        a_spec=pl.BlockSpec((m_len, ROW_SHARD), lambda i, j, k: (0, k)),
        b_spec=pl.BlockSpec((None, ROW_SHARD, D_C), lambda i, j, k: (k, 0, j)),
        o_spec=pl.BlockSpec((m_len, D_C), lambda i, j, k: (0, j)),
        out_shape=jax.ShapeDtypeStruct((m_len, 2 * D_C), F32), dims=NN)
    qg, kg = sm["q_norm_g"][l][None], sm["k_norm_g"][l][None]
    o_c = _mem_fwd(f"mem_fwd_{l}", proj, mem_kv, qg, kg)
    bias = _bias_rows(sm["sgu_b"][l])
    y, y_t = _gate_fwd(f"gate_fwd_{l}", proj, o_b, o_c, sm["sgu_ln_g"][l][None], sm["sgu_ln_b"][l][None],
                       sm["sgu_w"][l], bias)
    tn_o = 512
    x_next = _matmul(
        f"out_proj_{l}", y, w_out_all, grid=(s_len // tm, D_MODEL // tn_o, 1),
        a_spec=pl.BlockSpec((tm, D_MODEL), lambda i, j, k: (i, 0)),
        b_spec=pl.BlockSpec((N_CHIPS, ROW_SHARD, tn_o), lambda i, j, k: (0, 0, j)),
        o_spec=pl.BlockSpec((tm, tn_o), lambda i, j, k: (i, j)),
        out_shape=jax.ShapeDtypeStruct((s_len, D_MODEL), F32), dims=NN,
        res=x, res_spec=pl.BlockSpec((tm, tn_o), lambda i, j, k: (i, j)), after=hooks.before_out(y))
    saved = dict(x=x, h_t=h_t, proj=proj, mem_h=mem_h, mem_kv=mem_kv, o_b=o_b, o_c=o_c, y_t=y_t, bias=bias,
                 weights=(w_in_all, w_kv_all, w_out_all))
    return x_next, saved


class _NoExchange:
    def __init__(self):
        self.gave, self.kept = {}, {}

    def start(self, l, group, gives):
        self.gave[l, group] = gives
        return None

    def landed(self, l, group, after):
        return [jnp.zeros_like(g) for g in self.gave[l, group]]

    def send(self, l, group, parts):
        self.kept[l, group] = parts
        return None


def _layer_bwd(l, dxo, dxo_b, mem, sm, saved, place, exchange):
    s_len = dxo.shape[0]
    m_len = mem.shape[0]
    proj, y_t, h_t, mem_h, mem_kv = saved["proj"], saved["y_t"], saved["h_t"], saved["mem_h"], saved["mem_kv"]
    w_in_all, w_kv_all, w_out_all = saved["weights"]
    tm = min(1024, s_len)
    tn = 768
    per = IN_SHARD // tn
    half_rows = ROW_SHARD // 2

    def halves(make):
        give = lambda: make("give", lambda p: 1 - p[1], None, F32)
        keep = lambda theirs: make("keep", lambda p: p[1], theirs, BF16)
        return give, keep

    def grad_out(tag, half, theirs, dtype):
        o_spec = pl.BlockSpec((None, half_rows, 1024), lambda i, j, k, p: (i, 0, j))
        return _matmul(
            f"d_w_out_{l}_{tag}", y_t, dxo_b, grid=(N_CHIPS, D_MODEL // 1024, 1), place=place,
            a_spec=pl.BlockSpec((half_rows, s_len), lambda i, j, k, p: (2 * i + half(p), 0)),
            b_spec=pl.BlockSpec((s_len, 1024), lambda i, j, k, p: (0, j)), o_spec=o_spec,
            out_shape=jax.ShapeDtypeStruct((N_CHIPS, half_rows, D_MODEL), dtype), dims=NN,
            res=theirs, res_spec=o_spec)

    def grad_in(tag, half, theirs, dtype):
        o_spec = pl.BlockSpec((None, D_MODEL // 2, tn), lambda i, j, k, p: (j // per, 0, j % per))
        return _matmul(
            f"d_w_in_{l}_{tag}", h_t, dproj, grid=(1, IN_WIDTH // tn, 1), place=place,
            a_spec=pl.BlockSpec((D_MODEL // 2, s_len), lambda i, j, k, p: (half(p), 0)),
            b_spec=pl.BlockSpec((s_len, tn), lambda i, j, k, p: (0, j)), o_spec=o_spec,
            out_shape=jax.ShapeDtypeStruct((N_CHIPS, D_MODEL // 2, IN_SHARD), dtype), dims=NN,
            res=theirs, res_spec=o_spec)

    def grad_kv(tag, half, theirs, dtype):
        o_spec = pl.BlockSpec((None, half_rows, 2 * D_C), lambda i, j, k, p: (i, 0, 0))
        return _matmul(
            f"d_w_kv_{l}_{tag}", mem_h, dkv_b, grid=(N_CHIPS, 1, 1), place=place,
            a_spec=pl.BlockSpec((m_len, half_rows), lambda i, j, k, p: (0, 2 * i + half(p))),
            b_spec=pl.BlockSpec((m_len, 2 * D_C), lambda i, j, k, p: (0, 0)), o_spec=o_spec,
            out_shape=jax.ShapeDtypeStruct((N_CHIPS, half_rows, 2 * D_C), dtype), dims=TN,
            res=theirs, res_spec=o_spec)

    give_out, keep_out = halves(grad_out)
    token = exchange.start(l, "out", [give_out()])
    dy = _matmul(
        f"d_y_{l}", dxo_b, w_out_all, grid=(s_len // tm, N_CHIPS, 1),
        a_spec=pl.BlockSpec((tm, D_MODEL), lambda i, j, k: (i, 0)),
        b_spec=pl.BlockSpec((None, ROW_SHARD, D_MODEL), lambda i, j, k: (j, 0, 0)),
        o_spec=pl.BlockSpec((tm, ROW_SHARD), lambda i, j, k: (i, j)),
        out_shape=jax.ShapeDtypeStruct((s_len, D_MODEL), F32), dims=NT, after=token)
    (theirs_out,) = exchange.landed(l, "out", dy)
    token = exchange.send(l, "out", [keep_out(theirs_out)])
    qg, kg = sm["q_norm_g"][l][None], sm["k_norm_g"][l][None]
    dqkv = _sb_bwd(f"sb_bwd_{l}", proj, dy, token)
    dq_c, dmk, dmv, dqg, dkg = _mem_bwd(f"mem_bwd_{l}", proj, mem_kv, qg, kg, dy)
    w_s = sm["sgu_w"][l]
    dproj, dws, dbias, dlng, dlnb = _gate_bwd(
        f"gate_bwd_{l}", proj, dy, saved["o_b"], saved["o_c"], dqkv, dq_c, sm["sgu_ln_g"][l][None],
        sm["sgu_ln_b"][l][None], w_s, jnp.swapaxes(w_s, 1, 2), saved["bias"])
    dkv_b = jnp.concatenate([dmk, dmv], axis=1).astype(BF16)
    give_in, keep_in = halves(grad_in)
    give_kv, keep_kv = halves(grad_kv)
    token = exchange.start(l, "in", [give_in(), give_kv()])
    dh = _matmul(
        f"d_h_{l}", dproj, w_in_all, grid=(s_len // tm, D_MODEL // 512, 1),
        a_spec=pl.BlockSpec((tm, IN_WIDTH), lambda i, j, k: (i, 0)),
        b_spec=pl.BlockSpec((N_CHIPS, 512, IN_SHARD), lambda i, j, k: (0, j, 0)),
        o_spec=pl.BlockSpec((tm, 512), lambda i, j, k: (i, j)),
        out_shape=jax.ShapeDtypeStruct((s_len, D_MODEL), F32), dims=NT, after=token, vmem_mb=56)
    theirs_in, theirs_kv = exchange.landed(l, "in", dh)
    token = exchange.send(l, "in", [keep_in(theirs_in), keep_kv(theirs_kv)])
    dx, dx_b, dng = _rms_bwd(f"rms_bwd_{l}", saved["x"], dh, dxo, sm["norm_g"][l][None], min(256, s_len), token)
    d_mem_h = _matmul(
        f"d_mem_h_{l}", dkv_b, w_kv_all, grid=(1, N_CHIPS, 1),
        a_spec=pl.BlockSpec((m_len, 2 * D_C), lambda i, j, k: (0, 0)),
        b_spec=pl.BlockSpec((None, ROW_SHARD, 2 * D_C), lambda i, j, k: (j, 0, 0)),
        o_spec=pl.BlockSpec((m_len, ROW_SHARD), lambda i, j, k: (0, j)),
        out_shape=jax.ShapeDtypeStruct((m_len, D_MODEL), F32), dims=NT)
    dmng = _rms_gain_grad(f"mem_rms_bwd_{l}", mem, d_mem_h)
    dsgu_b = dbias[:, :A_GROUPS].T
    small = dict(norm_g=dng[0], sgu_ln_g=dlng[0], sgu_ln_b=dlnb[0], sgu_w=dws, sgu_b=dsgu_b, mem_norm_g=dmng[0],
                 q_norm_g=dqg[0], k_norm_g=dkg[0])
    return dx, dx_b, small


SMALL_NAMES = ("norm_g", "sgu_ln_g", "sgu_ln_b", "sgu_w", "sgu_b", "mem_norm_g", "q_norm_g", "k_norm_g")


def _local_step(x, mem, target, sm, w_all):
    saved = []
    cur = x
    for l in range(DEPTH):
        cur, sv = _layer_fwd(l, cur, mem, sm, _WholeWeights(*w_all[l]))
        saved.append(sv)
    dxo, dxo_b, loss = _loss_and_grad("loss", cur, target, min(256, x.shape[0]))
    small = [None] * DEPTH
    exchange = _NoExchange()
    place = jnp.zeros((2,), jnp.int32)
    for l in reversed(range(DEPTH)):
        dxo, dxo_b, small[l] = _layer_bwd(l, dxo, dxo_b, mem, sm, saved[l], place, exchange)
    small = {k: jnp.stack([small[l][k] for l in range(DEPTH)]) for k in SMALL_NAMES}
    return loss, dxo, small, exchange.gave, exchange.kept


def _place():
    x, y, c = lax.axis_index("x"), lax.axis_index("y"), lax.axis_index("c")
    return x, y, c


def _other_chips(x, y):
    return [(1 - x, y, 2 * (1 - x) + y), (x, 1 - y, 2 * x + 1 - y), (1 - x, 1 - y, 2 * (1 - x) + 1 - y)]


D2D_CHUNKS = 8


def _place_index():
    return jnp.stack([2 * lax.axis_index("x") + lax.axis_index("y"), lax.axis_index("c")]).astype(jnp.int32)


def _cast_into_slot(name, w, l, place):
    _, rows, cols = w.shape
    tr = min(256, rows)

    def body(p_ref, w_ref, o_ref):
        o_ref[...] = w_ref[...].astype(BF16)

    return pl.pallas_call(
        body, name=name,
        grid_spec=pltpu.PrefetchScalarGridSpec(
            num_scalar_prefetch=1, grid=(rows // tr,),
            in_specs=[pl.BlockSpec((None, tr, cols), lambda i, p: (l, i, 0))],
            out_specs=pl.BlockSpec((None, tr, cols), lambda i, p: (p[0], i, 0))),
        out_shape=jax.ShapeDtypeStruct((N_CHIPS, rows, cols), BF16),
        compiler_params=_params(("parallel",)),
    )(place, w)


HBM = pl.BlockSpec(memory_space=pltpu.HBM)
SEM = pl.BlockSpec(memory_space=pltpu.SEMAPHORE)
DATAFLOW = pltpu.SideEffectType.DATAFLOW_SIDE_EFFECTING


def _in_hbm(a):
    return pltpu.with_memory_space_constraint(a, pltpu.HBM)


def _chip_copies_start(name, srcs, lands, make_copy, after=None):
    n_t = len(srcs)
    in_place = lands is None
    n_after = 0 if after is None else 1

    def body(*refs):
        src = refs[:n_t]
        k = (n_t if in_place else 2 * n_t) + n_after
        send_sems, recv_sems = refs[k], refs[k + 1]
        land = refs[k + 2:k + 2 + n_t] if in_place else refs[k + 2 + n_t:k + 2 + 2 * n_t]
        token = refs[-1]
        x, y, c = _place()
        me = 2 * x + y
        for t in range(n_t):
            for px, py, pk in _other_chips(x, y):
                s, d = make_copy(src[t], land[t], me, pk, c)
                pltpu.make_async_remote_copy(
                    src_ref=s, dst_ref=d, send_sem=send_sems.at[t], recv_sem=recv_sems.at[t],
                    device_id=(px, py, c), device_id_type=MESH).start()
        token[...] = jnp.zeros_like(token)

    bufs = list(srcs) if in_place else list(srcs) + list(lands)
    outs = pl.pallas_call(
        body, name=name,
        in_specs=[HBM] * len(bufs) + [ANY] * n_after,
        out_specs=[SEM, SEM] + [HBM] * len(bufs) + [pl.BlockSpec(memory_space=pltpu.VMEM)],
        out_shape=[pltpu.SemaphoreType.DMA((n_t,)), pltpu.SemaphoreType.DMA((n_t,))]
        + [pltpu.HBM(b.shape, b.dtype) for b in bufs] + [jax.ShapeDtypeStruct((8, 128), F32)],
        input_output_aliases={i: 2 + i for i in range(len(bufs))},
        compiler_params=pltpu.CompilerParams(has_side_effects=DATAFLOW),
    )(*[_in_hbm(b) for b in bufs], *([] if after is None else [after]))
    return outs[0], outs[1], list(outs[2:2 + len(bufs)]), outs[-1]


def _chip_copies_wait(name, send_sems, recv_sems, bufs, sent, landed, after):
    n_b = len(bufs)

    def body(*refs):
        buf = refs[:n_b]
        send_ref, recv_ref = refs[n_b], refs[n_b + 1]
        x, y, c = _place()
        for t, (s, d) in enumerate(zip(sent(buf), landed(buf))):
            out = pltpu.make_async_remote_copy(src_ref=s, dst_ref=s, send_sem=send_ref.at[t], recv_sem=recv_ref.at[t],
                                               device_id=(x, y, c), device_id_type=MESH)
            out.wait_send()
            arrived = pltpu.make_async_remote_copy(src_ref=d, dst_ref=d, send_sem=send_ref.at[t],
                                                   recv_sem=recv_ref.at[t], device_id=(x, y, c), device_id_type=MESH)
            arrived.wait_recv()

    after = list(after) if isinstance(after, (list, tuple)) else [after]
    return pl.pallas_call(
        body, name=name,
        in_specs=[HBM] * n_b + [SEM, SEM] + [ANY] * len(after), out_specs=[HBM] * n_b,
        out_shape=[pltpu.HBM(b.shape, b.dtype) for b in bufs],
        input_output_aliases={i: i for i in range(n_b)},
        compiler_params=pltpu.CompilerParams(has_side_effects=DATAFLOW),
    )(*bufs, send_sems, recv_sems, *after)


def _gather_start(name, bufs, after=None):
    def make_copy(src, land, me, pk, c):
        hr = src.shape[1] // 2
        return src.at[me, pl.ds(c * hr, hr)], land.at[me, pl.ds(c * hr, hr)]

    return _chip_copies_start(name, bufs, None, make_copy, after)


def _gather_wait(name, send_sems, recv_sems, bufs, after):
    def three_halves(buf):
        return [b.at[pl.ds(0, 3), pl.ds(0, b.shape[1] // 2)] for b in buf]

    return _chip_copies_wait(name, send_sems, recv_sems, bufs, three_halves, three_halves, after)


def _gather_forward_start(name, bufs):
    n_t = len(bufs)

    def body(*refs):
        mine = refs[:n_t]
        send_sems, recv_sems = refs[n_t], refs[n_t + 1]
        buf = refs[n_t + 2:2 * n_t + 2]
        token = refs[-1]
        x, y, c = _place()
        for q in range(D2D_CHUNKS):
            for t in range(n_t):
                hr = mine[t].shape[1] // 2
                cr = hr // D2D_CHUNKS
                rows = pl.ds(c * hr + q * cr, cr)
                for _, _, pk in _other_chips(x, y):
                    pltpu.make_async_remote_copy(
                        src_ref=mine[t].at[pk, rows], dst_ref=buf[t].at[pk, rows], send_sem=send_sems.at[t],
                        recv_sem=recv_sems.at[t], device_id=(x, y, 1 - c), device_id_type=MESH).start()
        token[...] = jnp.zeros_like(token)

    outs = pl.pallas_call(
        body, name=name,
        in_specs=[HBM] * n_t,
        out_specs=[SEM, SEM] + [HBM] * n_t + [pl.BlockSpec(memory_space=pltpu.VMEM)],
        out_shape=[pltpu.SemaphoreType.DMA((n_t,)), pltpu.SemaphoreType.DMA((n_t,))]
        + [pltpu.HBM(b.shape, b.dtype) for b in bufs] + [jax.ShapeDtypeStruct((8, 128), F32)],
        input_output_aliases={i: 2 + i for i in range(n_t)},
        compiler_params=pltpu.CompilerParams(has_side_effects=DATAFLOW),
    )(*[_in_hbm(b) for b in bufs])
    return outs[0], outs[1], list(outs[2:2 + n_t]), outs[-1]


def _core_exchange_start(name, grads):
    n_t = len(grads)
    lands = [lax.empty(g.shape, g.dtype) for g in grads]

    def body(*refs):
        src = refs[:n_t]
        send_sems, recv_sems = refs[2 * n_t], refs[2 * n_t + 1]
        land = refs[2 * n_t + 2 + n_t:2 * n_t + 2 + 2 * n_t]
        token = refs[-1]
        x, y, c = _place()
        for q in range(D2D_CHUNKS):
            for t in range(n_t):
                cr = src[t].shape[1] // D2D_CHUNKS
                rows = pl.ds(q * cr, cr)
                pltpu.make_async_remote_copy(
                    src_ref=src[t].at[:, rows], dst_ref=land[t].at[:, rows], send_sem=send_sems.at[t],
                    recv_sem=recv_sems.at[t], device_id=(x, y, 1 - c), device_id_type=MESH).start()
        token[...] = jnp.zeros_like(token)

    bufs = list(grads) + lands
    outs = pl.pallas_call(
        body, name=name,
        in_specs=[HBM] * len(bufs),
        out_specs=[SEM, SEM] + [HBM] * len(bufs) + [pl.BlockSpec(memory_space=pltpu.VMEM)],
        out_shape=[pltpu.SemaphoreType.DMA((n_t,)), pltpu.SemaphoreType.DMA((n_t,))]
        + [pltpu.HBM(b.shape, b.dtype) for b in bufs] + [jax.ShapeDtypeStruct((8, 128), F32)],
        input_output_aliases={i: 2 + i for i in range(len(bufs))},
        compiler_params=pltpu.CompilerParams(has_side_effects=DATAFLOW),
    )(*[_in_hbm(b) for b in bufs])
    return outs[0], outs[1], list(outs[2:2 + len(bufs)]), outs[-1]


def _core_exchange_wait(name, send_sems, recv_sems, bufs, after):
    n_t = len(bufs) // 2

    def body(*refs):
        land = refs[n_t:2 * n_t]
        send_ref, recv_ref = refs[2 * n_t], refs[2 * n_t + 1]
        x, y, c = _place()
        for t in range(n_t):
            whole = pltpu.make_async_remote_copy(src_ref=land[t], dst_ref=land[t], send_sem=send_ref.at[t],
                                                 recv_sem=recv_ref.at[t], device_id=(x, y, c), device_id_type=MESH)
            whole.wait_send()
            whole.wait_recv()

    outs = pl.pallas_call(
        body, name=name,
        in_specs=[HBM] * (2 * n_t) + [SEM, SEM, ANY], out_specs=[HBM] * (2 * n_t),
        out_shape=[pltpu.HBM(b.shape, b.dtype) for b in bufs],
        input_output_aliases={i: i for i in range(2 * n_t)},
        compiler_params=pltpu.CompilerParams(has_side_effects=DATAFLOW),
    )(*bufs, send_sems, recv_sems, after)
    return list(outs[:n_t]), list(outs[n_t:])


def _chip_exchange_start(name, parts):
    lands = [lax.empty(p.shape, p.dtype) for p in parts]
    return _chip_copies_start(name, parts, lands, lambda src, land, me, pk, c: (src.at[pk], land.at[me]))


def _chip_exchange_wait(name, send_sems, recv_sems, bufs, after):
    n_t = len(bufs) // 2
    return _chip_copies_wait(name, send_sems, recv_sems, bufs,
                             lambda buf: [b.at[pl.ds(0, 3)] for b in buf[:n_t]],
                             lambda buf: [b.at[pl.ds(0, 3)] for b in buf[n_t:]], after)


def _sum_chips(name, parts, landed, place, l, stacked):
    chips, rows, cols = landed.shape
    tr = min(256, rows)
    per = rows // tr

    def body(p_ref, own_ref, *refs):
        land, o_ref = refs[:chips], refs[-1]
        tot = None
        for k in range(chips):
            term = jnp.where(p_ref[0] == k, own_ref[...], land[k][...]).astype(F32)
            tot = term if tot is None else tot + term
        o_ref[...] = tot

    def from_chip(k):
        return pl.BlockSpec((None, tr, cols), lambda i, p: (jnp.where(p[0] == k, (k + 1) % chips, k), i, 0))

    in_specs = [pl.BlockSpec((None, tr, cols), lambda i, p: (p[0], i, 0))] + [from_chip(k) for k in range(chips)]
    args = [parts] + [landed] * chips
    aliases = {}
    if stacked is not None:
        in_specs.append(ANY)
        args.append(stacked)
        aliases = {len(args): 0}
    return pl.pallas_call(
        body, name=name,
        grid_spec=pltpu.PrefetchScalarGridSpec(
            num_scalar_prefetch=1, grid=(per,), in_specs=in_specs,
            out_specs=pl.BlockSpec((None, tr, cols), lambda i, p: (l, p[1] * per + i, 0))),
        out_shape=jax.ShapeDtypeStruct((DEPTH, 2 * rows, cols), F32), input_output_aliases=aliases,
        compiler_params=_params(("parallel",)),
    )(place, *args)


def _core_share_start(name, bufs, l):
    n_t = len(bufs)

    def body(*refs):
        mine = refs[:n_t]
        send_sems, recv_sems = refs[n_t], refs[n_t + 1]
        buf = refs[n_t + 2:2 * n_t + 2]
        token = refs[-1]
        x, y, c = _place()
        for q in range(D2D_CHUNKS):
            for t in range(n_t):
                hr = mine[t].shape[1] // 2
                cr = hr // D2D_CHUNKS
                rows = pl.ds(c * hr + q * cr, cr)
                pltpu.make_async_remote_copy(
                    src_ref=mine[t].at[l, rows], dst_ref=buf[t].at[l, rows], send_sem=send_sems.at[t],
                    recv_sem=recv_sems.at[t], device_id=(x, y, 1 - c), device_id_type=MESH).start()
        token[...] = jnp.zeros_like(token)

    outs = pl.pallas_call(
        body, name=name,
        in_specs=[HBM] * n_t,
        out_specs=[SEM, SEM] + [HBM] * n_t + [pl.BlockSpec(memory_space=pltpu.VMEM)],
        out_shape=[pltpu.SemaphoreType.DMA((n_t,)), pltpu.SemaphoreType.DMA((n_t,))]
        + [pltpu.HBM(b.shape, b.dtype) for b in bufs] + [jax.ShapeDtypeStruct((8, 128), F32)],
        input_output_aliases={i: 2 + i for i in range(n_t)},
        compiler_params=pltpu.CompilerParams(has_side_effects=DATAFLOW),
    )(*[_in_hbm(b) for b in bufs])
    return outs[0], outs[1], list(outs[2:2 + n_t]), outs[-1]


def _core_share_wait(name, send_sems, recv_sems, bufs, l, after):
    def half_layer(buf):
        return [b.at[l, pl.ds(0, b.shape[1] // 2)] for b in buf]

    return _chip_copies_wait(name, send_sems, recv_sems, bufs, half_layer, half_layer, after)


def _all_reduce_small(vec, after=None):
    rows, lanes = vec.shape
    hr = rows // 2

    def body(v_ref, *refs):
        o_ref, sib_ref, chips_ref, send_sems, recv_sems = refs[-5:]
        x, y, c = _place()
        me = 2 * x + y
        sibling = (x, y, 1 - c)
        mine = pl.ds(pl.multiple_of(c * hr, 8), hr)
        theirs = pl.ds(pl.multiple_of((1 - c) * hr, 8), hr)
        swap = pltpu.make_async_remote_copy(
            src_ref=v_ref.at[theirs], dst_ref=sib_ref, send_sem=send_sems.at[0], recv_sem=recv_sems.at[0],
            device_id=sibling, device_id_type=MESH)
        swap.start()
        swap.wait_recv()
        chips_ref[me] = v_ref[mine] + sib_ref[...]
        copies = []
        for j, (px, py, pk) in enumerate(_other_chips(x, y)):
            cp = pltpu.make_async_remote_copy(
                src_ref=chips_ref.at[me], dst_ref=chips_ref.at[me], send_sem=send_sems.at[1 + j],
                recv_sem=recv_sems.at[1 + j], device_id=(px, py, c), device_id_type=MESH)
            cp.start()
            copies.append(cp)
        for j, (px, py, pk) in enumerate(_other_chips(x, y)):
            pltpu.make_async_remote_copy(
                src_ref=chips_ref.at[pk], dst_ref=chips_ref.at[pk], send_sem=send_sems.at[1 + j],
                recv_sem=recv_sems.at[1 + j], device_id=(px, py, c), device_id_type=MESH).wait_recv()
        tot = chips_ref[0]
        for k in range(1, N_CHIPS):
            tot = tot + chips_ref[k]
        o_ref[mine] = tot
        share = pltpu.make_async_remote_copy(
            src_ref=o_ref.at[mine], dst_ref=o_ref.at[mine], send_sem=send_sems.at[4], recv_sem=recv_sems.at[4],
            device_id=sibling, device_id_type=MESH)
        share.start()
        pltpu.make_async_remote_copy(
            src_ref=o_ref.at[theirs], dst_ref=o_ref.at[theirs], send_sem=send_sems.at[4], recv_sem=recv_sems.at[4],
            device_id=sibling, device_id_type=MESH).wait_recv()
        swap.wait_send()
        for cp in copies:
            cp.wait_send()
        share.wait_send()

    vm = pl.BlockSpec(memory_space=pltpu.VMEM)
    return pl.pallas_call(
        body, name="small_all_reduce", in_specs=[vm] + ([] if after is None else [ANY]), out_specs=vm,
        out_shape=jax.ShapeDtypeStruct((rows, lanes), F32),
        scratch_shapes=[pltpu.VMEM((hr, lanes), F32), pltpu.VMEM((N_CHIPS, hr, lanes), F32),
                        pltpu.SemaphoreType.DMA((5,)), pltpu.SemaphoreType.DMA((5,))],
        compiler_params=pltpu.CompilerParams(has_side_effects=True, vmem_limit_bytes=48 * MIB),
    )(vec, *([] if after is None else [after]))


def _adamw(name, w, g, m, v, place, l=0, half=None, done=None, after=None):
    layers, rows, cols = w.shape
    span = rows if half is None else rows // 2
    tr = span
    for cand in (256, 128, 64, 32, 16, 8):
        if span % cand == 0:
            tr = cand
            break
    per = span // tr
    c1 = 1.0 - ADAM_B1 ** ADAM_STEP
    c2 = 1.0 - ADAM_B2 ** ADAM_STEP

    def first_block(p):
        return 0 if half is None else (p[1] if half == "own" else 1 - p[1]) * per

    def body(p_ref, w_ref, g_ref, m_ref, v_ref, *refs):
        go_ref, d_ref, nm_ref, nv_ref = refs[-4:]
        gv = g_ref[...]
        nm = ADAM_B1 * m_ref[...] + (1.0 - ADAM_B1) * gv
        nv = ADAM_B2 * v_ref[...] + (1.0 - ADAM_B2) * (gv * gv)
        go_ref[...] = gv
        nm_ref[...] = nm
        nv_ref[...] = nv
        d_ref[...] = -ADAM_LR * ((nm / c1) / (jnp.sqrt(nv / c2) + ADAM_EPS) + ADAM_WD * w_ref[...])

    blk = pl.BlockSpec((None, tr, cols), lambda i, p: (l, first_block(p) + i, 0))
    out = jax.ShapeDtypeStruct((layers, rows, cols), F32)
    extra = ([] if done is None else list(done)) + ([] if after is None else [after])
    aliases = {} if done is None else {5 + i: i for i in range(4)}
    return pl.pallas_call(
        body, name=name,
        grid_spec=pltpu.PrefetchScalarGridSpec(
            num_scalar_prefetch=1, grid=(per,), in_specs=[blk] * 4 + [ANY] * len(extra), out_specs=[blk] * 4),
        out_shape=[out] * 4, input_output_aliases=aliases,
        compiler_params=_params(("parallel",)),
    )(place, w, g, m, v, *extra)


LANES = 128
SUBLANES = 8
SMALL_SHAPES = {
    "norm_g": (DEPTH, D_MODEL), "sgu_ln_g": (DEPTH, D_A), "sgu_ln_b": (DEPTH, D_A),
    "sgu_w": (DEPTH, A_GROUPS, CHUNK, CHUNK), "sgu_b": (DEPTH, A_GROUPS, CHUNK), "mem_norm_g": (DEPTH, D_MODEL),
    "q_norm_g": (DEPTH, HEAD_DIM), "k_norm_g": (DEPTH, HEAD_DIM)}


def _small_layout():
    at, off = {}, 0
    for k in SMALL_NAMES:
        n = math.prod(SMALL_SHAPES[k]) // LANES
        at[k] = (off, n)
        off += -(-n // SUBLANES) * SUBLANES
    return at, off, -(-(off + SUBLANES) // (2 * SUBLANES)) * 2 * SUBLANES


def _pack_small(parts, loss=None):
    at, loss_row, rows = _small_layout()
    pieces = []
    for k in SMALL_NAMES:
        n = at[k][1]
        pieces.append(jnp.pad(parts[k].reshape(n, LANES), ((0, -(-n // SUBLANES) * SUBLANES - n), (0, 0))))
    tile = jnp.zeros((SUBLANES, LANES), F32) if loss is None else jnp.broadcast_to(loss.reshape(1, 1), (SUBLANES, LANES))
    pieces += [tile, jnp.zeros((rows - loss_row - SUBLANES, LANES), F32)]
    return jnp.concatenate(pieces)


def _adamw_small(w, g, m, v):
    at, _, rows = _small_layout()
    c1 = 1.0 - ADAM_B1 ** ADAM_STEP
    c2 = 1.0 - ADAM_B2 ** ADAM_STEP
    n_names = len(SMALL_NAMES)

    def body(w_ref, g_ref, m_ref, v_ref, *refs):
        outs, (d_ref, nm_ref, nv_ref) = refs[:4 * n_names], refs[4 * n_names:]
        gv = g_ref[...]
        nm = ADAM_B1 * m_ref[...] + (1.0 - ADAM_B1) * gv
        nv = ADAM_B2 * v_ref[...] + (1.0 - ADAM_B2) * (gv * gv)
        nm_ref[...] = nm
        nv_ref[...] = nv
        d_ref[...] = -ADAM_LR * ((nm / c1) / (jnp.sqrt(nv / c2) + ADAM_EPS) + ADAM_WD * w_ref[...])
        for kind, src in enumerate((g_ref, d_ref, nm_ref, nv_ref)):
            for i, k in enumerate(SMALL_NAMES):
                o_ref = outs[kind * n_names + i]
                first, n = at[k]
                shape = SMALL_SHAPES[k]
                if shape[-1] == LANES:
                    o_ref[...] = src[pl.ds(first, n), :].reshape(shape)
                else:
                    per = shape[-1] // LANES
                    for r in range(n):
                        o_ref[pl.ds(r // per, 1), pl.ds((r % per) * LANES, LANES)] = src[pl.ds(first + r, 1), :]

    out_shape = [jax.ShapeDtypeStruct(SMALL_SHAPES[k], F32) for _ in range(4) for k in SMALL_NAMES]
    outs = pl.pallas_call(
        body, name="adamw_small", out_shape=out_shape,
        scratch_shapes=[pltpu.VMEM((rows, LANES), F32)] * 3, compiler_params=_params(None),
    )(w, g, m, v)
    return [dict(zip(SMALL_NAMES, outs[kind * n_names:(kind + 1) * n_names])) for kind in range(4)]


WEIGHT_ORDER = ("norm_g", "w_in", "sgu_ln_g", "sgu_ln_b", "sgu_w", "sgu_b", "mem_norm_g", "w_mem_kv", "q_norm_g",
                "k_norm_g", "w_out")


def kernel(x, mem, norm_g, w_in, sgu_ln_g, sgu_ln_b, sgu_w, sgu_b, mem_norm_g, w_mem_kv, q_norm_g, k_norm_g, w_out, loss_target, m_norm_g, m_w_in, m_sgu_ln_g, m_sgu_ln_b, m_sgu_w, m_sgu_b, m_mem_norm_g, m_w_mem_kv, m_q_norm_g, m_k_norm_g, m_w_out, v_norm_g, v_w_in, v_sgu_ln_g, v_sgu_ln_b, v_sgu_w, v_sgu_b, v_mem_norm_g, v_w_mem_kv, v_q_norm_g, v_k_norm_g, v_w_out):
    weights = dict(norm_g=norm_g, w_in=w_in, sgu_ln_g=sgu_ln_g, sgu_ln_b=sgu_ln_b, sgu_w=sgu_w, sgu_b=sgu_b,
                   mem_norm_g=mem_norm_g, w_mem_kv=w_mem_kv, q_norm_g=q_norm_g, k_norm_g=k_norm_g, w_out=w_out)
    mom_m = dict(norm_g=m_norm_g, w_in=m_w_in, sgu_ln_g=m_sgu_ln_g, sgu_ln_b=m_sgu_ln_b, sgu_w=m_sgu_w, sgu_b=m_sgu_b,
                 mem_norm_g=m_mem_norm_g, w_mem_kv=m_w_mem_kv, q_norm_g=m_q_norm_g, k_norm_g=m_k_norm_g, w_out=m_w_out)
    mom_v = dict(norm_g=v_norm_g, w_in=v_w_in, sgu_ln_g=v_sgu_ln_g, sgu_ln_b=v_sgu_ln_b, sgu_w=v_sgu_w, sgu_b=v_sgu_b,
                 mem_norm_g=v_mem_norm_g, w_mem_kv=v_w_mem_kv, q_norm_g=v_q_norm_g, k_norm_g=v_k_norm_g, w_out=v_w_out)
    big = ("w_in", "w_mem_kv", "w_out")
    sm = {k: weights[k] for k in SMALL_NAMES}

    place = _place_index()
    xs, mems, target = x[0], mem[0], loss_target[0]

    slots = [[_cast_into_slot(f"cast_{k}_{l}", weights[k], l, place) for k in big] for l in range(DEPTH)]
    saved = [None] * DEPTH

    chips, cores = {}, {}

    def start_gather(l, after=None):
        chips[l, "in"] = _gather_start(f"gather_start_{l}_in", slots[l][:1], after)
        chips[l, "rest"] = _gather_start(f"gather_start_{l}_rest", slots[l][1:], chips[l, "in"][3])
        return chips[l, "rest"][3]

    def hand_to_sibling(l, group, after):
        send_sems, recv_sems, bufs, _ = chips[l, group]
        bufs = _gather_wait(f"gather_wait_{l}_{group}", send_sems, recv_sems, bufs, after)
        cores[l, group] = _gather_forward_start(f"gather_forward_{l}_{group}", bufs)
        return cores[l, group][3]

    def whole(l, group, after):
        send_sems, recv_sems, bufs, _ = cores[l, group]
        return _gather_wait(f"gather_whole_{l}_{group}", send_sems, recv_sems, bufs, after)

    class Gathered:
        def __init__(self, l):
            self.l = l

        def w_in(self, h):
            return whole(self.l, "in", h)[0]

        def rest_start(self, proj):
            token = hand_to_sibling(self.l, "rest", proj)
            return start_gather(self.l + 1, token) if self.l + 1 < DEPTH else token

        def rest_finish(self, o_b):
            w_kv_all, w_out_all = whole(self.l, "rest", o_b)
            return w_kv_all, w_out_all, None

        def before_out(self, y):
            return hand_to_sibling(self.l + 1, "in", y) if self.l + 1 < DEPTH else None

    hand_to_sibling(0, "in", [start_gather(0)] + [s for layer in slots[1:] for s in layer])
    cur = xs
    for l in range(DEPTH):
        cur, saved[l] = _layer_fwd(l, cur, mems, sm, Gathered(l))
    dxo, dxo_b, loss_part = _loss_and_grad("loss", cur, target, min(256, xs.shape[0]))

    small_g = [None] * DEPTH
    flight = {}

    class Exchange:
        def __init__(self):
            self.cores = {}

        def start(self, l, group, gives):
            *self.cores[l, group], token = _core_exchange_start(f"grad_core_start_{l}_{group}", gives)
            return token

        def landed(self, l, group, after):
            send_sems, recv_sems, bufs = self.cores[l, group]
            return _core_exchange_wait(f"grad_core_wait_{l}_{group}", send_sems, recv_sems, bufs, after)[1]

        def send(self, l, group, parts):
            *flight[l, group], token = _chip_exchange_start(f"grad_chip_start_{l}_{group}", parts)
            return token

    exchange = Exchange()
    for l in reversed(range(DEPTH)):
        dxo, dxo_b, small_g[l] = _layer_bwd(l, dxo, dxo_b, mems, sm, saved[l], place, exchange)
    grad_x = dxo

    groups = (("out", ("w_out",)), ("in", ("w_in", "w_mem_kv")))
    halves, stepped = dict.fromkeys(big), dict.fromkeys(big)
    small_g = {k: jnp.stack([small_g[l][k] for l in range(DEPTH)]) for k in SMALL_NAMES}
    after = grad_x
    sharing = {}

    def reduce_group(l, group, names):
        nonlocal after
        send_sems, recv_sems, bufs = flight[l, group]
        bufs = _chip_exchange_wait(f"grad_chip_wait_{l}_{group}", send_sems, recv_sems, bufs, after)
        for t, k in enumerate(names):
            halves[k] = _sum_chips(f"grad_chip_sum_{l}_{k}", bufs[t], bufs[len(names) + t], place, l, halves[k])
        *sharing[l, group], after = _core_share_start(f"grad_core_share_{l}_{group}", [halves[k] for k in names], l)

    def step(l, k, buf, half):
        nonlocal after
        tag = "" if half is None else "_" + half
        stepped[k] = _adamw(f"adamw_{k}_{l}{tag}", weights[k], buf, mom_m[k], mom_v[k], place, l, half, stepped[k],
                            after)
        after = stepped[k][1]

    def step_group(l, group, names, overlap):
        nonlocal after
        send_sems, recv_sems, bufs = sharing[l, group]
        if overlap:
            for k, buf in zip(names, bufs):
                step(l, k, buf, "own")
        bufs = _core_share_wait(f"grad_core_shared_{l}_{group}", send_sems, recv_sems, bufs, l, after)
        for k, buf in zip(names, bufs):
            halves[k] = buf
            step(l, k, buf, "other" if overlap else None)

    for l in reversed(range(DEPTH)):
        last = l == 0
        (g_out, n_out), (g_in, n_in) = groups
        reduce_group(l, g_out, n_out)
        if last:
            step_group(l, g_out, n_out, False)
            small_sum = _all_reduce_small(_pack_small(small_g, loss_part), after)
            small_step = _adamw_small(_pack_small(sm), small_sum, _pack_small({k: mom_m[k] for k in SMALL_NAMES}),
                                      _pack_small({k: mom_v[k] for k in SMALL_NAMES}))
            after = small_step[1]["sgu_w"]
        reduce_group(l, g_in, n_in)
        if not last:
            step_group(l, g_out, n_out, False)
        step_group(l, g_in, n_in, last)

    grads, delta, new_m, new_v = ({k: stepped[k][i] for k in big} for i in range(4))
    for out, small in zip((grads, delta, new_m, new_v), small_step):
        out.update(small)
    loss = small_sum[_small_layout()[1], 0]
    return (loss, grad_x[None], *[grads[k] for k in WEIGHT_ORDER], *[delta[k] for k in WEIGHT_ORDER],
            *[new_m[k] for k in WEIGHT_ORDER], *[new_v[k] for k in WEIGHT_ORDER])
```

```python
import functools
import math

import jax
import jax.numpy as jnp
from jax import lax
from jax.experimental import pallas as pl
from jax.experimental.pallas import tpu as pltpu

F32 = jnp.float32
BF16 = jnp.bfloat16
MESH = pl.DeviceIdType.MESH

D_MODEL = 2048
DEPTH = 2
CHUNK = 128
D_A = 1024
A_GROUPS = 8
D_B = 512
D_C = 512
HEADS = 4
HEAD_DIM = 128
IN_WIDTH = 6144
N_CHIPS = 4
EPS = 1e-6
ATT_SCALE = 1.0 / math.sqrt(HEAD_DIM)

OFF_U, OFF_V, OFF_ZA = 0, 1024, 2048
OFF_QB, OFF_KB, OFF_VB, OFF_ZB = 3072, 3584, 4096, 4608
OFF_QC, OFF_ZC = 5120, 5632
OFF_YB, OFF_YC = 1024, 1536

ADAM_LR = 0.001
ADAM_B1 = 0.9
ADAM_B2 = 0.999
ADAM_EPS = 1e-08
ADAM_WD = 0.01
ADAM_STEP = 10

MIB = 1024 * 1024
ANY = pl.BlockSpec(memory_space=pl.ANY)


def _params(semantics=None, vmem_mb=48):
    return pltpu.CompilerParams(dimension_semantics=semantics, vmem_limit_bytes=vmem_mb * MIB)


def _gelu(x):
    return 0.5 * x * (1.0 + lax.erf(x * (1.0 / math.sqrt(2.0))))


def _gelu_grad(x):
    cdf = 0.5 * (1.0 + lax.erf(x * (1.0 / math.sqrt(2.0))))
    pdf = jnp.exp(-0.5 * x * x) * (1.0 / math.sqrt(2.0 * math.pi))
    return cdf + x * pdf


def _sigmoid(x):
    return 1.0 / (1.0 + jnp.exp(-x))


def _silu_and_grad(z):
    s = _sigmoid(z)
    return z * s, s * (1.0 + z * (1.0 - s))


def _split_bf16(x):
    hi = x.astype(BF16)
    lo = (x - hi.astype(F32)).astype(BF16)
    return hi, lo


def _dot(a, b, dims):
    return lax.dot_general(a, b, (dims, ((), ())), preferred_element_type=F32)


NN = ((1,), (0,))
NT = ((1,), (1,))
TN = ((0,), (0,))


def _matmul(name, a, b, *, grid, a_spec, b_spec, o_spec, out_shape, dims, res=None, res_spec=None, after=None,
            place=None, into=None, vmem_mb=48):
    nk = grid[2]
    n_in = 2 + (res is not None) + (after is not None) + (into is not None)

    def body(*refs):
        if place is not None:
            refs = refs[1:]
        a_ref, b_ref = refs[0], refs[1]
        r_ref = refs[2] if res is not None else None
        o_ref = refs[n_in]
        if len(b_ref.shape) == 3 and dims == NN:
            part = _dot(a_ref[...], b_ref[...].reshape(-1, b_ref.shape[-1]), dims)
        elif len(b_ref.shape) == 3:
            width = b_ref.shape[-1]
            part = None
            for s in range(b_ref.shape[0]):
                term = _dot(a_ref[:, s * width:(s + 1) * width], b_ref[s], dims)
                part = term if part is None else part + term
        else:
            part = _dot(a_ref[...], b_ref[...], dims)
        if nk == 1:
            if r_ref is not None:
                part = part + r_ref[...]
            o_ref[...] = part.astype(o_ref.dtype)
            return
        acc_ref = refs[n_in + 1]
        k = pl.program_id(2)

        @pl.when(k == 0)
        def _():
            acc_ref[...] = part

        @pl.when(k > 0)
        def _():
            acc_ref[...] += part

        @pl.when(k == nk - 1)
        def _():
            tot = acc_ref[...]
            if r_ref is not None:
                tot = tot + r_ref[...]
            o_ref[...] = tot.astype(o_ref.dtype)

    in_specs = [a_spec, b_spec]
    args = [a, b]
    if res is not None:
        in_specs.append(res_spec)
        args.append(res)
    if after is not None:
        in_specs.append(ANY)
        args.append(after)
    aliases = {}
    if into is not None:
        in_specs.append(ANY)
        args.append(into)
        aliases = {len(args) - 1 + (place is not None): 0}
    acc_shape = tuple(d for d in o_spec.block_shape if d is not None)
    scratch = [pltpu.VMEM(acc_shape, F32)] if nk > 1 else []
    params = _params(("parallel", "parallel", "arbitrary"), vmem_mb)
    if place is not None:
        return pl.pallas_call(
            body, name=name, out_shape=out_shape, compiler_params=params, input_output_aliases=aliases,
            grid_spec=pltpu.PrefetchScalarGridSpec(num_scalar_prefetch=1, grid=grid, in_specs=in_specs,
                                                   out_specs=o_spec, scratch_shapes=scratch),
        )(place, *args)
    return pl.pallas_call(
        body, name=name, grid=grid, in_specs=in_specs, out_specs=o_spec, out_shape=out_shape,
        scratch_shapes=scratch, compiler_params=params, input_output_aliases=aliases,
    )(*args)


def _rms_fwd(name, x, g, tr, after=None, transposed=False):
    rows, d = x.shape

    def body(x_ref, g_ref, *refs):
        outs = refs[1:] if after is not None else refs
        xv = x_ref[...]
        r = lax.rsqrt(jnp.mean(xv * xv, axis=-1, keepdims=True) + EPS)
        h = xv * r * g_ref[...]
        outs[0][...] = h.astype(BF16)
        if transposed:
            outs[1][...] = h.T.astype(BF16)

    out_specs = [pl.BlockSpec((tr, d), lambda i: (i, 0))]
    out_shape = [jax.ShapeDtypeStruct((rows, d), BF16)]
    if transposed:
        out_specs.append(pl.BlockSpec((d, tr), lambda i: (0, i)))
        out_shape.append(jax.ShapeDtypeStruct((d, rows), BF16))
    outs = pl.pallas_call(
        body, name=name, grid=(rows // tr,),
        in_specs=[pl.BlockSpec((tr, d), lambda i: (i, 0)), pl.BlockSpec((1, d), lambda i: (0, 0))]
        + ([] if after is None else [ANY]),
        out_specs=out_specs, out_shape=out_shape,
        compiler_params=_params(("parallel",)),
    )(x, g, *([] if after is None else [after]))
    return outs if transposed else outs[0]


def _rms_bwd(name, x, dh, dres, g, tr, after=None):
    rows, d = x.shape

    def body(x_ref, dh_ref, dres_ref, g_ref, *refs):
        dx_ref, dxb_ref, dg_ref = refs[-3:]
        xv = x_ref[...]
        r = lax.rsqrt(jnp.mean(xv * xv, axis=-1, keepdims=True) + EPS)
        xhat = xv * r
        dhv = dh_ref[...]
        dxh = dhv * g_ref[...]
        dx = r * (dxh - xhat * jnp.mean(dxh * xhat, axis=-1, keepdims=True)) + dres_ref[...]
        dx_ref[...] = dx
        dxb_ref[...] = dx.astype(BF16)
        part = jnp.sum(dhv * xhat, axis=0, keepdims=True)

        @pl.when(pl.program_id(0) == 0)
        def _():
            dg_ref[...] = part

        @pl.when(pl.program_id(0) > 0)
        def _():
            dg_ref[...] += part

    blk = pl.BlockSpec((tr, d), lambda i: (i, 0))
    vec = pl.BlockSpec((1, d), lambda i: (0, 0))
    return pl.pallas_call(
        body, name=name, grid=(rows // tr,), in_specs=[blk, blk, blk, vec] + ([] if after is None else [ANY]),
        out_specs=[blk, blk, vec],
        out_shape=[jax.ShapeDtypeStruct((rows, d), F32), jax.ShapeDtypeStruct((rows, d), BF16),
                   jax.ShapeDtypeStruct((1, d), F32)],
        compiler_params=_params(("arbitrary",)),
    )(x, dh, dres, g, *([] if after is None else [after]))


def _rms_gain_grad(name, x, dh):
    rows, d = x.shape

    def body(x_ref, dh_ref, dg_ref):
        xv = x_ref[...]
        r = lax.rsqrt(jnp.mean(xv * xv, axis=-1, keepdims=True) + EPS)
        dg_ref[...] = jnp.sum(dh_ref[...] * xv * r, axis=0, keepdims=True)

    return pl.pallas_call(
        body, name=name, out_shape=jax.ShapeDtypeStruct((1, d), F32), compiler_params=_params(None),
    )(x, dh)


def _loss_and_grad(name, y, target, tr):
    rows, d = y.shape
    n = rows // tr

    def body(y_ref, t_ref, dx_ref, dxb_ref, loss_ref, acc_ref):
        e = y_ref[...] - t_ref[...]
        dx = e * (1.0 / d)
        dx_ref[...] = dx
        dxb_ref[...] = dx.astype(BF16)
        part = jnp.sum(e * e, axis=0, keepdims=True)
        i = pl.program_id(0)

        @pl.when(i == 0)
        def _():
            acc_ref[...] = part

        @pl.when(i > 0)
        def _():
            acc_ref[...] += part

        @pl.when(i == n - 1)
        def _():
            loss_ref[...] = jnp.sum(acc_ref[...], axis=-1, keepdims=True) * (0.5 / d)

    blk = pl.BlockSpec((tr, d), lambda i: (i, 0))
    return pl.pallas_call(
        body, name=name, grid=(n,), in_specs=[blk, blk],
        out_specs=[blk, blk, pl.BlockSpec((1, 1), lambda i: (0, 0))],
        out_shape=[jax.ShapeDtypeStruct((rows, d), F32), jax.ShapeDtypeStruct((rows, d), BF16),
                   jax.ShapeDtypeStruct((1, 1), F32)],
        scratch_shapes=[pltpu.VMEM((1, d), F32)],
        compiler_params=_params(("arbitrary",)),
    )(y, target)


SB_T = 256
SB_HEADS = 4


LOG2E = 1.4426950408889634


def _sb_scores(q, kblk):
    z2 = _dot(q, kblk, NT) * (ATT_SCALE * LOG2E)
    e = jnp.exp2(-jnp.abs(z2))
    l1 = jnp.minimum(-z2, 0.0) - jnp.log2(1.0 + e)
    lb = l1 + z2
    return z2, e, lb, l1


def _sb_fwd(name, proj, after=None):
    s_len = proj.shape[0]
    t = SB_T
    nq = s_len // t

    def body(q_ref, k_ref, v_ref, *refs):
        o_ref = refs[-1]
        i = pl.program_id(1)
        row = lax.broadcasted_iota(jnp.int32, (t, t), 0)
        col = lax.broadcasted_iota(jnp.int32, (t, t), 1)
        causal = col < row
        after_mat = (row > col).astype(BF16)
        heads = [slice(hh * HEAD_DIM, (hh + 1) * HEAD_DIM) for hh in range(SB_HEADS)]
        q = [q_ref[:, sl].astype(BF16) for sl in heads]

        def tile(kb, state, masked):
            start = pl.multiple_of(kb * t, t)
            out = []
            for hh, sl in enumerate(heads):
                carry, acc = state[hh]
                kblk = k_ref[pl.ds(start, t), sl].astype(BF16)
                vblk = v_ref[pl.ds(start, t), sl].astype(BF16)
                _, _, lb, l1 = _sb_scores(q[hh], kblk)
                if masked:
                    l1 = jnp.where(causal, l1, 0.0)
                hi, lo = _split_bf16(l1)
                after = _dot(hi, after_mat, NN) + _dot(lo, after_mat, NN) + carry
                a = jnp.exp2(lb + after)
                if masked:
                    a = jnp.where(causal, a, 0.0)
                acc = acc + _dot(a.astype(BF16), vblk, NN)
                carry = carry + jnp.sum(l1, axis=-1, keepdims=True)
                out.append((carry, acc))
            return tuple(out)

        zero = (jnp.zeros((t, 1), F32), jnp.zeros((t, HEAD_DIM), F32))
        state = tile(i, (zero,) * SB_HEADS, True)
        state = lax.fori_loop(0, i, lambda n, st: tile(i - 1 - n, st, False), state)
        for hh, sl in enumerate(heads):
            o_ref[:, sl] = state[hh][1]

    cb = SB_HEADS * HEAD_DIM
    return pl.pallas_call(
        body, name=name, grid=(HEADS // SB_HEADS, nq),
        in_specs=[pl.BlockSpec((t, cb), lambda h, i: (i, OFF_QB // cb + h)),
                  pl.BlockSpec((s_len, cb), lambda h, i: (0, OFF_KB // cb + h)),
                  pl.BlockSpec((s_len, cb), lambda h, i: (0, OFF_VB // cb + h))] + ([] if after is None else [ANY]),
        out_specs=pl.BlockSpec((t, cb), lambda h, i: (i, h)),
        out_shape=jax.ShapeDtypeStruct((s_len, D_B), F32),
        compiler_params=_params(("parallel", "arbitrary")),
    )(proj, proj, proj, *([] if after is None else [after]))


def _sb_bwd(name, proj, dy, after=None):
    s_len = proj.shape[0]
    t = SB_T
    nq = s_len // t

    def body(q_ref, k_ref, v_ref, z_ref, dy_ref, *refs):
        dq_ref, dk_ref, dv_ref, a_ref, s_ref = refs[-5:]
        i = pl.program_id(1)

        @pl.when(i == 0)
        def _():
            dk_ref[...] = jnp.zeros_like(dk_ref)
            dv_ref[...] = jnp.zeros_like(dv_ref)

        heads = [slice(hh * HEAD_DIM, (hh + 1) * HEAD_DIM) for hh in range(SB_HEADS)]
        q = [q_ref[:, sl].astype(BF16) for sl in heads]
        silu_z, _ = _silu_and_grad(z_ref[...])
        do_all = dy_ref[...] * silu_z
        do_b = [do_all[:, sl].astype(BF16) for sl in heads]
        row = lax.broadcasted_iota(jnp.int32, (t, t), 0)
        col = lax.broadcasted_iota(jnp.int32, (t, t), 1)
        causal = col < row
        after_mat = (row > col).astype(BF16)
        before_mat = (row < col).astype(BF16)

        def weights(kb, carries, masked):
            start = pl.multiple_of(kb * t, t)
            out = []
            for hh, sl in enumerate(heads):
                kblk = k_ref[pl.ds(start, t), sl].astype(BF16)
                z, _, lb, l1 = _sb_scores(q[hh], kblk)
                if masked:
                    l1 = jnp.where(causal, l1, 0.0)
                hi, lo = _split_bf16(l1)
                after = _dot(hi, after_mat, NN) + _dot(lo, after_mat, NN) + carries[hh]
                a = jnp.exp2(lb + after)
                if masked:
                    a = jnp.where(causal, a, 0.0)
                a_ref[hh, kb] = a
                s_ref[hh, kb] = z
                out.append(carries[hh] + jnp.sum(l1, axis=-1, keepdims=True))
            return tuple(out)

        carries = weights(i, (jnp.zeros((t, 1), F32),) * SB_HEADS, True)
        lax.fori_loop(0, i, lambda n, c: weights(i - 1 - n, c, False), carries)

        def grads(kb, state, masked):
            start = pl.multiple_of(kb * t, t)
            out = []
            for hh, sl in enumerate(heads):
                carry, dq = state[hh]
                kblk = k_ref[pl.ds(start, t), sl].astype(BF16)
                vblk = v_ref[pl.ds(start, t), sl].astype(BF16)
                a = a_ref[hh, kb]
                z = s_ref[hh, kb]
                g = _dot(do_b[hh], vblk, NT) * a
                ghi, glo = _split_bf16(g)
                prefix = _dot(ghi, before_mat, NN) + _dot(glo, before_mat, NN) + carry
                e = jnp.exp2(-jnp.abs(z))
                inv = 1.0 / (1.0 + e)
                pos = z >= 0.0
                beta = jnp.where(pos, inv, e * inv)
                one_m_beta = jnp.where(pos, e * inv, inv)
                dz = (g * one_m_beta - prefix * beta) * ATT_SCALE
                if masked:
                    dz = jnp.where(causal, dz, 0.0)
                dz_b = dz.astype(BF16)
                dq = dq + _dot(dz_b, kblk, NN)
                dk_ref[pl.ds(start, t), sl] += _dot(dz_b, q[hh], TN)
                dv_ref[pl.ds(start, t), sl] += _dot(a.astype(BF16), do_b[hh], TN)
                out.append((carry + jnp.sum(g, axis=-1, keepdims=True), dq))
            return tuple(out)

        zero = (jnp.zeros((t, 1), F32), jnp.zeros((t, HEAD_DIM), F32))
        state = lax.fori_loop(0, i, lambda kb, st: grads(kb, st, False), (zero,) * SB_HEADS)
        state = grads(i, state, True)
        for hh, sl in enumerate(heads):
            dq_ref[:, sl] = state[hh][1]

    cb = SB_HEADS * HEAD_DIM
    qblk = lambda off: pl.BlockSpec((t, cb), lambda h, i: (i, off // cb + h))
    full = lambda off: pl.BlockSpec((s_len, cb), lambda h, i: (0, off // cb + h))
    out = jax.ShapeDtypeStruct((s_len, D_B), F32)
    return pl.pallas_call(
        body, name=name, grid=(HEADS // SB_HEADS, nq),
        in_specs=[qblk(OFF_QB), full(OFF_KB), full(OFF_VB), qblk(OFF_ZB), qblk(OFF_YB)]
        + ([] if after is None else [ANY]),
        out_specs=[qblk(0), full(0), full(0)],
        out_shape=[out, out, out],
        scratch_shapes=[pltpu.VMEM((SB_HEADS, nq, t, t), F32), pltpu.VMEM((SB_HEADS, nq, t, t), F32)],
        compiler_params=_params(("parallel", "arbitrary")),
    )(proj, proj, proj, proj, dy, *([] if after is None else [after]))


MEM_TQ = 512


def _qk_norm(x, g):
    r = lax.rsqrt(jnp.mean(x * x, axis=-1, keepdims=True) + EPS)
    xhat = x * r
    return xhat * g, xhat, r


def _qk_norm_bwd(dn, g, xhat, r):
    dxh = dn * g
    return r * (dxh - xhat * jnp.mean(dxh * xhat, axis=-1, keepdims=True))


def _mem_probs(q, mk, qg, kg):
    qn, qhat, rq = _qk_norm(q, qg)
    kn, khat, rk = _qk_norm(mk, kg)
    qn_b, kn_b = qn.astype(BF16), kn.astype(BF16)
    s = _dot(qn_b, kn_b, NT) * ATT_SCALE
    p = jnp.exp(s - jnp.max(s, axis=-1, keepdims=True))
    p = p / jnp.sum(p, axis=-1, keepdims=True)
    return p, qn_b, kn_b, qhat, rq, khat, rk


def _mem_fwd(name, proj, mem_kv, qg, kg):
    s_len = proj.shape[0]
    m_len = mem_kv.shape[0]
    tq = min(MEM_TQ, s_len)

    def body(q_ref, mk_ref, mv_ref, qg_ref, kg_ref, o_ref):
        p = _mem_probs(q_ref[...], mk_ref[...], qg_ref[...], kg_ref[...])[0]
        o_ref[...] = _dot(p.astype(BF16), mv_ref[...].astype(BF16), NN)

    cb = HEAD_DIM
    vec = pl.BlockSpec((1, cb), lambda h, i: (0, 0))
    return pl.pallas_call(
        body, name=name, grid=(HEADS, s_len // tq),
        in_specs=[pl.BlockSpec((tq, cb), lambda h, i: (i, OFF_QC // cb + h)),
                  pl.BlockSpec((m_len, cb), lambda h, i: (0, h)),
                  pl.BlockSpec((m_len, cb), lambda h, i: (0, HEADS + h)), vec, vec],
        out_specs=pl.BlockSpec((tq, cb), lambda h, i: (i, h)),
        out_shape=jax.ShapeDtypeStruct((s_len, D_C), F32),
        compiler_params=_params(("parallel", "parallel")),
    )(proj, mem_kv, mem_kv, qg, kg)


def _mem_bwd(name, proj, mem_kv, qg, kg, dy):
    s_len = proj.shape[0]
    m_len = mem_kv.shape[0]
    tq = min(MEM_TQ, s_len)

    def body(q_ref, mk_ref, mv_ref, qg_ref, kg_ref, z_ref, dy_ref, dq_ref, dmk_ref, dmv_ref, dqg_ref, dkg_ref):
        h, i = pl.program_id(0), pl.program_id(1)

        @pl.when(i == 0)
        def _():
            dmk_ref[...] = jnp.zeros_like(dmk_ref)
            dmv_ref[...] = jnp.zeros_like(dmv_ref)

        @pl.when((i == 0) & (h == 0))
        def _():
            dqg_ref[...] = jnp.zeros_like(dqg_ref)
            dkg_ref[...] = jnp.zeros_like(dkg_ref)

        qg, kg = qg_ref[...], kg_ref[...]
        p, qn_b, kn_b, qhat, rq, khat, rk = _mem_probs(q_ref[...], mk_ref[...], qg, kg)
        silu_z, _ = _silu_and_grad(z_ref[...])
        do_b = (dy_ref[...] * silu_z).astype(BF16)
        dmv_ref[...] += _dot(p.astype(BF16), do_b, TN)
        dp = _dot(do_b, mv_ref[...].astype(BF16), NT)
        ds = (p * (dp - jnp.sum(dp * p, axis=-1, keepdims=True)) * ATT_SCALE).astype(BF16)
        dqn = _dot(ds, kn_b, NN)
        dkn = _dot(ds, qn_b, TN)
        dq_ref[...] = _qk_norm_bwd(dqn, qg, qhat, rq)
        dmk_ref[...] += _qk_norm_bwd(dkn, kg, khat, rk)
        dqg_ref[...] += jnp.sum(dqn * qhat, axis=0, keepdims=True)
        dkg_ref[...] += jnp.sum(dkn * khat, axis=0, keepdims=True)

    cb = HEAD_DIM
    vec = pl.BlockSpec((1, cb), lambda h, i: (0, 0))
    qblk = lambda off: pl.BlockSpec((tq, cb), lambda h, i: (i, off // cb + h))
    memblk = lambda off: pl.BlockSpec((m_len, cb), lambda h, i: (0, off + h))
    return pl.pallas_call(
        body, name=name, grid=(HEADS, s_len // tq),
        in_specs=[qblk(OFF_QC), memblk(0), memblk(HEADS), vec, vec, qblk(OFF_ZC), qblk(OFF_YC)],
        out_specs=[qblk(0), memblk(0), memblk(0), vec, vec],
        out_shape=[jax.ShapeDtypeStruct((s_len, D_C), F32), jax.ShapeDtypeStruct((m_len, D_C), F32),
                   jax.ShapeDtypeStruct((m_len, D_C), F32), jax.ShapeDtypeStruct((1, cb), F32),
                   jax.ShapeDtypeStruct((1, cb), F32)],
        compiler_params=_params(("arbitrary", "arbitrary")),
    )(proj, mem_kv, mem_kv, qg, kg, proj, dy)


def _sgu_common(u_ref, v_ref, lng_ref, lnb_ref, w_ref, bias_ref):
    ug = _gelu(u_ref[...])
    vg = _gelu(v_ref[...])
    mu = jnp.mean(vg, axis=-1, keepdims=True)
    xc = vg - mu
    rstd = lax.rsqrt(jnp.mean(xc * xc, axis=-1, keepdims=True) + EPS)
    xhat = xc * rstd
    vn = xhat * lng_ref[...] + lnb_ref[...]
    vn_b = vn.astype(BF16)
    row = lax.broadcasted_iota(jnp.int32, (CHUNK, CHUNK), 0)
    col = lax.broadcasted_iota(jnp.int32, (CHUNK, CHUNK), 1)
    tril = row >= col
    mixed = []
    for g in range(A_GROUPS):
        w = jnp.where(tril, w_ref[g], 0.0).astype(BF16)
        sl = slice(g * CHUNK, (g + 1) * CHUNK)
        mixed.append(_dot(w, vn_b[:, sl], NN) + bias_ref[:, sl])
    return ug, xhat, rstd, vn_b, mixed, tril


def _gate_fwd(name, proj, o_b, o_c, lng, lnb, w_s, bias):
    s_len = proj.shape[0]

    def body(u_ref, v_ref, za_ref, zb_ref, zc_ref, ob_ref, oc_ref, lng_ref, lnb_ref, w_ref, bias_ref, y_ref, yt_ref):
        ug, _, _, _, mixed, _ = _sgu_common(u_ref, v_ref, lng_ref, lnb_ref, w_ref, bias_ref)
        sza, _ = _silu_and_grad(za_ref[...])
        gate = ug * sza

        def put(off, width, val):
            y_ref[:, off:off + width] = val.astype(BF16)
            yt_ref[off:off + width, :] = val.T.astype(BF16)

        for g in range(A_GROUPS):
            sl = slice(g * CHUNK, (g + 1) * CHUNK)
            put(g * CHUNK, CHUNK, gate[:, sl] * mixed[g])
        szb, _ = _silu_and_grad(zb_ref[...])
        put(OFF_YB, D_B, ob_ref[...] * szb)
        szc, _ = _silu_and_grad(zc_ref[...])
        put(OFF_YC, D_C, oc_ref[...] * szc)

    wide = lambda off: pl.BlockSpec((CHUNK, D_A), lambda i: (i, off // D_A))
    narrow = lambda off: pl.BlockSpec((CHUNK, D_B), lambda i: (i, off // D_B))
    vec = pl.BlockSpec((1, D_A), lambda i: (0, 0))
    return pl.pallas_call(
        body, name=name, grid=(s_len // CHUNK,),
        in_specs=[wide(OFF_U), wide(OFF_V), wide(OFF_ZA), narrow(OFF_ZB), narrow(OFF_ZC), narrow(0), narrow(0), vec, vec,
                  pl.BlockSpec((A_GROUPS, CHUNK, CHUNK), lambda i: (0, 0, 0)),
                  pl.BlockSpec((CHUNK, D_A), lambda i: (0, 0))],
        out_specs=[pl.BlockSpec((CHUNK, D_MODEL), lambda i: (i, 0)), pl.BlockSpec((D_MODEL, CHUNK), lambda i: (0, i))],
        out_shape=[jax.ShapeDtypeStruct((s_len, D_MODEL), BF16), jax.ShapeDtypeStruct((D_MODEL, s_len), BF16)],
        compiler_params=_params(("parallel",)),
    )(proj, proj, proj, proj, proj, o_b, o_c, lng, lnb, w_s, bias)


def _gate_bwd(name, proj, dy, o_b, o_c, dqkv, dq_c, lng, lnb, w_s, w_s_t, bias):
    s_len = proj.shape[0]
    n = s_len // CHUNK
    dq_b, dk_b, dv_b = dqkv

    def body(u_ref, v_ref, za_ref, zb_ref, zc_ref, dya_ref, dyb_ref, dyc_ref, ob_ref, oc_ref, dq_ref, dk_ref, dv_ref,
             dqc_ref, lng_ref, lnb_ref, w_ref, wt_ref, bias_ref, dp_ref, dw_ref, dsb_ref, dlng_ref, dlnb_ref, dbias_ref):
        i = pl.program_id(0)

        @pl.when(i == 0)
        def _():
            dw_ref[...] = jnp.zeros_like(dw_ref)
            dbias_ref[...] = jnp.zeros_like(dbias_ref)
            dlng_ref[...] = jnp.zeros_like(dlng_ref)
            dlnb_ref[...] = jnp.zeros_like(dlnb_ref)

        ug, xhat, rstd, vn_b, mixed, tril = _sgu_common(u_ref, v_ref, lng_ref, lnb_ref, w_ref, bias_ref)
        za = za_ref[...]
        sza, dsza = _silu_and_grad(za)
        dya = dya_ref[...]
        mixed_all = jnp.concatenate(mixed, axis=-1)
        d_mixed = dya * ug * sza
        dp_ref[:, OFF_U:OFF_U + D_A] = (dya * mixed_all * sza * _gelu_grad(u_ref[...])).astype(BF16)
        dp_ref[:, OFF_ZA:OFF_ZA + D_A] = (dya * ug * mixed_all * dsza).astype(BF16)
        dbias_ref[...] += d_mixed
        dm_b = d_mixed.astype(BF16)
        triu = lax.broadcasted_iota(jnp.int32, (CHUNK, CHUNK), 0) <= lax.broadcasted_iota(jnp.int32, (CHUNK, CHUNK), 1)
        d_vn = []
        for g in range(A_GROUPS):
            sl = slice(g * CHUNK, (g + 1) * CHUNK)
            wt = jnp.where(triu, wt_ref[g], 0.0).astype(BF16)
            d_vn.append(_dot(wt, dm_b[:, sl], NN))
            dw_ref[g] += jnp.where(tril, _dot(dm_b[:, sl], vn_b[:, sl], NT), 0.0)
        d_vn = jnp.concatenate(d_vn, axis=-1)
        dlng_ref[...] += jnp.sum(d_vn * xhat, axis=0, keepdims=True)
        dlnb_ref[...] += jnp.sum(d_vn, axis=0, keepdims=True)
        dxh = d_vn * lng_ref[...]
        d_vg = rstd * (dxh - jnp.mean(dxh, axis=-1, keepdims=True)
                       - xhat * jnp.mean(dxh * xhat, axis=-1, keepdims=True))
        dp_ref[:, OFF_V:OFF_V + D_A] = (d_vg * _gelu_grad(v_ref[...])).astype(BF16)
        dp_ref[:, OFF_QB:OFF_QB + D_B] = dq_ref[...].astype(BF16)
        dp_ref[:, OFF_KB:OFF_KB + D_B] = dk_ref[...].astype(BF16)
        dp_ref[:, OFF_VB:OFF_VB + D_B] = dv_ref[...].astype(BF16)
        _, dszb = _silu_and_grad(zb_ref[...])
        dp_ref[:, OFF_ZB:OFF_ZB + D_B] = (dyb_ref[...] * ob_ref[...] * dszb).astype(BF16)
        dp_ref[:, OFF_QC:OFF_QC + D_C] = dqc_ref[...].astype(BF16)
        _, dszc = _silu_and_grad(zc_ref[...])
        dp_ref[:, OFF_ZC:OFF_ZC + D_C] = (dyc_ref[...] * oc_ref[...] * dszc).astype(BF16)

        @pl.when(i == n - 1)
        def _():
            ch = lax.broadcasted_iota(jnp.int32, (D_A, CHUNK), 0)
            gcol = lax.broadcasted_iota(jnp.int32, (D_A, CHUNK), 1)
            pick = (ch // (D_A // A_GROUPS) == gcol).astype(BF16)
            rest = dbias_ref[...]
            tot = jnp.zeros((CHUNK, CHUNK), F32)
            for _ in range(3):
                term = rest.astype(BF16)
                tot = tot + _dot(term, pick, NN)
                rest = rest - term.astype(F32)
            dsb_ref[...] = tot

    wide = lambda off: pl.BlockSpec((CHUNK, D_A), lambda i: (i, off // D_A))
    narrow = lambda off: pl.BlockSpec((CHUNK, D_B), lambda i: (i, off // D_B))
    vec = pl.BlockSpec((1, D_A), lambda i: (0, 0))
    wspec = pl.BlockSpec((A_GROUPS, CHUNK, CHUNK), lambda i: (0, 0, 0))
    bspec = pl.BlockSpec((CHUNK, D_A), lambda i: (0, 0))
    return pl.pallas_call(
        body, name=name, grid=(n,),
        in_specs=[wide(OFF_U), wide(OFF_V), wide(OFF_ZA), narrow(OFF_ZB), narrow(OFF_ZC),
                  wide(0), narrow(OFF_YB), narrow(OFF_YC), narrow(0), narrow(0), narrow(0), narrow(0), narrow(0),
                  narrow(0), vec, vec, wspec, wspec, bspec],
        out_specs=[pl.BlockSpec((CHUNK, IN_WIDTH), lambda i: (i, 0)), wspec,
                   pl.BlockSpec((CHUNK, CHUNK), lambda i: (0, 0)), vec, vec],
        out_shape=[jax.ShapeDtypeStruct((s_len, IN_WIDTH), BF16), jax.ShapeDtypeStruct((A_GROUPS, CHUNK, CHUNK), F32),
                   jax.ShapeDtypeStruct((CHUNK, CHUNK), F32), jax.ShapeDtypeStruct((1, D_A), F32),
                   jax.ShapeDtypeStruct((1, D_A), F32)],
        scratch_shapes=[pltpu.VMEM((CHUNK, D_A), F32)],
        compiler_params=_params(("arbitrary",)),
    )(proj, proj, proj, proj, proj, dy, dy, dy, o_b, o_c, dq_b, dk_b, dv_b, dq_c, lng, lnb, w_s, w_s_t, bias)


IN_SHARD = IN_WIDTH // N_CHIPS
ROW_SHARD = D_MODEL // N_CHIPS


def _bias_rows(sgu_b_l):
    return jnp.repeat(sgu_b_l.T, D_A // A_GROUPS, axis=1)


class _WholeWeights:
    def __init__(self, w_in_all, w_kv_all, w_out_all):
        self.weights = (w_in_all, w_kv_all, w_out_all)

    def w_in(self, stage, h, proj):
        return (self.weights[0], jnp.arange(N_CHIPS, dtype=jnp.int32), 0, N_CHIPS) if stage == 0 else None

    def rest_start(self, proj):
        return None

    def rest_finish(self, o_b):
        return self.weights[1], self.weights[2], None

    def before_out(self, y):
        return None


def _layer_fwd(l, x, mem, sm, hooks):
    s_len = x.shape[0]
    m_len = mem.shape[0]
    tm = min(1024, s_len)
    h, h_t = _rms_fwd(f"rms_fwd_{l}", x, sm["norm_g"][l][None], min(256, s_len), transposed=True)
    proj, stage = None, 0
    while (ready := hooks.w_in(stage, h, proj)) is not None:
        w_in_all, order, first, count = ready
        proj = _matmul(
            f"in_proj_{l}_{stage}", h, w_in_all, grid=(s_len // tm, count, 1), place=order, into=proj,
            a_spec=pl.BlockSpec((tm, D_MODEL), lambda i, j, k, p: (i, 0)),
            b_spec=pl.BlockSpec((None, D_MODEL, IN_SHARD), lambda i, j, k, p: (p[first + j], 0, 0)),
            o_spec=pl.BlockSpec((tm, IN_SHARD), lambda i, j, k, p: (i, p[first + j])),
            out_shape=jax.ShapeDtypeStruct((s_len, IN_WIDTH), F32), dims=NN)
        stage += 1
    o_b = _sb_fwd(f"sb_fwd_{l}", proj, hooks.rest_start(proj))
    w_kv_all, w_out_all, after = hooks.rest_finish(o_b)
    mem_h = _rms_fwd(f"mem_rms_fwd_{l}", mem, sm["mem_norm_g"][l][None], m_len, after)
    mem_kv = _matmul(
        f"mem_kv_{l}", mem_h, w_kv_all, grid=(1, 2, N_CHIPS),
        a_spec=pl.BlockSpec((m_len, ROW_SHARD), lambda i, j, k: (0, k)),
        b_spec=pl.BlockSpec((None, ROW_SHARD, D_C), lambda i, j, k: (k, 0, j)),
        o_spec=pl.BlockSpec((m_len, D_C), lambda i, j, k: (0, j)),
        out_shape=jax.ShapeDtypeStruct((m_len, 2 * D_C), F32), dims=NN)
    qg, kg = sm["q_norm_g"][l][None], sm["k_norm_g"][l][None]
    o_c = _mem_fwd(f"mem_fwd_{l}", proj, mem_kv, qg, kg)
    bias = _bias_rows(sm["sgu_b"][l])
    y, y_t = _gate_fwd(f"gate_fwd_{l}", proj, o_b, o_c, sm["sgu_ln_g"][l][None], sm["sgu_ln_b"][l][None],
                       sm["sgu_w"][l], bias)
    tn_o = 512
    x_next = _matmul(
        f"out_proj_{l}", y, w_out_all, grid=(s_len // tm, D_MODEL // tn_o, 1),
        a_spec=pl.BlockSpec((tm, D_MODEL), lambda i, j, k: (i, 0)),
        b_spec=pl.BlockSpec((N_CHIPS, ROW_SHARD, tn_o), lambda i, j, k: (0, 0, j)),
        o_spec=pl.BlockSpec((tm, tn_o), lambda i, j, k: (i, j)),
        out_shape=jax.ShapeDtypeStruct((s_len, D_MODEL), F32), dims=NN,
        res=x, res_spec=pl.BlockSpec((tm, tn_o), lambda i, j, k: (i, j)), after=hooks.before_out(y))
    saved = dict(x=x, h_t=h_t, proj=proj, mem_h=mem_h, mem_kv=mem_kv, o_b=o_b, o_c=o_c, y_t=y_t, bias=bias,
                 weights=(w_in_all, w_kv_all, w_out_all))
    return x_next, saved


class _NoExchange:
    def __init__(self):
        self.gave, self.kept = {}, {}

    def start(self, l, group, gives):
        self.gave[l, group] = gives
        return None

    def landed(self, l, group, after):
        return [jnp.zeros_like(g) for g in self.gave[l, group]]

    def send(self, l, group, parts):
        self.kept[l, group] = parts
        return None


def _layer_bwd(l, dxo, dxo_b, mem, sm, saved, place, exchange):
    s_len = dxo.shape[0]
    m_len = mem.shape[0]
    proj, y_t, h_t, mem_h, mem_kv = saved["proj"], saved["y_t"], saved["h_t"], saved["mem_h"], saved["mem_kv"]
    w_in_all, w_kv_all, w_out_all = saved["weights"]
    tm = min(1024, s_len)
    tn = 768
    per = IN_SHARD // tn
    half_rows = ROW_SHARD // 2

    def halves(make):
        give = lambda: make("give", lambda p: 1 - p[1], None, F32)
        keep = lambda theirs: make("keep", lambda p: p[1], theirs, BF16)
        return give, keep

    def grad_out(tag, half, theirs, dtype):
        o_spec = pl.BlockSpec((None, half_rows, 1024), lambda i, j, k, p: (i, 0, j))
        return _matmul(
            f"d_w_out_{l}_{tag}", y_t, dxo_b, grid=(N_CHIPS, D_MODEL // 1024, 1), place=place,
            a_spec=pl.BlockSpec((half_rows, s_len), lambda i, j, k, p: (2 * i + half(p), 0)),
            b_spec=pl.BlockSpec((s_len, 1024), lambda i, j, k, p: (0, j)), o_spec=o_spec,
            out_shape=jax.ShapeDtypeStruct((N_CHIPS, half_rows, D_MODEL), dtype), dims=NN,
            res=theirs, res_spec=o_spec)

    def grad_in(tag, half, theirs, dtype):
        o_spec = pl.BlockSpec((None, D_MODEL // 2, tn), lambda i, j, k, p: (j // per, 0, j % per))
        return _matmul(
            f"d_w_in_{l}_{tag}", h_t, dproj, grid=(1, IN_WIDTH // tn, 1), place=place,
            a_spec=pl.BlockSpec((D_MODEL // 2, s_len), lambda i, j, k, p: (half(p), 0)),
            b_spec=pl.BlockSpec((s_len, tn), lambda i, j, k, p: (0, j)), o_spec=o_spec,
            out_shape=jax.ShapeDtypeStruct((N_CHIPS, D_MODEL // 2, IN_SHARD), dtype), dims=NN,
            res=theirs, res_spec=o_spec)

    def grad_kv(tag, half, theirs, dtype):
        o_spec = pl.BlockSpec((None, half_rows, 2 * D_C), lambda i, j, k, p: (i, 0, 0))
        return _matmul(
            f"d_w_kv_{l}_{tag}", mem_h, dkv_b, grid=(N_CHIPS, 1, 1), place=place,
            a_spec=pl.BlockSpec((m_len, half_rows), lambda i, j, k, p: (0, 2 * i + half(p))),
            b_spec=pl.BlockSpec((m_len, 2 * D_C), lambda i, j, k, p: (0, 0)), o_spec=o_spec,
            out_shape=jax.ShapeDtypeStruct((N_CHIPS, half_rows, 2 * D_C), dtype), dims=TN,
            res=theirs, res_spec=o_spec)

    give_out, keep_out = halves(grad_out)
    token = exchange.start(l, "out", [give_out()])
    dy = _matmul(
        f"d_y_{l}", dxo_b, w_out_all, grid=(s_len // tm, N_CHIPS, 1),
        a_spec=pl.BlockSpec((tm, D_MODEL), lambda i, j, k: (i, 0)),
        b_spec=pl.BlockSpec((None, ROW_SHARD, D_MODEL), lambda i, j, k: (j, 0, 0)),
        o_spec=pl.BlockSpec((tm, ROW_SHARD), lambda i, j, k: (i, j)),
        out_shape=jax.ShapeDtypeStruct((s_len, D_MODEL), F32), dims=NT, after=token)
    (theirs_out,) = exchange.landed(l, "out", dy)
    token = exchange.send(l, "out", [keep_out(theirs_out)])
    qg, kg = sm["q_norm_g"][l][None], sm["k_norm_g"][l][None]
    dqkv = _sb_bwd(f"sb_bwd_{l}", proj, dy, token)
    dq_c, dmk, dmv, dqg, dkg = _mem_bwd(f"mem_bwd_{l}", proj, mem_kv, qg, kg, dy)
    w_s = sm["sgu_w"][l]
    dproj, dws, dbias, dlng, dlnb = _gate_bwd(
        f"gate_bwd_{l}", proj, dy, saved["o_b"], saved["o_c"], dqkv, dq_c, sm["sgu_ln_g"][l][None],
        sm["sgu_ln_b"][l][None], w_s, jnp.swapaxes(w_s, 1, 2), saved["bias"])
    dkv_b = jnp.concatenate([dmk, dmv], axis=1).astype(BF16)
    give_in, keep_in = halves(grad_in)
    give_kv, keep_kv = halves(grad_kv)
    token = exchange.start(l, "in", [give_in(), give_kv()])
    dh = _matmul(
        f"d_h_{l}", dproj, w_in_all, grid=(s_len // tm, D_MODEL // 512, 1),
        a_spec=pl.BlockSpec((tm, IN_WIDTH), lambda i, j, k: (i, 0)),
        b_spec=pl.BlockSpec((N_CHIPS, 512, IN_SHARD), lambda i, j, k: (0, j, 0)),
        o_spec=pl.BlockSpec((tm, 512), lambda i, j, k: (i, j)),
        out_shape=jax.ShapeDtypeStruct((s_len, D_MODEL), F32), dims=NT, after=token, vmem_mb=56)
    theirs_in, theirs_kv = exchange.landed(l, "in", dh)
    token = exchange.send(l, "in", [keep_in(theirs_in), keep_kv(theirs_kv)])
    dx, dx_b, dng = _rms_bwd(f"rms_bwd_{l}", saved["x"], dh, dxo, sm["norm_g"][l][None], min(256, s_len), token)
    d_mem_h = _matmul(
        f"d_mem_h_{l}", dkv_b, w_kv_all, grid=(1, N_CHIPS, 1),
        a_spec=pl.BlockSpec((m_len, 2 * D_C), lambda i, j, k: (0, 0)),
        b_spec=pl.BlockSpec((None, ROW_SHARD, 2 * D_C), lambda i, j, k: (j, 0, 0)),
        o_spec=pl.BlockSpec((m_len, ROW_SHARD), lambda i, j, k: (0, j)),
        out_shape=jax.ShapeDtypeStruct((m_len, D_MODEL), F32), dims=NT)
    dmng = _rms_gain_grad(f"mem_rms_bwd_{l}", mem, d_mem_h)
    dsgu_b = dbias[:, :A_GROUPS].T
    small = dict(norm_g=dng[0], sgu_ln_g=dlng[0], sgu_ln_b=dlnb[0], sgu_w=dws, sgu_b=dsgu_b, mem_norm_g=dmng[0],
                 q_norm_g=dqg[0], k_norm_g=dkg[0])
    return dx, dx_b, small


SMALL_NAMES = ("norm_g", "sgu_ln_g", "sgu_ln_b", "sgu_w", "sgu_b", "mem_norm_g", "q_norm_g", "k_norm_g")


def _local_step(x, mem, target, sm, w_all):
    saved = []
    cur = x
    for l in range(DEPTH):
        cur, sv = _layer_fwd(l, cur, mem, sm, _WholeWeights(*w_all[l]))
        saved.append(sv)
    dxo, dxo_b, loss = _loss_and_grad("loss", cur, target, min(256, x.shape[0]))
    small = [None] * DEPTH
    exchange = _NoExchange()
    place = jnp.zeros((2,), jnp.int32)
    for l in reversed(range(DEPTH)):
        dxo, dxo_b, small[l] = _layer_bwd(l, dxo, dxo_b, mem, sm, saved[l], place, exchange)
    small = {k: jnp.stack([small[l][k] for l in range(DEPTH)]) for k in SMALL_NAMES}
    return loss, dxo, small, exchange.gave, exchange.kept


def _place():
    x, y, c = lax.axis_index("x"), lax.axis_index("y"), lax.axis_index("c")
    return x, y, c


def _other_chips(x, y):
    return [(1 - x, y, 2 * (1 - x) + y), (x, 1 - y, 2 * x + 1 - y), (1 - x, 1 - y, 2 * (1 - x) + 1 - y)]


D2D_CHUNKS = 8


def _place_index():
    return jnp.stack([2 * lax.axis_index("x") + lax.axis_index("y"), lax.axis_index("c")]).astype(jnp.int32)


def _cast_into_slot(name, w, l, place):
    _, rows, cols = w.shape
    tr = min(256, rows)

    def body(p_ref, w_ref, o_ref):
        o_ref[...] = w_ref[...].astype(BF16)

    return pl.pallas_call(
        body, name=name,
        grid_spec=pltpu.PrefetchScalarGridSpec(
            num_scalar_prefetch=1, grid=(rows // tr,),
            in_specs=[pl.BlockSpec((None, tr, cols), lambda i, p: (l, i, 0))],
            out_specs=pl.BlockSpec((None, tr, cols), lambda i, p: (p[0], i, 0))),
        out_shape=jax.ShapeDtypeStruct((N_CHIPS, rows, cols), BF16),
        compiler_params=_params(("parallel",)),
    )(place, w)


HBM = pl.BlockSpec(memory_space=pltpu.HBM)
SEM = pl.BlockSpec(memory_space=pltpu.SEMAPHORE)
DATAFLOW = pltpu.SideEffectType.DATAFLOW_SIDE_EFFECTING


def _in_hbm(a):
    return pltpu.with_memory_space_constraint(a, pltpu.HBM)


ALL_PEERS = (0, 1, 2)
NEIGHBOURS = (0, 1)
DIAGONAL = (2,)


def _chip_copies_start(name, srcs, lands, make_copy, after=None, peers=ALL_PEERS):
    n_t = len(srcs)
    in_place = lands is None
    n_after = 0 if after is None else 1

    def body(*refs):
        src = refs[:n_t]
        k = (n_t if in_place else 2 * n_t) + n_after
        send_sems, recv_sems = refs[k], refs[k + 1]
        land = refs[k + 2:k + 2 + n_t] if in_place else refs[k + 2 + n_t:k + 2 + 2 * n_t]
        token = refs[-1]
        x, y, c = _place()
        me = 2 * x + y
        others = _other_chips(x, y)
        for t in range(n_t):
            for px, py, pk in [others[p] for p in peers]:
                s, d = make_copy(src[t], land[t], me, pk, c)
                pltpu.make_async_remote_copy(
                    src_ref=s, dst_ref=d, send_sem=send_sems.at[t], recv_sem=recv_sems.at[t],
                    device_id=(px, py, c), device_id_type=MESH).start()
        token[...] = jnp.zeros_like(token)

    bufs = list(srcs) if in_place else list(srcs) + list(lands)
    outs = pl.pallas_call(
        body, name=name,
        in_specs=[HBM] * len(bufs) + [ANY] * n_after,
        out_specs=[SEM, SEM] + [HBM] * len(bufs) + [pl.BlockSpec(memory_space=pltpu.VMEM)],
        out_shape=[pltpu.SemaphoreType.DMA((n_t,)), pltpu.SemaphoreType.DMA((n_t,))]
        + [pltpu.HBM(b.shape, b.dtype) for b in bufs] + [jax.ShapeDtypeStruct((8, 128), F32)],
        input_output_aliases={i: 2 + i for i in range(len(bufs))},
        compiler_params=pltpu.CompilerParams(has_side_effects=DATAFLOW),
    )(*[_in_hbm(b) for b in bufs], *([] if after is None else [after]))
    return outs[0], outs[1], list(outs[2:2 + len(bufs)]), outs[-1]


def _chip_copies_wait(name, send_sems, recv_sems, bufs, sent, landed, after):
    n_b = len(bufs)

    def body(*refs):
        buf = refs[:n_b]
        send_ref, recv_ref = refs[n_b], refs[n_b + 1]
        x, y, c = _place()
        for t, (s, d) in enumerate(zip(sent(buf), landed(buf))):
            out = pltpu.make_async_remote_copy(src_ref=s, dst_ref=s, send_sem=send_ref.at[t], recv_sem=recv_ref.at[t],
                                               device_id=(x, y, c), device_id_type=MESH)
            out.wait_send()
            arrived = pltpu.make_async_remote_copy(src_ref=d, dst_ref=d, send_sem=send_ref.at[t],
                                                   recv_sem=recv_ref.at[t], device_id=(x, y, c), device_id_type=MESH)
            arrived.wait_recv()

    after = list(after) if isinstance(after, (list, tuple)) else [after]
    return pl.pallas_call(
        body, name=name,
        in_specs=[HBM] * n_b + [SEM, SEM] + [ANY] * len(after), out_specs=[HBM] * n_b,
        out_shape=[pltpu.HBM(b.shape, b.dtype) for b in bufs],
        input_output_aliases={i: i for i in range(n_b)},
        compiler_params=pltpu.CompilerParams(has_side_effects=DATAFLOW),
    )(*bufs, send_sems, recv_sems, *after)


def _gather_start(name, bufs, after=None, peers=ALL_PEERS):
    def make_copy(src, land, me, pk, c):
        hr = src.shape[1] // 2
        return src.at[me, pl.ds(c * hr, hr)], land.at[me, pl.ds(c * hr, hr)]

    return _chip_copies_start(name, bufs, None, make_copy, after, peers)


def _gather_wait(name, send_sems, recv_sems, bufs, after, peers=ALL_PEERS):
    def half_shards(buf):
        return [b.at[pl.ds(0, len(peers)), pl.ds(0, b.shape[1] // 2)] for b in buf]

    return _chip_copies_wait(name, send_sems, recv_sems, bufs, half_shards, half_shards, after)


def _gather_forward_start(name, bufs, peers=ALL_PEERS):
    n_t = len(bufs)

    def body(*refs):
        mine = refs[:n_t]
        send_sems, recv_sems = refs[n_t], refs[n_t + 1]
        buf = refs[n_t + 2:2 * n_t + 2]
        token = refs[-1]
        x, y, c = _place()
        others = _other_chips(x, y)
        for q in range(D2D_CHUNKS):
            for t in range(n_t):
                hr = mine[t].shape[1] // 2
                cr = hr // D2D_CHUNKS
                rows = pl.ds(c * hr + q * cr, cr)
                for _, _, pk in [others[p] for p in peers]:
                    pltpu.make_async_remote_copy(
                        src_ref=mine[t].at[pk, rows], dst_ref=buf[t].at[pk, rows], send_sem=send_sems.at[t],
                        recv_sem=recv_sems.at[t], device_id=(x, y, 1 - c), device_id_type=MESH).start()
        token[...] = jnp.zeros_like(token)

    outs = pl.pallas_call(
        body, name=name,
        in_specs=[HBM] * n_t,
        out_specs=[SEM, SEM] + [HBM] * n_t + [pl.BlockSpec(memory_space=pltpu.VMEM)],
        out_shape=[pltpu.SemaphoreType.DMA((n_t,)), pltpu.SemaphoreType.DMA((n_t,))]
        + [pltpu.HBM(b.shape, b.dtype) for b in bufs] + [jax.ShapeDtypeStruct((8, 128), F32)],
        input_output_aliases={i: 2 + i for i in range(n_t)},
        compiler_params=pltpu.CompilerParams(has_side_effects=DATAFLOW),
    )(*[_in_hbm(b) for b in bufs])
    return outs[0], outs[1], list(outs[2:2 + n_t]), outs[-1]


def _core_exchange_start(name, grads):
    n_t = len(grads)
    lands = [lax.empty(g.shape, g.dtype) for g in grads]

    def body(*refs):
        src = refs[:n_t]
        send_sems, recv_sems = refs[2 * n_t], refs[2 * n_t + 1]
        land = refs[2 * n_t + 2 + n_t:2 * n_t + 2 + 2 * n_t]
        token = refs[-1]
        x, y, c = _place()
        for q in range(D2D_CHUNKS):
            for t in range(n_t):
                cr = src[t].shape[1] // D2D_CHUNKS
                rows = pl.ds(q * cr, cr)
                pltpu.make_async_remote_copy(
                    src_ref=src[t].at[:, rows], dst_ref=land[t].at[:, rows], send_sem=send_sems.at[t],
                    recv_sem=recv_sems.at[t], device_id=(x, y, 1 - c), device_id_type=MESH).start()
        token[...] = jnp.zeros_like(token)

    bufs = list(grads) + lands
    outs = pl.pallas_call(
        body, name=name,
        in_specs=[HBM] * len(bufs),
        out_specs=[SEM, SEM] + [HBM] * len(bufs) + [pl.BlockSpec(memory_space=pltpu.VMEM)],
        out_shape=[pltpu.SemaphoreType.DMA((n_t,)), pltpu.SemaphoreType.DMA((n_t,))]
        + [pltpu.HBM(b.shape, b.dtype) for b in bufs] + [jax.ShapeDtypeStruct((8, 128), F32)],
        input_output_aliases={i: 2 + i for i in range(len(bufs))},
        compiler_params=pltpu.CompilerParams(has_side_effects=DATAFLOW),
    )(*[_in_hbm(b) for b in bufs])
    return outs[0], outs[1], list(outs[2:2 + len(bufs)]), outs[-1]


def _core_exchange_wait(name, send_sems, recv_sems, bufs, after):
    n_t = len(bufs) // 2

    def body(*refs):
        land = refs[n_t:2 * n_t]
        send_ref, recv_ref = refs[2 * n_t], refs[2 * n_t + 1]
        x, y, c = _place()
        for t in range(n_t):
            whole = pltpu.make_async_remote_copy(src_ref=land[t], dst_ref=land[t], send_sem=send_ref.at[t],
                                                 recv_sem=recv_ref.at[t], device_id=(x, y, c), device_id_type=MESH)
            whole.wait_send()
            whole.wait_recv()

    outs = pl.pallas_call(
        body, name=name,
        in_specs=[HBM] * (2 * n_t) + [SEM, SEM, ANY], out_specs=[HBM] * (2 * n_t),
        out_shape=[pltpu.HBM(b.shape, b.dtype) for b in bufs],
        input_output_aliases={i: i for i in range(2 * n_t)},
        compiler_params=pltpu.CompilerParams(has_side_effects=DATAFLOW),
    )(*bufs, send_sems, recv_sems, after)
    return list(outs[:n_t]), list(outs[n_t:])


def _chip_exchange_start(name, parts):
    lands = [lax.empty(p.shape, p.dtype) for p in parts]
    return _chip_copies_start(name, parts, lands, lambda src, land, me, pk, c: (src.at[pk], land.at[me]))


def _chip_exchange_wait(name, send_sems, recv_sems, bufs, after):
    n_t = len(bufs) // 2
    return _chip_copies_wait(name, send_sems, recv_sems, bufs,
                             lambda buf: [b.at[pl.ds(0, 3)] for b in buf[:n_t]],
                             lambda buf: [b.at[pl.ds(0, 3)] for b in buf[n_t:]], after)


def _sum_chips(name, parts, landed, place, l, stacked):
    chips, rows, cols = landed.shape
    tr = min(256, rows)
    per = rows // tr

    def body(p_ref, own_ref, *refs):
        land, o_ref = refs[:chips], refs[-1]
        tot = None
        for k in range(chips):
            term = jnp.where(p_ref[0] == k, own_ref[...], land[k][...]).astype(F32)
            tot = term if tot is None else tot + term
        o_ref[...] = tot

    def from_chip(k):
        return pl.BlockSpec((None, tr, cols), lambda i, p: (jnp.where(p[0] == k, (k + 1) % chips, k), i, 0))

    in_specs = [pl.BlockSpec((None, tr, cols), lambda i, p: (p[0], i, 0))] + [from_chip(k) for k in range(chips)]
    args = [parts] + [landed] * chips
    aliases = {}
    if stacked is not None:
        in_specs.append(ANY)
        args.append(stacked)
        aliases = {len(args): 0}
    return pl.pallas_call(
        body, name=name,
        grid_spec=pltpu.PrefetchScalarGridSpec(
            num_scalar_prefetch=1, grid=(per,), in_specs=in_specs,
            out_specs=pl.BlockSpec((None, tr, cols), lambda i, p: (l, p[1] * per + i, 0))),
        out_shape=jax.ShapeDtypeStruct((DEPTH, 2 * rows, cols), F32), input_output_aliases=aliases,
        compiler_params=_params(("parallel",)),
    )(place, *args)


def _core_share_start(name, bufs, l):
    n_t = len(bufs)

    def body(*refs):
        mine = refs[:n_t]
        send_sems, recv_sems = refs[n_t], refs[n_t + 1]
        buf = refs[n_t + 2:2 * n_t + 2]
        token = refs[-1]
        x, y, c = _place()
        for q in range(D2D_CHUNKS):
            for t in range(n_t):
                hr = mine[t].shape[1] // 2
                cr = hr // D2D_CHUNKS
                rows = pl.ds(c * hr + q * cr, cr)
                pltpu.make_async_remote_copy(
                    src_ref=mine[t].at[l, rows], dst_ref=buf[t].at[l, rows], send_sem=send_sems.at[t],
                    recv_sem=recv_sems.at[t], device_id=(x, y, 1 - c), device_id_type=MESH).start()
        token[...] = jnp.zeros_like(token)

    outs = pl.pallas_call(
        body, name=name,
        in_specs=[HBM] * n_t,
        out_specs=[SEM, SEM] + [HBM] * n_t + [pl.BlockSpec(memory_space=pltpu.VMEM)],
        out_shape=[pltpu.SemaphoreType.DMA((n_t,)), pltpu.SemaphoreType.DMA((n_t,))]
        + [pltpu.HBM(b.shape, b.dtype) for b in bufs] + [jax.ShapeDtypeStruct((8, 128), F32)],
        input_output_aliases={i: 2 + i for i in range(n_t)},
        compiler_params=pltpu.CompilerParams(has_side_effects=DATAFLOW),
    )(*[_in_hbm(b) for b in bufs])
    return outs[0], outs[1], list(outs[2:2 + n_t]), outs[-1]


def _core_share_wait(name, send_sems, recv_sems, bufs, l, after):
    def half_layer(buf):
        return [b.at[l, pl.ds(0, b.shape[1] // 2)] for b in buf]

    return _chip_copies_wait(name, send_sems, recv_sems, bufs, half_layer, half_layer, after)


def _all_reduce_small(vec, after=None):
    rows, lanes = vec.shape
    hr = rows // 2

    def body(v_ref, *refs):
        o_ref, sib_ref, chips_ref, send_sems, recv_sems = refs[-5:]
        x, y, c = _place()
        me = 2 * x + y
        sibling = (x, y, 1 - c)
        mine = pl.ds(pl.multiple_of(c * hr, 8), hr)
        theirs = pl.ds(pl.multiple_of((1 - c) * hr, 8), hr)
        swap = pltpu.make_async_remote_copy(
            src_ref=v_ref.at[theirs], dst_ref=sib_ref, send_sem=send_sems.at[0], recv_sem=recv_sems.at[0],
            device_id=sibling, device_id_type=MESH)
        swap.start()
        swap.wait_recv()
        chips_ref[me] = v_ref[mine] + sib_ref[...]
        copies = []
        for j, (px, py, pk) in enumerate(_other_chips(x, y)):
            cp = pltpu.make_async_remote_copy(
                src_ref=chips_ref.at[me], dst_ref=chips_ref.at[me], send_sem=send_sems.at[1 + j],
                recv_sem=recv_sems.at[1 + j], device_id=(px, py, c), device_id_type=MESH)
            cp.start()
            copies.append(cp)
        for j, (px, py, pk) in enumerate(_other_chips(x, y)):
            pltpu.make_async_remote_copy(
                src_ref=chips_ref.at[pk], dst_ref=chips_ref.at[pk], send_sem=send_sems.at[1 + j],
                recv_sem=recv_sems.at[1 + j], device_id=(px, py, c), device_id_type=MESH).wait_recv()
        tot = chips_ref[0]
        for k in range(1, N_CHIPS):
            tot = tot + chips_ref[k]
        o_ref[mine] = tot
        share = pltpu.make_async_remote_copy(
            src_ref=o_ref.at[mine], dst_ref=o_ref.at[mine], send_sem=send_sems.at[4], recv_sem=recv_sems.at[4],
            device_id=sibling, device_id_type=MESH)
        share.start()
        pltpu.make_async_remote_copy(
            src_ref=o_ref.at[theirs], dst_ref=o_ref.at[theirs], send_sem=send_sems.at[4], recv_sem=recv_sems.at[4],
            device_id=sibling, device_id_type=MESH).wait_recv()
        swap.wait_send()
        for cp in copies:
            cp.wait_send()
        share.wait_send()

    vm = pl.BlockSpec(memory_space=pltpu.VMEM)
    return pl.pallas_call(
        body, name="small_all_reduce", in_specs=[vm] + ([] if after is None else [ANY]), out_specs=vm,
        out_shape=jax.ShapeDtypeStruct((rows, lanes), F32),
        scratch_shapes=[pltpu.VMEM((hr, lanes), F32), pltpu.VMEM((N_CHIPS, hr, lanes), F32),
                        pltpu.SemaphoreType.DMA((5,)), pltpu.SemaphoreType.DMA((5,))],
        compiler_params=pltpu.CompilerParams(has_side_effects=True, vmem_limit_bytes=48 * MIB),
    )(vec, *([] if after is None else [after]))


def _adamw(name, w, g, m, v, place, l=0, half=None, done=None, after=None):
    layers, rows, cols = w.shape
    span = rows if half is None else rows // 2
    tr = span
    for cand in (256, 128, 64, 32, 16, 8):
        if span % cand == 0:
            tr = cand
            break
    per = span // tr
    c1 = 1.0 - ADAM_B1 ** ADAM_STEP
    c2 = 1.0 - ADAM_B2 ** ADAM_STEP

    def first_block(p):
        return 0 if half is None else (p[1] if half == "own" else 1 - p[1]) * per

    def body(p_ref, w_ref, g_ref, m_ref, v_ref, *refs):
        go_ref, d_ref, nm_ref, nv_ref = refs[-4:]
        gv = g_ref[...]
        nm = ADAM_B1 * m_ref[...] + (1.0 - ADAM_B1) * gv
        nv = ADAM_B2 * v_ref[...] + (1.0 - ADAM_B2) * (gv * gv)
        go_ref[...] = gv
        nm_ref[...] = nm
        nv_ref[...] = nv
        d_ref[...] = -ADAM_LR * ((nm / c1) / (jnp.sqrt(nv / c2) + ADAM_EPS) + ADAM_WD * w_ref[...])

    blk = pl.BlockSpec((None, tr, cols), lambda i, p: (l, first_block(p) + i, 0))
    out = jax.ShapeDtypeStruct((layers, rows, cols), F32)
    extra = ([] if done is None else list(done)) + ([] if after is None else [after])
    aliases = {} if done is None else {5 + i: i for i in range(4)}
    return pl.pallas_call(
        body, name=name,
        grid_spec=pltpu.PrefetchScalarGridSpec(
            num_scalar_prefetch=1, grid=(per,), in_specs=[blk] * 4 + [ANY] * len(extra), out_specs=[blk] * 4),
        out_shape=[out] * 4, input_output_aliases=aliases,
        compiler_params=_params(("parallel",)),
    )(place, w, g, m, v, *extra)


LANES = 128
SUBLANES = 8
SMALL_SHAPES = {
    "norm_g": (DEPTH, D_MODEL), "sgu_ln_g": (DEPTH, D_A), "sgu_ln_b": (DEPTH, D_A),
    "sgu_w": (DEPTH, A_GROUPS, CHUNK, CHUNK), "sgu_b": (DEPTH, A_GROUPS, CHUNK), "mem_norm_g": (DEPTH, D_MODEL),
    "q_norm_g": (DEPTH, HEAD_DIM), "k_norm_g": (DEPTH, HEAD_DIM)}


def _small_layout():
    at, off = {}, 0
    for k in SMALL_NAMES:
        n = math.prod(SMALL_SHAPES[k]) // LANES
        at[k] = (off, n)
        off += -(-n // SUBLANES) * SUBLANES
    return at, off, -(-(off + SUBLANES) // (2 * SUBLANES)) * 2 * SUBLANES


def _pack_small(parts, loss=None):
    at, loss_row, rows = _small_layout()
    pieces = []
    for k in SMALL_NAMES:
        n = at[k][1]
        pieces.append(jnp.pad(parts[k].reshape(n, LANES), ((0, -(-n // SUBLANES) * SUBLANES - n), (0, 0))))
    tile = jnp.zeros((SUBLANES, LANES), F32) if loss is None else jnp.broadcast_to(loss.reshape(1, 1), (SUBLANES, LANES))
    pieces += [tile, jnp.zeros((rows - loss_row - SUBLANES, LANES), F32)]
    return jnp.concatenate(pieces)


def _adamw_small(w, g, m, v):
    at, _, rows = _small_layout()
    c1 = 1.0 - ADAM_B1 ** ADAM_STEP
    c2 = 1.0 - ADAM_B2 ** ADAM_STEP
    n_names = len(SMALL_NAMES)

    def body(w_ref, g_ref, m_ref, v_ref, *refs):
        outs, (d_ref, nm_ref, nv_ref) = refs[:4 * n_names], refs[4 * n_names:]
        gv = g_ref[...]
        nm = ADAM_B1 * m_ref[...] + (1.0 - ADAM_B1) * gv
        nv = ADAM_B2 * v_ref[...] + (1.0 - ADAM_B2) * (gv * gv)
        nm_ref[...] = nm
        nv_ref[...] = nv
        d_ref[...] = -ADAM_LR * ((nm / c1) / (jnp.sqrt(nv / c2) + ADAM_EPS) + ADAM_WD * w_ref[...])
        for kind, src in enumerate((g_ref, d_ref, nm_ref, nv_ref)):
            for i, k in enumerate(SMALL_NAMES):
                o_ref = outs[kind * n_names + i]
                first, n = at[k]
                shape = SMALL_SHAPES[k]
                if shape[-1] == LANES:
                    o_ref[...] = src[pl.ds(first, n), :].reshape(shape)
                else:
                    per = shape[-1] // LANES
                    for r in range(n):
                        o_ref[pl.ds(r // per, 1), pl.ds((r % per) * LANES, LANES)] = src[pl.ds(first + r, 1), :]

    out_shape = [jax.ShapeDtypeStruct(SMALL_SHAPES[k], F32) for _ in range(4) for k in SMALL_NAMES]
    outs = pl.pallas_call(
        body, name="adamw_small", out_shape=out_shape,
        scratch_shapes=[pltpu.VMEM((rows, LANES), F32)] * 3, compiler_params=_params(None),
    )(w, g, m, v)
    return [dict(zip(SMALL_NAMES, outs[kind * n_names:(kind + 1) * n_names])) for kind in range(4)]


WEIGHT_ORDER = ("norm_g", "w_in", "sgu_ln_g", "sgu_ln_b", "sgu_w", "sgu_b", "mem_norm_g", "w_mem_kv", "q_norm_g",
                "k_norm_g", "w_out")


def kernel(x, mem, norm_g, w_in, sgu_ln_g, sgu_ln_b, sgu_w, sgu_b, mem_norm_g, w_mem_kv, q_norm_g, k_norm_g, w_out, loss_target, m_norm_g, m_w_in, m_sgu_ln_g, m_sgu_ln_b, m_sgu_w, m_sgu_b, m_mem_norm_g, m_w_mem_kv, m_q_norm_g, m_k_norm_g, m_w_out, v_norm_g, v_w_in, v_sgu_ln_g, v_sgu_ln_b, v_sgu_w, v_sgu_b, v_mem_norm_g, v_w_mem_kv, v_q_norm_g, v_k_norm_g, v_w_out):
    weights = dict(norm_g=norm_g, w_in=w_in, sgu_ln_g=sgu_ln_g, sgu_ln_b=sgu_ln_b, sgu_w=sgu_w, sgu_b=sgu_b,
                   mem_norm_g=mem_norm_g, w_mem_kv=w_mem_kv, q_norm_g=q_norm_g, k_norm_g=k_norm_g, w_out=w_out)
    mom_m = dict(norm_g=m_norm_g, w_in=m_w_in, sgu_ln_g=m_sgu_ln_g, sgu_ln_b=m_sgu_ln_b, sgu_w=m_sgu_w, sgu_b=m_sgu_b,
                 mem_norm_g=m_mem_norm_g, w_mem_kv=m_w_mem_kv, q_norm_g=m_q_norm_g, k_norm_g=m_k_norm_g, w_out=m_w_out)
    mom_v = dict(norm_g=v_norm_g, w_in=v_w_in, sgu_ln_g=v_sgu_ln_g, sgu_ln_b=v_sgu_ln_b, sgu_w=v_sgu_w, sgu_b=v_sgu_b,
                 mem_norm_g=v_mem_norm_g, w_mem_kv=v_w_mem_kv, q_norm_g=v_q_norm_g, k_norm_g=v_k_norm_g, w_out=v_w_out)
    big = ("w_in", "w_mem_kv", "w_out")
    sm = {k: weights[k] for k in SMALL_NAMES}

    place = _place_index()
    xs, mems, target = x[0], mem[0], loss_target[0]

    slots = [[_cast_into_slot(f"cast_{k}_{l}", weights[k], l, place) for k in big] for l in range(DEPTH)]
    saved = [None] * DEPTH

    chips, cores = {}, {}
    me = place[0]
    arrival = jnp.stack([me, me ^ 2, me ^ 1, 3 - me]).astype(jnp.int32)
    shard_order = jnp.arange(N_CHIPS, dtype=jnp.int32)

    def start_gather(l, after=None):
        if l == 0:
            chips[l, "in_n"] = _gather_start("gather_start_0_in_n", slots[l][:1], after, NEIGHBOURS)
            chips[l, "in_d"] = _gather_start("gather_start_0_in_d", chips[l, "in_n"][2], chips[l, "in_n"][3], DIAGONAL)
            last = chips[l, "in_d"]
        else:
            last = chips[l, "in"] = _gather_start(f"gather_start_{l}_in", slots[l][:1], after)
        chips[l, "rest"] = _gather_start(f"gather_start_{l}_rest", slots[l][1:], last[3])
        return chips[l, "rest"][3]

    def hand_to_sibling(l, group, after):
        send_sems, recv_sems, bufs, _ = chips[l, group]
        bufs = _gather_wait(f"gather_wait_{l}_{group}", send_sems, recv_sems, bufs, after)
        cores[l, group] = _gather_forward_start(f"gather_forward_{l}_{group}", bufs)
        return cores[l, group][3]

    def whole(l, group, after):
        send_sems, recv_sems, bufs, _ = cores[l, group]
        return _gather_wait(f"gather_whole_{l}_{group}", send_sems, recv_sems, bufs, after)

    later_slots = [s for layer in slots[1:] for s in layer]

    class Gathered:
        def __init__(self, l):
            self.l = l
            self.buf = None

        def landed_from(self, tag, peers, after, behind):
            send_sems, recv_sems, _, _ = chips[0, "in_" + tag]
            buf = _gather_wait(f"gather_wait_0_in_{tag}", send_sems, recv_sems, self.buf, after, peers)
            send_sems, recv_sems, buf, token = _gather_forward_start(f"gather_forward_0_in_{tag}", buf, peers)
            self.buf = _gather_wait(f"gather_whole_0_in_{tag}", send_sems, recv_sems, buf, [token] + behind, peers)

        def w_in(self, stage, h, proj):
            if self.l > 0:
                return (whole(self.l, "in", h)[0], shard_order, 0, N_CHIPS) if stage == 0 else None
            if stage == 0:
                self.buf = chips[0, "in_d"][2]
                return self.buf[0], arrival, 0, 1
            if stage == 1:
                self.landed_from("n", NEIGHBOURS, [proj, first_rest_token], later_slots)
                return self.buf[0], arrival, 1, 2
            if stage == 2:
                self.landed_from("d", DIAGONAL, proj, [])
                return self.buf[0], arrival, 3, 1
            return None

        def rest_start(self, proj):
            token = hand_to_sibling(self.l, "rest", proj)
            return start_gather(self.l + 1, token) if self.l + 1 < DEPTH else token

        def rest_finish(self, o_b):
            w_kv_all, w_out_all = whole(self.l, "rest", o_b)
            return w_kv_all, w_out_all, None

        def before_out(self, y):
            return hand_to_sibling(self.l + 1, "in", y) if self.l + 1 < DEPTH else None

    first_rest_token = start_gather(0)
    cur = xs
    for l in range(DEPTH):
        cur, saved[l] = _layer_fwd(l, cur, mems, sm, Gathered(l))
    dxo, dxo_b, loss_part = _loss_and_grad("loss", cur, target, min(256, xs.shape[0]))

    small_g = [None] * DEPTH
    flight = {}

    class Exchange:
        def __init__(self):
            self.cores = {}

        def start(self, l, group, gives):
            *self.cores[l, group], token = _core_exchange_start(f"grad_core_start_{l}_{group}", gives)
            return token

        def landed(self, l, group, after):
            send_sems, recv_sems, bufs = self.cores[l, group]
            return _core_exchange_wait(f"grad_core_wait_{l}_{group}", send_sems, recv_sems, bufs, after)[1]

        def send(self, l, group, parts):
            *flight[l, group], token = _chip_exchange_start(f"grad_chip_start_{l}_{group}", parts)
            return token

    exchange = Exchange()
    for l in reversed(range(DEPTH)):
        dxo, dxo_b, small_g[l] = _layer_bwd(l, dxo, dxo_b, mems, sm, saved[l], place, exchange)
    grad_x = dxo

    groups = (("out", ("w_out",)), ("in", ("w_in", "w_mem_kv")))
    halves, stepped = dict.fromkeys(big), dict.fromkeys(big)
    small_g = {k: jnp.stack([small_g[l][k] for l in range(DEPTH)]) for k in SMALL_NAMES}
    after = grad_x
    sharing = {}

    def reduce_group(l, group, names):
        nonlocal after
        send_sems, recv_sems, bufs = flight[l, group]
        bufs = _chip_exchange_wait(f"grad_chip_wait_{l}_{group}", send_sems, recv_sems, bufs, after)
        for t, k in enumerate(names):
            halves[k] = _sum_chips(f"grad_chip_sum_{l}_{k}", bufs[t], bufs[len(names) + t], place, l, halves[k])
        *sharing[l, group], after = _core_share_start(f"grad_core_share_{l}_{group}", [halves[k] for k in names], l)

    def step(l, k, buf, half):
        nonlocal after
        tag = "" if half is None else "_" + half
        stepped[k] = _adamw(f"adamw_{k}_{l}{tag}", weights[k], buf, mom_m[k], mom_v[k], place, l, half, stepped[k],
                            after)
        after = stepped[k][1]

    def step_group(l, group, names, overlap):
        nonlocal after
        send_sems, recv_sems, bufs = sharing[l, group]
        if overlap:
            for k, buf in zip(names, bufs):
                step(l, k, buf, "own")
        bufs = _core_share_wait(f"grad_core_shared_{l}_{group}", send_sems, recv_sems, bufs, l, after)
        for k, buf in zip(names, bufs):
            halves[k] = buf
            step(l, k, buf, "other" if overlap else None)

    for l in reversed(range(DEPTH)):
        last = l == 0
        (g_out, n_out), (g_in, n_in) = groups
        reduce_group(l, g_out, n_out)
        if last:
            step_group(l, g_out, n_out, False)
            small_sum = _all_reduce_small(_pack_small(small_g, loss_part), after)
            small_step = _adamw_small(_pack_small(sm), small_sum, _pack_small({k: mom_m[k] for k in SMALL_NAMES}),
                                      _pack_small({k: mom_v[k] for k in SMALL_NAMES}))
            after = small_step[1]["sgu_w"]
        reduce_group(l, g_in, n_in)
        if not last:
            step_group(l, g_out, n_out, False)
        step_group(l, g_in, n_in, last)

    grads, delta, new_m, new_v = ({k: stepped[k][i] for k in big} for i in range(4))
    for out, small in zip((grads, delta, new_m, new_v), small_step):
        out.update(small)
    loss = small_sum[_small_layout()[1], 0]
    return (loss, grad_x[None], *[grads[k] for k in WEIGHT_ORDER], *[delta[k] for k in WEIGHT_ORDER],
            *[new_m[k] for k in WEIGHT_ORDER], *[new_v[k] for k in WEIGHT_ORDER])
```

```python
import functools
import math

import jax
import jax.numpy as jnp
from jax import lax
from jax.experimental import pallas as pl
from jax.experimental.pallas import tpu as pltpu

F32 = jnp.float32
BF16 = jnp.bfloat16
MESH = pl.DeviceIdType.MESH

D_MODEL = 2048
DEPTH = 2
CHUNK = 128
D_A = 1024
A_GROUPS = 8
D_B = 512
D_C = 512
HEADS = 4
HEAD_DIM = 128
IN_WIDTH = 6144
N_CHIPS = 4
EPS = 1e-6
ATT_SCALE = 1.0 / math.sqrt(HEAD_DIM)

OFF_U, OFF_V, OFF_ZA = 0, 1024, 2048
OFF_QB, OFF_KB, OFF_VB, OFF_ZB = 3072, 3584, 4096, 4608
OFF_QC, OFF_ZC = 5120, 5632
OFF_YB, OFF_YC = 1024, 1536

ADAM_LR = 0.001
ADAM_B1 = 0.9
ADAM_B2 = 0.999
ADAM_EPS = 1e-08
ADAM_WD = 0.01
ADAM_STEP = 10

MIB = 1024 * 1024
ANY = pl.BlockSpec(memory_space=pl.ANY)


def _params(semantics=None, vmem_mb=48):
    return pltpu.CompilerParams(dimension_semantics=semantics, vmem_limit_bytes=vmem_mb * MIB)


def _gelu(x):
    return 0.5 * x * (1.0 + lax.erf(x * (1.0 / math.sqrt(2.0))))


def _gelu_grad(x):
    cdf = 0.5 * (1.0 + lax.erf(x * (1.0 / math.sqrt(2.0))))
    pdf = jnp.exp(-0.5 * x * x) * (1.0 / math.sqrt(2.0 * math.pi))
    return cdf + x * pdf


def _sigmoid(x):
    return 1.0 / (1.0 + jnp.exp(-x))


def _silu_and_grad(z):
    s = _sigmoid(z)
    return z * s, s * (1.0 + z * (1.0 - s))


def _split_bf16(x):
    hi = x.astype(BF16)
    lo = (x - hi.astype(F32)).astype(BF16)
    return hi, lo


def _dot(a, b, dims):
    return lax.dot_general(a, b, (dims, ((), ())), preferred_element_type=F32)


NN = ((1,), (0,))
NT = ((1,), (1,))
TN = ((0,), (0,))


def _matmul(name, a, b, *, grid, a_spec, b_spec, o_spec, out_shape, dims, res=None, res_spec=None, after=None,
            place=None, into=None, vmem_mb=48):
    nk = grid[2]
    n_in = 2 + (res is not None) + (after is not None) + (into is not None)

    def body(*refs):
        if place is not None:
            refs = refs[1:]
        a_ref, b_ref = refs[0], refs[1]
        r_ref = refs[2] if res is not None else None
        o_ref = refs[n_in]
        if len(b_ref.shape) == 3 and dims == NN:
            part = _dot(a_ref[...], b_ref[...].reshape(-1, b_ref.shape[-1]), dims)
        elif len(b_ref.shape) == 3:
            width = b_ref.shape[-1]
            part = None
            for s in range(b_ref.shape[0]):
                term = _dot(a_ref[:, s * width:(s + 1) * width], b_ref[s], dims)
                part = term if part is None else part + term
        else:
            part = _dot(a_ref[...], b_ref[...], dims)
        if nk == 1:
            if r_ref is not None:
                part = part + r_ref[...]
            o_ref[...] = part.astype(o_ref.dtype)
            return
        acc_ref = refs[n_in + 1]
        k = pl.program_id(2)

        @pl.when(k == 0)
        def _():
            acc_ref[...] = part

        @pl.when(k > 0)
        def _():
            acc_ref[...] += part

        @pl.when(k == nk - 1)
        def _():
            tot = acc_ref[...]
            if r_ref is not None:
                tot = tot + r_ref[...]
            o_ref[...] = tot.astype(o_ref.dtype)

    in_specs = [a_spec, b_spec]
    args = [a, b]
    if res is not None:
        in_specs.append(res_spec)
        args.append(res)
    if after is not None:
        in_specs.append(ANY)
        args.append(after)
    aliases = {}
    if into is not None:
        in_specs.append(ANY)
        args.append(into)
        aliases = {len(args) - 1 + (place is not None): 0}
    acc_shape = tuple(d for d in o_spec.block_shape if d is not None)
    scratch = [pltpu.VMEM(acc_shape, F32)] if nk > 1 else []
    params = _params(("parallel", "parallel", "arbitrary"), vmem_mb)
    if place is not None:
        return pl.pallas_call(
            body, name=name, out_shape=out_shape, compiler_params=params, input_output_aliases=aliases,
            grid_spec=pltpu.PrefetchScalarGridSpec(num_scalar_prefetch=1, grid=grid, in_specs=in_specs,
                                                   out_specs=o_spec, scratch_shapes=scratch),
        )(place, *args)
    return pl.pallas_call(
        body, name=name, grid=grid, in_specs=in_specs, out_specs=o_spec, out_shape=out_shape,
        scratch_shapes=scratch, compiler_params=params, input_output_aliases=aliases,
    )(*args)


def _rms_fwd(name, x, g, tr, after=None, transposed=False):
    rows, d = x.shape

    def body(x_ref, g_ref, *refs):
        outs = refs[1:] if after is not None else refs
        xv = x_ref[...]
        r = lax.rsqrt(jnp.mean(xv * xv, axis=-1, keepdims=True) + EPS)
        h = xv * r * g_ref[...]
        outs[0][...] = h.astype(BF16)
        if transposed:
            outs[1][...] = h.T.astype(BF16)

    out_specs = [pl.BlockSpec((tr, d), lambda i: (i, 0))]
    out_shape = [jax.ShapeDtypeStruct((rows, d), BF16)]
    if transposed:
        out_specs.append(pl.BlockSpec((d, tr), lambda i: (0, i)))
        out_shape.append(jax.ShapeDtypeStruct((d, rows), BF16))
    outs = pl.pallas_call(
        body, name=name, grid=(rows // tr,),
        in_specs=[pl.BlockSpec((tr, d), lambda i: (i, 0)), pl.BlockSpec((1, d), lambda i: (0, 0))]
        + ([] if after is None else [ANY]),
        out_specs=out_specs, out_shape=out_shape,
        compiler_params=_params(("parallel",)),
    )(x, g, *([] if after is None else [after]))
    return outs if transposed else outs[0]


def _rms_bwd(name, x, dh, dres, g, tr, after=None):
    rows, d = x.shape

    def body(x_ref, dh_ref, dres_ref, g_ref, *refs):
        dx_ref, dxb_ref, dg_ref = refs[-3:]
        xv = x_ref[...]
        r = lax.rsqrt(jnp.mean(xv * xv, axis=-1, keepdims=True) + EPS)
        xhat = xv * r
        dhv = dh_ref[...]
        dxh = dhv * g_ref[...]
        dx = r * (dxh - xhat * jnp.mean(dxh * xhat, axis=-1, keepdims=True)) + dres_ref[...]
        dx_ref[...] = dx
        dxb_ref[...] = dx.astype(BF16)
        part = jnp.sum(dhv * xhat, axis=0, keepdims=True)

        @pl.when(pl.program_id(0) == 0)
        def _():
            dg_ref[...] = part

        @pl.when(pl.program_id(0) > 0)
        def _():
            dg_ref[...] += part

    blk = pl.BlockSpec((tr, d), lambda i: (i, 0))
    vec = pl.BlockSpec((1, d), lambda i: (0, 0))
    return pl.pallas_call(
        body, name=name, grid=(rows // tr,), in_specs=[blk, blk, blk, vec] + ([] if after is None else [ANY]),
        out_specs=[blk, blk, vec],
        out_shape=[jax.ShapeDtypeStruct((rows, d), F32), jax.ShapeDtypeStruct((rows, d), BF16),
                   jax.ShapeDtypeStruct((1, d), F32)],
        compiler_params=_params(("arbitrary",)),
    )(x, dh, dres, g, *([] if after is None else [after]))


def _rms_gain_grad(name, x, dh):
    rows, d = x.shape

    def body(x_ref, dh_ref, dg_ref):
        xv = x_ref[...]
        r = lax.rsqrt(jnp.mean(xv * xv, axis=-1, keepdims=True) + EPS)
        dg_ref[...] = jnp.sum(dh_ref[...] * xv * r, axis=0, keepdims=True)

    return pl.pallas_call(
        body, name=name, out_shape=jax.ShapeDtypeStruct((1, d), F32), compiler_params=_params(None),
    )(x, dh)


def _loss_and_grad(name, y, target, tr):
    rows, d = y.shape
    n = rows // tr

    def body(y_ref, t_ref, dx_ref, dxb_ref, loss_ref, acc_ref):
        e = y_ref[...] - t_ref[...]
        dx = e * (1.0 / d)
        dx_ref[...] = dx
        dxb_ref[...] = dx.astype(BF16)
        part = jnp.sum(e * e, axis=0, keepdims=True)
        i = pl.program_id(0)

        @pl.when(i == 0)
        def _():
            acc_ref[...] = part

        @pl.when(i > 0)
        def _():
            acc_ref[...] += part

        @pl.when(i == n - 1)
        def _():
            loss_ref[...] = jnp.sum(acc_ref[...], axis=-1, keepdims=True) * (0.5 / d)

    blk = pl.BlockSpec((tr, d), lambda i: (i, 0))
    return pl.pallas_call(
        body, name=name, grid=(n,), in_specs=[blk, blk],
        out_specs=[blk, blk, pl.BlockSpec((1, 1), lambda i: (0, 0))],
        out_shape=[jax.ShapeDtypeStruct((rows, d), F32), jax.ShapeDtypeStruct((rows, d), BF16),
                   jax.ShapeDtypeStruct((1, 1), F32)],
        scratch_shapes=[pltpu.VMEM((1, d), F32)],
        compiler_params=_params(("arbitrary",)),
    )(y, target)


SB_T = 256
SB_HEADS = 4


LOG2E = 1.4426950408889634


def _sb_scores(q, kblk):
    z2 = _dot(q, kblk, NT) * (ATT_SCALE * LOG2E)
    e = jnp.exp2(-jnp.abs(z2))
    l1 = jnp.minimum(-z2, 0.0) - jnp.log2(1.0 + e)
    lb = l1 + z2
    return z2, e, lb, l1


def _sb_fwd(name, proj, after=None):
    s_len = proj.shape[0]
    t = SB_T
    nq = s_len // t

    def body(q_ref, k_ref, v_ref, *refs):
        o_ref = refs[-1]
        i = pl.program_id(1)
        row = lax.broadcasted_iota(jnp.int32, (t, t), 0)
        col = lax.broadcasted_iota(jnp.int32, (t, t), 1)
        causal = col < row
        after_mat = (row > col).astype(BF16)
        heads = [slice(hh * HEAD_DIM, (hh + 1) * HEAD_DIM) for hh in range(SB_HEADS)]
        q = [q_ref[:, sl].astype(BF16) for sl in heads]

        def tile(kb, state, masked):
            start = pl.multiple_of(kb * t, t)
            out = []
            for hh, sl in enumerate(heads):
                carry, acc = state[hh]
                kblk = k_ref[pl.ds(start, t), sl].astype(BF16)
                vblk = v_ref[pl.ds(start, t), sl].astype(BF16)
                _, _, lb, l1 = _sb_scores(q[hh], kblk)
                if masked:
                    l1 = jnp.where(causal, l1, 0.0)
                hi, lo = _split_bf16(l1)
                after = _dot(hi, after_mat, NN) + _dot(lo, after_mat, NN) + carry
                a = jnp.exp2(lb + after)
                if masked:
                    a = jnp.where(causal, a, 0.0)
                acc = acc + _dot(a.astype(BF16), vblk, NN)
                carry = carry + jnp.sum(l1, axis=-1, keepdims=True)
                out.append((carry, acc))
            return tuple(out)

        zero = (jnp.zeros((t, 1), F32), jnp.zeros((t, HEAD_DIM), F32))
        state = tile(i, (zero,) * SB_HEADS, True)
        state = lax.fori_loop(0, i, lambda n, st: tile(i - 1 - n, st, False), state)
        for hh, sl in enumerate(heads):
            o_ref[:, sl] = state[hh][1]

    cb = SB_HEADS * HEAD_DIM
    return pl.pallas_call(
        body, name=name, grid=(HEADS // SB_HEADS, nq),
        in_specs=[pl.BlockSpec((t, cb), lambda h, i: (i, OFF_QB // cb + h)),
                  pl.BlockSpec((s_len, cb), lambda h, i: (0, OFF_KB // cb + h)),
                  pl.BlockSpec((s_len, cb), lambda h, i: (0, OFF_VB // cb + h))] + ([] if after is None else [ANY]),
        out_specs=pl.BlockSpec((t, cb), lambda h, i: (i, h)),
        out_shape=jax.ShapeDtypeStruct((s_len, D_B), F32),
        compiler_params=_params(("parallel", "arbitrary")),
    )(proj, proj, proj, *([] if after is None else [after]))


def _sb_bwd(name, proj, dy, after=None):
    s_len = proj.shape[0]
    t = SB_T
    nq = s_len // t

    def body(q_ref, k_ref, v_ref, z_ref, dy_ref, *refs):
        dq_ref, dk_ref, dv_ref, a_ref, s_ref = refs[-5:]
        i = pl.program_id(1)

        @pl.when(i == 0)
        def _():
            dk_ref[...] = jnp.zeros_like(dk_ref)
            dv_ref[...] = jnp.zeros_like(dv_ref)

        heads = [slice(hh * HEAD_DIM, (hh + 1) * HEAD_DIM) for hh in range(SB_HEADS)]
        q = [q_ref[:, sl].astype(BF16) for sl in heads]
        silu_z, _ = _silu_and_grad(z_ref[...])
        do_all = dy_ref[...] * silu_z
        do_b = [do_all[:, sl].astype(BF16) for sl in heads]
        row = lax.broadcasted_iota(jnp.int32, (t, t), 0)
        col = lax.broadcasted_iota(jnp.int32, (t, t), 1)
        causal = col < row
        after_mat = (row > col).astype(BF16)
        before_mat = (row < col).astype(BF16)

        def weights(kb, carries, masked):
            start = pl.multiple_of(kb * t, t)
            out = []
            for hh, sl in enumerate(heads):
                kblk = k_ref[pl.ds(start, t), sl].astype(BF16)
                z, _, lb, l1 = _sb_scores(q[hh], kblk)
                if masked:
                    l1 = jnp.where(causal, l1, 0.0)
                hi, lo = _split_bf16(l1)
                after = _dot(hi, after_mat, NN) + _dot(lo, after_mat, NN) + carries[hh]
                a = jnp.exp2(lb + after)
                if masked:
                    a = jnp.where(causal, a, 0.0)
                a_ref[hh, kb] = a
                s_ref[hh, kb] = z
                out.append(carries[hh] + jnp.sum(l1, axis=-1, keepdims=True))
            return tuple(out)

        carries = weights(i, (jnp.zeros((t, 1), F32),) * SB_HEADS, True)
        lax.fori_loop(0, i, lambda n, c: weights(i - 1 - n, c, False), carries)

        def grads(kb, state, masked):
            start = pl.multiple_of(kb * t, t)
            out = []
            for hh, sl in enumerate(heads):
                carry, dq = state[hh]
                kblk = k_ref[pl.ds(start, t), sl].astype(BF16)
                vblk = v_ref[pl.ds(start, t), sl].astype(BF16)
                a = a_ref[hh, kb]
                z = s_ref[hh, kb]
                g = _dot(do_b[hh], vblk, NT) * a
                ghi, glo = _split_bf16(g)
                prefix = _dot(ghi, before_mat, NN) + _dot(glo, before_mat, NN) + carry
                e = jnp.exp2(-jnp.abs(z))
                inv = 1.0 / (1.0 + e)
                pos = z >= 0.0
                beta = jnp.where(pos, inv, e * inv)
                one_m_beta = jnp.where(pos, e * inv, inv)
                dz = (g * one_m_beta - prefix * beta) * ATT_SCALE
                if masked:
                    dz = jnp.where(causal, dz, 0.0)
                dz_b = dz.astype(BF16)
                dq = dq + _dot(dz_b, kblk, NN)
                dk_ref[pl.ds(start, t), sl] += _dot(dz_b, q[hh], TN)
                dv_ref[pl.ds(start, t), sl] += _dot(a.astype(BF16), do_b[hh], TN)
                out.append((carry + jnp.sum(g, axis=-1, keepdims=True), dq))
            return tuple(out)

        zero = (jnp.zeros((t, 1), F32), jnp.zeros((t, HEAD_DIM), F32))
        state = lax.fori_loop(0, i, lambda kb, st: grads(kb, st, False), (zero,) * SB_HEADS)
        state = grads(i, state, True)
        for hh, sl in enumerate(heads):
            dq_ref[:, sl] = state[hh][1]

    cb = SB_HEADS * HEAD_DIM
    qblk = lambda off: pl.BlockSpec((t, cb), lambda h, i: (i, off // cb + h))
    full = lambda off: pl.BlockSpec((s_len, cb), lambda h, i: (0, off // cb + h))
    out = jax.ShapeDtypeStruct((s_len, D_B), F32)
    return pl.pallas_call(
        body, name=name, grid=(HEADS // SB_HEADS, nq),
        in_specs=[qblk(OFF_QB), full(OFF_KB), full(OFF_VB), qblk(OFF_ZB), qblk(OFF_YB)]
        + ([] if after is None else [ANY]),
        out_specs=[qblk(0), full(0), full(0)],
        out_shape=[out, out, out],
        scratch_shapes=[pltpu.VMEM((SB_HEADS, nq, t, t), F32), pltpu.VMEM((SB_HEADS, nq, t, t), F32)],
        compiler_params=_params(("parallel", "arbitrary")),
    )(proj, proj, proj, proj, dy, *([] if after is None else [after]))


MEM_TQ = 512


def _qk_norm(x, g):
    r = lax.rsqrt(jnp.mean(x * x, axis=-1, keepdims=True) + EPS)
    xhat = x * r
    return xhat * g, xhat, r


def _qk_norm_bwd(dn, g, xhat, r):
    dxh = dn * g
    return r * (dxh - xhat * jnp.mean(dxh * xhat, axis=-1, keepdims=True))


def _mem_probs(q, mk, qg, kg):
    qn, qhat, rq = _qk_norm(q, qg)
    kn, khat, rk = _qk_norm(mk, kg)
    qn_b, kn_b = qn.astype(BF16), kn.astype(BF16)
    s = _dot(qn_b, kn_b, NT) * ATT_SCALE
    p = jnp.exp(s - jnp.max(s, axis=-1, keepdims=True))
    p = p / jnp.sum(p, axis=-1, keepdims=True)
    return p, qn_b, kn_b, qhat, rq, khat, rk


def _mem_fwd(name, proj, mem_kv, qg, kg):
    s_len = proj.shape[0]
    m_len = mem_kv.shape[0]
    tq = min(MEM_TQ, s_len)

    def body(q_ref, mk_ref, mv_ref, qg_ref, kg_ref, o_ref):
        p = _mem_probs(q_ref[...], mk_ref[...], qg_ref[...], kg_ref[...])[0]
        o_ref[...] = _dot(p.astype(BF16), mv_ref[...].astype(BF16), NN)

    cb = HEAD_DIM
    vec = pl.BlockSpec((1, cb), lambda h, i: (0, 0))
    return pl.pallas_call(
        body, name=name, grid=(HEADS, s_len // tq),
        in_specs=[pl.BlockSpec((tq, cb), lambda h, i: (i, OFF_QC // cb + h)),
                  pl.BlockSpec((m_len, cb), lambda h, i: (0, h)),
                  pl.BlockSpec((m_len, cb), lambda h, i: (0, HEADS + h)), vec, vec],
        out_specs=pl.BlockSpec((tq, cb), lambda h, i: (i, h)),
        out_shape=jax.ShapeDtypeStruct((s_len, D_C), F32),
        compiler_params=_params(("parallel", "parallel")),
    )(proj, mem_kv, mem_kv, qg, kg)


def _mem_bwd(name, proj, mem_kv, qg, kg, dy):
    s_len = proj.shape[0]
    m_len = mem_kv.shape[0]
    tq = min(MEM_TQ, s_len)

    def body(q_ref, mk_ref, mv_ref, qg_ref, kg_ref, z_ref, dy_ref, dq_ref, dmk_ref, dmv_ref, dqg_ref, dkg_ref):
        h, i = pl.program_id(0), pl.program_id(1)

        @pl.when(i == 0)
        def _():
            dmk_ref[...] = jnp.zeros_like(dmk_ref)
            dmv_ref[...] = jnp.zeros_like(dmv_ref)

        @pl.when((i == 0) & (h == 0))
        def _():
            dqg_ref[...] = jnp.zeros_like(dqg_ref)
            dkg_ref[...] = jnp.zeros_like(dkg_ref)

        qg, kg = qg_ref[...], kg_ref[...]
        p, qn_b, kn_b, qhat, rq, khat, rk = _mem_probs(q_ref[...], mk_ref[...], qg, kg)
        silu_z, _ = _silu_and_grad(z_ref[...])
        do_b = (dy_ref[...] * silu_z).astype(BF16)
        dmv_ref[...] += _dot(p.astype(BF16), do_b, TN)
        dp = _dot(do_b, mv_ref[...].astype(BF16), NT)
        ds = (p * (dp - jnp.sum(dp * p, axis=-1, keepdims=True)) * ATT_SCALE).astype(BF16)
        dqn = _dot(ds, kn_b, NN)
        dkn = _dot(ds, qn_b, TN)
        dq_ref[...] = _qk_norm_bwd(dqn, qg, qhat, rq)
        dmk_ref[...] += _qk_norm_bwd(dkn, kg, khat, rk)
        dqg_ref[...] += jnp.sum(dqn * qhat, axis=0, keepdims=True)
        dkg_ref[...] += jnp.sum(dkn * khat, axis=0, keepdims=True)

    cb = HEAD_DIM
    vec = pl.BlockSpec((1, cb), lambda h, i: (0, 0))
    qblk = lambda off: pl.BlockSpec((tq, cb), lambda h, i: (i, off // cb + h))
    memblk = lambda off: pl.BlockSpec((m_len, cb), lambda h, i: (0, off + h))
    return pl.pallas_call(
        body, name=name, grid=(HEADS, s_len // tq),
        in_specs=[qblk(OFF_QC), memblk(0), memblk(HEADS), vec, vec, qblk(OFF_ZC), qblk(OFF_YC)],
        out_specs=[qblk(0), memblk(0), memblk(0), vec, vec],
        out_shape=[jax.ShapeDtypeStruct((s_len, D_C), F32), jax.ShapeDtypeStruct((m_len, D_C), F32),
                   jax.ShapeDtypeStruct((m_len, D_C), F32), jax.ShapeDtypeStruct((1, cb), F32),
                   jax.ShapeDtypeStruct((1, cb), F32)],
        compiler_params=_params(("arbitrary", "arbitrary")),
    )(proj, mem_kv, mem_kv, qg, kg, proj, dy)


def _sgu_common(u_ref, v_ref, lng_ref, lnb_ref, w_ref, bias_ref):
    ug = _gelu(u_ref[...])
    vg = _gelu(v_ref[...])
    mu = jnp.mean(vg, axis=-1, keepdims=True)
    xc = vg - mu
    rstd = lax.rsqrt(jnp.mean(xc * xc, axis=-1, keepdims=True) + EPS)
    xhat = xc * rstd
    vn = xhat * lng_ref[...] + lnb_ref[...]
    vn_b = vn.astype(BF16)
    row = lax.broadcasted_iota(jnp.int32, (CHUNK, CHUNK), 0)
    col = lax.broadcasted_iota(jnp.int32, (CHUNK, CHUNK), 1)
    tril = row >= col
    mixed = []
    for g in range(A_GROUPS):
        w = jnp.where(tril, w_ref[g], 0.0).astype(BF16)
        sl = slice(g * CHUNK, (g + 1) * CHUNK)
        mixed.append(_dot(w, vn_b[:, sl], NN) + bias_ref[:, sl])
    return ug, xhat, rstd, vn_b, mixed, tril


def _gate_fwd(name, proj, o_b, o_c, lng, lnb, w_s, bias):
    s_len = proj.shape[0]

    def body(u_ref, v_ref, za_ref, zb_ref, zc_ref, ob_ref, oc_ref, lng_ref, lnb_ref, w_ref, bias_ref, y_ref, yt_ref):
        ug, _, _, _, mixed, _ = _sgu_common(u_ref, v_ref, lng_ref, lnb_ref, w_ref, bias_ref)
        sza, _ = _silu_and_grad(za_ref[...])
        gate = ug * sza

        def put(off, width, val):
            y_ref[:, off:off + width] = val.astype(BF16)
            yt_ref[off:off + width, :] = val.T.astype(BF16)

        for g in range(A_GROUPS):
            sl = slice(g * CHUNK, (g + 1) * CHUNK)
            put(g * CHUNK, CHUNK, gate[:, sl] * mixed[g])
        szb, _ = _silu_and_grad(zb_ref[...])
        put(OFF_YB, D_B, ob_ref[...] * szb)
        szc, _ = _silu_and_grad(zc_ref[...])
        put(OFF_YC, D_C, oc_ref[...] * szc)

    wide = lambda off: pl.BlockSpec((CHUNK, D_A), lambda i: (i, off // D_A))
    narrow = lambda off: pl.BlockSpec((CHUNK, D_B), lambda i: (i, off // D_B))
    vec = pl.BlockSpec((1, D_A), lambda i: (0, 0))
    return pl.pallas_call(
        body, name=name, grid=(s_len // CHUNK,),
        in_specs=[wide(OFF_U), wide(OFF_V), wide(OFF_ZA), narrow(OFF_ZB), narrow(OFF_ZC), narrow(0), narrow(0), vec, vec,
                  pl.BlockSpec((A_GROUPS, CHUNK, CHUNK), lambda i: (0, 0, 0)),
                  pl.BlockSpec((CHUNK, D_A), lambda i: (0, 0))],
        out_specs=[pl.BlockSpec((CHUNK, D_MODEL), lambda i: (i, 0)), pl.BlockSpec((D_MODEL, CHUNK), lambda i: (0, i))],
        out_shape=[jax.ShapeDtypeStruct((s_len, D_MODEL), BF16), jax.ShapeDtypeStruct((D_MODEL, s_len), BF16)],
        compiler_params=_params(("parallel",)),
    )(proj, proj, proj, proj, proj, o_b, o_c, lng, lnb, w_s, bias)


def _gate_bwd(name, proj, dy, o_b, o_c, dqkv, dq_c, lng, lnb, w_s, w_s_t, bias):
    s_len = proj.shape[0]
    n = s_len // CHUNK
    dq_b, dk_b, dv_b = dqkv

    def body(u_ref, v_ref, za_ref, zb_ref, zc_ref, dya_ref, dyb_ref, dyc_ref, ob_ref, oc_ref, dq_ref, dk_ref, dv_ref,
             dqc_ref, lng_ref, lnb_ref, w_ref, wt_ref, bias_ref, dp_ref, dw_ref, dsb_ref, dlng_ref, dlnb_ref, dbias_ref):
        i = pl.program_id(0)

        @pl.when(i == 0)
        def _():
            dw_ref[...] = jnp.zeros_like(dw_ref)
            dbias_ref[...] = jnp.zeros_like(dbias_ref)
            dlng_ref[...] = jnp.zeros_like(dlng_ref)
            dlnb_ref[...] = jnp.zeros_like(dlnb_ref)

        ug, xhat, rstd, vn_b, mixed, tril = _sgu_common(u_ref, v_ref, lng_ref, lnb_ref, w_ref, bias_ref)
        za = za_ref[...]
        sza, dsza = _silu_and_grad(za)
        dya = dya_ref[...]
        mixed_all = jnp.concatenate(mixed, axis=-1)
        d_mixed = dya * ug * sza
        dp_ref[:, OFF_U:OFF_U + D_A] = (dya * mixed_all * sza * _gelu_grad(u_ref[...])).astype(BF16)
        dp_ref[:, OFF_ZA:OFF_ZA + D_A] = (dya * ug * mixed_all * dsza).astype(BF16)
        dbias_ref[...] += d_mixed
        dm_b = d_mixed.astype(BF16)
        triu = lax.broadcasted_iota(jnp.int32, (CHUNK, CHUNK), 0) <= lax.broadcasted_iota(jnp.int32, (CHUNK, CHUNK), 1)
        d_vn = []
        for g in range(A_GROUPS):
            sl = slice(g * CHUNK, (g + 1) * CHUNK)
            wt = jnp.where(triu, wt_ref[g], 0.0).astype(BF16)
            d_vn.append(_dot(wt, dm_b[:, sl], NN))
            dw_ref[g] += jnp.where(tril, _dot(dm_b[:, sl], vn_b[:, sl], NT), 0.0)
        d_vn = jnp.concatenate(d_vn, axis=-1)
        dlng_ref[...] += jnp.sum(d_vn * xhat, axis=0, keepdims=True)
        dlnb_ref[...] += jnp.sum(d_vn, axis=0, keepdims=True)
        dxh = d_vn * lng_ref[...]
        d_vg = rstd * (dxh - jnp.mean(dxh, axis=-1, keepdims=True)
                       - xhat * jnp.mean(dxh * xhat, axis=-1, keepdims=True))
        dp_ref[:, OFF_V:OFF_V + D_A] = (d_vg * _gelu_grad(v_ref[...])).astype(BF16)
        dp_ref[:, OFF_QB:OFF_QB + D_B] = dq_ref[...].astype(BF16)
        dp_ref[:, OFF_KB:OFF_KB + D_B] = dk_ref[...].astype(BF16)
        dp_ref[:, OFF_VB:OFF_VB + D_B] = dv_ref[...].astype(BF16)
        _, dszb = _silu_and_grad(zb_ref[...])
        dp_ref[:, OFF_ZB:OFF_ZB + D_B] = (dyb_ref[...] * ob_ref[...] * dszb).astype(BF16)
        dp_ref[:, OFF_QC:OFF_QC + D_C] = dqc_ref[...].astype(BF16)
        _, dszc = _silu_and_grad(zc_ref[...])
        dp_ref[:, OFF_ZC:OFF_ZC + D_C] = (dyc_ref[...] * oc_ref[...] * dszc).astype(BF16)

        @pl.when(i == n - 1)
        def _():
            ch = lax.broadcasted_iota(jnp.int32, (D_A, CHUNK), 0)
            gcol = lax.broadcasted_iota(jnp.int32, (D_A, CHUNK), 1)
            pick = (ch // (D_A // A_GROUPS) == gcol).astype(BF16)
            rest = dbias_ref[...]
            tot = jnp.zeros((CHUNK, CHUNK), F32)
            for _ in range(3):
                term = rest.astype(BF16)
                tot = tot + _dot(term, pick, NN)
                rest = rest - term.astype(F32)
            dsb_ref[...] = tot

    wide = lambda off: pl.BlockSpec((CHUNK, D_A), lambda i: (i, off // D_A))
    narrow = lambda off: pl.BlockSpec((CHUNK, D_B), lambda i: (i, off // D_B))
    vec = pl.BlockSpec((1, D_A), lambda i: (0, 0))
    wspec = pl.BlockSpec((A_GROUPS, CHUNK, CHUNK), lambda i: (0, 0, 0))
    bspec = pl.BlockSpec((CHUNK, D_A), lambda i: (0, 0))
    return pl.pallas_call(
        body, name=name, grid=(n,),
        in_specs=[wide(OFF_U), wide(OFF_V), wide(OFF_ZA), narrow(OFF_ZB), narrow(OFF_ZC),
                  wide(0), narrow(OFF_YB), narrow(OFF_YC), narrow(0), narrow(0), narrow(0), narrow(0), narrow(0),
                  narrow(0), vec, vec, wspec, wspec, bspec],
        out_specs=[pl.BlockSpec((CHUNK, IN_WIDTH), lambda i: (i, 0)), wspec,
                   pl.BlockSpec((CHUNK, CHUNK), lambda i: (0, 0)), vec, vec],
        out_shape=[jax.ShapeDtypeStruct((s_len, IN_WIDTH), BF16), jax.ShapeDtypeStruct((A_GROUPS, CHUNK, CHUNK), F32),
                   jax.ShapeDtypeStruct((CHUNK, CHUNK), F32), jax.ShapeDtypeStruct((1, D_A), F32),
                   jax.ShapeDtypeStruct((1, D_A), F32)],
        scratch_shapes=[pltpu.VMEM((CHUNK, D_A), F32)],
        compiler_params=_params(("arbitrary",)),
    )(proj, proj, proj, proj, proj, dy, dy, dy, o_b, o_c, dq_b, dk_b, dv_b, dq_c, lng, lnb, w_s, w_s_t, bias)


IN_SHARD = IN_WIDTH // N_CHIPS
ROW_SHARD = D_MODEL // N_CHIPS


def _bias_rows(sgu_b_l):
    return jnp.repeat(sgu_b_l.T, D_A // A_GROUPS, axis=1)


class _WholeWeights:
    def __init__(self, w_in_all, w_kv_all, w_out_all):
        self.weights = (w_in_all, w_kv_all, w_out_all)

    def w_in(self, stage, h, proj):
        return (self.weights[0], jnp.arange(N_CHIPS, dtype=jnp.int32), 0, N_CHIPS) if stage == 0 else None

    def rest_start(self, proj):
        return None

    def rest_finish(self, o_b):
        return self.weights[1], self.weights[2], None

    def before_out(self, y):
        return None


def _layer_fwd(l, x, mem, sm, hooks):
    s_len = x.shape[0]
    m_len = mem.shape[0]
    tm = min(1024, s_len)
    h, h_t = _rms_fwd(f"rms_fwd_{l}", x, sm["norm_g"][l][None], min(256, s_len), transposed=True)
    proj, stage = None, 0
    while (ready := hooks.w_in(stage, h, proj)) is not None:
        w_in_all, order, first, count = ready
        proj = _matmul(
            f"in_proj_{l}_{stage}", h, w_in_all, grid=(s_len // tm, count, 1), place=order, into=proj,
            a_spec=pl.BlockSpec((tm, D_MODEL), lambda i, j, k, p: (i, 0)),
            b_spec=pl.BlockSpec((None, D_MODEL, IN_SHARD), lambda i, j, k, p: (p[first + j], 0, 0)),
            o_spec=pl.BlockSpec((tm, IN_SHARD), lambda i, j, k, p: (i, p[first + j])),
            out_shape=jax.ShapeDtypeStruct((s_len, IN_WIDTH), F32), dims=NN)
        stage += 1
    o_b = _sb_fwd(f"sb_fwd_{l}", proj, hooks.rest_start(proj))
    w_kv_all, w_out_all, after = hooks.rest_finish(o_b)
    mem_h = _rms_fwd(f"mem_rms_fwd_{l}", mem, sm["mem_norm_g"][l][None], m_len, after)
    mem_kv = _matmul(
        f"mem_kv_{l}", mem_h, w_kv_all, grid=(1, 2, N_CHIPS),
        a_spec=pl.BlockSpec((m_len, ROW_SHARD), lambda i, j, k: (0, k)),
        b_spec=pl.BlockSpec((None, ROW_SHARD, D_C), lambda i, j, k: (k, 0, j)),
        o_spec=pl.BlockSpec((m_len, D_C), lambda i, j, k: (0, j)),
        out_shape=jax.ShapeDtypeStruct((m_len, 2 * D_C), F32), dims=NN)
    qg, kg = sm["q_norm_g"][l][None], sm["k_norm_g"][l][None]
    o_c = _mem_fwd(f"mem_fwd_{l}", proj, mem_kv, qg, kg)
    bias = _bias_rows(sm["sgu_b"][l])
    y, y_t = _gate_fwd(f"gate_fwd_{l}", proj, o_b, o_c, sm["sgu_ln_g"][l][None], sm["sgu_ln_b"][l][None],
                       sm["sgu_w"][l], bias)
    tn_o = 512
    x_next = _matmul(
        f"out_proj_{l}", y, w_out_all, grid=(s_len // tm, D_MODEL // tn_o, 1),
        a_spec=pl.BlockSpec((tm, D_MODEL), lambda i, j, k: (i, 0)),
        b_spec=pl.BlockSpec((N_CHIPS, ROW_SHARD, tn_o), lambda i, j, k: (0, 0, j)),
        o_spec=pl.BlockSpec((tm, tn_o), lambda i, j, k: (i, j)),
        out_shape=jax.ShapeDtypeStruct((s_len, D_MODEL), F32), dims=NN,
        res=x, res_spec=pl.BlockSpec((tm, tn_o), lambda i, j, k: (i, j)), after=hooks.before_out(y))
    saved = dict(x=x, h_t=h_t, proj=proj, mem_h=mem_h, mem_kv=mem_kv, o_b=o_b, o_c=o_c, y_t=y_t, bias=bias,
                 weights=(w_in_all, w_kv_all, w_out_all))
    return x_next, saved


class _NoExchange:
    def __init__(self):
        self.gave, self.kept = {}, {}

    def start(self, l, group, gives):
        self.gave[l, group] = gives
        return None

    def landed(self, l, group, after):
        return [jnp.zeros_like(g) for g in self.gave[l, group]]

    def send(self, l, group, parts):
        self.kept[l, group] = parts
        return None


def _layer_bwd(l, dxo, dxo_b, mem, sm, saved, place, exchange):
    s_len = dxo.shape[0]
    m_len = mem.shape[0]
    proj, y_t, h_t, mem_h, mem_kv = saved["proj"], saved["y_t"], saved["h_t"], saved["mem_h"], saved["mem_kv"]
    w_in_all, w_kv_all, w_out_all = saved["weights"]
    tm = min(1024, s_len)
    tn = 768
    per = IN_SHARD // tn
    half_rows = ROW_SHARD // 2

    def halves(make):
        give = lambda: make("give", lambda p: 1 - p[1], None, F32)
        keep = lambda theirs: make("keep", lambda p: p[1], theirs, BF16)
        return give, keep

    def grad_out(tag, half, theirs, dtype):
        o_spec = pl.BlockSpec((None, half_rows, 1024), lambda i, j, k, p: (i, 0, j))
        return _matmul(
            f"d_w_out_{l}_{tag}", y_t, dxo_b, grid=(N_CHIPS, D_MODEL // 1024, 1), place=place,
            a_spec=pl.BlockSpec((half_rows, s_len), lambda i, j, k, p: (2 * i + half(p), 0)),
            b_spec=pl.BlockSpec((s_len, 1024), lambda i, j, k, p: (0, j)), o_spec=o_spec,
            out_shape=jax.ShapeDtypeStruct((N_CHIPS, half_rows, D_MODEL), dtype), dims=NN,
            res=theirs, res_spec=o_spec)

    def grad_in(tag, half, theirs, dtype):
        o_spec = pl.BlockSpec((None, D_MODEL // 2, tn), lambda i, j, k, p: (j // per, 0, j % per))
        return _matmul(
            f"d_w_in_{l}_{tag}", h_t, dproj, grid=(1, IN_WIDTH // tn, 1), place=place,
            a_spec=pl.BlockSpec((D_MODEL // 2, s_len), lambda i, j, k, p: (half(p), 0)),
            b_spec=pl.BlockSpec((s_len, tn), lambda i, j, k, p: (0, j)), o_spec=o_spec,
            out_shape=jax.ShapeDtypeStruct((N_CHIPS, D_MODEL // 2, IN_SHARD), dtype), dims=NN,
            res=theirs, res_spec=o_spec)

    def grad_kv(tag, half, theirs, dtype):
        o_spec = pl.BlockSpec((None, half_rows, 2 * D_C), lambda i, j, k, p: (i, 0, 0))
        return _matmul(
            f"d_w_kv_{l}_{tag}", mem_h, dkv_b, grid=(N_CHIPS, 1, 1), place=place,
            a_spec=pl.BlockSpec((m_len, half_rows), lambda i, j, k, p: (0, 2 * i + half(p))),
            b_spec=pl.BlockSpec((m_len, 2 * D_C), lambda i, j, k, p: (0, 0)), o_spec=o_spec,
            out_shape=jax.ShapeDtypeStruct((N_CHIPS, half_rows, 2 * D_C), dtype), dims=TN,
            res=theirs, res_spec=o_spec)

    give_out, keep_out = halves(grad_out)
    token = exchange.start(l, "out", [give_out()])
    dy = _matmul(
        f"d_y_{l}", dxo_b, w_out_all, grid=(s_len // tm, N_CHIPS, 1),
        a_spec=pl.BlockSpec((tm, D_MODEL), lambda i, j, k: (i, 0)),
        b_spec=pl.BlockSpec((None, ROW_SHARD, D_MODEL), lambda i, j, k: (j, 0, 0)),
        o_spec=pl.BlockSpec((tm, ROW_SHARD), lambda i, j, k: (i, j)),
        out_shape=jax.ShapeDtypeStruct((s_len, D_MODEL), F32), dims=NT, after=token)
    (theirs_out,) = exchange.landed(l, "out", dy)
    token = exchange.send(l, "out", [keep_out(theirs_out)])
    qg, kg = sm["q_norm_g"][l][None], sm["k_norm_g"][l][None]
    dqkv = _sb_bwd(f"sb_bwd_{l}", proj, dy, token)
    dq_c, dmk, dmv, dqg, dkg = _mem_bwd(f"mem_bwd_{l}", proj, mem_kv, qg, kg, dy)
    w_s = sm["sgu_w"][l]
    dproj, dws, dbias, dlng, dlnb = _gate_bwd(
        f"gate_bwd_{l}", proj, dy, saved["o_b"], saved["o_c"], dqkv, dq_c, sm["sgu_ln_g"][l][None],
        sm["sgu_ln_b"][l][None], w_s, jnp.swapaxes(w_s, 1, 2), saved["bias"])
    dkv_b = jnp.concatenate([dmk, dmv], axis=1).astype(BF16)
    give_in, keep_in = halves(grad_in)
    give_kv, keep_kv = halves(grad_kv)
    token = exchange.start(l, "in", [give_in(), give_kv()])
    dh = _matmul(
        f"d_h_{l}", dproj, w_in_all, grid=(s_len // tm, D_MODEL // 512, 1),
        a_spec=pl.BlockSpec((tm, IN_WIDTH), lambda i, j, k: (i, 0)),
        b_spec=pl.BlockSpec((N_CHIPS, 512, IN_SHARD), lambda i, j, k: (0, j, 0)),
        o_spec=pl.BlockSpec((tm, 512), lambda i, j, k: (i, j)),
        out_shape=jax.ShapeDtypeStruct((s_len, D_MODEL), F32), dims=NT, after=token, vmem_mb=56)
    theirs_in, theirs_kv = exchange.landed(l, "in", dh)
    token = exchange.send(l, "in", [keep_in(theirs_in), keep_kv(theirs_kv)])
    dx, dx_b, dng = _rms_bwd(f"rms_bwd_{l}", saved["x"], dh, dxo, sm["norm_g"][l][None], min(256, s_len), token)
    d_mem_h = _matmul(
        f"d_mem_h_{l}", dkv_b, w_kv_all, grid=(1, N_CHIPS, 1),
        a_spec=pl.BlockSpec((m_len, 2 * D_C), lambda i, j, k: (0, 0)),
        b_spec=pl.BlockSpec((None, ROW_SHARD, 2 * D_C), lambda i, j, k: (j, 0, 0)),
        o_spec=pl.BlockSpec((m_len, ROW_SHARD), lambda i, j, k: (0, j)),
        out_shape=jax.ShapeDtypeStruct((m_len, D_MODEL), F32), dims=NT)
    dmng = _rms_gain_grad(f"mem_rms_bwd_{l}", mem, d_mem_h)
    dsgu_b = dbias[:, :A_GROUPS].T
    small = dict(norm_g=dng[0], sgu_ln_g=dlng[0], sgu_ln_b=dlnb[0], sgu_w=dws, sgu_b=dsgu_b, mem_norm_g=dmng[0],
                 q_norm_g=dqg[0], k_norm_g=dkg[0])
    return dx, dx_b, small


SMALL_NAMES = ("norm_g", "sgu_ln_g", "sgu_ln_b", "sgu_w", "sgu_b", "mem_norm_g", "q_norm_g", "k_norm_g")


def _local_step(x, mem, target, sm, w_all):
    saved = []
    cur = x
    for l in range(DEPTH):
        cur, sv = _layer_fwd(l, cur, mem, sm, _WholeWeights(*w_all[l]))
        saved.append(sv)
    dxo, dxo_b, loss = _loss_and_grad("loss", cur, target, min(256, x.shape[0]))
    small = [None] * DEPTH
    exchange = _NoExchange()
    place = jnp.zeros((2,), jnp.int32)
    for l in reversed(range(DEPTH)):
        dxo, dxo_b, small[l] = _layer_bwd(l, dxo, dxo_b, mem, sm, saved[l], place, exchange)
    small = {k: jnp.stack([small[l][k] for l in range(DEPTH)]) for k in SMALL_NAMES}
    return loss, dxo, small, exchange.gave, exchange.kept


def _place():
    x, y, c = lax.axis_index("x"), lax.axis_index("y"), lax.axis_index("c")
    return x, y, c


def _other_chips(x, y):
    return [(1 - x, y, 2 * (1 - x) + y), (x, 1 - y, 2 * x + 1 - y), (1 - x, 1 - y, 2 * (1 - x) + 1 - y)]


D2D_CHUNKS = 8


def _place_index():
    return jnp.stack([2 * lax.axis_index("x") + lax.axis_index("y"), lax.axis_index("c")]).astype(jnp.int32)


def _cast_into_slot(name, w, l, place):
    _, rows, cols = w.shape
    tr = min(256, rows)

    def body(p_ref, w_ref, o_ref):
        o_ref[...] = w_ref[...].astype(BF16)

    return pl.pallas_call(
        body, name=name,
        grid_spec=pltpu.PrefetchScalarGridSpec(
            num_scalar_prefetch=1, grid=(rows // tr,),
            in_specs=[pl.BlockSpec((None, tr, cols), lambda i, p: (l, i, 0))],
            out_specs=pl.BlockSpec((None, tr, cols), lambda i, p: (p[0], i, 0))),
        out_shape=jax.ShapeDtypeStruct((N_CHIPS, rows, cols), BF16),
        compiler_params=_params(("parallel",)),
    )(place, w)


HBM = pl.BlockSpec(memory_space=pltpu.HBM)
SEM = pl.BlockSpec(memory_space=pltpu.SEMAPHORE)
DATAFLOW = pltpu.SideEffectType.DATAFLOW_SIDE_EFFECTING


def _in_hbm(a):
    return pltpu.with_memory_space_constraint(a, pltpu.HBM)


ALL_PEERS = (0, 1, 2)
NEIGHBOURS = (0, 1)
DIAGONAL = (2,)


def _chip_copies_start(name, srcs, lands, make_copy, after=None, peers=ALL_PEERS):
    n_t = len(srcs)
    in_place = lands is None
    n_after = 0 if after is None else 1

    def body(*refs):
        src = refs[:n_t]
        k = (n_t if in_place else 2 * n_t) + n_after
        send_sems, recv_sems = refs[k], refs[k + 1]
        land = refs[k + 2:k + 2 + n_t] if in_place else refs[k + 2 + n_t:k + 2 + 2 * n_t]
        token = refs[-1]
        x, y, c = _place()
        me = 2 * x + y
        others = _other_chips(x, y)
        for t in range(n_t):
            for px, py, pk in [others[p] for p in peers]:
                s, d = make_copy(src[t], land[t], me, pk, c)
                pltpu.make_async_remote_copy(
                    src_ref=s, dst_ref=d, send_sem=send_sems.at[t], recv_sem=recv_sems.at[t],
                    device_id=(px, py, c), device_id_type=MESH).start()
        token[...] = jnp.zeros_like(token)

    bufs = list(srcs) if in_place else list(srcs) + list(lands)
    outs = pl.pallas_call(
        body, name=name,
        in_specs=[HBM] * len(bufs) + [ANY] * n_after,
        out_specs=[SEM, SEM] + [HBM] * len(bufs) + [pl.BlockSpec(memory_space=pltpu.VMEM)],
        out_shape=[pltpu.SemaphoreType.DMA((n_t,)), pltpu.SemaphoreType.DMA((n_t,))]
        + [pltpu.HBM(b.shape, b.dtype) for b in bufs] + [jax.ShapeDtypeStruct((8, 128), F32)],
        input_output_aliases={i: 2 + i for i in range(len(bufs))},
        compiler_params=pltpu.CompilerParams(has_side_effects=DATAFLOW),
    )(*[_in_hbm(b) for b in bufs], *([] if after is None else [after]))
    return outs[0], outs[1], list(outs[2:2 + len(bufs)]), outs[-1]


def _chip_copies_wait(name, send_sems, recv_sems, bufs, sent, landed, after):
    n_b = len(bufs)

    def body(*refs):
        buf = refs[:n_b]
        send_ref, recv_ref = refs[n_b], refs[n_b + 1]
        x, y, c = _place()
        for t, (s, d) in enumerate(zip(sent(buf), landed(buf))):
            out = pltpu.make_async_remote_copy(src_ref=s, dst_ref=s, send_sem=send_ref.at[t], recv_sem=recv_ref.at[t],
                                               device_id=(x, y, c), device_id_type=MESH)
            out.wait_send()
            arrived = pltpu.make_async_remote_copy(src_ref=d, dst_ref=d, send_sem=send_ref.at[t],
                                                   recv_sem=recv_ref.at[t], device_id=(x, y, c), device_id_type=MESH)
            arrived.wait_recv()

    after = list(after) if isinstance(after, (list, tuple)) else [after]
    return pl.pallas_call(
        body, name=name,
        in_specs=[HBM] * n_b + [SEM, SEM] + [ANY] * len(after), out_specs=[HBM] * n_b,
        out_shape=[pltpu.HBM(b.shape, b.dtype) for b in bufs],
        input_output_aliases={i: i for i in range(n_b)},
        compiler_params=pltpu.CompilerParams(has_side_effects=DATAFLOW),
    )(*bufs, send_sems, recv_sems, *after)


def _gather_start(name, bufs, after=None, peers=ALL_PEERS):
    def make_copy(src, land, me, pk, c):
        hr = src.shape[1] // 2
        return src.at[me, pl.ds(c * hr, hr)], land.at[me, pl.ds(c * hr, hr)]

    return _chip_copies_start(name, bufs, None, make_copy, after, peers)


def _gather_wait(name, send_sems, recv_sems, bufs, after, peers=ALL_PEERS):
    def half_shards(buf):
        return [b.at[pl.ds(0, len(peers)), pl.ds(0, b.shape[1] // 2)] for b in buf]

    return _chip_copies_wait(name, send_sems, recv_sems, bufs, half_shards, half_shards, after)


def _gather_forward_start(name, bufs, peers=ALL_PEERS):
    n_t = len(bufs)

    def body(*refs):
        mine = refs[:n_t]
        send_sems, recv_sems = refs[n_t], refs[n_t + 1]
        buf = refs[n_t + 2:2 * n_t + 2]
        token = refs[-1]
        x, y, c = _place()
        others = _other_chips(x, y)
        for q in range(D2D_CHUNKS):
            for t in range(n_t):
                hr = mine[t].shape[1] // 2
                cr = hr // D2D_CHUNKS
                rows = pl.ds(c * hr + q * cr, cr)
                for _, _, pk in [others[p] for p in peers]:
                    pltpu.make_async_remote_copy(
                        src_ref=mine[t].at[pk, rows], dst_ref=buf[t].at[pk, rows], send_sem=send_sems.at[t],
                        recv_sem=recv_sems.at[t], device_id=(x, y, 1 - c), device_id_type=MESH).start()
        token[...] = jnp.zeros_like(token)

    outs = pl.pallas_call(
        body, name=name,
        in_specs=[HBM] * n_t,
        out_specs=[SEM, SEM] + [HBM] * n_t + [pl.BlockSpec(memory_space=pltpu.VMEM)],
        out_shape=[pltpu.SemaphoreType.DMA((n_t,)), pltpu.SemaphoreType.DMA((n_t,))]
        + [pltpu.HBM(b.shape, b.dtype) for b in bufs] + [jax.ShapeDtypeStruct((8, 128), F32)],
        input_output_aliases={i: 2 + i for i in range(n_t)},
        compiler_params=pltpu.CompilerParams(has_side_effects=DATAFLOW),
    )(*[_in_hbm(b) for b in bufs])
    return outs[0], outs[1], list(outs[2:2 + n_t]), outs[-1]


def _core_exchange_start(name, grads):
    n_t = len(grads)
    lands = [lax.empty(g.shape, g.dtype) for g in grads]

    def body(*refs):
        src = refs[:n_t]
        send_sems, recv_sems = refs[2 * n_t], refs[2 * n_t + 1]
        land = refs[2 * n_t + 2 + n_t:2 * n_t + 2 + 2 * n_t]
        token = refs[-1]
        x, y, c = _place()
        for q in range(D2D_CHUNKS):
            for t in range(n_t):
                cr = src[t].shape[1] // D2D_CHUNKS
                rows = pl.ds(q * cr, cr)
                pltpu.make_async_remote_copy(
                    src_ref=src[t].at[:, rows], dst_ref=land[t].at[:, rows], send_sem=send_sems.at[t],
                    recv_sem=recv_sems.at[t], device_id=(x, y, 1 - c), device_id_type=MESH).start()
        token[...] = jnp.zeros_like(token)

    bufs = list(grads) + lands
    outs = pl.pallas_call(
        body, name=name,
        in_specs=[HBM] * len(bufs),
        out_specs=[SEM, SEM] + [HBM] * len(bufs) + [pl.BlockSpec(memory_space=pltpu.VMEM)],
        out_shape=[pltpu.SemaphoreType.DMA((n_t,)), pltpu.SemaphoreType.DMA((n_t,))]
        + [pltpu.HBM(b.shape, b.dtype) for b in bufs] + [jax.ShapeDtypeStruct((8, 128), F32)],
        input_output_aliases={i: 2 + i for i in range(len(bufs))},
        compiler_params=pltpu.CompilerParams(has_side_effects=DATAFLOW),
    )(*[_in_hbm(b) for b in bufs])
    return outs[0], outs[1], list(outs[2:2 + len(bufs)]), outs[-1]


def _core_exchange_wait(name, send_sems, recv_sems, bufs, after):
    n_t = len(bufs) // 2

    def body(*refs):
        land = refs[n_t:2 * n_t]
        send_ref, recv_ref = refs[2 * n_t], refs[2 * n_t + 1]
        x, y, c = _place()
        for t in range(n_t):
            whole = pltpu.make_async_remote_copy(src_ref=land[t], dst_ref=land[t], send_sem=send_ref.at[t],
                                                 recv_sem=recv_ref.at[t], device_id=(x, y, c), device_id_type=MESH)
            whole.wait_send()
            whole.wait_recv()

    outs = pl.pallas_call(
        body, name=name,
        in_specs=[HBM] * (2 * n_t) + [SEM, SEM, ANY], out_specs=[HBM] * (2 * n_t),
        out_shape=[pltpu.HBM(b.shape, b.dtype) for b in bufs],
        input_output_aliases={i: i for i in range(2 * n_t)},
        compiler_params=pltpu.CompilerParams(has_side_effects=DATAFLOW),
    )(*bufs, send_sems, recv_sems, after)
    return list(outs[:n_t]), list(outs[n_t:])


def _chip_exchange_start(name, parts):
    lands = [lax.empty(p.shape, p.dtype) for p in parts]
    return _chip_copies_start(name, parts, lands, lambda src, land, me, pk, c: (src.at[pk], land.at[me]))


def _chip_exchange_wait(name, send_sems, recv_sems, bufs, after):
    n_t = len(bufs) // 2
    return _chip_copies_wait(name, send_sems, recv_sems, bufs,
                             lambda buf: [b.at[pl.ds(0, 3)] for b in buf[:n_t]],
                             lambda buf: [b.at[pl.ds(0, 3)] for b in buf[n_t:]], after)


def _sum_chips(name, parts, landed, place, l, stacked):
    chips, rows, cols = landed.shape
    tr = min(256, rows)
    per = rows // tr

    def body(p_ref, own_ref, *refs):
        land, o_ref = refs[:chips], refs[-1]
        tot = None
        for k in range(chips):
            term = jnp.where(p_ref[0] == k, own_ref[...], land[k][...]).astype(F32)
            tot = term if tot is None else tot + term
        o_ref[...] = tot

    def from_chip(k):
        return pl.BlockSpec((None, tr, cols), lambda i, p: (jnp.where(p[0] == k, (k + 1) % chips, k), i, 0))

    in_specs = [pl.BlockSpec((None, tr, cols), lambda i, p: (p[0], i, 0))] + [from_chip(k) for k in range(chips)]
    args = [parts] + [landed] * chips
    aliases = {}
    if stacked is not None:
        in_specs.append(ANY)
        args.append(stacked)
        aliases = {len(args): 0}
    return pl.pallas_call(
        body, name=name,
        grid_spec=pltpu.PrefetchScalarGridSpec(
            num_scalar_prefetch=1, grid=(per,), in_specs=in_specs,
            out_specs=pl.BlockSpec((None, tr, cols), lambda i, p: (l, p[1] * per + i, 0))),
        out_shape=jax.ShapeDtypeStruct((DEPTH, 2 * rows, cols), F32), input_output_aliases=aliases,
        compiler_params=_params(("parallel",)),
    )(place, *args)


def _core_share_start(name, bufs, l):
    n_t = len(bufs)

    def body(*refs):
        mine = refs[:n_t]
        send_sems, recv_sems = refs[n_t], refs[n_t + 1]
        buf = refs[n_t + 2:2 * n_t + 2]
        token = refs[-1]
        x, y, c = _place()
        for q in range(D2D_CHUNKS):
            for t in range(n_t):
                hr = mine[t].shape[1] // 2
                cr = hr // D2D_CHUNKS
                rows = pl.ds(c * hr + q * cr, cr)
                pltpu.make_async_remote_copy(
                    src_ref=mine[t].at[l, rows], dst_ref=buf[t].at[l, rows], send_sem=send_sems.at[t],
                    recv_sem=recv_sems.at[t], device_id=(x, y, 1 - c), device_id_type=MESH).start()
        token[...] = jnp.zeros_like(token)

    outs = pl.pallas_call(
        body, name=name,
        in_specs=[HBM] * n_t,
        out_specs=[SEM, SEM] + [HBM] * n_t + [pl.BlockSpec(memory_space=pltpu.VMEM)],
        out_shape=[pltpu.SemaphoreType.DMA((n_t,)), pltpu.SemaphoreType.DMA((n_t,))]
        + [pltpu.HBM(b.shape, b.dtype) for b in bufs] + [jax.ShapeDtypeStruct((8, 128), F32)],
        input_output_aliases={i: 2 + i for i in range(n_t)},
        compiler_params=pltpu.CompilerParams(has_side_effects=DATAFLOW),
    )(*[_in_hbm(b) for b in bufs])
    return outs[0], outs[1], list(outs[2:2 + n_t]), outs[-1]


def _core_share_wait(name, send_sems, recv_sems, bufs, l, after):
    def half_layer(buf):
        return [b.at[l, pl.ds(0, b.shape[1] // 2)] for b in buf]

    return _chip_copies_wait(name, send_sems, recv_sems, bufs, half_layer, half_layer, after)


def _all_reduce_small(vec, after=None):
    rows, lanes = vec.shape
    hr = rows // 2

    def body(v_ref, *refs):
        o_ref, sib_ref, chips_ref, send_sems, recv_sems = refs[-5:]
        x, y, c = _place()
        me = 2 * x + y
        sibling = (x, y, 1 - c)
        mine = pl.ds(pl.multiple_of(c * hr, 8), hr)
        theirs = pl.ds(pl.multiple_of((1 - c) * hr, 8), hr)
        swap = pltpu.make_async_remote_copy(
            src_ref=v_ref.at[theirs], dst_ref=sib_ref, send_sem=send_sems.at[0], recv_sem=recv_sems.at[0],
            device_id=sibling, device_id_type=MESH)
        swap.start()
        swap.wait_recv()
        chips_ref[me] = v_ref[mine] + sib_ref[...]
        copies = []
        for j, (px, py, pk) in enumerate(_other_chips(x, y)):
            cp = pltpu.make_async_remote_copy(
                src_ref=chips_ref.at[me], dst_ref=chips_ref.at[me], send_sem=send_sems.at[1 + j],
                recv_sem=recv_sems.at[1 + j], device_id=(px, py, c), device_id_type=MESH)
            cp.start()
            copies.append(cp)
        for j, (px, py, pk) in enumerate(_other_chips(x, y)):
            pltpu.make_async_remote_copy(
                src_ref=chips_ref.at[pk], dst_ref=chips_ref.at[pk], send_sem=send_sems.at[1 + j],
                recv_sem=recv_sems.at[1 + j], device_id=(px, py, c), device_id_type=MESH).wait_recv()
        tot = chips_ref[0]
        for k in range(1, N_CHIPS):
            tot = tot + chips_ref[k]
        o_ref[mine] = tot
        share = pltpu.make_async_remote_copy(
            src_ref=o_ref.at[mine], dst_ref=o_ref.at[mine], send_sem=send_sems.at[4], recv_sem=recv_sems.at[4],
            device_id=sibling, device_id_type=MESH)
        share.start()
        pltpu.make_async_remote_copy(
            src_ref=o_ref.at[theirs], dst_ref=o_ref.at[theirs], send_sem=send_sems.at[4], recv_sem=recv_sems.at[4],
            device_id=sibling, device_id_type=MESH).wait_recv()
        swap.wait_send()
        for cp in copies:
            cp.wait_send()
        share.wait_send()

    vm = pl.BlockSpec(memory_space=pltpu.VMEM)
    return pl.pallas_call(
        body, name="small_all_reduce", in_specs=[vm] + ([] if after is None else [ANY]), out_specs=vm,
        out_shape=jax.ShapeDtypeStruct((rows, lanes), F32),
        scratch_shapes=[pltpu.VMEM((hr, lanes), F32), pltpu.VMEM((N_CHIPS, hr, lanes), F32),
                        pltpu.SemaphoreType.DMA((5,)), pltpu.SemaphoreType.DMA((5,))],
        compiler_params=pltpu.CompilerParams(has_side_effects=True, vmem_limit_bytes=48 * MIB),
    )(vec, *([] if after is None else [after]))


def _adamw(name, w, g, m, v, place, l=0, half=None, done=None, after=None):
    layers, rows, cols = w.shape
    span = rows if half is None else rows // 2
    tr = span
    for cand in (256, 128, 64, 32, 16, 8):
        if span % cand == 0:
            tr = cand
            break
    per = span // tr
    c1 = 1.0 - ADAM_B1 ** ADAM_STEP
    c2 = 1.0 - ADAM_B2 ** ADAM_STEP

    def first_block(p):
        return 0 if half is None else (p[1] if half == "own" else 1 - p[1]) * per

    def body(p_ref, w_ref, g_ref, m_ref, v_ref, *refs):
        go_ref, d_ref, nm_ref, nv_ref = refs[-4:]
        gv = g_ref[...]
        nm = ADAM_B1 * m_ref[...] + (1.0 - ADAM_B1) * gv
        nv = ADAM_B2 * v_ref[...] + (1.0 - ADAM_B2) * (gv * gv)
        go_ref[...] = gv
        nm_ref[...] = nm
        nv_ref[...] = nv
        d_ref[...] = -ADAM_LR * ((nm / c1) / (jnp.sqrt(nv / c2) + ADAM_EPS) + ADAM_WD * w_ref[...])

    blk = pl.BlockSpec((None, tr, cols), lambda i, p: (l, first_block(p) + i, 0))
    out = jax.ShapeDtypeStruct((layers, rows, cols), F32)
    extra = ([] if done is None else list(done)) + ([] if after is None else [after])
    aliases = {} if done is None else {5 + i: i for i in range(4)}
    return pl.pallas_call(
        body, name=name,
        grid_spec=pltpu.PrefetchScalarGridSpec(
            num_scalar_prefetch=1, grid=(per,), in_specs=[blk] * 4 + [ANY] * len(extra), out_specs=[blk] * 4),
        out_shape=[out] * 4, input_output_aliases=aliases,
        compiler_params=_params(("parallel",)),
    )(place, w, g, m, v, *extra)


LANES = 128
SUBLANES = 8
SMALL_SHAPES = {
    "norm_g": (DEPTH, D_MODEL), "sgu_ln_g": (DEPTH, D_A), "sgu_ln_b": (DEPTH, D_A),
    "sgu_w": (DEPTH, A_GROUPS, CHUNK, CHUNK), "sgu_b": (DEPTH, A_GROUPS, CHUNK), "mem_norm_g": (DEPTH, D_MODEL),
    "q_norm_g": (DEPTH, HEAD_DIM), "k_norm_g": (DEPTH, HEAD_DIM)}


def _small_layout():
    at, off = {}, 0
    for k in SMALL_NAMES:
        n = math.prod(SMALL_SHAPES[k]) // LANES
        at[k] = (off, n)
        off += -(-n // SUBLANES) * SUBLANES
    return at, off, -(-(off + SUBLANES) // (2 * SUBLANES)) * 2 * SUBLANES


def _pack_small(parts, loss=None):
    at, loss_row, rows = _small_layout()
    pieces = []
    for k in SMALL_NAMES:
        n = at[k][1]
        pieces.append(jnp.pad(parts[k].reshape(n, LANES), ((0, -(-n // SUBLANES) * SUBLANES - n), (0, 0))))
    tile = jnp.zeros((SUBLANES, LANES), F32) if loss is None else jnp.broadcast_to(loss.reshape(1, 1), (SUBLANES, LANES))
    pieces += [tile, jnp.zeros((rows - loss_row - SUBLANES, LANES), F32)]
    return jnp.concatenate(pieces)


def _adamw_small(w, g, m, v):
    at, _, rows = _small_layout()
    c1 = 1.0 - ADAM_B1 ** ADAM_STEP
    c2 = 1.0 - ADAM_B2 ** ADAM_STEP
    n_names = len(SMALL_NAMES)

    def body(w_ref, g_ref, m_ref, v_ref, *refs):
        outs, (d_ref, nm_ref, nv_ref) = refs[:4 * n_names], refs[4 * n_names:]
        gv = g_ref[...]
        nm = ADAM_B1 * m_ref[...] + (1.0 - ADAM_B1) * gv
        nv = ADAM_B2 * v_ref[...] + (1.0 - ADAM_B2) * (gv * gv)
        nm_ref[...] = nm
        nv_ref[...] = nv
        d_ref[...] = -ADAM_LR * ((nm / c1) / (jnp.sqrt(nv / c2) + ADAM_EPS) + ADAM_WD * w_ref[...])
        for kind, src in enumerate((g_ref, d_ref, nm_ref, nv_ref)):
            for i, k in enumerate(SMALL_NAMES):
                o_ref = outs[kind * n_names + i]
                first, n = at[k]
                shape = SMALL_SHAPES[k]
                if shape[-1] == LANES:
                    o_ref[...] = src[pl.ds(first, n), :].reshape(shape)
                else:
                    per = shape[-1] // LANES
                    for r in range(n):
                        o_ref[pl.ds(r // per, 1), pl.ds((r % per) * LANES, LANES)] = src[pl.ds(first + r, 1), :]

    out_shape = [jax.ShapeDtypeStruct(SMALL_SHAPES[k], F32) for _ in range(4) for k in SMALL_NAMES]
    outs = pl.pallas_call(
        body, name="adamw_small", out_shape=out_shape,
        scratch_shapes=[pltpu.VMEM((rows, LANES), F32)] * 3, compiler_params=_params(None),
    )(w, g, m, v)
    return [dict(zip(SMALL_NAMES, outs[kind * n_names:(kind + 1) * n_names])) for kind in range(4)]


WEIGHT_ORDER = ("norm_g", "w_in", "sgu_ln_g", "sgu_ln_b", "sgu_w", "sgu_b", "mem_norm_g", "w_mem_kv", "q_norm_g",
                "k_norm_g", "w_out")


def kernel(x, mem, norm_g, w_in, sgu_ln_g, sgu_ln_b, sgu_w, sgu_b, mem_norm_g, w_mem_kv, q_norm_g, k_norm_g, w_out, loss_target, m_norm_g, m_w_in, m_sgu_ln_g, m_sgu_ln_b, m_sgu_w, m_sgu_b, m_mem_norm_g, m_w_mem_kv, m_q_norm_g, m_k_norm_g, m_w_out, v_norm_g, v_w_in, v_sgu_ln_g, v_sgu_ln_b, v_sgu_w, v_sgu_b, v_mem_norm_g, v_w_mem_kv, v_q_norm_g, v_k_norm_g, v_w_out):
    weights = dict(norm_g=norm_g, w_in=w_in, sgu_ln_g=sgu_ln_g, sgu_ln_b=sgu_ln_b, sgu_w=sgu_w, sgu_b=sgu_b,
                   mem_norm_g=mem_norm_g, w_mem_kv=w_mem_kv, q_norm_g=q_norm_g, k_norm_g=k_norm_g, w_out=w_out)
    mom_m = dict(norm_g=m_norm_g, w_in=m_w_in, sgu_ln_g=m_sgu_ln_g, sgu_ln_b=m_sgu_ln_b, sgu_w=m_sgu_w, sgu_b=m_sgu_b,
                 mem_norm_g=m_mem_norm_g, w_mem_kv=m_w_mem_kv, q_norm_g=m_q_norm_g, k_norm_g=m_k_norm_g, w_out=m_w_out)
    mom_v = dict(norm_g=v_norm_g, w_in=v_w_in, sgu_ln_g=v_sgu_ln_g, sgu_ln_b=v_sgu_ln_b, sgu_w=v_sgu_w, sgu_b=v_sgu_b,
                 mem_norm_g=v_mem_norm_g, w_mem_kv=v_w_mem_kv, q_norm_g=v_q_norm_g, k_norm_g=v_k_norm_g, w_out=v_w_out)
    big = ("w_in", "w_mem_kv", "w_out")
    sm = {k: weights[k] for k in SMALL_NAMES}

    place = _place_index()
    xs, mems, target = x[0], mem[0], loss_target[0]

    slots = [[_cast_into_slot(f"cast_{k}_{l}", weights[k], l, place) for k in big] for l in range(DEPTH)]
    saved = [None] * DEPTH

    chips, cores = {}, {}
    me = place[0]
    arrival = jnp.stack([me, me ^ 2, me ^ 1, 3 - me]).astype(jnp.int32)
    shard_order = jnp.arange(N_CHIPS, dtype=jnp.int32)

    def start_gather(l, after=None):
        chips[l, "in"] = _gather_start(f"gather_start_{l}_in", slots[l][:1], after)
        chips[l, "rest"] = _gather_start(f"gather_start_{l}_rest", slots[l][1:], chips[l, "in"][3])
        return chips[l, "rest"][3]

    def hand_to_sibling(l, group, after):
        send_sems, recv_sems, bufs, _ = chips[l, group]
        bufs = _gather_wait(f"gather_wait_{l}_{group}", send_sems, recv_sems, bufs, after)
        cores[l, group] = _gather_forward_start(f"gather_forward_{l}_{group}", bufs)
        return cores[l, group][3]

    def whole(l, group, after):
        send_sems, recv_sems, bufs, _ = cores[l, group]
        return _gather_wait(f"gather_whole_{l}_{group}", send_sems, recv_sems, bufs, after)

    later_slots = [s for layer in slots[1:] for s in layer]

    class Gathered:
        def __init__(self, l):
            self.l = l
            self.buf = None

        def landed_from(self, tag, peers, after, behind, then=None):
            send_sems, recv_sems, _, _ = chips[0, "in_" + tag]
            buf = _gather_wait(f"gather_wait_0_in_{tag}", send_sems, recv_sems, self.buf, after, peers)
            if then is not None:
                buf = then(buf)
            send_sems, recv_sems, buf, token = _gather_forward_start(f"gather_forward_0_in_{tag}", buf, peers)
            self.buf = _gather_wait(f"gather_whole_0_in_{tag}", send_sems, recv_sems, buf, [token] + behind, peers)

        def w_in(self, stage, h, proj):
            if self.l > 0:
                return (whole(self.l, "in", h)[0], shard_order, 0, N_CHIPS) if stage == 0 else None
            if stage == 0:
                self.buf = chips[0, "in_n"][2]
                return self.buf[0], arrival, 0, 1
            if stage == 1:
                def start_others(buf):
                    chips[0, "in_d"] = _gather_start("gather_start_0_in_d", buf, None, DIAGONAL)
                    chips[0, "rest"] = _gather_start("gather_start_0_rest", slots[0][1:], chips[0, "in_d"][3])
                    return chips[0, "in_d"][2]

                self.landed_from("n", NEIGHBOURS, proj, later_slots + [chips[0, "in_n"][3]], start_others)
                return self.buf[0], arrival, 1, 2
            if stage == 2:
                self.landed_from("d", DIAGONAL, [proj, chips[0, "rest"][3]], [])
                return self.buf[0], arrival, 3, 1
            return None

        def rest_start(self, proj):
            token = hand_to_sibling(self.l, "rest", proj)
            return start_gather(self.l + 1, token) if self.l + 1 < DEPTH else token

        def rest_finish(self, o_b):
            w_kv_all, w_out_all = whole(self.l, "rest", o_b)
            return w_kv_all, w_out_all, None

        def before_out(self, y):
            return hand_to_sibling(self.l + 1, "in", y) if self.l + 1 < DEPTH else None

    chips[0, "in_n"] = _gather_start("gather_start_0_in_n", slots[0][:1], None, NEIGHBOURS)
    cur = xs
    for l in range(DEPTH):
        cur, saved[l] = _layer_fwd(l, cur, mems, sm, Gathered(l))
    dxo, dxo_b, loss_part = _loss_and_grad("loss", cur, target, min(256, xs.shape[0]))

    small_g = [None] * DEPTH
    flight = {}

    class Exchange:
        def __init__(self):
            self.cores = {}

        def start(self, l, group, gives):
            *self.cores[l, group], token = _core_exchange_start(f"grad_core_start_{l}_{group}", gives)
            return token

        def landed(self, l, group, after):
            send_sems, recv_sems, bufs = self.cores[l, group]
            return _core_exchange_wait(f"grad_core_wait_{l}_{group}", send_sems, recv_sems, bufs, after)[1]

        def send(self, l, group, parts):
            *flight[l, group], token = _chip_exchange_start(f"grad_chip_start_{l}_{group}", parts)
            return token

    exchange = Exchange()
    for l in reversed(range(DEPTH)):
        dxo, dxo_b, small_g[l] = _layer_bwd(l, dxo, dxo_b, mems, sm, saved[l], place, exchange)
    grad_x = dxo

    groups = (("out", ("w_out",)), ("in", ("w_in", "w_mem_kv")))
    halves, stepped = dict.fromkeys(big), dict.fromkeys(big)
    small_g = {k: jnp.stack([small_g[l][k] for l in range(DEPTH)]) for k in SMALL_NAMES}
    after = grad_x
    sharing = {}

    def reduce_group(l, group, names):
        nonlocal after
        send_sems, recv_sems, bufs = flight[l, group]
        bufs = _chip_exchange_wait(f"grad_chip_wait_{l}_{group}", send_sems, recv_sems, bufs, after)
        for t, k in enumerate(names):
            halves[k] = _sum_chips(f"grad_chip_sum_{l}_{k}", bufs[t], bufs[len(names) + t], place, l, halves[k])
        *sharing[l, group], after = _core_share_start(f"grad_core_share_{l}_{group}", [halves[k] for k in names], l)

    def step(l, k, buf, half):
        nonlocal after
        tag = "" if half is None else "_" + half
        stepped[k] = _adamw(f"adamw_{k}_{l}{tag}", weights[k], buf, mom_m[k], mom_v[k], place, l, half, stepped[k],
                            after)
        after = stepped[k][1]

    def step_group(l, group, names, overlap):
        nonlocal after
        send_sems, recv_sems, bufs = sharing[l, group]
        if overlap:
            for k, buf in zip(names, bufs):
                step(l, k, buf, "own")
        bufs = _core_share_wait(f"grad_core_shared_{l}_{group}", send_sems, recv_sems, bufs, l, after)
        for k, buf in zip(names, bufs):
            halves[k] = buf
            step(l, k, buf, "other" if overlap else None)

    for l in reversed(range(DEPTH)):
        last = l == 0
        (g_out, n_out), (g_in, n_in) = groups
        reduce_group(l, g_out, n_out)
        if last:
            step_group(l, g_out, n_out, False)
            small_sum = _all_reduce_small(_pack_small(small_g, loss_part), after)
            small_step = _adamw_small(_pack_small(sm), small_sum, _pack_small({k: mom_m[k] for k in SMALL_NAMES}),
                                      _pack_small({k: mom_v[k] for k in SMALL_NAMES}))
            after = small_step[1]["sgu_w"]
        reduce_group(l, g_in, n_in)
        if not last:
            step_group(l, g_out, n_out, False)
        step_group(l, g_in, n_in, last)

    grads, delta, new_m, new_v = ({k: stepped[k][i] for k in big} for i in range(4))
    for out, small in zip((grads, delta, new_m, new_v), small_step):
        out.update(small)
    loss = small_sum[_small_layout()[1], 0]
    return (loss, grad_x[None], *[grads[k] for k in WEIGHT_ORDER], *[delta[k] for k in WEIGHT_ORDER],
            *[new_m[k] for k in WEIGHT_ORDER], *[new_v[k] for k in WEIGHT_ORDER])
```

```python
import functools
import math

import jax
import jax.numpy as jnp
from jax import lax
from jax.experimental import pallas as pl
from jax.experimental.pallas import tpu as pltpu

F32 = jnp.float32
BF16 = jnp.bfloat16
MESH = pl.DeviceIdType.MESH

D_MODEL = 2048
DEPTH = 2
CHUNK = 128
D_A = 1024
A_GROUPS = 8
D_B = 512
D_C = 512
HEADS = 4
HEAD_DIM = 128
IN_WIDTH = 6144
N_CHIPS = 4
EPS = 1e-6
ATT_SCALE = 1.0 / math.sqrt(HEAD_DIM)

OFF_U, OFF_V, OFF_ZA = 0, 1024, 2048
OFF_QB, OFF_KB, OFF_VB, OFF_ZB = 3072, 3584, 4096, 4608
OFF_QC, OFF_ZC = 5120, 5632
OFF_YB, OFF_YC = 1024, 1536

ADAM_LR = 0.001
ADAM_B1 = 0.9
ADAM_B2 = 0.999
ADAM_EPS = 1e-08
ADAM_WD = 0.01
ADAM_STEP = 10

MIB = 1024 * 1024
ANY = pl.BlockSpec(memory_space=pl.ANY)


def _params(semantics=None, vmem_mb=48):
    return pltpu.CompilerParams(dimension_semantics=semantics, vmem_limit_bytes=vmem_mb * MIB)


def _gelu(x):
    return 0.5 * x * (1.0 + lax.erf(x * (1.0 / math.sqrt(2.0))))


def _gelu_grad(x):
    cdf = 0.5 * (1.0 + lax.erf(x * (1.0 / math.sqrt(2.0))))
    pdf = jnp.exp(-0.5 * x * x) * (1.0 / math.sqrt(2.0 * math.pi))
    return cdf + x * pdf


def _sigmoid(x):
    return 1.0 / (1.0 + jnp.exp(-x))


def _silu_and_grad(z):
    s = _sigmoid(z)
    return z * s, s * (1.0 + z * (1.0 - s))


def _split_bf16(x):
    hi = x.astype(BF16)
    lo = (x - hi.astype(F32)).astype(BF16)
    return hi, lo


def _dot(a, b, dims):
    return lax.dot_general(a, b, (dims, ((), ())), preferred_element_type=F32)


NN = ((1,), (0,))
NT = ((1,), (1,))
TN = ((0,), (0,))


def _matmul(name, a, b, *, grid, a_spec, b_spec, o_spec, out_shape, dims, res=None, res_spec=None, after=None,
            place=None, into=None, vmem_mb=48):
    nk = grid[2]
    n_in = 2 + (res is not None) + (after is not None) + (into is not None)

    def body(*refs):
        if place is not None:
            refs = refs[1:]
        a_ref, b_ref = refs[0], refs[1]
        r_ref = refs[2] if res is not None else None
        o_ref = refs[n_in]
        if len(b_ref.shape) == 3 and dims == NN:
            part = _dot(a_ref[...], b_ref[...].reshape(-1, b_ref.shape[-1]), dims)
        elif len(b_ref.shape) == 3:
            width = b_ref.shape[-1]
            part = None
            for s in range(b_ref.shape[0]):
                term = _dot(a_ref[:, s * width:(s + 1) * width], b_ref[s], dims)
                part = term if part is None else part + term
        else:
            part = _dot(a_ref[...], b_ref[...], dims)
        if nk == 1:
            if r_ref is not None:
                part = part + r_ref[...]
            o_ref[...] = part.astype(o_ref.dtype)
            return
        acc_ref = refs[n_in + 1]
        k = pl.program_id(2)

        @pl.when(k == 0)
        def _():
            acc_ref[...] = part

        @pl.when(k > 0)
        def _():
            acc_ref[...] += part

        @pl.when(k == nk - 1)
        def _():
            tot = acc_ref[...]
            if r_ref is not None:
                tot = tot + r_ref[...]
            o_ref[...] = tot.astype(o_ref.dtype)

    in_specs = [a_spec, b_spec]
    args = [a, b]
    if res is not None:
        in_specs.append(res_spec)
        args.append(res)
    if after is not None:
        in_specs.append(ANY)
        args.append(after)
    aliases = {}
    if into is not None:
        in_specs.append(ANY)
        args.append(into)
        aliases = {len(args) - 1 + (place is not None): 0}
    acc_shape = tuple(d for d in o_spec.block_shape if d is not None)
    scratch = [pltpu.VMEM(acc_shape, F32)] if nk > 1 else []
    params = _params(("parallel", "parallel", "arbitrary"), vmem_mb)
    if place is not None:
        return pl.pallas_call(
            body, name=name, out_shape=out_shape, compiler_params=params, input_output_aliases=aliases,
            grid_spec=pltpu.PrefetchScalarGridSpec(num_scalar_prefetch=1, grid=grid, in_specs=in_specs,
                                                   out_specs=o_spec, scratch_shapes=scratch),
        )(place, *args)
    return pl.pallas_call(
        body, name=name, grid=grid, in_specs=in_specs, out_specs=o_spec, out_shape=out_shape,
        scratch_shapes=scratch, compiler_params=params, input_output_aliases=aliases,
    )(*args)


def _rms_fwd(name, x, g, tr, after=None, transposed=False):
    rows, d = x.shape

    def body(x_ref, g_ref, *refs):
        outs = refs[1:] if after is not None else refs
        xv = x_ref[...]
        r = lax.rsqrt(jnp.mean(xv * xv, axis=-1, keepdims=True) + EPS)
        h = xv * r * g_ref[...]
        outs[0][...] = h.astype(BF16)
        if transposed:
            outs[1][...] = h.T.astype(BF16)

    out_specs = [pl.BlockSpec((tr, d), lambda i: (i, 0))]
    out_shape = [jax.ShapeDtypeStruct((rows, d), BF16)]
    if transposed:
        out_specs.append(pl.BlockSpec((d, tr), lambda i: (0, i)))
        out_shape.append(jax.ShapeDtypeStruct((d, rows), BF16))
    outs = pl.pallas_call(
        body, name=name, grid=(rows // tr,),
        in_specs=[pl.BlockSpec((tr, d), lambda i: (i, 0)), pl.BlockSpec((1, d), lambda i: (0, 0))]
        + ([] if after is None else [ANY]),
        out_specs=out_specs, out_shape=out_shape,
        compiler_params=_params(("parallel",)),
    )(x, g, *([] if after is None else [after]))
    return outs if transposed else outs[0]


def _rms_bwd(name, x, dh, dres, g, tr, after=None):
    rows, d = x.shape

    def body(x_ref, dh_ref, dres_ref, g_ref, *refs):
        dx_ref, dxb_ref, dg_ref = refs[-3:]
        xv = x_ref[...]
        r = lax.rsqrt(jnp.mean(xv * xv, axis=-1, keepdims=True) + EPS)
        xhat = xv * r
        dhv = dh_ref[...]
        dxh = dhv * g_ref[...]
        dx = r * (dxh - xhat * jnp.mean(dxh * xhat, axis=-1, keepdims=True)) + dres_ref[...]
        dx_ref[...] = dx
        dxb_ref[...] = dx.astype(BF16)
        part = jnp.sum(dhv * xhat, axis=0, keepdims=True)

        @pl.when(pl.program_id(0) == 0)
        def _():
            dg_ref[...] = part

        @pl.when(pl.program_id(0) > 0)
        def _():
            dg_ref[...] += part

    blk = pl.BlockSpec((tr, d), lambda i: (i, 0))
    vec = pl.BlockSpec((1, d), lambda i: (0, 0))
    return pl.pallas_call(
        body, name=name, grid=(rows // tr,), in_specs=[blk, blk, blk, vec] + ([] if after is None else [ANY]),
        out_specs=[blk, blk, vec],
        out_shape=[jax.ShapeDtypeStruct((rows, d), F32), jax.ShapeDtypeStruct((rows, d), BF16),
                   jax.ShapeDtypeStruct((1, d), F32)],
        compiler_params=_params(("arbitrary",)),
    )(x, dh, dres, g, *([] if after is None else [after]))


def _rms_gain_grad(name, x, dh):
    rows, d = x.shape

    def body(x_ref, dh_ref, dg_ref):
        xv = x_ref[...]
        r = lax.rsqrt(jnp.mean(xv * xv, axis=-1, keepdims=True) + EPS)
        dg_ref[...] = jnp.sum(dh_ref[...] * xv * r, axis=0, keepdims=True)

    return pl.pallas_call(
        body, name=name, out_shape=jax.ShapeDtypeStruct((1, d), F32), compiler_params=_params(None),
    )(x, dh)


def _loss_and_grad(name, y, target, tr):
    rows, d = y.shape
    n = rows // tr

    def body(y_ref, t_ref, dx_ref, dxb_ref, loss_ref, acc_ref):
        e = y_ref[...] - t_ref[...]
        dx = e * (1.0 / d)
        dx_ref[...] = dx
        dxb_ref[...] = dx.astype(BF16)
        part = jnp.sum(e * e, axis=0, keepdims=True)
        i = pl.program_id(0)

        @pl.when(i == 0)
        def _():
            acc_ref[...] = part

        @pl.when(i > 0)
        def _():
            acc_ref[...] += part

        @pl.when(i == n - 1)
        def _():
            loss_ref[...] = jnp.sum(acc_ref[...], axis=-1, keepdims=True) * (0.5 / d)

    blk = pl.BlockSpec((tr, d), lambda i: (i, 0))
    return pl.pallas_call(
        body, name=name, grid=(n,), in_specs=[blk, blk],
        out_specs=[blk, blk, pl.BlockSpec((1, 1), lambda i: (0, 0))],
        out_shape=[jax.ShapeDtypeStruct((rows, d), F32), jax.ShapeDtypeStruct((rows, d), BF16),
                   jax.ShapeDtypeStruct((1, 1), F32)],
        scratch_shapes=[pltpu.VMEM((1, d), F32)],
        compiler_params=_params(("arbitrary",)),
    )(y, target)


SB_T = 256
SB_HEADS = 4


LOG2E = 1.4426950408889634


def _sb_scores(q, kblk):
    z2 = _dot(q, kblk, NT) * (ATT_SCALE * LOG2E)
    e = jnp.exp2(-jnp.abs(z2))
    l1 = jnp.minimum(-z2, 0.0) - jnp.log2(1.0 + e)
    lb = l1 + z2
    return z2, e, lb, l1


def _sb_fwd(name, proj, after=None):
    s_len = proj.shape[0]
    t = SB_T
    nq = s_len // t

    def body(q_ref, k_ref, v_ref, *refs):
        o_ref = refs[-1]
        i = pl.program_id(1)
        row = lax.broadcasted_iota(jnp.int32, (t, t), 0)
        col = lax.broadcasted_iota(jnp.int32, (t, t), 1)
        causal = col < row
        after_mat = (row > col).astype(BF16)
        heads = [slice(hh * HEAD_DIM, (hh + 1) * HEAD_DIM) for hh in range(SB_HEADS)]
        q = [q_ref[:, sl].astype(BF16) for sl in heads]

        def tile(kb, state, masked):
            start = pl.multiple_of(kb * t, t)
            out = []
            for hh, sl in enumerate(heads):
                carry, acc = state[hh]
                kblk = k_ref[pl.ds(start, t), sl].astype(BF16)
                vblk = v_ref[pl.ds(start, t), sl].astype(BF16)
                _, _, lb, l1 = _sb_scores(q[hh], kblk)
                if masked:
                    l1 = jnp.where(causal, l1, 0.0)
                hi, lo = _split_bf16(l1)
                after = _dot(hi, after_mat, NN) + _dot(lo, after_mat, NN) + carry
                a = jnp.exp2(lb + after)
                if masked:
                    a = jnp.where(causal, a, 0.0)
                acc = acc + _dot(a.astype(BF16), vblk, NN)
                carry = carry + jnp.sum(l1, axis=-1, keepdims=True)
                out.append((carry, acc))
            return tuple(out)

        zero = (jnp.zeros((t, 1), F32), jnp.zeros((t, HEAD_DIM), F32))
        state = tile(i, (zero,) * SB_HEADS, True)
        state = lax.fori_loop(0, i, lambda n, st: tile(i - 1 - n, st, False), state)
        for hh, sl in enumerate(heads):
            o_ref[:, sl] = state[hh][1]

    cb = SB_HEADS * HEAD_DIM
    return pl.pallas_call(
        body, name=name, grid=(HEADS // SB_HEADS, nq),
        in_specs=[pl.BlockSpec((t, cb), lambda h, i: (i, OFF_QB // cb + h)),
                  pl.BlockSpec((s_len, cb), lambda h, i: (0, OFF_KB // cb + h)),
                  pl.BlockSpec((s_len, cb), lambda h, i: (0, OFF_VB // cb + h))] + ([] if after is None else [ANY]),
        out_specs=pl.BlockSpec((t, cb), lambda h, i: (i, h)),
        out_shape=jax.ShapeDtypeStruct((s_len, D_B), F32),
        compiler_params=_params(("parallel", "arbitrary")),
    )(proj, proj, proj, *([] if after is None else [after]))


def _sb_bwd(name, proj, dy, after=None):
    s_len = proj.shape[0]
    t = SB_T
    nq = s_len // t

    def body(q_ref, k_ref, v_ref, z_ref, dy_ref, *refs):
        dq_ref, dk_ref, dv_ref, a_ref, s_ref = refs[-5:]
        i = pl.program_id(1)

        @pl.when(i == 0)
        def _():
            dk_ref[...] = jnp.zeros_like(dk_ref)
            dv_ref[...] = jnp.zeros_like(dv_ref)

        heads = [slice(hh * HEAD_DIM, (hh + 1) * HEAD_DIM) for hh in range(SB_HEADS)]
        q = [q_ref[:, sl].astype(BF16) for sl in heads]
        silu_z, _ = _silu_and_grad(z_ref[...])
        do_all = dy_ref[...] * silu_z
        do_b = [do_all[:, sl].astype(BF16) for sl in heads]
        row = lax.broadcasted_iota(jnp.int32, (t, t), 0)
        col = lax.broadcasted_iota(jnp.int32, (t, t), 1)
        causal = col < row
        after_mat = (row > col).astype(BF16)
        before_mat = (row < col).astype(BF16)

        def weights(kb, carries, masked):
            start = pl.multiple_of(kb * t, t)
            out = []
            for hh, sl in enumerate(heads):
                kblk = k_ref[pl.ds(start, t), sl].astype(BF16)
                z, _, lb, l1 = _sb_scores(q[hh], kblk)
                if masked:
                    l1 = jnp.where(causal, l1, 0.0)
                hi, lo = _split_bf16(l1)
                after = _dot(hi, after_mat, NN) + _dot(lo, after_mat, NN) + carries[hh]
                a = jnp.exp2(lb + after)
                if masked:
                    a = jnp.where(causal, a, 0.0)
                a_ref[hh, kb] = a
                s_ref[hh, kb] = z
                out.append(carries[hh] + jnp.sum(l1, axis=-1, keepdims=True))
            return tuple(out)

        carries = weights(i, (jnp.zeros((t, 1), F32),) * SB_HEADS, True)
        lax.fori_loop(0, i, lambda n, c: weights(i - 1 - n, c, False), carries)

        def grads(kb, state, masked):
            start = pl.multiple_of(kb * t, t)
            out = []
            for hh, sl in enumerate(heads):
                carry, dq = state[hh]
                kblk = k_ref[pl.ds(start, t), sl].astype(BF16)
                vblk = v_ref[pl.ds(start, t), sl].astype(BF16)
                a = a_ref[hh, kb]
                z = s_ref[hh, kb]
                g = _dot(do_b[hh], vblk, NT) * a
                ghi, glo = _split_bf16(g)
                prefix = _dot(ghi, before_mat, NN) + _dot(glo, before_mat, NN) + carry
                e = jnp.exp2(-jnp.abs(z))
                inv = 1.0 / (1.0 + e)
                pos = z >= 0.0
                beta = jnp.where(pos, inv, e * inv)
                one_m_beta = jnp.where(pos, e * inv, inv)
                dz = (g * one_m_beta - prefix * beta) * ATT_SCALE
                if masked:
                    dz = jnp.where(causal, dz, 0.0)
                dz_b = dz.astype(BF16)
                dq = dq + _dot(dz_b, kblk, NN)
                dk_ref[pl.ds(start, t), sl] += _dot(dz_b, q[hh], TN)
                dv_ref[pl.ds(start, t), sl] += _dot(a.astype(BF16), do_b[hh], TN)
                out.append((carry + jnp.sum(g, axis=-1, keepdims=True), dq))
            return tuple(out)

        zero = (jnp.zeros((t, 1), F32), jnp.zeros((t, HEAD_DIM), F32))
        state = lax.fori_loop(0, i, lambda kb, st: grads(kb, st, False), (zero,) * SB_HEADS)
        state = grads(i, state, True)
        for hh, sl in enumerate(heads):
            dq_ref[:, sl] = state[hh][1]

    cb = SB_HEADS * HEAD_DIM
    qblk = lambda off: pl.BlockSpec((t, cb), lambda h, i: (i, off // cb + h))
    full = lambda off: pl.BlockSpec((s_len, cb), lambda h, i: (0, off // cb + h))
    out = jax.ShapeDtypeStruct((s_len, D_B), F32)
    return pl.pallas_call(
        body, name=name, grid=(HEADS // SB_HEADS, nq),
        in_specs=[qblk(OFF_QB), full(OFF_KB), full(OFF_VB), qblk(OFF_ZB), qblk(OFF_YB)]
        + ([] if after is None else [ANY]),
        out_specs=[qblk(0), full(0), full(0)],
        out_shape=[out, out, out],
        scratch_shapes=[pltpu.VMEM((SB_HEADS, nq, t, t), F32), pltpu.VMEM((SB_HEADS, nq, t, t), F32)],
        compiler_params=_params(("parallel", "arbitrary")),
    )(proj, proj, proj, proj, dy, *([] if after is None else [after]))


MEM_TQ = 512


def _qk_norm(x, g):
    r = lax.rsqrt(jnp.mean(x * x, axis=-1, keepdims=True) + EPS)
    xhat = x * r
    return xhat * g, xhat, r


def _qk_norm_bwd(dn, g, xhat, r):
    dxh = dn * g
    return r * (dxh - xhat * jnp.mean(dxh * xhat, axis=-1, keepdims=True))


def _mem_probs(q, mk, qg, kg):
    qn, qhat, rq = _qk_norm(q, qg)
    kn, khat, rk = _qk_norm(mk, kg)
    qn_b, kn_b = qn.astype(BF16), kn.astype(BF16)
    s = _dot(qn_b, kn_b, NT) * ATT_SCALE
    p = jnp.exp(s - jnp.max(s, axis=-1, keepdims=True))
    p = p / jnp.sum(p, axis=-1, keepdims=True)
    return p, qn_b, kn_b, qhat, rq, khat, rk


def _mem_fwd(name, proj, mem_kv, qg, kg):
    s_len = proj.shape[0]
    m_len = mem_kv.shape[0]
    tq = min(MEM_TQ, s_len)

    def body(q_ref, mk_ref, mv_ref, qg_ref, kg_ref, o_ref):
        p = _mem_probs(q_ref[...], mk_ref[...], qg_ref[...], kg_ref[...])[0]
        o_ref[...] = _dot(p.astype(BF16), mv_ref[...].astype(BF16), NN)

    cb = HEAD_DIM
    vec = pl.BlockSpec((1, cb), lambda h, i: (0, 0))
    return pl.pallas_call(
        body, name=name, grid=(HEADS, s_len // tq),
        in_specs=[pl.BlockSpec((tq, cb), lambda h, i: (i, OFF_QC // cb + h)),
                  pl.BlockSpec((m_len, cb), lambda h, i: (0, h)),
                  pl.BlockSpec((m_len, cb), lambda h, i: (0, HEADS + h)), vec, vec],
        out_specs=pl.BlockSpec((tq, cb), lambda h, i: (i, h)),
        out_shape=jax.ShapeDtypeStruct((s_len, D_C), F32),
        compiler_params=_params(("parallel", "parallel")),
    )(proj, mem_kv, mem_kv, qg, kg)


def _mem_bwd(name, proj, mem_kv, qg, kg, dy):
    s_len = proj.shape[0]
    m_len = mem_kv.shape[0]
    tq = min(MEM_TQ, s_len)

    def body(q_ref, mk_ref, mv_ref, qg_ref, kg_ref, z_ref, dy_ref, dq_ref, dmk_ref, dmv_ref, dqg_ref, dkg_ref):
        h, i = pl.program_id(0), pl.program_id(1)

        @pl.when(i == 0)
        def _():
            dmk_ref[...] = jnp.zeros_like(dmk_ref)
            dmv_ref[...] = jnp.zeros_like(dmv_ref)

        @pl.when((i == 0) & (h == 0))
        def _():
            dqg_ref[...] = jnp.zeros_like(dqg_ref)
            dkg_ref[...] = jnp.zeros_like(dkg_ref)

        qg, kg = qg_ref[...], kg_ref[...]
        p, qn_b, kn_b, qhat, rq, khat, rk = _mem_probs(q_ref[...], mk_ref[...], qg, kg)
        silu_z, _ = _silu_and_grad(z_ref[...])
        do_b = (dy_ref[...] * silu_z).astype(BF16)
        dmv_ref[...] += _dot(p.astype(BF16), do_b, TN)
        dp = _dot(do_b, mv_ref[...].astype(BF16), NT)
        ds = (p * (dp - jnp.sum(dp * p, axis=-1, keepdims=True)) * ATT_SCALE).astype(BF16)
        dqn = _dot(ds, kn_b, NN)
        dkn = _dot(ds, qn_b, TN)
        dq_ref[...] = _qk_norm_bwd(dqn, qg, qhat, rq)
        dmk_ref[...] += _qk_norm_bwd(dkn, kg, khat, rk)
        dqg_ref[...] += jnp.sum(dqn * qhat, axis=0, keepdims=True)
        dkg_ref[...] += jnp.sum(dkn * khat, axis=0, keepdims=True)

    cb = HEAD_DIM
    vec = pl.BlockSpec((1, cb), lambda h, i: (0, 0))
    qblk = lambda off: pl.BlockSpec((tq, cb), lambda h, i: (i, off // cb + h))
    memblk = lambda off: pl.BlockSpec((m_len, cb), lambda h, i: (0, off + h))
    return pl.pallas_call(
        body, name=name, grid=(HEADS, s_len // tq),
        in_specs=[qblk(OFF_QC), memblk(0), memblk(HEADS), vec, vec, qblk(OFF_ZC), qblk(OFF_YC)],
        out_specs=[qblk(0), memblk(0), memblk(0), vec, vec],
        out_shape=[jax.ShapeDtypeStruct((s_len, D_C), F32), jax.ShapeDtypeStruct((m_len, D_C), F32),
                   jax.ShapeDtypeStruct((m_len, D_C), F32), jax.ShapeDtypeStruct((1, cb), F32),
                   jax.ShapeDtypeStruct((1, cb), F32)],
        compiler_params=_params(("arbitrary", "arbitrary")),
    )(proj, mem_kv, mem_kv, qg, kg, proj, dy)


def _sgu_common(u_ref, v_ref, lng_ref, lnb_ref, w_ref, bias_ref):
    ug = _gelu(u_ref[...])
    vg = _gelu(v_ref[...])
    mu = jnp.mean(vg, axis=-1, keepdims=True)
    xc = vg - mu
    rstd = lax.rsqrt(jnp.mean(xc * xc, axis=-1, keepdims=True) + EPS)
    xhat = xc * rstd
    vn = xhat * lng_ref[...] + lnb_ref[...]
    vn_b = vn.astype(BF16)
    row = lax.broadcasted_iota(jnp.int32, (CHUNK, CHUNK), 0)
    col = lax.broadcasted_iota(jnp.int32, (CHUNK, CHUNK), 1)
    tril = row >= col
    mixed = []
    for g in range(A_GROUPS):
        w = jnp.where(tril, w_ref[g], 0.0).astype(BF16)
        sl = slice(g * CHUNK, (g + 1) * CHUNK)
        mixed.append(_dot(w, vn_b[:, sl], NN) + bias_ref[:, sl])
    return ug, xhat, rstd, vn_b, mixed, tril


def _gate_fwd(name, proj, o_b, o_c, lng, lnb, w_s, bias):
    s_len = proj.shape[0]

    def body(u_ref, v_ref, za_ref, zb_ref, zc_ref, ob_ref, oc_ref, lng_ref, lnb_ref, w_ref, bias_ref, y_ref, yt_ref):
        ug, _, _, _, mixed, _ = _sgu_common(u_ref, v_ref, lng_ref, lnb_ref, w_ref, bias_ref)
        sza, _ = _silu_and_grad(za_ref[...])
        gate = ug * sza

        def put(off, width, val):
            y_ref[:, off:off + width] = val.astype(BF16)
            yt_ref[off:off + width, :] = val.T.astype(BF16)

        for g in range(A_GROUPS):
            sl = slice(g * CHUNK, (g + 1) * CHUNK)
            put(g * CHUNK, CHUNK, gate[:, sl] * mixed[g])
        szb, _ = _silu_and_grad(zb_ref[...])
        put(OFF_YB, D_B, ob_ref[...] * szb)
        szc, _ = _silu_and_grad(zc_ref[...])
        put(OFF_YC, D_C, oc_ref[...] * szc)

    wide = lambda off: pl.BlockSpec((CHUNK, D_A), lambda i: (i, off // D_A))
    narrow = lambda off: pl.BlockSpec((CHUNK, D_B), lambda i: (i, off // D_B))
    vec = pl.BlockSpec((1, D_A), lambda i: (0, 0))
    return pl.pallas_call(
        body, name=name, grid=(s_len // CHUNK,),
        in_specs=[wide(OFF_U), wide(OFF_V), wide(OFF_ZA), narrow(OFF_ZB), narrow(OFF_ZC), narrow(0), narrow(0), vec, vec,
                  pl.BlockSpec((A_GROUPS, CHUNK, CHUNK), lambda i: (0, 0, 0)),
                  pl.BlockSpec((CHUNK, D_A), lambda i: (0, 0))],
        out_specs=[pl.BlockSpec((CHUNK, D_MODEL), lambda i: (i, 0)), pl.BlockSpec((D_MODEL, CHUNK), lambda i: (0, i))],
        out_shape=[jax.ShapeDtypeStruct((s_len, D_MODEL), BF16), jax.ShapeDtypeStruct((D_MODEL, s_len), BF16)],
        compiler_params=_params(("parallel",)),
    )(proj, proj, proj, proj, proj, o_b, o_c, lng, lnb, w_s, bias)


def _gate_bwd(name, proj, dy, o_b, o_c, dqkv, dq_c, lng, lnb, w_s, w_s_t, bias):
    s_len = proj.shape[0]
    n = s_len // CHUNK
    dq_b, dk_b, dv_b = dqkv

    def body(u_ref, v_ref, za_ref, zb_ref, zc_ref, dya_ref, dyb_ref, dyc_ref, ob_ref, oc_ref, dq_ref, dk_ref, dv_ref,
             dqc_ref, lng_ref, lnb_ref, w_ref, wt_ref, bias_ref, dp_ref, dw_ref, dsb_ref, dlng_ref, dlnb_ref, dbias_ref):
        i = pl.program_id(0)

        @pl.when(i == 0)
        def _():
            dw_ref[...] = jnp.zeros_like(dw_ref)
            dbias_ref[...] = jnp.zeros_like(dbias_ref)
            dlng_ref[...] = jnp.zeros_like(dlng_ref)
            dlnb_ref[...] = jnp.zeros_like(dlnb_ref)

        ug, xhat, rstd, vn_b, mixed, tril = _sgu_common(u_ref, v_ref, lng_ref, lnb_ref, w_ref, bias_ref)
        za = za_ref[...]
        sza, dsza = _silu_and_grad(za)
        dya = dya_ref[...]
        mixed_all = jnp.concatenate(mixed, axis=-1)
        d_mixed = dya * ug * sza
        dp_ref[:, OFF_U:OFF_U + D_A] = (dya * mixed_all * sza * _gelu_grad(u_ref[...])).astype(BF16)
        dp_ref[:, OFF_ZA:OFF_ZA + D_A] = (dya * ug * mixed_all * dsza).astype(BF16)
        dbias_ref[...] += d_mixed
        dm_b = d_mixed.astype(BF16)
        triu = lax.broadcasted_iota(jnp.int32, (CHUNK, CHUNK), 0) <= lax.broadcasted_iota(jnp.int32, (CHUNK, CHUNK), 1)
        d_vn = []
        for g in range(A_GROUPS):
            sl = slice(g * CHUNK, (g + 1) * CHUNK)
            wt = jnp.where(triu, wt_ref[g], 0.0).astype(BF16)
            d_vn.append(_dot(wt, dm_b[:, sl], NN))
            dw_ref[g] += jnp.where(tril, _dot(dm_b[:, sl], vn_b[:, sl], NT), 0.0)
        d_vn = jnp.concatenate(d_vn, axis=-1)
        dlng_ref[...] += jnp.sum(d_vn * xhat, axis=0, keepdims=True)
        dlnb_ref[...] += jnp.sum(d_vn, axis=0, keepdims=True)
        dxh = d_vn * lng_ref[...]
        d_vg = rstd * (dxh - jnp.mean(dxh, axis=-1, keepdims=True)
                       - xhat * jnp.mean(dxh * xhat, axis=-1, keepdims=True))
        dp_ref[:, OFF_V:OFF_V + D_A] = (d_vg * _gelu_grad(v_ref[...])).astype(BF16)
        dp_ref[:, OFF_QB:OFF_QB + D_B] = dq_ref[...].astype(BF16)
        dp_ref[:, OFF_KB:OFF_KB + D_B] = dk_ref[...].astype(BF16)
        dp_ref[:, OFF_VB:OFF_VB + D_B] = dv_ref[...].astype(BF16)
        _, dszb = _silu_and_grad(zb_ref[...])
        dp_ref[:, OFF_ZB:OFF_ZB + D_B] = (dyb_ref[...] * ob_ref[...] * dszb).astype(BF16)
        dp_ref[:, OFF_QC:OFF_QC + D_C] = dqc_ref[...].astype(BF16)
        _, dszc = _silu_and_grad(zc_ref[...])
        dp_ref[:, OFF_ZC:OFF_ZC + D_C] = (dyc_ref[...] * oc_ref[...] * dszc).astype(BF16)

        @pl.when(i == n - 1)
        def _():
            ch = lax.broadcasted_iota(jnp.int32, (D_A, CHUNK), 0)
            gcol = lax.broadcasted_iota(jnp.int32, (D_A, CHUNK), 1)
            pick = (ch // (D_A // A_GROUPS) == gcol).astype(BF16)
            rest = dbias_ref[...]
            tot = jnp.zeros((CHUNK, CHUNK), F32)
            for _ in range(3):
                term = rest.astype(BF16)
                tot = tot + _dot(term, pick, NN)
                rest = rest - term.astype(F32)
            dsb_ref[...] = tot

    wide = lambda off: pl.BlockSpec((CHUNK, D_A), lambda i: (i, off // D_A))
    narrow = lambda off: pl.BlockSpec((CHUNK, D_B), lambda i: (i, off // D_B))
    vec = pl.BlockSpec((1, D_A), lambda i: (0, 0))
    wspec = pl.BlockSpec((A_GROUPS, CHUNK, CHUNK), lambda i: (0, 0, 0))
    bspec = pl.BlockSpec((CHUNK, D_A), lambda i: (0, 0))
    return pl.pallas_call(
        body, name=name, grid=(n,),
        in_specs=[wide(OFF_U), wide(OFF_V), wide(OFF_ZA), narrow(OFF_ZB), narrow(OFF_ZC),
                  wide(0), narrow(OFF_YB), narrow(OFF_YC), narrow(0), narrow(0), narrow(0), narrow(0), narrow(0),
                  narrow(0), vec, vec, wspec, wspec, bspec],
        out_specs=[pl.BlockSpec((CHUNK, IN_WIDTH), lambda i: (i, 0)), wspec,
                   pl.BlockSpec((CHUNK, CHUNK), lambda i: (0, 0)), vec, vec],
        out_shape=[jax.ShapeDtypeStruct((s_len, IN_WIDTH), BF16), jax.ShapeDtypeStruct((A_GROUPS, CHUNK, CHUNK), F32),
                   jax.ShapeDtypeStruct((CHUNK, CHUNK), F32), jax.ShapeDtypeStruct((1, D_A), F32),
                   jax.ShapeDtypeStruct((1, D_A), F32)],
        scratch_shapes=[pltpu.VMEM((CHUNK, D_A), F32)],
        compiler_params=_params(("arbitrary",)),
    )(proj, proj, proj, proj, proj, dy, dy, dy, o_b, o_c, dq_b, dk_b, dv_b, dq_c, lng, lnb, w_s, w_s_t, bias)


IN_SHARD = IN_WIDTH // N_CHIPS
ROW_SHARD = D_MODEL // N_CHIPS


def _bias_rows(sgu_b_l):
    return jnp.repeat(sgu_b_l.T, D_A // A_GROUPS, axis=1)


class _WholeWeights:
    def __init__(self, w_in_all, w_kv_all, w_out_all):
        self.weights = (w_in_all, w_kv_all, w_out_all)

    def w_in(self, stage, h, proj):
        return (self.weights[0], jnp.arange(N_CHIPS, dtype=jnp.int32), 0, N_CHIPS) if stage == 0 else None

    def rest_start(self, proj):
        return None

    def rest_finish(self, o_b):
        return self.weights[1], self.weights[2], None

    def before_out(self, y):
        return None


def _layer_fwd(l, x, mem, sm, hooks):
    s_len = x.shape[0]
    m_len = mem.shape[0]
    tm = min(1024, s_len)
    h, h_t = _rms_fwd(f"rms_fwd_{l}", x, sm["norm_g"][l][None], min(256, s_len), transposed=True)
    proj, stage = None, 0
    while (ready := hooks.w_in(stage, h, proj)) is not None:
        w_in_all, order, first, count = ready
        proj = _matmul(
            f"in_proj_{l}_{stage}", h, w_in_all, grid=(s_len // tm, count, 1), place=order, into=proj,
            a_spec=pl.BlockSpec((tm, D_MODEL), lambda i, j, k, p: (i, 0)),
            b_spec=pl.BlockSpec((None, D_MODEL, IN_SHARD), lambda i, j, k, p: (p[first + j], 0, 0)),
            o_spec=pl.BlockSpec((tm, IN_SHARD), lambda i, j, k, p: (i, p[first + j])),
            out_shape=jax.ShapeDtypeStruct((s_len, IN_WIDTH), F32), dims=NN)
        stage += 1
    o_b = _sb_fwd(f"sb_fwd_{l}", proj, hooks.rest_start(proj))
    w_kv_all, w_out_all, after = hooks.rest_finish(o_b)
    mem_h = _rms_fwd(f"mem_rms_fwd_{l}", mem, sm["mem_norm_g"][l][None], m_len, after)
    mem_kv = _matmul(
        f"mem_kv_{l}", mem_h, w_kv_all, grid=(1, 2, N_CHIPS),
        a_spec=pl.BlockSpec((m_len, ROW_SHARD), lambda i, j, k: (0, k)),
        b_spec=pl.BlockSpec((None, ROW_SHARD, D_C), lambda i, j, k: (k, 0, j)),
        o_spec=pl.BlockSpec((m_len, D_C), lambda i, j, k: (0, j)),
        out_shape=jax.ShapeDtypeStruct((m_len, 2 * D_C), F32), dims=NN)
    qg, kg = sm["q_norm_g"][l][None], sm["k_norm_g"][l][None]
    o_c = _mem_fwd(f"mem_fwd_{l}", proj, mem_kv, qg, kg)
    bias = _bias_rows(sm["sgu_b"][l])
    y, y_t = _gate_fwd(f"gate_fwd_{l}", proj, o_b, o_c, sm["sgu_ln_g"][l][None], sm["sgu_ln_b"][l][None],
                       sm["sgu_w"][l], bias)
    tn_o = 512
    x_next = _matmul(
        f"out_proj_{l}", y, w_out_all, grid=(s_len // tm, D_MODEL // tn_o, 1),
        a_spec=pl.BlockSpec((tm, D_MODEL), lambda i, j, k: (i, 0)),
        b_spec=pl.BlockSpec((N_CHIPS, ROW_SHARD, tn_o), lambda i, j, k: (0, 0, j)),
        o_spec=pl.BlockSpec((tm, tn_o), lambda i, j, k: (i, j)),
        out_shape=jax.ShapeDtypeStruct((s_len, D_MODEL), F32), dims=NN,
        res=x, res_spec=pl.BlockSpec((tm, tn_o), lambda i, j, k: (i, j)), after=hooks.before_out(y))
    saved = dict(x=x, h_t=h_t, proj=proj, mem_h=mem_h, mem_kv=mem_kv, o_b=o_b, o_c=o_c, y_t=y_t, bias=bias,
                 weights=(w_in_all, w_kv_all, w_out_all))
    return x_next, saved


class _NoExchange:
    def __init__(self):
        self.gave, self.kept = {}, {}

    def start(self, l, group, gives):
        self.gave[l, group] = gives
        return None

    def landed(self, l, group, after):
        return [jnp.zeros_like(g) for g in self.gave[l, group]]

    def send(self, l, group, parts):
        self.kept[l, group] = parts
        return None


def _layer_bwd(l, dxo, dxo_b, mem, sm, saved, place, exchange):
    s_len = dxo.shape[0]
    m_len = mem.shape[0]
    proj, y_t, h_t, mem_h, mem_kv = saved["proj"], saved["y_t"], saved["h_t"], saved["mem_h"], saved["mem_kv"]
    w_in_all, w_kv_all, w_out_all = saved["weights"]
    tm = min(1024, s_len)
    tn = 768
    per = IN_SHARD // tn
    half_rows = ROW_SHARD // 2

    def halves(make):
        give = lambda: make("give", lambda p: 1 - p[1], None, F32)
        keep = lambda theirs: make("keep", lambda p: p[1], theirs, BF16)
        return give, keep

    def grad_out(tag, half, theirs, dtype):
        o_spec = pl.BlockSpec((None, half_rows, 1024), lambda i, j, k, p: (i, 0, j))
        return _matmul(
            f"d_w_out_{l}_{tag}", y_t, dxo_b, grid=(N_CHIPS, D_MODEL // 1024, 1), place=place,
            a_spec=pl.BlockSpec((half_rows, s_len), lambda i, j, k, p: (2 * i + half(p), 0)),
            b_spec=pl.BlockSpec((s_len, 1024), lambda i, j, k, p: (0, j)), o_spec=o_spec,
            out_shape=jax.ShapeDtypeStruct((N_CHIPS, half_rows, D_MODEL), dtype), dims=NN,
            res=theirs, res_spec=o_spec)

    def grad_in(tag, half, theirs, dtype):
        o_spec = pl.BlockSpec((None, D_MODEL // 2, tn), lambda i, j, k, p: (j // per, 0, j % per))
        return _matmul(
            f"d_w_in_{l}_{tag}", h_t, dproj, grid=(1, IN_WIDTH // tn, 1), place=place,
            a_spec=pl.BlockSpec((D_MODEL // 2, s_len), lambda i, j, k, p: (half(p), 0)),
            b_spec=pl.BlockSpec((s_len, tn), lambda i, j, k, p: (0, j)), o_spec=o_spec,
            out_shape=jax.ShapeDtypeStruct((N_CHIPS, D_MODEL // 2, IN_SHARD), dtype), dims=NN,
            res=theirs, res_spec=o_spec)

    def grad_kv(tag, half, theirs, dtype):
        o_spec = pl.BlockSpec((None, half_rows, 2 * D_C), lambda i, j, k, p: (i, 0, 0))
        return _matmul(
            f"d_w_kv_{l}_{tag}", mem_h, dkv_b, grid=(N_CHIPS, 1, 1), place=place,
            a_spec=pl.BlockSpec((m_len, half_rows), lambda i, j, k, p: (0, 2 * i + half(p))),
            b_spec=pl.BlockSpec((m_len, 2 * D_C), lambda i, j, k, p: (0, 0)), o_spec=o_spec,
            out_shape=jax.ShapeDtypeStruct((N_CHIPS, half_rows, 2 * D_C), dtype), dims=TN,
            res=theirs, res_spec=o_spec)

    give_out, keep_out = halves(grad_out)
    token = exchange.start(l, "out", [give_out()])
    dy = _matmul(
        f"d_y_{l}", dxo_b, w_out_all, grid=(s_len // tm, N_CHIPS, 1),
        a_spec=pl.BlockSpec((tm, D_MODEL), lambda i, j, k: (i, 0)),
        b_spec=pl.BlockSpec((None, ROW_SHARD, D_MODEL), lambda i, j, k: (j, 0, 0)),
        o_spec=pl.BlockSpec((tm, ROW_SHARD), lambda i, j, k: (i, j)),
        out_shape=jax.ShapeDtypeStruct((s_len, D_MODEL), F32), dims=NT, after=token)
    (theirs_out,) = exchange.landed(l, "out", dy)
    token = exchange.send(l, "out", [keep_out(theirs_out)])
    qg, kg = sm["q_norm_g"][l][None], sm["k_norm_g"][l][None]
    dqkv = _sb_bwd(f"sb_bwd_{l}", proj, dy, token)
    dq_c, dmk, dmv, dqg, dkg = _mem_bwd(f"mem_bwd_{l}", proj, mem_kv, qg, kg, dy)
    w_s = sm["sgu_w"][l]
    dproj, dws, dbias, dlng, dlnb = _gate_bwd(
        f"gate_bwd_{l}", proj, dy, saved["o_b"], saved["o_c"], dqkv, dq_c, sm["sgu_ln_g"][l][None],
        sm["sgu_ln_b"][l][None], w_s, jnp.swapaxes(w_s, 1, 2), saved["bias"])
    dkv_b = jnp.concatenate([dmk, dmv], axis=1).astype(BF16)
    give_in, keep_in = halves(grad_in)
    give_kv, keep_kv = halves(grad_kv)
    token = exchange.start(l, "in", [give_in(), give_kv()])
    dh = _matmul(
        f"d_h_{l}", dproj, w_in_all, grid=(s_len // tm, D_MODEL // 512, 1),
        a_spec=pl.BlockSpec((tm, IN_WIDTH), lambda i, j, k: (i, 0)),
        b_spec=pl.BlockSpec((N_CHIPS, 512, IN_SHARD), lambda i, j, k: (0, j, 0)),
        o_spec=pl.BlockSpec((tm, 512), lambda i, j, k: (i, j)),
        out_shape=jax.ShapeDtypeStruct((s_len, D_MODEL), F32), dims=NT, after=token, vmem_mb=56)
    theirs_in, theirs_kv = exchange.landed(l, "in", dh)
    token = exchange.send(l, "in", [keep_in(theirs_in), keep_kv(theirs_kv)])
    dx, dx_b, dng = _rms_bwd(f"rms_bwd_{l}", saved["x"], dh, dxo, sm["norm_g"][l][None], min(256, s_len), token)
    d_mem_h = _matmul(
        f"d_mem_h_{l}", dkv_b, w_kv_all, grid=(1, N_CHIPS, 1),
        a_spec=pl.BlockSpec((m_len, 2 * D_C), lambda i, j, k: (0, 0)),
        b_spec=pl.BlockSpec((None, ROW_SHARD, 2 * D_C), lambda i, j, k: (j, 0, 0)),
        o_spec=pl.BlockSpec((m_len, ROW_SHARD), lambda i, j, k: (0, j)),
        out_shape=jax.ShapeDtypeStruct((m_len, D_MODEL), F32), dims=NT)
    dmng = _rms_gain_grad(f"mem_rms_bwd_{l}", mem, d_mem_h)
    dsgu_b = dbias[:, :A_GROUPS].T
    small = dict(norm_g=dng[0], sgu_ln_g=dlng[0], sgu_ln_b=dlnb[0], sgu_w=dws, sgu_b=dsgu_b, mem_norm_g=dmng[0],
                 q_norm_g=dqg[0], k_norm_g=dkg[0])
    return dx, dx_b, small


SMALL_NAMES = ("norm_g", "sgu_ln_g", "sgu_ln_b", "sgu_w", "sgu_b", "mem_norm_g", "q_norm_g", "k_norm_g")


def _local_step(x, mem, target, sm, w_all):
    saved = []
    cur = x
    for l in range(DEPTH):
        cur, sv = _layer_fwd(l, cur, mem, sm, _WholeWeights(*w_all[l]))
        saved.append(sv)
    dxo, dxo_b, loss = _loss_and_grad("loss", cur, target, min(256, x.shape[0]))
    small = [None] * DEPTH
    exchange = _NoExchange()
    place = jnp.zeros((2,), jnp.int32)
    for l in reversed(range(DEPTH)):
        dxo, dxo_b, small[l] = _layer_bwd(l, dxo, dxo_b, mem, sm, saved[l], place, exchange)
    small = {k: jnp.stack([small[l][k] for l in range(DEPTH)]) for k in SMALL_NAMES}
    return loss, dxo, small, exchange.gave, exchange.kept


def _place():
    x, y, c = lax.axis_index("x"), lax.axis_index("y"), lax.axis_index("c")
    return x, y, c


def _other_chips(x, y):
    return [(1 - x, y, 2 * (1 - x) + y), (x, 1 - y, 2 * x + 1 - y), (1 - x, 1 - y, 2 * (1 - x) + 1 - y)]


D2D_CHUNKS = 8


def _place_index():
    return jnp.stack([2 * lax.axis_index("x") + lax.axis_index("y"), lax.axis_index("c")]).astype(jnp.int32)


def _cast_into_slot(name, w, l, place):
    _, rows, cols = w.shape
    tr = min(256, rows)

    def body(p_ref, w_ref, o_ref):
        o_ref[...] = w_ref[...].astype(BF16)

    return pl.pallas_call(
        body, name=name,
        grid_spec=pltpu.PrefetchScalarGridSpec(
            num_scalar_prefetch=1, grid=(rows // tr,),
            in_specs=[pl.BlockSpec((None, tr, cols), lambda i, p: (l, i, 0))],
            out_specs=pl.BlockSpec((None, tr, cols), lambda i, p: (p[0], i, 0))),
        out_shape=jax.ShapeDtypeStruct((N_CHIPS, rows, cols), BF16),
        compiler_params=_params(("parallel",)),
    )(place, w)


HBM = pl.BlockSpec(memory_space=pltpu.HBM)
SEM = pl.BlockSpec(memory_space=pltpu.SEMAPHORE)
DATAFLOW = pltpu.SideEffectType.DATAFLOW_SIDE_EFFECTING


def _in_hbm(a):
    return pltpu.with_memory_space_constraint(a, pltpu.HBM)


ALL_PEERS = (0, 1, 2)
NEIGHBOURS = (0, 1)
DIAGONAL = (2,)


def _chip_copies_start(name, srcs, lands, make_copy, after=None, peers=ALL_PEERS):
    n_t = len(srcs)
    in_place = lands is None
    n_after = 0 if after is None else 1

    def body(*refs):
        src = refs[:n_t]
        k = (n_t if in_place else 2 * n_t) + n_after
        send_sems, recv_sems = refs[k], refs[k + 1]
        land = refs[k + 2:k + 2 + n_t] if in_place else refs[k + 2 + n_t:k + 2 + 2 * n_t]
        token = refs[-1]
        x, y, c = _place()
        me = 2 * x + y
        others = _other_chips(x, y)
        for t in range(n_t):
            for px, py, pk in [others[p] for p in peers]:
                s, d = make_copy(src[t], land[t], me, pk, c)
                pltpu.make_async_remote_copy(
                    src_ref=s, dst_ref=d, send_sem=send_sems.at[t], recv_sem=recv_sems.at[t],
                    device_id=(px, py, c), device_id_type=MESH).start()
        token[...] = jnp.zeros_like(token)

    bufs = list(srcs) if in_place else list(srcs) + list(lands)
    outs = pl.pallas_call(
        body, name=name,
        in_specs=[HBM] * len(bufs) + [ANY] * n_after,
        out_specs=[SEM, SEM] + [HBM] * len(bufs) + [pl.BlockSpec(memory_space=pltpu.VMEM)],
        out_shape=[pltpu.SemaphoreType.DMA((n_t,)), pltpu.SemaphoreType.DMA((n_t,))]
        + [pltpu.HBM(b.shape, b.dtype) for b in bufs] + [jax.ShapeDtypeStruct((8, 128), F32)],
        input_output_aliases={i: 2 + i for i in range(len(bufs))},
        compiler_params=pltpu.CompilerParams(has_side_effects=DATAFLOW),
    )(*[_in_hbm(b) for b in bufs], *([] if after is None else [after]))
    return outs[0], outs[1], list(outs[2:2 + len(bufs)]), outs[-1]


def _chip_copies_wait(name, send_sems, recv_sems, bufs, sent, landed, after):
    n_b = len(bufs)

    def body(*refs):
        buf = refs[:n_b]
        send_ref, recv_ref = refs[n_b], refs[n_b + 1]
        x, y, c = _place()
        for t, (s, d) in enumerate(zip(sent(buf), landed(buf))):
            out = pltpu.make_async_remote_copy(src_ref=s, dst_ref=s, send_sem=send_ref.at[t], recv_sem=recv_ref.at[t],
                                               device_id=(x, y, c), device_id_type=MESH)
            out.wait_send()
            arrived = pltpu.make_async_remote_copy(src_ref=d, dst_ref=d, send_sem=send_ref.at[t],
                                                   recv_sem=recv_ref.at[t], device_id=(x, y, c), device_id_type=MESH)
            arrived.wait_recv()

    after = list(after) if isinstance(after, (list, tuple)) else [after]
    return pl.pallas_call(
        body, name=name,
        in_specs=[HBM] * n_b + [SEM, SEM] + [ANY] * len(after), out_specs=[HBM] * n_b,
        out_shape=[pltpu.HBM(b.shape, b.dtype) for b in bufs],
        input_output_aliases={i: i for i in range(n_b)},
        compiler_params=pltpu.CompilerParams(has_side_effects=DATAFLOW),
    )(*bufs, send_sems, recv_sems, *after)


def _gather_start(name, bufs, after=None, peers=ALL_PEERS):
    def make_copy(src, land, me, pk, c):
        hr = src.shape[1] // 2
        return src.at[me, pl.ds(c * hr, hr)], land.at[me, pl.ds(c * hr, hr)]

    return _chip_copies_start(name, bufs, None, make_copy, after, peers)


def _gather_wait(name, send_sems, recv_sems, bufs, after, peers=ALL_PEERS):
    def half_shards(buf):
        return [b.at[pl.ds(0, len(peers)), pl.ds(0, b.shape[1] // 2)] for b in buf]

    return _chip_copies_wait(name, send_sems, recv_sems, bufs, half_shards, half_shards, after)


def _gather_forward_start(name, bufs, peers=ALL_PEERS):
    n_t = len(bufs)

    def body(*refs):
        mine = refs[:n_t]
        send_sems, recv_sems = refs[n_t], refs[n_t + 1]
        buf = refs[n_t + 2:2 * n_t + 2]
        token = refs[-1]
        x, y, c = _place()
        others = _other_chips(x, y)
        for q in range(D2D_CHUNKS):
            for t in range(n_t):
                hr = mine[t].shape[1] // 2
                cr = hr // D2D_CHUNKS
                rows = pl.ds(c * hr + q * cr, cr)
                for _, _, pk in [others[p] for p in peers]:
                    pltpu.make_async_remote_copy(
                        src_ref=mine[t].at[pk, rows], dst_ref=buf[t].at[pk, rows], send_sem=send_sems.at[t],
                        recv_sem=recv_sems.at[t], device_id=(x, y, 1 - c), device_id_type=MESH).start()
        token[...] = jnp.zeros_like(token)

    outs = pl.pallas_call(
        body, name=name,
        in_specs=[HBM] * n_t,
        out_specs=[SEM, SEM] + [HBM] * n_t + [pl.BlockSpec(memory_space=pltpu.VMEM)],
        out_shape=[pltpu.SemaphoreType.DMA((n_t,)), pltpu.SemaphoreType.DMA((n_t,))]
        + [pltpu.HBM(b.shape, b.dtype) for b in bufs] + [jax.ShapeDtypeStruct((8, 128), F32)],
        input_output_aliases={i: 2 + i for i in range(n_t)},
        compiler_params=pltpu.CompilerParams(has_side_effects=DATAFLOW),
    )(*[_in_hbm(b) for b in bufs])
    return outs[0], outs[1], list(outs[2:2 + n_t]), outs[-1]


def _core_exchange_start(name, grads):
    n_t = len(grads)
    lands = [lax.empty(g.shape, g.dtype) for g in grads]

    def body(*refs):
        src = refs[:n_t]
        send_sems, recv_sems = refs[2 * n_t], refs[2 * n_t + 1]
        land = refs[2 * n_t + 2 + n_t:2 * n_t + 2 + 2 * n_t]
        token = refs[-1]
        x, y, c = _place()
        for q in range(D2D_CHUNKS):
            for t in range(n_t):
                cr = src[t].shape[1] // D2D_CHUNKS
                rows = pl.ds(q * cr, cr)
                pltpu.make_async_remote_copy(
                    src_ref=src[t].at[:, rows], dst_ref=land[t].at[:, rows], send_sem=send_sems.at[t],
                    recv_sem=recv_sems.at[t], device_id=(x, y, 1 - c), device_id_type=MESH).start()
        token[...] = jnp.zeros_like(token)

    bufs = list(grads) + lands
    outs = pl.pallas_call(
        body, name=name,
        in_specs=[HBM] * len(bufs),
        out_specs=[SEM, SEM] + [HBM] * len(bufs) + [pl.BlockSpec(memory_space=pltpu.VMEM)],
        out_shape=[pltpu.SemaphoreType.DMA((n_t,)), pltpu.SemaphoreType.DMA((n_t,))]
        + [pltpu.HBM(b.shape, b.dtype) for b in bufs] + [jax.ShapeDtypeStruct((8, 128), F32)],
        input_output_aliases={i: 2 + i for i in range(len(bufs))},
        compiler_params=pltpu.CompilerParams(has_side_effects=DATAFLOW),
    )(*[_in_hbm(b) for b in bufs])
    return outs[0], outs[1], list(outs[2:2 + len(bufs)]), outs[-1]


def _core_exchange_wait(name, send_sems, recv_sems, bufs, after):
    n_t = len(bufs) // 2

    def body(*refs):
        land = refs[n_t:2 * n_t]
        send_ref, recv_ref = refs[2 * n_t], refs[2 * n_t + 1]
        x, y, c = _place()
        for t in range(n_t):
            whole = pltpu.make_async_remote_copy(src_ref=land[t], dst_ref=land[t], send_sem=send_ref.at[t],
                                                 recv_sem=recv_ref.at[t], device_id=(x, y, c), device_id_type=MESH)
            whole.wait_send()
            whole.wait_recv()

    outs = pl.pallas_call(
        body, name=name,
        in_specs=[HBM] * (2 * n_t) + [SEM, SEM, ANY], out_specs=[HBM] * (2 * n_t),
        out_shape=[pltpu.HBM(b.shape, b.dtype) for b in bufs],
        input_output_aliases={i: i for i in range(2 * n_t)},
        compiler_params=pltpu.CompilerParams(has_side_effects=DATAFLOW),
    )(*bufs, send_sems, recv_sems, after)
    return list(outs[:n_t]), list(outs[n_t:])


def _chip_exchange_start(name, parts):
    lands = [lax.empty(p.shape, p.dtype) for p in parts]
    return _chip_copies_start(name, parts, lands, lambda src, land, me, pk, c: (src.at[pk], land.at[me]))


def _chip_exchange_wait(name, send_sems, recv_sems, bufs, after):
    n_t = len(bufs) // 2
    return _chip_copies_wait(name, send_sems, recv_sems, bufs,
                             lambda buf: [b.at[pl.ds(0, 3)] for b in buf[:n_t]],
                             lambda buf: [b.at[pl.ds(0, 3)] for b in buf[n_t:]], after)


def _sum_chips(name, parts, landed, place, l, stacked):
    chips, rows, cols = landed.shape
    tr = min(256, rows)
    per = rows // tr

    def body(p_ref, own_ref, *refs):
        land, o_ref = refs[:chips], refs[-1]
        tot = None
        for k in range(chips):
            term = jnp.where(p_ref[0] == k, own_ref[...], land[k][...]).astype(F32)
            tot = term if tot is None else tot + term
        o_ref[...] = tot

    def from_chip(k):
        return pl.BlockSpec((None, tr, cols), lambda i, p: (jnp.where(p[0] == k, (k + 1) % chips, k), i, 0))

    in_specs = [pl.BlockSpec((None, tr, cols), lambda i, p: (p[0], i, 0))] + [from_chip(k) for k in range(chips)]
    args = [parts] + [landed] * chips
    aliases = {}
    if stacked is not None:
        in_specs.append(ANY)
        args.append(stacked)
        aliases = {len(args): 0}
    return pl.pallas_call(
        body, name=name,
        grid_spec=pltpu.PrefetchScalarGridSpec(
            num_scalar_prefetch=1, grid=(per,), in_specs=in_specs,
            out_specs=pl.BlockSpec((None, tr, cols), lambda i, p: (l, p[1] * per + i, 0))),
        out_shape=jax.ShapeDtypeStruct((DEPTH, 2 * rows, cols), F32), input_output_aliases=aliases,
        compiler_params=_params(("parallel",)),
    )(place, *args)


def _core_share_start(name, bufs, l):
    n_t = len(bufs)

    def body(*refs):
        mine = refs[:n_t]
        send_sems, recv_sems = refs[n_t], refs[n_t + 1]
        buf = refs[n_t + 2:2 * n_t + 2]
        token = refs[-1]
        x, y, c = _place()
        for q in range(D2D_CHUNKS):
            for t in range(n_t):
                hr = mine[t].shape[1] // 2
                cr = hr // D2D_CHUNKS
                rows = pl.ds(c * hr + q * cr, cr)
                pltpu.make_async_remote_copy(
                    src_ref=mine[t].at[l, rows], dst_ref=buf[t].at[l, rows], send_sem=send_sems.at[t],
                    recv_sem=recv_sems.at[t], device_id=(x, y, 1 - c), device_id_type=MESH).start()
        token[...] = jnp.zeros_like(token)

    outs = pl.pallas_call(
        body, name=name,
        in_specs=[HBM] * n_t,
        out_specs=[SEM, SEM] + [HBM] * n_t + [pl.BlockSpec(memory_space=pltpu.VMEM)],
        out_shape=[pltpu.SemaphoreType.DMA((n_t,)), pltpu.SemaphoreType.DMA((n_t,))]
        + [pltpu.HBM(b.shape, b.dtype) for b in bufs] + [jax.ShapeDtypeStruct((8, 128), F32)],
        input_output_aliases={i: 2 + i for i in range(n_t)},
        compiler_params=pltpu.CompilerParams(has_side_effects=DATAFLOW),
    )(*[_in_hbm(b) for b in bufs])
    return outs[0], outs[1], list(outs[2:2 + n_t]), outs[-1]


def _core_share_wait(name, send_sems, recv_sems, bufs, l, after):
    def half_layer(buf):
        return [b.at[l, pl.ds(0, b.shape[1] // 2)] for b in buf]

    return _chip_copies_wait(name, send_sems, recv_sems, bufs, half_layer, half_layer, after)


def _all_reduce_small(vec, after=None):
    rows, lanes = vec.shape
    hr = rows // 2

    def body(v_ref, *refs):
        o_ref, sib_ref, chips_ref, send_sems, recv_sems = refs[-5:]
        x, y, c = _place()
        me = 2 * x + y
        sibling = (x, y, 1 - c)
        mine = pl.ds(pl.multiple_of(c * hr, 8), hr)
        theirs = pl.ds(pl.multiple_of((1 - c) * hr, 8), hr)
        swap = pltpu.make_async_remote_copy(
            src_ref=v_ref.at[theirs], dst_ref=sib_ref, send_sem=send_sems.at[0], recv_sem=recv_sems.at[0],
            device_id=sibling, device_id_type=MESH)
        swap.start()
        swap.wait_recv()
        chips_ref[me] = v_ref[mine] + sib_ref[...]
        copies = []
        for j, (px, py, pk) in enumerate(_other_chips(x, y)):
            cp = pltpu.make_async_remote_copy(
                src_ref=chips_ref.at[me], dst_ref=chips_ref.at[me], send_sem=send_sems.at[1 + j],
                recv_sem=recv_sems.at[1 + j], device_id=(px, py, c), device_id_type=MESH)
            cp.start()
            copies.append(cp)
        for j, (px, py, pk) in enumerate(_other_chips(x, y)):
            pltpu.make_async_remote_copy(
                src_ref=chips_ref.at[pk], dst_ref=chips_ref.at[pk], send_sem=send_sems.at[1 + j],
                recv_sem=recv_sems.at[1 + j], device_id=(px, py, c), device_id_type=MESH).wait_recv()
        tot = chips_ref[0]
        for k in range(1, N_CHIPS):
            tot = tot + chips_ref[k]
        o_ref[mine] = tot
        share = pltpu.make_async_remote_copy(
            src_ref=o_ref.at[mine], dst_ref=o_ref.at[mine], send_sem=send_sems.at[4], recv_sem=recv_sems.at[4],
            device_id=sibling, device_id_type=MESH)
        share.start()
        pltpu.make_async_remote_copy(
            src_ref=o_ref.at[theirs], dst_ref=o_ref.at[theirs], send_sem=send_sems.at[4], recv_sem=recv_sems.at[4],
            device_id=sibling, device_id_type=MESH).wait_recv()
        swap.wait_send()
        for cp in copies:
            cp.wait_send()
        share.wait_send()

    vm = pl.BlockSpec(memory_space=pltpu.VMEM)
    return pl.pallas_call(
        body, name="small_all_reduce", in_specs=[vm] + ([] if after is None else [ANY]), out_specs=vm,
        out_shape=jax.ShapeDtypeStruct((rows, lanes), F32),
        scratch_shapes=[pltpu.VMEM((hr, lanes), F32), pltpu.VMEM((N_CHIPS, hr, lanes), F32),
                        pltpu.SemaphoreType.DMA((5,)), pltpu.SemaphoreType.DMA((5,))],
        compiler_params=pltpu.CompilerParams(has_side_effects=True, vmem_limit_bytes=48 * MIB),
    )(vec, *([] if after is None else [after]))


def _adamw(name, w, g, m, v, place, l=0, half=None, done=None, after=None):
    layers, rows, cols = w.shape
    span = rows if half is None else rows // 2
    tr = span
    for cand in (256, 128, 64, 32, 16, 8):
        if span % cand == 0:
            tr = cand
            break
    per = span // tr
    c1 = 1.0 - ADAM_B1 ** ADAM_STEP
    c2 = 1.0 - ADAM_B2 ** ADAM_STEP

    def first_block(p):
        return 0 if half is None else (p[1] if half == "own" else 1 - p[1]) * per

    def body(p_ref, w_ref, g_ref, m_ref, v_ref, *refs):
        go_ref, d_ref, nm_ref, nv_ref = refs[-4:]
        gv = g_ref[...]
        nm = ADAM_B1 * m_ref[...] + (1.0 - ADAM_B1) * gv
        nv = ADAM_B2 * v_ref[...] + (1.0 - ADAM_B2) * (gv * gv)
        go_ref[...] = gv
        nm_ref[...] = nm
        nv_ref[...] = nv
        d_ref[...] = -ADAM_LR * ((nm / c1) / (jnp.sqrt(nv / c2) + ADAM_EPS) + ADAM_WD * w_ref[...])

    blk = pl.BlockSpec((None, tr, cols), lambda i, p: (l, first_block(p) + i, 0))
    out = jax.ShapeDtypeStruct((layers, rows, cols), F32)
    extra = ([] if done is None else list(done)) + ([] if after is None else [after])
    aliases = {} if done is None else {5 + i: i for i in range(4)}
    return pl.pallas_call(
        body, name=name,
        grid_spec=pltpu.PrefetchScalarGridSpec(
            num_scalar_prefetch=1, grid=(per,), in_specs=[blk] * 4 + [ANY] * len(extra), out_specs=[blk] * 4),
        out_shape=[out] * 4, input_output_aliases=aliases,
        compiler_params=_params(("parallel",)),
    )(place, w, g, m, v, *extra)


LANES = 128
SUBLANES = 8
SMALL_SHAPES = {
    "norm_g": (DEPTH, D_MODEL), "sgu_ln_g": (DEPTH, D_A), "sgu_ln_b": (DEPTH, D_A),
    "sgu_w": (DEPTH, A_GROUPS, CHUNK, CHUNK), "sgu_b": (DEPTH, A_GROUPS, CHUNK), "mem_norm_g": (DEPTH, D_MODEL),
    "q_norm_g": (DEPTH, HEAD_DIM), "k_norm_g": (DEPTH, HEAD_DIM)}


def _small_layout():
    at, off = {}, 0
    for k in SMALL_NAMES:
        n = math.prod(SMALL_SHAPES[k]) // LANES
        at[k] = (off, n)
        off += -(-n // SUBLANES) * SUBLANES
    return at, off, -(-(off + SUBLANES) // (2 * SUBLANES)) * 2 * SUBLANES


def _pack_small(parts, loss=None):
    at, loss_row, rows = _small_layout()
    pieces = []
    for k in SMALL_NAMES:
        n = at[k][1]
        pieces.append(jnp.pad(parts[k].reshape(n, LANES), ((0, -(-n // SUBLANES) * SUBLANES - n), (0, 0))))
    tile = jnp.zeros((SUBLANES, LANES), F32) if loss is None else jnp.broadcast_to(loss.reshape(1, 1), (SUBLANES, LANES))
    pieces += [tile, jnp.zeros((rows - loss_row - SUBLANES, LANES), F32)]
    return jnp.concatenate(pieces)


def _adamw_small(w, g, m, v):
    at, _, rows = _small_layout()
    c1 = 1.0 - ADAM_B1 ** ADAM_STEP
    c2 = 1.0 - ADAM_B2 ** ADAM_STEP
    n_names = len(SMALL_NAMES)

    def body(w_ref, g_ref, m_ref, v_ref, *refs):
        outs, (d_ref, nm_ref, nv_ref) = refs[:4 * n_names], refs[4 * n_names:]
        gv = g_ref[...]
        nm = ADAM_B1 * m_ref[...] + (1.0 - ADAM_B1) * gv
        nv = ADAM_B2 * v_ref[...] + (1.0 - ADAM_B2) * (gv * gv)
        nm_ref[...] = nm
        nv_ref[...] = nv
        d_ref[...] = -ADAM_LR * ((nm / c1) / (jnp.sqrt(nv / c2) + ADAM_EPS) + ADAM_WD * w_ref[...])
        for kind, src in enumerate((g_ref, d_ref, nm_ref, nv_ref)):
            for i, k in enumerate(SMALL_NAMES):
                o_ref = outs[kind * n_names + i]
                first, n = at[k]
                shape = SMALL_SHAPES[k]
                if shape[-1] == LANES:
                    o_ref[...] = src[pl.ds(first, n), :].reshape(shape)
                else:
                    per = shape[-1] // LANES
                    for r in range(n):
                        o_ref[pl.ds(r // per, 1), pl.ds((r % per) * LANES, LANES)] = src[pl.ds(first + r, 1), :]

    out_shape = [jax.ShapeDtypeStruct(SMALL_SHAPES[k], F32) for _ in range(4) for k in SMALL_NAMES]
    outs = pl.pallas_call(
        body, name="adamw_small", out_shape=out_shape,
        scratch_shapes=[pltpu.VMEM((rows, LANES), F32)] * 3, compiler_params=_params(None),
    )(w, g, m, v)
    return [dict(zip(SMALL_NAMES, outs[kind * n_names:(kind + 1) * n_names])) for kind in range(4)]


WEIGHT_ORDER = ("norm_g", "w_in", "sgu_ln_g", "sgu_ln_b", "sgu_w", "sgu_b", "mem_norm_g", "w_mem_kv", "q_norm_g",
                "k_norm_g", "w_out")


def kernel(x, mem, norm_g, w_in, sgu_ln_g, sgu_ln_b, sgu_w, sgu_b, mem_norm_g, w_mem_kv, q_norm_g, k_norm_g, w_out, loss_target, m_norm_g, m_w_in, m_sgu_ln_g, m_sgu_ln_b, m_sgu_w, m_sgu_b, m_mem_norm_g, m_w_mem_kv, m_q_norm_g, m_k_norm_g, m_w_out, v_norm_g, v_w_in, v_sgu_ln_g, v_sgu_ln_b, v_sgu_w, v_sgu_b, v_mem_norm_g, v_w_mem_kv, v_q_norm_g, v_k_norm_g, v_w_out):
    weights = dict(norm_g=norm_g, w_in=w_in, sgu_ln_g=sgu_ln_g, sgu_ln_b=sgu_ln_b, sgu_w=sgu_w, sgu_b=sgu_b,
                   mem_norm_g=mem_norm_g, w_mem_kv=w_mem_kv, q_norm_g=q_norm_g, k_norm_g=k_norm_g, w_out=w_out)
    mom_m = dict(norm_g=m_norm_g, w_in=m_w_in, sgu_ln_g=m_sgu_ln_g, sgu_ln_b=m_sgu_ln_b, sgu_w=m_sgu_w, sgu_b=m_sgu_b,
                 mem_norm_g=m_mem_norm_g, w_mem_kv=m_w_mem_kv, q_norm_g=m_q_norm_g, k_norm_g=m_k_norm_g, w_out=m_w_out)
    mom_v = dict(norm_g=v_norm_g, w_in=v_w_in, sgu_ln_g=v_sgu_ln_g, sgu_ln_b=v_sgu_ln_b, sgu_w=v_sgu_w, sgu_b=v_sgu_b,
                 mem_norm_g=v_mem_norm_g, w_mem_kv=v_w_mem_kv, q_norm_g=v_q_norm_g, k_norm_g=v_k_norm_g, w_out=v_w_out)
    big = ("w_in", "w_mem_kv", "w_out")
    sm = {k: weights[k] for k in SMALL_NAMES}

    place = _place_index()
    xs, mems, target = x[0], mem[0], loss_target[0]

    slots = [[_cast_into_slot(f"cast_{k}_{l}", weights[k], l, place) for k in big] for l in range(DEPTH)]
    saved = [None] * DEPTH

    chips, cores = {}, {}
    me = place[0]
    arrival = jnp.stack([me, me ^ 2, me ^ 1, 3 - me]).astype(jnp.int32)
    shard_order = jnp.arange(N_CHIPS, dtype=jnp.int32)

    def start_gather(l, after=None):
        chips[l, "in"] = _gather_start(f"gather_start_{l}_in", slots[l][:1], after)
        chips[l, "rest"] = _gather_start(f"gather_start_{l}_rest", slots[l][1:], chips[l, "in"][3])
        return chips[l, "rest"][3]

    def hand_to_sibling(l, group, after):
        send_sems, recv_sems, bufs, _ = chips[l, group]
        bufs = _gather_wait(f"gather_wait_{l}_{group}", send_sems, recv_sems, bufs, after)
        cores[l, group] = _gather_forward_start(f"gather_forward_{l}_{group}", bufs)
        return cores[l, group][3]

    def whole(l, group, after):
        send_sems, recv_sems, bufs, _ = cores[l, group]
        return _gather_wait(f"gather_whole_{l}_{group}", send_sems, recv_sems, bufs, after)

    later_slots = [s for layer in slots[1:] for s in layer]

    class Gathered:
        def __init__(self, l):
            self.l = l
            self.buf = None

        def landed_from(self, tag, peers, after, behind, then=None):
            send_sems, recv_sems, _, _ = chips[0, "in_" + tag]
            buf = _gather_wait(f"gather_wait_0_in_{tag}", send_sems, recv_sems, self.buf, after, peers)
            if then is not None:
                buf, more = then(buf)
                behind = behind + more
            send_sems, recv_sems, buf, token = _gather_forward_start(f"gather_forward_0_in_{tag}", buf, peers)
            self.buf = _gather_wait(f"gather_whole_0_in_{tag}", send_sems, recv_sems, buf, [token] + behind, peers)

        def w_in(self, stage, h, proj):
            if self.l > 0:
                return (whole(self.l, "in", h)[0], shard_order, 0, N_CHIPS) if stage == 0 else None
            if stage == 0:
                self.buf = chips[0, "in_n"][2]
                return self.buf[0], arrival, 0, 1
            if stage == 1:
                def start_others(buf):
                    chips[0, "in_d"] = _gather_start("gather_start_0_in_d", buf, None, DIAGONAL)
                    chips[0, "rest"] = _gather_start("gather_start_0_rest", slots[0][1:], chips[0, "in_d"][3])
                    return chips[0, "in_d"][2], [chips[0, "rest"][3]]

                self.landed_from("n", NEIGHBOURS, proj, later_slots + [chips[0, "in_n"][3]], start_others)
                return self.buf[0], arrival, 1, 2
            if stage == 2:
                self.landed_from("d", DIAGONAL, [proj, chips[0, "rest"][3]], [])
                return self.buf[0], arrival, 3, 1
            return None

        def rest_start(self, proj):
            token = hand_to_sibling(self.l, "rest", proj)
            return start_gather(self.l + 1, token) if self.l + 1 < DEPTH else token

        def rest_finish(self, o_b):
            w_kv_all, w_out_all = whole(self.l, "rest", o_b)
            return w_kv_all, w_out_all, None

        def before_out(self, y):
            return hand_to_sibling(self.l + 1, "in", y) if self.l + 1 < DEPTH else None

    chips[0, "in_n"] = _gather_start("gather_start_0_in_n", slots[0][:1], None, NEIGHBOURS)
    cur = xs
    for l in range(DEPTH):
        cur, saved[l] = _layer_fwd(l, cur, mems, sm, Gathered(l))
    dxo, dxo_b, loss_part = _loss_and_grad("loss", cur, target, min(256, xs.shape[0]))

    small_g = [None] * DEPTH
    flight = {}

    class Exchange:
        def __init__(self):
            self.cores = {}

        def start(self, l, group, gives):
            *self.cores[l, group], token = _core_exchange_start(f"grad_core_start_{l}_{group}", gives)
            return token

        def landed(self, l, group, after):
            send_sems, recv_sems, bufs = self.cores[l, group]
            return _core_exchange_wait(f"grad_core_wait_{l}_{group}", send_sems, recv_sems, bufs, after)[1]

        def send(self, l, group, parts):
            *flight[l, group], token = _chip_exchange_start(f"grad_chip_start_{l}_{group}", parts)
            return token

    exchange = Exchange()
    for l in reversed(range(DEPTH)):
        dxo, dxo_b, small_g[l] = _layer_bwd(l, dxo, dxo_b, mems, sm, saved[l], place, exchange)
    grad_x = dxo

    groups = (("out", ("w_out",)), ("in", ("w_in", "w_mem_kv")))
    halves, stepped = dict.fromkeys(big), dict.fromkeys(big)
    small_g = {k: jnp.stack([small_g[l][k] for l in range(DEPTH)]) for k in SMALL_NAMES}
    after = grad_x
    sharing = {}

    def reduce_group(l, group, names):
        nonlocal after
        send_sems, recv_sems, bufs = flight[l, group]
        bufs = _chip_exchange_wait(f"grad_chip_wait_{l}_{group}", send_sems, recv_sems, bufs, after)
        for t, k in enumerate(names):
            halves[k] = _sum_chips(f"grad_chip_sum_{l}_{k}", bufs[t], bufs[len(names) + t], place, l, halves[k])
        *sharing[l, group], after = _core_share_start(f"grad_core_share_{l}_{group}", [halves[k] for k in names], l)

    def step(l, k, buf, half):
        nonlocal after
        tag = "" if half is None else "_" + half
        stepped[k] = _adamw(f"adamw_{k}_{l}{tag}", weights[k], buf, mom_m[k], mom_v[k], place, l, half, stepped[k],
                            after)
        after = stepped[k][1]

    def step_group(l, group, names, overlap):
        nonlocal after
        send_sems, recv_sems, bufs = sharing[l, group]
        if overlap:
            for k, buf in zip(names, bufs):
                step(l, k, buf, "own")
        bufs = _core_share_wait(f"grad_core_shared_{l}_{group}", send_sems, recv_sems, bufs, l, after)
        for k, buf in zip(names, bufs):
            halves[k] = buf
            step(l, k, buf, "other" if overlap else None)

    for l in reversed(range(DEPTH)):
        last = l == 0
        (g_out, n_out), (g_in, n_in) = groups
        reduce_group(l, g_out, n_out)
        if last:
            step_group(l, g_out, n_out, False)
            small_sum = _all_reduce_small(_pack_small(small_g, loss_part), after)
            small_step = _adamw_small(_pack_small(sm), small_sum, _pack_small({k: mom_m[k] for k in SMALL_NAMES}),
                                      _pack_small({k: mom_v[k] for k in SMALL_NAMES}))
            after = small_step[1]["sgu_w"]
        reduce_group(l, g_in, n_in)
        if not last:
            step_group(l, g_out, n_out, False)
        step_group(l, g_in, n_in, last)

    grads, delta, new_m, new_v = ({k: stepped[k][i] for k in big} for i in range(4))
    for out, small in zip((grads, delta, new_m, new_v), small_step):
        out.update(small)
    loss = small_sum[_small_layout()[1], 0]
    return (loss, grad_x[None], *[grads[k] for k in WEIGHT_ORDER], *[delta[k] for k in WEIGHT_ORDER],
            *[new_m[k] for k in WEIGHT_ORDER], *[new_v[k] for k in WEIGHT_ORDER])
```

```python
import functools
import math

import jax
import jax.numpy as jnp
from jax import lax
from jax.experimental import pallas as pl
from jax.experimental.pallas import tpu as pltpu

F32 = jnp.float32
BF16 = jnp.bfloat16
MESH = pl.DeviceIdType.MESH

D_MODEL = 2048
DEPTH = 2
CHUNK = 128
D_A = 1024
A_GROUPS = 8
D_B = 512
D_C = 512
HEADS = 4
HEAD_DIM = 128
IN_WIDTH = 6144
N_CHIPS = 4
EPS = 1e-6
ATT_SCALE = 1.0 / math.sqrt(HEAD_DIM)

OFF_U, OFF_V, OFF_ZA = 0, 1024, 2048
OFF_QB, OFF_KB, OFF_VB, OFF_ZB = 3072, 3584, 4096, 4608
OFF_QC, OFF_ZC = 5120, 5632
OFF_YB, OFF_YC = 1024, 1536

ADAM_LR = 0.001
ADAM_B1 = 0.9
ADAM_B2 = 0.999
ADAM_EPS = 1e-08
ADAM_WD = 0.01
ADAM_STEP = 10

MIB = 1024 * 1024
ANY = pl.BlockSpec(memory_space=pl.ANY)


def _params(semantics=None, vmem_mb=48):
    return pltpu.CompilerParams(dimension_semantics=semantics, vmem_limit_bytes=vmem_mb * MIB)


def _gelu(x):
    return 0.5 * x * (1.0 + lax.erf(x * (1.0 / math.sqrt(2.0))))


def _gelu_grad(x):
    cdf = 0.5 * (1.0 + lax.erf(x * (1.0 / math.sqrt(2.0))))
    pdf = jnp.exp(-0.5 * x * x) * (1.0 / math.sqrt(2.0 * math.pi))
    return cdf + x * pdf


def _sigmoid(x):
    return 1.0 / (1.0 + jnp.exp(-x))


def _silu_and_grad(z):
    s = _sigmoid(z)
    return z * s, s * (1.0 + z * (1.0 - s))


def _split_bf16(x):
    hi = x.astype(BF16)
    lo = (x - hi.astype(F32)).astype(BF16)
    return hi, lo


def _dot(a, b, dims):
    return lax.dot_general(a, b, (dims, ((), ())), preferred_element_type=F32)


NN = ((1,), (0,))
NT = ((1,), (1,))
TN = ((0,), (0,))


def _matmul(name, a, b, *, grid, a_spec, b_spec, o_spec, out_shape, dims, res=None, res_spec=None, after=None,
            place=None, into=None, vmem_mb=48):
    nk = grid[2]
    n_in = 2 + (res is not None) + (after is not None) + (into is not None)

    def body(*refs):
        if place is not None:
            refs = refs[1:]
        a_ref, b_ref = refs[0], refs[1]
        r_ref = refs[2] if res is not None else None
        o_ref = refs[n_in]
        if len(b_ref.shape) == 3 and dims == NN:
            part = _dot(a_ref[...], b_ref[...].reshape(-1, b_ref.shape[-1]), dims)
        elif len(b_ref.shape) == 3:
            width = b_ref.shape[-1]
            part = None
            for s in range(b_ref.shape[0]):
                term = _dot(a_ref[:, s * width:(s + 1) * width], b_ref[s], dims)
                part = term if part is None else part + term
        else:
            part = _dot(a_ref[...], b_ref[...], dims)
        if nk == 1:
            if r_ref is not None:
                part = part + r_ref[...]
            o_ref[...] = part.astype(o_ref.dtype)
            return
        acc_ref = refs[n_in + 1]
        k = pl.program_id(2)

        @pl.when(k == 0)
        def _():
            acc_ref[...] = part

        @pl.when(k > 0)
        def _():
            acc_ref[...] += part

        @pl.when(k == nk - 1)
        def _():
            tot = acc_ref[...]
            if r_ref is not None:
                tot = tot + r_ref[...]
            o_ref[...] = tot.astype(o_ref.dtype)

    in_specs = [a_spec, b_spec]
    args = [a, b]
    if res is not None:
        in_specs.append(res_spec)
        args.append(res)
    if after is not None:
        in_specs.append(ANY)
        args.append(after)
    aliases = {}
    if into is not None:
        in_specs.append(ANY)
        args.append(into)
        aliases = {len(args) - 1 + (place is not None): 0}
    acc_shape = tuple(d for d in o_spec.block_shape if d is not None)
    scratch = [pltpu.VMEM(acc_shape, F32)] if nk > 1 else []
    params = _params(("parallel", "parallel", "arbitrary"), vmem_mb)
    if place is not None:
        return pl.pallas_call(
            body, name=name, out_shape=out_shape, compiler_params=params, input_output_aliases=aliases,
            grid_spec=pltpu.PrefetchScalarGridSpec(num_scalar_prefetch=1, grid=grid, in_specs=in_specs,
                                                   out_specs=o_spec, scratch_shapes=scratch),
        )(place, *args)
    return pl.pallas_call(
        body, name=name, grid=grid, in_specs=in_specs, out_specs=o_spec, out_shape=out_shape,
        scratch_shapes=scratch, compiler_params=params, input_output_aliases=aliases,
    )(*args)


def _rms_fwd(name, x, g, tr, after=None, transposed=False):
    rows, d = x.shape

    def body(x_ref, g_ref, *refs):
        outs = refs[1:] if after is not None else refs
        xv = x_ref[...]
        r = lax.rsqrt(jnp.mean(xv * xv, axis=-1, keepdims=True) + EPS)
        h = xv * r * g_ref[...]
        outs[0][...] = h.astype(BF16)
        if transposed:
            outs[1][...] = h.T.astype(BF16)

    out_specs = [pl.BlockSpec((tr, d), lambda i: (i, 0))]
    out_shape = [jax.ShapeDtypeStruct((rows, d), BF16)]
    if transposed:
        out_specs.append(pl.BlockSpec((d, tr), lambda i: (0, i)))
        out_shape.append(jax.ShapeDtypeStruct((d, rows), BF16))
    outs = pl.pallas_call(
        body, name=name, grid=(rows // tr,),
        in_specs=[pl.BlockSpec((tr, d), lambda i: (i, 0)), pl.BlockSpec((1, d), lambda i: (0, 0))]
        + ([] if after is None else [ANY]),
        out_specs=out_specs, out_shape=out_shape,
        compiler_params=_params(("parallel",)),
    )(x, g, *([] if after is None else [after]))
    return outs if transposed else outs[0]


def _rms_bwd(name, x, dh, dres, g, tr, after=None):
    rows, d = x.shape

    def body(x_ref, dh_ref, dres_ref, g_ref, *refs):
        dx_ref, dxb_ref, dg_ref = refs[-3:]
        xv = x_ref[...]
        r = lax.rsqrt(jnp.mean(xv * xv, axis=-1, keepdims=True) + EPS)
        xhat = xv * r
        dhv = dh_ref[...]
        dxh = dhv * g_ref[...]
        dx = r * (dxh - xhat * jnp.mean(dxh * xhat, axis=-1, keepdims=True)) + dres_ref[...]
        dx_ref[...] = dx
        dxb_ref[...] = dx.astype(BF16)
        part = jnp.sum(dhv * xhat, axis=0, keepdims=True)

        @pl.when(pl.program_id(0) == 0)
        def _():
            dg_ref[...] = part

        @pl.when(pl.program_id(0) > 0)
        def _():
            dg_ref[...] += part

    blk = pl.BlockSpec((tr, d), lambda i: (i, 0))
    vec = pl.BlockSpec((1, d), lambda i: (0, 0))
    return pl.pallas_call(
        body, name=name, grid=(rows // tr,), in_specs=[blk, blk, blk, vec] + ([] if after is None else [ANY]),
        out_specs=[blk, blk, vec],
        out_shape=[jax.ShapeDtypeStruct((rows, d), F32), jax.ShapeDtypeStruct((rows, d), BF16),
                   jax.ShapeDtypeStruct((1, d), F32)],
        compiler_params=_params(("arbitrary",)),
    )(x, dh, dres, g, *([] if after is None else [after]))


def _rms_gain_grad(name, x, dh):
    rows, d = x.shape

    def body(x_ref, dh_ref, dg_ref):
        xv = x_ref[...]
        r = lax.rsqrt(jnp.mean(xv * xv, axis=-1, keepdims=True) + EPS)
        dg_ref[...] = jnp.sum(dh_ref[...] * xv * r, axis=0, keepdims=True)

    return pl.pallas_call(
        body, name=name, out_shape=jax.ShapeDtypeStruct((1, d), F32), compiler_params=_params(None),
    )(x, dh)


def _loss_and_grad(name, y, target, tr):
    rows, d = y.shape
    n = rows // tr

    def body(y_ref, t_ref, dx_ref, dxb_ref, loss_ref, acc_ref):
        e = y_ref[...] - t_ref[...]
        dx = e * (1.0 / d)
        dx_ref[...] = dx
        dxb_ref[...] = dx.astype(BF16)
        part = jnp.sum(e * e, axis=0, keepdims=True)
        i = pl.program_id(0)

        @pl.when(i == 0)
        def _():
            acc_ref[...] = part

        @pl.when(i > 0)
        def _():
            acc_ref[...] += part

        @pl.when(i == n - 1)
        def _():
            loss_ref[...] = jnp.sum(acc_ref[...], axis=-1, keepdims=True) * (0.5 / d)

    blk = pl.BlockSpec((tr, d), lambda i: (i, 0))
    return pl.pallas_call(
        body, name=name, grid=(n,), in_specs=[blk, blk],
        out_specs=[blk, blk, pl.BlockSpec((1, 1), lambda i: (0, 0))],
        out_shape=[jax.ShapeDtypeStruct((rows, d), F32), jax.ShapeDtypeStruct((rows, d), BF16),
                   jax.ShapeDtypeStruct((1, 1), F32)],
        scratch_shapes=[pltpu.VMEM((1, d), F32)],
        compiler_params=_params(("arbitrary",)),
    )(y, target)


SB_T = 256
SB_HEADS = 4


LOG2E = 1.4426950408889634


def _sb_scores(q, kblk):
    z2 = _dot(q, kblk, NT) * (ATT_SCALE * LOG2E)
    e = jnp.exp2(-jnp.abs(z2))
    l1 = jnp.minimum(-z2, 0.0) - jnp.log2(1.0 + e)
    lb = l1 + z2
    return z2, e, lb, l1


def _sb_fwd(name, proj, after=None):
    s_len = proj.shape[0]
    t = SB_T
    nq = s_len // t

    def body(q_ref, k_ref, v_ref, *refs):
        o_ref = refs[-1]
        i = pl.program_id(1)
        row = lax.broadcasted_iota(jnp.int32, (t, t), 0)
        col = lax.broadcasted_iota(jnp.int32, (t, t), 1)
        causal = col < row
        after_mat = (row > col).astype(BF16)
        heads = [slice(hh * HEAD_DIM, (hh + 1) * HEAD_DIM) for hh in range(SB_HEADS)]
        q = [q_ref[:, sl].astype(BF16) for sl in heads]

        def tile(kb, state, masked):
            start = pl.multiple_of(kb * t, t)
            out = []
            for hh, sl in enumerate(heads):
                carry, acc = state[hh]
                kblk = k_ref[pl.ds(start, t), sl].astype(BF16)
                vblk = v_ref[pl.ds(start, t), sl].astype(BF16)
                _, _, lb, l1 = _sb_scores(q[hh], kblk)
                if masked:
                    l1 = jnp.where(causal, l1, 0.0)
                hi, lo = _split_bf16(l1)
                after = _dot(hi, after_mat, NN) + _dot(lo, after_mat, NN) + carry
                a = jnp.exp2(lb + after)
                if masked:
                    a = jnp.where(causal, a, 0.0)
                acc = acc + _dot(a.astype(BF16), vblk, NN)
                carry = carry + jnp.sum(l1, axis=-1, keepdims=True)
                out.append((carry, acc))
            return tuple(out)

        zero = (jnp.zeros((t, 1), F32), jnp.zeros((t, HEAD_DIM), F32))
        state = tile(i, (zero,) * SB_HEADS, True)
        state = lax.fori_loop(0, i, lambda n, st: tile(i - 1 - n, st, False), state)
        for hh, sl in enumerate(heads):
            o_ref[:, sl] = state[hh][1]

    cb = SB_HEADS * HEAD_DIM
    return pl.pallas_call(
        body, name=name, grid=(HEADS // SB_HEADS, nq),
        in_specs=[pl.BlockSpec((t, cb), lambda h, i: (i, OFF_QB // cb + h)),
                  pl.BlockSpec((s_len, cb), lambda h, i: (0, OFF_KB // cb + h)),
                  pl.BlockSpec((s_len, cb), lambda h, i: (0, OFF_VB // cb + h))] + ([] if after is None else [ANY]),
        out_specs=pl.BlockSpec((t, cb), lambda h, i: (i, h)),
        out_shape=jax.ShapeDtypeStruct((s_len, D_B), F32),
        compiler_params=_params(("parallel", "arbitrary")),
    )(proj, proj, proj, *([] if after is None else [after]))


def _sb_bwd(name, proj, dy, after=None):
    s_len = proj.shape[0]
    t = SB_T
    nq = s_len // t

    def body(q_ref, k_ref, v_ref, z_ref, dy_ref, *refs):
        dq_ref, dk_ref, dv_ref, a_ref, s_ref = refs[-5:]
        i = pl.program_id(1)

        @pl.when(i == 0)
        def _():
            dk_ref[...] = jnp.zeros_like(dk_ref)
            dv_ref[...] = jnp.zeros_like(dv_ref)

        heads = [slice(hh * HEAD_DIM, (hh + 1) * HEAD_DIM) for hh in range(SB_HEADS)]
        q = [q_ref[:, sl].astype(BF16) for sl in heads]
        silu_z, _ = _silu_and_grad(z_ref[...])
        do_all = dy_ref[...] * silu_z
        do_b = [do_all[:, sl].astype(BF16) for sl in heads]
        row = lax.broadcasted_iota(jnp.int32, (t, t), 0)
        col = lax.broadcasted_iota(jnp.int32, (t, t), 1)
        causal = col < row
        after_mat = (row > col).astype(BF16)
        before_mat = (row < col).astype(BF16)

        def weights(kb, carries, masked):
            start = pl.multiple_of(kb * t, t)
            out = []
            for hh, sl in enumerate(heads):
                kblk = k_ref[pl.ds(start, t), sl].astype(BF16)
                z, _, lb, l1 = _sb_scores(q[hh], kblk)
                if masked:
                    l1 = jnp.where(causal, l1, 0.0)
                hi, lo = _split_bf16(l1)
                after = _dot(hi, after_mat, NN) + _dot(lo, after_mat, NN) + carries[hh]
                a = jnp.exp2(lb + after)
                if masked:
                    a = jnp.where(causal, a, 0.0)
                a_ref[hh, kb] = a
                s_ref[hh, kb] = z
                out.append(carries[hh] + jnp.sum(l1, axis=-1, keepdims=True))
            return tuple(out)

        carries = weights(i, (jnp.zeros((t, 1), F32),) * SB_HEADS, True)
        lax.fori_loop(0, i, lambda n, c: weights(i - 1 - n, c, False), carries)

        def grads(kb, state, masked):
            start = pl.multiple_of(kb * t, t)
            out = []
            for hh, sl in enumerate(heads):
                carry, dq = state[hh]
                kblk = k_ref[pl.ds(start, t), sl].astype(BF16)
                vblk = v_ref[pl.ds(start, t), sl].astype(BF16)
                a = a_ref[hh, kb]
                z = s_ref[hh, kb]
                g = _dot(do_b[hh], vblk, NT) * a
                ghi, glo = _split_bf16(g)
                prefix = _dot(ghi, before_mat, NN) + _dot(glo, before_mat, NN) + carry
                e = jnp.exp2(-jnp.abs(z))
                inv = 1.0 / (1.0 + e)
                pos = z >= 0.0
                beta = jnp.where(pos, inv, e * inv)
                one_m_beta = jnp.where(pos, e * inv, inv)
                dz = (g * one_m_beta - prefix * beta) * ATT_SCALE
                if masked:
                    dz = jnp.where(causal, dz, 0.0)
                dz_b = dz.astype(BF16)
                dq = dq + _dot(dz_b, kblk, NN)
                dk_ref[pl.ds(start, t), sl] += _dot(dz_b, q[hh], TN)
                dv_ref[pl.ds(start, t), sl] += _dot(a.astype(BF16), do_b[hh], TN)
                out.append((carry + jnp.sum(g, axis=-1, keepdims=True), dq))
            return tuple(out)

        zero = (jnp.zeros((t, 1), F32), jnp.zeros((t, HEAD_DIM), F32))
        state = lax.fori_loop(0, i, lambda kb, st: grads(kb, st, False), (zero,) * SB_HEADS)
        state = grads(i, state, True)
        for hh, sl in enumerate(heads):
            dq_ref[:, sl] = state[hh][1]

    cb = SB_HEADS * HEAD_DIM
    qblk = lambda off: pl.BlockSpec((t, cb), lambda h, i: (i, off // cb + h))
    full = lambda off: pl.BlockSpec((s_len, cb), lambda h, i: (0, off // cb + h))
    out = jax.ShapeDtypeStruct((s_len, D_B), F32)
    return pl.pallas_call(
        body, name=name, grid=(HEADS // SB_HEADS, nq),
        in_specs=[qblk(OFF_QB), full(OFF_KB), full(OFF_VB), qblk(OFF_ZB), qblk(OFF_YB)]
        + ([] if after is None else [ANY]),
        out_specs=[qblk(0), full(0), full(0)],
        out_shape=[out, out, out],
        scratch_shapes=[pltpu.VMEM((SB_HEADS, nq, t, t), F32), pltpu.VMEM((SB_HEADS, nq, t, t), F32)],
        compiler_params=_params(("parallel", "arbitrary")),
    )(proj, proj, proj, proj, dy, *([] if after is None else [after]))


MEM_TQ = 512


def _qk_norm(x, g):
    r = lax.rsqrt(jnp.mean(x * x, axis=-1, keepdims=True) + EPS)
    xhat = x * r
    return xhat * g, xhat, r


def _qk_norm_bwd(dn, g, xhat, r):
    dxh = dn * g
    return r * (dxh - xhat * jnp.mean(dxh * xhat, axis=-1, keepdims=True))


def _mem_probs(q, mk, qg, kg):
    qn, qhat, rq = _qk_norm(q, qg)
    kn, khat, rk = _qk_norm(mk, kg)
    qn_b, kn_b = qn.astype(BF16), kn.astype(BF16)
    s = _dot(qn_b, kn_b, NT) * ATT_SCALE
    p = jnp.exp(s - jnp.max(s, axis=-1, keepdims=True))
    p = p / jnp.sum(p, axis=-1, keepdims=True)
    return p, qn_b, kn_b, qhat, rq, khat, rk


def _mem_fwd(name, proj, mem_kv, qg, kg):
    s_len = proj.shape[0]
    m_len = mem_kv.shape[0]
    tq = min(MEM_TQ, s_len)

    def body(q_ref, mk_ref, mv_ref, qg_ref, kg_ref, o_ref):
        p = _mem_probs(q_ref[...], mk_ref[...], qg_ref[...], kg_ref[...])[0]
        o_ref[...] = _dot(p.astype(BF16), mv_ref[...].astype(BF16), NN)

    cb = HEAD_DIM
    vec = pl.BlockSpec((1, cb), lambda h, i: (0, 0))
    return pl.pallas_call(
        body, name=name, grid=(HEADS, s_len // tq),
        in_specs=[pl.BlockSpec((tq, cb), lambda h, i: (i, OFF_QC // cb + h)),
                  pl.BlockSpec((m_len, cb), lambda h, i: (0, h)),
                  pl.BlockSpec((m_len, cb), lambda h, i: (0, HEADS + h)), vec, vec],
        out_specs=pl.BlockSpec((tq, cb), lambda h, i: (i, h)),
        out_shape=jax.ShapeDtypeStruct((s_len, D_C), F32),
        compiler_params=_params(("parallel", "parallel")),
    )(proj, mem_kv, mem_kv, qg, kg)


def _mem_bwd(name, proj, mem_kv, qg, kg, dy):
    s_len = proj.shape[0]
    m_len = mem_kv.shape[0]
    tq = min(MEM_TQ, s_len)

    def body(q_ref, mk_ref, mv_ref, qg_ref, kg_ref, z_ref, dy_ref, dq_ref, dmk_ref, dmv_ref, dqg_ref, dkg_ref):
        h, i = pl.program_id(0), pl.program_id(1)

        @pl.when(i == 0)
        def _():
            dmk_ref[...] = jnp.zeros_like(dmk_ref)
            dmv_ref[...] = jnp.zeros_like(dmv_ref)

        @pl.when((i == 0) & (h == 0))
        def _():
            dqg_ref[...] = jnp.zeros_like(dqg_ref)
            dkg_ref[...] = jnp.zeros_like(dkg_ref)

        qg, kg = qg_ref[...], kg_ref[...]
        p, qn_b, kn_b, qhat, rq, khat, rk = _mem_probs(q_ref[...], mk_ref[...], qg, kg)
        silu_z, _ = _silu_and_grad(z_ref[...])
        do_b = (dy_ref[...] * silu_z).astype(BF16)
        dmv_ref[...] += _dot(p.astype(BF16), do_b, TN)
        dp = _dot(do_b, mv_ref[...].astype(BF16), NT)
        ds = (p * (dp - jnp.sum(dp * p, axis=-1, keepdims=True)) * ATT_SCALE).astype(BF16)
        dqn = _dot(ds, kn_b, NN)
        dkn = _dot(ds, qn_b, TN)
        dq_ref[...] = _qk_norm_bwd(dqn, qg, qhat, rq)
        dmk_ref[...] += _qk_norm_bwd(dkn, kg, khat, rk)
        dqg_ref[...] += jnp.sum(dqn * qhat, axis=0, keepdims=True)
        dkg_ref[...] += jnp.sum(dkn * khat, axis=0, keepdims=True)

    cb = HEAD_DIM
    vec = pl.BlockSpec((1, cb), lambda h, i: (0, 0))
    qblk = lambda off: pl.BlockSpec((tq, cb), lambda h, i: (i, off // cb + h))
    memblk = lambda off: pl.BlockSpec((m_len, cb), lambda h, i: (0, off + h))
    return pl.pallas_call(
        body, name=name, grid=(HEADS, s_len // tq),
        in_specs=[qblk(OFF_QC), memblk(0), memblk(HEADS), vec, vec, qblk(OFF_ZC), qblk(OFF_YC)],
        out_specs=[qblk(0), memblk(0), memblk(0), vec, vec],
        out_shape=[jax.ShapeDtypeStruct((s_len, D_C), F32), jax.ShapeDtypeStruct((m_len, D_C), F32),
                   jax.ShapeDtypeStruct((m_len, D_C), F32), jax.ShapeDtypeStruct((1, cb), F32),
                   jax.ShapeDtypeStruct((1, cb), F32)],
        compiler_params=_params(("arbitrary", "arbitrary")),
    )(proj, mem_kv, mem_kv, qg, kg, proj, dy)


def _sgu_common(u_ref, v_ref, lng_ref, lnb_ref, w_ref, bias_ref):
    ug = _gelu(u_ref[...])
    vg = _gelu(v_ref[...])
    mu = jnp.mean(vg, axis=-1, keepdims=True)
    xc = vg - mu
    rstd = lax.rsqrt(jnp.mean(xc * xc, axis=-1, keepdims=True) + EPS)
    xhat = xc * rstd
    vn = xhat * lng_ref[...] + lnb_ref[...]
    vn_b = vn.astype(BF16)
    row = lax.broadcasted_iota(jnp.int32, (CHUNK, CHUNK), 0)
    col = lax.broadcasted_iota(jnp.int32, (CHUNK, CHUNK), 1)
    tril = row >= col
    mixed = []
    for g in range(A_GROUPS):
        w = jnp.where(tril, w_ref[g], 0.0).astype(BF16)
        sl = slice(g * CHUNK, (g + 1) * CHUNK)
        mixed.append(_dot(w, vn_b[:, sl], NN) + bias_ref[:, sl])
    return ug, xhat, rstd, vn_b, mixed, tril


def _gate_fwd(name, proj, o_b, o_c, lng, lnb, w_s, bias):
    s_len = proj.shape[0]

    def body(u_ref, v_ref, za_ref, zb_ref, zc_ref, ob_ref, oc_ref, lng_ref, lnb_ref, w_ref, bias_ref, y_ref, yt_ref):
        ug, _, _, _, mixed, _ = _sgu_common(u_ref, v_ref, lng_ref, lnb_ref, w_ref, bias_ref)
        sza, _ = _silu_and_grad(za_ref[...])
        gate = ug * sza

        def put(off, width, val):
            y_ref[:, off:off + width] = val.astype(BF16)
            yt_ref[off:off + width, :] = val.T.astype(BF16)

        for g in range(A_GROUPS):
            sl = slice(g * CHUNK, (g + 1) * CHUNK)
            put(g * CHUNK, CHUNK, gate[:, sl] * mixed[g])
        szb, _ = _silu_and_grad(zb_ref[...])
        put(OFF_YB, D_B, ob_ref[...] * szb)
        szc, _ = _silu_and_grad(zc_ref[...])
        put(OFF_YC, D_C, oc_ref[...] * szc)

    wide = lambda off: pl.BlockSpec((CHUNK, D_A), lambda i: (i, off // D_A))
    narrow = lambda off: pl.BlockSpec((CHUNK, D_B), lambda i: (i, off // D_B))
    vec = pl.BlockSpec((1, D_A), lambda i: (0, 0))
    return pl.pallas_call(
        body, name=name, grid=(s_len // CHUNK,),
        in_specs=[wide(OFF_U), wide(OFF_V), wide(OFF_ZA), narrow(OFF_ZB), narrow(OFF_ZC), narrow(0), narrow(0), vec, vec,
                  pl.BlockSpec((A_GROUPS, CHUNK, CHUNK), lambda i: (0, 0, 0)),
                  pl.BlockSpec((CHUNK, D_A), lambda i: (0, 0))],
        out_specs=[pl.BlockSpec((CHUNK, D_MODEL), lambda i: (i, 0)), pl.BlockSpec((D_MODEL, CHUNK), lambda i: (0, i))],
        out_shape=[jax.ShapeDtypeStruct((s_len, D_MODEL), BF16), jax.ShapeDtypeStruct((D_MODEL, s_len), BF16)],
        compiler_params=_params(("parallel",)),
    )(proj, proj, proj, proj, proj, o_b, o_c, lng, lnb, w_s, bias)


def _gate_bwd(name, proj, dy, o_b, o_c, dqkv, dq_c, lng, lnb, w_s, w_s_t, bias):
    s_len = proj.shape[0]
    n = s_len // CHUNK
    dq_b, dk_b, dv_b = dqkv

    def body(u_ref, v_ref, za_ref, zb_ref, zc_ref, dya_ref, dyb_ref, dyc_ref, ob_ref, oc_ref, dq_ref, dk_ref, dv_ref,
             dqc_ref, lng_ref, lnb_ref, w_ref, wt_ref, bias_ref, dp_ref, dw_ref, dsb_ref, dlng_ref, dlnb_ref, dbias_ref):
        i = pl.program_id(0)

        @pl.when(i == 0)
        def _():
            dw_ref[...] = jnp.zeros_like(dw_ref)
            dbias_ref[...] = jnp.zeros_like(dbias_ref)
            dlng_ref[...] = jnp.zeros_like(dlng_ref)
            dlnb_ref[...] = jnp.zeros_like(dlnb_ref)

        ug, xhat, rstd, vn_b, mixed, tril = _sgu_common(u_ref, v_ref, lng_ref, lnb_ref, w_ref, bias_ref)
        za = za_ref[...]
        sza, dsza = _silu_and_grad(za)
        dya = dya_ref[...]
        mixed_all = jnp.concatenate(mixed, axis=-1)
        d_mixed = dya * ug * sza
        dp_ref[:, OFF_U:OFF_U + D_A] = (dya * mixed_all * sza * _gelu_grad(u_ref[...])).astype(BF16)
        dp_ref[:, OFF_ZA:OFF_ZA + D_A] = (dya * ug * mixed_all * dsza).astype(BF16)
        dbias_ref[...] += d_mixed
        dm_b = d_mixed.astype(BF16)
        triu = lax.broadcasted_iota(jnp.int32, (CHUNK, CHUNK), 0) <= lax.broadcasted_iota(jnp.int32, (CHUNK, CHUNK), 1)
        d_vn = []
        for g in range(A_GROUPS):
            sl = slice(g * CHUNK, (g + 1) * CHUNK)
            wt = jnp.where(triu, wt_ref[g], 0.0).astype(BF16)
            d_vn.append(_dot(wt, dm_b[:, sl], NN))
            dw_ref[g] += jnp.where(tril, _dot(dm_b[:, sl], vn_b[:, sl], NT), 0.0)
        d_vn = jnp.concatenate(d_vn, axis=-1)
        dlng_ref[...] += jnp.sum(d_vn * xhat, axis=0, keepdims=True)
        dlnb_ref[...] += jnp.sum(d_vn, axis=0, keepdims=True)
        dxh = d_vn * lng_ref[...]
        d_vg = rstd * (dxh - jnp.mean(dxh, axis=-1, keepdims=True)
                       - xhat * jnp.mean(dxh * xhat, axis=-1, keepdims=True))
        dp_ref[:, OFF_V:OFF_V + D_A] = (d_vg * _gelu_grad(v_ref[...])).astype(BF16)
        dp_ref[:, OFF_QB:OFF_QB + D_B] = dq_ref[...].astype(BF16)
        dp_ref[:, OFF_KB:OFF_KB + D_B] = dk_ref[...].astype(BF16)
        dp_ref[:, OFF_VB:OFF_VB + D_B] = dv_ref[...].astype(BF16)
        _, dszb = _silu_and_grad(zb_ref[...])
        dp_ref[:, OFF_ZB:OFF_ZB + D_B] = (dyb_ref[...] * ob_ref[...] * dszb).astype(BF16)
        dp_ref[:, OFF_QC:OFF_QC + D_C] = dqc_ref[...].astype(BF16)
        _, dszc = _silu_and_grad(zc_ref[...])
        dp_ref[:, OFF_ZC:OFF_ZC + D_C] = (dyc_ref[...] * oc_ref[...] * dszc).astype(BF16)

        @pl.when(i == n - 1)
        def _():
            ch = lax.broadcasted_iota(jnp.int32, (D_A, CHUNK), 0)
            gcol = lax.broadcasted_iota(jnp.int32, (D_A, CHUNK), 1)
            pick = (ch // (D_A // A_GROUPS) == gcol).astype(BF16)
            rest = dbias_ref[...]
            tot = jnp.zeros((CHUNK, CHUNK), F32)
            for _ in range(3):
                term = rest.astype(BF16)
                tot = tot + _dot(term, pick, NN)
                rest = rest - term.astype(F32)
            dsb_ref[...] = tot

    wide = lambda off: pl.BlockSpec((CHUNK, D_A), lambda i: (i, off // D_A))
    narrow = lambda off: pl.BlockSpec((CHUNK, D_B), lambda i: (i, off // D_B))
    vec = pl.BlockSpec((1, D_A), lambda i: (0, 0))
    wspec = pl.BlockSpec((A_GROUPS, CHUNK, CHUNK), lambda i: (0, 0, 0))
    bspec = pl.BlockSpec((CHUNK, D_A), lambda i: (0, 0))
    return pl.pallas_call(
        body, name=name, grid=(n,),
        in_specs=[wide(OFF_U), wide(OFF_V), wide(OFF_ZA), narrow(OFF_ZB), narrow(OFF_ZC),
                  wide(0), narrow(OFF_YB), narrow(OFF_YC), narrow(0), narrow(0), narrow(0), narrow(0), narrow(0),
                  narrow(0), vec, vec, wspec, wspec, bspec],
        out_specs=[pl.BlockSpec((CHUNK, IN_WIDTH), lambda i: (i, 0)), wspec,
                   pl.BlockSpec((CHUNK, CHUNK), lambda i: (0, 0)), vec, vec],
        out_shape=[jax.ShapeDtypeStruct((s_len, IN_WIDTH), BF16), jax.ShapeDtypeStruct((A_GROUPS, CHUNK, CHUNK), F32),
                   jax.ShapeDtypeStruct((CHUNK, CHUNK), F32), jax.ShapeDtypeStruct((1, D_A), F32),
                   jax.ShapeDtypeStruct((1, D_A), F32)],
        scratch_shapes=[pltpu.VMEM((CHUNK, D_A), F32)],
        compiler_params=_params(("arbitrary",)),
    )(proj, proj, proj, proj, proj, dy, dy, dy, o_b, o_c, dq_b, dk_b, dv_b, dq_c, lng, lnb, w_s, w_s_t, bias)


IN_SHARD = IN_WIDTH // N_CHIPS
ROW_SHARD = D_MODEL // N_CHIPS


def _bias_rows(sgu_b_l):
    return jnp.repeat(sgu_b_l.T, D_A // A_GROUPS, axis=1)


class _WholeWeights:
    def __init__(self, w_in_all, w_kv_all, w_out_all):
        self.weights = (w_in_all, w_kv_all, w_out_all)

    def w_in(self, stage, h, proj):
        return (self.weights[0], jnp.arange(N_CHIPS, dtype=jnp.int32), 0, N_CHIPS) if stage == 0 else None

    def rest_start(self, proj):
        return None

    def rest_finish(self, o_b):
        return self.weights[1], self.weights[2], None

    def before_out(self, y):
        return None


def _layer_fwd(l, x, mem, sm, hooks):
    s_len = x.shape[0]
    m_len = mem.shape[0]
    tm = min(1024, s_len)
    h, h_t = _rms_fwd(f"rms_fwd_{l}", x, sm["norm_g"][l][None], min(256, s_len), transposed=True)
    proj, stage = None, 0
    while (ready := hooks.w_in(stage, h, proj)) is not None:
        w_in_all, order, first, count = ready
        proj = _matmul(
            f"in_proj_{l}_{stage}", h, w_in_all, grid=(s_len // tm, count, 1), place=order, into=proj,
            a_spec=pl.BlockSpec((tm, D_MODEL), lambda i, j, k, p: (i, 0)),
            b_spec=pl.BlockSpec((None, D_MODEL, IN_SHARD), lambda i, j, k, p: (p[first + j], 0, 0)),
            o_spec=pl.BlockSpec((tm, IN_SHARD), lambda i, j, k, p: (i, p[first + j])),
            out_shape=jax.ShapeDtypeStruct((s_len, IN_WIDTH), F32), dims=NN)
        stage += 1
    o_b = _sb_fwd(f"sb_fwd_{l}", proj, hooks.rest_start(proj))
    w_kv_all, w_out_all, after = hooks.rest_finish(o_b)
    mem_h = _rms_fwd(f"mem_rms_fwd_{l}", mem, sm["mem_norm_g"][l][None], m_len, after)
    mem_kv = _matmul(
        f"mem_kv_{l}", mem_h, w_kv_all, grid=(1, 2, N_CHIPS),
        a_spec=pl.BlockSpec((m_len, ROW_SHARD), lambda i, j, k: (0, k)),
        b_spec=pl.BlockSpec((None, ROW_SHARD, D_C), lambda i, j, k: (k, 0, j)),
        o_spec=pl.BlockSpec((m_len, D_C), lambda i, j, k: (0, j)),
        out_shape=jax.ShapeDtypeStruct((m_len, 2 * D_C), F32), dims=NN)
    qg, kg = sm["q_norm_g"][l][None], sm["k_norm_g"][l][None]
    o_c = _mem_fwd(f"mem_fwd_{l}", proj, mem_kv, qg, kg)
    bias = _bias_rows(sm["sgu_b"][l])
    y, y_t = _gate_fwd(f"gate_fwd_{l}", proj, o_b, o_c, sm["sgu_ln_g"][l][None], sm["sgu_ln_b"][l][None],
                       sm["sgu_w"][l], bias)
    tn_o = 512
    x_next = _matmul(
        f"out_proj_{l}", y, w_out_all, grid=(s_len // tm, D_MODEL // tn_o, 1),
        a_spec=pl.BlockSpec((tm, D_MODEL), lambda i, j, k: (i, 0)),
        b_spec=pl.BlockSpec((N_CHIPS, ROW_SHARD, tn_o), lambda i, j, k: (0, 0, j)),
        o_spec=pl.BlockSpec((tm, tn_o), lambda i, j, k: (i, j)),
        out_shape=jax.ShapeDtypeStruct((s_len, D_MODEL), F32), dims=NN,
        res=x, res_spec=pl.BlockSpec((tm, tn_o), lambda i, j, k: (i, j)), after=hooks.before_out(y))
    saved = dict(x=x, h_t=h_t, proj=proj, mem_h=mem_h, mem_kv=mem_kv, o_b=o_b, o_c=o_c, y_t=y_t, bias=bias,
                 weights=(w_in_all, w_kv_all, w_out_all))
    return x_next, saved


class _NoExchange:
    def __init__(self):
        self.gave, self.kept = {}, {}

    def start(self, l, group, gives):
        self.gave[l, group] = gives
        return None

    def landed(self, l, group, after):
        return [jnp.zeros_like(g) for g in self.gave[l, group]]

    def send(self, l, group, parts):
        self.kept[l, group] = parts
        return None


def _layer_bwd(l, dxo, dxo_b, mem, sm, saved, place, exchange):
    s_len = dxo.shape[0]
    m_len = mem.shape[0]
    proj, y_t, h_t, mem_h, mem_kv = saved["proj"], saved["y_t"], saved["h_t"], saved["mem_h"], saved["mem_kv"]
    w_in_all, w_kv_all, w_out_all = saved["weights"]
    tm = min(1024, s_len)
    tn = 768
    per = IN_SHARD // tn
    half_rows = ROW_SHARD // 2

    def halves(make):
        give = lambda: make("give", lambda p: 1 - p[1], None, F32)
        keep = lambda theirs: make("keep", lambda p: p[1], theirs, BF16)
        return give, keep

    def grad_out(tag, half, theirs, dtype):
        o_spec = pl.BlockSpec((None, half_rows, 1024), lambda i, j, k, p: (i, 0, j))
        return _matmul(
            f"d_w_out_{l}_{tag}", y_t, dxo_b, grid=(N_CHIPS, D_MODEL // 1024, 1), place=place,
            a_spec=pl.BlockSpec((half_rows, s_len), lambda i, j, k, p: (2 * i + half(p), 0)),
            b_spec=pl.BlockSpec((s_len, 1024), lambda i, j, k, p: (0, j)), o_spec=o_spec,
            out_shape=jax.ShapeDtypeStruct((N_CHIPS, half_rows, D_MODEL), dtype), dims=NN,
            res=theirs, res_spec=o_spec)

    def grad_in(tag, half, theirs, dtype):
        o_spec = pl.BlockSpec((None, D_MODEL // 2, tn), lambda i, j, k, p: (j // per, 0, j % per))
        return _matmul(
            f"d_w_in_{l}_{tag}", h_t, dproj, grid=(1, IN_WIDTH // tn, 1), place=place,
            a_spec=pl.BlockSpec((D_MODEL // 2, s_len), lambda i, j, k, p: (half(p), 0)),
            b_spec=pl.BlockSpec((s_len, tn), lambda i, j, k, p: (0, j)), o_spec=o_spec,
            out_shape=jax.ShapeDtypeStruct((N_CHIPS, D_MODEL // 2, IN_SHARD), dtype), dims=NN,
            res=theirs, res_spec=o_spec)

    def grad_kv(tag, half, theirs, dtype):
        o_spec = pl.BlockSpec((None, half_rows, 2 * D_C), lambda i, j, k, p: (i, 0, 0))
        return _matmul(
            f"d_w_kv_{l}_{tag}", mem_h, dkv_b, grid=(N_CHIPS, 1, 1), place=place,
            a_spec=pl.BlockSpec((m_len, half_rows), lambda i, j, k, p: (0, 2 * i + half(p))),
            b_spec=pl.BlockSpec((m_len, 2 * D_C), lambda i, j, k, p: (0, 0)), o_spec=o_spec,
            out_shape=jax.ShapeDtypeStruct((N_CHIPS, half_rows, 2 * D_C), dtype), dims=TN,
            res=theirs, res_spec=o_spec)

    give_out, keep_out = halves(grad_out)
    token = exchange.start(l, "out", [give_out()])
    dy = _matmul(
        f"d_y_{l}", dxo_b, w_out_all, grid=(s_len // tm, N_CHIPS, 1),
        a_spec=pl.BlockSpec((tm, D_MODEL), lambda i, j, k: (i, 0)),
        b_spec=pl.BlockSpec((None, ROW_SHARD, D_MODEL), lambda i, j, k: (j, 0, 0)),
        o_spec=pl.BlockSpec((tm, ROW_SHARD), lambda i, j, k: (i, j)),
        out_shape=jax.ShapeDtypeStruct((s_len, D_MODEL), F32), dims=NT, after=token)
    (theirs_out,) = exchange.landed(l, "out", dy)
    token = exchange.send(l, "out", [keep_out(theirs_out)])
    qg, kg = sm["q_norm_g"][l][None], sm["k_norm_g"][l][None]
    dqkv = _sb_bwd(f"sb_bwd_{l}", proj, dy, token)
    dq_c, dmk, dmv, dqg, dkg = _mem_bwd(f"mem_bwd_{l}", proj, mem_kv, qg, kg, dy)
    w_s = sm["sgu_w"][l]
    dproj, dws, dbias, dlng, dlnb = _gate_bwd(
        f"gate_bwd_{l}", proj, dy, saved["o_b"], saved["o_c"], dqkv, dq_c, sm["sgu_ln_g"][l][None],
        sm["sgu_ln_b"][l][None], w_s, jnp.swapaxes(w_s, 1, 2), saved["bias"])
    dkv_b = jnp.concatenate([dmk, dmv], axis=1).astype(BF16)
    give_in, keep_in = halves(grad_in)
    give_kv, keep_kv = halves(grad_kv)
    token = exchange.start(l, "in", [give_in(), give_kv()])
    dh = _matmul(
        f"d_h_{l}", dproj, w_in_all, grid=(s_len // tm, D_MODEL // 512, 1),
        a_spec=pl.BlockSpec((tm, IN_WIDTH), lambda i, j, k: (i, 0)),
        b_spec=pl.BlockSpec((N_CHIPS, 512, IN_SHARD), lambda i, j, k: (0, j, 0)),
        o_spec=pl.BlockSpec((tm, 512), lambda i, j, k: (i, j)),
        out_shape=jax.ShapeDtypeStruct((s_len, D_MODEL), F32), dims=NT, after=token, vmem_mb=56)
    theirs_in, theirs_kv = exchange.landed(l, "in", dh)
    token = exchange.send(l, "in", [keep_in(theirs_in), keep_kv(theirs_kv)])
    dx, dx_b, dng = _rms_bwd(f"rms_bwd_{l}", saved["x"], dh, dxo, sm["norm_g"][l][None], min(256, s_len), token)
    d_mem_h = _matmul(
        f"d_mem_h_{l}", dkv_b, w_kv_all, grid=(1, N_CHIPS, 1),
        a_spec=pl.BlockSpec((m_len, 2 * D_C), lambda i, j, k: (0, 0)),
        b_spec=pl.BlockSpec((None, ROW_SHARD, 2 * D_C), lambda i, j, k: (j, 0, 0)),
        o_spec=pl.BlockSpec((m_len, ROW_SHARD), lambda i, j, k: (0, j)),
        out_shape=jax.ShapeDtypeStruct((m_len, D_MODEL), F32), dims=NT)
    dmng = _rms_gain_grad(f"mem_rms_bwd_{l}", mem, d_mem_h)
    dsgu_b = dbias[:, :A_GROUPS].T
    small = dict(norm_g=dng[0], sgu_ln_g=dlng[0], sgu_ln_b=dlnb[0], sgu_w=dws, sgu_b=dsgu_b, mem_norm_g=dmng[0],
                 q_norm_g=dqg[0], k_norm_g=dkg[0])
    return dx, dx_b, small


SMALL_NAMES = ("norm_g", "sgu_ln_g", "sgu_ln_b", "sgu_w", "sgu_b", "mem_norm_g", "q_norm_g", "k_norm_g")


def _local_step(x, mem, target, sm, w_all):
    saved = []
    cur = x
    for l in range(DEPTH):
        cur, sv = _layer_fwd(l, cur, mem, sm, _WholeWeights(*w_all[l]))
        saved.append(sv)
    dxo, dxo_b, loss = _loss_and_grad("loss", cur, target, min(256, x.shape[0]))
    small = [None] * DEPTH
    exchange = _NoExchange()
    place = jnp.zeros((2,), jnp.int32)
    for l in reversed(range(DEPTH)):
        dxo, dxo_b, small[l] = _layer_bwd(l, dxo, dxo_b, mem, sm, saved[l], place, exchange)
    small = {k: jnp.stack([small[l][k] for l in range(DEPTH)]) for k in SMALL_NAMES}
    return loss, dxo, small, exchange.gave, exchange.kept


def _place():
    x, y, c = lax.axis_index("x"), lax.axis_index("y"), lax.axis_index("c")
    return x, y, c


def _other_chips(x, y):
    return [(1 - x, y, 2 * (1 - x) + y), (x, 1 - y, 2 * x + 1 - y), (1 - x, 1 - y, 2 * (1 - x) + 1 - y)]


D2D_CHUNKS = 8


def _place_index():
    return jnp.stack([2 * lax.axis_index("x") + lax.axis_index("y"), lax.axis_index("c")]).astype(jnp.int32)


def _cast_into_slot(name, w, l, place):
    _, rows, cols = w.shape
    tr = min(256, rows)

    def body(p_ref, w_ref, o_ref):
        o_ref[...] = w_ref[...].astype(BF16)

    return pl.pallas_call(
        body, name=name,
        grid_spec=pltpu.PrefetchScalarGridSpec(
            num_scalar_prefetch=1, grid=(rows // tr,),
            in_specs=[pl.BlockSpec((None, tr, cols), lambda i, p: (l, i, 0))],
            out_specs=pl.BlockSpec((None, tr, cols), lambda i, p: (p[0], i, 0))),
        out_shape=jax.ShapeDtypeStruct((N_CHIPS, rows, cols), BF16),
        compiler_params=_params(("parallel",)),
    )(place, w)


HBM = pl.BlockSpec(memory_space=pltpu.HBM)
SEM = pl.BlockSpec(memory_space=pltpu.SEMAPHORE)
DATAFLOW = pltpu.SideEffectType.DATAFLOW_SIDE_EFFECTING


def _in_hbm(a):
    return pltpu.with_memory_space_constraint(a, pltpu.HBM)


ALL_PEERS = (0, 1, 2)
NEIGHBOURS = (0, 1)
DIAGONAL = (2,)


def _chip_copies_start(name, srcs, lands, make_copy, after=None, peers=ALL_PEERS):
    n_t = len(srcs)
    in_place = lands is None
    n_after = 0 if after is None else 1

    def body(*refs):
        src = refs[:n_t]
        k = (n_t if in_place else 2 * n_t) + n_after
        send_sems, recv_sems = refs[k], refs[k + 1]
        land = refs[k + 2:k + 2 + n_t] if in_place else refs[k + 2 + n_t:k + 2 + 2 * n_t]
        token = refs[-1]
        x, y, c = _place()
        me = 2 * x + y
        others = _other_chips(x, y)
        for t in range(n_t):
            for px, py, pk in [others[p] for p in peers]:
                s, d = make_copy(src[t], land[t], me, pk, c)
                pltpu.make_async_remote_copy(
                    src_ref=s, dst_ref=d, send_sem=send_sems.at[t], recv_sem=recv_sems.at[t],
                    device_id=(px, py, c), device_id_type=MESH).start()
        token[...] = jnp.zeros_like(token)

    bufs = list(srcs) if in_place else list(srcs) + list(lands)
    outs = pl.pallas_call(
        body, name=name,
        in_specs=[HBM] * len(bufs) + [ANY] * n_after,
        out_specs=[SEM, SEM] + [HBM] * len(bufs) + [pl.BlockSpec(memory_space=pltpu.VMEM)],
        out_shape=[pltpu.SemaphoreType.DMA((n_t,)), pltpu.SemaphoreType.DMA((n_t,))]
        + [pltpu.HBM(b.shape, b.dtype) for b in bufs] + [jax.ShapeDtypeStruct((8, 128), F32)],
        input_output_aliases={i: 2 + i for i in range(len(bufs))},
        compiler_params=pltpu.CompilerParams(has_side_effects=DATAFLOW),
    )(*[_in_hbm(b) for b in bufs], *([] if after is None else [after]))
    return outs[0], outs[1], list(outs[2:2 + len(bufs)]), outs[-1]


def _chip_copies_wait(name, send_sems, recv_sems, bufs, sent, landed, after):
    n_b = len(bufs)

    def body(*refs):
        buf = refs[:n_b]
        send_ref, recv_ref = refs[n_b], refs[n_b + 1]
        x, y, c = _place()
        for t, (s, d) in enumerate(zip(sent(buf), landed(buf))):
            out = pltpu.make_async_remote_copy(src_ref=s, dst_ref=s, send_sem=send_ref.at[t], recv_sem=recv_ref.at[t],
                                               device_id=(x, y, c), device_id_type=MESH)
            out.wait_send()
            arrived = pltpu.make_async_remote_copy(src_ref=d, dst_ref=d, send_sem=send_ref.at[t],
                                                   recv_sem=recv_ref.at[t], device_id=(x, y, c), device_id_type=MESH)
            arrived.wait_recv()

    after = list(after) if isinstance(after, (list, tuple)) else [after]
    return pl.pallas_call(
        body, name=name,
        in_specs=[HBM] * n_b + [SEM, SEM] + [ANY] * len(after), out_specs=[HBM] * n_b,
        out_shape=[pltpu.HBM(b.shape, b.dtype) for b in bufs],
        input_output_aliases={i: i for i in range(n_b)},
        compiler_params=pltpu.CompilerParams(has_side_effects=DATAFLOW),
    )(*bufs, send_sems, recv_sems, *after)


def _gather_start(name, bufs, after=None, peers=ALL_PEERS):
    def make_copy(src, land, me, pk, c):
        hr = src.shape[1] // 2
        return src.at[me, pl.ds(c * hr, hr)], land.at[me, pl.ds(c * hr, hr)]

    return _chip_copies_start(name, bufs, None, make_copy, after, peers)


def _gather_wait(name, send_sems, recv_sems, bufs, after, peers=ALL_PEERS):
    def half_shards(buf):
        return [b.at[pl.ds(0, len(peers)), pl.ds(0, b.shape[1] // 2)] for b in buf]

    return _chip_copies_wait(name, send_sems, recv_sems, bufs, half_shards, half_shards, after)


def _gather_forward_start(name, bufs, peers=ALL_PEERS):
    n_t = len(bufs)

    def body(*refs):
        mine = refs[:n_t]
        send_sems, recv_sems = refs[n_t], refs[n_t + 1]
        buf = refs[n_t + 2:2 * n_t + 2]
        token = refs[-1]
        x, y, c = _place()
        others = _other_chips(x, y)
        for q in range(D2D_CHUNKS):
            for t in range(n_t):
                hr = mine[t].shape[1] // 2
                cr = hr // D2D_CHUNKS
                rows = pl.ds(c * hr + q * cr, cr)
                for _, _, pk in [others[p] for p in peers]:
                    pltpu.make_async_remote_copy(
                        src_ref=mine[t].at[pk, rows], dst_ref=buf[t].at[pk, rows], send_sem=send_sems.at[t],
                        recv_sem=recv_sems.at[t], device_id=(x, y, 1 - c), device_id_type=MESH).start()
        token[...] = jnp.zeros_like(token)

    outs = pl.pallas_call(
        body, name=name,
        in_specs=[HBM] * n_t,
        out_specs=[SEM, SEM] + [HBM] * n_t + [pl.BlockSpec(memory_space=pltpu.VMEM)],
        out_shape=[pltpu.SemaphoreType.DMA((n_t,)), pltpu.SemaphoreType.DMA((n_t,))]
        + [pltpu.HBM(b.shape, b.dtype) for b in bufs] + [jax.ShapeDtypeStruct((8, 128), F32)],
        input_output_aliases={i: 2 + i for i in range(n_t)},
        compiler_params=pltpu.CompilerParams(has_side_effects=DATAFLOW),
    )(*[_in_hbm(b) for b in bufs])
    return outs[0], outs[1], list(outs[2:2 + n_t]), outs[-1]


def _core_exchange_start(name, grads):
    n_t = len(grads)
    lands = [lax.empty(g.shape, g.dtype) for g in grads]

    def body(*refs):
        src = refs[:n_t]
        send_sems, recv_sems = refs[2 * n_t], refs[2 * n_t + 1]
        land = refs[2 * n_t + 2 + n_t:2 * n_t + 2 + 2 * n_t]
        token = refs[-1]
        x, y, c = _place()
        for q in range(D2D_CHUNKS):
            for t in range(n_t):
                cr = src[t].shape[1] // D2D_CHUNKS
                rows = pl.ds(q * cr, cr)
                pltpu.make_async_remote_copy(
                    src_ref=src[t].at[:, rows], dst_ref=land[t].at[:, rows], send_sem=send_sems.at[t],
                    recv_sem=recv_sems.at[t], device_id=(x, y, 1 - c), device_id_type=MESH).start()
        token[...] = jnp.zeros_like(token)

    bufs = list(grads) + lands
    outs = pl.pallas_call(
        body, name=name,
        in_specs=[HBM] * len(bufs),
        out_specs=[SEM, SEM] + [HBM] * len(bufs) + [pl.BlockSpec(memory_space=pltpu.VMEM)],
        out_shape=[pltpu.SemaphoreType.DMA((n_t,)), pltpu.SemaphoreType.DMA((n_t,))]
        + [pltpu.HBM(b.shape, b.dtype) for b in bufs] + [jax.ShapeDtypeStruct((8, 128), F32)],
        input_output_aliases={i: 2 + i for i in range(len(bufs))},
        compiler_params=pltpu.CompilerParams(has_side_effects=DATAFLOW),
    )(*[_in_hbm(b) for b in bufs])
    return outs[0], outs[1], list(outs[2:2 + len(bufs)]), outs[-1]


def _core_exchange_wait(name, send_sems, recv_sems, bufs, after):
    n_t = len(bufs) // 2

    def body(*refs):
        land = refs[n_t:2 * n_t]
        send_ref, recv_ref = refs[2 * n_t], refs[2 * n_t + 1]
        x, y, c = _place()
        for t in range(n_t):
            whole = pltpu.make_async_remote_copy(src_ref=land[t], dst_ref=land[t], send_sem=send_ref.at[t],
                                                 recv_sem=recv_ref.at[t], device_id=(x, y, c), device_id_type=MESH)
            whole.wait_send()
            whole.wait_recv()

    outs = pl.pallas_call(
        body, name=name,
        in_specs=[HBM] * (2 * n_t) + [SEM, SEM, ANY], out_specs=[HBM] * (2 * n_t),
        out_shape=[pltpu.HBM(b.shape, b.dtype) for b in bufs],
        input_output_aliases={i: i for i in range(2 * n_t)},
        compiler_params=pltpu.CompilerParams(has_side_effects=DATAFLOW),
    )(*bufs, send_sems, recv_sems, after)
    return list(outs[:n_t]), list(outs[n_t:])


def _chip_exchange_start(name, parts):
    lands = [lax.empty(p.shape, p.dtype) for p in parts]
    return _chip_copies_start(name, parts, lands, lambda src, land, me, pk, c: (src.at[pk], land.at[me]))


def _chip_exchange_wait(name, send_sems, recv_sems, bufs, after):
    n_t = len(bufs) // 2
    return _chip_copies_wait(name, send_sems, recv_sems, bufs,
                             lambda buf: [b.at[pl.ds(0, 3)] for b in buf[:n_t]],
                             lambda buf: [b.at[pl.ds(0, 3)] for b in buf[n_t:]], after)


def _sum_chips(name, parts, landed, place, l, stacked):
    chips, rows, cols = landed.shape
    tr = min(256, rows)
    per = rows // tr

    def body(p_ref, own_ref, *refs):
        land, o_ref = refs[:chips], refs[-1]
        tot = None
        for k in range(chips):
            term = jnp.where(p_ref[0] == k, own_ref[...], land[k][...]).astype(F32)
            tot = term if tot is None else tot + term
        o_ref[...] = tot

    def from_chip(k):
        return pl.BlockSpec((None, tr, cols), lambda i, p: (jnp.where(p[0] == k, (k + 1) % chips, k), i, 0))

    in_specs = [pl.BlockSpec((None, tr, cols), lambda i, p: (p[0], i, 0))] + [from_chip(k) for k in range(chips)]
    args = [parts] + [landed] * chips
    aliases = {}
    if stacked is not None:
        in_specs.append(ANY)
        args.append(stacked)
        aliases = {len(args): 0}
    return pl.pallas_call(
        body, name=name,
        grid_spec=pltpu.PrefetchScalarGridSpec(
            num_scalar_prefetch=1, grid=(per,), in_specs=in_specs,
            out_specs=pl.BlockSpec((None, tr, cols), lambda i, p: (l, p[1] * per + i, 0))),
        out_shape=jax.ShapeDtypeStruct((DEPTH, 2 * rows, cols), F32), input_output_aliases=aliases,
        compiler_params=_params(("parallel",)),
    )(place, *args)


def _core_share_start(name, bufs, l):
    n_t = len(bufs)

    def body(*refs):
        mine = refs[:n_t]
        send_sems, recv_sems = refs[n_t], refs[n_t + 1]
        buf = refs[n_t + 2:2 * n_t + 2]
        token = refs[-1]
        x, y, c = _place()
        for q in range(D2D_CHUNKS):
            for t in range(n_t):
                hr = mine[t].shape[1] // 2
                cr = hr // D2D_CHUNKS
                rows = pl.ds(c * hr + q * cr, cr)
                pltpu.make_async_remote_copy(
                    src_ref=mine[t].at[l, rows], dst_ref=buf[t].at[l, rows], send_sem=send_sems.at[t],
                    recv_sem=recv_sems.at[t], device_id=(x, y, 1 - c), device_id_type=MESH).start()
        token[...] = jnp.zeros_like(token)

    outs = pl.pallas_call(
        body, name=name,
        in_specs=[HBM] * n_t,
        out_specs=[SEM, SEM] + [HBM] * n_t + [pl.BlockSpec(memory_space=pltpu.VMEM)],
        out_shape=[pltpu.SemaphoreType.DMA((n_t,)), pltpu.SemaphoreType.DMA((n_t,))]
        + [pltpu.HBM(b.shape, b.dtype) for b in bufs] + [jax.ShapeDtypeStruct((8, 128), F32)],
        input_output_aliases={i: 2 + i for i in range(n_t)},
        compiler_params=pltpu.CompilerParams(has_side_effects=DATAFLOW),
    )(*[_in_hbm(b) for b in bufs])
    return outs[0], outs[1], list(outs[2:2 + n_t]), outs[-1]


def _core_share_wait(name, send_sems, recv_sems, bufs, l, after):
    def half_layer(buf):
        return [b.at[l, pl.ds(0, b.shape[1] // 2)] for b in buf]

    return _chip_copies_wait(name, send_sems, recv_sems, bufs, half_layer, half_layer, after)


def _all_reduce_small(vec, after=None):
    rows, lanes = vec.shape
    hr = rows // 2

    def body(v_ref, *refs):
        o_ref, sib_ref, chips_ref, send_sems, recv_sems = refs[-5:]
        x, y, c = _place()
        me = 2 * x + y
        sibling = (x, y, 1 - c)
        mine = pl.ds(pl.multiple_of(c * hr, 8), hr)
        theirs = pl.ds(pl.multiple_of((1 - c) * hr, 8), hr)
        swap = pltpu.make_async_remote_copy(
            src_ref=v_ref.at[theirs], dst_ref=sib_ref, send_sem=send_sems.at[0], recv_sem=recv_sems.at[0],
            device_id=sibling, device_id_type=MESH)
        swap.start()
        swap.wait_recv()
        chips_ref[me] = v_ref[mine] + sib_ref[...]
        copies = []
        for j, (px, py, pk) in enumerate(_other_chips(x, y)):
            cp = pltpu.make_async_remote_copy(
                src_ref=chips_ref.at[me], dst_ref=chips_ref.at[me], send_sem=send_sems.at[1 + j],
                recv_sem=recv_sems.at[1 + j], device_id=(px, py, c), device_id_type=MESH)
            cp.start()
            copies.append(cp)
        for j, (px, py, pk) in enumerate(_other_chips(x, y)):
            pltpu.make_async_remote_copy(
                src_ref=chips_ref.at[pk], dst_ref=chips_ref.at[pk], send_sem=send_sems.at[1 + j],
                recv_sem=recv_sems.at[1 + j], device_id=(px, py, c), device_id_type=MESH).wait_recv()
        tot = chips_ref[0]
        for k in range(1, N_CHIPS):
            tot = tot + chips_ref[k]
        o_ref[mine] = tot
        share = pltpu.make_async_remote_copy(
            src_ref=o_ref.at[mine], dst_ref=o_ref.at[mine], send_sem=send_sems.at[4], recv_sem=recv_sems.at[4],
            device_id=sibling, device_id_type=MESH)
        share.start()
        pltpu.make_async_remote_copy(
            src_ref=o_ref.at[theirs], dst_ref=o_ref.at[theirs], send_sem=send_sems.at[4], recv_sem=recv_sems.at[4],
            device_id=sibling, device_id_type=MESH).wait_recv()
        swap.wait_send()
        for cp in copies:
            cp.wait_send()
        share.wait_send()

    vm = pl.BlockSpec(memory_space=pltpu.VMEM)
    return pl.pallas_call(
        body, name="small_all_reduce", in_specs=[vm] + ([] if after is None else [ANY]), out_specs=vm,
        out_shape=jax.ShapeDtypeStruct((rows, lanes), F32),
        scratch_shapes=[pltpu.VMEM((hr, lanes), F32), pltpu.VMEM((N_CHIPS, hr, lanes), F32),
                        pltpu.SemaphoreType.DMA((5,)), pltpu.SemaphoreType.DMA((5,))],
        compiler_params=pltpu.CompilerParams(has_side_effects=True, vmem_limit_bytes=48 * MIB),
    )(vec, *([] if after is None else [after]))


def _adamw(name, w, g, m, v, place, l=0, half=None, done=None, after=None):
    layers, rows, cols = w.shape
    span = rows if half is None else rows // 2
    tr = span
    for cand in (256, 128, 64, 32, 16, 8):
        if span % cand == 0:
            tr = cand
            break
    per = span // tr
    c1 = 1.0 - ADAM_B1 ** ADAM_STEP
    c2 = 1.0 - ADAM_B2 ** ADAM_STEP

    def first_block(p):
        return 0 if half is None else (p[1] if half == "own" else 1 - p[1]) * per

    def body(p_ref, w_ref, g_ref, m_ref, v_ref, *refs):
        go_ref, d_ref, nm_ref, nv_ref = refs[-4:]
        gv = g_ref[...]
        nm = ADAM_B1 * m_ref[...] + (1.0 - ADAM_B1) * gv
        nv = ADAM_B2 * v_ref[...] + (1.0 - ADAM_B2) * (gv * gv)
        go_ref[...] = gv
        nm_ref[...] = nm
        nv_ref[...] = nv
        d_ref[...] = -ADAM_LR * ((nm / c1) / (jnp.sqrt(nv / c2) + ADAM_EPS) + ADAM_WD * w_ref[...])

    blk = pl.BlockSpec((None, tr, cols), lambda i, p: (l, first_block(p) + i, 0))
    out = jax.ShapeDtypeStruct((layers, rows, cols), F32)
    extra = ([] if done is None else list(done)) + ([] if after is None else [after])
    aliases = {} if done is None else {5 + i: i for i in range(4)}
    return pl.pallas_call(
        body, name=name,
        grid_spec=pltpu.PrefetchScalarGridSpec(
            num_scalar_prefetch=1, grid=(per,), in_specs=[blk] * 4 + [ANY] * len(extra), out_specs=[blk] * 4),
        out_shape=[out] * 4, input_output_aliases=aliases,
        compiler_params=_params(("parallel",)),
    )(place, w, g, m, v, *extra)


LANES = 128
SUBLANES = 8
SMALL_SHAPES = {
    "norm_g": (DEPTH, D_MODEL), "sgu_ln_g": (DEPTH, D_A), "sgu_ln_b": (DEPTH, D_A),
    "sgu_w": (DEPTH, A_GROUPS, CHUNK, CHUNK), "sgu_b": (DEPTH, A_GROUPS, CHUNK), "mem_norm_g": (DEPTH, D_MODEL),
    "q_norm_g": (DEPTH, HEAD_DIM), "k_norm_g": (DEPTH, HEAD_DIM)}


def _small_layout():
    at, off = {}, 0
    for k in SMALL_NAMES:
        n = math.prod(SMALL_SHAPES[k]) // LANES
        at[k] = (off, n)
        off += -(-n // SUBLANES) * SUBLANES
    return at, off, -(-(off + SUBLANES) // (2 * SUBLANES)) * 2 * SUBLANES


def _pack_small(parts, loss=None):
    at, loss_row, rows = _small_layout()
    pieces = []
    for k in SMALL_NAMES:
        n = at[k][1]
        pieces.append(jnp.pad(parts[k].reshape(n, LANES), ((0, -(-n // SUBLANES) * SUBLANES - n), (0, 0))))
    tile = jnp.zeros((SUBLANES, LANES), F32) if loss is None else jnp.broadcast_to(loss.reshape(1, 1), (SUBLANES, LANES))
    pieces += [tile, jnp.zeros((rows - loss_row - SUBLANES, LANES), F32)]
    return jnp.concatenate(pieces)


def _adamw_small(w, g, m, v):
    at, _, rows = _small_layout()
    c1 = 1.0 - ADAM_B1 ** ADAM_STEP
    c2 = 1.0 - ADAM_B2 ** ADAM_STEP
    n_names = len(SMALL_NAMES)

    def body(w_ref, g_ref, m_ref, v_ref, *refs):
        outs, (d_ref, nm_ref, nv_ref) = refs[:4 * n_names], refs[4 * n_names:]
        gv = g_ref[...]
        nm = ADAM_B1 * m_ref[...] + (1.0 - ADAM_B1) * gv
        nv = ADAM_B2 * v_ref[...] + (1.0 - ADAM_B2) * (gv * gv)
        nm_ref[...] = nm
        nv_ref[...] = nv
        d_ref[...] = -ADAM_LR * ((nm / c1) / (jnp.sqrt(nv / c2) + ADAM_EPS) + ADAM_WD * w_ref[...])
        for kind, src in enumerate((g_ref, d_ref, nm_ref, nv_ref)):
            for i, k in enumerate(SMALL_NAMES):
                o_ref = outs[kind * n_names + i]
                first, n = at[k]
                shape = SMALL_SHAPES[k]
                if shape[-1] == LANES:
                    o_ref[...] = src[pl.ds(first, n), :].reshape(shape)
                else:
                    per = shape[-1] // LANES
                    for r in range(n):
                        o_ref[pl.ds(r // per, 1), pl.ds((r % per) * LANES, LANES)] = src[pl.ds(first + r, 1), :]

    out_shape = [jax.ShapeDtypeStruct(SMALL_SHAPES[k], F32) for _ in range(4) for k in SMALL_NAMES]
    outs = pl.pallas_call(
        body, name="adamw_small", out_shape=out_shape,
        scratch_shapes=[pltpu.VMEM((rows, LANES), F32)] * 3, compiler_params=_params(None),
    )(w, g, m, v)
    return [dict(zip(SMALL_NAMES, outs[kind * n_names:(kind + 1) * n_names])) for kind in range(4)]


WEIGHT_ORDER = ("norm_g", "w_in", "sgu_ln_g", "sgu_ln_b", "sgu_w", "sgu_b", "mem_norm_g", "w_mem_kv", "q_norm_g",
                "k_norm_g", "w_out")


def kernel(x, mem, norm_g, w_in, sgu_ln_g, sgu_ln_b, sgu_w, sgu_b, mem_norm_g, w_mem_kv, q_norm_g, k_norm_g, w_out, loss_target, m_norm_g, m_w_in, m_sgu_ln_g, m_sgu_ln_b, m_sgu_w, m_sgu_b, m_mem_norm_g, m_w_mem_kv, m_q_norm_g, m_k_norm_g, m_w_out, v_norm_g, v_w_in, v_sgu_ln_g, v_sgu_ln_b, v_sgu_w, v_sgu_b, v_mem_norm_g, v_w_mem_kv, v_q_norm_g, v_k_norm_g, v_w_out):
    weights = dict(norm_g=norm_g, w_in=w_in, sgu_ln_g=sgu_ln_g, sgu_ln_b=sgu_ln_b, sgu_w=sgu_w, sgu_b=sgu_b,
                   mem_norm_g=mem_norm_g, w_mem_kv=w_mem_kv, q_norm_g=q_norm_g, k_norm_g=k_norm_g, w_out=w_out)
    mom_m = dict(norm_g=m_norm_g, w_in=m_w_in, sgu_ln_g=m_sgu_ln_g, sgu_ln_b=m_sgu_ln_b, sgu_w=m_sgu_w, sgu_b=m_sgu_b,
                 mem_norm_g=m_mem_norm_g, w_mem_kv=m_w_mem_kv, q_norm_g=m_q_norm_g, k_norm_g=m_k_norm_g, w_out=m_w_out)
    mom_v = dict(norm_g=v_norm_g, w_in=v_w_in, sgu_ln_g=v_sgu_ln_g, sgu_ln_b=v_sgu_ln_b, sgu_w=v_sgu_w, sgu_b=v_sgu_b,
                 mem_norm_g=v_mem_norm_g, w_mem_kv=v_w_mem_kv, q_norm_g=v_q_norm_g, k_norm_g=v_k_norm_g, w_out=v_w_out)
    big = ("w_in", "w_mem_kv", "w_out")
    sm = {k: weights[k] for k in SMALL_NAMES}

    place = _place_index()
    xs, mems, target = x[0], mem[0], loss_target[0]

    slots = [[_cast_into_slot(f"cast_{k}_{l}", weights[k], l, place) for k in big] for l in range(DEPTH)]
    saved = [None] * DEPTH

    chips, cores = {}, {}
    me = place[0]
    arrival = jnp.stack([me, me ^ 2, me ^ 1, 3 - me]).astype(jnp.int32)
    shard_order = jnp.arange(N_CHIPS, dtype=jnp.int32)

    def start_gather(l, after=None):
        chips[l, "in"] = _gather_start(f"gather_start_{l}_in", slots[l][:1], after)
        chips[l, "rest"] = _gather_start(f"gather_start_{l}_rest", slots[l][1:], chips[l, "in"][3])
        return chips[l, "rest"][3]

    def hand_to_sibling(l, group, after):
        send_sems, recv_sems, bufs, _ = chips[l, group]
        bufs = _gather_wait(f"gather_wait_{l}_{group}", send_sems, recv_sems, bufs, after)
        cores[l, group] = _gather_forward_start(f"gather_forward_{l}_{group}", bufs)
        return cores[l, group][3]

    def whole(l, group, after):
        send_sems, recv_sems, bufs, _ = cores[l, group]
        return _gather_wait(f"gather_whole_{l}_{group}", send_sems, recv_sems, bufs, after)

    later_slots = [s for layer in slots[1:] for s in layer]

    class Gathered:
        def __init__(self, l):
            self.l = l
            self.buf = None

        def landed_from(self, tag, peers, after, behind, then=None):
            send_sems, recv_sems, _, _ = chips[0, "in_" + tag]
            buf = _gather_wait(f"gather_wait_0_in_{tag}", send_sems, recv_sems, self.buf, after, peers)
            if then is not None:
                buf, more = then(buf)
                behind = behind + more
            send_sems, recv_sems, buf, token = _gather_forward_start(f"gather_forward_0_in_{tag}", buf, peers)
            self.buf = _gather_wait(f"gather_whole_0_in_{tag}", send_sems, recv_sems, buf, [token] + behind, peers)

        def w_in(self, stage, h, proj):
            if self.l > 0:
                return (whole(self.l, "in", h)[0], shard_order, 0, N_CHIPS) if stage == 0 else None
            if stage == 0:
                self.buf = chips[0, "in_n"][2]
                return self.buf[0], arrival, 0, 1
            if stage == 1:
                def start_others(buf):
                    chips[0, "in_d"] = _gather_start("gather_start_0_in_d", buf, None, DIAGONAL)
                    chips[0, "rest"] = _gather_start("gather_start_0_rest", slots[0][1:], chips[0, "in_d"][3])
                    return chips[0, "in_d"][2], [chips[0, "rest"][3]]

                self.landed_from("n", NEIGHBOURS, proj, later_slots + [chips[0, "in_n"][3]], start_others)
                return self.buf[0], arrival, 1, 2
            if stage == 2:
                self.landed_from("d", DIAGONAL, [proj, chips[0, "rest"][3]], [])
                return self.buf[0], arrival, 3, 1
            return None

        def rest_start(self, proj):
            token = proj if self.l == 0 else hand_to_sibling(self.l, "rest", proj)
            return start_gather(self.l + 1, token) if self.l + 1 < DEPTH else token

        def rest_finish(self, o_b):
            if self.l == 0:
                o_b = hand_to_sibling(self.l, "rest", o_b)
            w_kv_all, w_out_all = whole(self.l, "rest", o_b)
            return w_kv_all, w_out_all, None

        def before_out(self, y):
            return hand_to_sibling(self.l + 1, "in", y) if self.l + 1 < DEPTH else None

    chips[0, "in_n"] = _gather_start("gather_start_0_in_n", slots[0][:1], None, NEIGHBOURS)
    cur = xs
    for l in range(DEPTH):
        cur, saved[l] = _layer_fwd(l, cur, mems, sm, Gathered(l))
    dxo, dxo_b, loss_part = _loss_and_grad("loss", cur, target, min(256, xs.shape[0]))

    small_g = [None] * DEPTH
    flight = {}

    class Exchange:
        def __init__(self):
            self.cores = {}

        def start(self, l, group, gives):
            *self.cores[l, group], token = _core_exchange_start(f"grad_core_start_{l}_{group}", gives)
            return token

        def landed(self, l, group, after):
            send_sems, recv_sems, bufs = self.cores[l, group]
            return _core_exchange_wait(f"grad_core_wait_{l}_{group}", send_sems, recv_sems, bufs, after)[1]

        def send(self, l, group, parts):
            *flight[l, group], token = _chip_exchange_start(f"grad_chip_start_{l}_{group}", parts)
            return token

    exchange = Exchange()
    for l in reversed(range(DEPTH)):
        dxo, dxo_b, small_g[l] = _layer_bwd(l, dxo, dxo_b, mems, sm, saved[l], place, exchange)
    grad_x = dxo

    groups = (("out", ("w_out",)), ("in", ("w_in", "w_mem_kv")))
    halves, stepped = dict.fromkeys(big), dict.fromkeys(big)
    small_g = {k: jnp.stack([small_g[l][k] for l in range(DEPTH)]) for k in SMALL_NAMES}
    after = grad_x
    sharing = {}

    def reduce_group(l, group, names):
        nonlocal after
        send_sems, recv_sems, bufs = flight[l, group]
        bufs = _chip_exchange_wait(f"grad_chip_wait_{l}_{group}", send_sems, recv_sems, bufs, after)
        for t, k in enumerate(names):
            halves[k] = _sum_chips(f"grad_chip_sum_{l}_{k}", bufs[t], bufs[len(names) + t], place, l, halves[k])
        *sharing[l, group], after = _core_share_start(f"grad_core_share_{l}_{group}", [halves[k] for k in names], l)

    def step(l, k, buf, half):
        nonlocal after
        tag = "" if half is None else "_" + half
        stepped[k] = _adamw(f"adamw_{k}_{l}{tag}", weights[k], buf, mom_m[k], mom_v[k], place, l, half, stepped[k],
                            after)
        after = stepped[k][1]

    def step_group(l, group, names, overlap):
        nonlocal after
        send_sems, recv_sems, bufs = sharing[l, group]
        if overlap:
            for k, buf in zip(names, bufs):
                step(l, k, buf, "own")
        bufs = _core_share_wait(f"grad_core_shared_{l}_{group}", send_sems, recv_sems, bufs, l, after)
        for k, buf in zip(names, bufs):
            halves[k] = buf
            step(l, k, buf, "other" if overlap else None)

    for l in reversed(range(DEPTH)):
        last = l == 0
        (g_out, n_out), (g_in, n_in) = groups
        reduce_group(l, g_out, n_out)
        if last:
            step_group(l, g_out, n_out, False)
            small_sum = _all_reduce_small(_pack_small(small_g, loss_part), after)
            small_step = _adamw_small(_pack_small(sm), small_sum, _pack_small({k: mom_m[k] for k in SMALL_NAMES}),
                                      _pack_small({k: mom_v[k] for k in SMALL_NAMES}))
            after = small_step[1]["sgu_w"]
        reduce_group(l, g_in, n_in)
        if not last:
            step_group(l, g_out, n_out, False)
        step_group(l, g_in, n_in, last)

    grads, delta, new_m, new_v = ({k: stepped[k][i] for k in big} for i in range(4))
    for out, small in zip((grads, delta, new_m, new_v), small_step):
        out.update(small)
    loss = small_sum[_small_layout()[1], 0]
    return (loss, grad_x[None], *[grads[k] for k in WEIGHT_ORDER], *[delta[k] for k in WEIGHT_ORDER],
            *[new_m[k] for k in WEIGHT_ORDER], *[new_v[k] for k in WEIGHT_ORDER])
```

```python
import functools
import math

import jax
import jax.numpy as jnp
from jax import lax
from jax.experimental import pallas as pl
from jax.experimental.pallas import tpu as pltpu

F32 = jnp.float32
BF16 = jnp.bfloat16
MESH = pl.DeviceIdType.MESH

D_MODEL = 2048
DEPTH = 2
CHUNK = 128
D_A = 1024
A_GROUPS = 8
D_B = 512
D_C = 512
HEADS = 4
HEAD_DIM = 128
IN_WIDTH = 6144
N_CHIPS = 4
EPS = 1e-6
ATT_SCALE = 1.0 / math.sqrt(HEAD_DIM)

OFF_U, OFF_V, OFF_ZA = 0, 1024, 2048
OFF_QB, OFF_KB, OFF_VB, OFF_ZB = 3072, 3584, 4096, 4608
OFF_QC, OFF_ZC = 5120, 5632
OFF_YB, OFF_YC = 1024, 1536

ADAM_LR = 0.001
ADAM_B1 = 0.9
ADAM_B2 = 0.999
ADAM_EPS = 1e-08
ADAM_WD = 0.01
ADAM_STEP = 10

MIB = 1024 * 1024
ANY = pl.BlockSpec(memory_space=pl.ANY)


def _params(semantics=None, vmem_mb=48):
    return pltpu.CompilerParams(dimension_semantics=semantics, vmem_limit_bytes=vmem_mb * MIB)


def _gelu(x):
    return 0.5 * x * (1.0 + lax.erf(x * (1.0 / math.sqrt(2.0))))


def _gelu_grad(x):
    cdf = 0.5 * (1.0 + lax.erf(x * (1.0 / math.sqrt(2.0))))
    pdf = jnp.exp(-0.5 * x * x) * (1.0 / math.sqrt(2.0 * math.pi))
    return cdf + x * pdf


def _sigmoid(x):
    return 1.0 / (1.0 + jnp.exp(-x))


def _silu_and_grad(z):
    s = _sigmoid(z)
    return z * s, s * (1.0 + z * (1.0 - s))


def _split_bf16(x):
    hi = x.astype(BF16)
    lo = (x - hi.astype(F32)).astype(BF16)
    return hi, lo


def _dot(a, b, dims):
    return lax.dot_general(a, b, (dims, ((), ())), preferred_element_type=F32)


NN = ((1,), (0,))
NT = ((1,), (1,))
TN = ((0,), (0,))


def _matmul(name, a, b, *, grid, a_spec, b_spec, o_spec, out_shape, dims, res=None, res_spec=None, after=None,
            place=None, into=None, vmem_mb=48):
    nk = grid[2]
    n_in = 2 + (res is not None) + (after is not None) + (into is not None)

    def body(*refs):
        if place is not None:
            refs = refs[1:]
        a_ref, b_ref = refs[0], refs[1]
        r_ref = refs[2] if res is not None else None
        o_ref = refs[n_in]
        if len(b_ref.shape) == 3 and dims == NN:
            part = _dot(a_ref[...], b_ref[...].reshape(-1, b_ref.shape[-1]), dims)
        elif len(b_ref.shape) == 3:
            width = b_ref.shape[-1]
            part = None
            for s in range(b_ref.shape[0]):
                term = _dot(a_ref[:, s * width:(s + 1) * width], b_ref[s], dims)
                part = term if part is None else part + term
        else:
            part = _dot(a_ref[...], b_ref[...], dims)
        if nk == 1:
            if r_ref is not None:
                part = part + r_ref[...]
            o_ref[...] = part.astype(o_ref.dtype)
            return
        acc_ref = refs[n_in + 1]
        k = pl.program_id(2)

        @pl.when(k == 0)
        def _():
            acc_ref[...] = part

        @pl.when(k > 0)
        def _():
            acc_ref[...] += part

        @pl.when(k == nk - 1)
        def _():
            tot = acc_ref[...]
            if r_ref is not None:
                tot = tot + r_ref[...]
            o_ref[...] = tot.astype(o_ref.dtype)

    in_specs = [a_spec, b_spec]
    args = [a, b]
    if res is not None:
        in_specs.append(res_spec)
        args.append(res)
    if after is not None:
        in_specs.append(ANY)
        args.append(after)
    aliases = {}
    if into is not None:
        in_specs.append(ANY)
        args.append(into)
        aliases = {len(args) - 1 + (place is not None): 0}
    acc_shape = tuple(d for d in o_spec.block_shape if d is not None)
    scratch = [pltpu.VMEM(acc_shape, F32)] if nk > 1 else []
    params = _params(("parallel", "parallel", "arbitrary"), vmem_mb)
    if place is not None:
        return pl.pallas_call(
            body, name=name, out_shape=out_shape, compiler_params=params, input_output_aliases=aliases,
            grid_spec=pltpu.PrefetchScalarGridSpec(num_scalar_prefetch=1, grid=grid, in_specs=in_specs,
                                                   out_specs=o_spec, scratch_shapes=scratch),
        )(place, *args)
    return pl.pallas_call(
        body, name=name, grid=grid, in_specs=in_specs, out_specs=o_spec, out_shape=out_shape,
        scratch_shapes=scratch, compiler_params=params, input_output_aliases=aliases,
    )(*args)


def _rms_fwd(name, x, g, tr, after=None, transposed=False):
    rows, d = x.shape

    def body(x_ref, g_ref, *refs):
        outs = refs[1:] if after is not None else refs
        xv = x_ref[...]
        r = lax.rsqrt(jnp.mean(xv * xv, axis=-1, keepdims=True) + EPS)
        h = xv * r * g_ref[...]
        outs[0][...] = h.astype(BF16)
        if transposed:
            outs[1][...] = h.T.astype(BF16)

    out_specs = [pl.BlockSpec((tr, d), lambda i: (i, 0))]
    out_shape = [jax.ShapeDtypeStruct((rows, d), BF16)]
    if transposed:
        out_specs.append(pl.BlockSpec((d, tr), lambda i: (0, i)))
        out_shape.append(jax.ShapeDtypeStruct((d, rows), BF16))
    outs = pl.pallas_call(
        body, name=name, grid=(rows // tr,),
        in_specs=[pl.BlockSpec((tr, d), lambda i: (i, 0)), pl.BlockSpec((1, d), lambda i: (0, 0))]
        + ([] if after is None else [ANY]),
        out_specs=out_specs, out_shape=out_shape,
        compiler_params=_params(("parallel",)),
    )(x, g, *([] if after is None else [after]))
    return outs if transposed else outs[0]


def _rms_bwd(name, x, dh, dres, g, tr, after=None):
    rows, d = x.shape

    def body(x_ref, dh_ref, dres_ref, g_ref, *refs):
        dx_ref, dxb_ref, dg_ref = refs[-3:]
        xv = x_ref[...]
        r = lax.rsqrt(jnp.mean(xv * xv, axis=-1, keepdims=True) + EPS)
        xhat = xv * r
        dhv = dh_ref[...]
        dxh = dhv * g_ref[...]
        dx = r * (dxh - xhat * jnp.mean(dxh * xhat, axis=-1, keepdims=True)) + dres_ref[...]
        dx_ref[...] = dx
        dxb_ref[...] = dx.astype(BF16)
        part = jnp.sum(dhv * xhat, axis=0, keepdims=True)

        @pl.when(pl.program_id(0) == 0)
        def _():
            dg_ref[...] = part

        @pl.when(pl.program_id(0) > 0)
        def _():
            dg_ref[...] += part

    blk = pl.BlockSpec((tr, d), lambda i: (i, 0))
    vec = pl.BlockSpec((1, d), lambda i: (0, 0))
    return pl.pallas_call(
        body, name=name, grid=(rows // tr,), in_specs=[blk, blk, blk, vec] + ([] if after is None else [ANY]),
        out_specs=[blk, blk, vec],
        out_shape=[jax.ShapeDtypeStruct((rows, d), F32), jax.ShapeDtypeStruct((rows, d), BF16),
                   jax.ShapeDtypeStruct((1, d), F32)],
        compiler_params=_params(("arbitrary",)),
    )(x, dh, dres, g, *([] if after is None else [after]))


def _rms_gain_grad(name, x, dh):
    rows, d = x.shape

    def body(x_ref, dh_ref, dg_ref):
        xv = x_ref[...]
        r = lax.rsqrt(jnp.mean(xv * xv, axis=-1, keepdims=True) + EPS)
        dg_ref[...] = jnp.sum(dh_ref[...] * xv * r, axis=0, keepdims=True)

    return pl.pallas_call(
        body, name=name, out_shape=jax.ShapeDtypeStruct((1, d), F32), compiler_params=_params(None),
    )(x, dh)


def _loss_and_grad(name, y, target, tr):
    rows, d = y.shape
    n = rows // tr

    def body(y_ref, t_ref, dx_ref, dxb_ref, loss_ref, acc_ref):
        e = y_ref[...] - t_ref[...]
        dx = e * (1.0 / d)
        dx_ref[...] = dx
        dxb_ref[...] = dx.astype(BF16)
        part = jnp.sum(e * e, axis=0, keepdims=True)
        i = pl.program_id(0)

        @pl.when(i == 0)
        def _():
            acc_ref[...] = part

        @pl.when(i > 0)
        def _():
            acc_ref[...] += part

        @pl.when(i == n - 1)
        def _():
            loss_ref[...] = jnp.sum(acc_ref[...], axis=-1, keepdims=True) * (0.5 / d)

    blk = pl.BlockSpec((tr, d), lambda i: (i, 0))
    return pl.pallas_call(
        body, name=name, grid=(n,), in_specs=[blk, blk],
        out_specs=[blk, blk, pl.BlockSpec((1, 1), lambda i: (0, 0))],
        out_shape=[jax.ShapeDtypeStruct((rows, d), F32), jax.ShapeDtypeStruct((rows, d), BF16),
                   jax.ShapeDtypeStruct((1, 1), F32)],
        scratch_shapes=[pltpu.VMEM((1, d), F32)],
        compiler_params=_params(("arbitrary",)),
    )(y, target)


SB_T = 256
SB_HEADS = 4


LOG2E = 1.4426950408889634


def _sb_scores(q, kblk):
    z2 = _dot(q, kblk, NT) * (ATT_SCALE * LOG2E)
    e = jnp.exp2(-jnp.abs(z2))
    l1 = jnp.minimum(-z2, 0.0) - jnp.log2(1.0 + e)
    lb = l1 + z2
    return z2, e, lb, l1


def _sb_fwd(name, proj, after=None):
    s_len = proj.shape[0]
    t = SB_T
    nq = s_len // t

    def body(q_ref, k_ref, v_ref, *refs):
        o_ref = refs[-1]
        i = pl.program_id(1)
        row = lax.broadcasted_iota(jnp.int32, (t, t), 0)
        col = lax.broadcasted_iota(jnp.int32, (t, t), 1)
        causal = col < row
        after_mat = (row > col).astype(BF16)
        heads = [slice(hh * HEAD_DIM, (hh + 1) * HEAD_DIM) for hh in range(SB_HEADS)]
        q = [q_ref[:, sl].astype(BF16) for sl in heads]

        def tile(kb, state, masked):
            start = pl.multiple_of(kb * t, t)
            out = []
            for hh, sl in enumerate(heads):
                carry, acc = state[hh]
                kblk = k_ref[pl.ds(start, t), sl].astype(BF16)
                vblk = v_ref[pl.ds(start, t), sl].astype(BF16)
                _, _, lb, l1 = _sb_scores(q[hh], kblk)
                if masked:
                    l1 = jnp.where(causal, l1, 0.0)
                hi, lo = _split_bf16(l1)
                after = _dot(hi, after_mat, NN) + _dot(lo, after_mat, NN) + carry
                a = jnp.exp2(lb + after)
                if masked:
                    a = jnp.where(causal, a, 0.0)
                acc = acc + _dot(a.astype(BF16), vblk, NN)
                carry = carry + jnp.sum(l1, axis=-1, keepdims=True)
                out.append((carry, acc))
            return tuple(out)

        zero = (jnp.zeros((t, 1), F32), jnp.zeros((t, HEAD_DIM), F32))
        state = tile(i, (zero,) * SB_HEADS, True)
        state = lax.fori_loop(0, i, lambda n, st: tile(i - 1 - n, st, False), state)
        for hh, sl in enumerate(heads):
            o_ref[:, sl] = state[hh][1]

    cb = SB_HEADS * HEAD_DIM
    return pl.pallas_call(
        body, name=name, grid=(HEADS // SB_HEADS, nq),
        in_specs=[pl.BlockSpec((t, cb), lambda h, i: (i, OFF_QB // cb + h)),
                  pl.BlockSpec((s_len, cb), lambda h, i: (0, OFF_KB // cb + h)),
                  pl.BlockSpec((s_len, cb), lambda h, i: (0, OFF_VB // cb + h))] + ([] if after is None else [ANY]),
        out_specs=pl.BlockSpec((t, cb), lambda h, i: (i, h)),
        out_shape=jax.ShapeDtypeStruct((s_len, D_B), F32),
        compiler_params=_params(("parallel", "arbitrary")),
    )(proj, proj, proj, *([] if after is None else [after]))


def _sb_bwd(name, proj, dy, after=None):
    s_len = proj.shape[0]
    t = SB_T
    nq = s_len // t

    def body(q_ref, k_ref, v_ref, z_ref, dy_ref, *refs):
        dq_ref, dk_ref, dv_ref, a_ref, s_ref = refs[-5:]
        i = pl.program_id(1)

        @pl.when(i == 0)
        def _():
            dk_ref[...] = jnp.zeros_like(dk_ref)
            dv_ref[...] = jnp.zeros_like(dv_ref)

        heads = [slice(hh * HEAD_DIM, (hh + 1) * HEAD_DIM) for hh in range(SB_HEADS)]
        q = [q_ref[:, sl].astype(BF16) for sl in heads]
        silu_z, _ = _silu_and_grad(z_ref[...])
        do_all = dy_ref[...] * silu_z
        do_b = [do_all[:, sl].astype(BF16) for sl in heads]
        row = lax.broadcasted_iota(jnp.int32, (t, t), 0)
        col = lax.broadcasted_iota(jnp.int32, (t, t), 1)
        causal = col < row
        after_mat = (row > col).astype(BF16)
        before_mat = (row < col).astype(BF16)

        def weights(kb, carries, masked):
            start = pl.multiple_of(kb * t, t)
            out = []
            for hh, sl in enumerate(heads):
                kblk = k_ref[pl.ds(start, t), sl].astype(BF16)
                z, _, lb, l1 = _sb_scores(q[hh], kblk)
                if masked:
                    l1 = jnp.where(causal, l1, 0.0)
                hi, lo = _split_bf16(l1)
                after = _dot(hi, after_mat, NN) + _dot(lo, after_mat, NN) + carries[hh]
                a = jnp.exp2(lb + after)
                if masked:
                    a = jnp.where(causal, a, 0.0)
                a_ref[hh, kb] = a
                s_ref[hh, kb] = z
                out.append(carries[hh] + jnp.sum(l1, axis=-1, keepdims=True))
            return tuple(out)

        carries = weights(i, (jnp.zeros((t, 1), F32),) * SB_HEADS, True)
        lax.fori_loop(0, i, lambda n, c: weights(i - 1 - n, c, False), carries)

        def grads(kb, state, masked):
            start = pl.multiple_of(kb * t, t)
            out = []
            for hh, sl in enumerate(heads):
                carry, dq = state[hh]
                kblk = k_ref[pl.ds(start, t), sl].astype(BF16)
                vblk = v_ref[pl.ds(start, t), sl].astype(BF16)
                a = a_ref[hh, kb]
                z = s_ref[hh, kb]
                g = _dot(do_b[hh], vblk, NT) * a
                ghi, glo = _split_bf16(g)
                prefix = _dot(ghi, before_mat, NN) + _dot(glo, before_mat, NN) + carry
                e = jnp.exp2(-jnp.abs(z))
                inv = 1.0 / (1.0 + e)
                pos = z >= 0.0
                beta = jnp.where(pos, inv, e * inv)
                one_m_beta = jnp.where(pos, e * inv, inv)
                dz = (g * one_m_beta - prefix * beta) * ATT_SCALE
                if masked:
                    dz = jnp.where(causal, dz, 0.0)
                dz_b = dz.astype(BF16)
                dq = dq + _dot(dz_b, kblk, NN)
                dk_ref[pl.ds(start, t), sl] += _dot(dz_b, q[hh], TN)
                dv_ref[pl.ds(start, t), sl] += _dot(a.astype(BF16), do_b[hh], TN)
                out.append((carry + jnp.sum(g, axis=-1, keepdims=True), dq))
            return tuple(out)

        zero = (jnp.zeros((t, 1), F32), jnp.zeros((t, HEAD_DIM), F32))
        state = lax.fori_loop(0, i, lambda kb, st: grads(kb, st, False), (zero,) * SB_HEADS)
        state = grads(i, state, True)
        for hh, sl in enumerate(heads):
            dq_ref[:, sl] = state[hh][1]

    cb = SB_HEADS * HEAD_DIM
    qblk = lambda off: pl.BlockSpec((t, cb), lambda h, i: (i, off // cb + h))
    full = lambda off: pl.BlockSpec((s_len, cb), lambda h, i: (0, off // cb + h))
    out = jax.ShapeDtypeStruct((s_len, D_B), F32)
    return pl.pallas_call(
        body, name=name, grid=(HEADS // SB_HEADS, nq),
        in_specs=[qblk(OFF_QB), full(OFF_KB), full(OFF_VB), qblk(OFF_ZB), qblk(OFF_YB)]
        + ([] if after is None else [ANY]),
        out_specs=[qblk(0), full(0), full(0)],
        out_shape=[out, out, out],
        scratch_shapes=[pltpu.VMEM((SB_HEADS, nq, t, t), F32), pltpu.VMEM((SB_HEADS, nq, t, t), F32)],
        compiler_params=_params(("parallel", "arbitrary")),
    )(proj, proj, proj, proj, dy, *([] if after is None else [after]))


MEM_TQ = 512


def _qk_norm(x, g):
    r = lax.rsqrt(jnp.mean(x * x, axis=-1, keepdims=True) + EPS)
    xhat = x * r
    return xhat * g, xhat, r


def _qk_norm_bwd(dn, g, xhat, r):
    dxh = dn * g
    return r * (dxh - xhat * jnp.mean(dxh * xhat, axis=-1, keepdims=True))


def _mem_probs(q, mk, qg, kg):
    qn, qhat, rq = _qk_norm(q, qg)
    kn, khat, rk = _qk_norm(mk, kg)
    qn_b, kn_b = qn.astype(BF16), kn.astype(BF16)
    s = _dot(qn_b, kn_b, NT) * ATT_SCALE
    p = jnp.exp(s - jnp.max(s, axis=-1, keepdims=True))
    p = p / jnp.sum(p, axis=-1, keepdims=True)
    return p, qn_b, kn_b, qhat, rq, khat, rk


def _mem_fwd(name, proj, mem_kv, qg, kg):
    s_len = proj.shape[0]
    m_len = mem_kv.shape[0]
    tq = min(MEM_TQ, s_len)

    def body(q_ref, mk_ref, mv_ref, qg_ref, kg_ref, o_ref):
        p = _mem_probs(q_ref[...], mk_ref[...], qg_ref[...], kg_ref[...])[0]
        o_ref[...] = _dot(p.astype(BF16), mv_ref[...].astype(BF16), NN)

    cb = HEAD_DIM
    vec = pl.BlockSpec((1, cb), lambda h, i: (0, 0))
    return pl.pallas_call(
        body, name=name, grid=(HEADS, s_len // tq),
        in_specs=[pl.BlockSpec((tq, cb), lambda h, i: (i, OFF_QC // cb + h)),
                  pl.BlockSpec((m_len, cb), lambda h, i: (0, h)),
                  pl.BlockSpec((m_len, cb), lambda h, i: (0, HEADS + h)), vec, vec],
        out_specs=pl.BlockSpec((tq, cb), lambda h, i: (i, h)),
        out_shape=jax.ShapeDtypeStruct((s_len, D_C), F32),
        compiler_params=_params(("parallel", "parallel")),
    )(proj, mem_kv, mem_kv, qg, kg)


def _mem_bwd(name, proj, mem_kv, qg, kg, dy):
    s_len = proj.shape[0]
    m_len = mem_kv.shape[0]
    tq = min(MEM_TQ, s_len)

    def body(q_ref, mk_ref, mv_ref, qg_ref, kg_ref, z_ref, dy_ref, dq_ref, dmk_ref, dmv_ref, dqg_ref, dkg_ref):
        h, i = pl.program_id(0), pl.program_id(1)

        @pl.when(i == 0)
        def _():
            dmk_ref[...] = jnp.zeros_like(dmk_ref)
            dmv_ref[...] = jnp.zeros_like(dmv_ref)

        @pl.when((i == 0) & (h == 0))
        def _():
            dqg_ref[...] = jnp.zeros_like(dqg_ref)
            dkg_ref[...] = jnp.zeros_like(dkg_ref)

        qg, kg = qg_ref[...], kg_ref[...]
        p, qn_b, kn_b, qhat, rq, khat, rk = _mem_probs(q_ref[...], mk_ref[...], qg, kg)
        silu_z, _ = _silu_and_grad(z_ref[...])
        do_b = (dy_ref[...] * silu_z).astype(BF16)
        dmv_ref[...] += _dot(p.astype(BF16), do_b, TN)
        dp = _dot(do_b, mv_ref[...].astype(BF16), NT)
        ds = (p * (dp - jnp.sum(dp * p, axis=-1, keepdims=True)) * ATT_SCALE).astype(BF16)
        dqn = _dot(ds, kn_b, NN)
        dkn = _dot(ds, qn_b, TN)
        dq_ref[...] = _qk_norm_bwd(dqn, qg, qhat, rq)
        dmk_ref[...] += _qk_norm_bwd(dkn, kg, khat, rk)
        dqg_ref[...] += jnp.sum(dqn * qhat, axis=0, keepdims=True)
        dkg_ref[...] += jnp.sum(dkn * khat, axis=0, keepdims=True)

    cb = HEAD_DIM
    vec = pl.BlockSpec((1, cb), lambda h, i: (0, 0))
    qblk = lambda off: pl.BlockSpec((tq, cb), lambda h, i: (i, off // cb + h))
    memblk = lambda off: pl.BlockSpec((m_len, cb), lambda h, i: (0, off + h))
    return pl.pallas_call(
        body, name=name, grid=(HEADS, s_len // tq),
        in_specs=[qblk(OFF_QC), memblk(0), memblk(HEADS), vec, vec, qblk(OFF_ZC), qblk(OFF_YC)],
        out_specs=[qblk(0), memblk(0), memblk(0), vec, vec],
        out_shape=[jax.ShapeDtypeStruct((s_len, D_C), F32), jax.ShapeDtypeStruct((m_len, D_C), F32),
                   jax.ShapeDtypeStruct((m_len, D_C), F32), jax.ShapeDtypeStruct((1, cb), F32),
                   jax.ShapeDtypeStruct((1, cb), F32)],
        compiler_params=_params(("arbitrary", "arbitrary")),
    )(proj, mem_kv, mem_kv, qg, kg, proj, dy)


def _sgu_common(u_ref, v_ref, lng_ref, lnb_ref, w_ref, bias_ref):
    ug = _gelu(u_ref[...])
    vg = _gelu(v_ref[...])
    mu = jnp.mean(vg, axis=-1, keepdims=True)
    xc = vg - mu
    rstd = lax.rsqrt(jnp.mean(xc * xc, axis=-1, keepdims=True) + EPS)
    xhat = xc * rstd
    vn = xhat * lng_ref[...] + lnb_ref[...]
    vn_b = vn.astype(BF16)
    row = lax.broadcasted_iota(jnp.int32, (CHUNK, CHUNK), 0)
    col = lax.broadcasted_iota(jnp.int32, (CHUNK, CHUNK), 1)
    tril = row >= col
    mixed = []
    for g in range(A_GROUPS):
        w = jnp.where(tril, w_ref[g], 0.0).astype(BF16)
        sl = slice(g * CHUNK, (g + 1) * CHUNK)
        mixed.append(_dot(w, vn_b[:, sl], NN) + bias_ref[:, sl])
    return ug, xhat, rstd, vn_b, mixed, tril


def _gate_fwd(name, proj, o_b, o_c, lng, lnb, w_s, bias):
    s_len = proj.shape[0]

    def body(u_ref, v_ref, za_ref, zb_ref, zc_ref, ob_ref, oc_ref, lng_ref, lnb_ref, w_ref, bias_ref, y_ref, yt_ref):
        ug, _, _, _, mixed, _ = _sgu_common(u_ref, v_ref, lng_ref, lnb_ref, w_ref, bias_ref)
        sza, _ = _silu_and_grad(za_ref[...])
        gate = ug * sza

        def put(off, width, val):
            y_ref[:, off:off + width] = val.astype(BF16)
            yt_ref[off:off + width, :] = val.T.astype(BF16)

        for g in range(A_GROUPS):
            sl = slice(g * CHUNK, (g + 1) * CHUNK)
            put(g * CHUNK, CHUNK, gate[:, sl] * mixed[g])
        szb, _ = _silu_and_grad(zb_ref[...])
        put(OFF_YB, D_B, ob_ref[...] * szb)
        szc, _ = _silu_and_grad(zc_ref[...])
        put(OFF_YC, D_C, oc_ref[...] * szc)

    wide = lambda off: pl.BlockSpec((CHUNK, D_A), lambda i: (i, off // D_A))
    narrow = lambda off: pl.BlockSpec((CHUNK, D_B), lambda i: (i, off // D_B))
    vec = pl.BlockSpec((1, D_A), lambda i: (0, 0))
    return pl.pallas_call(
        body, name=name, grid=(s_len // CHUNK,),
        in_specs=[wide(OFF_U), wide(OFF_V), wide(OFF_ZA), narrow(OFF_ZB), narrow(OFF_ZC), narrow(0), narrow(0), vec, vec,
                  pl.BlockSpec((A_GROUPS, CHUNK, CHUNK), lambda i: (0, 0, 0)),
                  pl.BlockSpec((CHUNK, D_A), lambda i: (0, 0))],
        out_specs=[pl.BlockSpec((CHUNK, D_MODEL), lambda i: (i, 0)), pl.BlockSpec((D_MODEL, CHUNK), lambda i: (0, i))],
        out_shape=[jax.ShapeDtypeStruct((s_len, D_MODEL), BF16), jax.ShapeDtypeStruct((D_MODEL, s_len), BF16)],
        compiler_params=_params(("parallel",)),
    )(proj, proj, proj, proj, proj, o_b, o_c, lng, lnb, w_s, bias)


def _gate_bwd(name, proj, dy, o_b, o_c, dqkv, dq_c, lng, lnb, w_s, w_s_t, bias):
    s_len = proj.shape[0]
    n = s_len // CHUNK
    dq_b, dk_b, dv_b = dqkv

    def body(u_ref, v_ref, za_ref, zb_ref, zc_ref, dya_ref, dyb_ref, dyc_ref, ob_ref, oc_ref, dq_ref, dk_ref, dv_ref,
             dqc_ref, lng_ref, lnb_ref, w_ref, wt_ref, bias_ref, dp_ref, dw_ref, dsb_ref, dlng_ref, dlnb_ref, dbias_ref):
        i = pl.program_id(0)

        @pl.when(i == 0)
        def _():
            dw_ref[...] = jnp.zeros_like(dw_ref)
            dbias_ref[...] = jnp.zeros_like(dbias_ref)
            dlng_ref[...] = jnp.zeros_like(dlng_ref)
            dlnb_ref[...] = jnp.zeros_like(dlnb_ref)

        ug, xhat, rstd, vn_b, mixed, tril = _sgu_common(u_ref, v_ref, lng_ref, lnb_ref, w_ref, bias_ref)
        za = za_ref[...]
        sza, dsza = _silu_and_grad(za)
        dya = dya_ref[...]
        mixed_all = jnp.concatenate(mixed, axis=-1)
        d_mixed = dya * ug * sza
        dp_ref[:, OFF_U:OFF_U + D_A] = (dya * mixed_all * sza * _gelu_grad(u_ref[...])).astype(BF16)
        dp_ref[:, OFF_ZA:OFF_ZA + D_A] = (dya * ug * mixed_all * dsza).astype(BF16)
        dbias_ref[...] += d_mixed
        dm_b = d_mixed.astype(BF16)
        triu = lax.broadcasted_iota(jnp.int32, (CHUNK, CHUNK), 0) <= lax.broadcasted_iota(jnp.int32, (CHUNK, CHUNK), 1)
        d_vn = []
        for g in range(A_GROUPS):
            sl = slice(g * CHUNK, (g + 1) * CHUNK)
            wt = jnp.where(triu, wt_ref[g], 0.0).astype(BF16)
            d_vn.append(_dot(wt, dm_b[:, sl], NN))
            dw_ref[g] += jnp.where(tril, _dot(dm_b[:, sl], vn_b[:, sl], NT), 0.0)
        d_vn = jnp.concatenate(d_vn, axis=-1)
        dlng_ref[...] += jnp.sum(d_vn * xhat, axis=0, keepdims=True)
        dlnb_ref[...] += jnp.sum(d_vn, axis=0, keepdims=True)
        dxh = d_vn * lng_ref[...]
        d_vg = rstd * (dxh - jnp.mean(dxh, axis=-1, keepdims=True)
                       - xhat * jnp.mean(dxh * xhat, axis=-1, keepdims=True))
        dp_ref[:, OFF_V:OFF_V + D_A] = (d_vg * _gelu_grad(v_ref[...])).astype(BF16)
        dp_ref[:, OFF_QB:OFF_QB + D_B] = dq_ref[...].astype(BF16)
        dp_ref[:, OFF_KB:OFF_KB + D_B] = dk_ref[...].astype(BF16)
        dp_ref[:, OFF_VB:OFF_VB + D_B] = dv_ref[...].astype(BF16)
        _, dszb = _silu_and_grad(zb_ref[...])
        dp_ref[:, OFF_ZB:OFF_ZB + D_B] = (dyb_ref[...] * ob_ref[...] * dszb).astype(BF16)
        dp_ref[:, OFF_QC:OFF_QC + D_C] = dqc_ref[...].astype(BF16)
        _, dszc = _silu_and_grad(zc_ref[...])
        dp_ref[:, OFF_ZC:OFF_ZC + D_C] = (dyc_ref[...] * oc_ref[...] * dszc).astype(BF16)

        @pl.when(i == n - 1)
        def _():
            ch = lax.broadcasted_iota(jnp.int32, (D_A, CHUNK), 0)
            gcol = lax.broadcasted_iota(jnp.int32, (D_A, CHUNK), 1)
            pick = (ch // (D_A // A_GROUPS) == gcol).astype(BF16)
            rest = dbias_ref[...]
            tot = jnp.zeros((CHUNK, CHUNK), F32)
            for _ in range(3):
                term = rest.astype(BF16)
                tot = tot + _dot(term, pick, NN)
                rest = rest - term.astype(F32)
            dsb_ref[...] = tot

    wide = lambda off: pl.BlockSpec((CHUNK, D_A), lambda i: (i, off // D_A))
    narrow = lambda off: pl.BlockSpec((CHUNK, D_B), lambda i: (i, off // D_B))
    vec = pl.BlockSpec((1, D_A), lambda i: (0, 0))
    wspec = pl.BlockSpec((A_GROUPS, CHUNK, CHUNK), lambda i: (0, 0, 0))
    bspec = pl.BlockSpec((CHUNK, D_A), lambda i: (0, 0))
    return pl.pallas_call(
        body, name=name, grid=(n,),
        in_specs=[wide(OFF_U), wide(OFF_V), wide(OFF_ZA), narrow(OFF_ZB), narrow(OFF_ZC),
                  wide(0), narrow(OFF_YB), narrow(OFF_YC), narrow(0), narrow(0), narrow(0), narrow(0), narrow(0),
                  narrow(0), vec, vec, wspec, wspec, bspec],
        out_specs=[pl.BlockSpec((CHUNK, IN_WIDTH), lambda i: (i, 0)), wspec,
                   pl.BlockSpec((CHUNK, CHUNK), lambda i: (0, 0)), vec, vec],
        out_shape=[jax.ShapeDtypeStruct((s_len, IN_WIDTH), BF16), jax.ShapeDtypeStruct((A_GROUPS, CHUNK, CHUNK), F32),
                   jax.ShapeDtypeStruct((CHUNK, CHUNK), F32), jax.ShapeDtypeStruct((1, D_A), F32),
                   jax.ShapeDtypeStruct((1, D_A), F32)],
        scratch_shapes=[pltpu.VMEM((CHUNK, D_A), F32)],
        compiler_params=_params(("arbitrary",)),
    )(proj, proj, proj, proj, proj, dy, dy, dy, o_b, o_c, dq_b, dk_b, dv_b, dq_c, lng, lnb, w_s, w_s_t, bias)


IN_SHARD = IN_WIDTH // N_CHIPS
ROW_SHARD = D_MODEL // N_CHIPS


def _bias_rows(sgu_b_l):
    return jnp.repeat(sgu_b_l.T, D_A // A_GROUPS, axis=1)


class _WholeWeights:
    def __init__(self, w_in_all, w_kv_all, w_out_all):
        self.weights = (w_in_all, w_kv_all, w_out_all)

    def w_in(self, stage, h, proj):
        return (self.weights[0], jnp.arange(N_CHIPS, dtype=jnp.int32), 0, N_CHIPS) if stage == 0 else None

    def rest_start(self, proj):
        return None

    def rest_finish(self, o_b):
        return self.weights[1], self.weights[2], None

    def before_out(self, y):
        return None


def _layer_fwd(l, x, mem, sm, hooks):
    s_len = x.shape[0]
    m_len = mem.shape[0]
    tm = min(1024, s_len)
    h, h_t = _rms_fwd(f"rms_fwd_{l}", x, sm["norm_g"][l][None], min(256, s_len), transposed=True)
    proj, stage = None, 0
    while (ready := hooks.w_in(stage, h, proj)) is not None:
        w_in_all, order, first, count = ready
        proj = _matmul(
            f"in_proj_{l}_{stage}", h, w_in_all, grid=(s_len // tm, count, 1), place=order, into=proj,
            a_spec=pl.BlockSpec((tm, D_MODEL), lambda i, j, k, p: (i, 0)),
            b_spec=pl.BlockSpec((None, D_MODEL, IN_SHARD), lambda i, j, k, p: (p[first + j], 0, 0)),
            o_spec=pl.BlockSpec((tm, IN_SHARD), lambda i, j, k, p: (i, p[first + j])),
            out_shape=jax.ShapeDtypeStruct((s_len, IN_WIDTH), F32), dims=NN)
        stage += 1
    o_b = _sb_fwd(f"sb_fwd_{l}", proj, hooks.rest_start(proj))
    w_kv_all, w_out_all, after = hooks.rest_finish(o_b)
    mem_h = _rms_fwd(f"mem_rms_fwd_{l}", mem, sm["mem_norm_g"][l][None], m_len, after)
    mem_kv = _matmul(
        f"mem_kv_{l}", mem_h, w_kv_all, grid=(1, 2, N_CHIPS),
        a_spec=pl.BlockSpec((m_len, ROW_SHARD), lambda i, j, k: (0, k)),
        b_spec=pl.BlockSpec((None, ROW_SHARD, D_C), lambda i, j, k: (k, 0, j)),
        o_spec=pl.BlockSpec((m_len, D_C), lambda i, j, k: (0, j)),
        out_shape=jax.ShapeDtypeStruct((m_len, 2 * D_C), F32), dims=NN)
    qg, kg = sm["q_norm_g"][l][None], sm["k_norm_g"][l][None]
    o_c = _mem_fwd(f"mem_fwd_{l}", proj, mem_kv, qg, kg)
    bias = _bias_rows(sm["sgu_b"][l])
    y, y_t = _gate_fwd(f"gate_fwd_{l}", proj, o_b, o_c, sm["sgu_ln_g"][l][None], sm["sgu_ln_b"][l][None],
                       sm["sgu_w"][l], bias)
    tn_o = 512
    x_next = _matmul(
        f"out_proj_{l}", y, w_out_all, grid=(s_len // tm, D_MODEL // tn_o, 1),
        a_spec=pl.BlockSpec((tm, D_MODEL), lambda i, j, k: (i, 0)),
        b_spec=pl.BlockSpec((N_CHIPS, ROW_SHARD, tn_o), lambda i, j, k: (0, 0, j)),
        o_spec=pl.BlockSpec((tm, tn_o), lambda i, j, k: (i, j)),
        out_shape=jax.ShapeDtypeStruct((s_len, D_MODEL), F32), dims=NN,
        res=x, res_spec=pl.BlockSpec((tm, tn_o), lambda i, j, k: (i, j)), after=hooks.before_out(y))
    saved = dict(x=x, h_t=h_t, proj=proj, mem_h=mem_h, mem_kv=mem_kv, o_b=o_b, o_c=o_c, y_t=y_t, bias=bias,
                 weights=(w_in_all, w_kv_all, w_out_all))
    return x_next, saved


class _NoExchange:
    def __init__(self):
        self.gave, self.kept = {}, {}

    def eager(self, l):
        return False

    def start(self, l, group, gives):
        self.gave[l, group] = gives
        return None

    def landed(self, l, group, after):
        return [jnp.zeros_like(g) for g in self.gave[l, group]]

    def send(self, l, group, parts):
        self.kept[l, group] = parts
        return None


def _layer_bwd(l, dxo, dxo_b, mem, sm, saved, place, exchange):
    s_len = dxo.shape[0]
    m_len = mem.shape[0]
    proj, y_t, h_t, mem_h, mem_kv = saved["proj"], saved["y_t"], saved["h_t"], saved["mem_h"], saved["mem_kv"]
    w_in_all, w_kv_all, w_out_all = saved["weights"]
    tm = min(1024, s_len)
    tn = 768
    per = IN_SHARD // tn
    half_rows = ROW_SHARD // 2

    def halves(make):
        give = lambda: make("give", lambda p: 1 - p[1], None, F32)
        keep = lambda theirs: make("keep", lambda p: p[1], theirs, BF16)
        return give, keep

    def grad_out(tag, half, theirs, dtype):
        o_spec = pl.BlockSpec((None, half_rows, 1024), lambda i, j, k, p: (i, 0, j))
        return _matmul(
            f"d_w_out_{l}_{tag}", y_t, dxo_b, grid=(N_CHIPS, D_MODEL // 1024, 1), place=place,
            a_spec=pl.BlockSpec((half_rows, s_len), lambda i, j, k, p: (2 * i + half(p), 0)),
            b_spec=pl.BlockSpec((s_len, 1024), lambda i, j, k, p: (0, j)), o_spec=o_spec,
            out_shape=jax.ShapeDtypeStruct((N_CHIPS, half_rows, D_MODEL), dtype), dims=NN,
            res=theirs, res_spec=o_spec)

    def grad_in(tag, half, theirs, dtype, piece=0, pieces=1, after=None):
        rows = D_MODEL // 2 // pieces
        o_spec = pl.BlockSpec((None, rows, tn), lambda i, j, k, p: (j // per, 0, j % per))
        return _matmul(
            f"d_w_in_{l}_{tag}_{piece}", h_t, dproj, grid=(1, IN_WIDTH // tn, 1), place=place, after=after,
            a_spec=pl.BlockSpec((rows, s_len), lambda i, j, k, p: (half(p) * pieces + piece, 0)),
            b_spec=pl.BlockSpec((s_len, tn), lambda i, j, k, p: (0, j)), o_spec=o_spec,
            out_shape=jax.ShapeDtypeStruct((N_CHIPS, rows, IN_SHARD), dtype), dims=NN,
            res=theirs, res_spec=o_spec)

    def grad_kv(tag, half, theirs, dtype):
        o_spec = pl.BlockSpec((None, half_rows, 2 * D_C), lambda i, j, k, p: (i, 0, 0))
        return _matmul(
            f"d_w_kv_{l}_{tag}", mem_h, dkv_b, grid=(N_CHIPS, 1, 1), place=place,
            a_spec=pl.BlockSpec((m_len, half_rows), lambda i, j, k, p: (0, 2 * i + half(p))),
            b_spec=pl.BlockSpec((m_len, 2 * D_C), lambda i, j, k, p: (0, 0)), o_spec=o_spec,
            out_shape=jax.ShapeDtypeStruct((N_CHIPS, half_rows, 2 * D_C), dtype), dims=TN,
            res=theirs, res_spec=o_spec)

    give_out, keep_out = halves(grad_out)
    token = exchange.start(l, "out", [give_out()])
    dy = _matmul(
        f"d_y_{l}", dxo_b, w_out_all, grid=(s_len // tm, N_CHIPS, 1),
        a_spec=pl.BlockSpec((tm, D_MODEL), lambda i, j, k: (i, 0)),
        b_spec=pl.BlockSpec((None, ROW_SHARD, D_MODEL), lambda i, j, k: (j, 0, 0)),
        o_spec=pl.BlockSpec((tm, ROW_SHARD), lambda i, j, k: (i, j)),
        out_shape=jax.ShapeDtypeStruct((s_len, D_MODEL), F32), dims=NT, after=token)
    (theirs_out,) = exchange.landed(l, "out", dy)
    token = exchange.send(l, "out", [keep_out(theirs_out)])
    qg, kg = sm["q_norm_g"][l][None], sm["k_norm_g"][l][None]
    dqkv = _sb_bwd(f"sb_bwd_{l}", proj, dy, token)
    dq_c, dmk, dmv, dqg, dkg = _mem_bwd(f"mem_bwd_{l}", proj, mem_kv, qg, kg, dy)
    w_s = sm["sgu_w"][l]
    dproj, dws, dbias, dlng, dlnb = _gate_bwd(
        f"gate_bwd_{l}", proj, dy, saved["o_b"], saved["o_c"], dqkv, dq_c, sm["sgu_ln_g"][l][None],
        sm["sgu_ln_b"][l][None], w_s, jnp.swapaxes(w_s, 1, 2), saved["bias"])
    dkv_b = jnp.concatenate([dmk, dmv], axis=1).astype(BF16)
    give_kv, keep_kv = halves(grad_kv)

    def d_h(after):
        return _matmul(
            f"d_h_{l}", dproj, w_in_all, grid=(s_len // tm, D_MODEL // 512, 1),
            a_spec=pl.BlockSpec((tm, IN_WIDTH), lambda i, j, k: (i, 0)),
            b_spec=pl.BlockSpec((N_CHIPS, 512, IN_SHARD), lambda i, j, k: (0, j, 0)),
            o_spec=pl.BlockSpec((tm, 512), lambda i, j, k: (i, j)),
            out_shape=jax.ShapeDtypeStruct((s_len, D_MODEL), F32), dims=NT, after=after, vmem_mb=56)

    if exchange.eager(l):
        gives = [functools.partial(grad_in, "give", lambda p: 1 - p[1], None, F32, piece, 2) for piece in range(2)]
        keeps = [functools.partial(grad_in, "keep", lambda p: p[1], dtype=BF16, piece=piece, pieces=2)
                 for piece in range(2)]
        token = exchange.start(l, "in_p", [gives[0](), give_kv()])
        second = gives[1](after=token)
        token = exchange.start(l, "in_q", [second])
        theirs_in, theirs_kv = exchange.landed(l, "in_p", token)
        first_kept = keeps[0](theirs=theirs_in)
        token = exchange.send(l, "in_p", [first_kept, keep_kv(theirs_kv)])
        (theirs_in,) = exchange.landed(l, "in_q", token)
        token = exchange.send(l, "in_q", [keeps[1](theirs=theirs_in)])
        dh = d_h(token)
        token = None
    else:
        give_in, keep_in = halves(grad_in)
        token = exchange.start(l, "in", [give_in(), give_kv()])
        dh = d_h(token)
        theirs_in, theirs_kv = exchange.landed(l, "in", dh)
        token = exchange.send(l, "in", [keep_in(theirs_in), keep_kv(theirs_kv)])
    dx, dx_b, dng = _rms_bwd(f"rms_bwd_{l}", saved["x"], dh, dxo, sm["norm_g"][l][None], min(256, s_len), token)
    d_mem_h = _matmul(
        f"d_mem_h_{l}", dkv_b, w_kv_all, grid=(1, N_CHIPS, 1),
        a_spec=pl.BlockSpec((m_len, 2 * D_C), lambda i, j, k: (0, 0)),
        b_spec=pl.BlockSpec((None, ROW_SHARD, 2 * D_C), lambda i, j, k: (j, 0, 0)),
        o_spec=pl.BlockSpec((m_len, ROW_SHARD), lambda i, j, k: (0, j)),
        out_shape=jax.ShapeDtypeStruct((m_len, D_MODEL), F32), dims=NT)
    dmng = _rms_gain_grad(f"mem_rms_bwd_{l}", mem, d_mem_h)
    dsgu_b = dbias[:, :A_GROUPS].T
    small = dict(norm_g=dng[0], sgu_ln_g=dlng[0], sgu_ln_b=dlnb[0], sgu_w=dws, sgu_b=dsgu_b, mem_norm_g=dmng[0],
                 q_norm_g=dqg[0], k_norm_g=dkg[0])
    return dx, dx_b, small


SMALL_NAMES = ("norm_g", "sgu_ln_g", "sgu_ln_b", "sgu_w", "sgu_b", "mem_norm_g", "q_norm_g", "k_norm_g")


def _local_step(x, mem, target, sm, w_all):
    saved = []
    cur = x
    for l in range(DEPTH):
        cur, sv = _layer_fwd(l, cur, mem, sm, _WholeWeights(*w_all[l]))
        saved.append(sv)
    dxo, dxo_b, loss = _loss_and_grad("loss", cur, target, min(256, x.shape[0]))
    small = [None] * DEPTH
    exchange = _NoExchange()
    place = jnp.zeros((2,), jnp.int32)
    for l in reversed(range(DEPTH)):
        dxo, dxo_b, small[l] = _layer_bwd(l, dxo, dxo_b, mem, sm, saved[l], place, exchange)
    small = {k: jnp.stack([small[l][k] for l in range(DEPTH)]) for k in SMALL_NAMES}
    return loss, dxo, small, exchange.gave, exchange.kept


def _place():
    x, y, c = lax.axis_index("x"), lax.axis_index("y"), lax.axis_index("c")
    return x, y, c


def _other_chips(x, y):
    return [(1 - x, y, 2 * (1 - x) + y), (x, 1 - y, 2 * x + 1 - y), (1 - x, 1 - y, 2 * (1 - x) + 1 - y)]


D2D_CHUNKS = 8


def _place_index():
    return jnp.stack([2 * lax.axis_index("x") + lax.axis_index("y"), lax.axis_index("c")]).astype(jnp.int32)


def _cast_into_slot(name, w, l, place):
    _, rows, cols = w.shape
    tr = min(256, rows)

    def body(p_ref, w_ref, o_ref):
        o_ref[...] = w_ref[...].astype(BF16)

    return pl.pallas_call(
        body, name=name,
        grid_spec=pltpu.PrefetchScalarGridSpec(
            num_scalar_prefetch=1, grid=(rows // tr,),
            in_specs=[pl.BlockSpec((None, tr, cols), lambda i, p: (l, i, 0))],
            out_specs=pl.BlockSpec((None, tr, cols), lambda i, p: (p[0], i, 0))),
        out_shape=jax.ShapeDtypeStruct((N_CHIPS, rows, cols), BF16),
        compiler_params=_params(("parallel",)),
    )(place, w)


HBM = pl.BlockSpec(memory_space=pltpu.HBM)
SEM = pl.BlockSpec(memory_space=pltpu.SEMAPHORE)
DATAFLOW = pltpu.SideEffectType.DATAFLOW_SIDE_EFFECTING


def _in_hbm(a):
    return pltpu.with_memory_space_constraint(a, pltpu.HBM)


ALL_PEERS = (0, 1, 2)
NEIGHBOURS = (0, 1)
DIAGONAL = (2,)


def _chip_copies_start(name, srcs, lands, make_copy, after=None, peers=ALL_PEERS):
    n_t = len(srcs)
    in_place = lands is None
    n_after = 0 if after is None else 1

    def body(*refs):
        src = refs[:n_t]
        k = (n_t if in_place else 2 * n_t) + n_after
        send_sems, recv_sems = refs[k], refs[k + 1]
        land = refs[k + 2:k + 2 + n_t] if in_place else refs[k + 2 + n_t:k + 2 + 2 * n_t]
        token = refs[-1]
        x, y, c = _place()
        me = 2 * x + y
        others = _other_chips(x, y)
        for t in range(n_t):
            for px, py, pk in [others[p] for p in peers]:
                s, d = make_copy(src[t], land[t], me, pk, c)
                pltpu.make_async_remote_copy(
                    src_ref=s, dst_ref=d, send_sem=send_sems.at[t], recv_sem=recv_sems.at[t],
                    device_id=(px, py, c), device_id_type=MESH).start()
        token[...] = jnp.zeros_like(token)

    bufs = list(srcs) if in_place else list(srcs) + list(lands)
    outs = pl.pallas_call(
        body, name=name,
        in_specs=[HBM] * len(bufs) + [ANY] * n_after,
        out_specs=[SEM, SEM] + [HBM] * len(bufs) + [pl.BlockSpec(memory_space=pltpu.VMEM)],
        out_shape=[pltpu.SemaphoreType.DMA((n_t,)), pltpu.SemaphoreType.DMA((n_t,))]
        + [pltpu.HBM(b.shape, b.dtype) for b in bufs] + [jax.ShapeDtypeStruct((8, 128), F32)],
        input_output_aliases={i: 2 + i for i in range(len(bufs))},
        compiler_params=pltpu.CompilerParams(has_side_effects=DATAFLOW),
    )(*[_in_hbm(b) for b in bufs], *([] if after is None else [after]))
    return outs[0], outs[1], list(outs[2:2 + len(bufs)]), outs[-1]


def _chip_copies_wait(name, send_sems, recv_sems, bufs, sent, landed, after):
    n_b = len(bufs)

    def body(*refs):
        buf = refs[:n_b]
        send_ref, recv_ref = refs[n_b], refs[n_b + 1]
        x, y, c = _place()
        for t, (s, d) in enumerate(zip(sent(buf), landed(buf))):
            out = pltpu.make_async_remote_copy(src_ref=s, dst_ref=s, send_sem=send_ref.at[t], recv_sem=recv_ref.at[t],
                                               device_id=(x, y, c), device_id_type=MESH)
            out.wait_send()
            arrived = pltpu.make_async_remote_copy(src_ref=d, dst_ref=d, send_sem=send_ref.at[t],
                                                   recv_sem=recv_ref.at[t], device_id=(x, y, c), device_id_type=MESH)
            arrived.wait_recv()

    after = list(after) if isinstance(after, (list, tuple)) else [after]
    return pl.pallas_call(
        body, name=name,
        in_specs=[HBM] * n_b + [SEM, SEM] + [ANY] * len(after), out_specs=[HBM] * n_b,
        out_shape=[pltpu.HBM(b.shape, b.dtype) for b in bufs],
        input_output_aliases={i: i for i in range(n_b)},
        compiler_params=pltpu.CompilerParams(has_side_effects=DATAFLOW),
    )(*bufs, send_sems, recv_sems, *after)


def _gather_start(name, bufs, after=None, peers=ALL_PEERS):
    def make_copy(src, land, me, pk, c):
        hr = src.shape[1] // 2
        return src.at[me, pl.ds(c * hr, hr)], land.at[me, pl.ds(c * hr, hr)]

    return _chip_copies_start(name, bufs, None, make_copy, after, peers)


def _gather_wait(name, send_sems, recv_sems, bufs, after, peers=ALL_PEERS):
    def half_shards(buf):
        return [b.at[pl.ds(0, len(peers)), pl.ds(0, b.shape[1] // 2)] for b in buf]

    return _chip_copies_wait(name, send_sems, recv_sems, bufs, half_shards, half_shards, after)


def _gather_forward_start(name, bufs, peers=ALL_PEERS):
    n_t = len(bufs)

    def body(*refs):
        mine = refs[:n_t]
        send_sems, recv_sems = refs[n_t], refs[n_t + 1]
        buf = refs[n_t + 2:2 * n_t + 2]
        token = refs[-1]
        x, y, c = _place()
        others = _other_chips(x, y)
        for q in range(D2D_CHUNKS):
            for t in range(n_t):
                hr = mine[t].shape[1] // 2
                cr = hr // D2D_CHUNKS
                rows = pl.ds(c * hr + q * cr, cr)
                for _, _, pk in [others[p] for p in peers]:
                    pltpu.make_async_remote_copy(
                        src_ref=mine[t].at[pk, rows], dst_ref=buf[t].at[pk, rows], send_sem=send_sems.at[t],
                        recv_sem=recv_sems.at[t], device_id=(x, y, 1 - c), device_id_type=MESH).start()
        token[...] = jnp.zeros_like(token)

    outs = pl.pallas_call(
        body, name=name,
        in_specs=[HBM] * n_t,
        out_specs=[SEM, SEM] + [HBM] * n_t + [pl.BlockSpec(memory_space=pltpu.VMEM)],
        out_shape=[pltpu.SemaphoreType.DMA((n_t,)), pltpu.SemaphoreType.DMA((n_t,))]
        + [pltpu.HBM(b.shape, b.dtype) for b in bufs] + [jax.ShapeDtypeStruct((8, 128), F32)],
        input_output_aliases={i: 2 + i for i in range(n_t)},
        compiler_params=pltpu.CompilerParams(has_side_effects=DATAFLOW),
    )(*[_in_hbm(b) for b in bufs])
    return outs[0], outs[1], list(outs[2:2 + n_t]), outs[-1]


def _core_exchange_start(name, grads):
    n_t = len(grads)
    lands = [lax.empty(g.shape, g.dtype) for g in grads]

    def body(*refs):
        src = refs[:n_t]
        send_sems, recv_sems = refs[2 * n_t], refs[2 * n_t + 1]
        land = refs[2 * n_t + 2 + n_t:2 * n_t + 2 + 2 * n_t]
        token = refs[-1]
        x, y, c = _place()
        for q in range(D2D_CHUNKS):
            for t in range(n_t):
                cr = src[t].shape[1] // D2D_CHUNKS
                rows = pl.ds(q * cr, cr)
                pltpu.make_async_remote_copy(
                    src_ref=src[t].at[:, rows], dst_ref=land[t].at[:, rows], send_sem=send_sems.at[t],
                    recv_sem=recv_sems.at[t], device_id=(x, y, 1 - c), device_id_type=MESH).start()
        token[...] = jnp.zeros_like(token)

    bufs = list(grads) + lands
    outs = pl.pallas_call(
        body, name=name,
        in_specs=[HBM] * len(bufs),
        out_specs=[SEM, SEM] + [HBM] * len(bufs) + [pl.BlockSpec(memory_space=pltpu.VMEM)],
        out_shape=[pltpu.SemaphoreType.DMA((n_t,)), pltpu.SemaphoreType.DMA((n_t,))]
        + [pltpu.HBM(b.shape, b.dtype) for b in bufs] + [jax.ShapeDtypeStruct((8, 128), F32)],
        input_output_aliases={i: 2 + i for i in range(len(bufs))},
        compiler_params=pltpu.CompilerParams(has_side_effects=DATAFLOW),
    )(*[_in_hbm(b) for b in bufs])
    return outs[0], outs[1], list(outs[2:2 + len(bufs)]), outs[-1]


def _core_exchange_wait(name, send_sems, recv_sems, bufs, after):
    n_t = len(bufs) // 2

    def body(*refs):
        land = refs[n_t:2 * n_t]
        send_ref, recv_ref = refs[2 * n_t], refs[2 * n_t + 1]
        x, y, c = _place()
        for t in range(n_t):
            whole = pltpu.make_async_remote_copy(src_ref=land[t], dst_ref=land[t], send_sem=send_ref.at[t],
                                                 recv_sem=recv_ref.at[t], device_id=(x, y, c), device_id_type=MESH)
            whole.wait_send()
            whole.wait_recv()

    outs = pl.pallas_call(
        body, name=name,
        in_specs=[HBM] * (2 * n_t) + [SEM, SEM, ANY], out_specs=[HBM] * (2 * n_t),
        out_shape=[pltpu.HBM(b.shape, b.dtype) for b in bufs],
        input_output_aliases={i: i for i in range(2 * n_t)},
        compiler_params=pltpu.CompilerParams(has_side_effects=DATAFLOW),
    )(*bufs, send_sems, recv_sems, after)
    return list(outs[:n_t]), list(outs[n_t:])


def _chip_exchange_start(name, parts):
    lands = [lax.empty(p.shape, p.dtype) for p in parts]
    return _chip_copies_start(name, parts, lands, lambda src, land, me, pk, c: (src.at[pk], land.at[me]))


def _chip_exchange_wait(name, send_sems, recv_sems, bufs, after):
    n_t = len(bufs) // 2
    return _chip_copies_wait(name, send_sems, recv_sems, bufs,
                             lambda buf: [b.at[pl.ds(0, 3)] for b in buf[:n_t]],
                             lambda buf: [b.at[pl.ds(0, 3)] for b in buf[n_t:]], after)


def _sum_chips(name, parts, landed, place, l, stacked, piece=0, pieces=1):
    chips, rows, cols = landed.shape
    tr = min(256, rows)
    per = rows // tr

    def body(p_ref, own_ref, *refs):
        land, o_ref = refs[:chips], refs[-1]
        tot = None
        for k in range(chips):
            term = jnp.where(p_ref[0] == k, own_ref[...], land[k][...]).astype(F32)
            tot = term if tot is None else tot + term
        o_ref[...] = tot

    def from_chip(k):
        return pl.BlockSpec((None, tr, cols), lambda i, p: (jnp.where(p[0] == k, (k + 1) % chips, k), i, 0))

    in_specs = [pl.BlockSpec((None, tr, cols), lambda i, p: (p[0], i, 0))] + [from_chip(k) for k in range(chips)]
    args = [parts] + [landed] * chips
    aliases = {}
    if stacked is not None:
        in_specs.append(ANY)
        args.append(stacked)
        aliases = {len(args): 0}
    return pl.pallas_call(
        body, name=name,
        grid_spec=pltpu.PrefetchScalarGridSpec(
            num_scalar_prefetch=1, grid=(per,), in_specs=in_specs,
            out_specs=pl.BlockSpec((None, tr, cols), lambda i, p: (l, (p[1] * pieces + piece) * per + i, 0))),
        out_shape=jax.ShapeDtypeStruct((DEPTH, 2 * rows * pieces, cols), F32), input_output_aliases=aliases,
        compiler_params=_params(("parallel",)),
    )(place, *args)


def _core_share_start(name, bufs, l):
    n_t = len(bufs)

    def body(*refs):
        mine = refs[:n_t]
        send_sems, recv_sems = refs[n_t], refs[n_t + 1]
        buf = refs[n_t + 2:2 * n_t + 2]
        token = refs[-1]
        x, y, c = _place()
        for q in range(D2D_CHUNKS):
            for t in range(n_t):
                hr = mine[t].shape[1] // 2
                cr = hr // D2D_CHUNKS
                rows = pl.ds(c * hr + q * cr, cr)
                pltpu.make_async_remote_copy(
                    src_ref=mine[t].at[l, rows], dst_ref=buf[t].at[l, rows], send_sem=send_sems.at[t],
                    recv_sem=recv_sems.at[t], device_id=(x, y, 1 - c), device_id_type=MESH).start()
        token[...] = jnp.zeros_like(token)

    outs = pl.pallas_call(
        body, name=name,
        in_specs=[HBM] * n_t,
        out_specs=[SEM, SEM] + [HBM] * n_t + [pl.BlockSpec(memory_space=pltpu.VMEM)],
        out_shape=[pltpu.SemaphoreType.DMA((n_t,)), pltpu.SemaphoreType.DMA((n_t,))]
        + [pltpu.HBM(b.shape, b.dtype) for b in bufs] + [jax.ShapeDtypeStruct((8, 128), F32)],
        input_output_aliases={i: 2 + i for i in range(n_t)},
        compiler_params=pltpu.CompilerParams(has_side_effects=DATAFLOW),
    )(*[_in_hbm(b) for b in bufs])
    return outs[0], outs[1], list(outs[2:2 + n_t]), outs[-1]


def _core_share_wait(name, send_sems, recv_sems, bufs, l, after):
    def half_layer(buf):
        return [b.at[l, pl.ds(0, b.shape[1] // 2)] for b in buf]

    return _chip_copies_wait(name, send_sems, recv_sems, bufs, half_layer, half_layer, after)


def _all_reduce_small(vec, after=None):
    rows, lanes = vec.shape
    hr = rows // 2

    def body(v_ref, *refs):
        o_ref, sib_ref, chips_ref, send_sems, recv_sems = refs[-5:]
        x, y, c = _place()
        me = 2 * x + y
        sibling = (x, y, 1 - c)
        mine = pl.ds(pl.multiple_of(c * hr, 8), hr)
        theirs = pl.ds(pl.multiple_of((1 - c) * hr, 8), hr)
        swap = pltpu.make_async_remote_copy(
            src_ref=v_ref.at[theirs], dst_ref=sib_ref, send_sem=send_sems.at[0], recv_sem=recv_sems.at[0],
            device_id=sibling, device_id_type=MESH)
        swap.start()
        swap.wait_recv()
        chips_ref[me] = v_ref[mine] + sib_ref[...]
        copies = []
        for j, (px, py, pk) in enumerate(_other_chips(x, y)):
            cp = pltpu.make_async_remote_copy(
                src_ref=chips_ref.at[me], dst_ref=chips_ref.at[me], send_sem=send_sems.at[1 + j],
                recv_sem=recv_sems.at[1 + j], device_id=(px, py, c), device_id_type=MESH)
            cp.start()
            copies.append(cp)
        for j, (px, py, pk) in enumerate(_other_chips(x, y)):
            pltpu.make_async_remote_copy(
                src_ref=chips_ref.at[pk], dst_ref=chips_ref.at[pk], send_sem=send_sems.at[1 + j],
                recv_sem=recv_sems.at[1 + j], device_id=(px, py, c), device_id_type=MESH).wait_recv()
        tot = chips_ref[0]
        for k in range(1, N_CHIPS):
            tot = tot + chips_ref[k]
        o_ref[mine] = tot
        share = pltpu.make_async_remote_copy(
            src_ref=o_ref.at[mine], dst_ref=o_ref.at[mine], send_sem=send_sems.at[4], recv_sem=recv_sems.at[4],
            device_id=sibling, device_id_type=MESH)
        share.start()
        pltpu.make_async_remote_copy(
            src_ref=o_ref.at[theirs], dst_ref=o_ref.at[theirs], send_sem=send_sems.at[4], recv_sem=recv_sems.at[4],
            device_id=sibling, device_id_type=MESH).wait_recv()
        swap.wait_send()
        for cp in copies:
            cp.wait_send()
        share.wait_send()

    vm = pl.BlockSpec(memory_space=pltpu.VMEM)
    return pl.pallas_call(
        body, name="small_all_reduce", in_specs=[vm] + ([] if after is None else [ANY]), out_specs=vm,
        out_shape=jax.ShapeDtypeStruct((rows, lanes), F32),
        scratch_shapes=[pltpu.VMEM((hr, lanes), F32), pltpu.VMEM((N_CHIPS, hr, lanes), F32),
                        pltpu.SemaphoreType.DMA((5,)), pltpu.SemaphoreType.DMA((5,))],
        compiler_params=pltpu.CompilerParams(has_side_effects=True, vmem_limit_bytes=48 * MIB),
    )(vec, *([] if after is None else [after]))


def _adamw(name, w, g, m, v, place, l=0, half=None, done=None, after=None):
    layers, rows, cols = w.shape
    span = rows if half is None else rows // 2
    tr = span
    for cand in (256, 128, 64, 32, 16, 8):
        if span % cand == 0:
            tr = cand
            break
    per = span // tr
    c1 = 1.0 - ADAM_B1 ** ADAM_STEP
    c2 = 1.0 - ADAM_B2 ** ADAM_STEP

    def first_block(p):
        return 0 if half is None else (p[1] if half == "own" else 1 - p[1]) * per

    def body(p_ref, w_ref, g_ref, m_ref, v_ref, *refs):
        go_ref, d_ref, nm_ref, nv_ref = refs[-4:]
        gv = g_ref[...]
        nm = ADAM_B1 * m_ref[...] + (1.0 - ADAM_B1) * gv
        nv = ADAM_B2 * v_ref[...] + (1.0 - ADAM_B2) * (gv * gv)
        go_ref[...] = gv
        nm_ref[...] = nm
        nv_ref[...] = nv
        d_ref[...] = -ADAM_LR * ((nm / c1) / (jnp.sqrt(nv / c2) + ADAM_EPS) + ADAM_WD * w_ref[...])

    blk = pl.BlockSpec((None, tr, cols), lambda i, p: (l, first_block(p) + i, 0))
    out = jax.ShapeDtypeStruct((layers, rows, cols), F32)
    extra = ([] if done is None else list(done)) + ([] if after is None else [after])
    aliases = {} if done is None else {5 + i: i for i in range(4)}
    return pl.pallas_call(
        body, name=name,
        grid_spec=pltpu.PrefetchScalarGridSpec(
            num_scalar_prefetch=1, grid=(per,), in_specs=[blk] * 4 + [ANY] * len(extra), out_specs=[blk] * 4),
        out_shape=[out] * 4, input_output_aliases=aliases,
        compiler_params=_params(("parallel",)),
    )(place, w, g, m, v, *extra)


LANES = 128
SUBLANES = 8
SMALL_SHAPES = {
    "norm_g": (DEPTH, D_MODEL), "sgu_ln_g": (DEPTH, D_A), "sgu_ln_b": (DEPTH, D_A),
    "sgu_w": (DEPTH, A_GROUPS, CHUNK, CHUNK), "sgu_b": (DEPTH, A_GROUPS, CHUNK), "mem_norm_g": (DEPTH, D_MODEL),
    "q_norm_g": (DEPTH, HEAD_DIM), "k_norm_g": (DEPTH, HEAD_DIM)}


def _small_layout():
    at, off = {}, 0
    for k in SMALL_NAMES:
        n = math.prod(SMALL_SHAPES[k]) // LANES
        at[k] = (off, n)
        off += -(-n // SUBLANES) * SUBLANES
    return at, off, -(-(off + SUBLANES) // (2 * SUBLANES)) * 2 * SUBLANES


def _pack_small(parts, loss=None):
    at, loss_row, rows = _small_layout()
    pieces = []
    for k in SMALL_NAMES:
        n = at[k][1]
        pieces.append(jnp.pad(parts[k].reshape(n, LANES), ((0, -(-n // SUBLANES) * SUBLANES - n), (0, 0))))
    tile = jnp.zeros((SUBLANES, LANES), F32) if loss is None else jnp.broadcast_to(loss.reshape(1, 1), (SUBLANES, LANES))
    pieces += [tile, jnp.zeros((rows - loss_row - SUBLANES, LANES), F32)]
    return jnp.concatenate(pieces)


def _adamw_small(w, g, m, v):
    at, _, rows = _small_layout()
    c1 = 1.0 - ADAM_B1 ** ADAM_STEP
    c2 = 1.0 - ADAM_B2 ** ADAM_STEP
    n_names = len(SMALL_NAMES)

    def body(w_ref, g_ref, m_ref, v_ref, *refs):
        outs, (d_ref, nm_ref, nv_ref) = refs[:4 * n_names], refs[4 * n_names:]
        gv = g_ref[...]
        nm = ADAM_B1 * m_ref[...] + (1.0 - ADAM_B1) * gv
        nv = ADAM_B2 * v_ref[...] + (1.0 - ADAM_B2) * (gv * gv)
        nm_ref[...] = nm
        nv_ref[...] = nv
        d_ref[...] = -ADAM_LR * ((nm / c1) / (jnp.sqrt(nv / c2) + ADAM_EPS) + ADAM_WD * w_ref[...])
        for kind, src in enumerate((g_ref, d_ref, nm_ref, nv_ref)):
            for i, k in enumerate(SMALL_NAMES):
                o_ref = outs[kind * n_names + i]
                first, n = at[k]
                shape = SMALL_SHAPES[k]
                if shape[-1] == LANES:
                    o_ref[...] = src[pl.ds(first, n), :].reshape(shape)
                else:
                    per = shape[-1] // LANES
                    for r in range(n):
                        o_ref[pl.ds(r // per, 1), pl.ds((r % per) * LANES, LANES)] = src[pl.ds(first + r, 1), :]

    out_shape = [jax.ShapeDtypeStruct(SMALL_SHAPES[k], F32) for _ in range(4) for k in SMALL_NAMES]
    outs = pl.pallas_call(
        body, name="adamw_small", out_shape=out_shape,
        scratch_shapes=[pltpu.VMEM((rows, LANES), F32)] * 3, compiler_params=_params(None),
    )(w, g, m, v)
    return [dict(zip(SMALL_NAMES, outs[kind * n_names:(kind + 1) * n_names])) for kind in range(4)]


WEIGHT_ORDER = ("norm_g", "w_in", "sgu_ln_g", "sgu_ln_b", "sgu_w", "sgu_b", "mem_norm_g", "w_mem_kv", "q_norm_g",
                "k_norm_g", "w_out")


def kernel(x, mem, norm_g, w_in, sgu_ln_g, sgu_ln_b, sgu_w, sgu_b, mem_norm_g, w_mem_kv, q_norm_g, k_norm_g, w_out, loss_target, m_norm_g, m_w_in, m_sgu_ln_g, m_sgu_ln_b, m_sgu_w, m_sgu_b, m_mem_norm_g, m_w_mem_kv, m_q_norm_g, m_k_norm_g, m_w_out, v_norm_g, v_w_in, v_sgu_ln_g, v_sgu_ln_b, v_sgu_w, v_sgu_b, v_mem_norm_g, v_w_mem_kv, v_q_norm_g, v_k_norm_g, v_w_out):
    weights = dict(norm_g=norm_g, w_in=w_in, sgu_ln_g=sgu_ln_g, sgu_ln_b=sgu_ln_b, sgu_w=sgu_w, sgu_b=sgu_b,
                   mem_norm_g=mem_norm_g, w_mem_kv=w_mem_kv, q_norm_g=q_norm_g, k_norm_g=k_norm_g, w_out=w_out)
    mom_m = dict(norm_g=m_norm_g, w_in=m_w_in, sgu_ln_g=m_sgu_ln_g, sgu_ln_b=m_sgu_ln_b, sgu_w=m_sgu_w, sgu_b=m_sgu_b,
                 mem_norm_g=m_mem_norm_g, w_mem_kv=m_w_mem_kv, q_norm_g=m_q_norm_g, k_norm_g=m_k_norm_g, w_out=m_w_out)
    mom_v = dict(norm_g=v_norm_g, w_in=v_w_in, sgu_ln_g=v_sgu_ln_g, sgu_ln_b=v_sgu_ln_b, sgu_w=v_sgu_w, sgu_b=v_sgu_b,
                 mem_norm_g=v_mem_norm_g, w_mem_kv=v_w_mem_kv, q_norm_g=v_q_norm_g, k_norm_g=v_k_norm_g, w_out=v_w_out)
    big = ("w_in", "w_mem_kv", "w_out")
    sm = {k: weights[k] for k in SMALL_NAMES}

    place = _place_index()
    xs, mems, target = x[0], mem[0], loss_target[0]

    slots = [[_cast_into_slot(f"cast_{k}_{l}", weights[k], l, place) for k in big] for l in range(DEPTH)]
    saved = [None] * DEPTH

    chips, cores = {}, {}
    me = place[0]
    arrival = jnp.stack([me, me ^ 2, me ^ 1, 3 - me]).astype(jnp.int32)
    shard_order = jnp.arange(N_CHIPS, dtype=jnp.int32)

    def start_gather(l, after=None):
        chips[l, "in"] = _gather_start(f"gather_start_{l}_in", slots[l][:1], after)
        chips[l, "rest"] = _gather_start(f"gather_start_{l}_rest", slots[l][1:], chips[l, "in"][3])
        return chips[l, "rest"][3]

    def hand_to_sibling(l, group, after):
        send_sems, recv_sems, bufs, _ = chips[l, group]
        bufs = _gather_wait(f"gather_wait_{l}_{group}", send_sems, recv_sems, bufs, after)
        cores[l, group] = _gather_forward_start(f"gather_forward_{l}_{group}", bufs)
        return cores[l, group][3]

    def whole(l, group, after):
        send_sems, recv_sems, bufs, _ = cores[l, group]
        return _gather_wait(f"gather_whole_{l}_{group}", send_sems, recv_sems, bufs, after)

    later_slots = [s for layer in slots[1:] for s in layer]

    class Gathered:
        def __init__(self, l):
            self.l = l
            self.buf = None

        def landed_from(self, tag, peers, after, behind, then=None):
            send_sems, recv_sems, _, _ = chips[0, "in_" + tag]
            buf = _gather_wait(f"gather_wait_0_in_{tag}", send_sems, recv_sems, self.buf, after, peers)
            if then is not None:
                buf, more = then(buf)
                behind = behind + more
            send_sems, recv_sems, buf, token = _gather_forward_start(f"gather_forward_0_in_{tag}", buf, peers)
            self.buf = _gather_wait(f"gather_whole_0_in_{tag}", send_sems, recv_sems, buf, [token] + behind, peers)

        def w_in(self, stage, h, proj):
            if self.l > 0:
                return (whole(self.l, "in", h)[0], shard_order, 0, N_CHIPS) if stage == 0 else None
            if stage == 0:
                self.buf = chips[0, "in_n"][2]
                return self.buf[0], arrival, 0, 1
            if stage == 1:
                def start_others(buf):
                    chips[0, "in_d"] = _gather_start("gather_start_0_in_d", buf, None, DIAGONAL)
                    chips[0, "rest"] = _gather_start("gather_start_0_rest", slots[0][1:], chips[0, "in_d"][3])
                    return chips[0, "in_d"][2], [chips[0, "rest"][3]]

                self.landed_from("n", NEIGHBOURS, proj, later_slots + [chips[0, "in_n"][3]], start_others)
                return self.buf[0], arrival, 1, 2
            if stage == 2:
                self.landed_from("d", DIAGONAL, [proj, chips[0, "rest"][3]], [])
                return self.buf[0], arrival, 3, 1
            return None

        def rest_start(self, proj):
            token = proj if self.l == 0 else hand_to_sibling(self.l, "rest", proj)
            return start_gather(self.l + 1, token) if self.l + 1 < DEPTH else token

        def rest_finish(self, o_b):
            if self.l == 0:
                o_b = hand_to_sibling(self.l, "rest", o_b)
            w_kv_all, w_out_all = whole(self.l, "rest", o_b)
            return w_kv_all, w_out_all, None

        def before_out(self, y):
            return hand_to_sibling(self.l + 1, "in", y) if self.l + 1 < DEPTH else None

    chips[0, "in_n"] = _gather_start("gather_start_0_in_n", slots[0][:1], None, NEIGHBOURS)
    cur = xs
    for l in range(DEPTH):
        cur, saved[l] = _layer_fwd(l, cur, mems, sm, Gathered(l))
    dxo, dxo_b, loss_part = _loss_and_grad("loss", cur, target, min(256, xs.shape[0]))

    small_g = [None] * DEPTH
    flight = {}

    class Exchange:
        def __init__(self):
            self.cores = {}

        def eager(self, l):
            return l == 0

        def start(self, l, group, gives):
            *self.cores[l, group], token = _core_exchange_start(f"grad_core_start_{l}_{group}", gives)
            return token

        def landed(self, l, group, after):
            send_sems, recv_sems, bufs = self.cores[l, group]
            return _core_exchange_wait(f"grad_core_wait_{l}_{group}", send_sems, recv_sems, bufs, after)[1]

        def send(self, l, group, parts):
            *flight[l, group], token = _chip_exchange_start(f"grad_chip_start_{l}_{group}", parts)
            return token

    exchange = Exchange()
    for l in reversed(range(DEPTH)):
        dxo, dxo_b, small_g[l] = _layer_bwd(l, dxo, dxo_b, mems, sm, saved[l], place, exchange)
    grad_x = dxo

    groups = (("out", ("w_out",)), ("in", ("w_in", "w_mem_kv")))
    halves, stepped = dict.fromkeys(big), dict.fromkeys(big)
    small_g = {k: jnp.stack([small_g[l][k] for l in range(DEPTH)]) for k in SMALL_NAMES}
    after = grad_x
    sharing = {}

    def land_group(l, group, entries):
        nonlocal after
        send_sems, recv_sems, bufs = flight[l, group]
        bufs = _chip_exchange_wait(f"grad_chip_wait_{l}_{group}", send_sems, recv_sems, bufs, after)
        for t, (k, piece, pieces) in enumerate(entries):
            halves[k] = _sum_chips(f"grad_chip_sum_{l}_{k}_{piece}", bufs[t], bufs[len(entries) + t], place, l,
                                   halves[k], piece, pieces)
        after = halves[entries[-1][0]]

    def share_group(l, group, names):
        nonlocal after
        *sharing[l, group], after = _core_share_start(f"grad_core_share_{l}_{group}", [halves[k] for k in names], l)

    def reduce_group(l, group, names):
        land_group(l, group, [(k, 0, 1) for k in names])
        share_group(l, group, names)

    def step(l, k, buf, half):
        nonlocal after
        tag = "" if half is None else "_" + half
        stepped[k] = _adamw(f"adamw_{k}_{l}{tag}", weights[k], buf, mom_m[k], mom_v[k], place, l, half, stepped[k],
                            after)
        after = stepped[k][1]

    def step_group(l, group, names, overlap):
        nonlocal after
        send_sems, recv_sems, bufs = sharing[l, group]
        if overlap:
            for k, buf in zip(names, bufs):
                step(l, k, buf, "own")
        bufs = _core_share_wait(f"grad_core_shared_{l}_{group}", send_sems, recv_sems, bufs, l, after)
        for k, buf in zip(names, bufs):
            halves[k] = buf
            step(l, k, buf, "other" if overlap else None)

    for l in reversed(range(DEPTH)):
        last = l == 0
        (g_out, n_out), (g_in, n_in) = groups
        reduce_group(l, g_out, n_out)
        if last:
            step_group(l, g_out, n_out, False)
            small_sum = _all_reduce_small(_pack_small(small_g, loss_part), after)
            small_step = _adamw_small(_pack_small(sm), small_sum, _pack_small({k: mom_m[k] for k in SMALL_NAMES}),
                                      _pack_small({k: mom_v[k] for k in SMALL_NAMES}))
            after = small_step[1]["sgu_w"]
            land_group(l, "in_p", [("w_in", 0, 2), ("w_mem_kv", 0, 1)])
            land_group(l, "in_q", [("w_in", 1, 2)])
            share_group(l, g_in, n_in)
        else:
            reduce_group(l, g_in, n_in)
            step_group(l, g_out, n_out, False)
        step_group(l, g_in, n_in, last)

    grads, delta, new_m, new_v = ({k: stepped[k][i] for k in big} for i in range(4))
    for out, small in zip((grads, delta, new_m, new_v), small_step):
        out.update(small)
    loss = small_sum[_small_layout()[1], 0]
    return (loss, grad_x[None], *[grads[k] for k in WEIGHT_ORDER], *[delta[k] for k in WEIGHT_ORDER],
            *[new_m[k] for k in WEIGHT_ORDER], *[new_v[k] for k in WEIGHT_ORDER])
```

```python
import functools
import math

import jax
import jax.numpy as jnp
from jax import lax
from jax.experimental import pallas as pl
from jax.experimental.pallas import tpu as pltpu

F32 = jnp.float32
BF16 = jnp.bfloat16
MESH = pl.DeviceIdType.MESH

D_MODEL = 2048
DEPTH = 2
CHUNK = 128
D_A = 1024
A_GROUPS = 8
D_B = 512
D_C = 512
HEADS = 4
HEAD_DIM = 128
IN_WIDTH = 6144
N_CHIPS = 4
EPS = 1e-6
ATT_SCALE = 1.0 / math.sqrt(HEAD_DIM)

OFF_U, OFF_V, OFF_ZA = 0, 1024, 2048
OFF_QB, OFF_KB, OFF_VB, OFF_ZB = 3072, 3584, 4096, 4608
OFF_QC, OFF_ZC = 5120, 5632
OFF_YB, OFF_YC = 1024, 1536

ADAM_LR = 0.001
ADAM_B1 = 0.9
ADAM_B2 = 0.999
ADAM_EPS = 1e-08
ADAM_WD = 0.01
ADAM_STEP = 10

MIB = 1024 * 1024
ANY = pl.BlockSpec(memory_space=pl.ANY)


def _params(semantics=None, vmem_mb=48):
    return pltpu.CompilerParams(dimension_semantics=semantics, vmem_limit_bytes=vmem_mb * MIB)


def _gelu(x):
    return 0.5 * x * (1.0 + lax.erf(x * (1.0 / math.sqrt(2.0))))


def _gelu_grad(x):
    cdf = 0.5 * (1.0 + lax.erf(x * (1.0 / math.sqrt(2.0))))
    pdf = jnp.exp(-0.5 * x * x) * (1.0 / math.sqrt(2.0 * math.pi))
    return cdf + x * pdf


def _sigmoid(x):
    return 1.0 / (1.0 + jnp.exp(-x))


def _silu_and_grad(z):
    s = _sigmoid(z)
    return z * s, s * (1.0 + z * (1.0 - s))


def _split_bf16(x):
    hi = x.astype(BF16)
    lo = (x - hi.astype(F32)).astype(BF16)
    return hi, lo


def _dot(a, b, dims):
    return lax.dot_general(a, b, (dims, ((), ())), preferred_element_type=F32)


NN = ((1,), (0,))
NT = ((1,), (1,))
TN = ((0,), (0,))


def _matmul(name, a, b, *, grid, a_spec, b_spec, o_spec, out_shape, dims, res=None, res_spec=None, after=None,
            place=None, into=None, vmem_mb=48):
    nk = grid[2]
    n_in = 2 + (res is not None) + (after is not None) + (into is not None)

    def body(*refs):
        if place is not None:
            refs = refs[1:]
        a_ref, b_ref = refs[0], refs[1]
        r_ref = refs[2] if res is not None else None
        o_ref = refs[n_in]
        if len(b_ref.shape) == 3 and dims == NN:
            part = _dot(a_ref[...], b_ref[...].reshape(-1, b_ref.shape[-1]), dims)
        elif len(b_ref.shape) == 3:
            width = b_ref.shape[-1]
            part = None
            for s in range(b_ref.shape[0]):
                term = _dot(a_ref[:, s * width:(s + 1) * width], b_ref[s], dims)
                part = term if part is None else part + term
        else:
            part = _dot(a_ref[...], b_ref[...], dims)
        if nk == 1:
            if r_ref is not None:
                part = part + r_ref[...]
            o_ref[...] = part.astype(o_ref.dtype)
            return
        acc_ref = refs[n_in + 1]
        k = pl.program_id(2)

        @pl.when(k == 0)
        def _():
            acc_ref[...] = part

        @pl.when(k > 0)
        def _():
            acc_ref[...] += part

        @pl.when(k == nk - 1)
        def _():
            tot = acc_ref[...]
            if r_ref is not None:
                tot = tot + r_ref[...]
            o_ref[...] = tot.astype(o_ref.dtype)

    in_specs = [a_spec, b_spec]
    args = [a, b]
    if res is not None:
        in_specs.append(res_spec)
        args.append(res)
    if after is not None:
        in_specs.append(ANY)
        args.append(after)
    aliases = {}
    if into is not None:
        in_specs.append(ANY)
        args.append(into)
        aliases = {len(args) - 1 + (place is not None): 0}
    acc_shape = tuple(d for d in o_spec.block_shape if d is not None)
    scratch = [pltpu.VMEM(acc_shape, F32)] if nk > 1 else []
    params = _params(("parallel", "parallel", "arbitrary"), vmem_mb)
    if place is not None:
        return pl.pallas_call(
            body, name=name, out_shape=out_shape, compiler_params=params, input_output_aliases=aliases,
            grid_spec=pltpu.PrefetchScalarGridSpec(num_scalar_prefetch=1, grid=grid, in_specs=in_specs,
                                                   out_specs=o_spec, scratch_shapes=scratch),
        )(place, *args)
    return pl.pallas_call(
        body, name=name, grid=grid, in_specs=in_specs, out_specs=o_spec, out_shape=out_shape,
        scratch_shapes=scratch, compiler_params=params, input_output_aliases=aliases,
    )(*args)


def _rms_fwd(name, x, g, tr, after=None, transposed=False):
    rows, d = x.shape

    def body(x_ref, g_ref, *refs):
        outs = refs[1:] if after is not None else refs
        xv = x_ref[...]
        r = lax.rsqrt(jnp.mean(xv * xv, axis=-1, keepdims=True) + EPS)
        h = xv * r * g_ref[...]
        outs[0][...] = h.astype(BF16)
        if transposed:
            outs[1][...] = h.T.astype(BF16)

    out_specs = [pl.BlockSpec((tr, d), lambda i: (i, 0))]
    out_shape = [jax.ShapeDtypeStruct((rows, d), BF16)]
    if transposed:
        out_specs.append(pl.BlockSpec((d, tr), lambda i: (0, i)))
        out_shape.append(jax.ShapeDtypeStruct((d, rows), BF16))
    outs = pl.pallas_call(
        body, name=name, grid=(rows // tr,),
        in_specs=[pl.BlockSpec((tr, d), lambda i: (i, 0)), pl.BlockSpec((1, d), lambda i: (0, 0))]
        + ([] if after is None else [ANY]),
        out_specs=out_specs, out_shape=out_shape,
        compiler_params=_params(("parallel",)),
    )(x, g, *([] if after is None else [after]))
    return outs if transposed else outs[0]


def _rms_bwd(name, x, dh, dres, g, tr, after=None):
    rows, d = x.shape

    def body(x_ref, dh_ref, dres_ref, g_ref, *refs):
        dx_ref, dxb_ref, dg_ref = refs[-3:]
        xv = x_ref[...]
        r = lax.rsqrt(jnp.mean(xv * xv, axis=-1, keepdims=True) + EPS)
        xhat = xv * r
        dhv = dh_ref[...]
        dxh = dhv * g_ref[...]
        dx = r * (dxh - xhat * jnp.mean(dxh * xhat, axis=-1, keepdims=True)) + dres_ref[...]
        dx_ref[...] = dx
        dxb_ref[...] = dx.astype(BF16)
        part = jnp.sum(dhv * xhat, axis=0, keepdims=True)

        @pl.when(pl.program_id(0) == 0)
        def _():
            dg_ref[...] = part

        @pl.when(pl.program_id(0) > 0)
        def _():
            dg_ref[...] += part

    blk = pl.BlockSpec((tr, d), lambda i: (i, 0))
    vec = pl.BlockSpec((1, d), lambda i: (0, 0))
    return pl.pallas_call(
        body, name=name, grid=(rows // tr,), in_specs=[blk, blk, blk, vec] + ([] if after is None else [ANY]),
        out_specs=[blk, blk, vec],
        out_shape=[jax.ShapeDtypeStruct((rows, d), F32), jax.ShapeDtypeStruct((rows, d), BF16),
                   jax.ShapeDtypeStruct((1, d), F32)],
        compiler_params=_params(("arbitrary",)),
    )(x, dh, dres, g, *([] if after is None else [after]))


def _rms_gain_grad(name, x, dh):
    rows, d = x.shape

    def body(x_ref, dh_ref, dg_ref):
        xv = x_ref[...]
        r = lax.rsqrt(jnp.mean(xv * xv, axis=-1, keepdims=True) + EPS)
        dg_ref[...] = jnp.sum(dh_ref[...] * xv * r, axis=0, keepdims=True)

    return pl.pallas_call(
        body, name=name, out_shape=jax.ShapeDtypeStruct((1, d), F32), compiler_params=_params(None),
    )(x, dh)


SB_T = 256
SB_HEADS = 4


LOG2E = 1.4426950408889634


def _sb_scores(q, kblk):
    z2 = _dot(q, kblk, NT) * (ATT_SCALE * LOG2E)
    e = jnp.exp2(-jnp.abs(z2))
    l1 = jnp.minimum(-z2, 0.0) - jnp.log2(1.0 + e)
    lb = l1 + z2
    return z2, e, lb, l1


def _sb_fwd(name, proj, after=None):
    s_len = proj.shape[0]
    t = SB_T
    nq = s_len // t

    def body(q_ref, k_ref, v_ref, *refs):
        o_ref = refs[-1]
        i = pl.program_id(1)
        row = lax.broadcasted_iota(jnp.int32, (t, t), 0)
        col = lax.broadcasted_iota(jnp.int32, (t, t), 1)
        causal = col < row
        after_mat = (row > col).astype(BF16)
        heads = [slice(hh * HEAD_DIM, (hh + 1) * HEAD_DIM) for hh in range(SB_HEADS)]
        q = [q_ref[:, sl].astype(BF16) for sl in heads]

        def tile(kb, state, masked):
            start = pl.multiple_of(kb * t, t)
            out = []
            for hh, sl in enumerate(heads):
                carry, acc = state[hh]
                kblk = k_ref[pl.ds(start, t), sl].astype(BF16)
                vblk = v_ref[pl.ds(start, t), sl].astype(BF16)
                _, _, lb, l1 = _sb_scores(q[hh], kblk)
                if masked:
                    l1 = jnp.where(causal, l1, 0.0)
                hi, lo = _split_bf16(l1)
                after = _dot(hi, after_mat, NN) + _dot(lo, after_mat, NN) + carry
                a = jnp.exp2(lb + after)
                if masked:
                    a = jnp.where(causal, a, 0.0)
                acc = acc + _dot(a.astype(BF16), vblk, NN)
                carry = carry + jnp.sum(l1, axis=-1, keepdims=True)
                out.append((carry, acc))
            return tuple(out)

        zero = (jnp.zeros((t, 1), F32), jnp.zeros((t, HEAD_DIM), F32))
        state = tile(i, (zero,) * SB_HEADS, True)
        state = lax.fori_loop(0, i, lambda n, st: tile(i - 1 - n, st, False), state)
        for hh, sl in enumerate(heads):
            o_ref[:, sl] = state[hh][1]

    cb = SB_HEADS * HEAD_DIM
    return pl.pallas_call(
        body, name=name, grid=(HEADS // SB_HEADS, nq),
        in_specs=[pl.BlockSpec((t, cb), lambda h, i: (i, OFF_QB // cb + h)),
                  pl.BlockSpec((s_len, cb), lambda h, i: (0, OFF_KB // cb + h)),
                  pl.BlockSpec((s_len, cb), lambda h, i: (0, OFF_VB // cb + h))] + ([] if after is None else [ANY]),
        out_specs=pl.BlockSpec((t, cb), lambda h, i: (i, h)),
        out_shape=jax.ShapeDtypeStruct((s_len, D_B), F32),
        compiler_params=_params(("parallel", "arbitrary")),
    )(proj, proj, proj, *([] if after is None else [after]))


def _sb_bwd(name, proj, dy, after=None):
    s_len = proj.shape[0]
    t = SB_T
    nq = s_len // t

    def body(q_ref, k_ref, v_ref, z_ref, dy_ref, *refs):
        dq_ref, dk_ref, dv_ref, a_ref, s_ref = refs[-5:]
        i = pl.program_id(1)

        @pl.when(i == 0)
        def _():
            dk_ref[...] = jnp.zeros_like(dk_ref)
            dv_ref[...] = jnp.zeros_like(dv_ref)

        heads = [slice(hh * HEAD_DIM, (hh + 1) * HEAD_DIM) for hh in range(SB_HEADS)]
        q = [q_ref[:, sl].astype(BF16) for sl in heads]
        silu_z, _ = _silu_and_grad(z_ref[...])
        do_all = dy_ref[...] * silu_z
        do_b = [do_all[:, sl].astype(BF16) for sl in heads]
        row = lax.broadcasted_iota(jnp.int32, (t, t), 0)
        col = lax.broadcasted_iota(jnp.int32, (t, t), 1)
        causal = col < row
        after_mat = (row > col).astype(BF16)
        before_mat = (row < col).astype(BF16)

        def weights(kb, carries, masked):
            start = pl.multiple_of(kb * t, t)
            out = []
            for hh, sl in enumerate(heads):
                kblk = k_ref[pl.ds(start, t), sl].astype(BF16)
                z, _, lb, l1 = _sb_scores(q[hh], kblk)
                if masked:
                    l1 = jnp.where(causal, l1, 0.0)
                hi, lo = _split_bf16(l1)
                after = _dot(hi, after_mat, NN) + _dot(lo, after_mat, NN) + carries[hh]
                a = jnp.exp2(lb + after)
                if masked:
                    a = jnp.where(causal, a, 0.0)
                a_ref[hh, kb] = a
                s_ref[hh, kb] = z
                out.append(carries[hh] + jnp.sum(l1, axis=-1, keepdims=True))
            return tuple(out)

        carries = weights(i, (jnp.zeros((t, 1), F32),) * SB_HEADS, True)
        lax.fori_loop(0, i, lambda n, c: weights(i - 1 - n, c, False), carries)

        def grads(kb, state, masked):
            start = pl.multiple_of(kb * t, t)
            out = []
            for hh, sl in enumerate(heads):
                carry, dq = state[hh]
                kblk = k_ref[pl.ds(start, t), sl].astype(BF16)
                vblk = v_ref[pl.ds(start, t), sl].astype(BF16)
                a = a_ref[hh, kb]
                z = s_ref[hh, kb]
                g = _dot(do_b[hh], vblk, NT) * a
                ghi, glo = _split_bf16(g)
                prefix = _dot(ghi, before_mat, NN) + _dot(glo, before_mat, NN) + carry
                e = jnp.exp2(-jnp.abs(z))
                inv = 1.0 / (1.0 + e)
                pos = z >= 0.0
                beta = jnp.where(pos, inv, e * inv)
                one_m_beta = jnp.where(pos, e * inv, inv)
                dz = (g * one_m_beta - prefix * beta) * ATT_SCALE
                if masked:
                    dz = jnp.where(causal, dz, 0.0)
                dz_b = dz.astype(BF16)
                dq = dq + _dot(dz_b, kblk, NN)
                dk_ref[pl.ds(start, t), sl] += _dot(dz_b, q[hh], TN)
                dv_ref[pl.ds(start, t), sl] += _dot(a.astype(BF16), do_b[hh], TN)
                out.append((carry + jnp.sum(g, axis=-1, keepdims=True), dq))
            return tuple(out)

        zero = (jnp.zeros((t, 1), F32), jnp.zeros((t, HEAD_DIM), F32))
        state = lax.fori_loop(0, i, lambda kb, st: grads(kb, st, False), (zero,) * SB_HEADS)
        state = grads(i, state, True)
        for hh, sl in enumerate(heads):
            dq_ref[:, sl] = state[hh][1]

    cb = SB_HEADS * HEAD_DIM
    qblk = lambda off: pl.BlockSpec((t, cb), lambda h, i: (i, off // cb + h))
    full = lambda off: pl.BlockSpec((s_len, cb), lambda h, i: (0, off // cb + h))
    out = jax.ShapeDtypeStruct((s_len, D_B), F32)
    return pl.pallas_call(
        body, name=name, grid=(HEADS // SB_HEADS, nq),
        in_specs=[qblk(OFF_QB), full(OFF_KB), full(OFF_VB), qblk(OFF_ZB), qblk(OFF_YB)]
        + ([] if after is None else [ANY]),
        out_specs=[qblk(0), full(0), full(0)],
        out_shape=[out, out, out],
        scratch_shapes=[pltpu.VMEM((SB_HEADS, nq, t, t), F32), pltpu.VMEM((SB_HEADS, nq, t, t), F32)],
        compiler_params=_params(("parallel", "arbitrary")),
    )(proj, proj, proj, proj, dy, *([] if after is None else [after]))


MEM_TQ = 512


def _qk_norm(x, g):
    r = lax.rsqrt(jnp.mean(x * x, axis=-1, keepdims=True) + EPS)
    xhat = x * r
    return xhat * g, xhat, r


def _qk_norm_bwd(dn, g, xhat, r):
    dxh = dn * g
    return r * (dxh - xhat * jnp.mean(dxh * xhat, axis=-1, keepdims=True))


def _mem_probs(q, mk, qg, kg):
    qn, qhat, rq = _qk_norm(q, qg)
    kn, khat, rk = _qk_norm(mk, kg)
    qn_b, kn_b = qn.astype(BF16), kn.astype(BF16)
    s = _dot(qn_b, kn_b, NT) * ATT_SCALE
    p = jnp.exp(s - jnp.max(s, axis=-1, keepdims=True))
    p = p / jnp.sum(p, axis=-1, keepdims=True)
    return p, qn_b, kn_b, qhat, rq, khat, rk


def _mem_fwd(name, proj, mem_kv, qg, kg):
    s_len = proj.shape[0]
    m_len = mem_kv.shape[0]
    tq = min(MEM_TQ, s_len)

    def body(q_ref, mk_ref, mv_ref, qg_ref, kg_ref, o_ref):
        p = _mem_probs(q_ref[...], mk_ref[...], qg_ref[...], kg_ref[...])[0]
        o_ref[...] = _dot(p.astype(BF16), mv_ref[...].astype(BF16), NN)

    cb = HEAD_DIM
    vec = pl.BlockSpec((1, cb), lambda h, i: (0, 0))
    return pl.pallas_call(
        body, name=name, grid=(HEADS, s_len // tq),
        in_specs=[pl.BlockSpec((tq, cb), lambda h, i: (i, OFF_QC // cb + h)),
                  pl.BlockSpec((m_len, cb), lambda h, i: (0, h)),
                  pl.BlockSpec((m_len, cb), lambda h, i: (0, HEADS + h)), vec, vec],
        out_specs=pl.BlockSpec((tq, cb), lambda h, i: (i, h)),
        out_shape=jax.ShapeDtypeStruct((s_len, D_C), F32),
        compiler_params=_params(("parallel", "parallel")),
    )(proj, mem_kv, mem_kv, qg, kg)


def _mem_bwd(name, proj, mem_kv, qg, kg, dy):
    s_len = proj.shape[0]
    m_len = mem_kv.shape[0]
    tq = min(MEM_TQ, s_len)

    def body(q_ref, mk_ref, mv_ref, qg_ref, kg_ref, z_ref, dy_ref, dq_ref, dmk_ref, dmv_ref, dqg_ref, dkg_ref):
        h, i = pl.program_id(0), pl.program_id(1)

        @pl.when(i == 0)
        def _():
            dmk_ref[...] = jnp.zeros_like(dmk_ref)
            dmv_ref[...] = jnp.zeros_like(dmv_ref)

        @pl.when((i == 0) & (h == 0))
        def _():
            dqg_ref[...] = jnp.zeros_like(dqg_ref)
            dkg_ref[...] = jnp.zeros_like(dkg_ref)

        qg, kg = qg_ref[...], kg_ref[...]
        p, qn_b, kn_b, qhat, rq, khat, rk = _mem_probs(q_ref[...], mk_ref[...], qg, kg)
        silu_z, _ = _silu_and_grad(z_ref[...])
        do_b = (dy_ref[...] * silu_z).astype(BF16)
        dmv_ref[...] += _dot(p.astype(BF16), do_b, TN)
        dp = _dot(do_b, mv_ref[...].astype(BF16), NT)
        ds = (p * (dp - jnp.sum(dp * p, axis=-1, keepdims=True)) * ATT_SCALE).astype(BF16)
        dqn = _dot(ds, kn_b, NN)
        dkn = _dot(ds, qn_b, TN)
        dq_ref[...] = _qk_norm_bwd(dqn, qg, qhat, rq)
        dmk_ref[...] += _qk_norm_bwd(dkn, kg, khat, rk)
        dqg_ref[...] += jnp.sum(dqn * qhat, axis=0, keepdims=True)
        dkg_ref[...] += jnp.sum(dkn * khat, axis=0, keepdims=True)

    cb = HEAD_DIM
    vec = pl.BlockSpec((1, cb), lambda h, i: (0, 0))
    qblk = lambda off: pl.BlockSpec((tq, cb), lambda h, i: (i, off // cb + h))
    memblk = lambda off: pl.BlockSpec((m_len, cb), lambda h, i: (0, off + h))
    return pl.pallas_call(
        body, name=name, grid=(HEADS, s_len // tq),
        in_specs=[qblk(OFF_QC), memblk(0), memblk(HEADS), vec, vec, qblk(OFF_ZC), qblk(OFF_YC)],
        out_specs=[qblk(0), memblk(0), memblk(0), vec, vec],
        out_shape=[jax.ShapeDtypeStruct((s_len, D_C), F32), jax.ShapeDtypeStruct((m_len, D_C), F32),
                   jax.ShapeDtypeStruct((m_len, D_C), F32), jax.ShapeDtypeStruct((1, cb), F32),
                   jax.ShapeDtypeStruct((1, cb), F32)],
        compiler_params=_params(("arbitrary", "arbitrary")),
    )(proj, mem_kv, mem_kv, qg, kg, proj, dy)


def _sgu_common(u_ref, v_ref, lng_ref, lnb_ref, w_ref, bias_ref):
    ug = _gelu(u_ref[...])
    vg = _gelu(v_ref[...])
    mu = jnp.mean(vg, axis=-1, keepdims=True)
    xc = vg - mu
    rstd = lax.rsqrt(jnp.mean(xc * xc, axis=-1, keepdims=True) + EPS)
    xhat = xc * rstd
    vn = xhat * lng_ref[...] + lnb_ref[...]
    vn_b = vn.astype(BF16)
    row = lax.broadcasted_iota(jnp.int32, (CHUNK, CHUNK), 0)
    col = lax.broadcasted_iota(jnp.int32, (CHUNK, CHUNK), 1)
    tril = row >= col
    mixed = []
    for g in range(A_GROUPS):
        w = jnp.where(tril, w_ref[g], 0.0).astype(BF16)
        sl = slice(g * CHUNK, (g + 1) * CHUNK)
        mixed.append(_dot(w, vn_b[:, sl], NN) + bias_ref[:, sl])
    return ug, xhat, rstd, vn_b, mixed, tril


def _gate_fwd(name, proj, o_b, o_c, lng, lnb, w_s, bias):
    s_len = proj.shape[0]

    def body(u_ref, v_ref, za_ref, zb_ref, zc_ref, ob_ref, oc_ref, lng_ref, lnb_ref, w_ref, bias_ref, y_ref, yt_ref):
        ug, _, _, _, mixed, _ = _sgu_common(u_ref, v_ref, lng_ref, lnb_ref, w_ref, bias_ref)
        sza, _ = _silu_and_grad(za_ref[...])
        gate = ug * sza

        def put(off, width, val):
            y_ref[:, off:off + width] = val.astype(BF16)
            yt_ref[off:off + width, :] = val.T.astype(BF16)

        for g in range(A_GROUPS):
            sl = slice(g * CHUNK, (g + 1) * CHUNK)
            put(g * CHUNK, CHUNK, gate[:, sl] * mixed[g])
        szb, _ = _silu_and_grad(zb_ref[...])
        put(OFF_YB, D_B, ob_ref[...] * szb)
        szc, _ = _silu_and_grad(zc_ref[...])
        put(OFF_YC, D_C, oc_ref[...] * szc)

    wide = lambda off: pl.BlockSpec((CHUNK, D_A), lambda i: (i, off // D_A))
    narrow = lambda off: pl.BlockSpec((CHUNK, D_B), lambda i: (i, off // D_B))
    vec = pl.BlockSpec((1, D_A), lambda i: (0, 0))
    return pl.pallas_call(
        body, name=name, grid=(s_len // CHUNK,),
        in_specs=[wide(OFF_U), wide(OFF_V), wide(OFF_ZA), narrow(OFF_ZB), narrow(OFF_ZC), narrow(0), narrow(0), vec, vec,
                  pl.BlockSpec((A_GROUPS, CHUNK, CHUNK), lambda i: (0, 0, 0)),
                  pl.BlockSpec((CHUNK, D_A), lambda i: (0, 0))],
        out_specs=[pl.BlockSpec((CHUNK, D_MODEL), lambda i: (i, 0)), pl.BlockSpec((D_MODEL, CHUNK), lambda i: (0, i))],
        out_shape=[jax.ShapeDtypeStruct((s_len, D_MODEL), BF16), jax.ShapeDtypeStruct((D_MODEL, s_len), BF16)],
        compiler_params=_params(("parallel",)),
    )(proj, proj, proj, proj, proj, o_b, o_c, lng, lnb, w_s, bias)


def _gate_bwd(name, proj, dy, o_b, o_c, dqkv, dq_c, lng, lnb, w_s, w_s_t, bias):
    s_len = proj.shape[0]
    n = s_len // CHUNK
    dq_b, dk_b, dv_b = dqkv

    def body(u_ref, v_ref, za_ref, zb_ref, zc_ref, dya_ref, dyb_ref, dyc_ref, ob_ref, oc_ref, dq_ref, dk_ref, dv_ref,
             dqc_ref, lng_ref, lnb_ref, w_ref, wt_ref, bias_ref, dp_ref, dw_ref, dsb_ref, dlng_ref, dlnb_ref, dbias_ref):
        i = pl.program_id(0)

        @pl.when(i == 0)
        def _():
            dw_ref[...] = jnp.zeros_like(dw_ref)
            dbias_ref[...] = jnp.zeros_like(dbias_ref)
            dlng_ref[...] = jnp.zeros_like(dlng_ref)
            dlnb_ref[...] = jnp.zeros_like(dlnb_ref)

        ug, xhat, rstd, vn_b, mixed, tril = _sgu_common(u_ref, v_ref, lng_ref, lnb_ref, w_ref, bias_ref)
        za = za_ref[...]
        sza, dsza = _silu_and_grad(za)
        dya = dya_ref[...]
        mixed_all = jnp.concatenate(mixed, axis=-1)
        d_mixed = dya * ug * sza
        dp_ref[:, OFF_U:OFF_U + D_A] = (dya * mixed_all * sza * _gelu_grad(u_ref[...])).astype(BF16)
        dp_ref[:, OFF_ZA:OFF_ZA + D_A] = (dya * ug * mixed_all * dsza).astype(BF16)
        dbias_ref[...] += d_mixed
        dm_b = d_mixed.astype(BF16)
        triu = lax.broadcasted_iota(jnp.int32, (CHUNK, CHUNK), 0) <= lax.broadcasted_iota(jnp.int32, (CHUNK, CHUNK), 1)
        d_vn = []
        for g in range(A_GROUPS):
            sl = slice(g * CHUNK, (g + 1) * CHUNK)
            wt = jnp.where(triu, wt_ref[g], 0.0).astype(BF16)
            d_vn.append(_dot(wt, dm_b[:, sl], NN))
            dw_ref[g] += jnp.where(tril, _dot(dm_b[:, sl], vn_b[:, sl], NT), 0.0)
        d_vn = jnp.concatenate(d_vn, axis=-1)
        dlng_ref[...] += jnp.sum(d_vn * xhat, axis=0, keepdims=True)
        dlnb_ref[...] += jnp.sum(d_vn, axis=0, keepdims=True)
        dxh = d_vn * lng_ref[...]
        d_vg = rstd * (dxh - jnp.mean(dxh, axis=-1, keepdims=True)
                       - xhat * jnp.mean(dxh * xhat, axis=-1, keepdims=True))
        dp_ref[:, OFF_V:OFF_V + D_A] = (d_vg * _gelu_grad(v_ref[...])).astype(BF16)
        dp_ref[:, OFF_QB:OFF_QB + D_B] = dq_ref[...].astype(BF16)
        dp_ref[:, OFF_KB:OFF_KB + D_B] = dk_ref[...].astype(BF16)
        dp_ref[:, OFF_VB:OFF_VB + D_B] = dv_ref[...].astype(BF16)
        _, dszb = _silu_and_grad(zb_ref[...])
        dp_ref[:, OFF_ZB:OFF_ZB + D_B] = (dyb_ref[...] * ob_ref[...] * dszb).astype(BF16)
        dp_ref[:, OFF_QC:OFF_QC + D_C] = dqc_ref[...].astype(BF16)
        _, dszc = _silu_and_grad(zc_ref[...])
        dp_ref[:, OFF_ZC:OFF_ZC + D_C] = (dyc_ref[...] * oc_ref[...] * dszc).astype(BF16)

        @pl.when(i == n - 1)
        def _():
            ch = lax.broadcasted_iota(jnp.int32, (D_A, CHUNK), 0)
            gcol = lax.broadcasted_iota(jnp.int32, (D_A, CHUNK), 1)
            pick = (ch // (D_A // A_GROUPS) == gcol).astype(BF16)
            rest = dbias_ref[...]
            tot = jnp.zeros((CHUNK, CHUNK), F32)
            for _ in range(3):
                term = rest.astype(BF16)
                tot = tot + _dot(term, pick, NN)
                rest = rest - term.astype(F32)
            dsb_ref[...] = tot

    wide = lambda off: pl.BlockSpec((CHUNK, D_A), lambda i: (i, off // D_A))
    narrow = lambda off: pl.BlockSpec((CHUNK, D_B), lambda i: (i, off // D_B))
    vec = pl.BlockSpec((1, D_A), lambda i: (0, 0))
    wspec = pl.BlockSpec((A_GROUPS, CHUNK, CHUNK), lambda i: (0, 0, 0))
    bspec = pl.BlockSpec((CHUNK, D_A), lambda i: (0, 0))
    return pl.pallas_call(
        body, name=name, grid=(n,),
        in_specs=[wide(OFF_U), wide(OFF_V), wide(OFF_ZA), narrow(OFF_ZB), narrow(OFF_ZC),
                  wide(0), narrow(OFF_YB), narrow(OFF_YC), narrow(0), narrow(0), narrow(0), narrow(0), narrow(0),
                  narrow(0), vec, vec, wspec, wspec, bspec],
        out_specs=[pl.BlockSpec((CHUNK, IN_WIDTH), lambda i: (i, 0)), wspec,
                   pl.BlockSpec((CHUNK, CHUNK), lambda i: (0, 0)), vec, vec],
        out_shape=[jax.ShapeDtypeStruct((s_len, IN_WIDTH), BF16), jax.ShapeDtypeStruct((A_GROUPS, CHUNK, CHUNK), F32),
                   jax.ShapeDtypeStruct((CHUNK, CHUNK), F32), jax.ShapeDtypeStruct((1, D_A), F32),
                   jax.ShapeDtypeStruct((1, D_A), F32)],
        scratch_shapes=[pltpu.VMEM((CHUNK, D_A), F32)],
        compiler_params=_params(("arbitrary",)),
    )(proj, proj, proj, proj, proj, dy, dy, dy, o_b, o_c, dq_b, dk_b, dv_b, dq_c, lng, lnb, w_s, w_s_t, bias)


IN_SHARD = IN_WIDTH // N_CHIPS
ROW_SHARD = D_MODEL // N_CHIPS


def _bias_rows(sgu_b_l):
    return jnp.repeat(sgu_b_l.T, D_A // A_GROUPS, axis=1)


class _WholeWeights:
    def __init__(self, w_in_all, w_kv_all, w_out_all):
        self.weights = (w_in_all, w_kv_all, w_out_all)

    def w_in(self, stage, h, proj):
        return (self.weights[0], jnp.arange(N_CHIPS, dtype=jnp.int32), 0, N_CHIPS) if stage == 0 else None

    def rest_start(self, proj):
        return None

    def rest_finish(self, o_b):
        return self.weights[1], self.weights[2], None

    def before_out(self, y):
        return None


def _layer_fwd(l, x, mem, sm, hooks, target=None):
    s_len = x.shape[0]
    m_len = mem.shape[0]
    tm = min(1024, s_len)
    h, h_t = _rms_fwd(f"rms_fwd_{l}", x, sm["norm_g"][l][None], min(256, s_len), transposed=True)
    proj, stage = None, 0
    while (ready := hooks.w_in(stage, h, proj)) is not None:
        w_in_all, order, first, count = ready
        proj = _matmul(
            f"in_proj_{l}_{stage}", h, w_in_all, grid=(s_len // tm, count, 1), place=order, into=proj,
            a_spec=pl.BlockSpec((tm, D_MODEL), lambda i, j, k, p: (i, 0)),
            b_spec=pl.BlockSpec((None, D_MODEL, IN_SHARD), lambda i, j, k, p: (p[first + j], 0, 0)),
            o_spec=pl.BlockSpec((tm, IN_SHARD), lambda i, j, k, p: (i, p[first + j])),
            out_shape=jax.ShapeDtypeStruct((s_len, IN_WIDTH), F32), dims=NN)
        stage += 1
    o_b = _sb_fwd(f"sb_fwd_{l}", proj, hooks.rest_start(proj))
    w_kv_all, w_out_all, after = hooks.rest_finish(o_b)
    mem_h = _rms_fwd(f"mem_rms_fwd_{l}", mem, sm["mem_norm_g"][l][None], m_len, after)
    mem_kv = _matmul(
        f"mem_kv_{l}", mem_h, w_kv_all, grid=(1, 2, N_CHIPS),
        a_spec=pl.BlockSpec((m_len, ROW_SHARD), lambda i, j, k: (0, k)),
        b_spec=pl.BlockSpec((None, ROW_SHARD, D_C), lambda i, j, k: (k, 0, j)),
        o_spec=pl.BlockSpec((m_len, D_C), lambda i, j, k: (0, j)),
        out_shape=jax.ShapeDtypeStruct((m_len, 2 * D_C), F32), dims=NN)
    qg, kg = sm["q_norm_g"][l][None], sm["k_norm_g"][l][None]
    o_c = _mem_fwd(f"mem_fwd_{l}", proj, mem_kv, qg, kg)
    bias = _bias_rows(sm["sgu_b"][l])
    y, y_t = _gate_fwd(f"gate_fwd_{l}", proj, o_b, o_c, sm["sgu_ln_g"][l][None], sm["sgu_ln_b"][l][None],
                       sm["sgu_w"][l], bias)
    saved = dict(x=x, h_t=h_t, proj=proj, mem_h=mem_h, mem_kv=mem_kv, o_b=o_b, o_c=o_c, y_t=y_t, bias=bias,
                 weights=(w_in_all, w_kv_all, w_out_all))
    if target is not None:
        return _out_proj_loss(f"out_proj_{l}", y, w_out_all, x, target, tm), saved
    tn_o = 512
    x_next = _matmul(
        f"out_proj_{l}", y, w_out_all, grid=(s_len // tm, D_MODEL // tn_o, 1),
        a_spec=pl.BlockSpec((tm, D_MODEL), lambda i, j, k: (i, 0)),
        b_spec=pl.BlockSpec((N_CHIPS, ROW_SHARD, tn_o), lambda i, j, k: (0, 0, j)),
        o_spec=pl.BlockSpec((tm, tn_o), lambda i, j, k: (i, j)),
        out_shape=jax.ShapeDtypeStruct((s_len, D_MODEL), F32), dims=NN,
        res=x, res_spec=pl.BlockSpec((tm, tn_o), lambda i, j, k: (i, j)), after=hooks.before_out(y))
    return x_next, saved


def _out_proj_loss(name, y, w_out_all, x, target, tm):
    s_len, d = x.shape
    tn = 512
    n_i = s_len // tm

    n_j = d // tn

    def body(y_ref, w_ref, x_ref, t_ref, dx_ref, dxb_ref, loss_ref, acc_ref):
        i, j = pl.program_id(0), pl.program_id(1)
        out = _dot(y_ref[...], w_ref[...].reshape(-1, tn), NN) + x_ref[...]
        e = out - t_ref[...]
        dx = e * (1.0 / d)
        dx_ref[...] = dx
        dxb_ref[...] = dx.astype(BF16)
        part = jnp.sum(e * e, axis=0, keepdims=True)

        @pl.when((i == 0) & (j == 0))
        def _():
            acc_ref[...] = part

        @pl.when((i > 0) | (j > 0))
        def _():
            acc_ref[...] += part

        @pl.when((i == n_i - 1) & (j == n_j - 1))
        def _():
            loss_ref[...] = jnp.sum(acc_ref[...], axis=-1, keepdims=True) * (0.5 / d)

    blk = pl.BlockSpec((tm, tn), lambda i, j: (i, j))
    return pl.pallas_call(
        body, name=name, grid=(n_i, n_j),
        in_specs=[pl.BlockSpec((tm, d), lambda i, j: (i, 0)), pl.BlockSpec((N_CHIPS, ROW_SHARD, tn), lambda i, j: (0, 0, j)),
                  blk, blk],
        out_specs=[blk, blk, pl.BlockSpec((1, 1), lambda i, j: (0, 0))],
        out_shape=[jax.ShapeDtypeStruct((s_len, d), F32), jax.ShapeDtypeStruct((s_len, d), BF16),
                   jax.ShapeDtypeStruct((1, 1), F32)],
        scratch_shapes=[pltpu.VMEM((1, tn), F32)],
        compiler_params=_params(("arbitrary", "arbitrary")),
    )(y, w_out_all, x, target)


class _NoExchange:
    def __init__(self):
        self.gave, self.kept = {}, {}

    def start(self, l, group, gives):
        self.gave[l, group] = gives
        return None

    def landed(self, l, group, after):
        return [jnp.zeros_like(g) for g in self.gave[l, group]]

    def send(self, l, group, parts):
        self.kept[l, group] = parts
        return None


def _layer_bwd(l, dxo, dxo_b, mem, sm, saved, place, exchange):
    s_len = dxo.shape[0]
    m_len = mem.shape[0]
    proj, y_t, h_t, mem_h, mem_kv = saved["proj"], saved["y_t"], saved["h_t"], saved["mem_h"], saved["mem_kv"]
    w_in_all, w_kv_all, w_out_all = saved["weights"]
    tm = min(1024, s_len)
    tn = 768
    per = IN_SHARD // tn
    half_rows = ROW_SHARD // 2

    def halves(make):
        give = lambda: make("give", lambda p: 1 - p[1], None, F32)
        keep = lambda theirs: make("keep", lambda p: p[1], theirs, BF16)
        return give, keep

    def grad_out(tag, half, theirs, dtype):
        o_spec = pl.BlockSpec((None, half_rows, 1024), lambda i, j, k, p: (i, 0, j))
        return _matmul(
            f"d_w_out_{l}_{tag}", y_t, dxo_b, grid=(N_CHIPS, D_MODEL // 1024, 1), place=place,
            a_spec=pl.BlockSpec((half_rows, s_len), lambda i, j, k, p: (2 * i + half(p), 0)),
            b_spec=pl.BlockSpec((s_len, 1024), lambda i, j, k, p: (0, j)), o_spec=o_spec,
            out_shape=jax.ShapeDtypeStruct((N_CHIPS, half_rows, D_MODEL), dtype), dims=NN,
            res=theirs, res_spec=o_spec)

    def grad_in(tag, half, theirs, dtype):
        o_spec = pl.BlockSpec((None, D_MODEL // 2, tn), lambda i, j, k, p: (j // per, 0, j % per))
        return _matmul(
            f"d_w_in_{l}_{tag}", h_t, dproj, grid=(1, IN_WIDTH // tn, 1), place=place,
            a_spec=pl.BlockSpec((D_MODEL // 2, s_len), lambda i, j, k, p: (half(p), 0)),
            b_spec=pl.BlockSpec((s_len, tn), lambda i, j, k, p: (0, j)), o_spec=o_spec,
            out_shape=jax.ShapeDtypeStruct((N_CHIPS, D_MODEL // 2, IN_SHARD), dtype), dims=NN,
            res=theirs, res_spec=o_spec)

    def grad_kv(tag, half, theirs, dtype):
        o_spec = pl.BlockSpec((None, half_rows, 2 * D_C), lambda i, j, k, p: (i, 0, 0))
        return _matmul(
            f"d_w_kv_{l}_{tag}", mem_h, dkv_b, grid=(N_CHIPS, 1, 1), place=place,
            a_spec=pl.BlockSpec((m_len, half_rows), lambda i, j, k, p: (0, 2 * i + half(p))),
            b_spec=pl.BlockSpec((m_len, 2 * D_C), lambda i, j, k, p: (0, 0)), o_spec=o_spec,
            out_shape=jax.ShapeDtypeStruct((N_CHIPS, half_rows, 2 * D_C), dtype), dims=TN,
            res=theirs, res_spec=o_spec)

    give_out, keep_out = halves(grad_out)
    token = exchange.start(l, "out", [give_out()])
    dy = _matmul(
        f"d_y_{l}", dxo_b, w_out_all, grid=(s_len // tm, N_CHIPS, 1),
        a_spec=pl.BlockSpec((tm, D_MODEL), lambda i, j, k: (i, 0)),
        b_spec=pl.BlockSpec((None, ROW_SHARD, D_MODEL), lambda i, j, k: (j, 0, 0)),
        o_spec=pl.BlockSpec((tm, ROW_SHARD), lambda i, j, k: (i, j)),
        out_shape=jax.ShapeDtypeStruct((s_len, D_MODEL), F32), dims=NT, after=token)
    (theirs_out,) = exchange.landed(l, "out", dy)
    token = exchange.send(l, "out", [keep_out(theirs_out)])
    qg, kg = sm["q_norm_g"][l][None], sm["k_norm_g"][l][None]
    dqkv = _sb_bwd(f"sb_bwd_{l}", proj, dy, token)
    dq_c, dmk, dmv, dqg, dkg = _mem_bwd(f"mem_bwd_{l}", proj, mem_kv, qg, kg, dy)
    w_s = sm["sgu_w"][l]
    dproj, dws, dbias, dlng, dlnb = _gate_bwd(
        f"gate_bwd_{l}", proj, dy, saved["o_b"], saved["o_c"], dqkv, dq_c, sm["sgu_ln_g"][l][None],
        sm["sgu_ln_b"][l][None], w_s, jnp.swapaxes(w_s, 1, 2), saved["bias"])
    dkv_b = jnp.concatenate([dmk, dmv], axis=1).astype(BF16)
    give_in, keep_in = halves(grad_in)
    give_kv, keep_kv = halves(grad_kv)
    token = exchange.start(l, "in", [give_in(), give_kv()])
    dh = _matmul(
        f"d_h_{l}", dproj, w_in_all, grid=(s_len // tm, D_MODEL // 512, 1),
        a_spec=pl.BlockSpec((tm, IN_WIDTH), lambda i, j, k: (i, 0)),
        b_spec=pl.BlockSpec((N_CHIPS, 512, IN_SHARD), lambda i, j, k: (0, j, 0)),
        o_spec=pl.BlockSpec((tm, 512), lambda i, j, k: (i, j)),
        out_shape=jax.ShapeDtypeStruct((s_len, D_MODEL), F32), dims=NT, after=token, vmem_mb=56)
    theirs_in, theirs_kv = exchange.landed(l, "in", dh)
    token = exchange.send(l, "in", [keep_in(theirs_in), keep_kv(theirs_kv)])
    dx, dx_b, dng = _rms_bwd(f"rms_bwd_{l}", saved["x"], dh, dxo, sm["norm_g"][l][None], min(256, s_len), token)
    d_mem_h = _matmul(
        f"d_mem_h_{l}", dkv_b, w_kv_all, grid=(1, N_CHIPS, 1),
        a_spec=pl.BlockSpec((m_len, 2 * D_C), lambda i, j, k: (0, 0)),
        b_spec=pl.BlockSpec((None, ROW_SHARD, 2 * D_C), lambda i, j, k: (j, 0, 0)),
        o_spec=pl.BlockSpec((m_len, ROW_SHARD), lambda i, j, k: (0, j)),
        out_shape=jax.ShapeDtypeStruct((m_len, D_MODEL), F32), dims=NT)
    dmng = _rms_gain_grad(f"mem_rms_bwd_{l}", mem, d_mem_h)
    dsgu_b = dbias[:, :A_GROUPS].T
    small = dict(norm_g=dng[0], sgu_ln_g=dlng[0], sgu_ln_b=dlnb[0], sgu_w=dws, sgu_b=dsgu_b, mem_norm_g=dmng[0],
                 q_norm_g=dqg[0], k_norm_g=dkg[0])
    return dx, dx_b, small


SMALL_NAMES = ("norm_g", "sgu_ln_g", "sgu_ln_b", "sgu_w", "sgu_b", "mem_norm_g", "q_norm_g", "k_norm_g")


def _local_step(x, mem, target, sm, w_all):
    saved = []
    cur = x
    for l in range(DEPTH):
        cur, sv = _layer_fwd(l, cur, mem, sm, _WholeWeights(*w_all[l]), target if l == DEPTH - 1 else None)
        saved.append(sv)
    dxo, dxo_b, loss = cur
    small = [None] * DEPTH
    exchange = _NoExchange()
    place = jnp.zeros((2,), jnp.int32)
    for l in reversed(range(DEPTH)):
        dxo, dxo_b, small[l] = _layer_bwd(l, dxo, dxo_b, mem, sm, saved[l], place, exchange)
    small = {k: jnp.stack([small[l][k] for l in range(DEPTH)]) for k in SMALL_NAMES}
    return loss, dxo, small, exchange.gave, exchange.kept


def _place():
    x, y, c = lax.axis_index("x"), lax.axis_index("y"), lax.axis_index("c")
    return x, y, c


def _other_chips(x, y):
    return [(1 - x, y, 2 * (1 - x) + y), (x, 1 - y, 2 * x + 1 - y), (1 - x, 1 - y, 2 * (1 - x) + 1 - y)]


D2D_CHUNKS = 8


def _place_index():
    return jnp.stack([2 * lax.axis_index("x") + lax.axis_index("y"), lax.axis_index("c")]).astype(jnp.int32)


def _cast_into_slot(name, w, l, place):
    _, rows, cols = w.shape
    tr = min(256, rows)

    def body(p_ref, w_ref, o_ref):
        o_ref[...] = w_ref[...].astype(BF16)

    return pl.pallas_call(
        body, name=name,
        grid_spec=pltpu.PrefetchScalarGridSpec(
            num_scalar_prefetch=1, grid=(rows // tr,),
            in_specs=[pl.BlockSpec((None, tr, cols), lambda i, p: (l, i, 0))],
            out_specs=pl.BlockSpec((None, tr, cols), lambda i, p: (p[0], i, 0))),
        out_shape=jax.ShapeDtypeStruct((N_CHIPS, rows, cols), BF16),
        compiler_params=_params(("parallel",)),
    )(place, w)


HBM = pl.BlockSpec(memory_space=pltpu.HBM)
SEM = pl.BlockSpec(memory_space=pltpu.SEMAPHORE)
DATAFLOW = pltpu.SideEffectType.DATAFLOW_SIDE_EFFECTING


def _in_hbm(a):
    return pltpu.with_memory_space_constraint(a, pltpu.HBM)


ALL_PEERS = (0, 1, 2)
NEIGHBOURS = (0, 1)
DIAGONAL = (2,)


def _chip_copies_start(name, srcs, lands, make_copy, after=None, peers=ALL_PEERS):
    n_t = len(srcs)
    in_place = lands is None
    n_after = 0 if after is None else 1

    def body(*refs):
        src = refs[:n_t]
        k = (n_t if in_place else 2 * n_t) + n_after
        send_sems, recv_sems = refs[k], refs[k + 1]
        land = refs[k + 2:k + 2 + n_t] if in_place else refs[k + 2 + n_t:k + 2 + 2 * n_t]
        token = refs[-1]
        x, y, c = _place()
        me = 2 * x + y
        others = _other_chips(x, y)
        for t in range(n_t):
            for px, py, pk in [others[p] for p in peers]:
                s, d = make_copy(src[t], land[t], me, pk, c)
                pltpu.make_async_remote_copy(
                    src_ref=s, dst_ref=d, send_sem=send_sems.at[t], recv_sem=recv_sems.at[t],
                    device_id=(px, py, c), device_id_type=MESH).start()
        token[...] = jnp.zeros_like(token)

    bufs = list(srcs) if in_place else list(srcs) + list(lands)
    outs = pl.pallas_call(
        body, name=name,
        in_specs=[HBM] * len(bufs) + [ANY] * n_after,
        out_specs=[SEM, SEM] + [HBM] * len(bufs) + [pl.BlockSpec(memory_space=pltpu.VMEM)],
        out_shape=[pltpu.SemaphoreType.DMA((n_t,)), pltpu.SemaphoreType.DMA((n_t,))]
        + [pltpu.HBM(b.shape, b.dtype) for b in bufs] + [jax.ShapeDtypeStruct((8, 128), F32)],
        input_output_aliases={i: 2 + i for i in range(len(bufs))},
        compiler_params=pltpu.CompilerParams(has_side_effects=DATAFLOW),
    )(*[_in_hbm(b) for b in bufs], *([] if after is None else [after]))
    return outs[0], outs[1], list(outs[2:2 + len(bufs)]), outs[-1]


def _chip_copies_wait(name, send_sems, recv_sems, bufs, sent, landed, after):
    n_b = len(bufs)

    def body(*refs):
        buf = refs[:n_b]
        send_ref, recv_ref = refs[n_b], refs[n_b + 1]
        x, y, c = _place()
        for t, (s, d) in enumerate(zip(sent(buf), landed(buf))):
            out = pltpu.make_async_remote_copy(src_ref=s, dst_ref=s, send_sem=send_ref.at[t], recv_sem=recv_ref.at[t],
                                               device_id=(x, y, c), device_id_type=MESH)
            out.wait_send()
            arrived = pltpu.make_async_remote_copy(src_ref=d, dst_ref=d, send_sem=send_ref.at[t],
                                                   recv_sem=recv_ref.at[t], device_id=(x, y, c), device_id_type=MESH)
            arrived.wait_recv()

    after = list(after) if isinstance(after, (list, tuple)) else [after]
    return pl.pallas_call(
        body, name=name,
        in_specs=[HBM] * n_b + [SEM, SEM] + [ANY] * len(after), out_specs=[HBM] * n_b,
        out_shape=[pltpu.HBM(b.shape, b.dtype) for b in bufs],
        input_output_aliases={i: i for i in range(n_b)},
        compiler_params=pltpu.CompilerParams(has_side_effects=DATAFLOW),
    )(*bufs, send_sems, recv_sems, *after)


def _gather_start(name, bufs, after=None, peers=ALL_PEERS):
    def make_copy(src, land, me, pk, c):
        hr = src.shape[1] // 2
        return src.at[me, pl.ds(c * hr, hr)], land.at[me, pl.ds(c * hr, hr)]

    return _chip_copies_start(name, bufs, None, make_copy, after, peers)


def _gather_wait(name, send_sems, recv_sems, bufs, after, peers=ALL_PEERS):
    def half_shards(buf):
        return [b.at[pl.ds(0, len(peers)), pl.ds(0, b.shape[1] // 2)] for b in buf]

    return _chip_copies_wait(name, send_sems, recv_sems, bufs, half_shards, half_shards, after)


def _gather_forward_start(name, bufs, peers=ALL_PEERS):
    n_t = len(bufs)

    def body(*refs):
        mine = refs[:n_t]
        send_sems, recv_sems = refs[n_t], refs[n_t + 1]
        buf = refs[n_t + 2:2 * n_t + 2]
        token = refs[-1]
        x, y, c = _place()
        others = _other_chips(x, y)
        for q in range(D2D_CHUNKS):
            for t in range(n_t):
                hr = mine[t].shape[1] // 2
                cr = hr // D2D_CHUNKS
                rows = pl.ds(c * hr + q * cr, cr)
                for _, _, pk in [others[p] for p in peers]:
                    pltpu.make_async_remote_copy(
                        src_ref=mine[t].at[pk, rows], dst_ref=buf[t].at[pk, rows], send_sem=send_sems.at[t],
                        recv_sem=recv_sems.at[t], device_id=(x, y, 1 - c), device_id_type=MESH).start()
        token[...] = jnp.zeros_like(token)

    outs = pl.pallas_call(
        body, name=name,
        in_specs=[HBM] * n_t,
        out_specs=[SEM, SEM] + [HBM] * n_t + [pl.BlockSpec(memory_space=pltpu.VMEM)],
        out_shape=[pltpu.SemaphoreType.DMA((n_t,)), pltpu.SemaphoreType.DMA((n_t,))]
        + [pltpu.HBM(b.shape, b.dtype) for b in bufs] + [jax.ShapeDtypeStruct((8, 128), F32)],
        input_output_aliases={i: 2 + i for i in range(n_t)},
        compiler_params=pltpu.CompilerParams(has_side_effects=DATAFLOW),
    )(*[_in_hbm(b) for b in bufs])
    return outs[0], outs[1], list(outs[2:2 + n_t]), outs[-1]


def _core_exchange_start(name, grads):
    n_t = len(grads)
    lands = [lax.empty(g.shape, g.dtype) for g in grads]

    def body(*refs):
        src = refs[:n_t]
        send_sems, recv_sems = refs[2 * n_t], refs[2 * n_t + 1]
        land = refs[2 * n_t + 2 + n_t:2 * n_t + 2 + 2 * n_t]
        token = refs[-1]
        x, y, c = _place()
        for q in range(D2D_CHUNKS):
            for t in range(n_t):
                cr = src[t].shape[1] // D2D_CHUNKS
                rows = pl.ds(q * cr, cr)
                pltpu.make_async_remote_copy(
                    src_ref=src[t].at[:, rows], dst_ref=land[t].at[:, rows], send_sem=send_sems.at[t],
                    recv_sem=recv_sems.at[t], device_id=(x, y, 1 - c), device_id_type=MESH).start()
        token[...] = jnp.zeros_like(token)

    bufs = list(grads) + lands
    outs = pl.pallas_call(
        body, name=name,
        in_specs=[HBM] * len(bufs),
        out_specs=[SEM, SEM] + [HBM] * len(bufs) + [pl.BlockSpec(memory_space=pltpu.VMEM)],
        out_shape=[pltpu.SemaphoreType.DMA((n_t,)), pltpu.SemaphoreType.DMA((n_t,))]
        + [pltpu.HBM(b.shape, b.dtype) for b in bufs] + [jax.ShapeDtypeStruct((8, 128), F32)],
        input_output_aliases={i: 2 + i for i in range(len(bufs))},
        compiler_params=pltpu.CompilerParams(has_side_effects=DATAFLOW),
    )(*[_in_hbm(b) for b in bufs])
    return outs[0], outs[1], list(outs[2:2 + len(bufs)]), outs[-1]


def _core_exchange_wait(name, send_sems, recv_sems, bufs, after):
    n_t = len(bufs) // 2

    def body(*refs):
        land = refs[n_t:2 * n_t]
        send_ref, recv_ref = refs[2 * n_t], refs[2 * n_t + 1]
        x, y, c = _place()
        for t in range(n_t):
            whole = pltpu.make_async_remote_copy(src_ref=land[t], dst_ref=land[t], send_sem=send_ref.at[t],
                                                 recv_sem=recv_ref.at[t], device_id=(x, y, c), device_id_type=MESH)
            whole.wait_send()
            whole.wait_recv()

    outs = pl.pallas_call(
        body, name=name,
        in_specs=[HBM] * (2 * n_t) + [SEM, SEM, ANY], out_specs=[HBM] * (2 * n_t),
        out_shape=[pltpu.HBM(b.shape, b.dtype) for b in bufs],
        input_output_aliases={i: i for i in range(2 * n_t)},
        compiler_params=pltpu.CompilerParams(has_side_effects=DATAFLOW),
    )(*bufs, send_sems, recv_sems, after)
    return list(outs[:n_t]), list(outs[n_t:])


def _chip_exchange_start(name, parts):
    lands = [lax.empty(p.shape, p.dtype) for p in parts]
    return _chip_copies_start(name, parts, lands, lambda src, land, me, pk, c: (src.at[pk], land.at[me]))


def _chip_exchange_wait(name, send_sems, recv_sems, bufs, after):
    n_t = len(bufs) // 2
    return _chip_copies_wait(name, send_sems, recv_sems, bufs,
                             lambda buf: [b.at[pl.ds(0, 3)] for b in buf[:n_t]],
                             lambda buf: [b.at[pl.ds(0, 3)] for b in buf[n_t:]], after)


def _sum_chips(name, parts, landed, place, l, stacked):
    chips, rows, cols = landed.shape
    tr = min(256, rows)
    per = rows // tr

    def body(p_ref, own_ref, *refs):
        land, o_ref = refs[:chips], refs[-1]
        tot = None
        for k in range(chips):
            term = jnp.where(p_ref[0] == k, own_ref[...], land[k][...]).astype(F32)
            tot = term if tot is None else tot + term
        o_ref[...] = tot

    def from_chip(k):
        return pl.BlockSpec((None, tr, cols), lambda i, p: (jnp.where(p[0] == k, (k + 1) % chips, k), i, 0))

    in_specs = [pl.BlockSpec((None, tr, cols), lambda i, p: (p[0], i, 0))] + [from_chip(k) for k in range(chips)]
    args = [parts] + [landed] * chips
    aliases = {}
    if stacked is not None:
        in_specs.append(ANY)
        args.append(stacked)
        aliases = {len(args): 0}
    return pl.pallas_call(
        body, name=name,
        grid_spec=pltpu.PrefetchScalarGridSpec(
            num_scalar_prefetch=1, grid=(per,), in_specs=in_specs,
            out_specs=pl.BlockSpec((None, tr, cols), lambda i, p: (l, p[1] * per + i, 0))),
        out_shape=jax.ShapeDtypeStruct((DEPTH, 2 * rows, cols), F32), input_output_aliases=aliases,
        compiler_params=_params(("parallel",)),
    )(place, *args)


def _core_share_start(name, bufs, l):
    n_t = len(bufs)

    def body(*refs):
        mine = refs[:n_t]
        send_sems, recv_sems = refs[n_t], refs[n_t + 1]
        buf = refs[n_t + 2:2 * n_t + 2]
        token = refs[-1]
        x, y, c = _place()
        for q in range(D2D_CHUNKS):
            for t in range(n_t):
                hr = mine[t].shape[1] // 2
                cr = hr // D2D_CHUNKS
                rows = pl.ds(c * hr + q * cr, cr)
                pltpu.make_async_remote_copy(
                    src_ref=mine[t].at[l, rows], dst_ref=buf[t].at[l, rows], send_sem=send_sems.at[t],
                    recv_sem=recv_sems.at[t], device_id=(x, y, 1 - c), device_id_type=MESH).start()
        token[...] = jnp.zeros_like(token)

    outs = pl.pallas_call(
        body, name=name,
        in_specs=[HBM] * n_t,
        out_specs=[SEM, SEM] + [HBM] * n_t + [pl.BlockSpec(memory_space=pltpu.VMEM)],
        out_shape=[pltpu.SemaphoreType.DMA((n_t,)), pltpu.SemaphoreType.DMA((n_t,))]
        + [pltpu.HBM(b.shape, b.dtype) for b in bufs] + [jax.ShapeDtypeStruct((8, 128), F32)],
        input_output_aliases={i: 2 + i for i in range(n_t)},
        compiler_params=pltpu.CompilerParams(has_side_effects=DATAFLOW),
    )(*[_in_hbm(b) for b in bufs])
    return outs[0], outs[1], list(outs[2:2 + n_t]), outs[-1]


def _core_share_wait(name, send_sems, recv_sems, bufs, l, after):
    def half_layer(buf):
        return [b.at[l, pl.ds(0, b.shape[1] // 2)] for b in buf]

    return _chip_copies_wait(name, send_sems, recv_sems, bufs, half_layer, half_layer, after)


def _all_reduce_small(vec, after=None):
    rows, lanes = vec.shape
    hr = rows // 2

    def body(v_ref, *refs):
        o_ref, sib_ref, chips_ref, send_sems, recv_sems = refs[-5:]
        x, y, c = _place()
        me = 2 * x + y
        sibling = (x, y, 1 - c)
        mine = pl.ds(pl.multiple_of(c * hr, 8), hr)
        theirs = pl.ds(pl.multiple_of((1 - c) * hr, 8), hr)
        swap = pltpu.make_async_remote_copy(
            src_ref=v_ref.at[theirs], dst_ref=sib_ref, send_sem=send_sems.at[0], recv_sem=recv_sems.at[0],
            device_id=sibling, device_id_type=MESH)
        swap.start()
        swap.wait_recv()
        chips_ref[me] = v_ref[mine] + sib_ref[...]
        copies = []
        for j, (px, py, pk) in enumerate(_other_chips(x, y)):
            cp = pltpu.make_async_remote_copy(
                src_ref=chips_ref.at[me], dst_ref=chips_ref.at[me], send_sem=send_sems.at[1 + j],
                recv_sem=recv_sems.at[1 + j], device_id=(px, py, c), device_id_type=MESH)
            cp.start()
            copies.append(cp)
        for j, (px, py, pk) in enumerate(_other_chips(x, y)):
            pltpu.make_async_remote_copy(
                src_ref=chips_ref.at[pk], dst_ref=chips_ref.at[pk], send_sem=send_sems.at[1 + j],
                recv_sem=recv_sems.at[1 + j], device_id=(px, py, c), device_id_type=MESH).wait_recv()
        tot = chips_ref[0]
        for k in range(1, N_CHIPS):
            tot = tot + chips_ref[k]
        o_ref[mine] = tot
        share = pltpu.make_async_remote_copy(
            src_ref=o_ref.at[mine], dst_ref=o_ref.at[mine], send_sem=send_sems.at[4], recv_sem=recv_sems.at[4],
            device_id=sibling, device_id_type=MESH)
        share.start()
        pltpu.make_async_remote_copy(
            src_ref=o_ref.at[theirs], dst_ref=o_ref.at[theirs], send_sem=send_sems.at[4], recv_sem=recv_sems.at[4],
            device_id=sibling, device_id_type=MESH).wait_recv()
        swap.wait_send()
        for cp in copies:
            cp.wait_send()
        share.wait_send()

    vm = pl.BlockSpec(memory_space=pltpu.VMEM)
    return pl.pallas_call(
        body, name="small_all_reduce", in_specs=[vm] + ([] if after is None else [ANY]), out_specs=vm,
        out_shape=jax.ShapeDtypeStruct((rows, lanes), F32),
        scratch_shapes=[pltpu.VMEM((hr, lanes), F32), pltpu.VMEM((N_CHIPS, hr, lanes), F32),
                        pltpu.SemaphoreType.DMA((5,)), pltpu.SemaphoreType.DMA((5,))],
        compiler_params=pltpu.CompilerParams(has_side_effects=True, vmem_limit_bytes=48 * MIB),
    )(vec, *([] if after is None else [after]))


def _adamw(name, w, g, m, v, place, l=0, half=None, done=None, after=None):
    layers, rows, cols = w.shape
    span = rows if half is None else rows // 2
    tr = span
    for cand in (256, 128, 64, 32, 16, 8):
        if span % cand == 0:
            tr = cand
            break
    per = span // tr
    c1 = 1.0 - ADAM_B1 ** ADAM_STEP
    c2 = 1.0 - ADAM_B2 ** ADAM_STEP

    def first_block(p):
        return 0 if half is None else (p[1] if half == "own" else 1 - p[1]) * per

    def body(p_ref, w_ref, g_ref, m_ref, v_ref, *refs):
        go_ref, d_ref, nm_ref, nv_ref = refs[-4:]
        gv = g_ref[...]
        nm = ADAM_B1 * m_ref[...] + (1.0 - ADAM_B1) * gv
        nv = ADAM_B2 * v_ref[...] + (1.0 - ADAM_B2) * (gv * gv)
        go_ref[...] = gv
        nm_ref[...] = nm
        nv_ref[...] = nv
        d_ref[...] = -ADAM_LR * ((nm / c1) / (jnp.sqrt(nv / c2) + ADAM_EPS) + ADAM_WD * w_ref[...])

    blk = pl.BlockSpec((None, tr, cols), lambda i, p: (l, first_block(p) + i, 0))
    out = jax.ShapeDtypeStruct((layers, rows, cols), F32)
    extra = ([] if done is None else list(done)) + ([] if after is None else [after])
    aliases = {} if done is None else {5 + i: i for i in range(4)}
    return pl.pallas_call(
        body, name=name,
        grid_spec=pltpu.PrefetchScalarGridSpec(
            num_scalar_prefetch=1, grid=(per,), in_specs=[blk] * 4 + [ANY] * len(extra), out_specs=[blk] * 4),
        out_shape=[out] * 4, input_output_aliases=aliases,
        compiler_params=_params(("parallel",)),
    )(place, w, g, m, v, *extra)


LANES = 128
SUBLANES = 8
SMALL_SHAPES = {
    "norm_g": (DEPTH, D_MODEL), "sgu_ln_g": (DEPTH, D_A), "sgu_ln_b": (DEPTH, D_A),
    "sgu_w": (DEPTH, A_GROUPS, CHUNK, CHUNK), "sgu_b": (DEPTH, A_GROUPS, CHUNK), "mem_norm_g": (DEPTH, D_MODEL),
    "q_norm_g": (DEPTH, HEAD_DIM), "k_norm_g": (DEPTH, HEAD_DIM)}


def _small_layout():
    at, off = {}, 0
    for k in SMALL_NAMES:
        n = math.prod(SMALL_SHAPES[k]) // LANES
        at[k] = (off, n)
        off += -(-n // SUBLANES) * SUBLANES
    return at, off, -(-(off + SUBLANES) // (2 * SUBLANES)) * 2 * SUBLANES


def _pack_small(parts, loss=None):
    at, loss_row, rows = _small_layout()
    pieces = []
    for k in SMALL_NAMES:
        n = at[k][1]
        pieces.append(jnp.pad(parts[k].reshape(n, LANES), ((0, -(-n // SUBLANES) * SUBLANES - n), (0, 0))))
    tile = jnp.zeros((SUBLANES, LANES), F32) if loss is None else jnp.broadcast_to(loss.reshape(1, 1), (SUBLANES, LANES))
    pieces += [tile, jnp.zeros((rows - loss_row - SUBLANES, LANES), F32)]
    return jnp.concatenate(pieces)


def _adamw_small(w, g, m, v):
    at, _, rows = _small_layout()
    c1 = 1.0 - ADAM_B1 ** ADAM_STEP
    c2 = 1.0 - ADAM_B2 ** ADAM_STEP
    n_names = len(SMALL_NAMES)

    def body(w_ref, g_ref, m_ref, v_ref, *refs):
        outs, (d_ref, nm_ref, nv_ref) = refs[:4 * n_names], refs[4 * n_names:]
        gv = g_ref[...]
        nm = ADAM_B1 * m_ref[...] + (1.0 - ADAM_B1) * gv
        nv = ADAM_B2 * v_ref[...] + (1.0 - ADAM_B2) * (gv * gv)
        nm_ref[...] = nm
        nv_ref[...] = nv
        d_ref[...] = -ADAM_LR * ((nm / c1) / (jnp.sqrt(nv / c2) + ADAM_EPS) + ADAM_WD * w_ref[...])
        for kind, src in enumerate((g_ref, d_ref, nm_ref, nv_ref)):
            for i, k in enumerate(SMALL_NAMES):
                o_ref = outs[kind * n_names + i]
                first, n = at[k]
                shape = SMALL_SHAPES[k]
                if shape[-1] == LANES:
                    o_ref[...] = src[pl.ds(first, n), :].reshape(shape)
                else:
                    per = shape[-1] // LANES
                    for r in range(n):
                        o_ref[pl.ds(r // per, 1), pl.ds((r % per) * LANES, LANES)] = src[pl.ds(first + r, 1), :]

    out_shape = [jax.ShapeDtypeStruct(SMALL_SHAPES[k], F32) for _ in range(4) for k in SMALL_NAMES]
    outs = pl.pallas_call(
        body, name="adamw_small", out_shape=out_shape,
        scratch_shapes=[pltpu.VMEM((rows, LANES), F32)] * 3, compiler_params=_params(None),
    )(w, g, m, v)
    return [dict(zip(SMALL_NAMES, outs[kind * n_names:(kind + 1) * n_names])) for kind in range(4)]


WEIGHT_ORDER = ("norm_g", "w_in", "sgu_ln_g", "sgu_ln_b", "sgu_w", "sgu_b", "mem_norm_g", "w_mem_kv", "q_norm_g",
                "k_norm_g", "w_out")


def kernel(x, mem, norm_g, w_in, sgu_ln_g, sgu_ln_b, sgu_w, sgu_b, mem_norm_g, w_mem_kv, q_norm_g, k_norm_g, w_out, loss_target, m_norm_g, m_w_in, m_sgu_ln_g, m_sgu_ln_b, m_sgu_w, m_sgu_b, m_mem_norm_g, m_w_mem_kv, m_q_norm_g, m_k_norm_g, m_w_out, v_norm_g, v_w_in, v_sgu_ln_g, v_sgu_ln_b, v_sgu_w, v_sgu_b, v_mem_norm_g, v_w_mem_kv, v_q_norm_g, v_k_norm_g, v_w_out):
    weights = dict(norm_g=norm_g, w_in=w_in, sgu_ln_g=sgu_ln_g, sgu_ln_b=sgu_ln_b, sgu_w=sgu_w, sgu_b=sgu_b,
                   mem_norm_g=mem_norm_g, w_mem_kv=w_mem_kv, q_norm_g=q_norm_g, k_norm_g=k_norm_g, w_out=w_out)
    mom_m = dict(norm_g=m_norm_g, w_in=m_w_in, sgu_ln_g=m_sgu_ln_g, sgu_ln_b=m_sgu_ln_b, sgu_w=m_sgu_w, sgu_b=m_sgu_b,
                 mem_norm_g=m_mem_norm_g, w_mem_kv=m_w_mem_kv, q_norm_g=m_q_norm_g, k_norm_g=m_k_norm_g, w_out=m_w_out)
    mom_v = dict(norm_g=v_norm_g, w_in=v_w_in, sgu_ln_g=v_sgu_ln_g, sgu_ln_b=v_sgu_ln_b, sgu_w=v_sgu_w, sgu_b=v_sgu_b,
                 mem_norm_g=v_mem_norm_g, w_mem_kv=v_w_mem_kv, q_norm_g=v_q_norm_g, k_norm_g=v_k_norm_g, w_out=v_w_out)
    big = ("w_in", "w_mem_kv", "w_out")
    sm = {k: weights[k] for k in SMALL_NAMES}

    place = _place_index()
    xs, mems, target = x[0], mem[0], loss_target[0]

    slots = [[_cast_into_slot(f"cast_{k}_{l}", weights[k], l, place) for k in big] for l in range(DEPTH)]
    saved = [None] * DEPTH

    chips, cores = {}, {}
    me = place[0]
    arrival = jnp.stack([me, me ^ 2, me ^ 1, 3 - me]).astype(jnp.int32)
    shard_order = jnp.arange(N_CHIPS, dtype=jnp.int32)

    def start_gather(l, after=None):
        chips[l, "in"] = _gather_start(f"gather_start_{l}_in", slots[l][:1], after)
        chips[l, "rest"] = _gather_start(f"gather_start_{l}_rest", slots[l][1:], chips[l, "in"][3])
        return chips[l, "rest"][3]

    def hand_to_sibling(l, group, after):
        send_sems, recv_sems, bufs, _ = chips[l, group]
        bufs = _gather_wait(f"gather_wait_{l}_{group}", send_sems, recv_sems, bufs, after)
        cores[l, group] = _gather_forward_start(f"gather_forward_{l}_{group}", bufs)
        return cores[l, group][3]

    def whole(l, group, after):
        send_sems, recv_sems, bufs, _ = cores[l, group]
        return _gather_wait(f"gather_whole_{l}_{group}", send_sems, recv_sems, bufs, after)

    later_slots = [s for layer in slots[1:] for s in layer]

    class Gathered:
        def __init__(self, l):
            self.l = l
            self.buf = None

        def landed_from(self, tag, peers, after, behind, then=None):
            send_sems, recv_sems, _, _ = chips[0, "in_" + tag]
            buf = _gather_wait(f"gather_wait_0_in_{tag}", send_sems, recv_sems, self.buf, after, peers)
            if then is not None:
                buf, more = then(buf)
                behind = behind + more
            send_sems, recv_sems, buf, token = _gather_forward_start(f"gather_forward_0_in_{tag}", buf, peers)
            self.buf = _gather_wait(f"gather_whole_0_in_{tag}", send_sems, recv_sems, buf, [token] + behind, peers)

        def w_in(self, stage, h, proj):
            if self.l > 0:
                return (whole(self.l, "in", h)[0], shard_order, 0, N_CHIPS) if stage == 0 else None
            if stage == 0:
                self.buf = chips[0, "in_n"][2]
                return self.buf[0], arrival, 0, 1
            if stage == 1:
                def start_others(buf):
                    chips[0, "in_d"] = _gather_start("gather_start_0_in_d", buf, None, DIAGONAL)
                    chips[0, "rest"] = _gather_start("gather_start_0_rest", slots[0][1:], chips[0, "in_d"][3])
                    return chips[0, "in_d"][2], [chips[0, "rest"][3]]

                self.landed_from("n", NEIGHBOURS, proj, later_slots + [chips[0, "in_n"][3]], start_others)
                return self.buf[0], arrival, 1, 2
            if stage == 2:
                self.landed_from("d", DIAGONAL, [proj, chips[0, "rest"][3]], [])
                return self.buf[0], arrival, 3, 1
            return None

        def rest_start(self, proj):
            token = proj if self.l == 0 else hand_to_sibling(self.l, "rest", proj)
            return start_gather(self.l + 1, token) if self.l + 1 < DEPTH else token

        def rest_finish(self, o_b):
            if self.l == 0:
                o_b = hand_to_sibling(self.l, "rest", o_b)
            w_kv_all, w_out_all = whole(self.l, "rest", o_b)
            return w_kv_all, w_out_all, None

        def before_out(self, y):
            return hand_to_sibling(self.l + 1, "in", y) if self.l + 1 < DEPTH else None

    chips[0, "in_n"] = _gather_start("gather_start_0_in_n", slots[0][:1], None, NEIGHBOURS)
    cur = xs
    for l in range(DEPTH):
        cur, saved[l] = _layer_fwd(l, cur, mems, sm, Gathered(l), target if l == DEPTH - 1 else None)
    dxo, dxo_b, loss_part = cur

    small_g = [None] * DEPTH
    flight = {}

    class Exchange:
        def __init__(self):
            self.cores = {}

        def start(self, l, group, gives):
            *self.cores[l, group], token = _core_exchange_start(f"grad_core_start_{l}_{group}", gives)
            return token

        def landed(self, l, group, after):
            send_sems, recv_sems, bufs = self.cores[l, group]
            return _core_exchange_wait(f"grad_core_wait_{l}_{group}", send_sems, recv_sems, bufs, after)[1]

        def send(self, l, group, parts):
            *flight[l, group], token = _chip_exchange_start(f"grad_chip_start_{l}_{group}", parts)
            return token

    exchange = Exchange()
    for l in reversed(range(DEPTH)):
        dxo, dxo_b, small_g[l] = _layer_bwd(l, dxo, dxo_b, mems, sm, saved[l], place, exchange)
    grad_x = dxo

    groups = (("out", ("w_out",)), ("in", ("w_in", "w_mem_kv")))
    halves, stepped = dict.fromkeys(big), dict.fromkeys(big)
    small_g = {k: jnp.stack([small_g[l][k] for l in range(DEPTH)]) for k in SMALL_NAMES}
    after = grad_x
    sharing = {}

    def reduce_group(l, group, names):
        nonlocal after
        send_sems, recv_sems, bufs = flight[l, group]
        bufs = _chip_exchange_wait(f"grad_chip_wait_{l}_{group}", send_sems, recv_sems, bufs, after)
        for t, k in enumerate(names):
            halves[k] = _sum_chips(f"grad_chip_sum_{l}_{k}", bufs[t], bufs[len(names) + t], place, l, halves[k])
        *sharing[l, group], after = _core_share_start(f"grad_core_share_{l}_{group}", [halves[k] for k in names], l)

    def step(l, k, buf, half):
        nonlocal after
        tag = "" if half is None else "_" + half
        stepped[k] = _adamw(f"adamw_{k}_{l}{tag}", weights[k], buf, mom_m[k], mom_v[k], place, l, half, stepped[k],
                            after)
        after = stepped[k][1]

    def step_group(l, group, names, overlap):
        nonlocal after
        send_sems, recv_sems, bufs = sharing[l, group]
        if overlap:
            for k, buf in zip(names, bufs):
                step(l, k, buf, "own")
        bufs = _core_share_wait(f"grad_core_shared_{l}_{group}", send_sems, recv_sems, bufs, l, after)
        for k, buf in zip(names, bufs):
            halves[k] = buf
            step(l, k, buf, "other" if overlap else None)

    for l in reversed(range(DEPTH)):
        last = l == 0
        (g_out, n_out), (g_in, n_in) = groups
        reduce_group(l, g_out, n_out)
        if last:
            step_group(l, g_out, n_out, False)
            small_sum = _all_reduce_small(_pack_small(small_g, loss_part), after)
            small_step = _adamw_small(_pack_small(sm), small_sum, _pack_small({k: mom_m[k] for k in SMALL_NAMES}),
                                      _pack_small({k: mom_v[k] for k in SMALL_NAMES}))
            after = small_step[1]["sgu_w"]
        reduce_group(l, g_in, n_in)
        if not last:
            step_group(l, g_out, n_out, False)
        step_group(l, g_in, n_in, last)

    grads, delta, new_m, new_v = ({k: stepped[k][i] for k in big} for i in range(4))
    for out, small in zip((grads, delta, new_m, new_v), small_step):
        out.update(small)
    loss = small_sum[_small_layout()[1], 0]
    return (loss, grad_x[None], *[grads[k] for k in WEIGHT_ORDER], *[delta[k] for k in WEIGHT_ORDER],
            *[new_m[k] for k in WEIGHT_ORDER], *[new_v[k] for k in WEIGHT_ORDER])
```

```python
import functools
import math

import jax
import jax.numpy as jnp
from jax import lax
from jax.experimental import pallas as pl
from jax.experimental.pallas import tpu as pltpu

F32 = jnp.float32
BF16 = jnp.bfloat16
MESH = pl.DeviceIdType.MESH

D_MODEL = 2048
DEPTH = 2
CHUNK = 128
D_A = 1024
A_GROUPS = 8
D_B = 512
D_C = 512
HEADS = 4
HEAD_DIM = 128
IN_WIDTH = 6144
N_CHIPS = 4
EPS = 1e-6
ATT_SCALE = 1.0 / math.sqrt(HEAD_DIM)

OFF_U, OFF_V, OFF_ZA = 0, 1024, 2048
OFF_QB, OFF_KB, OFF_VB, OFF_ZB = 3072, 3584, 4096, 4608
OFF_QC, OFF_ZC = 5120, 5632
OFF_YB, OFF_YC = 1024, 1536

ADAM_LR = 0.001
ADAM_B1 = 0.9
ADAM_B2 = 0.999
ADAM_EPS = 1e-08
ADAM_WD = 0.01
ADAM_STEP = 10

MIB = 1024 * 1024
ANY = pl.BlockSpec(memory_space=pl.ANY)


def _params(semantics=None, vmem_mb=48):
    return pltpu.CompilerParams(dimension_semantics=semantics, vmem_limit_bytes=vmem_mb * MIB)


def _gelu(x):
    return 0.5 * x * (1.0 + lax.erf(x * (1.0 / math.sqrt(2.0))))


def _gelu_grad(x):
    cdf = 0.5 * (1.0 + lax.erf(x * (1.0 / math.sqrt(2.0))))
    pdf = jnp.exp(-0.5 * x * x) * (1.0 / math.sqrt(2.0 * math.pi))
    return cdf + x * pdf


def _sigmoid(x):
    return 1.0 / (1.0 + jnp.exp(-x))


def _silu_and_grad(z):
    s = _sigmoid(z)
    return z * s, s * (1.0 + z * (1.0 - s))


def _split_bf16(x):
    hi = x.astype(BF16)
    lo = (x - hi.astype(F32)).astype(BF16)
    return hi, lo


def _dot(a, b, dims):
    return lax.dot_general(a, b, (dims, ((), ())), preferred_element_type=F32)


NN = ((1,), (0,))
NT = ((1,), (1,))
TN = ((0,), (0,))


def _matmul(name, a, b, *, grid, a_spec, b_spec, o_spec, out_shape, dims, res=None, res_spec=None, after=None,
            place=None, into=None, vmem_mb=48):
    nk = grid[2]
    n_in = 2 + (res is not None) + (after is not None) + (into is not None)

    def body(*refs):
        if place is not None:
            refs = refs[1:]
        a_ref, b_ref = refs[0], refs[1]
        r_ref = refs[2] if res is not None else None
        o_ref = refs[n_in]
        if len(b_ref.shape) == 3 and dims == NN:
            part = _dot(a_ref[...], b_ref[...].reshape(-1, b_ref.shape[-1]), dims)
        elif len(b_ref.shape) == 3:
            width = b_ref.shape[-1]
            part = None
            for s in range(b_ref.shape[0]):
                term = _dot(a_ref[:, s * width:(s + 1) * width], b_ref[s], dims)
                part = term if part is None else part + term
        else:
            part = _dot(a_ref[...], b_ref[...], dims)
        if nk == 1:
            if r_ref is not None:
                part = part + r_ref[...]
            o_ref[...] = part.astype(o_ref.dtype)
            return
        acc_ref = refs[n_in + 1]
        k = pl.program_id(2)

        @pl.when(k == 0)
        def _():
            acc_ref[...] = part

        @pl.when(k > 0)
        def _():
            acc_ref[...] += part

        @pl.when(k == nk - 1)
        def _():
            tot = acc_ref[...]
            if r_ref is not None:
                tot = tot + r_ref[...]
            o_ref[...] = tot.astype(o_ref.dtype)

    in_specs = [a_spec, b_spec]
    args = [a, b]
    if res is not None:
        in_specs.append(res_spec)
        args.append(res)
    if after is not None:
        in_specs.append(ANY)
        args.append(after)
    aliases = {}
    if into is not None:
        in_specs.append(ANY)
        args.append(into)
        aliases = {len(args) - 1 + (place is not None): 0}
    acc_shape = tuple(d for d in o_spec.block_shape if d is not None)
    scratch = [pltpu.VMEM(acc_shape, F32)] if nk > 1 else []
    params = _params(("parallel", "parallel", "arbitrary"), vmem_mb)
    if place is not None:
        return pl.pallas_call(
            body, name=name, out_shape=out_shape, compiler_params=params, input_output_aliases=aliases,
            grid_spec=pltpu.PrefetchScalarGridSpec(num_scalar_prefetch=1, grid=grid, in_specs=in_specs,
                                                   out_specs=o_spec, scratch_shapes=scratch),
        )(place, *args)
    return pl.pallas_call(
        body, name=name, grid=grid, in_specs=in_specs, out_specs=o_spec, out_shape=out_shape,
        scratch_shapes=scratch, compiler_params=params, input_output_aliases=aliases,
    )(*args)


def _rms_fwd(name, x, g, tr, after=None, transposed=False):
    rows, d = x.shape

    def body(x_ref, g_ref, *refs):
        outs = refs[1:] if after is not None else refs
        xv = x_ref[...]
        r = lax.rsqrt(jnp.mean(xv * xv, axis=-1, keepdims=True) + EPS)
        h = xv * r * g_ref[...]
        outs[0][...] = h.astype(BF16)
        if transposed:
            outs[1][...] = h.T.astype(BF16)

    out_specs = [pl.BlockSpec((tr, d), lambda i: (i, 0))]
    out_shape = [jax.ShapeDtypeStruct((rows, d), BF16)]
    if transposed:
        out_specs.append(pl.BlockSpec((d, tr), lambda i: (0, i)))
        out_shape.append(jax.ShapeDtypeStruct((d, rows), BF16))
    outs = pl.pallas_call(
        body, name=name, grid=(rows // tr,),
        in_specs=[pl.BlockSpec((tr, d), lambda i: (i, 0)), pl.BlockSpec((1, d), lambda i: (0, 0))]
        + ([] if after is None else [ANY]),
        out_specs=out_specs, out_shape=out_shape,
        compiler_params=_params(("parallel",)),
    )(x, g, *([] if after is None else [after]))
    return outs if transposed else outs[0]


def _rms_bwd(name, x, dh, dres, g, tr, after=None):
    rows, d = x.shape

    def body(x_ref, dh_ref, dres_ref, g_ref, *refs):
        dx_ref, dxb_ref, dg_ref = refs[-3:]
        xv = x_ref[...]
        r = lax.rsqrt(jnp.mean(xv * xv, axis=-1, keepdims=True) + EPS)
        xhat = xv * r
        dhv = dh_ref[...]
        dxh = dhv * g_ref[...]
        dx = r * (dxh - xhat * jnp.mean(dxh * xhat, axis=-1, keepdims=True)) + dres_ref[...]
        dx_ref[...] = dx
        dxb_ref[...] = dx.astype(BF16)
        part = jnp.sum(dhv * xhat, axis=0, keepdims=True)

        @pl.when(pl.program_id(0) == 0)
        def _():
            dg_ref[...] = part

        @pl.when(pl.program_id(0) > 0)
        def _():
            dg_ref[...] += part

    blk = pl.BlockSpec((tr, d), lambda i: (i, 0))
    vec = pl.BlockSpec((1, d), lambda i: (0, 0))
    return pl.pallas_call(
        body, name=name, grid=(rows // tr,), in_specs=[blk, blk, blk, vec] + ([] if after is None else [ANY]),
        out_specs=[blk, blk, vec],
        out_shape=[jax.ShapeDtypeStruct((rows, d), F32), jax.ShapeDtypeStruct((rows, d), BF16),
                   jax.ShapeDtypeStruct((1, d), F32)],
        compiler_params=_params(("arbitrary",)),
    )(x, dh, dres, g, *([] if after is None else [after]))


def _rms_gain_grad(name, x, dh):
    rows, d = x.shape

    def body(x_ref, dh_ref, dg_ref):
        xv = x_ref[...]
        r = lax.rsqrt(jnp.mean(xv * xv, axis=-1, keepdims=True) + EPS)
        dg_ref[...] = jnp.sum(dh_ref[...] * xv * r, axis=0, keepdims=True)

    return pl.pallas_call(
        body, name=name, out_shape=jax.ShapeDtypeStruct((1, d), F32), compiler_params=_params(None),
    )(x, dh)


SB_T = 256
SB_HEADS = 4


LOG2E = 1.4426950408889634


def _sb_scores(q, kblk):
    z2 = _dot(q, kblk, NT) * (ATT_SCALE * LOG2E)
    e = jnp.exp2(-jnp.abs(z2))
    l1 = jnp.minimum(-z2, 0.0) - jnp.log2(1.0 + e)
    lb = l1 + z2
    return z2, e, lb, l1


def _sb_fwd(name, proj, after=None):
    s_len = proj.shape[0]
    t = SB_T
    nq = s_len // t

    def body(q_ref, k_ref, v_ref, *refs):
        o_ref = refs[-1]
        i = pl.program_id(1)
        row = lax.broadcasted_iota(jnp.int32, (t, t), 0)
        col = lax.broadcasted_iota(jnp.int32, (t, t), 1)
        causal = col < row
        after_mat = (row > col).astype(BF16)
        heads = [slice(hh * HEAD_DIM, (hh + 1) * HEAD_DIM) for hh in range(SB_HEADS)]
        q = [q_ref[:, sl].astype(BF16) for sl in heads]

        def tile(kb, state, masked):
            start = pl.multiple_of(kb * t, t)
            out = []
            for hh, sl in enumerate(heads):
                carry, acc = state[hh]
                kblk = k_ref[pl.ds(start, t), sl].astype(BF16)
                vblk = v_ref[pl.ds(start, t), sl].astype(BF16)
                _, _, lb, l1 = _sb_scores(q[hh], kblk)
                if masked:
                    l1 = jnp.where(causal, l1, 0.0)
                hi, lo = _split_bf16(l1)
                after = _dot(hi, after_mat, NN) + _dot(lo, after_mat, NN) + carry
                a = jnp.exp2(lb + after)
                if masked:
                    a = jnp.where(causal, a, 0.0)
                acc = acc + _dot(a.astype(BF16), vblk, NN)
                carry = carry + jnp.sum(l1, axis=-1, keepdims=True)
                out.append((carry, acc))
            return tuple(out)

        zero = (jnp.zeros((t, 1), F32), jnp.zeros((t, HEAD_DIM), F32))
        state = tile(i, (zero,) * SB_HEADS, True)
        state = lax.fori_loop(0, i, lambda n, st: tile(i - 1 - n, st, False), state)
        for hh, sl in enumerate(heads):
            o_ref[:, sl] = state[hh][1]

    cb = SB_HEADS * HEAD_DIM
    return pl.pallas_call(
        body, name=name, grid=(HEADS // SB_HEADS, nq),
        in_specs=[pl.BlockSpec((t, cb), lambda h, i: (i, OFF_QB // cb + h)),
                  pl.BlockSpec((s_len, cb), lambda h, i: (0, OFF_KB // cb + h)),
                  pl.BlockSpec((s_len, cb), lambda h, i: (0, OFF_VB // cb + h))] + ([] if after is None else [ANY]),
        out_specs=pl.BlockSpec((t, cb), lambda h, i: (i, h)),
        out_shape=jax.ShapeDtypeStruct((s_len, D_B), F32),
        compiler_params=_params(("parallel", "arbitrary")),
    )(proj, proj, proj, *([] if after is None else [after]))


def _sb_bwd(name, proj, dy, after=None):
    s_len = proj.shape[0]
    t = SB_T
    nq = s_len // t

    def body(q_ref, k_ref, v_ref, z_ref, dy_ref, *refs):
        dq_ref, dk_ref, dv_ref, a_ref, s_ref = refs[-5:]
        i = pl.program_id(1)

        @pl.when(i == 0)
        def _():
            dk_ref[...] = jnp.zeros_like(dk_ref)
            dv_ref[...] = jnp.zeros_like(dv_ref)

        heads = [slice(hh * HEAD_DIM, (hh + 1) * HEAD_DIM) for hh in range(SB_HEADS)]
        q = [q_ref[:, sl].astype(BF16) for sl in heads]
        silu_z, _ = _silu_and_grad(z_ref[...])
        do_all = dy_ref[...] * silu_z
        do_b = [do_all[:, sl].astype(BF16) for sl in heads]
        row = lax.broadcasted_iota(jnp.int32, (t, t), 0)
        col = lax.broadcasted_iota(jnp.int32, (t, t), 1)
        causal = col < row
        after_mat = (row > col).astype(BF16)
        before_mat = (row < col).astype(BF16)

        def weights(kb, carries, masked):
            start = pl.multiple_of(kb * t, t)
            out = []
            for hh, sl in enumerate(heads):
                kblk = k_ref[pl.ds(start, t), sl].astype(BF16)
                z, _, lb, l1 = _sb_scores(q[hh], kblk)
                if masked:
                    l1 = jnp.where(causal, l1, 0.0)
                hi, lo = _split_bf16(l1)
                after = _dot(hi, after_mat, NN) + _dot(lo, after_mat, NN) + carries[hh]
                a = jnp.exp2(lb + after)
                if masked:
                    a = jnp.where(causal, a, 0.0)
                a_ref[hh, kb] = a
                s_ref[hh, kb] = z
                out.append(carries[hh] + jnp.sum(l1, axis=-1, keepdims=True))
            return tuple(out)

        carries = weights(i, (jnp.zeros((t, 1), F32),) * SB_HEADS, True)
        lax.fori_loop(0, i, lambda n, c: weights(i - 1 - n, c, False), carries)

        def grads(kb, state, masked):
            start = pl.multiple_of(kb * t, t)
            out = []
            for hh, sl in enumerate(heads):
                carry, dq = state[hh]
                kblk = k_ref[pl.ds(start, t), sl].astype(BF16)
                vblk = v_ref[pl.ds(start, t), sl].astype(BF16)
                a = a_ref[hh, kb]
                z = s_ref[hh, kb]
                g = _dot(do_b[hh], vblk, NT) * a
                ghi, glo = _split_bf16(g)
                prefix = _dot(ghi, before_mat, NN) + _dot(glo, before_mat, NN) + carry
                e = jnp.exp2(-jnp.abs(z))
                inv = 1.0 / (1.0 + e)
                pos = z >= 0.0
                beta = jnp.where(pos, inv, e * inv)
                one_m_beta = jnp.where(pos, e * inv, inv)
                dz = (g * one_m_beta - prefix * beta) * ATT_SCALE
                if masked:
                    dz = jnp.where(causal, dz, 0.0)
                dz_b = dz.astype(BF16)
                dq = dq + _dot(dz_b, kblk, NN)
                dk_ref[pl.ds(start, t), sl] += _dot(dz_b, q[hh], TN)
                dv_ref[pl.ds(start, t), sl] += _dot(a.astype(BF16), do_b[hh], TN)
                out.append((carry + jnp.sum(g, axis=-1, keepdims=True), dq))
            return tuple(out)

        zero = (jnp.zeros((t, 1), F32), jnp.zeros((t, HEAD_DIM), F32))
        state = lax.fori_loop(0, i, lambda kb, st: grads(kb, st, False), (zero,) * SB_HEADS)
        state = grads(i, state, True)
        for hh, sl in enumerate(heads):
            dq_ref[:, sl] = state[hh][1]

    cb = SB_HEADS * HEAD_DIM
    qblk = lambda off: pl.BlockSpec((t, cb), lambda h, i: (i, off // cb + h))
    full = lambda off: pl.BlockSpec((s_len, cb), lambda h, i: (0, off // cb + h))
    out = jax.ShapeDtypeStruct((s_len, D_B), F32)
    return pl.pallas_call(
        body, name=name, grid=(HEADS // SB_HEADS, nq),
        in_specs=[qblk(OFF_QB), full(OFF_KB), full(OFF_VB), qblk(OFF_ZB), qblk(OFF_YB)]
        + ([] if after is None else [ANY]),
        out_specs=[qblk(0), full(0), full(0)],
        out_shape=[out, out, out],
        scratch_shapes=[pltpu.VMEM((SB_HEADS, nq, t, t), F32), pltpu.VMEM((SB_HEADS, nq, t, t), F32)],
        compiler_params=_params(("parallel", "arbitrary")),
    )(proj, proj, proj, proj, dy, *([] if after is None else [after]))


MEM_TQ = 512


def _qk_norm(x, g):
    r = lax.rsqrt(jnp.mean(x * x, axis=-1, keepdims=True) + EPS)
    xhat = x * r
    return xhat * g, xhat, r


def _qk_norm_bwd(dn, g, xhat, r):
    dxh = dn * g
    return r * (dxh - xhat * jnp.mean(dxh * xhat, axis=-1, keepdims=True))


def _mem_probs(q, mk, qg, kg):
    qn, qhat, rq = _qk_norm(q, qg)
    kn, khat, rk = _qk_norm(mk, kg)
    qn_b, kn_b = qn.astype(BF16), kn.astype(BF16)
    s = _dot(qn_b, kn_b, NT) * ATT_SCALE
    p = jnp.exp(s - jnp.max(s, axis=-1, keepdims=True))
    p = p / jnp.sum(p, axis=-1, keepdims=True)
    return p, qn_b, kn_b, qhat, rq, khat, rk


def _mem_fwd(name, proj, mem_kv, qg, kg):
    s_len = proj.shape[0]
    m_len = mem_kv.shape[0]
    tq = min(MEM_TQ, s_len)

    def body(q_ref, mk_ref, mv_ref, qg_ref, kg_ref, o_ref):
        p = _mem_probs(q_ref[...], mk_ref[...], qg_ref[...], kg_ref[...])[0]
        o_ref[...] = _dot(p.astype(BF16), mv_ref[...].astype(BF16), NN)

    cb = HEAD_DIM
    vec = pl.BlockSpec((1, cb), lambda h, i: (0, 0))
    return pl.pallas_call(
        body, name=name, grid=(HEADS, s_len // tq),
        in_specs=[pl.BlockSpec((tq, cb), lambda h, i: (i, OFF_QC // cb + h)),
                  pl.BlockSpec((m_len, cb), lambda h, i: (0, h)),
                  pl.BlockSpec((m_len, cb), lambda h, i: (0, HEADS + h)), vec, vec],
        out_specs=pl.BlockSpec((tq, cb), lambda h, i: (i, h)),
        out_shape=jax.ShapeDtypeStruct((s_len, D_C), F32),
        compiler_params=_params(("parallel", "parallel")),
    )(proj, mem_kv, mem_kv, qg, kg)


def _mem_bwd(name, proj, mem_kv, qg, kg, dy):
    s_len = proj.shape[0]
    m_len = mem_kv.shape[0]
    tq = min(MEM_TQ, s_len)

    def body(q_ref, mk_ref, mv_ref, qg_ref, kg_ref, z_ref, dy_ref, dq_ref, dmk_ref, dmv_ref, dqg_ref, dkg_ref):
        h, i = pl.program_id(0), pl.program_id(1)

        @pl.when(i == 0)
        def _():
            dmk_ref[...] = jnp.zeros_like(dmk_ref)
            dmv_ref[...] = jnp.zeros_like(dmv_ref)

        @pl.when((i == 0) & (h == 0))
        def _():
            dqg_ref[...] = jnp.zeros_like(dqg_ref)
            dkg_ref[...] = jnp.zeros_like(dkg_ref)

        qg, kg = qg_ref[...], kg_ref[...]
        p, qn_b, kn_b, qhat, rq, khat, rk = _mem_probs(q_ref[...], mk_ref[...], qg, kg)
        silu_z, _ = _silu_and_grad(z_ref[...])
        do_b = (dy_ref[...] * silu_z).astype(BF16)
        dmv_ref[...] += _dot(p.astype(BF16), do_b, TN)
        dp = _dot(do_b, mv_ref[...].astype(BF16), NT)
        ds = (p * (dp - jnp.sum(dp * p, axis=-1, keepdims=True)) * ATT_SCALE).astype(BF16)
        dqn = _dot(ds, kn_b, NN)
        dkn = _dot(ds, qn_b, TN)
        dq_ref[...] = _qk_norm_bwd(dqn, qg, qhat, rq)
        dmk_ref[...] += _qk_norm_bwd(dkn, kg, khat, rk)
        dqg_ref[...] += jnp.sum(dqn * qhat, axis=0, keepdims=True)
        dkg_ref[...] += jnp.sum(dkn * khat, axis=0, keepdims=True)

    cb = HEAD_DIM
    vec = pl.BlockSpec((1, cb), lambda h, i: (0, 0))
    qblk = lambda off: pl.BlockSpec((tq, cb), lambda h, i: (i, off // cb + h))
    memblk = lambda off: pl.BlockSpec((m_len, cb), lambda h, i: (0, off + h))
    return pl.pallas_call(
        body, name=name, grid=(HEADS, s_len // tq),
        in_specs=[qblk(OFF_QC), memblk(0), memblk(HEADS), vec, vec, qblk(OFF_ZC), qblk(OFF_YC)],
        out_specs=[qblk(0), memblk(0), memblk(0), vec, vec],
        out_shape=[jax.ShapeDtypeStruct((s_len, D_C), F32), jax.ShapeDtypeStruct((m_len, D_C), F32),
                   jax.ShapeDtypeStruct((m_len, D_C), F32), jax.ShapeDtypeStruct((1, cb), F32),
                   jax.ShapeDtypeStruct((1, cb), F32)],
        compiler_params=_params(("arbitrary", "arbitrary")),
    )(proj, mem_kv, mem_kv, qg, kg, proj, dy)


def _sgu_common(u_ref, v_ref, lng_ref, lnb_ref, w_ref, bias_ref):
    ug = _gelu(u_ref[...])
    vg = _gelu(v_ref[...])
    mu = jnp.mean(vg, axis=-1, keepdims=True)
    xc = vg - mu
    rstd = lax.rsqrt(jnp.mean(xc * xc, axis=-1, keepdims=True) + EPS)
    xhat = xc * rstd
    vn = xhat * lng_ref[...] + lnb_ref[...]
    vn_b = vn.astype(BF16)
    row = lax.broadcasted_iota(jnp.int32, (CHUNK, CHUNK), 0)
    col = lax.broadcasted_iota(jnp.int32, (CHUNK, CHUNK), 1)
    tril = row >= col
    mixed = []
    for g in range(A_GROUPS):
        w = jnp.where(tril, w_ref[g], 0.0).astype(BF16)
        sl = slice(g * CHUNK, (g + 1) * CHUNK)
        mixed.append(_dot(w, vn_b[:, sl], NN) + bias_ref[:, sl])
    return ug, xhat, rstd, vn_b, mixed, tril


def _gate_fwd(name, proj, o_b, o_c, lng, lnb, w_s, bias):
    s_len = proj.shape[0]

    def body(u_ref, v_ref, za_ref, zb_ref, zc_ref, ob_ref, oc_ref, lng_ref, lnb_ref, w_ref, bias_ref, y_ref, yt_ref):
        ug, _, _, _, mixed, _ = _sgu_common(u_ref, v_ref, lng_ref, lnb_ref, w_ref, bias_ref)
        sza, _ = _silu_and_grad(za_ref[...])
        gate = ug * sza

        def put(off, width, val):
            y_ref[:, off:off + width] = val.astype(BF16)
            yt_ref[off:off + width, :] = val.T.astype(BF16)

        for g in range(A_GROUPS):
            sl = slice(g * CHUNK, (g + 1) * CHUNK)
            put(g * CHUNK, CHUNK, gate[:, sl] * mixed[g])
        szb, _ = _silu_and_grad(zb_ref[...])
        put(OFF_YB, D_B, ob_ref[...] * szb)
        szc, _ = _silu_and_grad(zc_ref[...])
        put(OFF_YC, D_C, oc_ref[...] * szc)

    wide = lambda off: pl.BlockSpec((CHUNK, D_A), lambda i: (i, off // D_A))
    narrow = lambda off: pl.BlockSpec((CHUNK, D_B), lambda i: (i, off // D_B))
    vec = pl.BlockSpec((1, D_A), lambda i: (0, 0))
    return pl.pallas_call(
        body, name=name, grid=(s_len // CHUNK,),
        in_specs=[wide(OFF_U), wide(OFF_V), wide(OFF_ZA), narrow(OFF_ZB), narrow(OFF_ZC), narrow(0), narrow(0), vec, vec,
                  pl.BlockSpec((A_GROUPS, CHUNK, CHUNK), lambda i: (0, 0, 0)),
                  pl.BlockSpec((CHUNK, D_A), lambda i: (0, 0))],
        out_specs=[pl.BlockSpec((CHUNK, D_MODEL), lambda i: (i, 0)), pl.BlockSpec((D_MODEL, CHUNK), lambda i: (0, i))],
        out_shape=[jax.ShapeDtypeStruct((s_len, D_MODEL), BF16), jax.ShapeDtypeStruct((D_MODEL, s_len), BF16)],
        compiler_params=_params(("parallel",)),
    )(proj, proj, proj, proj, proj, o_b, o_c, lng, lnb, w_s, bias)


def _gate_bwd(name, proj, dy, o_b, o_c, dqkv, dq_c, lng, lnb, w_s, w_s_t, bias):
    s_len = proj.shape[0]
    n = s_len // CHUNK
    dq_b, dk_b, dv_b = dqkv

    def body(u_ref, v_ref, za_ref, zb_ref, zc_ref, dya_ref, dyb_ref, dyc_ref, ob_ref, oc_ref, dq_ref, dk_ref, dv_ref,
             dqc_ref, lng_ref, lnb_ref, w_ref, wt_ref, bias_ref, dp_ref, dw_ref, dsb_ref, dlng_ref, dlnb_ref, dbias_ref):
        i = pl.program_id(0)

        @pl.when(i == 0)
        def _():
            dw_ref[...] = jnp.zeros_like(dw_ref)
            dbias_ref[...] = jnp.zeros_like(dbias_ref)
            dlng_ref[...] = jnp.zeros_like(dlng_ref)
            dlnb_ref[...] = jnp.zeros_like(dlnb_ref)

        ug, xhat, rstd, vn_b, mixed, tril = _sgu_common(u_ref, v_ref, lng_ref, lnb_ref, w_ref, bias_ref)
        za = za_ref[...]
        sza, dsza = _silu_and_grad(za)
        dya = dya_ref[...]
        mixed_all = jnp.concatenate(mixed, axis=-1)
        d_mixed = dya * ug * sza
        dp_ref[:, OFF_U:OFF_U + D_A] = (dya * mixed_all * sza * _gelu_grad(u_ref[...])).astype(BF16)
        dp_ref[:, OFF_ZA:OFF_ZA + D_A] = (dya * ug * mixed_all * dsza).astype(BF16)
        dbias_ref[...] += d_mixed
        dm_b = d_mixed.astype(BF16)
        triu = lax.broadcasted_iota(jnp.int32, (CHUNK, CHUNK), 0) <= lax.broadcasted_iota(jnp.int32, (CHUNK, CHUNK), 1)
        d_vn = []
        for g in range(A_GROUPS):
            sl = slice(g * CHUNK, (g + 1) * CHUNK)
            wt = jnp.where(triu, wt_ref[g], 0.0).astype(BF16)
            d_vn.append(_dot(wt, dm_b[:, sl], NN))
            dw_ref[g] += jnp.where(tril, _dot(dm_b[:, sl], vn_b[:, sl], NT), 0.0)
        d_vn = jnp.concatenate(d_vn, axis=-1)
        dlng_ref[...] += jnp.sum(d_vn * xhat, axis=0, keepdims=True)
        dlnb_ref[...] += jnp.sum(d_vn, axis=0, keepdims=True)
        dxh = d_vn * lng_ref[...]
        d_vg = rstd * (dxh - jnp.mean(dxh, axis=-1, keepdims=True)
                       - xhat * jnp.mean(dxh * xhat, axis=-1, keepdims=True))
        dp_ref[:, OFF_V:OFF_V + D_A] = (d_vg * _gelu_grad(v_ref[...])).astype(BF16)
        dp_ref[:, OFF_QB:OFF_QB + D_B] = dq_ref[...].astype(BF16)
        dp_ref[:, OFF_KB:OFF_KB + D_B] = dk_ref[...].astype(BF16)
        dp_ref[:, OFF_VB:OFF_VB + D_B] = dv_ref[...].astype(BF16)
        _, dszb = _silu_and_grad(zb_ref[...])
        dp_ref[:, OFF_ZB:OFF_ZB + D_B] = (dyb_ref[...] * ob_ref[...] * dszb).astype(BF16)
        dp_ref[:, OFF_QC:OFF_QC + D_C] = dqc_ref[...].astype(BF16)
        _, dszc = _silu_and_grad(zc_ref[...])
        dp_ref[:, OFF_ZC:OFF_ZC + D_C] = (dyc_ref[...] * oc_ref[...] * dszc).astype(BF16)

        @pl.when(i == n - 1)
        def _():
            ch = lax.broadcasted_iota(jnp.int32, (D_A, CHUNK), 0)
            gcol = lax.broadcasted_iota(jnp.int32, (D_A, CHUNK), 1)
            pick = (ch // (D_A // A_GROUPS) == gcol).astype(BF16)
            rest = dbias_ref[...]
            tot = jnp.zeros((CHUNK, CHUNK), F32)
            for _ in range(3):
                term = rest.astype(BF16)
                tot = tot + _dot(term, pick, NN)
                rest = rest - term.astype(F32)
            dsb_ref[...] = tot

    wide = lambda off: pl.BlockSpec((CHUNK, D_A), lambda i: (i, off // D_A))
    narrow = lambda off: pl.BlockSpec((CHUNK, D_B), lambda i: (i, off // D_B))
    vec = pl.BlockSpec((1, D_A), lambda i: (0, 0))
    wspec = pl.BlockSpec((A_GROUPS, CHUNK, CHUNK), lambda i: (0, 0, 0))
    bspec = pl.BlockSpec((CHUNK, D_A), lambda i: (0, 0))
    return pl.pallas_call(
        body, name=name, grid=(n,),
        in_specs=[wide(OFF_U), wide(OFF_V), wide(OFF_ZA), narrow(OFF_ZB), narrow(OFF_ZC),
                  wide(0), narrow(OFF_YB), narrow(OFF_YC), narrow(0), narrow(0), narrow(0), narrow(0), narrow(0),
                  narrow(0), vec, vec, wspec, wspec, bspec],
        out_specs=[pl.BlockSpec((CHUNK, IN_WIDTH), lambda i: (i, 0)), wspec,
                   pl.BlockSpec((CHUNK, CHUNK), lambda i: (0, 0)), vec, vec],
        out_shape=[jax.ShapeDtypeStruct((s_len, IN_WIDTH), BF16), jax.ShapeDtypeStruct((A_GROUPS, CHUNK, CHUNK), F32),
                   jax.ShapeDtypeStruct((CHUNK, CHUNK), F32), jax.ShapeDtypeStruct((1, D_A), F32),
                   jax.ShapeDtypeStruct((1, D_A), F32)],
        scratch_shapes=[pltpu.VMEM((CHUNK, D_A), F32)],
        compiler_params=_params(("arbitrary",)),
    )(proj, proj, proj, proj, proj, dy, dy, dy, o_b, o_c, dq_b, dk_b, dv_b, dq_c, lng, lnb, w_s, w_s_t, bias)


IN_SHARD = IN_WIDTH // N_CHIPS
ROW_SHARD = D_MODEL // N_CHIPS


def _bias_rows(sgu_b_l):
    return jnp.repeat(sgu_b_l.T, D_A // A_GROUPS, axis=1)


class _WholeWeights:
    def __init__(self, w_in_all, w_kv_all, w_out_all):
        self.weights = (w_in_all, w_kv_all, w_out_all)

    def w_in(self, stage, h, proj):
        return (self.weights[0], jnp.arange(N_CHIPS, dtype=jnp.int32), 0, N_CHIPS) if stage == 0 else None

    def rest_start(self, proj):
        return None

    def rest_finish(self, o_b):
        return self.weights[1], self.weights[2], None

    def before_out(self, y):
        return None


def _layer_fwd(l, x, mem, sm, hooks, target=None):
    s_len = x.shape[0]
    m_len = mem.shape[0]
    tm = min(1024, s_len)
    h, h_t = _rms_fwd(f"rms_fwd_{l}", x, sm["norm_g"][l][None], min(256, s_len), transposed=True)
    proj, stage = None, 0
    while (ready := hooks.w_in(stage, h, proj)) is not None:
        w_in_all, order, first, count = ready
        proj = _matmul(
            f"in_proj_{l}_{stage}", h, w_in_all, grid=(s_len // tm, count, 1), place=order, into=proj,
            a_spec=pl.BlockSpec((tm, D_MODEL), lambda i, j, k, p: (i, 0)),
            b_spec=pl.BlockSpec((None, D_MODEL, IN_SHARD), lambda i, j, k, p: (p[first + j], 0, 0)),
            o_spec=pl.BlockSpec((tm, IN_SHARD), lambda i, j, k, p: (i, p[first + j])),
            out_shape=jax.ShapeDtypeStruct((s_len, IN_WIDTH), F32), dims=NN)
        stage += 1
    o_b = _sb_fwd(f"sb_fwd_{l}", proj, hooks.rest_start(proj))
    w_kv_all, w_out_all, after = hooks.rest_finish(o_b)
    mem_h = _rms_fwd(f"mem_rms_fwd_{l}", mem, sm["mem_norm_g"][l][None], m_len, after)
    mem_kv = _matmul(
        f"mem_kv_{l}", mem_h, w_kv_all, grid=(1, 2, N_CHIPS),
        a_spec=pl.BlockSpec((m_len, ROW_SHARD), lambda i, j, k: (0, k)),
        b_spec=pl.BlockSpec((None, ROW_SHARD, D_C), lambda i, j, k: (k, 0, j)),
        o_spec=pl.BlockSpec((m_len, D_C), lambda i, j, k: (0, j)),
        out_shape=jax.ShapeDtypeStruct((m_len, 2 * D_C), F32), dims=NN)
    qg, kg = sm["q_norm_g"][l][None], sm["k_norm_g"][l][None]
    o_c = _mem_fwd(f"mem_fwd_{l}", proj, mem_kv, qg, kg)
    bias = _bias_rows(sm["sgu_b"][l])
    y, y_t = _gate_fwd(f"gate_fwd_{l}", proj, o_b, o_c, sm["sgu_ln_g"][l][None], sm["sgu_ln_b"][l][None],
                       sm["sgu_w"][l], bias)
    saved = dict(x=x, h_t=h_t, proj=proj, mem_h=mem_h, mem_kv=mem_kv, o_b=o_b, o_c=o_c, y_t=y_t, bias=bias,
                 weights=(w_in_all, w_kv_all, w_out_all))
    if target is not None:
        return _out_proj_loss(f"out_proj_{l}", y, w_out_all, x, target, tm), saved
    tn_o = 512
    x_next = _matmul(
        f"out_proj_{l}", y, w_out_all, grid=(s_len // tm, D_MODEL // tn_o, 1),
        a_spec=pl.BlockSpec((tm, D_MODEL), lambda i, j, k: (i, 0)),
        b_spec=pl.BlockSpec((N_CHIPS, ROW_SHARD, tn_o), lambda i, j, k: (0, 0, j)),
        o_spec=pl.BlockSpec((tm, tn_o), lambda i, j, k: (i, j)),
        out_shape=jax.ShapeDtypeStruct((s_len, D_MODEL), F32), dims=NN,
        res=x, res_spec=pl.BlockSpec((tm, tn_o), lambda i, j, k: (i, j)), after=hooks.before_out(y))
    return x_next, saved


def _out_proj_loss(name, y, w_out_all, x, target, tm):
    s_len, d = x.shape
    tn = 512
    n_i = s_len // tm

    n_j = d // tn

    def body(y_ref, w_ref, x_ref, t_ref, dx_ref, dxb_ref, loss_ref, acc_ref):
        i, j = pl.program_id(0), pl.program_id(1)
        out = _dot(y_ref[...], w_ref[...].reshape(-1, tn), NN) + x_ref[...]
        e = out - t_ref[...]
        dx = e * (1.0 / d)
        dx_ref[...] = dx
        dxb_ref[...] = dx.astype(BF16)
        part = jnp.sum(e * e, axis=0, keepdims=True)

        @pl.when((i == 0) & (j == 0))
        def _():
            acc_ref[...] = part

        @pl.when((i > 0) | (j > 0))
        def _():
            acc_ref[...] += part

        @pl.when((i == n_i - 1) & (j == n_j - 1))
        def _():
            loss_ref[...] = jnp.sum(acc_ref[...], axis=-1, keepdims=True) * (0.5 / d)

    blk = pl.BlockSpec((tm, tn), lambda i, j: (i, j))
    return pl.pallas_call(
        body, name=name, grid=(n_i, n_j),
        in_specs=[pl.BlockSpec((tm, d), lambda i, j: (i, 0)), pl.BlockSpec((N_CHIPS, ROW_SHARD, tn), lambda i, j: (0, 0, j)),
                  blk, blk],
        out_specs=[blk, blk, pl.BlockSpec((1, 1), lambda i, j: (0, 0))],
        out_shape=[jax.ShapeDtypeStruct((s_len, d), F32), jax.ShapeDtypeStruct((s_len, d), BF16),
                   jax.ShapeDtypeStruct((1, 1), F32)],
        scratch_shapes=[pltpu.VMEM((1, tn), F32)],
        compiler_params=_params(("arbitrary", "arbitrary")),
    )(y, w_out_all, x, target)


class _NoExchange:
    def __init__(self):
        self.gave, self.kept = {}, {}

    def start(self, l, group, gives):
        self.gave[l, group] = gives
        return None

    def landed(self, l, group, after):
        return [jnp.zeros_like(g) for g in self.gave[l, group]]

    def send(self, l, group, parts):
        self.kept[l, group] = parts
        return None


def _layer_bwd(l, dxo, dxo_b, mem, sm, saved, place, exchange):
    s_len = dxo.shape[0]
    m_len = mem.shape[0]
    proj, y_t, h_t, mem_h, mem_kv = saved["proj"], saved["y_t"], saved["h_t"], saved["mem_h"], saved["mem_kv"]
    w_in_all, w_kv_all, w_out_all = saved["weights"]
    tm = min(1024, s_len)
    tn = 768
    per = IN_SHARD // tn
    half_rows = ROW_SHARD // 2

    def halves(make):
        give = lambda: make("give", lambda p: 1 - p[1], None, BF16)
        keep = lambda theirs: make("keep", lambda p: p[1], theirs, BF16)
        return give, keep

    def grad_out(tag, half, theirs, dtype):
        o_spec = pl.BlockSpec((None, half_rows, 1024), lambda i, j, k, p: (i, 0, j))
        return _matmul(
            f"d_w_out_{l}_{tag}", y_t, dxo_b, grid=(N_CHIPS, D_MODEL // 1024, 1), place=place,
            a_spec=pl.BlockSpec((half_rows, s_len), lambda i, j, k, p: (2 * i + half(p), 0)),
            b_spec=pl.BlockSpec((s_len, 1024), lambda i, j, k, p: (0, j)), o_spec=o_spec,
            out_shape=jax.ShapeDtypeStruct((N_CHIPS, half_rows, D_MODEL), dtype), dims=NN,
            res=theirs, res_spec=o_spec)

    def grad_in(tag, half, theirs, dtype):
        o_spec = pl.BlockSpec((None, D_MODEL // 2, tn), lambda i, j, k, p: (j // per, 0, j % per))
        return _matmul(
            f"d_w_in_{l}_{tag}", h_t, dproj, grid=(1, IN_WIDTH // tn, 1), place=place,
            a_spec=pl.BlockSpec((D_MODEL // 2, s_len), lambda i, j, k, p: (half(p), 0)),
            b_spec=pl.BlockSpec((s_len, tn), lambda i, j, k, p: (0, j)), o_spec=o_spec,
            out_shape=jax.ShapeDtypeStruct((N_CHIPS, D_MODEL // 2, IN_SHARD), dtype), dims=NN,
            res=theirs, res_spec=o_spec)

    def grad_kv(tag, half, theirs, dtype):
        o_spec = pl.BlockSpec((None, half_rows, 2 * D_C), lambda i, j, k, p: (i, 0, 0))
        return _matmul(
            f"d_w_kv_{l}_{tag}", mem_h, dkv_b, grid=(N_CHIPS, 1, 1), place=place,
            a_spec=pl.BlockSpec((m_len, half_rows), lambda i, j, k, p: (0, 2 * i + half(p))),
            b_spec=pl.BlockSpec((m_len, 2 * D_C), lambda i, j, k, p: (0, 0)), o_spec=o_spec,
            out_shape=jax.ShapeDtypeStruct((N_CHIPS, half_rows, 2 * D_C), dtype), dims=TN,
            res=theirs, res_spec=o_spec)

    give_out, keep_out = halves(grad_out)
    token = exchange.start(l, "out", [give_out()])
    dy = _matmul(
        f"d_y_{l}", dxo_b, w_out_all, grid=(s_len // tm, N_CHIPS, 1),
        a_spec=pl.BlockSpec((tm, D_MODEL), lambda i, j, k: (i, 0)),
        b_spec=pl.BlockSpec((None, ROW_SHARD, D_MODEL), lambda i, j, k: (j, 0, 0)),
        o_spec=pl.BlockSpec((tm, ROW_SHARD), lambda i, j, k: (i, j)),
        out_shape=jax.ShapeDtypeStruct((s_len, D_MODEL), F32), dims=NT, after=token)
    (theirs_out,) = exchange.landed(l, "out", dy)
    token = exchange.send(l, "out", [keep_out(theirs_out)])
    qg, kg = sm["q_norm_g"][l][None], sm["k_norm_g"][l][None]
    dqkv = _sb_bwd(f"sb_bwd_{l}", proj, dy, token)
    dq_c, dmk, dmv, dqg, dkg = _mem_bwd(f"mem_bwd_{l}", proj, mem_kv, qg, kg, dy)
    w_s = sm["sgu_w"][l]
    dproj, dws, dbias, dlng, dlnb = _gate_bwd(
        f"gate_bwd_{l}", proj, dy, saved["o_b"], saved["o_c"], dqkv, dq_c, sm["sgu_ln_g"][l][None],
        sm["sgu_ln_b"][l][None], w_s, jnp.swapaxes(w_s, 1, 2), saved["bias"])
    dkv_b = jnp.concatenate([dmk, dmv], axis=1).astype(BF16)
    give_in, keep_in = halves(grad_in)
    give_kv, keep_kv = halves(grad_kv)
    token = exchange.start(l, "in", [give_in(), give_kv()])
    dh = _matmul(
        f"d_h_{l}", dproj, w_in_all, grid=(s_len // tm, D_MODEL // 512, 1),
        a_spec=pl.BlockSpec((tm, IN_WIDTH), lambda i, j, k: (i, 0)),
        b_spec=pl.BlockSpec((N_CHIPS, 512, IN_SHARD), lambda i, j, k: (0, j, 0)),
        o_spec=pl.BlockSpec((tm, 512), lambda i, j, k: (i, j)),
        out_shape=jax.ShapeDtypeStruct((s_len, D_MODEL), F32), dims=NT, after=token, vmem_mb=56)
    theirs_in, theirs_kv = exchange.landed(l, "in", dh)
    token = exchange.send(l, "in", [keep_in(theirs_in), keep_kv(theirs_kv)])
    dx, dx_b, dng = _rms_bwd(f"rms_bwd_{l}", saved["x"], dh, dxo, sm["norm_g"][l][None], min(256, s_len), token)
    d_mem_h = _matmul(
        f"d_mem_h_{l}", dkv_b, w_kv_all, grid=(1, N_CHIPS, 1),
        a_spec=pl.BlockSpec((m_len, 2 * D_C), lambda i, j, k: (0, 0)),
        b_spec=pl.BlockSpec((None, ROW_SHARD, 2 * D_C), lambda i, j, k: (j, 0, 0)),
        o_spec=pl.BlockSpec((m_len, ROW_SHARD), lambda i, j, k: (0, j)),
        out_shape=jax.ShapeDtypeStruct((m_len, D_MODEL), F32), dims=NT)
    dmng = _rms_gain_grad(f"mem_rms_bwd_{l}", mem, d_mem_h)
    dsgu_b = dbias[:, :A_GROUPS].T
    small = dict(norm_g=dng[0], sgu_ln_g=dlng[0], sgu_ln_b=dlnb[0], sgu_w=dws, sgu_b=dsgu_b, mem_norm_g=dmng[0],
                 q_norm_g=dqg[0], k_norm_g=dkg[0])
    return dx, dx_b, small


SMALL_NAMES = ("norm_g", "sgu_ln_g", "sgu_ln_b", "sgu_w", "sgu_b", "mem_norm_g", "q_norm_g", "k_norm_g")


def _local_step(x, mem, target, sm, w_all):
    saved = []
    cur = x
    for l in range(DEPTH):
        cur, sv = _layer_fwd(l, cur, mem, sm, _WholeWeights(*w_all[l]), target if l == DEPTH - 1 else None)
        saved.append(sv)
    dxo, dxo_b, loss = cur
    small = [None] * DEPTH
    exchange = _NoExchange()
    place = jnp.zeros((2,), jnp.int32)
    for l in reversed(range(DEPTH)):
        dxo, dxo_b, small[l] = _layer_bwd(l, dxo, dxo_b, mem, sm, saved[l], place, exchange)
    small = {k: jnp.stack([small[l][k] for l in range(DEPTH)]) for k in SMALL_NAMES}
    return loss, dxo, small, exchange.gave, exchange.kept


def _place():
    x, y, c = lax.axis_index("x"), lax.axis_index("y"), lax.axis_index("c")
    return x, y, c


def _other_chips(x, y):
    return [(1 - x, y, 2 * (1 - x) + y), (x, 1 - y, 2 * x + 1 - y), (1 - x, 1 - y, 2 * (1 - x) + 1 - y)]


D2D_CHUNKS = 8


def _place_index():
    return jnp.stack([2 * lax.axis_index("x") + lax.axis_index("y"), lax.axis_index("c")]).astype(jnp.int32)


def _cast_into_slot(name, w, l, place):
    _, rows, cols = w.shape
    tr = min(256, rows)

    def body(p_ref, w_ref, o_ref):
        o_ref[...] = w_ref[...].astype(BF16)

    return pl.pallas_call(
        body, name=name,
        grid_spec=pltpu.PrefetchScalarGridSpec(
            num_scalar_prefetch=1, grid=(rows // tr,),
            in_specs=[pl.BlockSpec((None, tr, cols), lambda i, p: (l, i, 0))],
            out_specs=pl.BlockSpec((None, tr, cols), lambda i, p: (p[0], i, 0))),
        out_shape=jax.ShapeDtypeStruct((N_CHIPS, rows, cols), BF16),
        compiler_params=_params(("parallel",)),
    )(place, w)


HBM = pl.BlockSpec(memory_space=pltpu.HBM)
SEM = pl.BlockSpec(memory_space=pltpu.SEMAPHORE)
DATAFLOW = pltpu.SideEffectType.DATAFLOW_SIDE_EFFECTING


def _in_hbm(a):
    return pltpu.with_memory_space_constraint(a, pltpu.HBM)


ALL_PEERS = (0, 1, 2)
NEIGHBOURS = (0, 1)
DIAGONAL = (2,)


def _chip_copies_start(name, srcs, lands, make_copy, after=None, peers=ALL_PEERS):
    n_t = len(srcs)
    in_place = lands is None
    n_after = 0 if after is None else 1

    def body(*refs):
        src = refs[:n_t]
        k = (n_t if in_place else 2 * n_t) + n_after
        send_sems, recv_sems = refs[k], refs[k + 1]
        land = refs[k + 2:k + 2 + n_t] if in_place else refs[k + 2 + n_t:k + 2 + 2 * n_t]
        token = refs[-1]
        x, y, c = _place()
        me = 2 * x + y
        others = _other_chips(x, y)
        for t in range(n_t):
            for px, py, pk in [others[p] for p in peers]:
                s, d = make_copy(src[t], land[t], me, pk, c)
                pltpu.make_async_remote_copy(
                    src_ref=s, dst_ref=d, send_sem=send_sems.at[t], recv_sem=recv_sems.at[t],
                    device_id=(px, py, c), device_id_type=MESH).start()
        token[...] = jnp.zeros_like(token)

    bufs = list(srcs) if in_place else list(srcs) + list(lands)
    outs = pl.pallas_call(
        body, name=name,
        in_specs=[HBM] * len(bufs) + [ANY] * n_after,
        out_specs=[SEM, SEM] + [HBM] * len(bufs) + [pl.BlockSpec(memory_space=pltpu.VMEM)],
        out_shape=[pltpu.SemaphoreType.DMA((n_t,)), pltpu.SemaphoreType.DMA((n_t,))]
        + [pltpu.HBM(b.shape, b.dtype) for b in bufs] + [jax.ShapeDtypeStruct((8, 128), F32)],
        input_output_aliases={i: 2 + i for i in range(len(bufs))},
        compiler_params=pltpu.CompilerParams(has_side_effects=DATAFLOW),
    )(*[_in_hbm(b) for b in bufs], *([] if after is None else [after]))
    return outs[0], outs[1], list(outs[2:2 + len(bufs)]), outs[-1]


def _chip_copies_wait(name, send_sems, recv_sems, bufs, sent, landed, after):
    n_b = len(bufs)

    def body(*refs):
        buf = refs[:n_b]
        send_ref, recv_ref = refs[n_b], refs[n_b + 1]
        x, y, c = _place()
        for t, (s, d) in enumerate(zip(sent(buf), landed(buf))):
            out = pltpu.make_async_remote_copy(src_ref=s, dst_ref=s, send_sem=send_ref.at[t], recv_sem=recv_ref.at[t],
                                               device_id=(x, y, c), device_id_type=MESH)
            out.wait_send()
            arrived = pltpu.make_async_remote_copy(src_ref=d, dst_ref=d, send_sem=send_ref.at[t],
                                                   recv_sem=recv_ref.at[t], device_id=(x, y, c), device_id_type=MESH)
            arrived.wait_recv()

    after = list(after) if isinstance(after, (list, tuple)) else [after]
    return pl.pallas_call(
        body, name=name,
        in_specs=[HBM] * n_b + [SEM, SEM] + [ANY] * len(after), out_specs=[HBM] * n_b,
        out_shape=[pltpu.HBM(b.shape, b.dtype) for b in bufs],
        input_output_aliases={i: i for i in range(n_b)},
        compiler_params=pltpu.CompilerParams(has_side_effects=DATAFLOW),
    )(*bufs, send_sems, recv_sems, *after)


def _gather_start(name, bufs, after=None, peers=ALL_PEERS):
    def make_copy(src, land, me, pk, c):
        hr = src.shape[1] // 2
        return src.at[me, pl.ds(c * hr, hr)], land.at[me, pl.ds(c * hr, hr)]

    return _chip_copies_start(name, bufs, None, make_copy, after, peers)


def _gather_wait(name, send_sems, recv_sems, bufs, after, peers=ALL_PEERS):
    def half_shards(buf):
        return [b.at[pl.ds(0, len(peers)), pl.ds(0, b.shape[1] // 2)] for b in buf]

    return _chip_copies_wait(name, send_sems, recv_sems, bufs, half_shards, half_shards, after)


def _gather_forward_start(name, bufs, peers=ALL_PEERS):
    n_t = len(bufs)

    def body(*refs):
        mine = refs[:n_t]
        send_sems, recv_sems = refs[n_t], refs[n_t + 1]
        buf = refs[n_t + 2:2 * n_t + 2]
        token = refs[-1]
        x, y, c = _place()
        others = _other_chips(x, y)
        for q in range(D2D_CHUNKS):
            for t in range(n_t):
                hr = mine[t].shape[1] // 2
                cr = hr // D2D_CHUNKS
                rows = pl.ds(c * hr + q * cr, cr)
                for _, _, pk in [others[p] for p in peers]:
                    pltpu.make_async_remote_copy(
                        src_ref=mine[t].at[pk, rows], dst_ref=buf[t].at[pk, rows], send_sem=send_sems.at[t],
                        recv_sem=recv_sems.at[t], device_id=(x, y, 1 - c), device_id_type=MESH).start()
        token[...] = jnp.zeros_like(token)

    outs = pl.pallas_call(
        body, name=name,
        in_specs=[HBM] * n_t,
        out_specs=[SEM, SEM] + [HBM] * n_t + [pl.BlockSpec(memory_space=pltpu.VMEM)],
        out_shape=[pltpu.SemaphoreType.DMA((n_t,)), pltpu.SemaphoreType.DMA((n_t,))]
        + [pltpu.HBM(b.shape, b.dtype) for b in bufs] + [jax.ShapeDtypeStruct((8, 128), F32)],
        input_output_aliases={i: 2 + i for i in range(n_t)},
        compiler_params=pltpu.CompilerParams(has_side_effects=DATAFLOW),
    )(*[_in_hbm(b) for b in bufs])
    return outs[0], outs[1], list(outs[2:2 + n_t]), outs[-1]


def _core_exchange_start(name, grads):
    n_t = len(grads)
    lands = [lax.empty(g.shape, g.dtype) for g in grads]

    def body(*refs):
        src = refs[:n_t]
        send_sems, recv_sems = refs[2 * n_t], refs[2 * n_t + 1]
        land = refs[2 * n_t + 2 + n_t:2 * n_t + 2 + 2 * n_t]
        token = refs[-1]
        x, y, c = _place()
        for q in range(D2D_CHUNKS):
            for t in range(n_t):
                cr = src[t].shape[1] // D2D_CHUNKS
                rows = pl.ds(q * cr, cr)
                pltpu.make_async_remote_copy(
                    src_ref=src[t].at[:, rows], dst_ref=land[t].at[:, rows], send_sem=send_sems.at[t],
                    recv_sem=recv_sems.at[t], device_id=(x, y, 1 - c), device_id_type=MESH).start()
        token[...] = jnp.zeros_like(token)

    bufs = list(grads) + lands
    outs = pl.pallas_call(
        body, name=name,
        in_specs=[HBM] * len(bufs),
        out_specs=[SEM, SEM] + [HBM] * len(bufs) + [pl.BlockSpec(memory_space=pltpu.VMEM)],
        out_shape=[pltpu.SemaphoreType.DMA((n_t,)), pltpu.SemaphoreType.DMA((n_t,))]
        + [pltpu.HBM(b.shape, b.dtype) for b in bufs] + [jax.ShapeDtypeStruct((8, 128), F32)],
        input_output_aliases={i: 2 + i for i in range(len(bufs))},
        compiler_params=pltpu.CompilerParams(has_side_effects=DATAFLOW),
    )(*[_in_hbm(b) for b in bufs])
    return outs[0], outs[1], list(outs[2:2 + len(bufs)]), outs[-1]


def _core_exchange_wait(name, send_sems, recv_sems, bufs, after):
    n_t = len(bufs) // 2

    def body(*refs):
        land = refs[n_t:2 * n_t]
        send_ref, recv_ref = refs[2 * n_t], refs[2 * n_t + 1]
        x, y, c = _place()
        for t in range(n_t):
            whole = pltpu.make_async_remote_copy(src_ref=land[t], dst_ref=land[t], send_sem=send_ref.at[t],
                                                 recv_sem=recv_ref.at[t], device_id=(x, y, c), device_id_type=MESH)
            whole.wait_send()
            whole.wait_recv()

    outs = pl.pallas_call(
        body, name=name,
        in_specs=[HBM] * (2 * n_t) + [SEM, SEM, ANY], out_specs=[HBM] * (2 * n_t),
        out_shape=[pltpu.HBM(b.shape, b.dtype) for b in bufs],
        input_output_aliases={i: i for i in range(2 * n_t)},
        compiler_params=pltpu.CompilerParams(has_side_effects=DATAFLOW),
    )(*bufs, send_sems, recv_sems, after)
    return list(outs[:n_t]), list(outs[n_t:])


def _chip_exchange_start(name, parts):
    lands = [lax.empty(p.shape, p.dtype) for p in parts]
    return _chip_copies_start(name, parts, lands, lambda src, land, me, pk, c: (src.at[pk], land.at[me]))


def _chip_exchange_wait(name, send_sems, recv_sems, bufs, after):
    n_t = len(bufs) // 2
    return _chip_copies_wait(name, send_sems, recv_sems, bufs,
                             lambda buf: [b.at[pl.ds(0, 3)] for b in buf[:n_t]],
                             lambda buf: [b.at[pl.ds(0, 3)] for b in buf[n_t:]], after)


def _sum_chips(name, parts, landed, place, l, stacked):
    chips, rows, cols = landed.shape
    tr = min(256, rows)
    per = rows // tr

    def body(p_ref, own_ref, *refs):
        land, o_ref = refs[:chips], refs[-1]
        tot = None
        for k in range(chips):
            term = jnp.where(p_ref[0] == k, own_ref[...], land[k][...]).astype(F32)
            tot = term if tot is None else tot + term
        o_ref[...] = tot

    def from_chip(k):
        return pl.BlockSpec((None, tr, cols), lambda i, p: (jnp.where(p[0] == k, (k + 1) % chips, k), i, 0))

    in_specs = [pl.BlockSpec((None, tr, cols), lambda i, p: (p[0], i, 0))] + [from_chip(k) for k in range(chips)]
    args = [parts] + [landed] * chips
    aliases = {}
    if stacked is not None:
        in_specs.append(ANY)
        args.append(stacked)
        aliases = {len(args): 0}
    return pl.pallas_call(
        body, name=name,
        grid_spec=pltpu.PrefetchScalarGridSpec(
            num_scalar_prefetch=1, grid=(per,), in_specs=in_specs,
            out_specs=pl.BlockSpec((None, tr, cols), lambda i, p: (l, p[1] * per + i, 0))),
        out_shape=jax.ShapeDtypeStruct((DEPTH, 2 * rows, cols), F32), input_output_aliases=aliases,
        compiler_params=_params(("parallel",)),
    )(place, *args)


def _core_share_start(name, bufs, l):
    n_t = len(bufs)

    def body(*refs):
        mine = refs[:n_t]
        send_sems, recv_sems = refs[n_t], refs[n_t + 1]
        buf = refs[n_t + 2:2 * n_t + 2]
        token = refs[-1]
        x, y, c = _place()
        for q in range(D2D_CHUNKS):
            for t in range(n_t):
                hr = mine[t].shape[1] // 2
                cr = hr // D2D_CHUNKS
                rows = pl.ds(c * hr + q * cr, cr)
                pltpu.make_async_remote_copy(
                    src_ref=mine[t].at[l, rows], dst_ref=buf[t].at[l, rows], send_sem=send_sems.at[t],
                    recv_sem=recv_sems.at[t], device_id=(x, y, 1 - c), device_id_type=MESH).start()
        token[...] = jnp.zeros_like(token)

    outs = pl.pallas_call(
        body, name=name,
        in_specs=[HBM] * n_t,
        out_specs=[SEM, SEM] + [HBM] * n_t + [pl.BlockSpec(memory_space=pltpu.VMEM)],
        out_shape=[pltpu.SemaphoreType.DMA((n_t,)), pltpu.SemaphoreType.DMA((n_t,))]
        + [pltpu.HBM(b.shape, b.dtype) for b in bufs] + [jax.ShapeDtypeStruct((8, 128), F32)],
        input_output_aliases={i: 2 + i for i in range(n_t)},
        compiler_params=pltpu.CompilerParams(has_side_effects=DATAFLOW),
    )(*[_in_hbm(b) for b in bufs])
    return outs[0], outs[1], list(outs[2:2 + n_t]), outs[-1]


def _core_share_wait(name, send_sems, recv_sems, bufs, l, after):
    def half_layer(buf):
        return [b.at[l, pl.ds(0, b.shape[1] // 2)] for b in buf]

    return _chip_copies_wait(name, send_sems, recv_sems, bufs, half_layer, half_layer, after)


def _all_reduce_small(vec, after=None):
    rows, lanes = vec.shape
    hr = rows // 2

    def body(v_ref, *refs):
        o_ref, sib_ref, chips_ref, send_sems, recv_sems = refs[-5:]
        x, y, c = _place()
        me = 2 * x + y
        sibling = (x, y, 1 - c)
        mine = pl.ds(pl.multiple_of(c * hr, 8), hr)
        theirs = pl.ds(pl.multiple_of((1 - c) * hr, 8), hr)
        swap = pltpu.make_async_remote_copy(
            src_ref=v_ref.at[theirs], dst_ref=sib_ref, send_sem=send_sems.at[0], recv_sem=recv_sems.at[0],
            device_id=sibling, device_id_type=MESH)
        swap.start()
        swap.wait_recv()
        chips_ref[me] = v_ref[mine] + sib_ref[...]
        copies = []
        for j, (px, py, pk) in enumerate(_other_chips(x, y)):
            cp = pltpu.make_async_remote_copy(
                src_ref=chips_ref.at[me], dst_ref=chips_ref.at[me], send_sem=send_sems.at[1 + j],
                recv_sem=recv_sems.at[1 + j], device_id=(px, py, c), device_id_type=MESH)
            cp.start()
            copies.append(cp)
        for j, (px, py, pk) in enumerate(_other_chips(x, y)):
            pltpu.make_async_remote_copy(
                src_ref=chips_ref.at[pk], dst_ref=chips_ref.at[pk], send_sem=send_sems.at[1 + j],
                recv_sem=recv_sems.at[1 + j], device_id=(px, py, c), device_id_type=MESH).wait_recv()
        tot = chips_ref[0]
        for k in range(1, N_CHIPS):
            tot = tot + chips_ref[k]
        o_ref[mine] = tot
        share = pltpu.make_async_remote_copy(
            src_ref=o_ref.at[mine], dst_ref=o_ref.at[mine], send_sem=send_sems.at[4], recv_sem=recv_sems.at[4],
            device_id=sibling, device_id_type=MESH)
        share.start()
        pltpu.make_async_remote_copy(
            src_ref=o_ref.at[theirs], dst_ref=o_ref.at[theirs], send_sem=send_sems.at[4], recv_sem=recv_sems.at[4],
            device_id=sibling, device_id_type=MESH).wait_recv()
        swap.wait_send()
        for cp in copies:
            cp.wait_send()
        share.wait_send()

    vm = pl.BlockSpec(memory_space=pltpu.VMEM)
    return pl.pallas_call(
        body, name="small_all_reduce", in_specs=[vm] + ([] if after is None else [ANY]), out_specs=vm,
        out_shape=jax.ShapeDtypeStruct((rows, lanes), F32),
        scratch_shapes=[pltpu.VMEM((hr, lanes), F32), pltpu.VMEM((N_CHIPS, hr, lanes), F32),
                        pltpu.SemaphoreType.DMA((5,)), pltpu.SemaphoreType.DMA((5,))],
        compiler_params=pltpu.CompilerParams(has_side_effects=True, vmem_limit_bytes=48 * MIB),
    )(vec, *([] if after is None else [after]))


def _adamw(name, w, g, m, v, place, l=0, half=None, done=None, after=None):
    layers, rows, cols = w.shape
    span = rows if half is None else rows // 2
    tr = span
    for cand in (256, 128, 64, 32, 16, 8):
        if span % cand == 0:
            tr = cand
            break
    per = span // tr
    c1 = 1.0 - ADAM_B1 ** ADAM_STEP
    c2 = 1.0 - ADAM_B2 ** ADAM_STEP

    def first_block(p):
        return 0 if half is None else (p[1] if half == "own" else 1 - p[1]) * per

    def body(p_ref, w_ref, g_ref, m_ref, v_ref, *refs):
        go_ref, d_ref, nm_ref, nv_ref = refs[-4:]
        gv = g_ref[...]
        nm = ADAM_B1 * m_ref[...] + (1.0 - ADAM_B1) * gv
        nv = ADAM_B2 * v_ref[...] + (1.0 - ADAM_B2) * (gv * gv)
        go_ref[...] = gv
        nm_ref[...] = nm
        nv_ref[...] = nv
        d_ref[...] = -ADAM_LR * ((nm / c1) / (jnp.sqrt(nv / c2) + ADAM_EPS) + ADAM_WD * w_ref[...])

    blk = pl.BlockSpec((None, tr, cols), lambda i, p: (l, first_block(p) + i, 0))
    out = jax.ShapeDtypeStruct((layers, rows, cols), F32)
    extra = ([] if done is None else list(done)) + ([] if after is None else [after])
    aliases = {} if done is None else {5 + i: i for i in range(4)}
    return pl.pallas_call(
        body, name=name,
        grid_spec=pltpu.PrefetchScalarGridSpec(
            num_scalar_prefetch=1, grid=(per,), in_specs=[blk] * 4 + [ANY] * len(extra), out_specs=[blk] * 4),
        out_shape=[out] * 4, input_output_aliases=aliases,
        compiler_params=_params(("parallel",)),
    )(place, w, g, m, v, *extra)


LANES = 128
SUBLANES = 8
SMALL_SHAPES = {
    "norm_g": (DEPTH, D_MODEL), "sgu_ln_g": (DEPTH, D_A), "sgu_ln_b": (DEPTH, D_A),
    "sgu_w": (DEPTH, A_GROUPS, CHUNK, CHUNK), "sgu_b": (DEPTH, A_GROUPS, CHUNK), "mem_norm_g": (DEPTH, D_MODEL),
    "q_norm_g": (DEPTH, HEAD_DIM), "k_norm_g": (DEPTH, HEAD_DIM)}


def _small_layout():
    at, off = {}, 0
    for k in SMALL_NAMES:
        n = math.prod(SMALL_SHAPES[k]) // LANES
        at[k] = (off, n)
        off += -(-n // SUBLANES) * SUBLANES
    return at, off, -(-(off + SUBLANES) // (2 * SUBLANES)) * 2 * SUBLANES


def _pack_small(parts, loss=None):
    at, loss_row, rows = _small_layout()
    pieces = []
    for k in SMALL_NAMES:
        n = at[k][1]
        pieces.append(jnp.pad(parts[k].reshape(n, LANES), ((0, -(-n // SUBLANES) * SUBLANES - n), (0, 0))))
    tile = jnp.zeros((SUBLANES, LANES), F32) if loss is None else jnp.broadcast_to(loss.reshape(1, 1), (SUBLANES, LANES))
    pieces += [tile, jnp.zeros((rows - loss_row - SUBLANES, LANES), F32)]
    return jnp.concatenate(pieces)


def _adamw_small(w, g, m, v):
    at, _, rows = _small_layout()
    c1 = 1.0 - ADAM_B1 ** ADAM_STEP
    c2 = 1.0 - ADAM_B2 ** ADAM_STEP
    n_names = len(SMALL_NAMES)

    def body(w_ref, g_ref, m_ref, v_ref, *refs):
        outs, (d_ref, nm_ref, nv_ref) = refs[:4 * n_names], refs[4 * n_names:]
        gv = g_ref[...]
        nm = ADAM_B1 * m_ref[...] + (1.0 - ADAM_B1) * gv
        nv = ADAM_B2 * v_ref[...] + (1.0 - ADAM_B2) * (gv * gv)
        nm_ref[...] = nm
        nv_ref[...] = nv
        d_ref[...] = -ADAM_LR * ((nm / c1) / (jnp.sqrt(nv / c2) + ADAM_EPS) + ADAM_WD * w_ref[...])
        for kind, src in enumerate((g_ref, d_ref, nm_ref, nv_ref)):
            for i, k in enumerate(SMALL_NAMES):
                o_ref = outs[kind * n_names + i]
                first, n = at[k]
                shape = SMALL_SHAPES[k]
                if shape[-1] == LANES:
                    o_ref[...] = src[pl.ds(first, n), :].reshape(shape)
                else:
                    per = shape[-1] // LANES
                    for r in range(n):
                        o_ref[pl.ds(r // per, 1), pl.ds((r % per) * LANES, LANES)] = src[pl.ds(first + r, 1), :]

    out_shape = [jax.ShapeDtypeStruct(SMALL_SHAPES[k], F32) for _ in range(4) for k in SMALL_NAMES]
    outs = pl.pallas_call(
        body, name="adamw_small", out_shape=out_shape,
        scratch_shapes=[pltpu.VMEM((rows, LANES), F32)] * 3, compiler_params=_params(None),
    )(w, g, m, v)
    return [dict(zip(SMALL_NAMES, outs[kind * n_names:(kind + 1) * n_names])) for kind in range(4)]


WEIGHT_ORDER = ("norm_g", "w_in", "sgu_ln_g", "sgu_ln_b", "sgu_w", "sgu_b", "mem_norm_g", "w_mem_kv", "q_norm_g",
                "k_norm_g", "w_out")


def kernel(x, mem, norm_g, w_in, sgu_ln_g, sgu_ln_b, sgu_w, sgu_b, mem_norm_g, w_mem_kv, q_norm_g, k_norm_g, w_out, loss_target, m_norm_g, m_w_in, m_sgu_ln_g, m_sgu_ln_b, m_sgu_w, m_sgu_b, m_mem_norm_g, m_w_mem_kv, m_q_norm_g, m_k_norm_g, m_w_out, v_norm_g, v_w_in, v_sgu_ln_g, v_sgu_ln_b, v_sgu_w, v_sgu_b, v_mem_norm_g, v_w_mem_kv, v_q_norm_g, v_k_norm_g, v_w_out):
    weights = dict(norm_g=norm_g, w_in=w_in, sgu_ln_g=sgu_ln_g, sgu_ln_b=sgu_ln_b, sgu_w=sgu_w, sgu_b=sgu_b,
                   mem_norm_g=mem_norm_g, w_mem_kv=w_mem_kv, q_norm_g=q_norm_g, k_norm_g=k_norm_g, w_out=w_out)
    mom_m = dict(norm_g=m_norm_g, w_in=m_w_in, sgu_ln_g=m_sgu_ln_g, sgu_ln_b=m_sgu_ln_b, sgu_w=m_sgu_w, sgu_b=m_sgu_b,
                 mem_norm_g=m_mem_norm_g, w_mem_kv=m_w_mem_kv, q_norm_g=m_q_norm_g, k_norm_g=m_k_norm_g, w_out=m_w_out)
    mom_v = dict(norm_g=v_norm_g, w_in=v_w_in, sgu_ln_g=v_sgu_ln_g, sgu_ln_b=v_sgu_ln_b, sgu_w=v_sgu_w, sgu_b=v_sgu_b,
                 mem_norm_g=v_mem_norm_g, w_mem_kv=v_w_mem_kv, q_norm_g=v_q_norm_g, k_norm_g=v_k_norm_g, w_out=v_w_out)
    big = ("w_in", "w_mem_kv", "w_out")
    sm = {k: weights[k] for k in SMALL_NAMES}

    place = _place_index()
    xs, mems, target = x[0], mem[0], loss_target[0]

    slots = [[_cast_into_slot(f"cast_{k}_{l}", weights[k], l, place) for k in big] for l in range(DEPTH)]
    saved = [None] * DEPTH

    chips, cores = {}, {}
    me = place[0]
    arrival = jnp.stack([me, me ^ 2, me ^ 1, 3 - me]).astype(jnp.int32)
    shard_order = jnp.arange(N_CHIPS, dtype=jnp.int32)

    def start_gather(l, after=None):
        chips[l, "in"] = _gather_start(f"gather_start_{l}_in", slots[l][:1], after)
        chips[l, "rest"] = _gather_start(f"gather_start_{l}_rest", slots[l][1:], chips[l, "in"][3])
        return chips[l, "rest"][3]

    def hand_to_sibling(l, group, after):
        send_sems, recv_sems, bufs, _ = chips[l, group]
        bufs = _gather_wait(f"gather_wait_{l}_{group}", send_sems, recv_sems, bufs, after)
        cores[l, group] = _gather_forward_start(f"gather_forward_{l}_{group}", bufs)
        return cores[l, group][3]

    def whole(l, group, after):
        send_sems, recv_sems, bufs, _ = cores[l, group]
        return _gather_wait(f"gather_whole_{l}_{group}", send_sems, recv_sems, bufs, after)

    later_slots = [s for layer in slots[1:] for s in layer]

    class Gathered:
        def __init__(self, l):
            self.l = l
            self.buf = None

        def landed_from(self, tag, peers, after, behind, then=None):
            send_sems, recv_sems, _, _ = chips[0, "in_" + tag]
            buf = _gather_wait(f"gather_wait_0_in_{tag}", send_sems, recv_sems, self.buf, after, peers)
            if then is not None:
                buf, more = then(buf)
                behind = behind + more
            send_sems, recv_sems, buf, token = _gather_forward_start(f"gather_forward_0_in_{tag}", buf, peers)
            self.buf = _gather_wait(f"gather_whole_0_in_{tag}", send_sems, recv_sems, buf, [token] + behind, peers)

        def w_in(self, stage, h, proj):
            if self.l > 0:
                return (whole(self.l, "in", h)[0], shard_order, 0, N_CHIPS) if stage == 0 else None
            if stage == 0:
                self.buf = chips[0, "in_n"][2]
                return self.buf[0], arrival, 0, 1
            if stage == 1:
                def start_others(buf):
                    chips[0, "in_d"] = _gather_start("gather_start_0_in_d", buf, None, DIAGONAL)
                    chips[0, "rest"] = _gather_start("gather_start_0_rest", slots[0][1:], chips[0, "in_d"][3])
                    return chips[0, "in_d"][2], [chips[0, "rest"][3]]

                self.landed_from("n", NEIGHBOURS, proj, later_slots + [chips[0, "in_n"][3]], start_others)
                return self.buf[0], arrival, 1, 2
            if stage == 2:
                self.landed_from("d", DIAGONAL, [proj, chips[0, "rest"][3]], [])
                return self.buf[0], arrival, 3, 1
            return None

        def rest_start(self, proj):
            token = proj if self.l == 0 else hand_to_sibling(self.l, "rest", proj)
            return start_gather(self.l + 1, token) if self.l + 1 < DEPTH else token

        def rest_finish(self, o_b):
            if self.l == 0:
                o_b = hand_to_sibling(self.l, "rest", o_b)
            w_kv_all, w_out_all = whole(self.l, "rest", o_b)
            return w_kv_all, w_out_all, None

        def before_out(self, y):
            return hand_to_sibling(self.l + 1, "in", y) if self.l + 1 < DEPTH else None

    chips[0, "in_n"] = _gather_start("gather_start_0_in_n", slots[0][:1], None, NEIGHBOURS)
    cur = xs
    for l in range(DEPTH):
        cur, saved[l] = _layer_fwd(l, cur, mems, sm, Gathered(l), target if l == DEPTH - 1 else None)
    dxo, dxo_b, loss_part = cur

    small_g = [None] * DEPTH
    flight = {}

    class Exchange:
        def __init__(self):
            self.cores = {}

        def start(self, l, group, gives):
            *self.cores[l, group], token = _core_exchange_start(f"grad_core_start_{l}_{group}", gives)
            return token

        def landed(self, l, group, after):
            send_sems, recv_sems, bufs = self.cores[l, group]
            return _core_exchange_wait(f"grad_core_wait_{l}_{group}", send_sems, recv_sems, bufs, after)[1]

        def send(self, l, group, parts):
            *flight[l, group], token = _chip_exchange_start(f"grad_chip_start_{l}_{group}", parts)
            return token

    exchange = Exchange()
    for l in reversed(range(DEPTH)):
        dxo, dxo_b, small_g[l] = _layer_bwd(l, dxo, dxo_b, mems, sm, saved[l], place, exchange)
    grad_x = dxo

    groups = (("out", ("w_out",)), ("in", ("w_in", "w_mem_kv")))
    halves, stepped = dict.fromkeys(big), dict.fromkeys(big)
    small_g = {k: jnp.stack([small_g[l][k] for l in range(DEPTH)]) for k in SMALL_NAMES}
    after = grad_x
    sharing = {}

    def reduce_group(l, group, names):
        nonlocal after
        send_sems, recv_sems, bufs = flight[l, group]
        bufs = _chip_exchange_wait(f"grad_chip_wait_{l}_{group}", send_sems, recv_sems, bufs, after)
        for t, k in enumerate(names):
            halves[k] = _sum_chips(f"grad_chip_sum_{l}_{k}", bufs[t], bufs[len(names) + t], place, l, halves[k])
        *sharing[l, group], after = _core_share_start(f"grad_core_share_{l}_{group}", [halves[k] for k in names], l)

    def step(l, k, buf, half):
        nonlocal after
        tag = "" if half is None else "_" + half
        stepped[k] = _adamw(f"adamw_{k}_{l}{tag}", weights[k], buf, mom_m[k], mom_v[k], place, l, half, stepped[k],
                            after)
        after = stepped[k][1]

    def step_group(l, group, names, overlap):
        nonlocal after
        send_sems, recv_sems, bufs = sharing[l, group]
        if overlap:
            for k, buf in zip(names, bufs):
                step(l, k, buf, "own")
        bufs = _core_share_wait(f"grad_core_shared_{l}_{group}", send_sems, recv_sems, bufs, l, after)
        for k, buf in zip(names, bufs):
            halves[k] = buf
            step(l, k, buf, "other" if overlap else None)

    for l in reversed(range(DEPTH)):
        last = l == 0
        (g_out, n_out), (g_in, n_in) = groups
        reduce_group(l, g_out, n_out)
        if last:
            step_group(l, g_out, n_out, False)
            small_sum = _all_reduce_small(_pack_small(small_g, loss_part), after)
            small_step = _adamw_small(_pack_small(sm), small_sum, _pack_small({k: mom_m[k] for k in SMALL_NAMES}),
                                      _pack_small({k: mom_v[k] for k in SMALL_NAMES}))
            after = small_step[1]["sgu_w"]
        reduce_group(l, g_in, n_in)
        if not last:
            step_group(l, g_out, n_out, False)
        step_group(l, g_in, n_in, last)

    grads, delta, new_m, new_v = ({k: stepped[k][i] for k in big} for i in range(4))
    for out, small in zip((grads, delta, new_m, new_v), small_step):
        out.update(small)
    loss = small_sum[_small_layout()[1], 0]
    return (loss, grad_x[None], *[grads[k] for k in WEIGHT_ORDER], *[delta[k] for k in WEIGHT_ORDER],
            *[new_m[k] for k in WEIGHT_ORDER], *[new_v[k] for k in WEIGHT_ORDER])
```

```python
import math

import jax
import jax.numpy as jnp
from jax import lax
from jax.experimental import pallas as pl
from jax.experimental.pallas import tpu as pltpu

F32 = jnp.float32
BF16 = jnp.bfloat16
MESH = pl.DeviceIdType.MESH

D_MODEL = 2048
DEPTH = 2
CHUNK = 128
D_A = 1024
A_GROUPS = 8
D_B = 512
D_C = 512
HEADS = 4
HEAD_DIM = 128
IN_WIDTH = 6144
N_CHIPS = 4
EPS = 1e-6
ATT_SCALE = 1.0 / math.sqrt(HEAD_DIM)

OFF_U, OFF_V, OFF_ZA = 0, 1024, 2048
OFF_QB, OFF_KB, OFF_VB, OFF_ZB = 3072, 3584, 4096, 4608
OFF_QC, OFF_ZC = 5120, 5632
OFF_YB, OFF_YC = 1024, 1536

ADAM_LR = 0.001
ADAM_B1 = 0.9
ADAM_B2 = 0.999
ADAM_EPS = 1e-08
ADAM_WD = 0.01
ADAM_STEP = 10

MIB = 1024 * 1024
ANY = pl.BlockSpec(memory_space=pl.ANY)


def _params(semantics=None, vmem_mb=48):
    return pltpu.CompilerParams(dimension_semantics=semantics, vmem_limit_bytes=vmem_mb * MIB)


def _gelu(x):
    return 0.5 * x * (1.0 + lax.erf(x * (1.0 / math.sqrt(2.0))))


def _gelu_grad(x):
    cdf = 0.5 * (1.0 + lax.erf(x * (1.0 / math.sqrt(2.0))))
    pdf = jnp.exp(-0.5 * x * x) * (1.0 / math.sqrt(2.0 * math.pi))
    return cdf + x * pdf


def _sigmoid(x):
    return 1.0 / (1.0 + jnp.exp(-x))


def _silu_and_grad(z):
    s = _sigmoid(z)
    return z * s, s * (1.0 + z * (1.0 - s))


def _split_bf16(x):
    hi = x.astype(BF16)
    lo = (x - hi.astype(F32)).astype(BF16)
    return hi, lo


def _dot(a, b, dims):
    return lax.dot_general(a, b, (dims, ((), ())), preferred_element_type=F32)


NN = ((1,), (0,))
NT = ((1,), (1,))
TN = ((0,), (0,))


def _matmul(name, a, b, *, grid, a_spec, b_spec, o_spec, out_shape, dims, res=None, res_spec=None, after=None,
            place=None, into=None, vmem_mb=48):
    nk = grid[2]
    n_in = 2 + (res is not None) + (after is not None) + (into is not None)

    def body(*refs):
        if place is not None:
            refs = refs[1:]
        a_ref, b_ref = refs[0], refs[1]
        r_ref = refs[2] if res is not None else None
        o_ref = refs[n_in]
        if len(b_ref.shape) == 3 and dims == NN:
            part = _dot(a_ref[...], b_ref[...].reshape(-1, b_ref.shape[-1]), dims)
        elif len(b_ref.shape) == 3:
            width = b_ref.shape[-1]
            part = None
            for s in range(b_ref.shape[0]):
                term = _dot(a_ref[:, s * width:(s + 1) * width], b_ref[s], dims)
                part = term if part is None else part + term
        else:
            part = _dot(a_ref[...], b_ref[...], dims)
        if nk == 1:
            if r_ref is not None:
                part = part + r_ref[...]
            o_ref[...] = part.astype(o_ref.dtype)
            return
        acc_ref = refs[n_in + 1]
        k = pl.program_id(2)

        @pl.when(k == 0)
        def _():
            acc_ref[...] = part

        @pl.when(k > 0)
        def _():
            acc_ref[...] += part

        @pl.when(k == nk - 1)
        def _():
            tot = acc_ref[...]
            if r_ref is not None:
                tot = tot + r_ref[...]
            o_ref[...] = tot.astype(o_ref.dtype)

    in_specs = [a_spec, b_spec]
    args = [a, b]
    if res is not None:
        in_specs.append(res_spec)
        args.append(res)
    if after is not None:
        in_specs.append(ANY)
        args.append(after)
    aliases = {}
    if into is not None:
        in_specs.append(ANY)
        args.append(into)
        aliases = {len(args) - 1 + (place is not None): 0}
    acc_shape = tuple(d for d in o_spec.block_shape if d is not None)
    scratch = [pltpu.VMEM(acc_shape, F32)] if nk > 1 else []
    params = _params(("parallel", "parallel", "arbitrary"), vmem_mb)
    if place is not None:
        return pl.pallas_call(
            body, name=name, out_shape=out_shape, compiler_params=params, input_output_aliases=aliases,
            grid_spec=pltpu.PrefetchScalarGridSpec(num_scalar_prefetch=1, grid=grid, in_specs=in_specs,
                                                   out_specs=o_spec, scratch_shapes=scratch),
        )(place, *args)
    return pl.pallas_call(
        body, name=name, grid=grid, in_specs=in_specs, out_specs=o_spec, out_shape=out_shape,
        scratch_shapes=scratch, compiler_params=params, input_output_aliases=aliases,
    )(*args)


def _rms_fwd(name, x, g, tr, after=None, transposed=False):
    rows, d = x.shape

    def body(x_ref, g_ref, *refs):
        outs = refs[1:] if after is not None else refs
        xv = x_ref[...]
        r = lax.rsqrt(jnp.mean(xv * xv, axis=-1, keepdims=True) + EPS)
        h = xv * r * g_ref[...]
        outs[0][...] = h.astype(BF16)
        if transposed:
            outs[1][...] = h.T.astype(BF16)

    out_specs = [pl.BlockSpec((tr, d), lambda i: (i, 0))]
    out_shape = [jax.ShapeDtypeStruct((rows, d), BF16)]
    if transposed:
        out_specs.append(pl.BlockSpec((d, tr), lambda i: (0, i)))
        out_shape.append(jax.ShapeDtypeStruct((d, rows), BF16))
    outs = pl.pallas_call(
        body, name=name, grid=(rows // tr,),
        in_specs=[pl.BlockSpec((tr, d), lambda i: (i, 0)), pl.BlockSpec((1, d), lambda i: (0, 0))]
        + ([] if after is None else [ANY]),
        out_specs=out_specs, out_shape=out_shape,
        compiler_params=_params(("parallel",)),
    )(x, g, *([] if after is None else [after]))
    return outs if transposed else outs[0]


def _rms_bwd(name, x, dh, dres, g, tr, after=None):
    rows, d = x.shape

    def body(x_ref, dh_ref, dres_ref, g_ref, *refs):
        dx_ref, dxb_ref, dg_ref = refs[-3:]
        xv = x_ref[...]
        r = lax.rsqrt(jnp.mean(xv * xv, axis=-1, keepdims=True) + EPS)
        xhat = xv * r
        dhv = dh_ref[...]
        dxh = dhv * g_ref[...]
        dx = r * (dxh - xhat * jnp.mean(dxh * xhat, axis=-1, keepdims=True)) + dres_ref[...]
        dx_ref[...] = dx
        dxb_ref[...] = dx.astype(BF16)
        part = jnp.sum(dhv * xhat, axis=0, keepdims=True)

        @pl.when(pl.program_id(0) == 0)
        def _():
            dg_ref[...] = part

        @pl.when(pl.program_id(0) > 0)
        def _():
            dg_ref[...] += part

    blk = pl.BlockSpec((tr, d), lambda i: (i, 0))
    vec = pl.BlockSpec((1, d), lambda i: (0, 0))
    return pl.pallas_call(
        body, name=name, grid=(rows // tr,), in_specs=[blk, blk, blk, vec] + ([] if after is None else [ANY]),
        out_specs=[blk, blk, vec],
        out_shape=[jax.ShapeDtypeStruct((rows, d), F32), jax.ShapeDtypeStruct((rows, d), BF16),
                   jax.ShapeDtypeStruct((1, d), F32)],
        compiler_params=_params(("arbitrary",)),
    )(x, dh, dres, g, *([] if after is None else [after]))


def _rms_gain_grad(name, x, dh):
    rows, d = x.shape

    def body(x_ref, dh_ref, dg_ref):
        xv = x_ref[...]
        r = lax.rsqrt(jnp.mean(xv * xv, axis=-1, keepdims=True) + EPS)
        dg_ref[...] = jnp.sum(dh_ref[...] * xv * r, axis=0, keepdims=True)

    return pl.pallas_call(
        body, name=name, out_shape=jax.ShapeDtypeStruct((1, d), F32), compiler_params=_params(None),
    )(x, dh)


SB_T = 256
SB_HEADS = 4


LOG2E = 1.4426950408889634


def _sb_scores(q, kblk):
    z2 = _dot(q, kblk, NT) * (ATT_SCALE * LOG2E)
    e = jnp.exp2(-jnp.abs(z2))
    l1 = jnp.minimum(-z2, 0.0) - jnp.log2(1.0 + e)
    lb = l1 + z2
    return z2, e, lb, l1


def _sb_fwd(name, proj, after=None):
    s_len = proj.shape[0]
    t = SB_T
    nq = s_len // t

    def body(q_ref, k_ref, v_ref, *refs):
        o_ref = refs[-1]
        i = pl.program_id(1)
        row = lax.broadcasted_iota(jnp.int32, (t, t), 0)
        col = lax.broadcasted_iota(jnp.int32, (t, t), 1)
        causal = col < row
        after_mat = (row > col).astype(BF16)
        heads = [slice(hh * HEAD_DIM, (hh + 1) * HEAD_DIM) for hh in range(SB_HEADS)]
        q = [q_ref[:, sl].astype(BF16) for sl in heads]

        def tile(kb, state, masked):
            start = pl.multiple_of(kb * t, t)
            out = []
            for hh, sl in enumerate(heads):
                carry, acc = state[hh]
                kblk = k_ref[pl.ds(start, t), sl].astype(BF16)
                vblk = v_ref[pl.ds(start, t), sl].astype(BF16)
                _, _, lb, l1 = _sb_scores(q[hh], kblk)
                if masked:
                    l1 = jnp.where(causal, l1, 0.0)
                hi, lo = _split_bf16(l1)
                after = _dot(hi, after_mat, NN) + _dot(lo, after_mat, NN) + carry
                a = jnp.exp2(lb + after)
                if masked:
                    a = jnp.where(causal, a, 0.0)
                acc = acc + _dot(a.astype(BF16), vblk, NN)
                carry = carry + jnp.sum(l1, axis=-1, keepdims=True)
                out.append((carry, acc))
            return tuple(out)

        zero = (jnp.zeros((t, 1), F32), jnp.zeros((t, HEAD_DIM), F32))
        state = tile(i, (zero,) * SB_HEADS, True)
        state = lax.fori_loop(0, i, lambda n, st: tile(i - 1 - n, st, False), state)
        for hh, sl in enumerate(heads):
            o_ref[:, sl] = state[hh][1]

    cb = SB_HEADS * HEAD_DIM
    return pl.pallas_call(
        body, name=name, grid=(HEADS // SB_HEADS, nq),
        in_specs=[pl.BlockSpec((t, cb), lambda h, i: (i, OFF_QB // cb + h)),
                  pl.BlockSpec((s_len, cb), lambda h, i: (0, OFF_KB // cb + h)),
                  pl.BlockSpec((s_len, cb), lambda h, i: (0, OFF_VB // cb + h))] + ([] if after is None else [ANY]),
        out_specs=pl.BlockSpec((t, cb), lambda h, i: (i, h)),
        out_shape=jax.ShapeDtypeStruct((s_len, D_B), F32),
        compiler_params=_params(("parallel", "arbitrary")),
    )(proj, proj, proj, *([] if after is None else [after]))


def _sb_bwd(name, proj, dy, after=None):
    s_len = proj.shape[0]
    t = SB_T
    nq = s_len // t

    def body(q_ref, k_ref, v_ref, z_ref, dy_ref, *refs):
        dq_ref, dk_ref, dv_ref, a_ref, s_ref = refs[-5:]
        i = pl.program_id(1)

        @pl.when(i == 0)
        def _():
            dk_ref[...] = jnp.zeros_like(dk_ref)
            dv_ref[...] = jnp.zeros_like(dv_ref)

        heads = [slice(hh * HEAD_DIM, (hh + 1) * HEAD_DIM) for hh in range(SB_HEADS)]
        q = [q_ref[:, sl].astype(BF16) for sl in heads]
        silu_z, _ = _silu_and_grad(z_ref[...])
        do_all = dy_ref[...] * silu_z
        do_b = [do_all[:, sl].astype(BF16) for sl in heads]
        row = lax.broadcasted_iota(jnp.int32, (t, t), 0)
        col = lax.broadcasted_iota(jnp.int32, (t, t), 1)
        causal = col < row
        after_mat = (row > col).astype(BF16)
        before_mat = (row < col).astype(BF16)

        def weights(kb, carries, masked):
            start = pl.multiple_of(kb * t, t)
            out = []
            for hh, sl in enumerate(heads):
                kblk = k_ref[pl.ds(start, t), sl].astype(BF16)
                z, _, lb, l1 = _sb_scores(q[hh], kblk)
                if masked:
                    l1 = jnp.where(causal, l1, 0.0)
                hi, lo = _split_bf16(l1)
                after = _dot(hi, after_mat, NN) + _dot(lo, after_mat, NN) + carries[hh]
                a = jnp.exp2(lb + after)
                if masked:
                    a = jnp.where(causal, a, 0.0)
                a_ref[hh, kb] = a
                s_ref[hh, kb] = z
                out.append(carries[hh] + jnp.sum(l1, axis=-1, keepdims=True))
            return tuple(out)

        carries = weights(i, (jnp.zeros((t, 1), F32),) * SB_HEADS, True)
        lax.fori_loop(0, i, lambda n, c: weights(i - 1 - n, c, False), carries)

        def grads(kb, state, masked):
            start = pl.multiple_of(kb * t, t)
            out = []
            for hh, sl in enumerate(heads):
                carry, dq = state[hh]
                kblk = k_ref[pl.ds(start, t), sl].astype(BF16)
                vblk = v_ref[pl.ds(start, t), sl].astype(BF16)
                a = a_ref[hh, kb]
                z = s_ref[hh, kb]
                g = _dot(do_b[hh], vblk, NT) * a
                ghi, glo = _split_bf16(g)
                prefix = _dot(ghi, before_mat, NN) + _dot(glo, before_mat, NN) + carry
                e = jnp.exp2(-jnp.abs(z))
                inv = 1.0 / (1.0 + e)
                pos = z >= 0.0
                beta = jnp.where(pos, inv, e * inv)
                one_m_beta = jnp.where(pos, e * inv, inv)
                dz = (g * one_m_beta - prefix * beta) * ATT_SCALE
                if masked:
                    dz = jnp.where(causal, dz, 0.0)
                dz_b = dz.astype(BF16)
                dq = dq + _dot(dz_b, kblk, NN)
                dk_ref[pl.ds(start, t), sl] += _dot(dz_b, q[hh], TN)
                dv_ref[pl.ds(start, t), sl] += _dot(a.astype(BF16), do_b[hh], TN)
                out.append((carry + jnp.sum(g, axis=-1, keepdims=True), dq))
            return tuple(out)

        zero = (jnp.zeros((t, 1), F32), jnp.zeros((t, HEAD_DIM), F32))
        state = lax.fori_loop(0, i, lambda kb, st: grads(kb, st, False), (zero,) * SB_HEADS)
        state = grads(i, state, True)
        for hh, sl in enumerate(heads):
            dq_ref[:, sl] = state[hh][1]

    cb = SB_HEADS * HEAD_DIM
    qblk = lambda off: pl.BlockSpec((t, cb), lambda h, i: (i, off // cb + h))
    full = lambda off: pl.BlockSpec((s_len, cb), lambda h, i: (0, off // cb + h))
    out = jax.ShapeDtypeStruct((s_len, D_B), F32)
    return pl.pallas_call(
        body, name=name, grid=(HEADS // SB_HEADS, nq),
        in_specs=[qblk(OFF_QB), full(OFF_KB), full(OFF_VB), qblk(OFF_ZB), qblk(OFF_YB)]
        + ([] if after is None else [ANY]),
        out_specs=[qblk(0), full(0), full(0)],
        out_shape=[out, out, out],
        scratch_shapes=[pltpu.VMEM((SB_HEADS, nq, t, t), F32), pltpu.VMEM((SB_HEADS, nq, t, t), F32)],
        compiler_params=_params(("parallel", "arbitrary")),
    )(proj, proj, proj, proj, dy, *([] if after is None else [after]))


MEM_TQ = 512


def _qk_norm(x, g):
    r = lax.rsqrt(jnp.mean(x * x, axis=-1, keepdims=True) + EPS)
    xhat = x * r
    return xhat * g, xhat, r


def _qk_norm_bwd(dn, g, xhat, r):
    dxh = dn * g
    return r * (dxh - xhat * jnp.mean(dxh * xhat, axis=-1, keepdims=True))


def _mem_probs(q, mk, qg, kg):
    qn, qhat, rq = _qk_norm(q, qg)
    kn, khat, rk = _qk_norm(mk, kg)
    qn_b, kn_b = qn.astype(BF16), kn.astype(BF16)
    s = _dot(qn_b, kn_b, NT) * ATT_SCALE
    p = jnp.exp(s - jnp.max(s, axis=-1, keepdims=True))
    p = p / jnp.sum(p, axis=-1, keepdims=True)
    return p, qn_b, kn_b, qhat, rq, khat, rk


def _mem_fwd(name, proj, mem_kv, qg, kg):
    s_len = proj.shape[0]
    m_len = mem_kv.shape[0]
    tq = min(MEM_TQ, s_len)

    def body(q_ref, mk_ref, mv_ref, qg_ref, kg_ref, o_ref):
        p = _mem_probs(q_ref[...], mk_ref[...], qg_ref[...], kg_ref[...])[0]
        o_ref[...] = _dot(p.astype(BF16), mv_ref[...].astype(BF16), NN)

    cb = HEAD_DIM
    vec = pl.BlockSpec((1, cb), lambda h, i: (0, 0))
    return pl.pallas_call(
        body, name=name, grid=(HEADS, s_len // tq),
        in_specs=[pl.BlockSpec((tq, cb), lambda h, i: (i, OFF_QC // cb + h)),
                  pl.BlockSpec((m_len, cb), lambda h, i: (0, h)),
                  pl.BlockSpec((m_len, cb), lambda h, i: (0, HEADS + h)), vec, vec],
        out_specs=pl.BlockSpec((tq, cb), lambda h, i: (i, h)),
        out_shape=jax.ShapeDtypeStruct((s_len, D_C), F32),
        compiler_params=_params(("parallel", "parallel")),
    )(proj, mem_kv, mem_kv, qg, kg)


def _mem_bwd(name, proj, mem_kv, qg, kg, dy):
    s_len = proj.shape[0]
    m_len = mem_kv.shape[0]
    tq = min(MEM_TQ, s_len)

    def body(q_ref, mk_ref, mv_ref, qg_ref, kg_ref, z_ref, dy_ref, dq_ref, dmk_ref, dmv_ref, dqg_ref, dkg_ref):
        h, i = pl.program_id(0), pl.program_id(1)

        @pl.when(i == 0)
        def _():
            dmk_ref[...] = jnp.zeros_like(dmk_ref)
            dmv_ref[...] = jnp.zeros_like(dmv_ref)

        @pl.when((i == 0) & (h == 0))
        def _():
            dqg_ref[...] = jnp.zeros_like(dqg_ref)
            dkg_ref[...] = jnp.zeros_like(dkg_ref)

        qg, kg = qg_ref[...], kg_ref[...]
        p, qn_b, kn_b, qhat, rq, khat, rk = _mem_probs(q_ref[...], mk_ref[...], qg, kg)
        silu_z, _ = _silu_and_grad(z_ref[...])
        do_b = (dy_ref[...] * silu_z).astype(BF16)
        dmv_ref[...] += _dot(p.astype(BF16), do_b, TN)
        dp = _dot(do_b, mv_ref[...].astype(BF16), NT)
        ds = (p * (dp - jnp.sum(dp * p, axis=-1, keepdims=True)) * ATT_SCALE).astype(BF16)
        dqn = _dot(ds, kn_b, NN)
        dkn = _dot(ds, qn_b, TN)
        dq_ref[...] = _qk_norm_bwd(dqn, qg, qhat, rq)
        dmk_ref[...] += _qk_norm_bwd(dkn, kg, khat, rk)
        dqg_ref[...] += jnp.sum(dqn * qhat, axis=0, keepdims=True)
        dkg_ref[...] += jnp.sum(dkn * khat, axis=0, keepdims=True)

    cb = HEAD_DIM
    vec = pl.BlockSpec((1, cb), lambda h, i: (0, 0))
    qblk = lambda off: pl.BlockSpec((tq, cb), lambda h, i: (i, off // cb + h))
    memblk = lambda off: pl.BlockSpec((m_len, cb), lambda h, i: (0, off + h))
    return pl.pallas_call(
        body, name=name, grid=(HEADS, s_len // tq),
        in_specs=[qblk(OFF_QC), memblk(0), memblk(HEADS), vec, vec, qblk(OFF_ZC), qblk(OFF_YC)],
        out_specs=[qblk(0), memblk(0), memblk(0), vec, vec],
        out_shape=[jax.ShapeDtypeStruct((s_len, D_C), F32), jax.ShapeDtypeStruct((m_len, D_C), F32),
                   jax.ShapeDtypeStruct((m_len, D_C), F32), jax.ShapeDtypeStruct((1, cb), F32),
                   jax.ShapeDtypeStruct((1, cb), F32)],
        compiler_params=_params(("arbitrary", "arbitrary")),
    )(proj, mem_kv, mem_kv, qg, kg, proj, dy)


def _sgu_common(u_ref, v_ref, lng_ref, lnb_ref, w_ref, bias_ref):
    ug = _gelu(u_ref[...])
    vg = _gelu(v_ref[...])
    mu = jnp.mean(vg, axis=-1, keepdims=True)
    xc = vg - mu
    rstd = lax.rsqrt(jnp.mean(xc * xc, axis=-1, keepdims=True) + EPS)
    xhat = xc * rstd
    vn = xhat * lng_ref[...] + lnb_ref[...]
    vn_b = vn.astype(BF16)
    row = lax.broadcasted_iota(jnp.int32, (CHUNK, CHUNK), 0)
    col = lax.broadcasted_iota(jnp.int32, (CHUNK, CHUNK), 1)
    tril = row >= col
    mixed = []
    for g in range(A_GROUPS):
        w = jnp.where(tril, w_ref[g], 0.0).astype(BF16)
        sl = slice(g * CHUNK, (g + 1) * CHUNK)
        mixed.append(_dot(w, vn_b[:, sl], NN) + bias_ref[:, sl])
    return ug, xhat, rstd, vn_b, mixed, tril


def _gate_fwd(name, proj, o_b, o_c, lng, lnb, w_s, bias):
    s_len = proj.shape[0]

    def body(u_ref, v_ref, za_ref, zb_ref, zc_ref, ob_ref, oc_ref, lng_ref, lnb_ref, w_ref, bias_ref, y_ref, yt_ref):
        ug, _, _, _, mixed, _ = _sgu_common(u_ref, v_ref, lng_ref, lnb_ref, w_ref, bias_ref)
        sza, _ = _silu_and_grad(za_ref[...])
        gate = ug * sza

        def put(off, width, val):
            y_ref[:, off:off + width] = val.astype(BF16)
            yt_ref[off:off + width, :] = val.T.astype(BF16)

        for g in range(A_GROUPS):
            sl = slice(g * CHUNK, (g + 1) * CHUNK)
            put(g * CHUNK, CHUNK, gate[:, sl] * mixed[g])
        szb, _ = _silu_and_grad(zb_ref[...])
        put(OFF_YB, D_B, ob_ref[...] * szb)
        szc, _ = _silu_and_grad(zc_ref[...])
        put(OFF_YC, D_C, oc_ref[...] * szc)

    wide = lambda off: pl.BlockSpec((CHUNK, D_A), lambda i: (i, off // D_A))
    narrow = lambda off: pl.BlockSpec((CHUNK, D_B), lambda i: (i, off // D_B))
    vec = pl.BlockSpec((1, D_A), lambda i: (0, 0))
    return pl.pallas_call(
        body, name=name, grid=(s_len // CHUNK,),
        in_specs=[wide(OFF_U), wide(OFF_V), wide(OFF_ZA), narrow(OFF_ZB), narrow(OFF_ZC), narrow(0), narrow(0), vec, vec,
                  pl.BlockSpec((A_GROUPS, CHUNK, CHUNK), lambda i: (0, 0, 0)),
                  pl.BlockSpec((CHUNK, D_A), lambda i: (0, 0))],
        out_specs=[pl.BlockSpec((CHUNK, D_MODEL), lambda i: (i, 0)), pl.BlockSpec((D_MODEL, CHUNK), lambda i: (0, i))],
        out_shape=[jax.ShapeDtypeStruct((s_len, D_MODEL), BF16), jax.ShapeDtypeStruct((D_MODEL, s_len), BF16)],
        compiler_params=_params(("parallel",)),
    )(proj, proj, proj, proj, proj, o_b, o_c, lng, lnb, w_s, bias)


def _gate_bwd(name, proj, dy, o_b, o_c, dqkv, dq_c, lng, lnb, w_s, w_s_t, bias):
    s_len = proj.shape[0]
    n = s_len // CHUNK
    dq_b, dk_b, dv_b = dqkv

    def body(u_ref, v_ref, za_ref, zb_ref, zc_ref, dya_ref, dyb_ref, dyc_ref, ob_ref, oc_ref, dq_ref, dk_ref, dv_ref,
             dqc_ref, lng_ref, lnb_ref, w_ref, wt_ref, bias_ref, dp_ref, dw_ref, dsb_ref, dlng_ref, dlnb_ref, dbias_ref):
        i = pl.program_id(0)

        @pl.when(i == 0)
        def _():
            dw_ref[...] = jnp.zeros_like(dw_ref)
            dbias_ref[...] = jnp.zeros_like(dbias_ref)
            dlng_ref[...] = jnp.zeros_like(dlng_ref)
            dlnb_ref[...] = jnp.zeros_like(dlnb_ref)

        ug, xhat, rstd, vn_b, mixed, tril = _sgu_common(u_ref, v_ref, lng_ref, lnb_ref, w_ref, bias_ref)
        za = za_ref[...]
        sza, dsza = _silu_and_grad(za)
        dya = dya_ref[...]
        mixed_all = jnp.concatenate(mixed, axis=-1)
        d_mixed = dya * ug * sza
        dp_ref[:, OFF_U:OFF_U + D_A] = (dya * mixed_all * sza * _gelu_grad(u_ref[...])).astype(BF16)
        dp_ref[:, OFF_ZA:OFF_ZA + D_A] = (dya * ug * mixed_all * dsza).astype(BF16)
        dbias_ref[...] += d_mixed
        dm_b = d_mixed.astype(BF16)
        triu = lax.broadcasted_iota(jnp.int32, (CHUNK, CHUNK), 0) <= lax.broadcasted_iota(jnp.int32, (CHUNK, CHUNK), 1)
        d_vn = []
        for g in range(A_GROUPS):
            sl = slice(g * CHUNK, (g + 1) * CHUNK)
            wt = jnp.where(triu, wt_ref[g], 0.0).astype(BF16)
            d_vn.append(_dot(wt, dm_b[:, sl], NN))
            dw_ref[g] += jnp.where(tril, _dot(dm_b[:, sl], vn_b[:, sl], NT), 0.0)
        d_vn = jnp.concatenate(d_vn, axis=-1)
        dlng_ref[...] += jnp.sum(d_vn * xhat, axis=0, keepdims=True)
        dlnb_ref[...] += jnp.sum(d_vn, axis=0, keepdims=True)
        dxh = d_vn * lng_ref[...]
        d_vg = rstd * (dxh - jnp.mean(dxh, axis=-1, keepdims=True)
                       - xhat * jnp.mean(dxh * xhat, axis=-1, keepdims=True))
        dp_ref[:, OFF_V:OFF_V + D_A] = (d_vg * _gelu_grad(v_ref[...])).astype(BF16)
        dp_ref[:, OFF_QB:OFF_QB + D_B] = dq_ref[...].astype(BF16)
        dp_ref[:, OFF_KB:OFF_KB + D_B] = dk_ref[...].astype(BF16)
        dp_ref[:, OFF_VB:OFF_VB + D_B] = dv_ref[...].astype(BF16)
        _, dszb = _silu_and_grad(zb_ref[...])
        dp_ref[:, OFF_ZB:OFF_ZB + D_B] = (dyb_ref[...] * ob_ref[...] * dszb).astype(BF16)
        dp_ref[:, OFF_QC:OFF_QC + D_C] = dqc_ref[...].astype(BF16)
        _, dszc = _silu_and_grad(zc_ref[...])
        dp_ref[:, OFF_ZC:OFF_ZC + D_C] = (dyc_ref[...] * oc_ref[...] * dszc).astype(BF16)

        @pl.when(i == n - 1)
        def _():
            ch = lax.broadcasted_iota(jnp.int32, (D_A, CHUNK), 0)
            gcol = lax.broadcasted_iota(jnp.int32, (D_A, CHUNK), 1)
            pick = (ch // (D_A // A_GROUPS) == gcol).astype(BF16)
            rest = dbias_ref[...]
            tot = jnp.zeros((CHUNK, CHUNK), F32)
            for _ in range(3):
                term = rest.astype(BF16)
                tot = tot + _dot(term, pick, NN)
                rest = rest - term.astype(F32)
            dsb_ref[...] = tot

    wide = lambda off: pl.BlockSpec((CHUNK, D_A), lambda i: (i, off // D_A))
    narrow = lambda off: pl.BlockSpec((CHUNK, D_B), lambda i: (i, off // D_B))
    vec = pl.BlockSpec((1, D_A), lambda i: (0, 0))
    wspec = pl.BlockSpec((A_GROUPS, CHUNK, CHUNK), lambda i: (0, 0, 0))
    bspec = pl.BlockSpec((CHUNK, D_A), lambda i: (0, 0))
    return pl.pallas_call(
        body, name=name, grid=(n,),
        in_specs=[wide(OFF_U), wide(OFF_V), wide(OFF_ZA), narrow(OFF_ZB), narrow(OFF_ZC),
                  wide(0), narrow(OFF_YB), narrow(OFF_YC), narrow(0), narrow(0), narrow(0), narrow(0), narrow(0),
                  narrow(0), vec, vec, wspec, wspec, bspec],
        out_specs=[pl.BlockSpec((CHUNK, IN_WIDTH), lambda i: (i, 0)), wspec,
                   pl.BlockSpec((CHUNK, CHUNK), lambda i: (0, 0)), vec, vec],
        out_shape=[jax.ShapeDtypeStruct((s_len, IN_WIDTH), BF16), jax.ShapeDtypeStruct((A_GROUPS, CHUNK, CHUNK), F32),
                   jax.ShapeDtypeStruct((CHUNK, CHUNK), F32), jax.ShapeDtypeStruct((1, D_A), F32),
                   jax.ShapeDtypeStruct((1, D_A), F32)],
        scratch_shapes=[pltpu.VMEM((CHUNK, D_A), F32)],
        compiler_params=_params(("arbitrary",)),
    )(proj, proj, proj, proj, proj, dy, dy, dy, o_b, o_c, dq_b, dk_b, dv_b, dq_c, lng, lnb, w_s, w_s_t, bias)


IN_SHARD = IN_WIDTH // N_CHIPS
ROW_SHARD = D_MODEL // N_CHIPS


def _bias_rows(sgu_b_l):
    return jnp.repeat(sgu_b_l.T, D_A // A_GROUPS, axis=1)


def _layer_fwd(l, x, mem, sm, hooks, target=None):
    s_len = x.shape[0]
    m_len = mem.shape[0]
    tm = min(1024, s_len)
    h, h_t = _rms_fwd(f"rms_fwd_{l}", x, sm["norm_g"][l][None], min(256, s_len), transposed=True)
    proj, stage = None, 0
    while (ready := hooks.w_in(stage, h, proj)) is not None:
        w_in_all, order, first, count = ready
        proj = _matmul(
            f"in_proj_{l}_{stage}", h, w_in_all, grid=(s_len // tm, count, 1), place=order, into=proj,
            a_spec=pl.BlockSpec((tm, D_MODEL), lambda i, j, k, p: (i, 0)),
            b_spec=pl.BlockSpec((None, D_MODEL, IN_SHARD), lambda i, j, k, p: (p[first + j], 0, 0)),
            o_spec=pl.BlockSpec((tm, IN_SHARD), lambda i, j, k, p: (i, p[first + j])),
            out_shape=jax.ShapeDtypeStruct((s_len, IN_WIDTH), F32), dims=NN)
        stage += 1
    o_b = _sb_fwd(f"sb_fwd_{l}", proj, hooks.rest_start(proj))
    w_kv_all, w_out_all, after = hooks.rest_finish(o_b)
    mem_h = _rms_fwd(f"mem_rms_fwd_{l}", mem, sm["mem_norm_g"][l][None], m_len, after)
    mem_kv = _matmul(
        f"mem_kv_{l}", mem_h, w_kv_all, grid=(1, 2, N_CHIPS),
        a_spec=pl.BlockSpec((m_len, ROW_SHARD), lambda i, j, k: (0, k)),
        b_spec=pl.BlockSpec((None, ROW_SHARD, D_C), lambda i, j, k: (k, 0, j)),
        o_spec=pl.BlockSpec((m_len, D_C), lambda i, j, k: (0, j)),
        out_shape=jax.ShapeDtypeStruct((m_len, 2 * D_C), F32), dims=NN)
    qg, kg = sm["q_norm_g"][l][None], sm["k_norm_g"][l][None]
    o_c = _mem_fwd(f"mem_fwd_{l}", proj, mem_kv, qg, kg)
    bias = _bias_rows(sm["sgu_b"][l])
    y, y_t = _gate_fwd(f"gate_fwd_{l}", proj, o_b, o_c, sm["sgu_ln_g"][l][None], sm["sgu_ln_b"][l][None],
                       sm["sgu_w"][l], bias)
    saved = dict(x=x, h_t=h_t, proj=proj, mem_h=mem_h, mem_kv=mem_kv, o_b=o_b, o_c=o_c, y_t=y_t, bias=bias,
                 weights=(w_in_all, w_kv_all, w_out_all))
    if target is not None:
        return _out_proj_loss(f"out_proj_{l}", y, w_out_all, x, target, tm), saved
    tn_o = 512
    x_next = _matmul(
        f"out_proj_{l}", y, w_out_all, grid=(s_len // tm, D_MODEL // tn_o, 1),
        a_spec=pl.BlockSpec((tm, D_MODEL), lambda i, j, k: (i, 0)),
        b_spec=pl.BlockSpec((N_CHIPS, ROW_SHARD, tn_o), lambda i, j, k: (0, 0, j)),
        o_spec=pl.BlockSpec((tm, tn_o), lambda i, j, k: (i, j)),
        out_shape=jax.ShapeDtypeStruct((s_len, D_MODEL), F32), dims=NN,
        res=x, res_spec=pl.BlockSpec((tm, tn_o), lambda i, j, k: (i, j)), after=hooks.before_out(y))
    return x_next, saved


def _out_proj_loss(name, y, w_out_all, x, target, tm):
    s_len, d = x.shape
    tn = 512
    n_i = s_len // tm

    n_j = d // tn

    def body(y_ref, w_ref, x_ref, t_ref, dx_ref, dxb_ref, loss_ref, acc_ref):
        i, j = pl.program_id(0), pl.program_id(1)
        out = _dot(y_ref[...], w_ref[...].reshape(-1, tn), NN) + x_ref[...]
        e = out - t_ref[...]
        dx = e * (1.0 / d)
        dx_ref[...] = dx
        dxb_ref[...] = dx.astype(BF16)
        part = jnp.sum(e * e, axis=0, keepdims=True)

        @pl.when((i == 0) & (j == 0))
        def _():
            acc_ref[...] = part

        @pl.when((i > 0) | (j > 0))
        def _():
            acc_ref[...] += part

        @pl.when((i == n_i - 1) & (j == n_j - 1))
        def _():
            loss_ref[...] = jnp.sum(acc_ref[...], axis=-1, keepdims=True) * (0.5 / d)

    blk = pl.BlockSpec((tm, tn), lambda i, j: (i, j))
    return pl.pallas_call(
        body, name=name, grid=(n_i, n_j),
        in_specs=[pl.BlockSpec((tm, d), lambda i, j: (i, 0)), pl.BlockSpec((N_CHIPS, ROW_SHARD, tn), lambda i, j: (0, 0, j)),
                  blk, blk],
        out_specs=[blk, blk, pl.BlockSpec((1, 1), lambda i, j: (0, 0))],
        out_shape=[jax.ShapeDtypeStruct((s_len, d), F32), jax.ShapeDtypeStruct((s_len, d), BF16),
                   jax.ShapeDtypeStruct((1, 1), F32)],
        scratch_shapes=[pltpu.VMEM((1, tn), F32)],
        compiler_params=_params(("arbitrary", "arbitrary")),
    )(y, w_out_all, x, target)


def _layer_bwd(l, dxo, dxo_b, mem, sm, saved, place, exchange):
    s_len = dxo.shape[0]
    m_len = mem.shape[0]
    proj, y_t, h_t, mem_h, mem_kv = saved["proj"], saved["y_t"], saved["h_t"], saved["mem_h"], saved["mem_kv"]
    w_in_all, w_kv_all, w_out_all = saved["weights"]
    tm = min(1024, s_len)
    tn = IN_SHARD
    per = IN_SHARD // tn
    half_rows = ROW_SHARD // 2

    def halves(make):
        give = lambda: make("give", lambda p: 1 - p[1], None, BF16)
        keep = lambda theirs: make("keep", lambda p: p[1], theirs, BF16)
        return give, keep

    def grad_out(tag, half, theirs, dtype):
        o_spec = pl.BlockSpec((None, half_rows, 1024), lambda i, j, k, p: (i, 0, j))
        return _matmul(
            f"d_w_out_{l}_{tag}", y_t, dxo_b, grid=(N_CHIPS, D_MODEL // 1024, 1), place=place,
            a_spec=pl.BlockSpec((half_rows, s_len), lambda i, j, k, p: (2 * i + half(p), 0)),
            b_spec=pl.BlockSpec((s_len, 1024), lambda i, j, k, p: (0, j)), o_spec=o_spec,
            out_shape=jax.ShapeDtypeStruct((N_CHIPS, half_rows, D_MODEL), dtype), dims=NN,
            res=theirs, res_spec=o_spec)

    def grad_in(tag, half, theirs, dtype):
        o_spec = pl.BlockSpec((None, D_MODEL // 2, tn), lambda i, j, k, p: (j // per, 0, j % per))
        return _matmul(
            f"d_w_in_{l}_{tag}", h_t, dproj, grid=(1, IN_WIDTH // tn, 1), place=place,
            a_spec=pl.BlockSpec((D_MODEL // 2, s_len), lambda i, j, k, p: (half(p), 0)),
            b_spec=pl.BlockSpec((s_len, tn), lambda i, j, k, p: (0, j)), o_spec=o_spec,
            out_shape=jax.ShapeDtypeStruct((N_CHIPS, D_MODEL // 2, IN_SHARD), dtype), dims=NN,
            res=theirs, res_spec=o_spec)

    def grad_kv(tag, half, theirs, dtype):
        o_spec = pl.BlockSpec((None, half_rows, 2 * D_C), lambda i, j, k, p: (i, 0, 0))
        return _matmul(
            f"d_w_kv_{l}_{tag}", mem_h, dkv_b, grid=(N_CHIPS, 1, 1), place=place,
            a_spec=pl.BlockSpec((m_len, half_rows), lambda i, j, k, p: (0, 2 * i + half(p))),
            b_spec=pl.BlockSpec((m_len, 2 * D_C), lambda i, j, k, p: (0, 0)), o_spec=o_spec,
            out_shape=jax.ShapeDtypeStruct((N_CHIPS, half_rows, 2 * D_C), dtype), dims=TN,
            res=theirs, res_spec=o_spec)

    give_out, keep_out = halves(grad_out)
    token = exchange.start(l, "out", [give_out()])
    dy = _matmul(
        f"d_y_{l}", dxo_b, w_out_all, grid=(s_len // tm, N_CHIPS, 1),
        a_spec=pl.BlockSpec((tm, D_MODEL), lambda i, j, k: (i, 0)),
        b_spec=pl.BlockSpec((None, ROW_SHARD, D_MODEL), lambda i, j, k: (j, 0, 0)),
        o_spec=pl.BlockSpec((tm, ROW_SHARD), lambda i, j, k: (i, j)),
        out_shape=jax.ShapeDtypeStruct((s_len, D_MODEL), F32), dims=NT, after=token)
    (theirs_out,) = exchange.landed(l, "out", dy)
    token = exchange.send(l, "out", [keep_out(theirs_out)])
    qg, kg = sm["q_norm_g"][l][None], sm["k_norm_g"][l][None]
    dqkv = _sb_bwd(f"sb_bwd_{l}", proj, dy, token)
    dq_c, dmk, dmv, dqg, dkg = _mem_bwd(f"mem_bwd_{l}", proj, mem_kv, qg, kg, dy)
    w_s = sm["sgu_w"][l]
    dproj, dws, dbias, dlng, dlnb = _gate_bwd(
        f"gate_bwd_{l}", proj, dy, saved["o_b"], saved["o_c"], dqkv, dq_c, sm["sgu_ln_g"][l][None],
        sm["sgu_ln_b"][l][None], w_s, jnp.swapaxes(w_s, 1, 2), saved["bias"])
    dkv_b = jnp.concatenate([dmk, dmv], axis=1).astype(BF16)
    give_in, keep_in = halves(grad_in)
    give_kv, keep_kv = halves(grad_kv)
    token = exchange.start(l, "in", [give_in(), give_kv()])
    dh = _matmul(
        f"d_h_{l}", dproj, w_in_all, grid=(s_len // tm, D_MODEL // 512, 1),
        a_spec=pl.BlockSpec((tm, IN_WIDTH), lambda i, j, k: (i, 0)),
        b_spec=pl.BlockSpec((N_CHIPS, 512, IN_SHARD), lambda i, j, k: (0, j, 0)),
        o_spec=pl.BlockSpec((tm, 512), lambda i, j, k: (i, j)),
        out_shape=jax.ShapeDtypeStruct((s_len, D_MODEL), F32), dims=NT, after=token, vmem_mb=56)
    theirs_in, theirs_kv = exchange.landed(l, "in", dh)
    token = exchange.send(l, "in", [keep_in(theirs_in), keep_kv(theirs_kv)])
    dx, dx_b, dng = _rms_bwd(f"rms_bwd_{l}", saved["x"], dh, dxo, sm["norm_g"][l][None], min(256, s_len), token)
    d_mem_h = _matmul(
        f"d_mem_h_{l}", dkv_b, w_kv_all, grid=(1, N_CHIPS, 1),
        a_spec=pl.BlockSpec((m_len, 2 * D_C), lambda i, j, k: (0, 0)),
        b_spec=pl.BlockSpec((None, ROW_SHARD, 2 * D_C), lambda i, j, k: (j, 0, 0)),
        o_spec=pl.BlockSpec((m_len, ROW_SHARD), lambda i, j, k: (0, j)),
        out_shape=jax.ShapeDtypeStruct((m_len, D_MODEL), F32), dims=NT)
    dmng = _rms_gain_grad(f"mem_rms_bwd_{l}", mem, d_mem_h)
    dsgu_b = dbias[:, :A_GROUPS].T
    small = dict(norm_g=dng[0], sgu_ln_g=dlng[0], sgu_ln_b=dlnb[0], sgu_w=dws, sgu_b=dsgu_b, mem_norm_g=dmng[0],
                 q_norm_g=dqg[0], k_norm_g=dkg[0])
    return dx, dx_b, small


SMALL_NAMES = ("norm_g", "sgu_ln_g", "sgu_ln_b", "sgu_w", "sgu_b", "mem_norm_g", "q_norm_g", "k_norm_g")


def _place():
    x, y, c = lax.axis_index("x"), lax.axis_index("y"), lax.axis_index("c")
    return x, y, c


def _other_chips(x, y):
    return [(1 - x, y, 2 * (1 - x) + y), (x, 1 - y, 2 * x + 1 - y), (1 - x, 1 - y, 2 * (1 - x) + 1 - y)]


D2D_CHUNKS = 8


def _place_index():
    return jnp.stack([2 * lax.axis_index("x") + lax.axis_index("y"), lax.axis_index("c")]).astype(jnp.int32)


def _cast_into_slot(name, w, l, place):
    _, rows, cols = w.shape
    tr = min(256, rows)

    def body(p_ref, w_ref, o_ref):
        o_ref[...] = w_ref[...].astype(BF16)

    return pl.pallas_call(
        body, name=name,
        grid_spec=pltpu.PrefetchScalarGridSpec(
            num_scalar_prefetch=1, grid=(rows // tr,),
            in_specs=[pl.BlockSpec((None, tr, cols), lambda i, p: (l, i, 0))],
            out_specs=pl.BlockSpec((None, tr, cols), lambda i, p: (p[0], i, 0))),
        out_shape=jax.ShapeDtypeStruct((N_CHIPS, rows, cols), BF16),
        compiler_params=_params(("parallel",)),
    )(place, w)


HBM = pl.BlockSpec(memory_space=pltpu.HBM)
SEM = pl.BlockSpec(memory_space=pltpu.SEMAPHORE)
DATAFLOW = pltpu.SideEffectType.DATAFLOW_SIDE_EFFECTING


def _in_hbm(a):
    return pltpu.with_memory_space_constraint(a, pltpu.HBM)


ALL_PEERS = (0, 1, 2)
NEIGHBOURS = (0, 1)
DIAGONAL = (2,)


def _chip_copies_start(name, srcs, lands, make_copy, after=None, peers=ALL_PEERS):
    n_t = len(srcs)
    in_place = lands is None
    n_after = 0 if after is None else 1

    def body(*refs):
        src = refs[:n_t]
        k = (n_t if in_place else 2 * n_t) + n_after
        send_sems, recv_sems = refs[k], refs[k + 1]
        land = refs[k + 2:k + 2 + n_t] if in_place else refs[k + 2 + n_t:k + 2 + 2 * n_t]
        token = refs[-1]
        x, y, c = _place()
        me = 2 * x + y
        others = _other_chips(x, y)
        for t in range(n_t):
            for px, py, pk in [others[p] for p in peers]:
                s, d = make_copy(src[t], land[t], me, pk, c)
                pltpu.make_async_remote_copy(
                    src_ref=s, dst_ref=d, send_sem=send_sems.at[t], recv_sem=recv_sems.at[t],
                    device_id=(px, py, c), device_id_type=MESH).start()
        token[...] = jnp.zeros_like(token)

    bufs = list(srcs) if in_place else list(srcs) + list(lands)
    outs = pl.pallas_call(
        body, name=name,
        in_specs=[HBM] * len(bufs) + [ANY] * n_after,
        out_specs=[SEM, SEM] + [HBM] * len(bufs) + [pl.BlockSpec(memory_space=pltpu.VMEM)],
        out_shape=[pltpu.SemaphoreType.DMA((n_t,)), pltpu.SemaphoreType.DMA((n_t,))]
        + [pltpu.HBM(b.shape, b.dtype) for b in bufs] + [jax.ShapeDtypeStruct((8, 128), F32)],
        input_output_aliases={i: 2 + i for i in range(len(bufs))},
        compiler_params=pltpu.CompilerParams(has_side_effects=DATAFLOW),
    )(*[_in_hbm(b) for b in bufs], *([] if after is None else [after]))
    return outs[0], outs[1], list(outs[2:2 + len(bufs)]), outs[-1]


def _chip_copies_wait(name, send_sems, recv_sems, bufs, sent, landed, after):
    n_b = len(bufs)

    def body(*refs):
        buf = refs[:n_b]
        send_ref, recv_ref = refs[n_b], refs[n_b + 1]
        x, y, c = _place()
        for t, (s, d) in enumerate(zip(sent(buf), landed(buf))):
            out = pltpu.make_async_remote_copy(src_ref=s, dst_ref=s, send_sem=send_ref.at[t], recv_sem=recv_ref.at[t],
                                               device_id=(x, y, c), device_id_type=MESH)
            out.wait_send()
            arrived = pltpu.make_async_remote_copy(src_ref=d, dst_ref=d, send_sem=send_ref.at[t],
                                                   recv_sem=recv_ref.at[t], device_id=(x, y, c), device_id_type=MESH)
            arrived.wait_recv()

    after = list(after) if isinstance(after, (list, tuple)) else [after]
    return pl.pallas_call(
        body, name=name,
        in_specs=[HBM] * n_b + [SEM, SEM] + [ANY] * len(after), out_specs=[HBM] * n_b,
        out_shape=[pltpu.HBM(b.shape, b.dtype) for b in bufs],
        input_output_aliases={i: i for i in range(n_b)},
        compiler_params=pltpu.CompilerParams(has_side_effects=DATAFLOW),
    )(*bufs, send_sems, recv_sems, *after)


def _gather_start(name, bufs, after=None, peers=ALL_PEERS):
    def make_copy(src, land, me, pk, c):
        hr = src.shape[1] // 2
        return src.at[me, pl.ds(c * hr, hr)], land.at[me, pl.ds(c * hr, hr)]

    return _chip_copies_start(name, bufs, None, make_copy, after, peers)


def _gather_wait(name, send_sems, recv_sems, bufs, after, peers=ALL_PEERS):
    def half_shards(buf):
        return [b.at[pl.ds(0, len(peers)), pl.ds(0, b.shape[1] // 2)] for b in buf]

    return _chip_copies_wait(name, send_sems, recv_sems, bufs, half_shards, half_shards, after)


def _gather_forward_start(name, bufs, peers=ALL_PEERS):
    n_t = len(bufs)

    def body(*refs):
        mine = refs[:n_t]
        send_sems, recv_sems = refs[n_t], refs[n_t + 1]
        buf = refs[n_t + 2:2 * n_t + 2]
        token = refs[-1]
        x, y, c = _place()
        others = _other_chips(x, y)
        for q in range(D2D_CHUNKS):
            for t in range(n_t):
                hr = mine[t].shape[1] // 2
                cr = hr // D2D_CHUNKS
                rows = pl.ds(c * hr + q * cr, cr)
                for _, _, pk in [others[p] for p in peers]:
                    pltpu.make_async_remote_copy(
                        src_ref=mine[t].at[pk, rows], dst_ref=buf[t].at[pk, rows], send_sem=send_sems.at[t],
                        recv_sem=recv_sems.at[t], device_id=(x, y, 1 - c), device_id_type=MESH).start()
        token[...] = jnp.zeros_like(token)

    outs = pl.pallas_call(
        body, name=name,
        in_specs=[HBM] * n_t,
        out_specs=[SEM, SEM] + [HBM] * n_t + [pl.BlockSpec(memory_space=pltpu.VMEM)],
        out_shape=[pltpu.SemaphoreType.DMA((n_t,)), pltpu.SemaphoreType.DMA((n_t,))]
        + [pltpu.HBM(b.shape, b.dtype) for b in bufs] + [jax.ShapeDtypeStruct((8, 128), F32)],
        input_output_aliases={i: 2 + i for i in range(n_t)},
        compiler_params=pltpu.CompilerParams(has_side_effects=DATAFLOW),
    )(*[_in_hbm(b) for b in bufs])
    return outs[0], outs[1], list(outs[2:2 + n_t]), outs[-1]


def _core_exchange_start(name, grads):
    n_t = len(grads)
    lands = [lax.empty(g.shape, g.dtype) for g in grads]

    def body(*refs):
        src = refs[:n_t]
        send_sems, recv_sems = refs[2 * n_t], refs[2 * n_t + 1]
        land = refs[2 * n_t + 2 + n_t:2 * n_t + 2 + 2 * n_t]
        token = refs[-1]
        x, y, c = _place()
        for q in range(D2D_CHUNKS):
            for t in range(n_t):
                cr = src[t].shape[1] // D2D_CHUNKS
                rows = pl.ds(q * cr, cr)
                pltpu.make_async_remote_copy(
                    src_ref=src[t].at[:, rows], dst_ref=land[t].at[:, rows], send_sem=send_sems.at[t],
                    recv_sem=recv_sems.at[t], device_id=(x, y, 1 - c), device_id_type=MESH).start()
        token[...] = jnp.zeros_like(token)

    bufs = list(grads) + lands
    outs = pl.pallas_call(
        body, name=name,
        in_specs=[HBM] * len(bufs),
        out_specs=[SEM, SEM] + [HBM] * len(bufs) + [pl.BlockSpec(memory_space=pltpu.VMEM)],
        out_shape=[pltpu.SemaphoreType.DMA((n_t,)), pltpu.SemaphoreType.DMA((n_t,))]
        + [pltpu.HBM(b.shape, b.dtype) for b in bufs] + [jax.ShapeDtypeStruct((8, 128), F32)],
        input_output_aliases={i: 2 + i for i in range(len(bufs))},
        compiler_params=pltpu.CompilerParams(has_side_effects=DATAFLOW),
    )(*[_in_hbm(b) for b in bufs])
    return outs[0], outs[1], list(outs[2:2 + len(bufs)]), outs[-1]


def _core_exchange_wait(name, send_sems, recv_sems, bufs, after):
    n_t = len(bufs) // 2

    def body(*refs):
        land = refs[n_t:2 * n_t]
        send_ref, recv_ref = refs[2 * n_t], refs[2 * n_t + 1]
        x, y, c = _place()
        for t in range(n_t):
            whole = pltpu.make_async_remote_copy(src_ref=land[t], dst_ref=land[t], send_sem=send_ref.at[t],
                                                 recv_sem=recv_ref.at[t], device_id=(x, y, c), device_id_type=MESH)
            whole.wait_send()
            whole.wait_recv()

    outs = pl.pallas_call(
        body, name=name,
        in_specs=[HBM] * (2 * n_t) + [SEM, SEM, ANY], out_specs=[HBM] * (2 * n_t),
        out_shape=[pltpu.HBM(b.shape, b.dtype) for b in bufs],
        input_output_aliases={i: i for i in range(2 * n_t)},
        compiler_params=pltpu.CompilerParams(has_side_effects=DATAFLOW),
    )(*bufs, send_sems, recv_sems, after)
    return list(outs[:n_t]), list(outs[n_t:])


def _chip_exchange_start(name, parts):
    lands = [lax.empty(p.shape, p.dtype) for p in parts]
    return _chip_copies_start(name, parts, lands, lambda src, land, me, pk, c: (src.at[pk], land.at[me]))


def _chip_exchange_wait(name, send_sems, recv_sems, bufs, after):
    n_t = len(bufs) // 2
    return _chip_copies_wait(name, send_sems, recv_sems, bufs,
                             lambda buf: [b.at[pl.ds(0, 3)] for b in buf[:n_t]],
                             lambda buf: [b.at[pl.ds(0, 3)] for b in buf[n_t:]], after)


def _sum_chips(name, parts, landed, place, l, stacked):
    chips, rows, cols = landed.shape
    tr = min(256, rows)
    per = rows // tr

    def body(p_ref, own_ref, *refs):
        land, o_ref = refs[:chips], refs[-1]
        tot = None
        for k in range(chips):
            term = jnp.where(p_ref[0] == k, own_ref[...], land[k][...]).astype(F32)
            tot = term if tot is None else tot + term
        o_ref[...] = tot

    def from_chip(k):
        return pl.BlockSpec((None, tr, cols), lambda i, p: (jnp.where(p[0] == k, (k + 1) % chips, k), i, 0))

    in_specs = [pl.BlockSpec((None, tr, cols), lambda i, p: (p[0], i, 0))] + [from_chip(k) for k in range(chips)]
    args = [parts] + [landed] * chips
    aliases = {}
    if stacked is not None:
        in_specs.append(ANY)
        args.append(stacked)
        aliases = {len(args): 0}
    return pl.pallas_call(
        body, name=name,
        grid_spec=pltpu.PrefetchScalarGridSpec(
            num_scalar_prefetch=1, grid=(per,), in_specs=in_specs,
            out_specs=pl.BlockSpec((None, tr, cols), lambda i, p: (l, p[1] * per + i, 0))),
        out_shape=jax.ShapeDtypeStruct((DEPTH, 2 * rows, cols), F32), input_output_aliases=aliases,
        compiler_params=_params(("parallel",)),
    )(place, *args)


def _core_share_start(name, bufs, l):
    n_t = len(bufs)

    def body(*refs):
        mine = refs[:n_t]
        send_sems, recv_sems = refs[n_t], refs[n_t + 1]
        buf = refs[n_t + 2:2 * n_t + 2]
        token = refs[-1]
        x, y, c = _place()
        for q in range(D2D_CHUNKS):
            for t in range(n_t):
                hr = mine[t].shape[1] // 2
                cr = hr // D2D_CHUNKS
                rows = pl.ds(c * hr + q * cr, cr)
                pltpu.make_async_remote_copy(
                    src_ref=mine[t].at[l, rows], dst_ref=buf[t].at[l, rows], send_sem=send_sems.at[t],
                    recv_sem=recv_sems.at[t], device_id=(x, y, 1 - c), device_id_type=MESH).start()
        token[...] = jnp.zeros_like(token)

    outs = pl.pallas_call(
        body, name=name,
        in_specs=[HBM] * n_t,
        out_specs=[SEM, SEM] + [HBM] * n_t + [pl.BlockSpec(memory_space=pltpu.VMEM)],
        out_shape=[pltpu.SemaphoreType.DMA((n_t,)), pltpu.SemaphoreType.DMA((n_t,))]
        + [pltpu.HBM(b.shape, b.dtype) for b in bufs] + [jax.ShapeDtypeStruct((8, 128), F32)],
        input_output_aliases={i: 2 + i for i in range(n_t)},
        compiler_params=pltpu.CompilerParams(has_side_effects=DATAFLOW),
    )(*[_in_hbm(b) for b in bufs])
    return outs[0], outs[1], list(outs[2:2 + n_t]), outs[-1]


def _core_share_wait(name, send_sems, recv_sems, bufs, l, after):
    def half_layer(buf):
        return [b.at[l, pl.ds(0, b.shape[1] // 2)] for b in buf]

    return _chip_copies_wait(name, send_sems, recv_sems, bufs, half_layer, half_layer, after)


def _all_reduce_small(vec, after=None):
    rows, lanes = vec.shape
    hr = rows // 2

    def body(v_ref, *refs):
        o_ref, sib_ref, chips_ref, send_sems, recv_sems = refs[-5:]
        x, y, c = _place()
        me = 2 * x + y
        sibling = (x, y, 1 - c)
        mine = pl.ds(pl.multiple_of(c * hr, 8), hr)
        theirs = pl.ds(pl.multiple_of((1 - c) * hr, 8), hr)
        swap = pltpu.make_async_remote_copy(
            src_ref=v_ref.at[theirs], dst_ref=sib_ref, send_sem=send_sems.at[0], recv_sem=recv_sems.at[0],
            device_id=sibling, device_id_type=MESH)
        swap.start()
        swap.wait_recv()
        chips_ref[me] = v_ref[mine] + sib_ref[...]
        copies = []
        for j, (px, py, pk) in enumerate(_other_chips(x, y)):
            cp = pltpu.make_async_remote_copy(
                src_ref=chips_ref.at[me], dst_ref=chips_ref.at[me], send_sem=send_sems.at[1 + j],
                recv_sem=recv_sems.at[1 + j], device_id=(px, py, c), device_id_type=MESH)
            cp.start()
            copies.append(cp)
        for j, (px, py, pk) in enumerate(_other_chips(x, y)):
            pltpu.make_async_remote_copy(
                src_ref=chips_ref.at[pk], dst_ref=chips_ref.at[pk], send_sem=send_sems.at[1 + j],
                recv_sem=recv_sems.at[1 + j], device_id=(px, py, c), device_id_type=MESH).wait_recv()
        tot = chips_ref[0]
        for k in range(1, N_CHIPS):
            tot = tot + chips_ref[k]
        o_ref[mine] = tot
        share = pltpu.make_async_remote_copy(
            src_ref=o_ref.at[mine], dst_ref=o_ref.at[mine], send_sem=send_sems.at[4], recv_sem=recv_sems.at[4],
            device_id=sibling, device_id_type=MESH)
        share.start()
        pltpu.make_async_remote_copy(
            src_ref=o_ref.at[theirs], dst_ref=o_ref.at[theirs], send_sem=send_sems.at[4], recv_sem=recv_sems.at[4],
            device_id=sibling, device_id_type=MESH).wait_recv()
        swap.wait_send()
        for cp in copies:
            cp.wait_send()
        share.wait_send()

    vm = pl.BlockSpec(memory_space=pltpu.VMEM)
    return pl.pallas_call(
        body, name="small_all_reduce", in_specs=[vm] + ([] if after is None else [ANY]), out_specs=vm,
        out_shape=jax.ShapeDtypeStruct((rows, lanes), F32),
        scratch_shapes=[pltpu.VMEM((hr, lanes), F32), pltpu.VMEM((N_CHIPS, hr, lanes), F32),
                        pltpu.SemaphoreType.DMA((5,)), pltpu.SemaphoreType.DMA((5,))],
        compiler_params=pltpu.CompilerParams(has_side_effects=True, vmem_limit_bytes=48 * MIB),
    )(vec, *([] if after is None else [after]))


def _adamw(name, w, g, m, v, place, l=0, half=None, done=None, after=None):
    layers, rows, cols = w.shape
    span = rows if half is None else rows // 2
    tr = span
    for cand in (256, 128, 64, 32, 16, 8):
        if span % cand == 0:
            tr = cand
            break
    per = span // tr
    c1 = 1.0 - ADAM_B1 ** ADAM_STEP
    c2 = 1.0 - ADAM_B2 ** ADAM_STEP

    def first_block(p):
        return 0 if half is None else (p[1] if half == "own" else 1 - p[1]) * per

    def body(p_ref, w_ref, g_ref, m_ref, v_ref, *refs):
        go_ref, d_ref, nm_ref, nv_ref = refs[-4:]
        gv = g_ref[...]
        nm = ADAM_B1 * m_ref[...] + (1.0 - ADAM_B1) * gv
        nv = ADAM_B2 * v_ref[...] + (1.0 - ADAM_B2) * (gv * gv)
        go_ref[...] = gv
        nm_ref[...] = nm
        nv_ref[...] = nv
        d_ref[...] = -ADAM_LR * ((nm / c1) / (jnp.sqrt(nv / c2) + ADAM_EPS) + ADAM_WD * w_ref[...])

    blk = pl.BlockSpec((None, tr, cols), lambda i, p: (l, first_block(p) + i, 0))
    out = jax.ShapeDtypeStruct((layers, rows, cols), F32)
    extra = ([] if done is None else list(done)) + ([] if after is None else [after])
    aliases = {} if done is None else {5 + i: i for i in range(4)}
    return pl.pallas_call(
        body, name=name,
        grid_spec=pltpu.PrefetchScalarGridSpec(
            num_scalar_prefetch=1, grid=(per,), in_specs=[blk] * 4 + [ANY] * len(extra), out_specs=[blk] * 4),
        out_shape=[out] * 4, input_output_aliases=aliases,
        compiler_params=_params(("parallel",)),
    )(place, w, g, m, v, *extra)


LANES = 128
SUBLANES = 8
SMALL_SHAPES = {
    "norm_g": (DEPTH, D_MODEL), "sgu_ln_g": (DEPTH, D_A), "sgu_ln_b": (DEPTH, D_A),
    "sgu_w": (DEPTH, A_GROUPS, CHUNK, CHUNK), "sgu_b": (DEPTH, A_GROUPS, CHUNK), "mem_norm_g": (DEPTH, D_MODEL),
    "q_norm_g": (DEPTH, HEAD_DIM), "k_norm_g": (DEPTH, HEAD_DIM)}


def _small_layout():
    at, off = {}, 0
    for k in SMALL_NAMES:
        n = math.prod(SMALL_SHAPES[k]) // LANES
        at[k] = (off, n)
        off += -(-n // SUBLANES) * SUBLANES
    return at, off, -(-(off + SUBLANES) // (2 * SUBLANES)) * 2 * SUBLANES


def _pack_small(parts, loss=None):
    at, loss_row, rows = _small_layout()
    pieces = []
    for k in SMALL_NAMES:
        n = at[k][1]
        pieces.append(jnp.pad(parts[k].reshape(n, LANES), ((0, -(-n // SUBLANES) * SUBLANES - n), (0, 0))))
    tile = jnp.zeros((SUBLANES, LANES), F32) if loss is None else jnp.broadcast_to(loss.reshape(1, 1), (SUBLANES, LANES))
    pieces += [tile, jnp.zeros((rows - loss_row - SUBLANES, LANES), F32)]
    return jnp.concatenate(pieces)


def _adamw_small(w, g, m, v):
    at, _, rows = _small_layout()
    c1 = 1.0 - ADAM_B1 ** ADAM_STEP
    c2 = 1.0 - ADAM_B2 ** ADAM_STEP
    n_names = len(SMALL_NAMES)

    def body(w_ref, g_ref, m_ref, v_ref, *refs):
        outs, (d_ref, nm_ref, nv_ref) = refs[:4 * n_names], refs[4 * n_names:]
        gv = g_ref[...]
        nm = ADAM_B1 * m_ref[...] + (1.0 - ADAM_B1) * gv
        nv = ADAM_B2 * v_ref[...] + (1.0 - ADAM_B2) * (gv * gv)
        nm_ref[...] = nm
        nv_ref[...] = nv
        d_ref[...] = -ADAM_LR * ((nm / c1) / (jnp.sqrt(nv / c2) + ADAM_EPS) + ADAM_WD * w_ref[...])
        for kind, src in enumerate((g_ref, d_ref, nm_ref, nv_ref)):
            for i, k in enumerate(SMALL_NAMES):
                o_ref = outs[kind * n_names + i]
                first, n = at[k]
                shape = SMALL_SHAPES[k]
                if shape[-1] == LANES:
                    o_ref[...] = src[pl.ds(first, n), :].reshape(shape)
                else:
                    per = shape[-1] // LANES
                    for r in range(n):
                        o_ref[pl.ds(r // per, 1), pl.ds((r % per) * LANES, LANES)] = src[pl.ds(first + r, 1), :]

    out_shape = [jax.ShapeDtypeStruct(SMALL_SHAPES[k], F32) for _ in range(4) for k in SMALL_NAMES]
    outs = pl.pallas_call(
        body, name="adamw_small", out_shape=out_shape,
        scratch_shapes=[pltpu.VMEM((rows, LANES), F32)] * 3, compiler_params=_params(None),
    )(w, g, m, v)
    return [dict(zip(SMALL_NAMES, outs[kind * n_names:(kind + 1) * n_names])) for kind in range(4)]


WEIGHT_ORDER = ("norm_g", "w_in", "sgu_ln_g", "sgu_ln_b", "sgu_w", "sgu_b", "mem_norm_g", "w_mem_kv", "q_norm_g",
                "k_norm_g", "w_out")


def kernel(x, mem, norm_g, w_in, sgu_ln_g, sgu_ln_b, sgu_w, sgu_b, mem_norm_g, w_mem_kv, q_norm_g, k_norm_g, w_out, loss_target, m_norm_g, m_w_in, m_sgu_ln_g, m_sgu_ln_b, m_sgu_w, m_sgu_b, m_mem_norm_g, m_w_mem_kv, m_q_norm_g, m_k_norm_g, m_w_out, v_norm_g, v_w_in, v_sgu_ln_g, v_sgu_ln_b, v_sgu_w, v_sgu_b, v_mem_norm_g, v_w_mem_kv, v_q_norm_g, v_k_norm_g, v_w_out):
    weights = dict(norm_g=norm_g, w_in=w_in, sgu_ln_g=sgu_ln_g, sgu_ln_b=sgu_ln_b, sgu_w=sgu_w, sgu_b=sgu_b,
                   mem_norm_g=mem_norm_g, w_mem_kv=w_mem_kv, q_norm_g=q_norm_g, k_norm_g=k_norm_g, w_out=w_out)
    mom_m = dict(norm_g=m_norm_g, w_in=m_w_in, sgu_ln_g=m_sgu_ln_g, sgu_ln_b=m_sgu_ln_b, sgu_w=m_sgu_w, sgu_b=m_sgu_b,
                 mem_norm_g=m_mem_norm_g, w_mem_kv=m_w_mem_kv, q_norm_g=m_q_norm_g, k_norm_g=m_k_norm_g, w_out=m_w_out)
    mom_v = dict(norm_g=v_norm_g, w_in=v_w_in, sgu_ln_g=v_sgu_ln_g, sgu_ln_b=v_sgu_ln_b, sgu_w=v_sgu_w, sgu_b=v_sgu_b,
                 mem_norm_g=v_mem_norm_g, w_mem_kv=v_w_mem_kv, q_norm_g=v_q_norm_g, k_norm_g=v_k_norm_g, w_out=v_w_out)
    big = ("w_in", "w_mem_kv", "w_out")
    sm = {k: weights[k] for k in SMALL_NAMES}

    place = _place_index()
    xs, mems, target = x[0], mem[0], loss_target[0]

    slots = [[_cast_into_slot(f"cast_{k}_{l}", weights[k], l, place) for k in big] for l in range(DEPTH)]
    saved = [None] * DEPTH

    chips, cores = {}, {}
    me = place[0]
    arrival = jnp.stack([me, me ^ 2, me ^ 1, 3 - me]).astype(jnp.int32)
    shard_order = jnp.arange(N_CHIPS, dtype=jnp.int32)

    def start_gather(l, after=None):
        chips[l, "in"] = _gather_start(f"gather_start_{l}_in", slots[l][:1], after)
        chips[l, "rest"] = _gather_start(f"gather_start_{l}_rest", slots[l][1:], chips[l, "in"][3])
        return chips[l, "rest"][3]

    def hand_to_sibling(l, group, after):
        send_sems, recv_sems, bufs, _ = chips[l, group]
        bufs = _gather_wait(f"gather_wait_{l}_{group}", send_sems, recv_sems, bufs, after)
        cores[l, group] = _gather_forward_start(f"gather_forward_{l}_{group}", bufs)
        return cores[l, group][3]

    def whole(l, group, after):
        send_sems, recv_sems, bufs, _ = cores[l, group]
        return _gather_wait(f"gather_whole_{l}_{group}", send_sems, recv_sems, bufs, after)

    later_slots = [s for layer in slots[1:] for s in layer]

    class Gathered:
        def __init__(self, l):
            self.l = l
            self.buf = None

        def landed_from(self, tag, peers, after, behind, then=None):
            send_sems, recv_sems, _, _ = chips[0, "in_" + tag]
            buf = _gather_wait(f"gather_wait_0_in_{tag}", send_sems, recv_sems, self.buf, after, peers)
            if then is not None:
                buf, more = then(buf)
                behind = behind + more
            send_sems, recv_sems, buf, token = _gather_forward_start(f"gather_forward_0_in_{tag}", buf, peers)
            self.buf = _gather_wait(f"gather_whole_0_in_{tag}", send_sems, recv_sems, buf, [token] + behind, peers)

        def w_in(self, stage, h, proj):
            if self.l > 0:
                return (whole(self.l, "in", h)[0], shard_order, 0, N_CHIPS) if stage == 0 else None
            if stage == 0:
                self.buf = chips[0, "in_n"][2]
                return self.buf[0], arrival, 0, 1
            if stage == 1:
                def start_others(buf):
                    chips[0, "in_d"] = _gather_start("gather_start_0_in_d", buf, None, DIAGONAL)
                    chips[0, "rest"] = _gather_start("gather_start_0_rest", slots[0][1:], chips[0, "in_d"][3])
                    return chips[0, "in_d"][2], [chips[0, "rest"][3]]

                self.landed_from("n", NEIGHBOURS, proj, later_slots + [chips[0, "in_n"][3]], start_others)
                return self.buf[0], arrival, 1, 2
            if stage == 2:
                self.landed_from("d", DIAGONAL, [proj, chips[0, "rest"][3]], [])
                return self.buf[0], arrival, 3, 1
            return None

        def rest_start(self, proj):
            token = proj if self.l == 0 else hand_to_sibling(self.l, "rest", proj)
            return start_gather(self.l + 1, token) if self.l + 1 < DEPTH else token

        def rest_finish(self, o_b):
            if self.l == 0:
                o_b = hand_to_sibling(self.l, "rest", o_b)
            w_kv_all, w_out_all = whole(self.l, "rest", o_b)
            return w_kv_all, w_out_all, None

        def before_out(self, y):
            return hand_to_sibling(self.l + 1, "in", y) if self.l + 1 < DEPTH else None

    chips[0, "in_n"] = _gather_start("gather_start_0_in_n", slots[0][:1], None, NEIGHBOURS)
    cur = xs
    for l in range(DEPTH):
        cur, saved[l] = _layer_fwd(l, cur, mems, sm, Gathered(l), target if l == DEPTH - 1 else None)
    dxo, dxo_b, loss_part = cur

    small_g = [None] * DEPTH
    flight = {}

    class Exchange:
        def __init__(self):
            self.cores = {}

        def start(self, l, group, gives):
            *self.cores[l, group], token = _core_exchange_start(f"grad_core_start_{l}_{group}", gives)
            return token

        def landed(self, l, group, after):
            send_sems, recv_sems, bufs = self.cores[l, group]
            return _core_exchange_wait(f"grad_core_wait_{l}_{group}", send_sems, recv_sems, bufs, after)[1]

        def send(self, l, group, parts):
            *flight[l, group], token = _chip_exchange_start(f"grad_chip_start_{l}_{group}", parts)
            return token

    exchange = Exchange()
    for l in reversed(range(DEPTH)):
        dxo, dxo_b, small_g[l] = _layer_bwd(l, dxo, dxo_b, mems, sm, saved[l], place, exchange)
    grad_x = dxo

    groups = (("out", ("w_out",)), ("in", ("w_in", "w_mem_kv")))
    halves, stepped = dict.fromkeys(big), dict.fromkeys(big)
    small_g = {k: jnp.stack([small_g[l][k] for l in range(DEPTH)]) for k in SMALL_NAMES}
    after = grad_x
    sharing = {}

    def reduce_group(l, group, names):
        nonlocal after
        send_sems, recv_sems, bufs = flight[l, group]
        bufs = _chip_exchange_wait(f"grad_chip_wait_{l}_{group}", send_sems, recv_sems, bufs, after)
        for t, k in enumerate(names):
            halves[k] = _sum_chips(f"grad_chip_sum_{l}_{k}", bufs[t], bufs[len(names) + t], place, l, halves[k])
        *sharing[l, group], after = _core_share_start(f"grad_core_share_{l}_{group}", [halves[k] for k in names], l)

    def step(l, k, buf, half):
        nonlocal after
        tag = "" if half is None else "_" + half
        stepped[k] = _adamw(f"adamw_{k}_{l}{tag}", weights[k], buf, mom_m[k], mom_v[k], place, l, half, stepped[k],
                            after)
        after = stepped[k][1]

    def step_group(l, group, names, overlap):
        nonlocal after
        send_sems, recv_sems, bufs = sharing[l, group]
        if overlap:
            for k, buf in zip(names, bufs):
                step(l, k, buf, "own")
        bufs = _core_share_wait(f"grad_core_shared_{l}_{group}", send_sems, recv_sems, bufs, l, after)
        for k, buf in zip(names, bufs):
            halves[k] = buf
            step(l, k, buf, "other" if overlap else None)

    for l in reversed(range(DEPTH)):
        last = l == 0
        (g_out, n_out), (g_in, n_in) = groups
        reduce_group(l, g_out, n_out)
        if last:
            step_group(l, g_out, n_out, False)
            small_sum = _all_reduce_small(_pack_small(small_g, loss_part), after)
            small_step = _adamw_small(_pack_small(sm), small_sum, _pack_small({k: mom_m[k] for k in SMALL_NAMES}),
                                      _pack_small({k: mom_v[k] for k in SMALL_NAMES}))
            after = small_step[1]["sgu_w"]
        reduce_group(l, g_in, n_in)
        if not last:
            step_group(l, g_out, n_out, False)
        step_group(l, g_in, n_in, last)

    grads, delta, new_m, new_v = ({k: stepped[k][i] for k in big} for i in range(4))
    for out, small in zip((grads, delta, new_m, new_v), small_step):
        out.update(small)
    loss = small_sum[_small_layout()[1], 0]
    return (loss, grad_x[None], *[grads[k] for k in WEIGHT_ORDER], *[delta[k] for k in WEIGHT_ORDER],
            *[new_m[k] for k in WEIGHT_ORDER], *[new_v[k] for k in WEIGHT_ORDER])
```

```python
import math

import jax
import jax.numpy as jnp
from jax import lax
from jax.experimental import pallas as pl
from jax.experimental.pallas import tpu as pltpu

F32 = jnp.float32
BF16 = jnp.bfloat16
MESH = pl.DeviceIdType.MESH

D_MODEL = 2048
DEPTH = 2
CHUNK = 128
D_A = 1024
A_GROUPS = 8
D_B = 512
D_C = 512
HEADS = 4
HEAD_DIM = 128
IN_WIDTH = 6144
N_CHIPS = 4
EPS = 1e-6
ATT_SCALE = 1.0 / math.sqrt(HEAD_DIM)

OFF_U, OFF_V, OFF_ZA = 0, 1024, 2048
OFF_QB, OFF_KB, OFF_VB, OFF_ZB = 3072, 3584, 4096, 4608
OFF_QC, OFF_ZC = 5120, 5632
OFF_YB, OFF_YC = 1024, 1536

ADAM_LR = 0.001
ADAM_B1 = 0.9
ADAM_B2 = 0.999
ADAM_EPS = 1e-08
ADAM_WD = 0.01
ADAM_STEP = 10

MIB = 1024 * 1024
ANY = pl.BlockSpec(memory_space=pl.ANY)


def _params(semantics=None, vmem_mb=48):
    return pltpu.CompilerParams(dimension_semantics=semantics, vmem_limit_bytes=vmem_mb * MIB)


def _gelu(x):
    return 0.5 * x * (1.0 + lax.erf(x * (1.0 / math.sqrt(2.0))))


def _gelu_grad(x):
    cdf = 0.5 * (1.0 + lax.erf(x * (1.0 / math.sqrt(2.0))))
    pdf = jnp.exp(-0.5 * x * x) * (1.0 / math.sqrt(2.0 * math.pi))
    return cdf + x * pdf


def _sigmoid(x):
    return 1.0 / (1.0 + jnp.exp(-x))


def _silu_and_grad(z):
    s = _sigmoid(z)
    return z * s, s * (1.0 + z * (1.0 - s))


def _split_bf16(x):
    hi = x.astype(BF16)
    lo = (x - hi.astype(F32)).astype(BF16)
    return hi, lo


def _dot(a, b, dims):
    return lax.dot_general(a, b, (dims, ((), ())), preferred_element_type=F32)


NN = ((1,), (0,))
NT = ((1,), (1,))
TN = ((0,), (0,))


def _matmul(name, a, b, *, grid, a_spec, b_spec, o_spec, out_shape, dims, res=None, res_spec=None, after=None,
            place=None, into=None, vmem_mb=48):
    nk = grid[2]
    n_in = 2 + (res is not None) + (after is not None) + (into is not None)

    def body(*refs):
        if place is not None:
            refs = refs[1:]
        a_ref, b_ref = refs[0], refs[1]
        r_ref = refs[2] if res is not None else None
        o_ref = refs[n_in]
        if len(b_ref.shape) == 3 and dims == NN:
            part = _dot(a_ref[...], b_ref[...].reshape(-1, b_ref.shape[-1]), dims)
        elif len(b_ref.shape) == 3:
            width = b_ref.shape[-1]
            part = None
            for s in range(b_ref.shape[0]):
                term = _dot(a_ref[:, s * width:(s + 1) * width], b_ref[s], dims)
                part = term if part is None else part + term
        else:
            part = _dot(a_ref[...], b_ref[...], dims)
        if nk == 1:
            if r_ref is not None:
                part = part + r_ref[...]
            o_ref[...] = part.astype(o_ref.dtype)
            return
        acc_ref = refs[n_in + 1]
        k = pl.program_id(2)

        @pl.when(k == 0)
        def _():
            acc_ref[...] = part

        @pl.when(k > 0)
        def _():
            acc_ref[...] += part

        @pl.when(k == nk - 1)
        def _():
            tot = acc_ref[...]
            if r_ref is not None:
                tot = tot + r_ref[...]
            o_ref[...] = tot.astype(o_ref.dtype)

    in_specs = [a_spec, b_spec]
    args = [a, b]
    if res is not None:
        in_specs.append(res_spec)
        args.append(res)
    if after is not None:
        in_specs.append(ANY)
        args.append(after)
    aliases = {}
    if into is not None:
        in_specs.append(ANY)
        args.append(into)
        aliases = {len(args) - 1 + (place is not None): 0}
    acc_shape = tuple(d for d in o_spec.block_shape if d is not None)
    scratch = [pltpu.VMEM(acc_shape, F32)] if nk > 1 else []
    params = _params(("parallel", "parallel", "arbitrary"), vmem_mb)
    if place is not None:
        return pl.pallas_call(
            body, name=name, out_shape=out_shape, compiler_params=params, input_output_aliases=aliases,
            grid_spec=pltpu.PrefetchScalarGridSpec(num_scalar_prefetch=1, grid=grid, in_specs=in_specs,
                                                   out_specs=o_spec, scratch_shapes=scratch),
        )(place, *args)
    return pl.pallas_call(
        body, name=name, grid=grid, in_specs=in_specs, out_specs=o_spec, out_shape=out_shape,
        scratch_shapes=scratch, compiler_params=params, input_output_aliases=aliases,
    )(*args)


def _rms_fwd(name, x, g, tr, after=None, transposed=False):
    rows, d = x.shape

    def body(x_ref, g_ref, *refs):
        outs = refs[1:] if after is not None else refs
        xv = x_ref[...]
        r = lax.rsqrt(jnp.mean(xv * xv, axis=-1, keepdims=True) + EPS)
        h = xv * r * g_ref[...]
        outs[0][...] = h.astype(BF16)
        if transposed:
            outs[1][...] = h.T.astype(BF16)

    out_specs = [pl.BlockSpec((tr, d), lambda i: (i, 0))]
    out_shape = [jax.ShapeDtypeStruct((rows, d), BF16)]
    if transposed:
        out_specs.append(pl.BlockSpec((d, tr), lambda i: (0, i)))
        out_shape.append(jax.ShapeDtypeStruct((d, rows), BF16))
    outs = pl.pallas_call(
        body, name=name, grid=(rows // tr,),
        in_specs=[pl.BlockSpec((tr, d), lambda i: (i, 0)), pl.BlockSpec((1, d), lambda i: (0, 0))]
        + ([] if after is None else [ANY]),
        out_specs=out_specs, out_shape=out_shape,
        compiler_params=_params(("parallel",)),
    )(x, g, *([] if after is None else [after]))
    return outs if transposed else outs[0]


def _rms_bwd(name, x, dh, dres, g, tr, after=None):
    rows, d = x.shape

    def body(x_ref, dh_ref, dres_ref, g_ref, *refs):
        dx_ref, dxb_ref, dg_ref = refs[-3:]
        xv = x_ref[...]
        r = lax.rsqrt(jnp.mean(xv * xv, axis=-1, keepdims=True) + EPS)
        xhat = xv * r
        dhv = dh_ref[...]
        dxh = dhv * g_ref[...]
        dx = r * (dxh - xhat * jnp.mean(dxh * xhat, axis=-1, keepdims=True)) + dres_ref[...]
        dx_ref[...] = dx
        dxb_ref[...] = dx.astype(BF16)
        part = jnp.sum(dhv * xhat, axis=0, keepdims=True)

        @pl.when(pl.program_id(0) == 0)
        def _():
            dg_ref[...] = part

        @pl.when(pl.program_id(0) > 0)
        def _():
            dg_ref[...] += part

    blk = pl.BlockSpec((tr, d), lambda i: (i, 0))
    vec = pl.BlockSpec((1, d), lambda i: (0, 0))
    return pl.pallas_call(
        body, name=name, grid=(rows // tr,), in_specs=[blk, blk, blk, vec] + ([] if after is None else [ANY]),
        out_specs=[blk, blk, vec],
        out_shape=[jax.ShapeDtypeStruct((rows, d), F32), jax.ShapeDtypeStruct((rows, d), BF16),
                   jax.ShapeDtypeStruct((1, d), F32)],
        compiler_params=_params(("arbitrary",)),
    )(x, dh, dres, g, *([] if after is None else [after]))


def _rms_gain_grad(name, x, dh):
    rows, d = x.shape

    def body(x_ref, dh_ref, dg_ref):
        xv = x_ref[...]
        r = lax.rsqrt(jnp.mean(xv * xv, axis=-1, keepdims=True) + EPS)
        dg_ref[...] = jnp.sum(dh_ref[...] * xv * r, axis=0, keepdims=True)

    return pl.pallas_call(
        body, name=name, out_shape=jax.ShapeDtypeStruct((1, d), F32), compiler_params=_params(None),
    )(x, dh)


SB_T = 256
SB_HEADS = 4


LOG2E = 1.4426950408889634


def _sb_scores(q, kblk):
    z2 = _dot(q, kblk, NT) * (ATT_SCALE * LOG2E)
    e = jnp.exp2(-jnp.abs(z2))
    l1 = jnp.minimum(-z2, 0.0) - jnp.log2(1.0 + e)
    lb = l1 + z2
    return z2, e, lb, l1


def _sb_fwd(name, proj, after=None):
    s_len = proj.shape[0]
    t = SB_T
    nq = s_len // t

    def body(q_ref, k_ref, v_ref, *refs):
        o_ref = refs[-1]
        i = pl.program_id(1)
        row = lax.broadcasted_iota(jnp.int32, (t, t), 0)
        col = lax.broadcasted_iota(jnp.int32, (t, t), 1)
        causal = col < row
        after_mat = (row > col).astype(BF16)
        heads = [slice(hh * HEAD_DIM, (hh + 1) * HEAD_DIM) for hh in range(SB_HEADS)]
        q = [q_ref[:, sl].astype(BF16) for sl in heads]

        def tile(kb, state, masked):
            start = pl.multiple_of(kb * t, t)
            out = []
            for hh, sl in enumerate(heads):
                carry, acc = state[hh]
                kblk = k_ref[pl.ds(start, t), sl].astype(BF16)
                vblk = v_ref[pl.ds(start, t), sl].astype(BF16)
                _, _, lb, l1 = _sb_scores(q[hh], kblk)
                if masked:
                    l1 = jnp.where(causal, l1, 0.0)
                hi, lo = _split_bf16(l1)
                after = _dot(hi, after_mat, NN) + _dot(lo, after_mat, NN) + carry
                a = jnp.exp2(lb + after)
                if masked:
                    a = jnp.where(causal, a, 0.0)
                acc = acc + _dot(a.astype(BF16), vblk, NN)
                carry = carry + jnp.sum(l1, axis=-1, keepdims=True)
                out.append((carry, acc))
            return tuple(out)

        zero = (jnp.zeros((t, 1), F32), jnp.zeros((t, HEAD_DIM), F32))
        state = tile(i, (zero,) * SB_HEADS, True)
        state = lax.fori_loop(0, i, lambda n, st: tile(i - 1 - n, st, False), state)
        for hh, sl in enumerate(heads):
            o_ref[:, sl] = state[hh][1]

    cb = SB_HEADS * HEAD_DIM
    return pl.pallas_call(
        body, name=name, grid=(HEADS // SB_HEADS, nq),
        in_specs=[pl.BlockSpec((t, cb), lambda h, i: (i, OFF_QB // cb + h)),
                  pl.BlockSpec((s_len, cb), lambda h, i: (0, OFF_KB // cb + h)),
                  pl.BlockSpec((s_len, cb), lambda h, i: (0, OFF_VB // cb + h))] + ([] if after is None else [ANY]),
        out_specs=pl.BlockSpec((t, cb), lambda h, i: (i, h)),
        out_shape=jax.ShapeDtypeStruct((s_len, D_B), F32),
        compiler_params=_params(("parallel", "arbitrary")),
    )(proj, proj, proj, *([] if after is None else [after]))


def _sb_bwd(name, proj, dy, after=None):
    s_len = proj.shape[0]
    t = SB_T
    nq = s_len // t

    def body(q_ref, k_ref, v_ref, z_ref, dy_ref, *refs):
        dq_ref, dk_ref, dv_ref, a_ref, s_ref = refs[-5:]
        i = pl.program_id(1)

        @pl.when(i == 0)
        def _():
            dk_ref[...] = jnp.zeros_like(dk_ref)
            dv_ref[...] = jnp.zeros_like(dv_ref)

        heads = [slice(hh * HEAD_DIM, (hh + 1) * HEAD_DIM) for hh in range(SB_HEADS)]
        q = [q_ref[:, sl].astype(BF16) for sl in heads]
        silu_z, _ = _silu_and_grad(z_ref[...])
        do_all = dy_ref[...] * silu_z
        do_b = [do_all[:, sl].astype(BF16) for sl in heads]
        row = lax.broadcasted_iota(jnp.int32, (t, t), 0)
        col = lax.broadcasted_iota(jnp.int32, (t, t), 1)
        causal = col < row
        after_mat = (row > col).astype(BF16)
        before_mat = (row < col).astype(BF16)

        def weights(kb, carries, masked):
            start = pl.multiple_of(kb * t, t)
            out = []
            for hh, sl in enumerate(heads):
                kblk = k_ref[pl.ds(start, t), sl].astype(BF16)
                z, _, lb, l1 = _sb_scores(q[hh], kblk)
                if masked:
                    l1 = jnp.where(causal, l1, 0.0)
                hi, lo = _split_bf16(l1)
                after = _dot(hi, after_mat, NN) + _dot(lo, after_mat, NN) + carries[hh]
                a = jnp.exp2(lb + after)
                if masked:
                    a = jnp.where(causal, a, 0.0)
                a_ref[hh, kb] = a
                s_ref[hh, kb] = z
                out.append(carries[hh] + jnp.sum(l1, axis=-1, keepdims=True))
            return tuple(out)

        carries = weights(i, (jnp.zeros((t, 1), F32),) * SB_HEADS, True)
        lax.fori_loop(0, i, lambda n, c: weights(i - 1 - n, c, False), carries)

        def grads(kb, state, masked):
            start = pl.multiple_of(kb * t, t)
            out = []
            for hh, sl in enumerate(heads):
                carry, dq = state[hh]
                kblk = k_ref[pl.ds(start, t), sl].astype(BF16)
                vblk = v_ref[pl.ds(start, t), sl].astype(BF16)
                a = a_ref[hh, kb]
                z = s_ref[hh, kb]
                g = _dot(do_b[hh], vblk, NT) * a
                ghi, glo = _split_bf16(g)
                prefix = _dot(ghi, before_mat, NN) + _dot(glo, before_mat, NN) + carry
                e = jnp.exp2(-jnp.abs(z))
                inv = 1.0 / (1.0 + e)
                pos = z >= 0.0
                beta = jnp.where(pos, inv, e * inv)
                one_m_beta = jnp.where(pos, e * inv, inv)
                dz = (g * one_m_beta - prefix * beta) * ATT_SCALE
                if masked:
                    dz = jnp.where(causal, dz, 0.0)
                dz_b = dz.astype(BF16)
                dq = dq + _dot(dz_b, kblk, NN)
                dk_ref[pl.ds(start, t), sl] += _dot(dz_b, q[hh], TN)
                dv_ref[pl.ds(start, t), sl] += _dot(a.astype(BF16), do_b[hh], TN)
                out.append((carry + jnp.sum(g, axis=-1, keepdims=True), dq))
            return tuple(out)

        zero = (jnp.zeros((t, 1), F32), jnp.zeros((t, HEAD_DIM), F32))
        state = lax.fori_loop(0, i, lambda kb, st: grads(kb, st, False), (zero,) * SB_HEADS)
        state = grads(i, state, True)
        for hh, sl in enumerate(heads):
            dq_ref[:, sl] = state[hh][1]

    cb = SB_HEADS * HEAD_DIM
    qblk = lambda off: pl.BlockSpec((t, cb), lambda h, i: (i, off // cb + h))
    full = lambda off: pl.BlockSpec((s_len, cb), lambda h, i: (0, off // cb + h))
    out = jax.ShapeDtypeStruct((s_len, D_B), F32)
    return pl.pallas_call(
        body, name=name, grid=(HEADS // SB_HEADS, nq),
        in_specs=[qblk(OFF_QB), full(OFF_KB), full(OFF_VB), qblk(OFF_ZB), qblk(OFF_YB)]
        + ([] if after is None else [ANY]),
        out_specs=[qblk(0), full(0), full(0)],
        out_shape=[out, out, out],
        scratch_shapes=[pltpu.VMEM((SB_HEADS, nq, t, t), F32), pltpu.VMEM((SB_HEADS, nq, t, t), F32)],
        compiler_params=_params(("parallel", "arbitrary")),
    )(proj, proj, proj, proj, dy, *([] if after is None else [after]))


MEM_TQ = 512


def _qk_norm(x, g):
    r = lax.rsqrt(jnp.mean(x * x, axis=-1, keepdims=True) + EPS)
    xhat = x * r
    return xhat * g, xhat, r


def _qk_norm_bwd(dn, g, xhat, r):
    dxh = dn * g
    return r * (dxh - xhat * jnp.mean(dxh * xhat, axis=-1, keepdims=True))


def _mem_probs(q, mk, qg, kg):
    qn, qhat, rq = _qk_norm(q, qg)
    kn, khat, rk = _qk_norm(mk, kg)
    qn_b, kn_b = qn.astype(BF16), kn.astype(BF16)
    s = _dot(qn_b, kn_b, NT) * ATT_SCALE
    p = jnp.exp(s - jnp.max(s, axis=-1, keepdims=True))
    p = p / jnp.sum(p, axis=-1, keepdims=True)
    return p, qn_b, kn_b, qhat, rq, khat, rk


def _mem_fwd(name, proj, mem_kv, qg, kg):
    s_len = proj.shape[0]
    m_len = mem_kv.shape[0]
    tq = min(MEM_TQ, s_len)

    def body(q_ref, mk_ref, mv_ref, qg_ref, kg_ref, o_ref):
        p = _mem_probs(q_ref[...], mk_ref[...], qg_ref[...], kg_ref[...])[0]
        o_ref[...] = _dot(p.astype(BF16), mv_ref[...].astype(BF16), NN)

    cb = HEAD_DIM
    vec = pl.BlockSpec((1, cb), lambda h, i: (0, 0))
    return pl.pallas_call(
        body, name=name, grid=(HEADS, s_len // tq),
        in_specs=[pl.BlockSpec((tq, cb), lambda h, i: (i, OFF_QC // cb + h)),
                  pl.BlockSpec((m_len, cb), lambda h, i: (0, h)),
                  pl.BlockSpec((m_len, cb), lambda h, i: (0, HEADS + h)), vec, vec],
        out_specs=pl.BlockSpec((tq, cb), lambda h, i: (i, h)),
        out_shape=jax.ShapeDtypeStruct((s_len, D_C), F32),
        compiler_params=_params(("parallel", "parallel")),
    )(proj, mem_kv, mem_kv, qg, kg)


def _mem_bwd(name, proj, mem_kv, qg, kg, dy):
    s_len = proj.shape[0]
    m_len = mem_kv.shape[0]
    tq = min(MEM_TQ, s_len)

    def body(q_ref, mk_ref, mv_ref, qg_ref, kg_ref, z_ref, dy_ref, dq_ref, dmk_ref, dmv_ref, dqg_ref, dkg_ref):
        h, i = pl.program_id(0), pl.program_id(1)

        @pl.when(i == 0)
        def _():
            dmk_ref[...] = jnp.zeros_like(dmk_ref)
            dmv_ref[...] = jnp.zeros_like(dmv_ref)

        @pl.when((i == 0) & (h == 0))
        def _():
            dqg_ref[...] = jnp.zeros_like(dqg_ref)
            dkg_ref[...] = jnp.zeros_like(dkg_ref)

        qg, kg = qg_ref[...], kg_ref[...]
        p, qn_b, kn_b, qhat, rq, khat, rk = _mem_probs(q_ref[...], mk_ref[...], qg, kg)
        silu_z, _ = _silu_and_grad(z_ref[...])
        do_b = (dy_ref[...] * silu_z).astype(BF16)
        dmv_ref[...] += _dot(p.astype(BF16), do_b, TN)
        dp = _dot(do_b, mv_ref[...].astype(BF16), NT)
        ds = (p * (dp - jnp.sum(dp * p, axis=-1, keepdims=True)) * ATT_SCALE).astype(BF16)
        dqn = _dot(ds, kn_b, NN)
        dkn = _dot(ds, qn_b, TN)
        dq_ref[...] = _qk_norm_bwd(dqn, qg, qhat, rq)
        dmk_ref[...] += _qk_norm_bwd(dkn, kg, khat, rk)
        dqg_ref[...] += jnp.sum(dqn * qhat, axis=0, keepdims=True)
        dkg_ref[...] += jnp.sum(dkn * khat, axis=0, keepdims=True)

    cb = HEAD_DIM
    vec = pl.BlockSpec((1, cb), lambda h, i: (0, 0))
    qblk = lambda off: pl.BlockSpec((tq, cb), lambda h, i: (i, off // cb + h))
    memblk = lambda off: pl.BlockSpec((m_len, cb), lambda h, i: (0, off + h))
    return pl.pallas_call(
        body, name=name, grid=(HEADS, s_len // tq),
        in_specs=[qblk(OFF_QC), memblk(0), memblk(HEADS), vec, vec, qblk(OFF_ZC), qblk(OFF_YC)],
        out_specs=[qblk(0), memblk(0), memblk(0), vec, vec],
        out_shape=[jax.ShapeDtypeStruct((s_len, D_C), F32), jax.ShapeDtypeStruct((m_len, D_C), F32),
                   jax.ShapeDtypeStruct((m_len, D_C), F32), jax.ShapeDtypeStruct((1, cb), F32),
                   jax.ShapeDtypeStruct((1, cb), F32)],
        compiler_params=_params(("arbitrary", "arbitrary")),
    )(proj, mem_kv, mem_kv, qg, kg, proj, dy)


def _sgu_common(u_ref, v_ref, lng_ref, lnb_ref, w_ref, bias_ref):
    ug = _gelu(u_ref[...])
    vg = _gelu(v_ref[...])
    mu = jnp.mean(vg, axis=-1, keepdims=True)
    xc = vg - mu
    rstd = lax.rsqrt(jnp.mean(xc * xc, axis=-1, keepdims=True) + EPS)
    xhat = xc * rstd
    vn = xhat * lng_ref[...] + lnb_ref[...]
    vn_b = vn.astype(BF16)
    row = lax.broadcasted_iota(jnp.int32, (CHUNK, CHUNK), 0)
    col = lax.broadcasted_iota(jnp.int32, (CHUNK, CHUNK), 1)
    tril = row >= col
    mixed = []
    for g in range(A_GROUPS):
        w = jnp.where(tril, w_ref[g], 0.0).astype(BF16)
        sl = slice(g * CHUNK, (g + 1) * CHUNK)
        mixed.append(_dot(w, vn_b[:, sl], NN) + bias_ref[:, sl])
    return ug, xhat, rstd, vn_b, mixed, tril


def _gate_fwd(name, proj, o_b, o_c, lng, lnb, w_s, bias):
    s_len = proj.shape[0]

    def body(u_ref, v_ref, za_ref, zb_ref, zc_ref, ob_ref, oc_ref, lng_ref, lnb_ref, w_ref, bias_ref, y_ref):
        ug, _, _, _, mixed, _ = _sgu_common(u_ref, v_ref, lng_ref, lnb_ref, w_ref, bias_ref)
        sza, _ = _silu_and_grad(za_ref[...])
        gate = ug * sza

        def put(off, width, val):
            y_ref[:, off:off + width] = val.astype(BF16)

        for g in range(A_GROUPS):
            sl = slice(g * CHUNK, (g + 1) * CHUNK)
            put(g * CHUNK, CHUNK, gate[:, sl] * mixed[g])
        szb, _ = _silu_and_grad(zb_ref[...])
        put(OFF_YB, D_B, ob_ref[...] * szb)
        szc, _ = _silu_and_grad(zc_ref[...])
        put(OFF_YC, D_C, oc_ref[...] * szc)

    wide = lambda off: pl.BlockSpec((CHUNK, D_A), lambda i: (i, off // D_A))
    narrow = lambda off: pl.BlockSpec((CHUNK, D_B), lambda i: (i, off // D_B))
    vec = pl.BlockSpec((1, D_A), lambda i: (0, 0))
    return pl.pallas_call(
        body, name=name, grid=(s_len // CHUNK,),
        in_specs=[wide(OFF_U), wide(OFF_V), wide(OFF_ZA), narrow(OFF_ZB), narrow(OFF_ZC), narrow(0), narrow(0), vec, vec,
                  pl.BlockSpec((A_GROUPS, CHUNK, CHUNK), lambda i: (0, 0, 0)),
                  pl.BlockSpec((CHUNK, D_A), lambda i: (0, 0))],
        out_specs=pl.BlockSpec((CHUNK, D_MODEL), lambda i: (i, 0)),
        out_shape=jax.ShapeDtypeStruct((s_len, D_MODEL), BF16),
        compiler_params=_params(("parallel",)),
    )(proj, proj, proj, proj, proj, o_b, o_c, lng, lnb, w_s, bias)


def _gate_bwd(name, proj, dy, o_b, o_c, dqkv, dq_c, lng, lnb, w_s, w_s_t, bias):
    s_len = proj.shape[0]
    n = s_len // CHUNK
    dq_b, dk_b, dv_b = dqkv

    def body(u_ref, v_ref, za_ref, zb_ref, zc_ref, dya_ref, dyb_ref, dyc_ref, ob_ref, oc_ref, dq_ref, dk_ref, dv_ref,
             dqc_ref, lng_ref, lnb_ref, w_ref, wt_ref, bias_ref, dp_ref, dw_ref, dsb_ref, dlng_ref, dlnb_ref, dbias_ref):
        i = pl.program_id(0)

        @pl.when(i == 0)
        def _():
            dw_ref[...] = jnp.zeros_like(dw_ref)
            dbias_ref[...] = jnp.zeros_like(dbias_ref)
            dlng_ref[...] = jnp.zeros_like(dlng_ref)
            dlnb_ref[...] = jnp.zeros_like(dlnb_ref)

        ug, xhat, rstd, vn_b, mixed, tril = _sgu_common(u_ref, v_ref, lng_ref, lnb_ref, w_ref, bias_ref)
        za = za_ref[...]
        sza, dsza = _silu_and_grad(za)
        dya = dya_ref[...]
        mixed_all = jnp.concatenate(mixed, axis=-1)
        d_mixed = dya * ug * sza
        dp_ref[:, OFF_U:OFF_U + D_A] = (dya * mixed_all * sza * _gelu_grad(u_ref[...])).astype(BF16)
        dp_ref[:, OFF_ZA:OFF_ZA + D_A] = (dya * ug * mixed_all * dsza).astype(BF16)
        dbias_ref[...] += d_mixed
        dm_b = d_mixed.astype(BF16)
        triu = lax.broadcasted_iota(jnp.int32, (CHUNK, CHUNK), 0) <= lax.broadcasted_iota(jnp.int32, (CHUNK, CHUNK), 1)
        d_vn = []
        for g in range(A_GROUPS):
            sl = slice(g * CHUNK, (g + 1) * CHUNK)
            wt = jnp.where(triu, wt_ref[g], 0.0).astype(BF16)
            d_vn.append(_dot(wt, dm_b[:, sl], NN))
            dw_ref[g] += jnp.where(tril, _dot(dm_b[:, sl], vn_b[:, sl], NT), 0.0)
        d_vn = jnp.concatenate(d_vn, axis=-1)
        dlng_ref[...] += jnp.sum(d_vn * xhat, axis=0, keepdims=True)
        dlnb_ref[...] += jnp.sum(d_vn, axis=0, keepdims=True)
        dxh = d_vn * lng_ref[...]
        d_vg = rstd * (dxh - jnp.mean(dxh, axis=-1, keepdims=True)
                       - xhat * jnp.mean(dxh * xhat, axis=-1, keepdims=True))
        dp_ref[:, OFF_V:OFF_V + D_A] = (d_vg * _gelu_grad(v_ref[...])).astype(BF16)
        dp_ref[:, OFF_QB:OFF_QB + D_B] = dq_ref[...].astype(BF16)
        dp_ref[:, OFF_KB:OFF_KB + D_B] = dk_ref[...].astype(BF16)
        dp_ref[:, OFF_VB:OFF_VB + D_B] = dv_ref[...].astype(BF16)
        _, dszb = _silu_and_grad(zb_ref[...])
        dp_ref[:, OFF_ZB:OFF_ZB + D_B] = (dyb_ref[...] * ob_ref[...] * dszb).astype(BF16)
        dp_ref[:, OFF_QC:OFF_QC + D_C] = dqc_ref[...].astype(BF16)
        _, dszc = _silu_and_grad(zc_ref[...])
        dp_ref[:, OFF_ZC:OFF_ZC + D_C] = (dyc_ref[...] * oc_ref[...] * dszc).astype(BF16)

        @pl.when(i == n - 1)
        def _():
            ch = lax.broadcasted_iota(jnp.int32, (D_A, CHUNK), 0)
            gcol = lax.broadcasted_iota(jnp.int32, (D_A, CHUNK), 1)
            pick = (ch // (D_A // A_GROUPS) == gcol).astype(BF16)
            rest = dbias_ref[...]
            tot = jnp.zeros((CHUNK, CHUNK), F32)
            for _ in range(3):
                term = rest.astype(BF16)
                tot = tot + _dot(term, pick, NN)
                rest = rest - term.astype(F32)
            dsb_ref[...] = tot

    wide = lambda off: pl.BlockSpec((CHUNK, D_A), lambda i: (i, off // D_A))
    narrow = lambda off: pl.BlockSpec((CHUNK, D_B), lambda i: (i, off // D_B))
    vec = pl.BlockSpec((1, D_A), lambda i: (0, 0))
    wspec = pl.BlockSpec((A_GROUPS, CHUNK, CHUNK), lambda i: (0, 0, 0))
    bspec = pl.BlockSpec((CHUNK, D_A), lambda i: (0, 0))
    return pl.pallas_call(
        body, name=name, grid=(n,),
        in_specs=[wide(OFF_U), wide(OFF_V), wide(OFF_ZA), narrow(OFF_ZB), narrow(OFF_ZC),
                  wide(0), narrow(OFF_YB), narrow(OFF_YC), narrow(0), narrow(0), narrow(0), narrow(0), narrow(0),
                  narrow(0), vec, vec, wspec, wspec, bspec],
        out_specs=[pl.BlockSpec((CHUNK, IN_WIDTH), lambda i: (i, 0)), wspec,
                   pl.BlockSpec((CHUNK, CHUNK), lambda i: (0, 0)), vec, vec],
        out_shape=[jax.ShapeDtypeStruct((s_len, IN_WIDTH), BF16), jax.ShapeDtypeStruct((A_GROUPS, CHUNK, CHUNK), F32),
                   jax.ShapeDtypeStruct((CHUNK, CHUNK), F32), jax.ShapeDtypeStruct((1, D_A), F32),
                   jax.ShapeDtypeStruct((1, D_A), F32)],
        scratch_shapes=[pltpu.VMEM((CHUNK, D_A), F32)],
        compiler_params=_params(("arbitrary",)),
    )(proj, proj, proj, proj, proj, dy, dy, dy, o_b, o_c, dq_b, dk_b, dv_b, dq_c, lng, lnb, w_s, w_s_t, bias)


IN_SHARD = IN_WIDTH // N_CHIPS
ROW_SHARD = D_MODEL // N_CHIPS


def _bias_rows(sgu_b_l):
    return jnp.repeat(sgu_b_l.T, D_A // A_GROUPS, axis=1)


def _layer_fwd(l, x, mem, sm, hooks, target=None):
    s_len = x.shape[0]
    m_len = mem.shape[0]
    tm = min(1024, s_len)
    h = _rms_fwd(f"rms_fwd_{l}", x, sm["norm_g"][l][None], min(256, s_len))
    proj, stage = None, 0
    while (ready := hooks.w_in(stage, h, proj)) is not None:
        w_in_all, order, first, count = ready
        proj = _matmul(
            f"in_proj_{l}_{stage}", h, w_in_all, grid=(s_len // tm, count, 1), place=order, into=proj,
            a_spec=pl.BlockSpec((tm, D_MODEL), lambda i, j, k, p: (i, 0)),
            b_spec=pl.BlockSpec((None, D_MODEL, IN_SHARD), lambda i, j, k, p: (p[first + j], 0, 0)),
            o_spec=pl.BlockSpec((tm, IN_SHARD), lambda i, j, k, p: (i, p[first + j])),
            out_shape=jax.ShapeDtypeStruct((s_len, IN_WIDTH), F32), dims=NN)
        stage += 1
    o_b = _sb_fwd(f"sb_fwd_{l}", proj, hooks.rest_start(proj))
    w_kv_all, w_out_all, after = hooks.rest_finish(o_b)
    mem_h = _rms_fwd(f"mem_rms_fwd_{l}", mem, sm["mem_norm_g"][l][None], m_len, after)
    mem_kv = _matmul(
        f"mem_kv_{l}", mem_h, w_kv_all, grid=(1, 2, N_CHIPS),
        a_spec=pl.BlockSpec((m_len, ROW_SHARD), lambda i, j, k: (0, k)),
        b_spec=pl.BlockSpec((None, ROW_SHARD, D_C), lambda i, j, k: (k, 0, j)),
        o_spec=pl.BlockSpec((m_len, D_C), lambda i, j, k: (0, j)),
        out_shape=jax.ShapeDtypeStruct((m_len, 2 * D_C), F32), dims=NN)
    qg, kg = sm["q_norm_g"][l][None], sm["k_norm_g"][l][None]
    o_c = _mem_fwd(f"mem_fwd_{l}", proj, mem_kv, qg, kg)
    bias = _bias_rows(sm["sgu_b"][l])
    y = _gate_fwd(f"gate_fwd_{l}", proj, o_b, o_c, sm["sgu_ln_g"][l][None], sm["sgu_ln_b"][l][None],
                  sm["sgu_w"][l], bias)
    saved = dict(x=x, h=h, proj=proj, mem_h=mem_h, mem_kv=mem_kv, o_b=o_b, o_c=o_c, y=y, bias=bias,
                 weights=(w_in_all, w_kv_all, w_out_all))
    if target is not None:
        return _out_proj_loss(f"out_proj_{l}", y, w_out_all, x, target, tm), saved
    tn_o = 512
    x_next = _matmul(
        f"out_proj_{l}", y, w_out_all, grid=(s_len // tm, D_MODEL // tn_o, 1),
        a_spec=pl.BlockSpec((tm, D_MODEL), lambda i, j, k: (i, 0)),
        b_spec=pl.BlockSpec((N_CHIPS, ROW_SHARD, tn_o), lambda i, j, k: (0, 0, j)),
        o_spec=pl.BlockSpec((tm, tn_o), lambda i, j, k: (i, j)),
        out_shape=jax.ShapeDtypeStruct((s_len, D_MODEL), F32), dims=NN,
        res=x, res_spec=pl.BlockSpec((tm, tn_o), lambda i, j, k: (i, j)), after=hooks.before_out(y))
    return x_next, saved


def _out_proj_loss(name, y, w_out_all, x, target, tm):
    s_len, d = x.shape
    tn = 512
    n_i = s_len // tm

    n_j = d // tn

    def body(y_ref, w_ref, x_ref, t_ref, dx_ref, dxb_ref, loss_ref, acc_ref):
        i, j = pl.program_id(0), pl.program_id(1)
        out = _dot(y_ref[...], w_ref[...].reshape(-1, tn), NN) + x_ref[...]
        e = out - t_ref[...]
        dx = e * (1.0 / d)
        dx_ref[...] = dx
        dxb_ref[...] = dx.astype(BF16)
        part = jnp.sum(e * e, axis=0, keepdims=True)

        @pl.when((i == 0) & (j == 0))
        def _():
            acc_ref[...] = part

        @pl.when((i > 0) | (j > 0))
        def _():
            acc_ref[...] += part

        @pl.when((i == n_i - 1) & (j == n_j - 1))
        def _():
            loss_ref[...] = jnp.sum(acc_ref[...], axis=-1, keepdims=True) * (0.5 / d)

    blk = pl.BlockSpec((tm, tn), lambda i, j: (i, j))
    return pl.pallas_call(
        body, name=name, grid=(n_i, n_j),
        in_specs=[pl.BlockSpec((tm, d), lambda i, j: (i, 0)), pl.BlockSpec((N_CHIPS, ROW_SHARD, tn), lambda i, j: (0, 0, j)),
                  blk, blk],
        out_specs=[blk, blk, pl.BlockSpec((1, 1), lambda i, j: (0, 0))],
        out_shape=[jax.ShapeDtypeStruct((s_len, d), F32), jax.ShapeDtypeStruct((s_len, d), BF16),
                   jax.ShapeDtypeStruct((1, 1), F32)],
        scratch_shapes=[pltpu.VMEM((1, tn), F32)],
        compiler_params=_params(("arbitrary", "arbitrary")),
    )(y, w_out_all, x, target)


def _layer_bwd(l, dxo, dxo_b, mem, sm, saved, place, exchange):
    s_len = dxo.shape[0]
    m_len = mem.shape[0]
    proj, y, h, mem_h, mem_kv = saved["proj"], saved["y"], saved["h"], saved["mem_h"], saved["mem_kv"]
    w_in_all, w_kv_all, w_out_all = saved["weights"]
    tm = min(1024, s_len)
    tn = IN_SHARD
    per = IN_SHARD // tn
    half_rows = ROW_SHARD // 2

    def halves(make):
        give = lambda: make("give", lambda p: 1 - p[1], None, BF16)
        keep = lambda theirs: make("keep", lambda p: p[1], theirs, BF16)
        return give, keep

    def grad_out(tag, half, theirs, dtype):
        o_spec = pl.BlockSpec((None, half_rows, 1024), lambda i, j, k, p: (i, 0, j))
        return _matmul(
            f"d_w_out_{l}_{tag}", y, dxo_b, grid=(N_CHIPS, D_MODEL // 1024, 1), place=place,
            a_spec=pl.BlockSpec((s_len, half_rows), lambda i, j, k, p: (0, 2 * i + half(p))),
            b_spec=pl.BlockSpec((s_len, 1024), lambda i, j, k, p: (0, j)), o_spec=o_spec,
            out_shape=jax.ShapeDtypeStruct((N_CHIPS, half_rows, D_MODEL), dtype), dims=TN,
            res=theirs, res_spec=o_spec)

    def grad_in(tag, half, theirs, dtype):
        o_spec = pl.BlockSpec((None, D_MODEL // 2, tn), lambda i, j, k, p: (j // per, 0, j % per))
        return _matmul(
            f"d_w_in_{l}_{tag}", h, dproj, grid=(1, IN_WIDTH // tn, 1), place=place,
            a_spec=pl.BlockSpec((s_len, D_MODEL // 2), lambda i, j, k, p: (0, half(p))),
            b_spec=pl.BlockSpec((s_len, tn), lambda i, j, k, p: (0, j)), o_spec=o_spec,
            out_shape=jax.ShapeDtypeStruct((N_CHIPS, D_MODEL // 2, IN_SHARD), dtype), dims=TN,
            res=theirs, res_spec=o_spec)

    def grad_kv(tag, half, theirs, dtype):
        o_spec = pl.BlockSpec((None, half_rows, 2 * D_C), lambda i, j, k, p: (i, 0, 0))
        return _matmul(
            f"d_w_kv_{l}_{tag}", mem_h, dkv_b, grid=(N_CHIPS, 1, 1), place=place,
            a_spec=pl.BlockSpec((m_len, half_rows), lambda i, j, k, p: (0, 2 * i + half(p))),
            b_spec=pl.BlockSpec((m_len, 2 * D_C), lambda i, j, k, p: (0, 0)), o_spec=o_spec,
            out_shape=jax.ShapeDtypeStruct((N_CHIPS, half_rows, 2 * D_C), dtype), dims=TN,
            res=theirs, res_spec=o_spec)

    give_out, keep_out = halves(grad_out)
    token = exchange.start(l, "out", [give_out()])
    dy = _matmul(
        f"d_y_{l}", dxo_b, w_out_all, grid=(s_len // tm, N_CHIPS, 1),
        a_spec=pl.BlockSpec((tm, D_MODEL), lambda i, j, k: (i, 0)),
        b_spec=pl.BlockSpec((None, ROW_SHARD, D_MODEL), lambda i, j, k: (j, 0, 0)),
        o_spec=pl.BlockSpec((tm, ROW_SHARD), lambda i, j, k: (i, j)),
        out_shape=jax.ShapeDtypeStruct((s_len, D_MODEL), F32), dims=NT, after=token)
    (theirs_out,) = exchange.landed(l, "out", dy)
    token = exchange.send(l, "out", [keep_out(theirs_out)])
    qg, kg = sm["q_norm_g"][l][None], sm["k_norm_g"][l][None]
    dqkv = _sb_bwd(f"sb_bwd_{l}", proj, dy, token)
    dq_c, dmk, dmv, dqg, dkg = _mem_bwd(f"mem_bwd_{l}", proj, mem_kv, qg, kg, dy)
    w_s = sm["sgu_w"][l]
    dproj, dws, dbias, dlng, dlnb = _gate_bwd(
        f"gate_bwd_{l}", proj, dy, saved["o_b"], saved["o_c"], dqkv, dq_c, sm["sgu_ln_g"][l][None],
        sm["sgu_ln_b"][l][None], w_s, jnp.swapaxes(w_s, 1, 2), saved["bias"])
    dkv_b = jnp.concatenate([dmk, dmv], axis=1).astype(BF16)
    give_in, keep_in = halves(grad_in)
    give_kv, keep_kv = halves(grad_kv)
    token = exchange.start(l, "in", [give_in(), give_kv()])
    dh = _matmul(
        f"d_h_{l}", dproj, w_in_all, grid=(s_len // tm, D_MODEL // 512, 1),
        a_spec=pl.BlockSpec((tm, IN_WIDTH), lambda i, j, k: (i, 0)),
        b_spec=pl.BlockSpec((N_CHIPS, 512, IN_SHARD), lambda i, j, k: (0, j, 0)),
        o_spec=pl.BlockSpec((tm, 512), lambda i, j, k: (i, j)),
        out_shape=jax.ShapeDtypeStruct((s_len, D_MODEL), F32), dims=NT, after=token, vmem_mb=56)
    theirs_in, theirs_kv = exchange.landed(l, "in", dh)
    token = exchange.send(l, "in", [keep_in(theirs_in), keep_kv(theirs_kv)])
    dx, dx_b, dng = _rms_bwd(f"rms_bwd_{l}", saved["x"], dh, dxo, sm["norm_g"][l][None], min(256, s_len), token)
    d_mem_h = _matmul(
        f"d_mem_h_{l}", dkv_b, w_kv_all, grid=(1, N_CHIPS, 1),
        a_spec=pl.BlockSpec((m_len, 2 * D_C), lambda i, j, k: (0, 0)),
        b_spec=pl.BlockSpec((None, ROW_SHARD, 2 * D_C), lambda i, j, k: (j, 0, 0)),
        o_spec=pl.BlockSpec((m_len, ROW_SHARD), lambda i, j, k: (0, j)),
        out_shape=jax.ShapeDtypeStruct((m_len, D_MODEL), F32), dims=NT)
    dmng = _rms_gain_grad(f"mem_rms_bwd_{l}", mem, d_mem_h)
    dsgu_b = dbias[:, :A_GROUPS].T
    small = dict(norm_g=dng[0], sgu_ln_g=dlng[0], sgu_ln_b=dlnb[0], sgu_w=dws, sgu_b=dsgu_b, mem_norm_g=dmng[0],
                 q_norm_g=dqg[0], k_norm_g=dkg[0])
    return dx, dx_b, small


SMALL_NAMES = ("norm_g", "sgu_ln_g", "sgu_ln_b", "sgu_w", "sgu_b", "mem_norm_g", "q_norm_g", "k_norm_g")


def _place():
    x, y, c = lax.axis_index("x"), lax.axis_index("y"), lax.axis_index("c")
    return x, y, c


def _other_chips(x, y):
    return [(1 - x, y, 2 * (1 - x) + y), (x, 1 - y, 2 * x + 1 - y), (1 - x, 1 - y, 2 * (1 - x) + 1 - y)]


D2D_CHUNKS = 8


def _place_index():
    return jnp.stack([2 * lax.axis_index("x") + lax.axis_index("y"), lax.axis_index("c")]).astype(jnp.int32)


def _cast_into_slot(name, w, l, place):
    _, rows, cols = w.shape
    tr = min(256, rows)

    def body(p_ref, w_ref, o_ref):
        o_ref[...] = w_ref[...].astype(BF16)

    return pl.pallas_call(
        body, name=name,
        grid_spec=pltpu.PrefetchScalarGridSpec(
            num_scalar_prefetch=1, grid=(rows // tr,),
            in_specs=[pl.BlockSpec((None, tr, cols), lambda i, p: (l, i, 0))],
            out_specs=pl.BlockSpec((None, tr, cols), lambda i, p: (p[0], i, 0))),
        out_shape=jax.ShapeDtypeStruct((N_CHIPS, rows, cols), BF16),
        compiler_params=_params(("parallel",)),
    )(place, w)


HBM = pl.BlockSpec(memory_space=pltpu.HBM)
SEM = pl.BlockSpec(memory_space=pltpu.SEMAPHORE)
DATAFLOW = pltpu.SideEffectType.DATAFLOW_SIDE_EFFECTING


def _in_hbm(a):
    return pltpu.with_memory_space_constraint(a, pltpu.HBM)


ALL_PEERS = (0, 1, 2)
NEIGHBOURS = (0, 1)
DIAGONAL = (2,)


def _chip_copies_start(name, srcs, lands, make_copy, after=None, peers=ALL_PEERS):
    n_t = len(srcs)
    in_place = lands is None
    n_after = 0 if after is None else 1

    def body(*refs):
        src = refs[:n_t]
        k = (n_t if in_place else 2 * n_t) + n_after
        send_sems, recv_sems = refs[k], refs[k + 1]
        land = refs[k + 2:k + 2 + n_t] if in_place else refs[k + 2 + n_t:k + 2 + 2 * n_t]
        token = refs[-1]
        x, y, c = _place()
        me = 2 * x + y
        others = _other_chips(x, y)
        for t in range(n_t):
            for px, py, pk in [others[p] for p in peers]:
                s, d = make_copy(src[t], land[t], me, pk, c)
                pltpu.make_async_remote_copy(
                    src_ref=s, dst_ref=d, send_sem=send_sems.at[t], recv_sem=recv_sems.at[t],
                    device_id=(px, py, c), device_id_type=MESH).start()
        token[...] = jnp.zeros_like(token)

    bufs = list(srcs) if in_place else list(srcs) + list(lands)
    outs = pl.pallas_call(
        body, name=name,
        in_specs=[HBM] * len(bufs) + [ANY] * n_after,
        out_specs=[SEM, SEM] + [HBM] * len(bufs) + [pl.BlockSpec(memory_space=pltpu.VMEM)],
        out_shape=[pltpu.SemaphoreType.DMA((n_t,)), pltpu.SemaphoreType.DMA((n_t,))]
        + [pltpu.HBM(b.shape, b.dtype) for b in bufs] + [jax.ShapeDtypeStruct((8, 128), F32)],
        input_output_aliases={i: 2 + i for i in range(len(bufs))},
        compiler_params=pltpu.CompilerParams(has_side_effects=DATAFLOW),
    )(*[_in_hbm(b) for b in bufs], *([] if after is None else [after]))
    return outs[0], outs[1], list(outs[2:2 + len(bufs)]), outs[-1]


def _chip_copies_wait(name, send_sems, recv_sems, bufs, sent, landed, after):
    n_b = len(bufs)

    def body(*refs):
        buf = refs[:n_b]
        send_ref, recv_ref = refs[n_b], refs[n_b + 1]
        x, y, c = _place()
        for t, (s, d) in enumerate(zip(sent(buf), landed(buf))):
            out = pltpu.make_async_remote_copy(src_ref=s, dst_ref=s, send_sem=send_ref.at[t], recv_sem=recv_ref.at[t],
                                               device_id=(x, y, c), device_id_type=MESH)
            out.wait_send()
            arrived = pltpu.make_async_remote_copy(src_ref=d, dst_ref=d, send_sem=send_ref.at[t],
                                                   recv_sem=recv_ref.at[t], device_id=(x, y, c), device_id_type=MESH)
            arrived.wait_recv()

    after = list(after) if isinstance(after, (list, tuple)) else [after]
    return pl.pallas_call(
        body, name=name,
        in_specs=[HBM] * n_b + [SEM, SEM] + [ANY] * len(after), out_specs=[HBM] * n_b,
        out_shape=[pltpu.HBM(b.shape, b.dtype) for b in bufs],
        input_output_aliases={i: i for i in range(n_b)},
        compiler_params=pltpu.CompilerParams(has_side_effects=DATAFLOW),
    )(*bufs, send_sems, recv_sems, *after)


def _gather_start(name, bufs, after=None, peers=ALL_PEERS):
    def make_copy(src, land, me, pk, c):
        hr = src.shape[1] // 2
        return src.at[me, pl.ds(c * hr, hr)], land.at[me, pl.ds(c * hr, hr)]

    return _chip_copies_start(name, bufs, None, make_copy, after, peers)


def _gather_wait(name, send_sems, recv_sems, bufs, after, peers=ALL_PEERS):
    def half_shards(buf):
        return [b.at[pl.ds(0, len(peers)), pl.ds(0, b.shape[1] // 2)] for b in buf]

    return _chip_copies_wait(name, send_sems, recv_sems, bufs, half_shards, half_shards, after)


def _gather_forward_start(name, bufs, peers=ALL_PEERS):
    n_t = len(bufs)

    def body(*refs):
        mine = refs[:n_t]
        send_sems, recv_sems = refs[n_t], refs[n_t + 1]
        buf = refs[n_t + 2:2 * n_t + 2]
        token = refs[-1]
        x, y, c = _place()
        others = _other_chips(x, y)
        for q in range(D2D_CHUNKS):
            for t in range(n_t):
                hr = mine[t].shape[1] // 2
                cr = hr // D2D_CHUNKS
                rows = pl.ds(c * hr + q * cr, cr)
                for _, _, pk in [others[p] for p in peers]:
                    pltpu.make_async_remote_copy(
                        src_ref=mine[t].at[pk, rows], dst_ref=buf[t].at[pk, rows], send_sem=send_sems.at[t],
                        recv_sem=recv_sems.at[t], device_id=(x, y, 1 - c), device_id_type=MESH).start()
        token[...] = jnp.zeros_like(token)

    outs = pl.pallas_call(
        body, name=name,
        in_specs=[HBM] * n_t,
        out_specs=[SEM, SEM] + [HBM] * n_t + [pl.BlockSpec(memory_space=pltpu.VMEM)],
        out_shape=[pltpu.SemaphoreType.DMA((n_t,)), pltpu.SemaphoreType.DMA((n_t,))]
        + [pltpu.HBM(b.shape, b.dtype) for b in bufs] + [jax.ShapeDtypeStruct((8, 128), F32)],
        input_output_aliases={i: 2 + i for i in range(n_t)},
        compiler_params=pltpu.CompilerParams(has_side_effects=DATAFLOW),
    )(*[_in_hbm(b) for b in bufs])
    return outs[0], outs[1], list(outs[2:2 + n_t]), outs[-1]


def _core_exchange_start(name, grads):
    n_t = len(grads)
    lands = [lax.empty(g.shape, g.dtype) for g in grads]

    def body(*refs):
        src = refs[:n_t]
        send_sems, recv_sems = refs[2 * n_t], refs[2 * n_t + 1]
        land = refs[2 * n_t + 2 + n_t:2 * n_t + 2 + 2 * n_t]
        token = refs[-1]
        x, y, c = _place()
        for q in range(D2D_CHUNKS):
            for t in range(n_t):
                cr = src[t].shape[1] // D2D_CHUNKS
                rows = pl.ds(q * cr, cr)
                pltpu.make_async_remote_copy(
                    src_ref=src[t].at[:, rows], dst_ref=land[t].at[:, rows], send_sem=send_sems.at[t],
                    recv_sem=recv_sems.at[t], device_id=(x, y, 1 - c), device_id_type=MESH).start()
        token[...] = jnp.zeros_like(token)

    bufs = list(grads) + lands
    outs = pl.pallas_call(
        body, name=name,
        in_specs=[HBM] * len(bufs),
        out_specs=[SEM, SEM] + [HBM] * len(bufs) + [pl.BlockSpec(memory_space=pltpu.VMEM)],
        out_shape=[pltpu.SemaphoreType.DMA((n_t,)), pltpu.SemaphoreType.DMA((n_t,))]
        + [pltpu.HBM(b.shape, b.dtype) for b in bufs] + [jax.ShapeDtypeStruct((8, 128), F32)],
        input_output_aliases={i: 2 + i for i in range(len(bufs))},
        compiler_params=pltpu.CompilerParams(has_side_effects=DATAFLOW),
    )(*[_in_hbm(b) for b in bufs])
    return outs[0], outs[1], list(outs[2:2 + len(bufs)]), outs[-1]


def _core_exchange_wait(name, send_sems, recv_sems, bufs, after):
    n_t = len(bufs) // 2

    def body(*refs):
        land = refs[n_t:2 * n_t]
        send_ref, recv_ref = refs[2 * n_t], refs[2 * n_t + 1]
        x, y, c = _place()
        for t in range(n_t):
            whole = pltpu.make_async_remote_copy(src_ref=land[t], dst_ref=land[t], send_sem=send_ref.at[t],
                                                 recv_sem=recv_ref.at[t], device_id=(x, y, c), device_id_type=MESH)
            whole.wait_send()
            whole.wait_recv()

    outs = pl.pallas_call(
        body, name=name,
        in_specs=[HBM] * (2 * n_t) + [SEM, SEM, ANY], out_specs=[HBM] * (2 * n_t),
        out_shape=[pltpu.HBM(b.shape, b.dtype) for b in bufs],
        input_output_aliases={i: i for i in range(2 * n_t)},
        compiler_params=pltpu.CompilerParams(has_side_effects=DATAFLOW),
    )(*bufs, send_sems, recv_sems, after)
    return list(outs[:n_t]), list(outs[n_t:])


def _chip_exchange_start(name, parts):
    lands = [lax.empty(p.shape, p.dtype) for p in parts]
    return _chip_copies_start(name, parts, lands, lambda src, land, me, pk, c: (src.at[pk], land.at[me]))


def _chip_exchange_wait(name, send_sems, recv_sems, bufs, after):
    n_t = len(bufs) // 2
    return _chip_copies_wait(name, send_sems, recv_sems, bufs,
                             lambda buf: [b.at[pl.ds(0, 3)] for b in buf[:n_t]],
                             lambda buf: [b.at[pl.ds(0, 3)] for b in buf[n_t:]], after)


def _sum_chips(name, parts, landed, place, l, stacked):
    chips, rows, cols = landed.shape
    tr = min(256, rows)
    per = rows // tr

    def body(p_ref, own_ref, *refs):
        land, o_ref = refs[:chips], refs[-1]
        tot = None
        for k in range(chips):
            term = jnp.where(p_ref[0] == k, own_ref[...], land[k][...]).astype(F32)
            tot = term if tot is None else tot + term
        o_ref[...] = tot

    def from_chip(k):
        return pl.BlockSpec((None, tr, cols), lambda i, p: (jnp.where(p[0] == k, (k + 1) % chips, k), i, 0))

    in_specs = [pl.BlockSpec((None, tr, cols), lambda i, p: (p[0], i, 0))] + [from_chip(k) for k in range(chips)]
    args = [parts] + [landed] * chips
    aliases = {}
    if stacked is not None:
        in_specs.append(ANY)
        args.append(stacked)
        aliases = {len(args): 0}
    return pl.pallas_call(
        body, name=name,
        grid_spec=pltpu.PrefetchScalarGridSpec(
            num_scalar_prefetch=1, grid=(per,), in_specs=in_specs,
            out_specs=pl.BlockSpec((None, tr, cols), lambda i, p: (l, p[1] * per + i, 0))),
        out_shape=jax.ShapeDtypeStruct((DEPTH, 2 * rows, cols), F32), input_output_aliases=aliases,
        compiler_params=_params(("parallel",)),
    )(place, *args)


def _core_share_start(name, bufs, l):
    n_t = len(bufs)

    def body(*refs):
        mine = refs[:n_t]
        send_sems, recv_sems = refs[n_t], refs[n_t + 1]
        buf = refs[n_t + 2:2 * n_t + 2]
        token = refs[-1]
        x, y, c = _place()
        for q in range(D2D_CHUNKS):
            for t in range(n_t):
                hr = mine[t].shape[1] // 2
                cr = hr // D2D_CHUNKS
                rows = pl.ds(c * hr + q * cr, cr)
                pltpu.make_async_remote_copy(
                    src_ref=mine[t].at[l, rows], dst_ref=buf[t].at[l, rows], send_sem=send_sems.at[t],
                    recv_sem=recv_sems.at[t], device_id=(x, y, 1 - c), device_id_type=MESH).start()
        token[...] = jnp.zeros_like(token)

    outs = pl.pallas_call(
        body, name=name,
        in_specs=[HBM] * n_t,
        out_specs=[SEM, SEM] + [HBM] * n_t + [pl.BlockSpec(memory_space=pltpu.VMEM)],
        out_shape=[pltpu.SemaphoreType.DMA((n_t,)), pltpu.SemaphoreType.DMA((n_t,))]
        + [pltpu.HBM(b.shape, b.dtype) for b in bufs] + [jax.ShapeDtypeStruct((8, 128), F32)],
        input_output_aliases={i: 2 + i for i in range(n_t)},
        compiler_params=pltpu.CompilerParams(has_side_effects=DATAFLOW),
    )(*[_in_hbm(b) for b in bufs])
    return outs[0], outs[1], list(outs[2:2 + n_t]), outs[-1]


def _core_share_wait(name, send_sems, recv_sems, bufs, l, after):
    def half_layer(buf):
        return [b.at[l, pl.ds(0, b.shape[1] // 2)] for b in buf]

    return _chip_copies_wait(name, send_sems, recv_sems, bufs, half_layer, half_layer, after)


def _all_reduce_small(vec, after=None):
    rows, lanes = vec.shape
    hr = rows // 2

    def body(v_ref, *refs):
        o_ref, sib_ref, chips_ref, send_sems, recv_sems = refs[-5:]
        x, y, c = _place()
        me = 2 * x + y
        sibling = (x, y, 1 - c)
        mine = pl.ds(pl.multiple_of(c * hr, 8), hr)
        theirs = pl.ds(pl.multiple_of((1 - c) * hr, 8), hr)
        swap = pltpu.make_async_remote_copy(
            src_ref=v_ref.at[theirs], dst_ref=sib_ref, send_sem=send_sems.at[0], recv_sem=recv_sems.at[0],
            device_id=sibling, device_id_type=MESH)
        swap.start()
        swap.wait_recv()
        chips_ref[me] = v_ref[mine] + sib_ref[...]
        copies = []
        for j, (px, py, pk) in enumerate(_other_chips(x, y)):
            cp = pltpu.make_async_remote_copy(
                src_ref=chips_ref.at[me], dst_ref=chips_ref.at[me], send_sem=send_sems.at[1 + j],
                recv_sem=recv_sems.at[1 + j], device_id=(px, py, c), device_id_type=MESH)
            cp.start()
            copies.append(cp)
        for j, (px, py, pk) in enumerate(_other_chips(x, y)):
            pltpu.make_async_remote_copy(
                src_ref=chips_ref.at[pk], dst_ref=chips_ref.at[pk], send_sem=send_sems.at[1 + j],
                recv_sem=recv_sems.at[1 + j], device_id=(px, py, c), device_id_type=MESH).wait_recv()
        tot = chips_ref[0]
        for k in range(1, N_CHIPS):
            tot = tot + chips_ref[k]
        o_ref[mine] = tot
        share = pltpu.make_async_remote_copy(
            src_ref=o_ref.at[mine], dst_ref=o_ref.at[mine], send_sem=send_sems.at[4], recv_sem=recv_sems.at[4],
            device_id=sibling, device_id_type=MESH)
        share.start()
        pltpu.make_async_remote_copy(
            src_ref=o_ref.at[theirs], dst_ref=o_ref.at[theirs], send_sem=send_sems.at[4], recv_sem=recv_sems.at[4],
            device_id=sibling, device_id_type=MESH).wait_recv()
        swap.wait_send()
        for cp in copies:
            cp.wait_send()
        share.wait_send()

    vm = pl.BlockSpec(memory_space=pltpu.VMEM)
    return pl.pallas_call(
        body, name="small_all_reduce", in_specs=[vm] + ([] if after is None else [ANY]), out_specs=vm,
        out_shape=jax.ShapeDtypeStruct((rows, lanes), F32),
        scratch_shapes=[pltpu.VMEM((hr, lanes), F32), pltpu.VMEM((N_CHIPS, hr, lanes), F32),
                        pltpu.SemaphoreType.DMA((5,)), pltpu.SemaphoreType.DMA((5,))],
        compiler_params=pltpu.CompilerParams(has_side_effects=True, vmem_limit_bytes=48 * MIB),
    )(vec, *([] if after is None else [after]))


def _adamw(name, w, g, m, v, place, l=0, half=None, done=None, after=None):
    layers, rows, cols = w.shape
    span = rows if half is None else rows // 2
    tr = span
    for cand in (256, 128, 64, 32, 16, 8):
        if span % cand == 0:
            tr = cand
            break
    per = span // tr
    c1 = 1.0 - ADAM_B1 ** ADAM_STEP
    c2 = 1.0 - ADAM_B2 ** ADAM_STEP

    def first_block(p):
        return 0 if half is None else (p[1] if half == "own" else 1 - p[1]) * per

    def body(p_ref, w_ref, g_ref, m_ref, v_ref, *refs):
        go_ref, d_ref, nm_ref, nv_ref = refs[-4:]
        gv = g_ref[...]
        nm = ADAM_B1 * m_ref[...] + (1.0 - ADAM_B1) * gv
        nv = ADAM_B2 * v_ref[...] + (1.0 - ADAM_B2) * (gv * gv)
        go_ref[...] = gv
        nm_ref[...] = nm
        nv_ref[...] = nv
        d_ref[...] = -ADAM_LR * ((nm / c1) / (jnp.sqrt(nv / c2) + ADAM_EPS) + ADAM_WD * w_ref[...])

    blk = pl.BlockSpec((None, tr, cols), lambda i, p: (l, first_block(p) + i, 0))
    out = jax.ShapeDtypeStruct((layers, rows, cols), F32)
    extra = ([] if done is None else list(done)) + ([] if after is None else [after])
    aliases = {} if done is None else {5 + i: i for i in range(4)}
    return pl.pallas_call(
        body, name=name,
        grid_spec=pltpu.PrefetchScalarGridSpec(
            num_scalar_prefetch=1, grid=(per,), in_specs=[blk] * 4 + [ANY] * len(extra), out_specs=[blk] * 4),
        out_shape=[out] * 4, input_output_aliases=aliases,
        compiler_params=_params(("parallel",)),
    )(place, w, g, m, v, *extra)


LANES = 128
SUBLANES = 8
SMALL_SHAPES = {
    "norm_g": (DEPTH, D_MODEL), "sgu_ln_g": (DEPTH, D_A), "sgu_ln_b": (DEPTH, D_A),
    "sgu_w": (DEPTH, A_GROUPS, CHUNK, CHUNK), "sgu_b": (DEPTH, A_GROUPS, CHUNK), "mem_norm_g": (DEPTH, D_MODEL),
    "q_norm_g": (DEPTH, HEAD_DIM), "k_norm_g": (DEPTH, HEAD_DIM)}


def _small_layout():
    at, off = {}, 0
    for k in SMALL_NAMES:
        n = math.prod(SMALL_SHAPES[k]) // LANES
        at[k] = (off, n)
        off += -(-n // SUBLANES) * SUBLANES
    return at, off, -(-(off + SUBLANES) // (2 * SUBLANES)) * 2 * SUBLANES


def _pack_small(parts, loss=None):
    at, loss_row, rows = _small_layout()
    pieces = []
    for k in SMALL_NAMES:
        n = at[k][1]
        pieces.append(jnp.pad(parts[k].reshape(n, LANES), ((0, -(-n // SUBLANES) * SUBLANES - n), (0, 0))))
    tile = jnp.zeros((SUBLANES, LANES), F32) if loss is None else jnp.broadcast_to(loss.reshape(1, 1), (SUBLANES, LANES))
    pieces += [tile, jnp.zeros((rows - loss_row - SUBLANES, LANES), F32)]
    return jnp.concatenate(pieces)


def _adamw_small(w, g, m, v):
    at, _, rows = _small_layout()
    c1 = 1.0 - ADAM_B1 ** ADAM_STEP
    c2 = 1.0 - ADAM_B2 ** ADAM_STEP
    n_names = len(SMALL_NAMES)

    def body(w_ref, g_ref, m_ref, v_ref, *refs):
        outs, (d_ref, nm_ref, nv_ref) = refs[:4 * n_names], refs[4 * n_names:]
        gv = g_ref[...]
        nm = ADAM_B1 * m_ref[...] + (1.0 - ADAM_B1) * gv
        nv = ADAM_B2 * v_ref[...] + (1.0 - ADAM_B2) * (gv * gv)
        nm_ref[...] = nm
        nv_ref[...] = nv
        d_ref[...] = -ADAM_LR * ((nm / c1) / (jnp.sqrt(nv / c2) + ADAM_EPS) + ADAM_WD * w_ref[...])
        for kind, src in enumerate((g_ref, d_ref, nm_ref, nv_ref)):
            for i, k in enumerate(SMALL_NAMES):
                o_ref = outs[kind * n_names + i]
                first, n = at[k]
                shape = SMALL_SHAPES[k]
                if shape[-1] == LANES:
                    o_ref[...] = src[pl.ds(first, n), :].reshape(shape)
                else:
                    per = shape[-1] // LANES
                    for r in range(n):
                        o_ref[pl.ds(r // per, 1), pl.ds((r % per) * LANES, LANES)] = src[pl.ds(first + r, 1), :]

    out_shape = [jax.ShapeDtypeStruct(SMALL_SHAPES[k], F32) for _ in range(4) for k in SMALL_NAMES]
    outs = pl.pallas_call(
        body, name="adamw_small", out_shape=out_shape,
        scratch_shapes=[pltpu.VMEM((rows, LANES), F32)] * 3, compiler_params=_params(None),
    )(w, g, m, v)
    return [dict(zip(SMALL_NAMES, outs[kind * n_names:(kind + 1) * n_names])) for kind in range(4)]


WEIGHT_ORDER = ("norm_g", "w_in", "sgu_ln_g", "sgu_ln_b", "sgu_w", "sgu_b", "mem_norm_g", "w_mem_kv", "q_norm_g",
                "k_norm_g", "w_out")


def kernel(x, mem, norm_g, w_in, sgu_ln_g, sgu_ln_b, sgu_w, sgu_b, mem_norm_g, w_mem_kv, q_norm_g, k_norm_g, w_out, loss_target, m_norm_g, m_w_in, m_sgu_ln_g, m_sgu_ln_b, m_sgu_w, m_sgu_b, m_mem_norm_g, m_w_mem_kv, m_q_norm_g, m_k_norm_g, m_w_out, v_norm_g, v_w_in, v_sgu_ln_g, v_sgu_ln_b, v_sgu_w, v_sgu_b, v_mem_norm_g, v_w_mem_kv, v_q_norm_g, v_k_norm_g, v_w_out):
    weights = dict(norm_g=norm_g, w_in=w_in, sgu_ln_g=sgu_ln_g, sgu_ln_b=sgu_ln_b, sgu_w=sgu_w, sgu_b=sgu_b,
                   mem_norm_g=mem_norm_g, w_mem_kv=w_mem_kv, q_norm_g=q_norm_g, k_norm_g=k_norm_g, w_out=w_out)
    mom_m = dict(norm_g=m_norm_g, w_in=m_w_in, sgu_ln_g=m_sgu_ln_g, sgu_ln_b=m_sgu_ln_b, sgu_w=m_sgu_w, sgu_b=m_sgu_b,
                 mem_norm_g=m_mem_norm_g, w_mem_kv=m_w_mem_kv, q_norm_g=m_q_norm_g, k_norm_g=m_k_norm_g, w_out=m_w_out)
    mom_v = dict(norm_g=v_norm_g, w_in=v_w_in, sgu_ln_g=v_sgu_ln_g, sgu_ln_b=v_sgu_ln_b, sgu_w=v_sgu_w, sgu_b=v_sgu_b,
                 mem_norm_g=v_mem_norm_g, w_mem_kv=v_w_mem_kv, q_norm_g=v_q_norm_g, k_norm_g=v_k_norm_g, w_out=v_w_out)
    big = ("w_in", "w_mem_kv", "w_out")
    sm = {k: weights[k] for k in SMALL_NAMES}

    place = _place_index()
    xs, mems, target = x[0], mem[0], loss_target[0]

    slots = [[_cast_into_slot(f"cast_{k}_{l}", weights[k], l, place) for k in big] for l in range(DEPTH)]
    saved = [None] * DEPTH

    chips, cores = {}, {}
    me = place[0]
    arrival = jnp.stack([me, me ^ 2, me ^ 1, 3 - me]).astype(jnp.int32)
    shard_order = jnp.arange(N_CHIPS, dtype=jnp.int32)

    def start_gather(l, after=None):
        chips[l, "in"] = _gather_start(f"gather_start_{l}_in", slots[l][:1], after)
        chips[l, "rest"] = _gather_start(f"gather_start_{l}_rest", slots[l][1:], chips[l, "in"][3])
        return chips[l, "rest"][3]

    def hand_to_sibling(l, group, after):
        send_sems, recv_sems, bufs, _ = chips[l, group]
        bufs = _gather_wait(f"gather_wait_{l}_{group}", send_sems, recv_sems, bufs, after)
        cores[l, group] = _gather_forward_start(f"gather_forward_{l}_{group}", bufs)
        return cores[l, group][3]

    def whole(l, group, after):
        send_sems, recv_sems, bufs, _ = cores[l, group]
        return _gather_wait(f"gather_whole_{l}_{group}", send_sems, recv_sems, bufs, after)

    later_slots = [s for layer in slots[1:] for s in layer]

    class Gathered:
        def __init__(self, l):
            self.l = l
            self.buf = None

        def landed_from(self, tag, peers, after, behind, then=None):
            send_sems, recv_sems, _, _ = chips[0, "in_" + tag]
            buf = _gather_wait(f"gather_wait_0_in_{tag}", send_sems, recv_sems, self.buf, after, peers)
            if then is not None:
                buf, more = then(buf)
                behind = behind + more
            send_sems, recv_sems, buf, token = _gather_forward_start(f"gather_forward_0_in_{tag}", buf, peers)
            self.buf = _gather_wait(f"gather_whole_0_in_{tag}", send_sems, recv_sems, buf, [token] + behind, peers)

        def w_in(self, stage, h, proj):
            if self.l > 0:
                return (whole(self.l, "in", h)[0], shard_order, 0, N_CHIPS) if stage == 0 else None
            if stage == 0:
                self.buf = chips[0, "in_n"][2]
                return self.buf[0], arrival, 0, 1
            if stage == 1:
                def start_others(buf):
                    chips[0, "in_d"] = _gather_start("gather_start_0_in_d", buf, None, DIAGONAL)
                    chips[0, "rest"] = _gather_start("gather_start_0_rest", slots[0][1:], chips[0, "in_d"][3])
                    return chips[0, "in_d"][2], [chips[0, "rest"][3]]

                self.landed_from("n", NEIGHBOURS, proj, later_slots + [chips[0, "in_n"][3]], start_others)
                return self.buf[0], arrival, 1, 2
            if stage == 2:
                self.landed_from("d", DIAGONAL, [proj, chips[0, "rest"][3]], [])
                return self.buf[0], arrival, 3, 1
            return None

        def rest_start(self, proj):
            token = proj if self.l == 0 else hand_to_sibling(self.l, "rest", proj)
            return start_gather(self.l + 1, token) if self.l + 1 < DEPTH else token

        def rest_finish(self, o_b):
            if self.l == 0:
                o_b = hand_to_sibling(self.l, "rest", o_b)
            w_kv_all, w_out_all = whole(self.l, "rest", o_b)
            return w_kv_all, w_out_all, None

        def before_out(self, y):
            return hand_to_sibling(self.l + 1, "in", y) if self.l + 1 < DEPTH else None

    chips[0, "in_n"] = _gather_start("gather_start_0_in_n", slots[0][:1], None, NEIGHBOURS)
    cur = xs
    for l in range(DEPTH):
        cur, saved[l] = _layer_fwd(l, cur, mems, sm, Gathered(l), target if l == DEPTH - 1 else None)
    dxo, dxo_b, loss_part = cur

    small_g = [None] * DEPTH
    flight = {}

    class Exchange:
        def __init__(self):
            self.cores = {}

        def start(self, l, group, gives):
            *self.cores[l, group], token = _core_exchange_start(f"grad_core_start_{l}_{group}", gives)
            return token

        def landed(self, l, group, after):
            send_sems, recv_sems, bufs = self.cores[l, group]
            return _core_exchange_wait(f"grad_core_wait_{l}_{group}", send_sems, recv_sems, bufs, after)[1]

        def send(self, l, group, parts):
            *flight[l, group], token = _chip_exchange_start(f"grad_chip_start_{l}_{group}", parts)
            return token

    exchange = Exchange()
    for l in reversed(range(DEPTH)):
        dxo, dxo_b, small_g[l] = _layer_bwd(l, dxo, dxo_b, mems, sm, saved[l], place, exchange)
    grad_x = dxo

    groups = (("out", ("w_out",)), ("in", ("w_in", "w_mem_kv")))
    halves, stepped = dict.fromkeys(big), dict.fromkeys(big)
    small_g = {k: jnp.stack([small_g[l][k] for l in range(DEPTH)]) for k in SMALL_NAMES}
    after = grad_x
    sharing = {}

    def reduce_group(l, group, names):
        nonlocal after
        send_sems, recv_sems, bufs = flight[l, group]
        bufs = _chip_exchange_wait(f"grad_chip_wait_{l}_{group}", send_sems, recv_sems, bufs, after)
        for t, k in enumerate(names):
            halves[k] = _sum_chips(f"grad_chip_sum_{l}_{k}", bufs[t], bufs[len(names) + t], place, l, halves[k])
        *sharing[l, group], after = _core_share_start(f"grad_core_share_{l}_{group}", [halves[k] for k in names], l)

    def step(l, k, buf, half):
        nonlocal after
        tag = "" if half is None else "_" + half
        stepped[k] = _adamw(f"adamw_{k}_{l}{tag}", weights[k], buf, mom_m[k], mom_v[k], place, l, half, stepped[k],
                            after)
        after = stepped[k][1]

    def step_group(l, group, names, overlap):
        nonlocal after
        send_sems, recv_sems, bufs = sharing[l, group]
        if overlap:
            for k, buf in zip(names, bufs):
                step(l, k, buf, "own")
        bufs = _core_share_wait(f"grad_core_shared_{l}_{group}", send_sems, recv_sems, bufs, l, after)
        for k, buf in zip(names, bufs):
            halves[k] = buf
            step(l, k, buf, "other" if overlap else None)

    for l in reversed(range(DEPTH)):
        last = l == 0
        (g_out, n_out), (g_in, n_in) = groups
        reduce_group(l, g_out, n_out)
        if last:
            step_group(l, g_out, n_out, False)
            small_sum = _all_reduce_small(_pack_small(small_g, loss_part), after)
            small_step = _adamw_small(_pack_small(sm), small_sum, _pack_small({k: mom_m[k] for k in SMALL_NAMES}),
                                      _pack_small({k: mom_v[k] for k in SMALL_NAMES}))
            after = small_step[1]["sgu_w"]
        reduce_group(l, g_in, n_in)
        if not last:
            step_group(l, g_out, n_out, False)
        step_group(l, g_in, n_in, last)

    grads, delta, new_m, new_v = ({k: stepped[k][i] for k in big} for i in range(4))
    for out, small in zip((grads, delta, new_m, new_v), small_step):
        out.update(small)
    loss = small_sum[_small_layout()[1], 0]
    return (loss, grad_x[None], *[grads[k] for k in WEIGHT_ORDER], *[delta[k] for k in WEIGHT_ORDER],
            *[new_m[k] for k in WEIGHT_ORDER], *[new_v[k] for k in WEIGHT_ORDER])
```

```python
import math

import jax
import jax.numpy as jnp
from jax import lax
from jax.experimental import pallas as pl
from jax.experimental.pallas import tpu as pltpu

F32 = jnp.float32
BF16 = jnp.bfloat16
MESH = pl.DeviceIdType.MESH

D_MODEL = 2048
DEPTH = 2
CHUNK = 128
D_A = 1024
A_GROUPS = 8
D_B = 512
D_C = 512
HEADS = 4
HEAD_DIM = 128
IN_WIDTH = 6144
N_CHIPS = 4
EPS = 1e-6
ATT_SCALE = 1.0 / math.sqrt(HEAD_DIM)

OFF_U, OFF_V, OFF_ZA = 0, 1024, 2048
OFF_QB, OFF_KB, OFF_VB, OFF_ZB = 3072, 3584, 4096, 4608
OFF_QC, OFF_ZC = 5120, 5632
OFF_YB, OFF_YC = 1024, 1536

ADAM_LR = 0.001
ADAM_B1 = 0.9
ADAM_B2 = 0.999
ADAM_EPS = 1e-08
ADAM_WD = 0.01
ADAM_STEP = 10

MIB = 1024 * 1024
ANY = pl.BlockSpec(memory_space=pl.ANY)


def _params(semantics=None, vmem_mb=48):
    return pltpu.CompilerParams(dimension_semantics=semantics, vmem_limit_bytes=vmem_mb * MIB)


def _gelu(x):
    return 0.5 * x * (1.0 + lax.erf(x * (1.0 / math.sqrt(2.0))))


def _gelu_grad(x):
    cdf = 0.5 * (1.0 + lax.erf(x * (1.0 / math.sqrt(2.0))))
    pdf = jnp.exp(-0.5 * x * x) * (1.0 / math.sqrt(2.0 * math.pi))
    return cdf + x * pdf


def _sigmoid(x):
    return 1.0 / (1.0 + jnp.exp(-x))


def _silu_and_grad(z):
    s = _sigmoid(z)
    return z * s, s * (1.0 + z * (1.0 - s))


def _split_bf16(x):
    hi = x.astype(BF16)
    lo = (x - hi.astype(F32)).astype(BF16)
    return hi, lo


def _dot(a, b, dims):
    return lax.dot_general(a, b, (dims, ((), ())), preferred_element_type=F32)


NN = ((1,), (0,))
NT = ((1,), (1,))
TN = ((0,), (0,))


def _matmul(name, a, b, *, grid, a_spec, b_spec, o_spec, out_shape, dims, res=None, res_spec=None, after=None,
            place=None, into=None, vmem_mb=48):
    nk = grid[2]
    n_in = 2 + (res is not None) + (after is not None) + (into is not None)

    def body(*refs):
        if place is not None:
            refs = refs[1:]
        a_ref, b_ref = refs[0], refs[1]
        r_ref = refs[2] if res is not None else None
        o_ref = refs[n_in]
        if len(b_ref.shape) == 3 and dims == NN:
            part = _dot(a_ref[...], b_ref[...].reshape(-1, b_ref.shape[-1]), dims)
        elif len(b_ref.shape) == 3:
            width = b_ref.shape[-1]
            part = None
            for s in range(b_ref.shape[0]):
                term = _dot(a_ref[:, s * width:(s + 1) * width], b_ref[s], dims)
                part = term if part is None else part + term
        else:
            part = _dot(a_ref[...], b_ref[...], dims)
        if nk == 1:
            if r_ref is not None:
                part = part + r_ref[...]
            o_ref[...] = part.astype(o_ref.dtype)
            return
        acc_ref = refs[n_in + 1]
        k = pl.program_id(2)

        @pl.when(k == 0)
        def _():
            acc_ref[...] = part

        @pl.when(k > 0)
        def _():
            acc_ref[...] += part

        @pl.when(k == nk - 1)
        def _():
            tot = acc_ref[...]
            if r_ref is not None:
                tot = tot + r_ref[...]
            o_ref[...] = tot.astype(o_ref.dtype)

    in_specs = [a_spec, b_spec]
    args = [a, b]
    if res is not None:
        in_specs.append(res_spec)
        args.append(res)
    if after is not None:
        in_specs.append(ANY)
        args.append(after)
    aliases = {}
    if into is not None:
        in_specs.append(ANY)
        args.append(into)
        aliases = {len(args) - 1 + (place is not None): 0}
    acc_shape = tuple(d for d in o_spec.block_shape if d is not None)
    scratch = [pltpu.VMEM(acc_shape, F32)] if nk > 1 else []
    params = _params(("parallel", "parallel", "arbitrary"), vmem_mb)
    if place is not None:
        return pl.pallas_call(
            body, name=name, out_shape=out_shape, compiler_params=params, input_output_aliases=aliases,
            grid_spec=pltpu.PrefetchScalarGridSpec(num_scalar_prefetch=1, grid=grid, in_specs=in_specs,
                                                   out_specs=o_spec, scratch_shapes=scratch),
        )(place, *args)
    return pl.pallas_call(
        body, name=name, grid=grid, in_specs=in_specs, out_specs=o_spec, out_shape=out_shape,
        scratch_shapes=scratch, compiler_params=params, input_output_aliases=aliases,
    )(*args)


def _rms_fwd(name, x, g, tr, after=None, transposed=False):
    rows, d = x.shape

    def body(x_ref, g_ref, *refs):
        outs = refs[1:] if after is not None else refs
        xv = x_ref[...]
        r = lax.rsqrt(jnp.mean(xv * xv, axis=-1, keepdims=True) + EPS)
        h = xv * r * g_ref[...]
        outs[0][...] = h.astype(BF16)
        if transposed:
            outs[1][...] = h.T.astype(BF16)

    out_specs = [pl.BlockSpec((tr, d), lambda i: (i, 0))]
    out_shape = [jax.ShapeDtypeStruct((rows, d), BF16)]
    if transposed:
        out_specs.append(pl.BlockSpec((d, tr), lambda i: (0, i)))
        out_shape.append(jax.ShapeDtypeStruct((d, rows), BF16))
    outs = pl.pallas_call(
        body, name=name, grid=(rows // tr,),
        in_specs=[pl.BlockSpec((tr, d), lambda i: (i, 0)), pl.BlockSpec((1, d), lambda i: (0, 0))]
        + ([] if after is None else [ANY]),
        out_specs=out_specs, out_shape=out_shape,
        compiler_params=_params(("parallel",)),
    )(x, g, *([] if after is None else [after]))
    return outs if transposed else outs[0]


def _rms_bwd(name, x, dh, dres, g, tr, after=None):
    rows, d = x.shape

    def body(x_ref, dh_ref, dres_ref, g_ref, *refs):
        dx_ref, dxb_ref, dg_ref = refs[-3:]
        xv = x_ref[...]
        r = lax.rsqrt(jnp.mean(xv * xv, axis=-1, keepdims=True) + EPS)
        xhat = xv * r
        dhv = dh_ref[...]
        dxh = dhv * g_ref[...]
        dx = r * (dxh - xhat * jnp.mean(dxh * xhat, axis=-1, keepdims=True)) + dres_ref[...]
        dx_ref[...] = dx
        dxb_ref[...] = dx.astype(BF16)
        part = jnp.sum(dhv * xhat, axis=0, keepdims=True)

        @pl.when(pl.program_id(0) == 0)
        def _():
            dg_ref[...] = part

        @pl.when(pl.program_id(0) > 0)
        def _():
            dg_ref[...] += part

    blk = pl.BlockSpec((tr, d), lambda i: (i, 0))
    vec = pl.BlockSpec((1, d), lambda i: (0, 0))
    return pl.pallas_call(
        body, name=name, grid=(rows // tr,), in_specs=[blk, blk, blk, vec] + ([] if after is None else [ANY]),
        out_specs=[blk, blk, vec],
        out_shape=[jax.ShapeDtypeStruct((rows, d), F32), jax.ShapeDtypeStruct((rows, d), BF16),
                   jax.ShapeDtypeStruct((1, d), F32)],
        compiler_params=_params(("arbitrary",)),
    )(x, dh, dres, g, *([] if after is None else [after]))


def _rms_gain_grad(name, x, dh):
    rows, d = x.shape

    def body(x_ref, dh_ref, dg_ref):
        xv = x_ref[...]
        r = lax.rsqrt(jnp.mean(xv * xv, axis=-1, keepdims=True) + EPS)
        dg_ref[...] = jnp.sum(dh_ref[...] * xv * r, axis=0, keepdims=True)

    return pl.pallas_call(
        body, name=name, out_shape=jax.ShapeDtypeStruct((1, d), F32), compiler_params=_params(None),
    )(x, dh)


SB_T = 256
SB_HEADS = 4


LOG2E = 1.4426950408889634


def _sb_scores(q, kblk):
    z2 = _dot(q, kblk, NT) * (ATT_SCALE * LOG2E)
    e = jnp.exp2(-jnp.abs(z2))
    l1 = jnp.minimum(-z2, 0.0) - jnp.log2(1.0 + e)
    lb = l1 + z2
    return z2, e, lb, l1


def _sb_fwd(name, proj, after=None):
    s_len = proj.shape[0]
    t = SB_T
    nq = s_len // t

    def body(q_ref, k_ref, v_ref, *refs):
        o_ref = refs[-1]
        i = pl.program_id(1)
        row = lax.broadcasted_iota(jnp.int32, (t, t), 0)
        col = lax.broadcasted_iota(jnp.int32, (t, t), 1)
        causal = col < row
        after_mat = (row > col).astype(BF16)
        heads = [slice(hh * HEAD_DIM, (hh + 1) * HEAD_DIM) for hh in range(SB_HEADS)]
        q = [q_ref[:, sl].astype(BF16) for sl in heads]

        def tile(kb, state, masked):
            start = pl.multiple_of(kb * t, t)
            out = []
            for hh, sl in enumerate(heads):
                carry, acc = state[hh]
                kblk = k_ref[pl.ds(start, t), sl].astype(BF16)
                vblk = v_ref[pl.ds(start, t), sl].astype(BF16)
                _, _, lb, l1 = _sb_scores(q[hh], kblk)
                if masked:
                    l1 = jnp.where(causal, l1, 0.0)
                hi, lo = _split_bf16(l1)
                after = _dot(hi, after_mat, NN) + _dot(lo, after_mat, NN) + carry
                a = jnp.exp2(lb + after)
                if masked:
                    a = jnp.where(causal, a, 0.0)
                acc = acc + _dot(a.astype(BF16), vblk, NN)
                carry = carry + jnp.sum(l1, axis=-1, keepdims=True)
                out.append((carry, acc))
            return tuple(out)

        zero = (jnp.zeros((t, 1), F32), jnp.zeros((t, HEAD_DIM), F32))
        state = tile(i, (zero,) * SB_HEADS, True)
        state = lax.fori_loop(0, i, lambda n, st: tile(i - 1 - n, st, False), state)
        for hh, sl in enumerate(heads):
            o_ref[:, sl] = state[hh][1]

    cb = SB_HEADS * HEAD_DIM
    return pl.pallas_call(
        body, name=name, grid=(HEADS // SB_HEADS, nq),
        in_specs=[pl.BlockSpec((t, cb), lambda h, i: (i, OFF_QB // cb + h)),
                  pl.BlockSpec((s_len, cb), lambda h, i: (0, OFF_KB // cb + h)),
                  pl.BlockSpec((s_len, cb), lambda h, i: (0, OFF_VB // cb + h))] + ([] if after is None else [ANY]),
        out_specs=pl.BlockSpec((t, cb), lambda h, i: (i, h)),
        out_shape=jax.ShapeDtypeStruct((s_len, D_B), F32),
        compiler_params=_params(("parallel", "arbitrary")),
    )(proj, proj, proj, *([] if after is None else [after]))


def _sb_bwd(name, proj, dy, after=None):
    s_len = proj.shape[0]
    t = SB_T
    nq = s_len // t

    def body(q_ref, k_ref, v_ref, z_ref, dy_ref, *refs):
        dq_ref, dk_ref, dv_ref, a_ref, s_ref = refs[-5:]
        i = pl.program_id(1)

        @pl.when(i == 0)
        def _():
            dk_ref[...] = jnp.zeros_like(dk_ref)
            dv_ref[...] = jnp.zeros_like(dv_ref)

        heads = [slice(hh * HEAD_DIM, (hh + 1) * HEAD_DIM) for hh in range(SB_HEADS)]
        q = [q_ref[:, sl].astype(BF16) for sl in heads]
        silu_z, _ = _silu_and_grad(z_ref[...])
        do_all = dy_ref[...] * silu_z
        do_b = [do_all[:, sl].astype(BF16) for sl in heads]
        row = lax.broadcasted_iota(jnp.int32, (t, t), 0)
        col = lax.broadcasted_iota(jnp.int32, (t, t), 1)
        causal = col < row
        after_mat = (row > col).astype(BF16)
        before_mat = (row < col).astype(BF16)

        def weights(kb, carries, masked):
            start = pl.multiple_of(kb * t, t)
            out = []
            for hh, sl in enumerate(heads):
                kblk = k_ref[pl.ds(start, t), sl].astype(BF16)
                z, _, lb, l1 = _sb_scores(q[hh], kblk)
                if masked:
                    l1 = jnp.where(causal, l1, 0.0)
                hi, lo = _split_bf16(l1)
                after = _dot(hi, after_mat, NN) + _dot(lo, after_mat, NN) + carries[hh]
                a = jnp.exp2(lb + after)
                if masked:
                    a = jnp.where(causal, a, 0.0)
                a_ref[hh, kb] = a
                s_ref[hh, kb] = z
                out.append(carries[hh] + jnp.sum(l1, axis=-1, keepdims=True))
            return tuple(out)

        carries = weights(i, (jnp.zeros((t, 1), F32),) * SB_HEADS, True)
        lax.fori_loop(0, i, lambda n, c: weights(i - 1 - n, c, False), carries)

        def grads(kb, state, masked):
            start = pl.multiple_of(kb * t, t)
            out = []
            for hh, sl in enumerate(heads):
                carry, dq = state[hh]
                kblk = k_ref[pl.ds(start, t), sl].astype(BF16)
                vblk = v_ref[pl.ds(start, t), sl].astype(BF16)
                a = a_ref[hh, kb]
                z = s_ref[hh, kb]
                g = _dot(do_b[hh], vblk, NT) * a
                ghi, glo = _split_bf16(g)
                prefix = _dot(ghi, before_mat, NN) + _dot(glo, before_mat, NN) + carry
                e = jnp.exp2(-jnp.abs(z))
                inv = 1.0 / (1.0 + e)
                pos = z >= 0.0
                beta = jnp.where(pos, inv, e * inv)
                one_m_beta = jnp.where(pos, e * inv, inv)
                dz = (g * one_m_beta - prefix * beta) * ATT_SCALE
                if masked:
                    dz = jnp.where(causal, dz, 0.0)
                dz_b = dz.astype(BF16)
                dq = dq + _dot(dz_b, kblk, NN)
                dk_ref[pl.ds(start, t), sl] += _dot(dz_b, q[hh], TN)
                dv_ref[pl.ds(start, t), sl] += _dot(a.astype(BF16), do_b[hh], TN)
                out.append((carry + jnp.sum(g, axis=-1, keepdims=True), dq))
            return tuple(out)

        zero = (jnp.zeros((t, 1), F32), jnp.zeros((t, HEAD_DIM), F32))
        state = lax.fori_loop(0, i, lambda kb, st: grads(kb, st, False), (zero,) * SB_HEADS)
        state = grads(i, state, True)
        for hh, sl in enumerate(heads):
            dq_ref[:, sl] = state[hh][1]

    cb = SB_HEADS * HEAD_DIM
    qblk = lambda off: pl.BlockSpec((t, cb), lambda h, i: (i, off // cb + h))
    full = lambda off: pl.BlockSpec((s_len, cb), lambda h, i: (0, off // cb + h))
    out = jax.ShapeDtypeStruct((s_len, D_B), F32)
    return pl.pallas_call(
        body, name=name, grid=(HEADS // SB_HEADS, nq),
        in_specs=[qblk(OFF_QB), full(OFF_KB), full(OFF_VB), qblk(OFF_ZB), qblk(OFF_YB)]
        + ([] if after is None else [ANY]),
        out_specs=[qblk(0), full(0), full(0)],
        out_shape=[out, out, out],
        scratch_shapes=[pltpu.VMEM((SB_HEADS, nq, t, t), F32), pltpu.VMEM((SB_HEADS, nq, t, t), F32)],
        compiler_params=_params(("parallel", "arbitrary")),
    )(proj, proj, proj, proj, dy, *([] if after is None else [after]))


MEM_TQ = 512


def _qk_norm(x, g):
    r = lax.rsqrt(jnp.mean(x * x, axis=-1, keepdims=True) + EPS)
    xhat = x * r
    return xhat * g, xhat, r


def _qk_norm_bwd(dn, g, xhat, r):
    dxh = dn * g
    return r * (dxh - xhat * jnp.mean(dxh * xhat, axis=-1, keepdims=True))


def _mem_probs(q, mk, qg, kg):
    qn, qhat, rq = _qk_norm(q, qg)
    kn, khat, rk = _qk_norm(mk, kg)
    qn_b, kn_b = qn.astype(BF16), kn.astype(BF16)
    s = _dot(qn_b, kn_b, NT) * ATT_SCALE
    p = jnp.exp(s - jnp.max(s, axis=-1, keepdims=True))
    p = p / jnp.sum(p, axis=-1, keepdims=True)
    return p, qn_b, kn_b, qhat, rq, khat, rk


def _mem_fwd(name, proj, mem_kv, qg, kg):
    s_len = proj.shape[0]
    m_len = mem_kv.shape[0]
    tq = min(MEM_TQ, s_len)

    def body(q_ref, mk_ref, mv_ref, qg_ref, kg_ref, o_ref):
        p = _mem_probs(q_ref[...], mk_ref[...], qg_ref[...], kg_ref[...])[0]
        o_ref[...] = _dot(p.astype(BF16), mv_ref[...].astype(BF16), NN)

    cb = HEAD_DIM
    vec = pl.BlockSpec((1, cb), lambda h, i: (0, 0))
    return pl.pallas_call(
        body, name=name, grid=(HEADS, s_len // tq),
        in_specs=[pl.BlockSpec((tq, cb), lambda h, i: (i, OFF_QC // cb + h)),
                  pl.BlockSpec((m_len, cb), lambda h, i: (0, h)),
                  pl.BlockSpec((m_len, cb), lambda h, i: (0, HEADS + h)), vec, vec],
        out_specs=pl.BlockSpec((tq, cb), lambda h, i: (i, h)),
        out_shape=jax.ShapeDtypeStruct((s_len, D_C), F32),
        compiler_params=_params(("parallel", "parallel")),
    )(proj, mem_kv, mem_kv, qg, kg)


def _mem_bwd(name, proj, mem_kv, qg, kg, dy):
    s_len = proj.shape[0]
    m_len = mem_kv.shape[0]
    tq = min(MEM_TQ, s_len)

    def body(q_ref, mk_ref, mv_ref, qg_ref, kg_ref, z_ref, dy_ref, dq_ref, dmk_ref, dmv_ref, dqg_ref, dkg_ref):
        h, i = pl.program_id(0), pl.program_id(1)

        @pl.when(i == 0)
        def _():
            dmk_ref[...] = jnp.zeros_like(dmk_ref)
            dmv_ref[...] = jnp.zeros_like(dmv_ref)

        @pl.when((i == 0) & (h == 0))
        def _():
            dqg_ref[...] = jnp.zeros_like(dqg_ref)
            dkg_ref[...] = jnp.zeros_like(dkg_ref)

        qg, kg = qg_ref[...], kg_ref[...]
        p, qn_b, kn_b, qhat, rq, khat, rk = _mem_probs(q_ref[...], mk_ref[...], qg, kg)
        silu_z, _ = _silu_and_grad(z_ref[...])
        do_b = (dy_ref[...] * silu_z).astype(BF16)
        dmv_ref[...] += _dot(p.astype(BF16), do_b, TN)
        dp = _dot(do_b, mv_ref[...].astype(BF16), NT)
        ds = (p * (dp - jnp.sum(dp * p, axis=-1, keepdims=True)) * ATT_SCALE).astype(BF16)
        dqn = _dot(ds, kn_b, NN)
        dkn = _dot(ds, qn_b, TN)
        dq_ref[...] = _qk_norm_bwd(dqn, qg, qhat, rq)
        dmk_ref[...] += _qk_norm_bwd(dkn, kg, khat, rk)
        dqg_ref[...] += jnp.sum(dqn * qhat, axis=0, keepdims=True)
        dkg_ref[...] += jnp.sum(dkn * khat, axis=0, keepdims=True)

    cb = HEAD_DIM
    vec = pl.BlockSpec((1, cb), lambda h, i: (0, 0))
    qblk = lambda off: pl.BlockSpec((tq, cb), lambda h, i: (i, off // cb + h))
    memblk = lambda off: pl.BlockSpec((m_len, cb), lambda h, i: (0, off + h))
    return pl.pallas_call(
        body, name=name, grid=(HEADS, s_len // tq),
        in_specs=[qblk(OFF_QC), memblk(0), memblk(HEADS), vec, vec, qblk(OFF_ZC), qblk(OFF_YC)],
        out_specs=[qblk(0), memblk(0), memblk(0), vec, vec],
        out_shape=[jax.ShapeDtypeStruct((s_len, D_C), F32), jax.ShapeDtypeStruct((m_len, D_C), F32),
                   jax.ShapeDtypeStruct((m_len, D_C), F32), jax.ShapeDtypeStruct((1, cb), F32),
                   jax.ShapeDtypeStruct((1, cb), F32)],
        compiler_params=_params(("arbitrary", "arbitrary")),
    )(proj, mem_kv, mem_kv, qg, kg, proj, dy)


def _sgu_common(u_ref, v_ref, lng_ref, lnb_ref, w_ref, bias_ref):
    ug = _gelu(u_ref[...])
    vg = _gelu(v_ref[...])
    mu = jnp.mean(vg, axis=-1, keepdims=True)
    xc = vg - mu
    rstd = lax.rsqrt(jnp.mean(xc * xc, axis=-1, keepdims=True) + EPS)
    xhat = xc * rstd
    vn = xhat * lng_ref[...] + lnb_ref[...]
    vn_b = vn.astype(BF16)
    row = lax.broadcasted_iota(jnp.int32, (CHUNK, CHUNK), 0)
    col = lax.broadcasted_iota(jnp.int32, (CHUNK, CHUNK), 1)
    tril = row >= col
    mixed = []
    for g in range(A_GROUPS):
        w = jnp.where(tril, w_ref[g], 0.0).astype(BF16)
        sl = slice(g * CHUNK, (g + 1) * CHUNK)
        mixed.append(_dot(w, vn_b[:, sl], NN) + bias_ref[:, sl])
    return ug, xhat, rstd, vn_b, mixed, tril


def _gate_fwd(name, proj, o_b, o_c, lng, lnb, w_s, bias):
    s_len = proj.shape[0]

    def body(u_ref, v_ref, za_ref, zb_ref, zc_ref, ob_ref, oc_ref, lng_ref, lnb_ref, w_ref, bias_ref, y_ref, yt_ref):
        ug, _, _, _, mixed, _ = _sgu_common(u_ref, v_ref, lng_ref, lnb_ref, w_ref, bias_ref)
        sza, _ = _silu_and_grad(za_ref[...])
        gate = ug * sza

        def put(off, width, val):
            y_ref[:, off:off + width] = val.astype(BF16)
            yt_ref[off:off + width, :] = val.T.astype(BF16)

        for g in range(A_GROUPS):
            sl = slice(g * CHUNK, (g + 1) * CHUNK)
            put(g * CHUNK, CHUNK, gate[:, sl] * mixed[g])
        szb, _ = _silu_and_grad(zb_ref[...])
        put(OFF_YB, D_B, ob_ref[...] * szb)
        szc, _ = _silu_and_grad(zc_ref[...])
        put(OFF_YC, D_C, oc_ref[...] * szc)

    wide = lambda off: pl.BlockSpec((CHUNK, D_A), lambda i: (i, off // D_A))
    narrow = lambda off: pl.BlockSpec((CHUNK, D_B), lambda i: (i, off // D_B))
    vec = pl.BlockSpec((1, D_A), lambda i: (0, 0))
    return pl.pallas_call(
        body, name=name, grid=(s_len // CHUNK,),
        in_specs=[wide(OFF_U), wide(OFF_V), wide(OFF_ZA), narrow(OFF_ZB), narrow(OFF_ZC), narrow(0), narrow(0), vec, vec,
                  pl.BlockSpec((A_GROUPS, CHUNK, CHUNK), lambda i: (0, 0, 0)),
                  pl.BlockSpec((CHUNK, D_A), lambda i: (0, 0))],
        out_specs=[pl.BlockSpec((CHUNK, D_MODEL), lambda i: (i, 0)), pl.BlockSpec((D_MODEL, CHUNK), lambda i: (0, i))],
        out_shape=[jax.ShapeDtypeStruct((s_len, D_MODEL), BF16), jax.ShapeDtypeStruct((D_MODEL, s_len), BF16)],
        compiler_params=_params(("parallel",)),
    )(proj, proj, proj, proj, proj, o_b, o_c, lng, lnb, w_s, bias)


def _gate_bwd(name, proj, dy, o_b, o_c, dqkv, dq_c, lng, lnb, w_s, w_s_t, bias):
    s_len = proj.shape[0]
    n = s_len // CHUNK
    dq_b, dk_b, dv_b = dqkv

    def body(u_ref, v_ref, za_ref, zb_ref, zc_ref, dya_ref, dyb_ref, dyc_ref, ob_ref, oc_ref, dq_ref, dk_ref, dv_ref,
             dqc_ref, lng_ref, lnb_ref, w_ref, wt_ref, bias_ref, dp_ref, dw_ref, dsb_ref, dlng_ref, dlnb_ref, dbias_ref):
        i = pl.program_id(0)

        @pl.when(i == 0)
        def _():
            dw_ref[...] = jnp.zeros_like(dw_ref)
            dbias_ref[...] = jnp.zeros_like(dbias_ref)
            dlng_ref[...] = jnp.zeros_like(dlng_ref)
            dlnb_ref[...] = jnp.zeros_like(dlnb_ref)

        ug, xhat, rstd, vn_b, mixed, tril = _sgu_common(u_ref, v_ref, lng_ref, lnb_ref, w_ref, bias_ref)
        za = za_ref[...]
        sza, dsza = _silu_and_grad(za)
        dya = dya_ref[...]
        mixed_all = jnp.concatenate(mixed, axis=-1)
        d_mixed = dya * ug * sza
        dp_ref[:, OFF_U:OFF_U + D_A] = (dya * mixed_all * sza * _gelu_grad(u_ref[...])).astype(BF16)
        dp_ref[:, OFF_ZA:OFF_ZA + D_A] = (dya * ug * mixed_all * dsza).astype(BF16)
        dbias_ref[...] += d_mixed
        dm_b = d_mixed.astype(BF16)
        triu = lax.broadcasted_iota(jnp.int32, (CHUNK, CHUNK), 0) <= lax.broadcasted_iota(jnp.int32, (CHUNK, CHUNK), 1)
        d_vn = []
        for g in range(A_GROUPS):
            sl = slice(g * CHUNK, (g + 1) * CHUNK)
            wt = jnp.where(triu, wt_ref[g], 0.0).astype(BF16)
            d_vn.append(_dot(wt, dm_b[:, sl], NN))
            dw_ref[g] += jnp.where(tril, _dot(dm_b[:, sl], vn_b[:, sl], NT), 0.0)
        d_vn = jnp.concatenate(d_vn, axis=-1)
        dlng_ref[...] += jnp.sum(d_vn * xhat, axis=0, keepdims=True)
        dlnb_ref[...] += jnp.sum(d_vn, axis=0, keepdims=True)
        dxh = d_vn * lng_ref[...]
        d_vg = rstd * (dxh - jnp.mean(dxh, axis=-1, keepdims=True)
                       - xhat * jnp.mean(dxh * xhat, axis=-1, keepdims=True))
        dp_ref[:, OFF_V:OFF_V + D_A] = (d_vg * _gelu_grad(v_ref[...])).astype(BF16)
        dp_ref[:, OFF_QB:OFF_QB + D_B] = dq_ref[...].astype(BF16)
        dp_ref[:, OFF_KB:OFF_KB + D_B] = dk_ref[...].astype(BF16)
        dp_ref[:, OFF_VB:OFF_VB + D_B] = dv_ref[...].astype(BF16)
        _, dszb = _silu_and_grad(zb_ref[...])
        dp_ref[:, OFF_ZB:OFF_ZB + D_B] = (dyb_ref[...] * ob_ref[...] * dszb).astype(BF16)
        dp_ref[:, OFF_QC:OFF_QC + D_C] = dqc_ref[...].astype(BF16)
        _, dszc = _silu_and_grad(zc_ref[...])
        dp_ref[:, OFF_ZC:OFF_ZC + D_C] = (dyc_ref[...] * oc_ref[...] * dszc).astype(BF16)

        @pl.when(i == n - 1)
        def _():
            ch = lax.broadcasted_iota(jnp.int32, (D_A, CHUNK), 0)
            gcol = lax.broadcasted_iota(jnp.int32, (D_A, CHUNK), 1)
            pick = (ch // (D_A // A_GROUPS) == gcol).astype(BF16)
            rest = dbias_ref[...]
            tot = jnp.zeros((CHUNK, CHUNK), F32)
            for _ in range(3):
                term = rest.astype(BF16)
                tot = tot + _dot(term, pick, NN)
                rest = rest - term.astype(F32)
            dsb_ref[...] = tot

    wide = lambda off: pl.BlockSpec((CHUNK, D_A), lambda i: (i, off // D_A))
    narrow = lambda off: pl.BlockSpec((CHUNK, D_B), lambda i: (i, off // D_B))
    vec = pl.BlockSpec((1, D_A), lambda i: (0, 0))
    wspec = pl.BlockSpec((A_GROUPS, CHUNK, CHUNK), lambda i: (0, 0, 0))
    bspec = pl.BlockSpec((CHUNK, D_A), lambda i: (0, 0))
    return pl.pallas_call(
        body, name=name, grid=(n,),
        in_specs=[wide(OFF_U), wide(OFF_V), wide(OFF_ZA), narrow(OFF_ZB), narrow(OFF_ZC),
                  wide(0), narrow(OFF_YB), narrow(OFF_YC), narrow(0), narrow(0), narrow(0), narrow(0), narrow(0),
                  narrow(0), vec, vec, wspec, wspec, bspec],
        out_specs=[pl.BlockSpec((CHUNK, IN_WIDTH), lambda i: (i, 0)), wspec,
                   pl.BlockSpec((CHUNK, CHUNK), lambda i: (0, 0)), vec, vec],
        out_shape=[jax.ShapeDtypeStruct((s_len, IN_WIDTH), BF16), jax.ShapeDtypeStruct((A_GROUPS, CHUNK, CHUNK), F32),
                   jax.ShapeDtypeStruct((CHUNK, CHUNK), F32), jax.ShapeDtypeStruct((1, D_A), F32),
                   jax.ShapeDtypeStruct((1, D_A), F32)],
        scratch_shapes=[pltpu.VMEM((CHUNK, D_A), F32)],
        compiler_params=_params(("arbitrary",)),
    )(proj, proj, proj, proj, proj, dy, dy, dy, o_b, o_c, dq_b, dk_b, dv_b, dq_c, lng, lnb, w_s, w_s_t, bias)


IN_SHARD = IN_WIDTH // N_CHIPS
ROW_SHARD = D_MODEL // N_CHIPS


def _bias_rows(sgu_b_l):
    return jnp.repeat(sgu_b_l.T, D_A // A_GROUPS, axis=1)


def _layer_fwd(l, x, mem, sm, hooks, target=None):
    s_len = x.shape[0]
    m_len = mem.shape[0]
    tm = min(1024, s_len)
    h, h_t = _rms_fwd(f"rms_fwd_{l}", x, sm["norm_g"][l][None], min(256, s_len), transposed=True)
    proj, stage = None, 0
    while (ready := hooks.w_in(stage, h, proj)) is not None:
        w_in_all, order, first, count = ready
        proj = _matmul(
            f"in_proj_{l}_{stage}", h, w_in_all, grid=(s_len // tm, count, 1), place=order, into=proj,
            a_spec=pl.BlockSpec((tm, D_MODEL), lambda i, j, k, p: (i, 0)),
            b_spec=pl.BlockSpec((None, D_MODEL, IN_SHARD), lambda i, j, k, p: (p[first + j], 0, 0)),
            o_spec=pl.BlockSpec((tm, IN_SHARD), lambda i, j, k, p: (i, p[first + j])),
            out_shape=jax.ShapeDtypeStruct((s_len, IN_WIDTH), F32), dims=NN)
        stage += 1
    o_b = _sb_fwd(f"sb_fwd_{l}", proj, hooks.rest_start(proj))
    w_kv_all, w_out_all, after = hooks.rest_finish(o_b)
    mem_h = _rms_fwd(f"mem_rms_fwd_{l}", mem, sm["mem_norm_g"][l][None], m_len, after)
    mem_kv = _matmul(
        f"mem_kv_{l}", mem_h, w_kv_all, grid=(1, 2, N_CHIPS),
        a_spec=pl.BlockSpec((m_len, ROW_SHARD), lambda i, j, k: (0, k)),
        b_spec=pl.BlockSpec((None, ROW_SHARD, D_C), lambda i, j, k: (k, 0, j)),
        o_spec=pl.BlockSpec((m_len, D_C), lambda i, j, k: (0, j)),
        out_shape=jax.ShapeDtypeStruct((m_len, 2 * D_C), F32), dims=NN)
    qg, kg = sm["q_norm_g"][l][None], sm["k_norm_g"][l][None]
    o_c = _mem_fwd(f"mem_fwd_{l}", proj, mem_kv, qg, kg)
    bias = _bias_rows(sm["sgu_b"][l])
    y, y_t = _gate_fwd(f"gate_fwd_{l}", proj, o_b, o_c, sm["sgu_ln_g"][l][None], sm["sgu_ln_b"][l][None],
                       sm["sgu_w"][l], bias)
    saved = dict(x=x, h_t=h_t, proj=proj, mem_h=mem_h, mem_kv=mem_kv, o_b=o_b, o_c=o_c, y_t=y_t, bias=bias,
                 weights=(w_in_all, w_kv_all, w_out_all))
    if target is not None:
        return _out_proj_loss(f"out_proj_{l}", y, w_out_all, x, target, tm), saved
    tn_o = 512
    x_next = _matmul(
        f"out_proj_{l}", y, w_out_all, grid=(s_len // tm, D_MODEL // tn_o, 1),
        a_spec=pl.BlockSpec((tm, D_MODEL), lambda i, j, k: (i, 0)),
        b_spec=pl.BlockSpec((N_CHIPS, ROW_SHARD, tn_o), lambda i, j, k: (0, 0, j)),
        o_spec=pl.BlockSpec((tm, tn_o), lambda i, j, k: (i, j)),
        out_shape=jax.ShapeDtypeStruct((s_len, D_MODEL), F32), dims=NN,
        res=x, res_spec=pl.BlockSpec((tm, tn_o), lambda i, j, k: (i, j)), after=hooks.before_out(y))
    return x_next, saved


def _out_proj_loss(name, y, w_out_all, x, target, tm):
    s_len, d = x.shape
    tn = 512
    n_i = s_len // tm

    n_j = d // tn

    def body(y_ref, w_ref, x_ref, t_ref, dx_ref, dxb_ref, loss_ref, acc_ref):
        i, j = pl.program_id(0), pl.program_id(1)
        out = _dot(y_ref[...], w_ref[...].reshape(-1, tn), NN) + x_ref[...]
        e = out - t_ref[...]
        dx = e * (1.0 / d)
        dx_ref[...] = dx
        dxb_ref[...] = dx.astype(BF16)
        part = jnp.sum(e * e, axis=0, keepdims=True)

        @pl.when((i == 0) & (j == 0))
        def _():
            acc_ref[...] = part

        @pl.when((i > 0) | (j > 0))
        def _():
            acc_ref[...] += part

        @pl.when((i == n_i - 1) & (j == n_j - 1))
        def _():
            loss_ref[...] = jnp.sum(acc_ref[...], axis=-1, keepdims=True) * (0.5 / d)

    blk = pl.BlockSpec((tm, tn), lambda i, j: (i, j))
    return pl.pallas_call(
        body, name=name, grid=(n_i, n_j),
        in_specs=[pl.BlockSpec((tm, d), lambda i, j: (i, 0)), pl.BlockSpec((N_CHIPS, ROW_SHARD, tn), lambda i, j: (0, 0, j)),
                  blk, blk],
        out_specs=[blk, blk, pl.BlockSpec((1, 1), lambda i, j: (0, 0))],
        out_shape=[jax.ShapeDtypeStruct((s_len, d), F32), jax.ShapeDtypeStruct((s_len, d), BF16),
                   jax.ShapeDtypeStruct((1, 1), F32)],
        scratch_shapes=[pltpu.VMEM((1, tn), F32)],
        compiler_params=_params(("arbitrary", "arbitrary")),
    )(y, w_out_all, x, target)


def _layer_bwd(l, dxo, dxo_b, mem, sm, saved, place, exchange):
    s_len = dxo.shape[0]
    m_len = mem.shape[0]
    proj, y_t, h_t, mem_h, mem_kv = saved["proj"], saved["y_t"], saved["h_t"], saved["mem_h"], saved["mem_kv"]
    w_in_all, w_kv_all, w_out_all = saved["weights"]
    tm = min(1024, s_len)
    tn = 768
    per = IN_SHARD // tn
    half_rows = ROW_SHARD // 2

    def halves(make):
        give = lambda: make("give", lambda p: 1 - p[1], None, BF16)
        keep = lambda theirs: make("keep", lambda p: p[1], theirs, BF16)
        return give, keep

    def grad_out(tag, half, theirs, dtype):
        o_spec = pl.BlockSpec((None, half_rows, 1024), lambda i, j, k, p: (i, 0, j))
        return _matmul(
            f"d_w_out_{l}_{tag}", y_t, dxo_b, grid=(N_CHIPS, D_MODEL // 1024, 1), place=place,
            a_spec=pl.BlockSpec((half_rows, s_len), lambda i, j, k, p: (2 * i + half(p), 0)),
            b_spec=pl.BlockSpec((s_len, 1024), lambda i, j, k, p: (0, j)), o_spec=o_spec,
            out_shape=jax.ShapeDtypeStruct((N_CHIPS, half_rows, D_MODEL), dtype), dims=NN,
            res=theirs, res_spec=o_spec)

    def grad_in(tag, half, theirs, dtype):
        o_spec = pl.BlockSpec((None, D_MODEL // 2, tn), lambda i, j, k, p: (j // per, 0, j % per))
        return _matmul(
            f"d_w_in_{l}_{tag}", h_t, dproj, grid=(1, IN_WIDTH // tn, 1), place=place,
            a_spec=pl.BlockSpec((D_MODEL // 2, s_len), lambda i, j, k, p: (half(p), 0)),
            b_spec=pl.BlockSpec((s_len, tn), lambda i, j, k, p: (0, j)), o_spec=o_spec,
            out_shape=jax.ShapeDtypeStruct((N_CHIPS, D_MODEL // 2, IN_SHARD), dtype), dims=NN,
            res=theirs, res_spec=o_spec)

    def grad_kv(tag, half, theirs, dtype):
        o_spec = pl.BlockSpec((None, half_rows, 2 * D_C), lambda i, j, k, p: (i, 0, 0))
        return _matmul(
            f"d_w_kv_{l}_{tag}", mem_h, dkv_b, grid=(N_CHIPS, 1, 1), place=place,
            a_spec=pl.BlockSpec((m_len, half_rows), lambda i, j, k, p: (0, 2 * i + half(p))),
            b_spec=pl.BlockSpec((m_len, 2 * D_C), lambda i, j, k, p: (0, 0)), o_spec=o_spec,
            out_shape=jax.ShapeDtypeStruct((N_CHIPS, half_rows, 2 * D_C), dtype), dims=TN,
            res=theirs, res_spec=o_spec)

    give_out, keep_out = halves(grad_out)
    token = exchange.start(l, "out", [give_out()])
    dy = _matmul(
        f"d_y_{l}", dxo_b, w_out_all, grid=(s_len // tm, N_CHIPS, 1),
        a_spec=pl.BlockSpec((tm, D_MODEL), lambda i, j, k: (i, 0)),
        b_spec=pl.BlockSpec((None, ROW_SHARD, D_MODEL), lambda i, j, k: (j, 0, 0)),
        o_spec=pl.BlockSpec((tm, ROW_SHARD), lambda i, j, k: (i, j)),
        out_shape=jax.ShapeDtypeStruct((s_len, D_MODEL), F32), dims=NT, after=token)
    (theirs_out,) = exchange.landed(l, "out", dy)
    token = exchange.send(l, "out", [keep_out(theirs_out)])
    qg, kg = sm["q_norm_g"][l][None], sm["k_norm_g"][l][None]
    dqkv = _sb_bwd(f"sb_bwd_{l}", proj, dy, token)
    dq_c, dmk, dmv, dqg, dkg = _mem_bwd(f"mem_bwd_{l}", proj, mem_kv, qg, kg, dy)
    w_s = sm["sgu_w"][l]
    dproj, dws, dbias, dlng, dlnb = _gate_bwd(
        f"gate_bwd_{l}", proj, dy, saved["o_b"], saved["o_c"], dqkv, dq_c, sm["sgu_ln_g"][l][None],
        sm["sgu_ln_b"][l][None], w_s, jnp.swapaxes(w_s, 1, 2), saved["bias"])
    dkv_b = jnp.concatenate([dmk, dmv], axis=1).astype(BF16)
    give_in, keep_in = halves(grad_in)
    give_kv, keep_kv = halves(grad_kv)
    token = exchange.start(l, "in", [give_in(), give_kv()])
    dh = _matmul(
        f"d_h_{l}", dproj, w_in_all, grid=(s_len // tm, D_MODEL // 512, 1),
        a_spec=pl.BlockSpec((tm, IN_WIDTH), lambda i, j, k: (i, 0)),
        b_spec=pl.BlockSpec((N_CHIPS, 512, IN_SHARD), lambda i, j, k: (0, j, 0)),
        o_spec=pl.BlockSpec((tm, 512), lambda i, j, k: (i, j)),
        out_shape=jax.ShapeDtypeStruct((s_len, D_MODEL), F32), dims=NT, after=token, vmem_mb=56)
    theirs_in, theirs_kv = exchange.landed(l, "in", dh)
    token = exchange.send(l, "in", [keep_in(theirs_in), keep_kv(theirs_kv)])
    dx, dx_b, dng = _rms_bwd(f"rms_bwd_{l}", saved["x"], dh, dxo, sm["norm_g"][l][None], min(256, s_len), token)
    d_mem_h = _matmul(
        f"d_mem_h_{l}", dkv_b, w_kv_all, grid=(1, N_CHIPS, 1),
        a_spec=pl.BlockSpec((m_len, 2 * D_C), lambda i, j, k: (0, 0)),
        b_spec=pl.BlockSpec((None, ROW_SHARD, 2 * D_C), lambda i, j, k: (j, 0, 0)),
        o_spec=pl.BlockSpec((m_len, ROW_SHARD), lambda i, j, k: (0, j)),
        out_shape=jax.ShapeDtypeStruct((m_len, D_MODEL), F32), dims=NT)
    dmng = _rms_gain_grad(f"mem_rms_bwd_{l}", mem, d_mem_h)
    dsgu_b = dbias[:, :A_GROUPS].T
    small = dict(norm_g=dng[0], sgu_ln_g=dlng[0], sgu_ln_b=dlnb[0], sgu_w=dws, sgu_b=dsgu_b, mem_norm_g=dmng[0],
                 q_norm_g=dqg[0], k_norm_g=dkg[0])
    return dx, dx_b, small


SMALL_NAMES = ("norm_g", "sgu_ln_g", "sgu_ln_b", "sgu_w", "sgu_b", "mem_norm_g", "q_norm_g", "k_norm_g")


def _place():
    x, y, c = lax.axis_index("x"), lax.axis_index("y"), lax.axis_index("c")
    return x, y, c


def _other_chips(x, y):
    return [(1 - x, y, 2 * (1 - x) + y), (x, 1 - y, 2 * x + 1 - y), (1 - x, 1 - y, 2 * (1 - x) + 1 - y)]


D2D_CHUNKS = 8


def _place_index():
    return jnp.stack([2 * lax.axis_index("x") + lax.axis_index("y"), lax.axis_index("c")]).astype(jnp.int32)


def _cast_into_slot(name, w, l, place):
    _, rows, cols = w.shape
    tr = min(256, rows)

    def body(p_ref, w_ref, o_ref):
        o_ref[...] = w_ref[...].astype(BF16)

    return pl.pallas_call(
        body, name=name,
        grid_spec=pltpu.PrefetchScalarGridSpec(
            num_scalar_prefetch=1, grid=(rows // tr,),
            in_specs=[pl.BlockSpec((None, tr, cols), lambda i, p: (l, i, 0))],
            out_specs=pl.BlockSpec((None, tr, cols), lambda i, p: (p[0], i, 0))),
        out_shape=jax.ShapeDtypeStruct((N_CHIPS, rows, cols), BF16),
        compiler_params=_params(("parallel",)),
    )(place, w)


HBM = pl.BlockSpec(memory_space=pltpu.HBM)
SEM = pl.BlockSpec(memory_space=pltpu.SEMAPHORE)
DATAFLOW = pltpu.SideEffectType.DATAFLOW_SIDE_EFFECTING


def _in_hbm(a):
    return pltpu.with_memory_space_constraint(a, pltpu.HBM)


ALL_PEERS = (0, 1, 2)
NEIGHBOURS = (0, 1)
DIAGONAL = (2,)


def _chip_copies_start(name, srcs, lands, make_copy, after=None, peers=ALL_PEERS):
    n_t = len(srcs)
    in_place = lands is None
    n_after = 0 if after is None else 1

    def body(*refs):
        src = refs[:n_t]
        k = (n_t if in_place else 2 * n_t) + n_after
        send_sems, recv_sems = refs[k], refs[k + 1]
        land = refs[k + 2:k + 2 + n_t] if in_place else refs[k + 2 + n_t:k + 2 + 2 * n_t]
        token = refs[-1]
        x, y, c = _place()
        me = 2 * x + y
        others = _other_chips(x, y)
        for t in range(n_t):
            for px, py, pk in [others[p] for p in peers]:
                s, d = make_copy(src[t], land[t], me, pk, c)
                pltpu.make_async_remote_copy(
                    src_ref=s, dst_ref=d, send_sem=send_sems.at[t], recv_sem=recv_sems.at[t],
                    device_id=(px, py, c), device_id_type=MESH).start()
        token[...] = jnp.zeros_like(token)

    bufs = list(srcs) if in_place else list(srcs) + list(lands)
    outs = pl.pallas_call(
        body, name=name,
        in_specs=[HBM] * len(bufs) + [ANY] * n_after,
        out_specs=[SEM, SEM] + [HBM] * len(bufs) + [pl.BlockSpec(memory_space=pltpu.VMEM)],
        out_shape=[pltpu.SemaphoreType.DMA((n_t,)), pltpu.SemaphoreType.DMA((n_t,))]
        + [pltpu.HBM(b.shape, b.dtype) for b in bufs] + [jax.ShapeDtypeStruct((8, 128), F32)],
        input_output_aliases={i: 2 + i for i in range(len(bufs))},
        compiler_params=pltpu.CompilerParams(has_side_effects=DATAFLOW),
    )(*[_in_hbm(b) for b in bufs], *([] if after is None else [after]))
    return outs[0], outs[1], list(outs[2:2 + len(bufs)]), outs[-1]


def _chip_copies_wait(name, send_sems, recv_sems, bufs, sent, landed, after):
    n_b = len(bufs)

    def body(*refs):
        buf = refs[:n_b]
        send_ref, recv_ref = refs[n_b], refs[n_b + 1]
        x, y, c = _place()
        for t, (s, d) in enumerate(zip(sent(buf), landed(buf))):
            out = pltpu.make_async_remote_copy(src_ref=s, dst_ref=s, send_sem=send_ref.at[t], recv_sem=recv_ref.at[t],
                                               device_id=(x, y, c), device_id_type=MESH)
            out.wait_send()
            arrived = pltpu.make_async_remote_copy(src_ref=d, dst_ref=d, send_sem=send_ref.at[t],
                                                   recv_sem=recv_ref.at[t], device_id=(x, y, c), device_id_type=MESH)
            arrived.wait_recv()

    after = list(after) if isinstance(after, (list, tuple)) else [after]
    return pl.pallas_call(
        body, name=name,
        in_specs=[HBM] * n_b + [SEM, SEM] + [ANY] * len(after), out_specs=[HBM] * n_b,
        out_shape=[pltpu.HBM(b.shape, b.dtype) for b in bufs],
        input_output_aliases={i: i for i in range(n_b)},
        compiler_params=pltpu.CompilerParams(has_side_effects=DATAFLOW),
    )(*bufs, send_sems, recv_sems, *after)


def _gather_start(name, bufs, after=None, peers=ALL_PEERS):
    def make_copy(src, land, me, pk, c):
        hr = src.shape[1] // 2
        return src.at[me, pl.ds(c * hr, hr)], land.at[me, pl.ds(c * hr, hr)]

    return _chip_copies_start(name, bufs, None, make_copy, after, peers)


def _gather_wait(name, send_sems, recv_sems, bufs, after, peers=ALL_PEERS):
    def half_shards(buf):
        return [b.at[pl.ds(0, len(peers)), pl.ds(0, b.shape[1] // 2)] for b in buf]

    return _chip_copies_wait(name, send_sems, recv_sems, bufs, half_shards, half_shards, after)


def _gather_forward_start(name, bufs, peers=ALL_PEERS):
    n_t = len(bufs)

    def body(*refs):
        mine = refs[:n_t]
        send_sems, recv_sems = refs[n_t], refs[n_t + 1]
        buf = refs[n_t + 2:2 * n_t + 2]
        token = refs[-1]
        x, y, c = _place()
        others = _other_chips(x, y)
        for q in range(D2D_CHUNKS):
            for t in range(n_t):
                hr = mine[t].shape[1] // 2
                cr = hr // D2D_CHUNKS
                rows = pl.ds(c * hr + q * cr, cr)
                for _, _, pk in [others[p] for p in peers]:
                    pltpu.make_async_remote_copy(
                        src_ref=mine[t].at[pk, rows], dst_ref=buf[t].at[pk, rows], send_sem=send_sems.at[t],
                        recv_sem=recv_sems.at[t], device_id=(x, y, 1 - c), device_id_type=MESH).start()
        token[...] = jnp.zeros_like(token)

    outs = pl.pallas_call(
        body, name=name,
        in_specs=[HBM] * n_t,
        out_specs=[SEM, SEM] + [HBM] * n_t + [pl.BlockSpec(memory_space=pltpu.VMEM)],
        out_shape=[pltpu.SemaphoreType.DMA((n_t,)), pltpu.SemaphoreType.DMA((n_t,))]
        + [pltpu.HBM(b.shape, b.dtype) for b in bufs] + [jax.ShapeDtypeStruct((8, 128), F32)],
        input_output_aliases={i: 2 + i for i in range(n_t)},
        compiler_params=pltpu.CompilerParams(has_side_effects=DATAFLOW),
    )(*[_in_hbm(b) for b in bufs])
    return outs[0], outs[1], list(outs[2:2 + n_t]), outs[-1]


def _core_exchange_start(name, grads):
    n_t = len(grads)
    lands = [lax.empty(g.shape, g.dtype) for g in grads]

    def body(*refs):
        src = refs[:n_t]
        send_sems, recv_sems = refs[2 * n_t], refs[2 * n_t + 1]
        land = refs[2 * n_t + 2 + n_t:2 * n_t + 2 + 2 * n_t]
        token = refs[-1]
        x, y, c = _place()
        for q in range(D2D_CHUNKS):
            for t in range(n_t):
                cr = src[t].shape[1] // D2D_CHUNKS
                rows = pl.ds(q * cr, cr)
                pltpu.make_async_remote_copy(
                    src_ref=src[t].at[:, rows], dst_ref=land[t].at[:, rows], send_sem=send_sems.at[t],
                    recv_sem=recv_sems.at[t], device_id=(x, y, 1 - c), device_id_type=MESH).start()
        token[...] = jnp.zeros_like(token)

    bufs = list(grads) + lands
    outs = pl.pallas_call(
        body, name=name,
        in_specs=[HBM] * len(bufs),
        out_specs=[SEM, SEM] + [HBM] * len(bufs) + [pl.BlockSpec(memory_space=pltpu.VMEM)],
        out_shape=[pltpu.SemaphoreType.DMA((n_t,)), pltpu.SemaphoreType.DMA((n_t,))]
        + [pltpu.HBM(b.shape, b.dtype) for b in bufs] + [jax.ShapeDtypeStruct((8, 128), F32)],
        input_output_aliases={i: 2 + i for i in range(len(bufs))},
        compiler_params=pltpu.CompilerParams(has_side_effects=DATAFLOW),
    )(*[_in_hbm(b) for b in bufs])
    return outs[0], outs[1], list(outs[2:2 + len(bufs)]), outs[-1]


def _core_exchange_wait(name, send_sems, recv_sems, bufs, after):
    n_t = len(bufs) // 2

    def body(*refs):
        land = refs[n_t:2 * n_t]
        send_ref, recv_ref = refs[2 * n_t], refs[2 * n_t + 1]
        x, y, c = _place()
        for t in range(n_t):
            whole = pltpu.make_async_remote_copy(src_ref=land[t], dst_ref=land[t], send_sem=send_ref.at[t],
                                                 recv_sem=recv_ref.at[t], device_id=(x, y, c), device_id_type=MESH)
            whole.wait_send()
            whole.wait_recv()

    outs = pl.pallas_call(
        body, name=name,
        in_specs=[HBM] * (2 * n_t) + [SEM, SEM, ANY], out_specs=[HBM] * (2 * n_t),
        out_shape=[pltpu.HBM(b.shape, b.dtype) for b in bufs],
        input_output_aliases={i: i for i in range(2 * n_t)},
        compiler_params=pltpu.CompilerParams(has_side_effects=DATAFLOW),
    )(*bufs, send_sems, recv_sems, after)
    return list(outs[:n_t]), list(outs[n_t:])


def _chip_exchange_start(name, parts):
    lands = [lax.empty(p.shape, p.dtype) for p in parts]
    return _chip_copies_start(name, parts, lands, lambda src, land, me, pk, c: (src.at[pk], land.at[me]))


def _chip_exchange_wait(name, send_sems, recv_sems, bufs, after):
    n_t = len(bufs) // 2
    return _chip_copies_wait(name, send_sems, recv_sems, bufs,
                             lambda buf: [b.at[pl.ds(0, 3)] for b in buf[:n_t]],
                             lambda buf: [b.at[pl.ds(0, 3)] for b in buf[n_t:]], after)


def _sum_chips(name, parts, landed, place, l, stacked):
    chips, rows, cols = landed.shape
    tr = min(256, rows)
    per = rows // tr

    def body(p_ref, own_ref, *refs):
        land, o_ref = refs[:chips], refs[-1]
        tot = None
        for k in range(chips):
            term = jnp.where(p_ref[0] == k, own_ref[...], land[k][...]).astype(F32)
            tot = term if tot is None else tot + term
        o_ref[...] = tot

    def from_chip(k):
        return pl.BlockSpec((None, tr, cols), lambda i, p: (jnp.where(p[0] == k, (k + 1) % chips, k), i, 0))

    in_specs = [pl.BlockSpec((None, tr, cols), lambda i, p: (p[0], i, 0))] + [from_chip(k) for k in range(chips)]
    args = [parts] + [landed] * chips
    aliases = {}
    if stacked is not None:
        in_specs.append(ANY)
        args.append(stacked)
        aliases = {len(args): 0}
    return pl.pallas_call(
        body, name=name,
        grid_spec=pltpu.PrefetchScalarGridSpec(
            num_scalar_prefetch=1, grid=(per,), in_specs=in_specs,
            out_specs=pl.BlockSpec((None, tr, cols), lambda i, p: (l, p[1] * per + i, 0))),
        out_shape=jax.ShapeDtypeStruct((DEPTH, 2 * rows, cols), F32), input_output_aliases=aliases,
        compiler_params=_params(("parallel",)),
    )(place, *args)


def _core_share_start(name, bufs, l):
    n_t = len(bufs)

    def body(*refs):
        mine = refs[:n_t]
        send_sems, recv_sems = refs[n_t], refs[n_t + 1]
        buf = refs[n_t + 2:2 * n_t + 2]
        token = refs[-1]
        x, y, c = _place()
        for q in range(D2D_CHUNKS):
            for t in range(n_t):
                hr = mine[t].shape[1] // 2
                cr = hr // D2D_CHUNKS
                rows = pl.ds(c * hr + q * cr, cr)
                pltpu.make_async_remote_copy(
                    src_ref=mine[t].at[l, rows], dst_ref=buf[t].at[l, rows], send_sem=send_sems.at[t],
                    recv_sem=recv_sems.at[t], device_id=(x, y, 1 - c), device_id_type=MESH).start()
        token[...] = jnp.zeros_like(token)

    outs = pl.pallas_call(
        body, name=name,
        in_specs=[HBM] * n_t,
        out_specs=[SEM, SEM] + [HBM] * n_t + [pl.BlockSpec(memory_space=pltpu.VMEM)],
        out_shape=[pltpu.SemaphoreType.DMA((n_t,)), pltpu.SemaphoreType.DMA((n_t,))]
        + [pltpu.HBM(b.shape, b.dtype) for b in bufs] + [jax.ShapeDtypeStruct((8, 128), F32)],
        input_output_aliases={i: 2 + i for i in range(n_t)},
        compiler_params=pltpu.CompilerParams(has_side_effects=DATAFLOW),
    )(*[_in_hbm(b) for b in bufs])
    return outs[0], outs[1], list(outs[2:2 + n_t]), outs[-1]


def _core_share_wait(name, send_sems, recv_sems, bufs, l, after):
    def half_layer(buf):
        return [b.at[l, pl.ds(0, b.shape[1] // 2)] for b in buf]

    return _chip_copies_wait(name, send_sems, recv_sems, bufs, half_layer, half_layer, after)


def _all_reduce_small(vec, after=None):
    rows, lanes = vec.shape
    hr = rows // 2

    def body(v_ref, *refs):
        o_ref, sib_ref, chips_ref, send_sems, recv_sems = refs[-5:]
        x, y, c = _place()
        me = 2 * x + y
        sibling = (x, y, 1 - c)
        mine = pl.ds(pl.multiple_of(c * hr, 8), hr)
        theirs = pl.ds(pl.multiple_of((1 - c) * hr, 8), hr)
        swap = pltpu.make_async_remote_copy(
            src_ref=v_ref.at[theirs], dst_ref=sib_ref, send_sem=send_sems.at[0], recv_sem=recv_sems.at[0],
            device_id=sibling, device_id_type=MESH)
        swap.start()
        swap.wait_recv()
        chips_ref[me] = v_ref[mine] + sib_ref[...]
        copies = []
        for j, (px, py, pk) in enumerate(_other_chips(x, y)):
            cp = pltpu.make_async_remote_copy(
                src_ref=chips_ref.at[me], dst_ref=chips_ref.at[me], send_sem=send_sems.at[1 + j],
                recv_sem=recv_sems.at[1 + j], device_id=(px, py, c), device_id_type=MESH)
            cp.start()
            copies.append(cp)
        for j, (px, py, pk) in enumerate(_other_chips(x, y)):
            pltpu.make_async_remote_copy(
                src_ref=chips_ref.at[pk], dst_ref=chips_ref.at[pk], send_sem=send_sems.at[1 + j],
                recv_sem=recv_sems.at[1 + j], device_id=(px, py, c), device_id_type=MESH).wait_recv()
        tot = chips_ref[0]
        for k in range(1, N_CHIPS):
            tot = tot + chips_ref[k]
        o_ref[mine] = tot
        share = pltpu.make_async_remote_copy(
            src_ref=o_ref.at[mine], dst_ref=o_ref.at[mine], send_sem=send_sems.at[4], recv_sem=recv_sems.at[4],
            device_id=sibling, device_id_type=MESH)
        share.start()
        pltpu.make_async_remote_copy(
            src_ref=o_ref.at[theirs], dst_ref=o_ref.at[theirs], send_sem=send_sems.at[4], recv_sem=recv_sems.at[4],
            device_id=sibling, device_id_type=MESH).wait_recv()
        swap.wait_send()
        for cp in copies:
            cp.wait_send()
        share.wait_send()

    vm = pl.BlockSpec(memory_space=pltpu.VMEM)
    return pl.pallas_call(
        body, name="small_all_reduce", in_specs=[vm] + ([] if after is None else [ANY]), out_specs=vm,
        out_shape=jax.ShapeDtypeStruct((rows, lanes), F32),
        scratch_shapes=[pltpu.VMEM((hr, lanes), F32), pltpu.VMEM((N_CHIPS, hr, lanes), F32),
                        pltpu.SemaphoreType.DMA((5,)), pltpu.SemaphoreType.DMA((5,))],
        compiler_params=pltpu.CompilerParams(has_side_effects=True, vmem_limit_bytes=48 * MIB),
    )(vec, *([] if after is None else [after]))


def _adamw(name, w, g, m, v, place, l=0, half=None, done=None, after=None):
    layers, rows, cols = w.shape
    span = rows if half is None else rows // 2
    tr = span
    for cand in (256, 128, 64, 32, 16, 8):
        if span % cand == 0 and cand * cols * 4 <= 3 * MIB // 2:
            tr = cand
            break
    per = span // tr
    c1 = 1.0 - ADAM_B1 ** ADAM_STEP
    c2 = 1.0 - ADAM_B2 ** ADAM_STEP
    n_in, n_out = 3, 2

    def body(p_ref, w_ref, g_ref, m_ref, v_ref, *refs):
        ins = (w_ref, g_ref, m_ref, v_ref)
        outs = refs[-8:-4]
        in_buf, out_buf, in_sem, out_sem = refs[-4:]
        first = 0 if half is None else (p_ref[1] if half == "own" else 1 - p_ref[1]) * span

        def rows_of(ref, i):
            at = i * tr if half is None else pl.multiple_of(first + i * tr, tr)
            return ref.at[l, pl.ds(at, tr)]

        def reads(i):
            return [pltpu.make_async_copy(rows_of(ins[k], i), in_buf.at[k, i % n_in], in_sem.at[k, i % n_in])
                    for k in range(4)]

        def writes(i):
            return [pltpu.make_async_copy(out_buf.at[k, i % n_out], rows_of(outs[k], i), out_sem.at[k, i % n_out])
                    for k in range(4)]

        for i in range(min(n_in - 1, per)):
            for c in reads(i):
                c.start()
        for i in range(per):
            if i + n_in - 1 < per:
                for c in reads(i + n_in - 1):
                    c.start()
            for c in reads(i):
                c.wait()
            if i >= n_out:
                for c in writes(i - n_out):
                    c.wait()
            s, t = i % n_in, i % n_out
            gv = in_buf[1, s]
            nm = ADAM_B1 * in_buf[2, s] + (1.0 - ADAM_B1) * gv
            nv = ADAM_B2 * in_buf[3, s] + (1.0 - ADAM_B2) * (gv * gv)
            out_buf[0, t] = gv
            out_buf[1, t] = -ADAM_LR * ((nm / c1) / (jnp.sqrt(nv / c2) + ADAM_EPS) + ADAM_WD * in_buf[0, s])
            out_buf[2, t] = nm
            out_buf[3, t] = nv
            for c in writes(i):
                c.start()
        for i in range(max(per - n_out, 0), per):
            for c in writes(i):
                c.wait()

    out = jax.ShapeDtypeStruct((layers, rows, cols), F32)
    extra = ([] if done is None else list(done)) + ([] if after is None else [after])
    aliases = {} if done is None else {5 + i: i for i in range(4)}
    return pl.pallas_call(
        body, name=name,
        grid_spec=pltpu.PrefetchScalarGridSpec(
            num_scalar_prefetch=1, grid=(1,), in_specs=[ANY] * (4 + len(extra)), out_specs=[ANY] * 4,
            scratch_shapes=[pltpu.VMEM((4, n_in, tr, cols), F32), pltpu.VMEM((4, n_out, tr, cols), F32),
                            pltpu.SemaphoreType.DMA((4, n_in)), pltpu.SemaphoreType.DMA((4, n_out))]),
        out_shape=[out] * 4, input_output_aliases=aliases,
        compiler_params=_params(("arbitrary",)),
    )(place, w, g, m, v, *extra)


LANES = 128
SUBLANES = 8
SMALL_SHAPES = {
    "norm_g": (DEPTH, D_MODEL), "sgu_ln_g": (DEPTH, D_A), "sgu_ln_b": (DEPTH, D_A),
    "sgu_w": (DEPTH, A_GROUPS, CHUNK, CHUNK), "sgu_b": (DEPTH, A_GROUPS, CHUNK), "mem_norm_g": (DEPTH, D_MODEL),
    "q_norm_g": (DEPTH, HEAD_DIM), "k_norm_g": (DEPTH, HEAD_DIM)}


def _small_layout():
    at, off = {}, 0
    for k in SMALL_NAMES:
        n = math.prod(SMALL_SHAPES[k]) // LANES
        at[k] = (off, n)
        off += -(-n // SUBLANES) * SUBLANES
    return at, off, -(-(off + SUBLANES) // (2 * SUBLANES)) * 2 * SUBLANES


def _pack_small(parts, loss=None):
    at, loss_row, rows = _small_layout()
    pieces = []
    for k in SMALL_NAMES:
        n = at[k][1]
        pieces.append(jnp.pad(parts[k].reshape(n, LANES), ((0, -(-n // SUBLANES) * SUBLANES - n), (0, 0))))
    tile = jnp.zeros((SUBLANES, LANES), F32) if loss is None else jnp.broadcast_to(loss.reshape(1, 1), (SUBLANES, LANES))
    pieces += [tile, jnp.zeros((rows - loss_row - SUBLANES, LANES), F32)]
    return jnp.concatenate(pieces)


def _adamw_small(w, g, m, v):
    at, _, rows = _small_layout()
    c1 = 1.0 - ADAM_B1 ** ADAM_STEP
    c2 = 1.0 - ADAM_B2 ** ADAM_STEP
    n_names = len(SMALL_NAMES)

    def body(w_ref, g_ref, m_ref, v_ref, *refs):
        outs, (d_ref, nm_ref, nv_ref) = refs[:4 * n_names], refs[4 * n_names:]
        gv = g_ref[...]
        nm = ADAM_B1 * m_ref[...] + (1.0 - ADAM_B1) * gv
        nv = ADAM_B2 * v_ref[...] + (1.0 - ADAM_B2) * (gv * gv)
        nm_ref[...] = nm
        nv_ref[...] = nv
        d_ref[...] = -ADAM_LR * ((nm / c1) / (jnp.sqrt(nv / c2) + ADAM_EPS) + ADAM_WD * w_ref[...])
        for kind, src in enumerate((g_ref, d_ref, nm_ref, nv_ref)):
            for i, k in enumerate(SMALL_NAMES):
                o_ref = outs[kind * n_names + i]
                first, n = at[k]
                shape = SMALL_SHAPES[k]
                if shape[-1] == LANES:
                    o_ref[...] = src[pl.ds(first, n), :].reshape(shape)
                else:
                    per = shape[-1] // LANES
                    for r in range(n):
                        o_ref[pl.ds(r // per, 1), pl.ds((r % per) * LANES, LANES)] = src[pl.ds(first + r, 1), :]

    out_shape = [jax.ShapeDtypeStruct(SMALL_SHAPES[k], F32) for _ in range(4) for k in SMALL_NAMES]
    outs = pl.pallas_call(
        body, name="adamw_small", out_shape=out_shape,
        scratch_shapes=[pltpu.VMEM((rows, LANES), F32)] * 3, compiler_params=_params(None),
    )(w, g, m, v)
    return [dict(zip(SMALL_NAMES, outs[kind * n_names:(kind + 1) * n_names])) for kind in range(4)]


WEIGHT_ORDER = ("norm_g", "w_in", "sgu_ln_g", "sgu_ln_b", "sgu_w", "sgu_b", "mem_norm_g", "w_mem_kv", "q_norm_g",
                "k_norm_g", "w_out")


def kernel(x, mem, norm_g, w_in, sgu_ln_g, sgu_ln_b, sgu_w, sgu_b, mem_norm_g, w_mem_kv, q_norm_g, k_norm_g, w_out, loss_target, m_norm_g, m_w_in, m_sgu_ln_g, m_sgu_ln_b, m_sgu_w, m_sgu_b, m_mem_norm_g, m_w_mem_kv, m_q_norm_g, m_k_norm_g, m_w_out, v_norm_g, v_w_in, v_sgu_ln_g, v_sgu_ln_b, v_sgu_w, v_sgu_b, v_mem_norm_g, v_w_mem_kv, v_q_norm_g, v_k_norm_g, v_w_out):
    weights = dict(norm_g=norm_g, w_in=w_in, sgu_ln_g=sgu_ln_g, sgu_ln_b=sgu_ln_b, sgu_w=sgu_w, sgu_b=sgu_b,
                   mem_norm_g=mem_norm_g, w_mem_kv=w_mem_kv, q_norm_g=q_norm_g, k_norm_g=k_norm_g, w_out=w_out)
    mom_m = dict(norm_g=m_norm_g, w_in=m_w_in, sgu_ln_g=m_sgu_ln_g, sgu_ln_b=m_sgu_ln_b, sgu_w=m_sgu_w, sgu_b=m_sgu_b,
                 mem_norm_g=m_mem_norm_g, w_mem_kv=m_w_mem_kv, q_norm_g=m_q_norm_g, k_norm_g=m_k_norm_g, w_out=m_w_out)
    mom_v = dict(norm_g=v_norm_g, w_in=v_w_in, sgu_ln_g=v_sgu_ln_g, sgu_ln_b=v_sgu_ln_b, sgu_w=v_sgu_w, sgu_b=v_sgu_b,
                 mem_norm_g=v_mem_norm_g, w_mem_kv=v_w_mem_kv, q_norm_g=v_q_norm_g, k_norm_g=v_k_norm_g, w_out=v_w_out)
    big = ("w_in", "w_mem_kv", "w_out")
    sm = {k: weights[k] for k in SMALL_NAMES}

    place = _place_index()
    xs, mems, target = x[0], mem[0], loss_target[0]

    slots = [[_cast_into_slot(f"cast_{k}_{l}", weights[k], l, place) for k in big] for l in range(DEPTH)]
    saved = [None] * DEPTH

    chips, cores = {}, {}
    me = place[0]
    arrival = jnp.stack([me, me ^ 2, me ^ 1, 3 - me]).astype(jnp.int32)
    shard_order = jnp.arange(N_CHIPS, dtype=jnp.int32)

    def start_gather(l, after=None):
        chips[l, "in"] = _gather_start(f"gather_start_{l}_in", slots[l][:1], after)
        chips[l, "rest"] = _gather_start(f"gather_start_{l}_rest", slots[l][1:], chips[l, "in"][3])
        return chips[l, "rest"][3]

    def hand_to_sibling(l, group, after):
        send_sems, recv_sems, bufs, _ = chips[l, group]
        bufs = _gather_wait(f"gather_wait_{l}_{group}", send_sems, recv_sems, bufs, after)
        cores[l, group] = _gather_forward_start(f"gather_forward_{l}_{group}", bufs)
        return cores[l, group][3]

    def whole(l, group, after):
        send_sems, recv_sems, bufs, _ = cores[l, group]
        return _gather_wait(f"gather_whole_{l}_{group}", send_sems, recv_sems, bufs, after)

    later_slots = [s for layer in slots[1:] for s in layer]

    class Gathered:
        def __init__(self, l):
            self.l = l
            self.buf = None

        def landed_from(self, tag, peers, after, behind, then=None):
            send_sems, recv_sems, _, _ = chips[0, "in_" + tag]
            buf = _gather_wait(f"gather_wait_0_in_{tag}", send_sems, recv_sems, self.buf, after, peers)
            if then is not None:
                buf, more = then(buf)
                behind = behind + more
            send_sems, recv_sems, buf, token = _gather_forward_start(f"gather_forward_0_in_{tag}", buf, peers)
            self.buf = _gather_wait(f"gather_whole_0_in_{tag}", send_sems, recv_sems, buf, [token] + behind, peers)

        def w_in(self, stage, h, proj):
            if self.l > 0:
                return (whole(self.l, "in", h)[0], shard_order, 0, N_CHIPS) if stage == 0 else None
            if stage == 0:
                self.buf = chips[0, "in_n"][2]
                return self.buf[0], arrival, 0, 1
            if stage == 1:
                def start_others(buf):
                    chips[0, "in_d"] = _gather_start("gather_start_0_in_d", buf, None, DIAGONAL)
                    chips[0, "rest"] = _gather_start("gather_start_0_rest", slots[0][1:], chips[0, "in_d"][3])
                    return chips[0, "in_d"][2], [chips[0, "rest"][3]]

                self.landed_from("n", NEIGHBOURS, proj, later_slots + [chips[0, "in_n"][3]], start_others)
                return self.buf[0], arrival, 1, 2
            if stage == 2:
                self.landed_from("d", DIAGONAL, [proj, chips[0, "rest"][3]], [])
                return self.buf[0], arrival, 3, 1
            return None

        def rest_start(self, proj):
            token = proj if self.l == 0 else hand_to_sibling(self.l, "rest", proj)
            return start_gather(self.l + 1, token) if self.l + 1 < DEPTH else token

        def rest_finish(self, o_b):
            if self.l == 0:
                o_b = hand_to_sibling(self.l, "rest", o_b)
            w_kv_all, w_out_all = whole(self.l, "rest", o_b)
            return w_kv_all, w_out_all, None

        def before_out(self, y):
            return hand_to_sibling(self.l + 1, "in", y) if self.l + 1 < DEPTH else None

    chips[0, "in_n"] = _gather_start("gather_start_0_in_n", slots[0][:1], None, NEIGHBOURS)
    cur = xs
    for l in range(DEPTH):
        cur, saved[l] = _layer_fwd(l, cur, mems, sm, Gathered(l), target if l == DEPTH - 1 else None)
    dxo, dxo_b, loss_part = cur

    small_g = [None] * DEPTH
    flight = {}

    class Exchange:
        def __init__(self):
            self.cores = {}

        def start(self, l, group, gives):
            *self.cores[l, group], token = _core_exchange_start(f"grad_core_start_{l}_{group}", gives)
            return token

        def landed(self, l, group, after):
            send_sems, recv_sems, bufs = self.cores[l, group]
            return _core_exchange_wait(f"grad_core_wait_{l}_{group}", send_sems, recv_sems, bufs, after)[1]

        def send(self, l, group, parts):
            *flight[l, group], token = _chip_exchange_start(f"grad_chip_start_{l}_{group}", parts)
            return token

    exchange = Exchange()
    for l in reversed(range(DEPTH)):
        dxo, dxo_b, small_g[l] = _layer_bwd(l, dxo, dxo_b, mems, sm, saved[l], place, exchange)
    grad_x = dxo

    groups = (("out", ("w_out",)), ("in", ("w_in", "w_mem_kv")))
    halves, stepped = dict.fromkeys(big), dict.fromkeys(big)
    small_g = {k: jnp.stack([small_g[l][k] for l in range(DEPTH)]) for k in SMALL_NAMES}
    after = grad_x
    sharing = {}

    def reduce_group(l, group, names):
        nonlocal after
        send_sems, recv_sems, bufs = flight[l, group]
        bufs = _chip_exchange_wait(f"grad_chip_wait_{l}_{group}", send_sems, recv_sems, bufs, after)
        for t, k in enumerate(names):
            halves[k] = _sum_chips(f"grad_chip_sum_{l}_{k}", bufs[t], bufs[len(names) + t], place, l, halves[k])
        *sharing[l, group], after = _core_share_start(f"grad_core_share_{l}_{group}", [halves[k] for k in names], l)

    def step(l, k, buf, half):
        nonlocal after
        tag = "" if half is None else "_" + half
        stepped[k] = _adamw(f"adamw_{k}_{l}{tag}", weights[k], buf, mom_m[k], mom_v[k], place, l, half, stepped[k],
                            after)
        after = stepped[k][1]

    def step_group(l, group, names, overlap):
        nonlocal after
        send_sems, recv_sems, bufs = sharing[l, group]
        if overlap:
            for k, buf in zip(names, bufs):
                step(l, k, buf, "own")
        bufs = _core_share_wait(f"grad_core_shared_{l}_{group}", send_sems, recv_sems, bufs, l, after)
        for k, buf in zip(names, bufs):
            halves[k] = buf
            step(l, k, buf, "other" if overlap else None)

    for l in reversed(range(DEPTH)):
        last = l == 0
        (g_out, n_out), (g_in, n_in) = groups
        reduce_group(l, g_out, n_out)
        if last:
            step_group(l, g_out, n_out, False)
            small_sum = _all_reduce_small(_pack_small(small_g, loss_part), after)
            small_step = _adamw_small(_pack_small(sm), small_sum, _pack_small({k: mom_m[k] for k in SMALL_NAMES}),
                                      _pack_small({k: mom_v[k] for k in SMALL_NAMES}))
            after = small_step[1]["sgu_w"]
        reduce_group(l, g_in, n_in)
        if not last:
            step_group(l, g_out, n_out, False)
        step_group(l, g_in, n_in, last)

    grads, delta, new_m, new_v = ({k: stepped[k][i] for k in big} for i in range(4))
    for out, small in zip((grads, delta, new_m, new_v), small_step):
        out.update(small)
    loss = small_sum[_small_layout()[1], 0]
    return (loss, grad_x[None], *[grads[k] for k in WEIGHT_ORDER], *[delta[k] for k in WEIGHT_ORDER],
            *[new_m[k] for k in WEIGHT_ORDER], *[new_v[k] for k in WEIGHT_ORDER])
```

```python
import math

import jax
import jax.numpy as jnp
from jax import lax
from jax.experimental import pallas as pl
from jax.experimental.pallas import tpu as pltpu

F32 = jnp.float32
BF16 = jnp.bfloat16
MESH = pl.DeviceIdType.MESH

D_MODEL = 2048
DEPTH = 2
CHUNK = 128
D_A = 1024
A_GROUPS = 8
D_B = 512
D_C = 512
HEADS = 4
HEAD_DIM = 128
IN_WIDTH = 6144
N_CHIPS = 4
EPS = 1e-6
ATT_SCALE = 1.0 / math.sqrt(HEAD_DIM)

OFF_U, OFF_V, OFF_ZA = 0, 1024, 2048
OFF_QB, OFF_KB, OFF_VB, OFF_ZB = 3072, 3584, 4096, 4608
OFF_QC, OFF_ZC = 5120, 5632
OFF_YB, OFF_YC = 1024, 1536

ADAM_LR = 0.001
ADAM_B1 = 0.9
ADAM_B2 = 0.999
ADAM_EPS = 1e-08
ADAM_WD = 0.01
ADAM_STEP = 10

MIB = 1024 * 1024
ANY = pl.BlockSpec(memory_space=pl.ANY)


def _params(semantics=None, vmem_mb=48):
    return pltpu.CompilerParams(dimension_semantics=semantics, vmem_limit_bytes=vmem_mb * MIB)


def _gelu(x):
    return 0.5 * x * (1.0 + lax.erf(x * (1.0 / math.sqrt(2.0))))


def _gelu_grad(x):
    cdf = 0.5 * (1.0 + lax.erf(x * (1.0 / math.sqrt(2.0))))
    pdf = jnp.exp(-0.5 * x * x) * (1.0 / math.sqrt(2.0 * math.pi))
    return cdf + x * pdf


def _sigmoid(x):
    return 1.0 / (1.0 + jnp.exp(-x))


def _silu_and_grad(z):
    s = _sigmoid(z)
    return z * s, s * (1.0 + z * (1.0 - s))


def _split_bf16(x):
    hi = x.astype(BF16)
    lo = (x - hi.astype(F32)).astype(BF16)
    return hi, lo


def _dot(a, b, dims):
    return lax.dot_general(a, b, (dims, ((), ())), preferred_element_type=F32)


NN = ((1,), (0,))
NT = ((1,), (1,))
TN = ((0,), (0,))


def _matmul(name, a, b, *, grid, a_spec, b_spec, o_spec, out_shape, dims, res=None, res_spec=None, after=None,
            place=None, into=None, vmem_mb=48):
    nk = grid[2]
    n_in = 2 + (res is not None) + (after is not None) + (into is not None)

    def body(*refs):
        if place is not None:
            refs = refs[1:]
        a_ref, b_ref = refs[0], refs[1]
        r_ref = refs[2] if res is not None else None
        o_ref = refs[n_in]
        if len(b_ref.shape) == 3 and dims == NN:
            part = _dot(a_ref[...], b_ref[...].reshape(-1, b_ref.shape[-1]), dims)
        elif len(b_ref.shape) == 3:
            width = b_ref.shape[-1]
            part = None
            for s in range(b_ref.shape[0]):
                term = _dot(a_ref[:, s * width:(s + 1) * width], b_ref[s], dims)
                part = term if part is None else part + term
        else:
            part = _dot(a_ref[...], b_ref[...], dims)
        if nk == 1:
            if r_ref is not None:
                part = part + r_ref[...]
            o_ref[...] = part.astype(o_ref.dtype)
            return
        acc_ref = refs[n_in + 1]
        k = pl.program_id(2)

        @pl.when(k == 0)
        def _():
            acc_ref[...] = part

        @pl.when(k > 0)
        def _():
            acc_ref[...] += part

        @pl.when(k == nk - 1)
        def _():
            tot = acc_ref[...]
            if r_ref is not None:
                tot = tot + r_ref[...]
            o_ref[...] = tot.astype(o_ref.dtype)

    in_specs = [a_spec, b_spec]
    args = [a, b]
    if res is not None:
        in_specs.append(res_spec)
        args.append(res)
    if after is not None:
        in_specs.append(ANY)
        args.append(after)
    aliases = {}
    if into is not None:
        in_specs.append(ANY)
        args.append(into)
        aliases = {len(args) - 1 + (place is not None): 0}
    acc_shape = tuple(d for d in o_spec.block_shape if d is not None)
    scratch = [pltpu.VMEM(acc_shape, F32)] if nk > 1 else []
    params = _params(("parallel", "parallel", "arbitrary"), vmem_mb)
    if place is not None:
        return pl.pallas_call(
            body, name=name, out_shape=out_shape, compiler_params=params, input_output_aliases=aliases,
            grid_spec=pltpu.PrefetchScalarGridSpec(num_scalar_prefetch=1, grid=grid, in_specs=in_specs,
                                                   out_specs=o_spec, scratch_shapes=scratch),
        )(place, *args)
    return pl.pallas_call(
        body, name=name, grid=grid, in_specs=in_specs, out_specs=o_spec, out_shape=out_shape,
        scratch_shapes=scratch, compiler_params=params, input_output_aliases=aliases,
    )(*args)


def _rms_fwd(name, x, g, tr, after=None, transposed=False):
    rows, d = x.shape

    def body(x_ref, g_ref, *refs):
        outs = refs[1:] if after is not None else refs
        xv = x_ref[...]
        r = lax.rsqrt(jnp.mean(xv * xv, axis=-1, keepdims=True) + EPS)
        h = xv * r * g_ref[...]
        outs[0][...] = h.astype(BF16)
        if transposed:
            outs[1][...] = h.T.astype(BF16)

    out_specs = [pl.BlockSpec((tr, d), lambda i: (i, 0))]
    out_shape = [jax.ShapeDtypeStruct((rows, d), BF16)]
    if transposed:
        out_specs.append(pl.BlockSpec((d, tr), lambda i: (0, i)))
        out_shape.append(jax.ShapeDtypeStruct((d, rows), BF16))
    outs = pl.pallas_call(
        body, name=name, grid=(rows // tr,),
        in_specs=[pl.BlockSpec((tr, d), lambda i: (i, 0)), pl.BlockSpec((1, d), lambda i: (0, 0))]
        + ([] if after is None else [ANY]),
        out_specs=out_specs, out_shape=out_shape,
        compiler_params=_params(("parallel",)),
    )(x, g, *([] if after is None else [after]))
    return outs if transposed else outs[0]


def _rms_bwd(name, x, dh, dres, g, tr, after=None):
    rows, d = x.shape

    def body(x_ref, dh_ref, dres_ref, g_ref, *refs):
        dx_ref, dxb_ref, dg_ref = refs[-3:]
        xv = x_ref[...]
        r = lax.rsqrt(jnp.mean(xv * xv, axis=-1, keepdims=True) + EPS)
        xhat = xv * r
        dhv = dh_ref[...]
        dxh = dhv * g_ref[...]
        dx = r * (dxh - xhat * jnp.mean(dxh * xhat, axis=-1, keepdims=True)) + dres_ref[...]
        dx_ref[...] = dx
        dxb_ref[...] = dx.astype(BF16)
        part = jnp.sum(dhv * xhat, axis=0, keepdims=True)

        @pl.when(pl.program_id(0) == 0)
        def _():
            dg_ref[...] = part

        @pl.when(pl.program_id(0) > 0)
        def _():
            dg_ref[...] += part

    blk = pl.BlockSpec((tr, d), lambda i: (i, 0))
    vec = pl.BlockSpec((1, d), lambda i: (0, 0))
    return pl.pallas_call(
        body, name=name, grid=(rows // tr,), in_specs=[blk, blk, blk, vec] + ([] if after is None else [ANY]),
        out_specs=[blk, blk, vec],
        out_shape=[jax.ShapeDtypeStruct((rows, d), F32), jax.ShapeDtypeStruct((rows, d), BF16),
                   jax.ShapeDtypeStruct((1, d), F32)],
        compiler_params=_params(("arbitrary",)),
    )(x, dh, dres, g, *([] if after is None else [after]))


def _rms_gain_grad(name, x, dh):
    rows, d = x.shape

    def body(x_ref, dh_ref, dg_ref):
        xv = x_ref[...]
        r = lax.rsqrt(jnp.mean(xv * xv, axis=-1, keepdims=True) + EPS)
        dg_ref[...] = jnp.sum(dh_ref[...] * xv * r, axis=0, keepdims=True)

    return pl.pallas_call(
        body, name=name, out_shape=jax.ShapeDtypeStruct((1, d), F32), compiler_params=_params(None),
    )(x, dh)


SB_T = 256
SB_HEADS = 4


LOG2E = 1.4426950408889634


def _sb_scores(q, kblk):
    z2 = _dot(q, kblk, NT) * (ATT_SCALE * LOG2E)
    e = jnp.exp2(-jnp.abs(z2))
    l1 = jnp.minimum(-z2, 0.0) - jnp.log2(1.0 + e)
    lb = l1 + z2
    return z2, e, lb, l1


def _sb_fwd(name, proj, after=None):
    s_len = proj.shape[0]
    t = SB_T
    nq = s_len // t

    def body(q_ref, k_ref, v_ref, *refs):
        o_ref = refs[-1]
        i = pl.program_id(1)
        row = lax.broadcasted_iota(jnp.int32, (t, t), 0)
        col = lax.broadcasted_iota(jnp.int32, (t, t), 1)
        causal = col < row
        after_mat = (row > col).astype(BF16)
        heads = [slice(hh * HEAD_DIM, (hh + 1) * HEAD_DIM) for hh in range(SB_HEADS)]
        q = [q_ref[:, sl].astype(BF16) for sl in heads]

        def tile(kb, state, masked):
            start = pl.multiple_of(kb * t, t)
            out = []
            for hh, sl in enumerate(heads):
                carry, acc = state[hh]
                kblk = k_ref[pl.ds(start, t), sl].astype(BF16)
                vblk = v_ref[pl.ds(start, t), sl].astype(BF16)
                _, _, lb, l1 = _sb_scores(q[hh], kblk)
                if masked:
                    l1 = jnp.where(causal, l1, 0.0)
                hi, lo = _split_bf16(l1)
                after = _dot(hi, after_mat, NN) + _dot(lo, after_mat, NN) + carry
                a = jnp.exp2(lb + after)
                if masked:
                    a = jnp.where(causal, a, 0.0)
                acc = acc + _dot(a.astype(BF16), vblk, NN)
                carry = carry + jnp.sum(l1, axis=-1, keepdims=True)
                out.append((carry, acc))
            return tuple(out)

        zero = (jnp.zeros((t, 1), F32), jnp.zeros((t, HEAD_DIM), F32))
        state = tile(i, (zero,) * SB_HEADS, True)
        state = lax.fori_loop(0, i, lambda n, st: tile(i - 1 - n, st, False), state)
        for hh, sl in enumerate(heads):
            o_ref[:, sl] = state[hh][1]

    cb = SB_HEADS * HEAD_DIM
    return pl.pallas_call(
        body, name=name, grid=(HEADS // SB_HEADS, nq),
        in_specs=[pl.BlockSpec((t, cb), lambda h, i: (i, OFF_QB // cb + h)),
                  pl.BlockSpec((s_len, cb), lambda h, i: (0, OFF_KB // cb + h)),
                  pl.BlockSpec((s_len, cb), lambda h, i: (0, OFF_VB // cb + h))] + ([] if after is None else [ANY]),
        out_specs=pl.BlockSpec((t, cb), lambda h, i: (i, h)),
        out_shape=jax.ShapeDtypeStruct((s_len, D_B), F32),
        compiler_params=_params(("parallel", "arbitrary")),
    )(proj, proj, proj, *([] if after is None else [after]))


def _sb_bwd(name, proj, dy, after=None):
    s_len = proj.shape[0]
    t = SB_T
    nq = s_len // t

    def body(q_ref, k_ref, v_ref, z_ref, dy_ref, *refs):
        dq_ref, dk_ref, dv_ref, a_ref, s_ref = refs[-5:]
        i = pl.program_id(1)

        @pl.when(i == 0)
        def _():
            dk_ref[...] = jnp.zeros_like(dk_ref)
            dv_ref[...] = jnp.zeros_like(dv_ref)

        heads = [slice(hh * HEAD_DIM, (hh + 1) * HEAD_DIM) for hh in range(SB_HEADS)]
        q = [q_ref[:, sl].astype(BF16) for sl in heads]
        silu_z, _ = _silu_and_grad(z_ref[...])
        do_all = dy_ref[...] * silu_z
        do_b = [do_all[:, sl].astype(BF16) for sl in heads]
        row = lax.broadcasted_iota(jnp.int32, (t, t), 0)
        col = lax.broadcasted_iota(jnp.int32, (t, t), 1)
        causal = col < row
        after_mat = (row > col).astype(BF16)
        before_mat = (row < col).astype(BF16)

        def weights(kb, carries, masked):
            start = pl.multiple_of(kb * t, t)
            out = []
            for hh, sl in enumerate(heads):
                kblk = k_ref[pl.ds(start, t), sl].astype(BF16)
                z, _, lb, l1 = _sb_scores(q[hh], kblk)
                if masked:
                    l1 = jnp.where(causal, l1, 0.0)
                hi, lo = _split_bf16(l1)
                after = _dot(hi, after_mat, NN) + _dot(lo, after_mat, NN) + carries[hh]
                a = jnp.exp2(lb + after)
                if masked:
                    a = jnp.where(causal, a, 0.0)
                a_ref[hh, kb] = a
                s_ref[hh, kb] = z
                out.append(carries[hh] + jnp.sum(l1, axis=-1, keepdims=True))
            return tuple(out)

        carries = weights(i, (jnp.zeros((t, 1), F32),) * SB_HEADS, True)
        lax.fori_loop(0, i, lambda n, c: weights(i - 1 - n, c, False), carries)

        def grads(kb, state, masked):
            start = pl.multiple_of(kb * t, t)
            out = []
            for hh, sl in enumerate(heads):
                carry, dq = state[hh]
                kblk = k_ref[pl.ds(start, t), sl].astype(BF16)
                vblk = v_ref[pl.ds(start, t), sl].astype(BF16)
                a = a_ref[hh, kb]
                z = s_ref[hh, kb]
                g = _dot(do_b[hh], vblk, NT) * a
                ghi, glo = _split_bf16(g)
                prefix = _dot(ghi, before_mat, NN) + _dot(glo, before_mat, NN) + carry
                e = jnp.exp2(-jnp.abs(z))
                inv = 1.0 / (1.0 + e)
                pos = z >= 0.0
                beta = jnp.where(pos, inv, e * inv)
                one_m_beta = jnp.where(pos, e * inv, inv)
                dz = (g * one_m_beta - prefix * beta) * ATT_SCALE
                if masked:
                    dz = jnp.where(causal, dz, 0.0)
                dz_b = dz.astype(BF16)
                dq = dq + _dot(dz_b, kblk, NN)
                dk_ref[pl.ds(start, t), sl] += _dot(dz_b, q[hh], TN)
                dv_ref[pl.ds(start, t), sl] += _dot(a.astype(BF16), do_b[hh], TN)
                out.append((carry + jnp.sum(g, axis=-1, keepdims=True), dq))
            return tuple(out)

        zero = (jnp.zeros((t, 1), F32), jnp.zeros((t, HEAD_DIM), F32))
        state = lax.fori_loop(0, i, lambda kb, st: grads(kb, st, False), (zero,) * SB_HEADS)
        state = grads(i, state, True)
        for hh, sl in enumerate(heads):
            dq_ref[:, sl] = state[hh][1]

    cb = SB_HEADS * HEAD_DIM
    qblk = lambda off: pl.BlockSpec((t, cb), lambda h, i: (i, off // cb + h))
    full = lambda off: pl.BlockSpec((s_len, cb), lambda h, i: (0, off // cb + h))
    out = jax.ShapeDtypeStruct((s_len, D_B), F32)
    return pl.pallas_call(
        body, name=name, grid=(HEADS // SB_HEADS, nq),
        in_specs=[qblk(OFF_QB), full(OFF_KB), full(OFF_VB), qblk(OFF_ZB), qblk(OFF_YB)]
        + ([] if after is None else [ANY]),
        out_specs=[qblk(0), full(0), full(0)],
        out_shape=[out, out, out],
        scratch_shapes=[pltpu.VMEM((SB_HEADS, nq, t, t), F32), pltpu.VMEM((SB_HEADS, nq, t, t), F32)],
        compiler_params=_params(("parallel", "arbitrary")),
    )(proj, proj, proj, proj, dy, *([] if after is None else [after]))


MEM_TQ = 512


def _qk_norm(x, g):
    r = lax.rsqrt(jnp.mean(x * x, axis=-1, keepdims=True) + EPS)
    xhat = x * r
    return xhat * g, xhat, r


def _qk_norm_bwd(dn, g, xhat, r):
    dxh = dn * g
    return r * (dxh - xhat * jnp.mean(dxh * xhat, axis=-1, keepdims=True))


def _mem_probs(q, mk, qg, kg):
    qn, qhat, rq = _qk_norm(q, qg)
    kn, khat, rk = _qk_norm(mk, kg)
    qn_b, kn_b = qn.astype(BF16), kn.astype(BF16)
    s = _dot(qn_b, kn_b, NT) * ATT_SCALE
    p = jnp.exp(s - jnp.max(s, axis=-1, keepdims=True))
    p = p / jnp.sum(p, axis=-1, keepdims=True)
    return p, qn_b, kn_b, qhat, rq, khat, rk


def _mem_fwd(name, proj, mem_kv, qg, kg):
    s_len = proj.shape[0]
    m_len = mem_kv.shape[0]
    tq = min(MEM_TQ, s_len)

    def body(q_ref, mk_ref, mv_ref, qg_ref, kg_ref, o_ref):
        p = _mem_probs(q_ref[...], mk_ref[...], qg_ref[...], kg_ref[...])[0]
        o_ref[...] = _dot(p.astype(BF16), mv_ref[...].astype(BF16), NN)

    cb = HEAD_DIM
    vec = pl.BlockSpec((1, cb), lambda h, i: (0, 0))
    return pl.pallas_call(
        body, name=name, grid=(HEADS, s_len // tq),
        in_specs=[pl.BlockSpec((tq, cb), lambda h, i: (i, OFF_QC // cb + h)),
                  pl.BlockSpec((m_len, cb), lambda h, i: (0, h)),
                  pl.BlockSpec((m_len, cb), lambda h, i: (0, HEADS + h)), vec, vec],
        out_specs=pl.BlockSpec((tq, cb), lambda h, i: (i, h)),
        out_shape=jax.ShapeDtypeStruct((s_len, D_C), F32),
        compiler_params=_params(("parallel", "parallel")),
    )(proj, mem_kv, mem_kv, qg, kg)


def _mem_bwd(name, proj, mem_kv, qg, kg, dy):
    s_len = proj.shape[0]
    m_len = mem_kv.shape[0]
    tq = min(MEM_TQ, s_len)

    def body(q_ref, mk_ref, mv_ref, qg_ref, kg_ref, z_ref, dy_ref, dq_ref, dmk_ref, dmv_ref, dqg_ref, dkg_ref):
        h, i = pl.program_id(0), pl.program_id(1)

        @pl.when(i == 0)
        def _():
            dmk_ref[...] = jnp.zeros_like(dmk_ref)
            dmv_ref[...] = jnp.zeros_like(dmv_ref)

        @pl.when((i == 0) & (h == 0))
        def _():
            dqg_ref[...] = jnp.zeros_like(dqg_ref)
            dkg_ref[...] = jnp.zeros_like(dkg_ref)

        qg, kg = qg_ref[...], kg_ref[...]
        p, qn_b, kn_b, qhat, rq, khat, rk = _mem_probs(q_ref[...], mk_ref[...], qg, kg)
        silu_z, _ = _silu_and_grad(z_ref[...])
        do_b = (dy_ref[...] * silu_z).astype(BF16)
        dmv_ref[...] += _dot(p.astype(BF16), do_b, TN)
        dp = _dot(do_b, mv_ref[...].astype(BF16), NT)
        ds = (p * (dp - jnp.sum(dp * p, axis=-1, keepdims=True)) * ATT_SCALE).astype(BF16)
        dqn = _dot(ds, kn_b, NN)
        dkn = _dot(ds, qn_b, TN)
        dq_ref[...] = _qk_norm_bwd(dqn, qg, qhat, rq)
        dmk_ref[...] += _qk_norm_bwd(dkn, kg, khat, rk)
        dqg_ref[...] += jnp.sum(dqn * qhat, axis=0, keepdims=True)
        dkg_ref[...] += jnp.sum(dkn * khat, axis=0, keepdims=True)

    cb = HEAD_DIM
    vec = pl.BlockSpec((1, cb), lambda h, i: (0, 0))
    qblk = lambda off: pl.BlockSpec((tq, cb), lambda h, i: (i, off // cb + h))
    memblk = lambda off: pl.BlockSpec((m_len, cb), lambda h, i: (0, off + h))
    return pl.pallas_call(
        body, name=name, grid=(HEADS, s_len // tq),
        in_specs=[qblk(OFF_QC), memblk(0), memblk(HEADS), vec, vec, qblk(OFF_ZC), qblk(OFF_YC)],
        out_specs=[qblk(0), memblk(0), memblk(0), vec, vec],
        out_shape=[jax.ShapeDtypeStruct((s_len, D_C), F32), jax.ShapeDtypeStruct((m_len, D_C), F32),
                   jax.ShapeDtypeStruct((m_len, D_C), F32), jax.ShapeDtypeStruct((1, cb), F32),
                   jax.ShapeDtypeStruct((1, cb), F32)],
        compiler_params=_params(("arbitrary", "arbitrary")),
    )(proj, mem_kv, mem_kv, qg, kg, proj, dy)


def _sgu_common(u_ref, v_ref, lng_ref, lnb_ref, w_ref, bias_ref):
    ug = _gelu(u_ref[...])
    vg = _gelu(v_ref[...])
    mu = jnp.mean(vg, axis=-1, keepdims=True)
    xc = vg - mu
    rstd = lax.rsqrt(jnp.mean(xc * xc, axis=-1, keepdims=True) + EPS)
    xhat = xc * rstd
    vn = xhat * lng_ref[...] + lnb_ref[...]
    vn_b = vn.astype(BF16)
    row = lax.broadcasted_iota(jnp.int32, (CHUNK, CHUNK), 0)
    col = lax.broadcasted_iota(jnp.int32, (CHUNK, CHUNK), 1)
    tril = row >= col
    mixed = []
    for g in range(A_GROUPS):
        w = jnp.where(tril, w_ref[g], 0.0).astype(BF16)
        sl = slice(g * CHUNK, (g + 1) * CHUNK)
        mixed.append(_dot(w, vn_b[:, sl], NN) + bias_ref[:, sl])
    return ug, xhat, rstd, vn_b, mixed, tril


def _gate_fwd(name, proj, o_b, o_c, lng, lnb, w_s, bias):
    s_len = proj.shape[0]

    def body(u_ref, v_ref, za_ref, zb_ref, zc_ref, ob_ref, oc_ref, lng_ref, lnb_ref, w_ref, bias_ref, y_ref, yt_ref):
        ug, _, _, _, mixed, _ = _sgu_common(u_ref, v_ref, lng_ref, lnb_ref, w_ref, bias_ref)
        sza, _ = _silu_and_grad(za_ref[...])
        gate = ug * sza

        def put(off, width, val):
            y_ref[:, off:off + width] = val.astype(BF16)
            yt_ref[off:off + width, :] = val.T.astype(BF16)

        for g in range(A_GROUPS):
            sl = slice(g * CHUNK, (g + 1) * CHUNK)
            put(g * CHUNK, CHUNK, gate[:, sl] * mixed[g])
        szb, _ = _silu_and_grad(zb_ref[...])
        put(OFF_YB, D_B, ob_ref[...] * szb)
        szc, _ = _silu_and_grad(zc_ref[...])
        put(OFF_YC, D_C, oc_ref[...] * szc)

    wide = lambda off: pl.BlockSpec((CHUNK, D_A), lambda i: (i, off // D_A))
    narrow = lambda off: pl.BlockSpec((CHUNK, D_B), lambda i: (i, off // D_B))
    vec = pl.BlockSpec((1, D_A), lambda i: (0, 0))
    return pl.pallas_call(
        body, name=name, grid=(s_len // CHUNK,),
        in_specs=[wide(OFF_U), wide(OFF_V), wide(OFF_ZA), narrow(OFF_ZB), narrow(OFF_ZC), narrow(0), narrow(0), vec, vec,
                  pl.BlockSpec((A_GROUPS, CHUNK, CHUNK), lambda i: (0, 0, 0)),
                  pl.BlockSpec((CHUNK, D_A), lambda i: (0, 0))],
        out_specs=[pl.BlockSpec((CHUNK, D_MODEL), lambda i: (i, 0)), pl.BlockSpec((D_MODEL, CHUNK), lambda i: (0, i))],
        out_shape=[jax.ShapeDtypeStruct((s_len, D_MODEL), BF16), jax.ShapeDtypeStruct((D_MODEL, s_len), BF16)],
        compiler_params=_params(("parallel",)),
    )(proj, proj, proj, proj, proj, o_b, o_c, lng, lnb, w_s, bias)


def _gate_bwd(name, proj, dy, o_b, o_c, dqkv, dq_c, lng, lnb, w_s, w_s_t, bias):
    s_len = proj.shape[0]
    n = s_len // CHUNK
    dq_b, dk_b, dv_b = dqkv

    def body(u_ref, v_ref, za_ref, zb_ref, zc_ref, dya_ref, dyb_ref, dyc_ref, ob_ref, oc_ref, dq_ref, dk_ref, dv_ref,
             dqc_ref, lng_ref, lnb_ref, w_ref, wt_ref, bias_ref, dp_ref, dw_ref, dsb_ref, dlng_ref, dlnb_ref, dbias_ref):
        i = pl.program_id(0)

        @pl.when(i == 0)
        def _():
            dw_ref[...] = jnp.zeros_like(dw_ref)
            dbias_ref[...] = jnp.zeros_like(dbias_ref)
            dlng_ref[...] = jnp.zeros_like(dlng_ref)
            dlnb_ref[...] = jnp.zeros_like(dlnb_ref)

        ug, xhat, rstd, vn_b, mixed, tril = _sgu_common(u_ref, v_ref, lng_ref, lnb_ref, w_ref, bias_ref)
        za = za_ref[...]
        sza, dsza = _silu_and_grad(za)
        dya = dya_ref[...]
        mixed_all = jnp.concatenate(mixed, axis=-1)
        d_mixed = dya * ug * sza
        dp_ref[:, OFF_U:OFF_U + D_A] = (dya * mixed_all * sza * _gelu_grad(u_ref[...])).astype(BF16)
        dp_ref[:, OFF_ZA:OFF_ZA + D_A] = (dya * ug * mixed_all * dsza).astype(BF16)
        dbias_ref[...] += d_mixed
        dm_b = d_mixed.astype(BF16)
        triu = lax.broadcasted_iota(jnp.int32, (CHUNK, CHUNK), 0) <= lax.broadcasted_iota(jnp.int32, (CHUNK, CHUNK), 1)
        d_vn = []
        for g in range(A_GROUPS):
            sl = slice(g * CHUNK, (g + 1) * CHUNK)
            wt = jnp.where(triu, wt_ref[g], 0.0).astype(BF16)
            d_vn.append(_dot(wt, dm_b[:, sl], NN))
            dw_ref[g] += jnp.where(tril, _dot(dm_b[:, sl], vn_b[:, sl], NT), 0.0)
        d_vn = jnp.concatenate(d_vn, axis=-1)
        dlng_ref[...] += jnp.sum(d_vn * xhat, axis=0, keepdims=True)
        dlnb_ref[...] += jnp.sum(d_vn, axis=0, keepdims=True)
        dxh = d_vn * lng_ref[...]
        d_vg = rstd * (dxh - jnp.mean(dxh, axis=-1, keepdims=True)
                       - xhat * jnp.mean(dxh * xhat, axis=-1, keepdims=True))
        dp_ref[:, OFF_V:OFF_V + D_A] = (d_vg * _gelu_grad(v_ref[...])).astype(BF16)
        dp_ref[:, OFF_QB:OFF_QB + D_B] = dq_ref[...].astype(BF16)
        dp_ref[:, OFF_KB:OFF_KB + D_B] = dk_ref[...].astype(BF16)
        dp_ref[:, OFF_VB:OFF_VB + D_B] = dv_ref[...].astype(BF16)
        _, dszb = _silu_and_grad(zb_ref[...])
        dp_ref[:, OFF_ZB:OFF_ZB + D_B] = (dyb_ref[...] * ob_ref[...] * dszb).astype(BF16)
        dp_ref[:, OFF_QC:OFF_QC + D_C] = dqc_ref[...].astype(BF16)
        _, dszc = _silu_and_grad(zc_ref[...])
        dp_ref[:, OFF_ZC:OFF_ZC + D_C] = (dyc_ref[...] * oc_ref[...] * dszc).astype(BF16)

        @pl.when(i == n - 1)
        def _():
            ch = lax.broadcasted_iota(jnp.int32, (D_A, CHUNK), 0)
            gcol = lax.broadcasted_iota(jnp.int32, (D_A, CHUNK), 1)
            pick = (ch // (D_A // A_GROUPS) == gcol).astype(BF16)
            rest = dbias_ref[...]
            tot = jnp.zeros((CHUNK, CHUNK), F32)
            for _ in range(3):
                term = rest.astype(BF16)
                tot = tot + _dot(term, pick, NN)
                rest = rest - term.astype(F32)
            dsb_ref[...] = tot

    wide = lambda off: pl.BlockSpec((CHUNK, D_A), lambda i: (i, off // D_A))
    narrow = lambda off: pl.BlockSpec((CHUNK, D_B), lambda i: (i, off // D_B))
    vec = pl.BlockSpec((1, D_A), lambda i: (0, 0))
    wspec = pl.BlockSpec((A_GROUPS, CHUNK, CHUNK), lambda i: (0, 0, 0))
    bspec = pl.BlockSpec((CHUNK, D_A), lambda i: (0, 0))
    return pl.pallas_call(
        body, name=name, grid=(n,),
        in_specs=[wide(OFF_U), wide(OFF_V), wide(OFF_ZA), narrow(OFF_ZB), narrow(OFF_ZC),
                  wide(0), narrow(OFF_YB), narrow(OFF_YC), narrow(0), narrow(0), narrow(0), narrow(0), narrow(0),
                  narrow(0), vec, vec, wspec, wspec, bspec],
        out_specs=[pl.BlockSpec((CHUNK, IN_WIDTH), lambda i: (i, 0)), wspec,
                   pl.BlockSpec((CHUNK, CHUNK), lambda i: (0, 0)), vec, vec],
        out_shape=[jax.ShapeDtypeStruct((s_len, IN_WIDTH), BF16), jax.ShapeDtypeStruct((A_GROUPS, CHUNK, CHUNK), F32),
                   jax.ShapeDtypeStruct((CHUNK, CHUNK), F32), jax.ShapeDtypeStruct((1, D_A), F32),
                   jax.ShapeDtypeStruct((1, D_A), F32)],
        scratch_shapes=[pltpu.VMEM((CHUNK, D_A), F32)],
        compiler_params=_params(("arbitrary",)),
    )(proj, proj, proj, proj, proj, dy, dy, dy, o_b, o_c, dq_b, dk_b, dv_b, dq_c, lng, lnb, w_s, w_s_t, bias)


IN_SHARD = IN_WIDTH // N_CHIPS
ROW_SHARD = D_MODEL // N_CHIPS


def _bias_rows(sgu_b_l):
    return jnp.repeat(sgu_b_l.T, D_A // A_GROUPS, axis=1)


def _layer_fwd(l, x, mem, sm, hooks, target=None):
    s_len = x.shape[0]
    m_len = mem.shape[0]
    tm = min(1024, s_len)
    h, h_t = _rms_fwd(f"rms_fwd_{l}", x, sm["norm_g"][l][None], min(256, s_len), transposed=True)
    proj, stage = None, 0
    while (ready := hooks.w_in(stage, h, proj)) is not None:
        w_in_all, order, first, count = ready
        proj = _matmul(
            f"in_proj_{l}_{stage}", h, w_in_all, grid=(s_len // tm, count, 1), place=order, into=proj,
            a_spec=pl.BlockSpec((tm, D_MODEL), lambda i, j, k, p: (i, 0)),
            b_spec=pl.BlockSpec((None, D_MODEL, IN_SHARD), lambda i, j, k, p: (p[first + j], 0, 0)),
            o_spec=pl.BlockSpec((tm, IN_SHARD), lambda i, j, k, p: (i, p[first + j])),
            out_shape=jax.ShapeDtypeStruct((s_len, IN_WIDTH), F32), dims=NN)
        stage += 1
    o_b = _sb_fwd(f"sb_fwd_{l}", proj, hooks.rest_start(proj))
    w_kv_all, w_out_all, after = hooks.rest_finish(o_b)
    mem_h = _rms_fwd(f"mem_rms_fwd_{l}", mem, sm["mem_norm_g"][l][None], m_len, after)
    mem_kv = _matmul(
        f"mem_kv_{l}", mem_h, w_kv_all, grid=(1, 2, N_CHIPS),
        a_spec=pl.BlockSpec((m_len, ROW_SHARD), lambda i, j, k: (0, k)),
        b_spec=pl.BlockSpec((None, ROW_SHARD, D_C), lambda i, j, k: (k, 0, j)),
        o_spec=pl.BlockSpec((m_len, D_C), lambda i, j, k: (0, j)),
        out_shape=jax.ShapeDtypeStruct((m_len, 2 * D_C), F32), dims=NN)
    qg, kg = sm["q_norm_g"][l][None], sm["k_norm_g"][l][None]
    o_c = _mem_fwd(f"mem_fwd_{l}", proj, mem_kv, qg, kg)
    bias = _bias_rows(sm["sgu_b"][l])
    y, y_t = _gate_fwd(f"gate_fwd_{l}", proj, o_b, o_c, sm["sgu_ln_g"][l][None], sm["sgu_ln_b"][l][None],
                       sm["sgu_w"][l], bias)
    saved = dict(x=x, h_t=h_t, proj=proj, mem_h=mem_h, mem_kv=mem_kv, o_b=o_b, o_c=o_c, y_t=y_t, bias=bias,
                 weights=(w_in_all, w_kv_all, w_out_all))
    if target is not None:
        return _out_proj_loss(f"out_proj_{l}", y, w_out_all, x, target, tm), saved
    tn_o = 512
    x_next = _matmul(
        f"out_proj_{l}", y, w_out_all, grid=(s_len // tm, D_MODEL // tn_o, 1),
        a_spec=pl.BlockSpec((tm, D_MODEL), lambda i, j, k: (i, 0)),
        b_spec=pl.BlockSpec((N_CHIPS, ROW_SHARD, tn_o), lambda i, j, k: (0, 0, j)),
        o_spec=pl.BlockSpec((tm, tn_o), lambda i, j, k: (i, j)),
        out_shape=jax.ShapeDtypeStruct((s_len, D_MODEL), F32), dims=NN,
        res=x, res_spec=pl.BlockSpec((tm, tn_o), lambda i, j, k: (i, j)), after=hooks.before_out(y))
    return x_next, saved


def _out_proj_loss(name, y, w_out_all, x, target, tm):
    s_len, d = x.shape
    tn = 512
    n_i = s_len // tm

    n_j = d // tn

    def body(y_ref, w_ref, x_ref, t_ref, dx_ref, dxb_ref, loss_ref, acc_ref):
        i, j = pl.program_id(0), pl.program_id(1)
        out = _dot(y_ref[...], w_ref[...].reshape(-1, tn), NN) + x_ref[...]
        e = out - t_ref[...]
        dx = e * (1.0 / d)
        dx_ref[...] = dx
        dxb_ref[...] = dx.astype(BF16)
        part = jnp.sum(e * e, axis=0, keepdims=True)

        @pl.when((i == 0) & (j == 0))
        def _():
            acc_ref[...] = part

        @pl.when((i > 0) | (j > 0))
        def _():
            acc_ref[...] += part

        @pl.when((i == n_i - 1) & (j == n_j - 1))
        def _():
            loss_ref[...] = jnp.sum(acc_ref[...], axis=-1, keepdims=True) * (0.5 / d)

    blk = pl.BlockSpec((tm, tn), lambda i, j: (i, j))
    return pl.pallas_call(
        body, name=name, grid=(n_i, n_j),
        in_specs=[pl.BlockSpec((tm, d), lambda i, j: (i, 0)), pl.BlockSpec((N_CHIPS, ROW_SHARD, tn), lambda i, j: (0, 0, j)),
                  blk, blk],
        out_specs=[blk, blk, pl.BlockSpec((1, 1), lambda i, j: (0, 0))],
        out_shape=[jax.ShapeDtypeStruct((s_len, d), F32), jax.ShapeDtypeStruct((s_len, d), BF16),
                   jax.ShapeDtypeStruct((1, 1), F32)],
        scratch_shapes=[pltpu.VMEM((1, tn), F32)],
        compiler_params=_params(("arbitrary", "arbitrary")),
    )(y, w_out_all, x, target)


def _layer_bwd(l, dxo, dxo_b, mem, sm, saved, place, exchange):
    s_len = dxo.shape[0]
    m_len = mem.shape[0]
    proj, y_t, h_t, mem_h, mem_kv = saved["proj"], saved["y_t"], saved["h_t"], saved["mem_h"], saved["mem_kv"]
    w_in_all, w_kv_all, w_out_all = saved["weights"]
    tm = min(1024, s_len)
    tn = 768
    per = IN_SHARD // tn
    half_rows = ROW_SHARD // 2

    def halves(make):
        give = lambda: make("give", lambda p: 1 - p[1], None, BF16)
        keep = lambda theirs: make("keep", lambda p: p[1], theirs, BF16)
        return give, keep

    def grad_out(tag, half, theirs, dtype):
        o_spec = pl.BlockSpec((None, half_rows, 1024), lambda i, j, k, p: (i, 0, j))
        return _matmul(
            f"d_w_out_{l}_{tag}", y_t, dxo_b, grid=(N_CHIPS, D_MODEL // 1024, 1), place=place,
            a_spec=pl.BlockSpec((half_rows, s_len), lambda i, j, k, p: (2 * i + half(p), 0)),
            b_spec=pl.BlockSpec((s_len, 1024), lambda i, j, k, p: (0, j)), o_spec=o_spec,
            out_shape=jax.ShapeDtypeStruct((N_CHIPS, half_rows, D_MODEL), dtype), dims=NN,
            res=theirs, res_spec=o_spec)

    def grad_in(tag, half, theirs, dtype):
        o_spec = pl.BlockSpec((None, D_MODEL // 2, tn), lambda i, j, k, p: (j // per, 0, j % per))
        return _matmul(
            f"d_w_in_{l}_{tag}", h_t, dproj, grid=(1, IN_WIDTH // tn, 1), place=place,
            a_spec=pl.BlockSpec((D_MODEL // 2, s_len), lambda i, j, k, p: (half(p), 0)),
            b_spec=pl.BlockSpec((s_len, tn), lambda i, j, k, p: (0, j)), o_spec=o_spec,
            out_shape=jax.ShapeDtypeStruct((N_CHIPS, D_MODEL // 2, IN_SHARD), dtype), dims=NN,
            res=theirs, res_spec=o_spec)

    def grad_kv(tag, half, theirs, dtype):
        o_spec = pl.BlockSpec((None, half_rows, 2 * D_C), lambda i, j, k, p: (i, 0, 0))
        return _matmul(
            f"d_w_kv_{l}_{tag}", mem_h, dkv_b, grid=(N_CHIPS, 1, 1), place=place,
            a_spec=pl.BlockSpec((m_len, half_rows), lambda i, j, k, p: (0, 2 * i + half(p))),
            b_spec=pl.BlockSpec((m_len, 2 * D_C), lambda i, j, k, p: (0, 0)), o_spec=o_spec,
            out_shape=jax.ShapeDtypeStruct((N_CHIPS, half_rows, 2 * D_C), dtype), dims=TN,
            res=theirs, res_spec=o_spec)

    give_out, keep_out = halves(grad_out)
    token = exchange.start(l, "out", [give_out()])
    dy = _matmul(
        f"d_y_{l}", dxo_b, w_out_all, grid=(s_len // tm, N_CHIPS, 1),
        a_spec=pl.BlockSpec((tm, D_MODEL), lambda i, j, k: (i, 0)),
        b_spec=pl.BlockSpec((None, ROW_SHARD, D_MODEL), lambda i, j, k: (j, 0, 0)),
        o_spec=pl.BlockSpec((tm, ROW_SHARD), lambda i, j, k: (i, j)),
        out_shape=jax.ShapeDtypeStruct((s_len, D_MODEL), F32), dims=NT, after=token)
    (theirs_out,) = exchange.landed(l, "out", dy)
    token = exchange.send(l, "out", [keep_out(theirs_out)])
    qg, kg = sm["q_norm_g"][l][None], sm["k_norm_g"][l][None]
    dqkv = _sb_bwd(f"sb_bwd_{l}", proj, dy, token)
    dq_c, dmk, dmv, dqg, dkg = _mem_bwd(f"mem_bwd_{l}", proj, mem_kv, qg, kg, dy)
    w_s = sm["sgu_w"][l]
    dproj, dws, dbias, dlng, dlnb = _gate_bwd(
        f"gate_bwd_{l}", proj, dy, saved["o_b"], saved["o_c"], dqkv, dq_c, sm["sgu_ln_g"][l][None],
        sm["sgu_ln_b"][l][None], w_s, jnp.swapaxes(w_s, 1, 2), saved["bias"])
    dkv_b = jnp.concatenate([dmk, dmv], axis=1).astype(BF16)
    give_in, keep_in = halves(grad_in)
    give_kv, keep_kv = halves(grad_kv)
    token = exchange.start(l, "in", [give_in(), give_kv()])
    dh = _matmul(
        f"d_h_{l}", dproj, w_in_all, grid=(s_len // tm, D_MODEL // 512, 1),
        a_spec=pl.BlockSpec((tm, IN_WIDTH), lambda i, j, k: (i, 0)),
        b_spec=pl.BlockSpec((N_CHIPS, 512, IN_SHARD), lambda i, j, k: (0, j, 0)),
        o_spec=pl.BlockSpec((tm, 512), lambda i, j, k: (i, j)),
        out_shape=jax.ShapeDtypeStruct((s_len, D_MODEL), F32), dims=NT, after=token, vmem_mb=56)
    theirs_in, theirs_kv = exchange.landed(l, "in", dh)
    token = exchange.send(l, "in", [keep_in(theirs_in), keep_kv(theirs_kv)])
    dx, dx_b, dng = _rms_bwd(f"rms_bwd_{l}", saved["x"], dh, dxo, sm["norm_g"][l][None], min(256, s_len), token)
    d_mem_h = _matmul(
        f"d_mem_h_{l}", dkv_b, w_kv_all, grid=(1, N_CHIPS, 1),
        a_spec=pl.BlockSpec((m_len, 2 * D_C), lambda i, j, k: (0, 0)),
        b_spec=pl.BlockSpec((None, ROW_SHARD, 2 * D_C), lambda i, j, k: (j, 0, 0)),
        o_spec=pl.BlockSpec((m_len, ROW_SHARD), lambda i, j, k: (0, j)),
        out_shape=jax.ShapeDtypeStruct((m_len, D_MODEL), F32), dims=NT)
    dmng = _rms_gain_grad(f"mem_rms_bwd_{l}", mem, d_mem_h)
    dsgu_b = dbias[:, :A_GROUPS].T
    small = dict(norm_g=dng[0], sgu_ln_g=dlng[0], sgu_ln_b=dlnb[0], sgu_w=dws, sgu_b=dsgu_b, mem_norm_g=dmng[0],
                 q_norm_g=dqg[0], k_norm_g=dkg[0])
    return dx, dx_b, small


SMALL_NAMES = ("norm_g", "sgu_ln_g", "sgu_ln_b", "sgu_w", "sgu_b", "mem_norm_g", "q_norm_g", "k_norm_g")


def _place():
    x, y, c = lax.axis_index("x"), lax.axis_index("y"), lax.axis_index("c")
    return x, y, c


def _other_chips(x, y):
    return [(1 - x, y, 2 * (1 - x) + y), (x, 1 - y, 2 * x + 1 - y), (1 - x, 1 - y, 2 * (1 - x) + 1 - y)]


D2D_CHUNKS = 8


def _place_index():
    return jnp.stack([2 * lax.axis_index("x") + lax.axis_index("y"), lax.axis_index("c")]).astype(jnp.int32)


def _cast_into_slot(name, w, l, place):
    _, rows, cols = w.shape
    tr = min(256, rows)

    def body(p_ref, w_ref, o_ref):
        o_ref[...] = w_ref[...].astype(BF16)

    return pl.pallas_call(
        body, name=name,
        grid_spec=pltpu.PrefetchScalarGridSpec(
            num_scalar_prefetch=1, grid=(rows // tr,),
            in_specs=[pl.BlockSpec((None, tr, cols), lambda i, p: (l, i, 0))],
            out_specs=pl.BlockSpec((None, tr, cols), lambda i, p: (p[0], i, 0))),
        out_shape=jax.ShapeDtypeStruct((N_CHIPS, rows, cols), BF16),
        compiler_params=_params(("parallel",)),
    )(place, w)


HBM = pl.BlockSpec(memory_space=pltpu.HBM)
SEM = pl.BlockSpec(memory_space=pltpu.SEMAPHORE)
DATAFLOW = pltpu.SideEffectType.DATAFLOW_SIDE_EFFECTING


def _in_hbm(a):
    return pltpu.with_memory_space_constraint(a, pltpu.HBM)


ALL_PEERS = (0, 1, 2)
NEIGHBOURS = (0, 1)
DIAGONAL = (2,)


def _chip_copies_start(name, srcs, lands, make_copy, after=None, peers=ALL_PEERS):
    n_t = len(srcs)
    in_place = lands is None
    n_after = 0 if after is None else 1

    def body(*refs):
        src = refs[:n_t]
        k = (n_t if in_place else 2 * n_t) + n_after
        send_sems, recv_sems = refs[k], refs[k + 1]
        land = refs[k + 2:k + 2 + n_t] if in_place else refs[k + 2 + n_t:k + 2 + 2 * n_t]
        token = refs[-1]
        x, y, c = _place()
        me = 2 * x + y
        others = _other_chips(x, y)
        for t in range(n_t):
            for px, py, pk in [others[p] for p in peers]:
                s, d = make_copy(src[t], land[t], me, pk, c)
                pltpu.make_async_remote_copy(
                    src_ref=s, dst_ref=d, send_sem=send_sems.at[t], recv_sem=recv_sems.at[t],
                    device_id=(px, py, c), device_id_type=MESH).start()
        token[...] = jnp.zeros_like(token)

    bufs = list(srcs) if in_place else list(srcs) + list(lands)
    outs = pl.pallas_call(
        body, name=name,
        in_specs=[HBM] * len(bufs) + [ANY] * n_after,
        out_specs=[SEM, SEM] + [HBM] * len(bufs) + [pl.BlockSpec(memory_space=pltpu.VMEM)],
        out_shape=[pltpu.SemaphoreType.DMA((n_t,)), pltpu.SemaphoreType.DMA((n_t,))]
        + [pltpu.HBM(b.shape, b.dtype) for b in bufs] + [jax.ShapeDtypeStruct((8, 128), F32)],
        input_output_aliases={i: 2 + i for i in range(len(bufs))},
        compiler_params=pltpu.CompilerParams(has_side_effects=DATAFLOW),
    )(*[_in_hbm(b) for b in bufs], *([] if after is None else [after]))
    return outs[0], outs[1], list(outs[2:2 + len(bufs)]), outs[-1]


def _chip_copies_wait(name, send_sems, recv_sems, bufs, sent, landed, after):
    n_b = len(bufs)

    def body(*refs):
        buf = refs[:n_b]
        send_ref, recv_ref = refs[n_b], refs[n_b + 1]
        x, y, c = _place()
        for t, (s, d) in enumerate(zip(sent(buf), landed(buf))):
            out = pltpu.make_async_remote_copy(src_ref=s, dst_ref=s, send_sem=send_ref.at[t], recv_sem=recv_ref.at[t],
                                               device_id=(x, y, c), device_id_type=MESH)
            out.wait_send()
            arrived = pltpu.make_async_remote_copy(src_ref=d, dst_ref=d, send_sem=send_ref.at[t],
                                                   recv_sem=recv_ref.at[t], device_id=(x, y, c), device_id_type=MESH)
            arrived.wait_recv()

    after = list(after) if isinstance(after, (list, tuple)) else [after]
    return pl.pallas_call(
        body, name=name,
        in_specs=[HBM] * n_b + [SEM, SEM] + [ANY] * len(after), out_specs=[HBM] * n_b,
        out_shape=[pltpu.HBM(b.shape, b.dtype) for b in bufs],
        input_output_aliases={i: i for i in range(n_b)},
        compiler_params=pltpu.CompilerParams(has_side_effects=DATAFLOW),
    )(*bufs, send_sems, recv_sems, *after)


def _gather_start(name, bufs, after=None, peers=ALL_PEERS):
    def make_copy(src, land, me, pk, c):
        hr = src.shape[1] // 2
        return src.at[me, pl.ds(c * hr, hr)], land.at[me, pl.ds(c * hr, hr)]

    return _chip_copies_start(name, bufs, None, make_copy, after, peers)


def _gather_wait(name, send_sems, recv_sems, bufs, after, peers=ALL_PEERS):
    def half_shards(buf):
        return [b.at[pl.ds(0, len(peers)), pl.ds(0, b.shape[1] // 2)] for b in buf]

    return _chip_copies_wait(name, send_sems, recv_sems, bufs, half_shards, half_shards, after)


def _gather_forward_start(name, bufs, peers=ALL_PEERS):
    n_t = len(bufs)

    def body(*refs):
        mine = refs[:n_t]
        send_sems, recv_sems = refs[n_t], refs[n_t + 1]
        buf = refs[n_t + 2:2 * n_t + 2]
        token = refs[-1]
        x, y, c = _place()
        others = _other_chips(x, y)
        for q in range(D2D_CHUNKS):
            for t in range(n_t):
                hr = mine[t].shape[1] // 2
                cr = hr // D2D_CHUNKS
                rows = pl.ds(c * hr + q * cr, cr)
                for _, _, pk in [others[p] for p in peers]:
                    pltpu.make_async_remote_copy(
                        src_ref=mine[t].at[pk, rows], dst_ref=buf[t].at[pk, rows], send_sem=send_sems.at[t],
                        recv_sem=recv_sems.at[t], device_id=(x, y, 1 - c), device_id_type=MESH).start()
        token[...] = jnp.zeros_like(token)

    outs = pl.pallas_call(
        body, name=name,
        in_specs=[HBM] * n_t,
        out_specs=[SEM, SEM] + [HBM] * n_t + [pl.BlockSpec(memory_space=pltpu.VMEM)],
        out_shape=[pltpu.SemaphoreType.DMA((n_t,)), pltpu.SemaphoreType.DMA((n_t,))]
        + [pltpu.HBM(b.shape, b.dtype) for b in bufs] + [jax.ShapeDtypeStruct((8, 128), F32)],
        input_output_aliases={i: 2 + i for i in range(n_t)},
        compiler_params=pltpu.CompilerParams(has_side_effects=DATAFLOW),
    )(*[_in_hbm(b) for b in bufs])
    return outs[0], outs[1], list(outs[2:2 + n_t]), outs[-1]


def _core_exchange_start(name, grads):
    n_t = len(grads)
    lands = [lax.empty(g.shape, g.dtype) for g in grads]

    def body(*refs):
        src = refs[:n_t]
        send_sems, recv_sems = refs[2 * n_t], refs[2 * n_t + 1]
        land = refs[2 * n_t + 2 + n_t:2 * n_t + 2 + 2 * n_t]
        token = refs[-1]
        x, y, c = _place()
        for q in range(D2D_CHUNKS):
            for t in range(n_t):
                cr = src[t].shape[1] // D2D_CHUNKS
                rows = pl.ds(q * cr, cr)
                pltpu.make_async_remote_copy(
                    src_ref=src[t].at[:, rows], dst_ref=land[t].at[:, rows], send_sem=send_sems.at[t],
                    recv_sem=recv_sems.at[t], device_id=(x, y, 1 - c), device_id_type=MESH).start()
        token[...] = jnp.zeros_like(token)

    bufs = list(grads) + lands
    outs = pl.pallas_call(
        body, name=name,
        in_specs=[HBM] * len(bufs),
        out_specs=[SEM, SEM] + [HBM] * len(bufs) + [pl.BlockSpec(memory_space=pltpu.VMEM)],
        out_shape=[pltpu.SemaphoreType.DMA((n_t,)), pltpu.SemaphoreType.DMA((n_t,))]
        + [pltpu.HBM(b.shape, b.dtype) for b in bufs] + [jax.ShapeDtypeStruct((8, 128), F32)],
        input_output_aliases={i: 2 + i for i in range(len(bufs))},
        compiler_params=pltpu.CompilerParams(has_side_effects=DATAFLOW),
    )(*[_in_hbm(b) for b in bufs])
    return outs[0], outs[1], list(outs[2:2 + len(bufs)]), outs[-1]


def _core_exchange_wait(name, send_sems, recv_sems, bufs, after):
    n_t = len(bufs) // 2

    def body(*refs):
        land = refs[n_t:2 * n_t]
        send_ref, recv_ref = refs[2 * n_t], refs[2 * n_t + 1]
        x, y, c = _place()
        for t in range(n_t):
            whole = pltpu.make_async_remote_copy(src_ref=land[t], dst_ref=land[t], send_sem=send_ref.at[t],
                                                 recv_sem=recv_ref.at[t], device_id=(x, y, c), device_id_type=MESH)
            whole.wait_send()
            whole.wait_recv()

    outs = pl.pallas_call(
        body, name=name,
        in_specs=[HBM] * (2 * n_t) + [SEM, SEM, ANY], out_specs=[HBM] * (2 * n_t),
        out_shape=[pltpu.HBM(b.shape, b.dtype) for b in bufs],
        input_output_aliases={i: i for i in range(2 * n_t)},
        compiler_params=pltpu.CompilerParams(has_side_effects=DATAFLOW),
    )(*bufs, send_sems, recv_sems, after)
    return list(outs[:n_t]), list(outs[n_t:])


def _chip_exchange_start(name, parts):
    lands = [lax.empty(p.shape, p.dtype) for p in parts]
    return _chip_copies_start(name, parts, lands, lambda src, land, me, pk, c: (src.at[pk], land.at[me]))


def _chip_exchange_wait(name, send_sems, recv_sems, bufs, after):
    n_t = len(bufs) // 2
    return _chip_copies_wait(name, send_sems, recv_sems, bufs,
                             lambda buf: [b.at[pl.ds(0, 3)] for b in buf[:n_t]],
                             lambda buf: [b.at[pl.ds(0, 3)] for b in buf[n_t:]], after)


def _sum_chips(name, parts, landed, place, l, stacked):
    chips, rows, cols = landed.shape
    tr = min(256, rows)
    per = rows // tr

    def body(p_ref, own_ref, *refs):
        land, o_ref = refs[:chips], refs[-1]
        tot = None
        for k in range(chips):
            term = jnp.where(p_ref[0] == k, own_ref[...], land[k][...]).astype(F32)
            tot = term if tot is None else tot + term
        o_ref[...] = tot

    def from_chip(k):
        return pl.BlockSpec((None, tr, cols), lambda i, p: (jnp.where(p[0] == k, (k + 1) % chips, k), i, 0))

    in_specs = [pl.BlockSpec((None, tr, cols), lambda i, p: (p[0], i, 0))] + [from_chip(k) for k in range(chips)]
    args = [parts] + [landed] * chips
    aliases = {}
    if stacked is not None:
        in_specs.append(ANY)
        args.append(stacked)
        aliases = {len(args): 0}
    return pl.pallas_call(
        body, name=name,
        grid_spec=pltpu.PrefetchScalarGridSpec(
            num_scalar_prefetch=1, grid=(per,), in_specs=in_specs,
            out_specs=pl.BlockSpec((None, tr, cols), lambda i, p: (l, p[1] * per + i, 0))),
        out_shape=jax.ShapeDtypeStruct((DEPTH, 2 * rows, cols), F32), input_output_aliases=aliases,
        compiler_params=_params(("parallel",)),
    )(place, *args)


def _core_share_start(name, bufs, l):
    n_t = len(bufs)

    def body(*refs):
        mine = refs[:n_t]
        send_sems, recv_sems = refs[n_t], refs[n_t + 1]
        buf = refs[n_t + 2:2 * n_t + 2]
        token = refs[-1]
        x, y, c = _place()
        for q in range(D2D_CHUNKS):
            for t in range(n_t):
                hr = mine[t].shape[1] // 2
                cr = hr // D2D_CHUNKS
                rows = pl.ds(c * hr + q * cr, cr)
                pltpu.make_async_remote_copy(
                    src_ref=mine[t].at[l, rows], dst_ref=buf[t].at[l, rows], send_sem=send_sems.at[t],
                    recv_sem=recv_sems.at[t], device_id=(x, y, 1 - c), device_id_type=MESH).start()
        token[...] = jnp.zeros_like(token)

    outs = pl.pallas_call(
        body, name=name,
        in_specs=[HBM] * n_t,
        out_specs=[SEM, SEM] + [HBM] * n_t + [pl.BlockSpec(memory_space=pltpu.VMEM)],
        out_shape=[pltpu.SemaphoreType.DMA((n_t,)), pltpu.SemaphoreType.DMA((n_t,))]
        + [pltpu.HBM(b.shape, b.dtype) for b in bufs] + [jax.ShapeDtypeStruct((8, 128), F32)],
        input_output_aliases={i: 2 + i for i in range(n_t)},
        compiler_params=pltpu.CompilerParams(has_side_effects=DATAFLOW),
    )(*[_in_hbm(b) for b in bufs])
    return outs[0], outs[1], list(outs[2:2 + n_t]), outs[-1]


def _core_share_wait(name, send_sems, recv_sems, bufs, l, after):
    def half_layer(buf):
        return [b.at[l, pl.ds(0, b.shape[1] // 2)] for b in buf]

    return _chip_copies_wait(name, send_sems, recv_sems, bufs, half_layer, half_layer, after)


def _all_reduce_small(vec, after=None):
    rows, lanes = vec.shape
    hr = rows // 2

    def body(v_ref, *refs):
        o_ref, sib_ref, chips_ref, send_sems, recv_sems = refs[-5:]
        x, y, c = _place()
        me = 2 * x + y
        sibling = (x, y, 1 - c)
        mine = pl.ds(pl.multiple_of(c * hr, 8), hr)
        theirs = pl.ds(pl.multiple_of((1 - c) * hr, 8), hr)
        swap = pltpu.make_async_remote_copy(
            src_ref=v_ref.at[theirs], dst_ref=sib_ref, send_sem=send_sems.at[0], recv_sem=recv_sems.at[0],
            device_id=sibling, device_id_type=MESH)
        swap.start()
        swap.wait_recv()
        chips_ref[me] = v_ref[mine] + sib_ref[...]
        copies = []
        for j, (px, py, pk) in enumerate(_other_chips(x, y)):
            cp = pltpu.make_async_remote_copy(
                src_ref=chips_ref.at[me], dst_ref=chips_ref.at[me], send_sem=send_sems.at[1 + j],
                recv_sem=recv_sems.at[1 + j], device_id=(px, py, c), device_id_type=MESH)
            cp.start()
            copies.append(cp)
        for j, (px, py, pk) in enumerate(_other_chips(x, y)):
            pltpu.make_async_remote_copy(
                src_ref=chips_ref.at[pk], dst_ref=chips_ref.at[pk], send_sem=send_sems.at[1 + j],
                recv_sem=recv_sems.at[1 + j], device_id=(px, py, c), device_id_type=MESH).wait_recv()
        tot = chips_ref[0]
        for k in range(1, N_CHIPS):
            tot = tot + chips_ref[k]
        o_ref[mine] = tot
        share = pltpu.make_async_remote_copy(
            src_ref=o_ref.at[mine], dst_ref=o_ref.at[mine], send_sem=send_sems.at[4], recv_sem=recv_sems.at[4],
            device_id=sibling, device_id_type=MESH)
        share.start()
        pltpu.make_async_remote_copy(
            src_ref=o_ref.at[theirs], dst_ref=o_ref.at[theirs], send_sem=send_sems.at[4], recv_sem=recv_sems.at[4],
            device_id=sibling, device_id_type=MESH).wait_recv()
        swap.wait_send()
        for cp in copies:
            cp.wait_send()
        share.wait_send()

    vm = pl.BlockSpec(memory_space=pltpu.VMEM)
    return pl.pallas_call(
        body, name="small_all_reduce", in_specs=[vm] + ([] if after is None else [ANY]), out_specs=vm,
        out_shape=jax.ShapeDtypeStruct((rows, lanes), F32),
        scratch_shapes=[pltpu.VMEM((hr, lanes), F32), pltpu.VMEM((N_CHIPS, hr, lanes), F32),
                        pltpu.SemaphoreType.DMA((5,)), pltpu.SemaphoreType.DMA((5,))],
        compiler_params=pltpu.CompilerParams(has_side_effects=True, vmem_limit_bytes=48 * MIB),
    )(vec, *([] if after is None else [after]))


def _adamw(name, w, g, m, v, place, l=0, half=None, done=None, after=None):
    layers, rows, cols = w.shape
    span = rows if half is None else rows // 2
    tr = span
    for cand in (256, 128, 64, 32, 16, 8):
        if span % cand == 0 and cand * cols * 4 <= MIB:
            tr = cand
            break
    per = span // tr
    c1 = 1.0 - ADAM_B1 ** ADAM_STEP
    c2 = 1.0 - ADAM_B2 ** ADAM_STEP
    n_in, n_out = 3, 2

    def body(p_ref, w_ref, g_ref, m_ref, v_ref, *refs):
        ins = (w_ref, g_ref, m_ref, v_ref)
        outs = refs[-8:-4]
        in_buf, out_buf, in_sem, out_sem = refs[-4:]
        first = 0 if half is None else (p_ref[1] if half == "own" else 1 - p_ref[1]) * span

        def rows_of(ref, i):
            at = i * tr if half is None else pl.multiple_of(first + i * tr, tr)
            return ref.at[l, pl.ds(at, tr)]

        def reads(i):
            return [pltpu.make_async_copy(rows_of(ins[k], i), in_buf.at[k, i % n_in], in_sem.at[k, i % n_in])
                    for k in range(4)]

        def writes(i):
            return [pltpu.make_async_copy(out_buf.at[k, i % n_out], rows_of(outs[k], i), out_sem.at[k, i % n_out])
                    for k in range(4)]

        for i in range(min(n_in - 1, per)):
            for c in reads(i):
                c.start()
        for i in range(per):
            if i + n_in - 1 < per:
                for c in reads(i + n_in - 1):
                    c.start()
            for c in reads(i):
                c.wait()
            if i >= n_out:
                for c in writes(i - n_out):
                    c.wait()
            s, t = i % n_in, i % n_out
            gv = in_buf[1, s]
            nm = ADAM_B1 * in_buf[2, s] + (1.0 - ADAM_B1) * gv
            nv = ADAM_B2 * in_buf[3, s] + (1.0 - ADAM_B2) * (gv * gv)
            out_buf[0, t] = gv
            out_buf[1, t] = -ADAM_LR * ((nm / c1) / (jnp.sqrt(nv / c2) + ADAM_EPS) + ADAM_WD * in_buf[0, s])
            out_buf[2, t] = nm
            out_buf[3, t] = nv
            for c in writes(i):
                c.start()
        for i in range(max(per - n_out, 0), per):
            for c in writes(i):
                c.wait()

    out = jax.ShapeDtypeStruct((layers, rows, cols), F32)
    extra = ([] if done is None else list(done)) + ([] if after is None else [after])
    aliases = {} if done is None else {5 + i: i for i in range(4)}
    return pl.pallas_call(
        body, name=name,
        grid_spec=pltpu.PrefetchScalarGridSpec(
            num_scalar_prefetch=1, grid=(1,), in_specs=[ANY] * (4 + len(extra)), out_specs=[ANY] * 4,
            scratch_shapes=[pltpu.VMEM((4, n_in, tr, cols), F32), pltpu.VMEM((4, n_out, tr, cols), F32),
                            pltpu.SemaphoreType.DMA((4, n_in)), pltpu.SemaphoreType.DMA((4, n_out))]),
        out_shape=[out] * 4, input_output_aliases=aliases,
        compiler_params=_params(("arbitrary",)),
    )(place, w, g, m, v, *extra)


LANES = 128
SUBLANES = 8
SMALL_SHAPES = {
    "norm_g": (DEPTH, D_MODEL), "sgu_ln_g": (DEPTH, D_A), "sgu_ln_b": (DEPTH, D_A),
    "sgu_w": (DEPTH, A_GROUPS, CHUNK, CHUNK), "sgu_b": (DEPTH, A_GROUPS, CHUNK), "mem_norm_g": (DEPTH, D_MODEL),
    "q_norm_g": (DEPTH, HEAD_DIM), "k_norm_g": (DEPTH, HEAD_DIM)}


def _small_layout():
    at, off = {}, 0
    for k in SMALL_NAMES:
        n = math.prod(SMALL_SHAPES[k]) // LANES
        at[k] = (off, n)
        off += -(-n // SUBLANES) * SUBLANES
    return at, off, -(-(off + SUBLANES) // (2 * SUBLANES)) * 2 * SUBLANES


def _pack_small(parts, loss=None):
    at, loss_row, rows = _small_layout()
    pieces = []
    for k in SMALL_NAMES:
        n = at[k][1]
        pieces.append(jnp.pad(parts[k].reshape(n, LANES), ((0, -(-n // SUBLANES) * SUBLANES - n), (0, 0))))
    tile = jnp.zeros((SUBLANES, LANES), F32) if loss is None else jnp.broadcast_to(loss.reshape(1, 1), (SUBLANES, LANES))
    pieces += [tile, jnp.zeros((rows - loss_row - SUBLANES, LANES), F32)]
    return jnp.concatenate(pieces)


def _adamw_small(w, g, m, v):
    at, _, rows = _small_layout()
    c1 = 1.0 - ADAM_B1 ** ADAM_STEP
    c2 = 1.0 - ADAM_B2 ** ADAM_STEP
    n_names = len(SMALL_NAMES)

    def body(w_ref, g_ref, m_ref, v_ref, *refs):
        outs, (d_ref, nm_ref, nv_ref) = refs[:4 * n_names], refs[4 * n_names:]
        gv = g_ref[...]
        nm = ADAM_B1 * m_ref[...] + (1.0 - ADAM_B1) * gv
        nv = ADAM_B2 * v_ref[...] + (1.0 - ADAM_B2) * (gv * gv)
        nm_ref[...] = nm
        nv_ref[...] = nv
        d_ref[...] = -ADAM_LR * ((nm / c1) / (jnp.sqrt(nv / c2) + ADAM_EPS) + ADAM_WD * w_ref[...])
        for kind, src in enumerate((g_ref, d_ref, nm_ref, nv_ref)):
            for i, k in enumerate(SMALL_NAMES):
                o_ref = outs[kind * n_names + i]
                first, n = at[k]
                shape = SMALL_SHAPES[k]
                if shape[-1] == LANES:
                    o_ref[...] = src[pl.ds(first, n), :].reshape(shape)
                else:
                    per = shape[-1] // LANES
                    for r in range(n):
                        o_ref[pl.ds(r // per, 1), pl.ds((r % per) * LANES, LANES)] = src[pl.ds(first + r, 1), :]

    out_shape = [jax.ShapeDtypeStruct(SMALL_SHAPES[k], F32) for _ in range(4) for k in SMALL_NAMES]
    outs = pl.pallas_call(
        body, name="adamw_small", out_shape=out_shape,
        scratch_shapes=[pltpu.VMEM((rows, LANES), F32)] * 3, compiler_params=_params(None),
    )(w, g, m, v)
    return [dict(zip(SMALL_NAMES, outs[kind * n_names:(kind + 1) * n_names])) for kind in range(4)]


WEIGHT_ORDER = ("norm_g", "w_in", "sgu_ln_g", "sgu_ln_b", "sgu_w", "sgu_b", "mem_norm_g", "w_mem_kv", "q_norm_g",
                "k_norm_g", "w_out")


def kernel(x, mem, norm_g, w_in, sgu_ln_g, sgu_ln_b, sgu_w, sgu_b, mem_norm_g, w_mem_kv, q_norm_g, k_norm_g, w_out, loss_target, m_norm_g, m_w_in, m_sgu_ln_g, m_sgu_ln_b, m_sgu_w, m_sgu_b, m_mem_norm_g, m_w_mem_kv, m_q_norm_g, m_k_norm_g, m_w_out, v_norm_g, v_w_in, v_sgu_ln_g, v_sgu_ln_b, v_sgu_w, v_sgu_b, v_mem_norm_g, v_w_mem_kv, v_q_norm_g, v_k_norm_g, v_w_out):
    weights = dict(norm_g=norm_g, w_in=w_in, sgu_ln_g=sgu_ln_g, sgu_ln_b=sgu_ln_b, sgu_w=sgu_w, sgu_b=sgu_b,
                   mem_norm_g=mem_norm_g, w_mem_kv=w_mem_kv, q_norm_g=q_norm_g, k_norm_g=k_norm_g, w_out=w_out)
    mom_m = dict(norm_g=m_norm_g, w_in=m_w_in, sgu_ln_g=m_sgu_ln_g, sgu_ln_b=m_sgu_ln_b, sgu_w=m_sgu_w, sgu_b=m_sgu_b,
                 mem_norm_g=m_mem_norm_g, w_mem_kv=m_w_mem_kv, q_norm_g=m_q_norm_g, k_norm_g=m_k_norm_g, w_out=m_w_out)
    mom_v = dict(norm_g=v_norm_g, w_in=v_w_in, sgu_ln_g=v_sgu_ln_g, sgu_ln_b=v_sgu_ln_b, sgu_w=v_sgu_w, sgu_b=v_sgu_b,
                 mem_norm_g=v_mem_norm_g, w_mem_kv=v_w_mem_kv, q_norm_g=v_q_norm_g, k_norm_g=v_k_norm_g, w_out=v_w_out)
    big = ("w_in", "w_mem_kv", "w_out")
    sm = {k: weights[k] for k in SMALL_NAMES}

    place = _place_index()
    xs, mems, target = x[0], mem[0], loss_target[0]

    slots = [[_cast_into_slot(f"cast_{k}_{l}", weights[k], l, place) for k in big] for l in range(DEPTH)]
    saved = [None] * DEPTH

    chips, cores = {}, {}
    me = place[0]
    arrival = jnp.stack([me, me ^ 2, me ^ 1, 3 - me]).astype(jnp.int32)
    shard_order = jnp.arange(N_CHIPS, dtype=jnp.int32)

    def start_gather(l, after=None):
        chips[l, "in"] = _gather_start(f"gather_start_{l}_in", slots[l][:1], after)
        chips[l, "rest"] = _gather_start(f"gather_start_{l}_rest", slots[l][1:], chips[l, "in"][3])
        return chips[l, "rest"][3]

    def hand_to_sibling(l, group, after):
        send_sems, recv_sems, bufs, _ = chips[l, group]
        bufs = _gather_wait(f"gather_wait_{l}_{group}", send_sems, recv_sems, bufs, after)
        cores[l, group] = _gather_forward_start(f"gather_forward_{l}_{group}", bufs)
        return cores[l, group][3]

    def whole(l, group, after):
        send_sems, recv_sems, bufs, _ = cores[l, group]
        return _gather_wait(f"gather_whole_{l}_{group}", send_sems, recv_sems, bufs, after)

    later_slots = [s for layer in slots[1:] for s in layer]

    class Gathered:
        def __init__(self, l):
            self.l = l
            self.buf = None

        def landed_from(self, tag, peers, after, behind, then=None):
            send_sems, recv_sems, _, _ = chips[0, "in_" + tag]
            buf = _gather_wait(f"gather_wait_0_in_{tag}", send_sems, recv_sems, self.buf, after, peers)
            if then is not None:
                buf, more = then(buf)
                behind = behind + more
            send_sems, recv_sems, buf, token = _gather_forward_start(f"gather_forward_0_in_{tag}", buf, peers)
            self.buf = _gather_wait(f"gather_whole_0_in_{tag}", send_sems, recv_sems, buf, [token] + behind, peers)

        def w_in(self, stage, h, proj):
            if self.l > 0:
                return (whole(self.l, "in", h)[0], shard_order, 0, N_CHIPS) if stage == 0 else None
            if stage == 0:
                self.buf = chips[0, "in_n"][2]
                return self.buf[0], arrival, 0, 1
            if stage == 1:
                def start_others(buf):
                    chips[0, "in_d"] = _gather_start("gather_start_0_in_d", buf, None, DIAGONAL)
                    chips[0, "rest"] = _gather_start("gather_start_0_rest", slots[0][1:], chips[0, "in_d"][3])
                    return chips[0, "in_d"][2], [chips[0, "rest"][3]]

                self.landed_from("n", NEIGHBOURS, proj, later_slots + [chips[0, "in_n"][3]], start_others)
                return self.buf[0], arrival, 1, 2
            if stage == 2:
                self.landed_from("d", DIAGONAL, [proj, chips[0, "rest"][3]], [])
                return self.buf[0], arrival, 3, 1
            return None

        def rest_start(self, proj):
            token = proj if self.l == 0 else hand_to_sibling(self.l, "rest", proj)
            return start_gather(self.l + 1, token) if self.l + 1 < DEPTH else token

        def rest_finish(self, o_b):
            if self.l == 0:
                o_b = hand_to_sibling(self.l, "rest", o_b)
            w_kv_all, w_out_all = whole(self.l, "rest", o_b)
            return w_kv_all, w_out_all, None

        def before_out(self, y):
            return hand_to_sibling(self.l + 1, "in", y) if self.l + 1 < DEPTH else None

    chips[0, "in_n"] = _gather_start("gather_start_0_in_n", slots[0][:1], None, NEIGHBOURS)
    cur = xs
    for l in range(DEPTH):
        cur, saved[l] = _layer_fwd(l, cur, mems, sm, Gathered(l), target if l == DEPTH - 1 else None)
    dxo, dxo_b, loss_part = cur

    small_g = [None] * DEPTH
    flight = {}

    class Exchange:
        def __init__(self):
            self.cores = {}

        def start(self, l, group, gives):
            *self.cores[l, group], token = _core_exchange_start(f"grad_core_start_{l}_{group}", gives)
            return token

        def landed(self, l, group, after):
            send_sems, recv_sems, bufs = self.cores[l, group]
            return _core_exchange_wait(f"grad_core_wait_{l}_{group}", send_sems, recv_sems, bufs, after)[1]

        def send(self, l, group, parts):
            *flight[l, group], token = _chip_exchange_start(f"grad_chip_start_{l}_{group}", parts)
            return token

    exchange = Exchange()
    for l in reversed(range(DEPTH)):
        dxo, dxo_b, small_g[l] = _layer_bwd(l, dxo, dxo_b, mems, sm, saved[l], place, exchange)
    grad_x = dxo

    groups = (("out", ("w_out",)), ("in", ("w_in", "w_mem_kv")))
    halves, stepped = dict.fromkeys(big), dict.fromkeys(big)
    small_g = {k: jnp.stack([small_g[l][k] for l in range(DEPTH)]) for k in SMALL_NAMES}
    after = grad_x
    sharing = {}

    def reduce_group(l, group, names):
        nonlocal after
        send_sems, recv_sems, bufs = flight[l, group]
        bufs = _chip_exchange_wait(f"grad_chip_wait_{l}_{group}", send_sems, recv_sems, bufs, after)
        for t, k in enumerate(names):
            halves[k] = _sum_chips(f"grad_chip_sum_{l}_{k}", bufs[t], bufs[len(names) + t], place, l, halves[k])
        *sharing[l, group], after = _core_share_start(f"grad_core_share_{l}_{group}", [halves[k] for k in names], l)

    def step(l, k, buf, half):
        nonlocal after
        tag = "" if half is None else "_" + half
        stepped[k] = _adamw(f"adamw_{k}_{l}{tag}", weights[k], buf, mom_m[k], mom_v[k], place, l, half, stepped[k],
                            after)
        after = stepped[k][1]

    def step_group(l, group, names, overlap):
        nonlocal after
        send_sems, recv_sems, bufs = sharing[l, group]
        if overlap:
            for k, buf in zip(names, bufs):
                step(l, k, buf, "own")
        bufs = _core_share_wait(f"grad_core_shared_{l}_{group}", send_sems, recv_sems, bufs, l, after)
        for k, buf in zip(names, bufs):
            halves[k] = buf
            step(l, k, buf, "other" if overlap else None)

    for l in reversed(range(DEPTH)):
        last = l == 0
        (g_out, n_out), (g_in, n_in) = groups
        reduce_group(l, g_out, n_out)
        if last:
            step_group(l, g_out, n_out, False)
            small_sum = _all_reduce_small(_pack_small(small_g, loss_part), after)
            small_step = _adamw_small(_pack_small(sm), small_sum, _pack_small({k: mom_m[k] for k in SMALL_NAMES}),
                                      _pack_small({k: mom_v[k] for k in SMALL_NAMES}))
            after = small_step[1]["sgu_w"]
        reduce_group(l, g_in, n_in)
        if not last:
            step_group(l, g_out, n_out, False)
        step_group(l, g_in, n_in, last)

    grads, delta, new_m, new_v = ({k: stepped[k][i] for k in big} for i in range(4))
    for out, small in zip((grads, delta, new_m, new_v), small_step):
        out.update(small)
    loss = small_sum[_small_layout()[1], 0]
    return (loss, grad_x[None], *[grads[k] for k in WEIGHT_ORDER], *[delta[k] for k in WEIGHT_ORDER],
            *[new_m[k] for k in WEIGHT_ORDER], *[new_v[k] for k in WEIGHT_ORDER])
```
